```python
import jax, jax.numpy as jnp
from jax import lax
import numpy as np

D_MODEL = 1024
BATCH = 8
SEQ = 2048
DEPTH = 2

D_MIX = D_MODEL
D_POOL = D_MIX // 2
POOL_WINDOWS = (2, 4, 8, 16)
N_POOL_GROUPS = len(POOL_WINDOWS)
POOL_GROUP = D_POOL // N_POOL_GROUPS
HEAD_DIM = 64
D_ATTN = D_MIX - D_POOL
N_HEADS = D_ATTN // HEAD_DIM
N_KV_HEADS = 2
GQA_GROUP = N_HEADS // N_KV_HEADS
D_KV = N_KV_HEADS * HEAD_DIM
WINDOW = 128
BLOCK = 128
IN_WIDTHS = (D_POOL, D_POOL, D_ATTN, D_KV, D_KV, D_ATTN)
D_IN = sum(IN_WIDTHS)
EPS = 1e-6
NEG_INF = -1e30

kernel_name = "hybrid_pool_swa_sink_parallel_heads"


def rmsnorm(x, gain):
    x32 = x.astype(jnp.float32)
    y = x32 * lax.rsqrt(jnp.mean(x32 * x32, axis=-1, keepdims=True) + EPS) * gain.astype(jnp.float32)
    return y.astype(x.dtype)


def alibi_slopes():
    return jnp.exp2(-8.0 * jnp.arange(1, N_HEADS + 1, dtype=jnp.float32) / N_HEADS)


def pool_mixer(u, w_grp, scale):
    B, S, _ = u.shape
    u32 = u.astype(jnp.float32).reshape(B, S, N_POOL_GROUPS, POOL_GROUP)
    csum = jnp.cumsum(u32, axis=1)
    csum = jnp.concatenate([jnp.zeros_like(csum[:, :1]), csum], axis=1)
    pos = jnp.arange(1, S + 1, dtype=jnp.float32)
    means = []
    for g, w in enumerate(POOL_WINDOWS):
        c = csum[:, :, g]
        lo = jnp.concatenate([jnp.zeros_like(c[:, :w - 1]), c[:, :S + 1 - w]], axis=1)
        count = jnp.minimum(pos, float(w))[None, :, None]
        means.append((c[:, 1:] - lo) / count)
    pooled = jnp.stack(means, axis=2) - u32
    mixed = jnp.einsum('bsgc,gcd->bsgd', pooled.astype(u.dtype), w_grp)
    return mixed.reshape(B, S, D_POOL) * scale


def swa_sink_attention(q, k, v, sinks):
    B, S, _ = q.shape
    NB = S // BLOCK
    q = q.reshape(B, NB, BLOCK, N_KV_HEADS, GQA_GROUP, HEAD_DIM)
    k = k.reshape(B, NB, BLOCK, N_KV_HEADS, HEAD_DIM)
    v = v.reshape(B, NB, BLOCK, N_KV_HEADS, HEAD_DIM)

    def with_prev(t):
        prev = jnp.concatenate([jnp.zeros_like(t[:, :1]), t[:, :-1]], axis=1)
        return jnp.concatenate([prev, t], axis=2)

    kb, vb = with_prev(k), with_prev(v)
    scores = jnp.einsum('bnqhgd,bnkhd->bnhgqk', q, kb).astype(jnp.float32) * (HEAD_DIM ** -0.5)
    qi = jnp.arange(BLOCK)[:, None]
    kj = jnp.arange(2 * BLOCK)[None, :]
    dist = qi + BLOCK - kj
    in_win = (dist >= 0) & (dist < WINDOW)
    key_exists = (jnp.arange(NB)[:, None, None] > 0) | (kj >= BLOCK)[None]
    valid = in_win[None] & key_exists
    slopes = alibi_slopes().reshape(N_KV_HEADS, GQA_GROUP)
    bias = -slopes[:, :, None, None] * dist.astype(jnp.float32)
    scores = jnp.where(valid[None, :, None, None], scores + bias, NEG_INF)
    sink = jnp.broadcast_to(sinks.astype(jnp.float32).reshape(N_KV_HEADS, GQA_GROUP, 1, 1),
                            scores.shape[:-1] + (1,))
    probs = jax.nn.softmax(jnp.concatenate([scores, sink], axis=-1), axis=-1)[..., :-1]
    out = jnp.einsum('bnhgqk,bnkhd->bnqhgd', probs.astype(v.dtype), vb)
    return out.reshape(B, S, D_ATTN)


def _fwd_setup_inputs(seed: int = 0) -> dict:
    key = jax.random.key(seed)
    ks = jax.random.split(key, 9)
    x = jax.random.normal(ks[0], (BATCH, SEQ, D_MODEL), jnp.float32)
    w_in = jax.random.normal(ks[1], (DEPTH, D_MODEL, D_IN), jnp.float32) * D_MODEL ** -0.5
    pool_w = jax.random.normal(ks[2], (DEPTH, N_POOL_GROUPS, POOL_GROUP, POOL_GROUP), jnp.float32) * POOL_GROUP ** -0.5
    pool_scale = 1.0 + 0.1 * jax.random.normal(ks[3], (DEPTH, D_POOL), jnp.float32)
    attn_sinks = 0.5 * jax.random.normal(ks[4], (DEPTH, N_HEADS), jnp.float32)
    w_out = jax.random.normal(ks[5], (DEPTH, D_MIX, D_MODEL), jnp.float32) * D_MIX ** -0.5
    norm_pre = 1.0 + 0.1 * jax.random.normal(ks[6], (DEPTH, D_MODEL), jnp.float32)
    norm_post = 1.0 + 0.1 * jax.random.normal(ks[7], (DEPTH, D_MODEL), jnp.float32)
    return {"x": x, "w_in": w_in, "pool_w": pool_w, "pool_scale": pool_scale,
            "attn_sinks": attn_sinks, "w_out": w_out, "norm_pre": norm_pre, "norm_post": norm_post}


def _fwd_reference(x, w_in, pool_w, pool_scale, attn_sinks, w_out, norm_pre, norm_post):
    splits = [int(s) for s in np.cumsum(IN_WIDTHS)[:-1]]
    for layer in range(DEPTH):
        h = rmsnorm(x, norm_pre[layer])
        proj = h @ w_in[layer]
        pool_u, pool_gate, q, k, v, attn_gate = jnp.split(proj, splits, axis=-1)
        pool_out = pool_mixer(pool_u, pool_w[layer], pool_scale[layer]) * jax.nn.silu(pool_gate)
        attn_out = swa_sink_attention(q, k, v, attn_sinks[layer]) * jax.nn.silu(attn_gate)
        y = jnp.concatenate([pool_out, attn_out], axis=-1) @ w_out[layer]
        x = x + rmsnorm(y, norm_post[layer])
    return x


import jax as _jax
import jax.numpy as _jnp

TWIN_FORMAT = 'train_step'
FWD_PARAMS = ['x', 'w_in', 'pool_w', 'pool_scale', 'attn_sinks', 'w_out', 'norm_pre', 'norm_post']
TWIN_WEIGHTS = ['w_in', 'pool_w', 'pool_scale', 'attn_sinks', 'w_out', 'norm_pre', 'norm_post']
TWIN_DIFF_INPUT = 'x'
TWIN_INPUTS = ['x', 'w_in', 'pool_w', 'pool_scale', 'attn_sinks', 'w_out', 'norm_pre', 'norm_post', 'loss_target', 'm_w_in', 'm_pool_w', 'm_pool_scale', 'm_attn_sinks', 'm_w_out', 'm_norm_pre', 'm_norm_post', 'v_w_in', 'v_pool_w', 'v_pool_scale', 'v_attn_sinks', 'v_w_out', 'v_norm_pre', 'v_norm_post']
TWIN_OUTPUTS = ['loss', 'grad_x', 'grad_w_in', 'grad_pool_w', 'grad_pool_scale', 'grad_attn_sinks', 'grad_w_out', 'grad_norm_pre', 'grad_norm_post', 'delta_w_in', 'delta_pool_w', 'delta_pool_scale', 'delta_attn_sinks', 'delta_w_out', 'delta_norm_pre', 'delta_norm_post', 'new_m_w_in', 'new_m_pool_w', 'new_m_pool_scale', 'new_m_attn_sinks', 'new_m_w_out', 'new_m_norm_pre', 'new_m_norm_post', 'new_v_w_in', 'new_v_pool_w', 'new_v_pool_scale', 'new_v_attn_sinks', 'new_v_w_out', 'new_v_norm_pre', 'new_v_norm_post']
TWIN_LEAF_KINDS = {'loss': 'loss', 'grad_x': 'grad_x', 'grad_w_in': 'grad_w', 'grad_pool_w': 'grad_w', 'grad_pool_scale': 'grad_w', 'grad_attn_sinks': 'grad_w', 'grad_w_out': 'grad_w', 'grad_norm_pre': 'grad_w', 'grad_norm_post': 'grad_w', 'delta_w_in': 'delta_w', 'delta_pool_w': 'delta_w', 'delta_pool_scale': 'delta_w', 'delta_attn_sinks': 'delta_w', 'delta_w_out': 'delta_w', 'delta_norm_pre': 'delta_w', 'delta_norm_post': 'delta_w', 'new_m_w_in': 'new_m', 'new_m_pool_w': 'new_m', 'new_m_pool_scale': 'new_m', 'new_m_attn_sinks': 'new_m', 'new_m_w_out': 'new_m', 'new_m_norm_pre': 'new_m', 'new_m_norm_post': 'new_m', 'new_v_w_in': 'new_v', 'new_v_pool_w': 'new_v', 'new_v_pool_scale': 'new_v', 'new_v_attn_sinks': 'new_v', 'new_v_w_out': 'new_v', 'new_v_norm_pre': 'new_v', 'new_v_norm_post': 'new_v'}


def _forward(args):
    return _fwd_reference(*[args[k] for k in FWD_PARAMS])


def _output_shape():
    out = _jax.eval_shape(lambda: _forward(_fwd_setup_inputs(0)))
    return out.shape, out.dtype

N_MICROBATCH = 1
ADAM_LR = 0.001
ADAM_B1 = 0.9
ADAM_B2 = 0.999
ADAM_EPS = 1e-08
ADAM_WD = 0.01
ADAM_STEP = 10
PER_EXAMPLE_BATCH_AXIS = {'x': 0, 'loss_target': 0}
SHARED_INPUTS = []
_WEIGHT_DTYPES = {'w_in': _jnp.float32, 'pool_w': _jnp.float32, 'pool_scale': _jnp.float32, 'attn_sinks': _jnp.float32, 'w_out': _jnp.float32, 'norm_pre': _jnp.float32, 'norm_post': _jnp.float32}
MOMENT_SCALE = {'w_in': 2.879034e-01, 'pool_w': 4.249461e-01, 'pool_scale': 5.068469e-01, 'attn_sinks': 1.805778e-01, 'w_out': 3.411053e-01, 'norm_pre': 4.284252e-01, 'norm_post': 1.607725e+01}


def _to_microbatches(a, axis):
    t = _jnp.moveaxis(a, axis, 0)
    t = t.reshape((N_MICROBATCH, t.shape[0] // N_MICROBATCH) + t.shape[1:])
    return _jnp.moveaxis(t, 1, axis + 1)


def setup_inputs(seed: int = 0) -> dict:
    inp = _fwd_setup_inputs(seed)
    key = _jax.random.fold_in(_jax.random.key(seed), 7919)
    shape, _ = _output_shape()
    out = dict(inp)
    out["loss_target"] = _jax.random.normal(_jax.random.fold_in(key, 0), shape, _jnp.float32)
    for i, name in enumerate(TWIN_WEIGHTS):
        w = inp[name].astype(_jnp.float32)
        if MOMENT_SCALE is None:
            s = _jnp.sqrt(_jnp.mean(_jnp.square(w)) + 1e-30)
        else:
            s = MOMENT_SCALE[name]
        km, kv = _jax.random.split(_jax.random.fold_in(key, i + 1))
        out[name] = w
        out["m_" + name] = s * _jax.random.normal(km, w.shape, _jnp.float32)
        out["v_" + name] = (s * s) * _jax.random.uniform(kv, w.shape, _jnp.float32, 0.5, 1.5)
    if N_MICROBATCH > 1:
        for name, axis in PER_EXAMPLE_BATCH_AXIS.items():
            out[name] = _to_microbatches(out[name], axis)
    return {'x': out['x'], 'w_in': out['w_in'], 'pool_w': out['pool_w'], 'pool_scale': out['pool_scale'], 'attn_sinks': out['attn_sinks'], 'w_out': out['w_out'], 'norm_pre': out['norm_pre'], 'norm_post': out['norm_post'], 'loss_target': out['loss_target'], 'm_w_in': out['m_w_in'], 'm_pool_w': out['m_pool_w'], 'm_pool_scale': out['m_pool_scale'], 'm_attn_sinks': out['m_attn_sinks'], 'm_w_out': out['m_w_out'], 'm_norm_pre': out['m_norm_pre'], 'm_norm_post': out['m_norm_post'], 'v_w_in': out['v_w_in'], 'v_pool_w': out['v_pool_w'], 'v_pool_scale': out['v_pool_scale'], 'v_attn_sinks': out['v_attn_sinks'], 'v_w_out': out['v_w_out'], 'v_norm_pre': out['v_norm_pre'], 'v_norm_post': out['v_norm_post']}


def _loss(weights, diff, rest, loss_target):
    with _jax.named_scope("forward"):
        args = {**rest, TWIN_DIFF_INPUT: diff, **{k: w.astype(_WEIGHT_DTYPES[k]) for k, w in weights.items()}}
        y = _forward(args)
    with _jax.named_scope("loss_head"):
        err = _jnp.square(y.astype(_jnp.float32) - loss_target)
        return 0.5 * _jnp.sum(_jnp.mean(err, axis=-1)) if err.ndim else 0.5 * err


def _adamw(w, g, m, v):
    m = ADAM_B1 * m + (1.0 - ADAM_B1) * g
    v = ADAM_B2 * v + (1.0 - ADAM_B2) * _jnp.square(g)
    m_hat = m / (1.0 - ADAM_B1 ** ADAM_STEP)
    v_hat = v / (1.0 - ADAM_B2 ** ADAM_STEP)
    delta = -ADAM_LR * (m_hat / (_jnp.sqrt(v_hat) + ADAM_EPS) + ADAM_WD * w)
    return delta, m, v


def reference(x, w_in, pool_w, pool_scale, attn_sinks, w_out, norm_pre, norm_post, loss_target, m_w_in, m_pool_w, m_pool_scale, m_attn_sinks, m_w_out, m_norm_pre, m_norm_post, v_w_in, v_pool_w, v_pool_scale, v_attn_sinks, v_w_out, v_norm_pre, v_norm_post):
    given = dict(x=x, w_in=w_in, pool_w=pool_w, pool_scale=pool_scale, attn_sinks=attn_sinks, w_out=w_out, norm_pre=norm_pre, norm_post=norm_post, loss_target=loss_target, m_w_in=m_w_in, m_pool_w=m_pool_w, m_pool_scale=m_pool_scale, m_attn_sinks=m_attn_sinks, m_w_out=m_w_out, m_norm_pre=m_norm_pre, m_norm_post=m_norm_post, v_w_in=v_w_in, v_pool_w=v_pool_w, v_pool_scale=v_pool_scale, v_attn_sinks=v_attn_sinks, v_w_out=v_w_out, v_norm_pre=v_norm_pre, v_norm_post=v_norm_post)
    weights = {n: given[n] for n in TWIN_WEIGHTS}
    shared = {n: given[n] for n in SHARED_INPUTS}
    per_example = {n: given[n] for n in ['x']}
    grad_fn = _jax.value_and_grad(_loss, argnums=(0, 1))

    def one_microbatch(ex, loss_target):
        ex = dict(ex)
        diff = ex.pop(TWIN_DIFF_INPUT)
        return grad_fn(weights, diff, {**shared, **ex}, loss_target)

    if N_MICROBATCH == 1:
        loss, (grad_w, grad_x) = one_microbatch(per_example, given["loss_target"])
    else:
        def body(carry, xs):
            loss_sum, grad_sum = carry
            l_k, (gw_k, gx_k) = one_microbatch(xs[0], xs[1])
            with _jax.named_scope("update"):
                return (loss_sum + l_k, _jax.tree.map(_jnp.add, grad_sum, gw_k)), gx_k

        init = (_jnp.zeros((), _jnp.float32), _jax.tree.map(_jnp.zeros_like, weights))
        (loss, grad_w), grad_x = _jax.lax.scan(body, init, (per_example, given["loss_target"]))
    with _jax.named_scope("update"):
        delta_w, new_m, new_v = {}, {}, {}
        for n in TWIN_WEIGHTS:
            delta_w[n], new_m[n], new_v[n] = _adamw(weights[n], grad_w[n], given["m_" + n], given["v_" + n])
    return (loss, grad_x, *[grad_w[n] for n in TWIN_WEIGHTS], *[delta_w[n] for n in TWIN_WEIGHTS],
            *[new_m[n] for n in TWIN_WEIGHTS], *[new_v[n] for n in TWIN_WEIGHTS])
```

```python
import functools

import numpy as np
import jax
import jax.numpy as jnp
from jax import lax
from jax.experimental import pallas as pl
from jax.experimental.pallas import tpu as pltpu

F32 = jnp.float32
BF16 = jnp.bfloat16

N_DEV = 8
SEQ = 2048
D_MODEL = 1024
D_POOL = 512
D_ATTN = 512
D_KV = 128
D_IN = 2304
N_HEADS = 8
GQA = 4
HEAD_DIM = 64
BLOCK = 128
N_BLOCKS = SEQ // BLOCK
POOL_WINDOWS = (2, 4, 8, 16)
DEPTH = 2
EPS = 1e-6
NEG_INF = -1e30
SCALE = HEAD_DIM ** -0.5

COL_U, COL_PG, COL_Q, COL_K, COL_V, COL_AG = 0, 512, 1024, 1536, 1664, 1792

ADAM_LR = 0.001
ADAM_B1 = 0.9
ADAM_B2 = 0.999
ADAM_EPS = 1e-08
ADAM_WD = 0.01
ADAM_STEP = 10

TOKEN_TILE = 256
VMEM_LIMIT = 56 * 1024 * 1024
MESH = pl.DeviceIdType.MESH

SMALL_ROWS = 1072


def _nn(a, b):
    return jnp.dot(a, b, preferred_element_type=F32)


def _nt(a, b):
    return lax.dot_general(a, b, (((1,), (1,)), ((), ())), preferred_element_type=F32)


def _t_bf16(a32):
    return jnp.transpose(a32).astype(BF16)


def _silu_parts(g):
    s = jax.nn.sigmoid(g)
    return g * s, s * (1.0 + g * (1.0 - s))


def _compiler_params(sem=None):
    if sem is None:
        return pltpu.CompilerParams(vmem_limit_bytes=VMEM_LIMIT)
    return pltpu.CompilerParams(dimension_semantics=sem, vmem_limit_bytes=VMEM_LIMIT)


def _attn_bias():
    t = np.arange(BLOCK)[:, None]
    j = np.arange(2 * BLOCK)[None, :]
    dist = t + BLOCK - j
    in_win = (dist >= 0) & (dist < BLOCK)
    out = np.zeros((2, 2, BLOCK, GQA * 2 * BLOCK), np.float32)
    for variant in range(2):
        valid = in_win & ((j >= BLOCK) | (variant == 1))
        for kv in range(2):
            for g in range(GQA):
                slope = np.float32(2.0 ** (-(kv * GQA + g + 1)))
                b = np.where(valid, -slope * dist.astype(np.float32), np.float32(NEG_INF))
                out[variant, kv, :, g * 256:(g + 1) * 256] = b
    return out


def _block_diag(kx, kv):
    rolled = pltpu.roll(kx, 64, 1)
    lane = lax.broadcasted_iota(jnp.int32, kx.shape, 1)
    dup = jnp.where(lane < 64, kx, rolled) if kv == 0 else jnp.where(lane < 64, rolled, kx)
    rep = jnp.concatenate([dup, dup], axis=1).astype(BF16)
    lane2 = lax.broadcasted_iota(jnp.int32, rep.shape, 1)
    zero = jnp.zeros_like(rep)
    parts = [jnp.where((lane2 >= 64 * g) & (lane2 < 64 * g + 64), rep, zero) for g in range(GQA)]
    return jnp.concatenate(parts, axis=0)


def _diag_fold(m):
    lane = lax.broadcasted_iota(jnp.int32, (256, 256), 1)
    r = jnp.where(lane < 64, m[0:256], jnp.where(lane < 128, m[256:512], jnp.where(lane < 192, m[512:768], m[768:1024])))
    h = r[:, 0:128] + r[:, 128:256]
    return h + pltpu.roll(h, 64, 1)


def _window_sum(ext, w, forward):
    s = ext
    sh = 1
    while sh < w:
        s = s + pltpu.roll(s, (256 - sh) if forward else sh, 0)
        sh *= 2
    return s


def _inv_count(n, w):
    t = n * BLOCK + lax.broadcasted_iota(jnp.int32, (BLOCK, 1), 0) + 1
    return 1.0 / jnp.minimum(t.astype(F32), float(w))


def _kv_ext(ref, n):
    r0 = pl.multiple_of(jnp.maximum(n - 1, 0) * BLOCK, BLOCK)
    r1 = pl.multiple_of(n * BLOCK, BLOCK)
    return jnp.concatenate([ref[pl.ds(r0, BLOCK), :], ref[pl.ds(r1, BLOCK), :]], axis=0)


def _softmax_chunk(sg, sink):
    m = jnp.maximum(jnp.max(sg, axis=-1, keepdims=True), sink)
    p = jnp.exp(sg - m)
    esink = jnp.exp(sink - m)
    rl = 1.0 / (jnp.sum(p, axis=-1, keepdims=True) + esink)
    return p * rl, esink * rl


def _fwd_in(x, g_pre, w_in):
    tm = TOKEN_TILE

    def body(x_ref, g_ref, w_ref, u_ref, pg_ref, q_ref, k_ref, v_ref, ag_ref):
        xv = x_ref[...]
        r = lax.rsqrt(jnp.mean(xv * xv, axis=-1, keepdims=True) + EPS)
        h = (xv * r * g_ref[...]).astype(BF16)
        u_ref[...] = _nn(h, w_ref[:, COL_U:COL_PG])
        pg_ref[...] = _nn(h, w_ref[:, COL_PG:COL_Q])
        q_ref[...] = _nn(h, w_ref[:, COL_Q:COL_K]).astype(BF16)
        k_ref[...] = _nn(h, w_ref[:, COL_K:COL_V])
        v_ref[...] = _nn(h, w_ref[:, COL_V:COL_AG])
        ag_ref[...] = _nn(h, w_ref[:, COL_AG:D_IN])

    row = lambda c: pl.BlockSpec((tm, c), lambda i: (i, 0))
    return pl.pallas_call(
        body, name="fwd_in", grid=(SEQ // tm,),
        in_specs=[row(D_MODEL), pl.BlockSpec((1, D_MODEL), lambda i: (0, 0)),
                  pl.BlockSpec((D_MODEL, D_IN), lambda i: (0, 0))],
        out_specs=[row(D_POOL), row(D_POOL), row(D_ATTN), row(D_KV), row(D_KV), row(D_ATTN)],
        out_shape=[jax.ShapeDtypeStruct((SEQ, D_POOL), F32), jax.ShapeDtypeStruct((SEQ, D_POOL), F32),
                   jax.ShapeDtypeStruct((SEQ, D_ATTN), BF16), jax.ShapeDtypeStruct((SEQ, D_KV), F32),
                   jax.ShapeDtypeStruct((SEQ, D_KV), F32), jax.ShapeDtypeStruct((SEQ, D_ATTN), F32)],
        compiler_params=_compiler_params(("arbitrary",)),
    )(x, g_pre, w_in)


def _fwd_mix(sinks, u, pg, q, k, v, ag, pw, scale, bias):
    def body(sink_ref, u_ref, pg_ref, q_ref, k_ref, v_ref, ag_ref, pw_ref, sc_ref, bias_ref,
             z_ref, a_ref, uprev):
        n = pl.program_id(0)

        @pl.when(n == 0)
        def _():
            uprev[...] = jnp.zeros_like(uprev)

        uv = u_ref[...]
        ext = jnp.concatenate([uprev[...], uv], axis=0)
        uprev[...] = uv
        for g, w in enumerate(POOL_WINDOWS):
            cs = slice(BLOCK * g, BLOCK * (g + 1))
            win = _window_sum(ext[:, cs], w, forward=False)[BLOCK:]
            pooled = win * _inv_count(n, w) - uv[:, cs]
            mixed = _nn(pooled.astype(BF16), pw_ref[g])
            gate, _ = _silu_parts(pg_ref[:, cs])
            z_ref[:, cs] = (mixed * sc_ref[:, cs] * gate).astype(BF16)

        kx = _kv_ext(k_ref, n)
        vx = _kv_ext(v_ref, n)
        for kv in range(2):
            cs = slice(256 * kv, 256 * (kv + 1))
            bk = _block_diag(kx, kv)
            bv = _block_diag(vx, kv)
            s = _nt(q_ref[:, cs], bk) * SCALE + bias_ref[0, kv]
            ps = []
            for g in range(GQA):
                p, _ = _softmax_chunk(s[:, 256 * g:256 * (g + 1)], sink_ref[kv * GQA + g])
                ps.append(p.astype(BF16))
            o = _nn(jnp.concatenate(ps, axis=1), bv)
            a_ref[:, cs] = o
            gate, _ = _silu_parts(ag_ref[:, cs])
            z_ref[:, D_POOL + 256 * kv:D_POOL + 256 * (kv + 1)] = (o * gate).astype(BF16)

    blk = lambda c: pl.BlockSpec((BLOCK, c), lambda n: (n, 0))
    full = lambda shape: pl.BlockSpec(shape, lambda n: (0,) * len(shape))
    return pl.pallas_call(
        body, name="fwd_mix", grid=(N_BLOCKS,),
        in_specs=[pl.BlockSpec(memory_space=pltpu.SMEM), blk(D_POOL), blk(D_POOL), blk(D_ATTN),
                  full((SEQ, D_KV)), full((SEQ, D_KV)), blk(D_ATTN), full((4, BLOCK, BLOCK)), full((1, D_POOL)),
                  pl.BlockSpec((1, 2, BLOCK, 1024), lambda n: (jnp.minimum(n, 1), 0, 0, 0))],
        out_specs=[blk(D_MODEL), blk(D_ATTN)],
        out_shape=[jax.ShapeDtypeStruct((SEQ, D_MODEL), BF16), jax.ShapeDtypeStruct((SEQ, D_ATTN), F32)],
        scratch_shapes=[pltpu.VMEM((BLOCK, D_POOL), F32)],
        compiler_params=_compiler_params(("arbitrary",)),
    )(sinks, u, pg, q, k, v, ag, pw, scale, bias)


def _fwd_out(z, x, g_post, w_out):
    tm = TOKEN_TILE

    def body(z_ref, x_ref, g_ref, w_ref, xn_ref, y_ref):
        y = _nn(z_ref[...], w_ref[...])
        y_ref[...] = y
        r = lax.rsqrt(jnp.mean(y * y, axis=-1, keepdims=True) + EPS)
        xn_ref[...] = x_ref[...] + y * r * g_ref[...]

    row = lambda c: pl.BlockSpec((tm, c), lambda i: (i, 0))
    return pl.pallas_call(
        body, name="fwd_out", grid=(SEQ // tm,),
        in_specs=[row(D_MODEL), row(D_MODEL), pl.BlockSpec((1, D_MODEL), lambda i: (0, 0)),
                  pl.BlockSpec((D_MODEL, D_MODEL), lambda i: (0, 0))],
        out_specs=[row(D_MODEL), row(D_MODEL)],
        out_shape=[jax.ShapeDtypeStruct((SEQ, D_MODEL), F32), jax.ShapeDtypeStruct((SEQ, D_MODEL), F32)],
        compiler_params=_compiler_params(("arbitrary",)),
    )(z, x, g_post, w_out)


def _loss_grad(xf, target):
    tm = TOKEN_TILE

    def body(x_ref, t_ref, loss_ref, dx_ref):
        @pl.when(pl.program_id(0) == 0)
        def _():
            loss_ref[...] = jnp.zeros_like(loss_ref)

        d = x_ref[...] - t_ref[...]
        dx_ref[...] = d * (1.0 / D_MODEL)
        part = jnp.sum(d * d, axis=-1, keepdims=True) * (1.0 / D_MODEL)
        loss_ref[...] += 0.5 * jnp.sum(part, axis=0, keepdims=True)

    row = pl.BlockSpec((tm, D_MODEL), lambda i: (i, 0))
    return pl.pallas_call(
        body, name="loss_grad", grid=(SEQ // tm,),
        in_specs=[row, row],
        out_specs=[pl.BlockSpec((1, 1), lambda i: (0, 0)), row],
        out_shape=[jax.ShapeDtypeStruct((1, 1), F32), jax.ShapeDtypeStruct((SEQ, D_MODEL), F32)],
        compiler_params=_compiler_params(("arbitrary",)),
    )(xf, target)


def _bwd_out(dxo, y, z, g_post, w_out):
    tm = TOKEN_TILE
    steps = SEQ // tm

    def body(dxo_ref, y_ref, z_ref, g_ref, w_ref, dz_ref, dw_ref, dg_ref, acc):
        i = pl.program_id(0)

        @pl.when(i == 0)
        def _():
            acc[...] = jnp.zeros_like(acc)
            dg_ref[...] = jnp.zeros_like(dg_ref)

        y = y_ref[...]
        dxo_v = dxo_ref[...]
        r = lax.rsqrt(jnp.mean(y * y, axis=-1, keepdims=True) + EPS)
        yn = y * r
        dg_ref[...] += jnp.sum(dxo_v * yn, axis=0, keepdims=True)
        dyn = dxo_v * g_ref[...]
        dy = (r * (dyn - yn * jnp.mean(dyn * yn, axis=-1, keepdims=True))).astype(BF16)
        dz_ref[...] = _nt(dy, w_ref[...])
        acc[...] += _nn(_t_bf16(z_ref[...].astype(F32)), dy)

        @pl.when(i == steps - 1)
        def _():
            dw_ref[...] = acc[...].astype(BF16)

    row = lambda c: pl.BlockSpec((tm, c), lambda i: (i, 0))
    const = lambda shape: pl.BlockSpec(shape, lambda i: (0,) * len(shape))
    return pl.pallas_call(
        body, name="bwd_out", grid=(steps,),
        in_specs=[row(D_MODEL), row(D_MODEL), row(D_MODEL), const((1, D_MODEL)), const((D_MODEL, D_MODEL))],
        out_specs=[row(D_MODEL), const((D_MODEL, D_MODEL)), const((1, D_MODEL))],
        out_shape=[jax.ShapeDtypeStruct((SEQ, D_MODEL), F32), jax.ShapeDtypeStruct((D_MODEL, D_MODEL), BF16),
                   jax.ShapeDtypeStruct((1, D_MODEL), F32)],
        scratch_shapes=[pltpu.VMEM((D_MODEL, D_MODEL), F32)],
        compiler_params=_compiler_params(("arbitrary",)),
    )(dxo, y, z, g_post, w_out)


def _bwd_mix(sinks, dz, u, pg, q, k, v, ag, a, pw, scale, bias):
    last = N_BLOCKS - 1

    def body(sink_ref, dz_ref, u_ref, up_ref, pg_ref, q_ref, k_ref, v_ref, ag_ref, a_ref, pw_ref, sc_ref,
             bias_ref, dp_ref, dpw_ref, dsc_ref, dsink_ref, ck, cv, ce):
        i = pl.program_id(0)
        n = last - i

        @pl.when(i == 0)
        def _():
            ck[...] = jnp.zeros_like(ck)
            cv[...] = jnp.zeros_like(cv)
            ce[...] = jnp.zeros_like(ce)
            dpw_ref[...] = jnp.zeros_like(dpw_ref)
            dsc_ref[...] = jnp.zeros_like(dsc_ref)
            dsink_ref[...] = jnp.zeros_like(dsink_ref)

        uv = u_ref[...]
        has_prev = (n > 0).astype(F32)
        ext = jnp.concatenate([up_ref[...] * has_prev, uv], axis=0)
        for g, w in enumerate(POOL_WINDOWS):
            cs = slice(BLOCK * g, BLOCK * (g + 1))
            inv = _inv_count(n, w)
            win = _window_sum(ext[:, cs], w, forward=False)[BLOCK:]
            pooled = win * inv - uv[:, cs]
            pw_g = pw_ref[g]
            mixed = _nn(pooled.astype(BF16), pw_g)
            gate, dgate = _silu_parts(pg_ref[:, cs])
            dzp = dz_ref[:, cs]
            sc = sc_ref[:, cs]
            dpm = dzp * gate
            dp_ref[:, COL_PG + BLOCK * g:COL_PG + BLOCK * (g + 1)] = (dzp * (mixed * sc) * dgate).astype(BF16)
            dsc_ref[:, cs] += jnp.sum(dpm * mixed, axis=0, keepdims=True)
            dmixed = (dpm * sc).astype(BF16)
            dpw_ref[g] += _nn(_t_bf16(pooled), dmixed)
            dpooled = _nt(dmixed, pw_g)
            e = dpooled * inv
            lead = _window_sum(jnp.concatenate([e, ce[:, cs]], axis=0), w, forward=True)[:BLOCK]
            dp_ref[:, COL_U + BLOCK * g:COL_U + BLOCK * (g + 1)] = (lead - dpooled).astype(BF16)
            ce[:, cs] = e

        kx = _kv_ext(k_ref, n)
        vx = _kv_ext(v_ref, n)
        lane = lax.broadcasted_iota(jnp.int32, (1, 128), 1)
        tks, tvs = [], []
        for kv in range(2):
            cs = slice(256 * kv, 256 * (kv + 1))
            bk = _block_diag(kx, kv)
            bv = _block_diag(vx, kv)
            qv = q_ref[:, cs]
            s = _nt(qv, bk) * SCALE + bias_ref[0, kv]
            gate, dgate = _silu_parts(ag_ref[:, cs])
            dza = dz_ref[:, D_POOL + 256 * kv:D_POOL + 256 * (kv + 1)]
            dp_ref[:, COL_AG + 256 * kv:COL_AG + 256 * (kv + 1)] = (dza * a_ref[:, cs] * dgate).astype(BF16)
            da = dza * gate
            dab = da.astype(BF16)
            dpr = _nt(dab, bv)
            ps, dss = [], []
            for g in range(GQA):
                gs = slice(256 * g, 256 * (g + 1))
                p, psink = _softmax_chunk(s[:, gs], sink_ref[kv * GQA + g])
                dpg_ = dpr[:, gs]
                delta = jnp.sum(p * dpg_, axis=-1, keepdims=True)
                dsink = -jnp.sum(psink * delta, axis=0, keepdims=True)
                dsink_ref[...] += jnp.where(lane == kv * GQA + g, dsink, 0.0)
                ps.append(p)
                dss.append(p * (dpg_ - delta) * SCALE)
            p_all = jnp.concatenate(ps, axis=1)
            ds_all = jnp.concatenate(dss, axis=1)
            dp_ref[:, COL_Q + 256 * kv:COL_Q + 256 * (kv + 1)] = _nn(ds_all.astype(BF16), bk).astype(BF16)
            tks.append(_diag_fold(_nn(_t_bf16(ds_all), qv)))
            tvs.append(_diag_fold(_nn(_t_bf16(p_all), dab)))
        lane2 = lax.broadcasted_iota(jnp.int32, (256, 128), 1)
        dkx = jnp.where(lane2 < 64, tks[0], tks[1])
        dvx = jnp.where(lane2 < 64, tvs[0], tvs[1])
        dp_ref[:, COL_K:COL_V] = (ck[...] + dkx[BLOCK:]).astype(BF16)
        dp_ref[:, COL_V:COL_AG] = (cv[...] + dvx[BLOCK:]).astype(BF16)
        ck[...] = dkx[:BLOCK]
        cv[...] = dvx[:BLOCK]

    blk = lambda c: pl.BlockSpec((BLOCK, c), lambda i: (last - i, 0))
    full = lambda shape: pl.BlockSpec(shape, lambda i: (0,) * len(shape))
    return pl.pallas_call(
        body, name="bwd_mix", grid=(N_BLOCKS,),
        in_specs=[pl.BlockSpec(memory_space=pltpu.SMEM), blk(D_MODEL), blk(D_POOL),
                  pl.BlockSpec((BLOCK, D_POOL), lambda i: (jnp.maximum(last - i - 1, 0), 0)),
                  blk(D_POOL), blk(D_ATTN), full((SEQ, D_KV)), full((SEQ, D_KV)), blk(D_ATTN), blk(D_ATTN),
                  full((4, BLOCK, BLOCK)), full((1, D_POOL)),
                  pl.BlockSpec((1, 2, BLOCK, 1024), lambda i: (jnp.minimum(last - i, 1), 0, 0, 0))],
        out_specs=[blk(D_IN), full((4, BLOCK, BLOCK)), full((1, D_POOL)), full((1, 128))],
        out_shape=[jax.ShapeDtypeStruct((SEQ, D_IN), BF16), jax.ShapeDtypeStruct((4, BLOCK, BLOCK), F32),
                   jax.ShapeDtypeStruct((1, D_POOL), F32), jax.ShapeDtypeStruct((1, 128), F32)],
        scratch_shapes=[pltpu.VMEM((BLOCK, D_KV), F32), pltpu.VMEM((BLOCK, D_KV), F32),
                        pltpu.VMEM((BLOCK, D_POOL), F32)],
        compiler_params=_compiler_params(("arbitrary",)),
    )(sinks, dz, u, u, pg, q, k, v, ag, a, pw, scale, bias)


def _bwd_in(dproj, x, dxo, g_pre, w_in):
    tm = TOKEN_TILE
    steps = SEQ // tm
    cw = 256

    def body(dp_ref, x_ref, dxo_ref, g_ref, w_ref, dx_ref, dw_ref, dg_ref, acc):
        i = pl.program_id(0)

        @pl.when(i == 0)
        def _():
            acc[...] = jnp.zeros_like(acc)
            dg_ref[...] = jnp.zeros_like(dg_ref)

        xv = x_ref[...]
        gv = g_ref[...]
        r = lax.rsqrt(jnp.mean(xv * xv, axis=-1, keepdims=True) + EPS)
        xn = xv * r
        ht = _t_bf16(xn * gv)
        dh = _nt(dp_ref[...], w_ref[...])
        for c in range(0, D_IN, cw):
            acc[:, c:c + cw] += _nn(ht, dp_ref[:, c:c + cw])
        dg_ref[...] += jnp.sum(dh * xn, axis=0, keepdims=True)
        dhn = dh * gv
        dx_ref[...] = dxo_ref[...] + r * (dhn - xn * jnp.mean(dhn * xn, axis=-1, keepdims=True))

        @pl.when(i == steps - 1)
        def _():
            dw_ref[...] = acc[...].astype(BF16)

    row = lambda c: pl.BlockSpec((tm, c), lambda i: (i, 0))
    const = lambda shape: pl.BlockSpec(shape, lambda i: (0,) * len(shape))
    return pl.pallas_call(
        body, name="bwd_in", grid=(steps,),
        in_specs=[row(D_IN), row(D_MODEL), row(D_MODEL), const((1, D_MODEL)), const((D_MODEL, D_IN))],
        out_specs=[row(D_MODEL), const((D_MODEL, D_IN)), const((1, D_MODEL))],
        out_shape=[jax.ShapeDtypeStruct((SEQ, D_MODEL), F32), jax.ShapeDtypeStruct((D_MODEL, D_IN), BF16),
                   jax.ShapeDtypeStruct((1, D_MODEL), F32)],
        scratch_shapes=[pltpu.VMEM((D_MODEL, D_IN), F32)],
        compiler_params=_compiler_params(("arbitrary",)),
    )(dproj, x, dxo, g_pre, w_in)


def _mesh_pos():
    return lax.axis_index("x"), lax.axis_index("y"), lax.axis_index("c")


def _allgather(arrs, name):
    na = len(arrs)

    def body(*refs):
        xs, outs = refs[:na], refs[na:2 * na]
        send_sems, recv_sems, local_sems = refs[2 * na:]
        x, y, c = _mesh_pos()
        me, sibling = (x, y, c), (x, y, 1 - c)
        chips = [(1 - x, y), (x, 1 - y), (1 - x, 1 - y)]

        def slot(a, px, py, pc):
            return outs[a].at[4 * px + 2 * py + pc]

        def copy(a, k, block, to, src=None):
            return pltpu.make_async_remote_copy(
                src_ref=slot(a, *block) if src is None else src, dst_ref=slot(a, *block),
                send_sem=send_sems.at[a, k], recv_sem=recv_sems.at[a, k], device_id=to, device_id_type=MESH)

        mine = [pltpu.make_async_copy(xs[a], slot(a, *me), local_sems.at[a]) for a in range(na)]
        for cp in mine:
            cp.start()
        first = []
        for a in range(na):
            first.append(copy(a, 0, me, sibling, src=xs[a]))
            first += [copy(a, 1 + j, me, (*chip, c), src=xs[a]) for j, chip in enumerate(chips)]
        for cp in first:
            cp.start()
        passed = []
        for j, chip in enumerate(chips):
            for a in range(na):
                copy(a, 1 + j, (*chip, c), me).wait_recv()
                cp = copy(a, 4 + j, (*chip, c), sibling)
                cp.start()
                passed.append(cp)
        for a in range(na):
            copy(a, 0, sibling, me).wait_recv()
            for j, chip in enumerate(chips):
                copy(a, 4 + j, (*chip, 1 - c), me).wait_recv()
        for cp in first + passed:
            cp.wait_send()
        for cp in mine:
            cp.wait()

    vmem = pl.BlockSpec(memory_space=pltpu.VMEM)
    return pl.pallas_call(
        body, name=name,
        in_specs=[vmem] * na, out_specs=[vmem] * na,
        out_shape=[jax.ShapeDtypeStruct((N_DEV,) + t.shape, t.dtype) for t in arrs],
        scratch_shapes=[pltpu.SemaphoreType.DMA((na, 7)), pltpu.SemaphoreType.DMA((na, 7)),
                        pltpu.SemaphoreType.DMA((na,))],
        compiler_params=_compiler_params(),
    )(*arrs)


def _reduce_scatter(arrs, name):
    na = len(arrs)
    rows_per_step = 256

    def body(*refs):
        gs, outs = refs[:na], refs[na:2 * na]
        recv, own = refs[2 * na:3 * na], refs[3 * na:4 * na]
        send_sems, recv_sems, local_sems = refs[4 * na:]
        x, y, c = _mesh_pos()
        me = 4 * x + 2 * y + c
        mine = [pltpu.make_async_copy(gs[a].at[me], own[a], local_sems.at[a]) for a in range(na)]
        for cp in mine:
            cp.start()
        copies = []
        for k in range(1, N_DEV):
            px, py, pc = x ^ ((k >> 2) & 1), y ^ ((k >> 1) & 1), c ^ (k & 1)
            for a in range(na):
                cp = pltpu.make_async_remote_copy(
                    src_ref=gs[a].at[4 * px + 2 * py + pc], dst_ref=recv[a].at[k - 1],
                    send_sem=send_sems.at[a, k - 1], recv_sem=recv_sems.at[a, k - 1],
                    device_id=(px, py, pc), device_id_type=MESH)
                cp.start()
                copies.append(cp)
        for cp in mine:
            cp.wait()
        for cp in copies:
            cp.wait()
        for a in range(na):
            m = arrs[a].shape[1]

            def step(i, carry, a=a):
                rs = pl.ds(pl.multiple_of(i * rows_per_step, rows_per_step), rows_per_step)
                acc = own[a][rs, :].astype(F32)
                for k in range(N_DEV - 1):
                    acc = acc + recv[a][k, rs, :].astype(F32)
                outs[a][rs, :] = acc
                return carry

            lax.fori_loop(0, m // rows_per_step, step, 0)

    vmem = pl.BlockSpec(memory_space=pltpu.VMEM)
    return pl.pallas_call(
        body, name=name,
        in_specs=[pl.BlockSpec(memory_space=pl.ANY)] * na, out_specs=[vmem] * na,
        out_shape=[jax.ShapeDtypeStruct(t.shape[1:], F32) for t in arrs],
        scratch_shapes=([pltpu.VMEM((N_DEV - 1,) + t.shape[1:], BF16) for t in arrs]
                        + [pltpu.VMEM(t.shape[1:], BF16) for t in arrs]
                        + [pltpu.SemaphoreType.DMA((na, N_DEV - 1)), pltpu.SemaphoreType.DMA((na, N_DEV - 1)),
                           pltpu.SemaphoreType.DMA((na,))]),
        compiler_params=_compiler_params(),
    )(*arrs)


def _adamw_math(w, g, m, v):
    m = ADAM_B1 * m + (1.0 - ADAM_B1) * g
    v = ADAM_B2 * v + (1.0 - ADAM_B2) * (g * g)
    m_hat = m / (1.0 - ADAM_B1 ** ADAM_STEP)
    v_hat = v / (1.0 - ADAM_B2 ** ADAM_STEP)
    delta = -ADAM_LR * (m_hat / (jnp.sqrt(v_hat) + ADAM_EPS) + ADAM_WD * w)
    return delta, m, v


def _adamw(w, g, m, v, name, rows):
    r, c = w.shape

    def body(w_ref, g_ref, m_ref, v_ref, d_ref, nm_ref, nv_ref):
        d, nm, nv = _adamw_math(w_ref[...], g_ref[...], m_ref[...], v_ref[...])
        d_ref[...] = d
        nm_ref[...] = nm
        nv_ref[...] = nv

    spec = pl.BlockSpec((rows, c), lambda i: (i, 0))
    return pl.pallas_call(
        body, name=name, grid=(r // rows,),
        in_specs=[spec] * 4, out_specs=[spec] * 3,
        out_shape=[jax.ShapeDtypeStruct((r, c), F32)] * 3,
        compiler_params=_compiler_params(("arbitrary",)),
    )(w, g, m, v)


def _adamw_small(w, g8, m, v):
    r, c = w.shape

    def body(w_ref, g_ref, m_ref, v_ref, go_ref, d_ref, nm_ref, nv_ref):
        g = g_ref[0]
        for j in range(1, N_DEV):
            g = g + g_ref[j]
        go_ref[...] = g
        d, nm, nv = _adamw_math(w_ref[...], g, m_ref[...], v_ref[...])
        d_ref[...] = d
        nm_ref[...] = nm
        nv_ref[...] = nv

    return pl.pallas_call(
        body, name="adamw_small",
        out_shape=[jax.ShapeDtypeStruct((r, c), F32)] * 4,
        compiler_params=_compiler_params(),
    )(w, g8, m, v)


def _pack_small(pool_w, pool_scale, sinks, norm_pre, norm_post):
    return jnp.concatenate([
        pool_w.reshape(1024, 128), pool_scale.reshape(8, 128),
        jnp.pad(sinks, ((0, 6), (0, 120))), norm_pre.reshape(16, 128), norm_post.reshape(16, 128)], axis=0)


def _unpack_small(p):
    return (p[0:1024].reshape(DEPTH, 4, BLOCK, BLOCK), p[1024:1032].reshape(DEPTH, D_POOL),
            p[1032:1034, 0:N_HEADS], p[1040:1056].reshape(DEPTH, D_MODEL), p[1056:1072].reshape(DEPTH, D_MODEL))


def kernel(x, w_in, pool_w, pool_scale, attn_sinks, w_out, norm_pre, norm_post, loss_target, m_w_in, m_pool_w, m_pool_scale, m_attn_sinks, m_w_out, m_norm_pre, m_norm_post, v_w_in, v_pool_w, v_pool_scale, v_attn_sinks, v_w_out, v_norm_pre, v_norm_post):
    cols = D_IN // N_DEV
    rows_out = D_MODEL // N_DEV
    x0 = x[0]
    target = loss_target[0]
    bias = jnp.asarray(_attn_bias())

    win_g, wout_g = _allgather(
        [w_in.astype(BF16).reshape(DEPTH * D_MODEL, cols), w_out.astype(BF16).reshape(DEPTH * rows_out, D_MODEL)],
        "gather_weights")
    win_full = win_g.reshape(N_DEV, DEPTH, D_MODEL, cols).transpose(1, 2, 0, 3).reshape(DEPTH, D_MODEL, D_IN)
    wout_full = wout_g.reshape(N_DEV, DEPTH, rows_out, D_MODEL).transpose(1, 0, 2, 3).reshape(DEPTH, D_MODEL, D_MODEL)
    pw_b = pool_w.astype(BF16)

    saved = []
    xl = x0
    for layer in range(DEPTH):
        g_pre = norm_pre[layer].reshape(1, D_MODEL)
        g_post = norm_post[layer].reshape(1, D_MODEL)
        scale = pool_scale[layer].reshape(1, D_POOL)
        u, pg, q, k, v, ag = _fwd_in(xl, g_pre, win_full[layer])
        z, a = _fwd_mix(attn_sinks[layer], u, pg, q, k, v, ag, pw_b[layer], scale, bias)
        x_next, y = _fwd_out(z, xl, g_post, wout_full[layer])
        saved.append((xl, u, pg, q, k, v, ag, z, a, y, g_pre, g_post, scale))
        xl = x_next

    loss_part, dx = _loss_grad(xl, target)
    loss = lax.psum(loss_part[0, 0], ("x", "y", "c"))

    gw_in, gw_out, g_small = [None] * DEPTH, [None] * DEPTH, [None] * DEPTH
    for layer in reversed(range(DEPTH)):
        xin, u, pg, q, k, v, ag, z, a, y, g_pre, g_post, scale = saved[layer]
        dz, gw_out[layer], dg_post = _bwd_out(dx, y, z, g_post, wout_full[layer])
        dproj, dpw, dsc, dsink = _bwd_mix(attn_sinks[layer], dz, u, pg, q, k, v, ag, a, pw_b[layer], scale, bias)
        dx, gw_in[layer], dg_pre = _bwd_in(dproj, xin, dx, g_pre, win_full[layer])
        g_small[layer] = (dpw, dsc, dsink[:, 0:N_HEADS], dg_pre, dg_post)

    gin_blocks = jnp.stack(gw_in).reshape(DEPTH, D_MODEL, N_DEV, cols).transpose(2, 0, 1, 3).reshape(N_DEV, DEPTH * D_MODEL, cols)
    gout_blocks = jnp.stack(gw_out).reshape(DEPTH, N_DEV, rows_out, D_MODEL).transpose(1, 0, 2, 3).reshape(N_DEV, DEPTH * rows_out, D_MODEL)
    g_in, g_out = _reduce_scatter([gin_blocks, gout_blocks], "scatter_grads")

    small_part = _pack_small(
        jnp.stack([g[0] for g in g_small]), jnp.concatenate([g[1] for g in g_small], axis=0),
        jnp.concatenate([g[2] for g in g_small], axis=0), jnp.concatenate([g[3] for g in g_small], axis=0),
        jnp.concatenate([g[4] for g in g_small], axis=0))
    (small_all,) = _allgather([small_part], "gather_small")
    gs, ds, ms, vs = _adamw_small(
        _pack_small(pool_w, pool_scale, attn_sinks, norm_pre, norm_post), small_all,
        _pack_small(m_pool_w, m_pool_scale, m_attn_sinks, m_norm_pre, m_norm_post),
        _pack_small(v_pool_w, v_pool_scale, v_attn_sinks, v_norm_pre, v_norm_post))

    d_in, nm_in, nv_in = _adamw(w_in.reshape(DEPTH * D_MODEL, cols), g_in, m_w_in.reshape(DEPTH * D_MODEL, cols),
                                v_w_in.reshape(DEPTH * D_MODEL, cols), "adamw_in", 512)
    d_out, nm_out, nv_out = _adamw(w_out.reshape(DEPTH * rows_out, D_MODEL), g_out,
                                   m_w_out.reshape(DEPTH * rows_out, D_MODEL),
                                   v_w_out.reshape(DEPTH * rows_out, D_MODEL), "adamw_out", 128)

    def assemble(big_in, small, big_out):
        pw_, sc_, sk_, npre_, npost_ = _unpack_small(small)
        return [big_in.reshape(DEPTH, D_MODEL, cols), pw_, sc_, sk_, big_out.reshape(DEPTH, rows_out, D_MODEL), npre_, npost_]

    return (loss, dx.reshape(1, SEQ, D_MODEL), *assemble(g_in, gs, g_out), *assemble(d_in, ds, d_out),
            *assemble(nm_in, ms, nm_out), *assemble(nv_in, vs, nv_out))
```

```python
import numpy as np
import jax
import jax.numpy as jnp
from jax import lax
from jax.experimental import pallas as pl
from jax.experimental.pallas import tpu as pltpu

F32 = jnp.float32
BF16 = jnp.bfloat16

N_DEV = 8
SEQ = 2048
D_MODEL = 1024
D_POOL = 512
D_ATTN = 512
D_KV = 128
D_IN = 2304
N_HEADS = 8
GQA = 4
HEAD_DIM = 64
BLOCK = 128
N_BLOCKS = SEQ // BLOCK
POOL_WINDOWS = (2, 4, 8, 16)
DEPTH = 2
EPS = 1e-6
NEG_INF = -1e30
SCALE = HEAD_DIM ** -0.5
IN_SHARD = D_IN // N_DEV
OUT_SHARD = D_MODEL // N_DEV

COL_U, COL_PG, COL_Q, COL_K, COL_V, COL_AG = 0, 512, 1024, 1536, 1664, 1792

ADAM_LR = 0.001
ADAM_B1 = 0.9
ADAM_B2 = 0.999
ADAM_EPS = 1e-08
ADAM_WD = 0.01
ADAM_STEP = 10

TOKEN_TILE = 256
VMEM_LIMIT = 56 * 1024 * 1024
MESH = pl.DeviceIdType.MESH

ROW_PW, ROW_SC, ROW_SINK, ROW_NPRE, ROW_NPOST, ROW_LOSS = 0, 512, 520, 528, 536, 544
PACK_ROWS = 576
PACK_SLICE = PACK_ROWS // N_DEV


def _nn(a, b):
    return jnp.dot(a, b, preferred_element_type=F32)


def _nt(a, b):
    return lax.dot_general(a, b, (((1,), (1,)), ((), ())), preferred_element_type=F32)


def _t_bf16(a32):
    return jnp.transpose(a32).astype(BF16)


def _silu_parts(g):
    s = jax.nn.sigmoid(g)
    return g * s, s * (1.0 + g * (1.0 - s))


def _compiler_params(sem=None):
    if sem is None:
        return pltpu.CompilerParams(vmem_limit_bytes=VMEM_LIMIT)
    return pltpu.CompilerParams(dimension_semantics=sem, vmem_limit_bytes=VMEM_LIMIT)


def _attn_bias():
    t = np.arange(BLOCK)[:, None]
    j = np.arange(2 * BLOCK)[None, :]
    dist = t + BLOCK - j
    in_win = (dist >= 0) & (dist < BLOCK)
    out = np.zeros((2, 2, BLOCK, GQA * 2 * BLOCK), np.float32)
    for variant in range(2):
        valid = in_win & ((j >= BLOCK) | (variant == 1))
        for kv in range(2):
            for g in range(GQA):
                slope = np.float32(2.0 ** (-(kv * GQA + g + 1)))
                b = np.where(valid, -slope * dist.astype(np.float32), np.float32(NEG_INF))
                out[variant, kv, :, g * 256:(g + 1) * 256] = b
    return out


def _block_diag(kx, kv):
    rolled = pltpu.roll(kx, 64, 1)
    lane = lax.broadcasted_iota(jnp.int32, kx.shape, 1)
    dup = jnp.where(lane < 64, kx, rolled) if kv == 0 else jnp.where(lane < 64, rolled, kx)
    rep = jnp.concatenate([dup, dup], axis=1).astype(BF16)
    lane2 = lax.broadcasted_iota(jnp.int32, rep.shape, 1)
    zero = jnp.zeros_like(rep)
    parts = [jnp.where((lane2 >= 64 * g) & (lane2 < 64 * g + 64), rep, zero) for g in range(GQA)]
    return jnp.concatenate(parts, axis=0)


def _diag_fold(m):
    lane = lax.broadcasted_iota(jnp.int32, (256, 256), 1)
    r = jnp.where(lane < 64, m[0:256], jnp.where(lane < 128, m[256:512], jnp.where(lane < 192, m[512:768], m[768:1024])))
    h = r[:, 0:128] + r[:, 128:256]
    return h + pltpu.roll(h, 64, 1)


def _window_sum(ext, w, forward):
    s = ext
    sh = 1
    while sh < w:
        s = s + pltpu.roll(s, (256 - sh) if forward else sh, 0)
        sh *= 2
    return s


def _inv_count(n, w):
    t = n * BLOCK + lax.broadcasted_iota(jnp.int32, (BLOCK, 1), 0) + 1
    return 1.0 / jnp.minimum(t.astype(F32), float(w))


def _kv_ext(ref, n):
    r0 = pl.multiple_of(jnp.maximum(n - 1, 0) * BLOCK, BLOCK)
    r1 = pl.multiple_of(n * BLOCK, BLOCK)
    return jnp.concatenate([ref[pl.ds(r0, BLOCK), :], ref[pl.ds(r1, BLOCK), :]], axis=0)


def _softmax_chunk(sg, sink):
    m = jnp.maximum(jnp.max(sg, axis=-1, keepdims=True), sink)
    p = jnp.exp(sg - m)
    esink = jnp.exp(sink - m)
    rl = 1.0 / (jnp.sum(p, axis=-1, keepdims=True) + esink)
    return p * rl, esink * rl


def _rows_of(vec_ref, pack_ref, row0):
    for r in range(D_MODEL // 128):
        pack_ref[row0 + r:row0 + r + 1, :] = vec_ref[:, 128 * r:128 * (r + 1)]


def _fwd_in(layer, x, norm_pre, w_in_t):
    tm = TOKEN_TILE

    def body(x_ref, g_ref, w_ref, u_ref, pg_ref, q_ref, k_ref, v_ref, ag_ref):
        xv = x_ref[...]
        r = lax.rsqrt(jnp.mean(xv * xv, axis=-1, keepdims=True) + EPS)
        h = (xv * r * g_ref[layer:layer + 1, :]).astype(BF16)
        u_ref[...] = _nt(h, w_ref[COL_U:COL_PG, :])
        pg_ref[...] = _nt(h, w_ref[COL_PG:COL_Q, :])
        q_ref[...] = _nt(h, w_ref[COL_Q:COL_K, :]).astype(BF16)
        k_ref[...] = _nt(h, w_ref[COL_K:COL_V, :])
        v_ref[...] = _nt(h, w_ref[COL_V:COL_AG, :])
        ag_ref[...] = _nt(h, w_ref[COL_AG:D_IN, :])

    row = lambda c: pl.BlockSpec((tm, c), lambda i: (i, 0))
    return pl.pallas_call(
        body, name=f"fwd_in{layer}", grid=(SEQ // tm,),
        in_specs=[row(D_MODEL), pl.BlockSpec((DEPTH, D_MODEL), lambda i: (0, 0)),
                  pl.BlockSpec((D_IN, D_MODEL), lambda i: (0, 0))],
        out_specs=[row(D_POOL), row(D_POOL), row(D_ATTN), row(D_KV), row(D_KV), row(D_ATTN)],
        out_shape=[jax.ShapeDtypeStruct((SEQ, D_POOL), F32), jax.ShapeDtypeStruct((SEQ, D_POOL), F32),
                   jax.ShapeDtypeStruct((SEQ, D_ATTN), BF16), jax.ShapeDtypeStruct((SEQ, D_KV), F32),
                   jax.ShapeDtypeStruct((SEQ, D_KV), F32), jax.ShapeDtypeStruct((SEQ, D_ATTN), F32)],
        compiler_params=_compiler_params(("arbitrary",)),
    )(x, norm_pre, w_in_t)


def _fwd_mix(layer, sinks, u, pg, q, k, v, ag, pool_w, pool_scale, bias):
    def body(sink_ref, u_ref, pg_ref, q_ref, k_ref, v_ref, ag_ref, pw_ref, sc_ref, bias_ref,
             z_ref, a_ref, uprev):
        n = pl.program_id(0)

        @pl.when(n == 0)
        def _():
            uprev[...] = jnp.zeros_like(uprev)

        uv = u_ref[...]
        ext = jnp.concatenate([uprev[...], uv], axis=0)
        uprev[...] = uv
        for g, w in enumerate(POOL_WINDOWS):
            cs = slice(BLOCK * g, BLOCK * (g + 1))
            win = _window_sum(ext[:, cs], w, forward=False)[BLOCK:]
            pooled = win * _inv_count(n, w) - uv[:, cs]
            mixed = _nn(pooled.astype(BF16), pw_ref[g].astype(BF16))
            gate, _ = _silu_parts(pg_ref[:, cs])
            z_ref[:, cs] = (mixed * sc_ref[layer:layer + 1, cs] * gate).astype(BF16)

        kx = _kv_ext(k_ref, n)
        vx = _kv_ext(v_ref, n)
        for kv in range(2):
            cs = slice(256 * kv, 256 * (kv + 1))
            bk = _block_diag(kx, kv)
            bv = _block_diag(vx, kv)
            s = _nt(q_ref[:, cs], bk) * SCALE + bias_ref[0, kv]
            ps = []
            for g in range(GQA):
                p, _ = _softmax_chunk(s[:, 256 * g:256 * (g + 1)], sink_ref[layer, kv * GQA + g])
                ps.append(p.astype(BF16))
            o = _nn(jnp.concatenate(ps, axis=1), bv)
            a_ref[:, cs] = o
            gate, _ = _silu_parts(ag_ref[:, cs])
            z_ref[:, D_POOL + 256 * kv:D_POOL + 256 * (kv + 1)] = (o * gate).astype(BF16)

    blk = lambda c: pl.BlockSpec((BLOCK, c), lambda n: (n, 0))
    full = lambda shape: pl.BlockSpec(shape, lambda n: (0,) * len(shape))
    return pl.pallas_call(
        body, name=f"fwd_mix{layer}", grid=(N_BLOCKS,),
        in_specs=[pl.BlockSpec(memory_space=pltpu.SMEM), blk(D_POOL), blk(D_POOL), blk(D_ATTN),
                  full((SEQ, D_KV)), full((SEQ, D_KV)), blk(D_ATTN),
                  pl.BlockSpec((None, 4, BLOCK, BLOCK), lambda n: (layer, 0, 0, 0)), full((DEPTH, D_POOL)),
                  pl.BlockSpec((1, 2, BLOCK, 1024), lambda n: (jnp.minimum(n, 1), 0, 0, 0))],
        out_specs=[blk(D_MODEL), blk(D_ATTN)],
        out_shape=[jax.ShapeDtypeStruct((SEQ, D_MODEL), BF16), jax.ShapeDtypeStruct((SEQ, D_ATTN), F32)],
        scratch_shapes=[pltpu.VMEM((BLOCK, D_POOL), F32)],
        compiler_params=_compiler_params(("arbitrary",)),
    )(sinks, u, pg, q, k, v, ag, pool_w, pool_scale, bias)


def _fwd_out(layer, z, x, norm_post, w_out):
    tm = TOKEN_TILE

    def body(z_ref, x_ref, g_ref, w_ref, xn_ref, y_ref):
        y = _nn(z_ref[...], w_ref[...])
        y_ref[...] = y
        r = lax.rsqrt(jnp.mean(y * y, axis=-1, keepdims=True) + EPS)
        xn_ref[...] = x_ref[...] + y * r * g_ref[layer:layer + 1, :]

    row = lambda c: pl.BlockSpec((tm, c), lambda i: (i, 0))
    return pl.pallas_call(
        body, name=f"fwd_out{layer}", grid=(SEQ // tm,),
        in_specs=[row(D_MODEL), row(D_MODEL), pl.BlockSpec((DEPTH, D_MODEL), lambda i: (0, 0)),
                  pl.BlockSpec((D_MODEL, D_MODEL), lambda i: (0, 0))],
        out_specs=[row(D_MODEL), row(D_MODEL)],
        out_shape=[jax.ShapeDtypeStruct((SEQ, D_MODEL), F32), jax.ShapeDtypeStruct((SEQ, D_MODEL), F32)],
        compiler_params=_compiler_params(("arbitrary",)),
    )(z, x, norm_post, w_out)


def _bwd_out(layer, top, dxo_or_xf, target, y, z, norm_post, w_out):
    tm = TOKEN_TILE
    steps = SEQ // tm

    def body(*refs):
        if top:
            xf_ref, t_ref, y_ref, z_ref, g_ref, w_ref, dxo_ref, dz_ref, dw_ref, pack_ref, acc, dg, lacc = refs
        else:
            dxi_ref, y_ref, z_ref, g_ref, w_ref, dz_ref, dw_ref, pack_ref, acc, dg, lacc = refs
        i = pl.program_id(0)

        @pl.when(i == 0)
        def _():
            acc[...] = jnp.zeros_like(acc)
            dg[...] = jnp.zeros_like(dg)
            lacc[...] = jnp.zeros_like(lacc)
            pack_ref[...] = jnp.zeros_like(pack_ref)

        if top:
            d = xf_ref[...] - t_ref[...]
            dxo_v = d * (1.0 / D_MODEL)
            dxo_ref[...] = dxo_v
            part = jnp.sum(d * d, axis=-1, keepdims=True) * (1.0 / D_MODEL)
            lacc[...] += 0.5 * jnp.sum(part, axis=0, keepdims=True)
        else:
            dxo_v = dxi_ref[...]
        y = y_ref[...]
        r = lax.rsqrt(jnp.mean(y * y, axis=-1, keepdims=True) + EPS)
        yn = y * r
        dg[...] += jnp.sum(dxo_v * yn, axis=0, keepdims=True)
        dyn = dxo_v * g_ref[layer:layer + 1, :]
        dy = (r * (dyn - yn * jnp.mean(dyn * yn, axis=-1, keepdims=True))).astype(BF16)
        dz_ref[...] = _nt(dy, w_ref[...])
        acc[...] += _nn(_t_bf16(z_ref[...].astype(F32)), dy)

        @pl.when(i == steps - 1)
        def _():
            dw_ref[...] = acc[...].astype(BF16)
            _rows_of(dg, pack_ref, ROW_NPOST)
            lane = lax.broadcasted_iota(jnp.int32, (1, 128), 1)
            pack_ref[ROW_LOSS:ROW_LOSS + 1, :] = jnp.where(lane == 0, lacc[...], 0.0)

    row = lambda c: pl.BlockSpec((tm, c), lambda i: (i, 0))
    const = lambda shape: pl.BlockSpec(shape, lambda i: (0,) * len(shape))
    act = jax.ShapeDtypeStruct((SEQ, D_MODEL), F32)
    return pl.pallas_call(
        body, name=f"bwd_out{layer}", grid=(steps,),
        in_specs=([row(D_MODEL)] * (2 if top else 1)
                  + [row(D_MODEL), row(D_MODEL), const((DEPTH, D_MODEL)), const((D_MODEL, D_MODEL))]),
        out_specs=([row(D_MODEL)] * (2 if top else 1) + [const((D_MODEL, D_MODEL)), const((PACK_ROWS, 128))]),
        out_shape=([act] * (2 if top else 1)
                   + [jax.ShapeDtypeStruct((D_MODEL, D_MODEL), BF16), jax.ShapeDtypeStruct((PACK_ROWS, 128), F32)]),
        scratch_shapes=[pltpu.VMEM((D_MODEL, D_MODEL), F32), pltpu.VMEM((1, D_MODEL), F32), pltpu.VMEM((1, 1), F32)],
        compiler_params=_compiler_params(("arbitrary",)),
    )(*((dxo_or_xf, target) if top else (dxo_or_xf,)), y, z, norm_post, w_out)


def _bwd_mix(layer, sinks, dz, u, pg, q, k, v, ag, a, pool_w, pool_scale, bias, pack):
    last = N_BLOCKS - 1

    def body(sink_ref, dz_ref, u_ref, up_ref, pg_ref, q_ref, k_ref, v_ref, ag_ref, a_ref, pw_ref, sc_ref,
             bias_ref, pin_ref, dp_ref, pack_ref, ck, cv, ce):
        i = pl.program_id(0)
        n = last - i

        @pl.when(i == 0)
        def _():
            ck[...] = jnp.zeros_like(ck)
            cv[...] = jnp.zeros_like(cv)
            ce[...] = jnp.zeros_like(ce)
            pack_ref[...] = pin_ref[...]

        uv = u_ref[...]
        has_prev = (n > 0).astype(F32)
        ext = jnp.concatenate([up_ref[...] * has_prev, uv], axis=0)
        for g, w in enumerate(POOL_WINDOWS):
            cs = slice(BLOCK * g, BLOCK * (g + 1))
            inv = _inv_count(n, w)
            win = _window_sum(ext[:, cs], w, forward=False)[BLOCK:]
            pooled = win * inv - uv[:, cs]
            pw_g = pw_ref[g].astype(BF16)
            mixed = _nn(pooled.astype(BF16), pw_g)
            gate, dgate = _silu_parts(pg_ref[:, cs])
            dzp = dz_ref[:, cs]
            sc = sc_ref[layer:layer + 1, cs]
            dpm = dzp * gate
            dp_ref[:, COL_PG + BLOCK * g:COL_PG + BLOCK * (g + 1)] = (dzp * (mixed * sc) * dgate).astype(BF16)
            pack_ref[ROW_SC + g:ROW_SC + g + 1, :] += jnp.sum(dpm * mixed, axis=0, keepdims=True)
            dmixed = (dpm * sc).astype(BF16)
            pack_ref[ROW_PW + BLOCK * g:ROW_PW + BLOCK * (g + 1), :] += _nn(_t_bf16(pooled), dmixed)
            dpooled = _nt(dmixed, pw_g)
            e = dpooled * inv
            lead = _window_sum(jnp.concatenate([e, ce[:, cs]], axis=0), w, forward=True)[:BLOCK]
            dp_ref[:, COL_U + BLOCK * g:COL_U + BLOCK * (g + 1)] = (lead - dpooled).astype(BF16)
            ce[:, cs] = e

        kx = _kv_ext(k_ref, n)
        vx = _kv_ext(v_ref, n)
        lane = lax.broadcasted_iota(jnp.int32, (1, 128), 1)
        dsink_row = jnp.zeros((1, 128), F32)
        tks, tvs = [], []
        for kv in range(2):
            cs = slice(256 * kv, 256 * (kv + 1))
            bk = _block_diag(kx, kv)
            bv = _block_diag(vx, kv)
            qv = q_ref[:, cs]
            s = _nt(qv, bk) * SCALE + bias_ref[0, kv]
            gate, dgate = _silu_parts(ag_ref[:, cs])
            dza = dz_ref[:, D_POOL + 256 * kv:D_POOL + 256 * (kv + 1)]
            dp_ref[:, COL_AG + 256 * kv:COL_AG + 256 * (kv + 1)] = (dza * a_ref[:, cs] * dgate).astype(BF16)
            da = dza * gate
            dab = da.astype(BF16)
            dpr = _nt(dab, bv)
            ps, dss = [], []
            for g in range(GQA):
                gs = slice(256 * g, 256 * (g + 1))
                p, psink = _softmax_chunk(s[:, gs], sink_ref[layer, kv * GQA + g])
                dpg_ = dpr[:, gs]
                delta = jnp.sum(p * dpg_, axis=-1, keepdims=True)
                dsink = -jnp.sum(psink * delta, axis=0, keepdims=True)
                dsink_row = dsink_row + jnp.where(lane == kv * GQA + g, dsink, 0.0)
                ps.append(p)
                dss.append(p * (dpg_ - delta) * SCALE)
            p_all = jnp.concatenate(ps, axis=1)
            ds_all = jnp.concatenate(dss, axis=1)
            dp_ref[:, COL_Q + 256 * kv:COL_Q + 256 * (kv + 1)] = _nn(ds_all.astype(BF16), bk).astype(BF16)
            tks.append(_diag_fold(_nn(_t_bf16(ds_all), qv)))
            tvs.append(_diag_fold(_nn(_t_bf16(p_all), dab)))
        pack_ref[ROW_SINK:ROW_SINK + 1, :] += dsink_row
        lane2 = lax.broadcasted_iota(jnp.int32, (256, 128), 1)
        dkx = jnp.where(lane2 < 64, tks[0], tks[1])
        dvx = jnp.where(lane2 < 64, tvs[0], tvs[1])
        dp_ref[:, COL_K:COL_V] = (ck[...] + dkx[BLOCK:]).astype(BF16)
        dp_ref[:, COL_V:COL_AG] = (cv[...] + dvx[BLOCK:]).astype(BF16)
        ck[...] = dkx[:BLOCK]
        cv[...] = dvx[:BLOCK]

    blk = lambda c: pl.BlockSpec((BLOCK, c), lambda i: (last - i, 0))
    full = lambda shape: pl.BlockSpec(shape, lambda i: (0,) * len(shape))
    return pl.pallas_call(
        body, name=f"bwd_mix{layer}", grid=(N_BLOCKS,),
        in_specs=[pl.BlockSpec(memory_space=pltpu.SMEM), blk(D_MODEL), blk(D_POOL),
                  pl.BlockSpec((BLOCK, D_POOL), lambda i: (jnp.maximum(last - i - 1, 0), 0)),
                  blk(D_POOL), blk(D_ATTN), full((SEQ, D_KV)), full((SEQ, D_KV)), blk(D_ATTN), blk(D_ATTN),
                  pl.BlockSpec((None, 4, BLOCK, BLOCK), lambda i: (layer, 0, 0, 0)), full((DEPTH, D_POOL)),
                  pl.BlockSpec((1, 2, BLOCK, 1024), lambda i: (jnp.minimum(last - i, 1), 0, 0, 0)),
                  full((PACK_ROWS, 128))],
        out_specs=[blk(D_IN), full((PACK_ROWS, 128))],
        out_shape=[jax.ShapeDtypeStruct((SEQ, D_IN), BF16), jax.ShapeDtypeStruct((PACK_ROWS, 128), F32)],
        scratch_shapes=[pltpu.VMEM((BLOCK, D_KV), F32), pltpu.VMEM((BLOCK, D_KV), F32),
                        pltpu.VMEM((BLOCK, D_POOL), F32)],
        input_output_aliases={13: 1},
        compiler_params=_compiler_params(("arbitrary",)),
    )(sinks, dz, u, u, pg, q, k, v, ag, a, pool_w, pool_scale, bias, pack)


def _bwd_in(layer, dproj, x, dxo, norm_pre, w_in_t, pack):
    tm = TOKEN_TILE
    steps = SEQ // tm
    cw = 256

    def body(dp_ref, x_ref, dxo_ref, g_ref, w_ref, pin_ref, dx_ref, dw_ref, pack_ref, acc, dg):
        i = pl.program_id(0)

        @pl.when(i == 0)
        def _():
            acc[...] = jnp.zeros_like(acc)
            dg[...] = jnp.zeros_like(dg)
            pack_ref[...] = pin_ref[...]

        xv = x_ref[...]
        gv = g_ref[layer:layer + 1, :]
        r = lax.rsqrt(jnp.mean(xv * xv, axis=-1, keepdims=True) + EPS)
        xn = xv * r
        hb = (xn * gv).astype(BF16)
        dh = _nn(dp_ref[...], w_ref[...])
        for c in range(0, D_IN, cw):
            acc[c:c + cw, :] += _nn(_t_bf16(dp_ref[:, c:c + cw].astype(F32)), hb)
        dg[...] += jnp.sum(dh * xn, axis=0, keepdims=True)
        dhn = dh * gv
        dx_ref[...] = dxo_ref[...] + r * (dhn - xn * jnp.mean(dhn * xn, axis=-1, keepdims=True))

        @pl.when(i == steps - 1)
        def _():
            dw_ref[...] = acc[...].astype(BF16)
            _rows_of(dg, pack_ref, ROW_NPRE)

    row = lambda c: pl.BlockSpec((tm, c), lambda i: (i, 0))
    const = lambda shape: pl.BlockSpec(shape, lambda i: (0,) * len(shape))
    return pl.pallas_call(
        body, name=f"bwd_in{layer}", grid=(steps,),
        in_specs=[row(D_IN), row(D_MODEL), row(D_MODEL), const((DEPTH, D_MODEL)), const((D_IN, D_MODEL)),
                  const((PACK_ROWS, 128))],
        out_specs=[row(D_MODEL), const((D_IN, D_MODEL)), const((PACK_ROWS, 128))],
        out_shape=[jax.ShapeDtypeStruct((SEQ, D_MODEL), F32), jax.ShapeDtypeStruct((D_IN, D_MODEL), BF16),
                   jax.ShapeDtypeStruct((PACK_ROWS, 128), F32)],
        scratch_shapes=[pltpu.VMEM((D_IN, D_MODEL), F32), pltpu.VMEM((1, D_MODEL), F32)],
        input_output_aliases={5: 2},
        compiler_params=_compiler_params(("arbitrary",)),
    )(dproj, x, dxo, norm_pre, w_in_t, pack)


def _mesh_pos():
    return lax.axis_index("x"), lax.axis_index("y"), lax.axis_index("c")


def _allgather(srcs, out_dtype, name):
    na = len(srcs)
    shapes = [(a.shape[-2], a.shape[-1]) for a, _ in srcs]

    def body(*refs):
        xs, outs, stage = refs[:na], refs[na:2 * na], refs[2 * na:3 * na]
        send_sems, recv_sems, local_sems = refs[3 * na:]
        x, y, c = _mesh_pos()
        me, sibling = (x, y, c), (x, y, 1 - c)
        chips = [(1 - x, y), (x, 1 - y), (1 - x, 1 - y)]

        def slot(a, px, py, pc):
            m = shapes[a][0]
            return outs[a].at[pl.ds(pl.multiple_of((4 * px + 2 * py + pc) * m, 16 if m % 16 == 0 else 8), m), :]

        def copy(a, k, block, to, src=None):
            return pltpu.make_async_remote_copy(
                src_ref=slot(a, *block) if src is None else src, dst_ref=slot(a, *block),
                send_sem=send_sems.at[a, k], recv_sem=recv_sems.at[a, k], device_id=to, device_id_type=MESH)

        for a, (_, layer) in enumerate(srcs):
            stage[a][...] = (xs[a][...] if layer is None else xs[a][layer]).astype(out_dtype)
        mine = [pltpu.make_async_copy(stage[a], slot(a, *me), local_sems.at[a]) for a in range(na)]
        for cp in mine:
            cp.start()
        first = []
        for a in range(na):
            first.append(copy(a, 0, me, sibling, src=stage[a]))
            first += [copy(a, 1 + j, me, (*chip, c), src=stage[a]) for j, chip in enumerate(chips)]
        for cp in first:
            cp.start()
        passed = []
        for a in range(na):
            for j, chip in enumerate(chips):
                copy(a, 1 + j, (*chip, c), me).wait_recv()
                cp = copy(a, 4 + j, (*chip, c), sibling)
                cp.start()
                passed.append(cp)
        for a in range(na):
            copy(a, 0, sibling, me).wait_recv()
            for j, chip in enumerate(chips):
                copy(a, 4 + j, (*chip, 1 - c), me).wait_recv()
        for cp in first + passed:
            cp.wait_send()
        for cp in mine:
            cp.wait()

    vmem = pl.BlockSpec(memory_space=pltpu.VMEM)
    return pl.pallas_call(
        body, name=name,
        in_specs=[vmem] * na, out_specs=[vmem] * na,
        out_shape=[jax.ShapeDtypeStruct((N_DEV * m, n), out_dtype) for m, n in shapes],
        scratch_shapes=([pltpu.VMEM(s, out_dtype) for s in shapes]
                        + [pltpu.SemaphoreType.DMA((na, 7)), pltpu.SemaphoreType.DMA((na, 7)),
                           pltpu.SemaphoreType.DMA((na,))]),
        compiler_params=_compiler_params(),
    )(*[a for a, _ in srcs])


def _reduce_scatter(arrs, name):
    na = len(arrs)
    row_chunk = 32

    def body(*refs):
        gs, outs = refs[:na], refs[na:2 * na]
        own, ra, hb, rb = (refs[(2 + t) * na:(3 + t) * na] for t in range(4))
        d2d_send, d2d_recv, ici_send, ici_recv, local_sems = refs[6 * na:]
        x, y, c = _mesh_pos()
        sibling = (x, y, 1 - c)
        loads, sends = [], []
        for a in range(na):
            for q in range(4):
                blk_mine = 4 * (q >> 1) + 2 * (q & 1) + c
                blk_sib = 4 * (q >> 1) + 2 * (q & 1) + (1 - c)
                cp = pltpu.make_async_copy(gs[a].at[blk_mine], own[a].at[q], local_sems.at[a, q])
                cp.start()
                loads.append(cp)
                cp = pltpu.make_async_remote_copy(
                    src_ref=gs[a].at[blk_sib], dst_ref=ra[a].at[q], send_sem=d2d_send.at[a, q],
                    recv_sem=d2d_recv.at[a, q], device_id=sibling, device_id_type=MESH)
                cp.start()
                sends.append(cp)
        for cp in loads:
            cp.wait()
        for cp in sends:
            cp.wait_recv()
        others = [(1 - x, y), (x, 1 - y), (1 - x, 1 - y)]

        def chunks(a):
            m = arrs[a].shape[1]
            step = next(s for s in (row_chunk, 24, 16, 8) if m % s == 0)
            return m // step, step

        def pair_sum(a, q, rs):
            return own[a][q, rs, :].astype(F32) + ra[a][q, rs, :].astype(F32)

        ici = []
        for j, (qx, qy) in enumerate(others):
            q = 2 * qx + qy
            for a in range(na):
                n_chunks, step = chunks(a)

                def to_send(i, carry, a=a, q=q, j=j, step=step):
                    rs = pl.ds(pl.multiple_of(i * step, step), step)
                    hb[a][j, rs, :] = pair_sum(a, q, rs).astype(hb[a].dtype)
                    return carry

                lax.fori_loop(0, n_chunks, to_send, 0)
                cp = pltpu.make_async_remote_copy(
                    src_ref=hb[a].at[j], dst_ref=rb[a].at[j], send_sem=ici_send.at[a, j],
                    recv_sem=ici_recv.at[a, j], device_id=(qx, qy, c), device_id_type=MESH)
                cp.start()
                ici.append(cp)
        for a in range(na):
            n_chunks, step = chunks(a)

            def mine(i, carry, a=a, step=step):
                rs = pl.ds(pl.multiple_of(i * step, step), step)
                outs[a][rs, :] = pair_sum(a, 2 * x + y, rs)
                return carry

            lax.fori_loop(0, n_chunks, mine, 0)
        for cp in ici:
            cp.wait_recv()
        for a in range(na):
            n_chunks, step = chunks(a)

            def total(i, carry, a=a, step=step):
                rs = pl.ds(pl.multiple_of(i * step, step), step)
                acc = outs[a][rs, :]
                for j in range(3):
                    acc = acc + rb[a][j, rs, :].astype(F32)
                outs[a][rs, :] = acc
                return carry

            lax.fori_loop(0, n_chunks, total, 0)
        for cp in sends + ici:
            cp.wait_send()

    vmem = pl.BlockSpec(memory_space=pltpu.VMEM)
    scratch = []
    for count in (4, 4, 3, 3):
        scratch += [pltpu.VMEM((count,) + t.shape[1:], t.dtype) for t in arrs]
    scratch += [pltpu.SemaphoreType.DMA((na, 4)), pltpu.SemaphoreType.DMA((na, 4)),
                pltpu.SemaphoreType.DMA((na, 3)), pltpu.SemaphoreType.DMA((na, 3)),
                pltpu.SemaphoreType.DMA((na, 4))]
    return pl.pallas_call(
        body, name=name,
        in_specs=[pl.BlockSpec(memory_space=pl.ANY)] * na, out_specs=[vmem] * na,
        out_shape=[jax.ShapeDtypeStruct(t.shape[1:], F32) for t in arrs],
        scratch_shapes=scratch,
        compiler_params=_compiler_params(),
    )(*arrs)


def _adamw_math(w, g, m, v):
    m = ADAM_B1 * m + (1.0 - ADAM_B1) * g
    v = ADAM_B2 * v + (1.0 - ADAM_B2) * (g * g)
    m_hat = m / (1.0 - ADAM_B1 ** ADAM_STEP)
    v_hat = v / (1.0 - ADAM_B2 ** ADAM_STEP)
    delta = -ADAM_LR * (m_hat / (jnp.sqrt(v_hat) + ADAM_EPS) + ADAM_WD * w)
    return delta, m, v


def _adamw_big(g0, g1, w, m, v, name, rows):
    _, mm, nn = w.shape

    def body(g0_ref, g1_ref, w_ref, m_ref, v_ref, g_ref, d_ref, nm_ref, nv_ref):
        g = jnp.where(pl.program_id(0) == 0, g0_ref[...], g1_ref[...])
        g_ref[...] = g
        d, nm, nv = _adamw_math(w_ref[...], g, m_ref[...], v_ref[...])
        d_ref[...] = d
        nm_ref[...] = nm
        nv_ref[...] = nv

    gspec = pl.BlockSpec((rows, nn), lambda l, i: (i, 0))
    spec = pl.BlockSpec((None, rows, nn), lambda l, i: (l, i, 0))
    return pl.pallas_call(
        body, name=name, grid=(DEPTH, mm // rows),
        in_specs=[gspec, gspec, spec, spec, spec], out_specs=[spec] * 4,
        out_shape=[jax.ShapeDtypeStruct(w.shape, F32)] * 4,
        compiler_params=_compiler_params(("arbitrary", "arbitrary")),
    )(g0, g1, w, m, v)


def _adamw_small(g0, g1, params):
    def body(g0_ref, g1_ref, *refs):
        ins, outs = refs[:15], refs[15:]
        loss_ref = outs[0]
        packs = (g0_ref, g1_ref)
        loss_ref[...] = g1_ref[ROW_LOSS:ROW_LOSS + 1, 0:1]

        def update(p, sel, g):
            w_ref, m_ref, v_ref = ins[p], ins[5 + p], ins[10 + p]
            d, nm, nv = _adamw_math(w_ref[sel], g, m_ref[sel], v_ref[sel])
            for t, val in enumerate((g, d, nm, nv)):
                outs[1 + 5 * t + p][sel] = val

        for l in range(DEPTH):
            gp = packs[l]
            for grp in range(4):
                update(0, (l, grp), gp[ROW_PW + BLOCK * grp:ROW_PW + BLOCK * (grp + 1), :])
                update(1, (slice(l, l + 1), slice(128 * grp, 128 * (grp + 1))), gp[ROW_SC + grp:ROW_SC + grp + 1, :])
            update(2, (slice(l, l + 1), slice(None)), gp[ROW_SINK:ROW_SINK + 1, 0:N_HEADS])
            for r in range(D_MODEL // 128):
                sel = (slice(l, l + 1), slice(128 * r, 128 * (r + 1)))
                update(3, sel, gp[ROW_NPRE + r:ROW_NPRE + r + 1, :])
                update(4, sel, gp[ROW_NPOST + r:ROW_NPOST + r + 1, :])

    shapes = [jax.ShapeDtypeStruct(p.shape, F32) for p in params[:5]]
    return pl.pallas_call(
        body, name="adamw_small",
        out_shape=[jax.ShapeDtypeStruct((1, 1), F32)] + shapes * 4,
        compiler_params=_compiler_params(),
    )(g0, g1, *params)


def kernel(x, w_in, pool_w, pool_scale, attn_sinks, w_out, norm_pre, norm_post, loss_target, m_w_in, m_pool_w, m_pool_scale, m_attn_sinks, m_w_out, m_norm_pre, m_norm_post, v_w_in, v_pool_w, v_pool_scale, v_attn_sinks, v_w_out, v_norm_pre, v_norm_post):
    x0 = x.reshape(SEQ, D_MODEL)
    target = loss_target.reshape(SEQ, D_MODEL)
    bias = jnp.asarray(_attn_bias())
    w_in_t, m_in_t, v_in_t = (jnp.swapaxes(t, 1, 2) for t in (w_in, m_w_in, v_w_in))

    (win0,) = _allgather([(w_in_t, 0)], BF16, "gather_w_in0")
    wout0, win1, wout1 = _allgather([(w_out, 0), (w_in_t, 1), (w_out, 1)], BF16, "gather_w_rest")
    win_full, wout_full = (win0, win1), (wout0, wout1)

    saved = []
    xl = x0
    for layer in range(DEPTH):
        u, pg, q, k, v, ag = _fwd_in(layer, xl, norm_pre, win_full[layer])
        z, a = _fwd_mix(layer, attn_sinks, u, pg, q, k, v, ag, pool_w, pool_scale, bias)
        x_next, y = _fwd_out(layer, z, xl, norm_post, wout_full[layer])
        saved.append((xl, u, pg, q, k, v, ag, z, a, y))
        xl = x_next

    reduced = [None] * DEPTH
    dx = None
    for layer in reversed(range(DEPTH)):
        xin, u, pg, q, k, v, ag, z, a, y = saved[layer]
        if layer == DEPTH - 1:
            dx, dz, gw_out, pack = _bwd_out(layer, True, xl, target, y, z, norm_post, wout_full[layer])
        else:
            dz, gw_out, pack = _bwd_out(layer, False, dx, None, y, z, norm_post, wout_full[layer])
        dproj, pack = _bwd_mix(layer, attn_sinks, dz, u, pg, q, k, v, ag, a, pool_w, pool_scale, bias, pack)
        dx, gw_in_t, pack = _bwd_in(layer, dproj, xin, dx, norm_pre, win_full[layer], pack)
        reduced[layer] = _reduce_scatter(
            [gw_in_t.reshape(N_DEV, IN_SHARD, D_MODEL), gw_out.reshape(N_DEV, OUT_SHARD, D_MODEL),
             pack.reshape(N_DEV, PACK_SLICE, 128)], f"scatter_grads{layer}")

    small0, small1 = _allgather([(reduced[0][2], None), (reduced[1][2], None)], F32, "gather_small")
    small_out = _adamw_small(small0, small1, [
        pool_w, pool_scale, attn_sinks, norm_pre, norm_post,
        m_pool_w, m_pool_scale, m_attn_sinks, m_norm_pre, m_norm_post,
        v_pool_w, v_pool_scale, v_attn_sinks, v_norm_pre, v_norm_post])
    loss = small_out[0].reshape(())
    big_in = _adamw_big(reduced[0][0], reduced[1][0], w_in_t, m_in_t, v_in_t, "adamw_in", 96)
    big_out = _adamw_big(reduced[0][1], reduced[1][1], w_out, m_w_out, v_w_out, "adamw_out", 128)

    outs = [loss, dx.reshape(1, SEQ, D_MODEL)]
    for t in range(4):
        pw_, sc_, sk_, npre_, npost_ = small_out[1 + 5 * t:6 + 5 * t]
        outs += [jnp.swapaxes(big_in[t], 1, 2), pw_, sc_, sk_, big_out[t], npre_, npost_]
    return tuple(outs)
```

```python
import numpy as np
import jax
import jax.numpy as jnp
from jax import lax
from jax.experimental import pallas as pl
from jax.experimental.pallas import tpu as pltpu

F32 = jnp.float32
BF16 = jnp.bfloat16

N_DEV = 8
SEQ = 2048
D_MODEL = 1024
D_POOL = 512
D_ATTN = 512
D_KV = 128
D_IN = 2304
N_HEADS = 8
GQA = 4
HEAD_DIM = 64
BLOCK = 128
N_BLOCKS = SEQ // BLOCK
POOL_WINDOWS = (2, 4, 8, 16)
DEPTH = 2
EPS = 1e-6
NEG_INF = -1e30
SCALE = HEAD_DIM ** -0.5
IN_SHARD = D_IN // N_DEV
OUT_SHARD = D_MODEL // N_DEV

COL_U, COL_PG, COL_Q, COL_K, COL_V, COL_AG = 0, 512, 1024, 1536, 1664, 1792

ADAM_LR = 0.001
ADAM_B1 = 0.9
ADAM_B2 = 0.999
ADAM_EPS = 1e-08
ADAM_WD = 0.01
ADAM_STEP = 10

TOKEN_TILE = 256
VMEM_LIMIT = 56 * 1024 * 1024
MESH = pl.DeviceIdType.MESH

ROW_PW, ROW_SC, ROW_SINK, ROW_NPRE, ROW_NPOST, ROW_LOSS = 0, 512, 520, 528, 536, 544
PACK_ROWS = 576
PACK_SLICE = PACK_ROWS // N_DEV


def _nn(a, b):
    return jnp.dot(a, b, preferred_element_type=F32)


def _nt(a, b):
    return lax.dot_general(a, b, (((1,), (1,)), ((), ())), preferred_element_type=F32)


def _t_bf16(a32):
    return jnp.transpose(a32).astype(BF16)


def _silu_parts(g):
    s = jax.nn.sigmoid(g)
    return g * s, s * (1.0 + g * (1.0 - s))


def _compiler_params(sem=None):
    if sem is None:
        return pltpu.CompilerParams(vmem_limit_bytes=VMEM_LIMIT)
    return pltpu.CompilerParams(dimension_semantics=sem, vmem_limit_bytes=VMEM_LIMIT)


def _attn_bias():
    t = np.arange(BLOCK)[:, None]
    j = np.arange(2 * BLOCK)[None, :]
    dist = t + BLOCK - j
    in_win = (dist >= 0) & (dist < BLOCK)
    out = np.zeros((2, 2, BLOCK, GQA * 2 * BLOCK), np.float32)
    for variant in range(2):
        valid = in_win & ((j >= BLOCK) | (variant == 1))
        for kv in range(2):
            for g in range(GQA):
                slope = np.float32(2.0 ** (-(kv * GQA + g + 1)))
                b = np.where(valid, -slope * dist.astype(np.float32), np.float32(NEG_INF))
                out[variant, kv, :, g * 256:(g + 1) * 256] = b
    return out


def _block_diag(kx, kv):
    rolled = pltpu.roll(kx, 64, 1)
    lane = lax.broadcasted_iota(jnp.int32, kx.shape, 1)
    dup = jnp.where(lane < 64, kx, rolled) if kv == 0 else jnp.where(lane < 64, rolled, kx)
    rep = jnp.concatenate([dup, dup], axis=1).astype(BF16)
    lane2 = lax.broadcasted_iota(jnp.int32, rep.shape, 1)
    zero = jnp.zeros_like(rep)
    parts = [jnp.where((lane2 >= 64 * g) & (lane2 < 64 * g + 64), rep, zero) for g in range(GQA)]
    return jnp.concatenate(parts, axis=0)


def _diag_fold(m):
    lane = lax.broadcasted_iota(jnp.int32, (256, 256), 1)
    r = jnp.where(lane < 64, m[0:256], jnp.where(lane < 128, m[256:512], jnp.where(lane < 192, m[512:768], m[768:1024])))
    h = r[:, 0:128] + r[:, 128:256]
    return h + pltpu.roll(h, 64, 1)


def _window_sum(ext, w, forward):
    s = ext
    sh = 1
    while sh < w:
        s = s + pltpu.roll(s, (256 - sh) if forward else sh, 0)
        sh *= 2
    return s


def _inv_count(n, w):
    t = n * BLOCK + lax.broadcasted_iota(jnp.int32, (BLOCK, 1), 0) + 1
    return 1.0 / jnp.minimum(t.astype(F32), float(w))


def _kv_ext(ref, n):
    r0 = pl.multiple_of(jnp.maximum(n - 1, 0) * BLOCK, BLOCK)
    r1 = pl.multiple_of(n * BLOCK, BLOCK)
    return jnp.concatenate([ref[pl.ds(r0, BLOCK), :], ref[pl.ds(r1, BLOCK), :]], axis=0)


def _softmax_chunk(sg, sink):
    m = jnp.maximum(jnp.max(sg, axis=-1, keepdims=True), sink)
    p = jnp.exp(sg - m)
    esink = jnp.exp(sink - m)
    rl = 1.0 / (jnp.sum(p, axis=-1, keepdims=True) + esink)
    return p * rl, esink * rl


def _rows_of(vec_ref, pack_ref, row0):
    for r in range(D_MODEL // 128):
        pack_ref[row0 + r:row0 + r + 1, :] = vec_ref[:, 128 * r:128 * (r + 1)]


def _fwd_in(layer, x, norm_pre, w_in_t):
    tm = TOKEN_TILE

    def body(x_ref, g_ref, w_ref, u_ref, pg_ref, q_ref, k_ref, v_ref, ag_ref):
        xv = x_ref[...]
        r = lax.rsqrt(jnp.mean(xv * xv, axis=-1, keepdims=True) + EPS)
        h = (xv * r * g_ref[layer:layer + 1, :]).astype(BF16)
        u_ref[...] = _nt(h, w_ref[COL_U:COL_PG, :])
        pg_ref[...] = _nt(h, w_ref[COL_PG:COL_Q, :])
        q_ref[...] = _nt(h, w_ref[COL_Q:COL_K, :]).astype(BF16)
        k_ref[...] = _nt(h, w_ref[COL_K:COL_V, :])
        v_ref[...] = _nt(h, w_ref[COL_V:COL_AG, :])
        ag_ref[...] = _nt(h, w_ref[COL_AG:D_IN, :])

    row = lambda c: pl.BlockSpec((tm, c), lambda i: (i, 0))
    return pl.pallas_call(
        body, name=f"fwd_in{layer}", grid=(SEQ // tm,),
        in_specs=[row(D_MODEL), pl.BlockSpec((DEPTH, D_MODEL), lambda i: (0, 0)),
                  pl.BlockSpec((D_IN, D_MODEL), lambda i: (0, 0))],
        out_specs=[row(D_POOL), row(D_POOL), row(D_ATTN), row(D_KV), row(D_KV), row(D_ATTN)],
        out_shape=[jax.ShapeDtypeStruct((SEQ, D_POOL), F32), jax.ShapeDtypeStruct((SEQ, D_POOL), F32),
                   jax.ShapeDtypeStruct((SEQ, D_ATTN), BF16), jax.ShapeDtypeStruct((SEQ, D_KV), F32),
                   jax.ShapeDtypeStruct((SEQ, D_KV), F32), jax.ShapeDtypeStruct((SEQ, D_ATTN), F32)],
        compiler_params=_compiler_params(("arbitrary",)),
    )(x, norm_pre, w_in_t)


def _fwd_mix(layer, sinks, u, pg, q, k, v, ag, pool_w, pool_scale, bias):
    def body(sink_ref, u_ref, pg_ref, q_ref, k_ref, v_ref, ag_ref, pw_ref, sc_ref, bias_ref,
             z_ref, a_ref, uprev):
        n = pl.program_id(0)

        @pl.when(n == 0)
        def _():
            uprev[...] = jnp.zeros_like(uprev)

        uv = u_ref[...]
        ext = jnp.concatenate([uprev[...], uv], axis=0)
        uprev[...] = uv
        for g, w in enumerate(POOL_WINDOWS):
            cs = slice(BLOCK * g, BLOCK * (g + 1))
            win = _window_sum(ext[:, cs], w, forward=False)[BLOCK:]
            pooled = win * _inv_count(n, w) - uv[:, cs]
            mixed = _nn(pooled.astype(BF16), pw_ref[g].astype(BF16))
            gate, _ = _silu_parts(pg_ref[:, cs])
            z_ref[:, cs] = (mixed * sc_ref[layer:layer + 1, cs] * gate).astype(BF16)

        kx = _kv_ext(k_ref, n)
        vx = _kv_ext(v_ref, n)
        for kv in range(2):
            cs = slice(256 * kv, 256 * (kv + 1))
            bk = _block_diag(kx, kv)
            bv = _block_diag(vx, kv)
            s = _nt(q_ref[:, cs], bk) * SCALE + bias_ref[0, kv]
            ps = []
            for g in range(GQA):
                p, _ = _softmax_chunk(s[:, 256 * g:256 * (g + 1)], sink_ref[layer, kv * GQA + g])
                ps.append(p.astype(BF16))
            o = _nn(jnp.concatenate(ps, axis=1), bv)
            a_ref[:, cs] = o
            gate, _ = _silu_parts(ag_ref[:, cs])
            z_ref[:, D_POOL + 256 * kv:D_POOL + 256 * (kv + 1)] = (o * gate).astype(BF16)

    blk = lambda c: pl.BlockSpec((BLOCK, c), lambda n: (n, 0))
    full = lambda shape: pl.BlockSpec(shape, lambda n: (0,) * len(shape))
    return pl.pallas_call(
        body, name=f"fwd_mix{layer}", grid=(N_BLOCKS,),
        in_specs=[pl.BlockSpec(memory_space=pltpu.SMEM), blk(D_POOL), blk(D_POOL), blk(D_ATTN),
                  full((SEQ, D_KV)), full((SEQ, D_KV)), blk(D_ATTN),
                  pl.BlockSpec((None, 4, BLOCK, BLOCK), lambda n: (layer, 0, 0, 0)), full((DEPTH, D_POOL)),
                  pl.BlockSpec((1, 2, BLOCK, 1024), lambda n: (jnp.minimum(n, 1), 0, 0, 0))],
        out_specs=[blk(D_MODEL), blk(D_ATTN)],
        out_shape=[jax.ShapeDtypeStruct((SEQ, D_MODEL), BF16), jax.ShapeDtypeStruct((SEQ, D_ATTN), F32)],
        scratch_shapes=[pltpu.VMEM((BLOCK, D_POOL), F32)],
        compiler_params=_compiler_params(("arbitrary",)),
    )(sinks, u, pg, q, k, v, ag, pool_w, pool_scale, bias)


def _fwd_out(layer, z, x, norm_post, w_out):
    tm = TOKEN_TILE

    def body(z_ref, x_ref, g_ref, w_ref, xn_ref, y_ref):
        y = _nn(z_ref[...], w_ref[...])
        y_ref[...] = y
        r = lax.rsqrt(jnp.mean(y * y, axis=-1, keepdims=True) + EPS)
        xn_ref[...] = x_ref[...] + y * r * g_ref[layer:layer + 1, :]

    row = lambda c: pl.BlockSpec((tm, c), lambda i: (i, 0))
    return pl.pallas_call(
        body, name=f"fwd_out{layer}", grid=(SEQ // tm,),
        in_specs=[row(D_MODEL), row(D_MODEL), pl.BlockSpec((DEPTH, D_MODEL), lambda i: (0, 0)),
                  pl.BlockSpec((D_MODEL, D_MODEL), lambda i: (0, 0))],
        out_specs=[row(D_MODEL), row(D_MODEL)],
        out_shape=[jax.ShapeDtypeStruct((SEQ, D_MODEL), F32), jax.ShapeDtypeStruct((SEQ, D_MODEL), F32)],
        compiler_params=_compiler_params(("arbitrary",)),
    )(z, x, norm_post, w_out)


def _bwd_out(layer, top, dxo_or_xf, target_or_token, y, z, norm_post, w_out):
    tm = TOKEN_TILE
    steps = SEQ // tm

    def body(*refs):
        if top:
            xf_ref, t_ref, y_ref, z_ref, g_ref, w_ref, dxo_ref, dz_ref, dw_ref, pack_ref, acc, dg, lacc = refs
        else:
            dxi_ref, _, y_ref, z_ref, g_ref, w_ref, dz_ref, dw_ref, pack_ref, acc, dg, lacc = refs
        i = pl.program_id(0)

        @pl.when(i == 0)
        def _():
            acc[...] = jnp.zeros_like(acc)
            dg[...] = jnp.zeros_like(dg)
            lacc[...] = jnp.zeros_like(lacc)
            pack_ref[...] = jnp.zeros_like(pack_ref)

        if top:
            d = xf_ref[...] - t_ref[...]
            dxo_v = d * (1.0 / D_MODEL)
            dxo_ref[...] = dxo_v
            part = jnp.sum(d * d, axis=-1, keepdims=True) * (1.0 / D_MODEL)
            lacc[...] += 0.5 * jnp.sum(part, axis=0, keepdims=True)
        else:
            dxo_v = dxi_ref[...]
        y = y_ref[...]
        r = lax.rsqrt(jnp.mean(y * y, axis=-1, keepdims=True) + EPS)
        yn = y * r
        dg[...] += jnp.sum(dxo_v * yn, axis=0, keepdims=True)
        dyn = dxo_v * g_ref[layer:layer + 1, :]
        dy = (r * (dyn - yn * jnp.mean(dyn * yn, axis=-1, keepdims=True))).astype(BF16)
        dz_ref[...] = _nt(dy, w_ref[...])
        acc[...] += _nn(_t_bf16(z_ref[...].astype(F32)), dy)

        @pl.when(i == steps - 1)
        def _():
            dw_ref[...] = acc[...].astype(BF16)
            _rows_of(dg, pack_ref, ROW_NPOST)
            lane = lax.broadcasted_iota(jnp.int32, (1, 128), 1)
            pack_ref[ROW_LOSS:ROW_LOSS + 1, :] = jnp.where(lane == 0, lacc[...], 0.0)

    row = lambda c: pl.BlockSpec((tm, c), lambda i: (i, 0))
    const = lambda shape: pl.BlockSpec(shape, lambda i: (0,) * len(shape))
    act = jax.ShapeDtypeStruct((SEQ, D_MODEL), F32)
    return pl.pallas_call(
        body, name=f"bwd_out{layer}", grid=(steps,),
        in_specs=([row(D_MODEL), row(D_MODEL) if top else const((8, 128))]
                  + [row(D_MODEL), row(D_MODEL), const((DEPTH, D_MODEL)), const((D_MODEL, D_MODEL))]),
        out_specs=([row(D_MODEL)] * (2 if top else 1) + [const((D_MODEL, D_MODEL)), const((PACK_ROWS, 128))]),
        out_shape=([act] * (2 if top else 1)
                   + [jax.ShapeDtypeStruct((D_MODEL, D_MODEL), BF16), jax.ShapeDtypeStruct((PACK_ROWS, 128), F32)]),
        scratch_shapes=[pltpu.VMEM((D_MODEL, D_MODEL), F32), pltpu.VMEM((1, D_MODEL), F32), pltpu.VMEM((1, 1), F32)],
        compiler_params=_compiler_params(("arbitrary",)),
    )(dxo_or_xf, target_or_token, y, z, norm_post, w_out)


def _bwd_mix(layer, sinks, dz, u, pg, q, k, v, ag, a, pool_w, pool_scale, bias, pack):
    last = N_BLOCKS - 1

    def body(sink_ref, dz_ref, u_ref, up_ref, pg_ref, q_ref, k_ref, v_ref, ag_ref, a_ref, pw_ref, sc_ref,
             bias_ref, pin_ref, dp_ref, pack_ref, ck, cv, ce):
        i = pl.program_id(0)
        n = last - i

        @pl.when(i == 0)
        def _():
            ck[...] = jnp.zeros_like(ck)
            cv[...] = jnp.zeros_like(cv)
            ce[...] = jnp.zeros_like(ce)
            pack_ref[...] = pin_ref[...]

        uv = u_ref[...]
        has_prev = (n > 0).astype(F32)
        ext = jnp.concatenate([up_ref[...] * has_prev, uv], axis=0)
        for g, w in enumerate(POOL_WINDOWS):
            cs = slice(BLOCK * g, BLOCK * (g + 1))
            inv = _inv_count(n, w)
            win = _window_sum(ext[:, cs], w, forward=False)[BLOCK:]
            pooled = win * inv - uv[:, cs]
            pw_g = pw_ref[g].astype(BF16)
            mixed = _nn(pooled.astype(BF16), pw_g)
            gate, dgate = _silu_parts(pg_ref[:, cs])
            dzp = dz_ref[:, cs]
            sc = sc_ref[layer:layer + 1, cs]
            dpm = dzp * gate
            dp_ref[:, COL_PG + BLOCK * g:COL_PG + BLOCK * (g + 1)] = (dzp * (mixed * sc) * dgate).astype(BF16)
            pack_ref[ROW_SC + g:ROW_SC + g + 1, :] += jnp.sum(dpm * mixed, axis=0, keepdims=True)
            dmixed = (dpm * sc).astype(BF16)
            pack_ref[ROW_PW + BLOCK * g:ROW_PW + BLOCK * (g + 1), :] += _nn(_t_bf16(pooled), dmixed)
            dpooled = _nt(dmixed, pw_g)
            e = dpooled * inv
            lead = _window_sum(jnp.concatenate([e, ce[:, cs]], axis=0), w, forward=True)[:BLOCK]
            dp_ref[:, COL_U + BLOCK * g:COL_U + BLOCK * (g + 1)] = (lead - dpooled).astype(BF16)
            ce[:, cs] = e

        kx = _kv_ext(k_ref, n)
        vx = _kv_ext(v_ref, n)
        lane = lax.broadcasted_iota(jnp.int32, (1, 128), 1)
        dsink_row = jnp.zeros((1, 128), F32)
        tks, tvs = [], []
        for kv in range(2):
            cs = slice(256 * kv, 256 * (kv + 1))
            bk = _block_diag(kx, kv)
            bv = _block_diag(vx, kv)
            qv = q_ref[:, cs]
            s = _nt(qv, bk) * SCALE + bias_ref[0, kv]
            gate, dgate = _silu_parts(ag_ref[:, cs])
            dza = dz_ref[:, D_POOL + 256 * kv:D_POOL + 256 * (kv + 1)]
            dp_ref[:, COL_AG + 256 * kv:COL_AG + 256 * (kv + 1)] = (dza * a_ref[:, cs] * dgate).astype(BF16)
            da = dza * gate
            dab = da.astype(BF16)
            dpr = _nt(dab, bv)
            ps, dss = [], []
            for g in range(GQA):
                gs = slice(256 * g, 256 * (g + 1))
                p, psink = _softmax_chunk(s[:, gs], sink_ref[layer, kv * GQA + g])
                dpg_ = dpr[:, gs]
                delta = jnp.sum(p * dpg_, axis=-1, keepdims=True)
                dsink = -jnp.sum(psink * delta, axis=0, keepdims=True)
                dsink_row = dsink_row + jnp.where(lane == kv * GQA + g, dsink, 0.0)
                ps.append(p)
                dss.append(p * (dpg_ - delta) * SCALE)
            p_all = jnp.concatenate(ps, axis=1)
            ds_all = jnp.concatenate(dss, axis=1)
            dp_ref[:, COL_Q + 256 * kv:COL_Q + 256 * (kv + 1)] = _nn(ds_all.astype(BF16), bk).astype(BF16)
            tks.append(_diag_fold(_nn(_t_bf16(ds_all), qv)))
            tvs.append(_diag_fold(_nn(_t_bf16(p_all), dab)))
        pack_ref[ROW_SINK:ROW_SINK + 1, :] += dsink_row
        lane2 = lax.broadcasted_iota(jnp.int32, (256, 128), 1)
        dkx = jnp.where(lane2 < 64, tks[0], tks[1])
        dvx = jnp.where(lane2 < 64, tvs[0], tvs[1])
        dp_ref[:, COL_K:COL_V] = (ck[...] + dkx[BLOCK:]).astype(BF16)
        dp_ref[:, COL_V:COL_AG] = (cv[...] + dvx[BLOCK:]).astype(BF16)
        ck[...] = dkx[:BLOCK]
        cv[...] = dvx[:BLOCK]

    blk = lambda c: pl.BlockSpec((BLOCK, c), lambda i: (last - i, 0))
    full = lambda shape: pl.BlockSpec(shape, lambda i: (0,) * len(shape))
    return pl.pallas_call(
        body, name=f"bwd_mix{layer}", grid=(N_BLOCKS,),
        in_specs=[pl.BlockSpec(memory_space=pltpu.SMEM), blk(D_MODEL), blk(D_POOL),
                  pl.BlockSpec((BLOCK, D_POOL), lambda i: (jnp.maximum(last - i - 1, 0), 0)),
                  blk(D_POOL), blk(D_ATTN), full((SEQ, D_KV)), full((SEQ, D_KV)), blk(D_ATTN), blk(D_ATTN),
                  pl.BlockSpec((None, 4, BLOCK, BLOCK), lambda i: (layer, 0, 0, 0)), full((DEPTH, D_POOL)),
                  pl.BlockSpec((1, 2, BLOCK, 1024), lambda i: (jnp.minimum(last - i, 1), 0, 0, 0)),
                  full((PACK_ROWS, 128))],
        out_specs=[blk(D_IN), full((PACK_ROWS, 128))],
        out_shape=[jax.ShapeDtypeStruct((SEQ, D_IN), BF16), jax.ShapeDtypeStruct((PACK_ROWS, 128), F32)],
        scratch_shapes=[pltpu.VMEM((BLOCK, D_KV), F32), pltpu.VMEM((BLOCK, D_KV), F32),
                        pltpu.VMEM((BLOCK, D_POOL), F32)],
        input_output_aliases={13: 1},
        compiler_params=_compiler_params(("arbitrary",)),
    )(sinks, dz, u, u, pg, q, k, v, ag, a, pool_w, pool_scale, bias, pack)


def _bwd_in(layer, dproj, x, dxo, norm_pre, w_in_t, pack):
    tm = TOKEN_TILE
    steps = SEQ // tm
    cw = 256

    def body(dp_ref, x_ref, dxo_ref, g_ref, w_ref, pin_ref, dx_ref, dw_ref, pack_ref, acc, dg):
        i = pl.program_id(0)

        @pl.when(i == 0)
        def _():
            acc[...] = jnp.zeros_like(acc)
            dg[...] = jnp.zeros_like(dg)
            pack_ref[...] = pin_ref[...]

        xv = x_ref[...]
        gv = g_ref[layer:layer + 1, :]
        r = lax.rsqrt(jnp.mean(xv * xv, axis=-1, keepdims=True) + EPS)
        xn = xv * r
        hb = (xn * gv).astype(BF16)
        dh = _nn(dp_ref[...], w_ref[...])
        for c in range(0, D_IN, cw):
            acc[c:c + cw, :] += _nn(_t_bf16(dp_ref[:, c:c + cw].astype(F32)), hb)
        dg[...] += jnp.sum(dh * xn, axis=0, keepdims=True)
        dhn = dh * gv
        dx_ref[...] = dxo_ref[...] + r * (dhn - xn * jnp.mean(dhn * xn, axis=-1, keepdims=True))

        @pl.when(i == steps - 1)
        def _():
            dw_ref[...] = acc[...].astype(BF16)
            _rows_of(dg, pack_ref, ROW_NPRE)

    row = lambda c: pl.BlockSpec((tm, c), lambda i: (i, 0))
    const = lambda shape: pl.BlockSpec(shape, lambda i: (0,) * len(shape))
    return pl.pallas_call(
        body, name=f"bwd_in{layer}", grid=(steps,),
        in_specs=[row(D_IN), row(D_MODEL), row(D_MODEL), const((DEPTH, D_MODEL)), const((D_IN, D_MODEL)),
                  const((PACK_ROWS, 128))],
        out_specs=[row(D_MODEL), const((D_IN, D_MODEL)), const((PACK_ROWS, 128))],
        out_shape=[jax.ShapeDtypeStruct((SEQ, D_MODEL), F32), jax.ShapeDtypeStruct((D_IN, D_MODEL), BF16),
                   jax.ShapeDtypeStruct((PACK_ROWS, 128), F32)],
        scratch_shapes=[pltpu.VMEM((D_IN, D_MODEL), F32), pltpu.VMEM((1, D_MODEL), F32)],
        input_output_aliases={5: 2},
        compiler_params=_compiler_params(("arbitrary",)),
    )(dproj, x, dxo, norm_pre, w_in_t, pack)


def _mesh_pos():
    return lax.axis_index("x"), lax.axis_index("y"), lax.axis_index("c")


def _allgather(srcs, out_dtype, name):
    na = len(srcs)
    shapes = [(a.shape[-2], a.shape[-1]) for a, _ in srcs]

    def body(*refs):
        xs, outs, stage = refs[:na], refs[na:2 * na], refs[2 * na:3 * na]
        send_sems, recv_sems, local_sems = refs[3 * na:]
        x, y, c = _mesh_pos()
        me, sibling = (x, y, c), (x, y, 1 - c)
        chips = [(1 - x, y), (x, 1 - y), (1 - x, 1 - y)]

        def slot(a, px, py, pc):
            m = shapes[a][0]
            return outs[a].at[pl.ds(pl.multiple_of((4 * px + 2 * py + pc) * m, 16 if m % 16 == 0 else 8), m), :]

        def copy(a, k, block, to, src=None):
            return pltpu.make_async_remote_copy(
                src_ref=slot(a, *block) if src is None else src, dst_ref=slot(a, *block),
                send_sem=send_sems.at[a, k], recv_sem=recv_sems.at[a, k], device_id=to, device_id_type=MESH)

        for a, (_, layer) in enumerate(srcs):
            stage[a][...] = (xs[a][...] if layer is None else xs[a][layer]).astype(out_dtype)
        mine = [pltpu.make_async_copy(stage[a], slot(a, *me), local_sems.at[a]) for a in range(na)]
        for cp in mine:
            cp.start()
        first = []
        for a in range(na):
            first.append(copy(a, 0, me, sibling, src=stage[a]))
            first += [copy(a, 1 + j, me, (*chip, c), src=stage[a]) for j, chip in enumerate(chips)]
        for cp in first:
            cp.start()
        passed = []
        for a in range(na):
            for j, chip in enumerate(chips):
                copy(a, 1 + j, (*chip, c), me).wait_recv()
                cp = copy(a, 4 + j, (*chip, c), sibling)
                cp.start()
                passed.append(cp)
        for a in range(na):
            copy(a, 0, sibling, me).wait_recv()
            for j, chip in enumerate(chips):
                copy(a, 4 + j, (*chip, 1 - c), me).wait_recv()
        for cp in first + passed:
            cp.wait_send()
        for cp in mine:
            cp.wait()

    vmem = pl.BlockSpec(memory_space=pltpu.VMEM)
    return pl.pallas_call(
        body, name=name,
        in_specs=[vmem] * na, out_specs=[vmem] * na,
        out_shape=[jax.ShapeDtypeStruct((N_DEV * m, n), out_dtype) for m, n in shapes],
        scratch_shapes=([pltpu.VMEM(s, out_dtype) for s in shapes]
                        + [pltpu.SemaphoreType.DMA((na, 7)), pltpu.SemaphoreType.DMA((na, 7)),
                           pltpu.SemaphoreType.DMA((na,))]),
        compiler_params=_compiler_params(),
    )(*[a for a, _ in srcs])


def _reduce_scatter(arrs, name):
    na = len(arrs)
    row_chunk = 32

    def body(*refs):
        gs, outs = refs[:na], refs[na:2 * na]
        own, ra, hb, rb = (refs[(2 + t) * na:(3 + t) * na] for t in range(4))
        d2d_send, d2d_recv, ici_send, ici_recv, local_sems = refs[6 * na:]
        x, y, c = _mesh_pos()
        sibling = (x, y, 1 - c)
        loads, sends = [], []
        for a in range(na):
            for q in range(4):
                blk_mine = 4 * (q >> 1) + 2 * (q & 1) + c
                blk_sib = 4 * (q >> 1) + 2 * (q & 1) + (1 - c)
                cp = pltpu.make_async_copy(gs[a].at[blk_mine], own[a].at[q], local_sems.at[a, q])
                cp.start()
                loads.append(cp)
                cp = pltpu.make_async_remote_copy(
                    src_ref=gs[a].at[blk_sib], dst_ref=ra[a].at[q], send_sem=d2d_send.at[a, q],
                    recv_sem=d2d_recv.at[a, q], device_id=sibling, device_id_type=MESH)
                cp.start()
                sends.append(cp)
        for cp in loads:
            cp.wait()
        for cp in sends:
            cp.wait_recv()
        others = [(1 - x, y), (x, 1 - y), (1 - x, 1 - y)]

        def chunks(a):
            m = arrs[a].shape[1]
            step = next(s for s in (row_chunk, 24, 16, 8) if m % s == 0)
            return m // step, step

        def pair_sum(a, q, rs):
            return own[a][q, rs, :].astype(F32) + ra[a][q, rs, :].astype(F32)

        ici = []
        for j, (qx, qy) in enumerate(others):
            q = 2 * qx + qy
            for a in range(na):
                n_chunks, step = chunks(a)

                def to_send(i, carry, a=a, q=q, j=j, step=step):
                    rs = pl.ds(pl.multiple_of(i * step, step), step)
                    hb[a][j, rs, :] = pair_sum(a, q, rs).astype(hb[a].dtype)
                    return carry

                lax.fori_loop(0, n_chunks, to_send, 0)
                cp = pltpu.make_async_remote_copy(
                    src_ref=hb[a].at[j], dst_ref=rb[a].at[j], send_sem=ici_send.at[a, j],
                    recv_sem=ici_recv.at[a, j], device_id=(qx, qy, c), device_id_type=MESH)
                cp.start()
                ici.append(cp)
        for a in range(na):
            n_chunks, step = chunks(a)

            def mine(i, carry, a=a, step=step):
                rs = pl.ds(pl.multiple_of(i * step, step), step)
                outs[a][rs, :] = pair_sum(a, 2 * x + y, rs)
                return carry

            lax.fori_loop(0, n_chunks, mine, 0)
        for cp in ici:
            cp.wait_recv()
        for a in range(na):
            n_chunks, step = chunks(a)

            def total(i, carry, a=a, step=step):
                rs = pl.ds(pl.multiple_of(i * step, step), step)
                acc = outs[a][rs, :]
                for j in range(3):
                    acc = acc + rb[a][j, rs, :].astype(F32)
                outs[a][rs, :] = acc
                return carry

            lax.fori_loop(0, n_chunks, total, 0)
        for cp in sends + ici:
            cp.wait_send()

    vmem = pl.BlockSpec(memory_space=pltpu.VMEM)
    scratch = []
    for count in (4, 4, 3, 3):
        scratch += [pltpu.VMEM((count,) + t.shape[1:], t.dtype) for t in arrs]
    scratch += [pltpu.SemaphoreType.DMA((na, 4)), pltpu.SemaphoreType.DMA((na, 4)),
                pltpu.SemaphoreType.DMA((na, 3)), pltpu.SemaphoreType.DMA((na, 3)),
                pltpu.SemaphoreType.DMA((na, 4))]
    return pl.pallas_call(
        body, name=name,
        in_specs=[pl.BlockSpec(memory_space=pl.ANY)] * na, out_specs=[vmem] * na,
        out_shape=[jax.ShapeDtypeStruct(t.shape[1:], F32) for t in arrs],
        scratch_shapes=scratch,
        compiler_params=_compiler_params(),
    )(*arrs)


def _row_step(m):
    return next(s for s in (32, 24, 16, 8) if m % s == 0)


def _pair_reduce(arrs, name):
    na = len(arrs)

    def body(*refs):
        gs, hs, hm = refs[:na], refs[na:2 * na], refs[2 * na:3 * na]
        own, ra = refs[3 * na:4 * na], refs[4 * na:5 * na]
        d2d_send, d2d_recv, local_sems = refs[5 * na:]
        x, y, c = _mesh_pos()
        sibling = (x, y, 1 - c)
        loads, sends = [], []
        for a in range(na):
            for q in range(4):
                cp = pltpu.make_async_copy(gs[a].at[2 * q + c], own[a].at[q], local_sems.at[a, q])
                cp.start()
                loads.append(cp)
                cp = pltpu.make_async_remote_copy(
                    src_ref=gs[a].at[2 * q + (1 - c)], dst_ref=ra[a].at[q], send_sem=d2d_send.at[a, q],
                    recv_sem=d2d_recv.at[a, q], device_id=sibling, device_id_type=MESH)
                cp.start()
                sends.append(cp)
        for cp in loads:
            cp.wait()
        for cp in sends:
            cp.wait_recv()
        others = [2 * (1 - x) + y, 2 * x + (1 - y), 2 * (1 - x) + (1 - y)]
        for a in range(na):
            m = arrs[a].shape[1]
            step = _row_step(m)

            def add(i, carry, a=a, step=step):
                rs = pl.ds(pl.multiple_of(i * step, step), step)
                for j, q in enumerate(others):
                    hs[a][j, rs, :] = (own[a][q, rs, :].astype(F32) + ra[a][q, rs, :].astype(F32)).astype(hs[a].dtype)
                q = 2 * x + y
                hm[a][rs, :] = own[a][q, rs, :].astype(F32) + ra[a][q, rs, :].astype(F32)
                return carry

            lax.fori_loop(0, m // step, add, 0)
        for cp in sends:
            cp.wait_send()

    vmem = pl.BlockSpec(memory_space=pltpu.VMEM)
    scratch = [pltpu.VMEM((4,) + t.shape[1:], t.dtype) for t in arrs] * 2
    scratch += [pltpu.SemaphoreType.DMA((na, 4)), pltpu.SemaphoreType.DMA((na, 4)), pltpu.SemaphoreType.DMA((na, 4))]
    out = pl.pallas_call(
        body, name=name,
        in_specs=[pl.BlockSpec(memory_space=pl.ANY)] * na, out_specs=[vmem] * (2 * na),
        out_shape=([jax.ShapeDtypeStruct((3,) + t.shape[1:], t.dtype) for t in arrs]
                   + [jax.ShapeDtypeStruct(t.shape[1:], F32) for t in arrs]),
        scratch_shapes=scratch,
        compiler_params=_compiler_params(),
    )(*arrs)
    return out[:na], out[na:]


_HBM = pl.BlockSpec(memory_space=pltpu.HBM)
_SEM = pl.BlockSpec(memory_space=pltpu.SEMAPHORE)
_EFFECT = pltpu.SideEffectType.DATAFLOW_SIDE_EFFECTING


def _chip_exchange_start(srcs, name):
    na = len(srcs)

    def body(*refs):
        src, land = refs[:na], refs[na:2 * na]
        send_sem, recv_sem = refs[2 * na], refs[2 * na + 1]
        token = refs[-1]
        x, y, c = _mesh_pos()
        for j, (qx, qy) in enumerate([(1 - x, y), (x, 1 - y), (1 - x, 1 - y)]):
            for a in range(na):
                pltpu.make_async_remote_copy(
                    src_ref=src[a].at[j], dst_ref=land[a].at[j], send_sem=send_sem.at[3 * a + j],
                    recv_sem=recv_sem.at[3 * a + j], device_id=(qx, qy, c), device_id_type=MESH).start()
        token[...] = jnp.zeros_like(token)

    bufs = [pltpu.HBM(t.shape, t.dtype) for t in srcs]
    out = pl.pallas_call(
        body, name=name,
        out_shape=(pltpu.SemaphoreType.DMA((3 * na,)), pltpu.SemaphoreType.DMA((3 * na,)), *bufs, *bufs,
                   jax.ShapeDtypeStruct((8, 128), F32)),
        in_specs=[_HBM] * (2 * na),
        out_specs=(_SEM, _SEM, *([_HBM] * (2 * na)), pl.BlockSpec(memory_space=pltpu.VMEM)),
        input_output_aliases={i: 2 + i for i in range(2 * na)},
        compiler_params=pltpu.CompilerParams(has_side_effects=_EFFECT),
    )(*[pltpu.with_memory_space_constraint(t, pltpu.HBM) for t in srcs],
      *[pltpu.with_memory_space_constraint(lax.empty(t.shape, t.dtype), pltpu.HBM) for t in srcs])
    return out[0], out[1], out[2:2 + na], out[2 + na:2 + 2 * na], out[-1]


def _chip_exchange_wait(send_sem, recv_sem, srcs, lands, after, name):
    na = len(srcs)

    def body(*refs):
        src, land = refs[:na], refs[na:2 * na]
        send_sem_ref, recv_sem_ref = refs[2 * na], refs[2 * na + 1]
        x, y, c = _mesh_pos()
        for j, (qx, qy) in enumerate([(1 - x, y), (x, 1 - y), (1 - x, 1 - y)]):
            for a in range(na):
                cp = pltpu.make_async_remote_copy(
                    src_ref=src[a].at[j], dst_ref=land[a].at[j], send_sem=send_sem_ref.at[3 * a + j],
                    recv_sem=recv_sem_ref.at[3 * a + j], device_id=(qx, qy, c), device_id_type=MESH)
                cp.wait_send()
                cp.wait_recv()

    bufs = [pltpu.HBM(t.shape, t.dtype) for t in srcs]
    out = pl.pallas_call(
        body, name=name,
        out_shape=(*bufs, *bufs),
        in_specs=[_HBM] * (2 * na) + [_SEM, _SEM, pl.BlockSpec(memory_space=pl.ANY)],
        out_specs=[_HBM] * (2 * na),
        input_output_aliases={i: i for i in range(2 * na)},
        compiler_params=pltpu.CompilerParams(has_side_effects=_EFFECT),
    )(*srcs, *lands, send_sem, recv_sem, after)
    return out[na:]


def _sum_chip_sums(mine, lands, name):
    na = len(mine)

    def body(*refs):
        hm, ld, outs = refs[:na], refs[na:2 * na], refs[2 * na:]
        for a in range(na):
            m = mine[a].shape[0]
            step = _row_step(m)

            def total(i, carry, a=a, step=step):
                rs = pl.ds(pl.multiple_of(i * step, step), step)
                acc = hm[a][rs, :]
                for j in range(3):
                    acc = acc + ld[a][j, rs, :].astype(F32)
                outs[a][rs, :] = acc
                return carry

            lax.fori_loop(0, m // step, total, 0)

    return pl.pallas_call(
        body, name=name,
        out_shape=[jax.ShapeDtypeStruct(t.shape, F32) for t in mine],
        compiler_params=_compiler_params(),
    )(*mine, *lands)


def _adamw_math(w, g, m, v):
    m = ADAM_B1 * m + (1.0 - ADAM_B1) * g
    v = ADAM_B2 * v + (1.0 - ADAM_B2) * (g * g)
    m_hat = m / (1.0 - ADAM_B1 ** ADAM_STEP)
    v_hat = v / (1.0 - ADAM_B2 ** ADAM_STEP)
    delta = -ADAM_LR * (m_hat / (jnp.sqrt(v_hat) + ADAM_EPS) + ADAM_WD * w)
    return delta, m, v


def _adamw_big(g0, g1, w, m, v, name, rows):
    _, mm, nn = w.shape

    def body(g0_ref, g1_ref, w_ref, m_ref, v_ref, g_ref, d_ref, nm_ref, nv_ref):
        g = jnp.where(pl.program_id(0) == 0, g0_ref[...], g1_ref[...])
        g_ref[...] = g
        d, nm, nv = _adamw_math(w_ref[...], g, m_ref[...], v_ref[...])
        d_ref[...] = d
        nm_ref[...] = nm
        nv_ref[...] = nv

    gspec = pl.BlockSpec((rows, nn), lambda l, i: (i, 0))
    spec = pl.BlockSpec((None, rows, nn), lambda l, i: (l, i, 0))
    return pl.pallas_call(
        body, name=name, grid=(DEPTH, mm // rows),
        in_specs=[gspec, gspec, spec, spec, spec], out_specs=[spec] * 4,
        out_shape=[jax.ShapeDtypeStruct(w.shape, F32)] * 4,
        compiler_params=_compiler_params(("arbitrary", "arbitrary")),
    )(g0, g1, w, m, v)


def _adamw_small(g0, g1, params):
    def body(g0_ref, g1_ref, *refs):
        ins, outs = refs[:15], refs[15:]
        loss_ref = outs[0]
        packs = (g0_ref, g1_ref)
        loss_ref[...] = g1_ref[ROW_LOSS:ROW_LOSS + 1, 0:1]

        def update(p, sel, g):
            w_ref, m_ref, v_ref = ins[p], ins[5 + p], ins[10 + p]
            d, nm, nv = _adamw_math(w_ref[sel], g, m_ref[sel], v_ref[sel])
            for t, val in enumerate((g, d, nm, nv)):
                outs[1 + 5 * t + p][sel] = val

        for l in range(DEPTH):
            gp = packs[l]
            for grp in range(4):
                update(0, (l, grp), gp[ROW_PW + BLOCK * grp:ROW_PW + BLOCK * (grp + 1), :])
                update(1, (slice(l, l + 1), slice(128 * grp, 128 * (grp + 1))), gp[ROW_SC + grp:ROW_SC + grp + 1, :])
            update(2, (slice(l, l + 1), slice(None)), gp[ROW_SINK:ROW_SINK + 1, 0:N_HEADS])
            for r in range(D_MODEL // 128):
                sel = (slice(l, l + 1), slice(128 * r, 128 * (r + 1)))
                update(3, sel, gp[ROW_NPRE + r:ROW_NPRE + r + 1, :])
                update(4, sel, gp[ROW_NPOST + r:ROW_NPOST + r + 1, :])

    shapes = [jax.ShapeDtypeStruct(p.shape, F32) for p in params[:5]]
    return pl.pallas_call(
        body, name="adamw_small",
        out_shape=[jax.ShapeDtypeStruct((1, 1), F32)] + shapes * 4,
        compiler_params=_compiler_params(),
    )(g0, g1, *params)


def kernel(x, w_in, pool_w, pool_scale, attn_sinks, w_out, norm_pre, norm_post, loss_target, m_w_in, m_pool_w, m_pool_scale, m_attn_sinks, m_w_out, m_norm_pre, m_norm_post, v_w_in, v_pool_w, v_pool_scale, v_attn_sinks, v_w_out, v_norm_pre, v_norm_post):
    x0 = x.reshape(SEQ, D_MODEL)
    target = loss_target.reshape(SEQ, D_MODEL)
    bias = jnp.asarray(_attn_bias())
    w_in_t, m_in_t, v_in_t = (jnp.swapaxes(t, 1, 2) for t in (w_in, m_w_in, v_w_in))

    (win0,) = _allgather([(w_in_t, 0)], BF16, "gather_w_in0")
    wout0, win1, wout1 = _allgather([(w_out, 0), (w_in_t, 1), (w_out, 1)], BF16, "gather_w_rest")
    win_full, wout_full = (win0, win1), (wout0, wout1)

    saved = []
    xl = x0
    for layer in range(DEPTH):
        u, pg, q, k, v, ag = _fwd_in(layer, xl, norm_pre, win_full[layer])
        z, a = _fwd_mix(layer, attn_sinks, u, pg, q, k, v, ag, pool_w, pool_scale, bias)
        x_next, y = _fwd_out(layer, z, xl, norm_post, wout_full[layer])
        saved.append((xl, u, pg, q, k, v, ag, z, a, y))
        xl = x_next

    reduced = [None] * DEPTH
    dx, token, pending = None, None, None
    for layer in reversed(range(DEPTH)):
        xin, u, pg, q, k, v, ag, z, a, y = saved[layer]
        if layer == DEPTH - 1:
            dx, dz, gw_out, pack = _bwd_out(layer, True, xl, target, y, z, norm_post, wout_full[layer])
        else:
            dz, gw_out, pack = _bwd_out(layer, False, dx, token, y, z, norm_post, wout_full[layer])
        dproj, pack = _bwd_mix(layer, attn_sinks, dz, u, pg, q, k, v, ag, a, pool_w, pool_scale, bias, pack)
        dx, gw_in_t, pack = _bwd_in(layer, dproj, xin, dx, norm_pre, win_full[layer], pack)
        blocks = [gw_in_t.reshape(N_DEV, IN_SHARD, D_MODEL), gw_out.reshape(N_DEV, OUT_SHARD, D_MODEL),
                  pack.reshape(N_DEV, PACK_SLICE, 128)]
        if layer == DEPTH - 1:
            chip_sums, mine = _pair_reduce(blocks, "pair_reduce1")
            send_sem, recv_sem, srcs, lands, token = _chip_exchange_start(chip_sums, "chip_exchange_start1")
            pending = (send_sem, recv_sem, srcs, lands, mine)
        else:
            reduced[layer] = _reduce_scatter(blocks, f"scatter_grads{layer}")
    send_sem, recv_sem, srcs, lands, mine = pending
    lands = _chip_exchange_wait(send_sem, recv_sem, srcs, lands, dx, "chip_exchange_wait1")
    reduced[DEPTH - 1] = _sum_chip_sums(mine, lands, "sum_chip_sums1")

    small0, small1 = _allgather([(reduced[0][2], None), (reduced[1][2], None)], F32, "gather_small")
    small_out = _adamw_small(small0, small1, [
        pool_w, pool_scale, attn_sinks, norm_pre, norm_post,
        m_pool_w, m_pool_scale, m_attn_sinks, m_norm_pre, m_norm_post,
        v_pool_w, v_pool_scale, v_attn_sinks, v_norm_pre, v_norm_post])
    loss = small_out[0].reshape(())
    big_in = _adamw_big(reduced[0][0], reduced[1][0], w_in_t, m_in_t, v_in_t, "adamw_in", 96)
    big_out = _adamw_big(reduced[0][1], reduced[1][1], w_out, m_w_out, v_w_out, "adamw_out", 128)

    outs = [loss, dx.reshape(1, SEQ, D_MODEL)]
    for t in range(4):
        pw_, sc_, sk_, npre_, npost_ = small_out[1 + 5 * t:6 + 5 * t]
        outs += [jnp.swapaxes(big_in[t], 1, 2), pw_, sc_, sk_, big_out[t], npre_, npost_]
    return tuple(outs)
```

```python
import numpy as np
import jax
import jax.numpy as jnp
from jax import lax
from jax.experimental import pallas as pl
from jax.experimental.pallas import tpu as pltpu

F32 = jnp.float32
BF16 = jnp.bfloat16

N_DEV = 8
SEQ = 2048
D_MODEL = 1024
D_POOL = 512
D_ATTN = 512
D_KV = 128
D_IN = 2304
N_HEADS = 8
GQA = 4
HEAD_DIM = 64
BLOCK = 128
N_BLOCKS = SEQ // BLOCK
POOL_WINDOWS = (2, 4, 8, 16)
DEPTH = 2
EPS = 1e-6
NEG_INF = -1e30
SCALE = HEAD_DIM ** -0.5
IN_SHARD = D_IN // N_DEV
OUT_SHARD = D_MODEL // N_DEV

COL_U, COL_PG, COL_Q, COL_K, COL_V, COL_AG = 0, 512, 1024, 1536, 1664, 1792

ADAM_LR = 0.001
ADAM_B1 = 0.9
ADAM_B2 = 0.999
ADAM_EPS = 1e-08
ADAM_WD = 0.01
ADAM_STEP = 10

TOKEN_TILE = 256
VMEM_LIMIT = 56 * 1024 * 1024
MESH = pl.DeviceIdType.MESH

ROW_PW, ROW_SC, ROW_SINK, ROW_NPRE, ROW_NPOST, ROW_LOSS = 0, 512, 520, 528, 536, 544
PACK_ROWS = 576
PACK_SLICE = PACK_ROWS // N_DEV


def _nn(a, b):
    return jnp.dot(a, b, preferred_element_type=F32)


def _nt(a, b):
    return lax.dot_general(a, b, (((1,), (1,)), ((), ())), preferred_element_type=F32)


def _t_bf16(a32):
    return jnp.transpose(a32).astype(BF16)


def _silu_parts(g):
    s = jax.nn.sigmoid(g)
    return g * s, s * (1.0 + g * (1.0 - s))


def _compiler_params(sem=None):
    if sem is None:
        return pltpu.CompilerParams(vmem_limit_bytes=VMEM_LIMIT)
    return pltpu.CompilerParams(dimension_semantics=sem, vmem_limit_bytes=VMEM_LIMIT)


def _attn_bias():
    t = np.arange(BLOCK)[:, None]
    j = np.arange(2 * BLOCK)[None, :]
    dist = t + BLOCK - j
    in_win = (dist >= 0) & (dist < BLOCK)
    out = np.zeros((2, 2, BLOCK, GQA * 2 * BLOCK), np.float32)
    for variant in range(2):
        valid = in_win & ((j >= BLOCK) | (variant == 1))
        for kv in range(2):
            for g in range(GQA):
                slope = np.float32(2.0 ** (-(kv * GQA + g + 1)))
                b = np.where(valid, -slope * dist.astype(np.float32), np.float32(NEG_INF))
                out[variant, kv, :, g * 256:(g + 1) * 256] = b
    return out


def _block_diag(kx, kv):
    rolled = pltpu.roll(kx, 64, 1)
    lane = lax.broadcasted_iota(jnp.int32, kx.shape, 1)
    dup = jnp.where(lane < 64, kx, rolled) if kv == 0 else jnp.where(lane < 64, rolled, kx)
    rep = jnp.concatenate([dup, dup], axis=1).astype(BF16)
    lane2 = lax.broadcasted_iota(jnp.int32, rep.shape, 1)
    zero = jnp.zeros_like(rep)
    parts = [jnp.where((lane2 >= 64 * g) & (lane2 < 64 * g + 64), rep, zero) for g in range(GQA)]
    return jnp.concatenate(parts, axis=0)


def _diag_fold(m):
    lane = lax.broadcasted_iota(jnp.int32, (256, 256), 1)
    r = jnp.where(lane < 64, m[0:256], jnp.where(lane < 128, m[256:512], jnp.where(lane < 192, m[512:768], m[768:1024])))
    h = r[:, 0:128] + r[:, 128:256]
    return h + pltpu.roll(h, 64, 1)


def _window_sum(ext, w, forward):
    s = ext
    sh = 1
    while sh < w:
        s = s + pltpu.roll(s, (256 - sh) if forward else sh, 0)
        sh *= 2
    return s


def _inv_count(n, w):
    t = n * BLOCK + lax.broadcasted_iota(jnp.int32, (BLOCK, 1), 0) + 1
    return 1.0 / jnp.minimum(t.astype(F32), float(w))


def _kv_ext(ref, n):
    r0 = pl.multiple_of(jnp.maximum(n - 1, 0) * BLOCK, BLOCK)
    r1 = pl.multiple_of(n * BLOCK, BLOCK)
    return jnp.concatenate([ref[pl.ds(r0, BLOCK), :], ref[pl.ds(r1, BLOCK), :]], axis=0)


def _softmax_chunk(sg, sink):
    m = jnp.maximum(jnp.max(sg, axis=-1, keepdims=True), sink)
    p = jnp.exp(sg - m)
    esink = jnp.exp(sink - m)
    rl = 1.0 / (jnp.sum(p, axis=-1, keepdims=True) + esink)
    return p * rl, esink * rl


def _rows_of(vec_ref, pack_ref, row0):
    for r in range(D_MODEL // 128):
        pack_ref[row0 + r:row0 + r + 1, :] = vec_ref[:, 128 * r:128 * (r + 1)]


def _fwd_in(layer, x, norm_pre, w_in_t, token):
    tm = TOKEN_TILE

    def body(x_ref, g_ref, w_ref, _, u_ref, pg_ref, q_ref, k_ref, v_ref, ag_ref):
        xv = x_ref[...]
        r = lax.rsqrt(jnp.mean(xv * xv, axis=-1, keepdims=True) + EPS)
        h = (xv * r * g_ref[layer:layer + 1, :]).astype(BF16)
        u_ref[...] = _nt(h, w_ref[COL_U:COL_PG, :])
        pg_ref[...] = _nt(h, w_ref[COL_PG:COL_Q, :])
        q_ref[...] = _nt(h, w_ref[COL_Q:COL_K, :]).astype(BF16)
        k_ref[...] = _nt(h, w_ref[COL_K:COL_V, :])
        v_ref[...] = _nt(h, w_ref[COL_V:COL_AG, :])
        ag_ref[...] = _nt(h, w_ref[COL_AG:D_IN, :])

    row = lambda c: pl.BlockSpec((tm, c), lambda i: (i, 0))
    return pl.pallas_call(
        body, name=f"fwd_in{layer}", grid=(SEQ // tm,),
        in_specs=[row(D_MODEL), pl.BlockSpec((DEPTH, D_MODEL), lambda i: (0, 0)),
                  pl.BlockSpec((D_IN, D_MODEL), lambda i: (0, 0)), pl.BlockSpec((8, 128), lambda i: (0, 0))],
        out_specs=[row(D_POOL), row(D_POOL), row(D_ATTN), row(D_KV), row(D_KV), row(D_ATTN)],
        out_shape=[jax.ShapeDtypeStruct((SEQ, D_POOL), F32), jax.ShapeDtypeStruct((SEQ, D_POOL), F32),
                   jax.ShapeDtypeStruct((SEQ, D_ATTN), BF16), jax.ShapeDtypeStruct((SEQ, D_KV), F32),
                   jax.ShapeDtypeStruct((SEQ, D_KV), F32), jax.ShapeDtypeStruct((SEQ, D_ATTN), F32)],
        compiler_params=_compiler_params(("arbitrary",)),
    )(x, norm_pre, w_in_t, token)


def _fwd_mix(layer, sinks, u, pg, q, k, v, ag, pool_w, pool_scale, bias):
    def body(sink_ref, u_ref, pg_ref, q_ref, k_ref, v_ref, ag_ref, pw_ref, sc_ref, bias_ref,
             z_ref, a_ref, uprev):
        n = pl.program_id(0)

        @pl.when(n == 0)
        def _():
            uprev[...] = jnp.zeros_like(uprev)

        uv = u_ref[...]
        ext = jnp.concatenate([uprev[...], uv], axis=0)
        uprev[...] = uv
        for g, w in enumerate(POOL_WINDOWS):
            cs = slice(BLOCK * g, BLOCK * (g + 1))
            win = _window_sum(ext[:, cs], w, forward=False)[BLOCK:]
            pooled = win * _inv_count(n, w) - uv[:, cs]
            mixed = _nn(pooled.astype(BF16), pw_ref[g].astype(BF16))
            gate, _ = _silu_parts(pg_ref[:, cs])
            z_ref[:, cs] = (mixed * sc_ref[layer:layer + 1, cs] * gate).astype(BF16)

        kx = _kv_ext(k_ref, n)
        vx = _kv_ext(v_ref, n)
        for kv in range(2):
            cs = slice(256 * kv, 256 * (kv + 1))
            bk = _block_diag(kx, kv)
            bv = _block_diag(vx, kv)
            s = _nt(q_ref[:, cs], bk) * SCALE + bias_ref[0, kv]
            ps = []
            for g in range(GQA):
                p, _ = _softmax_chunk(s[:, 256 * g:256 * (g + 1)], sink_ref[layer, kv * GQA + g])
                ps.append(p.astype(BF16))
            o = _nn(jnp.concatenate(ps, axis=1), bv)
            a_ref[:, cs] = o
            gate, _ = _silu_parts(ag_ref[:, cs])
            z_ref[:, D_POOL + 256 * kv:D_POOL + 256 * (kv + 1)] = (o * gate).astype(BF16)

    blk = lambda c: pl.BlockSpec((BLOCK, c), lambda n: (n, 0))
    full = lambda shape: pl.BlockSpec(shape, lambda n: (0,) * len(shape))
    return pl.pallas_call(
        body, name=f"fwd_mix{layer}", grid=(N_BLOCKS,),
        in_specs=[pl.BlockSpec(memory_space=pltpu.SMEM), blk(D_POOL), blk(D_POOL), blk(D_ATTN),
                  full((SEQ, D_KV)), full((SEQ, D_KV)), blk(D_ATTN),
                  pl.BlockSpec((None, 4, BLOCK, BLOCK), lambda n: (layer, 0, 0, 0)), full((DEPTH, D_POOL)),
                  pl.BlockSpec((1, 2, BLOCK, 1024), lambda n: (jnp.minimum(n, 1), 0, 0, 0))],
        out_specs=[blk(D_MODEL), blk(D_ATTN)],
        out_shape=[jax.ShapeDtypeStruct((SEQ, D_MODEL), BF16), jax.ShapeDtypeStruct((SEQ, D_ATTN), F32)],
        scratch_shapes=[pltpu.VMEM((BLOCK, D_POOL), F32)],
        compiler_params=_compiler_params(("arbitrary",)),
    )(sinks, u, pg, q, k, v, ag, pool_w, pool_scale, bias)


def _fwd_out(layer, z, x, norm_post, w_out):
    tm = TOKEN_TILE

    def body(z_ref, x_ref, g_ref, w_ref, xn_ref, y_ref):
        y = _nn(z_ref[...], w_ref[...])
        y_ref[...] = y
        r = lax.rsqrt(jnp.mean(y * y, axis=-1, keepdims=True) + EPS)
        xn_ref[...] = x_ref[...] + y * r * g_ref[layer:layer + 1, :]

    row = lambda c: pl.BlockSpec((tm, c), lambda i: (i, 0))
    return pl.pallas_call(
        body, name=f"fwd_out{layer}", grid=(SEQ // tm,),
        in_specs=[row(D_MODEL), row(D_MODEL), pl.BlockSpec((DEPTH, D_MODEL), lambda i: (0, 0)),
                  pl.BlockSpec((D_MODEL, D_MODEL), lambda i: (0, 0))],
        out_specs=[row(D_MODEL), row(D_MODEL)],
        out_shape=[jax.ShapeDtypeStruct((SEQ, D_MODEL), F32), jax.ShapeDtypeStruct((SEQ, D_MODEL), F32)],
        compiler_params=_compiler_params(("arbitrary",)),
    )(z, x, norm_post, w_out)


def _bwd_out(layer, top, dxo_or_xf, target_or_token, y, z, norm_post, w_out):
    tm = TOKEN_TILE
    steps = SEQ // tm

    def body(*refs):
        if top:
            xf_ref, t_ref, y_ref, z_ref, g_ref, w_ref, dxo_ref, dz_ref, dw_ref, pack_ref, acc, dg, lacc = refs
        else:
            dxi_ref, _, y_ref, z_ref, g_ref, w_ref, dz_ref, dw_ref, pack_ref, acc, dg, lacc = refs
        i = pl.program_id(0)

        @pl.when(i == 0)
        def _():
            acc[...] = jnp.zeros_like(acc)
            dg[...] = jnp.zeros_like(dg)
            lacc[...] = jnp.zeros_like(lacc)
            pack_ref[...] = jnp.zeros_like(pack_ref)

        if top:
            d = xf_ref[...] - t_ref[...]
            dxo_v = d * (1.0 / D_MODEL)
            dxo_ref[...] = dxo_v
            part = jnp.sum(d * d, axis=-1, keepdims=True) * (1.0 / D_MODEL)
            lacc[...] += 0.5 * jnp.sum(part, axis=0, keepdims=True)
        else:
            dxo_v = dxi_ref[...]
        y = y_ref[...]
        r = lax.rsqrt(jnp.mean(y * y, axis=-1, keepdims=True) + EPS)
        yn = y * r
        dg[...] += jnp.sum(dxo_v * yn, axis=0, keepdims=True)
        dyn = dxo_v * g_ref[layer:layer + 1, :]
        dy = (r * (dyn - yn * jnp.mean(dyn * yn, axis=-1, keepdims=True))).astype(BF16)
        dz_ref[...] = _nt(dy, w_ref[...])
        acc[...] += _nn(_t_bf16(z_ref[...].astype(F32)), dy)

        @pl.when(i == steps - 1)
        def _():
            dw_ref[...] = acc[...].astype(BF16)
            _rows_of(dg, pack_ref, ROW_NPOST)
            lane = lax.broadcasted_iota(jnp.int32, (1, 128), 1)
            pack_ref[ROW_LOSS:ROW_LOSS + 1, :] = jnp.where(lane == 0, lacc[...], 0.0)

    row = lambda c: pl.BlockSpec((tm, c), lambda i: (i, 0))
    const = lambda shape: pl.BlockSpec(shape, lambda i: (0,) * len(shape))
    act = jax.ShapeDtypeStruct((SEQ, D_MODEL), F32)
    return pl.pallas_call(
        body, name=f"bwd_out{layer}", grid=(steps,),
        in_specs=([row(D_MODEL), row(D_MODEL) if top else const((8, 128))]
                  + [row(D_MODEL), row(D_MODEL), const((DEPTH, D_MODEL)), const((D_MODEL, D_MODEL))]),
        out_specs=([row(D_MODEL)] * (2 if top else 1) + [const((D_MODEL, D_MODEL)), const((PACK_ROWS, 128))]),
        out_shape=([act] * (2 if top else 1)
                   + [jax.ShapeDtypeStruct((D_MODEL, D_MODEL), BF16), jax.ShapeDtypeStruct((PACK_ROWS, 128), F32)]),
        scratch_shapes=[pltpu.VMEM((D_MODEL, D_MODEL), F32), pltpu.VMEM((1, D_MODEL), F32), pltpu.VMEM((1, 1), F32)],
        compiler_params=_compiler_params(("arbitrary",)),
    )(dxo_or_xf, target_or_token, y, z, norm_post, w_out)


def _bwd_mix(layer, sinks, dz, u, pg, q, k, v, ag, a, pool_w, pool_scale, bias, pack):
    last = N_BLOCKS - 1

    def body(sink_ref, dz_ref, u_ref, up_ref, pg_ref, q_ref, k_ref, v_ref, ag_ref, a_ref, pw_ref, sc_ref,
             bias_ref, pin_ref, dp_ref, pack_ref, ck, cv, ce):
        i = pl.program_id(0)
        n = last - i

        @pl.when(i == 0)
        def _():
            ck[...] = jnp.zeros_like(ck)
            cv[...] = jnp.zeros_like(cv)
            ce[...] = jnp.zeros_like(ce)
            pack_ref[...] = pin_ref[...]

        uv = u_ref[...]
        has_prev = (n > 0).astype(F32)
        ext = jnp.concatenate([up_ref[...] * has_prev, uv], axis=0)
        for g, w in enumerate(POOL_WINDOWS):
            cs = slice(BLOCK * g, BLOCK * (g + 1))
            inv = _inv_count(n, w)
            win = _window_sum(ext[:, cs], w, forward=False)[BLOCK:]
            pooled = win * inv - uv[:, cs]
            pw_g = pw_ref[g].astype(BF16)
            mixed = _nn(pooled.astype(BF16), pw_g)
            gate, dgate = _silu_parts(pg_ref[:, cs])
            dzp = dz_ref[:, cs]
            sc = sc_ref[layer:layer + 1, cs]
            dpm = dzp * gate
            dp_ref[:, COL_PG + BLOCK * g:COL_PG + BLOCK * (g + 1)] = (dzp * (mixed * sc) * dgate).astype(BF16)
            pack_ref[ROW_SC + g:ROW_SC + g + 1, :] += jnp.sum(dpm * mixed, axis=0, keepdims=True)
            dmixed = (dpm * sc).astype(BF16)
            pack_ref[ROW_PW + BLOCK * g:ROW_PW + BLOCK * (g + 1), :] += _nn(_t_bf16(pooled), dmixed)
            dpooled = _nt(dmixed, pw_g)
            e = dpooled * inv
            lead = _window_sum(jnp.concatenate([e, ce[:, cs]], axis=0), w, forward=True)[:BLOCK]
            dp_ref[:, COL_U + BLOCK * g:COL_U + BLOCK * (g + 1)] = (lead - dpooled).astype(BF16)
            ce[:, cs] = e

        kx = _kv_ext(k_ref, n)
        vx = _kv_ext(v_ref, n)
        lane = lax.broadcasted_iota(jnp.int32, (1, 128), 1)
        dsink_row = jnp.zeros((1, 128), F32)
        tks, tvs = [], []
        for kv in range(2):
            cs = slice(256 * kv, 256 * (kv + 1))
            bk = _block_diag(kx, kv)
            bv = _block_diag(vx, kv)
            qv = q_ref[:, cs]
            s = _nt(qv, bk) * SCALE + bias_ref[0, kv]
            gate, dgate = _silu_parts(ag_ref[:, cs])
            dza = dz_ref[:, D_POOL + 256 * kv:D_POOL + 256 * (kv + 1)]
            dp_ref[:, COL_AG + 256 * kv:COL_AG + 256 * (kv + 1)] = (dza * a_ref[:, cs] * dgate).astype(BF16)
            da = dza * gate
            dab = da.astype(BF16)
            dpr = _nt(dab, bv)
            ps, dss = [], []
            for g in range(GQA):
                gs = slice(256 * g, 256 * (g + 1))
                p, psink = _softmax_chunk(s[:, gs], sink_ref[layer, kv * GQA + g])
                dpg_ = dpr[:, gs]
                delta = jnp.sum(p * dpg_, axis=-1, keepdims=True)
                dsink = -jnp.sum(psink * delta, axis=0, keepdims=True)
                dsink_row = dsink_row + jnp.where(lane == kv * GQA + g, dsink, 0.0)
                ps.append(p)
                dss.append(p * (dpg_ - delta) * SCALE)
            p_all = jnp.concatenate(ps, axis=1)
            ds_all = jnp.concatenate(dss, axis=1)
            dp_ref[:, COL_Q + 256 * kv:COL_Q + 256 * (kv + 1)] = _nn(ds_all.astype(BF16), bk).astype(BF16)
            tks.append(_diag_fold(_nn(_t_bf16(ds_all), qv)))
            tvs.append(_diag_fold(_nn(_t_bf16(p_all), dab)))
        pack_ref[ROW_SINK:ROW_SINK + 1, :] += dsink_row
        lane2 = lax.broadcasted_iota(jnp.int32, (256, 128), 1)
        dkx = jnp.where(lane2 < 64, tks[0], tks[1])
        dvx = jnp.where(lane2 < 64, tvs[0], tvs[1])
        dp_ref[:, COL_K:COL_V] = (ck[...] + dkx[BLOCK:]).astype(BF16)
        dp_ref[:, COL_V:COL_AG] = (cv[...] + dvx[BLOCK:]).astype(BF16)
        ck[...] = dkx[:BLOCK]
        cv[...] = dvx[:BLOCK]

    blk = lambda c: pl.BlockSpec((BLOCK, c), lambda i: (last - i, 0))
    full = lambda shape: pl.BlockSpec(shape, lambda i: (0,) * len(shape))
    return pl.pallas_call(
        body, name=f"bwd_mix{layer}", grid=(N_BLOCKS,),
        in_specs=[pl.BlockSpec(memory_space=pltpu.SMEM), blk(D_MODEL), blk(D_POOL),
                  pl.BlockSpec((BLOCK, D_POOL), lambda i: (jnp.maximum(last - i - 1, 0), 0)),
                  blk(D_POOL), blk(D_ATTN), full((SEQ, D_KV)), full((SEQ, D_KV)), blk(D_ATTN), blk(D_ATTN),
                  pl.BlockSpec((None, 4, BLOCK, BLOCK), lambda i: (layer, 0, 0, 0)), full((DEPTH, D_POOL)),
                  pl.BlockSpec((1, 2, BLOCK, 1024), lambda i: (jnp.minimum(last - i, 1), 0, 0, 0)),
                  full((PACK_ROWS, 128))],
        out_specs=[blk(D_IN), full((PACK_ROWS, 128))],
        out_shape=[jax.ShapeDtypeStruct((SEQ, D_IN), BF16), jax.ShapeDtypeStruct((PACK_ROWS, 128), F32)],
        scratch_shapes=[pltpu.VMEM((BLOCK, D_KV), F32), pltpu.VMEM((BLOCK, D_KV), F32),
                        pltpu.VMEM((BLOCK, D_POOL), F32)],
        input_output_aliases={13: 1},
        compiler_params=_compiler_params(("arbitrary",)),
    )(sinks, dz, u, u, pg, q, k, v, ag, a, pool_w, pool_scale, bias, pack)


def _bwd_in(layer, dproj, x, dxo, norm_pre, w_in_t, pack):
    tm = TOKEN_TILE
    steps = SEQ // tm
    cw = 256

    def body(dp_ref, x_ref, dxo_ref, g_ref, w_ref, pin_ref, dx_ref, dw_ref, pack_ref, acc, dg):
        i = pl.program_id(0)

        @pl.when(i == 0)
        def _():
            acc[...] = jnp.zeros_like(acc)
            dg[...] = jnp.zeros_like(dg)
            pack_ref[...] = pin_ref[...]

        xv = x_ref[...]
        gv = g_ref[layer:layer + 1, :]
        r = lax.rsqrt(jnp.mean(xv * xv, axis=-1, keepdims=True) + EPS)
        xn = xv * r
        hb = (xn * gv).astype(BF16)
        dh = _nn(dp_ref[...], w_ref[...])
        for c in range(0, D_IN, cw):
            acc[c:c + cw, :] += _nn(_t_bf16(dp_ref[:, c:c + cw].astype(F32)), hb)
        dg[...] += jnp.sum(dh * xn, axis=0, keepdims=True)
        dhn = dh * gv
        dx_ref[...] = dxo_ref[...] + r * (dhn - xn * jnp.mean(dhn * xn, axis=-1, keepdims=True))

        @pl.when(i == steps - 1)
        def _():
            dw_ref[...] = acc[...].astype(BF16)
            _rows_of(dg, pack_ref, ROW_NPRE)

    row = lambda c: pl.BlockSpec((tm, c), lambda i: (i, 0))
    const = lambda shape: pl.BlockSpec(shape, lambda i: (0,) * len(shape))
    return pl.pallas_call(
        body, name=f"bwd_in{layer}", grid=(steps,),
        in_specs=[row(D_IN), row(D_MODEL), row(D_MODEL), const((DEPTH, D_MODEL)), const((D_IN, D_MODEL)),
                  const((PACK_ROWS, 128))],
        out_specs=[row(D_MODEL), const((D_IN, D_MODEL)), const((PACK_ROWS, 128))],
        out_shape=[jax.ShapeDtypeStruct((SEQ, D_MODEL), F32), jax.ShapeDtypeStruct((D_IN, D_MODEL), BF16),
                   jax.ShapeDtypeStruct((PACK_ROWS, 128), F32)],
        scratch_shapes=[pltpu.VMEM((D_IN, D_MODEL), F32), pltpu.VMEM((1, D_MODEL), F32)],
        input_output_aliases={5: 2},
        compiler_params=_compiler_params(("arbitrary",)),
    )(dproj, x, dxo, norm_pre, w_in_t, pack)


def _mesh_pos():
    return lax.axis_index("x"), lax.axis_index("y"), lax.axis_index("c")


def _device_rows(ref, m, px, py, pc):
    return ref.at[pl.ds(pl.multiple_of((4 * px + 2 * py + pc) * m, 16 if m % 16 == 0 else 8), m), :]


def _allgather(srcs, out_dtype, name, later=()):
    na, nb = len(srcs), len(later)
    every = list(srcs) + list(later)
    shapes = [(a.shape[-2], a.shape[-1]) for a, _ in every]

    def body(*refs):
        xs, refs = refs[:na + nb], refs[na + nb:]
        outs, cast, land, refs = refs[:na], refs[na:na + nb], refs[na + nb:na + 2 * nb], refs[na + 2 * nb:]
        stage, (send_sems, recv_sems, local_sems) = refs[:na], refs[na:]
        x, y, c = _mesh_pos()
        me, sibling = (x, y, c), (x, y, 1 - c)
        chips = [(1 - x, y), (x, 1 - y), (1 - x, 1 - y)]

        def slot(a, px, py, pc):
            return _device_rows(outs[a], shapes[a][0], px, py, pc)

        def copy(a, k, block, to, src=None):
            return pltpu.make_async_remote_copy(
                src_ref=slot(a, *block) if src is None else src, dst_ref=slot(a, *block),
                send_sem=send_sems.at[a, k], recv_sem=recv_sems.at[a, k], device_id=to, device_id_type=MESH)

        def cast_block(i):
            layer = every[i][1]
            return (xs[i][...] if layer is None else xs[i][layer]).astype(out_dtype)

        for a in range(na):
            stage[a][...] = cast_block(a)
        mine = [pltpu.make_async_copy(stage[a], slot(a, *me), local_sems.at[a]) for a in range(na)]
        for cp in mine:
            cp.start()
        first = []
        for a in range(na):
            first.append(copy(a, 0, me, sibling, src=stage[a]))
            first += [copy(a, 1 + j, me, (*chip, c), src=stage[a]) for j, chip in enumerate(chips)]
        for cp in first:
            cp.start()
        for b in range(nb):
            cast[b][...] = cast_block(na + b)
            cp = pltpu.make_async_copy(cast[b], _device_rows(land[b], shapes[na + b][0], *me), local_sems.at[na + b])
            cp.start()
            mine.append(cp)
        passed = []
        for a in range(na):
            for j, chip in enumerate(chips):
                copy(a, 1 + j, (*chip, c), me).wait_recv()
                cp = copy(a, 4 + j, (*chip, c), sibling)
                cp.start()
                passed.append(cp)
        for a in range(na):
            copy(a, 0, sibling, me).wait_recv()
            for j, chip in enumerate(chips):
                copy(a, 4 + j, (*chip, 1 - c), me).wait_recv()
        for cp in first + passed:
            cp.wait_send()
        for cp in mine:
            cp.wait()

    vmem = pl.BlockSpec(memory_space=pltpu.VMEM)
    gathered = [jax.ShapeDtypeStruct((N_DEV * m, n), out_dtype) for m, n in shapes]
    out = pl.pallas_call(
        body, name=name,
        in_specs=[vmem] * (na + nb),
        out_specs=[vmem] * (na + nb) + [pl.BlockSpec(memory_space=pl.ANY)] * nb,
        out_shape=gathered[:na] + [jax.ShapeDtypeStruct(s, out_dtype) for s in shapes[na:]] + gathered[na:],
        scratch_shapes=([pltpu.VMEM(s, out_dtype) for s in shapes[:na]]
                        + [pltpu.SemaphoreType.DMA((na, 7)), pltpu.SemaphoreType.DMA((na, 7)),
                           pltpu.SemaphoreType.DMA((na + nb,))]),
        compiler_params=_compiler_params(),
    )(*[a for a, _ in every])
    return out[:na], out[na:na + nb], out[na + nb:]


def _gather_start(blocks, lands, name):
    na = len(blocks)

    def body(*refs):
        src, land, sems, token = refs[:na], refs[na:2 * na], refs[2 * na:4 * na], refs[-1]
        x, y, c = _mesh_pos()
        for a in range(na):
            for k in range(1, N_DEV):
                peer = (x ^ ((k >> 2) & 1), y ^ ((k >> 1) & 1), c ^ (k & 1))
                pltpu.make_async_remote_copy(
                    src_ref=src[a], dst_ref=_device_rows(land[a], blocks[a].shape[0], x, y, c),
                    send_sem=sems[2 * a].at[k - 1], recv_sem=sems[2 * a + 1].at[k - 1],
                    device_id=peer, device_id_type=MESH).start()
        token[...] = jnp.zeros_like(token)

    bufs = [pltpu.HBM(t.shape, t.dtype) for t in list(blocks) + list(lands)]
    out = pl.pallas_call(
        body, name=name,
        out_shape=(*([pltpu.SemaphoreType.DMA((N_DEV - 1,))] * (2 * na)), *bufs, jax.ShapeDtypeStruct((8, 128), F32)),
        in_specs=[_HBM] * (2 * na),
        out_specs=(*([_SEM] * (2 * na)), *([_HBM] * (2 * na)), pl.BlockSpec(memory_space=pltpu.VMEM)),
        input_output_aliases={i: 2 * na + i for i in range(2 * na)},
        compiler_params=pltpu.CompilerParams(has_side_effects=_EFFECT),
    )(*[pltpu.with_memory_space_constraint(t, pltpu.HBM) for t in list(blocks) + list(lands)])
    sems = [(out[2 * a], out[2 * a + 1]) for a in range(na)]
    return sems, out[2 * na:3 * na], out[3 * na:4 * na], out[-1]


def _gather_wait(sems, block, land, after, name):
    def body(src, land_ref, send_sem, recv_sem, after_ref, src_out, land_out):
        x, y, c = _mesh_pos()
        for k in range(1, N_DEV):
            peer = (x ^ ((k >> 2) & 1), y ^ ((k >> 1) & 1), c ^ (k & 1))
            cp = pltpu.make_async_remote_copy(
                src_ref=src, dst_ref=_device_rows(land_ref, block.shape[0], *peer),
                send_sem=send_sem.at[k - 1], recv_sem=recv_sem.at[k - 1], device_id=peer, device_id_type=MESH)
            cp.wait_send()
            cp.wait_recv()

    out = pl.pallas_call(
        body, name=name,
        out_shape=(pltpu.HBM(block.shape, block.dtype), pltpu.HBM(land.shape, land.dtype)),
        in_specs=[_HBM, _HBM, _SEM, _SEM, pl.BlockSpec(memory_space=pl.ANY)],
        out_specs=[_HBM, _HBM],
        input_output_aliases={0: 0, 1: 1},
        compiler_params=pltpu.CompilerParams(has_side_effects=_EFFECT),
    )(block, land, sems[0], sems[1], after)
    return out[1]


def _reduce_scatter(arrs, name):
    na = len(arrs)
    row_chunk = 32

    def body(*refs):
        gs, outs = refs[:na], refs[na:2 * na]
        own, ra, hb, rb = (refs[(2 + t) * na:(3 + t) * na] for t in range(4))
        d2d_send, d2d_recv, ici_send, ici_recv, local_sems = refs[6 * na:]
        x, y, c = _mesh_pos()
        sibling = (x, y, 1 - c)
        loads, sends = [], []
        for a in range(na):
            for q in range(4):
                blk_mine = 4 * (q >> 1) + 2 * (q & 1) + c
                blk_sib = 4 * (q >> 1) + 2 * (q & 1) + (1 - c)
                cp = pltpu.make_async_copy(gs[a].at[blk_mine], own[a].at[q], local_sems.at[a, q])
                cp.start()
                loads.append(cp)
                cp = pltpu.make_async_remote_copy(
                    src_ref=gs[a].at[blk_sib], dst_ref=ra[a].at[q], send_sem=d2d_send.at[a, q],
                    recv_sem=d2d_recv.at[a, q], device_id=sibling, device_id_type=MESH)
                cp.start()
                sends.append(cp)
        for cp in loads:
            cp.wait()
        for cp in sends:
            cp.wait_recv()
        others = [(1 - x, y), (x, 1 - y), (1 - x, 1 - y)]

        def chunks(a):
            m = arrs[a].shape[1]
            step = next(s for s in (row_chunk, 24, 16, 8) if m % s == 0)
            return m // step, step

        def pair_sum(a, q, rs):
            return own[a][q, rs, :].astype(F32) + ra[a][q, rs, :].astype(F32)

        ici = []
        for j, (qx, qy) in enumerate(others):
            q = 2 * qx + qy
            for a in range(na):
                n_chunks, step = chunks(a)

                def to_send(i, carry, a=a, q=q, j=j, step=step):
                    rs = pl.ds(pl.multiple_of(i * step, step), step)
                    hb[a][j, rs, :] = pair_sum(a, q, rs).astype(hb[a].dtype)
                    return carry

                lax.fori_loop(0, n_chunks, to_send, 0)
                cp = pltpu.make_async_remote_copy(
                    src_ref=hb[a].at[j], dst_ref=rb[a].at[j], send_sem=ici_send.at[a, j],
                    recv_sem=ici_recv.at[a, j], device_id=(qx, qy, c), device_id_type=MESH)
                cp.start()
                ici.append(cp)
        for a in range(na):
            n_chunks, step = chunks(a)

            def mine(i, carry, a=a, step=step):
                rs = pl.ds(pl.multiple_of(i * step, step), step)
                outs[a][rs, :] = pair_sum(a, 2 * x + y, rs)
                return carry

            lax.fori_loop(0, n_chunks, mine, 0)
        for cp in ici:
            cp.wait_recv()
        for a in range(na):
            n_chunks, step = chunks(a)

            def total(i, carry, a=a, step=step):
                rs = pl.ds(pl.multiple_of(i * step, step), step)
                acc = outs[a][rs, :]
                for j in range(3):
                    acc = acc + rb[a][j, rs, :].astype(F32)
                outs[a][rs, :] = acc
                return carry

            lax.fori_loop(0, n_chunks, total, 0)
        for cp in sends + ici:
            cp.wait_send()

    vmem = pl.BlockSpec(memory_space=pltpu.VMEM)
    scratch = []
    for count in (4, 4, 3, 3):
        scratch += [pltpu.VMEM((count,) + t.shape[1:], t.dtype) for t in arrs]
    scratch += [pltpu.SemaphoreType.DMA((na, 4)), pltpu.SemaphoreType.DMA((na, 4)),
                pltpu.SemaphoreType.DMA((na, 3)), pltpu.SemaphoreType.DMA((na, 3)),
                pltpu.SemaphoreType.DMA((na, 4))]
    return pl.pallas_call(
        body, name=name,
        in_specs=[pl.BlockSpec(memory_space=pl.ANY)] * na, out_specs=[vmem] * na,
        out_shape=[jax.ShapeDtypeStruct(t.shape[1:], F32) for t in arrs],
        scratch_shapes=scratch,
        compiler_params=_compiler_params(),
    )(*arrs)


def _row_step(m):
    return next(s for s in (32, 24, 16, 8) if m % s == 0)


def _pair_reduce(arrs, name):
    na = len(arrs)

    def body(*refs):
        gs, hs, hm = refs[:na], refs[na:2 * na], refs[2 * na:3 * na]
        own, ra = refs[3 * na:4 * na], refs[4 * na:5 * na]
        d2d_send, d2d_recv, local_sems = refs[5 * na:]
        x, y, c = _mesh_pos()
        sibling = (x, y, 1 - c)
        loads, sends = [], []
        for a in range(na):
            for q in range(4):
                cp = pltpu.make_async_copy(gs[a].at[2 * q + c], own[a].at[q], local_sems.at[a, q])
                cp.start()
                loads.append(cp)
                cp = pltpu.make_async_remote_copy(
                    src_ref=gs[a].at[2 * q + (1 - c)], dst_ref=ra[a].at[q], send_sem=d2d_send.at[a, q],
                    recv_sem=d2d_recv.at[a, q], device_id=sibling, device_id_type=MESH)
                cp.start()
                sends.append(cp)
        for cp in loads:
            cp.wait()
        for cp in sends:
            cp.wait_recv()
        others = [2 * (1 - x) + y, 2 * x + (1 - y), 2 * (1 - x) + (1 - y)]
        for a in range(na):
            m = arrs[a].shape[1]
            step = _row_step(m)

            def add(i, carry, a=a, step=step):
                rs = pl.ds(pl.multiple_of(i * step, step), step)
                for j, q in enumerate(others):
                    hs[a][j, rs, :] = (own[a][q, rs, :].astype(F32) + ra[a][q, rs, :].astype(F32)).astype(hs[a].dtype)
                q = 2 * x + y
                hm[a][rs, :] = own[a][q, rs, :].astype(F32) + ra[a][q, rs, :].astype(F32)
                return carry

            lax.fori_loop(0, m // step, add, 0)
        for cp in sends:
            cp.wait_send()

    vmem = pl.BlockSpec(memory_space=pltpu.VMEM)
    scratch = [pltpu.VMEM((4,) + t.shape[1:], t.dtype) for t in arrs] * 2
    scratch += [pltpu.SemaphoreType.DMA((na, 4)), pltpu.SemaphoreType.DMA((na, 4)), pltpu.SemaphoreType.DMA((na, 4))]
    out = pl.pallas_call(
        body, name=name,
        in_specs=[pl.BlockSpec(memory_space=pl.ANY)] * na, out_specs=[vmem] * (2 * na),
        out_shape=([jax.ShapeDtypeStruct((3,) + t.shape[1:], t.dtype) for t in arrs]
                   + [jax.ShapeDtypeStruct(t.shape[1:], F32) for t in arrs]),
        scratch_shapes=scratch,
        compiler_params=_compiler_params(),
    )(*arrs)
    return out[:na], out[na:]


_HBM = pl.BlockSpec(memory_space=pltpu.HBM)
_SEM = pl.BlockSpec(memory_space=pltpu.SEMAPHORE)
_EFFECT = pltpu.SideEffectType.DATAFLOW_SIDE_EFFECTING


def _chip_exchange_start(srcs, name):
    na = len(srcs)

    def body(*refs):
        src, land = refs[:na], refs[na:2 * na]
        send_sem, recv_sem = refs[2 * na], refs[2 * na + 1]
        token = refs[-1]
        x, y, c = _mesh_pos()
        for j, (qx, qy) in enumerate([(1 - x, y), (x, 1 - y), (1 - x, 1 - y)]):
            for a in range(na):
                pltpu.make_async_remote_copy(
                    src_ref=src[a].at[j], dst_ref=land[a].at[j], send_sem=send_sem.at[3 * a + j],
                    recv_sem=recv_sem.at[3 * a + j], device_id=(qx, qy, c), device_id_type=MESH).start()
        token[...] = jnp.zeros_like(token)

    bufs = [pltpu.HBM(t.shape, t.dtype) for t in srcs]
    out = pl.pallas_call(
        body, name=name,
        out_shape=(pltpu.SemaphoreType.DMA((3 * na,)), pltpu.SemaphoreType.DMA((3 * na,)), *bufs, *bufs,
                   jax.ShapeDtypeStruct((8, 128), F32)),
        in_specs=[_HBM] * (2 * na),
        out_specs=(_SEM, _SEM, *([_HBM] * (2 * na)), pl.BlockSpec(memory_space=pltpu.VMEM)),
        input_output_aliases={i: 2 + i for i in range(2 * na)},
        compiler_params=pltpu.CompilerParams(has_side_effects=_EFFECT),
    )(*[pltpu.with_memory_space_constraint(t, pltpu.HBM) for t in srcs],
      *[pltpu.with_memory_space_constraint(lax.empty(t.shape, t.dtype), pltpu.HBM) for t in srcs])
    return out[0], out[1], out[2:2 + na], out[2 + na:2 + 2 * na], out[-1]


def _chip_exchange_wait(send_sem, recv_sem, srcs, lands, after, name):
    na = len(srcs)

    def body(*refs):
        src, land = refs[:na], refs[na:2 * na]
        send_sem_ref, recv_sem_ref = refs[2 * na], refs[2 * na + 1]
        x, y, c = _mesh_pos()
        for j, (qx, qy) in enumerate([(1 - x, y), (x, 1 - y), (1 - x, 1 - y)]):
            for a in range(na):
                cp = pltpu.make_async_remote_copy(
                    src_ref=src[a].at[j], dst_ref=land[a].at[j], send_sem=send_sem_ref.at[3 * a + j],
                    recv_sem=recv_sem_ref.at[3 * a + j], device_id=(qx, qy, c), device_id_type=MESH)
                cp.wait_send()
                cp.wait_recv()

    bufs = [pltpu.HBM(t.shape, t.dtype) for t in srcs]
    out = pl.pallas_call(
        body, name=name,
        out_shape=(*bufs, *bufs),
        in_specs=[_HBM] * (2 * na) + [_SEM, _SEM, pl.BlockSpec(memory_space=pl.ANY)],
        out_specs=[_HBM] * (2 * na),
        input_output_aliases={i: i for i in range(2 * na)},
        compiler_params=pltpu.CompilerParams(has_side_effects=_EFFECT),
    )(*srcs, *lands, send_sem, recv_sem, after)
    return out[na:]


def _sum_chip_sums(mine, lands, name):
    na = len(mine)

    def body(*refs):
        hm, ld, outs = refs[:na], refs[na:2 * na], refs[2 * na:]
        for a in range(na):
            m = mine[a].shape[0]
            step = _row_step(m)

            def total(i, carry, a=a, step=step):
                rs = pl.ds(pl.multiple_of(i * step, step), step)
                acc = hm[a][rs, :]
                for j in range(3):
                    acc = acc + ld[a][j, rs, :].astype(F32)
                outs[a][rs, :] = acc
                return carry

            lax.fori_loop(0, m // step, total, 0)

    return pl.pallas_call(
        body, name=name,
        out_shape=[jax.ShapeDtypeStruct(t.shape, F32) for t in mine],
        compiler_params=_compiler_params(),
    )(*mine, *lands)


def _adamw_math(w, g, m, v):
    m = ADAM_B1 * m + (1.0 - ADAM_B1) * g
    v = ADAM_B2 * v + (1.0 - ADAM_B2) * (g * g)
    m_hat = m / (1.0 - ADAM_B1 ** ADAM_STEP)
    v_hat = v / (1.0 - ADAM_B2 ** ADAM_STEP)
    delta = -ADAM_LR * (m_hat / (jnp.sqrt(v_hat) + ADAM_EPS) + ADAM_WD * w)
    return delta, m, v


def _adamw_big(g0, g1, w, m, v, name, rows):
    _, mm, nn = w.shape

    def body(g0_ref, g1_ref, w_ref, m_ref, v_ref, g_ref, d_ref, nm_ref, nv_ref):
        g = jnp.where(pl.program_id(0) == 0, g0_ref[...], g1_ref[...])
        g_ref[...] = g
        d, nm, nv = _adamw_math(w_ref[...], g, m_ref[...], v_ref[...])
        d_ref[...] = d
        nm_ref[...] = nm
        nv_ref[...] = nv

    gspec = pl.BlockSpec((rows, nn), lambda l, i: (i, 0))
    spec = pl.BlockSpec((None, rows, nn), lambda l, i: (l, i, 0))
    return pl.pallas_call(
        body, name=name, grid=(DEPTH, mm // rows),
        in_specs=[gspec, gspec, spec, spec, spec], out_specs=[spec] * 4,
        out_shape=[jax.ShapeDtypeStruct(w.shape, F32)] * 4,
        compiler_params=_compiler_params(("arbitrary", "arbitrary")),
    )(g0, g1, w, m, v)


def _adamw_small(g0, g1, params):
    def body(g0_ref, g1_ref, *refs):
        ins, outs = refs[:15], refs[15:]
        loss_ref = outs[0]
        packs = (g0_ref, g1_ref)
        loss_ref[...] = g1_ref[ROW_LOSS:ROW_LOSS + 1, 0:1]

        def update(p, sel, g):
            w_ref, m_ref, v_ref = ins[p], ins[5 + p], ins[10 + p]
            d, nm, nv = _adamw_math(w_ref[sel], g, m_ref[sel], v_ref[sel])
            for t, val in enumerate((g, d, nm, nv)):
                outs[1 + 5 * t + p][sel] = val

        for l in range(DEPTH):
            gp = packs[l]
            for grp in range(4):
                update(0, (l, grp), gp[ROW_PW + BLOCK * grp:ROW_PW + BLOCK * (grp + 1), :])
                update(1, (slice(l, l + 1), slice(128 * grp, 128 * (grp + 1))), gp[ROW_SC + grp:ROW_SC + grp + 1, :])
            update(2, (slice(l, l + 1), slice(None)), gp[ROW_SINK:ROW_SINK + 1, 0:N_HEADS])
            for r in range(D_MODEL // 128):
                sel = (slice(l, l + 1), slice(128 * r, 128 * (r + 1)))
                update(3, sel, gp[ROW_NPRE + r:ROW_NPRE + r + 1, :])
                update(4, sel, gp[ROW_NPOST + r:ROW_NPOST + r + 1, :])

    shapes = [jax.ShapeDtypeStruct(p.shape, F32) for p in params[:5]]
    return pl.pallas_call(
        body, name="adamw_small",
        out_shape=[jax.ShapeDtypeStruct((1, 1), F32)] + shapes * 4,
        compiler_params=_compiler_params(),
    )(g0, g1, *params)


def kernel(x, w_in, pool_w, pool_scale, attn_sinks, w_out, norm_pre, norm_post, loss_target, m_w_in, m_pool_w, m_pool_scale, m_attn_sinks, m_w_out, m_norm_pre, m_norm_post, v_w_in, v_pool_w, v_pool_scale, v_attn_sinks, v_w_out, v_norm_pre, v_norm_post):
    x0 = x.reshape(SEQ, D_MODEL)
    target = loss_target.reshape(SEQ, D_MODEL)
    bias = jnp.asarray(_attn_bias())
    w_in_t, m_in_t, v_in_t = (jnp.swapaxes(t, 1, 2) for t in (w_in, m_w_in, v_w_in))

    (win0,), later, lands = _allgather([(w_in_t, 0)], BF16, "gather_w_in0",
                                       later=[(w_out, 0), (w_in_t, 1), (w_out, 1)])
    sems, later, lands, token = _gather_start(later, lands, "gather_w_rest_start")
    win_full, wout_full = [win0, None], [None, None]

    saved = []
    xl = x0
    for layer in range(DEPTH):
        if layer > 0:
            win_full[layer] = _gather_wait(sems[1], later[1], lands[1], xl, "gather_w_in1_wait")
        u, pg, q, k, v, ag = _fwd_in(layer, xl, norm_pre, win_full[layer], token)
        z, a = _fwd_mix(layer, attn_sinks, u, pg, q, k, v, ag, pool_w, pool_scale, bias)
        i_out = 2 * layer
        wout_full[layer] = _gather_wait(sems[i_out], later[i_out], lands[i_out], z, f"gather_w_out{layer}_wait")
        x_next, y = _fwd_out(layer, z, xl, norm_post, wout_full[layer])
        saved.append((xl, u, pg, q, k, v, ag, z, a, y))
        xl = x_next

    reduced = [None] * DEPTH
    dx, token, pending = None, None, None
    for layer in reversed(range(DEPTH)):
        xin, u, pg, q, k, v, ag, z, a, y = saved[layer]
        if layer == DEPTH - 1:
            dx, dz, gw_out, pack = _bwd_out(layer, True, xl, target, y, z, norm_post, wout_full[layer])
        else:
            dz, gw_out, pack = _bwd_out(layer, False, dx, token, y, z, norm_post, wout_full[layer])
        dproj, pack = _bwd_mix(layer, attn_sinks, dz, u, pg, q, k, v, ag, a, pool_w, pool_scale, bias, pack)
        dx, gw_in_t, pack = _bwd_in(layer, dproj, xin, dx, norm_pre, win_full[layer], pack)
        blocks = [gw_in_t.reshape(N_DEV, IN_SHARD, D_MODEL), gw_out.reshape(N_DEV, OUT_SHARD, D_MODEL),
                  pack.reshape(N_DEV, PACK_SLICE, 128)]
        if layer == DEPTH - 1:
            chip_sums, mine = _pair_reduce(blocks, "pair_reduce1")
            send_sem, recv_sem, srcs, lands, token = _chip_exchange_start(chip_sums, "chip_exchange_start1")
            pending = (send_sem, recv_sem, srcs, lands, mine)
        else:
            reduced[layer] = _reduce_scatter(blocks, f"scatter_grads{layer}")
    send_sem, recv_sem, srcs, lands, mine = pending
    lands = _chip_exchange_wait(send_sem, recv_sem, srcs, lands, dx, "chip_exchange_wait1")
    reduced[DEPTH - 1] = _sum_chip_sums(mine, lands, "sum_chip_sums1")

    (small0, small1), _, _ = _allgather([(reduced[0][2], None), (reduced[1][2], None)], F32, "gather_small")
    small_out = _adamw_small(small0, small1, [
        pool_w, pool_scale, attn_sinks, norm_pre, norm_post,
        m_pool_w, m_pool_scale, m_attn_sinks, m_norm_pre, m_norm_post,
        v_pool_w, v_pool_scale, v_attn_sinks, v_norm_pre, v_norm_post])
    loss = small_out[0].reshape(())
    big_in = _adamw_big(reduced[0][0], reduced[1][0], w_in_t, m_in_t, v_in_t, "adamw_in", 96)
    big_out = _adamw_big(reduced[0][1], reduced[1][1], w_out, m_w_out, v_w_out, "adamw_out", 128)

    outs = [loss, dx.reshape(1, SEQ, D_MODEL)]
    for t in range(4):
        pw_, sc_, sk_, npre_, npost_ = small_out[1 + 5 * t:6 + 5 * t]
        outs += [jnp.swapaxes(big_in[t], 1, 2), pw_, sc_, sk_, big_out[t], npre_, npost_]
    return tuple(outs)
```

```python
import numpy as np
import jax
import jax.numpy as jnp
from jax import lax
from jax.experimental import pallas as pl
from jax.experimental.pallas import tpu as pltpu

F32 = jnp.float32
BF16 = jnp.bfloat16

N_DEV = 8
SEQ = 2048
D_MODEL = 1024
D_POOL = 512
D_ATTN = 512
D_KV = 128
D_IN = 2304
N_HEADS = 8
GQA = 4
HEAD_DIM = 64
BLOCK = 128
N_BLOCKS = SEQ // BLOCK
POOL_WINDOWS = (2, 4, 8, 16)
DEPTH = 2
EPS = 1e-6
NEG_INF = -1e30
SCALE = HEAD_DIM ** -0.5
IN_SHARD = D_IN // N_DEV
OUT_SHARD = D_MODEL // N_DEV

COL_U, COL_PG, COL_Q, COL_K, COL_V, COL_AG = 0, 512, 1024, 1536, 1664, 1792

ADAM_LR = 0.001
ADAM_B1 = 0.9
ADAM_B2 = 0.999
ADAM_EPS = 1e-08
ADAM_WD = 0.01
ADAM_STEP = 10

TOKEN_TILE = 512
VMEM_LIMIT = 56 * 1024 * 1024
MESH = pl.DeviceIdType.MESH

ROW_PW, ROW_SC, ROW_SINK, ROW_NPRE, ROW_NPOST, ROW_LOSS = 0, 512, 520, 528, 536, 544
PACK_ROWS = 576
PACK_SLICE = PACK_ROWS // N_DEV


def _nn(a, b):
    return jnp.dot(a, b, preferred_element_type=F32)


def _nt(a, b):
    return lax.dot_general(a, b, (((1,), (1,)), ((), ())), preferred_element_type=F32)


def _tn(a, b):
    return lax.dot_general(a, b, (((0,), (0,)), ((), ())), preferred_element_type=F32)


def _silu_parts(g):
    s = jax.nn.sigmoid(g)
    return g * s, s * (1.0 + g * (1.0 - s))


def _resident(shape):
    return pl.BlockSpec(shape, lambda *_: (0,) * len(shape), pipeline_mode=pl.Buffered(1))


def _compiler_params(sem=None):
    if sem is None:
        return pltpu.CompilerParams(vmem_limit_bytes=VMEM_LIMIT)
    return pltpu.CompilerParams(dimension_semantics=sem, vmem_limit_bytes=VMEM_LIMIT)


def _attn_bias():
    t = np.arange(BLOCK)[:, None]
    j = np.arange(2 * BLOCK)[None, :]
    dist = t + BLOCK - j
    in_win = (dist >= 0) & (dist < BLOCK)
    out = np.zeros((2, 2, BLOCK, GQA * 2 * BLOCK), np.float32)
    for variant in range(2):
        valid = in_win & ((j >= BLOCK) | (variant == 1))
        for kv in range(2):
            for g in range(GQA):
                slope = np.float32(2.0 ** (-(kv * GQA + g + 1)))
                b = np.where(valid, -slope * dist.astype(np.float32), np.float32(NEG_INF))
                out[variant, kv, :, g * 256:(g + 1) * 256] = b
    return out


def _block_diag(kx, kv):
    rolled = pltpu.roll(kx, 64, 1)
    lane = lax.broadcasted_iota(jnp.int32, kx.shape, 1)
    dup = jnp.where(lane < 64, kx, rolled) if kv == 0 else jnp.where(lane < 64, rolled, kx)
    rep = jnp.concatenate([dup, dup], axis=1).astype(BF16)
    lane2 = lax.broadcasted_iota(jnp.int32, rep.shape, 1)
    zero = jnp.zeros_like(rep)
    parts = [jnp.where((lane2 >= 64 * g) & (lane2 < 64 * g + 64), rep, zero) for g in range(GQA)]
    return jnp.concatenate(parts, axis=0)


def _diag_fold(m):
    lane = lax.broadcasted_iota(jnp.int32, (256, 256), 1)
    r = jnp.where(lane < 64, m[0:256], jnp.where(lane < 128, m[256:512], jnp.where(lane < 192, m[512:768], m[768:1024])))
    h = r[:, 0:128] + r[:, 128:256]
    return h + pltpu.roll(h, 64, 1)


def _window_sum(ext, w, forward):
    s = ext
    sh = 1
    while sh < w:
        s = s + pltpu.roll(s, (256 - sh) if forward else sh, 0)
        sh *= 2
    return s


def _inv_count(n, w):
    t = n * BLOCK + lax.broadcasted_iota(jnp.int32, (BLOCK, 1), 0) + 1
    return 1.0 / jnp.minimum(t.astype(F32), float(w))


def _kv_ext(ref, n):
    r0 = pl.multiple_of(jnp.maximum(n - 1, 0) * BLOCK, BLOCK)
    r1 = pl.multiple_of(n * BLOCK, BLOCK)
    return jnp.concatenate([ref[pl.ds(r0, BLOCK), :], ref[pl.ds(r1, BLOCK), :]], axis=0)


def _softmax_chunk(sg, sink):
    m = jnp.maximum(jnp.max(sg, axis=-1, keepdims=True), sink)
    p = jnp.exp(sg - m)
    esink = jnp.exp(sink - m)
    rl = 1.0 / (jnp.sum(p, axis=-1, keepdims=True) + esink)
    return p * rl, esink * rl


def _rows_of(vec_ref, pack_ref, row0):
    for r in range(D_MODEL // 128):
        pack_ref[row0 + r:row0 + r + 1, :] = vec_ref[:, 128 * r:128 * (r + 1)]


FRONT_TILE = 2 * BLOCK


def _fwd_front(layer, x, norm_pre, w_in_t, token, sinks, pool_w, pool_scale, bias):
    tm = FRONT_TILE

    def body(sink_ref, x_ref, g_ref, w_ref, _, pw_ref, sc_ref, bias_ref,
             u_ref, pg_ref, q_ref, k_ref, v_ref, ag_ref, z_ref, a_ref, uprev, kprev, vprev):
        i = pl.program_id(0)

        @pl.when(i == 0)
        def _():
            uprev[...] = jnp.zeros_like(uprev)
            kprev[...] = jnp.zeros_like(kprev)
            vprev[...] = jnp.zeros_like(vprev)

        xv = x_ref[...]
        r = lax.rsqrt(jnp.mean(xv * xv, axis=-1, keepdims=True) + EPS)
        h = (xv * r * g_ref[layer:layer + 1, :]).astype(BF16)
        u_ref[...] = _nt(h, w_ref[COL_U:COL_PG, :])
        pg_ref[...] = _nt(h, w_ref[COL_PG:COL_Q, :])
        q_ref[...] = _nt(h, w_ref[COL_Q:COL_K, :]).astype(BF16)
        k_ref[...] = _nt(h, w_ref[COL_K:COL_V, :])
        v_ref[...] = _nt(h, w_ref[COL_V:COL_AG, :])
        ag_ref[...] = _nt(h, w_ref[COL_AG:D_IN, :])

        for sb in range(tm // BLOCK):
            n = (tm // BLOCK) * i + sb
            rows = slice(BLOCK * sb, BLOCK * (sb + 1))
            before = slice(BLOCK * (sb - 1), BLOCK * sb)
            uv = u_ref[rows, :]
            ext = jnp.concatenate([uprev[...] if sb == 0 else u_ref[before, :], uv], axis=0)
            for g, w in enumerate(POOL_WINDOWS):
                cs = slice(BLOCK * g, BLOCK * (g + 1))
                win = _window_sum(ext[:, cs], w, forward=False)[BLOCK:]
                pooled = win * _inv_count(n, w) - uv[:, cs]
                mixed = _nn(pooled.astype(BF16), pw_ref[g].astype(BF16))
                gate, _ = _silu_parts(pg_ref[rows, cs])
                z_ref[rows, cs] = (mixed * sc_ref[layer:layer + 1, cs] * gate).astype(BF16)

            kx = jnp.concatenate([kprev[...] if sb == 0 else k_ref[before, :], k_ref[rows, :]], axis=0)
            vx = jnp.concatenate([vprev[...] if sb == 0 else v_ref[before, :], v_ref[rows, :]], axis=0)
            variant = jnp.minimum(n, 1) if sb == 0 else 1
            for kv in range(2):
                cs = slice(256 * kv, 256 * (kv + 1))
                bk = _block_diag(kx, kv)
                bv = _block_diag(vx, kv)
                s = _nt(q_ref[rows, cs], bk) * SCALE + bias_ref[variant, kv]
                ps = []
                for g in range(GQA):
                    p, _ = _softmax_chunk(s[:, 256 * g:256 * (g + 1)], sink_ref[layer, kv * GQA + g])
                    ps.append(p.astype(BF16))
                o = _nn(jnp.concatenate(ps, axis=1), bv)
                a_ref[rows, cs] = o
                gate, _ = _silu_parts(ag_ref[rows, cs])
                z_ref[rows, D_POOL + 256 * kv:D_POOL + 256 * (kv + 1)] = (o * gate).astype(BF16)

        tail = slice(tm - BLOCK, tm)
        uprev[...] = u_ref[tail, :]
        kprev[...] = k_ref[tail, :]
        vprev[...] = v_ref[tail, :]

    row = lambda c: pl.BlockSpec((tm, c), lambda i: (i, 0))
    const = lambda shape: pl.BlockSpec(shape, lambda i: (0,) * len(shape))
    return pl.pallas_call(
        body, name=f"fwd_front{layer}", grid=(SEQ // tm,),
        in_specs=[pl.BlockSpec(memory_space=pltpu.SMEM), row(D_MODEL), const((DEPTH, D_MODEL)),
                  _resident((D_IN, D_MODEL)), const((8, 128)),
                  pl.BlockSpec((None, 4, BLOCK, BLOCK), lambda i: (layer, 0, 0, 0)), const((DEPTH, D_POOL)),
                  _resident((2, 2, BLOCK, 1024))],
        out_specs=[row(D_POOL), row(D_POOL), row(D_ATTN), row(D_KV), row(D_KV), row(D_ATTN), row(D_MODEL),
                   row(D_ATTN)],
        out_shape=[jax.ShapeDtypeStruct((SEQ, D_POOL), F32), jax.ShapeDtypeStruct((SEQ, D_POOL), F32),
                   jax.ShapeDtypeStruct((SEQ, D_ATTN), BF16), jax.ShapeDtypeStruct((SEQ, D_KV), F32),
                   jax.ShapeDtypeStruct((SEQ, D_KV), F32), jax.ShapeDtypeStruct((SEQ, D_ATTN), F32),
                   jax.ShapeDtypeStruct((SEQ, D_MODEL), BF16), jax.ShapeDtypeStruct((SEQ, D_ATTN), F32)],
        scratch_shapes=[pltpu.VMEM((BLOCK, D_POOL), F32), pltpu.VMEM((BLOCK, D_KV), F32),
                        pltpu.VMEM((BLOCK, D_KV), F32)],
        compiler_params=_compiler_params(("arbitrary",)),
    )(sinks, x, norm_pre, w_in_t, token, pool_w, pool_scale, bias)


def _fwd_out(layer, z, x, norm_post, w_out):
    tm = TOKEN_TILE

    def body(z_ref, x_ref, g_ref, w_ref, xn_ref, y_ref):
        y = _nn(z_ref[...], w_ref[...])
        y_ref[...] = y
        r = lax.rsqrt(jnp.mean(y * y, axis=-1, keepdims=True) + EPS)
        xn_ref[...] = x_ref[...] + y * r * g_ref[layer:layer + 1, :]

    row = lambda c: pl.BlockSpec((tm, c), lambda i: (i, 0))
    return pl.pallas_call(
        body, name=f"fwd_out{layer}", grid=(SEQ // tm,),
        in_specs=[row(D_MODEL), row(D_MODEL), pl.BlockSpec((DEPTH, D_MODEL), lambda i: (0, 0)),
                  _resident((D_MODEL, D_MODEL))],
        out_specs=[row(D_MODEL), row(D_MODEL)],
        out_shape=[jax.ShapeDtypeStruct((SEQ, D_MODEL), F32), jax.ShapeDtypeStruct((SEQ, D_MODEL), F32)],
        compiler_params=_compiler_params(("arbitrary",)),
    )(z, x, norm_post, w_out)


def _bwd_out(layer, top, dxo_or_xf, target_or_token, y, z, norm_post, w_out):
    tm = TOKEN_TILE
    steps = SEQ // tm

    def body(*refs):
        if top:
            xf_ref, t_ref, y_ref, z_ref, g_ref, w_ref, dxo_ref, dz_ref, dw_ref, pack_ref, acc, dg, lacc = refs
        else:
            dxi_ref, _, y_ref, z_ref, g_ref, w_ref, dz_ref, dw_ref, pack_ref, acc, dg, lacc = refs
        i = pl.program_id(0)

        @pl.when(i == 0)
        def _():
            acc[...] = jnp.zeros_like(acc)
            dg[...] = jnp.zeros_like(dg)
            lacc[...] = jnp.zeros_like(lacc)
            pack_ref[...] = jnp.zeros_like(pack_ref)

        if top:
            d = xf_ref[...] - t_ref[...]
            dxo_v = d * (1.0 / D_MODEL)
            dxo_ref[...] = dxo_v
            part = jnp.sum(d * d, axis=-1, keepdims=True) * (1.0 / D_MODEL)
            lacc[...] += 0.5 * jnp.sum(part, axis=0, keepdims=True)
        else:
            dxo_v = dxi_ref[...]
        y = y_ref[...]
        r = lax.rsqrt(jnp.mean(y * y, axis=-1, keepdims=True) + EPS)
        yn = y * r
        dg[...] += jnp.sum(dxo_v * yn, axis=0, keepdims=True)
        dyn = dxo_v * g_ref[layer:layer + 1, :]
        dy = (r * (dyn - yn * jnp.mean(dyn * yn, axis=-1, keepdims=True))).astype(BF16)
        dz_ref[...] = _nt(dy, w_ref[...])
        acc[...] += _tn(z_ref[...], dy)

        @pl.when(i == steps - 1)
        def _():
            dw_ref[...] = acc[...].astype(BF16)
            _rows_of(dg, pack_ref, ROW_NPOST)
            lane = lax.broadcasted_iota(jnp.int32, (1, 128), 1)
            pack_ref[ROW_LOSS:ROW_LOSS + 1, :] = jnp.where(lane == 0, lacc[...], 0.0)

    row = lambda c: pl.BlockSpec((tm, c), lambda i: (i, 0))
    const = lambda shape: pl.BlockSpec(shape, lambda i: (0,) * len(shape))
    act = jax.ShapeDtypeStruct((SEQ, D_MODEL), F32)
    return pl.pallas_call(
        body, name=f"bwd_out{layer}", grid=(steps,),
        in_specs=([row(D_MODEL), row(D_MODEL) if top else const((8, 128))]
                  + [row(D_MODEL), row(D_MODEL), const((DEPTH, D_MODEL)), _resident((D_MODEL, D_MODEL))]),
        out_specs=([row(D_MODEL)] * (2 if top else 1) + [const((D_MODEL, D_MODEL)), const((PACK_ROWS, 128))]),
        out_shape=([act] * (2 if top else 1)
                   + [jax.ShapeDtypeStruct((D_MODEL, D_MODEL), BF16), jax.ShapeDtypeStruct((PACK_ROWS, 128), F32)]),
        scratch_shapes=[pltpu.VMEM((D_MODEL, D_MODEL), F32), pltpu.VMEM((1, D_MODEL), F32), pltpu.VMEM((1, 1), F32)],
        compiler_params=_compiler_params(("arbitrary",)),
    )(dxo_or_xf, target_or_token, y, z, norm_post, w_out)


def _bwd_mix(layer, sinks, dz, u, pg, q, k, v, ag, a, pool_w, pool_scale, bias, pack):
    last = N_BLOCKS - 1

    def body(sink_ref, dz_ref, u_ref, up_ref, pg_ref, q_ref, k_ref, v_ref, ag_ref, a_ref, pw_ref, sc_ref,
             bias_ref, pin_ref, dp_ref, pack_ref, ck, cv, ce):
        i = pl.program_id(0)
        n = last - i

        @pl.when(i == 0)
        def _():
            ck[...] = jnp.zeros_like(ck)
            cv[...] = jnp.zeros_like(cv)
            ce[...] = jnp.zeros_like(ce)
            pack_ref[...] = pin_ref[...]

        uv = u_ref[...]
        has_prev = (n > 0).astype(F32)
        ext = jnp.concatenate([up_ref[...] * has_prev, uv], axis=0)
        for g, w in enumerate(POOL_WINDOWS):
            cs = slice(BLOCK * g, BLOCK * (g + 1))
            inv = _inv_count(n, w)
            win = _window_sum(ext[:, cs], w, forward=False)[BLOCK:]
            pooled = win * inv - uv[:, cs]
            pw_g = pw_ref[g].astype(BF16)
            mixed = _nn(pooled.astype(BF16), pw_g)
            gate, dgate = _silu_parts(pg_ref[:, cs])
            dzp = dz_ref[:, cs]
            sc = sc_ref[layer:layer + 1, cs]
            dpm = dzp * gate
            dp_ref[:, COL_PG + BLOCK * g:COL_PG + BLOCK * (g + 1)] = (dzp * (mixed * sc) * dgate).astype(BF16)
            pack_ref[ROW_SC + g:ROW_SC + g + 1, :] += jnp.sum(dpm * mixed, axis=0, keepdims=True)
            dmixed = (dpm * sc).astype(BF16)
            pack_ref[ROW_PW + BLOCK * g:ROW_PW + BLOCK * (g + 1), :] += _tn(pooled.astype(BF16), dmixed)
            dpooled = _nt(dmixed, pw_g)
            e = dpooled * inv
            lead = _window_sum(jnp.concatenate([e, ce[:, cs]], axis=0), w, forward=True)[:BLOCK]
            dp_ref[:, COL_U + BLOCK * g:COL_U + BLOCK * (g + 1)] = (lead - dpooled).astype(BF16)
            ce[:, cs] = e

        kx = _kv_ext(k_ref, n)
        vx = _kv_ext(v_ref, n)
        lane = lax.broadcasted_iota(jnp.int32, (1, 128), 1)
        dsink_row = jnp.zeros((1, 128), F32)
        tks, tvs = [], []
        for kv in range(2):
            cs = slice(256 * kv, 256 * (kv + 1))
            bk = _block_diag(kx, kv)
            bv = _block_diag(vx, kv)
            qv = q_ref[:, cs]
            s = _nt(qv, bk) * SCALE + bias_ref[0, kv]
            gate, dgate = _silu_parts(ag_ref[:, cs])
            dza = dz_ref[:, D_POOL + 256 * kv:D_POOL + 256 * (kv + 1)]
            dp_ref[:, COL_AG + 256 * kv:COL_AG + 256 * (kv + 1)] = (dza * a_ref[:, cs] * dgate).astype(BF16)
            da = dza * gate
            dab = da.astype(BF16)
            dpr = _nt(dab, bv)
            ps, dss = [], []
            for g in range(GQA):
                gs = slice(256 * g, 256 * (g + 1))
                p, psink = _softmax_chunk(s[:, gs], sink_ref[layer, kv * GQA + g])
                dpg_ = dpr[:, gs]
                delta = jnp.sum(p * dpg_, axis=-1, keepdims=True)
                dsink = -jnp.sum(psink * delta, axis=0, keepdims=True)
                dsink_row = dsink_row + jnp.where(lane == kv * GQA + g, dsink, 0.0)
                ps.append(p)
                dss.append(p * (dpg_ - delta) * SCALE)
            p_all = jnp.concatenate(ps, axis=1)
            ds_all = jnp.concatenate(dss, axis=1)
            dp_ref[:, COL_Q + 256 * kv:COL_Q + 256 * (kv + 1)] = _nn(ds_all.astype(BF16), bk).astype(BF16)
            tks.append(_diag_fold(_tn(ds_all.astype(BF16), qv)))
            tvs.append(_diag_fold(_tn(p_all.astype(BF16), dab)))
        pack_ref[ROW_SINK:ROW_SINK + 1, :] += dsink_row
        lane2 = lax.broadcasted_iota(jnp.int32, (256, 128), 1)
        dkx = jnp.where(lane2 < 64, tks[0], tks[1])
        dvx = jnp.where(lane2 < 64, tvs[0], tvs[1])
        dp_ref[:, COL_K:COL_V] = (ck[...] + dkx[BLOCK:]).astype(BF16)
        dp_ref[:, COL_V:COL_AG] = (cv[...] + dvx[BLOCK:]).astype(BF16)
        ck[...] = dkx[:BLOCK]
        cv[...] = dvx[:BLOCK]

    blk = lambda c: pl.BlockSpec((BLOCK, c), lambda i: (last - i, 0))
    full = lambda shape: pl.BlockSpec(shape, lambda i: (0,) * len(shape))
    return pl.pallas_call(
        body, name=f"bwd_mix{layer}", grid=(N_BLOCKS,),
        in_specs=[pl.BlockSpec(memory_space=pltpu.SMEM), blk(D_MODEL), blk(D_POOL),
                  pl.BlockSpec((BLOCK, D_POOL), lambda i: (jnp.maximum(last - i - 1, 0), 0)),
                  blk(D_POOL), blk(D_ATTN), full((SEQ, D_KV)), full((SEQ, D_KV)), blk(D_ATTN), blk(D_ATTN),
                  pl.BlockSpec((None, 4, BLOCK, BLOCK), lambda i: (layer, 0, 0, 0)), full((DEPTH, D_POOL)),
                  pl.BlockSpec((1, 2, BLOCK, 1024), lambda i: (jnp.minimum(last - i, 1), 0, 0, 0)),
                  full((PACK_ROWS, 128))],
        out_specs=[blk(D_IN), full((PACK_ROWS, 128))],
        out_shape=[jax.ShapeDtypeStruct((SEQ, D_IN), BF16), jax.ShapeDtypeStruct((PACK_ROWS, 128), F32)],
        scratch_shapes=[pltpu.VMEM((BLOCK, D_KV), F32), pltpu.VMEM((BLOCK, D_KV), F32),
                        pltpu.VMEM((BLOCK, D_POOL), F32)],
        input_output_aliases={13: 1},
        compiler_params=_compiler_params(("arbitrary",)),
    )(sinks, dz, u, u, pg, q, k, v, ag, a, pool_w, pool_scale, bias, pack)


def _bwd_in(layer, dproj, x, dxo, norm_pre, w_in_t, pack):
    tm = TOKEN_TILE
    steps = SEQ // tm
    cw = 256

    def body(dp_ref, x_ref, dxo_ref, g_ref, w_ref, pin_ref, dx_ref, dw_ref, pack_ref, acc, dg):
        i = pl.program_id(0)

        @pl.when(i == 0)
        def _():
            acc[...] = jnp.zeros_like(acc)
            dg[...] = jnp.zeros_like(dg)
            pack_ref[...] = pin_ref[...]

        xv = x_ref[...]
        gv = g_ref[layer:layer + 1, :]
        r = lax.rsqrt(jnp.mean(xv * xv, axis=-1, keepdims=True) + EPS)
        xn = xv * r
        hb = (xn * gv).astype(BF16)
        dh = _nn(dp_ref[...], w_ref[...])
        for c in range(0, D_IN, cw):
            acc[c:c + cw, :] += _tn(dp_ref[:, c:c + cw], hb)
        dg[...] += jnp.sum(dh * xn, axis=0, keepdims=True)
        dhn = dh * gv
        dx_ref[...] = dxo_ref[...] + r * (dhn - xn * jnp.mean(dhn * xn, axis=-1, keepdims=True))

        @pl.when(i == steps - 1)
        def _():
            dw_ref[...] = acc[...].astype(BF16)
            _rows_of(dg, pack_ref, ROW_NPRE)

    row = lambda c: pl.BlockSpec((tm, c), lambda i: (i, 0))
    const = lambda shape: pl.BlockSpec(shape, lambda i: (0,) * len(shape))
    return pl.pallas_call(
        body, name=f"bwd_in{layer}", grid=(steps,),
        in_specs=[row(D_IN), row(D_MODEL), row(D_MODEL), const((DEPTH, D_MODEL)), _resident((D_IN, D_MODEL)),
                  const((PACK_ROWS, 128))],
        out_specs=[row(D_MODEL), const((D_IN, D_MODEL)), const((PACK_ROWS, 128))],
        out_shape=[jax.ShapeDtypeStruct((SEQ, D_MODEL), F32), jax.ShapeDtypeStruct((D_IN, D_MODEL), BF16),
                   jax.ShapeDtypeStruct((PACK_ROWS, 128), F32)],
        scratch_shapes=[pltpu.VMEM((D_IN, D_MODEL), F32), pltpu.VMEM((1, D_MODEL), F32)],
        input_output_aliases={5: 2},
        compiler_params=_compiler_params(("arbitrary",)),
    )(dproj, x, dxo, norm_pre, w_in_t, pack)


def _mesh_pos():
    return lax.axis_index("x"), lax.axis_index("y"), lax.axis_index("c")


def _device_rows(ref, m, px, py, pc):
    return ref.at[pl.ds(pl.multiple_of((4 * px + 2 * py + pc) * m, 16 if m % 16 == 0 else 8), m), :]


def _allgather(srcs, out_dtype, name, later=()):
    na, nb = len(srcs), len(later)
    every = list(srcs) + list(later)
    shapes = [(a.shape[-2], a.shape[-1]) for a, _ in every]

    def body(*refs):
        xs, refs = refs[:na + nb], refs[na + nb:]
        outs, cast, land, refs = refs[:na], refs[na:na + nb], refs[na + nb:na + 2 * nb], refs[na + 2 * nb:]
        stage, (send_sems, recv_sems, local_sems) = refs[:na], refs[na:]
        x, y, c = _mesh_pos()
        me, sibling = (x, y, c), (x, y, 1 - c)
        chips = [(1 - x, y), (x, 1 - y), (1 - x, 1 - y)]

        def slot(a, px, py, pc):
            return _device_rows(outs[a], shapes[a][0], px, py, pc)

        def copy(a, k, block, to, src=None):
            return pltpu.make_async_remote_copy(
                src_ref=slot(a, *block) if src is None else src, dst_ref=slot(a, *block),
                send_sem=send_sems.at[a, k], recv_sem=recv_sems.at[a, k], device_id=to, device_id_type=MESH)

        def cast_block(i):
            layer = every[i][1]
            return (xs[i][...] if layer is None else xs[i][layer]).astype(out_dtype)

        for a in range(na):
            stage[a][...] = cast_block(a)
        mine = [pltpu.make_async_copy(stage[a], slot(a, *me), local_sems.at[a]) for a in range(na)]
        for cp in mine:
            cp.start()
        first = []
        for a in range(na):
            first.append(copy(a, 0, me, sibling, src=stage[a]))
            first += [copy(a, 1 + j, me, (*chip, c), src=stage[a]) for j, chip in enumerate(chips)]
        for cp in first:
            cp.start()
        for b in range(nb):
            cast[b][...] = cast_block(na + b)
            cp = pltpu.make_async_copy(cast[b], _device_rows(land[b], shapes[na + b][0], *me), local_sems.at[na + b])
            cp.start()
            mine.append(cp)
        passed = []
        for a in range(na):
            for j, chip in enumerate(chips):
                copy(a, 1 + j, (*chip, c), me).wait_recv()
                cp = copy(a, 4 + j, (*chip, c), sibling)
                cp.start()
                passed.append(cp)
        for a in range(na):
            copy(a, 0, sibling, me).wait_recv()
            for j, chip in enumerate(chips):
                copy(a, 4 + j, (*chip, 1 - c), me).wait_recv()
        for cp in first + passed:
            cp.wait_send()
        for cp in mine:
            cp.wait()

    vmem = pl.BlockSpec(memory_space=pltpu.VMEM)
    gathered = [jax.ShapeDtypeStruct((N_DEV * m, n), out_dtype) for m, n in shapes]
    out = pl.pallas_call(
        body, name=name,
        in_specs=[vmem] * (na + nb),
        out_specs=[vmem] * (na + nb) + [pl.BlockSpec(memory_space=pl.ANY)] * nb,
        out_shape=gathered[:na] + [jax.ShapeDtypeStruct(s, out_dtype) for s in shapes[na:]] + gathered[na:],
        scratch_shapes=([pltpu.VMEM(s, out_dtype) for s in shapes[:na]]
                        + [pltpu.SemaphoreType.DMA((na, 7)), pltpu.SemaphoreType.DMA((na, 7)),
                           pltpu.SemaphoreType.DMA((na + nb,))]),
        compiler_params=_compiler_params(),
    )(*[a for a, _ in every])
    return out[:na], out[na:na + nb], out[na + nb:]


def _gather_start(blocks, lands, name):
    na = len(blocks)

    def body(*refs):
        src, land, sems, token = refs[:na], refs[na:2 * na], refs[2 * na:4 * na], refs[-1]
        x, y, c = _mesh_pos()
        for a in range(na):
            for k in range(1, N_DEV):
                peer = (x ^ ((k >> 2) & 1), y ^ ((k >> 1) & 1), c ^ (k & 1))
                pltpu.make_async_remote_copy(
                    src_ref=src[a], dst_ref=_device_rows(land[a], blocks[a].shape[0], x, y, c),
                    send_sem=sems[2 * a].at[k - 1], recv_sem=sems[2 * a + 1].at[k - 1],
                    device_id=peer, device_id_type=MESH).start()
        token[...] = jnp.zeros_like(token)

    bufs = [pltpu.HBM(t.shape, t.dtype) for t in list(blocks) + list(lands)]
    out = pl.pallas_call(
        body, name=name,
        out_shape=(*([pltpu.SemaphoreType.DMA((N_DEV - 1,))] * (2 * na)), *bufs, jax.ShapeDtypeStruct((8, 128), F32)),
        in_specs=[_HBM] * (2 * na),
        out_specs=(*([_SEM] * (2 * na)), *([_HBM] * (2 * na)), pl.BlockSpec(memory_space=pltpu.VMEM)),
        input_output_aliases={i: 2 * na + i for i in range(2 * na)},
        compiler_params=pltpu.CompilerParams(has_side_effects=_EFFECT),
    )(*[pltpu.with_memory_space_constraint(t, pltpu.HBM) for t in list(blocks) + list(lands)])
    sems = [(out[2 * a], out[2 * a + 1]) for a in range(na)]
    return sems, out[2 * na:3 * na], out[3 * na:4 * na], out[-1]


def _gather_wait(sems, block, land, after, name):
    def body(src, land_ref, send_sem, recv_sem, after_ref, src_out, land_out):
        x, y, c = _mesh_pos()
        for k in range(1, N_DEV):
            peer = (x ^ ((k >> 2) & 1), y ^ ((k >> 1) & 1), c ^ (k & 1))
            cp = pltpu.make_async_remote_copy(
                src_ref=src, dst_ref=_device_rows(land_ref, block.shape[0], *peer),
                send_sem=send_sem.at[k - 1], recv_sem=recv_sem.at[k - 1], device_id=peer, device_id_type=MESH)
            cp.wait_send()
            cp.wait_recv()

    out = pl.pallas_call(
        body, name=name,
        out_shape=(pltpu.HBM(block.shape, block.dtype), pltpu.HBM(land.shape, land.dtype)),
        in_specs=[_HBM, _HBM, _SEM, _SEM, pl.BlockSpec(memory_space=pl.ANY)],
        out_specs=[_HBM, _HBM],
        input_output_aliases={0: 0, 1: 1},
        compiler_params=pltpu.CompilerParams(has_side_effects=_EFFECT),
    )(block, land, sems[0], sems[1], after)
    return out[1]


def _reduce_scatter(arrs, name):
    na = len(arrs)
    row_chunk = 32

    def body(*refs):
        gs, outs = refs[:na], refs[na:2 * na]
        own, ra, hb, rb = (refs[(2 + t) * na:(3 + t) * na] for t in range(4))
        d2d_send, d2d_recv, ici_send, ici_recv, local_sems = refs[6 * na:]
        x, y, c = _mesh_pos()
        sibling = (x, y, 1 - c)
        loads, sends = [], []
        for a in range(na):
            for q in range(4):
                blk_mine = 4 * (q >> 1) + 2 * (q & 1) + c
                blk_sib = 4 * (q >> 1) + 2 * (q & 1) + (1 - c)
                cp = pltpu.make_async_copy(gs[a].at[blk_mine], own[a].at[q], local_sems.at[a, q])
                cp.start()
                loads.append(cp)
                cp = pltpu.make_async_remote_copy(
                    src_ref=gs[a].at[blk_sib], dst_ref=ra[a].at[q], send_sem=d2d_send.at[a, q],
                    recv_sem=d2d_recv.at[a, q], device_id=sibling, device_id_type=MESH)
                cp.start()
                sends.append(cp)
        for cp in loads:
            cp.wait()
        for cp in sends:
            cp.wait_recv()
        others = [(1 - x, y), (x, 1 - y), (1 - x, 1 - y)]

        def chunks(a):
            m = arrs[a].shape[1]
            step = next(s for s in (row_chunk, 24, 16, 8) if m % s == 0)
            return m // step, step

        def pair_sum(a, q, rs):
            return own[a][q, rs, :].astype(F32) + ra[a][q, rs, :].astype(F32)

        ici = []
        for j, (qx, qy) in enumerate(others):
            q = 2 * qx + qy
            for a in range(na):
                n_chunks, step = chunks(a)

                def to_send(i, carry, a=a, q=q, j=j, step=step):
                    rs = pl.ds(pl.multiple_of(i * step, step), step)
                    hb[a][j, rs, :] = pair_sum(a, q, rs).astype(hb[a].dtype)
                    return carry

                lax.fori_loop(0, n_chunks, to_send, 0)
                cp = pltpu.make_async_remote_copy(
                    src_ref=hb[a].at[j], dst_ref=rb[a].at[j], send_sem=ici_send.at[a, j],
                    recv_sem=ici_recv.at[a, j], device_id=(qx, qy, c), device_id_type=MESH)
                cp.start()
                ici.append(cp)
        for a in range(na):
            n_chunks, step = chunks(a)

            def mine(i, carry, a=a, step=step):
                rs = pl.ds(pl.multiple_of(i * step, step), step)
                outs[a][rs, :] = pair_sum(a, 2 * x + y, rs)
                return carry

            lax.fori_loop(0, n_chunks, mine, 0)
        for cp in ici:
            cp.wait_recv()
        for a in range(na):
            n_chunks, step = chunks(a)

            def total(i, carry, a=a, step=step):
                rs = pl.ds(pl.multiple_of(i * step, step), step)
                acc = outs[a][rs, :]
                for j in range(3):
                    acc = acc + rb[a][j, rs, :].astype(F32)
                outs[a][rs, :] = acc
                return carry

            lax.fori_loop(0, n_chunks, total, 0)
        for cp in sends + ici:
            cp.wait_send()

    vmem = pl.BlockSpec(memory_space=pltpu.VMEM)
    scratch = []
    for count in (4, 4, 3, 3):
        scratch += [pltpu.VMEM((count,) + t.shape[1:], t.dtype) for t in arrs]
    scratch += [pltpu.SemaphoreType.DMA((na, 4)), pltpu.SemaphoreType.DMA((na, 4)),
                pltpu.SemaphoreType.DMA((na, 3)), pltpu.SemaphoreType.DMA((na, 3)),
                pltpu.SemaphoreType.DMA((na, 4))]
    return pl.pallas_call(
        body, name=name,
        in_specs=[pl.BlockSpec(memory_space=pl.ANY)] * na, out_specs=[vmem] * na,
        out_shape=[jax.ShapeDtypeStruct(t.shape[1:], F32) for t in arrs],
        scratch_shapes=scratch,
        compiler_params=_compiler_params(),
    )(*arrs)


def _row_step(m):
    return next(s for s in (32, 24, 16, 8) if m % s == 0)


def _pair_reduce(arrs, name):
    na = len(arrs)

    def body(*refs):
        gs, hs, hm = refs[:na], refs[na:2 * na], refs[2 * na:3 * na]
        own, ra = refs[3 * na:4 * na], refs[4 * na:5 * na]
        d2d_send, d2d_recv, local_sems = refs[5 * na:]
        x, y, c = _mesh_pos()
        sibling = (x, y, 1 - c)
        loads, sends = [], []
        for a in range(na):
            for q in range(4):
                cp = pltpu.make_async_copy(gs[a].at[2 * q + c], own[a].at[q], local_sems.at[a, q])
                cp.start()
                loads.append(cp)
                cp = pltpu.make_async_remote_copy(
                    src_ref=gs[a].at[2 * q + (1 - c)], dst_ref=ra[a].at[q], send_sem=d2d_send.at[a, q],
                    recv_sem=d2d_recv.at[a, q], device_id=sibling, device_id_type=MESH)
                cp.start()
                sends.append(cp)
        for cp in loads:
            cp.wait()
        for cp in sends:
            cp.wait_recv()
        others = [2 * (1 - x) + y, 2 * x + (1 - y), 2 * (1 - x) + (1 - y)]
        for a in range(na):
            m = arrs[a].shape[1]
            step = _row_step(m)

            def add(i, carry, a=a, step=step):
                rs = pl.ds(pl.multiple_of(i * step, step), step)
                for j, q in enumerate(others):
                    hs[a][j, rs, :] = (own[a][q, rs, :].astype(F32) + ra[a][q, rs, :].astype(F32)).astype(hs[a].dtype)
                q = 2 * x + y
                hm[a][rs, :] = own[a][q, rs, :].astype(F32) + ra[a][q, rs, :].astype(F32)
                return carry

            lax.fori_loop(0, m // step, add, 0)
        for cp in sends:
            cp.wait_send()

    vmem = pl.BlockSpec(memory_space=pltpu.VMEM)
    scratch = [pltpu.VMEM((4,) + t.shape[1:], t.dtype) for t in arrs] * 2
    scratch += [pltpu.SemaphoreType.DMA((na, 4)), pltpu.SemaphoreType.DMA((na, 4)), pltpu.SemaphoreType.DMA((na, 4))]
    out = pl.pallas_call(
        body, name=name,
        in_specs=[pl.BlockSpec(memory_space=pl.ANY)] * na, out_specs=[vmem] * (2 * na),
        out_shape=([jax.ShapeDtypeStruct((3,) + t.shape[1:], t.dtype) for t in arrs]
                   + [jax.ShapeDtypeStruct(t.shape[1:], F32) for t in arrs]),
        scratch_shapes=scratch,
        compiler_params=_compiler_params(),
    )(*arrs)
    return out[:na], out[na:]


_HBM = pl.BlockSpec(memory_space=pltpu.HBM)
_SEM = pl.BlockSpec(memory_space=pltpu.SEMAPHORE)
_EFFECT = pltpu.SideEffectType.DATAFLOW_SIDE_EFFECTING


def _chip_exchange_start(srcs, name):
    na = len(srcs)

    def body(*refs):
        src, land = refs[:na], refs[na:2 * na]
        send_sem, recv_sem = refs[2 * na], refs[2 * na + 1]
        token = refs[-1]
        x, y, c = _mesh_pos()
        for j, (qx, qy) in enumerate([(1 - x, y), (x, 1 - y), (1 - x, 1 - y)]):
            for a in range(na):
                pltpu.make_async_remote_copy(
                    src_ref=src[a].at[j], dst_ref=land[a].at[j], send_sem=send_sem.at[3 * a + j],
                    recv_sem=recv_sem.at[3 * a + j], device_id=(qx, qy, c), device_id_type=MESH).start()
        token[...] = jnp.zeros_like(token)

    bufs = [pltpu.HBM(t.shape, t.dtype) for t in srcs]
    out = pl.pallas_call(
        body, name=name,
        out_shape=(pltpu.SemaphoreType.DMA((3 * na,)), pltpu.SemaphoreType.DMA((3 * na,)), *bufs, *bufs,
                   jax.ShapeDtypeStruct((8, 128), F32)),
        in_specs=[_HBM] * (2 * na),
        out_specs=(_SEM, _SEM, *([_HBM] * (2 * na)), pl.BlockSpec(memory_space=pltpu.VMEM)),
        input_output_aliases={i: 2 + i for i in range(2 * na)},
        compiler_params=pltpu.CompilerParams(has_side_effects=_EFFECT),
    )(*[pltpu.with_memory_space_constraint(t, pltpu.HBM) for t in srcs],
      *[pltpu.with_memory_space_constraint(lax.empty(t.shape, t.dtype), pltpu.HBM) for t in srcs])
    return out[0], out[1], out[2:2 + na], out[2 + na:2 + 2 * na], out[-1]


def _chip_exchange_wait(send_sem, recv_sem, srcs, lands, after, name):
    na = len(srcs)

    def body(*refs):
        src, land = refs[:na], refs[na:2 * na]
        send_sem_ref, recv_sem_ref = refs[2 * na], refs[2 * na + 1]
        x, y, c = _mesh_pos()
        for j, (qx, qy) in enumerate([(1 - x, y), (x, 1 - y), (1 - x, 1 - y)]):
            for a in range(na):
                cp = pltpu.make_async_remote_copy(
                    src_ref=src[a].at[j], dst_ref=land[a].at[j], send_sem=send_sem_ref.at[3 * a + j],
                    recv_sem=recv_sem_ref.at[3 * a + j], device_id=(qx, qy, c), device_id_type=MESH)
                cp.wait_send()
                cp.wait_recv()

    bufs = [pltpu.HBM(t.shape, t.dtype) for t in srcs]
    out = pl.pallas_call(
        body, name=name,
        out_shape=(*bufs, *bufs),
        in_specs=[_HBM] * (2 * na) + [_SEM, _SEM, pl.BlockSpec(memory_space=pl.ANY)],
        out_specs=[_HBM] * (2 * na),
        input_output_aliases={i: i for i in range(2 * na)},
        compiler_params=pltpu.CompilerParams(has_side_effects=_EFFECT),
    )(*srcs, *lands, send_sem, recv_sem, after)
    return out[na:]


def _sum_chip_sums(mine, lands, name):
    na = len(mine)

    def body(*refs):
        hm, ld, outs = refs[:na], refs[na:2 * na], refs[2 * na:]
        for a in range(na):
            m = mine[a].shape[0]
            step = _row_step(m)

            def total(i, carry, a=a, step=step):
                rs = pl.ds(pl.multiple_of(i * step, step), step)
                acc = hm[a][rs, :]
                for j in range(3):
                    acc = acc + ld[a][j, rs, :].astype(F32)
                outs[a][rs, :] = acc
                return carry

            lax.fori_loop(0, m // step, total, 0)

    return pl.pallas_call(
        body, name=name,
        out_shape=[jax.ShapeDtypeStruct(t.shape, F32) for t in mine],
        compiler_params=_compiler_params(),
    )(*mine, *lands)


def _adamw_math(w, g, m, v):
    m = ADAM_B1 * m + (1.0 - ADAM_B1) * g
    v = ADAM_B2 * v + (1.0 - ADAM_B2) * (g * g)
    m_hat = m / (1.0 - ADAM_B1 ** ADAM_STEP)
    v_hat = v / (1.0 - ADAM_B2 ** ADAM_STEP)
    delta = -ADAM_LR * (m_hat / (jnp.sqrt(v_hat) + ADAM_EPS) + ADAM_WD * w)
    return delta, m, v


def _adamw_big(g0, g1, w, m, v, name, rows):
    _, mm, nn = w.shape

    def body(g0_ref, g1_ref, w_ref, m_ref, v_ref, g_ref, d_ref, nm_ref, nv_ref):
        g = jnp.where(pl.program_id(0) == 0, g0_ref[...], g1_ref[...])
        g_ref[...] = g
        d, nm, nv = _adamw_math(w_ref[...], g, m_ref[...], v_ref[...])
        d_ref[...] = d
        nm_ref[...] = nm
        nv_ref[...] = nv

    gspec = pl.BlockSpec((rows, nn), lambda l, i: (i, 0))
    spec = pl.BlockSpec((None, rows, nn), lambda l, i: (l, i, 0))
    return pl.pallas_call(
        body, name=name, grid=(DEPTH, mm // rows),
        in_specs=[gspec, gspec, spec, spec, spec], out_specs=[spec] * 4,
        out_shape=[jax.ShapeDtypeStruct(w.shape, F32)] * 4,
        compiler_params=_compiler_params(("arbitrary", "arbitrary")),
    )(g0, g1, w, m, v)


def _adamw_small(g0, g1, params):
    def body(g0_ref, g1_ref, *refs):
        ins, outs = refs[:15], refs[15:]
        loss_ref = outs[0]
        packs = (g0_ref, g1_ref)
        loss_ref[...] = g1_ref[ROW_LOSS:ROW_LOSS + 1, 0:1]

        def update(p, sel, g):
            w_ref, m_ref, v_ref = ins[p], ins[5 + p], ins[10 + p]
            d, nm, nv = _adamw_math(w_ref[sel], g, m_ref[sel], v_ref[sel])
            for t, val in enumerate((g, d, nm, nv)):
                outs[1 + 5 * t + p][sel] = val

        for l in range(DEPTH):
            gp = packs[l]
            for grp in range(4):
                update(0, (l, grp), gp[ROW_PW + BLOCK * grp:ROW_PW + BLOCK * (grp + 1), :])
                update(1, (slice(l, l + 1), slice(128 * grp, 128 * (grp + 1))), gp[ROW_SC + grp:ROW_SC + grp + 1, :])
            update(2, (slice(l, l + 1), slice(None)), gp[ROW_SINK:ROW_SINK + 1, 0:N_HEADS])
            for r in range(D_MODEL // 128):
                sel = (slice(l, l + 1), slice(128 * r, 128 * (r + 1)))
                update(3, sel, gp[ROW_NPRE + r:ROW_NPRE + r + 1, :])
                update(4, sel, gp[ROW_NPOST + r:ROW_NPOST + r + 1, :])

    shapes = [jax.ShapeDtypeStruct(p.shape, F32) for p in params[:5]]
    return pl.pallas_call(
        body, name="adamw_small",
        out_shape=[jax.ShapeDtypeStruct((1, 1), F32)] + shapes * 4,
        compiler_params=_compiler_params(),
    )(g0, g1, *params)


def kernel(x, w_in, pool_w, pool_scale, attn_sinks, w_out, norm_pre, norm_post, loss_target, m_w_in, m_pool_w, m_pool_scale, m_attn_sinks, m_w_out, m_norm_pre, m_norm_post, v_w_in, v_pool_w, v_pool_scale, v_attn_sinks, v_w_out, v_norm_pre, v_norm_post):
    x0 = x.reshape(SEQ, D_MODEL)
    target = loss_target.reshape(SEQ, D_MODEL)
    bias = jnp.asarray(_attn_bias())
    w_in_t, m_in_t, v_in_t = (jnp.swapaxes(t, 1, 2) for t in (w_in, m_w_in, v_w_in))

    (win0, wout0), later, lands = _allgather([(w_in_t, 0), (w_out, 0)], BF16, "gather_w0",
                                              later=[(w_in_t, 1), (w_out, 1)])
    sems, later, lands, token = _gather_start(later, lands, "gather_w1_start")
    win_full, wout_full = [win0, None], [wout0, None]

    saved = []
    xl = x0
    for layer in range(DEPTH):
        if layer > 0:
            win_full[layer] = _gather_wait(sems[0], later[0], lands[0], xl, "gather_w_in1_wait")
        u, pg, q, k, v, ag, z, a = _fwd_front(layer, xl, norm_pre, win_full[layer], token, attn_sinks,
                                             pool_w, pool_scale, bias)
        if layer > 0:
            wout_full[layer] = _gather_wait(sems[1], later[1], lands[1], z, "gather_w_out1_wait")
        x_next, y = _fwd_out(layer, z, xl, norm_post, wout_full[layer])
        saved.append((xl, u, pg, q, k, v, ag, z, a, y))
        xl = x_next

    reduced = [None] * DEPTH
    dx, token, pending = None, None, None
    for layer in reversed(range(DEPTH)):
        xin, u, pg, q, k, v, ag, z, a, y = saved[layer]
        if layer == DEPTH - 1:
            dx, dz, gw_out, pack = _bwd_out(layer, True, xl, target, y, z, norm_post, wout_full[layer])
        else:
            dz, gw_out, pack = _bwd_out(layer, False, dx, token, y, z, norm_post, wout_full[layer])
        dproj, pack = _bwd_mix(layer, attn_sinks, dz, u, pg, q, k, v, ag, a, pool_w, pool_scale, bias, pack)
        dx, gw_in_t, pack = _bwd_in(layer, dproj, xin, dx, norm_pre, win_full[layer], pack)
        blocks = [gw_in_t.reshape(N_DEV, IN_SHARD, D_MODEL), gw_out.reshape(N_DEV, OUT_SHARD, D_MODEL),
                  pack.reshape(N_DEV, PACK_SLICE, 128)]
        if layer == DEPTH - 1:
            chip_sums, mine = _pair_reduce(blocks, "pair_reduce1")
            send_sem, recv_sem, srcs, lands, token = _chip_exchange_start(chip_sums, "chip_exchange_start1")
            pending = (send_sem, recv_sem, srcs, lands, mine)
        else:
            reduced[layer] = _reduce_scatter(blocks, f"scatter_grads{layer}")
    send_sem, recv_sem, srcs, lands, mine = pending
    lands = _chip_exchange_wait(send_sem, recv_sem, srcs, lands, dx, "chip_exchange_wait1")
    reduced[DEPTH - 1] = _sum_chip_sums(mine, lands, "sum_chip_sums1")

    (small0, small1), _, _ = _allgather([(reduced[0][2], None), (reduced[1][2], None)], F32, "gather_small")
    small_out = _adamw_small(small0, small1, [
        pool_w, pool_scale, attn_sinks, norm_pre, norm_post,
        m_pool_w, m_pool_scale, m_attn_sinks, m_norm_pre, m_norm_post,
        v_pool_w, v_pool_scale, v_attn_sinks, v_norm_pre, v_norm_post])
    loss = small_out[0].reshape(())
    big_in = _adamw_big(reduced[0][0], reduced[1][0], w_in_t, m_in_t, v_in_t, "adamw_in", 96)
    big_out = _adamw_big(reduced[0][1], reduced[1][1], w_out, m_w_out, v_w_out, "adamw_out", 128)

    outs = [loss, dx.reshape(1, SEQ, D_MODEL)]
    for t in range(4):
        pw_, sc_, sk_, npre_, npost_ = small_out[1 + 5 * t:6 + 5 * t]
        outs += [jnp.swapaxes(big_in[t], 1, 2), pw_, sc_, sk_, big_out[t], npre_, npost_]
    return tuple(outs)
```

```python
import numpy as np
import jax
import jax.numpy as jnp
from jax import lax
from jax.experimental import pallas as pl
from jax.experimental.pallas import tpu as pltpu

F32 = jnp.float32
BF16 = jnp.bfloat16

N_DEV = 8
SEQ = 2048
D_MODEL = 1024
D_POOL = 512
D_ATTN = 512
D_KV = 128
D_IN = 2304
N_HEADS = 8
GQA = 4
HEAD_DIM = 64
BLOCK = 128
N_BLOCKS = SEQ // BLOCK
POOL_WINDOWS = (2, 4, 8, 16)
DEPTH = 2
EPS = 1e-6
NEG_INF = -1e30
SCALE = HEAD_DIM ** -0.5
IN_SHARD = D_IN // N_DEV
OUT_SHARD = D_MODEL // N_DEV

COL_U, COL_PG, COL_Q, COL_K, COL_V, COL_AG = 0, 512, 1024, 1536, 1664, 1792

ADAM_LR = 0.001
ADAM_B1 = 0.9
ADAM_B2 = 0.999
ADAM_EPS = 1e-08
ADAM_WD = 0.01
ADAM_STEP = 10

TOKEN_TILE = 512
VMEM_LIMIT = 56 * 1024 * 1024
MESH = pl.DeviceIdType.MESH

ROW_PW, ROW_SC, ROW_SINK, ROW_NPRE, ROW_NPOST, ROW_LOSS = 0, 512, 520, 528, 536, 544
PACK_ROWS = 576
PACK_SLICE = PACK_ROWS // N_DEV


def _nn(a, b):
    return jnp.dot(a, b, preferred_element_type=F32)


def _nt(a, b):
    return lax.dot_general(a, b, (((1,), (1,)), ((), ())), preferred_element_type=F32)


def _tn(a, b):
    return lax.dot_general(a, b, (((0,), (0,)), ((), ())), preferred_element_type=F32)


def _silu_parts(g):
    s = jax.nn.sigmoid(g)
    return g * s, s * (1.0 + g * (1.0 - s))


def _resident(shape):
    return pl.BlockSpec(shape, lambda *_: (0,) * len(shape), pipeline_mode=pl.Buffered(1))


def _compiler_params(sem=None):
    if sem is None:
        return pltpu.CompilerParams(vmem_limit_bytes=VMEM_LIMIT)
    return pltpu.CompilerParams(dimension_semantics=sem, vmem_limit_bytes=VMEM_LIMIT)


def _attn_bias():
    t = np.arange(BLOCK)[:, None]
    j = np.arange(2 * BLOCK)[None, :]
    dist = t + BLOCK - j
    in_win = (dist >= 0) & (dist < BLOCK)
    out = np.zeros((2, 2, BLOCK, GQA * 2 * BLOCK), np.float32)
    for variant in range(2):
        valid = in_win & ((j >= BLOCK) | (variant == 1))
        for kv in range(2):
            for g in range(GQA):
                slope = np.float32(2.0 ** (-(kv * GQA + g + 1)))
                b = np.where(valid, -slope * dist.astype(np.float32), np.float32(NEG_INF))
                out[variant, kv, :, g * 256:(g + 1) * 256] = b
    return out


def _block_diag(kx, kv):
    rolled = pltpu.roll(kx, 64, 1)
    lane = lax.broadcasted_iota(jnp.int32, kx.shape, 1)
    dup = jnp.where(lane < 64, kx, rolled) if kv == 0 else jnp.where(lane < 64, rolled, kx)
    rep = jnp.concatenate([dup, dup], axis=1).astype(BF16)
    lane2 = lax.broadcasted_iota(jnp.int32, rep.shape, 1)
    zero = jnp.zeros_like(rep)
    parts = [jnp.where((lane2 >= 64 * g) & (lane2 < 64 * g + 64), rep, zero) for g in range(GQA)]
    return jnp.concatenate(parts, axis=0)


def _diag_fold(m):
    lane = lax.broadcasted_iota(jnp.int32, (256, 256), 1)
    r = jnp.where(lane < 64, m[0:256], jnp.where(lane < 128, m[256:512], jnp.where(lane < 192, m[512:768], m[768:1024])))
    h = r[:, 0:128] + r[:, 128:256]
    return h + pltpu.roll(h, 64, 1)


def _window_sum(ext, w, forward):
    s = ext
    sh = 1
    while sh < w:
        s = s + pltpu.roll(s, (256 - sh) if forward else sh, 0)
        sh *= 2
    return s


def _inv_count(n, w):
    t = n * BLOCK + lax.broadcasted_iota(jnp.int32, (BLOCK, 1), 0) + 1
    return 1.0 / jnp.minimum(t.astype(F32), float(w))


def _kv_ext(ref, n):
    r0 = pl.multiple_of(jnp.maximum(n - 1, 0) * BLOCK, BLOCK)
    r1 = pl.multiple_of(n * BLOCK, BLOCK)
    return jnp.concatenate([ref[pl.ds(r0, BLOCK), :], ref[pl.ds(r1, BLOCK), :]], axis=0)


def _softmax_chunk(sg, sink):
    m = jnp.maximum(jnp.max(sg, axis=-1, keepdims=True), sink)
    p = jnp.exp(sg - m)
    esink = jnp.exp(sink - m)
    rl = 1.0 / (jnp.sum(p, axis=-1, keepdims=True) + esink)
    return p * rl, esink * rl


def _rows_of(vec_ref, pack_ref, row0):
    for r in range(D_MODEL // 128):
        pack_ref[row0 + r:row0 + r + 1, :] = vec_ref[:, 128 * r:128 * (r + 1)]


FRONT_TILE = 2 * BLOCK


def _fwd_front(layer, x, norm_pre, w_in_t, token, sinks, pool_w, pool_scale, bias):
    tm = FRONT_TILE

    def body(sink_ref, x_ref, g_ref, w_ref, _, pw_ref, sc_ref, bias_ref,
             u_ref, pg_ref, q_ref, k_ref, v_ref, ag_ref, z_ref, a_ref, uprev, kprev, vprev):
        i = pl.program_id(0)

        @pl.when(i == 0)
        def _():
            uprev[...] = jnp.zeros_like(uprev)
            kprev[...] = jnp.zeros_like(kprev)
            vprev[...] = jnp.zeros_like(vprev)

        xv = x_ref[...]
        r = lax.rsqrt(jnp.mean(xv * xv, axis=-1, keepdims=True) + EPS)
        h = (xv * r * g_ref[layer:layer + 1, :]).astype(BF16)
        u_ref[...] = _nt(h, w_ref[COL_U:COL_PG, :])
        pg_ref[...] = _nt(h, w_ref[COL_PG:COL_Q, :])
        q_ref[...] = _nt(h, w_ref[COL_Q:COL_K, :]).astype(BF16)
        k_ref[...] = _nt(h, w_ref[COL_K:COL_V, :])
        v_ref[...] = _nt(h, w_ref[COL_V:COL_AG, :])
        ag_ref[...] = _nt(h, w_ref[COL_AG:D_IN, :])

        for sb in range(tm // BLOCK):
            n = (tm // BLOCK) * i + sb
            rows = slice(BLOCK * sb, BLOCK * (sb + 1))
            before = slice(BLOCK * (sb - 1), BLOCK * sb)
            uv = u_ref[rows, :]
            ext = jnp.concatenate([uprev[...] if sb == 0 else u_ref[before, :], uv], axis=0)
            for g, w in enumerate(POOL_WINDOWS):
                cs = slice(BLOCK * g, BLOCK * (g + 1))
                win = _window_sum(ext[:, cs], w, forward=False)[BLOCK:]
                pooled = win * _inv_count(n, w) - uv[:, cs]
                mixed = _nn(pooled.astype(BF16), pw_ref[g].astype(BF16))
                gate, _ = _silu_parts(pg_ref[rows, cs])
                z_ref[rows, cs] = (mixed * sc_ref[layer:layer + 1, cs] * gate).astype(BF16)

            kx = jnp.concatenate([kprev[...] if sb == 0 else k_ref[before, :], k_ref[rows, :]], axis=0)
            vx = jnp.concatenate([vprev[...] if sb == 0 else v_ref[before, :], v_ref[rows, :]], axis=0)
            variant = jnp.minimum(n, 1) if sb == 0 else 1
            for kv in range(2):
                cs = slice(256 * kv, 256 * (kv + 1))
                bk = _block_diag(kx, kv)
                bv = _block_diag(vx, kv)
                s = _nt(q_ref[rows, cs], bk) * SCALE + bias_ref[variant, kv]
                ps = []
                for g in range(GQA):
                    p, _ = _softmax_chunk(s[:, 256 * g:256 * (g + 1)], sink_ref[layer, kv * GQA + g])
                    ps.append(p.astype(BF16))
                o = _nn(jnp.concatenate(ps, axis=1), bv)
                a_ref[rows, cs] = o
                gate, _ = _silu_parts(ag_ref[rows, cs])
                z_ref[rows, D_POOL + 256 * kv:D_POOL + 256 * (kv + 1)] = (o * gate).astype(BF16)

        tail = slice(tm - BLOCK, tm)
        uprev[...] = u_ref[tail, :]
        kprev[...] = k_ref[tail, :]
        vprev[...] = v_ref[tail, :]

    row = lambda c: pl.BlockSpec((tm, c), lambda i: (i, 0))
    const = lambda shape: pl.BlockSpec(shape, lambda i: (0,) * len(shape))
    return pl.pallas_call(
        body, name=f"fwd_front{layer}", grid=(SEQ // tm,),
        in_specs=[pl.BlockSpec(memory_space=pltpu.SMEM), row(D_MODEL), const((DEPTH, D_MODEL)),
                  _resident((D_IN, D_MODEL)), const((8, 128)),
                  pl.BlockSpec((None, 4, BLOCK, BLOCK), lambda i: (layer, 0, 0, 0)), const((DEPTH, D_POOL)),
                  _resident((2, 2, BLOCK, 1024))],
        out_specs=[row(D_POOL), row(D_POOL), row(D_ATTN), row(D_KV), row(D_KV), row(D_ATTN), row(D_MODEL),
                   row(D_ATTN)],
        out_shape=[jax.ShapeDtypeStruct((SEQ, D_POOL), F32), jax.ShapeDtypeStruct((SEQ, D_POOL), F32),
                   jax.ShapeDtypeStruct((SEQ, D_ATTN), BF16), jax.ShapeDtypeStruct((SEQ, D_KV), F32),
                   jax.ShapeDtypeStruct((SEQ, D_KV), F32), jax.ShapeDtypeStruct((SEQ, D_ATTN), F32),
                   jax.ShapeDtypeStruct((SEQ, D_MODEL), BF16), jax.ShapeDtypeStruct((SEQ, D_ATTN), F32)],
        scratch_shapes=[pltpu.VMEM((BLOCK, D_POOL), F32), pltpu.VMEM((BLOCK, D_KV), F32),
                        pltpu.VMEM((BLOCK, D_KV), F32)],
        compiler_params=_compiler_params(("arbitrary",)),
    )(sinks, x, norm_pre, w_in_t, token, pool_w, pool_scale, bias)


def _fwd_out(layer, z, x, norm_post, w_out):
    tm = TOKEN_TILE

    def body(z_ref, x_ref, g_ref, w_ref, xn_ref, y_ref):
        y = _nn(z_ref[...], w_ref[...])
        y_ref[...] = y
        r = lax.rsqrt(jnp.mean(y * y, axis=-1, keepdims=True) + EPS)
        xn_ref[...] = x_ref[...] + y * r * g_ref[layer:layer + 1, :]

    row = lambda c: pl.BlockSpec((tm, c), lambda i: (i, 0))
    return pl.pallas_call(
        body, name=f"fwd_out{layer}", grid=(SEQ // tm,),
        in_specs=[row(D_MODEL), row(D_MODEL), pl.BlockSpec((DEPTH, D_MODEL), lambda i: (0, 0)),
                  _resident((D_MODEL, D_MODEL))],
        out_specs=[row(D_MODEL), row(D_MODEL)],
        out_shape=[jax.ShapeDtypeStruct((SEQ, D_MODEL), F32), jax.ShapeDtypeStruct((SEQ, D_MODEL), F32)],
        compiler_params=_compiler_params(("arbitrary",)),
    )(z, x, norm_post, w_out)


def _bwd_out(layer, top, dxo_or_xf, target_or_token, y, z, norm_post, w_out):
    tm = TOKEN_TILE
    steps = SEQ // tm

    def body(*refs):
        if top:
            xf_ref, t_ref, y_ref, z_ref, g_ref, w_ref, dxo_ref, dz_ref, dw_ref, pack_ref, acc, dg, lacc = refs
        else:
            dxi_ref, _, y_ref, z_ref, g_ref, w_ref, dz_ref, dw_ref, pack_ref, acc, dg, lacc = refs
        i = pl.program_id(0)

        @pl.when(i == 0)
        def _():
            acc[...] = jnp.zeros_like(acc)
            dg[...] = jnp.zeros_like(dg)
            lacc[...] = jnp.zeros_like(lacc)
            pack_ref[...] = jnp.zeros_like(pack_ref)

        if top:
            d = xf_ref[...] - t_ref[...]
            dxo_v = d * (1.0 / D_MODEL)
            dxo_ref[...] = dxo_v
            part = jnp.sum(d * d, axis=-1, keepdims=True) * (1.0 / D_MODEL)
            lacc[...] += 0.5 * jnp.sum(part, axis=0, keepdims=True)
        else:
            dxo_v = dxi_ref[...]
        y = y_ref[...]
        r = lax.rsqrt(jnp.mean(y * y, axis=-1, keepdims=True) + EPS)
        yn = y * r
        dg[...] += jnp.sum(dxo_v * yn, axis=0, keepdims=True)
        dyn = dxo_v * g_ref[layer:layer + 1, :]
        dy = (r * (dyn - yn * jnp.mean(dyn * yn, axis=-1, keepdims=True))).astype(BF16)
        dz_ref[...] = _nt(dy, w_ref[...])
        acc[...] += _tn(z_ref[...], dy)

        @pl.when(i == steps - 1)
        def _():
            dw_ref[...] = acc[...].astype(BF16)
            _rows_of(dg, pack_ref, ROW_NPOST)
            lane = lax.broadcasted_iota(jnp.int32, (1, 128), 1)
            pack_ref[ROW_LOSS:ROW_LOSS + 1, :] = jnp.where(lane == 0, lacc[...], 0.0)

    row = lambda c: pl.BlockSpec((tm, c), lambda i: (i, 0))
    const = lambda shape: pl.BlockSpec(shape, lambda i: (0,) * len(shape))
    act = jax.ShapeDtypeStruct((SEQ, D_MODEL), F32)
    return pl.pallas_call(
        body, name=f"bwd_out{layer}", grid=(steps,),
        in_specs=([row(D_MODEL), row(D_MODEL) if top else const((8, 128))]
                  + [row(D_MODEL), row(D_MODEL), const((DEPTH, D_MODEL)), _resident((D_MODEL, D_MODEL))]),
        out_specs=([row(D_MODEL)] * (2 if top else 1) + [const((D_MODEL, D_MODEL)), const((PACK_ROWS, 128))]),
        out_shape=([act] * (2 if top else 1)
                   + [jax.ShapeDtypeStruct((D_MODEL, D_MODEL), BF16), jax.ShapeDtypeStruct((PACK_ROWS, 128), F32)]),
        scratch_shapes=[pltpu.VMEM((D_MODEL, D_MODEL), F32), pltpu.VMEM((1, D_MODEL), F32), pltpu.VMEM((1, 1), F32)],
        compiler_params=_compiler_params(("arbitrary",)),
    )(dxo_or_xf, target_or_token, y, z, norm_post, w_out)


def _bwd_mix(layer, sinks, dz, u, pg, q, k, v, ag, a, pool_w, pool_scale, bias, pack):
    last = N_BLOCKS - 1

    def body(sink_ref, dz_ref, u_ref, up_ref, pg_ref, q_ref, k_ref, v_ref, ag_ref, a_ref, pw_ref, sc_ref,
             bias_ref, pin_ref, dp_ref, pack_ref, ck, cv, ce):
        i = pl.program_id(0)
        n = last - i

        @pl.when(i == 0)
        def _():
            ck[...] = jnp.zeros_like(ck)
            cv[...] = jnp.zeros_like(cv)
            ce[...] = jnp.zeros_like(ce)
            pack_ref[...] = pin_ref[...]

        uv = u_ref[...]
        has_prev = (n > 0).astype(F32)
        ext = jnp.concatenate([up_ref[...] * has_prev, uv], axis=0)
        for g, w in enumerate(POOL_WINDOWS):
            cs = slice(BLOCK * g, BLOCK * (g + 1))
            inv = _inv_count(n, w)
            win = _window_sum(ext[:, cs], w, forward=False)[BLOCK:]
            pooled = win * inv - uv[:, cs]
            pw_g = pw_ref[g].astype(BF16)
            mixed = _nn(pooled.astype(BF16), pw_g)
            gate, dgate = _silu_parts(pg_ref[:, cs])
            dzp = dz_ref[:, cs]
            sc = sc_ref[layer:layer + 1, cs]
            dpm = dzp * gate
            dp_ref[:, COL_PG + BLOCK * g:COL_PG + BLOCK * (g + 1)] = (dzp * (mixed * sc) * dgate).astype(BF16)
            pack_ref[ROW_SC + g:ROW_SC + g + 1, :] += jnp.sum(dpm * mixed, axis=0, keepdims=True)
            dmixed = (dpm * sc).astype(BF16)
            pack_ref[ROW_PW + BLOCK * g:ROW_PW + BLOCK * (g + 1), :] += _tn(pooled.astype(BF16), dmixed)
            dpooled = _nt(dmixed, pw_g)
            e = dpooled * inv
            lead = _window_sum(jnp.concatenate([e, ce[:, cs]], axis=0), w, forward=True)[:BLOCK]
            dp_ref[:, COL_U + BLOCK * g:COL_U + BLOCK * (g + 1)] = (lead - dpooled).astype(BF16)
            ce[:, cs] = e

        kx = _kv_ext(k_ref, n)
        vx = _kv_ext(v_ref, n)
        lane = lax.broadcasted_iota(jnp.int32, (1, 128), 1)
        dsink_row = jnp.zeros((1, 128), F32)
        tks, tvs = [], []
        for kv in range(2):
            cs = slice(256 * kv, 256 * (kv + 1))
            bk = _block_diag(kx, kv)
            bv = _block_diag(vx, kv)
            qv = q_ref[:, cs]
            s = _nt(qv, bk) * SCALE + bias_ref[0, kv]
            gate, dgate = _silu_parts(ag_ref[:, cs])
            dza = dz_ref[:, D_POOL + 256 * kv:D_POOL + 256 * (kv + 1)]
            dp_ref[:, COL_AG + 256 * kv:COL_AG + 256 * (kv + 1)] = (dza * a_ref[:, cs] * dgate).astype(BF16)
            da = dza * gate
            dab = da.astype(BF16)
            dpr = _nt(dab, bv)
            ps, dss = [], []
            for g in range(GQA):
                gs = slice(256 * g, 256 * (g + 1))
                p, psink = _softmax_chunk(s[:, gs], sink_ref[layer, kv * GQA + g])
                dpg_ = dpr[:, gs]
                delta = jnp.sum(p * dpg_, axis=-1, keepdims=True)
                dsink = -jnp.sum(psink * delta, axis=0, keepdims=True)
                dsink_row = dsink_row + jnp.where(lane == kv * GQA + g, dsink, 0.0)
                ps.append(p)
                dss.append(p * (dpg_ - delta) * SCALE)
            p_all = jnp.concatenate(ps, axis=1)
            ds_all = jnp.concatenate(dss, axis=1)
            dp_ref[:, COL_Q + 256 * kv:COL_Q + 256 * (kv + 1)] = _nn(ds_all.astype(BF16), bk).astype(BF16)
            tks.append(_diag_fold(_tn(ds_all.astype(BF16), qv)))
            tvs.append(_diag_fold(_tn(p_all.astype(BF16), dab)))
        pack_ref[ROW_SINK:ROW_SINK + 1, :] += dsink_row
        lane2 = lax.broadcasted_iota(jnp.int32, (256, 128), 1)
        dkx = jnp.where(lane2 < 64, tks[0], tks[1])
        dvx = jnp.where(lane2 < 64, tvs[0], tvs[1])
        dp_ref[:, COL_K:COL_V] = (ck[...] + dkx[BLOCK:]).astype(BF16)
        dp_ref[:, COL_V:COL_AG] = (cv[...] + dvx[BLOCK:]).astype(BF16)
        ck[...] = dkx[:BLOCK]
        cv[...] = dvx[:BLOCK]

    blk = lambda c: pl.BlockSpec((BLOCK, c), lambda i: (last - i, 0))
    full = lambda shape: pl.BlockSpec(shape, lambda i: (0,) * len(shape))
    return pl.pallas_call(
        body, name=f"bwd_mix{layer}", grid=(N_BLOCKS,),
        in_specs=[pl.BlockSpec(memory_space=pltpu.SMEM), blk(D_MODEL), blk(D_POOL),
                  pl.BlockSpec((BLOCK, D_POOL), lambda i: (jnp.maximum(last - i - 1, 0), 0)),
                  blk(D_POOL), blk(D_ATTN), full((SEQ, D_KV)), full((SEQ, D_KV)), blk(D_ATTN), blk(D_ATTN),
                  pl.BlockSpec((None, 4, BLOCK, BLOCK), lambda i: (layer, 0, 0, 0)), full((DEPTH, D_POOL)),
                  pl.BlockSpec((1, 2, BLOCK, 1024), lambda i: (jnp.minimum(last - i, 1), 0, 0, 0)),
                  full((PACK_ROWS, 128))],
        out_specs=[blk(D_IN), full((PACK_ROWS, 128))],
        out_shape=[jax.ShapeDtypeStruct((SEQ, D_IN), BF16), jax.ShapeDtypeStruct((PACK_ROWS, 128), F32)],
        scratch_shapes=[pltpu.VMEM((BLOCK, D_KV), F32), pltpu.VMEM((BLOCK, D_KV), F32),
                        pltpu.VMEM((BLOCK, D_POOL), F32)],
        input_output_aliases={13: 1},
        compiler_params=_compiler_params(("arbitrary",)),
    )(sinks, dz, u, u, pg, q, k, v, ag, a, pool_w, pool_scale, bias, pack)


def _bwd_in(layer, part, token, dproj, x, norm_pre, dxo=None, w_in_t=None):
    want_dw, want_dx = part in ("both", "dw"), part in ("both", "dx")
    tm = TOKEN_TILE
    steps = SEQ // tm
    cw = 256

    def body(*refs):
        refs = list(refs)
        dp_ref, x_ref, g_ref = refs[1:4]
        del refs[:4]
        if want_dx:
            dxo_ref, w_ref, dx_ref, dgo_ref = refs[:4]
            del refs[:4]
            dg = refs.pop()
        if want_dw:
            dw_ref, acc = refs
        i = pl.program_id(0)

        @pl.when(i == 0)
        def _():
            if want_dw:
                acc[...] = jnp.zeros_like(acc)
            if want_dx:
                dg[...] = jnp.zeros_like(dg)

        xv = x_ref[...]
        gv = g_ref[layer:layer + 1, :]
        r = lax.rsqrt(jnp.mean(xv * xv, axis=-1, keepdims=True) + EPS)
        xn = xv * r
        if want_dw:
            hb = (xn * gv).astype(BF16)
            for c in range(0, D_IN, cw):
                acc[c:c + cw, :] += _tn(dp_ref[:, c:c + cw], hb)
        if want_dx:
            dh = _nn(dp_ref[...], w_ref[...])
            dg[...] += jnp.sum(dh * xn, axis=0, keepdims=True)
            dhn = dh * gv
            dx_ref[...] = dxo_ref[...] + r * (dhn - xn * jnp.mean(dhn * xn, axis=-1, keepdims=True))

        @pl.when(i == steps - 1)
        def _():
            if want_dw:
                dw_ref[...] = acc[...].astype(BF16)
            if want_dx:
                _rows_of(dg, dgo_ref, 0)

    row = lambda c: pl.BlockSpec((tm, c), lambda i: (i, 0))
    const = lambda shape: pl.BlockSpec(shape, lambda i: (0,) * len(shape))
    in_specs = [const((8, 128)), row(D_IN), row(D_MODEL), const((DEPTH, D_MODEL))]
    operands = [token, dproj, x, norm_pre]
    out_specs, out_shape, scratch = [], [], []
    if want_dx:
        in_specs += [row(D_MODEL), _resident((D_IN, D_MODEL))]
        operands += [dxo, w_in_t]
        out_specs += [row(D_MODEL), const((8, 128))]
        out_shape += [jax.ShapeDtypeStruct((SEQ, D_MODEL), F32), jax.ShapeDtypeStruct((8, 128), F32)]
    if want_dw:
        out_specs.append(const((D_IN, D_MODEL)))
        out_shape.append(jax.ShapeDtypeStruct((D_IN, D_MODEL), BF16))
        scratch.append(pltpu.VMEM((D_IN, D_MODEL), F32))
    if want_dx:
        scratch.append(pltpu.VMEM((1, D_MODEL), F32))
    return pl.pallas_call(
        body, name=f"bwd_in_{part}{layer}", grid=(steps,),
        in_specs=in_specs, out_specs=out_specs, out_shape=out_shape, scratch_shapes=scratch,
        compiler_params=_compiler_params(("arbitrary",)),
    )(*operands)


def _mesh_pos():
    return lax.axis_index("x"), lax.axis_index("y"), lax.axis_index("c")


def _device_rows(ref, m, px, py, pc):
    return ref.at[pl.ds(pl.multiple_of((4 * px + 2 * py + pc) * m, 16 if m % 16 == 0 else 8), m), :]


def _allgather(srcs, out_dtype, name, later=()):
    na, nb = len(srcs), len(later)
    every = list(srcs) + list(later)
    shapes = [(a.shape[-2], a.shape[-1]) for a, _ in every]

    def body(*refs):
        xs, refs = refs[:na + nb], refs[na + nb:]
        outs, cast, land, refs = refs[:na], refs[na:na + nb], refs[na + nb:na + 2 * nb], refs[na + 2 * nb:]
        stage, (send_sems, recv_sems, local_sems) = refs[:na], refs[na:]
        x, y, c = _mesh_pos()
        me, sibling = (x, y, c), (x, y, 1 - c)
        chips = [(1 - x, y), (x, 1 - y), (1 - x, 1 - y)]

        def slot(a, px, py, pc):
            return _device_rows(outs[a], shapes[a][0], px, py, pc)

        def copy(a, k, block, to, src=None):
            return pltpu.make_async_remote_copy(
                src_ref=slot(a, *block) if src is None else src, dst_ref=slot(a, *block),
                send_sem=send_sems.at[a, k], recv_sem=recv_sems.at[a, k], device_id=to, device_id_type=MESH)

        def cast_block(i):
            layer = every[i][1]
            return (xs[i][...] if layer is None else xs[i][layer]).astype(out_dtype)

        for a in range(na):
            stage[a][...] = cast_block(a)
        mine = [pltpu.make_async_copy(stage[a], slot(a, *me), local_sems.at[a]) for a in range(na)]
        for cp in mine:
            cp.start()
        first = []
        for a in range(na):
            first.append(copy(a, 0, me, sibling, src=stage[a]))
            first += [copy(a, 1 + j, me, (*chip, c), src=stage[a]) for j, chip in enumerate(chips)]
        for cp in first:
            cp.start()
        for b in range(nb):
            cast[b][...] = cast_block(na + b)
            cp = pltpu.make_async_copy(cast[b], _device_rows(land[b], shapes[na + b][0], *me), local_sems.at[na + b])
            cp.start()
            mine.append(cp)
        passed = []
        for a in range(na):
            for j, chip in enumerate(chips):
                copy(a, 1 + j, (*chip, c), me).wait_recv()
                cp = copy(a, 4 + j, (*chip, c), sibling)
                cp.start()
                passed.append(cp)
        for a in range(na):
            copy(a, 0, sibling, me).wait_recv()
            for j, chip in enumerate(chips):
                copy(a, 4 + j, (*chip, 1 - c), me).wait_recv()
        for cp in first + passed:
            cp.wait_send()
        for cp in mine:
            cp.wait()

    vmem = pl.BlockSpec(memory_space=pltpu.VMEM)
    gathered = [jax.ShapeDtypeStruct((N_DEV * m, n), out_dtype) for m, n in shapes]
    out = pl.pallas_call(
        body, name=name,
        in_specs=[vmem] * (na + nb),
        out_specs=[vmem] * (na + nb) + [pl.BlockSpec(memory_space=pl.ANY)] * nb,
        out_shape=gathered[:na] + [jax.ShapeDtypeStruct(s, out_dtype) for s in shapes[na:]] + gathered[na:],
        scratch_shapes=([pltpu.VMEM(s, out_dtype) for s in shapes[:na]]
                        + [pltpu.SemaphoreType.DMA((na, 7)), pltpu.SemaphoreType.DMA((na, 7)),
                           pltpu.SemaphoreType.DMA((na + nb,))]),
        compiler_params=_compiler_params(),
    )(*[a for a, _ in every])
    return out[:na], out[na:na + nb], out[na + nb:]


def _gather_start(blocks, lands, name):
    na = len(blocks)

    def body(*refs):
        src, land, sems, token = refs[:na], refs[na:2 * na], refs[2 * na:4 * na], refs[-1]
        x, y, c = _mesh_pos()
        for a in range(na):
            for k in range(1, N_DEV):
                peer = (x ^ ((k >> 2) & 1), y ^ ((k >> 1) & 1), c ^ (k & 1))
                pltpu.make_async_remote_copy(
                    src_ref=src[a], dst_ref=_device_rows(land[a], blocks[a].shape[0], x, y, c),
                    send_sem=sems[2 * a].at[k - 1], recv_sem=sems[2 * a + 1].at[k - 1],
                    device_id=peer, device_id_type=MESH).start()
        token[...] = jnp.zeros_like(token)

    bufs = [pltpu.HBM(t.shape, t.dtype) for t in list(blocks) + list(lands)]
    out = pl.pallas_call(
        body, name=name,
        out_shape=(*([pltpu.SemaphoreType.DMA((N_DEV - 1,))] * (2 * na)), *bufs, jax.ShapeDtypeStruct((8, 128), F32)),
        in_specs=[_HBM] * (2 * na),
        out_specs=(*([_SEM] * (2 * na)), *([_HBM] * (2 * na)), pl.BlockSpec(memory_space=pltpu.VMEM)),
        input_output_aliases={i: 2 * na + i for i in range(2 * na)},
        compiler_params=pltpu.CompilerParams(has_side_effects=_EFFECT),
    )(*[pltpu.with_memory_space_constraint(t, pltpu.HBM) for t in list(blocks) + list(lands)])
    sems = [(out[2 * a], out[2 * a + 1]) for a in range(na)]
    return sems, out[2 * na:3 * na], out[3 * na:4 * na], out[-1]


def _gather_wait(sems, block, land, after, name):
    def body(src, land_ref, send_sem, recv_sem, after_ref, src_out, land_out):
        x, y, c = _mesh_pos()
        for k in range(1, N_DEV):
            peer = (x ^ ((k >> 2) & 1), y ^ ((k >> 1) & 1), c ^ (k & 1))
            cp = pltpu.make_async_remote_copy(
                src_ref=src, dst_ref=_device_rows(land_ref, block.shape[0], *peer),
                send_sem=send_sem.at[k - 1], recv_sem=recv_sem.at[k - 1], device_id=peer, device_id_type=MESH)
            cp.wait_send()
            cp.wait_recv()

    out = pl.pallas_call(
        body, name=name,
        out_shape=(pltpu.HBM(block.shape, block.dtype), pltpu.HBM(land.shape, land.dtype)),
        in_specs=[_HBM, _HBM, _SEM, _SEM, pl.BlockSpec(memory_space=pl.ANY)],
        out_specs=[_HBM, _HBM],
        input_output_aliases={0: 0, 1: 1},
        compiler_params=pltpu.CompilerParams(has_side_effects=_EFFECT),
    )(block, land, sems[0], sems[1], after)
    return out[1]


def _row_step(m):
    return next(s for s in (32, 24, 16, 8) if m % s == 0)


def _pair_reduce(arrs, name):
    na = len(arrs)

    def body(*refs):
        gs, hs, hm = refs[:na], refs[na:2 * na], refs[2 * na:3 * na]
        own, ra = refs[3 * na:4 * na], refs[4 * na:5 * na]
        d2d_send, d2d_recv, local_sems = refs[5 * na:]
        x, y, c = _mesh_pos()
        sibling = (x, y, 1 - c)
        loads, sends = [], []
        for a in range(na):
            for q in range(4):
                cp = pltpu.make_async_copy(gs[a].at[2 * q + c], own[a].at[q], local_sems.at[a, q])
                cp.start()
                loads.append(cp)
                cp = pltpu.make_async_remote_copy(
                    src_ref=gs[a].at[2 * q + (1 - c)], dst_ref=ra[a].at[q], send_sem=d2d_send.at[a, q],
                    recv_sem=d2d_recv.at[a, q], device_id=sibling, device_id_type=MESH)
                cp.start()
                sends.append(cp)
        for cp in loads:
            cp.wait()
        for cp in sends:
            cp.wait_recv()
        others = [2 * (1 - x) + y, 2 * x + (1 - y), 2 * (1 - x) + (1 - y)]
        for a in range(na):
            m = arrs[a].shape[1]
            step = _row_step(m)

            def add(i, carry, a=a, step=step):
                rs = pl.ds(pl.multiple_of(i * step, step), step)
                for j, q in enumerate(others):
                    hs[a][j, rs, :] = (own[a][q, rs, :].astype(F32) + ra[a][q, rs, :].astype(F32)).astype(hs[a].dtype)
                q = 2 * x + y
                hm[a][rs, :] = own[a][q, rs, :].astype(F32) + ra[a][q, rs, :].astype(F32)
                return carry

            lax.fori_loop(0, m // step, add, 0)
        for cp in sends:
            cp.wait_send()

    vmem = pl.BlockSpec(memory_space=pltpu.VMEM)
    scratch = [pltpu.VMEM((4,) + t.shape[1:], t.dtype) for t in arrs] * 2
    scratch += [pltpu.SemaphoreType.DMA((na, 4)), pltpu.SemaphoreType.DMA((na, 4)), pltpu.SemaphoreType.DMA((na, 4))]
    out = pl.pallas_call(
        body, name=name,
        in_specs=[pl.BlockSpec(memory_space=pl.ANY)] * na, out_specs=[vmem] * (2 * na),
        out_shape=([jax.ShapeDtypeStruct((3,) + t.shape[1:], t.dtype) for t in arrs]
                   + [jax.ShapeDtypeStruct(t.shape[1:], F32) for t in arrs]),
        scratch_shapes=scratch,
        compiler_params=_compiler_params(),
    )(*arrs)
    return out[:na], out[na:]


_HBM = pl.BlockSpec(memory_space=pltpu.HBM)
_SEM = pl.BlockSpec(memory_space=pltpu.SEMAPHORE)
_EFFECT = pltpu.SideEffectType.DATAFLOW_SIDE_EFFECTING


def _chip_exchange_start(srcs, name):
    na = len(srcs)

    def body(*refs):
        src, land = refs[:na], refs[na:2 * na]
        send_sem, recv_sem = refs[2 * na], refs[2 * na + 1]
        token = refs[-1]
        x, y, c = _mesh_pos()
        for j, (qx, qy) in enumerate([(1 - x, y), (x, 1 - y), (1 - x, 1 - y)]):
            for a in range(na):
                pltpu.make_async_remote_copy(
                    src_ref=src[a].at[j], dst_ref=land[a].at[j], send_sem=send_sem.at[3 * a + j],
                    recv_sem=recv_sem.at[3 * a + j], device_id=(qx, qy, c), device_id_type=MESH).start()
        token[...] = jnp.zeros_like(token)

    bufs = [pltpu.HBM(t.shape, t.dtype) for t in srcs]
    out = pl.pallas_call(
        body, name=name,
        out_shape=(pltpu.SemaphoreType.DMA((3 * na,)), pltpu.SemaphoreType.DMA((3 * na,)), *bufs, *bufs,
                   jax.ShapeDtypeStruct((8, 128), F32)),
        in_specs=[_HBM] * (2 * na),
        out_specs=(_SEM, _SEM, *([_HBM] * (2 * na)), pl.BlockSpec(memory_space=pltpu.VMEM)),
        input_output_aliases={i: 2 + i for i in range(2 * na)},
        compiler_params=pltpu.CompilerParams(has_side_effects=_EFFECT),
    )(*[pltpu.with_memory_space_constraint(t, pltpu.HBM) for t in srcs],
      *[pltpu.with_memory_space_constraint(lax.empty(t.shape, t.dtype), pltpu.HBM) for t in srcs])
    return out[0], out[1], out[2:2 + na], out[2 + na:2 + 2 * na], out[-1]


def _chip_exchange_wait(send_sem, recv_sem, srcs, lands, after, name):
    na = len(srcs)

    def body(*refs):
        src, land = refs[:na], refs[na:2 * na]
        send_sem_ref, recv_sem_ref = refs[2 * na], refs[2 * na + 1]
        x, y, c = _mesh_pos()
        for j, (qx, qy) in enumerate([(1 - x, y), (x, 1 - y), (1 - x, 1 - y)]):
            for a in range(na):
                cp = pltpu.make_async_remote_copy(
                    src_ref=src[a].at[j], dst_ref=land[a].at[j], send_sem=send_sem_ref.at[3 * a + j],
                    recv_sem=recv_sem_ref.at[3 * a + j], device_id=(qx, qy, c), device_id_type=MESH)
                cp.wait_send()
                cp.wait_recv()

    bufs = [pltpu.HBM(t.shape, t.dtype) for t in srcs]
    out = pl.pallas_call(
        body, name=name,
        out_shape=(*bufs, *bufs),
        in_specs=[_HBM] * (2 * na) + [_SEM, _SEM, pl.BlockSpec(memory_space=pl.ANY)],
        out_specs=[_HBM] * (2 * na),
        input_output_aliases={i: i for i in range(2 * na)},
        compiler_params=pltpu.CompilerParams(has_side_effects=_EFFECT),
    )(*srcs, *lands, send_sem, recv_sem, after)
    return out[na:]


SMALL_ROWS = 2 * PACK_SLICE + 2 * 8


def _small_block(mine, lands, dgpre, name):
    def body(*refs):
        hm, ld, dg = refs[:DEPTH], refs[DEPTH:2 * DEPTH], refs[2 * DEPTH:3 * DEPTH]
        blk, land, sem = refs[3 * DEPTH:]
        for l in range(DEPTH):
            acc = hm[l][...]
            for j in range(3):
                acc = acc + ld[l][j]
            blk[PACK_SLICE * l:PACK_SLICE * (l + 1), :] = acc
            blk[2 * PACK_SLICE + 8 * l:2 * PACK_SLICE + 8 * (l + 1), :] = dg[l][...]
        cp = pltpu.make_async_copy(blk, _device_rows(land, SMALL_ROWS, *_mesh_pos()), sem)
        cp.start()
        cp.wait()

    vmem = pl.BlockSpec(memory_space=pltpu.VMEM)
    return pl.pallas_call(
        body, name=name,
        in_specs=[vmem] * (3 * DEPTH), out_specs=[vmem, pl.BlockSpec(memory_space=pl.ANY)],
        out_shape=[jax.ShapeDtypeStruct((SMALL_ROWS, 128), F32), jax.ShapeDtypeStruct((N_DEV * SMALL_ROWS, 128), F32)],
        scratch_shapes=[pltpu.SemaphoreType.DMA],
        compiler_params=_compiler_params(),
    )(*mine, *lands, *dgpre)


def _adamw_math(w, g, m, v):
    m = ADAM_B1 * m + (1.0 - ADAM_B1) * g
    v = ADAM_B2 * v + (1.0 - ADAM_B2) * (g * g)
    m_hat = m / (1.0 - ADAM_B1 ** ADAM_STEP)
    v_hat = v / (1.0 - ADAM_B2 ** ADAM_STEP)
    delta = -ADAM_LR * (m_hat / (jnp.sqrt(v_hat) + ADAM_EPS) + ADAM_WD * w)
    return delta, m, v


def _adamw_layer(layer, mine, lands, w, m, v, earlier, token, name, rows):
    _, mm, nn = w.shape

    def body(hm_ref, ld_ref, w_ref, m_ref, v_ref, _, *refs):
        g_ref, d_ref, nm_ref, nv_ref = refs[-4:]
        g = hm_ref[...]
        for j in range(3):
            g = g + ld_ref[j].astype(F32)
        g_ref[...] = g
        d, nm, nv = _adamw_math(w_ref[...], g, m_ref[...], v_ref[...])
        d_ref[...] = d
        nm_ref[...] = nm
        nv_ref[...] = nv

    spec = pl.BlockSpec((None, rows, nn), lambda i: (layer, i, 0))
    carried = [] if earlier is None else list(earlier)
    return pl.pallas_call(
        body, name=name, grid=(mm // rows,),
        in_specs=([pl.BlockSpec((rows, nn), lambda i: (i, 0)), pl.BlockSpec((3, rows, nn), lambda i: (0, i, 0)),
                   spec, spec, spec] + [pl.BlockSpec(memory_space=pl.ANY)] * (1 + len(carried))),
        out_specs=[spec] * 4,
        out_shape=[jax.ShapeDtypeStruct(w.shape, F32)] * 4,
        input_output_aliases={6 + t: t for t in range(len(carried))},
        compiler_params=_compiler_params(("arbitrary",)),
    )(mine, lands, w, m, v, token, *carried)


def _adamw_small(gathered, params):
    def body(all_ref, *refs):
        ins, outs, packs = refs[:15], refs[15:15 + 21], refs[15 + 21]
        loss_ref = outs[0]
        for dev in range(N_DEV):
            for l in range(DEPTH):
                packs[l, PACK_SLICE * dev:PACK_SLICE * (dev + 1), :] = (
                    all_ref[SMALL_ROWS * dev + PACK_SLICE * l:SMALL_ROWS * dev + PACK_SLICE * (l + 1), :])
        loss_ref[...] = packs[DEPTH - 1, ROW_LOSS:ROW_LOSS + 1, 0:1]

        def update(p, sel, g):
            w_ref, m_ref, v_ref = ins[p], ins[5 + p], ins[10 + p]
            d, nm, nv = _adamw_math(w_ref[sel], g, m_ref[sel], v_ref[sel])
            for t, val in enumerate((g, d, nm, nv)):
                outs[1 + 5 * t + p][sel] = val

        for l in range(DEPTH):
            gp = packs.at[l]
            row0 = 2 * PACK_SLICE + 8 * l
            dgpre = all_ref[row0:row0 + 8, :]
            for dev in range(1, N_DEV):
                dgpre = dgpre + all_ref[SMALL_ROWS * dev + row0:SMALL_ROWS * dev + row0 + 8, :]
            for grp in range(4):
                update(0, (l, grp), gp[ROW_PW + BLOCK * grp:ROW_PW + BLOCK * (grp + 1), :])
                update(1, (slice(l, l + 1), slice(128 * grp, 128 * (grp + 1))), gp[ROW_SC + grp:ROW_SC + grp + 1, :])
            update(2, (slice(l, l + 1), slice(None)), gp[ROW_SINK:ROW_SINK + 1, 0:N_HEADS])
            for r in range(D_MODEL // 128):
                sel = (slice(l, l + 1), slice(128 * r, 128 * (r + 1)))
                update(3, sel, dgpre[r:r + 1, :])
                update(4, sel, gp[ROW_NPOST + r:ROW_NPOST + r + 1, :])

    shapes = [jax.ShapeDtypeStruct(p.shape, F32) for p in params[:5]]
    return pl.pallas_call(
        body, name="adamw_small",
        out_shape=[jax.ShapeDtypeStruct((1, 1), F32)] + shapes * 4,
        scratch_shapes=[pltpu.VMEM((DEPTH, PACK_ROWS, 128), F32)],
        compiler_params=_compiler_params(),
    )(gathered, *params)


def kernel(x, w_in, pool_w, pool_scale, attn_sinks, w_out, norm_pre, norm_post, loss_target, m_w_in, m_pool_w, m_pool_scale, m_attn_sinks, m_w_out, m_norm_pre, m_norm_post, v_w_in, v_pool_w, v_pool_scale, v_attn_sinks, v_w_out, v_norm_pre, v_norm_post):
    x0 = x.reshape(SEQ, D_MODEL)
    target = loss_target.reshape(SEQ, D_MODEL)
    bias = jnp.asarray(_attn_bias())
    w_in_t, m_in_t, v_in_t = (jnp.swapaxes(t, 1, 2) for t in (w_in, m_w_in, v_w_in))

    (win0, wout0), later, lands = _allgather([(w_in_t, 0), (w_out, 0)], BF16, "gather_w0",
                                              later=[(w_in_t, 1), (w_out, 1)])
    sems, later, lands, token = _gather_start(later, lands, "gather_w1_start")
    win_full, wout_full = [win0, None], [wout0, None]

    saved = []
    xl = x0
    for layer in range(DEPTH):
        if layer > 0:
            win_full[layer] = _gather_wait(sems[0], later[0], lands[0], xl, "gather_w_in1_wait")
        u, pg, q, k, v, ag, z, a = _fwd_front(layer, xl, norm_pre, win_full[layer], token, attn_sinks,
                                             pool_w, pool_scale, bias)
        if layer > 0:
            wout_full[layer] = _gather_wait(sems[1], later[1], lands[1], z, "gather_w_out1_wait")
        x_next, y = _fwd_out(layer, z, xl, norm_post, wout_full[layer])
        saved.append((xl, u, pg, q, k, v, ag, z, a, y))
        xl = x_next

    params_small = [pool_w, pool_scale, attn_sinks, norm_pre, norm_post,
                    m_pool_w, m_pool_scale, m_attn_sinks, m_norm_pre, m_norm_post,
                    v_pool_w, v_pool_scale, v_attn_sinks, v_norm_pre, v_norm_post]
    exchange, dgpre = [None] * DEPTH, [None] * DEPTH
    dx = None
    for layer in reversed(range(DEPTH)):
        xin, u, pg, q, k, v, ag, z, a, y = saved[layer]
        if layer == DEPTH - 1:
            dx, dz, gw_out, pack = _bwd_out(layer, True, xl, target, y, z, norm_post, wout_full[layer])
        else:
            dz, gw_out, pack = _bwd_out(layer, False, dx, token, y, z, norm_post, wout_full[layer])
        dproj, pack = _bwd_mix(layer, attn_sinks, dz, u, pg, q, k, v, ag, a, pool_w, pool_scale, bias, pack)
        if layer > 0:
            dx, dgpre[layer], gw_in_t = _bwd_in(layer, "both", token, dproj, xin, norm_pre, dx, win_full[layer])
        else:
            (gw_in_t,) = _bwd_in(layer, "dw", token, dproj, xin, norm_pre)
        chip_sums, mine = _pair_reduce(
            [gw_in_t.reshape(N_DEV, IN_SHARD, D_MODEL), gw_out.reshape(N_DEV, OUT_SHARD, D_MODEL),
             pack.reshape(N_DEV, PACK_SLICE, 128)], f"pair_reduce{layer}")
        send_sem, recv_sem, srcs, lands, token = _chip_exchange_start(chip_sums, f"chip_exchange_start{layer}")
        exchange[layer] = (send_sem, recv_sem, srcs, lands, mine)
        if layer == 0:
            dx, dgpre[layer] = _bwd_in(layer, "dx", token, dproj, xin, norm_pre, dx, win_full[layer])

    waited = [None] * DEPTH
    big_in, big_out, after = None, None, dx
    for layer in reversed(range(DEPTH)):
        send_sem, recv_sem, srcs, lands, mine = exchange[layer]
        waited[layer] = _chip_exchange_wait(send_sem, recv_sem, srcs, lands, after, f"chip_exchange_wait{layer}")
        if layer > 0:
            big_in = _adamw_layer(layer, mine[0], waited[layer][0], w_in_t, m_in_t, v_in_t, big_in, token,
                                  f"adamw_in{layer}", 96)
            big_out = _adamw_layer(layer, mine[1], waited[layer][1], w_out, m_w_out, v_w_out, big_out, token,
                                   f"adamw_out{layer}", 128)
            after = big_out[0]
    block, land = _small_block([exchange[l][4][2] for l in range(DEPTH)], [waited[l][2] for l in range(DEPTH)],
                               dgpre, "small_block")
    sems, block, land, token = _gather_start([block], [land], "gather_small_start")
    mine = exchange[0][4]
    big_in = _adamw_layer(0, mine[0], waited[0][0], w_in_t, m_in_t, v_in_t, big_in, token, "adamw_in0", 96)
    big_out = _adamw_layer(0, mine[1], waited[0][1], w_out, m_w_out, v_w_out, big_out, token, "adamw_out0", 128)
    gathered = _gather_wait(sems[0], block[0], land[0], big_out[0], "gather_small_wait")
    small_out = _adamw_small(gathered, params_small)
    loss = small_out[0].reshape(())

    outs = [loss, dx.reshape(1, SEQ, D_MODEL)]
    for t in range(4):
        pw_, sc_, sk_, npre_, npost_ = small_out[1 + 5 * t:6 + 5 * t]
        outs += [jnp.swapaxes(big_in[t], 1, 2), pw_, sc_, sk_, big_out[t], npre_, npost_]
    return tuple(outs)
```

```python
import numpy as np
import jax
import jax.numpy as jnp
from jax import lax
from jax.experimental import pallas as pl
from jax.experimental.pallas import tpu as pltpu

F32 = jnp.float32
BF16 = jnp.bfloat16

N_DEV = 8
SEQ = 2048
D_MODEL = 1024
D_POOL = 512
D_ATTN = 512
D_KV = 128
D_IN = 2304
N_HEADS = 8
GQA = 4
HEAD_DIM = 64
BLOCK = 128
N_BLOCKS = SEQ // BLOCK
POOL_WINDOWS = (2, 4, 8, 16)
DEPTH = 2
EPS = 1e-6
NEG_INF = -1e30
SCALE = HEAD_DIM ** -0.5
IN_SHARD = D_IN // N_DEV
OUT_SHARD = D_MODEL // N_DEV

COL_U, COL_PG, COL_Q, COL_K, COL_V, COL_AG = 0, 512, 1024, 1536, 1664, 1792

ADAM_LR = 0.001
ADAM_B1 = 0.9
ADAM_B2 = 0.999
ADAM_EPS = 1e-08
ADAM_WD = 0.01
ADAM_STEP = 10

TOKEN_TILE = 512
VMEM_LIMIT = 56 * 1024 * 1024
MESH = pl.DeviceIdType.MESH

ROW_PW, ROW_SC, ROW_SINK, ROW_NPRE, ROW_NPOST, ROW_LOSS = 0, 512, 520, 528, 536, 544
PACK_ROWS = 576
PACK_SLICE = PACK_ROWS // N_DEV


def _nn(a, b):
    return jnp.dot(a, b, preferred_element_type=F32)


def _nt(a, b):
    return lax.dot_general(a, b, (((1,), (1,)), ((), ())), preferred_element_type=F32)


def _tn(a, b):
    return lax.dot_general(a, b, (((0,), (0,)), ((), ())), preferred_element_type=F32)


def _silu_parts(g):
    s = jax.nn.sigmoid(g)
    return g * s, s * (1.0 + g * (1.0 - s))


def _resident(shape):
    return pl.BlockSpec(shape, lambda *_: (0,) * len(shape), pipeline_mode=pl.Buffered(1))


def _compiler_params(sem=None):
    if sem is None:
        return pltpu.CompilerParams(vmem_limit_bytes=VMEM_LIMIT)
    return pltpu.CompilerParams(dimension_semantics=sem, vmem_limit_bytes=VMEM_LIMIT)


def _attn_bias():
    t = np.arange(BLOCK)[:, None]
    j = np.arange(2 * BLOCK)[None, :]
    dist = t + BLOCK - j
    in_win = (dist >= 0) & (dist < BLOCK)
    out = np.zeros((2, 2, BLOCK, GQA * 2 * BLOCK), np.float32)
    for variant in range(2):
        valid = in_win & ((j >= BLOCK) | (variant == 1))
        for kv in range(2):
            for g in range(GQA):
                slope = np.float32(2.0 ** (-(kv * GQA + g + 1)))
                b = np.where(valid, -slope * dist.astype(np.float32), np.float32(NEG_INF))
                out[variant, kv, :, g * 256:(g + 1) * 256] = b
    return out


def _block_diag(kx, kv):
    rolled = pltpu.roll(kx, 64, 1)
    lane = lax.broadcasted_iota(jnp.int32, kx.shape, 1)
    dup = jnp.where(lane < 64, kx, rolled) if kv == 0 else jnp.where(lane < 64, rolled, kx)
    rep = jnp.concatenate([dup, dup], axis=1).astype(BF16)
    lane2 = lax.broadcasted_iota(jnp.int32, rep.shape, 1)
    zero = jnp.zeros_like(rep)
    parts = [jnp.where((lane2 >= 64 * g) & (lane2 < 64 * g + 64), rep, zero) for g in range(GQA)]
    return jnp.concatenate(parts, axis=0)


def _diag_fold(m):
    lane = lax.broadcasted_iota(jnp.int32, (256, 256), 1)
    r = jnp.where(lane < 64, m[0:256], jnp.where(lane < 128, m[256:512], jnp.where(lane < 192, m[512:768], m[768:1024])))
    h = r[:, 0:128] + r[:, 128:256]
    return h + pltpu.roll(h, 64, 1)


def _window_sum(ext, w, forward):
    s = ext
    sh = 1
    while sh < w:
        s = s + pltpu.roll(s, (256 - sh) if forward else sh, 0)
        sh *= 2
    return s


def _inv_count(n, w):
    t = n * BLOCK + lax.broadcasted_iota(jnp.int32, (BLOCK, 1), 0) + 1
    return 1.0 / jnp.minimum(t.astype(F32), float(w))


def _kv_ext(ref, n):
    r0 = pl.multiple_of(jnp.maximum(n - 1, 0) * BLOCK, BLOCK)
    r1 = pl.multiple_of(n * BLOCK, BLOCK)
    return jnp.concatenate([ref[pl.ds(r0, BLOCK), :], ref[pl.ds(r1, BLOCK), :]], axis=0)


def _softmax_chunk(sg, sink):
    m = jnp.maximum(jnp.max(sg, axis=-1, keepdims=True), sink)
    p = jnp.exp(sg - m)
    esink = jnp.exp(sink - m)
    rl = 1.0 / (jnp.sum(p, axis=-1, keepdims=True) + esink)
    return p * rl, esink * rl


def _rows_of(vec_ref, pack_ref, row0):
    for r in range(D_MODEL // 128):
        pack_ref[row0 + r:row0 + r + 1, :] = vec_ref[:, 128 * r:128 * (r + 1)]


FRONT_TILE = 2 * BLOCK


def _fwd_front(layer, x, norm_pre, w_in_t, token, sinks, pool_w, pool_scale, bias):
    tm = FRONT_TILE

    def body(sink_ref, x_ref, g_ref, w_ref, _, pw_ref, sc_ref, bias_ref,
             u_ref, pg_ref, q_ref, k_ref, v_ref, ag_ref, z_ref, a_ref, uprev, kprev, vprev):
        i = pl.program_id(0)

        @pl.when(i == 0)
        def _():
            uprev[...] = jnp.zeros_like(uprev)
            kprev[...] = jnp.zeros_like(kprev)
            vprev[...] = jnp.zeros_like(vprev)

        xv = x_ref[...]
        r = lax.rsqrt(jnp.mean(xv * xv, axis=-1, keepdims=True) + EPS)
        h = (xv * r * g_ref[layer:layer + 1, :]).astype(BF16)
        u_ref[...] = _nt(h, w_ref[COL_U:COL_PG, :])
        pg_ref[...] = _nt(h, w_ref[COL_PG:COL_Q, :])
        q_ref[...] = _nt(h, w_ref[COL_Q:COL_K, :]).astype(BF16)
        k_ref[...] = _nt(h, w_ref[COL_K:COL_V, :])
        v_ref[...] = _nt(h, w_ref[COL_V:COL_AG, :])
        ag_ref[...] = _nt(h, w_ref[COL_AG:D_IN, :])

        for sb in range(tm // BLOCK):
            n = (tm // BLOCK) * i + sb
            rows = slice(BLOCK * sb, BLOCK * (sb + 1))
            before = slice(BLOCK * (sb - 1), BLOCK * sb)
            uv = u_ref[rows, :]
            ext = jnp.concatenate([uprev[...] if sb == 0 else u_ref[before, :], uv], axis=0)
            for g, w in enumerate(POOL_WINDOWS):
                cs = slice(BLOCK * g, BLOCK * (g + 1))
                win = _window_sum(ext[:, cs], w, forward=False)[BLOCK:]
                pooled = win * _inv_count(n, w) - uv[:, cs]
                mixed = _nn(pooled.astype(BF16), pw_ref[g].astype(BF16))
                gate, _ = _silu_parts(pg_ref[rows, cs])
                z_ref[rows, cs] = (mixed * sc_ref[layer:layer + 1, cs] * gate).astype(BF16)

            kx = jnp.concatenate([kprev[...] if sb == 0 else k_ref[before, :], k_ref[rows, :]], axis=0)
            vx = jnp.concatenate([vprev[...] if sb == 0 else v_ref[before, :], v_ref[rows, :]], axis=0)
            variant = jnp.minimum(n, 1) if sb == 0 else 1
            for kv in range(2):
                cs = slice(256 * kv, 256 * (kv + 1))
                bk = _block_diag(kx, kv)
                bv = _block_diag(vx, kv)
                s = _nt(q_ref[rows, cs], bk) * SCALE + bias_ref[variant, kv]
                ps = []
                for g in range(GQA):
                    p, _ = _softmax_chunk(s[:, 256 * g:256 * (g + 1)], sink_ref[layer, kv * GQA + g])
                    ps.append(p.astype(BF16))
                o = _nn(jnp.concatenate(ps, axis=1), bv)
                a_ref[rows, cs] = o
                gate, _ = _silu_parts(ag_ref[rows, cs])
                z_ref[rows, D_POOL + 256 * kv:D_POOL + 256 * (kv + 1)] = (o * gate).astype(BF16)

        tail = slice(tm - BLOCK, tm)
        uprev[...] = u_ref[tail, :]
        kprev[...] = k_ref[tail, :]
        vprev[...] = v_ref[tail, :]

    row = lambda c: pl.BlockSpec((tm, c), lambda i: (i, 0))
    const = lambda shape: pl.BlockSpec(shape, lambda i: (0,) * len(shape))
    return pl.pallas_call(
        body, name=f"fwd_front{layer}", grid=(SEQ // tm,),
        in_specs=[pl.BlockSpec(memory_space=pltpu.SMEM), row(D_MODEL), const((DEPTH, D_MODEL)),
                  _resident((D_IN, D_MODEL)), const((8, 128)),
                  pl.BlockSpec((None, 4, BLOCK, BLOCK), lambda i: (layer, 0, 0, 0)), const((DEPTH, D_POOL)),
                  _resident((2, 2, BLOCK, 1024))],
        out_specs=[row(D_POOL), row(D_POOL), row(D_ATTN), row(D_KV), row(D_KV), row(D_ATTN), row(D_MODEL),
                   row(D_ATTN)],
        out_shape=[jax.ShapeDtypeStruct((SEQ, D_POOL), F32), jax.ShapeDtypeStruct((SEQ, D_POOL), F32),
                   jax.ShapeDtypeStruct((SEQ, D_ATTN), BF16), jax.ShapeDtypeStruct((SEQ, D_KV), F32),
                   jax.ShapeDtypeStruct((SEQ, D_KV), F32), jax.ShapeDtypeStruct((SEQ, D_ATTN), F32),
                   jax.ShapeDtypeStruct((SEQ, D_MODEL), BF16), jax.ShapeDtypeStruct((SEQ, D_ATTN), F32)],
        scratch_shapes=[pltpu.VMEM((BLOCK, D_POOL), F32), pltpu.VMEM((BLOCK, D_KV), F32),
                        pltpu.VMEM((BLOCK, D_KV), F32)],
        compiler_params=_compiler_params(("arbitrary",)),
    )(sinks, x, norm_pre, w_in_t, token, pool_w, pool_scale, bias)


def _fwd_out(layer, z, x, norm_post, w_out):
    tm = TOKEN_TILE

    def body(z_ref, x_ref, g_ref, w_ref, xn_ref, y_ref):
        y = _nn(z_ref[...], w_ref[...])
        y_ref[...] = y
        r = lax.rsqrt(jnp.mean(y * y, axis=-1, keepdims=True) + EPS)
        xn_ref[...] = x_ref[...] + y * r * g_ref[layer:layer + 1, :]

    row = lambda c: pl.BlockSpec((tm, c), lambda i: (i, 0))
    return pl.pallas_call(
        body, name=f"fwd_out{layer}", grid=(SEQ // tm,),
        in_specs=[row(D_MODEL), row(D_MODEL), pl.BlockSpec((DEPTH, D_MODEL), lambda i: (0, 0)),
                  _resident((D_MODEL, D_MODEL))],
        out_specs=[row(D_MODEL), row(D_MODEL)],
        out_shape=[jax.ShapeDtypeStruct((SEQ, D_MODEL), F32), jax.ShapeDtypeStruct((SEQ, D_MODEL), F32)],
        compiler_params=_compiler_params(("arbitrary",)),
    )(z, x, norm_post, w_out)


def _bwd_out(layer, top, dxo_or_xf, target_or_token, y, z, norm_post, w_out):
    tm = TOKEN_TILE
    steps = SEQ // tm

    def body(*refs):
        if top:
            xf_ref, t_ref, y_ref, z_ref, g_ref, w_ref, dxo_ref, dz_ref, dw_ref, pack_ref, acc, dg, lacc = refs
        else:
            dxi_ref, _, y_ref, z_ref, g_ref, w_ref, dz_ref, dw_ref, pack_ref, acc, dg, lacc = refs
        i = pl.program_id(0)

        @pl.when(i == 0)
        def _():
            acc[...] = jnp.zeros_like(acc)
            dg[...] = jnp.zeros_like(dg)
            lacc[...] = jnp.zeros_like(lacc)
            pack_ref[...] = jnp.zeros_like(pack_ref)

        if top:
            d = xf_ref[...] - t_ref[...]
            dxo_v = d * (1.0 / D_MODEL)
            dxo_ref[...] = dxo_v
            part = jnp.sum(d * d, axis=-1, keepdims=True) * (1.0 / D_MODEL)
            lacc[...] += 0.5 * jnp.sum(part, axis=0, keepdims=True)
        else:
            dxo_v = dxi_ref[...]
        y = y_ref[...]
        r = lax.rsqrt(jnp.mean(y * y, axis=-1, keepdims=True) + EPS)
        yn = y * r
        dg[...] += jnp.sum(dxo_v * yn, axis=0, keepdims=True)
        dyn = dxo_v * g_ref[layer:layer + 1, :]
        dy = (r * (dyn - yn * jnp.mean(dyn * yn, axis=-1, keepdims=True))).astype(BF16)
        dz_ref[...] = _nt(dy, w_ref[...])
        acc[...] += _tn(z_ref[...], dy)

        @pl.when(i == steps - 1)
        def _():
            dw_ref[...] = acc[...].astype(BF16)
            _rows_of(dg, pack_ref, ROW_NPOST)
            lane = lax.broadcasted_iota(jnp.int32, (1, 128), 1)
            pack_ref[ROW_LOSS:ROW_LOSS + 1, :] = jnp.where(lane == 0, lacc[...], 0.0)

    row = lambda c: pl.BlockSpec((tm, c), lambda i: (i, 0))
    const = lambda shape: pl.BlockSpec(shape, lambda i: (0,) * len(shape))
    act = jax.ShapeDtypeStruct((SEQ, D_MODEL), F32)
    return pl.pallas_call(
        body, name=f"bwd_out{layer}", grid=(steps,),
        in_specs=([row(D_MODEL), row(D_MODEL) if top else const((8, 128))]
                  + [row(D_MODEL), row(D_MODEL), const((DEPTH, D_MODEL)), _resident((D_MODEL, D_MODEL))]),
        out_specs=([row(D_MODEL)] * (2 if top else 1) + [const((D_MODEL, D_MODEL)), const((PACK_ROWS, 128))]),
        out_shape=([act] * (2 if top else 1)
                   + [jax.ShapeDtypeStruct((D_MODEL, D_MODEL), BF16), jax.ShapeDtypeStruct((PACK_ROWS, 128), F32)]),
        scratch_shapes=[pltpu.VMEM((D_MODEL, D_MODEL), F32), pltpu.VMEM((1, D_MODEL), F32), pltpu.VMEM((1, 1), F32)],
        compiler_params=_compiler_params(("arbitrary",)),
    )(dxo_or_xf, target_or_token, y, z, norm_post, w_out)


def _bwd_mix(layer, sinks, dz, u, pg, q, k, v, ag, a, pool_w, pool_scale, bias, pack):
    last = N_BLOCKS - 1

    def body(sink_ref, dz_ref, u_ref, up_ref, pg_ref, q_ref, k_ref, v_ref, ag_ref, a_ref, pw_ref, sc_ref,
             bias_ref, pin_ref, dp_ref, pack_ref, ck, cv, ce):
        i = pl.program_id(0)
        n = last - i

        @pl.when(i == 0)
        def _():
            ck[...] = jnp.zeros_like(ck)
            cv[...] = jnp.zeros_like(cv)
            ce[...] = jnp.zeros_like(ce)
            pack_ref[...] = pin_ref[...]

        uv = u_ref[...]
        has_prev = (n > 0).astype(F32)
        ext = jnp.concatenate([up_ref[...] * has_prev, uv], axis=0)
        for g, w in enumerate(POOL_WINDOWS):
            cs = slice(BLOCK * g, BLOCK * (g + 1))
            inv = _inv_count(n, w)
            win = _window_sum(ext[:, cs], w, forward=False)[BLOCK:]
            pooled = win * inv - uv[:, cs]
            pw_g = pw_ref[g].astype(BF16)
            mixed = _nn(pooled.astype(BF16), pw_g)
            gate, dgate = _silu_parts(pg_ref[:, cs])
            dzp = dz_ref[:, cs]
            sc = sc_ref[layer:layer + 1, cs]
            dpm = dzp * gate
            dp_ref[:, COL_PG + BLOCK * g:COL_PG + BLOCK * (g + 1)] = (dzp * (mixed * sc) * dgate).astype(BF16)
            pack_ref[ROW_SC + g:ROW_SC + g + 1, :] += jnp.sum(dpm * mixed, axis=0, keepdims=True)
            dmixed = (dpm * sc).astype(BF16)
            pack_ref[ROW_PW + BLOCK * g:ROW_PW + BLOCK * (g + 1), :] += _tn(pooled.astype(BF16), dmixed)
            dpooled = _nt(dmixed, pw_g)
            e = dpooled * inv
            lead = _window_sum(jnp.concatenate([e, ce[:, cs]], axis=0), w, forward=True)[:BLOCK]
            dp_ref[:, COL_U + BLOCK * g:COL_U + BLOCK * (g + 1)] = (lead - dpooled).astype(BF16)
            ce[:, cs] = e

        kx = _kv_ext(k_ref, n)
        vx = _kv_ext(v_ref, n)
        lane = lax.broadcasted_iota(jnp.int32, (1, 128), 1)
        dsink_row = jnp.zeros((1, 128), F32)
        tks, tvs = [], []
        for kv in range(2):
            cs = slice(256 * kv, 256 * (kv + 1))
            bk = _block_diag(kx, kv)
            bv = _block_diag(vx, kv)
            qv = q_ref[:, cs]
            s = _nt(qv, bk) * SCALE + bias_ref[0, kv]
            gate, dgate = _silu_parts(ag_ref[:, cs])
            dza = dz_ref[:, D_POOL + 256 * kv:D_POOL + 256 * (kv + 1)]
            dp_ref[:, COL_AG + 256 * kv:COL_AG + 256 * (kv + 1)] = (dza * a_ref[:, cs] * dgate).astype(BF16)
            da = dza * gate
            dab = da.astype(BF16)
            dpr = _nt(dab, bv)
            ps, dss = [], []
            for g in range(GQA):
                gs = slice(256 * g, 256 * (g + 1))
                p, psink = _softmax_chunk(s[:, gs], sink_ref[layer, kv * GQA + g])
                dpg_ = dpr[:, gs]
                delta = jnp.sum(p * dpg_, axis=-1, keepdims=True)
                dsink = -jnp.sum(psink * delta, axis=0, keepdims=True)
                dsink_row = dsink_row + jnp.where(lane == kv * GQA + g, dsink, 0.0)
                ps.append(p)
                dss.append(p * (dpg_ - delta) * SCALE)
            p_all = jnp.concatenate(ps, axis=1)
            ds_all = jnp.concatenate(dss, axis=1)
            dp_ref[:, COL_Q + 256 * kv:COL_Q + 256 * (kv + 1)] = _nn(ds_all.astype(BF16), bk).astype(BF16)
            tks.append(_diag_fold(_tn(ds_all.astype(BF16), qv)))
            tvs.append(_diag_fold(_tn(p_all.astype(BF16), dab)))
        pack_ref[ROW_SINK:ROW_SINK + 1, :] += dsink_row
        lane2 = lax.broadcasted_iota(jnp.int32, (256, 128), 1)
        dkx = jnp.where(lane2 < 64, tks[0], tks[1])
        dvx = jnp.where(lane2 < 64, tvs[0], tvs[1])
        dp_ref[:, COL_K:COL_V] = (ck[...] + dkx[BLOCK:]).astype(BF16)
        dp_ref[:, COL_V:COL_AG] = (cv[...] + dvx[BLOCK:]).astype(BF16)
        ck[...] = dkx[:BLOCK]
        cv[...] = dvx[:BLOCK]

    blk = lambda c: pl.BlockSpec((BLOCK, c), lambda i: (last - i, 0))
    full = lambda shape: pl.BlockSpec(shape, lambda i: (0,) * len(shape))
    return pl.pallas_call(
        body, name=f"bwd_mix{layer}", grid=(N_BLOCKS,),
        in_specs=[pl.BlockSpec(memory_space=pltpu.SMEM), blk(D_MODEL), blk(D_POOL),
                  pl.BlockSpec((BLOCK, D_POOL), lambda i: (jnp.maximum(last - i - 1, 0), 0)),
                  blk(D_POOL), blk(D_ATTN), full((SEQ, D_KV)), full((SEQ, D_KV)), blk(D_ATTN), blk(D_ATTN),
                  pl.BlockSpec((None, 4, BLOCK, BLOCK), lambda i: (layer, 0, 0, 0)), full((DEPTH, D_POOL)),
                  pl.BlockSpec((1, 2, BLOCK, 1024), lambda i: (jnp.minimum(last - i, 1), 0, 0, 0)),
                  full((PACK_ROWS, 128))],
        out_specs=[blk(D_IN), full((PACK_ROWS, 128))],
        out_shape=[jax.ShapeDtypeStruct((SEQ, D_IN), BF16), jax.ShapeDtypeStruct((PACK_ROWS, 128), F32)],
        scratch_shapes=[pltpu.VMEM((BLOCK, D_KV), F32), pltpu.VMEM((BLOCK, D_KV), F32),
                        pltpu.VMEM((BLOCK, D_POOL), F32)],
        input_output_aliases={13: 1},
        compiler_params=_compiler_params(("arbitrary",)),
    )(sinks, dz, u, u, pg, q, k, v, ag, a, pool_w, pool_scale, bias, pack)


def _bwd_in(layer, part, token, dproj, x, norm_pre, dxo=None, w_in_t=None):
    want_dw, want_dx = part in ("both", "dw"), part in ("both", "dx")
    tm = TOKEN_TILE
    steps = SEQ // tm
    cw = 256

    def body(*refs):
        refs = list(refs)
        dp_ref, x_ref, g_ref = refs[1:4]
        del refs[:4]
        if want_dx:
            dxo_ref, w_ref, dx_ref, dgo_ref = refs[:4]
            del refs[:4]
            dg = refs.pop()
        if want_dw:
            dw_ref, acc = refs
        i = pl.program_id(0)

        @pl.when(i == 0)
        def _():
            if want_dw:
                acc[...] = jnp.zeros_like(acc)
            if want_dx:
                dg[...] = jnp.zeros_like(dg)

        xv = x_ref[...]
        gv = g_ref[layer:layer + 1, :]
        r = lax.rsqrt(jnp.mean(xv * xv, axis=-1, keepdims=True) + EPS)
        xn = xv * r
        if want_dw:
            hb = (xn * gv).astype(BF16)
            for c in range(0, D_IN, cw):
                acc[c:c + cw, :] += _tn(dp_ref[:, c:c + cw], hb)
        if want_dx:
            dh = _nn(dp_ref[...], w_ref[...])
            dg[...] += jnp.sum(dh * xn, axis=0, keepdims=True)
            dhn = dh * gv
            dx_ref[...] = dxo_ref[...] + r * (dhn - xn * jnp.mean(dhn * xn, axis=-1, keepdims=True))

        @pl.when(i == steps - 1)
        def _():
            if want_dw:
                dw_ref[...] = acc[...].astype(BF16)
            if want_dx:
                _rows_of(dg, dgo_ref, 0)

    row = lambda c: pl.BlockSpec((tm, c), lambda i: (i, 0))
    const = lambda shape: pl.BlockSpec(shape, lambda i: (0,) * len(shape))
    in_specs = [const((8, 128)), row(D_IN), row(D_MODEL), const((DEPTH, D_MODEL))]
    operands = [token, dproj, x, norm_pre]
    out_specs, out_shape, scratch = [], [], []
    if want_dx:
        in_specs += [row(D_MODEL), _resident((D_IN, D_MODEL))]
        operands += [dxo, w_in_t]
        out_specs += [row(D_MODEL), const((8, 128))]
        out_shape += [jax.ShapeDtypeStruct((SEQ, D_MODEL), F32), jax.ShapeDtypeStruct((8, 128), F32)]
    if want_dw:
        out_specs.append(const((D_IN, D_MODEL)))
        out_shape.append(jax.ShapeDtypeStruct((D_IN, D_MODEL), BF16))
        scratch.append(pltpu.VMEM((D_IN, D_MODEL), F32))
    if want_dx:
        scratch.append(pltpu.VMEM((1, D_MODEL), F32))
    return pl.pallas_call(
        body, name=f"bwd_in_{part}{layer}", grid=(steps,),
        in_specs=in_specs, out_specs=out_specs, out_shape=out_shape, scratch_shapes=scratch,
        compiler_params=_compiler_params(("arbitrary",)),
    )(*operands)


def _mesh_pos():
    return lax.axis_index("x"), lax.axis_index("y"), lax.axis_index("c")


def _device_rows(ref, m, px, py, pc):
    return ref.at[pl.ds(pl.multiple_of((4 * px + 2 * py + pc) * m, 16 if m % 16 == 0 else 8), m), :]


def _allgather(srcs, out_dtype, name, later=()):
    na, nb = len(srcs), len(later)
    every = list(srcs) + list(later)
    shapes = [(a.shape[-2], a.shape[-1]) for a, _ in every]

    def body(*refs):
        xs, refs = refs[:na + nb], refs[na + nb:]
        outs, cast, land, refs = refs[:na], refs[na:na + nb], refs[na + nb:na + 2 * nb], refs[na + 2 * nb:]
        stage, (send_sems, recv_sems, local_sems) = refs[:na], refs[na:]
        x, y, c = _mesh_pos()
        me, sibling = (x, y, c), (x, y, 1 - c)
        chips = [(1 - x, y), (x, 1 - y), (1 - x, 1 - y)]

        def slot(a, px, py, pc):
            return _device_rows(outs[a], shapes[a][0], px, py, pc)

        def copy(a, k, block, to, src=None):
            return pltpu.make_async_remote_copy(
                src_ref=slot(a, *block) if src is None else src, dst_ref=slot(a, *block),
                send_sem=send_sems.at[a, k], recv_sem=recv_sems.at[a, k], device_id=to, device_id_type=MESH)

        def cast_block(i):
            layer = every[i][1]
            return (xs[i][...] if layer is None else xs[i][layer]).astype(out_dtype)

        for a in range(na):
            stage[a][...] = cast_block(a)
        mine = [pltpu.make_async_copy(stage[a], slot(a, *me), local_sems.at[a]) for a in range(na)]
        for cp in mine:
            cp.start()
        first = []
        for a in range(na):
            first.append(copy(a, 0, me, sibling, src=stage[a]))
            first += [copy(a, 1 + j, me, (*chip, c), src=stage[a]) for j, chip in enumerate(chips)]
        for cp in first:
            cp.start()
        for b in range(nb):
            cast[b][...] = cast_block(na + b)
            cp = pltpu.make_async_copy(cast[b], _device_rows(land[b], shapes[na + b][0], *me), local_sems.at[na + b])
            cp.start()
            mine.append(cp)
        passed = []
        for a in range(na):
            for j, chip in enumerate(chips):
                copy(a, 1 + j, (*chip, c), me).wait_recv()
                cp = copy(a, 4 + j, (*chip, c), sibling)
                cp.start()
                passed.append(cp)
        for a in range(na):
            copy(a, 0, sibling, me).wait_recv()
            for j, chip in enumerate(chips):
                copy(a, 4 + j, (*chip, 1 - c), me).wait_recv()
        for cp in first + passed:
            cp.wait_send()
        for cp in mine:
            cp.wait()

    vmem = pl.BlockSpec(memory_space=pltpu.VMEM)
    gathered = [jax.ShapeDtypeStruct((N_DEV * m, n), out_dtype) for m, n in shapes]
    out = pl.pallas_call(
        body, name=name,
        in_specs=[vmem] * (na + nb),
        out_specs=[vmem] * (na + nb) + [pl.BlockSpec(memory_space=pl.ANY)] * nb,
        out_shape=gathered[:na] + [jax.ShapeDtypeStruct(s, out_dtype) for s in shapes[na:]] + gathered[na:],
        scratch_shapes=([pltpu.VMEM(s, out_dtype) for s in shapes[:na]]
                        + [pltpu.SemaphoreType.DMA((na, 7)), pltpu.SemaphoreType.DMA((na, 7)),
                           pltpu.SemaphoreType.DMA((na + nb,))]),
        compiler_params=_compiler_params(),
    )(*[a for a, _ in every])
    return out[:na], out[na:na + nb], out[na + nb:]


def _gather_start(blocks, lands, name):
    na = len(blocks)

    def body(*refs):
        src, land, sems, token = refs[:na], refs[na:2 * na], refs[2 * na:4 * na], refs[-1]
        x, y, c = _mesh_pos()
        for a in range(na):
            for k in range(1, N_DEV):
                peer = (x ^ ((k >> 2) & 1), y ^ ((k >> 1) & 1), c ^ (k & 1))
                pltpu.make_async_remote_copy(
                    src_ref=src[a], dst_ref=_device_rows(land[a], blocks[a].shape[0], x, y, c),
                    send_sem=sems[2 * a].at[k - 1], recv_sem=sems[2 * a + 1].at[k - 1],
                    device_id=peer, device_id_type=MESH).start()
        token[...] = jnp.zeros_like(token)

    bufs = [pltpu.HBM(t.shape, t.dtype) for t in list(blocks) + list(lands)]
    out = pl.pallas_call(
        body, name=name,
        out_shape=(*([pltpu.SemaphoreType.DMA((N_DEV - 1,))] * (2 * na)), *bufs, jax.ShapeDtypeStruct((8, 128), F32)),
        in_specs=[_HBM] * (2 * na),
        out_specs=(*([_SEM] * (2 * na)), *([_HBM] * (2 * na)), pl.BlockSpec(memory_space=pltpu.VMEM)),
        input_output_aliases={i: 2 * na + i for i in range(2 * na)},
        compiler_params=pltpu.CompilerParams(has_side_effects=_EFFECT),
    )(*[pltpu.with_memory_space_constraint(t, pltpu.HBM) for t in list(blocks) + list(lands)])
    sems = [(out[2 * a], out[2 * a + 1]) for a in range(na)]
    return sems, out[2 * na:3 * na], out[3 * na:4 * na], out[-1]


def _gather_wait(sems, block, land, after, name):
    def body(src, land_ref, send_sem, recv_sem, after_ref, src_out, land_out):
        x, y, c = _mesh_pos()
        for k in range(1, N_DEV):
            peer = (x ^ ((k >> 2) & 1), y ^ ((k >> 1) & 1), c ^ (k & 1))
            cp = pltpu.make_async_remote_copy(
                src_ref=src, dst_ref=_device_rows(land_ref, block.shape[0], *peer),
                send_sem=send_sem.at[k - 1], recv_sem=recv_sem.at[k - 1], device_id=peer, device_id_type=MESH)
            cp.wait_send()
            cp.wait_recv()

    out = pl.pallas_call(
        body, name=name,
        out_shape=(pltpu.HBM(block.shape, block.dtype), pltpu.HBM(land.shape, land.dtype)),
        in_specs=[_HBM, _HBM, _SEM, _SEM, pl.BlockSpec(memory_space=pl.ANY)],
        out_specs=[_HBM, _HBM],
        input_output_aliases={0: 0, 1: 1},
        compiler_params=pltpu.CompilerParams(has_side_effects=_EFFECT),
    )(block, land, sems[0], sems[1], after)
    return out[1]


def _row_step(m):
    return next(s for s in (32, 24, 16, 8) if m % s == 0)


def _pair_reduce(arrs, name):
    na = len(arrs)

    def body(*refs):
        gs, hs, hm = refs[:na], refs[na:2 * na], refs[2 * na:3 * na]
        own, ra = refs[3 * na:4 * na], refs[4 * na:5 * na]
        d2d_send, d2d_recv, local_sems = refs[5 * na:]
        x, y, c = _mesh_pos()
        sibling = (x, y, 1 - c)
        loads, sends = [], []
        for a in range(na):
            for q in range(4):
                cp = pltpu.make_async_copy(gs[a].at[2 * q + c], own[a].at[q], local_sems.at[a, q])
                cp.start()
                loads.append(cp)
                cp = pltpu.make_async_remote_copy(
                    src_ref=gs[a].at[2 * q + (1 - c)], dst_ref=ra[a].at[q], send_sem=d2d_send.at[a, q],
                    recv_sem=d2d_recv.at[a, q], device_id=sibling, device_id_type=MESH)
                cp.start()
                sends.append(cp)
        for cp in loads:
            cp.wait()
        for cp in sends:
            cp.wait_recv()
        others = [2 * (1 - x) + y, 2 * x + (1 - y), 2 * (1 - x) + (1 - y)]
        for a in range(na):
            m = arrs[a].shape[1]
            step = _row_step(m)

            def add(i, carry, a=a, step=step):
                rs = pl.ds(pl.multiple_of(i * step, step), step)
                for j, q in enumerate(others):
                    hs[a][j, rs, :] = (own[a][q, rs, :].astype(F32) + ra[a][q, rs, :].astype(F32)).astype(hs[a].dtype)
                q = 2 * x + y
                hm[a][rs, :] = own[a][q, rs, :].astype(F32) + ra[a][q, rs, :].astype(F32)
                return carry

            lax.fori_loop(0, m // step, add, 0)
        for cp in sends:
            cp.wait_send()

    vmem = pl.BlockSpec(memory_space=pltpu.VMEM)
    scratch = [pltpu.VMEM((4,) + t.shape[1:], t.dtype) for t in arrs] * 2
    scratch += [pltpu.SemaphoreType.DMA((na, 4)), pltpu.SemaphoreType.DMA((na, 4)), pltpu.SemaphoreType.DMA((na, 4))]
    out = pl.pallas_call(
        body, name=name,
        in_specs=[pl.BlockSpec(memory_space=pl.ANY)] * na, out_specs=[vmem] * (2 * na),
        out_shape=([jax.ShapeDtypeStruct((3,) + t.shape[1:], t.dtype) for t in arrs]
                   + [jax.ShapeDtypeStruct(t.shape[1:], F32) for t in arrs]),
        scratch_shapes=scratch,
        compiler_params=_compiler_params(),
    )(*arrs)
    return out[:na], out[na:]


_HBM = pl.BlockSpec(memory_space=pltpu.HBM)
_SEM = pl.BlockSpec(memory_space=pltpu.SEMAPHORE)
_EFFECT = pltpu.SideEffectType.DATAFLOW_SIDE_EFFECTING


def _exchange_plan(direct):
    x, y, c = _mesh_pos()
    if not direct:
        return [(j, j, (qx, qy, c)) for j, (qx, qy) in enumerate([(1 - x, y), (x, 1 - y), (1 - x, 1 - y)])]
    plan = []
    for k in range(1, N_DEV):
        px, py, pc = x ^ ((k >> 2) & 1), y ^ ((k >> 1) & 1), c ^ (k & 1)
        plan.append((4 * px + 2 * py + pc, k - 1, (px, py, pc)))
    return plan


def _exchange_start(srcs, direct, name):
    na = len(srcs)
    slots = N_DEV - 1 if direct else 3

    def body(*refs):
        src, land = refs[:na], refs[na:2 * na]
        send_sem, recv_sem = refs[2 * na], refs[2 * na + 1]
        token = refs[-1]
        for block, slot, peer in _exchange_plan(direct):
            for a in range(na):
                pltpu.make_async_remote_copy(
                    src_ref=src[a].at[block], dst_ref=land[a].at[slot], send_sem=send_sem.at[slots * a + slot],
                    recv_sem=recv_sem.at[slots * a + slot], device_id=peer, device_id_type=MESH).start()
        token[...] = jnp.zeros_like(token)

    zones = [jax.ShapeDtypeStruct((slots,) + t.shape[1:], t.dtype) for t in srcs]
    bufs = [pltpu.HBM(t.shape, t.dtype) for t in list(srcs) + zones]
    out = pl.pallas_call(
        body, name=name,
        out_shape=(pltpu.SemaphoreType.DMA((slots * na,)), pltpu.SemaphoreType.DMA((slots * na,)), *bufs,
                   jax.ShapeDtypeStruct((8, 128), F32)),
        in_specs=[_HBM] * (2 * na),
        out_specs=(_SEM, _SEM, *([_HBM] * (2 * na)), pl.BlockSpec(memory_space=pltpu.VMEM)),
        input_output_aliases={i: 2 + i for i in range(2 * na)},
        compiler_params=pltpu.CompilerParams(has_side_effects=_EFFECT),
    )(*[pltpu.with_memory_space_constraint(t, pltpu.HBM) for t in srcs],
      *[pltpu.with_memory_space_constraint(lax.empty(t.shape, t.dtype), pltpu.HBM) for t in zones])
    return out[0], out[1], out[2:2 + na], out[2 + na:2 + 2 * na], out[-1]


def _exchange_wait(send_sem, recv_sem, srcs, lands, direct, after, name):
    na = len(srcs)
    slots = N_DEV - 1 if direct else 3

    def body(*refs):
        src, land = refs[:na], refs[na:2 * na]
        send_sem_ref, recv_sem_ref = refs[2 * na], refs[2 * na + 1]
        for block, slot, peer in _exchange_plan(direct):
            for a in range(na):
                cp = pltpu.make_async_remote_copy(
                    src_ref=src[a].at[block], dst_ref=land[a].at[slot], send_sem=send_sem_ref.at[slots * a + slot],
                    recv_sem=recv_sem_ref.at[slots * a + slot], device_id=peer, device_id_type=MESH)
                cp.wait_send()
                cp.wait_recv()

    bufs = [pltpu.HBM(t.shape, t.dtype) for t in list(srcs) + list(lands)]
    out = pl.pallas_call(
        body, name=name,
        out_shape=tuple(bufs),
        in_specs=[_HBM] * (2 * na) + [_SEM, _SEM, pl.BlockSpec(memory_space=pl.ANY)],
        out_specs=[_HBM] * (2 * na),
        input_output_aliases={i: i for i in range(2 * na)},
        compiler_params=pltpu.CompilerParams(has_side_effects=_EFFECT),
    )(*srcs, *lands, send_sem, recv_sem, after)
    return out[:na], out[na:]


def _own_then_slots(mine_ref, lands_ref, rows=slice(None)):
    if len(mine_ref.shape) == 3:
        x, y, c = _mesh_pos()
        total = mine_ref[4 * x + 2 * y + c, rows, :].astype(F32)
    else:
        total = mine_ref[rows, :].astype(F32)
    for j in range(lands_ref.shape[0]):
        total = total + lands_ref[j, rows, :].astype(F32)
    return total


SMALL_ROWS = 2 * PACK_SLICE + 2 * 8


def _small_block(mine, lands, dgpre, name):
    def body(*refs):
        hm, ld, dg = refs[:DEPTH], refs[DEPTH:2 * DEPTH], refs[2 * DEPTH:3 * DEPTH]
        blk, land, sem = refs[3 * DEPTH:]
        for l in range(DEPTH):
            blk[PACK_SLICE * l:PACK_SLICE * (l + 1), :] = _own_then_slots(hm[l], ld[l])
            blk[2 * PACK_SLICE + 8 * l:2 * PACK_SLICE + 8 * (l + 1), :] = dg[l][...]
        cp = pltpu.make_async_copy(blk, _device_rows(land, SMALL_ROWS, *_mesh_pos()), sem)
        cp.start()
        cp.wait()

    vmem = pl.BlockSpec(memory_space=pltpu.VMEM)
    return pl.pallas_call(
        body, name=name,
        in_specs=[vmem] * (3 * DEPTH), out_specs=[vmem, pl.BlockSpec(memory_space=pl.ANY)],
        out_shape=[jax.ShapeDtypeStruct((SMALL_ROWS, 128), F32), jax.ShapeDtypeStruct((N_DEV * SMALL_ROWS, 128), F32)],
        scratch_shapes=[pltpu.SemaphoreType.DMA],
        compiler_params=_compiler_params(),
    )(*mine, *lands, *dgpre)


def _adamw_math(w, g, m, v):
    m = ADAM_B1 * m + (1.0 - ADAM_B1) * g
    v = ADAM_B2 * v + (1.0 - ADAM_B2) * (g * g)
    m_hat = m / (1.0 - ADAM_B1 ** ADAM_STEP)
    v_hat = v / (1.0 - ADAM_B2 ** ADAM_STEP)
    delta = -ADAM_LR * (m_hat / (jnp.sqrt(v_hat) + ADAM_EPS) + ADAM_WD * w)
    return delta, m, v


def _adamw_layer(layer, mine, lands, w, m, v, earlier, token, name, rows):
    _, mm, nn = w.shape

    def body(hm_ref, ld_ref, w_ref, m_ref, v_ref, _, *refs):
        g_ref, d_ref, nm_ref, nv_ref = refs[-4:]
        g = _own_then_slots(hm_ref, ld_ref)
        g_ref[...] = g
        d, nm, nv = _adamw_math(w_ref[...], g, m_ref[...], v_ref[...])
        d_ref[...] = d
        nm_ref[...] = nm
        nv_ref[...] = nv

    spec = pl.BlockSpec((None, rows, nn), lambda i: (layer, i, 0))
    carried = [] if earlier is None else list(earlier)
    return pl.pallas_call(
        body, name=name, grid=(mm // rows,),
        in_specs=([pl.BlockSpec((rows, nn), lambda i: (i, 0)) if mine.ndim == 2
                   else pl.BlockSpec((N_DEV, rows, nn), lambda i: (0, i, 0)),
                   pl.BlockSpec((lands.shape[0], rows, nn), lambda i: (0, i, 0)),
                   spec, spec, spec] + [pl.BlockSpec(memory_space=pl.ANY)] * (1 + len(carried))),
        out_specs=[spec] * 4,
        out_shape=[jax.ShapeDtypeStruct(w.shape, F32)] * 4,
        input_output_aliases={6 + t: t for t in range(len(carried))},
        compiler_params=_compiler_params(("arbitrary",)),
    )(mine, lands, w, m, v, token, *carried)


def _adamw_small(gathered, params):
    def body(all_ref, *refs):
        ins, outs, packs = refs[:15], refs[15:15 + 21], refs[15 + 21]
        loss_ref = outs[0]
        for dev in range(N_DEV):
            for l in range(DEPTH):
                packs[l, PACK_SLICE * dev:PACK_SLICE * (dev + 1), :] = (
                    all_ref[SMALL_ROWS * dev + PACK_SLICE * l:SMALL_ROWS * dev + PACK_SLICE * (l + 1), :])
        loss_ref[...] = packs[DEPTH - 1, ROW_LOSS:ROW_LOSS + 1, 0:1]

        def update(p, sel, g):
            w_ref, m_ref, v_ref = ins[p], ins[5 + p], ins[10 + p]
            d, nm, nv = _adamw_math(w_ref[sel], g, m_ref[sel], v_ref[sel])
            for t, val in enumerate((g, d, nm, nv)):
                outs[1 + 5 * t + p][sel] = val

        for l in range(DEPTH):
            gp = packs.at[l]
            row0 = 2 * PACK_SLICE + 8 * l
            dgpre = all_ref[row0:row0 + 8, :]
            for dev in range(1, N_DEV):
                dgpre = dgpre + all_ref[SMALL_ROWS * dev + row0:SMALL_ROWS * dev + row0 + 8, :]
            for grp in range(4):
                update(0, (l, grp), gp[ROW_PW + BLOCK * grp:ROW_PW + BLOCK * (grp + 1), :])
                update(1, (slice(l, l + 1), slice(128 * grp, 128 * (grp + 1))), gp[ROW_SC + grp:ROW_SC + grp + 1, :])
            update(2, (slice(l, l + 1), slice(None)), gp[ROW_SINK:ROW_SINK + 1, 0:N_HEADS])
            for r in range(D_MODEL // 128):
                sel = (slice(l, l + 1), slice(128 * r, 128 * (r + 1)))
                update(3, sel, dgpre[r:r + 1, :])
                update(4, sel, gp[ROW_NPOST + r:ROW_NPOST + r + 1, :])

    shapes = [jax.ShapeDtypeStruct(p.shape, F32) for p in params[:5]]
    return pl.pallas_call(
        body, name="adamw_small",
        out_shape=[jax.ShapeDtypeStruct((1, 1), F32)] + shapes * 4,
        scratch_shapes=[pltpu.VMEM((DEPTH, PACK_ROWS, 128), F32)],
        compiler_params=_compiler_params(),
    )(gathered, *params)


def kernel(x, w_in, pool_w, pool_scale, attn_sinks, w_out, norm_pre, norm_post, loss_target, m_w_in, m_pool_w, m_pool_scale, m_attn_sinks, m_w_out, m_norm_pre, m_norm_post, v_w_in, v_pool_w, v_pool_scale, v_attn_sinks, v_w_out, v_norm_pre, v_norm_post):
    x0 = x.reshape(SEQ, D_MODEL)
    target = loss_target.reshape(SEQ, D_MODEL)
    bias = jnp.asarray(_attn_bias())
    w_in_t, m_in_t, v_in_t = (jnp.swapaxes(t, 1, 2) for t in (w_in, m_w_in, v_w_in))

    (win0, wout0), later, lands = _allgather([(w_in_t, 0), (w_out, 0)], BF16, "gather_w0",
                                              later=[(w_in_t, 1), (w_out, 1)])
    sems, later, lands, token = _gather_start(later, lands, "gather_w1_start")
    win_full, wout_full = [win0, None], [wout0, None]

    saved = []
    xl = x0
    for layer in range(DEPTH):
        if layer > 0:
            win_full[layer] = _gather_wait(sems[0], later[0], lands[0], xl, "gather_w_in1_wait")
        u, pg, q, k, v, ag, z, a = _fwd_front(layer, xl, norm_pre, win_full[layer], token, attn_sinks,
                                             pool_w, pool_scale, bias)
        if layer > 0:
            wout_full[layer] = _gather_wait(sems[1], later[1], lands[1], z, "gather_w_out1_wait")
        x_next, y = _fwd_out(layer, z, xl, norm_post, wout_full[layer])
        saved.append((xl, u, pg, q, k, v, ag, z, a, y))
        xl = x_next

    params_small = [pool_w, pool_scale, attn_sinks, norm_pre, norm_post,
                    m_pool_w, m_pool_scale, m_attn_sinks, m_norm_pre, m_norm_post,
                    v_pool_w, v_pool_scale, v_attn_sinks, v_norm_pre, v_norm_post]
    exchange, dgpre = [None] * DEPTH, [None] * DEPTH
    dx = None
    for layer in reversed(range(DEPTH)):
        xin, u, pg, q, k, v, ag, z, a, y = saved[layer]
        if layer == DEPTH - 1:
            dx, dz, gw_out, pack = _bwd_out(layer, True, xl, target, y, z, norm_post, wout_full[layer])
        else:
            dz, gw_out, pack = _bwd_out(layer, False, dx, token, y, z, norm_post, wout_full[layer])
        dproj, pack = _bwd_mix(layer, attn_sinks, dz, u, pg, q, k, v, ag, a, pool_w, pool_scale, bias, pack)
        if layer > 0:
            dx, dgpre[layer], gw_in_t = _bwd_in(layer, "both", token, dproj, xin, norm_pre, dx, win_full[layer])
        else:
            (gw_in_t,) = _bwd_in(layer, "dw", token, dproj, xin, norm_pre)
        blocks = [gw_in_t.reshape(N_DEV, IN_SHARD, D_MODEL), gw_out.reshape(N_DEV, OUT_SHARD, D_MODEL),
                  pack.reshape(N_DEV, PACK_SLICE, 128)]
        direct = layer > 0
        srcs, mine = (blocks, None) if direct else _pair_reduce(blocks, f"pair_reduce{layer}")
        send_sem, recv_sem, srcs, lands, token = _exchange_start(srcs, direct, f"exchange_start{layer}")
        exchange[layer] = (send_sem, recv_sem, srcs, lands, mine)
        if layer == 0:
            dx, dgpre[layer] = _bwd_in(layer, "dx", token, dproj, xin, norm_pre, dx, win_full[layer])

    own, waited = [None] * DEPTH, [None] * DEPTH
    big_in, big_out, after = None, None, dx
    for layer in reversed(range(DEPTH)):
        send_sem, recv_sem, srcs, lands, mine = exchange[layer]
        srcs, waited[layer] = _exchange_wait(send_sem, recv_sem, srcs, lands, layer > 0, after, f"exchange_wait{layer}")
        own[layer] = srcs if mine is None else mine
        if layer > 0:
            big_in = _adamw_layer(layer, own[layer][0], waited[layer][0], w_in_t, m_in_t, v_in_t, big_in, token,
                                  f"adamw_in{layer}", 96)
            big_out = _adamw_layer(layer, own[layer][1], waited[layer][1], w_out, m_w_out, v_w_out, big_out, token,
                                   f"adamw_out{layer}", 128)
            after = big_out[0]
    block, land = _small_block([own[l][2] for l in range(DEPTH)], [waited[l][2] for l in range(DEPTH)],
                               dgpre, "small_block")
    sems, block, land, token = _gather_start([block], [land], "gather_small_start")
    big_in = _adamw_layer(0, own[0][0], waited[0][0], w_in_t, m_in_t, v_in_t, big_in, token, "adamw_in0", 96)
    big_out = _adamw_layer(0, own[0][1], waited[0][1], w_out, m_w_out, v_w_out, big_out, token, "adamw_out0", 128)
    gathered = _gather_wait(sems[0], block[0], land[0], big_out[0], "gather_small_wait")
    small_out = _adamw_small(gathered, params_small)
    loss = small_out[0].reshape(())

    outs = [loss, dx.reshape(1, SEQ, D_MODEL)]
    for t in range(4):
        pw_, sc_, sk_, npre_, npost_ = small_out[1 + 5 * t:6 + 5 * t]
        outs += [jnp.swapaxes(big_in[t], 1, 2), pw_, sc_, sk_, big_out[t], npre_, npost_]
    return tuple(outs)
```

```python
import numpy as np
import jax
import jax.numpy as jnp
from jax import lax
from jax.experimental import pallas as pl
from jax.experimental.pallas import tpu as pltpu

F32 = jnp.float32
BF16 = jnp.bfloat16

N_DEV = 8
SEQ = 2048
D_MODEL = 1024
D_POOL = 512
D_ATTN = 512
D_KV = 128
D_IN = 2304
N_HEADS = 8
GQA = 4
HEAD_DIM = 64
BLOCK = 128
N_BLOCKS = SEQ // BLOCK
POOL_WINDOWS = (2, 4, 8, 16)
DEPTH = 2
EPS = 1e-6
NEG_INF = -1e30
SCALE = HEAD_DIM ** -0.5
IN_SHARD = D_IN // N_DEV
OUT_SHARD = D_MODEL // N_DEV

COL_U, COL_PG, COL_Q, COL_K, COL_V, COL_AG = 0, 512, 1024, 1536, 1664, 1792

ADAM_LR = 0.001
ADAM_B1 = 0.9
ADAM_B2 = 0.999
ADAM_EPS = 1e-08
ADAM_WD = 0.01
ADAM_STEP = 10

TOKEN_TILE = 512
VMEM_LIMIT = 56 * 1024 * 1024
MESH = pl.DeviceIdType.MESH

ROW_PW, ROW_SC, ROW_SINK, ROW_NPRE, ROW_NPOST, ROW_LOSS = 0, 512, 520, 528, 536, 544
PACK_ROWS = 576
PACK_SLICE = PACK_ROWS // N_DEV


def _nn(a, b):
    return jnp.dot(a, b, preferred_element_type=F32)


def _nt(a, b):
    return lax.dot_general(a, b, (((1,), (1,)), ((), ())), preferred_element_type=F32)


def _tn(a, b):
    return lax.dot_general(a, b, (((0,), (0,)), ((), ())), preferred_element_type=F32)


def _silu_parts(g):
    s = jax.nn.sigmoid(g)
    return g * s, s * (1.0 + g * (1.0 - s))


def _resident(shape):
    return pl.BlockSpec(shape, lambda *_: (0,) * len(shape), pipeline_mode=pl.Buffered(1))


def _compiler_params(sem=None):
    if sem is None:
        return pltpu.CompilerParams(vmem_limit_bytes=VMEM_LIMIT)
    return pltpu.CompilerParams(dimension_semantics=sem, vmem_limit_bytes=VMEM_LIMIT)


def _attn_bias():
    t = np.arange(BLOCK)[:, None]
    j = np.arange(2 * BLOCK)[None, :]
    dist = t + BLOCK - j
    in_win = (dist >= 0) & (dist < BLOCK)
    out = np.zeros((2, 2, BLOCK, GQA * 2 * BLOCK), np.float32)
    for variant in range(2):
        valid = in_win & ((j >= BLOCK) | (variant == 1))
        for kv in range(2):
            for g in range(GQA):
                slope = np.float32(2.0 ** (-(kv * GQA + g + 1)))
                b = np.where(valid, -slope * dist.astype(np.float32), np.float32(NEG_INF))
                out[variant, kv, :, g * 256:(g + 1) * 256] = b
    return out


def _block_diag(kx, kv):
    rolled = pltpu.roll(kx, 64, 1)
    lane = lax.broadcasted_iota(jnp.int32, kx.shape, 1)
    dup = jnp.where(lane < 64, kx, rolled) if kv == 0 else jnp.where(lane < 64, rolled, kx)
    rep = jnp.concatenate([dup, dup], axis=1).astype(BF16)
    lane2 = lax.broadcasted_iota(jnp.int32, rep.shape, 1)
    zero = jnp.zeros_like(rep)
    parts = [jnp.where((lane2 >= 64 * g) & (lane2 < 64 * g + 64), rep, zero) for g in range(GQA)]
    return jnp.concatenate(parts, axis=0)


def _diag_fold(m):
    lane = lax.broadcasted_iota(jnp.int32, (256, 256), 1)
    r = jnp.where(lane < 64, m[0:256], jnp.where(lane < 128, m[256:512], jnp.where(lane < 192, m[512:768], m[768:1024])))
    h = r[:, 0:128] + r[:, 128:256]
    return h + pltpu.roll(h, 64, 1)


def _window_sum(ext, w, forward):
    s = ext
    sh = 1
    while sh < w:
        s = s + pltpu.roll(s, (256 - sh) if forward else sh, 0)
        sh *= 2
    return s


def _inv_count(n, w):
    t = n * BLOCK + lax.broadcasted_iota(jnp.int32, (BLOCK, 1), 0) + 1
    return 1.0 / jnp.minimum(t.astype(F32), float(w))


def _kv_ext(ref, n):
    r0 = pl.multiple_of(jnp.maximum(n - 1, 0) * BLOCK, BLOCK)
    r1 = pl.multiple_of(n * BLOCK, BLOCK)
    return jnp.concatenate([ref[pl.ds(r0, BLOCK), :], ref[pl.ds(r1, BLOCK), :]], axis=0)


def _softmax_chunk(sg, sink):
    m = jnp.maximum(jnp.max(sg, axis=-1, keepdims=True), sink)
    p = jnp.exp(sg - m)
    esink = jnp.exp(sink - m)
    rl = 1.0 / (jnp.sum(p, axis=-1, keepdims=True) + esink)
    return p * rl, esink * rl


def _rows_of(vec_ref, pack_ref, row0):
    for r in range(D_MODEL // 128):
        pack_ref[row0 + r:row0 + r + 1, :] = vec_ref[:, 128 * r:128 * (r + 1)]


FRONT_TILE = 2 * BLOCK


def _fwd_front(layer, x, norm_pre, w_in_t, token, sinks, pool_w, pool_scale, bias):
    tm = FRONT_TILE

    def body(sink_ref, x_ref, g_ref, w_ref, _, pw_ref, sc_ref, bias_ref,
             u_ref, pg_ref, q_ref, k_ref, v_ref, ag_ref, z_ref, a_ref, uprev, kprev, vprev):
        i = pl.program_id(0)

        @pl.when(i == 0)
        def _():
            uprev[...] = jnp.zeros_like(uprev)
            kprev[...] = jnp.zeros_like(kprev)
            vprev[...] = jnp.zeros_like(vprev)

        xv = x_ref[...]
        r = lax.rsqrt(jnp.mean(xv * xv, axis=-1, keepdims=True) + EPS)
        h = (xv * r * g_ref[layer:layer + 1, :]).astype(BF16)
        u_ref[...] = _nt(h, w_ref[COL_U:COL_PG, :])
        pg_ref[...] = _nt(h, w_ref[COL_PG:COL_Q, :])
        for sb in range(tm // BLOCK):
            n = (tm // BLOCK) * i + sb
            rows = slice(BLOCK * sb, BLOCK * (sb + 1))
            before = slice(BLOCK * (sb - 1), BLOCK * sb)
            uv = u_ref[rows, :]
            ext = jnp.concatenate([uprev[...] if sb == 0 else u_ref[before, :], uv], axis=0)
            for g, w in enumerate(POOL_WINDOWS):
                cs = slice(BLOCK * g, BLOCK * (g + 1))
                win = _window_sum(ext[:, cs], w, forward=False)[BLOCK:]
                pooled = win * _inv_count(n, w) - uv[:, cs]
                mixed = _nn(pooled.astype(BF16), pw_ref[g].astype(BF16))
                gate, _ = _silu_parts(pg_ref[rows, cs])
                z_ref[rows, cs] = (mixed * sc_ref[layer:layer + 1, cs] * gate).astype(BF16)

        q_ref[...] = _nt(h, w_ref[COL_Q:COL_K, :]).astype(BF16)
        k_ref[...] = _nt(h, w_ref[COL_K:COL_V, :])
        v_ref[...] = _nt(h, w_ref[COL_V:COL_AG, :])
        ag_ref[...] = _nt(h, w_ref[COL_AG:D_IN, :])

        for sb in range(tm // BLOCK):
            n = (tm // BLOCK) * i + sb
            rows = slice(BLOCK * sb, BLOCK * (sb + 1))
            before = slice(BLOCK * (sb - 1), BLOCK * sb)
            kx = jnp.concatenate([kprev[...] if sb == 0 else k_ref[before, :], k_ref[rows, :]], axis=0)
            vx = jnp.concatenate([vprev[...] if sb == 0 else v_ref[before, :], v_ref[rows, :]], axis=0)
            variant = jnp.minimum(n, 1) if sb == 0 else 1
            for kv in range(2):
                cs = slice(256 * kv, 256 * (kv + 1))
                bk = _block_diag(kx, kv)
                bv = _block_diag(vx, kv)
                s = _nt(q_ref[rows, cs], bk) * SCALE + bias_ref[variant, kv]
                ps = []
                for g in range(GQA):
                    p, _ = _softmax_chunk(s[:, 256 * g:256 * (g + 1)], sink_ref[layer, kv * GQA + g])
                    ps.append(p.astype(BF16))
                o = _nn(jnp.concatenate(ps, axis=1), bv)
                a_ref[rows, cs] = o
                gate, _ = _silu_parts(ag_ref[rows, cs])
                z_ref[rows, D_POOL + 256 * kv:D_POOL + 256 * (kv + 1)] = (o * gate).astype(BF16)

        tail = slice(tm - BLOCK, tm)
        uprev[...] = u_ref[tail, :]
        kprev[...] = k_ref[tail, :]
        vprev[...] = v_ref[tail, :]

    row = lambda c: pl.BlockSpec((tm, c), lambda i: (i, 0))
    const = lambda shape: pl.BlockSpec(shape, lambda i: (0,) * len(shape))
    return pl.pallas_call(
        body, name=f"fwd_front{layer}", grid=(SEQ // tm,),
        in_specs=[pl.BlockSpec(memory_space=pltpu.SMEM), row(D_MODEL), const((DEPTH, D_MODEL)),
                  _resident((D_IN, D_MODEL)), const((8, 128)),
                  pl.BlockSpec((None, 4, BLOCK, BLOCK), lambda i: (layer, 0, 0, 0)), const((DEPTH, D_POOL)),
                  _resident((2, 2, BLOCK, 1024))],
        out_specs=[row(D_POOL), row(D_POOL), row(D_ATTN), row(D_KV), row(D_KV), row(D_ATTN), row(D_MODEL),
                   row(D_ATTN)],
        out_shape=[jax.ShapeDtypeStruct((SEQ, D_POOL), F32), jax.ShapeDtypeStruct((SEQ, D_POOL), F32),
                   jax.ShapeDtypeStruct((SEQ, D_ATTN), BF16), jax.ShapeDtypeStruct((SEQ, D_KV), F32),
                   jax.ShapeDtypeStruct((SEQ, D_KV), F32), jax.ShapeDtypeStruct((SEQ, D_ATTN), F32),
                   jax.ShapeDtypeStruct((SEQ, D_MODEL), BF16), jax.ShapeDtypeStruct((SEQ, D_ATTN), F32)],
        scratch_shapes=[pltpu.VMEM((BLOCK, D_POOL), F32), pltpu.VMEM((BLOCK, D_KV), F32),
                        pltpu.VMEM((BLOCK, D_KV), F32)],
        compiler_params=_compiler_params(("arbitrary",)),
    )(sinks, x, norm_pre, w_in_t, token, pool_w, pool_scale, bias)


def _fwd_out(layer, z, x, norm_post, w_out):
    tm = TOKEN_TILE

    def body(z_ref, x_ref, g_ref, w_ref, xn_ref, y_ref):
        y = _nn(z_ref[...], w_ref[...])
        y_ref[...] = y
        r = lax.rsqrt(jnp.mean(y * y, axis=-1, keepdims=True) + EPS)
        xn_ref[...] = x_ref[...] + y * r * g_ref[layer:layer + 1, :]

    row = lambda c: pl.BlockSpec((tm, c), lambda i: (i, 0))
    return pl.pallas_call(
        body, name=f"fwd_out{layer}", grid=(SEQ // tm,),
        in_specs=[row(D_MODEL), row(D_MODEL), pl.BlockSpec((DEPTH, D_MODEL), lambda i: (0, 0)),
                  _resident((D_MODEL, D_MODEL))],
        out_specs=[row(D_MODEL), row(D_MODEL)],
        out_shape=[jax.ShapeDtypeStruct((SEQ, D_MODEL), F32), jax.ShapeDtypeStruct((SEQ, D_MODEL), F32)],
        compiler_params=_compiler_params(("arbitrary",)),
    )(z, x, norm_post, w_out)


def _bwd_out(layer, top, dxo_or_xf, target_or_token, y, z, norm_post, w_out):
    tm = TOKEN_TILE
    steps = SEQ // tm

    def body(*refs):
        if top:
            xf_ref, t_ref, y_ref, z_ref, g_ref, w_ref, dxo_ref, dz_ref, dw_ref, pack_ref, acc, dg, lacc = refs
        else:
            dxi_ref, _, y_ref, z_ref, g_ref, w_ref, dz_ref, dw_ref, pack_ref, acc, dg, lacc = refs
        i = pl.program_id(0)

        @pl.when(i == 0)
        def _():
            acc[...] = jnp.zeros_like(acc)
            dg[...] = jnp.zeros_like(dg)
            lacc[...] = jnp.zeros_like(lacc)
            pack_ref[...] = jnp.zeros_like(pack_ref)

        if top:
            d = xf_ref[...] - t_ref[...]
            dxo_v = d * (1.0 / D_MODEL)
            dxo_ref[...] = dxo_v
            part = jnp.sum(d * d, axis=-1, keepdims=True) * (1.0 / D_MODEL)
            lacc[...] += 0.5 * jnp.sum(part, axis=0, keepdims=True)
        else:
            dxo_v = dxi_ref[...]
        y = y_ref[...]
        r = lax.rsqrt(jnp.mean(y * y, axis=-1, keepdims=True) + EPS)
        yn = y * r
        dg[...] += jnp.sum(dxo_v * yn, axis=0, keepdims=True)
        dyn = dxo_v * g_ref[layer:layer + 1, :]
        dy = (r * (dyn - yn * jnp.mean(dyn * yn, axis=-1, keepdims=True))).astype(BF16)
        dz_ref[...] = _nt(dy, w_ref[...])
        acc[...] += _tn(z_ref[...], dy)

        @pl.when(i == steps - 1)
        def _():
            dw_ref[...] = acc[...].astype(BF16)
            _rows_of(dg, pack_ref, ROW_NPOST)
            lane = lax.broadcasted_iota(jnp.int32, (1, 128), 1)
            pack_ref[ROW_LOSS:ROW_LOSS + 1, :] = jnp.where(lane == 0, lacc[...], 0.0)

    row = lambda c: pl.BlockSpec((tm, c), lambda i: (i, 0))
    const = lambda shape: pl.BlockSpec(shape, lambda i: (0,) * len(shape))
    act = jax.ShapeDtypeStruct((SEQ, D_MODEL), F32)
    return pl.pallas_call(
        body, name=f"bwd_out{layer}", grid=(steps,),
        in_specs=([row(D_MODEL), row(D_MODEL) if top else const((8, 128))]
                  + [row(D_MODEL), row(D_MODEL), const((DEPTH, D_MODEL)), _resident((D_MODEL, D_MODEL))]),
        out_specs=([row(D_MODEL)] * (2 if top else 1) + [const((D_MODEL, D_MODEL)), const((PACK_ROWS, 128))]),
        out_shape=([act] * (2 if top else 1)
                   + [jax.ShapeDtypeStruct((D_MODEL, D_MODEL), BF16), jax.ShapeDtypeStruct((PACK_ROWS, 128), F32)]),
        scratch_shapes=[pltpu.VMEM((D_MODEL, D_MODEL), F32), pltpu.VMEM((1, D_MODEL), F32), pltpu.VMEM((1, 1), F32)],
        compiler_params=_compiler_params(("arbitrary",)),
    )(dxo_or_xf, target_or_token, y, z, norm_post, w_out)


def _bwd_mix(layer, sinks, dz, u, pg, q, k, v, ag, a, pool_w, pool_scale, bias, pack):
    last = N_BLOCKS - 1

    def body(sink_ref, dz_ref, u_ref, up_ref, pg_ref, q_ref, k_ref, v_ref, ag_ref, a_ref, pw_ref, sc_ref,
             bias_ref, pin_ref, dp_ref, pack_ref, ck, cv, ce):
        i = pl.program_id(0)
        n = last - i

        @pl.when(i == 0)
        def _():
            ck[...] = jnp.zeros_like(ck)
            cv[...] = jnp.zeros_like(cv)
            ce[...] = jnp.zeros_like(ce)
            pack_ref[...] = pin_ref[...]

        uv = u_ref[...]
        has_prev = (n > 0).astype(F32)
        ext = jnp.concatenate([up_ref[...] * has_prev, uv], axis=0)
        for g, w in enumerate(POOL_WINDOWS):
            cs = slice(BLOCK * g, BLOCK * (g + 1))
            inv = _inv_count(n, w)
            win = _window_sum(ext[:, cs], w, forward=False)[BLOCK:]
            pooled = win * inv - uv[:, cs]
            pw_g = pw_ref[g].astype(BF16)
            mixed = _nn(pooled.astype(BF16), pw_g)
            gate, dgate = _silu_parts(pg_ref[:, cs])
            dzp = dz_ref[:, cs]
            sc = sc_ref[layer:layer + 1, cs]
            dpm = dzp * gate
            dp_ref[:, COL_PG + BLOCK * g:COL_PG + BLOCK * (g + 1)] = (dzp * (mixed * sc) * dgate).astype(BF16)
            pack_ref[ROW_SC + g:ROW_SC + g + 1, :] += jnp.sum(dpm * mixed, axis=0, keepdims=True)
            dmixed = (dpm * sc).astype(BF16)
            pack_ref[ROW_PW + BLOCK * g:ROW_PW + BLOCK * (g + 1), :] += _tn(pooled.astype(BF16), dmixed)
            dpooled = _nt(dmixed, pw_g)
            e = dpooled * inv
            lead = _window_sum(jnp.concatenate([e, ce[:, cs]], axis=0), w, forward=True)[:BLOCK]
            dp_ref[:, COL_U + BLOCK * g:COL_U + BLOCK * (g + 1)] = (lead - dpooled).astype(BF16)
            ce[:, cs] = e

        kx = _kv_ext(k_ref, n)
        vx = _kv_ext(v_ref, n)
        lane = lax.broadcasted_iota(jnp.int32, (1, 128), 1)
        dsink_row = jnp.zeros((1, 128), F32)
        tks, tvs = [], []
        for kv in range(2):
            cs = slice(256 * kv, 256 * (kv + 1))
            bk = _block_diag(kx, kv)
            bv = _block_diag(vx, kv)
            qv = q_ref[:, cs]
            s = _nt(qv, bk) * SCALE + bias_ref[0, kv]
            gate, dgate = _silu_parts(ag_ref[:, cs])
            dza = dz_ref[:, D_POOL + 256 * kv:D_POOL + 256 * (kv + 1)]
            dp_ref[:, COL_AG + 256 * kv:COL_AG + 256 * (kv + 1)] = (dza * a_ref[:, cs] * dgate).astype(BF16)
            da = dza * gate
            dab = da.astype(BF16)
            dpr = _nt(dab, bv)
            ps, dss = [], []
            for g in range(GQA):
                gs = slice(256 * g, 256 * (g + 1))
                p, psink = _softmax_chunk(s[:, gs], sink_ref[layer, kv * GQA + g])
                dpg_ = dpr[:, gs]
                delta = jnp.sum(p * dpg_, axis=-1, keepdims=True)
                dsink = -jnp.sum(psink * delta, axis=0, keepdims=True)
                dsink_row = dsink_row + jnp.where(lane == kv * GQA + g, dsink, 0.0)
                ps.append(p)
                dss.append(p * (dpg_ - delta) * SCALE)
            p_all = jnp.concatenate(ps, axis=1)
            ds_all = jnp.concatenate(dss, axis=1)
            dp_ref[:, COL_Q + 256 * kv:COL_Q + 256 * (kv + 1)] = _nn(ds_all.astype(BF16), bk).astype(BF16)
            tks.append(_diag_fold(_tn(ds_all.astype(BF16), qv)))
            tvs.append(_diag_fold(_tn(p_all.astype(BF16), dab)))
        pack_ref[ROW_SINK:ROW_SINK + 1, :] += dsink_row
        lane2 = lax.broadcasted_iota(jnp.int32, (256, 128), 1)
        dkx = jnp.where(lane2 < 64, tks[0], tks[1])
        dvx = jnp.where(lane2 < 64, tvs[0], tvs[1])
        dp_ref[:, COL_K:COL_V] = (ck[...] + dkx[BLOCK:]).astype(BF16)
        dp_ref[:, COL_V:COL_AG] = (cv[...] + dvx[BLOCK:]).astype(BF16)
        ck[...] = dkx[:BLOCK]
        cv[...] = dvx[:BLOCK]

    blk = lambda c: pl.BlockSpec((BLOCK, c), lambda i: (last - i, 0))
    full = lambda shape: pl.BlockSpec(shape, lambda i: (0,) * len(shape))
    return pl.pallas_call(
        body, name=f"bwd_mix{layer}", grid=(N_BLOCKS,),
        in_specs=[pl.BlockSpec(memory_space=pltpu.SMEM), blk(D_MODEL), blk(D_POOL),
                  pl.BlockSpec((BLOCK, D_POOL), lambda i: (jnp.maximum(last - i - 1, 0), 0)),
                  blk(D_POOL), blk(D_ATTN), full((SEQ, D_KV)), full((SEQ, D_KV)), blk(D_ATTN), blk(D_ATTN),
                  pl.BlockSpec((None, 4, BLOCK, BLOCK), lambda i: (layer, 0, 0, 0)), full((DEPTH, D_POOL)),
                  pl.BlockSpec((1, 2, BLOCK, 1024), lambda i: (jnp.minimum(last - i, 1), 0, 0, 0)),
                  full((PACK_ROWS, 128))],
        out_specs=[blk(D_IN), full((PACK_ROWS, 128))],
        out_shape=[jax.ShapeDtypeStruct((SEQ, D_IN), BF16), jax.ShapeDtypeStruct((PACK_ROWS, 128), F32)],
        scratch_shapes=[pltpu.VMEM((BLOCK, D_KV), F32), pltpu.VMEM((BLOCK, D_KV), F32),
                        pltpu.VMEM((BLOCK, D_POOL), F32)],
        input_output_aliases={13: 1},
        compiler_params=_compiler_params(("arbitrary",)),
    )(sinks, dz, u, u, pg, q, k, v, ag, a, pool_w, pool_scale, bias, pack)


def _bwd_in(layer, part, token, dproj, x, norm_pre, dxo=None, w_in_t=None):
    want_dw, want_dx = part in ("both", "dw"), part in ("both", "dx")
    tm = TOKEN_TILE
    steps = SEQ // tm
    cw = 256

    def body(*refs):
        refs = list(refs)
        dp_ref, x_ref, g_ref = refs[1:4]
        del refs[:4]
        if want_dx:
            dxo_ref, w_ref, dx_ref, dgo_ref = refs[:4]
            del refs[:4]
            dg = refs.pop()
        if want_dw:
            dw_ref, acc = refs
        i = pl.program_id(0)

        @pl.when(i == 0)
        def _():
            if want_dw:
                acc[...] = jnp.zeros_like(acc)
            if want_dx:
                dg[...] = jnp.zeros_like(dg)

        xv = x_ref[...]
        gv = g_ref[layer:layer + 1, :]
        r = lax.rsqrt(jnp.mean(xv * xv, axis=-1, keepdims=True) + EPS)
        xn = xv * r
        if want_dw:
            hb = (xn * gv).astype(BF16)
            for c in range(0, D_IN, cw):
                acc[c:c + cw, :] += _tn(dp_ref[:, c:c + cw], hb)
        if want_dx:
            dh = _nn(dp_ref[...], w_ref[...])
            dg[...] += jnp.sum(dh * xn, axis=0, keepdims=True)
            dhn = dh * gv
            dx_ref[...] = dxo_ref[...] + r * (dhn - xn * jnp.mean(dhn * xn, axis=-1, keepdims=True))

        @pl.when(i == steps - 1)
        def _():
            if want_dw:
                dw_ref[...] = acc[...].astype(BF16)
            if want_dx:
                _rows_of(dg, dgo_ref, 0)

    row = lambda c: pl.BlockSpec((tm, c), lambda i: (i, 0))
    const = lambda shape: pl.BlockSpec(shape, lambda i: (0,) * len(shape))
    in_specs = [const((8, 128)), row(D_IN), row(D_MODEL), const((DEPTH, D_MODEL))]
    operands = [token, dproj, x, norm_pre]
    out_specs, out_shape, scratch = [], [], []
    if want_dx:
        in_specs += [row(D_MODEL), _resident((D_IN, D_MODEL))]
        operands += [dxo, w_in_t]
        out_specs += [row(D_MODEL), const((8, 128))]
        out_shape += [jax.ShapeDtypeStruct((SEQ, D_MODEL), F32), jax.ShapeDtypeStruct((8, 128), F32)]
    if want_dw:
        out_specs.append(const((D_IN, D_MODEL)))
        out_shape.append(jax.ShapeDtypeStruct((D_IN, D_MODEL), BF16))
        scratch.append(pltpu.VMEM((D_IN, D_MODEL), F32))
    if want_dx:
        scratch.append(pltpu.VMEM((1, D_MODEL), F32))
    return pl.pallas_call(
        body, name=f"bwd_in_{part}{layer}", grid=(steps,),
        in_specs=in_specs, out_specs=out_specs, out_shape=out_shape, scratch_shapes=scratch,
        compiler_params=_compiler_params(("arbitrary",)),
    )(*operands)


def _mesh_pos():
    return lax.axis_index("x"), lax.axis_index("y"), lax.axis_index("c")


def _device_rows(ref, m, px, py, pc):
    return ref.at[pl.ds(pl.multiple_of((4 * px + 2 * py + pc) * m, 16 if m % 16 == 0 else 8), m), :]


def _allgather(srcs, out_dtype, name, later=()):
    na, nb = len(srcs), len(later)
    every = list(srcs) + list(later)
    shapes = [(a.shape[-2], a.shape[-1]) for a, _ in every]

    def body(*refs):
        xs, refs = refs[:na + nb], refs[na + nb:]
        outs, cast, land, refs = refs[:na], refs[na:na + nb], refs[na + nb:na + 2 * nb], refs[na + 2 * nb:]
        stage, (send_sems, recv_sems, local_sems) = refs[:na], refs[na:]
        x, y, c = _mesh_pos()
        me, sibling = (x, y, c), (x, y, 1 - c)
        near = [(1 - x, y), (x, 1 - y)]
        far = (1 - x, 1 - y)
        relay_from, relay_to = (x ^ (1 - c), y ^ c), (x ^ c, y ^ (1 - c))
        k_from, k_to = 1 + c, 2 - c

        def slot(a, px, py, pc):
            return _device_rows(outs[a], shapes[a][0], px, py, pc)

        def copy(a, k, block, to, src=None):
            return pltpu.make_async_remote_copy(
                src_ref=slot(a, *block) if src is None else src, dst_ref=slot(a, *block),
                send_sem=send_sems.at[a, k], recv_sem=recv_sems.at[a, k], device_id=to, device_id_type=MESH)

        def cast_block(i):
            layer = every[i][1]
            return (xs[i][...] if layer is None else xs[i][layer]).astype(out_dtype)

        for a in range(na):
            stage[a][...] = cast_block(a)
        mine = [pltpu.make_async_copy(stage[a], slot(a, *me), local_sems.at[a]) for a in range(na)]
        for cp in mine:
            cp.start()
        sent = []
        for a in range(na):
            sent.append(copy(a, 0, me, sibling, src=stage[a]))
            sent += [copy(a, 1 + j, me, (*chip, c), src=stage[a]) for j, chip in enumerate(near)]
        for cp in sent:
            cp.start()
        for b in range(nb):
            cast[b][...] = cast_block(na + b)
            cp = pltpu.make_async_copy(cast[b], _device_rows(land[b], shapes[na + b][0], *me), local_sems.at[na + b])
            cp.start()
            mine.append(cp)
        for a in range(na):
            copy(a, k_from, (*relay_from, c), me).wait_recv()
            sent += [copy(a, 3, (*relay_from, c), (*relay_to, c)), copy(a, 3 + k_from, (*relay_from, c), sibling)]
            sent[-2].start()
            sent[-1].start()
        for a in range(na):
            copy(a, k_to, (*relay_to, c), me).wait_recv()
            sent.append(copy(a, 3 + k_to, (*relay_to, c), sibling))
            sent[-1].start()
        for a in range(na):
            copy(a, 3, (*far, c), me).wait_recv()
            sent.append(copy(a, 6, (*far, c), sibling))
            sent[-1].start()
        for a in range(na):
            copy(a, 0, sibling, me).wait_recv()
            for j, chip in enumerate(near + [far]):
                copy(a, 4 + j, (*chip, 1 - c), me).wait_recv()
        for cp in sent:
            cp.wait_send()
        for cp in mine:
            cp.wait()

    vmem = pl.BlockSpec(memory_space=pltpu.VMEM)
    hbm = pl.BlockSpec(memory_space=pl.ANY)
    gathered = [jax.ShapeDtypeStruct((N_DEV * m, n), out_dtype) for m, n in shapes]
    out = pl.pallas_call(
        body, name=name,
        in_specs=[vmem] * (na + nb),
        out_specs=[hbm] * na + [vmem] * nb + [hbm] * nb,
        out_shape=gathered[:na] + [jax.ShapeDtypeStruct(s, out_dtype) for s in shapes[na:]] + gathered[na:],
        scratch_shapes=([pltpu.VMEM(s, out_dtype) for s in shapes[:na]]
                        + [pltpu.SemaphoreType.DMA((na, 7)), pltpu.SemaphoreType.DMA((na, 7)),
                           pltpu.SemaphoreType.DMA((na + nb,))]),
        compiler_params=_compiler_params(),
    )(*[a for a, _ in every])
    return out[:na], out[na:na + nb], out[na + nb:]


def _gather_start(blocks, lands, name):
    na = len(blocks)

    def body(*refs):
        src, land, sems, token = refs[:na], refs[na:2 * na], refs[2 * na:4 * na], refs[-1]
        x, y, c = _mesh_pos()
        for a in range(na):
            for k in range(1, N_DEV):
                peer = (x ^ ((k >> 2) & 1), y ^ ((k >> 1) & 1), c ^ (k & 1))
                pltpu.make_async_remote_copy(
                    src_ref=src[a], dst_ref=_device_rows(land[a], blocks[a].shape[0], x, y, c),
                    send_sem=sems[2 * a].at[k - 1], recv_sem=sems[2 * a + 1].at[k - 1],
                    device_id=peer, device_id_type=MESH).start()
        token[...] = jnp.zeros_like(token)

    bufs = [pltpu.HBM(t.shape, t.dtype) for t in list(blocks) + list(lands)]
    out = pl.pallas_call(
        body, name=name,
        out_shape=(*([pltpu.SemaphoreType.DMA((N_DEV - 1,))] * (2 * na)), *bufs, jax.ShapeDtypeStruct((8, 128), F32)),
        in_specs=[_HBM] * (2 * na),
        out_specs=(*([_SEM] * (2 * na)), *([_HBM] * (2 * na)), pl.BlockSpec(memory_space=pltpu.VMEM)),
        input_output_aliases={i: 2 * na + i for i in range(2 * na)},
        compiler_params=pltpu.CompilerParams(has_side_effects=_EFFECT),
    )(*[pltpu.with_memory_space_constraint(t, pltpu.HBM) for t in list(blocks) + list(lands)])
    sems = [(out[2 * a], out[2 * a + 1]) for a in range(na)]
    return sems, out[2 * na:3 * na], out[3 * na:4 * na], out[-1]


def _gather_wait(sems, block, land, after, name):
    def body(src, land_ref, send_sem, recv_sem, after_ref, src_out, land_out):
        x, y, c = _mesh_pos()
        for k in range(1, N_DEV):
            peer = (x ^ ((k >> 2) & 1), y ^ ((k >> 1) & 1), c ^ (k & 1))
            cp = pltpu.make_async_remote_copy(
                src_ref=src, dst_ref=_device_rows(land_ref, block.shape[0], *peer),
                send_sem=send_sem.at[k - 1], recv_sem=recv_sem.at[k - 1], device_id=peer, device_id_type=MESH)
            cp.wait_send()
            cp.wait_recv()

    out = pl.pallas_call(
        body, name=name,
        out_shape=(pltpu.HBM(block.shape, block.dtype), pltpu.HBM(land.shape, land.dtype)),
        in_specs=[_HBM, _HBM, _SEM, _SEM, pl.BlockSpec(memory_space=pl.ANY)],
        out_specs=[_HBM, _HBM],
        input_output_aliases={0: 0, 1: 1},
        compiler_params=pltpu.CompilerParams(has_side_effects=_EFFECT),
    )(block, land, sems[0], sems[1], after)
    return out[1]


def _row_step(m):
    return next(s for s in (32, 24, 16, 8) if m % s == 0)


def _pair_reduce(arrs, name):
    na = len(arrs)

    def body(*refs):
        gs, hs, hm = refs[:na], refs[na:2 * na], refs[2 * na:3 * na]
        own, ra = refs[3 * na:4 * na], refs[4 * na:5 * na]
        d2d_send, d2d_recv, local_sems = refs[5 * na:]
        x, y, c = _mesh_pos()
        sibling = (x, y, 1 - c)
        loads, sends = [], []
        for a in range(na):
            for q in range(4):
                cp = pltpu.make_async_copy(gs[a].at[2 * q + c], own[a].at[q], local_sems.at[a, q])
                cp.start()
                loads.append(cp)
                cp = pltpu.make_async_remote_copy(
                    src_ref=gs[a].at[2 * q + (1 - c)], dst_ref=ra[a].at[q], send_sem=d2d_send.at[a, q],
                    recv_sem=d2d_recv.at[a, q], device_id=sibling, device_id_type=MESH)
                cp.start()
                sends.append(cp)
        for cp in loads:
            cp.wait()
        for cp in sends:
            cp.wait_recv()
        others = [2 * (1 - x) + y, 2 * x + (1 - y), 2 * (1 - x) + (1 - y)]
        for a in range(na):
            m = arrs[a].shape[1]
            step = _row_step(m)

            def add(i, carry, a=a, step=step):
                rs = pl.ds(pl.multiple_of(i * step, step), step)
                for j, q in enumerate(others):
                    hs[a][j, rs, :] = (own[a][q, rs, :].astype(F32) + ra[a][q, rs, :].astype(F32)).astype(hs[a].dtype)
                q = 2 * x + y
                hm[a][rs, :] = own[a][q, rs, :].astype(F32) + ra[a][q, rs, :].astype(F32)
                return carry

            lax.fori_loop(0, m // step, add, 0)
        for cp in sends:
            cp.wait_send()

    vmem = pl.BlockSpec(memory_space=pltpu.VMEM)
    scratch = [pltpu.VMEM((4,) + t.shape[1:], t.dtype) for t in arrs] * 2
    scratch += [pltpu.SemaphoreType.DMA((na, 4)), pltpu.SemaphoreType.DMA((na, 4)), pltpu.SemaphoreType.DMA((na, 4))]
    out = pl.pallas_call(
        body, name=name,
        in_specs=[pl.BlockSpec(memory_space=pl.ANY)] * na, out_specs=[vmem] * (2 * na),
        out_shape=([jax.ShapeDtypeStruct((3,) + t.shape[1:], t.dtype) for t in arrs]
                   + [jax.ShapeDtypeStruct(t.shape[1:], F32) for t in arrs]),
        scratch_shapes=scratch,
        compiler_params=_compiler_params(),
    )(*arrs)
    return out[:na], out[na:]


_HBM = pl.BlockSpec(memory_space=pltpu.HBM)
_SEM = pl.BlockSpec(memory_space=pltpu.SEMAPHORE)
_EFFECT = pltpu.SideEffectType.DATAFLOW_SIDE_EFFECTING


def _exchange_plan(direct):
    x, y, c = _mesh_pos()
    if not direct:
        return [(j, j, (qx, qy, c)) for j, (qx, qy) in enumerate([(1 - x, y), (x, 1 - y), (1 - x, 1 - y)])]
    plan = []
    for k in range(1, N_DEV):
        px, py, pc = x ^ ((k >> 2) & 1), y ^ ((k >> 1) & 1), c ^ (k & 1)
        plan.append((4 * px + 2 * py + pc, k - 1, (px, py, pc)))
    return plan


def _exchange_start(srcs, direct, name):
    na = len(srcs)
    slots = N_DEV - 1 if direct else 3

    def body(*refs):
        src, land = refs[:na], refs[na:2 * na]
        send_sem, recv_sem = refs[2 * na], refs[2 * na + 1]
        token = refs[-1]
        for block, slot, peer in _exchange_plan(direct):
            for a in range(na):
                pltpu.make_async_remote_copy(
                    src_ref=src[a].at[block], dst_ref=land[a].at[slot], send_sem=send_sem.at[slots * a + slot],
                    recv_sem=recv_sem.at[slots * a + slot], device_id=peer, device_id_type=MESH).start()
        token[...] = jnp.zeros_like(token)

    zones = [jax.ShapeDtypeStruct((slots,) + t.shape[1:], t.dtype) for t in srcs]
    bufs = [pltpu.HBM(t.shape, t.dtype) for t in list(srcs) + zones]
    out = pl.pallas_call(
        body, name=name,
        out_shape=(pltpu.SemaphoreType.DMA((slots * na,)), pltpu.SemaphoreType.DMA((slots * na,)), *bufs,
                   jax.ShapeDtypeStruct((8, 128), F32)),
        in_specs=[_HBM] * (2 * na),
        out_specs=(_SEM, _SEM, *([_HBM] * (2 * na)), pl.BlockSpec(memory_space=pltpu.VMEM)),
        input_output_aliases={i: 2 + i for i in range(2 * na)},
        compiler_params=pltpu.CompilerParams(has_side_effects=_EFFECT),
    )(*[pltpu.with_memory_space_constraint(t, pltpu.HBM) for t in srcs],
      *[pltpu.with_memory_space_constraint(lax.empty(t.shape, t.dtype), pltpu.HBM) for t in zones])
    return out[0], out[1], out[2:2 + na], out[2 + na:2 + 2 * na], out[-1]


def _exchange_wait(send_sem, recv_sem, srcs, lands, direct, after, name):
    na = len(srcs)
    slots = N_DEV - 1 if direct else 3

    def body(*refs):
        src, land = refs[:na], refs[na:2 * na]
        send_sem_ref, recv_sem_ref = refs[2 * na], refs[2 * na + 1]
        for block, slot, peer in _exchange_plan(direct):
            for a in range(na):
                cp = pltpu.make_async_remote_copy(
                    src_ref=src[a].at[block], dst_ref=land[a].at[slot], send_sem=send_sem_ref.at[slots * a + slot],
                    recv_sem=recv_sem_ref.at[slots * a + slot], device_id=peer, device_id_type=MESH)
                cp.wait_send()
                cp.wait_recv()

    bufs = [pltpu.HBM(t.shape, t.dtype) for t in list(srcs) + list(lands)]
    out = pl.pallas_call(
        body, name=name,
        out_shape=tuple(bufs),
        in_specs=[_HBM] * (2 * na) + [_SEM, _SEM, pl.BlockSpec(memory_space=pl.ANY)],
        out_specs=[_HBM] * (2 * na),
        input_output_aliases={i: i for i in range(2 * na)},
        compiler_params=pltpu.CompilerParams(has_side_effects=_EFFECT),
    )(*srcs, *lands, send_sem, recv_sem, after)
    return out[:na], out[na:]


def _own_then_slots(mine_ref, lands_ref, rows=slice(None)):
    if len(mine_ref.shape) == 3:
        x, y, c = _mesh_pos()
        total = mine_ref[4 * x + 2 * y + c, rows, :].astype(F32)
    else:
        total = mine_ref[rows, :].astype(F32)
    for j in range(lands_ref.shape[0]):
        total = total + lands_ref[j, rows, :].astype(F32)
    return total


SMALL_ROWS = 2 * PACK_SLICE + 2 * 8


def _small_block(mine, lands, dgpre, name):
    def body(*refs):
        hm, ld, dg = refs[:DEPTH], refs[DEPTH:2 * DEPTH], refs[2 * DEPTH:3 * DEPTH]
        blk, land, sem = refs[3 * DEPTH:]
        for l in range(DEPTH):
            blk[PACK_SLICE * l:PACK_SLICE * (l + 1), :] = _own_then_slots(hm[l], ld[l])
            blk[2 * PACK_SLICE + 8 * l:2 * PACK_SLICE + 8 * (l + 1), :] = dg[l][...]
        cp = pltpu.make_async_copy(blk, _device_rows(land, SMALL_ROWS, *_mesh_pos()), sem)
        cp.start()
        cp.wait()

    vmem = pl.BlockSpec(memory_space=pltpu.VMEM)
    return pl.pallas_call(
        body, name=name,
        in_specs=[vmem] * (3 * DEPTH), out_specs=[vmem, pl.BlockSpec(memory_space=pl.ANY)],
        out_shape=[jax.ShapeDtypeStruct((SMALL_ROWS, 128), F32), jax.ShapeDtypeStruct((N_DEV * SMALL_ROWS, 128), F32)],
        scratch_shapes=[pltpu.SemaphoreType.DMA],
        compiler_params=_compiler_params(),
    )(*mine, *lands, *dgpre)


def _adamw_math(w, g, m, v):
    m = ADAM_B1 * m + (1.0 - ADAM_B1) * g
    v = ADAM_B2 * v + (1.0 - ADAM_B2) * (g * g)
    m_hat = m / (1.0 - ADAM_B1 ** ADAM_STEP)
    v_hat = v / (1.0 - ADAM_B2 ** ADAM_STEP)
    delta = -ADAM_LR * (m_hat / (jnp.sqrt(v_hat) + ADAM_EPS) + ADAM_WD * w)
    return delta, m, v


def _adamw_layer(layer, mine, lands, w, m, v, earlier, token, name, rows):
    _, mm, nn = w.shape

    def body(hm_ref, ld_ref, w_ref, m_ref, v_ref, _, *refs):
        g_ref, d_ref, nm_ref, nv_ref = refs[-4:]
        g = _own_then_slots(hm_ref, ld_ref)
        g_ref[...] = g
        d, nm, nv = _adamw_math(w_ref[...], g, m_ref[...], v_ref[...])
        d_ref[...] = d
        nm_ref[...] = nm
        nv_ref[...] = nv

    spec = pl.BlockSpec((None, rows, nn), lambda i: (layer, i, 0))
    carried = [] if earlier is None else list(earlier)
    return pl.pallas_call(
        body, name=name, grid=(mm // rows,),
        in_specs=([pl.BlockSpec((rows, nn), lambda i: (i, 0)) if mine.ndim == 2
                   else pl.BlockSpec((N_DEV, rows, nn), lambda i: (0, i, 0)),
                   pl.BlockSpec((lands.shape[0], rows, nn), lambda i: (0, i, 0)),
                   spec, spec, spec] + [pl.BlockSpec(memory_space=pl.ANY)] * (1 + len(carried))),
        out_specs=[spec] * 4,
        out_shape=[jax.ShapeDtypeStruct(w.shape, F32)] * 4,
        input_output_aliases={6 + t: t for t in range(len(carried))},
        compiler_params=_compiler_params(("arbitrary",)),
    )(mine, lands, w, m, v, token, *carried)


def _adamw_small(gathered, params):
    def body(all_ref, *refs):
        ins, outs, packs = refs[:15], refs[15:15 + 21], refs[15 + 21]
        loss_ref = outs[0]
        for dev in range(N_DEV):
            for l in range(DEPTH):
                packs[l, PACK_SLICE * dev:PACK_SLICE * (dev + 1), :] = (
                    all_ref[SMALL_ROWS * dev + PACK_SLICE * l:SMALL_ROWS * dev + PACK_SLICE * (l + 1), :])
        loss_ref[...] = packs[DEPTH - 1, ROW_LOSS:ROW_LOSS + 1, 0:1]

        def update(p, sel, g):
            w_ref, m_ref, v_ref = ins[p], ins[5 + p], ins[10 + p]
            d, nm, nv = _adamw_math(w_ref[sel], g, m_ref[sel], v_ref[sel])
            for t, val in enumerate((g, d, nm, nv)):
                outs[1 + 5 * t + p][sel] = val

        for l in range(DEPTH):
            gp = packs.at[l]
            row0 = 2 * PACK_SLICE + 8 * l
            dgpre = all_ref[row0:row0 + 8, :]
            for dev in range(1, N_DEV):
                dgpre = dgpre + all_ref[SMALL_ROWS * dev + row0:SMALL_ROWS * dev + row0 + 8, :]
            for grp in range(4):
                update(0, (l, grp), gp[ROW_PW + BLOCK * grp:ROW_PW + BLOCK * (grp + 1), :])
                update(1, (slice(l, l + 1), slice(128 * grp, 128 * (grp + 1))), gp[ROW_SC + grp:ROW_SC + grp + 1, :])
            update(2, (slice(l, l + 1), slice(None)), gp[ROW_SINK:ROW_SINK + 1, 0:N_HEADS])
            for r in range(D_MODEL // 128):
                sel = (slice(l, l + 1), slice(128 * r, 128 * (r + 1)))
                update(3, sel, dgpre[r:r + 1, :])
                update(4, sel, gp[ROW_NPOST + r:ROW_NPOST + r + 1, :])

    shapes = [jax.ShapeDtypeStruct(p.shape, F32) for p in params[:5]]
    return pl.pallas_call(
        body, name="adamw_small",
        out_shape=[jax.ShapeDtypeStruct((1, 1), F32)] + shapes * 4,
        scratch_shapes=[pltpu.VMEM((DEPTH, PACK_ROWS, 128), F32)],
        compiler_params=_compiler_params(),
    )(gathered, *params)


def kernel(x, w_in, pool_w, pool_scale, attn_sinks, w_out, norm_pre, norm_post, loss_target, m_w_in, m_pool_w, m_pool_scale, m_attn_sinks, m_w_out, m_norm_pre, m_norm_post, v_w_in, v_pool_w, v_pool_scale, v_attn_sinks, v_w_out, v_norm_pre, v_norm_post):
    x0 = x.reshape(SEQ, D_MODEL)
    target = loss_target.reshape(SEQ, D_MODEL)
    bias = jnp.asarray(_attn_bias())
    w_in_t, m_in_t, v_in_t = (jnp.swapaxes(t, 1, 2) for t in (w_in, m_w_in, v_w_in))

    (win0, wout0), later, lands = _allgather([(w_in_t, 0), (w_out, 0)], BF16, "gather_w0",
                                              later=[(w_in_t, 1), (w_out, 1)])
    sems, later, lands, token = _gather_start(later, lands, "gather_w1_start")
    win_full, wout_full = [win0, None], [wout0, None]

    saved = []
    xl = x0
    for layer in range(DEPTH):
        if layer > 0:
            win_full[layer] = _gather_wait(sems[0], later[0], lands[0], xl, "gather_w_in1_wait")
        u, pg, q, k, v, ag, z, a = _fwd_front(layer, xl, norm_pre, win_full[layer], token, attn_sinks,
                                             pool_w, pool_scale, bias)
        if layer > 0:
            wout_full[layer] = _gather_wait(sems[1], later[1], lands[1], z, "gather_w_out1_wait")
        x_next, y = _fwd_out(layer, z, xl, norm_post, wout_full[layer])
        saved.append((xl, u, pg, q, k, v, ag, z, a, y))
        xl = x_next

    params_small = [pool_w, pool_scale, attn_sinks, norm_pre, norm_post,
                    m_pool_w, m_pool_scale, m_attn_sinks, m_norm_pre, m_norm_post,
                    v_pool_w, v_pool_scale, v_attn_sinks, v_norm_pre, v_norm_post]
    exchange, dgpre = [None] * DEPTH, [None] * DEPTH
    dx = None
    for layer in reversed(range(DEPTH)):
        xin, u, pg, q, k, v, ag, z, a, y = saved[layer]
        if layer == DEPTH - 1:
            dx, dz, gw_out, pack = _bwd_out(layer, True, xl, target, y, z, norm_post, wout_full[layer])
        else:
            dz, gw_out, pack = _bwd_out(layer, False, dx, token, y, z, norm_post, wout_full[layer])
        dproj, pack = _bwd_mix(layer, attn_sinks, dz, u, pg, q, k, v, ag, a, pool_w, pool_scale, bias, pack)
        if layer > 0:
            dx, dgpre[layer], gw_in_t = _bwd_in(layer, "both", token, dproj, xin, norm_pre, dx, win_full[layer])
        else:
            (gw_in_t,) = _bwd_in(layer, "dw", token, dproj, xin, norm_pre)
        blocks = [gw_in_t.reshape(N_DEV, IN_SHARD, D_MODEL), gw_out.reshape(N_DEV, OUT_SHARD, D_MODEL),
                  pack.reshape(N_DEV, PACK_SLICE, 128)]
        direct = layer > 0
        srcs, mine = (blocks, None) if direct else _pair_reduce(blocks, f"pair_reduce{layer}")
        send_sem, recv_sem, srcs, lands, token = _exchange_start(srcs, direct, f"exchange_start{layer}")
        exchange[layer] = (send_sem, recv_sem, srcs, lands, mine)
        if layer == 0:
            dx, dgpre[layer] = _bwd_in(layer, "dx", token, dproj, xin, norm_pre, dx, win_full[layer])

    own, waited = [None] * DEPTH, [None] * DEPTH
    big_in, big_out, after = None, None, dx
    for layer in reversed(range(DEPTH)):
        send_sem, recv_sem, srcs, lands, mine = exchange[layer]
        srcs, waited[layer] = _exchange_wait(send_sem, recv_sem, srcs, lands, layer > 0, after, f"exchange_wait{layer}")
        own[layer] = srcs if mine is None else mine
        if layer > 0:
            big_in = _adamw_layer(layer, own[layer][0], waited[layer][0], w_in_t, m_in_t, v_in_t, big_in, token,
                                  f"adamw_in{layer}", 96)
            big_out = _adamw_layer(layer, own[layer][1], waited[layer][1], w_out, m_w_out, v_w_out, big_out, token,
                                   f"adamw_out{layer}", 128)
            after = big_out[0]
    block, land = _small_block([own[l][2] for l in range(DEPTH)], [waited[l][2] for l in range(DEPTH)],
                               dgpre, "small_block")
    sems, block, land, token = _gather_start([block], [land], "gather_small_start")
    big_in = _adamw_layer(0, own[0][0], waited[0][0], w_in_t, m_in_t, v_in_t, big_in, token, "adamw_in0", 96)
    big_out = _adamw_layer(0, own[0][1], waited[0][1], w_out, m_w_out, v_w_out, big_out, token, "adamw_out0", 128)
    gathered = _gather_wait(sems[0], block[0], land[0], big_out[0], "gather_small_wait")
    small_out = _adamw_small(gathered, params_small)
    loss = small_out[0].reshape(())

    outs = [loss, dx.reshape(1, SEQ, D_MODEL)]
    for t in range(4):
        pw_, sc_, sk_, npre_, npost_ = small_out[1 + 5 * t:6 + 5 * t]
        outs += [jnp.swapaxes(big_in[t], 1, 2), pw_, sc_, sk_, big_out[t], npre_, npost_]
    return tuple(outs)
```

```python
import numpy as np
import jax
import jax.numpy as jnp
from jax import lax
from jax.experimental import pallas as pl
from jax.experimental.pallas import tpu as pltpu

F32 = jnp.float32
BF16 = jnp.bfloat16

N_DEV = 8
SEQ = 2048
D_MODEL = 1024
D_POOL = 512
D_ATTN = 512
D_KV = 128
D_IN = 2304
N_HEADS = 8
GQA = 4
HEAD_DIM = 64
BLOCK = 128
N_BLOCKS = SEQ // BLOCK
POOL_WINDOWS = (2, 4, 8, 16)
DEPTH = 2
EPS = 1e-6
NEG_INF = -1e30
SCALE = HEAD_DIM ** -0.5
IN_SHARD = D_IN // N_DEV
OUT_SHARD = D_MODEL // N_DEV

COL_U, COL_PG, COL_Q, COL_K, COL_V, COL_AG = 0, 512, 1024, 1536, 1664, 1792

ADAM_LR = 0.001
ADAM_B1 = 0.9
ADAM_B2 = 0.999
ADAM_EPS = 1e-08
ADAM_WD = 0.01
ADAM_STEP = 10

TOKEN_TILE = 512
VMEM_LIMIT = 56 * 1024 * 1024
MESH = pl.DeviceIdType.MESH

ROW_PW, ROW_SC, ROW_SINK, ROW_NPRE, ROW_NPOST, ROW_LOSS = 0, 512, 520, 528, 536, 544
PACK_ROWS = 576
PACK_SLICE = PACK_ROWS // N_DEV


def _nn(a, b):
    return jnp.dot(a, b, preferred_element_type=F32)


def _nt(a, b):
    return lax.dot_general(a, b, (((1,), (1,)), ((), ())), preferred_element_type=F32)


def _tn(a, b):
    return lax.dot_general(a, b, (((0,), (0,)), ((), ())), preferred_element_type=F32)


def _silu_parts(g):
    s = jax.nn.sigmoid(g)
    return g * s, s * (1.0 + g * (1.0 - s))


def _resident(shape):
    return pl.BlockSpec(shape, lambda *_: (0,) * len(shape), pipeline_mode=pl.Buffered(1))


def _compiler_params(sem=None):
    if sem is None:
        return pltpu.CompilerParams(vmem_limit_bytes=VMEM_LIMIT)
    return pltpu.CompilerParams(dimension_semantics=sem, vmem_limit_bytes=VMEM_LIMIT)


def _attn_bias():
    t = np.arange(BLOCK)[:, None]
    j = np.arange(2 * BLOCK)[None, :]
    dist = t + BLOCK - j
    in_win = (dist >= 0) & (dist < BLOCK)
    out = np.zeros((2, 2, BLOCK, GQA * 2 * BLOCK), np.float32)
    for variant in range(2):
        valid = in_win & ((j >= BLOCK) | (variant == 1))
        for kv in range(2):
            for g in range(GQA):
                slope = np.float32(2.0 ** (-(kv * GQA + g + 1)))
                b = np.where(valid, -slope * dist.astype(np.float32), np.float32(NEG_INF))
                out[variant, kv, :, g * 256:(g + 1) * 256] = b
    return out


def _block_diag(kx, kv):
    rolled = pltpu.roll(kx, 64, 1)
    lane = lax.broadcasted_iota(jnp.int32, kx.shape, 1)
    dup = jnp.where(lane < 64, kx, rolled) if kv == 0 else jnp.where(lane < 64, rolled, kx)
    rep = jnp.concatenate([dup, dup], axis=1).astype(BF16)
    lane2 = lax.broadcasted_iota(jnp.int32, rep.shape, 1)
    zero = jnp.zeros_like(rep)
    parts = [jnp.where((lane2 >= 64 * g) & (lane2 < 64 * g + 64), rep, zero) for g in range(GQA)]
    return jnp.concatenate(parts, axis=0)


def _diag_fold(m):
    lane = lax.broadcasted_iota(jnp.int32, (256, 256), 1)
    r = jnp.where(lane < 64, m[0:256], jnp.where(lane < 128, m[256:512], jnp.where(lane < 192, m[512:768], m[768:1024])))
    h = r[:, 0:128] + r[:, 128:256]
    return h + pltpu.roll(h, 64, 1)


def _window_sum(ext, w, forward):
    s = ext
    sh = 1
    while sh < w:
        s = s + pltpu.roll(s, (256 - sh) if forward else sh, 0)
        sh *= 2
    return s


def _inv_count(n, w):
    t = n * BLOCK + lax.broadcasted_iota(jnp.int32, (BLOCK, 1), 0) + 1
    return 1.0 / jnp.minimum(t.astype(F32), float(w))


def _kv_ext(ref, n):
    r0 = pl.multiple_of(jnp.maximum(n - 1, 0) * BLOCK, BLOCK)
    r1 = pl.multiple_of(n * BLOCK, BLOCK)
    return jnp.concatenate([ref[pl.ds(r0, BLOCK), :], ref[pl.ds(r1, BLOCK), :]], axis=0)


def _softmax_chunk(sg, sink):
    m = jnp.maximum(jnp.max(sg, axis=-1, keepdims=True), sink)
    p = jnp.exp(sg - m)
    esink = jnp.exp(sink - m)
    rl = 1.0 / (jnp.sum(p, axis=-1, keepdims=True) + esink)
    return p * rl, esink * rl


def _rows_of(vec_ref, pack_ref, row0):
    for r in range(D_MODEL // 128):
        pack_ref[row0 + r:row0 + r + 1, :] = vec_ref[:, 128 * r:128 * (r + 1)]


FRONT_TILE = 2 * BLOCK


def _fwd_front(layer, x, norm_pre, w_in_t, token, sinks, pool_w, pool_scale, bias):
    tm = FRONT_TILE

    def body(sink_ref, x_ref, g_ref, w_ref, _, pw_ref, sc_ref, bias_ref,
             u_ref, pg_ref, q_ref, k_ref, v_ref, ag_ref, z_ref, a_ref, uprev, kprev, vprev):
        i = pl.program_id(0)

        @pl.when(i == 0)
        def _():
            uprev[...] = jnp.zeros_like(uprev)
            kprev[...] = jnp.zeros_like(kprev)
            vprev[...] = jnp.zeros_like(vprev)

        xv = x_ref[...]
        r = lax.rsqrt(jnp.mean(xv * xv, axis=-1, keepdims=True) + EPS)
        h = (xv * r * g_ref[layer:layer + 1, :]).astype(BF16)
        u_ref[...] = _nt(h, w_ref[COL_U:COL_PG, :])
        pg_ref[...] = _nt(h, w_ref[COL_PG:COL_Q, :])
        for sb in range(tm // BLOCK):
            n = (tm // BLOCK) * i + sb
            rows = slice(BLOCK * sb, BLOCK * (sb + 1))
            before = slice(BLOCK * (sb - 1), BLOCK * sb)
            uv = u_ref[rows, :]
            ext = jnp.concatenate([uprev[...] if sb == 0 else u_ref[before, :], uv], axis=0)
            for g, w in enumerate(POOL_WINDOWS):
                cs = slice(BLOCK * g, BLOCK * (g + 1))
                win = _window_sum(ext[:, cs], w, forward=False)[BLOCK:]
                pooled = win * _inv_count(n, w) - uv[:, cs]
                mixed = _nn(pooled.astype(BF16), pw_ref[g].astype(BF16))
                gate, _ = _silu_parts(pg_ref[rows, cs])
                z_ref[rows, cs] = (mixed * sc_ref[layer:layer + 1, cs] * gate).astype(BF16)

        q_ref[...] = _nt(h, w_ref[COL_Q:COL_K, :]).astype(BF16)
        k_ref[...] = _nt(h, w_ref[COL_K:COL_V, :])
        v_ref[...] = _nt(h, w_ref[COL_V:COL_AG, :])
        ag_ref[...] = _nt(h, w_ref[COL_AG:D_IN, :])

        for sb in range(tm // BLOCK):
            n = (tm // BLOCK) * i + sb
            rows = slice(BLOCK * sb, BLOCK * (sb + 1))
            before = slice(BLOCK * (sb - 1), BLOCK * sb)
            kx = jnp.concatenate([kprev[...] if sb == 0 else k_ref[before, :], k_ref[rows, :]], axis=0)
            vx = jnp.concatenate([vprev[...] if sb == 0 else v_ref[before, :], v_ref[rows, :]], axis=0)
            variant = jnp.minimum(n, 1) if sb == 0 else 1
            for kv in range(2):
                cs = slice(256 * kv, 256 * (kv + 1))
                bk = _block_diag(kx, kv)
                bv = _block_diag(vx, kv)
                s = _nt(q_ref[rows, cs], bk) * SCALE + bias_ref[variant, kv]
                ps = []
                for g in range(GQA):
                    p, _ = _softmax_chunk(s[:, 256 * g:256 * (g + 1)], sink_ref[layer, kv * GQA + g])
                    ps.append(p.astype(BF16))
                o = _nn(jnp.concatenate(ps, axis=1), bv)
                a_ref[rows, cs] = o
                gate, _ = _silu_parts(ag_ref[rows, cs])
                z_ref[rows, D_POOL + 256 * kv:D_POOL + 256 * (kv + 1)] = (o * gate).astype(BF16)

        tail = slice(tm - BLOCK, tm)
        uprev[...] = u_ref[tail, :]
        kprev[...] = k_ref[tail, :]
        vprev[...] = v_ref[tail, :]

    row = lambda c: pl.BlockSpec((tm, c), lambda i: (i, 0))
    const = lambda shape: pl.BlockSpec(shape, lambda i: (0,) * len(shape))
    return pl.pallas_call(
        body, name=f"fwd_front{layer}", grid=(SEQ // tm,),
        in_specs=[pl.BlockSpec(memory_space=pltpu.SMEM), row(D_MODEL), const((DEPTH, D_MODEL)),
                  _resident((D_IN, D_MODEL)), const((8, 128)),
                  pl.BlockSpec((None, 4, BLOCK, BLOCK), lambda i: (layer, 0, 0, 0)), const((DEPTH, D_POOL)),
                  _resident((2, 2, BLOCK, 1024))],
        out_specs=[row(D_POOL), row(D_POOL), row(D_ATTN), row(D_KV), row(D_KV), row(D_ATTN), row(D_MODEL),
                   row(D_ATTN)],
        out_shape=[jax.ShapeDtypeStruct((SEQ, D_POOL), F32), jax.ShapeDtypeStruct((SEQ, D_POOL), F32),
                   jax.ShapeDtypeStruct((SEQ, D_ATTN), BF16), jax.ShapeDtypeStruct((SEQ, D_KV), F32),
                   jax.ShapeDtypeStruct((SEQ, D_KV), F32), jax.ShapeDtypeStruct((SEQ, D_ATTN), F32),
                   jax.ShapeDtypeStruct((SEQ, D_MODEL), BF16), jax.ShapeDtypeStruct((SEQ, D_ATTN), F32)],
        scratch_shapes=[pltpu.VMEM((BLOCK, D_POOL), F32), pltpu.VMEM((BLOCK, D_KV), F32),
                        pltpu.VMEM((BLOCK, D_KV), F32)],
        compiler_params=_compiler_params(("arbitrary",)),
    )(sinks, x, norm_pre, w_in_t, token, pool_w, pool_scale, bias)


def _fwd_out(layer, z, x, norm_post, w_out):
    tm = TOKEN_TILE

    def body(z_ref, x_ref, g_ref, w_ref, xn_ref, y_ref):
        y = _nn(z_ref[...], w_ref[...])
        y_ref[...] = y
        r = lax.rsqrt(jnp.mean(y * y, axis=-1, keepdims=True) + EPS)
        xn_ref[...] = x_ref[...] + y * r * g_ref[layer:layer + 1, :]

    row = lambda c: pl.BlockSpec((tm, c), lambda i: (i, 0))
    return pl.pallas_call(
        body, name=f"fwd_out{layer}", grid=(SEQ // tm,),
        in_specs=[row(D_MODEL), row(D_MODEL), pl.BlockSpec((DEPTH, D_MODEL), lambda i: (0, 0)),
                  _resident((D_MODEL, D_MODEL))],
        out_specs=[row(D_MODEL), row(D_MODEL)],
        out_shape=[jax.ShapeDtypeStruct((SEQ, D_MODEL), F32), jax.ShapeDtypeStruct((SEQ, D_MODEL), F32)],
        compiler_params=_compiler_params(("arbitrary",)),
    )(z, x, norm_post, w_out)


def _bwd_out(layer, top, dxo_or_xf, target_or_token, y, z, norm_post, w_out):
    tm = TOKEN_TILE
    steps = SEQ // tm

    def body(*refs):
        if top:
            xf_ref, t_ref, y_ref, z_ref, g_ref, w_ref, dxo_ref, dz_ref, dw_ref, pack_ref, acc, dg, lacc = refs
        else:
            dxi_ref, _, y_ref, z_ref, g_ref, w_ref, dz_ref, dw_ref, pack_ref, acc, dg, lacc = refs
        i = pl.program_id(0)

        @pl.when(i == 0)
        def _():
            acc[...] = jnp.zeros_like(acc)
            dg[...] = jnp.zeros_like(dg)
            lacc[...] = jnp.zeros_like(lacc)
            pack_ref[...] = jnp.zeros_like(pack_ref)

        if top:
            d = xf_ref[...] - t_ref[...]
            dxo_v = d * (1.0 / D_MODEL)
            dxo_ref[...] = dxo_v
            part = jnp.sum(d * d, axis=-1, keepdims=True) * (1.0 / D_MODEL)
            lacc[...] += 0.5 * jnp.sum(part, axis=0, keepdims=True)
        else:
            dxo_v = dxi_ref[...]
        y = y_ref[...]
        r = lax.rsqrt(jnp.mean(y * y, axis=-1, keepdims=True) + EPS)
        yn = y * r
        dg[...] += jnp.sum(dxo_v * yn, axis=0, keepdims=True)
        dyn = dxo_v * g_ref[layer:layer + 1, :]
        dy = (r * (dyn - yn * jnp.mean(dyn * yn, axis=-1, keepdims=True))).astype(BF16)
        dz_ref[...] = _nt(dy, w_ref[...])
        acc[...] += _tn(z_ref[...], dy)

        @pl.when(i == steps - 1)
        def _():
            dw_ref[...] = acc[...].astype(BF16)
            _rows_of(dg, pack_ref, ROW_NPOST)
            lane = lax.broadcasted_iota(jnp.int32, (1, 128), 1)
            pack_ref[ROW_LOSS:ROW_LOSS + 1, :] = jnp.where(lane == 0, lacc[...], 0.0)

    row = lambda c: pl.BlockSpec((tm, c), lambda i: (i, 0))
    const = lambda shape: pl.BlockSpec(shape, lambda i: (0,) * len(shape))
    act = jax.ShapeDtypeStruct((SEQ, D_MODEL), F32)
    return pl.pallas_call(
        body, name=f"bwd_out{layer}", grid=(steps,),
        in_specs=([row(D_MODEL), row(D_MODEL) if top else const((8, 128))]
                  + [row(D_MODEL), row(D_MODEL), const((DEPTH, D_MODEL)), _resident((D_MODEL, D_MODEL))]),
        out_specs=([row(D_MODEL)] * (2 if top else 1) + [const((D_MODEL, D_MODEL)), const((PACK_ROWS, 128))]),
        out_shape=([act] * (2 if top else 1)
                   + [jax.ShapeDtypeStruct((D_MODEL, D_MODEL), BF16), jax.ShapeDtypeStruct((PACK_ROWS, 128), F32)]),
        scratch_shapes=[pltpu.VMEM((D_MODEL, D_MODEL), F32), pltpu.VMEM((1, D_MODEL), F32), pltpu.VMEM((1, 1), F32)],
        compiler_params=_compiler_params(("arbitrary",)),
    )(dxo_or_xf, target_or_token, y, z, norm_post, w_out)


def _bwd_mix(layer, sinks, dz, u, pg, q, k, v, ag, a, pool_w, pool_scale, bias, pack):
    last = N_BLOCKS - 1

    def body(sink_ref, dz_ref, u_ref, up_ref, pg_ref, q_ref, k_ref, v_ref, ag_ref, a_ref, pw_ref, sc_ref,
             bias_ref, pin_ref, dp_ref, pack_ref, ck, cv, ce):
        i = pl.program_id(0)
        n = last - i

        @pl.when(i == 0)
        def _():
            ck[...] = jnp.zeros_like(ck)
            cv[...] = jnp.zeros_like(cv)
            ce[...] = jnp.zeros_like(ce)
            pack_ref[...] = pin_ref[...]

        uv = u_ref[...]
        has_prev = (n > 0).astype(F32)
        ext = jnp.concatenate([up_ref[...] * has_prev, uv], axis=0)
        for g, w in enumerate(POOL_WINDOWS):
            cs = slice(BLOCK * g, BLOCK * (g + 1))
            inv = _inv_count(n, w)
            win = _window_sum(ext[:, cs], w, forward=False)[BLOCK:]
            pooled = win * inv - uv[:, cs]
            pw_g = pw_ref[g].astype(BF16)
            mixed = _nn(pooled.astype(BF16), pw_g)
            gate, dgate = _silu_parts(pg_ref[:, cs])
            dzp = dz_ref[:, cs]
            sc = sc_ref[layer:layer + 1, cs]
            dpm = dzp * gate
            dp_ref[:, COL_PG + BLOCK * g:COL_PG + BLOCK * (g + 1)] = (dzp * (mixed * sc) * dgate).astype(BF16)
            pack_ref[ROW_SC + g:ROW_SC + g + 1, :] += jnp.sum(dpm * mixed, axis=0, keepdims=True)
            dmixed = (dpm * sc).astype(BF16)
            pack_ref[ROW_PW + BLOCK * g:ROW_PW + BLOCK * (g + 1), :] += _tn(pooled.astype(BF16), dmixed)
            dpooled = _nt(dmixed, pw_g)
            e = dpooled * inv
            lead = _window_sum(jnp.concatenate([e, ce[:, cs]], axis=0), w, forward=True)[:BLOCK]
            dp_ref[:, COL_U + BLOCK * g:COL_U + BLOCK * (g + 1)] = (lead - dpooled).astype(BF16)
            ce[:, cs] = e

        kx = _kv_ext(k_ref, n)
        vx = _kv_ext(v_ref, n)
        lane = lax.broadcasted_iota(jnp.int32, (1, 128), 1)
        dsink_row = jnp.zeros((1, 128), F32)
        tks, tvs = [], []
        for kv in range(2):
            cs = slice(256 * kv, 256 * (kv + 1))
            bk = _block_diag(kx, kv)
            bv = _block_diag(vx, kv)
            qv = q_ref[:, cs]
            s = _nt(qv, bk) * SCALE + bias_ref[0, kv]
            gate, dgate = _silu_parts(ag_ref[:, cs])
            dza = dz_ref[:, D_POOL + 256 * kv:D_POOL + 256 * (kv + 1)]
            dp_ref[:, COL_AG + 256 * kv:COL_AG + 256 * (kv + 1)] = (dza * a_ref[:, cs] * dgate).astype(BF16)
            da = dza * gate
            dab = da.astype(BF16)
            dpr = _nt(dab, bv)
            ps, dss = [], []
            for g in range(GQA):
                gs = slice(256 * g, 256 * (g + 1))
                p, psink = _softmax_chunk(s[:, gs], sink_ref[layer, kv * GQA + g])
                dpg_ = dpr[:, gs]
                delta = jnp.sum(p * dpg_, axis=-1, keepdims=True)
                dsink = -jnp.sum(psink * delta, axis=0, keepdims=True)
                dsink_row = dsink_row + jnp.where(lane == kv * GQA + g, dsink, 0.0)
                ps.append(p)
                dss.append(p * (dpg_ - delta) * SCALE)
            p_all = jnp.concatenate(ps, axis=1)
            ds_all = jnp.concatenate(dss, axis=1)
            dp_ref[:, COL_Q + 256 * kv:COL_Q + 256 * (kv + 1)] = _nn(ds_all.astype(BF16), bk).astype(BF16)
            tks.append(_diag_fold(_tn(ds_all.astype(BF16), qv)))
            tvs.append(_diag_fold(_tn(p_all.astype(BF16), dab)))
        pack_ref[ROW_SINK:ROW_SINK + 1, :] += dsink_row
        lane2 = lax.broadcasted_iota(jnp.int32, (256, 128), 1)
        dkx = jnp.where(lane2 < 64, tks[0], tks[1])
        dvx = jnp.where(lane2 < 64, tvs[0], tvs[1])
        dp_ref[:, COL_K:COL_V] = (ck[...] + dkx[BLOCK:]).astype(BF16)
        dp_ref[:, COL_V:COL_AG] = (cv[...] + dvx[BLOCK:]).astype(BF16)
        ck[...] = dkx[:BLOCK]
        cv[...] = dvx[:BLOCK]

    blk = lambda c: pl.BlockSpec((BLOCK, c), lambda i: (last - i, 0))
    full = lambda shape: pl.BlockSpec(shape, lambda i: (0,) * len(shape))
    return pl.pallas_call(
        body, name=f"bwd_mix{layer}", grid=(N_BLOCKS,),
        in_specs=[pl.BlockSpec(memory_space=pltpu.SMEM), blk(D_MODEL), blk(D_POOL),
                  pl.BlockSpec((BLOCK, D_POOL), lambda i: (jnp.maximum(last - i - 1, 0), 0)),
                  blk(D_POOL), blk(D_ATTN), full((SEQ, D_KV)), full((SEQ, D_KV)), blk(D_ATTN), blk(D_ATTN),
                  pl.BlockSpec((None, 4, BLOCK, BLOCK), lambda i: (layer, 0, 0, 0)), full((DEPTH, D_POOL)),
                  pl.BlockSpec((1, 2, BLOCK, 1024), lambda i: (jnp.minimum(last - i, 1), 0, 0, 0)),
                  full((PACK_ROWS, 128))],
        out_specs=[blk(D_IN), full((PACK_ROWS, 128))],
        out_shape=[jax.ShapeDtypeStruct((SEQ, D_IN), BF16), jax.ShapeDtypeStruct((PACK_ROWS, 128), F32)],
        scratch_shapes=[pltpu.VMEM((BLOCK, D_KV), F32), pltpu.VMEM((BLOCK, D_KV), F32),
                        pltpu.VMEM((BLOCK, D_POOL), F32)],
        input_output_aliases={13: 1},
        compiler_params=_compiler_params(("arbitrary",)),
    )(sinks, dz, u, u, pg, q, k, v, ag, a, pool_w, pool_scale, bias, pack)


def _bwd_in(layer, part, token, dproj, x, norm_pre, dxo=None, w_in_t=None):
    want_dw, want_dx = part in ("both", "dw"), part in ("both", "dx")
    tm = TOKEN_TILE
    steps = SEQ // tm
    cw = 256

    def body(*refs):
        refs = list(refs)
        dp_ref, x_ref, g_ref = refs[1:4]
        del refs[:4]
        if want_dx:
            dxo_ref, w_ref, dx_ref, dgo_ref = refs[:4]
            del refs[:4]
            dg = refs.pop()
        if want_dw:
            dw_ref, acc = refs
        i = pl.program_id(0)

        @pl.when(i == 0)
        def _():
            if want_dw:
                acc[...] = jnp.zeros_like(acc)
            if want_dx:
                dg[...] = jnp.zeros_like(dg)

        xv = x_ref[...]
        gv = g_ref[layer:layer + 1, :]
        r = lax.rsqrt(jnp.mean(xv * xv, axis=-1, keepdims=True) + EPS)
        xn = xv * r
        if want_dw:
            hb = (xn * gv).astype(BF16)
            for c in range(0, D_IN, cw):
                acc[c:c + cw, :] += _tn(dp_ref[:, c:c + cw], hb)
        if want_dx:
            dh = _nn(dp_ref[...], w_ref[...])
            dg[...] += jnp.sum(dh * xn, axis=0, keepdims=True)
            dhn = dh * gv
            dx_ref[...] = dxo_ref[...] + r * (dhn - xn * jnp.mean(dhn * xn, axis=-1, keepdims=True))

        @pl.when(i == steps - 1)
        def _():
            if want_dw:
                dw_ref[...] = acc[...].astype(BF16)
            if want_dx:
                _rows_of(dg, dgo_ref, 0)

    row = lambda c: pl.BlockSpec((tm, c), lambda i: (i, 0))
    const = lambda shape: pl.BlockSpec(shape, lambda i: (0,) * len(shape))
    in_specs = [const((8, 128)), row(D_IN), row(D_MODEL), const((DEPTH, D_MODEL))]
    operands = [token, dproj, x, norm_pre]
    out_specs, out_shape, scratch = [], [], []
    if want_dx:
        in_specs += [row(D_MODEL), _resident((D_IN, D_MODEL))]
        operands += [dxo, w_in_t]
        out_specs += [row(D_MODEL), const((8, 128))]
        out_shape += [jax.ShapeDtypeStruct((SEQ, D_MODEL), F32), jax.ShapeDtypeStruct((8, 128), F32)]
    if want_dw:
        out_specs.append(const((D_IN, D_MODEL)))
        out_shape.append(jax.ShapeDtypeStruct((D_IN, D_MODEL), BF16))
        scratch.append(pltpu.VMEM((D_IN, D_MODEL), F32))
    if want_dx:
        scratch.append(pltpu.VMEM((1, D_MODEL), F32))
    return pl.pallas_call(
        body, name=f"bwd_in_{part}{layer}", grid=(steps,),
        in_specs=in_specs, out_specs=out_specs, out_shape=out_shape, scratch_shapes=scratch,
        compiler_params=_compiler_params(("arbitrary",)),
    )(*operands)


def _mesh_pos():
    return lax.axis_index("x"), lax.axis_index("y"), lax.axis_index("c")


def _device_rows(ref, m, px, py, pc):
    return ref.at[pl.ds(pl.multiple_of((4 * px + 2 * py + pc) * m, 16 if m % 16 == 0 else 8), m), :]


def _allgather(srcs, out_dtype, name, later=()):
    na, nb = len(srcs), len(later)
    every = list(srcs) + list(later)
    shapes = [(a.shape[-2], a.shape[-1]) for a, _ in every]

    def body(*refs):
        xs, refs = refs[:na + nb], refs[na + nb:]
        outs, cast, land, refs = refs[:na], refs[na:na + nb], refs[na + nb:na + 2 * nb], refs[na + 2 * nb:]
        stage, (send_sems, recv_sems, local_sems) = refs[:na], refs[na:]
        x, y, c = _mesh_pos()
        me, sibling = (x, y, c), (x, y, 1 - c)
        near = [(1 - x, y), (x, 1 - y)]
        far = (1 - x, 1 - y)
        relay_from, relay_to = (x ^ (1 - c), y ^ c), (x ^ c, y ^ (1 - c))
        k_from, k_to = 1 + c, 2 - c

        def slot(a, px, py, pc):
            return _device_rows(outs[a], shapes[a][0], px, py, pc)

        def copy(a, k, block, to, src=None):
            return pltpu.make_async_remote_copy(
                src_ref=slot(a, *block) if src is None else src, dst_ref=slot(a, *block),
                send_sem=send_sems.at[a, k], recv_sem=recv_sems.at[a, k], device_id=to, device_id_type=MESH)

        def cast_block(i):
            layer = every[i][1]
            return (xs[i][...] if layer is None else xs[i][layer]).astype(out_dtype)

        for a in range(na):
            stage[a][...] = cast_block(a)
        mine = [pltpu.make_async_copy(stage[a], slot(a, *me), local_sems.at[a]) for a in range(na)]
        for cp in mine:
            cp.start()
        sent = []
        for a in range(na):
            sent.append(copy(a, 0, me, sibling, src=stage[a]))
            sent += [copy(a, 1 + j, me, (*chip, c), src=stage[a]) for j, chip in enumerate(near)]
        for cp in sent:
            cp.start()
        for b in range(nb):
            cast[b][...] = cast_block(na + b)
            cp = pltpu.make_async_copy(cast[b], _device_rows(land[b], shapes[na + b][0], *me), local_sems.at[na + b])
            cp.start()
            mine.append(cp)
        for a in range(na):
            copy(a, k_from, (*relay_from, c), me).wait_recv()
            sent += [copy(a, 3, (*relay_from, c), (*relay_to, c)), copy(a, 3 + k_from, (*relay_from, c), sibling)]
            sent[-2].start()
            sent[-1].start()
        for a in range(na):
            copy(a, k_to, (*relay_to, c), me).wait_recv()
            sent.append(copy(a, 3 + k_to, (*relay_to, c), sibling))
            sent[-1].start()
        for a in range(na):
            copy(a, 3, (*far, c), me).wait_recv()
            sent.append(copy(a, 6, (*far, c), sibling))
            sent[-1].start()
        for a in range(na):
            copy(a, 0, sibling, me).wait_recv()
            for j, chip in enumerate(near + [far]):
                copy(a, 4 + j, (*chip, 1 - c), me).wait_recv()
        for cp in sent:
            cp.wait_send()
        for cp in mine:
            cp.wait()

    vmem = pl.BlockSpec(memory_space=pltpu.VMEM)
    hbm = pl.BlockSpec(memory_space=pl.ANY)
    gathered = [jax.ShapeDtypeStruct((N_DEV * m, n), out_dtype) for m, n in shapes]
    out = pl.pallas_call(
        body, name=name,
        in_specs=[vmem] * (na + nb),
        out_specs=[hbm] * na + [vmem] * nb + [hbm] * nb,
        out_shape=gathered[:na] + [jax.ShapeDtypeStruct(s, out_dtype) for s in shapes[na:]] + gathered[na:],
        scratch_shapes=([pltpu.VMEM(s, out_dtype) for s in shapes[:na]]
                        + [pltpu.SemaphoreType.DMA((na, 7)), pltpu.SemaphoreType.DMA((na, 7)),
                           pltpu.SemaphoreType.DMA((na + nb,))]),
        compiler_params=_compiler_params(),
    )(*[a for a, _ in every])
    return out[:na], out[na:na + nb], out[na + nb:]


def _gather_start(blocks, lands, name):
    na = len(blocks)

    def body(*refs):
        src, land, sems, token = refs[:na], refs[na:2 * na], refs[2 * na:4 * na], refs[-1]
        x, y, c = _mesh_pos()
        for a in range(na):
            for k in range(1, N_DEV):
                peer = (x ^ ((k >> 2) & 1), y ^ ((k >> 1) & 1), c ^ (k & 1))
                pltpu.make_async_remote_copy(
                    src_ref=src[a], dst_ref=_device_rows(land[a], blocks[a].shape[0], x, y, c),
                    send_sem=sems[2 * a].at[k - 1], recv_sem=sems[2 * a + 1].at[k - 1],
                    device_id=peer, device_id_type=MESH).start()
        token[...] = jnp.zeros_like(token)

    bufs = [pltpu.HBM(t.shape, t.dtype) for t in list(blocks) + list(lands)]
    out = pl.pallas_call(
        body, name=name,
        out_shape=(*([pltpu.SemaphoreType.DMA((N_DEV - 1,))] * (2 * na)), *bufs, jax.ShapeDtypeStruct((8, 128), F32)),
        in_specs=[_HBM] * (2 * na),
        out_specs=(*([_SEM] * (2 * na)), *([_HBM] * (2 * na)), pl.BlockSpec(memory_space=pltpu.VMEM)),
        input_output_aliases={i: 2 * na + i for i in range(2 * na)},
        compiler_params=pltpu.CompilerParams(has_side_effects=_EFFECT),
    )(*[pltpu.with_memory_space_constraint(t, pltpu.HBM) for t in list(blocks) + list(lands)])
    sems = [(out[2 * a], out[2 * a + 1]) for a in range(na)]
    return sems, out[2 * na:3 * na], out[3 * na:4 * na], out[-1]


def _gather_wait(sems, block, land, after, name):
    def body(src, land_ref, send_sem, recv_sem, after_ref, src_out, land_out):
        x, y, c = _mesh_pos()
        for k in range(1, N_DEV):
            peer = (x ^ ((k >> 2) & 1), y ^ ((k >> 1) & 1), c ^ (k & 1))
            cp = pltpu.make_async_remote_copy(
                src_ref=src, dst_ref=_device_rows(land_ref, block.shape[0], *peer),
                send_sem=send_sem.at[k - 1], recv_sem=recv_sem.at[k - 1], device_id=peer, device_id_type=MESH)
            cp.wait_send()
            cp.wait_recv()

    out = pl.pallas_call(
        body, name=name,
        out_shape=(pltpu.HBM(block.shape, block.dtype), pltpu.HBM(land.shape, land.dtype)),
        in_specs=[_HBM, _HBM, _SEM, _SEM, pl.BlockSpec(memory_space=pl.ANY)],
        out_specs=[_HBM, _HBM],
        input_output_aliases={0: 0, 1: 1},
        compiler_params=pltpu.CompilerParams(has_side_effects=_EFFECT),
    )(block, land, sems[0], sems[1], after)
    return out[1]


def _row_step(m):
    return next(s for s in (32, 24, 16, 8) if m % s == 0)


def _pair_reduce(arrs, name):
    na = len(arrs)

    def body(*refs):
        gs, hs, hm = refs[:na], refs[na:2 * na], refs[2 * na:3 * na]
        own, ra = refs[3 * na:4 * na], refs[4 * na:5 * na]
        d2d_send, d2d_recv, local_sems = refs[5 * na:]
        x, y, c = _mesh_pos()
        sibling = (x, y, 1 - c)
        loads, sends = [], []
        for a in range(na):
            for q in range(4):
                cp = pltpu.make_async_copy(gs[a].at[2 * q + c], own[a].at[q], local_sems.at[a, q])
                cp.start()
                loads.append(cp)
                cp = pltpu.make_async_remote_copy(
                    src_ref=gs[a].at[2 * q + (1 - c)], dst_ref=ra[a].at[q], send_sem=d2d_send.at[a, q],
                    recv_sem=d2d_recv.at[a, q], device_id=sibling, device_id_type=MESH)
                cp.start()
                sends.append(cp)
        for cp in loads:
            cp.wait()
        for cp in sends:
            cp.wait_recv()
        others = [2 * (1 - x) + y, 2 * x + (1 - y), 2 * (1 - x) + (1 - y)]
        for a in range(na):
            m = arrs[a].shape[1]
            step = _row_step(m)

            def add(i, carry, a=a, step=step):
                rs = pl.ds(pl.multiple_of(i * step, step), step)
                for j, q in enumerate(others):
                    hs[a][j, rs, :] = (own[a][q, rs, :].astype(F32) + ra[a][q, rs, :].astype(F32)).astype(hs[a].dtype)
                q = 2 * x + y
                hm[a][rs, :] = own[a][q, rs, :].astype(F32) + ra[a][q, rs, :].astype(F32)
                return carry

            lax.fori_loop(0, m // step, add, 0)
        for cp in sends:
            cp.wait_send()

    vmem = pl.BlockSpec(memory_space=pltpu.VMEM)
    scratch = [pltpu.VMEM((4,) + t.shape[1:], t.dtype) for t in arrs] * 2
    scratch += [pltpu.SemaphoreType.DMA((na, 4)), pltpu.SemaphoreType.DMA((na, 4)), pltpu.SemaphoreType.DMA((na, 4))]
    out = pl.pallas_call(
        body, name=name,
        in_specs=[pl.BlockSpec(memory_space=pl.ANY)] * na, out_specs=[vmem] * (2 * na),
        out_shape=([jax.ShapeDtypeStruct((3,) + t.shape[1:], t.dtype) for t in arrs]
                   + [jax.ShapeDtypeStruct(t.shape[1:], F32) for t in arrs]),
        scratch_shapes=scratch,
        compiler_params=_compiler_params(),
    )(*arrs)
    return out[:na], out[na:]


_HBM = pl.BlockSpec(memory_space=pltpu.HBM)
_SEM = pl.BlockSpec(memory_space=pltpu.SEMAPHORE)
_EFFECT = pltpu.SideEffectType.DATAFLOW_SIDE_EFFECTING


def _exchange_plan(direct):
    x, y, c = _mesh_pos()
    if not direct:
        return [(j, j, (qx, qy, c)) for j, (qx, qy) in enumerate([(1 - x, y), (x, 1 - y), (1 - x, 1 - y)])]
    plan = []
    for k in range(1, N_DEV):
        px, py, pc = x ^ ((k >> 2) & 1), y ^ ((k >> 1) & 1), c ^ (k & 1)
        plan.append((4 * px + 2 * py + pc, k - 1, (px, py, pc)))
    return plan


def _exchange_start(srcs, direct, name):
    na = len(srcs)
    slots = N_DEV - 1 if direct else 3

    def body(*refs):
        src, land = refs[:na], refs[na:2 * na]
        send_sem, recv_sem = refs[2 * na], refs[2 * na + 1]
        token = refs[-1]
        for block, slot, peer in _exchange_plan(direct):
            for a in range(na):
                pltpu.make_async_remote_copy(
                    src_ref=src[a].at[block], dst_ref=land[a].at[slot], send_sem=send_sem.at[slots * a + slot],
                    recv_sem=recv_sem.at[slots * a + slot], device_id=peer, device_id_type=MESH).start()
        token[...] = jnp.zeros_like(token)

    zones = [jax.ShapeDtypeStruct((slots,) + t.shape[1:], t.dtype) for t in srcs]
    bufs = [pltpu.HBM(t.shape, t.dtype) for t in list(srcs) + zones]
    out = pl.pallas_call(
        body, name=name,
        out_shape=(pltpu.SemaphoreType.DMA((slots * na,)), pltpu.SemaphoreType.DMA((slots * na,)), *bufs,
                   jax.ShapeDtypeStruct((8, 128), F32)),
        in_specs=[_HBM] * (2 * na),
        out_specs=(_SEM, _SEM, *([_HBM] * (2 * na)), pl.BlockSpec(memory_space=pltpu.VMEM)),
        input_output_aliases={i: 2 + i for i in range(2 * na)},
        compiler_params=pltpu.CompilerParams(has_side_effects=_EFFECT),
    )(*[pltpu.with_memory_space_constraint(t, pltpu.HBM) for t in srcs],
      *[pltpu.with_memory_space_constraint(lax.empty(t.shape, t.dtype), pltpu.HBM) for t in zones])
    return out[0], out[1], out[2:2 + na], out[2 + na:2 + 2 * na], out[-1]


def _exchange_wait(send_sem, recv_sem, srcs, lands, direct, after, name):
    na = len(srcs)
    slots = N_DEV - 1 if direct else 3

    def body(*refs):
        src, land = refs[:na], refs[na:2 * na]
        send_sem_ref, recv_sem_ref = refs[2 * na], refs[2 * na + 1]
        for block, slot, peer in _exchange_plan(direct):
            for a in range(na):
                cp = pltpu.make_async_remote_copy(
                    src_ref=src[a].at[block], dst_ref=land[a].at[slot], send_sem=send_sem_ref.at[slots * a + slot],
                    recv_sem=recv_sem_ref.at[slots * a + slot], device_id=peer, device_id_type=MESH)
                cp.wait_send()
                cp.wait_recv()

    bufs = [pltpu.HBM(t.shape, t.dtype) for t in list(srcs) + list(lands)]
    out = pl.pallas_call(
        body, name=name,
        out_shape=tuple(bufs),
        in_specs=[_HBM] * (2 * na) + [_SEM, _SEM, pl.BlockSpec(memory_space=pl.ANY)],
        out_specs=[_HBM] * (2 * na),
        input_output_aliases={i: i for i in range(2 * na)},
        compiler_params=pltpu.CompilerParams(has_side_effects=_EFFECT),
    )(*srcs, *lands, send_sem, recv_sem, after)
    return out[:na], out[na:]


def _own_then_slots(mine_ref, lands_ref, rows=slice(None)):
    if len(mine_ref.shape) == 3:
        x, y, c = _mesh_pos()
        total = mine_ref[4 * x + 2 * y + c, rows, :].astype(F32)
    else:
        total = mine_ref[rows, :].astype(F32)
    for j in range(lands_ref.shape[0]):
        total = total + lands_ref[j, rows, :].astype(F32)
    return total


SMALL_ROWS = 2 * PACK_SLICE + 2 * 8


def _small_block(mine, lands, dgpre, name):
    def body(*refs):
        hm, ld, dg = refs[:DEPTH], refs[DEPTH:2 * DEPTH], refs[2 * DEPTH:3 * DEPTH]
        blk, land, sem = refs[3 * DEPTH:]
        for l in range(DEPTH):
            blk[PACK_SLICE * l:PACK_SLICE * (l + 1), :] = _own_then_slots(hm[l], ld[l])
            blk[2 * PACK_SLICE + 8 * l:2 * PACK_SLICE + 8 * (l + 1), :] = dg[l][...]
        cp = pltpu.make_async_copy(blk, _device_rows(land, SMALL_ROWS, *_mesh_pos()), sem)
        cp.start()
        cp.wait()

    vmem = pl.BlockSpec(memory_space=pltpu.VMEM)
    return pl.pallas_call(
        body, name=name,
        in_specs=[vmem] * (3 * DEPTH), out_specs=[vmem, pl.BlockSpec(memory_space=pl.ANY)],
        out_shape=[jax.ShapeDtypeStruct((SMALL_ROWS, 128), F32), jax.ShapeDtypeStruct((N_DEV * SMALL_ROWS, 128), F32)],
        scratch_shapes=[pltpu.SemaphoreType.DMA],
        compiler_params=_compiler_params(),
    )(*mine, *lands, *dgpre)


def _adamw_math(w, g, m, v):
    m = ADAM_B1 * m + (1.0 - ADAM_B1) * g
    v = ADAM_B2 * v + (1.0 - ADAM_B2) * (g * g)
    m_hat = m / (1.0 - ADAM_B1 ** ADAM_STEP)
    v_hat = v / (1.0 - ADAM_B2 ** ADAM_STEP)
    delta = -ADAM_LR * (m_hat / (jnp.sqrt(v_hat) + ADAM_EPS) + ADAM_WD * w)
    return delta, m, v


def _adamw_layer(layer, mine, lands, w, m, v, earlier, token, name, rows):
    _, mm, nn = w.shape

    def body(hm_ref, ld_ref, w_ref, m_ref, v_ref, _, *refs):
        g_ref, d_ref, nm_ref, nv_ref = refs[-4:]
        g = _own_then_slots(hm_ref, ld_ref)
        g_ref[...] = g
        d, nm, nv = _adamw_math(w_ref[...], g, m_ref[...], v_ref[...])
        d_ref[...] = d
        nm_ref[...] = nm
        nv_ref[...] = nv

    spec = pl.BlockSpec((None, rows, nn), lambda i: (layer, i, 0))
    carried = [] if earlier is None else list(earlier)
    return pl.pallas_call(
        body, name=name, grid=(mm // rows,),
        in_specs=([pl.BlockSpec((rows, nn), lambda i: (i, 0)) if mine.ndim == 2
                   else pl.BlockSpec((N_DEV, rows, nn), lambda i: (0, i, 0)),
                   pl.BlockSpec((lands.shape[0], rows, nn), lambda i: (0, i, 0)),
                   spec, spec, spec] + [pl.BlockSpec(memory_space=pl.ANY)] * (1 + len(carried))),
        out_specs=[spec] * 4,
        out_shape=[jax.ShapeDtypeStruct(w.shape, F32)] * 4,
        input_output_aliases={6 + t: t for t in range(len(carried))},
        compiler_params=_compiler_params(("arbitrary",)),
    )(mine, lands, w, m, v, token, *carried)


def _adamw_small(gathered, params):
    def body(all_ref, *refs):
        ins, outs, packs = refs[:15], refs[15:15 + 21], refs[15 + 21]
        loss_ref = outs[0]
        for dev in range(N_DEV):
            for l in range(DEPTH):
                packs[l, PACK_SLICE * dev:PACK_SLICE * (dev + 1), :] = (
                    all_ref[SMALL_ROWS * dev + PACK_SLICE * l:SMALL_ROWS * dev + PACK_SLICE * (l + 1), :])
        loss_ref[...] = packs[DEPTH - 1, ROW_LOSS:ROW_LOSS + 1, 0:1]

        def update(p, sel, g):
            w_ref, m_ref, v_ref = ins[p], ins[5 + p], ins[10 + p]
            d, nm, nv = _adamw_math(w_ref[sel], g, m_ref[sel], v_ref[sel])
            for t, val in enumerate((g, d, nm, nv)):
                outs[1 + 5 * t + p][sel] = val

        for l in range(DEPTH):
            gp = packs.at[l]
            row0 = 2 * PACK_SLICE + 8 * l
            dgpre = all_ref[row0:row0 + 8, :]
            for dev in range(1, N_DEV):
                dgpre = dgpre + all_ref[SMALL_ROWS * dev + row0:SMALL_ROWS * dev + row0 + 8, :]
            for grp in range(4):
                update(0, (l, grp), gp[ROW_PW + BLOCK * grp:ROW_PW + BLOCK * (grp + 1), :])
                update(1, (slice(l, l + 1), slice(128 * grp, 128 * (grp + 1))), gp[ROW_SC + grp:ROW_SC + grp + 1, :])
            update(2, (slice(l, l + 1), slice(None)), gp[ROW_SINK:ROW_SINK + 1, 0:N_HEADS])
            for r in range(D_MODEL // 128):
                sel = (slice(l, l + 1), slice(128 * r, 128 * (r + 1)))
                update(3, sel, dgpre[r:r + 1, :])
                update(4, sel, gp[ROW_NPOST + r:ROW_NPOST + r + 1, :])

    shapes = [jax.ShapeDtypeStruct(p.shape, F32) for p in params[:5]]
    return pl.pallas_call(
        body, name="adamw_small",
        out_shape=[jax.ShapeDtypeStruct((1, 1), F32)] + shapes * 4,
        scratch_shapes=[pltpu.VMEM((DEPTH, PACK_ROWS, 128), F32)],
        compiler_params=_compiler_params(),
    )(gathered, *params)


def kernel(x, w_in, pool_w, pool_scale, attn_sinks, w_out, norm_pre, norm_post, loss_target, m_w_in, m_pool_w, m_pool_scale, m_attn_sinks, m_w_out, m_norm_pre, m_norm_post, v_w_in, v_pool_w, v_pool_scale, v_attn_sinks, v_w_out, v_norm_pre, v_norm_post):
    x0 = x.reshape(SEQ, D_MODEL)
    target = loss_target.reshape(SEQ, D_MODEL)
    bias = jnp.asarray(_attn_bias())
    w_in_t, m_in_t, v_in_t = (jnp.swapaxes(t, 1, 2) for t in (w_in, m_w_in, v_w_in))

    (win0, wout0), later, lands = _allgather([(w_in_t, 0), (w_out, 0)], BF16, "gather_w0",
                                              later=[(w_in_t, 1), (w_out, 1)])
    sems, later, lands, token = _gather_start(later, lands, "gather_w1_start")
    win_full, wout_full = [win0, None], [wout0, None]

    saved = []
    xl = x0
    for layer in range(DEPTH):
        if layer > 0:
            win_full[layer] = _gather_wait(sems[0], later[0], lands[0], xl, "gather_w_in1_wait")
        u, pg, q, k, v, ag, z, a = _fwd_front(layer, xl, norm_pre, win_full[layer], token, attn_sinks,
                                             pool_w, pool_scale, bias)
        if layer > 0:
            wout_full[layer] = _gather_wait(sems[1], later[1], lands[1], z, "gather_w_out1_wait")
        x_next, y = _fwd_out(layer, z, xl, norm_post, wout_full[layer])
        saved.append((xl, u, pg, q, k, v, ag, z, a, y))
        xl = x_next

    params_small = [pool_w, pool_scale, attn_sinks, norm_pre, norm_post,
                    m_pool_w, m_pool_scale, m_attn_sinks, m_norm_pre, m_norm_post,
                    v_pool_w, v_pool_scale, v_attn_sinks, v_norm_pre, v_norm_post]

    def start(blocks, direct, tag):
        srcs, mine = (blocks, None) if direct else _pair_reduce(blocks, f"pair_reduce{tag}")
        send_sem, recv_sem, srcs, lands, started = _exchange_start(srcs, direct, f"exchange_start{tag}")
        return (send_sem, recv_sem, srcs, lands, mine, direct), started

    def finish(handle, after, tag):
        send_sem, recv_sem, srcs, lands, mine, direct = handle
        srcs, lands = _exchange_wait(send_sem, recv_sem, srcs, lands, direct, after, f"exchange_wait{tag}")
        return (srcs if mine is None else mine), lands

    xin, u, pg, q, k, v, ag, z, a, y = saved[1]
    dx, dz, gw_out, pack = _bwd_out(1, True, xl, target, y, z, norm_post, wout_full[1])
    dproj, pack = _bwd_mix(1, attn_sinks, dz, u, pg, q, k, v, ag, a, pool_w, pool_scale, bias, pack)
    dgpre = [None] * DEPTH
    dx, dgpre[1], gw_in_t = _bwd_in(1, "both", token, dproj, xin, norm_pre, dx, win_full[1])
    top, token = start([gw_in_t.reshape(N_DEV, IN_SHARD, D_MODEL), gw_out.reshape(N_DEV, OUT_SHARD, D_MODEL),
                        pack.reshape(N_DEV, PACK_SLICE, 128)], True, "1")

    xin, u, pg, q, k, v, ag, z, a, y = saved[0]
    dz, gw_out, pack = _bwd_out(0, False, dx, token, y, z, norm_post, wout_full[0])
    dproj, pack = _bwd_mix(0, attn_sinks, dz, u, pg, q, k, v, ag, a, pool_w, pool_scale, bias, pack)
    early, token = start([gw_out.reshape(N_DEV, OUT_SHARD, D_MODEL), pack.reshape(N_DEV, PACK_SLICE, 128)], True, "0a")
    (gw_in_t,) = _bwd_in(0, "dw", token, dproj, xin, norm_pre)
    late, token = start([gw_in_t.reshape(N_DEV, IN_SHARD, D_MODEL)], False, "0b")
    dx, dgpre[0] = _bwd_in(0, "dx", token, dproj, xin, norm_pre, dx, win_full[0])

    own1, lands1 = finish(top, dx, "1")
    big_in = _adamw_layer(1, own1[0], lands1[0], w_in_t, m_in_t, v_in_t, None, token, "adamw_in1", 96)
    big_out = _adamw_layer(1, own1[1], lands1[1], w_out, m_w_out, v_w_out, None, token, "adamw_out1", 128)
    own0a, lands0a = finish(early, big_out[0], "0a")
    big_out = _adamw_layer(0, own0a[0], lands0a[0], w_out, m_w_out, v_w_out, big_out, token, "adamw_out0", 128)
    block, land = _small_block([own0a[1], own1[2]], [lands0a[1], lands1[2]], dgpre, "small_block")
    sems, block, land, token = _gather_start([block], [land], "gather_small_start")
    own0b, lands0b = finish(late, big_out[0], "0b")
    big_in = _adamw_layer(0, own0b[0], lands0b[0], w_in_t, m_in_t, v_in_t, big_in, token, "adamw_in0", 96)
    gathered = _gather_wait(sems[0], block[0], land[0], big_in[0], "gather_small_wait")
    small_out = _adamw_small(gathered, params_small)
    loss = small_out[0].reshape(())

    outs = [loss, dx.reshape(1, SEQ, D_MODEL)]
    for t in range(4):
        pw_, sc_, sk_, npre_, npost_ = small_out[1 + 5 * t:6 + 5 * t]
        outs += [jnp.swapaxes(big_in[t], 1, 2), pw_, sc_, sk_, big_out[t], npre_, npost_]
    return tuple(outs)
```

```python
import numpy as np
import jax
import jax.numpy as jnp
from jax import lax
from jax.experimental import pallas as pl
from jax.experimental.pallas import tpu as pltpu

F32 = jnp.float32
BF16 = jnp.bfloat16

N_DEV = 8
SEQ = 2048
D_MODEL = 1024
D_POOL = 512
D_ATTN = 512
D_KV = 128
D_IN = 2304
N_HEADS = 8
GQA = 4
HEAD_DIM = 64
BLOCK = 128
N_BLOCKS = SEQ // BLOCK
POOL_WINDOWS = (2, 4, 8, 16)
DEPTH = 2
EPS = 1e-6
NEG_INF = -1e30
SCALE = HEAD_DIM ** -0.5
IN_SHARD = D_IN // N_DEV
OUT_SHARD = D_MODEL // N_DEV

COL_U, COL_PG, COL_Q, COL_K, COL_V, COL_AG = 0, 512, 1024, 1536, 1664, 1792

ADAM_LR = 0.001
ADAM_B1 = 0.9
ADAM_B2 = 0.999
ADAM_EPS = 1e-08
ADAM_WD = 0.01
ADAM_STEP = 10

TOKEN_TILE = 512
VMEM_LIMIT = 56 * 1024 * 1024
MESH = pl.DeviceIdType.MESH

ROW_PW, ROW_SC, ROW_SINK, ROW_NPRE, ROW_NPOST, ROW_LOSS = 0, 512, 520, 528, 536, 544
PACK_ROWS = 576
PACK_SLICE = PACK_ROWS // N_DEV


def _nn(a, b):
    return jnp.dot(a, b, preferred_element_type=F32)


def _nt(a, b):
    return lax.dot_general(a, b, (((1,), (1,)), ((), ())), preferred_element_type=F32)


def _tn(a, b):
    return lax.dot_general(a, b, (((0,), (0,)), ((), ())), preferred_element_type=F32)


def _silu_parts(g):
    s = jax.nn.sigmoid(g)
    return g * s, s * (1.0 + g * (1.0 - s))


def _resident(shape):
    return pl.BlockSpec(shape, lambda *_: (0,) * len(shape), pipeline_mode=pl.Buffered(1))


def _compiler_params(sem=None):
    if sem is None:
        return pltpu.CompilerParams(vmem_limit_bytes=VMEM_LIMIT)
    return pltpu.CompilerParams(dimension_semantics=sem, vmem_limit_bytes=VMEM_LIMIT)


def _attn_bias():
    t = np.arange(BLOCK)[None, :]
    j = np.arange(2 * BLOCK)[:, None]
    dist = t + BLOCK - j
    in_win = (dist >= 0) & (dist < BLOCK)
    out = np.zeros((2, 2, 2 * BLOCK, GQA * BLOCK), np.float32)
    for variant in range(2):
        valid = in_win & ((j >= BLOCK) | (variant == 1))
        for kv in range(2):
            for g in range(GQA):
                slope = np.float32(2.0 ** (-(kv * GQA + g + 1)))
                b = np.where(valid, -slope * dist.astype(np.float32), np.float32(NEG_INF))
                out[variant, kv, :, g * BLOCK:(g + 1) * BLOCK] = b
    return out


def _replicate_head(kx, kv):
    rolled = pltpu.roll(kx, 64, 1)
    lane = lax.broadcasted_iota(jnp.int32, kx.shape, 1)
    dup = jnp.where(lane < 64, kx, rolled) if kv == 0 else jnp.where(lane < 64, rolled, kx)
    return jnp.concatenate([dup, dup], axis=1).astype(BF16)


def _stack_heads(qv):
    lane = lax.broadcasted_iota(jnp.int32, qv.shape, 1)
    zero = jnp.zeros_like(qv)
    return jnp.concatenate([jnp.where((lane >= 64 * g) & (lane < 64 * g + 64), qv, zero) for g in range(GQA)], axis=0)


def _unstack_heads(xs):
    lane = lax.broadcasted_iota(jnp.int32, (BLOCK, 256), 1)
    return jnp.where(lane < 64, xs[0:128], jnp.where(lane < 128, xs[128:256], jnp.where(lane < 192, xs[256:384], xs[384:512])))


def _fold_heads(r):
    h = r[:, 0:128] + r[:, 128:256]
    return h + pltpu.roll(h, 64, 1)


def _sink_row(sink_ref, layer, kv):
    lane = lax.broadcasted_iota(jnp.int32, (1, GQA * BLOCK), 1)
    s4 = [sink_ref[layer, kv * GQA + g] for g in range(GQA)]
    return jnp.where(lane < 128, s4[0], jnp.where(lane < 256, s4[1], jnp.where(lane < 384, s4[2], s4[3])))


def _probs_keys_major(k_rep, q_st, bias, sink):
    st = _nt(k_rep, q_st) * SCALE + bias
    m = jnp.maximum(jnp.max(st, axis=0, keepdims=True), sink)
    p = jnp.exp(st - m)
    esink = jnp.exp(sink - m)
    rl = 1.0 / (jnp.sum(p, axis=0, keepdims=True) + esink)
    return p * rl, esink * rl


WINDOW_HALO = 16


def _window_sum(ext, w, forward):
    s = ext
    sh = 1
    while sh < w:
        s = s + pltpu.roll(s, (ext.shape[0] - sh) if forward else sh, 0)
        sh *= 2
    return s


def _inv_count(n, w):
    t = n * BLOCK + lax.broadcasted_iota(jnp.int32, (BLOCK, 1), 0) + 1
    return 1.0 / jnp.minimum(t.astype(F32), float(w))


def _kv_ext(ref, n):
    r0 = pl.multiple_of(jnp.maximum(n - 1, 0) * BLOCK, BLOCK)
    r1 = pl.multiple_of(n * BLOCK, BLOCK)
    return jnp.concatenate([ref[pl.ds(r0, BLOCK), :], ref[pl.ds(r1, BLOCK), :]], axis=0)


def _rows_of(vec_ref, pack_ref, row0):
    for r in range(D_MODEL // 128):
        pack_ref[row0 + r:row0 + r + 1, :] = vec_ref[:, 128 * r:128 * (r + 1)]


FRONT_TILE = 2 * BLOCK


def _fwd_front(layer, x, norm_pre, w_in_t, token, sinks, pool_w, pool_scale, bias):
    tm = FRONT_TILE

    def body(sink_ref, x_ref, g_ref, w_ref, _, pw_ref, sc_ref, bias_ref,
             u_ref, pg_ref, q_ref, k_ref, v_ref, ag_ref, z_ref, a_ref, uprev, kprev, vprev):
        i = pl.program_id(0)

        @pl.when(i == 0)
        def _():
            uprev[...] = jnp.zeros_like(uprev)
            kprev[...] = jnp.zeros_like(kprev)
            vprev[...] = jnp.zeros_like(vprev)

        xv = x_ref[...]
        r = lax.rsqrt(jnp.mean(xv * xv, axis=-1, keepdims=True) + EPS)
        h = (xv * r * g_ref[layer:layer + 1, :]).astype(BF16)
        u_ref[...] = _nt(h, w_ref[COL_U:COL_PG, :])
        pg_ref[...] = _nt(h, w_ref[COL_PG:COL_Q, :])
        for sb in range(tm // BLOCK):
            n = (tm // BLOCK) * i + sb
            rows = slice(BLOCK * sb, BLOCK * (sb + 1))
            before = slice(BLOCK * (sb - 1), BLOCK * sb)
            uv = u_ref[rows, :]
            halo = (uprev[BLOCK - WINDOW_HALO:, :] if sb == 0
                    else u_ref[BLOCK * sb - WINDOW_HALO:BLOCK * sb, :])
            ext = jnp.concatenate([halo, uv], axis=0)
            for g, w in enumerate(POOL_WINDOWS):
                cs = slice(BLOCK * g, BLOCK * (g + 1))
                win = _window_sum(ext[:, cs], w, forward=False)[WINDOW_HALO:]
                pooled = win * _inv_count(n, w) - uv[:, cs]
                mixed = _nn(pooled.astype(BF16), pw_ref[g].astype(BF16))
                gate, _ = _silu_parts(pg_ref[rows, cs])
                z_ref[rows, cs] = (mixed * sc_ref[layer:layer + 1, cs] * gate).astype(BF16)

        q_ref[...] = _nt(h, w_ref[COL_Q:COL_K, :]).astype(BF16)
        k_ref[...] = _nt(h, w_ref[COL_K:COL_V, :])
        v_ref[...] = _nt(h, w_ref[COL_V:COL_AG, :])
        ag_ref[...] = _nt(h, w_ref[COL_AG:D_IN, :])

        for sb in range(tm // BLOCK):
            n = (tm // BLOCK) * i + sb
            rows = slice(BLOCK * sb, BLOCK * (sb + 1))
            before = slice(BLOCK * (sb - 1), BLOCK * sb)
            kx = jnp.concatenate([kprev[...] if sb == 0 else k_ref[before, :], k_ref[rows, :]], axis=0)
            vx = jnp.concatenate([vprev[...] if sb == 0 else v_ref[before, :], v_ref[rows, :]], axis=0)
            variant = jnp.minimum(n, 1) if sb == 0 else 1
            for kv in range(2):
                cs = slice(256 * kv, 256 * (kv + 1))
                p, _ = _probs_keys_major(_replicate_head(kx, kv), _stack_heads(q_ref[rows, cs]),
                                         bias_ref[variant, kv], _sink_row(sink_ref, layer, kv))
                o = _unstack_heads(_tn(p.astype(BF16), _replicate_head(vx, kv)))
                a_ref[rows, cs] = o
                gate, _ = _silu_parts(ag_ref[rows, cs])
                z_ref[rows, D_POOL + 256 * kv:D_POOL + 256 * (kv + 1)] = (o * gate).astype(BF16)

        tail = slice(tm - BLOCK, tm)
        uprev[...] = u_ref[tail, :]
        kprev[...] = k_ref[tail, :]
        vprev[...] = v_ref[tail, :]

    row = lambda c: pl.BlockSpec((tm, c), lambda i: (i, 0))
    const = lambda shape: pl.BlockSpec(shape, lambda i: (0,) * len(shape))
    return pl.pallas_call(
        body, name=f"fwd_front{layer}", grid=(SEQ // tm,),
        in_specs=[pl.BlockSpec(memory_space=pltpu.SMEM), row(D_MODEL), const((DEPTH, D_MODEL)),
                  _resident((D_IN, D_MODEL)), const((8, 128)),
                  pl.BlockSpec((None, 4, BLOCK, BLOCK), lambda i: (layer, 0, 0, 0)), const((DEPTH, D_POOL)),
                  _resident((2, 2, 2 * BLOCK, GQA * BLOCK))],
        out_specs=[row(D_POOL), row(D_POOL), row(D_ATTN), row(D_KV), row(D_KV), row(D_ATTN), row(D_MODEL),
                   row(D_ATTN)],
        out_shape=[jax.ShapeDtypeStruct((SEQ, D_POOL), F32), jax.ShapeDtypeStruct((SEQ, D_POOL), F32),
                   jax.ShapeDtypeStruct((SEQ, D_ATTN), BF16), jax.ShapeDtypeStruct((SEQ, D_KV), F32),
                   jax.ShapeDtypeStruct((SEQ, D_KV), F32), jax.ShapeDtypeStruct((SEQ, D_ATTN), F32),
                   jax.ShapeDtypeStruct((SEQ, D_MODEL), BF16), jax.ShapeDtypeStruct((SEQ, D_ATTN), F32)],
        scratch_shapes=[pltpu.VMEM((BLOCK, D_POOL), F32), pltpu.VMEM((BLOCK, D_KV), F32),
                        pltpu.VMEM((BLOCK, D_KV), F32)],
        compiler_params=_compiler_params(("arbitrary",)),
    )(sinks, x, norm_pre, w_in_t, token, pool_w, pool_scale, bias)


def _fwd_out(layer, z, x, norm_post, w_out):
    tm = TOKEN_TILE

    def body(z_ref, x_ref, g_ref, w_ref, xn_ref, y_ref):
        y = _nn(z_ref[...], w_ref[...])
        y_ref[...] = y
        r = lax.rsqrt(jnp.mean(y * y, axis=-1, keepdims=True) + EPS)
        xn_ref[...] = x_ref[...] + y * r * g_ref[layer:layer + 1, :]

    row = lambda c: pl.BlockSpec((tm, c), lambda i: (i, 0))
    return pl.pallas_call(
        body, name=f"fwd_out{layer}", grid=(SEQ // tm,),
        in_specs=[row(D_MODEL), row(D_MODEL), pl.BlockSpec((DEPTH, D_MODEL), lambda i: (0, 0)),
                  _resident((D_MODEL, D_MODEL))],
        out_specs=[row(D_MODEL), row(D_MODEL)],
        out_shape=[jax.ShapeDtypeStruct((SEQ, D_MODEL), F32), jax.ShapeDtypeStruct((SEQ, D_MODEL), F32)],
        compiler_params=_compiler_params(("arbitrary",)),
    )(z, x, norm_post, w_out)


def _bwd_out(layer, top, dxo_or_xf, target_or_token, y, z, norm_post, w_out):
    tm = TOKEN_TILE
    steps = SEQ // tm

    def body(*refs):
        if top:
            xf_ref, t_ref, y_ref, z_ref, g_ref, w_ref, dxo_ref, dz_ref, dw_ref, pack_ref, acc, dg, lacc = refs
        else:
            dxi_ref, _, y_ref, z_ref, g_ref, w_ref, dz_ref, dw_ref, pack_ref, acc, dg, lacc = refs
        i = pl.program_id(0)

        @pl.when(i == 0)
        def _():
            acc[...] = jnp.zeros_like(acc)
            dg[...] = jnp.zeros_like(dg)
            lacc[...] = jnp.zeros_like(lacc)
            pack_ref[...] = jnp.zeros_like(pack_ref)

        if top:
            d = xf_ref[...] - t_ref[...]
            dxo_v = d * (1.0 / D_MODEL)
            dxo_ref[...] = dxo_v
            part = jnp.sum(d * d, axis=-1, keepdims=True) * (1.0 / D_MODEL)
            lacc[...] += 0.5 * jnp.sum(part, axis=0, keepdims=True)
        else:
            dxo_v = dxi_ref[...]
        y = y_ref[...]
        r = lax.rsqrt(jnp.mean(y * y, axis=-1, keepdims=True) + EPS)
        yn = y * r
        dg[...] += jnp.sum(dxo_v * yn, axis=0, keepdims=True)
        dyn = dxo_v * g_ref[layer:layer + 1, :]
        dy = (r * (dyn - yn * jnp.mean(dyn * yn, axis=-1, keepdims=True))).astype(BF16)
        dz_ref[...] = _nt(dy, w_ref[...])
        acc[...] += _tn(z_ref[...], dy)

        @pl.when(i == steps - 1)
        def _():
            dw_ref[...] = acc[...].astype(BF16)
            _rows_of(dg, pack_ref, ROW_NPOST)
            lane = lax.broadcasted_iota(jnp.int32, (1, 128), 1)
            pack_ref[ROW_LOSS:ROW_LOSS + 1, :] = jnp.where(lane == 0, lacc[...], 0.0)

    row = lambda c: pl.BlockSpec((tm, c), lambda i: (i, 0))
    const = lambda shape: pl.BlockSpec(shape, lambda i: (0,) * len(shape))
    act = jax.ShapeDtypeStruct((SEQ, D_MODEL), F32)
    return pl.pallas_call(
        body, name=f"bwd_out{layer}", grid=(steps,),
        in_specs=([row(D_MODEL), row(D_MODEL) if top else const((8, 128))]
                  + [row(D_MODEL), row(D_MODEL), const((DEPTH, D_MODEL)), _resident((D_MODEL, D_MODEL))]),
        out_specs=([row(D_MODEL)] * (2 if top else 1) + [const((D_MODEL, D_MODEL)), const((PACK_ROWS, 128))]),
        out_shape=([act] * (2 if top else 1)
                   + [jax.ShapeDtypeStruct((D_MODEL, D_MODEL), BF16), jax.ShapeDtypeStruct((PACK_ROWS, 128), F32)]),
        scratch_shapes=[pltpu.VMEM((D_MODEL, D_MODEL), F32), pltpu.VMEM((1, D_MODEL), F32), pltpu.VMEM((1, 1), F32)],
        compiler_params=_compiler_params(("arbitrary",)),
    )(dxo_or_xf, target_or_token, y, z, norm_post, w_out)


def _bwd_mix(layer, sinks, dz, u, pg, q, k, v, ag, a, pool_w, pool_scale, bias, pack):
    last = N_BLOCKS - 1

    def body(sink_ref, dz_ref, u_ref, up_ref, pg_ref, q_ref, k_ref, v_ref, ag_ref, a_ref, pw_ref, sc_ref,
             bias_ref, pin_ref, dp_ref, pack_ref, ck, cv, ce):
        i = pl.program_id(0)
        n = last - i

        @pl.when(i == 0)
        def _():
            ck[...] = jnp.zeros_like(ck)
            cv[...] = jnp.zeros_like(cv)
            ce[...] = jnp.zeros_like(ce)
            pack_ref[...] = pin_ref[...]

        uv = u_ref[...]
        has_prev = (n > 0).astype(F32)
        ext = jnp.concatenate([up_ref[BLOCK - WINDOW_HALO:, :] * has_prev, uv], axis=0)
        for g, w in enumerate(POOL_WINDOWS):
            cs = slice(BLOCK * g, BLOCK * (g + 1))
            inv = _inv_count(n, w)
            win = _window_sum(ext[:, cs], w, forward=False)[WINDOW_HALO:]
            pooled = win * inv - uv[:, cs]
            pw_g = pw_ref[g].astype(BF16)
            mixed = _nn(pooled.astype(BF16), pw_g)
            gate, dgate = _silu_parts(pg_ref[:, cs])
            dzp = dz_ref[:, cs]
            sc = sc_ref[layer:layer + 1, cs]
            dpm = dzp * gate
            dp_ref[:, COL_PG + BLOCK * g:COL_PG + BLOCK * (g + 1)] = (dzp * (mixed * sc) * dgate).astype(BF16)
            pack_ref[ROW_SC + g:ROW_SC + g + 1, :] += jnp.sum(dpm * mixed, axis=0, keepdims=True)
            dmixed = (dpm * sc).astype(BF16)
            pack_ref[ROW_PW + BLOCK * g:ROW_PW + BLOCK * (g + 1), :] += _tn(pooled.astype(BF16), dmixed)
            dpooled = _nt(dmixed, pw_g)
            e = dpooled * inv
            lead = _window_sum(jnp.concatenate([e, ce[:WINDOW_HALO, cs]], axis=0), w, forward=True)[:BLOCK]
            dp_ref[:, COL_U + BLOCK * g:COL_U + BLOCK * (g + 1)] = (lead - dpooled).astype(BF16)
            ce[:, cs] = e

        kx = _kv_ext(k_ref, n)
        vx = _kv_ext(v_ref, n)
        lane = lax.broadcasted_iota(jnp.int32, (1, 128), 1)
        dsink_row = jnp.zeros((1, 128), F32)
        tks, tvs = [], []
        for kv in range(2):
            cs = slice(256 * kv, 256 * (kv + 1))
            k_rep = _replicate_head(kx, kv)
            v_rep = _replicate_head(vx, kv)
            q_st = _stack_heads(q_ref[:, cs])
            gate, dgate = _silu_parts(ag_ref[:, cs])
            dza = dz_ref[:, D_POOL + 256 * kv:D_POOL + 256 * (kv + 1)]
            dp_ref[:, COL_AG + 256 * kv:COL_AG + 256 * (kv + 1)] = (dza * a_ref[:, cs] * dgate).astype(BF16)
            da_st = _stack_heads((dza * gate).astype(BF16))
            p, psink = _probs_keys_major(k_rep, q_st, bias_ref[0, kv], _sink_row(sink_ref, layer, kv))
            dpt = _nt(v_rep, da_st)
            delta = jnp.sum(p * dpt, axis=0, keepdims=True)
            dst = (p * (dpt - delta) * SCALE).astype(BF16)
            sink_terms = psink * delta
            for g in range(GQA):
                dsink = -jnp.sum(sink_terms[:, BLOCK * g:BLOCK * (g + 1)], axis=1, keepdims=True)
                dsink_row = dsink_row + jnp.where(lane == kv * GQA + g, dsink, 0.0)
            dp_ref[:, COL_Q + 256 * kv:COL_Q + 256 * (kv + 1)] = _unstack_heads(_tn(dst, k_rep)).astype(BF16)
            tks.append(_fold_heads(_nn(dst, q_st)))
            tvs.append(_fold_heads(_nn(p.astype(BF16), da_st)))
        pack_ref[ROW_SINK:ROW_SINK + 1, :] += dsink_row
        lane2 = lax.broadcasted_iota(jnp.int32, (256, 128), 1)
        dkx = jnp.where(lane2 < 64, tks[0], tks[1])
        dvx = jnp.where(lane2 < 64, tvs[0], tvs[1])
        dp_ref[:, COL_K:COL_V] = (ck[...] + dkx[BLOCK:]).astype(BF16)
        dp_ref[:, COL_V:COL_AG] = (cv[...] + dvx[BLOCK:]).astype(BF16)
        ck[...] = dkx[:BLOCK]
        cv[...] = dvx[:BLOCK]

    blk = lambda c: pl.BlockSpec((BLOCK, c), lambda i: (last - i, 0))
    full = lambda shape: pl.BlockSpec(shape, lambda i: (0,) * len(shape))
    return pl.pallas_call(
        body, name=f"bwd_mix{layer}", grid=(N_BLOCKS,),
        in_specs=[pl.BlockSpec(memory_space=pltpu.SMEM), blk(D_MODEL), blk(D_POOL),
                  pl.BlockSpec((BLOCK, D_POOL), lambda i: (jnp.maximum(last - i - 1, 0), 0)),
                  blk(D_POOL), blk(D_ATTN), full((SEQ, D_KV)), full((SEQ, D_KV)), blk(D_ATTN), blk(D_ATTN),
                  pl.BlockSpec((None, 4, BLOCK, BLOCK), lambda i: (layer, 0, 0, 0)), full((DEPTH, D_POOL)),
                  pl.BlockSpec((1, 2, 2 * BLOCK, GQA * BLOCK), lambda i: (jnp.minimum(last - i, 1), 0, 0, 0)),
                  full((PACK_ROWS, 128))],
        out_specs=[blk(D_IN), full((PACK_ROWS, 128))],
        out_shape=[jax.ShapeDtypeStruct((SEQ, D_IN), BF16), jax.ShapeDtypeStruct((PACK_ROWS, 128), F32)],
        scratch_shapes=[pltpu.VMEM((BLOCK, D_KV), F32), pltpu.VMEM((BLOCK, D_KV), F32),
                        pltpu.VMEM((BLOCK, D_POOL), F32)],
        input_output_aliases={13: 1},
        compiler_params=_compiler_params(("arbitrary",)),
    )(sinks, dz, u, u, pg, q, k, v, ag, a, pool_w, pool_scale, bias, pack)


def _bwd_in(layer, part, token, dproj, x, norm_pre, dxo=None, w_in_t=None):
    want_dw, want_dx = part in ("both", "dw"), part in ("both", "dx")
    tm = TOKEN_TILE
    steps = SEQ // tm
    cw = 256

    def body(*refs):
        refs = list(refs)
        dp_ref, x_ref, g_ref = refs[1:4]
        del refs[:4]
        if want_dx:
            dxo_ref, w_ref, dx_ref, dgo_ref = refs[:4]
            del refs[:4]
            dg = refs.pop()
        if want_dw:
            dw_ref, acc = refs
        i = pl.program_id(0)

        @pl.when(i == 0)
        def _():
            if want_dw:
                acc[...] = jnp.zeros_like(acc)
            if want_dx:
                dg[...] = jnp.zeros_like(dg)

        xv = x_ref[...]
        gv = g_ref[layer:layer + 1, :]
        r = lax.rsqrt(jnp.mean(xv * xv, axis=-1, keepdims=True) + EPS)
        xn = xv * r
        if want_dw:
            hb = (xn * gv).astype(BF16)
            for c in range(0, D_IN, cw):
                acc[c:c + cw, :] += _tn(dp_ref[:, c:c + cw], hb)
        if want_dx:
            dh = _nn(dp_ref[...], w_ref[...])
            dg[...] += jnp.sum(dh * xn, axis=0, keepdims=True)
            dhn = dh * gv
            dx_ref[...] = dxo_ref[...] + r * (dhn - xn * jnp.mean(dhn * xn, axis=-1, keepdims=True))

        @pl.when(i == steps - 1)
        def _():
            if want_dw:
                dw_ref[...] = acc[...].astype(BF16)
            if want_dx:
                _rows_of(dg, dgo_ref, 0)

    row = lambda c: pl.BlockSpec((tm, c), lambda i: (i, 0))
    const = lambda shape: pl.BlockSpec(shape, lambda i: (0,) * len(shape))
    in_specs = [const((8, 128)), row(D_IN), row(D_MODEL), const((DEPTH, D_MODEL))]
    operands = [token, dproj, x, norm_pre]
    out_specs, out_shape, scratch = [], [], []
    if want_dx:
        in_specs += [row(D_MODEL), _resident((D_IN, D_MODEL))]
        operands += [dxo, w_in_t]
        out_specs += [row(D_MODEL), const((8, 128))]
        out_shape += [jax.ShapeDtypeStruct((SEQ, D_MODEL), F32), jax.ShapeDtypeStruct((8, 128), F32)]
    if want_dw:
        out_specs.append(const((D_IN, D_MODEL)))
        out_shape.append(jax.ShapeDtypeStruct((D_IN, D_MODEL), BF16))
        scratch.append(pltpu.VMEM((D_IN, D_MODEL), F32))
    if want_dx:
        scratch.append(pltpu.VMEM((1, D_MODEL), F32))
    return pl.pallas_call(
        body, name=f"bwd_in_{part}{layer}", grid=(steps,),
        in_specs=in_specs, out_specs=out_specs, out_shape=out_shape, scratch_shapes=scratch,
        compiler_params=_compiler_params(("arbitrary",)),
    )(*operands)


def _mesh_pos():
    return lax.axis_index("x"), lax.axis_index("y"), lax.axis_index("c")


def _device_rows(ref, m, px, py, pc):
    return ref.at[pl.ds(pl.multiple_of((4 * px + 2 * py + pc) * m, 16 if m % 16 == 0 else 8), m), :]


def _allgather(srcs, out_dtype, name, later=()):
    na, nb = len(srcs), len(later)
    every = list(srcs) + list(later)
    shapes = [(a.shape[-2], a.shape[-1]) for a, _ in every]

    def body(*refs):
        xs, refs = refs[:na + nb], refs[na + nb:]
        outs, cast, land, refs = refs[:na], refs[na:na + nb], refs[na + nb:na + 2 * nb], refs[na + 2 * nb:]
        stage, (send_sems, recv_sems, local_sems) = refs[:na], refs[na:]
        x, y, c = _mesh_pos()
        me, sibling = (x, y, c), (x, y, 1 - c)
        near = [(1 - x, y), (x, 1 - y)]
        far = (1 - x, 1 - y)
        relay_from, relay_to = (x ^ (1 - c), y ^ c), (x ^ c, y ^ (1 - c))
        k_from, k_to = 1 + c, 2 - c

        def slot(a, px, py, pc):
            return _device_rows(outs[a], shapes[a][0], px, py, pc)

        def copy(a, k, block, to, src=None):
            return pltpu.make_async_remote_copy(
                src_ref=slot(a, *block) if src is None else src, dst_ref=slot(a, *block),
                send_sem=send_sems.at[a, k], recv_sem=recv_sems.at[a, k], device_id=to, device_id_type=MESH)

        def cast_block(i):
            layer = every[i][1]
            return (xs[i][...] if layer is None else xs[i][layer]).astype(out_dtype)

        for a in range(na):
            stage[a][...] = cast_block(a)
        mine = [pltpu.make_async_copy(stage[a], slot(a, *me), local_sems.at[a]) for a in range(na)]
        for cp in mine:
            cp.start()
        sent = []
        for a in range(na):
            sent.append(copy(a, 0, me, sibling, src=stage[a]))
            sent += [copy(a, 1 + j, me, (*chip, c), src=stage[a]) for j, chip in enumerate(near)]
        for cp in sent:
            cp.start()
        for b in range(nb):
            cast[b][...] = cast_block(na + b)
            cp = pltpu.make_async_copy(cast[b], _device_rows(land[b], shapes[na + b][0], *me), local_sems.at[na + b])
            cp.start()
            mine.append(cp)
        for a in range(na):
            copy(a, k_from, (*relay_from, c), me).wait_recv()
            sent += [copy(a, 3, (*relay_from, c), (*relay_to, c)), copy(a, 3 + k_from, (*relay_from, c), sibling)]
            sent[-2].start()
            sent[-1].start()
        for a in range(na):
            copy(a, k_to, (*relay_to, c), me).wait_recv()
            sent.append(copy(a, 3 + k_to, (*relay_to, c), sibling))
            sent[-1].start()
        for a in range(na):
            copy(a, 3, (*far, c), me).wait_recv()
            sent.append(copy(a, 6, (*far, c), sibling))
            sent[-1].start()
        for a in range(na):
            copy(a, 0, sibling, me).wait_recv()
            for j, chip in enumerate(near + [far]):
                copy(a, 4 + j, (*chip, 1 - c), me).wait_recv()
        for cp in sent:
            cp.wait_send()
        for cp in mine:
            cp.wait()

    vmem = pl.BlockSpec(memory_space=pltpu.VMEM)
    hbm = pl.BlockSpec(memory_space=pl.ANY)
    gathered = [jax.ShapeDtypeStruct((N_DEV * m, n), out_dtype) for m, n in shapes]
    out = pl.pallas_call(
        body, name=name,
        in_specs=[vmem] * (na + nb),
        out_specs=[hbm] * na + [vmem] * nb + [hbm] * nb,
        out_shape=gathered[:na] + [jax.ShapeDtypeStruct(s, out_dtype) for s in shapes[na:]] + gathered[na:],
        scratch_shapes=([pltpu.VMEM(s, out_dtype) for s in shapes[:na]]
                        + [pltpu.SemaphoreType.DMA((na, 7)), pltpu.SemaphoreType.DMA((na, 7)),
                           pltpu.SemaphoreType.DMA((na + nb,))]),
        compiler_params=_compiler_params(),
    )(*[a for a, _ in every])
    return out[:na], out[na:na + nb], out[na + nb:]


def _gather_start(blocks, lands, name):
    na = len(blocks)

    def body(*refs):
        src, land, sems, token = refs[:na], refs[na:2 * na], refs[2 * na:4 * na], refs[-1]
        x, y, c = _mesh_pos()
        for a in range(na):
            for k in range(1, N_DEV):
                peer = (x ^ ((k >> 2) & 1), y ^ ((k >> 1) & 1), c ^ (k & 1))
                pltpu.make_async_remote_copy(
                    src_ref=src[a], dst_ref=_device_rows(land[a], blocks[a].shape[0], x, y, c),
                    send_sem=sems[2 * a].at[k - 1], recv_sem=sems[2 * a + 1].at[k - 1],
                    device_id=peer, device_id_type=MESH).start()
        token[...] = jnp.zeros_like(token)

    bufs = [pltpu.HBM(t.shape, t.dtype) for t in list(blocks) + list(lands)]
    out = pl.pallas_call(
        body, name=name,
        out_shape=(*([pltpu.SemaphoreType.DMA((N_DEV - 1,))] * (2 * na)), *bufs, jax.ShapeDtypeStruct((8, 128), F32)),
        in_specs=[_HBM] * (2 * na),
        out_specs=(*([_SEM] * (2 * na)), *([_HBM] * (2 * na)), pl.BlockSpec(memory_space=pltpu.VMEM)),
        input_output_aliases={i: 2 * na + i for i in range(2 * na)},
        compiler_params=pltpu.CompilerParams(has_side_effects=_EFFECT),
    )(*[pltpu.with_memory_space_constraint(t, pltpu.HBM) for t in list(blocks) + list(lands)])
    sems = [(out[2 * a], out[2 * a + 1]) for a in range(na)]
    return sems, out[2 * na:3 * na], out[3 * na:4 * na], out[-1]


def _gather_wait(sems, block, land, after, name):
    def body(src, land_ref, send_sem, recv_sem, after_ref, src_out, land_out):
        x, y, c = _mesh_pos()
        for k in range(1, N_DEV):
            peer = (x ^ ((k >> 2) & 1), y ^ ((k >> 1) & 1), c ^ (k & 1))
            cp = pltpu.make_async_remote_copy(
                src_ref=src, dst_ref=_device_rows(land_ref, block.shape[0], *peer),
                send_sem=send_sem.at[k - 1], recv_sem=recv_sem.at[k - 1], device_id=peer, device_id_type=MESH)
            cp.wait_send()
            cp.wait_recv()

    out = pl.pallas_call(
        body, name=name,
        out_shape=(pltpu.HBM(block.shape, block.dtype), pltpu.HBM(land.shape, land.dtype)),
        in_specs=[_HBM, _HBM, _SEM, _SEM, pl.BlockSpec(memory_space=pl.ANY)],
        out_specs=[_HBM, _HBM],
        input_output_aliases={0: 0, 1: 1},
        compiler_params=pltpu.CompilerParams(has_side_effects=_EFFECT),
    )(block, land, sems[0], sems[1], after)
    return out[1]


def _row_step(m):
    return next(s for s in (32, 24, 16, 8) if m % s == 0)


def _pair_reduce(arrs, name):
    na = len(arrs)

    def body(*refs):
        gs, hs, hm = refs[:na], refs[na:2 * na], refs[2 * na:3 * na]
        own, ra = refs[3 * na:4 * na], refs[4 * na:5 * na]
        d2d_send, d2d_recv, local_sems = refs[5 * na:]
        x, y, c = _mesh_pos()
        sibling = (x, y, 1 - c)
        loads, sends = [], []
        for a in range(na):
            for q in range(4):
                cp = pltpu.make_async_copy(gs[a].at[2 * q + c], own[a].at[q], local_sems.at[a, q])
                cp.start()
                loads.append(cp)
                cp = pltpu.make_async_remote_copy(
                    src_ref=gs[a].at[2 * q + (1 - c)], dst_ref=ra[a].at[q], send_sem=d2d_send.at[a, q],
                    recv_sem=d2d_recv.at[a, q], device_id=sibling, device_id_type=MESH)
                cp.start()
                sends.append(cp)
        for cp in loads:
            cp.wait()
        for cp in sends:
            cp.wait_recv()
        others = [2 * (1 - x) + y, 2 * x + (1 - y), 2 * (1 - x) + (1 - y)]
        for a in range(na):
            m = arrs[a].shape[1]
            step = _row_step(m)

            def add(i, carry, a=a, step=step):
                rs = pl.ds(pl.multiple_of(i * step, step), step)
                for j, q in enumerate(others):
                    hs[a][j, rs, :] = (own[a][q, rs, :].astype(F32) + ra[a][q, rs, :].astype(F32)).astype(hs[a].dtype)
                q = 2 * x + y
                hm[a][rs, :] = own[a][q, rs, :].astype(F32) + ra[a][q, rs, :].astype(F32)
                return carry

            lax.fori_loop(0, m // step, add, 0)
        for cp in sends:
            cp.wait_send()

    vmem = pl.BlockSpec(memory_space=pltpu.VMEM)
    scratch = [pltpu.VMEM((4,) + t.shape[1:], t.dtype) for t in arrs] * 2
    scratch += [pltpu.SemaphoreType.DMA((na, 4)), pltpu.SemaphoreType.DMA((na, 4)), pltpu.SemaphoreType.DMA((na, 4))]
    out = pl.pallas_call(
        body, name=name,
        in_specs=[pl.BlockSpec(memory_space=pl.ANY)] * na, out_specs=[vmem] * (2 * na),
        out_shape=([jax.ShapeDtypeStruct((3,) + t.shape[1:], t.dtype) for t in arrs]
                   + [jax.ShapeDtypeStruct(t.shape[1:], F32) for t in arrs]),
        scratch_shapes=scratch,
        compiler_params=_compiler_params(),
    )(*arrs)
    return out[:na], out[na:]


_HBM = pl.BlockSpec(memory_space=pltpu.HBM)
_SEM = pl.BlockSpec(memory_space=pltpu.SEMAPHORE)
_EFFECT = pltpu.SideEffectType.DATAFLOW_SIDE_EFFECTING


def _exchange_plan(direct):
    x, y, c = _mesh_pos()
    if not direct:
        return [(j, j, (qx, qy, c)) for j, (qx, qy) in enumerate([(1 - x, y), (x, 1 - y), (1 - x, 1 - y)])]
    plan = []
    for k in range(1, N_DEV):
        px, py, pc = x ^ ((k >> 2) & 1), y ^ ((k >> 1) & 1), c ^ (k & 1)
        plan.append((4 * px + 2 * py + pc, k - 1, (px, py, pc)))
    return plan


def _exchange_start(srcs, direct, name):
    na = len(srcs)
    slots = N_DEV - 1 if direct else 3

    def body(*refs):
        src, land = refs[:na], refs[na:2 * na]
        send_sem, recv_sem = refs[2 * na], refs[2 * na + 1]
        token = refs[-1]
        for block, slot, peer in _exchange_plan(direct):
            for a in range(na):
                pltpu.make_async_remote_copy(
                    src_ref=src[a].at[block], dst_ref=land[a].at[slot], send_sem=send_sem.at[slots * a + slot],
                    recv_sem=recv_sem.at[slots * a + slot], device_id=peer, device_id_type=MESH).start()
        token[...] = jnp.zeros_like(token)

    zones = [jax.ShapeDtypeStruct((slots,) + t.shape[1:], t.dtype) for t in srcs]
    bufs = [pltpu.HBM(t.shape, t.dtype) for t in list(srcs) + zones]
    out = pl.pallas_call(
        body, name=name,
        out_shape=(pltpu.SemaphoreType.DMA((slots * na,)), pltpu.SemaphoreType.DMA((slots * na,)), *bufs,
                   jax.ShapeDtypeStruct((8, 128), F32)),
        in_specs=[_HBM] * (2 * na),
        out_specs=(_SEM, _SEM, *([_HBM] * (2 * na)), pl.BlockSpec(memory_space=pltpu.VMEM)),
        input_output_aliases={i: 2 + i for i in range(2 * na)},
        compiler_params=pltpu.CompilerParams(has_side_effects=_EFFECT),
    )(*[pltpu.with_memory_space_constraint(t, pltpu.HBM) for t in srcs],
      *[pltpu.with_memory_space_constraint(lax.empty(t.shape, t.dtype), pltpu.HBM) for t in zones])
    return out[0], out[1], out[2:2 + na], out[2 + na:2 + 2 * na], out[-1]


def _exchange_wait(send_sem, recv_sem, srcs, lands, direct, after, name):
    na = len(srcs)
    slots = N_DEV - 1 if direct else 3

    def body(*refs):
        src, land = refs[:na], refs[na:2 * na]
        send_sem_ref, recv_sem_ref = refs[2 * na], refs[2 * na + 1]
        for block, slot, peer in _exchange_plan(direct):
            for a in range(na):
                cp = pltpu.make_async_remote_copy(
                    src_ref=src[a].at[block], dst_ref=land[a].at[slot], send_sem=send_sem_ref.at[slots * a + slot],
                    recv_sem=recv_sem_ref.at[slots * a + slot], device_id=peer, device_id_type=MESH)
                cp.wait_send()
                cp.wait_recv()

    bufs = [pltpu.HBM(t.shape, t.dtype) for t in list(srcs) + list(lands)]
    out = pl.pallas_call(
        body, name=name,
        out_shape=tuple(bufs),
        in_specs=[_HBM] * (2 * na) + [_SEM, _SEM, pl.BlockSpec(memory_space=pl.ANY)],
        out_specs=[_HBM] * (2 * na),
        input_output_aliases={i: i for i in range(2 * na)},
        compiler_params=pltpu.CompilerParams(has_side_effects=_EFFECT),
    )(*srcs, *lands, send_sem, recv_sem, after)
    return out[:na], out[na:]


def _own_then_slots(mine_ref, lands_ref, rows=slice(None)):
    if len(mine_ref.shape) == 3:
        x, y, c = _mesh_pos()
        total = mine_ref[4 * x + 2 * y + c, rows, :].astype(F32)
    else:
        total = mine_ref[rows, :].astype(F32)
    for j in range(lands_ref.shape[0]):
        total = total + lands_ref[j, rows, :].astype(F32)
    return total


SMALL_ROWS = 2 * PACK_SLICE + 2 * 8


def _small_block(mine, lands, dgpre, name):
    def body(*refs):
        hm, ld, dg = refs[:DEPTH], refs[DEPTH:2 * DEPTH], refs[2 * DEPTH:3 * DEPTH]
        blk, land, sem = refs[3 * DEPTH:]
        for l in range(DEPTH):
            blk[PACK_SLICE * l:PACK_SLICE * (l + 1), :] = _own_then_slots(hm[l], ld[l])
            blk[2 * PACK_SLICE + 8 * l:2 * PACK_SLICE + 8 * (l + 1), :] = dg[l][...]
        cp = pltpu.make_async_copy(blk, _device_rows(land, SMALL_ROWS, *_mesh_pos()), sem)
        cp.start()
        cp.wait()

    vmem = pl.BlockSpec(memory_space=pltpu.VMEM)
    return pl.pallas_call(
        body, name=name,
        in_specs=[vmem] * (3 * DEPTH), out_specs=[vmem, pl.BlockSpec(memory_space=pl.ANY)],
        out_shape=[jax.ShapeDtypeStruct((SMALL_ROWS, 128), F32), jax.ShapeDtypeStruct((N_DEV * SMALL_ROWS, 128), F32)],
        scratch_shapes=[pltpu.SemaphoreType.DMA],
        compiler_params=_compiler_params(),
    )(*mine, *lands, *dgpre)


def _adamw_math(w, g, m, v):
    m = ADAM_B1 * m + (1.0 - ADAM_B1) * g
    v = ADAM_B2 * v + (1.0 - ADAM_B2) * (g * g)
    m_hat = m / (1.0 - ADAM_B1 ** ADAM_STEP)
    v_hat = v / (1.0 - ADAM_B2 ** ADAM_STEP)
    delta = -ADAM_LR * (m_hat / (jnp.sqrt(v_hat) + ADAM_EPS) + ADAM_WD * w)
    return delta, m, v


def _adamw_layer(layer, mine, lands, w, m, v, earlier, token, name, rows):
    _, mm, nn = w.shape

    def body(hm_ref, ld_ref, w_ref, m_ref, v_ref, _, *refs):
        g_ref, d_ref, nm_ref, nv_ref = refs[-4:]
        g = _own_then_slots(hm_ref, ld_ref)
        g_ref[...] = g
        d, nm, nv = _adamw_math(w_ref[...], g, m_ref[...], v_ref[...])
        d_ref[...] = d
        nm_ref[...] = nm
        nv_ref[...] = nv

    spec = pl.BlockSpec((None, rows, nn), lambda i: (layer, i, 0))
    carried = [] if earlier is None else list(earlier)
    return pl.pallas_call(
        body, name=name, grid=(mm // rows,),
        in_specs=([pl.BlockSpec((rows, nn), lambda i: (i, 0)) if mine.ndim == 2
                   else pl.BlockSpec((N_DEV, rows, nn), lambda i: (0, i, 0)),
                   pl.BlockSpec((lands.shape[0], rows, nn), lambda i: (0, i, 0)),
                   spec, spec, spec] + [pl.BlockSpec(memory_space=pl.ANY)] * (1 + len(carried))),
        out_specs=[spec] * 4,
        out_shape=[jax.ShapeDtypeStruct(w.shape, F32)] * 4,
        input_output_aliases={6 + t: t for t in range(len(carried))},
        compiler_params=_compiler_params(("arbitrary",)),
    )(mine, lands, w, m, v, token, *carried)


def _adamw_small(gathered, params):
    def body(all_ref, *refs):
        ins, outs, packs = refs[:15], refs[15:15 + 21], refs[15 + 21]
        loss_ref = outs[0]
        for dev in range(N_DEV):
            for l in range(DEPTH):
                packs[l, PACK_SLICE * dev:PACK_SLICE * (dev + 1), :] = (
                    all_ref[SMALL_ROWS * dev + PACK_SLICE * l:SMALL_ROWS * dev + PACK_SLICE * (l + 1), :])
        loss_ref[...] = packs[DEPTH - 1, ROW_LOSS:ROW_LOSS + 1, 0:1]

        def update(p, sel, g):
            w_ref, m_ref, v_ref = ins[p], ins[5 + p], ins[10 + p]
            d, nm, nv = _adamw_math(w_ref[sel], g, m_ref[sel], v_ref[sel])
            for t, val in enumerate((g, d, nm, nv)):
                outs[1 + 5 * t + p][sel] = val

        for l in range(DEPTH):
            gp = packs.at[l]
            row0 = 2 * PACK_SLICE + 8 * l
            dgpre = all_ref[row0:row0 + 8, :]
            for dev in range(1, N_DEV):
                dgpre = dgpre + all_ref[SMALL_ROWS * dev + row0:SMALL_ROWS * dev + row0 + 8, :]
            for grp in range(4):
                update(0, (l, grp), gp[ROW_PW + BLOCK * grp:ROW_PW + BLOCK * (grp + 1), :])
                update(1, (slice(l, l + 1), slice(128 * grp, 128 * (grp + 1))), gp[ROW_SC + grp:ROW_SC + grp + 1, :])
            update(2, (slice(l, l + 1), slice(None)), gp[ROW_SINK:ROW_SINK + 1, 0:N_HEADS])
            for r in range(D_MODEL // 128):
                sel = (slice(l, l + 1), slice(128 * r, 128 * (r + 1)))
                update(3, sel, dgpre[r:r + 1, :])
                update(4, sel, gp[ROW_NPOST + r:ROW_NPOST + r + 1, :])

    shapes = [jax.ShapeDtypeStruct(p.shape, F32) for p in params[:5]]
    return pl.pallas_call(
        body, name="adamw_small",
        out_shape=[jax.ShapeDtypeStruct((1, 1), F32)] + shapes * 4,
        scratch_shapes=[pltpu.VMEM((DEPTH, PACK_ROWS, 128), F32)],
        compiler_params=_compiler_params(),
    )(gathered, *params)


def kernel(x, w_in, pool_w, pool_scale, attn_sinks, w_out, norm_pre, norm_post, loss_target, m_w_in, m_pool_w, m_pool_scale, m_attn_sinks, m_w_out, m_norm_pre, m_norm_post, v_w_in, v_pool_w, v_pool_scale, v_attn_sinks, v_w_out, v_norm_pre, v_norm_post):
    x0 = x.reshape(SEQ, D_MODEL)
    target = loss_target.reshape(SEQ, D_MODEL)
    bias = jnp.asarray(_attn_bias())
    w_in_t, m_in_t, v_in_t = (jnp.swapaxes(t, 1, 2) for t in (w_in, m_w_in, v_w_in))

    (win0, wout0), later, lands = _allgather([(w_in_t, 0), (w_out, 0)], BF16, "gather_w0",
                                              later=[(w_in_t, 1), (w_out, 1)])
    sems, later, lands, token = _gather_start(later, lands, "gather_w1_start")
    win_full, wout_full = [win0, None], [wout0, None]

    saved = []
    xl = x0
    for layer in range(DEPTH):
        if layer > 0:
            win_full[layer] = _gather_wait(sems[0], later[0], lands[0], xl, "gather_w_in1_wait")
        u, pg, q, k, v, ag, z, a = _fwd_front(layer, xl, norm_pre, win_full[layer], token, attn_sinks,
                                             pool_w, pool_scale, bias)
        if layer > 0:
            wout_full[layer] = _gather_wait(sems[1], later[1], lands[1], z, "gather_w_out1_wait")
        x_next, y = _fwd_out(layer, z, xl, norm_post, wout_full[layer])
        saved.append((xl, u, pg, q, k, v, ag, z, a, y))
        xl = x_next

    params_small = [pool_w, pool_scale, attn_sinks, norm_pre, norm_post,
                    m_pool_w, m_pool_scale, m_attn_sinks, m_norm_pre, m_norm_post,
                    v_pool_w, v_pool_scale, v_attn_sinks, v_norm_pre, v_norm_post]
    exchange, dgpre = [None] * DEPTH, [None] * DEPTH
    dx = None
    for layer in reversed(range(DEPTH)):
        xin, u, pg, q, k, v, ag, z, a, y = saved[layer]
        if layer == DEPTH - 1:
            dx, dz, gw_out, pack = _bwd_out(layer, True, xl, target, y, z, norm_post, wout_full[layer])
        else:
            dz, gw_out, pack = _bwd_out(layer, False, dx, token, y, z, norm_post, wout_full[layer])
        dproj, pack = _bwd_mix(layer, attn_sinks, dz, u, pg, q, k, v, ag, a, pool_w, pool_scale, bias, pack)
        if layer > 0:
            dx, dgpre[layer], gw_in_t = _bwd_in(layer, "both", token, dproj, xin, norm_pre, dx, win_full[layer])
        else:
            (gw_in_t,) = _bwd_in(layer, "dw", token, dproj, xin, norm_pre)
        blocks = [gw_in_t.reshape(N_DEV, IN_SHARD, D_MODEL), gw_out.reshape(N_DEV, OUT_SHARD, D_MODEL),
                  pack.reshape(N_DEV, PACK_SLICE, 128)]
        direct = layer > 0
        srcs, mine = (blocks, None) if direct else _pair_reduce(blocks, f"pair_reduce{layer}")
        send_sem, recv_sem, srcs, lands, token = _exchange_start(srcs, direct, f"exchange_start{layer}")
        exchange[layer] = (send_sem, recv_sem, srcs, lands, mine)
        if layer == 0:
            dx, dgpre[layer] = _bwd_in(layer, "dx", token, dproj, xin, norm_pre, dx, win_full[layer])

    own, waited = [None] * DEPTH, [None] * DEPTH
    big_in, big_out, after = None, None, dx
    for layer in reversed(range(DEPTH)):
        send_sem, recv_sem, srcs, lands, mine = exchange[layer]
        srcs, waited[layer] = _exchange_wait(send_sem, recv_sem, srcs, lands, layer > 0, after, f"exchange_wait{layer}")
        own[layer] = srcs if mine is None else mine
        if layer > 0:
            big_in = _adamw_layer(layer, own[layer][0], waited[layer][0], w_in_t, m_in_t, v_in_t, big_in, token,
                                  f"adamw_in{layer}", 96)
            big_out = _adamw_layer(layer, own[layer][1], waited[layer][1], w_out, m_w_out, v_w_out, big_out, token,
                                   f"adamw_out{layer}", 128)
            after = big_out[0]
    block, land = _small_block([own[l][2] for l in range(DEPTH)], [waited[l][2] for l in range(DEPTH)],
                               dgpre, "small_block")
    sems, block, land, token = _gather_start([block], [land], "gather_small_start")
    big_in = _adamw_layer(0, own[0][0], waited[0][0], w_in_t, m_in_t, v_in_t, big_in, token, "adamw_in0", 96)
    big_out = _adamw_layer(0, own[0][1], waited[0][1], w_out, m_w_out, v_w_out, big_out, token, "adamw_out0", 128)
    gathered = _gather_wait(sems[0], block[0], land[0], big_out[0], "gather_small_wait")
    small_out = _adamw_small(gathered, params_small)
    loss = small_out[0].reshape(())

    outs = [loss, dx.reshape(1, SEQ, D_MODEL)]
    for t in range(4):
        pw_, sc_, sk_, npre_, npost_ = small_out[1 + 5 * t:6 + 5 * t]
        outs += [jnp.swapaxes(big_in[t], 1, 2), pw_, sc_, sk_, big_out[t], npre_, npost_]
    return tuple(outs)
```

```python
import numpy as np
import jax
import jax.numpy as jnp
from jax import lax
from jax.experimental import pallas as pl
from jax.experimental.pallas import tpu as pltpu

F32 = jnp.float32
BF16 = jnp.bfloat16

N_DEV = 8
SEQ = 2048
D_MODEL = 1024
D_POOL = 512
D_ATTN = 512
D_KV = 128
D_IN = 2304
N_HEADS = 8
GQA = 4
HEAD_DIM = 64
BLOCK = 128
N_BLOCKS = SEQ // BLOCK
POOL_WINDOWS = (2, 4, 8, 16)
DEPTH = 2
EPS = 1e-6
NEG_INF = -1e30
SCALE = HEAD_DIM ** -0.5
IN_SHARD = D_IN // N_DEV
OUT_SHARD = D_MODEL // N_DEV

COL_U, COL_PG, COL_Q, COL_K, COL_V, COL_AG = 0, 512, 1024, 1536, 1664, 1792

ADAM_LR = 0.001
ADAM_B1 = 0.9
ADAM_B2 = 0.999
ADAM_EPS = 1e-08
ADAM_WD = 0.01
ADAM_STEP = 10

TOKEN_TILE = 512
VMEM_LIMIT = 56 * 1024 * 1024
MESH = pl.DeviceIdType.MESH

ROW_PW, ROW_SC, ROW_SINK, ROW_NPRE, ROW_NPOST, ROW_LOSS = 0, 512, 520, 528, 536, 544
PACK_ROWS = 576
PACK_SLICE = PACK_ROWS // N_DEV


def _nn(a, b):
    return jnp.dot(a, b, preferred_element_type=F32)


def _nt(a, b):
    return lax.dot_general(a, b, (((1,), (1,)), ((), ())), preferred_element_type=F32)


def _tn(a, b):
    return lax.dot_general(a, b, (((0,), (0,)), ((), ())), preferred_element_type=F32)


def _silu_parts(g):
    s = jax.nn.sigmoid(g)
    return g * s, s * (1.0 + g * (1.0 - s))


def _resident(shape):
    return pl.BlockSpec(shape, lambda *_: (0,) * len(shape), pipeline_mode=pl.Buffered(1))


def _compiler_params(sem=None):
    if sem is None:
        return pltpu.CompilerParams(vmem_limit_bytes=VMEM_LIMIT)
    return pltpu.CompilerParams(dimension_semantics=sem, vmem_limit_bytes=VMEM_LIMIT)


def _attn_bias():
    t = np.arange(BLOCK)[None, :]
    j = np.arange(2 * BLOCK)[:, None]
    dist = t + BLOCK - j
    in_win = (dist >= 0) & (dist < BLOCK)
    out = np.zeros((2, 2, 2 * BLOCK, GQA * BLOCK), np.float32)
    for variant in range(2):
        valid = in_win & ((j >= BLOCK) | (variant == 1))
        for kv in range(2):
            for g in range(GQA):
                slope = np.float32(2.0 ** (-(kv * GQA + g + 1)))
                b = np.where(valid, -slope * dist.astype(np.float32), np.float32(NEG_INF))
                out[variant, kv, :, g * BLOCK:(g + 1) * BLOCK] = b
    return out


def _replicate_head(kx, kv):
    rolled = pltpu.roll(kx, 64, 1)
    lane = lax.broadcasted_iota(jnp.int32, kx.shape, 1)
    dup = jnp.where(lane < 64, kx, rolled) if kv == 0 else jnp.where(lane < 64, rolled, kx)
    return jnp.concatenate([dup, dup], axis=1).astype(BF16)


def _stack_heads(qv):
    lane = lax.broadcasted_iota(jnp.int32, qv.shape, 1)
    zero = jnp.zeros_like(qv)
    return jnp.concatenate([jnp.where((lane >= 64 * g) & (lane < 64 * g + 64), qv, zero) for g in range(GQA)], axis=0)


def _unstack_heads(xs):
    lane = lax.broadcasted_iota(jnp.int32, (BLOCK, 256), 1)
    return jnp.where(lane < 64, xs[0:128], jnp.where(lane < 128, xs[128:256], jnp.where(lane < 192, xs[256:384], xs[384:512])))


def _fold_heads(r):
    h = r[:, 0:128] + r[:, 128:256]
    return h + pltpu.roll(h, 64, 1)


def _sink_row(sink_ref, layer, kv):
    lane = lax.broadcasted_iota(jnp.int32, (1, GQA * BLOCK), 1)
    s4 = [sink_ref[layer, kv * GQA + g] for g in range(GQA)]
    return jnp.where(lane < 128, s4[0], jnp.where(lane < 256, s4[1], jnp.where(lane < 384, s4[2], s4[3])))


def _probs_keys_major(k_rep, q_st, bias, sink):
    st = _nt(k_rep, q_st) * SCALE + bias
    m = jnp.maximum(jnp.max(st, axis=0, keepdims=True), sink)
    p = jnp.exp(st - m)
    esink = jnp.exp(sink - m)
    rl = 1.0 / (jnp.sum(p, axis=0, keepdims=True) + esink)
    return p * rl, esink * rl


WINDOW_HALO = 16


def _window_sum(ext, w, forward):
    s = ext
    sh = 1
    while sh < w:
        s = s + pltpu.roll(s, (ext.shape[0] - sh) if forward else sh, 0)
        sh *= 2
    return s


def _inv_count(n, w):
    t = n * BLOCK + lax.broadcasted_iota(jnp.int32, (BLOCK, 1), 0) + 1
    return 1.0 / jnp.minimum(t.astype(F32), float(w))


def _kv_ext(ref, n):
    r0 = pl.multiple_of(jnp.maximum(n - 1, 0) * BLOCK, BLOCK)
    r1 = pl.multiple_of(n * BLOCK, BLOCK)
    return jnp.concatenate([ref[pl.ds(r0, BLOCK), :], ref[pl.ds(r1, BLOCK), :]], axis=0)


def _rows_of(vec_ref, pack_ref, row0):
    for r in range(D_MODEL // 128):
        pack_ref[row0 + r:row0 + r + 1, :] = vec_ref[:, 128 * r:128 * (r + 1)]


FRONT_TILE = 2 * BLOCK


def _fwd_front(layer, x, norm_pre, w_in_t, token, sinks, pool_w, pool_scale, bias):
    tm = FRONT_TILE

    def body(sink_ref, x_ref, g_ref, w_ref, _, pw_ref, sc_ref, bias_ref,
             u_ref, pg_ref, q_ref, k_ref, v_ref, ag_ref, z_ref, a_ref, uprev, kprev, vprev):
        i = pl.program_id(0)

        @pl.when(i == 0)
        def _():
            uprev[...] = jnp.zeros_like(uprev)
            kprev[...] = jnp.zeros_like(kprev)
            vprev[...] = jnp.zeros_like(vprev)

        xv = x_ref[...]
        r = lax.rsqrt(jnp.mean(xv * xv, axis=-1, keepdims=True) + EPS)
        h = (xv * r * g_ref[layer:layer + 1, :]).astype(BF16)
        u_ref[...] = _nt(h, w_ref[COL_U:COL_PG, :])
        pg_ref[...] = _nt(h, w_ref[COL_PG:COL_Q, :])
        for sb in range(tm // BLOCK):
            n = (tm // BLOCK) * i + sb
            rows = slice(BLOCK * sb, BLOCK * (sb + 1))
            before = slice(BLOCK * (sb - 1), BLOCK * sb)
            uv = u_ref[rows, :]
            halo = (uprev[BLOCK - WINDOW_HALO:, :] if sb == 0
                    else u_ref[BLOCK * sb - WINDOW_HALO:BLOCK * sb, :])
            ext = jnp.concatenate([halo, uv], axis=0)
            for g, w in enumerate(POOL_WINDOWS):
                cs = slice(BLOCK * g, BLOCK * (g + 1))
                win = _window_sum(ext[:, cs], w, forward=False)[WINDOW_HALO:]
                pooled = win * _inv_count(n, w) - uv[:, cs]
                mixed = _nn(pooled.astype(BF16), pw_ref[g].astype(BF16))
                gate, _ = _silu_parts(pg_ref[rows, cs])
                z_ref[rows, cs] = (mixed * sc_ref[layer:layer + 1, cs] * gate).astype(BF16)

        q_ref[...] = _nt(h, w_ref[COL_Q:COL_K, :]).astype(BF16)
        k_ref[...] = _nt(h, w_ref[COL_K:COL_V, :])
        v_ref[...] = _nt(h, w_ref[COL_V:COL_AG, :])
        ag_ref[...] = _nt(h, w_ref[COL_AG:D_IN, :])

        for sb in range(tm // BLOCK):
            n = (tm // BLOCK) * i + sb
            rows = slice(BLOCK * sb, BLOCK * (sb + 1))
            before = slice(BLOCK * (sb - 1), BLOCK * sb)
            kx = jnp.concatenate([kprev[...] if sb == 0 else k_ref[before, :], k_ref[rows, :]], axis=0)
            vx = jnp.concatenate([vprev[...] if sb == 0 else v_ref[before, :], v_ref[rows, :]], axis=0)
            variant = jnp.minimum(n, 1) if sb == 0 else 1
            for kv in range(2):
                cs = slice(256 * kv, 256 * (kv + 1))
                p, _ = _probs_keys_major(_replicate_head(kx, kv), _stack_heads(q_ref[rows, cs]),
                                         bias_ref[variant, kv], _sink_row(sink_ref, layer, kv))
                o = _unstack_heads(_tn(p.astype(BF16), _replicate_head(vx, kv)))
                a_ref[rows, cs] = o
                gate, _ = _silu_parts(ag_ref[rows, cs])
                z_ref[rows, D_POOL + 256 * kv:D_POOL + 256 * (kv + 1)] = (o * gate).astype(BF16)

        tail = slice(tm - BLOCK, tm)
        uprev[...] = u_ref[tail, :]
        kprev[...] = k_ref[tail, :]
        vprev[...] = v_ref[tail, :]

    row = lambda c: pl.BlockSpec((tm, c), lambda i: (i, 0))
    const = lambda shape: pl.BlockSpec(shape, lambda i: (0,) * len(shape))
    return pl.pallas_call(
        body, name=f"fwd_front{layer}", grid=(SEQ // tm,),
        in_specs=[pl.BlockSpec(memory_space=pltpu.SMEM), row(D_MODEL), const((DEPTH, D_MODEL)),
                  _resident((D_IN, D_MODEL)), const((8, 128)),
                  pl.BlockSpec((None, 4, BLOCK, BLOCK), lambda i: (layer, 0, 0, 0)), const((DEPTH, D_POOL)),
                  _resident((2, 2, 2 * BLOCK, GQA * BLOCK))],
        out_specs=[row(D_POOL), row(D_POOL), row(D_ATTN), row(D_KV), row(D_KV), row(D_ATTN), row(D_MODEL),
                   row(D_ATTN)],
        out_shape=[jax.ShapeDtypeStruct((SEQ, D_POOL), F32), jax.ShapeDtypeStruct((SEQ, D_POOL), F32),
                   jax.ShapeDtypeStruct((SEQ, D_ATTN), BF16), jax.ShapeDtypeStruct((SEQ, D_KV), F32),
                   jax.ShapeDtypeStruct((SEQ, D_KV), F32), jax.ShapeDtypeStruct((SEQ, D_ATTN), F32),
                   jax.ShapeDtypeStruct((SEQ, D_MODEL), BF16), jax.ShapeDtypeStruct((SEQ, D_ATTN), F32)],
        scratch_shapes=[pltpu.VMEM((BLOCK, D_POOL), F32), pltpu.VMEM((BLOCK, D_KV), F32),
                        pltpu.VMEM((BLOCK, D_KV), F32)],
        compiler_params=_compiler_params(("arbitrary",)),
    )(sinks, x, norm_pre, w_in_t, token, pool_w, pool_scale, bias)


def _fwd_out(layer, z, x, norm_post, w_out, token):
    tm = TOKEN_TILE

    def body(z_ref, x_ref, g_ref, w_ref, _, xn_ref, y_ref):
        y = _nn(z_ref[...], w_ref[...])
        y_ref[...] = y
        r = lax.rsqrt(jnp.mean(y * y, axis=-1, keepdims=True) + EPS)
        xn_ref[...] = x_ref[...] + y * r * g_ref[layer:layer + 1, :]

    row = lambda c: pl.BlockSpec((tm, c), lambda i: (i, 0))
    return pl.pallas_call(
        body, name=f"fwd_out{layer}", grid=(SEQ // tm,),
        in_specs=[row(D_MODEL), row(D_MODEL), pl.BlockSpec((DEPTH, D_MODEL), lambda i: (0, 0)),
                  _resident((D_MODEL, D_MODEL)), pl.BlockSpec((8, 128), lambda i: (0, 0))],
        out_specs=[row(D_MODEL), row(D_MODEL)],
        out_shape=[jax.ShapeDtypeStruct((SEQ, D_MODEL), F32), jax.ShapeDtypeStruct((SEQ, D_MODEL), F32)],
        compiler_params=_compiler_params(("arbitrary",)),
    )(z, x, norm_post, w_out, token)


def _bwd_out(layer, top, dxo_or_xf, target_or_token, y, z, norm_post, w_out):
    tm = TOKEN_TILE
    steps = SEQ // tm

    def body(*refs):
        if top:
            xf_ref, t_ref, y_ref, z_ref, g_ref, w_ref, dxo_ref, dz_ref, dw_ref, pack_ref, acc, dg, lacc = refs
        else:
            dxi_ref, _, y_ref, z_ref, g_ref, w_ref, dz_ref, dw_ref, pack_ref, acc, dg, lacc = refs
        i = pl.program_id(0)

        @pl.when(i == 0)
        def _():
            acc[...] = jnp.zeros_like(acc)
            dg[...] = jnp.zeros_like(dg)
            lacc[...] = jnp.zeros_like(lacc)
            pack_ref[...] = jnp.zeros_like(pack_ref)

        if top:
            d = xf_ref[...] - t_ref[...]
            dxo_v = d * (1.0 / D_MODEL)
            dxo_ref[...] = dxo_v
            part = jnp.sum(d * d, axis=-1, keepdims=True) * (1.0 / D_MODEL)
            lacc[...] += 0.5 * jnp.sum(part, axis=0, keepdims=True)
        else:
            dxo_v = dxi_ref[...]
        y = y_ref[...]
        r = lax.rsqrt(jnp.mean(y * y, axis=-1, keepdims=True) + EPS)
        yn = y * r
        dg[...] += jnp.sum(dxo_v * yn, axis=0, keepdims=True)
        dyn = dxo_v * g_ref[layer:layer + 1, :]
        dy = (r * (dyn - yn * jnp.mean(dyn * yn, axis=-1, keepdims=True))).astype(BF16)
        dz_ref[...] = _nt(dy, w_ref[...])
        acc[...] += _tn(z_ref[...], dy)

        @pl.when(i == steps - 1)
        def _():
            dw_ref[...] = acc[...].astype(BF16)
            _rows_of(dg, pack_ref, ROW_NPOST)
            lane = lax.broadcasted_iota(jnp.int32, (1, 128), 1)
            pack_ref[ROW_LOSS:ROW_LOSS + 1, :] = jnp.where(lane == 0, lacc[...], 0.0)

    row = lambda c: pl.BlockSpec((tm, c), lambda i: (i, 0))
    const = lambda shape: pl.BlockSpec(shape, lambda i: (0,) * len(shape))
    act = jax.ShapeDtypeStruct((SEQ, D_MODEL), F32)
    return pl.pallas_call(
        body, name=f"bwd_out{layer}", grid=(steps,),
        in_specs=([row(D_MODEL), row(D_MODEL) if top else const((8, 128))]
                  + [row(D_MODEL), row(D_MODEL), const((DEPTH, D_MODEL)), _resident((D_MODEL, D_MODEL))]),
        out_specs=([row(D_MODEL)] * (2 if top else 1) + [const((D_MODEL, D_MODEL)), const((PACK_ROWS, 128))]),
        out_shape=([act] * (2 if top else 1)
                   + [jax.ShapeDtypeStruct((D_MODEL, D_MODEL), BF16), jax.ShapeDtypeStruct((PACK_ROWS, 128), F32)]),
        scratch_shapes=[pltpu.VMEM((D_MODEL, D_MODEL), F32), pltpu.VMEM((1, D_MODEL), F32), pltpu.VMEM((1, 1), F32)],
        compiler_params=_compiler_params(("arbitrary",)),
    )(dxo_or_xf, target_or_token, y, z, norm_post, w_out)


def _bwd_mix(layer, sinks, dz, u, pg, q, k, v, ag, a, pool_w, pool_scale, bias, pack):
    last = N_BLOCKS - 1

    def body(sink_ref, dz_ref, u_ref, up_ref, pg_ref, q_ref, k_ref, v_ref, ag_ref, a_ref, pw_ref, sc_ref,
             bias_ref, pin_ref, dp_ref, pack_ref, ck, cv, ce):
        i = pl.program_id(0)
        n = last - i

        @pl.when(i == 0)
        def _():
            ck[...] = jnp.zeros_like(ck)
            cv[...] = jnp.zeros_like(cv)
            ce[...] = jnp.zeros_like(ce)
            pack_ref[...] = pin_ref[...]

        uv = u_ref[...]
        has_prev = (n > 0).astype(F32)
        ext = jnp.concatenate([up_ref[BLOCK - WINDOW_HALO:, :] * has_prev, uv], axis=0)
        for g, w in enumerate(POOL_WINDOWS):
            cs = slice(BLOCK * g, BLOCK * (g + 1))
            inv = _inv_count(n, w)
            win = _window_sum(ext[:, cs], w, forward=False)[WINDOW_HALO:]
            pooled = win * inv - uv[:, cs]
            pw_g = pw_ref[g].astype(BF16)
            mixed = _nn(pooled.astype(BF16), pw_g)
            gate, dgate = _silu_parts(pg_ref[:, cs])
            dzp = dz_ref[:, cs]
            sc = sc_ref[layer:layer + 1, cs]
            dpm = dzp * gate
            dp_ref[:, COL_PG + BLOCK * g:COL_PG + BLOCK * (g + 1)] = (dzp * (mixed * sc) * dgate).astype(BF16)
            pack_ref[ROW_SC + g:ROW_SC + g + 1, :] += jnp.sum(dpm * mixed, axis=0, keepdims=True)
            dmixed = (dpm * sc).astype(BF16)
            pack_ref[ROW_PW + BLOCK * g:ROW_PW + BLOCK * (g + 1), :] += _tn(pooled.astype(BF16), dmixed)
            dpooled = _nt(dmixed, pw_g)
            e = dpooled * inv
            lead = _window_sum(jnp.concatenate([e, ce[:WINDOW_HALO, cs]], axis=0), w, forward=True)[:BLOCK]
            dp_ref[:, COL_U + BLOCK * g:COL_U + BLOCK * (g + 1)] = (lead - dpooled).astype(BF16)
            ce[:, cs] = e

        kx = _kv_ext(k_ref, n)
        vx = _kv_ext(v_ref, n)
        lane = lax.broadcasted_iota(jnp.int32, (1, 128), 1)
        dsink_row = jnp.zeros((1, 128), F32)
        tks, tvs = [], []
        for kv in range(2):
            cs = slice(256 * kv, 256 * (kv + 1))
            k_rep = _replicate_head(kx, kv)
            v_rep = _replicate_head(vx, kv)
            q_st = _stack_heads(q_ref[:, cs])
            gate, dgate = _silu_parts(ag_ref[:, cs])
            dza = dz_ref[:, D_POOL + 256 * kv:D_POOL + 256 * (kv + 1)]
            dp_ref[:, COL_AG + 256 * kv:COL_AG + 256 * (kv + 1)] = (dza * a_ref[:, cs] * dgate).astype(BF16)
            da_st = _stack_heads((dza * gate).astype(BF16))
            p, psink = _probs_keys_major(k_rep, q_st, bias_ref[0, kv], _sink_row(sink_ref, layer, kv))
            dpt = _nt(v_rep, da_st)
            delta = jnp.sum(p * dpt, axis=0, keepdims=True)
            dst = (p * (dpt - delta) * SCALE).astype(BF16)
            sink_terms = psink * delta
            for g in range(GQA):
                dsink = -jnp.sum(sink_terms[:, BLOCK * g:BLOCK * (g + 1)], axis=1, keepdims=True)
                dsink_row = dsink_row + jnp.where(lane == kv * GQA + g, dsink, 0.0)
            dp_ref[:, COL_Q + 256 * kv:COL_Q + 256 * (kv + 1)] = _unstack_heads(_tn(dst, k_rep)).astype(BF16)
            tks.append(_fold_heads(_nn(dst, q_st)))
            tvs.append(_fold_heads(_nn(p.astype(BF16), da_st)))
        pack_ref[ROW_SINK:ROW_SINK + 1, :] += dsink_row
        lane2 = lax.broadcasted_iota(jnp.int32, (256, 128), 1)
        dkx = jnp.where(lane2 < 64, tks[0], tks[1])
        dvx = jnp.where(lane2 < 64, tvs[0], tvs[1])
        dp_ref[:, COL_K:COL_V] = (ck[...] + dkx[BLOCK:]).astype(BF16)
        dp_ref[:, COL_V:COL_AG] = (cv[...] + dvx[BLOCK:]).astype(BF16)
        ck[...] = dkx[:BLOCK]
        cv[...] = dvx[:BLOCK]

    blk = lambda c: pl.BlockSpec((BLOCK, c), lambda i: (last - i, 0))
    full = lambda shape: pl.BlockSpec(shape, lambda i: (0,) * len(shape))
    return pl.pallas_call(
        body, name=f"bwd_mix{layer}", grid=(N_BLOCKS,),
        in_specs=[pl.BlockSpec(memory_space=pltpu.SMEM), blk(D_MODEL), blk(D_POOL),
                  pl.BlockSpec((BLOCK, D_POOL), lambda i: (jnp.maximum(last - i - 1, 0), 0)),
                  blk(D_POOL), blk(D_ATTN), full((SEQ, D_KV)), full((SEQ, D_KV)), blk(D_ATTN), blk(D_ATTN),
                  pl.BlockSpec((None, 4, BLOCK, BLOCK), lambda i: (layer, 0, 0, 0)), full((DEPTH, D_POOL)),
                  pl.BlockSpec((1, 2, 2 * BLOCK, GQA * BLOCK), lambda i: (jnp.minimum(last - i, 1), 0, 0, 0)),
                  full((PACK_ROWS, 128))],
        out_specs=[blk(D_IN), full((PACK_ROWS, 128))],
        out_shape=[jax.ShapeDtypeStruct((SEQ, D_IN), BF16), jax.ShapeDtypeStruct((PACK_ROWS, 128), F32)],
        scratch_shapes=[pltpu.VMEM((BLOCK, D_KV), F32), pltpu.VMEM((BLOCK, D_KV), F32),
                        pltpu.VMEM((BLOCK, D_POOL), F32)],
        input_output_aliases={13: 1},
        compiler_params=_compiler_params(("arbitrary",)),
    )(sinks, dz, u, u, pg, q, k, v, ag, a, pool_w, pool_scale, bias, pack)


def _bwd_in(layer, part, token, dproj, x, norm_pre, dxo=None, w_in_t=None):
    want_dw, want_dx = part in ("both", "dw"), part in ("both", "dx")
    tm = TOKEN_TILE
    steps = SEQ // tm
    cw = 256

    def body(*refs):
        refs = list(refs)
        dp_ref, x_ref, g_ref = refs[1:4]
        del refs[:4]
        if want_dx:
            dxo_ref, w_ref, dx_ref, dgo_ref = refs[:4]
            del refs[:4]
            dg = refs.pop()
        if want_dw:
            dw_ref, acc = refs
        i = pl.program_id(0)

        @pl.when(i == 0)
        def _():
            if want_dw:
                acc[...] = jnp.zeros_like(acc)
            if want_dx:
                dg[...] = jnp.zeros_like(dg)

        xv = x_ref[...]
        gv = g_ref[layer:layer + 1, :]
        r = lax.rsqrt(jnp.mean(xv * xv, axis=-1, keepdims=True) + EPS)
        xn = xv * r
        if want_dw:
            hb = (xn * gv).astype(BF16)
            for c in range(0, D_IN, cw):
                acc[c:c + cw, :] += _tn(dp_ref[:, c:c + cw], hb)
        if want_dx:
            dh = _nn(dp_ref[...], w_ref[...])
            dg[...] += jnp.sum(dh * xn, axis=0, keepdims=True)
            dhn = dh * gv
            dx_ref[...] = dxo_ref[...] + r * (dhn - xn * jnp.mean(dhn * xn, axis=-1, keepdims=True))

        @pl.when(i == steps - 1)
        def _():
            if want_dw:
                dw_ref[...] = acc[...].astype(BF16)
            if want_dx:
                _rows_of(dg, dgo_ref, 0)

    row = lambda c: pl.BlockSpec((tm, c), lambda i: (i, 0))
    const = lambda shape: pl.BlockSpec(shape, lambda i: (0,) * len(shape))
    in_specs = [const((8, 128)), row(D_IN), row(D_MODEL), const((DEPTH, D_MODEL))]
    operands = [token, dproj, x, norm_pre]
    out_specs, out_shape, scratch = [], [], []
    if want_dx:
        in_specs += [row(D_MODEL), _resident((D_IN, D_MODEL))]
        operands += [dxo, w_in_t]
        out_specs += [row(D_MODEL), const((8, 128))]
        out_shape += [jax.ShapeDtypeStruct((SEQ, D_MODEL), F32), jax.ShapeDtypeStruct((8, 128), F32)]
    if want_dw:
        out_specs.append(const((D_IN, D_MODEL)))
        out_shape.append(jax.ShapeDtypeStruct((D_IN, D_MODEL), BF16))
        scratch.append(pltpu.VMEM((D_IN, D_MODEL), F32))
    if want_dx:
        scratch.append(pltpu.VMEM((1, D_MODEL), F32))
    return pl.pallas_call(
        body, name=f"bwd_in_{part}{layer}", grid=(steps,),
        in_specs=in_specs, out_specs=out_specs, out_shape=out_shape, scratch_shapes=scratch,
        compiler_params=_compiler_params(("arbitrary",)),
    )(*operands)


def _mesh_pos():
    return lax.axis_index("x"), lax.axis_index("y"), lax.axis_index("c")


def _device_rows(ref, m, px, py, pc):
    return ref.at[pl.ds(pl.multiple_of((4 * px + 2 * py + pc) * m, 16 if m % 16 == 0 else 8), m), :]


def _allgather(srcs, out_dtype, name, later=()):
    na, nb = len(srcs), len(later)
    every = list(srcs) + list(later)
    shapes = [(a.shape[-2], a.shape[-1]) for a, _ in every]

    def body(*refs):
        xs, refs = refs[:na + nb], refs[na + nb:]
        outs, cast, land, refs = refs[:na], refs[na:na + nb], refs[na + nb:na + 2 * nb], refs[na + 2 * nb:]
        stage, (send_sems, recv_sems, local_sems) = refs[:na], refs[na:]
        x, y, c = _mesh_pos()
        me, sibling = (x, y, c), (x, y, 1 - c)
        near = [(1 - x, y), (x, 1 - y)]
        far = (1 - x, 1 - y)
        relay_from, relay_to = (x ^ (1 - c), y ^ c), (x ^ c, y ^ (1 - c))
        k_from, k_to = 1 + c, 2 - c

        def slot(a, px, py, pc):
            return _device_rows(outs[a], shapes[a][0], px, py, pc)

        def copy(a, k, block, to, src=None):
            return pltpu.make_async_remote_copy(
                src_ref=slot(a, *block) if src is None else src, dst_ref=slot(a, *block),
                send_sem=send_sems.at[a, k], recv_sem=recv_sems.at[a, k], device_id=to, device_id_type=MESH)

        def cast_block(i):
            layer = every[i][1]
            return (xs[i][...] if layer is None else xs[i][layer]).astype(out_dtype)

        for a in range(na):
            stage[a][...] = cast_block(a)
        mine = [pltpu.make_async_copy(stage[a], slot(a, *me), local_sems.at[a]) for a in range(na)]
        for cp in mine:
            cp.start()
        sent = []
        for a in range(na):
            sent.append(copy(a, 0, me, sibling, src=stage[a]))
            sent += [copy(a, 1 + j, me, (*chip, c), src=stage[a]) for j, chip in enumerate(near)]
        for cp in sent:
            cp.start()
        for b in range(nb):
            cast[b][...] = cast_block(na + b)
            cp = pltpu.make_async_copy(cast[b], _device_rows(land[b], shapes[na + b][0], *me), local_sems.at[na + b])
            cp.start()
            mine.append(cp)
        for a in range(na):
            copy(a, k_from, (*relay_from, c), me).wait_recv()
            sent += [copy(a, 3, (*relay_from, c), (*relay_to, c)), copy(a, 3 + k_from, (*relay_from, c), sibling)]
            sent[-2].start()
            sent[-1].start()
        for a in range(na):
            copy(a, k_to, (*relay_to, c), me).wait_recv()
            sent.append(copy(a, 3 + k_to, (*relay_to, c), sibling))
            sent[-1].start()
        for a in range(na):
            copy(a, 3, (*far, c), me).wait_recv()
            sent.append(copy(a, 6, (*far, c), sibling))
            sent[-1].start()
        for a in range(na):
            copy(a, 0, sibling, me).wait_recv()
            for j, chip in enumerate(near + [far]):
                copy(a, 4 + j, (*chip, 1 - c), me).wait_recv()
        for cp in sent:
            cp.wait_send()
        for cp in mine:
            cp.wait()

    vmem = pl.BlockSpec(memory_space=pltpu.VMEM)
    hbm = pl.BlockSpec(memory_space=pl.ANY)
    gathered = [jax.ShapeDtypeStruct((N_DEV * m, n), out_dtype) for m, n in shapes]
    out = pl.pallas_call(
        body, name=name,
        in_specs=[vmem] * (na + nb),
        out_specs=[hbm] * na + [vmem] * nb + [hbm] * nb,
        out_shape=gathered[:na] + [jax.ShapeDtypeStruct(s, out_dtype) for s in shapes[na:]] + gathered[na:],
        scratch_shapes=([pltpu.VMEM(s, out_dtype) for s in shapes[:na]]
                        + [pltpu.SemaphoreType.DMA((na, 7)), pltpu.SemaphoreType.DMA((na, 7)),
                           pltpu.SemaphoreType.DMA((na + nb,))]),
        compiler_params=_compiler_params(),
    )(*[a for a, _ in every])
    return out[:na], out[na:na + nb], out[na + nb:]


ALL_PEERS = tuple(range(1, N_DEV))
SIBLING_AND_SAME_CORE = (1, 2, 4, 6)


def _related(k, x, y, c):
    return x ^ ((k >> 2) & 1), y ^ ((k >> 1) & 1), c ^ (k & 1)


def _gather_start(blocks, lands, relations, name):
    na = len(blocks)

    def body(*refs):
        src, land, sems, token = refs[:na], refs[na:2 * na], refs[2 * na:4 * na], refs[-1]
        x, y, c = _mesh_pos()
        for a in range(na):
            for k in relations[a]:
                pltpu.make_async_remote_copy(
                    src_ref=src[a], dst_ref=_device_rows(land[a], blocks[a].shape[0], x, y, c),
                    send_sem=sems[2 * a].at[k - 1], recv_sem=sems[2 * a + 1].at[k - 1],
                    device_id=_related(k, x, y, c), device_id_type=MESH).start()
        token[...] = jnp.zeros_like(token)

    bufs = [pltpu.HBM(t.shape, t.dtype) for t in list(blocks) + list(lands)]
    out = pl.pallas_call(
        body, name=name,
        out_shape=(*([pltpu.SemaphoreType.DMA((N_DEV - 1,))] * (2 * na)), *bufs, jax.ShapeDtypeStruct((8, 128), F32)),
        in_specs=[_HBM] * (2 * na),
        out_specs=(*([_SEM] * (2 * na)), *([_HBM] * (2 * na)), pl.BlockSpec(memory_space=pltpu.VMEM)),
        input_output_aliases={i: 2 * na + i for i in range(2 * na)},
        compiler_params=pltpu.CompilerParams(has_side_effects=_EFFECT),
    )(*[pltpu.with_memory_space_constraint(t, pltpu.HBM) for t in list(blocks) + list(lands)])
    sems = [(out[2 * a], out[2 * a + 1]) for a in range(na)]
    return sems, out[2 * na:3 * na], out[3 * na:4 * na], out[-1]


def _gather_wait(sems, block, land, relations, after, name):
    def body(src, land_ref, send_sem, recv_sem, after_ref, src_out, land_out):
        x, y, c = _mesh_pos()
        for k in relations:
            peer = _related(k, x, y, c)
            cp = pltpu.make_async_remote_copy(
                src_ref=src, dst_ref=_device_rows(land_ref, block.shape[0], *peer),
                send_sem=send_sem.at[k - 1], recv_sem=recv_sem.at[k - 1], device_id=peer, device_id_type=MESH)
            cp.wait_send()
            cp.wait_recv()

    out = pl.pallas_call(
        body, name=name,
        out_shape=(pltpu.HBM(block.shape, block.dtype), pltpu.HBM(land.shape, land.dtype)),
        in_specs=[_HBM, _HBM, _SEM, _SEM, pl.BlockSpec(memory_space=pl.ANY)],
        out_specs=[_HBM, _HBM],
        input_output_aliases={0: 0, 1: 1},
        compiler_params=pltpu.CompilerParams(has_side_effects=_EFFECT),
    )(block, land, sems[0], sems[1], after)
    return out[1]


def _forward_plan(land_ref, m):
    x, y, c = _mesh_pos()
    return [_device_rows(land_ref, m, qx, qy, c) for qx, qy in ((1 - x, y), (x, 1 - y), (1 - x, 1 - y))], (x, y, 1 - c)


def _forward_start(land, m, name):
    def body(land_ref, send_sem, recv_sem, land_out, token):
        rows, sibling = _forward_plan(land_ref, m)
        for j, r in enumerate(rows):
            pltpu.make_async_remote_copy(src_ref=r, dst_ref=r, send_sem=send_sem.at[j], recv_sem=recv_sem.at[j],
                                         device_id=sibling, device_id_type=MESH).start()
        token[...] = jnp.zeros_like(token)

    out = pl.pallas_call(
        body, name=name,
        out_shape=(pltpu.SemaphoreType.DMA((3,)), pltpu.SemaphoreType.DMA((3,)), pltpu.HBM(land.shape, land.dtype),
                   jax.ShapeDtypeStruct((8, 128), F32)),
        in_specs=[_HBM],
        out_specs=(_SEM, _SEM, _HBM, pl.BlockSpec(memory_space=pltpu.VMEM)),
        input_output_aliases={0: 2},
        compiler_params=pltpu.CompilerParams(has_side_effects=_EFFECT),
    )(pltpu.with_memory_space_constraint(land, pltpu.HBM))
    return (out[0], out[1]), out[2], out[3]


def _forward_wait(sems, land, m, after, name):
    def body(land_ref, send_sem, recv_sem, after_ref, land_out):
        x, y, c = _mesh_pos()
        mine, sibling = _forward_plan(land_ref, m)
        theirs = [_device_rows(land_ref, m, qx, qy, 1 - c) for qx, qy in ((1 - x, y), (x, 1 - y), (1 - x, 1 - y))]
        for j in range(3):
            cp = pltpu.make_async_remote_copy(src_ref=mine[j], dst_ref=theirs[j], send_sem=send_sem.at[j],
                                              recv_sem=recv_sem.at[j], device_id=sibling, device_id_type=MESH)
            cp.wait_send()
            cp.wait_recv()

    return pl.pallas_call(
        body, name=name,
        out_shape=pltpu.HBM(land.shape, land.dtype),
        in_specs=[_HBM, _SEM, _SEM, pl.BlockSpec(memory_space=pl.ANY)],
        out_specs=_HBM,
        input_output_aliases={0: 0},
        compiler_params=pltpu.CompilerParams(has_side_effects=_EFFECT),
    )(land, sems[0], sems[1], after)


def _row_step(m):
    return next(s for s in (32, 24, 16, 8) if m % s == 0)


def _pair_reduce(arrs, name):
    na = len(arrs)

    def body(*refs):
        gs, hs, hm = refs[:na], refs[na:2 * na], refs[2 * na:3 * na]
        own, ra = refs[3 * na:4 * na], refs[4 * na:5 * na]
        d2d_send, d2d_recv, local_sems = refs[5 * na:]
        x, y, c = _mesh_pos()
        sibling = (x, y, 1 - c)
        loads, sends = [], []
        for a in range(na):
            for q in range(4):
                cp = pltpu.make_async_copy(gs[a].at[2 * q + c], own[a].at[q], local_sems.at[a, q])
                cp.start()
                loads.append(cp)
                cp = pltpu.make_async_remote_copy(
                    src_ref=gs[a].at[2 * q + (1 - c)], dst_ref=ra[a].at[q], send_sem=d2d_send.at[a, q],
                    recv_sem=d2d_recv.at[a, q], device_id=sibling, device_id_type=MESH)
                cp.start()
                sends.append(cp)
        for cp in loads:
            cp.wait()
        for cp in sends:
            cp.wait_recv()
        others = [2 * (1 - x) + y, 2 * x + (1 - y), 2 * (1 - x) + (1 - y)]
        for a in range(na):
            m = arrs[a].shape[1]
            step = _row_step(m)

            def add(i, carry, a=a, step=step):
                rs = pl.ds(pl.multiple_of(i * step, step), step)
                for j, q in enumerate(others):
                    hs[a][j, rs, :] = (own[a][q, rs, :].astype(F32) + ra[a][q, rs, :].astype(F32)).astype(hs[a].dtype)
                q = 2 * x + y
                hm[a][rs, :] = own[a][q, rs, :].astype(F32) + ra[a][q, rs, :].astype(F32)
                return carry

            lax.fori_loop(0, m // step, add, 0)
        for cp in sends:
            cp.wait_send()

    vmem = pl.BlockSpec(memory_space=pltpu.VMEM)
    scratch = [pltpu.VMEM((4,) + t.shape[1:], t.dtype) for t in arrs] * 2
    scratch += [pltpu.SemaphoreType.DMA((na, 4)), pltpu.SemaphoreType.DMA((na, 4)), pltpu.SemaphoreType.DMA((na, 4))]
    out = pl.pallas_call(
        body, name=name,
        in_specs=[pl.BlockSpec(memory_space=pl.ANY)] * na, out_specs=[vmem] * (2 * na),
        out_shape=([jax.ShapeDtypeStruct((3,) + t.shape[1:], t.dtype) for t in arrs]
                   + [jax.ShapeDtypeStruct(t.shape[1:], F32) for t in arrs]),
        scratch_shapes=scratch,
        compiler_params=_compiler_params(),
    )(*arrs)
    return out[:na], out[na:]


_HBM = pl.BlockSpec(memory_space=pltpu.HBM)
_SEM = pl.BlockSpec(memory_space=pltpu.SEMAPHORE)
_EFFECT = pltpu.SideEffectType.DATAFLOW_SIDE_EFFECTING


def _exchange_plan(direct):
    x, y, c = _mesh_pos()
    if not direct:
        return [(j, j, (qx, qy, c)) for j, (qx, qy) in enumerate([(1 - x, y), (x, 1 - y), (1 - x, 1 - y)])]
    plan = []
    for k in range(1, N_DEV):
        px, py, pc = x ^ ((k >> 2) & 1), y ^ ((k >> 1) & 1), c ^ (k & 1)
        plan.append((4 * px + 2 * py + pc, k - 1, (px, py, pc)))
    return plan


def _exchange_start(srcs, direct, name):
    na = len(srcs)
    slots = N_DEV - 1 if direct else 3

    def body(*refs):
        src, land = refs[:na], refs[na:2 * na]
        send_sem, recv_sem = refs[2 * na], refs[2 * na + 1]
        token = refs[-1]
        for block, slot, peer in _exchange_plan(direct):
            for a in range(na):
                pltpu.make_async_remote_copy(
                    src_ref=src[a].at[block], dst_ref=land[a].at[slot], send_sem=send_sem.at[slots * a + slot],
                    recv_sem=recv_sem.at[slots * a + slot], device_id=peer, device_id_type=MESH).start()
        token[...] = jnp.zeros_like(token)

    zones = [jax.ShapeDtypeStruct((slots,) + t.shape[1:], t.dtype) for t in srcs]
    bufs = [pltpu.HBM(t.shape, t.dtype) for t in list(srcs) + zones]
    out = pl.pallas_call(
        body, name=name,
        out_shape=(pltpu.SemaphoreType.DMA((slots * na,)), pltpu.SemaphoreType.DMA((slots * na,)), *bufs,
                   jax.ShapeDtypeStruct((8, 128), F32)),
        in_specs=[_HBM] * (2 * na),
        out_specs=(_SEM, _SEM, *([_HBM] * (2 * na)), pl.BlockSpec(memory_space=pltpu.VMEM)),
        input_output_aliases={i: 2 + i for i in range(2 * na)},
        compiler_params=pltpu.CompilerParams(has_side_effects=_EFFECT),
    )(*[pltpu.with_memory_space_constraint(t, pltpu.HBM) for t in srcs],
      *[pltpu.with_memory_space_constraint(lax.empty(t.shape, t.dtype), pltpu.HBM) for t in zones])
    return out[0], out[1], out[2:2 + na], out[2 + na:2 + 2 * na], out[-1]


def _exchange_wait(send_sem, recv_sem, srcs, lands, direct, after, name):
    na = len(srcs)
    slots = N_DEV - 1 if direct else 3

    def body(*refs):
        src, land = refs[:na], refs[na:2 * na]
        send_sem_ref, recv_sem_ref = refs[2 * na], refs[2 * na + 1]
        for block, slot, peer in _exchange_plan(direct):
            for a in range(na):
                cp = pltpu.make_async_remote_copy(
                    src_ref=src[a].at[block], dst_ref=land[a].at[slot], send_sem=send_sem_ref.at[slots * a + slot],
                    recv_sem=recv_sem_ref.at[slots * a + slot], device_id=peer, device_id_type=MESH)
                cp.wait_send()
                cp.wait_recv()

    bufs = [pltpu.HBM(t.shape, t.dtype) for t in list(srcs) + list(lands)]
    out = pl.pallas_call(
        body, name=name,
        out_shape=tuple(bufs),
        in_specs=[_HBM] * (2 * na) + [_SEM, _SEM, pl.BlockSpec(memory_space=pl.ANY)],
        out_specs=[_HBM] * (2 * na),
        input_output_aliases={i: i for i in range(2 * na)},
        compiler_params=pltpu.CompilerParams(has_side_effects=_EFFECT),
    )(*srcs, *lands, send_sem, recv_sem, after)
    return out[:na], out[na:]


def _own_then_slots(mine_ref, lands_ref, rows=slice(None)):
    if len(mine_ref.shape) == 3:
        x, y, c = _mesh_pos()
        total = mine_ref[4 * x + 2 * y + c, rows, :].astype(F32)
    else:
        total = mine_ref[rows, :].astype(F32)
    for j in range(lands_ref.shape[0]):
        total = total + lands_ref[j, rows, :].astype(F32)
    return total


SMALL_ROWS = 2 * PACK_SLICE + 2 * 8


def _small_block(mine, lands, dgpre, name):
    def body(*refs):
        hm, ld, dg = refs[:DEPTH], refs[DEPTH:2 * DEPTH], refs[2 * DEPTH:3 * DEPTH]
        blk, land, sem = refs[3 * DEPTH:]
        for l in range(DEPTH):
            blk[PACK_SLICE * l:PACK_SLICE * (l + 1), :] = _own_then_slots(hm[l], ld[l])
            blk[2 * PACK_SLICE + 8 * l:2 * PACK_SLICE + 8 * (l + 1), :] = dg[l][...]
        cp = pltpu.make_async_copy(blk, _device_rows(land, SMALL_ROWS, *_mesh_pos()), sem)
        cp.start()
        cp.wait()

    vmem = pl.BlockSpec(memory_space=pltpu.VMEM)
    return pl.pallas_call(
        body, name=name,
        in_specs=[vmem] * (3 * DEPTH), out_specs=[vmem, pl.BlockSpec(memory_space=pl.ANY)],
        out_shape=[jax.ShapeDtypeStruct((SMALL_ROWS, 128), F32), jax.ShapeDtypeStruct((N_DEV * SMALL_ROWS, 128), F32)],
        scratch_shapes=[pltpu.SemaphoreType.DMA],
        compiler_params=_compiler_params(),
    )(*mine, *lands, *dgpre)


def _adamw_math(w, g, m, v):
    m = ADAM_B1 * m + (1.0 - ADAM_B1) * g
    v = ADAM_B2 * v + (1.0 - ADAM_B2) * (g * g)
    m_hat = m / (1.0 - ADAM_B1 ** ADAM_STEP)
    v_hat = v / (1.0 - ADAM_B2 ** ADAM_STEP)
    delta = -ADAM_LR * (m_hat / (jnp.sqrt(v_hat) + ADAM_EPS) + ADAM_WD * w)
    return delta, m, v


def _adamw_layer(layer, mine, lands, w, m, v, earlier, token, name, rows):
    _, mm, nn = w.shape

    def body(hm_ref, ld_ref, w_ref, m_ref, v_ref, _, *refs):
        g_ref, d_ref, nm_ref, nv_ref = refs[-4:]
        g = _own_then_slots(hm_ref, ld_ref)
        g_ref[...] = g
        d, nm, nv = _adamw_math(w_ref[...], g, m_ref[...], v_ref[...])
        d_ref[...] = d
        nm_ref[...] = nm
        nv_ref[...] = nv

    spec = pl.BlockSpec((None, rows, nn), lambda i: (layer, i, 0))
    carried = [] if earlier is None else list(earlier)
    return pl.pallas_call(
        body, name=name, grid=(mm // rows,),
        in_specs=([pl.BlockSpec((rows, nn), lambda i: (i, 0)) if mine.ndim == 2
                   else pl.BlockSpec((N_DEV, rows, nn), lambda i: (0, i, 0)),
                   pl.BlockSpec((lands.shape[0], rows, nn), lambda i: (0, i, 0)),
                   spec, spec, spec] + [pl.BlockSpec(memory_space=pl.ANY)] * (1 + len(carried))),
        out_specs=[spec] * 4,
        out_shape=[jax.ShapeDtypeStruct(w.shape, F32)] * 4,
        input_output_aliases={6 + t: t for t in range(len(carried))},
        compiler_params=_compiler_params(("arbitrary",)),
    )(mine, lands, w, m, v, token, *carried)


def _adamw_small(gathered, params):
    def body(all_ref, *refs):
        ins, outs, packs = refs[:15], refs[15:15 + 21], refs[15 + 21]
        loss_ref = outs[0]
        for dev in range(N_DEV):
            for l in range(DEPTH):
                packs[l, PACK_SLICE * dev:PACK_SLICE * (dev + 1), :] = (
                    all_ref[SMALL_ROWS * dev + PACK_SLICE * l:SMALL_ROWS * dev + PACK_SLICE * (l + 1), :])
        loss_ref[...] = packs[DEPTH - 1, ROW_LOSS:ROW_LOSS + 1, 0:1]

        def update(p, sel, g):
            w_ref, m_ref, v_ref = ins[p], ins[5 + p], ins[10 + p]
            d, nm, nv = _adamw_math(w_ref[sel], g, m_ref[sel], v_ref[sel])
            for t, val in enumerate((g, d, nm, nv)):
                outs[1 + 5 * t + p][sel] = val

        for l in range(DEPTH):
            gp = packs.at[l]
            row0 = 2 * PACK_SLICE + 8 * l
            dgpre = all_ref[row0:row0 + 8, :]
            for dev in range(1, N_DEV):
                dgpre = dgpre + all_ref[SMALL_ROWS * dev + row0:SMALL_ROWS * dev + row0 + 8, :]
            for grp in range(4):
                update(0, (l, grp), gp[ROW_PW + BLOCK * grp:ROW_PW + BLOCK * (grp + 1), :])
                update(1, (slice(l, l + 1), slice(128 * grp, 128 * (grp + 1))), gp[ROW_SC + grp:ROW_SC + grp + 1, :])
            update(2, (slice(l, l + 1), slice(None)), gp[ROW_SINK:ROW_SINK + 1, 0:N_HEADS])
            for r in range(D_MODEL // 128):
                sel = (slice(l, l + 1), slice(128 * r, 128 * (r + 1)))
                update(3, sel, dgpre[r:r + 1, :])
                update(4, sel, gp[ROW_NPOST + r:ROW_NPOST + r + 1, :])

    shapes = [jax.ShapeDtypeStruct(p.shape, F32) for p in params[:5]]
    return pl.pallas_call(
        body, name="adamw_small",
        out_shape=[jax.ShapeDtypeStruct((1, 1), F32)] + shapes * 4,
        scratch_shapes=[pltpu.VMEM((DEPTH, PACK_ROWS, 128), F32)],
        compiler_params=_compiler_params(),
    )(gathered, *params)


def kernel(x, w_in, pool_w, pool_scale, attn_sinks, w_out, norm_pre, norm_post, loss_target, m_w_in, m_pool_w, m_pool_scale, m_attn_sinks, m_w_out, m_norm_pre, m_norm_post, v_w_in, v_pool_w, v_pool_scale, v_attn_sinks, v_w_out, v_norm_pre, v_norm_post):
    x0 = x.reshape(SEQ, D_MODEL)
    target = loss_target.reshape(SEQ, D_MODEL)
    bias = jnp.asarray(_attn_bias())
    w_in_t, m_in_t, v_in_t = (jnp.swapaxes(t, 1, 2) for t in (w_in, m_w_in, v_w_in))

    (win0, wout0), later, lands = _allgather([(w_in_t, 0), (w_out, 0)], BF16, "gather_w0",
                                              later=[(w_in_t, 1), (w_out, 1)])
    sems, later, lands, token = _gather_start(later, lands, [SIBLING_AND_SAME_CORE, ALL_PEERS], "gather_w1_start")
    win_full, wout_full = [win0, None], [wout0, None]

    saved = []
    xl = x0
    for layer in range(DEPTH):
        u, pg, q, k, v, ag, z, a = _fwd_front(layer, xl, norm_pre, win_full[layer], token, attn_sinks,
                                             pool_w, pool_scale, bias)
        if layer == 0:
            land = _gather_wait(sems[0], later[0], lands[0], SIBLING_AND_SAME_CORE, z, "gather_w_in1_wait")
            fsems, land, token = _forward_start(land, IN_SHARD, "forward_w_in1_start")
        else:
            wout_full[layer] = _gather_wait(sems[1], later[1], lands[1], ALL_PEERS, z, "gather_w_out1_wait")
        x_next, y = _fwd_out(layer, z, xl, norm_post, wout_full[layer], token)
        if layer == 0:
            win_full[1] = _forward_wait(fsems, land, IN_SHARD, x_next, "forward_w_in1_wait")
        saved.append((xl, u, pg, q, k, v, ag, z, a, y))
        xl = x_next

    params_small = [pool_w, pool_scale, attn_sinks, norm_pre, norm_post,
                    m_pool_w, m_pool_scale, m_attn_sinks, m_norm_pre, m_norm_post,
                    v_pool_w, v_pool_scale, v_attn_sinks, v_norm_pre, v_norm_post]
    exchange, dgpre = [None] * DEPTH, [None] * DEPTH
    dx = None
    for layer in reversed(range(DEPTH)):
        xin, u, pg, q, k, v, ag, z, a, y = saved[layer]
        if layer == DEPTH - 1:
            dx, dz, gw_out, pack = _bwd_out(layer, True, xl, target, y, z, norm_post, wout_full[layer])
        else:
            dz, gw_out, pack = _bwd_out(layer, False, dx, token, y, z, norm_post, wout_full[layer])
        dproj, pack = _bwd_mix(layer, attn_sinks, dz, u, pg, q, k, v, ag, a, pool_w, pool_scale, bias, pack)
        if layer > 0:
            dx, dgpre[layer], gw_in_t = _bwd_in(layer, "both", token, dproj, xin, norm_pre, dx, win_full[layer])
        else:
            (gw_in_t,) = _bwd_in(layer, "dw", token, dproj, xin, norm_pre)
        blocks = [gw_in_t.reshape(N_DEV, IN_SHARD, D_MODEL), gw_out.reshape(N_DEV, OUT_SHARD, D_MODEL),
                  pack.reshape(N_DEV, PACK_SLICE, 128)]
        direct = layer > 0
        srcs, mine = (blocks, None) if direct else _pair_reduce(blocks, f"pair_reduce{layer}")
        send_sem, recv_sem, srcs, lands, token = _exchange_start(srcs, direct, f"exchange_start{layer}")
        exchange[layer] = (send_sem, recv_sem, srcs, lands, mine)
        if layer == 0:
            dx, dgpre[layer] = _bwd_in(layer, "dx", token, dproj, xin, norm_pre, dx, win_full[layer])

    own, waited = [None] * DEPTH, [None] * DEPTH
    big_in, big_out, after = None, None, dx
    for layer in reversed(range(DEPTH)):
        send_sem, recv_sem, srcs, lands, mine = exchange[layer]
        srcs, waited[layer] = _exchange_wait(send_sem, recv_sem, srcs, lands, layer > 0, after, f"exchange_wait{layer}")
        own[layer] = srcs if mine is None else mine
        if layer > 0:
            big_in = _adamw_layer(layer, own[layer][0], waited[layer][0], w_in_t, m_in_t, v_in_t, big_in, token,
                                  f"adamw_in{layer}", 96)
            big_out = _adamw_layer(layer, own[layer][1], waited[layer][1], w_out, m_w_out, v_w_out, big_out, token,
                                   f"adamw_out{layer}", 128)
            after = big_out[0]
    block, land = _small_block([own[l][2] for l in range(DEPTH)], [waited[l][2] for l in range(DEPTH)],
                               dgpre, "small_block")
    sems, block, land, token = _gather_start([block], [land], [ALL_PEERS], "gather_small_start")
    big_in = _adamw_layer(0, own[0][0], waited[0][0], w_in_t, m_in_t, v_in_t, big_in, token, "adamw_in0", 96)
    big_out = _adamw_layer(0, own[0][1], waited[0][1], w_out, m_w_out, v_w_out, big_out, token, "adamw_out0", 128)
    gathered = _gather_wait(sems[0], block[0], land[0], ALL_PEERS, big_out[0], "gather_small_wait")
    small_out = _adamw_small(gathered, params_small)
    loss = small_out[0].reshape(())

    outs = [loss, dx.reshape(1, SEQ, D_MODEL)]
    for t in range(4):
        pw_, sc_, sk_, npre_, npost_ = small_out[1 + 5 * t:6 + 5 * t]
        outs += [jnp.swapaxes(big_in[t], 1, 2), pw_, sc_, sk_, big_out[t], npre_, npost_]
    return tuple(outs)
```

```python
import numpy as np
import jax
import jax.numpy as jnp
from jax import lax
from jax.experimental import pallas as pl
from jax.experimental.pallas import tpu as pltpu

F32 = jnp.float32
BF16 = jnp.bfloat16

N_DEV = 8
SEQ = 2048
D_MODEL = 1024
D_POOL = 512
D_ATTN = 512
D_KV = 128
D_IN = 2304
N_HEADS = 8
GQA = 4
HEAD_DIM = 64
BLOCK = 128
N_BLOCKS = SEQ // BLOCK
POOL_WINDOWS = (2, 4, 8, 16)
DEPTH = 2
EPS = 1e-6
NEG_INF = -1e30
SCALE = HEAD_DIM ** -0.5
IN_SHARD = D_IN // N_DEV
OUT_SHARD = D_MODEL // N_DEV

COL_U, COL_PG, COL_Q, COL_K, COL_V, COL_AG = 0, 512, 1024, 1536, 1664, 1792

ADAM_LR = 0.001
ADAM_B1 = 0.9
ADAM_B2 = 0.999
ADAM_EPS = 1e-08
ADAM_WD = 0.01
ADAM_STEP = 10

TOKEN_TILE = 512
VMEM_LIMIT = 56 * 1024 * 1024
MESH = pl.DeviceIdType.MESH

ROW_PW, ROW_SC, ROW_SINK, ROW_NPRE, ROW_NPOST, ROW_LOSS = 0, 512, 520, 528, 536, 544
PACK_ROWS = 576
PACK_SLICE = PACK_ROWS // N_DEV


def _nn(a, b):
    return jnp.dot(a, b, preferred_element_type=F32)


def _nt(a, b):
    return lax.dot_general(a, b, (((1,), (1,)), ((), ())), preferred_element_type=F32)


def _tn(a, b):
    return lax.dot_general(a, b, (((0,), (0,)), ((), ())), preferred_element_type=F32)


def _silu_parts(g):
    s = jax.nn.sigmoid(g)
    return g * s, s * (1.0 + g * (1.0 - s))


def _resident(shape):
    return pl.BlockSpec(shape, lambda *_: (0,) * len(shape), pipeline_mode=pl.Buffered(1))


def _compiler_params(sem=None):
    if sem is None:
        return pltpu.CompilerParams(vmem_limit_bytes=VMEM_LIMIT)
    return pltpu.CompilerParams(dimension_semantics=sem, vmem_limit_bytes=VMEM_LIMIT)


def _attn_bias():
    t = np.arange(BLOCK)[None, :]
    j = np.arange(2 * BLOCK)[:, None]
    dist = t + BLOCK - j
    in_win = (dist >= 0) & (dist < BLOCK)
    out = np.zeros((2, 2, 2 * BLOCK, GQA * BLOCK), np.float32)
    for variant in range(2):
        valid = in_win & ((j >= BLOCK) | (variant == 1))
        for kv in range(2):
            for g in range(GQA):
                slope = np.float32(2.0 ** (-(kv * GQA + g + 1)))
                b = np.where(valid, -slope * dist.astype(np.float32), np.float32(NEG_INF))
                out[variant, kv, :, g * BLOCK:(g + 1) * BLOCK] = b
    return out


def _replicate_head(kx, kv):
    rolled = pltpu.roll(kx, 64, 1)
    lane = lax.broadcasted_iota(jnp.int32, kx.shape, 1)
    dup = jnp.where(lane < 64, kx, rolled) if kv == 0 else jnp.where(lane < 64, rolled, kx)
    return jnp.concatenate([dup, dup], axis=1).astype(BF16)


def _stack_heads(qv):
    lane = lax.broadcasted_iota(jnp.int32, qv.shape, 1)
    zero = jnp.zeros_like(qv)
    return jnp.concatenate([jnp.where((lane >= 64 * g) & (lane < 64 * g + 64), qv, zero) for g in range(GQA)], axis=0)


def _unstack_heads(xs):
    lane = lax.broadcasted_iota(jnp.int32, (BLOCK, 256), 1)
    return jnp.where(lane < 64, xs[0:128], jnp.where(lane < 128, xs[128:256], jnp.where(lane < 192, xs[256:384], xs[384:512])))


def _fold_heads(r):
    h = r[:, 0:128] + r[:, 128:256]
    return h + pltpu.roll(h, 64, 1)


def _sink_row(sink_ref, layer, kv):
    lane = lax.broadcasted_iota(jnp.int32, (1, GQA * BLOCK), 1)
    s4 = [sink_ref[layer, kv * GQA + g] for g in range(GQA)]
    return jnp.where(lane < 128, s4[0], jnp.where(lane < 256, s4[1], jnp.where(lane < 384, s4[2], s4[3])))


def _probs_keys_major(k_rep, q_st, bias, sink):
    st = _nt(k_rep, q_st) * SCALE + bias
    m = jnp.maximum(jnp.max(st, axis=0, keepdims=True), sink)
    p = jnp.exp(st - m)
    esink = jnp.exp(sink - m)
    rl = 1.0 / (jnp.sum(p, axis=0, keepdims=True) + esink)
    return p * rl, esink * rl


WINDOW_HALO = 16


def _window_sum(ext, w, forward):
    s = ext
    sh = 1
    while sh < w:
        s = s + pltpu.roll(s, (ext.shape[0] - sh) if forward else sh, 0)
        sh *= 2
    return s


def _inv_count(n, w):
    t = n * BLOCK + lax.broadcasted_iota(jnp.int32, (BLOCK, 1), 0) + 1
    return 1.0 / jnp.minimum(t.astype(F32), float(w))


def _kv_ext(ref, n):
    r0 = pl.multiple_of(jnp.maximum(n - 1, 0) * BLOCK, BLOCK)
    r1 = pl.multiple_of(n * BLOCK, BLOCK)
    return jnp.concatenate([ref[pl.ds(r0, BLOCK), :], ref[pl.ds(r1, BLOCK), :]], axis=0)


def _rows_of(vec_ref, pack_ref, row0):
    for r in range(D_MODEL // 128):
        pack_ref[row0 + r:row0 + r + 1, :] = vec_ref[:, 128 * r:128 * (r + 1)]


FRONT_TILE = 2 * BLOCK


def _fwd_front(layer, x, norm_pre, w_in_t, token, sinks, pool_w, pool_scale, bias):
    tm = FRONT_TILE

    def body(sink_ref, x_ref, g_ref, w_ref, _, pw_ref, sc_ref, bias_ref,
             u_ref, pg_ref, q_ref, k_ref, v_ref, ag_ref, z_ref, a_ref, uprev, kprev, vprev):
        i = pl.program_id(0)

        @pl.when(i == 0)
        def _():
            uprev[...] = jnp.zeros_like(uprev)
            kprev[...] = jnp.zeros_like(kprev)
            vprev[...] = jnp.zeros_like(vprev)

        xv = x_ref[...]
        r = lax.rsqrt(jnp.mean(xv * xv, axis=-1, keepdims=True) + EPS)
        h = (xv * r * g_ref[layer:layer + 1, :]).astype(BF16)
        u_ref[...] = _nt(h, w_ref[COL_U:COL_PG, :])
        pg_ref[...] = _nt(h, w_ref[COL_PG:COL_Q, :])
        for sb in range(tm // BLOCK):
            n = (tm // BLOCK) * i + sb
            rows = slice(BLOCK * sb, BLOCK * (sb + 1))
            before = slice(BLOCK * (sb - 1), BLOCK * sb)
            uv = u_ref[rows, :]
            halo = (uprev[BLOCK - WINDOW_HALO:, :] if sb == 0
                    else u_ref[BLOCK * sb - WINDOW_HALO:BLOCK * sb, :])
            ext = jnp.concatenate([halo, uv], axis=0)
            for g, w in enumerate(POOL_WINDOWS):
                cs = slice(BLOCK * g, BLOCK * (g + 1))
                win = _window_sum(ext[:, cs], w, forward=False)[WINDOW_HALO:]
                pooled = win * _inv_count(n, w) - uv[:, cs]
                mixed = _nn(pooled.astype(BF16), pw_ref[g].astype(BF16))
                gate, _ = _silu_parts(pg_ref[rows, cs])
                z_ref[rows, cs] = (mixed * sc_ref[layer:layer + 1, cs] * gate).astype(BF16)

        q_ref[...] = _nt(h, w_ref[COL_Q:COL_K, :]).astype(BF16)
        k_ref[...] = _nt(h, w_ref[COL_K:COL_V, :])
        v_ref[...] = _nt(h, w_ref[COL_V:COL_AG, :])
        ag_ref[...] = _nt(h, w_ref[COL_AG:D_IN, :])

        for sb in range(tm // BLOCK):
            n = (tm // BLOCK) * i + sb
            rows = slice(BLOCK * sb, BLOCK * (sb + 1))
            before = slice(BLOCK * (sb - 1), BLOCK * sb)
            kx = jnp.concatenate([kprev[...] if sb == 0 else k_ref[before, :], k_ref[rows, :]], axis=0)
            vx = jnp.concatenate([vprev[...] if sb == 0 else v_ref[before, :], v_ref[rows, :]], axis=0)
            variant = jnp.minimum(n, 1) if sb == 0 else 1
            for kv in range(2):
                cs = slice(256 * kv, 256 * (kv + 1))
                p, _ = _probs_keys_major(_replicate_head(kx, kv), _stack_heads(q_ref[rows, cs]),
                                         bias_ref[variant, kv], _sink_row(sink_ref, layer, kv))
                o = _unstack_heads(_tn(p.astype(BF16), _replicate_head(vx, kv)))
                a_ref[rows, cs] = o
                gate, _ = _silu_parts(ag_ref[rows, cs])
                z_ref[rows, D_POOL + 256 * kv:D_POOL + 256 * (kv + 1)] = (o * gate).astype(BF16)

        tail = slice(tm - BLOCK, tm)
        uprev[...] = u_ref[tail, :]
        kprev[...] = k_ref[tail, :]
        vprev[...] = v_ref[tail, :]

    row = lambda c: pl.BlockSpec((tm, c), lambda i: (i, 0))
    const = lambda shape: pl.BlockSpec(shape, lambda i: (0,) * len(shape))
    return pl.pallas_call(
        body, name=f"fwd_front{layer}", grid=(SEQ // tm,),
        in_specs=[pl.BlockSpec(memory_space=pltpu.SMEM), row(D_MODEL), const((DEPTH, D_MODEL)),
                  _resident((D_IN, D_MODEL)), const((8, 128)),
                  pl.BlockSpec((None, 4, BLOCK, BLOCK), lambda i: (layer, 0, 0, 0)), const((DEPTH, D_POOL)),
                  _resident((2, 2, 2 * BLOCK, GQA * BLOCK))],
        out_specs=[row(D_POOL), row(D_POOL), row(D_ATTN), row(D_KV), row(D_KV), row(D_ATTN), row(D_MODEL),
                   row(D_ATTN)],
        out_shape=[jax.ShapeDtypeStruct((SEQ, D_POOL), F32), jax.ShapeDtypeStruct((SEQ, D_POOL), F32),
                   jax.ShapeDtypeStruct((SEQ, D_ATTN), BF16), jax.ShapeDtypeStruct((SEQ, D_KV), F32),
                   jax.ShapeDtypeStruct((SEQ, D_KV), F32), jax.ShapeDtypeStruct((SEQ, D_ATTN), F32),
                   jax.ShapeDtypeStruct((SEQ, D_MODEL), BF16), jax.ShapeDtypeStruct((SEQ, D_ATTN), F32)],
        scratch_shapes=[pltpu.VMEM((BLOCK, D_POOL), F32), pltpu.VMEM((BLOCK, D_KV), F32),
                        pltpu.VMEM((BLOCK, D_KV), F32)],
        compiler_params=_compiler_params(("arbitrary",)),
    )(sinks, x, norm_pre, w_in_t, token, pool_w, pool_scale, bias)


def _fwd_out(layer, z, x, norm_post, w_out, token):
    tm = TOKEN_TILE

    def body(z_ref, x_ref, g_ref, w_ref, _, xn_ref, y_ref):
        y = _nn(z_ref[...], w_ref[...])
        y_ref[...] = y
        r = lax.rsqrt(jnp.mean(y * y, axis=-1, keepdims=True) + EPS)
        xn_ref[...] = x_ref[...] + y * r * g_ref[layer:layer + 1, :]

    row = lambda c: pl.BlockSpec((tm, c), lambda i: (i, 0))
    return pl.pallas_call(
        body, name=f"fwd_out{layer}", grid=(SEQ // tm,),
        in_specs=[row(D_MODEL), row(D_MODEL), pl.BlockSpec((DEPTH, D_MODEL), lambda i: (0, 0)),
                  _resident((D_MODEL, D_MODEL)), pl.BlockSpec((8, 128), lambda i: (0, 0))],
        out_specs=[row(D_MODEL), row(D_MODEL)],
        out_shape=[jax.ShapeDtypeStruct((SEQ, D_MODEL), F32), jax.ShapeDtypeStruct((SEQ, D_MODEL), F32)],
        compiler_params=_compiler_params(("arbitrary",)),
    )(z, x, norm_post, w_out, token)


BACK_TILE = 2 * BLOCK


def _bwd_back(layer, top, dxo_or_xf, target_or_token, y, z, norm_post, w_out, sinks, u, pg, q, k, v, ag, a,
              pool_w, pool_scale, bias):
    tm = BACK_TILE
    steps = SEQ // tm
    last = steps - 1
    per = tm // BLOCK

    def body(*refs):
        refs = list(refs)
        sink_ref, first, second = refs[:3]
        (y_ref, z_ref, g_ref, w_ref, u_ref, up_ref, pg_ref, q_ref, k_ref, v_ref, ag_ref, a_ref, pw_ref, sc_ref,
         bias_ref) = refs[3:18]
        del refs[:18]
        dxo_ref = refs.pop(0) if top else None
        dp_ref, dw_ref, pack_ref, acc, dg, lacc, dzs, ck, cv, ce = refs
        i = pl.program_id(0)
        blk = last - i

        @pl.when(i == 0)
        def _():
            acc[...] = jnp.zeros_like(acc)
            dg[...] = jnp.zeros_like(dg)
            lacc[...] = jnp.zeros_like(lacc)
            pack_ref[...] = jnp.zeros_like(pack_ref)
            ck[...] = jnp.zeros_like(ck)
            cv[...] = jnp.zeros_like(cv)
            ce[...] = jnp.zeros_like(ce)

        if top:
            d = first[...] - second[...]
            dxo_v = d * (1.0 / D_MODEL)
            dxo_ref[...] = dxo_v
            part = jnp.sum(d * d, axis=-1, keepdims=True) * (1.0 / D_MODEL)
            lacc[...] += 0.5 * jnp.sum(part, axis=0, keepdims=True)
        else:
            dxo_v = first[...]
        yv = y_ref[...]
        r = lax.rsqrt(jnp.mean(yv * yv, axis=-1, keepdims=True) + EPS)
        yn = yv * r
        dg[...] += jnp.sum(dxo_v * yn, axis=0, keepdims=True)
        dyn = dxo_v * g_ref[layer:layer + 1, :]
        dy = (r * (dyn - yn * jnp.mean(dyn * yn, axis=-1, keepdims=True))).astype(BF16)
        dzs[...] = _nt(dy, w_ref[...])
        acc[...] += _tn(z_ref[...], dy)

        lane = lax.broadcasted_iota(jnp.int32, (1, 128), 1)
        lane2 = lax.broadcasted_iota(jnp.int32, (256, 128), 1)
        for sb in reversed(range(per)):
            n = per * blk + sb
            rows = slice(BLOCK * sb, BLOCK * (sb + 1))

            uv = u_ref[rows, :]
            if sb == 0:
                halo = up_ref[BLOCK - WINDOW_HALO:, :] * (n > 0).astype(F32)
            else:
                halo = u_ref[BLOCK * sb - WINDOW_HALO:BLOCK * sb, :]
            ext = jnp.concatenate([halo, uv], axis=0)
            for g, w in enumerate(POOL_WINDOWS):
                cs = slice(BLOCK * g, BLOCK * (g + 1))
                inv = _inv_count(n, w)
                win = _window_sum(ext[:, cs], w, forward=False)[WINDOW_HALO:]
                pooled = win * inv - uv[:, cs]
                pw_g = pw_ref[g].astype(BF16)
                mixed = _nn(pooled.astype(BF16), pw_g)
                gate, dgate = _silu_parts(pg_ref[rows, cs])
                dzp = dzs[rows, cs]
                sc = sc_ref[layer:layer + 1, cs]
                dpm = dzp * gate
                dp_ref[rows, COL_PG + BLOCK * g:COL_PG + BLOCK * (g + 1)] = (dzp * (mixed * sc) * dgate).astype(BF16)
                pack_ref[ROW_SC + g:ROW_SC + g + 1, :] += jnp.sum(dpm * mixed, axis=0, keepdims=True)
                dmixed = (dpm * sc).astype(BF16)
                pack_ref[ROW_PW + BLOCK * g:ROW_PW + BLOCK * (g + 1), :] += _tn(pooled.astype(BF16), dmixed)
                dpooled = _nt(dmixed, pw_g)
                e = dpooled * inv
                lead = _window_sum(jnp.concatenate([e, ce[:WINDOW_HALO, cs]], axis=0), w, forward=True)[:BLOCK]
                dp_ref[rows, COL_U + BLOCK * g:COL_U + BLOCK * (g + 1)] = (lead - dpooled).astype(BF16)
                ce[:, cs] = e

            kx = _kv_ext(k_ref, n)
            vx = _kv_ext(v_ref, n)
            variant = jnp.minimum(n, 1) if sb == 0 else 1
            dsink_row = jnp.zeros((1, 128), F32)
            tks, tvs = [], []
            for kv in range(2):
                cs = slice(256 * kv, 256 * (kv + 1))
                k_rep = _replicate_head(kx, kv)
                v_rep = _replicate_head(vx, kv)
                q_st = _stack_heads(q_ref[rows, cs])
                gate, dgate = _silu_parts(ag_ref[rows, cs])
                dza = dzs[rows, D_POOL + 256 * kv:D_POOL + 256 * (kv + 1)]
                dp_ref[rows, COL_AG + 256 * kv:COL_AG + 256 * (kv + 1)] = (dza * a_ref[rows, cs] * dgate).astype(BF16)
                da_st = _stack_heads((dza * gate).astype(BF16))
                p, psink = _probs_keys_major(k_rep, q_st, bias_ref[variant, kv], _sink_row(sink_ref, layer, kv))
                dpt = _nt(v_rep, da_st)
                delta = jnp.sum(p * dpt, axis=0, keepdims=True)
                dst = (p * (dpt - delta) * SCALE).astype(BF16)
                sink_terms = psink * delta
                for g in range(GQA):
                    dsink = -jnp.sum(sink_terms[:, BLOCK * g:BLOCK * (g + 1)], axis=1, keepdims=True)
                    dsink_row = dsink_row + jnp.where(lane == kv * GQA + g, dsink, 0.0)
                dp_ref[rows, COL_Q + 256 * kv:COL_Q + 256 * (kv + 1)] = _unstack_heads(_tn(dst, k_rep)).astype(BF16)
                tks.append(_fold_heads(_nn(dst, q_st)))
                tvs.append(_fold_heads(_nn(p.astype(BF16), da_st)))
            pack_ref[ROW_SINK:ROW_SINK + 1, :] += dsink_row
            dkx = jnp.where(lane2 < 64, tks[0], tks[1])
            dvx = jnp.where(lane2 < 64, tvs[0], tvs[1])
            dp_ref[rows, COL_K:COL_V] = (ck[...] + dkx[BLOCK:]).astype(BF16)
            dp_ref[rows, COL_V:COL_AG] = (cv[...] + dvx[BLOCK:]).astype(BF16)
            ck[...] = dkx[:BLOCK]
            cv[...] = dvx[:BLOCK]

        @pl.when(i == steps - 1)
        def _():
            dw_ref[...] = acc[...].astype(BF16)
            _rows_of(dg, pack_ref, ROW_NPOST)
            pack_ref[ROW_LOSS:ROW_LOSS + 1, :] = jnp.where(lane == 0, lacc[...], 0.0)

    row = lambda c: pl.BlockSpec((tm, c), lambda i: (last - i, 0))
    const = lambda shape: pl.BlockSpec(shape, lambda i: (0,) * len(shape))
    act = jax.ShapeDtypeStruct((SEQ, D_MODEL), F32)
    return pl.pallas_call(
        body, name=f"bwd_back{layer}", grid=(steps,),
        in_specs=[pl.BlockSpec(memory_space=pltpu.SMEM), row(D_MODEL), row(D_MODEL) if top else const((8, 128)),
                  row(D_MODEL), row(D_MODEL), const((DEPTH, D_MODEL)), _resident((D_MODEL, D_MODEL)),
                  row(D_POOL), pl.BlockSpec((BLOCK, D_POOL), lambda i: (jnp.maximum(per * (last - i) - 1, 0), 0)),
                  row(D_POOL), row(D_ATTN), _resident((SEQ, D_KV)), _resident((SEQ, D_KV)), row(D_ATTN), row(D_ATTN),
                  pl.BlockSpec((None, 4, BLOCK, BLOCK), lambda i: (layer, 0, 0, 0)), const((DEPTH, D_POOL)),
                  _resident((2, 2, 2 * BLOCK, GQA * BLOCK))],
        out_specs=([row(D_MODEL)] * (1 if top else 0)
                   + [row(D_IN), const((D_MODEL, D_MODEL)), const((PACK_ROWS, 128))]),
        out_shape=([act] * (1 if top else 0)
                   + [jax.ShapeDtypeStruct((SEQ, D_IN), BF16), jax.ShapeDtypeStruct((D_MODEL, D_MODEL), BF16),
                      jax.ShapeDtypeStruct((PACK_ROWS, 128), F32)]),
        scratch_shapes=[pltpu.VMEM((D_MODEL, D_MODEL), F32), pltpu.VMEM((1, D_MODEL), F32), pltpu.VMEM((1, 1), F32),
                        pltpu.VMEM((tm, D_MODEL), F32), pltpu.VMEM((BLOCK, D_KV), F32), pltpu.VMEM((BLOCK, D_KV), F32),
                        pltpu.VMEM((BLOCK, D_POOL), F32)],
        compiler_params=_compiler_params(("arbitrary",)),
    )(sinks, dxo_or_xf, target_or_token, y, z, norm_post, w_out, u, u, pg, q, k, v, ag, a, pool_w, pool_scale, bias)


def _bwd_out(layer, top, dxo_or_xf, target_or_token, y, z, norm_post, w_out):
    tm = TOKEN_TILE
    steps = SEQ // tm

    def body(*refs):
        if top:
            xf_ref, t_ref, y_ref, z_ref, g_ref, w_ref, dxo_ref, dz_ref, dw_ref, pack_ref, acc, dg, lacc = refs
        else:
            dxi_ref, _, y_ref, z_ref, g_ref, w_ref, dz_ref, dw_ref, pack_ref, acc, dg, lacc = refs
        i = pl.program_id(0)

        @pl.when(i == 0)
        def _():
            acc[...] = jnp.zeros_like(acc)
            dg[...] = jnp.zeros_like(dg)
            lacc[...] = jnp.zeros_like(lacc)
            pack_ref[...] = jnp.zeros_like(pack_ref)

        if top:
            d = xf_ref[...] - t_ref[...]
            dxo_v = d * (1.0 / D_MODEL)
            dxo_ref[...] = dxo_v
            part = jnp.sum(d * d, axis=-1, keepdims=True) * (1.0 / D_MODEL)
            lacc[...] += 0.5 * jnp.sum(part, axis=0, keepdims=True)
        else:
            dxo_v = dxi_ref[...]
        y = y_ref[...]
        r = lax.rsqrt(jnp.mean(y * y, axis=-1, keepdims=True) + EPS)
        yn = y * r
        dg[...] += jnp.sum(dxo_v * yn, axis=0, keepdims=True)
        dyn = dxo_v * g_ref[layer:layer + 1, :]
        dy = (r * (dyn - yn * jnp.mean(dyn * yn, axis=-1, keepdims=True))).astype(BF16)
        dz_ref[...] = _nt(dy, w_ref[...])
        acc[...] += _tn(z_ref[...], dy)

        @pl.when(i == steps - 1)
        def _():
            dw_ref[...] = acc[...].astype(BF16)
            _rows_of(dg, pack_ref, ROW_NPOST)
            lane = lax.broadcasted_iota(jnp.int32, (1, 128), 1)
            pack_ref[ROW_LOSS:ROW_LOSS + 1, :] = jnp.where(lane == 0, lacc[...], 0.0)

    row = lambda c: pl.BlockSpec((tm, c), lambda i: (i, 0))
    const = lambda shape: pl.BlockSpec(shape, lambda i: (0,) * len(shape))
    act = jax.ShapeDtypeStruct((SEQ, D_MODEL), F32)
    return pl.pallas_call(
        body, name=f"bwd_out{layer}", grid=(steps,),
        in_specs=([row(D_MODEL), row(D_MODEL) if top else const((8, 128))]
                  + [row(D_MODEL), row(D_MODEL), const((DEPTH, D_MODEL)), _resident((D_MODEL, D_MODEL))]),
        out_specs=([row(D_MODEL)] * (2 if top else 1) + [const((D_MODEL, D_MODEL)), const((PACK_ROWS, 128))]),
        out_shape=([act] * (2 if top else 1)
                   + [jax.ShapeDtypeStruct((D_MODEL, D_MODEL), BF16), jax.ShapeDtypeStruct((PACK_ROWS, 128), F32)]),
        scratch_shapes=[pltpu.VMEM((D_MODEL, D_MODEL), F32), pltpu.VMEM((1, D_MODEL), F32), pltpu.VMEM((1, 1), F32)],
        compiler_params=_compiler_params(("arbitrary",)),
    )(dxo_or_xf, target_or_token, y, z, norm_post, w_out)


def _bwd_mix(layer, sinks, dz, u, pg, q, k, v, ag, a, pool_w, pool_scale, bias, pack):
    last = N_BLOCKS - 1

    def body(sink_ref, dz_ref, u_ref, up_ref, pg_ref, q_ref, k_ref, v_ref, ag_ref, a_ref, pw_ref, sc_ref,
             bias_ref, pin_ref, dp_ref, pack_ref, ck, cv, ce):
        i = pl.program_id(0)
        n = last - i

        @pl.when(i == 0)
        def _():
            ck[...] = jnp.zeros_like(ck)
            cv[...] = jnp.zeros_like(cv)
            ce[...] = jnp.zeros_like(ce)
            pack_ref[...] = pin_ref[...]

        uv = u_ref[...]
        has_prev = (n > 0).astype(F32)
        ext = jnp.concatenate([up_ref[BLOCK - WINDOW_HALO:, :] * has_prev, uv], axis=0)
        for g, w in enumerate(POOL_WINDOWS):
            cs = slice(BLOCK * g, BLOCK * (g + 1))
            inv = _inv_count(n, w)
            win = _window_sum(ext[:, cs], w, forward=False)[WINDOW_HALO:]
            pooled = win * inv - uv[:, cs]
            pw_g = pw_ref[g].astype(BF16)
            mixed = _nn(pooled.astype(BF16), pw_g)
            gate, dgate = _silu_parts(pg_ref[:, cs])
            dzp = dz_ref[:, cs]
            sc = sc_ref[layer:layer + 1, cs]
            dpm = dzp * gate
            dp_ref[:, COL_PG + BLOCK * g:COL_PG + BLOCK * (g + 1)] = (dzp * (mixed * sc) * dgate).astype(BF16)
            pack_ref[ROW_SC + g:ROW_SC + g + 1, :] += jnp.sum(dpm * mixed, axis=0, keepdims=True)
            dmixed = (dpm * sc).astype(BF16)
            pack_ref[ROW_PW + BLOCK * g:ROW_PW + BLOCK * (g + 1), :] += _tn(pooled.astype(BF16), dmixed)
            dpooled = _nt(dmixed, pw_g)
            e = dpooled * inv
            lead = _window_sum(jnp.concatenate([e, ce[:WINDOW_HALO, cs]], axis=0), w, forward=True)[:BLOCK]
            dp_ref[:, COL_U + BLOCK * g:COL_U + BLOCK * (g + 1)] = (lead - dpooled).astype(BF16)
            ce[:, cs] = e

        kx = _kv_ext(k_ref, n)
        vx = _kv_ext(v_ref, n)
        lane = lax.broadcasted_iota(jnp.int32, (1, 128), 1)
        dsink_row = jnp.zeros((1, 128), F32)
        tks, tvs = [], []
        for kv in range(2):
            cs = slice(256 * kv, 256 * (kv + 1))
            k_rep = _replicate_head(kx, kv)
            v_rep = _replicate_head(vx, kv)
            q_st = _stack_heads(q_ref[:, cs])
            gate, dgate = _silu_parts(ag_ref[:, cs])
            dza = dz_ref[:, D_POOL + 256 * kv:D_POOL + 256 * (kv + 1)]
            dp_ref[:, COL_AG + 256 * kv:COL_AG + 256 * (kv + 1)] = (dza * a_ref[:, cs] * dgate).astype(BF16)
            da_st = _stack_heads((dza * gate).astype(BF16))
            p, psink = _probs_keys_major(k_rep, q_st, bias_ref[0, kv], _sink_row(sink_ref, layer, kv))
            dpt = _nt(v_rep, da_st)
            delta = jnp.sum(p * dpt, axis=0, keepdims=True)
            dst = (p * (dpt - delta) * SCALE).astype(BF16)
            sink_terms = psink * delta
            for g in range(GQA):
                dsink = -jnp.sum(sink_terms[:, BLOCK * g:BLOCK * (g + 1)], axis=1, keepdims=True)
                dsink_row = dsink_row + jnp.where(lane == kv * GQA + g, dsink, 0.0)
            dp_ref[:, COL_Q + 256 * kv:COL_Q + 256 * (kv + 1)] = _unstack_heads(_tn(dst, k_rep)).astype(BF16)
            tks.append(_fold_heads(_nn(dst, q_st)))
            tvs.append(_fold_heads(_nn(p.astype(BF16), da_st)))
        pack_ref[ROW_SINK:ROW_SINK + 1, :] += dsink_row
        lane2 = lax.broadcasted_iota(jnp.int32, (256, 128), 1)
        dkx = jnp.where(lane2 < 64, tks[0], tks[1])
        dvx = jnp.where(lane2 < 64, tvs[0], tvs[1])
        dp_ref[:, COL_K:COL_V] = (ck[...] + dkx[BLOCK:]).astype(BF16)
        dp_ref[:, COL_V:COL_AG] = (cv[...] + dvx[BLOCK:]).astype(BF16)
        ck[...] = dkx[:BLOCK]
        cv[...] = dvx[:BLOCK]

    blk = lambda c: pl.BlockSpec((BLOCK, c), lambda i: (last - i, 0))
    full = lambda shape: pl.BlockSpec(shape, lambda i: (0,) * len(shape))
    return pl.pallas_call(
        body, name=f"bwd_mix{layer}", grid=(N_BLOCKS,),
        in_specs=[pl.BlockSpec(memory_space=pltpu.SMEM), blk(D_MODEL), blk(D_POOL),
                  pl.BlockSpec((BLOCK, D_POOL), lambda i: (jnp.maximum(last - i - 1, 0), 0)),
                  blk(D_POOL), blk(D_ATTN), full((SEQ, D_KV)), full((SEQ, D_KV)), blk(D_ATTN), blk(D_ATTN),
                  pl.BlockSpec((None, 4, BLOCK, BLOCK), lambda i: (layer, 0, 0, 0)), full((DEPTH, D_POOL)),
                  pl.BlockSpec((1, 2, 2 * BLOCK, GQA * BLOCK), lambda i: (jnp.minimum(last - i, 1), 0, 0, 0)),
                  full((PACK_ROWS, 128))],
        out_specs=[blk(D_IN), full((PACK_ROWS, 128))],
        out_shape=[jax.ShapeDtypeStruct((SEQ, D_IN), BF16), jax.ShapeDtypeStruct((PACK_ROWS, 128), F32)],
        scratch_shapes=[pltpu.VMEM((BLOCK, D_KV), F32), pltpu.VMEM((BLOCK, D_KV), F32),
                        pltpu.VMEM((BLOCK, D_POOL), F32)],
        input_output_aliases={13: 1},
        compiler_params=_compiler_params(("arbitrary",)),
    )(sinks, dz, u, u, pg, q, k, v, ag, a, pool_w, pool_scale, bias, pack)


def _bwd_in(layer, part, token, dproj, x, norm_pre, dxo=None, w_in_t=None):
    want_dw, want_dx = part in ("both", "dw"), part in ("both", "dx")
    tm = TOKEN_TILE
    steps = SEQ // tm
    cw = 256

    def body(*refs):
        refs = list(refs)
        dp_ref, x_ref, g_ref = refs[1:4]
        del refs[:4]
        if want_dx:
            dxo_ref, w_ref, dx_ref, dgo_ref = refs[:4]
            del refs[:4]
            dg = refs.pop()
        if want_dw:
            dw_ref, acc = refs
        i = pl.program_id(0)

        @pl.when(i == 0)
        def _():
            if want_dw:
                acc[...] = jnp.zeros_like(acc)
            if want_dx:
                dg[...] = jnp.zeros_like(dg)

        xv = x_ref[...]
        gv = g_ref[layer:layer + 1, :]
        r = lax.rsqrt(jnp.mean(xv * xv, axis=-1, keepdims=True) + EPS)
        xn = xv * r
        if want_dw:
            hb = (xn * gv).astype(BF16)
            for c in range(0, D_IN, cw):
                acc[c:c + cw, :] += _tn(dp_ref[:, c:c + cw], hb)
        if want_dx:
            dh = _nn(dp_ref[...], w_ref[...])
            dg[...] += jnp.sum(dh * xn, axis=0, keepdims=True)
            dhn = dh * gv
            dx_ref[...] = dxo_ref[...] + r * (dhn - xn * jnp.mean(dhn * xn, axis=-1, keepdims=True))

        @pl.when(i == steps - 1)
        def _():
            if want_dw:
                dw_ref[...] = acc[...].astype(BF16)
            if want_dx:
                _rows_of(dg, dgo_ref, 0)

    row = lambda c: pl.BlockSpec((tm, c), lambda i: (i, 0))
    const = lambda shape: pl.BlockSpec(shape, lambda i: (0,) * len(shape))
    in_specs = [const((8, 128)), row(D_IN), row(D_MODEL), const((DEPTH, D_MODEL))]
    operands = [token, dproj, x, norm_pre]
    out_specs, out_shape, scratch = [], [], []
    if want_dx:
        in_specs += [row(D_MODEL), _resident((D_IN, D_MODEL))]
        operands += [dxo, w_in_t]
        out_specs += [row(D_MODEL), const((8, 128))]
        out_shape += [jax.ShapeDtypeStruct((SEQ, D_MODEL), F32), jax.ShapeDtypeStruct((8, 128), F32)]
    if want_dw:
        out_specs.append(const((D_IN, D_MODEL)))
        out_shape.append(jax.ShapeDtypeStruct((D_IN, D_MODEL), BF16))
        scratch.append(pltpu.VMEM((D_IN, D_MODEL), F32))
    if want_dx:
        scratch.append(pltpu.VMEM((1, D_MODEL), F32))
    return pl.pallas_call(
        body, name=f"bwd_in_{part}{layer}", grid=(steps,),
        in_specs=in_specs, out_specs=out_specs, out_shape=out_shape, scratch_shapes=scratch,
        compiler_params=_compiler_params(("arbitrary",)),
    )(*operands)


def _mesh_pos():
    return lax.axis_index("x"), lax.axis_index("y"), lax.axis_index("c")


def _device_rows(ref, m, px, py, pc):
    return ref.at[pl.ds(pl.multiple_of((4 * px + 2 * py + pc) * m, 16 if m % 16 == 0 else 8), m), :]


def _allgather(srcs, out_dtype, name, later=()):
    na, nb = len(srcs), len(later)
    every = list(srcs) + list(later)
    shapes = [(a.shape[-2], a.shape[-1]) for a, _ in every]

    def body(*refs):
        xs, refs = refs[:na + nb], refs[na + nb:]
        outs, cast, land, refs = refs[:na], refs[na:na + nb], refs[na + nb:na + 2 * nb], refs[na + 2 * nb:]
        stage, (send_sems, recv_sems, local_sems) = refs[:na], refs[na:]
        x, y, c = _mesh_pos()
        me, sibling = (x, y, c), (x, y, 1 - c)
        near = [(1 - x, y), (x, 1 - y)]
        far = (1 - x, 1 - y)
        relay_from, relay_to = (x ^ (1 - c), y ^ c), (x ^ c, y ^ (1 - c))
        k_from, k_to = 1 + c, 2 - c

        def slot(a, px, py, pc):
            return _device_rows(outs[a], shapes[a][0], px, py, pc)

        def copy(a, k, block, to, src=None):
            return pltpu.make_async_remote_copy(
                src_ref=slot(a, *block) if src is None else src, dst_ref=slot(a, *block),
                send_sem=send_sems.at[a, k], recv_sem=recv_sems.at[a, k], device_id=to, device_id_type=MESH)

        def cast_block(i):
            layer = every[i][1]
            return (xs[i][...] if layer is None else xs[i][layer]).astype(out_dtype)

        for a in range(na):
            stage[a][...] = cast_block(a)
        mine = [pltpu.make_async_copy(stage[a], slot(a, *me), local_sems.at[a]) for a in range(na)]
        for cp in mine:
            cp.start()
        sent = []
        for a in range(na):
            sent.append(copy(a, 0, me, sibling, src=stage[a]))
            sent += [copy(a, 1 + j, me, (*chip, c), src=stage[a]) for j, chip in enumerate(near)]
        for cp in sent:
            cp.start()
        for b in range(nb):
            cast[b][...] = cast_block(na + b)
            cp = pltpu.make_async_copy(cast[b], _device_rows(land[b], shapes[na + b][0], *me), local_sems.at[na + b])
            cp.start()
            mine.append(cp)
        for a in range(na):
            copy(a, k_from, (*relay_from, c), me).wait_recv()
            sent += [copy(a, 3, (*relay_from, c), (*relay_to, c)), copy(a, 3 + k_from, (*relay_from, c), sibling)]
            sent[-2].start()
            sent[-1].start()
        for a in range(na):
            copy(a, k_to, (*relay_to, c), me).wait_recv()
            sent.append(copy(a, 3 + k_to, (*relay_to, c), sibling))
            sent[-1].start()
        for a in range(na):
            copy(a, 3, (*far, c), me).wait_recv()
            sent.append(copy(a, 6, (*far, c), sibling))
            sent[-1].start()
        for a in range(na):
            copy(a, 0, sibling, me).wait_recv()
            for j, chip in enumerate(near + [far]):
                copy(a, 4 + j, (*chip, 1 - c), me).wait_recv()
        for cp in sent:
            cp.wait_send()
        for cp in mine:
            cp.wait()

    vmem = pl.BlockSpec(memory_space=pltpu.VMEM)
    hbm = pl.BlockSpec(memory_space=pl.ANY)
    gathered = [jax.ShapeDtypeStruct((N_DEV * m, n), out_dtype) for m, n in shapes]
    out = pl.pallas_call(
        body, name=name,
        in_specs=[vmem] * (na + nb),
        out_specs=[hbm] * na + [vmem] * nb + [hbm] * nb,
        out_shape=gathered[:na] + [jax.ShapeDtypeStruct(s, out_dtype) for s in shapes[na:]] + gathered[na:],
        scratch_shapes=([pltpu.VMEM(s, out_dtype) for s in shapes[:na]]
                        + [pltpu.SemaphoreType.DMA((na, 7)), pltpu.SemaphoreType.DMA((na, 7)),
                           pltpu.SemaphoreType.DMA((na + nb,))]),
        compiler_params=_compiler_params(),
    )(*[a for a, _ in every])
    return out[:na], out[na:na + nb], out[na + nb:]


ALL_PEERS = tuple(range(1, N_DEV))
SIBLING_AND_SAME_CORE = (1, 2, 4, 6)


def _related(k, x, y, c):
    return x ^ ((k >> 2) & 1), y ^ ((k >> 1) & 1), c ^ (k & 1)


def _gather_start(blocks, lands, relations, name):
    na = len(blocks)

    def body(*refs):
        src, land, sems, token = refs[:na], refs[na:2 * na], refs[2 * na:4 * na], refs[-1]
        x, y, c = _mesh_pos()
        for a in range(na):
            for k in relations[a]:
                pltpu.make_async_remote_copy(
                    src_ref=src[a], dst_ref=_device_rows(land[a], blocks[a].shape[0], x, y, c),
                    send_sem=sems[2 * a].at[k - 1], recv_sem=sems[2 * a + 1].at[k - 1],
                    device_id=_related(k, x, y, c), device_id_type=MESH).start()
        token[...] = jnp.zeros_like(token)

    bufs = [pltpu.HBM(t.shape, t.dtype) for t in list(blocks) + list(lands)]
    out = pl.pallas_call(
        body, name=name,
        out_shape=(*([pltpu.SemaphoreType.DMA((N_DEV - 1,))] * (2 * na)), *bufs, jax.ShapeDtypeStruct((8, 128), F32)),
        in_specs=[_HBM] * (2 * na),
        out_specs=(*([_SEM] * (2 * na)), *([_HBM] * (2 * na)), pl.BlockSpec(memory_space=pltpu.VMEM)),
        input_output_aliases={i: 2 * na + i for i in range(2 * na)},
        compiler_params=pltpu.CompilerParams(has_side_effects=_EFFECT),
    )(*[pltpu.with_memory_space_constraint(t, pltpu.HBM) for t in list(blocks) + list(lands)])
    sems = [(out[2 * a], out[2 * a + 1]) for a in range(na)]
    return sems, out[2 * na:3 * na], out[3 * na:4 * na], out[-1]


def _gather_wait(sems, block, land, relations, after, name):
    def body(src, land_ref, send_sem, recv_sem, after_ref, src_out, land_out):
        x, y, c = _mesh_pos()
        for k in relations:
            peer = _related(k, x, y, c)
            cp = pltpu.make_async_remote_copy(
                src_ref=src, dst_ref=_device_rows(land_ref, block.shape[0], *peer),
                send_sem=send_sem.at[k - 1], recv_sem=recv_sem.at[k - 1], device_id=peer, device_id_type=MESH)
            cp.wait_send()
            cp.wait_recv()

    out = pl.pallas_call(
        body, name=name,
        out_shape=(pltpu.HBM(block.shape, block.dtype), pltpu.HBM(land.shape, land.dtype)),
        in_specs=[_HBM, _HBM, _SEM, _SEM, pl.BlockSpec(memory_space=pl.ANY)],
        out_specs=[_HBM, _HBM],
        input_output_aliases={0: 0, 1: 1},
        compiler_params=pltpu.CompilerParams(has_side_effects=_EFFECT),
    )(block, land, sems[0], sems[1], after)
    return out[1]


def _forward_plan(land_ref, m):
    x, y, c = _mesh_pos()
    return [_device_rows(land_ref, m, qx, qy, c) for qx, qy in ((1 - x, y), (x, 1 - y), (1 - x, 1 - y))], (x, y, 1 - c)


def _forward_start(land, m, name):
    def body(land_ref, send_sem, recv_sem, land_out, token):
        rows, sibling = _forward_plan(land_ref, m)
        for j, r in enumerate(rows):
            pltpu.make_async_remote_copy(src_ref=r, dst_ref=r, send_sem=send_sem.at[j], recv_sem=recv_sem.at[j],
                                         device_id=sibling, device_id_type=MESH).start()
        token[...] = jnp.zeros_like(token)

    out = pl.pallas_call(
        body, name=name,
        out_shape=(pltpu.SemaphoreType.DMA((3,)), pltpu.SemaphoreType.DMA((3,)), pltpu.HBM(land.shape, land.dtype),
                   jax.ShapeDtypeStruct((8, 128), F32)),
        in_specs=[_HBM],
        out_specs=(_SEM, _SEM, _HBM, pl.BlockSpec(memory_space=pltpu.VMEM)),
        input_output_aliases={0: 2},
        compiler_params=pltpu.CompilerParams(has_side_effects=_EFFECT),
    )(pltpu.with_memory_space_constraint(land, pltpu.HBM))
    return (out[0], out[1]), out[2], out[3]


def _forward_wait(sems, land, m, after, name):
    def body(land_ref, send_sem, recv_sem, after_ref, land_out):
        x, y, c = _mesh_pos()
        mine, sibling = _forward_plan(land_ref, m)
        theirs = [_device_rows(land_ref, m, qx, qy, 1 - c) for qx, qy in ((1 - x, y), (x, 1 - y), (1 - x, 1 - y))]
        for j in range(3):
            cp = pltpu.make_async_remote_copy(src_ref=mine[j], dst_ref=theirs[j], send_sem=send_sem.at[j],
                                              recv_sem=recv_sem.at[j], device_id=sibling, device_id_type=MESH)
            cp.wait_send()
            cp.wait_recv()

    return pl.pallas_call(
        body, name=name,
        out_shape=pltpu.HBM(land.shape, land.dtype),
        in_specs=[_HBM, _SEM, _SEM, pl.BlockSpec(memory_space=pl.ANY)],
        out_specs=_HBM,
        input_output_aliases={0: 0},
        compiler_params=pltpu.CompilerParams(has_side_effects=_EFFECT),
    )(land, sems[0], sems[1], after)


def _row_step(m):
    return next(s for s in (32, 24, 16, 8) if m % s == 0)


def _pair_reduce(arrs, name):
    na = len(arrs)

    def body(*refs):
        gs, hs, hm = refs[:na], refs[na:2 * na], refs[2 * na:3 * na]
        own, ra = refs[3 * na:4 * na], refs[4 * na:5 * na]
        d2d_send, d2d_recv, local_sems = refs[5 * na:]
        x, y, c = _mesh_pos()
        sibling = (x, y, 1 - c)
        loads, sends = [], []
        for a in range(na):
            for q in range(4):
                cp = pltpu.make_async_copy(gs[a].at[2 * q + c], own[a].at[q], local_sems.at[a, q])
                cp.start()
                loads.append(cp)
                cp = pltpu.make_async_remote_copy(
                    src_ref=gs[a].at[2 * q + (1 - c)], dst_ref=ra[a].at[q], send_sem=d2d_send.at[a, q],
                    recv_sem=d2d_recv.at[a, q], device_id=sibling, device_id_type=MESH)
                cp.start()
                sends.append(cp)
        for cp in loads:
            cp.wait()
        for cp in sends:
            cp.wait_recv()
        others = [2 * (1 - x) + y, 2 * x + (1 - y), 2 * (1 - x) + (1 - y)]
        for a in range(na):
            m = arrs[a].shape[1]
            step = _row_step(m)

            def add(i, carry, a=a, step=step):
                rs = pl.ds(pl.multiple_of(i * step, step), step)
                for j, q in enumerate(others):
                    hs[a][j, rs, :] = (own[a][q, rs, :].astype(F32) + ra[a][q, rs, :].astype(F32)).astype(hs[a].dtype)
                q = 2 * x + y
                hm[a][rs, :] = own[a][q, rs, :].astype(F32) + ra[a][q, rs, :].astype(F32)
                return carry

            lax.fori_loop(0, m // step, add, 0)
        for cp in sends:
            cp.wait_send()

    vmem = pl.BlockSpec(memory_space=pltpu.VMEM)
    scratch = [pltpu.VMEM((4,) + t.shape[1:], t.dtype) for t in arrs] * 2
    scratch += [pltpu.SemaphoreType.DMA((na, 4)), pltpu.SemaphoreType.DMA((na, 4)), pltpu.SemaphoreType.DMA((na, 4))]
    out = pl.pallas_call(
        body, name=name,
        in_specs=[pl.BlockSpec(memory_space=pl.ANY)] * na, out_specs=[vmem] * (2 * na),
        out_shape=([jax.ShapeDtypeStruct((3,) + t.shape[1:], t.dtype) for t in arrs]
                   + [jax.ShapeDtypeStruct(t.shape[1:], F32) for t in arrs]),
        scratch_shapes=scratch,
        compiler_params=_compiler_params(),
    )(*arrs)
    return out[:na], out[na:]


_HBM = pl.BlockSpec(memory_space=pltpu.HBM)
_SEM = pl.BlockSpec(memory_space=pltpu.SEMAPHORE)
_EFFECT = pltpu.SideEffectType.DATAFLOW_SIDE_EFFECTING


def _exchange_plan(direct):
    x, y, c = _mesh_pos()
    if not direct:
        return [(j, j, (qx, qy, c)) for j, (qx, qy) in enumerate([(1 - x, y), (x, 1 - y), (1 - x, 1 - y)])]
    plan = []
    for k in range(1, N_DEV):
        px, py, pc = x ^ ((k >> 2) & 1), y ^ ((k >> 1) & 1), c ^ (k & 1)
        plan.append((4 * px + 2 * py + pc, k - 1, (px, py, pc)))
    return plan


def _exchange_start(srcs, direct, name):
    na = len(srcs)
    slots = N_DEV - 1 if direct else 3

    def body(*refs):
        src, land = refs[:na], refs[na:2 * na]
        send_sem, recv_sem = refs[2 * na], refs[2 * na + 1]
        token = refs[-1]
        for block, slot, peer in _exchange_plan(direct):
            for a in range(na):
                pltpu.make_async_remote_copy(
                    src_ref=src[a].at[block], dst_ref=land[a].at[slot], send_sem=send_sem.at[slots * a + slot],
                    recv_sem=recv_sem.at[slots * a + slot], device_id=peer, device_id_type=MESH).start()
        token[...] = jnp.zeros_like(token)

    zones = [jax.ShapeDtypeStruct((slots,) + t.shape[1:], t.dtype) for t in srcs]
    bufs = [pltpu.HBM(t.shape, t.dtype) for t in list(srcs) + zones]
    out = pl.pallas_call(
        body, name=name,
        out_shape=(pltpu.SemaphoreType.DMA((slots * na,)), pltpu.SemaphoreType.DMA((slots * na,)), *bufs,
                   jax.ShapeDtypeStruct((8, 128), F32)),
        in_specs=[_HBM] * (2 * na),
        out_specs=(_SEM, _SEM, *([_HBM] * (2 * na)), pl.BlockSpec(memory_space=pltpu.VMEM)),
        input_output_aliases={i: 2 + i for i in range(2 * na)},
        compiler_params=pltpu.CompilerParams(has_side_effects=_EFFECT),
    )(*[pltpu.with_memory_space_constraint(t, pltpu.HBM) for t in srcs],
      *[pltpu.with_memory_space_constraint(lax.empty(t.shape, t.dtype), pltpu.HBM) for t in zones])
    return out[0], out[1], out[2:2 + na], out[2 + na:2 + 2 * na], out[-1]


def _exchange_wait(send_sem, recv_sem, srcs, lands, direct, after, name):
    na = len(srcs)
    slots = N_DEV - 1 if direct else 3

    def body(*refs):
        src, land = refs[:na], refs[na:2 * na]
        send_sem_ref, recv_sem_ref = refs[2 * na], refs[2 * na + 1]
        for block, slot, peer in _exchange_plan(direct):
            for a in range(na):
                cp = pltpu.make_async_remote_copy(
                    src_ref=src[a].at[block], dst_ref=land[a].at[slot], send_sem=send_sem_ref.at[slots * a + slot],
                    recv_sem=recv_sem_ref.at[slots * a + slot], device_id=peer, device_id_type=MESH)
                cp.wait_send()
                cp.wait_recv()

    bufs = [pltpu.HBM(t.shape, t.dtype) for t in list(srcs) + list(lands)]
    out = pl.pallas_call(
        body, name=name,
        out_shape=tuple(bufs),
        in_specs=[_HBM] * (2 * na) + [_SEM, _SEM, pl.BlockSpec(memory_space=pl.ANY)],
        out_specs=[_HBM] * (2 * na),
        input_output_aliases={i: i for i in range(2 * na)},
        compiler_params=pltpu.CompilerParams(has_side_effects=_EFFECT),
    )(*srcs, *lands, send_sem, recv_sem, after)
    return out[:na], out[na:]


def _own_then_slots(mine_ref, lands_ref, rows=slice(None)):
    if len(mine_ref.shape) == 3:
        x, y, c = _mesh_pos()
        total = mine_ref[4 * x + 2 * y + c, rows, :].astype(F32)
    else:
        total = mine_ref[rows, :].astype(F32)
    for j in range(lands_ref.shape[0]):
        total = total + lands_ref[j, rows, :].astype(F32)
    return total


SMALL_ROWS = 2 * PACK_SLICE + 2 * 8


def _small_block(mine, lands, dgpre, name):
    def body(*refs):
        hm, ld, dg = refs[:DEPTH], refs[DEPTH:2 * DEPTH], refs[2 * DEPTH:3 * DEPTH]
        blk, land, sem = refs[3 * DEPTH:]
        for l in range(DEPTH):
            blk[PACK_SLICE * l:PACK_SLICE * (l + 1), :] = _own_then_slots(hm[l], ld[l])
            blk[2 * PACK_SLICE + 8 * l:2 * PACK_SLICE + 8 * (l + 1), :] = dg[l][...]
        cp = pltpu.make_async_copy(blk, _device_rows(land, SMALL_ROWS, *_mesh_pos()), sem)
        cp.start()
        cp.wait()

    vmem = pl.BlockSpec(memory_space=pltpu.VMEM)
    return pl.pallas_call(
        body, name=name,
        in_specs=[vmem] * (3 * DEPTH), out_specs=[vmem, pl.BlockSpec(memory_space=pl.ANY)],
        out_shape=[jax.ShapeDtypeStruct((SMALL_ROWS, 128), F32), jax.ShapeDtypeStruct((N_DEV * SMALL_ROWS, 128), F32)],
        scratch_shapes=[pltpu.SemaphoreType.DMA],
        compiler_params=_compiler_params(),
    )(*mine, *lands, *dgpre)


def _adamw_math(w, g, m, v):
    m = ADAM_B1 * m + (1.0 - ADAM_B1) * g
    v = ADAM_B2 * v + (1.0 - ADAM_B2) * (g * g)
    m_hat = m / (1.0 - ADAM_B1 ** ADAM_STEP)
    v_hat = v / (1.0 - ADAM_B2 ** ADAM_STEP)
    delta = -ADAM_LR * (m_hat / (jnp.sqrt(v_hat) + ADAM_EPS) + ADAM_WD * w)
    return delta, m, v


def _adamw_layer(layer, mine, lands, w, m, v, earlier, token, name, rows):
    _, mm, nn = w.shape

    def body(hm_ref, ld_ref, w_ref, m_ref, v_ref, _, *refs):
        g_ref, d_ref, nm_ref, nv_ref = refs[-4:]
        g = _own_then_slots(hm_ref, ld_ref)
        g_ref[...] = g
        d, nm, nv = _adamw_math(w_ref[...], g, m_ref[...], v_ref[...])
        d_ref[...] = d
        nm_ref[...] = nm
        nv_ref[...] = nv

    spec = pl.BlockSpec((None, rows, nn), lambda i: (layer, i, 0))
    carried = [] if earlier is None else list(earlier)
    return pl.pallas_call(
        body, name=name, grid=(mm // rows,),
        in_specs=([pl.BlockSpec((rows, nn), lambda i: (i, 0)) if mine.ndim == 2
                   else pl.BlockSpec((N_DEV, rows, nn), lambda i: (0, i, 0)),
                   pl.BlockSpec((lands.shape[0], rows, nn), lambda i: (0, i, 0)),
                   spec, spec, spec] + [pl.BlockSpec(memory_space=pl.ANY)] * (1 + len(carried))),
        out_specs=[spec] * 4,
        out_shape=[jax.ShapeDtypeStruct(w.shape, F32)] * 4,
        input_output_aliases={6 + t: t for t in range(len(carried))},
        compiler_params=_compiler_params(("arbitrary",)),
    )(mine, lands, w, m, v, token, *carried)


def _adamw_small(gathered, params):
    def body(all_ref, *refs):
        ins, outs, packs = refs[:15], refs[15:15 + 21], refs[15 + 21]
        loss_ref = outs[0]
        for dev in range(N_DEV):
            for l in range(DEPTH):
                packs[l, PACK_SLICE * dev:PACK_SLICE * (dev + 1), :] = (
                    all_ref[SMALL_ROWS * dev + PACK_SLICE * l:SMALL_ROWS * dev + PACK_SLICE * (l + 1), :])
        loss_ref[...] = packs[DEPTH - 1, ROW_LOSS:ROW_LOSS + 1, 0:1]

        def update(p, sel, g):
            w_ref, m_ref, v_ref = ins[p], ins[5 + p], ins[10 + p]
            d, nm, nv = _adamw_math(w_ref[sel], g, m_ref[sel], v_ref[sel])
            for t, val in enumerate((g, d, nm, nv)):
                outs[1 + 5 * t + p][sel] = val

        for l in range(DEPTH):
            gp = packs.at[l]
            row0 = 2 * PACK_SLICE + 8 * l
            dgpre = all_ref[row0:row0 + 8, :]
            for dev in range(1, N_DEV):
                dgpre = dgpre + all_ref[SMALL_ROWS * dev + row0:SMALL_ROWS * dev + row0 + 8, :]
            for grp in range(4):
                update(0, (l, grp), gp[ROW_PW + BLOCK * grp:ROW_PW + BLOCK * (grp + 1), :])
                update(1, (slice(l, l + 1), slice(128 * grp, 128 * (grp + 1))), gp[ROW_SC + grp:ROW_SC + grp + 1, :])
            update(2, (slice(l, l + 1), slice(None)), gp[ROW_SINK:ROW_SINK + 1, 0:N_HEADS])
            for r in range(D_MODEL // 128):
                sel = (slice(l, l + 1), slice(128 * r, 128 * (r + 1)))
                update(3, sel, dgpre[r:r + 1, :])
                update(4, sel, gp[ROW_NPOST + r:ROW_NPOST + r + 1, :])

    shapes = [jax.ShapeDtypeStruct(p.shape, F32) for p in params[:5]]
    return pl.pallas_call(
        body, name="adamw_small",
        out_shape=[jax.ShapeDtypeStruct((1, 1), F32)] + shapes * 4,
        scratch_shapes=[pltpu.VMEM((DEPTH, PACK_ROWS, 128), F32)],
        compiler_params=_compiler_params(),
    )(gathered, *params)


def kernel(x, w_in, pool_w, pool_scale, attn_sinks, w_out, norm_pre, norm_post, loss_target, m_w_in, m_pool_w, m_pool_scale, m_attn_sinks, m_w_out, m_norm_pre, m_norm_post, v_w_in, v_pool_w, v_pool_scale, v_attn_sinks, v_w_out, v_norm_pre, v_norm_post):
    x0 = x.reshape(SEQ, D_MODEL)
    target = loss_target.reshape(SEQ, D_MODEL)
    bias = jnp.asarray(_attn_bias())
    w_in_t, m_in_t, v_in_t = (jnp.swapaxes(t, 1, 2) for t in (w_in, m_w_in, v_w_in))

    (win0, wout0), later, lands = _allgather([(w_in_t, 0), (w_out, 0)], BF16, "gather_w0",
                                              later=[(w_in_t, 1), (w_out, 1)])
    sems, later, lands, token = _gather_start(later, lands, [SIBLING_AND_SAME_CORE, ALL_PEERS], "gather_w1_start")
    win_full, wout_full = [win0, None], [wout0, None]

    saved = []
    xl = x0
    for layer in range(DEPTH):
        u, pg, q, k, v, ag, z, a = _fwd_front(layer, xl, norm_pre, win_full[layer], token, attn_sinks,
                                             pool_w, pool_scale, bias)
        if layer == 0:
            land = _gather_wait(sems[0], later[0], lands[0], SIBLING_AND_SAME_CORE, z, "gather_w_in1_wait")
            fsems, land, token = _forward_start(land, IN_SHARD, "forward_w_in1_start")
        else:
            wout_full[layer] = _gather_wait(sems[1], later[1], lands[1], ALL_PEERS, z, "gather_w_out1_wait")
        x_next, y = _fwd_out(layer, z, xl, norm_post, wout_full[layer], token)
        if layer == 0:
            win_full[1] = _forward_wait(fsems, land, IN_SHARD, x_next, "forward_w_in1_wait")
        saved.append((xl, u, pg, q, k, v, ag, z, a, y))
        xl = x_next

    params_small = [pool_w, pool_scale, attn_sinks, norm_pre, norm_post,
                    m_pool_w, m_pool_scale, m_attn_sinks, m_norm_pre, m_norm_post,
                    v_pool_w, v_pool_scale, v_attn_sinks, v_norm_pre, v_norm_post]
    exchange, dgpre = [None] * DEPTH, [None] * DEPTH
    dx = None
    for layer in reversed(range(DEPTH)):
        xin, u, pg, q, k, v, ag, z, a, y = saved[layer]
        rest = (y, z, norm_post, wout_full[layer], attn_sinks, u, pg, q, k, v, ag, a, pool_w, pool_scale, bias)
        if layer == DEPTH - 1:
            dx, dproj, gw_out, pack = _bwd_back(layer, True, xl, target, *rest)
        else:
            dproj, gw_out, pack = _bwd_back(layer, False, dx, token, *rest)
        if layer > 0:
            dx, dgpre[layer], gw_in_t = _bwd_in(layer, "both", token, dproj, xin, norm_pre, dx, win_full[layer])
        else:
            (gw_in_t,) = _bwd_in(layer, "dw", token, dproj, xin, norm_pre)
        blocks = [gw_in_t.reshape(N_DEV, IN_SHARD, D_MODEL), gw_out.reshape(N_DEV, OUT_SHARD, D_MODEL),
                  pack.reshape(N_DEV, PACK_SLICE, 128)]
        direct = layer > 0
        srcs, mine = (blocks, None) if direct else _pair_reduce(blocks, f"pair_reduce{layer}")
        send_sem, recv_sem, srcs, lands, token = _exchange_start(srcs, direct, f"exchange_start{layer}")
        exchange[layer] = (send_sem, recv_sem, srcs, lands, mine)
        if layer == 0:
            dx, dgpre[layer] = _bwd_in(layer, "dx", token, dproj, xin, norm_pre, dx, win_full[layer])

    own, waited = [None] * DEPTH, [None] * DEPTH
    big_in, big_out, after = None, None, dx
    for layer in reversed(range(DEPTH)):
        send_sem, recv_sem, srcs, lands, mine = exchange[layer]
        srcs, waited[layer] = _exchange_wait(send_sem, recv_sem, srcs, lands, layer > 0, after, f"exchange_wait{layer}")
        own[layer] = srcs if mine is None else mine
        if layer > 0:
            big_in = _adamw_layer(layer, own[layer][0], waited[layer][0], w_in_t, m_in_t, v_in_t, big_in, token,
                                  f"adamw_in{layer}", 96)
            big_out = _adamw_layer(layer, own[layer][1], waited[layer][1], w_out, m_w_out, v_w_out, big_out, token,
                                   f"adamw_out{layer}", 128)
            after = big_out[0]
    block, land = _small_block([own[l][2] for l in range(DEPTH)], [waited[l][2] for l in range(DEPTH)],
                               dgpre, "small_block")
    sems, block, land, token = _gather_start([block], [land], [ALL_PEERS], "gather_small_start")
    big_in = _adamw_layer(0, own[0][0], waited[0][0], w_in_t, m_in_t, v_in_t, big_in, token, "adamw_in0", 96)
    big_out = _adamw_layer(0, own[0][1], waited[0][1], w_out, m_w_out, v_w_out, big_out, token, "adamw_out0", 128)
    gathered = _gather_wait(sems[0], block[0], land[0], ALL_PEERS, big_out[0], "gather_small_wait")
    small_out = _adamw_small(gathered, params_small)
    loss = small_out[0].reshape(())

    outs = [loss, dx.reshape(1, SEQ, D_MODEL)]
    for t in range(4):
        pw_, sc_, sk_, npre_, npost_ = small_out[1 + 5 * t:6 + 5 * t]
        outs += [jnp.swapaxes(big_in[t], 1, 2), pw_, sc_, sk_, big_out[t], npre_, npost_]
    return tuple(outs)
```

```python
import numpy as np
import jax
import jax.numpy as jnp
from jax import lax
from jax.experimental import pallas as pl
from jax.experimental.pallas import tpu as pltpu

F32 = jnp.float32
BF16 = jnp.bfloat16

N_DEV = 8
SEQ = 2048
D_MODEL = 1024
D_POOL = 512
D_ATTN = 512
D_KV = 128
D_IN = 2304
N_HEADS = 8
GQA = 4
HEAD_DIM = 64
BLOCK = 128
N_BLOCKS = SEQ // BLOCK
POOL_WINDOWS = (2, 4, 8, 16)
DEPTH = 2
EPS = 1e-6
NEG_INF = -1e30
SCALE = HEAD_DIM ** -0.5
IN_SHARD = D_IN // N_DEV
OUT_SHARD = D_MODEL // N_DEV

COL_U, COL_PG, COL_Q, COL_K, COL_V, COL_AG = 0, 512, 1024, 1536, 1664, 1792

ADAM_LR = 0.001
ADAM_B1 = 0.9
ADAM_B2 = 0.999
ADAM_EPS = 1e-08
ADAM_WD = 0.01
ADAM_STEP = 10

TOKEN_TILE = 512
VMEM_LIMIT = 56 * 1024 * 1024
MESH = pl.DeviceIdType.MESH

ROW_PW, ROW_SC, ROW_SINK, ROW_NPRE, ROW_NPOST, ROW_LOSS = 0, 512, 520, 528, 536, 544
PACK_ROWS = 576
PACK_SLICE = PACK_ROWS // N_DEV


def _nn(a, b):
    return jnp.dot(a, b, preferred_element_type=F32)


def _nt(a, b):
    return lax.dot_general(a, b, (((1,), (1,)), ((), ())), preferred_element_type=F32)


def _tn(a, b):
    return lax.dot_general(a, b, (((0,), (0,)), ((), ())), preferred_element_type=F32)


def _silu_parts(g):
    s = jax.nn.sigmoid(g)
    return g * s, s * (1.0 + g * (1.0 - s))


def _resident(shape):
    return pl.BlockSpec(shape, lambda *_: (0,) * len(shape), pipeline_mode=pl.Buffered(1))


def _compiler_params(sem=None):
    if sem is None:
        return pltpu.CompilerParams(vmem_limit_bytes=VMEM_LIMIT)
    return pltpu.CompilerParams(dimension_semantics=sem, vmem_limit_bytes=VMEM_LIMIT)


def _attn_bias():
    t = np.arange(BLOCK)[None, :]
    j = np.arange(BLOCK)[:, None]
    current = j <= t
    dist = np.where(current, t - j, t + BLOCK - j).astype(np.float32)
    out = np.zeros((2, 2, BLOCK, GQA * BLOCK), np.float32)
    for variant in range(2):
        valid = current | (variant == 1)
        for kv in range(2):
            for g in range(GQA):
                slope = np.float32(2.0 ** (-(kv * GQA + g + 1)))
                out[variant, kv, :, g * BLOCK:(g + 1) * BLOCK] = np.where(valid, -slope * dist, np.float32(NEG_INF))
    return out


def _replicate_head(kx, kv):
    rolled = pltpu.roll(kx, 64, 1)
    lane = lax.broadcasted_iota(jnp.int32, kx.shape, 1)
    dup = jnp.where(lane < 64, kx, rolled) if kv == 0 else jnp.where(lane < 64, rolled, kx)
    return jnp.concatenate([dup, dup], axis=1).astype(BF16)


def _stack_heads(qv):
    lane = lax.broadcasted_iota(jnp.int32, qv.shape, 1)
    zero = jnp.zeros_like(qv)
    return jnp.concatenate([jnp.where((lane >= 64 * g) & (lane < 64 * g + 64), qv, zero) for g in range(GQA)], axis=0)


def _unstack_heads(xs):
    lane = lax.broadcasted_iota(jnp.int32, (BLOCK, 256), 1)
    return jnp.where(lane < 64, xs[0:128], jnp.where(lane < 128, xs[128:256], jnp.where(lane < 192, xs[256:384], xs[384:512])))


def _fold_heads(r):
    h = r[:, 0:128] + r[:, 128:256]
    return h + pltpu.roll(h, 64, 1)


def _sink_row(sink_ref, layer, kv):
    lane = lax.broadcasted_iota(jnp.int32, (1, GQA * BLOCK), 1)
    s4 = [sink_ref[layer, kv * GQA + g] for g in range(GQA)]
    return jnp.where(lane < 128, s4[0], jnp.where(lane < 256, s4[1], jnp.where(lane < 384, s4[2], s4[3])))


def _band_is_current():
    j = lax.broadcasted_iota(jnp.int32, (BLOCK, GQA * BLOCK), 0)
    t = lax.broadcasted_iota(jnp.int32, (BLOCK, GQA * BLOCK), 1) & (BLOCK - 1)
    return j <= t


def _pack_band(full, current):
    return jnp.where(current, full[BLOCK:], full[:BLOCK])


def _unpack_band(packed, current):
    zero = jnp.zeros_like(packed)
    return jnp.concatenate([jnp.where(current, zero, packed), jnp.where(current, packed, zero)], axis=0)


def _probs_keys_major(k_rep, q_st, bias, sink, current):
    st = _pack_band(_nt(k_rep, q_st), current) * SCALE + bias
    m = jnp.maximum(jnp.max(st, axis=0, keepdims=True), sink)
    p = jnp.exp(st - m)
    esink = jnp.exp(sink - m)
    rl = 1.0 / (jnp.sum(p, axis=0, keepdims=True) + esink)
    return p * rl, esink * rl


WINDOW_HALO = 16


def _window_sum(ext, w, forward):
    s = ext
    sh = 1
    while sh < w:
        s = s + pltpu.roll(s, (ext.shape[0] - sh) if forward else sh, 0)
        sh *= 2
    return s


def _inv_count(n, w):
    t = n * BLOCK + lax.broadcasted_iota(jnp.int32, (BLOCK, 1), 0) + 1
    return 1.0 / jnp.minimum(t.astype(F32), float(w))


def _kv_ext(ref, n):
    r0 = pl.multiple_of(jnp.maximum(n - 1, 0) * BLOCK, BLOCK)
    r1 = pl.multiple_of(n * BLOCK, BLOCK)
    return jnp.concatenate([ref[pl.ds(r0, BLOCK), :], ref[pl.ds(r1, BLOCK), :]], axis=0)


def _rows_of(vec_ref, pack_ref, row0):
    for r in range(D_MODEL // 128):
        pack_ref[row0 + r:row0 + r + 1, :] = vec_ref[:, 128 * r:128 * (r + 1)]


FRONT_TILE = 2 * BLOCK


def _fwd_front(layer, x, norm_pre, w_in_t, token, sinks, pool_w, pool_scale, bias):
    tm = FRONT_TILE

    def body(sink_ref, x_ref, g_ref, w_ref, _, pw_ref, sc_ref, bias_ref,
             u_ref, pg_ref, q_ref, k_ref, v_ref, ag_ref, z_ref, a_ref, uprev, kprev, vprev):
        i = pl.program_id(0)

        @pl.when(i == 0)
        def _():
            uprev[...] = jnp.zeros_like(uprev)
            kprev[...] = jnp.zeros_like(kprev)
            vprev[...] = jnp.zeros_like(vprev)

        xv = x_ref[...]
        r = lax.rsqrt(jnp.mean(xv * xv, axis=-1, keepdims=True) + EPS)
        h = (xv * r * g_ref[layer:layer + 1, :]).astype(BF16)
        u_ref[...] = _nt(h, w_ref[COL_U:COL_PG, :])
        pg_ref[...] = _nt(h, w_ref[COL_PG:COL_Q, :])
        for sb in range(tm // BLOCK):
            n = (tm // BLOCK) * i + sb
            rows = slice(BLOCK * sb, BLOCK * (sb + 1))
            before = slice(BLOCK * (sb - 1), BLOCK * sb)
            uv = u_ref[rows, :]
            halo = (uprev[BLOCK - WINDOW_HALO:, :] if sb == 0
                    else u_ref[BLOCK * sb - WINDOW_HALO:BLOCK * sb, :])
            ext = jnp.concatenate([halo, uv], axis=0)
            for g, w in enumerate(POOL_WINDOWS):
                cs = slice(BLOCK * g, BLOCK * (g + 1))
                win = _window_sum(ext[:, cs], w, forward=False)[WINDOW_HALO:]
                pooled = win * _inv_count(n, w) - uv[:, cs]
                mixed = _nn(pooled.astype(BF16), pw_ref[g].astype(BF16))
                gate, _ = _silu_parts(pg_ref[rows, cs])
                z_ref[rows, cs] = (mixed * sc_ref[layer:layer + 1, cs] * gate).astype(BF16)

        q_ref[...] = _nt(h, w_ref[COL_Q:COL_K, :]).astype(BF16)
        k_ref[...] = _nt(h, w_ref[COL_K:COL_V, :])
        v_ref[...] = _nt(h, w_ref[COL_V:COL_AG, :])
        ag_ref[...] = _nt(h, w_ref[COL_AG:D_IN, :])

        current = _band_is_current()
        for sb in range(tm // BLOCK):
            n = (tm // BLOCK) * i + sb
            rows = slice(BLOCK * sb, BLOCK * (sb + 1))
            before = slice(BLOCK * (sb - 1), BLOCK * sb)
            kx = jnp.concatenate([kprev[...] if sb == 0 else k_ref[before, :], k_ref[rows, :]], axis=0)
            vx = jnp.concatenate([vprev[...] if sb == 0 else v_ref[before, :], v_ref[rows, :]], axis=0)
            variant = jnp.minimum(n, 1) if sb == 0 else 1
            for kv in range(2):
                cs = slice(256 * kv, 256 * (kv + 1))
                p, _ = _probs_keys_major(_replicate_head(kx, kv), _stack_heads(q_ref[rows, cs]),
                                         bias_ref[variant, kv], _sink_row(sink_ref, layer, kv), current)
                o = _unstack_heads(_tn(_unpack_band(p.astype(BF16), current), _replicate_head(vx, kv)))
                a_ref[rows, cs] = o
                gate, _ = _silu_parts(ag_ref[rows, cs])
                z_ref[rows, D_POOL + 256 * kv:D_POOL + 256 * (kv + 1)] = (o * gate).astype(BF16)

        tail = slice(tm - BLOCK, tm)
        uprev[...] = u_ref[tail, :]
        kprev[...] = k_ref[tail, :]
        vprev[...] = v_ref[tail, :]

    row = lambda c: pl.BlockSpec((tm, c), lambda i: (i, 0))
    const = lambda shape: pl.BlockSpec(shape, lambda i: (0,) * len(shape))
    return pl.pallas_call(
        body, name=f"fwd_front{layer}", grid=(SEQ // tm,),
        in_specs=[pl.BlockSpec(memory_space=pltpu.SMEM), row(D_MODEL), const((DEPTH, D_MODEL)),
                  _resident((D_IN, D_MODEL)), const((8, 128)),
                  pl.BlockSpec((None, 4, BLOCK, BLOCK), lambda i: (layer, 0, 0, 0)), const((DEPTH, D_POOL)),
                  _resident((2, 2, BLOCK, GQA * BLOCK))],
        out_specs=[row(D_POOL), row(D_POOL), row(D_ATTN), row(D_KV), row(D_KV), row(D_ATTN), row(D_MODEL),
                   row(D_ATTN)],
        out_shape=[jax.ShapeDtypeStruct((SEQ, D_POOL), F32), jax.ShapeDtypeStruct((SEQ, D_POOL), F32),
                   jax.ShapeDtypeStruct((SEQ, D_ATTN), BF16), jax.ShapeDtypeStruct((SEQ, D_KV), F32),
                   jax.ShapeDtypeStruct((SEQ, D_KV), F32), jax.ShapeDtypeStruct((SEQ, D_ATTN), F32),
                   jax.ShapeDtypeStruct((SEQ, D_MODEL), BF16), jax.ShapeDtypeStruct((SEQ, D_ATTN), F32)],
        scratch_shapes=[pltpu.VMEM((BLOCK, D_POOL), F32), pltpu.VMEM((BLOCK, D_KV), F32),
                        pltpu.VMEM((BLOCK, D_KV), F32)],
        compiler_params=_compiler_params(("arbitrary",)),
    )(sinks, x, norm_pre, w_in_t, token, pool_w, pool_scale, bias)


def _fwd_out(layer, z, x, norm_post, w_out, token):
    tm = TOKEN_TILE

    def body(z_ref, x_ref, g_ref, w_ref, _, xn_ref, y_ref):
        y = _nn(z_ref[...], w_ref[...])
        y_ref[...] = y
        r = lax.rsqrt(jnp.mean(y * y, axis=-1, keepdims=True) + EPS)
        xn_ref[...] = x_ref[...] + y * r * g_ref[layer:layer + 1, :]

    row = lambda c: pl.BlockSpec((tm, c), lambda i: (i, 0))
    return pl.pallas_call(
        body, name=f"fwd_out{layer}", grid=(SEQ // tm,),
        in_specs=[row(D_MODEL), row(D_MODEL), pl.BlockSpec((DEPTH, D_MODEL), lambda i: (0, 0)),
                  _resident((D_MODEL, D_MODEL)), pl.BlockSpec((8, 128), lambda i: (0, 0))],
        out_specs=[row(D_MODEL), row(D_MODEL)],
        out_shape=[jax.ShapeDtypeStruct((SEQ, D_MODEL), F32), jax.ShapeDtypeStruct((SEQ, D_MODEL), F32)],
        compiler_params=_compiler_params(("arbitrary",)),
    )(z, x, norm_post, w_out, token)


BACK_TILE = 2 * BLOCK


def _bwd_back(layer, top, dxo_or_xf, target_or_token, y, z, norm_post, w_out, sinks, u, pg, q, k, v, ag, a,
              pool_w, pool_scale, bias):
    tm = BACK_TILE
    steps = SEQ // tm
    last = steps - 1
    per = tm // BLOCK

    def body(*refs):
        refs = list(refs)
        sink_ref, first, second = refs[:3]
        (y_ref, z_ref, g_ref, w_ref, u_ref, up_ref, pg_ref, q_ref, k_ref, v_ref, ag_ref, a_ref, pw_ref, sc_ref,
         bias_ref) = refs[3:18]
        del refs[:18]
        dxo_ref = refs.pop(0) if top else None
        dp_ref, dw_ref, pack_ref, acc, dg, lacc, dzs, ck, cv, ce = refs
        i = pl.program_id(0)
        blk = last - i

        @pl.when(i == 0)
        def _():
            acc[...] = jnp.zeros_like(acc)
            dg[...] = jnp.zeros_like(dg)
            lacc[...] = jnp.zeros_like(lacc)
            pack_ref[...] = jnp.zeros_like(pack_ref)
            ck[...] = jnp.zeros_like(ck)
            cv[...] = jnp.zeros_like(cv)
            ce[...] = jnp.zeros_like(ce)

        if top:
            d = first[...] - second[...]
            dxo_v = d * (1.0 / D_MODEL)
            dxo_ref[...] = dxo_v
            part = jnp.sum(d * d, axis=-1, keepdims=True) * (1.0 / D_MODEL)
            lacc[...] += 0.5 * jnp.sum(part, axis=0, keepdims=True)
        else:
            dxo_v = first[...]
        yv = y_ref[...]
        r = lax.rsqrt(jnp.mean(yv * yv, axis=-1, keepdims=True) + EPS)
        yn = yv * r
        dg[...] += jnp.sum(dxo_v * yn, axis=0, keepdims=True)
        dyn = dxo_v * g_ref[layer:layer + 1, :]
        dy = (r * (dyn - yn * jnp.mean(dyn * yn, axis=-1, keepdims=True))).astype(BF16)
        dzs[...] = _nt(dy, w_ref[...])
        acc[...] += _tn(z_ref[...], dy)

        lane = lax.broadcasted_iota(jnp.int32, (1, 128), 1)
        lane2 = lax.broadcasted_iota(jnp.int32, (256, 128), 1)
        current = _band_is_current()
        for sb in reversed(range(per)):
            n = per * blk + sb
            rows = slice(BLOCK * sb, BLOCK * (sb + 1))

            uv = u_ref[rows, :]
            if sb == 0:
                halo = up_ref[BLOCK - WINDOW_HALO:, :] * (n > 0).astype(F32)
            else:
                halo = u_ref[BLOCK * sb - WINDOW_HALO:BLOCK * sb, :]
            ext = jnp.concatenate([halo, uv], axis=0)
            for g, w in enumerate(POOL_WINDOWS):
                cs = slice(BLOCK * g, BLOCK * (g + 1))
                inv = _inv_count(n, w)
                win = _window_sum(ext[:, cs], w, forward=False)[WINDOW_HALO:]
                pooled = win * inv - uv[:, cs]
                pw_g = pw_ref[g].astype(BF16)
                mixed = _nn(pooled.astype(BF16), pw_g)
                gate, dgate = _silu_parts(pg_ref[rows, cs])
                dzp = dzs[rows, cs]
                sc = sc_ref[layer:layer + 1, cs]
                dpm = dzp * gate
                dp_ref[rows, COL_PG + BLOCK * g:COL_PG + BLOCK * (g + 1)] = (dzp * (mixed * sc) * dgate).astype(BF16)
                pack_ref[ROW_SC + g:ROW_SC + g + 1, :] += jnp.sum(dpm * mixed, axis=0, keepdims=True)
                dmixed = (dpm * sc).astype(BF16)
                pack_ref[ROW_PW + BLOCK * g:ROW_PW + BLOCK * (g + 1), :] += _tn(pooled.astype(BF16), dmixed)
                dpooled = _nt(dmixed, pw_g)
                e = dpooled * inv
                lead = _window_sum(jnp.concatenate([e, ce[:WINDOW_HALO, cs]], axis=0), w, forward=True)[:BLOCK]
                dp_ref[rows, COL_U + BLOCK * g:COL_U + BLOCK * (g + 1)] = (lead - dpooled).astype(BF16)
                ce[:, cs] = e

            kx = _kv_ext(k_ref, n)
            vx = _kv_ext(v_ref, n)
            variant = jnp.minimum(n, 1) if sb == 0 else 1
            dsink_row = jnp.zeros((1, 128), F32)
            tks, tvs = [], []
            for kv in range(2):
                cs = slice(256 * kv, 256 * (kv + 1))
                k_rep = _replicate_head(kx, kv)
                v_rep = _replicate_head(vx, kv)
                q_st = _stack_heads(q_ref[rows, cs])
                gate, dgate = _silu_parts(ag_ref[rows, cs])
                dza = dzs[rows, D_POOL + 256 * kv:D_POOL + 256 * (kv + 1)]
                dp_ref[rows, COL_AG + 256 * kv:COL_AG + 256 * (kv + 1)] = (dza * a_ref[rows, cs] * dgate).astype(BF16)
                da_st = _stack_heads((dza * gate).astype(BF16))
                p, psink = _probs_keys_major(k_rep, q_st, bias_ref[variant, kv], _sink_row(sink_ref, layer, kv),
                                             current)
                dpt = _pack_band(_nt(v_rep, da_st), current)
                delta = jnp.sum(p * dpt, axis=0, keepdims=True)
                dst = _unpack_band((p * (dpt - delta) * SCALE).astype(BF16), current)
                sink_terms = psink * delta
                for g in range(GQA):
                    dsink = -jnp.sum(sink_terms[:, BLOCK * g:BLOCK * (g + 1)], axis=1, keepdims=True)
                    dsink_row = dsink_row + jnp.where(lane == kv * GQA + g, dsink, 0.0)
                dp_ref[rows, COL_Q + 256 * kv:COL_Q + 256 * (kv + 1)] = _unstack_heads(_tn(dst, k_rep)).astype(BF16)
                tks.append(_fold_heads(_nn(dst, q_st)))
                tvs.append(_fold_heads(_nn(_unpack_band(p.astype(BF16), current), da_st)))
            pack_ref[ROW_SINK:ROW_SINK + 1, :] += dsink_row
            dkx = jnp.where(lane2 < 64, tks[0], tks[1])
            dvx = jnp.where(lane2 < 64, tvs[0], tvs[1])
            dp_ref[rows, COL_K:COL_V] = (ck[...] + dkx[BLOCK:]).astype(BF16)
            dp_ref[rows, COL_V:COL_AG] = (cv[...] + dvx[BLOCK:]).astype(BF16)
            ck[...] = dkx[:BLOCK]
            cv[...] = dvx[:BLOCK]

        @pl.when(i == steps - 1)
        def _():
            dw_ref[...] = acc[...].astype(BF16)
            _rows_of(dg, pack_ref, ROW_NPOST)
            pack_ref[ROW_LOSS:ROW_LOSS + 1, :] = jnp.where(lane == 0, lacc[...], 0.0)

    row = lambda c: pl.BlockSpec((tm, c), lambda i: (last - i, 0))
    const = lambda shape: pl.BlockSpec(shape, lambda i: (0,) * len(shape))
    act = jax.ShapeDtypeStruct((SEQ, D_MODEL), F32)
    return pl.pallas_call(
        body, name=f"bwd_back{layer}", grid=(steps,),
        in_specs=[pl.BlockSpec(memory_space=pltpu.SMEM), row(D_MODEL), row(D_MODEL) if top else const((8, 128)),
                  row(D_MODEL), row(D_MODEL), const((DEPTH, D_MODEL)), _resident((D_MODEL, D_MODEL)),
                  row(D_POOL), pl.BlockSpec((BLOCK, D_POOL), lambda i: (jnp.maximum(per * (last - i) - 1, 0), 0)),
                  row(D_POOL), row(D_ATTN), _resident((SEQ, D_KV)), _resident((SEQ, D_KV)), row(D_ATTN), row(D_ATTN),
                  pl.BlockSpec((None, 4, BLOCK, BLOCK), lambda i: (layer, 0, 0, 0)), const((DEPTH, D_POOL)),
                  _resident((2, 2, BLOCK, GQA * BLOCK))],
        out_specs=([row(D_MODEL)] * (1 if top else 0)
                   + [row(D_IN), const((D_MODEL, D_MODEL)), const((PACK_ROWS, 128))]),
        out_shape=([act] * (1 if top else 0)
                   + [jax.ShapeDtypeStruct((SEQ, D_IN), BF16), jax.ShapeDtypeStruct((D_MODEL, D_MODEL), BF16),
                      jax.ShapeDtypeStruct((PACK_ROWS, 128), F32)]),
        scratch_shapes=[pltpu.VMEM((D_MODEL, D_MODEL), F32), pltpu.VMEM((1, D_MODEL), F32), pltpu.VMEM((1, 1), F32),
                        pltpu.VMEM((tm, D_MODEL), F32), pltpu.VMEM((BLOCK, D_KV), F32), pltpu.VMEM((BLOCK, D_KV), F32),
                        pltpu.VMEM((BLOCK, D_POOL), F32)],
        compiler_params=_compiler_params(("arbitrary",)),
    )(sinks, dxo_or_xf, target_or_token, y, z, norm_post, w_out, u, u, pg, q, k, v, ag, a, pool_w, pool_scale, bias)


def _bwd_in(layer, part, token, dproj, x, norm_pre, dxo=None, w_in_t=None):
    want_dw, want_dx = part in ("both", "dw"), part in ("both", "dx")
    tm = TOKEN_TILE
    steps = SEQ // tm
    cw = 256

    def body(*refs):
        refs = list(refs)
        dp_ref, x_ref, g_ref = refs[1:4]
        del refs[:4]
        if want_dx:
            dxo_ref, w_ref, dx_ref, dgo_ref = refs[:4]
            del refs[:4]
            dg = refs.pop()
        if want_dw:
            dw_ref, acc = refs
        i = pl.program_id(0)

        @pl.when(i == 0)
        def _():
            if want_dw:
                acc[...] = jnp.zeros_like(acc)
            if want_dx:
                dg[...] = jnp.zeros_like(dg)

        xv = x_ref[...]
        gv = g_ref[layer:layer + 1, :]
        r = lax.rsqrt(jnp.mean(xv * xv, axis=-1, keepdims=True) + EPS)
        xn = xv * r
        if want_dw:
            hb = (xn * gv).astype(BF16)
            for c in range(0, D_IN, cw):
                acc[c:c + cw, :] += _tn(dp_ref[:, c:c + cw], hb)
        if want_dx:
            dh = _nn(dp_ref[...], w_ref[...])
            dg[...] += jnp.sum(dh * xn, axis=0, keepdims=True)
            dhn = dh * gv
            dx_ref[...] = dxo_ref[...] + r * (dhn - xn * jnp.mean(dhn * xn, axis=-1, keepdims=True))

        @pl.when(i == steps - 1)
        def _():
            if want_dw:
                dw_ref[...] = acc[...].astype(BF16)
            if want_dx:
                _rows_of(dg, dgo_ref, 0)

    row = lambda c: pl.BlockSpec((tm, c), lambda i: (i, 0))
    const = lambda shape: pl.BlockSpec(shape, lambda i: (0,) * len(shape))
    in_specs = [const((8, 128)), row(D_IN), row(D_MODEL), const((DEPTH, D_MODEL))]
    operands = [token, dproj, x, norm_pre]
    out_specs, out_shape, scratch = [], [], []
    if want_dx:
        in_specs += [row(D_MODEL), _resident((D_IN, D_MODEL))]
        operands += [dxo, w_in_t]
        out_specs += [row(D_MODEL), const((8, 128))]
        out_shape += [jax.ShapeDtypeStruct((SEQ, D_MODEL), F32), jax.ShapeDtypeStruct((8, 128), F32)]
    if want_dw:
        out_specs.append(const((D_IN, D_MODEL)))
        out_shape.append(jax.ShapeDtypeStruct((D_IN, D_MODEL), BF16))
        scratch.append(pltpu.VMEM((D_IN, D_MODEL), F32))
    if want_dx:
        scratch.append(pltpu.VMEM((1, D_MODEL), F32))
    return pl.pallas_call(
        body, name=f"bwd_in_{part}{layer}", grid=(steps,),
        in_specs=in_specs, out_specs=out_specs, out_shape=out_shape, scratch_shapes=scratch,
        compiler_params=_compiler_params(("arbitrary",)),
    )(*operands)


def _mesh_pos():
    return lax.axis_index("x"), lax.axis_index("y"), lax.axis_index("c")


def _device_rows(ref, m, px, py, pc):
    return ref.at[pl.ds(pl.multiple_of((4 * px + 2 * py + pc) * m, 16 if m % 16 == 0 else 8), m), :]


def _allgather(srcs, out_dtype, name, later=()):
    na, nb = len(srcs), len(later)
    every = list(srcs) + list(later)
    shapes = [(a.shape[-2], a.shape[-1]) for a, _ in every]

    def body(*refs):
        xs, refs = refs[:na + nb], refs[na + nb:]
        outs, cast, land, refs = refs[:na], refs[na:na + nb], refs[na + nb:na + 2 * nb], refs[na + 2 * nb:]
        stage, (send_sems, recv_sems, local_sems) = refs[:na], refs[na:]
        x, y, c = _mesh_pos()
        me, sibling = (x, y, c), (x, y, 1 - c)
        near = [(1 - x, y), (x, 1 - y)]
        far = (1 - x, 1 - y)
        relay_from, relay_to = (x ^ (1 - c), y ^ c), (x ^ c, y ^ (1 - c))
        k_from, k_to = 1 + c, 2 - c

        def slot(a, px, py, pc):
            return _device_rows(outs[a], shapes[a][0], px, py, pc)

        def copy(a, k, block, to, src=None):
            return pltpu.make_async_remote_copy(
                src_ref=slot(a, *block) if src is None else src, dst_ref=slot(a, *block),
                send_sem=send_sems.at[a, k], recv_sem=recv_sems.at[a, k], device_id=to, device_id_type=MESH)

        def cast_block(i):
            layer = every[i][1]
            return (xs[i][...] if layer is None else xs[i][layer]).astype(out_dtype)

        for a in range(na):
            stage[a][...] = cast_block(a)
        mine = [pltpu.make_async_copy(stage[a], slot(a, *me), local_sems.at[a]) for a in range(na)]
        for cp in mine:
            cp.start()
        sent = []
        for a in range(na):
            sent.append(copy(a, 0, me, sibling, src=stage[a]))
            sent += [copy(a, 1 + j, me, (*chip, c), src=stage[a]) for j, chip in enumerate(near)]
        for cp in sent:
            cp.start()
        for b in range(nb):
            cast[b][...] = cast_block(na + b)
            cp = pltpu.make_async_copy(cast[b], _device_rows(land[b], shapes[na + b][0], *me), local_sems.at[na + b])
            cp.start()
            mine.append(cp)
        for a in range(na):
            copy(a, k_from, (*relay_from, c), me).wait_recv()
            sent += [copy(a, 3, (*relay_from, c), (*relay_to, c)), copy(a, 3 + k_from, (*relay_from, c), sibling)]
            sent[-2].start()
            sent[-1].start()
        for a in range(na):
            copy(a, k_to, (*relay_to, c), me).wait_recv()
            sent.append(copy(a, 3 + k_to, (*relay_to, c), sibling))
            sent[-1].start()
        for a in range(na):
            copy(a, 3, (*far, c), me).wait_recv()
            sent.append(copy(a, 6, (*far, c), sibling))
            sent[-1].start()
        for a in range(na):
            copy(a, 0, sibling, me).wait_recv()
            for j, chip in enumerate(near + [far]):
                copy(a, 4 + j, (*chip, 1 - c), me).wait_recv()
        for cp in sent:
            cp.wait_send()
        for cp in mine:
            cp.wait()

    vmem = pl.BlockSpec(memory_space=pltpu.VMEM)
    hbm = pl.BlockSpec(memory_space=pl.ANY)
    gathered = [jax.ShapeDtypeStruct((N_DEV * m, n), out_dtype) for m, n in shapes]
    out = pl.pallas_call(
        body, name=name,
        in_specs=[vmem] * (na + nb),
        out_specs=[hbm] * na + [vmem] * nb + [hbm] * nb,
        out_shape=gathered[:na] + [jax.ShapeDtypeStruct(s, out_dtype) for s in shapes[na:]] + gathered[na:],
        scratch_shapes=([pltpu.VMEM(s, out_dtype) for s in shapes[:na]]
                        + [pltpu.SemaphoreType.DMA((na, 7)), pltpu.SemaphoreType.DMA((na, 7)),
                           pltpu.SemaphoreType.DMA((na + nb,))]),
        compiler_params=_compiler_params(),
    )(*[a for a, _ in every])
    return out[:na], out[na:na + nb], out[na + nb:]


ALL_PEERS = tuple(range(1, N_DEV))
SIBLING_AND_SAME_CORE = (1, 2, 4, 6)


def _related(k, x, y, c):
    return x ^ ((k >> 2) & 1), y ^ ((k >> 1) & 1), c ^ (k & 1)


def _gather_start(blocks, lands, relations, name):
    na = len(blocks)

    def body(*refs):
        src, land, sems, token = refs[:na], refs[na:2 * na], refs[2 * na:4 * na], refs[-1]
        x, y, c = _mesh_pos()
        for a in range(na):
            for k in relations[a]:
                pltpu.make_async_remote_copy(
                    src_ref=src[a], dst_ref=_device_rows(land[a], blocks[a].shape[0], x, y, c),
                    send_sem=sems[2 * a].at[k - 1], recv_sem=sems[2 * a + 1].at[k - 1],
                    device_id=_related(k, x, y, c), device_id_type=MESH).start()
        token[...] = jnp.zeros_like(token)

    bufs = [pltpu.HBM(t.shape, t.dtype) for t in list(blocks) + list(lands)]
    out = pl.pallas_call(
        body, name=name,
        out_shape=(*([pltpu.SemaphoreType.DMA((N_DEV - 1,))] * (2 * na)), *bufs, jax.ShapeDtypeStruct((8, 128), F32)),
        in_specs=[_HBM] * (2 * na),
        out_specs=(*([_SEM] * (2 * na)), *([_HBM] * (2 * na)), pl.BlockSpec(memory_space=pltpu.VMEM)),
        input_output_aliases={i: 2 * na + i for i in range(2 * na)},
        compiler_params=pltpu.CompilerParams(has_side_effects=_EFFECT),
    )(*[pltpu.with_memory_space_constraint(t, pltpu.HBM) for t in list(blocks) + list(lands)])
    sems = [(out[2 * a], out[2 * a + 1]) for a in range(na)]
    return sems, out[2 * na:3 * na], out[3 * na:4 * na], out[-1]


def _gather_wait(sems, block, land, relations, after, name):
    def body(src, land_ref, send_sem, recv_sem, after_ref, src_out, land_out):
        x, y, c = _mesh_pos()
        for k in relations:
            peer = _related(k, x, y, c)
            cp = pltpu.make_async_remote_copy(
                src_ref=src, dst_ref=_device_rows(land_ref, block.shape[0], *peer),
                send_sem=send_sem.at[k - 1], recv_sem=recv_sem.at[k - 1], device_id=peer, device_id_type=MESH)
            cp.wait_send()
            cp.wait_recv()

    out = pl.pallas_call(
        body, name=name,
        out_shape=(pltpu.HBM(block.shape, block.dtype), pltpu.HBM(land.shape, land.dtype)),
        in_specs=[_HBM, _HBM, _SEM, _SEM, pl.BlockSpec(memory_space=pl.ANY)],
        out_specs=[_HBM, _HBM],
        input_output_aliases={0: 0, 1: 1},
        compiler_params=pltpu.CompilerParams(has_side_effects=_EFFECT),
    )(block, land, sems[0], sems[1], after)
    return out[1]


def _forward_plan(land_ref, m):
    x, y, c = _mesh_pos()
    return [_device_rows(land_ref, m, qx, qy, c) for qx, qy in ((1 - x, y), (x, 1 - y), (1 - x, 1 - y))], (x, y, 1 - c)


def _forward_start(land, m, name):
    def body(land_ref, send_sem, recv_sem, land_out, token):
        rows, sibling = _forward_plan(land_ref, m)
        for j, r in enumerate(rows):
            pltpu.make_async_remote_copy(src_ref=r, dst_ref=r, send_sem=send_sem.at[j], recv_sem=recv_sem.at[j],
                                         device_id=sibling, device_id_type=MESH).start()
        token[...] = jnp.zeros_like(token)

    out = pl.pallas_call(
        body, name=name,
        out_shape=(pltpu.SemaphoreType.DMA((3,)), pltpu.SemaphoreType.DMA((3,)), pltpu.HBM(land.shape, land.dtype),
                   jax.ShapeDtypeStruct((8, 128), F32)),
        in_specs=[_HBM],
        out_specs=(_SEM, _SEM, _HBM, pl.BlockSpec(memory_space=pltpu.VMEM)),
        input_output_aliases={0: 2},
        compiler_params=pltpu.CompilerParams(has_side_effects=_EFFECT),
    )(pltpu.with_memory_space_constraint(land, pltpu.HBM))
    return (out[0], out[1]), out[2], out[3]


def _forward_wait(sems, land, m, after, name):
    def body(land_ref, send_sem, recv_sem, after_ref, land_out):
        x, y, c = _mesh_pos()
        mine, sibling = _forward_plan(land_ref, m)
        theirs = [_device_rows(land_ref, m, qx, qy, 1 - c) for qx, qy in ((1 - x, y), (x, 1 - y), (1 - x, 1 - y))]
        for j in range(3):
            cp = pltpu.make_async_remote_copy(src_ref=mine[j], dst_ref=theirs[j], send_sem=send_sem.at[j],
                                              recv_sem=recv_sem.at[j], device_id=sibling, device_id_type=MESH)
            cp.wait_send()
            cp.wait_recv()

    return pl.pallas_call(
        body, name=name,
        out_shape=pltpu.HBM(land.shape, land.dtype),
        in_specs=[_HBM, _SEM, _SEM, pl.BlockSpec(memory_space=pl.ANY)],
        out_specs=_HBM,
        input_output_aliases={0: 0},
        compiler_params=pltpu.CompilerParams(has_side_effects=_EFFECT),
    )(land, sems[0], sems[1], after)


def _row_step(m):
    return next(s for s in (32, 24, 16, 8) if m % s == 0)


def _pair_reduce(arrs, name):
    na = len(arrs)

    def body(*refs):
        gs, hs, hm = refs[:na], refs[na:2 * na], refs[2 * na:3 * na]
        own, ra = refs[3 * na:4 * na], refs[4 * na:5 * na]
        d2d_send, d2d_recv, local_sems = refs[5 * na:]
        x, y, c = _mesh_pos()
        sibling = (x, y, 1 - c)
        loads, sends = [], []
        for a in range(na):
            for q in range(4):
                cp = pltpu.make_async_copy(gs[a].at[2 * q + c], own[a].at[q], local_sems.at[a, q])
                cp.start()
                loads.append(cp)
                cp = pltpu.make_async_remote_copy(
                    src_ref=gs[a].at[2 * q + (1 - c)], dst_ref=ra[a].at[q], send_sem=d2d_send.at[a, q],
                    recv_sem=d2d_recv.at[a, q], device_id=sibling, device_id_type=MESH)
                cp.start()
                sends.append(cp)
        for cp in loads:
            cp.wait()
        for cp in sends:
            cp.wait_recv()
        others = [2 * (1 - x) + y, 2 * x + (1 - y), 2 * (1 - x) + (1 - y)]
        for a in range(na):
            m = arrs[a].shape[1]
            step = _row_step(m)

            def add(i, carry, a=a, step=step):
                rs = pl.ds(pl.multiple_of(i * step, step), step)
                for j, q in enumerate(others):
                    hs[a][j, rs, :] = (own[a][q, rs, :].astype(F32) + ra[a][q, rs, :].astype(F32)).astype(hs[a].dtype)
                q = 2 * x + y
                hm[a][rs, :] = own[a][q, rs, :].astype(F32) + ra[a][q, rs, :].astype(F32)
                return carry

            lax.fori_loop(0, m // step, add, 0)
        for cp in sends:
            cp.wait_send()

    vmem = pl.BlockSpec(memory_space=pltpu.VMEM)
    scratch = [pltpu.VMEM((4,) + t.shape[1:], t.dtype) for t in arrs] * 2
    scratch += [pltpu.SemaphoreType.DMA((na, 4)), pltpu.SemaphoreType.DMA((na, 4)), pltpu.SemaphoreType.DMA((na, 4))]
    out = pl.pallas_call(
        body, name=name,
        in_specs=[pl.BlockSpec(memory_space=pl.ANY)] * na, out_specs=[vmem] * (2 * na),
        out_shape=([jax.ShapeDtypeStruct((3,) + t.shape[1:], t.dtype) for t in arrs]
                   + [jax.ShapeDtypeStruct(t.shape[1:], F32) for t in arrs]),
        scratch_shapes=scratch,
        compiler_params=_compiler_params(),
    )(*arrs)
    return out[:na], out[na:]


_HBM = pl.BlockSpec(memory_space=pltpu.HBM)
_SEM = pl.BlockSpec(memory_space=pltpu.SEMAPHORE)
_EFFECT = pltpu.SideEffectType.DATAFLOW_SIDE_EFFECTING


def _exchange_plan(direct):
    x, y, c = _mesh_pos()
    if not direct:
        return [(j, j, (qx, qy, c)) for j, (qx, qy) in enumerate([(1 - x, y), (x, 1 - y), (1 - x, 1 - y)])]
    plan = []
    for k in range(1, N_DEV):
        px, py, pc = x ^ ((k >> 2) & 1), y ^ ((k >> 1) & 1), c ^ (k & 1)
        plan.append((4 * px + 2 * py + pc, k - 1, (px, py, pc)))
    return plan


def _exchange_start(srcs, direct, name):
    na = len(srcs)
    slots = N_DEV - 1 if direct else 3

    def body(*refs):
        src, land = refs[:na], refs[na:2 * na]
        send_sem, recv_sem = refs[2 * na], refs[2 * na + 1]
        token = refs[-1]
        for block, slot, peer in _exchange_plan(direct):
            for a in range(na):
                pltpu.make_async_remote_copy(
                    src_ref=src[a].at[block], dst_ref=land[a].at[slot], send_sem=send_sem.at[slots * a + slot],
                    recv_sem=recv_sem.at[slots * a + slot], device_id=peer, device_id_type=MESH).start()
        token[...] = jnp.zeros_like(token)

    zones = [jax.ShapeDtypeStruct((slots,) + t.shape[1:], t.dtype) for t in srcs]
    bufs = [pltpu.HBM(t.shape, t.dtype) for t in list(srcs) + zones]
    out = pl.pallas_call(
        body, name=name,
        out_shape=(pltpu.SemaphoreType.DMA((slots * na,)), pltpu.SemaphoreType.DMA((slots * na,)), *bufs,
                   jax.ShapeDtypeStruct((8, 128), F32)),
        in_specs=[_HBM] * (2 * na),
        out_specs=(_SEM, _SEM, *([_HBM] * (2 * na)), pl.BlockSpec(memory_space=pltpu.VMEM)),
        input_output_aliases={i: 2 + i for i in range(2 * na)},
        compiler_params=pltpu.CompilerParams(has_side_effects=_EFFECT),
    )(*[pltpu.with_memory_space_constraint(t, pltpu.HBM) for t in srcs],
      *[pltpu.with_memory_space_constraint(lax.empty(t.shape, t.dtype), pltpu.HBM) for t in zones])
    return out[0], out[1], out[2:2 + na], out[2 + na:2 + 2 * na], out[-1]


def _exchange_wait(send_sem, recv_sem, srcs, lands, direct, after, name):
    na = len(srcs)
    slots = N_DEV - 1 if direct else 3

    def body(*refs):
        src, land = refs[:na], refs[na:2 * na]
        send_sem_ref, recv_sem_ref = refs[2 * na], refs[2 * na + 1]
        for block, slot, peer in _exchange_plan(direct):
            for a in range(na):
                cp = pltpu.make_async_remote_copy(
                    src_ref=src[a].at[block], dst_ref=land[a].at[slot], send_sem=send_sem_ref.at[slots * a + slot],
                    recv_sem=recv_sem_ref.at[slots * a + slot], device_id=peer, device_id_type=MESH)
                cp.wait_send()
                cp.wait_recv()

    bufs = [pltpu.HBM(t.shape, t.dtype) for t in list(srcs) + list(lands)]
    out = pl.pallas_call(
        body, name=name,
        out_shape=tuple(bufs),
        in_specs=[_HBM] * (2 * na) + [_SEM, _SEM, pl.BlockSpec(memory_space=pl.ANY)],
        out_specs=[_HBM] * (2 * na),
        input_output_aliases={i: i for i in range(2 * na)},
        compiler_params=pltpu.CompilerParams(has_side_effects=_EFFECT),
    )(*srcs, *lands, send_sem, recv_sem, after)
    return out[:na], out[na:]


def _own_then_slots(mine_ref, lands_ref, rows=slice(None)):
    if len(mine_ref.shape) == 3:
        x, y, c = _mesh_pos()
        total = mine_ref[4 * x + 2 * y + c, rows, :].astype(F32)
    else:
        total = mine_ref[rows, :].astype(F32)
    for j in range(lands_ref.shape[0]):
        total = total + lands_ref[j, rows, :].astype(F32)
    return total


SMALL_ROWS = 2 * PACK_SLICE + 2 * 8


def _small_block(mine, lands, dgpre, name):
    def body(*refs):
        hm, ld, dg = refs[:DEPTH], refs[DEPTH:2 * DEPTH], refs[2 * DEPTH:3 * DEPTH]
        blk, land, sem = refs[3 * DEPTH:]
        for l in range(DEPTH):
            blk[PACK_SLICE * l:PACK_SLICE * (l + 1), :] = _own_then_slots(hm[l], ld[l])
            blk[2 * PACK_SLICE + 8 * l:2 * PACK_SLICE + 8 * (l + 1), :] = dg[l][...]
        cp = pltpu.make_async_copy(blk, _device_rows(land, SMALL_ROWS, *_mesh_pos()), sem)
        cp.start()
        cp.wait()

    vmem = pl.BlockSpec(memory_space=pltpu.VMEM)
    return pl.pallas_call(
        body, name=name,
        in_specs=[vmem] * (3 * DEPTH), out_specs=[vmem, pl.BlockSpec(memory_space=pl.ANY)],
        out_shape=[jax.ShapeDtypeStruct((SMALL_ROWS, 128), F32), jax.ShapeDtypeStruct((N_DEV * SMALL_ROWS, 128), F32)],
        scratch_shapes=[pltpu.SemaphoreType.DMA],
        compiler_params=_compiler_params(),
    )(*mine, *lands, *dgpre)


def _adamw_math(w, g, m, v):
    m = ADAM_B1 * m + (1.0 - ADAM_B1) * g
    v = ADAM_B2 * v + (1.0 - ADAM_B2) * (g * g)
    m_hat = m / (1.0 - ADAM_B1 ** ADAM_STEP)
    v_hat = v / (1.0 - ADAM_B2 ** ADAM_STEP)
    delta = -ADAM_LR * (m_hat / (jnp.sqrt(v_hat) + ADAM_EPS) + ADAM_WD * w)
    return delta, m, v


def _adamw_layer(layer, mine, lands, w, m, v, earlier, token, name, rows):
    _, mm, nn = w.shape

    def body(hm_ref, ld_ref, w_ref, m_ref, v_ref, _, *refs):
        g_ref, d_ref, nm_ref, nv_ref = refs[-4:]
        g = _own_then_slots(hm_ref, ld_ref)
        g_ref[...] = g
        d, nm, nv = _adamw_math(w_ref[...], g, m_ref[...], v_ref[...])
        d_ref[...] = d
        nm_ref[...] = nm
        nv_ref[...] = nv

    spec = pl.BlockSpec((None, rows, nn), lambda i: (layer, i, 0))
    carried = [] if earlier is None else list(earlier)
    return pl.pallas_call(
        body, name=name, grid=(mm // rows,),
        in_specs=([pl.BlockSpec((rows, nn), lambda i: (i, 0)) if mine.ndim == 2
                   else pl.BlockSpec((N_DEV, rows, nn), lambda i: (0, i, 0)),
                   pl.BlockSpec((lands.shape[0], rows, nn), lambda i: (0, i, 0)),
                   spec, spec, spec] + [pl.BlockSpec(memory_space=pl.ANY)] * (1 + len(carried))),
        out_specs=[spec] * 4,
        out_shape=[jax.ShapeDtypeStruct(w.shape, F32)] * 4,
        input_output_aliases={6 + t: t for t in range(len(carried))},
        compiler_params=_compiler_params(("arbitrary",)),
    )(mine, lands, w, m, v, token, *carried)


def _adamw_small(gathered, params):
    def body(all_ref, *refs):
        ins, outs, packs = refs[:15], refs[15:15 + 21], refs[15 + 21]
        loss_ref = outs[0]
        for dev in range(N_DEV):
            for l in range(DEPTH):
                packs[l, PACK_SLICE * dev:PACK_SLICE * (dev + 1), :] = (
                    all_ref[SMALL_ROWS * dev + PACK_SLICE * l:SMALL_ROWS * dev + PACK_SLICE * (l + 1), :])
        loss_ref[...] = packs[DEPTH - 1, ROW_LOSS:ROW_LOSS + 1, 0:1]

        def update(p, sel, g):
            w_ref, m_ref, v_ref = ins[p], ins[5 + p], ins[10 + p]
            d, nm, nv = _adamw_math(w_ref[sel], g, m_ref[sel], v_ref[sel])
            for t, val in enumerate((g, d, nm, nv)):
                outs[1 + 5 * t + p][sel] = val

        for l in range(DEPTH):
            gp = packs.at[l]
            row0 = 2 * PACK_SLICE + 8 * l
            dgpre = all_ref[row0:row0 + 8, :]
            for dev in range(1, N_DEV):
                dgpre = dgpre + all_ref[SMALL_ROWS * dev + row0:SMALL_ROWS * dev + row0 + 8, :]
            for grp in range(4):
                update(0, (l, grp), gp[ROW_PW + BLOCK * grp:ROW_PW + BLOCK * (grp + 1), :])
                update(1, (slice(l, l + 1), slice(128 * grp, 128 * (grp + 1))), gp[ROW_SC + grp:ROW_SC + grp + 1, :])
            update(2, (slice(l, l + 1), slice(None)), gp[ROW_SINK:ROW_SINK + 1, 0:N_HEADS])
            for r in range(D_MODEL // 128):
                sel = (slice(l, l + 1), slice(128 * r, 128 * (r + 1)))
                update(3, sel, dgpre[r:r + 1, :])
                update(4, sel, gp[ROW_NPOST + r:ROW_NPOST + r + 1, :])

    shapes = [jax.ShapeDtypeStruct(p.shape, F32) for p in params[:5]]
    return pl.pallas_call(
        body, name="adamw_small",
        out_shape=[jax.ShapeDtypeStruct((1, 1), F32)] + shapes * 4,
        scratch_shapes=[pltpu.VMEM((DEPTH, PACK_ROWS, 128), F32)],
        compiler_params=_compiler_params(),
    )(gathered, *params)


def kernel(x, w_in, pool_w, pool_scale, attn_sinks, w_out, norm_pre, norm_post, loss_target, m_w_in, m_pool_w, m_pool_scale, m_attn_sinks, m_w_out, m_norm_pre, m_norm_post, v_w_in, v_pool_w, v_pool_scale, v_attn_sinks, v_w_out, v_norm_pre, v_norm_post):
    x0 = x.reshape(SEQ, D_MODEL)
    target = loss_target.reshape(SEQ, D_MODEL)
    bias = jnp.asarray(_attn_bias())
    w_in_t, m_in_t, v_in_t = (jnp.swapaxes(t, 1, 2) for t in (w_in, m_w_in, v_w_in))

    (win0, wout0), later, lands = _allgather([(w_in_t, 0), (w_out, 0)], BF16, "gather_w0",
                                              later=[(w_in_t, 1), (w_out, 1)])
    sems, later, lands, token = _gather_start(later, lands, [SIBLING_AND_SAME_CORE, ALL_PEERS], "gather_w1_start")
    win_full, wout_full = [win0, None], [wout0, None]

    saved = []
    xl = x0
    for layer in range(DEPTH):
        u, pg, q, k, v, ag, z, a = _fwd_front(layer, xl, norm_pre, win_full[layer], token, attn_sinks,
                                             pool_w, pool_scale, bias)
        if layer == 0:
            land = _gather_wait(sems[0], later[0], lands[0], SIBLING_AND_SAME_CORE, z, "gather_w_in1_wait")
            fsems, land, token = _forward_start(land, IN_SHARD, "forward_w_in1_start")
        else:
            wout_full[layer] = _gather_wait(sems[1], later[1], lands[1], ALL_PEERS, z, "gather_w_out1_wait")
        x_next, y = _fwd_out(layer, z, xl, norm_post, wout_full[layer], token)
        if layer == 0:
            win_full[1] = _forward_wait(fsems, land, IN_SHARD, x_next, "forward_w_in1_wait")
        saved.append((xl, u, pg, q, k, v, ag, z, a, y))
        xl = x_next

    params_small = [pool_w, pool_scale, attn_sinks, norm_pre, norm_post,
                    m_pool_w, m_pool_scale, m_attn_sinks, m_norm_pre, m_norm_post,
                    v_pool_w, v_pool_scale, v_attn_sinks, v_norm_pre, v_norm_post]
    exchange, dgpre = [None] * DEPTH, [None] * DEPTH
    dx = None
    for layer in reversed(range(DEPTH)):
        xin, u, pg, q, k, v, ag, z, a, y = saved[layer]
        rest = (y, z, norm_post, wout_full[layer], attn_sinks, u, pg, q, k, v, ag, a, pool_w, pool_scale, bias)
        if layer == DEPTH - 1:
            dx, dproj, gw_out, pack = _bwd_back(layer, True, xl, target, *rest)
        else:
            dproj, gw_out, pack = _bwd_back(layer, False, dx, token, *rest)
        if layer > 0:
            dx, dgpre[layer], gw_in_t = _bwd_in(layer, "both", token, dproj, xin, norm_pre, dx, win_full[layer])
        else:
            (gw_in_t,) = _bwd_in(layer, "dw", token, dproj, xin, norm_pre)
        blocks = [gw_in_t.reshape(N_DEV, IN_SHARD, D_MODEL), gw_out.reshape(N_DEV, OUT_SHARD, D_MODEL),
                  pack.reshape(N_DEV, PACK_SLICE, 128)]
        direct = layer > 0
        srcs, mine = (blocks, None) if direct else _pair_reduce(blocks, f"pair_reduce{layer}")
        send_sem, recv_sem, srcs, lands, token = _exchange_start(srcs, direct, f"exchange_start{layer}")
        exchange[layer] = (send_sem, recv_sem, srcs, lands, mine)
        if layer == 0:
            dx, dgpre[layer] = _bwd_in(layer, "dx", token, dproj, xin, norm_pre, dx, win_full[layer])

    own, waited = [None] * DEPTH, [None] * DEPTH
    big_in, big_out, after = None, None, dx
    for layer in reversed(range(DEPTH)):
        send_sem, recv_sem, srcs, lands, mine = exchange[layer]
        srcs, waited[layer] = _exchange_wait(send_sem, recv_sem, srcs, lands, layer > 0, after, f"exchange_wait{layer}")
        own[layer] = srcs if mine is None else mine
        if layer > 0:
            big_in = _adamw_layer(layer, own[layer][0], waited[layer][0], w_in_t, m_in_t, v_in_t, big_in, token,
                                  f"adamw_in{layer}", 96)
            big_out = _adamw_layer(layer, own[layer][1], waited[layer][1], w_out, m_w_out, v_w_out, big_out, token,
                                   f"adamw_out{layer}", 128)
            after = big_out[0]
    block, land = _small_block([own[l][2] for l in range(DEPTH)], [waited[l][2] for l in range(DEPTH)],
                               dgpre, "small_block")
    sems, block, land, token = _gather_start([block], [land], [ALL_PEERS], "gather_small_start")
    big_in = _adamw_layer(0, own[0][0], waited[0][0], w_in_t, m_in_t, v_in_t, big_in, token, "adamw_in0", 96)
    big_out = _adamw_layer(0, own[0][1], waited[0][1], w_out, m_w_out, v_w_out, big_out, token, "adamw_out0", 128)
    gathered = _gather_wait(sems[0], block[0], land[0], ALL_PEERS, big_out[0], "gather_small_wait")
    small_out = _adamw_small(gathered, params_small)
    loss = small_out[0].reshape(())

    outs = [loss, dx.reshape(1, SEQ, D_MODEL)]
    for t in range(4):
        pw_, sc_, sk_, npre_, npost_ = small_out[1 + 5 * t:6 + 5 * t]
        outs += [jnp.swapaxes(big_in[t], 1, 2), pw_, sc_, sk_, big_out[t], npre_, npost_]
    return tuple(outs)
```

```python
import numpy as np
import jax
import jax.numpy as jnp
from jax import lax
from jax.experimental import pallas as pl
from jax.experimental.pallas import tpu as pltpu

F32 = jnp.float32
BF16 = jnp.bfloat16

N_DEV = 8
SEQ = 2048
D_MODEL = 1024
D_POOL = 512
D_ATTN = 512
D_KV = 128
D_IN = 2304
N_HEADS = 8
GQA = 4
HEAD_DIM = 64
BLOCK = 128
N_BLOCKS = SEQ // BLOCK
POOL_WINDOWS = (2, 4, 8, 16)
DEPTH = 2
EPS = 1e-6
NEG_INF = -1e30
SCALE = HEAD_DIM ** -0.5
IN_SHARD = D_IN // N_DEV
OUT_SHARD = D_MODEL // N_DEV

COL_U, COL_PG, COL_Q, COL_K, COL_V, COL_AG = 0, 512, 1024, 1536, 1664, 1792

ADAM_LR = 0.001
ADAM_B1 = 0.9
ADAM_B2 = 0.999
ADAM_EPS = 1e-08
ADAM_WD = 0.01
ADAM_STEP = 10

TOKEN_TILE = 512
ADAM_ROWS_IN, ADAM_ROWS_OUT = 144, 128
VMEM_LIMIT = 56 * 1024 * 1024
MESH = pl.DeviceIdType.MESH

ROW_PW, ROW_SC, ROW_SINK, ROW_NPRE, ROW_NPOST, ROW_LOSS = 0, 512, 520, 528, 536, 544
PACK_ROWS = 576
PACK_SLICE = PACK_ROWS // N_DEV


def _nn(a, b):
    return jnp.dot(a, b, preferred_element_type=F32)


def _nt(a, b):
    return lax.dot_general(a, b, (((1,), (1,)), ((), ())), preferred_element_type=F32)


def _tn(a, b):
    return lax.dot_general(a, b, (((0,), (0,)), ((), ())), preferred_element_type=F32)


def _silu_parts(g):
    s = jax.nn.sigmoid(g)
    return g * s, s * (1.0 + g * (1.0 - s))


def _resident(shape):
    return pl.BlockSpec(shape, lambda *_: (0,) * len(shape), pipeline_mode=pl.Buffered(1))


def _compiler_params(sem=None):
    if sem is None:
        return pltpu.CompilerParams(vmem_limit_bytes=VMEM_LIMIT)
    return pltpu.CompilerParams(dimension_semantics=sem, vmem_limit_bytes=VMEM_LIMIT)


def _attn_bias():
    t = np.arange(BLOCK)[None, :]
    j = np.arange(BLOCK)[:, None]
    current = j <= t
    dist = np.where(current, t - j, t + BLOCK - j).astype(np.float32)
    out = np.zeros((2, 2, BLOCK, GQA * BLOCK), np.float32)
    for variant in range(2):
        valid = current | (variant == 1)
        for kv in range(2):
            for g in range(GQA):
                slope = np.float32(2.0 ** (-(kv * GQA + g + 1)))
                out[variant, kv, :, g * BLOCK:(g + 1) * BLOCK] = np.where(valid, -slope * dist, np.float32(NEG_INF))
    return out


def _replicate_head(kx, kv):
    rolled = pltpu.roll(kx, 64, 1)
    lane = lax.broadcasted_iota(jnp.int32, kx.shape, 1)
    dup = jnp.where(lane < 64, kx, rolled) if kv == 0 else jnp.where(lane < 64, rolled, kx)
    return jnp.concatenate([dup, dup], axis=1).astype(BF16)


def _stack_heads(qv):
    lane = lax.broadcasted_iota(jnp.int32, qv.shape, 1)
    zero = jnp.zeros_like(qv)
    return jnp.concatenate([jnp.where((lane >= 64 * g) & (lane < 64 * g + 64), qv, zero) for g in range(GQA)], axis=0)


def _unstack_heads(xs):
    lane = lax.broadcasted_iota(jnp.int32, (BLOCK, 256), 1)
    return jnp.where(lane < 64, xs[0:128], jnp.where(lane < 128, xs[128:256], jnp.where(lane < 192, xs[256:384], xs[384:512])))


def _fold_heads(r):
    h = r[:, 0:128] + r[:, 128:256]
    return h + pltpu.roll(h, 64, 1)


def _sink_row(sink_ref, layer, kv):
    lane = lax.broadcasted_iota(jnp.int32, (1, GQA * BLOCK), 1)
    s4 = [sink_ref[layer, kv * GQA + g] for g in range(GQA)]
    return jnp.where(lane < 128, s4[0], jnp.where(lane < 256, s4[1], jnp.where(lane < 384, s4[2], s4[3])))


def _band_is_current():
    j = lax.broadcasted_iota(jnp.int32, (BLOCK, GQA * BLOCK), 0)
    t = lax.broadcasted_iota(jnp.int32, (BLOCK, GQA * BLOCK), 1) & (BLOCK - 1)
    return j <= t


def _pack_band(full, current):
    return jnp.where(current, full[BLOCK:], full[:BLOCK])


def _unpack_band(packed, current):
    zero = jnp.zeros_like(packed)
    return jnp.concatenate([jnp.where(current, zero, packed), jnp.where(current, packed, zero)], axis=0)


def _probs_keys_major(k_rep, q_st, bias, sink, current):
    st = _pack_band(_nt(k_rep, q_st), current) * SCALE + bias
    m = jnp.maximum(jnp.max(st, axis=0, keepdims=True), sink)
    p = jnp.exp(st - m)
    esink = jnp.exp(sink - m)
    rl = 1.0 / (jnp.sum(p, axis=0, keepdims=True) + esink)
    return p * rl, esink * rl


WINDOW_HALO = 16


def _window_sum(ext, w, forward):
    s = ext
    sh = 1
    while sh < w:
        s = s + pltpu.roll(s, (ext.shape[0] - sh) if forward else sh, 0)
        sh *= 2
    return s


def _inv_count(n, w):
    t = n * BLOCK + lax.broadcasted_iota(jnp.int32, (BLOCK, 1), 0) + 1
    return 1.0 / jnp.minimum(t.astype(F32), float(w))


def _kv_ext(ref, n):
    r0 = pl.multiple_of(jnp.maximum(n - 1, 0) * BLOCK, BLOCK)
    r1 = pl.multiple_of(n * BLOCK, BLOCK)
    return jnp.concatenate([ref[pl.ds(r0, BLOCK), :], ref[pl.ds(r1, BLOCK), :]], axis=0)


def _rows_of(vec_ref, pack_ref, row0):
    for r in range(D_MODEL // 128):
        pack_ref[row0 + r:row0 + r + 1, :] = vec_ref[:, 128 * r:128 * (r + 1)]


FRONT_TILE = 2 * BLOCK


def _fwd_front(layer, x, norm_pre, w_in_t, token, sinks, pool_w, pool_scale, bias):
    tm = FRONT_TILE

    def body(sink_ref, x_ref, g_ref, w_ref, _, pw_ref, sc_ref, bias_ref,
             u_ref, pg_ref, q_ref, k_ref, v_ref, ag_ref, z_ref, a_ref, uprev, kprev, vprev):
        i = pl.program_id(0)

        @pl.when(i == 0)
        def _():
            uprev[...] = jnp.zeros_like(uprev)
            kprev[...] = jnp.zeros_like(kprev)
            vprev[...] = jnp.zeros_like(vprev)

        xv = x_ref[...]
        r = lax.rsqrt(jnp.mean(xv * xv, axis=-1, keepdims=True) + EPS)
        h = (xv * r * g_ref[layer:layer + 1, :]).astype(BF16)
        u_ref[...] = _nt(h, w_ref[COL_U:COL_PG, :])
        pg_ref[...] = _nt(h, w_ref[COL_PG:COL_Q, :])
        for sb in range(tm // BLOCK):
            n = (tm // BLOCK) * i + sb
            rows = slice(BLOCK * sb, BLOCK * (sb + 1))
            before = slice(BLOCK * (sb - 1), BLOCK * sb)
            uv = u_ref[rows, :]
            halo = (uprev[BLOCK - WINDOW_HALO:, :] if sb == 0
                    else u_ref[BLOCK * sb - WINDOW_HALO:BLOCK * sb, :])
            ext = jnp.concatenate([halo, uv], axis=0)
            for g, w in enumerate(POOL_WINDOWS):
                cs = slice(BLOCK * g, BLOCK * (g + 1))
                win = _window_sum(ext[:, cs], w, forward=False)[WINDOW_HALO:]
                pooled = win * _inv_count(n, w) - uv[:, cs]
                mixed = _nn(pooled.astype(BF16), pw_ref[g].astype(BF16))
                gate, _ = _silu_parts(pg_ref[rows, cs])
                z_ref[rows, cs] = (mixed * sc_ref[layer:layer + 1, cs] * gate).astype(BF16)

        q_ref[...] = _nt(h, w_ref[COL_Q:COL_K, :]).astype(BF16)
        k_ref[...] = _nt(h, w_ref[COL_K:COL_V, :])
        v_ref[...] = _nt(h, w_ref[COL_V:COL_AG, :])
        ag_ref[...] = _nt(h, w_ref[COL_AG:D_IN, :])

        current = _band_is_current()
        for sb in range(tm // BLOCK):
            n = (tm // BLOCK) * i + sb
            rows = slice(BLOCK * sb, BLOCK * (sb + 1))
            before = slice(BLOCK * (sb - 1), BLOCK * sb)
            kx = jnp.concatenate([kprev[...] if sb == 0 else k_ref[before, :], k_ref[rows, :]], axis=0)
            vx = jnp.concatenate([vprev[...] if sb == 0 else v_ref[before, :], v_ref[rows, :]], axis=0)
            variant = jnp.minimum(n, 1) if sb == 0 else 1
            for kv in range(2):
                cs = slice(256 * kv, 256 * (kv + 1))
                p, _ = _probs_keys_major(_replicate_head(kx, kv), _stack_heads(q_ref[rows, cs]),
                                         bias_ref[variant, kv], _sink_row(sink_ref, layer, kv), current)
                o = _unstack_heads(_tn(_unpack_band(p.astype(BF16), current), _replicate_head(vx, kv)))
                a_ref[rows, cs] = o
                gate, _ = _silu_parts(ag_ref[rows, cs])
                z_ref[rows, D_POOL + 256 * kv:D_POOL + 256 * (kv + 1)] = (o * gate).astype(BF16)

        tail = slice(tm - BLOCK, tm)
        uprev[...] = u_ref[tail, :]
        kprev[...] = k_ref[tail, :]
        vprev[...] = v_ref[tail, :]

    row = lambda c: pl.BlockSpec((tm, c), lambda i: (i, 0))
    const = lambda shape: pl.BlockSpec(shape, lambda i: (0,) * len(shape))
    return pl.pallas_call(
        body, name=f"fwd_front{layer}", grid=(SEQ // tm,),
        in_specs=[pl.BlockSpec(memory_space=pltpu.SMEM), row(D_MODEL), const((DEPTH, D_MODEL)),
                  _resident((D_IN, D_MODEL)), const((8, 128)),
                  pl.BlockSpec((None, 4, BLOCK, BLOCK), lambda i: (layer, 0, 0, 0)), const((DEPTH, D_POOL)),
                  _resident((2, 2, BLOCK, GQA * BLOCK))],
        out_specs=[row(D_POOL), row(D_POOL), row(D_ATTN), row(D_KV), row(D_KV), row(D_ATTN), row(D_MODEL),
                   row(D_ATTN)],
        out_shape=[jax.ShapeDtypeStruct((SEQ, D_POOL), F32), jax.ShapeDtypeStruct((SEQ, D_POOL), F32),
                   jax.ShapeDtypeStruct((SEQ, D_ATTN), BF16), jax.ShapeDtypeStruct((SEQ, D_KV), F32),
                   jax.ShapeDtypeStruct((SEQ, D_KV), F32), jax.ShapeDtypeStruct((SEQ, D_ATTN), F32),
                   jax.ShapeDtypeStruct((SEQ, D_MODEL), BF16), jax.ShapeDtypeStruct((SEQ, D_ATTN), F32)],
        scratch_shapes=[pltpu.VMEM((BLOCK, D_POOL), F32), pltpu.VMEM((BLOCK, D_KV), F32),
                        pltpu.VMEM((BLOCK, D_KV), F32)],
        compiler_params=_compiler_params(("arbitrary",)),
    )(sinks, x, norm_pre, w_in_t, token, pool_w, pool_scale, bias)


def _fwd_out(layer, z, x, norm_post, w_out, token):
    tm = TOKEN_TILE

    def body(z_ref, x_ref, g_ref, w_ref, _, xn_ref, y_ref):
        y = _nn(z_ref[...], w_ref[...])
        y_ref[...] = y
        r = lax.rsqrt(jnp.mean(y * y, axis=-1, keepdims=True) + EPS)
        xn_ref[...] = x_ref[...] + y * r * g_ref[layer:layer + 1, :]

    row = lambda c: pl.BlockSpec((tm, c), lambda i: (i, 0))
    return pl.pallas_call(
        body, name=f"fwd_out{layer}", grid=(SEQ // tm,),
        in_specs=[row(D_MODEL), row(D_MODEL), pl.BlockSpec((DEPTH, D_MODEL), lambda i: (0, 0)),
                  _resident((D_MODEL, D_MODEL)), pl.BlockSpec((8, 128), lambda i: (0, 0))],
        out_specs=[row(D_MODEL), row(D_MODEL)],
        out_shape=[jax.ShapeDtypeStruct((SEQ, D_MODEL), F32), jax.ShapeDtypeStruct((SEQ, D_MODEL), F32)],
        compiler_params=_compiler_params(("arbitrary",)),
    )(z, x, norm_post, w_out, token)


BACK_TILE = 2 * BLOCK


def _bwd_back(layer, top, dxo_or_xf, target_or_token, y, z, norm_post, w_out, sinks, u, pg, q, k, v, ag, a,
              pool_w, pool_scale, bias):
    tm = BACK_TILE
    steps = SEQ // tm
    last = steps - 1
    per = tm // BLOCK

    def body(*refs):
        refs = list(refs)
        sink_ref, first, second = refs[:3]
        (y_ref, z_ref, g_ref, w_ref, u_ref, up_ref, pg_ref, q_ref, k_ref, v_ref, ag_ref, a_ref, pw_ref, sc_ref,
         bias_ref) = refs[3:18]
        del refs[:18]
        dxo_ref = refs.pop(0) if top else None
        dp_ref, dw_ref, pack_ref, acc, dg, lacc, dzs, ck, cv, ce = refs
        i = pl.program_id(0)
        blk = last - i

        @pl.when(i == 0)
        def _():
            acc[...] = jnp.zeros_like(acc)
            dg[...] = jnp.zeros_like(dg)
            lacc[...] = jnp.zeros_like(lacc)
            pack_ref[...] = jnp.zeros_like(pack_ref)
            ck[...] = jnp.zeros_like(ck)
            cv[...] = jnp.zeros_like(cv)
            ce[...] = jnp.zeros_like(ce)

        if top:
            d = first[...] - second[...]
            dxo_v = d * (1.0 / D_MODEL)
            dxo_ref[...] = dxo_v
            part = jnp.sum(d * d, axis=-1, keepdims=True) * (1.0 / D_MODEL)
            lacc[...] += 0.5 * jnp.sum(part, axis=0, keepdims=True)
        else:
            dxo_v = first[...]
        yv = y_ref[...]
        r = lax.rsqrt(jnp.mean(yv * yv, axis=-1, keepdims=True) + EPS)
        yn = yv * r
        dg[...] += jnp.sum(dxo_v * yn, axis=0, keepdims=True)
        dyn = dxo_v * g_ref[layer:layer + 1, :]
        dy = (r * (dyn - yn * jnp.mean(dyn * yn, axis=-1, keepdims=True))).astype(BF16)
        dzs[...] = _nt(dy, w_ref[...])
        acc[...] += _tn(z_ref[...], dy)

        lane = lax.broadcasted_iota(jnp.int32, (1, 128), 1)
        lane2 = lax.broadcasted_iota(jnp.int32, (256, 128), 1)
        current = _band_is_current()
        for sb in reversed(range(per)):
            n = per * blk + sb
            rows = slice(BLOCK * sb, BLOCK * (sb + 1))

            uv = u_ref[rows, :]
            if sb == 0:
                halo = up_ref[BLOCK - WINDOW_HALO:, :] * (n > 0).astype(F32)
            else:
                halo = u_ref[BLOCK * sb - WINDOW_HALO:BLOCK * sb, :]
            ext = jnp.concatenate([halo, uv], axis=0)
            for g, w in enumerate(POOL_WINDOWS):
                cs = slice(BLOCK * g, BLOCK * (g + 1))
                inv = _inv_count(n, w)
                win = _window_sum(ext[:, cs], w, forward=False)[WINDOW_HALO:]
                pooled = win * inv - uv[:, cs]
                pw_g = pw_ref[g].astype(BF16)
                mixed = _nn(pooled.astype(BF16), pw_g)
                gate, dgate = _silu_parts(pg_ref[rows, cs])
                dzp = dzs[rows, cs]
                sc = sc_ref[layer:layer + 1, cs]
                dpm = dzp * gate
                dp_ref[rows, COL_PG + BLOCK * g:COL_PG + BLOCK * (g + 1)] = (dzp * (mixed * sc) * dgate).astype(BF16)
                pack_ref[ROW_SC + g:ROW_SC + g + 1, :] += jnp.sum(dpm * mixed, axis=0, keepdims=True)
                dmixed = (dpm * sc).astype(BF16)
                pack_ref[ROW_PW + BLOCK * g:ROW_PW + BLOCK * (g + 1), :] += _tn(pooled.astype(BF16), dmixed)
                dpooled = _nt(dmixed, pw_g)
                e = dpooled * inv
                lead = _window_sum(jnp.concatenate([e, ce[:WINDOW_HALO, cs]], axis=0), w, forward=True)[:BLOCK]
                dp_ref[rows, COL_U + BLOCK * g:COL_U + BLOCK * (g + 1)] = (lead - dpooled).astype(BF16)
                ce[:, cs] = e

            kx = _kv_ext(k_ref, n)
            vx = _kv_ext(v_ref, n)
            variant = jnp.minimum(n, 1) if sb == 0 else 1
            dsink_row = jnp.zeros((1, 128), F32)
            tks, tvs = [], []
            for kv in range(2):
                cs = slice(256 * kv, 256 * (kv + 1))
                k_rep = _replicate_head(kx, kv)
                v_rep = _replicate_head(vx, kv)
                q_st = _stack_heads(q_ref[rows, cs])
                gate, dgate = _silu_parts(ag_ref[rows, cs])
                dza = dzs[rows, D_POOL + 256 * kv:D_POOL + 256 * (kv + 1)]
                dp_ref[rows, COL_AG + 256 * kv:COL_AG + 256 * (kv + 1)] = (dza * a_ref[rows, cs] * dgate).astype(BF16)
                da_st = _stack_heads((dza * gate).astype(BF16))
                p, psink = _probs_keys_major(k_rep, q_st, bias_ref[variant, kv], _sink_row(sink_ref, layer, kv),
                                             current)
                dpt = _pack_band(_nt(v_rep, da_st), current)
                delta = jnp.sum(p * dpt, axis=0, keepdims=True)
                dst = _unpack_band((p * (dpt - delta) * SCALE).astype(BF16), current)
                sink_terms = psink * delta
                for g in range(GQA):
                    dsink = -jnp.sum(sink_terms[:, BLOCK * g:BLOCK * (g + 1)], axis=1, keepdims=True)
                    dsink_row = dsink_row + jnp.where(lane == kv * GQA + g, dsink, 0.0)
                dp_ref[rows, COL_Q + 256 * kv:COL_Q + 256 * (kv + 1)] = _unstack_heads(_tn(dst, k_rep)).astype(BF16)
                tks.append(_fold_heads(_nn(dst, q_st)))
                tvs.append(_fold_heads(_nn(_unpack_band(p.astype(BF16), current), da_st)))
            pack_ref[ROW_SINK:ROW_SINK + 1, :] += dsink_row
            dkx = jnp.where(lane2 < 64, tks[0], tks[1])
            dvx = jnp.where(lane2 < 64, tvs[0], tvs[1])
            dp_ref[rows, COL_K:COL_V] = (ck[...] + dkx[BLOCK:]).astype(BF16)
            dp_ref[rows, COL_V:COL_AG] = (cv[...] + dvx[BLOCK:]).astype(BF16)
            ck[...] = dkx[:BLOCK]
            cv[...] = dvx[:BLOCK]

        @pl.when(i == steps - 1)
        def _():
            dw_ref[...] = acc[...].astype(BF16)
            _rows_of(dg, pack_ref, ROW_NPOST)
            pack_ref[ROW_LOSS:ROW_LOSS + 1, :] = jnp.where(lane == 0, lacc[...], 0.0)

    row = lambda c: pl.BlockSpec((tm, c), lambda i: (last - i, 0))
    const = lambda shape: pl.BlockSpec(shape, lambda i: (0,) * len(shape))
    act = jax.ShapeDtypeStruct((SEQ, D_MODEL), F32)
    return pl.pallas_call(
        body, name=f"bwd_back{layer}", grid=(steps,),
        in_specs=[pl.BlockSpec(memory_space=pltpu.SMEM), row(D_MODEL), row(D_MODEL) if top else const((8, 128)),
                  row(D_MODEL), row(D_MODEL), const((DEPTH, D_MODEL)), _resident((D_MODEL, D_MODEL)),
                  row(D_POOL), pl.BlockSpec((BLOCK, D_POOL), lambda i: (jnp.maximum(per * (last - i) - 1, 0), 0)),
                  row(D_POOL), row(D_ATTN), _resident((SEQ, D_KV)), _resident((SEQ, D_KV)), row(D_ATTN), row(D_ATTN),
                  pl.BlockSpec((None, 4, BLOCK, BLOCK), lambda i: (layer, 0, 0, 0)), const((DEPTH, D_POOL)),
                  _resident((2, 2, BLOCK, GQA * BLOCK))],
        out_specs=([row(D_MODEL)] * (1 if top else 0)
                   + [row(D_IN), const((D_MODEL, D_MODEL)), const((PACK_ROWS, 128))]),
        out_shape=([act] * (1 if top else 0)
                   + [jax.ShapeDtypeStruct((SEQ, D_IN), BF16), jax.ShapeDtypeStruct((D_MODEL, D_MODEL), BF16),
                      jax.ShapeDtypeStruct((PACK_ROWS, 128), F32)]),
        scratch_shapes=[pltpu.VMEM((D_MODEL, D_MODEL), F32), pltpu.VMEM((1, D_MODEL), F32), pltpu.VMEM((1, 1), F32),
                        pltpu.VMEM((tm, D_MODEL), F32), pltpu.VMEM((BLOCK, D_KV), F32), pltpu.VMEM((BLOCK, D_KV), F32),
                        pltpu.VMEM((BLOCK, D_POOL), F32)],
        compiler_params=_compiler_params(("arbitrary",)),
    )(sinks, dxo_or_xf, target_or_token, y, z, norm_post, w_out, u, u, pg, q, k, v, ag, a, pool_w, pool_scale, bias)


def _bwd_in(layer, part, token, dproj, x, norm_pre, dxo=None, w_in_t=None):
    want_dw, want_dx = part in ("both", "dw"), part in ("both", "dx")
    tm = TOKEN_TILE
    steps = SEQ // tm
    cw = 256

    def body(*refs):
        refs = list(refs)
        dp_ref, x_ref, g_ref = refs[1:4]
        del refs[:4]
        if want_dx:
            dxo_ref, w_ref, dx_ref, dgo_ref = refs[:4]
            del refs[:4]
            dg = refs.pop()
        if want_dw:
            dw_ref, acc = refs
        i = pl.program_id(0)

        @pl.when(i == 0)
        def _():
            if want_dw:
                acc[...] = jnp.zeros_like(acc)
            if want_dx:
                dg[...] = jnp.zeros_like(dg)

        xv = x_ref[...]
        gv = g_ref[layer:layer + 1, :]
        r = lax.rsqrt(jnp.mean(xv * xv, axis=-1, keepdims=True) + EPS)
        xn = xv * r
        if want_dw:
            hb = (xn * gv).astype(BF16)
            for c in range(0, D_IN, cw):
                acc[c:c + cw, :] += _tn(dp_ref[:, c:c + cw], hb)
        if want_dx:
            dh = _nn(dp_ref[...], w_ref[...])
            dg[...] += jnp.sum(dh * xn, axis=0, keepdims=True)
            dhn = dh * gv
            dx_ref[...] = dxo_ref[...] + r * (dhn - xn * jnp.mean(dhn * xn, axis=-1, keepdims=True))

        @pl.when(i == steps - 1)
        def _():
            if want_dw:
                dw_ref[...] = acc[...].astype(BF16)
            if want_dx:
                _rows_of(dg, dgo_ref, 0)

    row = lambda c: pl.BlockSpec((tm, c), lambda i: (i, 0))
    const = lambda shape: pl.BlockSpec(shape, lambda i: (0,) * len(shape))
    in_specs = [const((8, 128)), row(D_IN), row(D_MODEL), const((DEPTH, D_MODEL))]
    operands = [token, dproj, x, norm_pre]
    out_specs, out_shape, scratch = [], [], []
    if want_dx:
        in_specs += [row(D_MODEL), _resident((D_IN, D_MODEL))]
        operands += [dxo, w_in_t]
        out_specs += [row(D_MODEL), const((8, 128))]
        out_shape += [jax.ShapeDtypeStruct((SEQ, D_MODEL), F32), jax.ShapeDtypeStruct((8, 128), F32)]
    if want_dw:
        out_specs.append(const((D_IN, D_MODEL)))
        out_shape.append(jax.ShapeDtypeStruct((D_IN, D_MODEL), BF16))
        scratch.append(pltpu.VMEM((D_IN, D_MODEL), F32))
    if want_dx:
        scratch.append(pltpu.VMEM((1, D_MODEL), F32))
    return pl.pallas_call(
        body, name=f"bwd_in_{part}{layer}", grid=(steps,),
        in_specs=in_specs, out_specs=out_specs, out_shape=out_shape, scratch_shapes=scratch,
        compiler_params=_compiler_params(("arbitrary",)),
    )(*operands)


def _mesh_pos():
    return lax.axis_index("x"), lax.axis_index("y"), lax.axis_index("c")


def _device_index():
    x, y, c = _mesh_pos()
    return 4 * x + 2 * y + c


def _device_rows(ref, m, px, py, pc):
    return ref.at[pl.ds(pl.multiple_of((4 * px + 2 * py + pc) * m, 16 if m % 16 == 0 else 8), m), :]


def _allgather(srcs, out_dtype, name, later=()):
    na, nb = len(srcs), len(later)
    every = list(srcs) + list(later)
    shapes = [(a.shape[-2], a.shape[-1]) for a, _ in every]

    def body(*refs):
        xs, refs = refs[:na + nb], refs[na + nb:]
        outs, cast, land, refs = refs[:na], refs[na:na + nb], refs[na + nb:na + 2 * nb], refs[na + 2 * nb:]
        stage, (send_sems, recv_sems, local_sems) = refs[:na], refs[na:]
        x, y, c = _mesh_pos()
        me, sibling = (x, y, c), (x, y, 1 - c)
        near = [(1 - x, y), (x, 1 - y)]
        far = (1 - x, 1 - y)
        relay_from, relay_to = (x ^ (1 - c), y ^ c), (x ^ c, y ^ (1 - c))
        k_from, k_to = 1 + c, 2 - c

        def slot(a, px, py, pc):
            return _device_rows(outs[a], shapes[a][0], px, py, pc)

        def copy(a, k, block, to, src=None):
            return pltpu.make_async_remote_copy(
                src_ref=slot(a, *block) if src is None else src, dst_ref=slot(a, *block),
                send_sem=send_sems.at[a, k], recv_sem=recv_sems.at[a, k], device_id=to, device_id_type=MESH)

        def cast_block(i):
            layer = every[i][1]
            return (xs[i][...] if layer is None else xs[i][layer]).astype(out_dtype)

        for a in range(na):
            stage[a][...] = cast_block(a)
        mine = [pltpu.make_async_copy(stage[a], slot(a, *me), local_sems.at[a]) for a in range(na)]
        for cp in mine:
            cp.start()
        sent = []
        for a in range(na):
            sent.append(copy(a, 0, me, sibling, src=stage[a]))
            sent += [copy(a, 1 + j, me, (*chip, c), src=stage[a]) for j, chip in enumerate(near)]
        for cp in sent:
            cp.start()
        for b in range(nb):
            cast[b][...] = cast_block(na + b)
            cp = pltpu.make_async_copy(cast[b], _device_rows(land[b], shapes[na + b][0], *me), local_sems.at[na + b])
            cp.start()
            mine.append(cp)
        for a in range(na):
            copy(a, k_from, (*relay_from, c), me).wait_recv()
            sent += [copy(a, 3, (*relay_from, c), (*relay_to, c)), copy(a, 3 + k_from, (*relay_from, c), sibling)]
            sent[-2].start()
            sent[-1].start()
        for a in range(na):
            copy(a, k_to, (*relay_to, c), me).wait_recv()
            sent.append(copy(a, 3 + k_to, (*relay_to, c), sibling))
            sent[-1].start()
        for a in range(na):
            copy(a, 3, (*far, c), me).wait_recv()
            sent.append(copy(a, 6, (*far, c), sibling))
            sent[-1].start()
        for a in range(na):
            copy(a, 0, sibling, me).wait_recv()
            for j, chip in enumerate(near + [far]):
                copy(a, 4 + j, (*chip, 1 - c), me).wait_recv()
        for cp in sent:
            cp.wait_send()
        for cp in mine:
            cp.wait()

    vmem = pl.BlockSpec(memory_space=pltpu.VMEM)
    hbm = pl.BlockSpec(memory_space=pl.ANY)
    gathered = [jax.ShapeDtypeStruct((N_DEV * m, n), out_dtype) for m, n in shapes]
    out = pl.pallas_call(
        body, name=name,
        in_specs=[vmem] * (na + nb),
        out_specs=[hbm] * na + [vmem] * nb + [hbm] * nb,
        out_shape=gathered[:na] + [jax.ShapeDtypeStruct(s, out_dtype) for s in shapes[na:]] + gathered[na:],
        scratch_shapes=([pltpu.VMEM(s, out_dtype) for s in shapes[:na]]
                        + [pltpu.SemaphoreType.DMA((na, 7)), pltpu.SemaphoreType.DMA((na, 7)),
                           pltpu.SemaphoreType.DMA((na + nb,))]),
        compiler_params=_compiler_params(),
    )(*[a for a, _ in every])
    return out[:na], out[na:na + nb], out[na + nb:]


ALL_PEERS = tuple(range(1, N_DEV))
SIBLING_AND_SAME_CORE = (1, 2, 4, 6)


def _related(k, x, y, c):
    return x ^ ((k >> 2) & 1), y ^ ((k >> 1) & 1), c ^ (k & 1)


def _gather_start(blocks, lands, relations, name):
    na = len(blocks)

    def body(*refs):
        src, land, sems, token = refs[:na], refs[na:2 * na], refs[2 * na:4 * na], refs[-1]
        x, y, c = _mesh_pos()
        for a in range(na):
            for k in relations[a]:
                pltpu.make_async_remote_copy(
                    src_ref=src[a], dst_ref=_device_rows(land[a], blocks[a].shape[0], x, y, c),
                    send_sem=sems[2 * a].at[k - 1], recv_sem=sems[2 * a + 1].at[k - 1],
                    device_id=_related(k, x, y, c), device_id_type=MESH).start()
        token[...] = jnp.zeros_like(token)

    bufs = [pltpu.HBM(t.shape, t.dtype) for t in list(blocks) + list(lands)]
    out = pl.pallas_call(
        body, name=name,
        out_shape=(*([pltpu.SemaphoreType.DMA((N_DEV - 1,))] * (2 * na)), *bufs, jax.ShapeDtypeStruct((8, 128), F32)),
        in_specs=[_HBM] * (2 * na),
        out_specs=(*([_SEM] * (2 * na)), *([_HBM] * (2 * na)), pl.BlockSpec(memory_space=pltpu.VMEM)),
        input_output_aliases={i: 2 * na + i for i in range(2 * na)},
        compiler_params=pltpu.CompilerParams(has_side_effects=_EFFECT),
    )(*[pltpu.with_memory_space_constraint(t, pltpu.HBM) for t in list(blocks) + list(lands)])
    sems = [(out[2 * a], out[2 * a + 1]) for a in range(na)]
    return sems, out[2 * na:3 * na], out[3 * na:4 * na], out[-1]


def _gather_wait(sems, block, land, relations, after, name):
    def body(src, land_ref, send_sem, recv_sem, after_ref, src_out, land_out):
        x, y, c = _mesh_pos()
        for k in relations:
            peer = _related(k, x, y, c)
            cp = pltpu.make_async_remote_copy(
                src_ref=src, dst_ref=_device_rows(land_ref, block.shape[0], *peer),
                send_sem=send_sem.at[k - 1], recv_sem=recv_sem.at[k - 1], device_id=peer, device_id_type=MESH)
            cp.wait_send()
            cp.wait_recv()

    out = pl.pallas_call(
        body, name=name,
        out_shape=(pltpu.HBM(block.shape, block.dtype), pltpu.HBM(land.shape, land.dtype)),
        in_specs=[_HBM, _HBM, _SEM, _SEM, pl.BlockSpec(memory_space=pl.ANY)],
        out_specs=[_HBM, _HBM],
        input_output_aliases={0: 0, 1: 1},
        compiler_params=pltpu.CompilerParams(has_side_effects=_EFFECT),
    )(block, land, sems[0], sems[1], after)
    return out[1]


def _forward_plan(land_ref, m):
    x, y, c = _mesh_pos()
    return [_device_rows(land_ref, m, qx, qy, c) for qx, qy in ((1 - x, y), (x, 1 - y), (1 - x, 1 - y))], (x, y, 1 - c)


def _forward_start(land, m, name):
    def body(land_ref, send_sem, recv_sem, land_out, token):
        rows, sibling = _forward_plan(land_ref, m)
        for j, r in enumerate(rows):
            pltpu.make_async_remote_copy(src_ref=r, dst_ref=r, send_sem=send_sem.at[j], recv_sem=recv_sem.at[j],
                                         device_id=sibling, device_id_type=MESH).start()
        token[...] = jnp.zeros_like(token)

    out = pl.pallas_call(
        body, name=name,
        out_shape=(pltpu.SemaphoreType.DMA((3,)), pltpu.SemaphoreType.DMA((3,)), pltpu.HBM(land.shape, land.dtype),
                   jax.ShapeDtypeStruct((8, 128), F32)),
        in_specs=[_HBM],
        out_specs=(_SEM, _SEM, _HBM, pl.BlockSpec(memory_space=pltpu.VMEM)),
        input_output_aliases={0: 2},
        compiler_params=pltpu.CompilerParams(has_side_effects=_EFFECT),
    )(pltpu.with_memory_space_constraint(land, pltpu.HBM))
    return (out[0], out[1]), out[2], out[3]


def _forward_wait(sems, land, m, after, name):
    def body(land_ref, send_sem, recv_sem, after_ref, land_out):
        x, y, c = _mesh_pos()
        mine, sibling = _forward_plan(land_ref, m)
        theirs = [_device_rows(land_ref, m, qx, qy, 1 - c) for qx, qy in ((1 - x, y), (x, 1 - y), (1 - x, 1 - y))]
        for j in range(3):
            cp = pltpu.make_async_remote_copy(src_ref=mine[j], dst_ref=theirs[j], send_sem=send_sem.at[j],
                                              recv_sem=recv_sem.at[j], device_id=sibling, device_id_type=MESH)
            cp.wait_send()
            cp.wait_recv()

    return pl.pallas_call(
        body, name=name,
        out_shape=pltpu.HBM(land.shape, land.dtype),
        in_specs=[_HBM, _SEM, _SEM, pl.BlockSpec(memory_space=pl.ANY)],
        out_specs=_HBM,
        input_output_aliases={0: 0},
        compiler_params=pltpu.CompilerParams(has_side_effects=_EFFECT),
    )(land, sems[0], sems[1], after)


def _row_step(m):
    return next(s for s in (32, 24, 16, 8) if m % s == 0)


def _pair_reduce(arrs, name):
    na = len(arrs)

    def body(*refs):
        gs, hs, hm = refs[:na], refs[na:2 * na], refs[2 * na:3 * na]
        own, ra = refs[3 * na:4 * na], refs[4 * na:5 * na]
        d2d_send, d2d_recv, local_sems = refs[5 * na:]
        x, y, c = _mesh_pos()
        sibling = (x, y, 1 - c)
        loads, sends = [], []
        for a in range(na):
            for q in range(4):
                cp = pltpu.make_async_copy(gs[a].at[2 * q + c], own[a].at[q], local_sems.at[a, q])
                cp.start()
                loads.append(cp)
                cp = pltpu.make_async_remote_copy(
                    src_ref=gs[a].at[2 * q + (1 - c)], dst_ref=ra[a].at[q], send_sem=d2d_send.at[a, q],
                    recv_sem=d2d_recv.at[a, q], device_id=sibling, device_id_type=MESH)
                cp.start()
                sends.append(cp)
        for cp in loads:
            cp.wait()
        for cp in sends:
            cp.wait_recv()
        others = [2 * (1 - x) + y, 2 * x + (1 - y), 2 * (1 - x) + (1 - y)]
        for a in range(na):
            m = arrs[a].shape[1]
            step = _row_step(m)

            def add(i, carry, a=a, step=step):
                rs = pl.ds(pl.multiple_of(i * step, step), step)
                for j, q in enumerate(others):
                    hs[a][j, rs, :] = (own[a][q, rs, :].astype(F32) + ra[a][q, rs, :].astype(F32)).astype(hs[a].dtype)
                q = 2 * x + y
                hm[a][rs, :] = own[a][q, rs, :].astype(F32) + ra[a][q, rs, :].astype(F32)
                return carry

            lax.fori_loop(0, m // step, add, 0)
        for cp in sends:
            cp.wait_send()

    vmem = pl.BlockSpec(memory_space=pltpu.VMEM)
    scratch = [pltpu.VMEM((4,) + t.shape[1:], t.dtype) for t in arrs] * 2
    scratch += [pltpu.SemaphoreType.DMA((na, 4)), pltpu.SemaphoreType.DMA((na, 4)), pltpu.SemaphoreType.DMA((na, 4))]
    out = pl.pallas_call(
        body, name=name,
        in_specs=[pl.BlockSpec(memory_space=pl.ANY)] * na, out_specs=[vmem] * (2 * na),
        out_shape=([jax.ShapeDtypeStruct((3,) + t.shape[1:], t.dtype) for t in arrs]
                   + [jax.ShapeDtypeStruct(t.shape[1:], F32) for t in arrs]),
        scratch_shapes=scratch,
        compiler_params=_compiler_params(),
    )(*arrs)
    return out[:na], out[na:]


_HBM = pl.BlockSpec(memory_space=pltpu.HBM)
_SEM = pl.BlockSpec(memory_space=pltpu.SEMAPHORE)
_EFFECT = pltpu.SideEffectType.DATAFLOW_SIDE_EFFECTING


def _exchange_plan(direct):
    x, y, c = _mesh_pos()
    if not direct:
        return [(j, j, (qx, qy, c)) for j, (qx, qy) in enumerate([(1 - x, y), (x, 1 - y), (1 - x, 1 - y)])]
    plan = []
    for k in range(1, N_DEV):
        px, py, pc = x ^ ((k >> 2) & 1), y ^ ((k >> 1) & 1), c ^ (k & 1)
        plan.append((4 * px + 2 * py + pc, k - 1, (px, py, pc)))
    return plan


def _exchange_start(srcs, direct, name):
    na = len(srcs)
    slots = N_DEV - 1 if direct else 3

    def body(*refs):
        src, land = refs[:na], refs[na:2 * na]
        send_sem, recv_sem = refs[2 * na], refs[2 * na + 1]
        token = refs[-1]
        for block, slot, peer in _exchange_plan(direct):
            for a in range(na):
                pltpu.make_async_remote_copy(
                    src_ref=src[a].at[block], dst_ref=land[a].at[slot], send_sem=send_sem.at[slots * a + slot],
                    recv_sem=recv_sem.at[slots * a + slot], device_id=peer, device_id_type=MESH).start()
        token[...] = jnp.zeros_like(token)

    zones = [jax.ShapeDtypeStruct((slots,) + t.shape[1:], t.dtype) for t in srcs]
    bufs = [pltpu.HBM(t.shape, t.dtype) for t in list(srcs) + zones]
    out = pl.pallas_call(
        body, name=name,
        out_shape=(pltpu.SemaphoreType.DMA((slots * na,)), pltpu.SemaphoreType.DMA((slots * na,)), *bufs,
                   jax.ShapeDtypeStruct((8, 128), F32)),
        in_specs=[_HBM] * (2 * na),
        out_specs=(_SEM, _SEM, *([_HBM] * (2 * na)), pl.BlockSpec(memory_space=pltpu.VMEM)),
        input_output_aliases={i: 2 + i for i in range(2 * na)},
        compiler_params=pltpu.CompilerParams(has_side_effects=_EFFECT),
    )(*[pltpu.with_memory_space_constraint(t, pltpu.HBM) for t in srcs],
      *[pltpu.with_memory_space_constraint(lax.empty(t.shape, t.dtype), pltpu.HBM) for t in zones])
    return out[0], out[1], out[2:2 + na], out[2 + na:2 + 2 * na], out[-1]


def _exchange_wait(send_sem, recv_sem, srcs, lands, direct, after, name):
    na = len(srcs)
    slots = N_DEV - 1 if direct else 3

    def body(*refs):
        src, land = refs[:na], refs[na:2 * na]
        send_sem_ref, recv_sem_ref = refs[2 * na], refs[2 * na + 1]
        for block, slot, peer in _exchange_plan(direct):
            for a in range(na):
                cp = pltpu.make_async_remote_copy(
                    src_ref=src[a].at[block], dst_ref=land[a].at[slot], send_sem=send_sem_ref.at[slots * a + slot],
                    recv_sem=recv_sem_ref.at[slots * a + slot], device_id=peer, device_id_type=MESH)
                cp.wait_send()
                cp.wait_recv()

    bufs = [pltpu.HBM(t.shape, t.dtype) for t in list(srcs) + list(lands)]
    out = pl.pallas_call(
        body, name=name,
        out_shape=tuple(bufs),
        in_specs=[_HBM] * (2 * na) + [_SEM, _SEM, pl.BlockSpec(memory_space=pl.ANY)],
        out_specs=[_HBM] * (2 * na),
        input_output_aliases={i: i for i in range(2 * na)},
        compiler_params=pltpu.CompilerParams(has_side_effects=_EFFECT),
    )(*srcs, *lands, send_sem, recv_sem, after)
    return out[:na], out[na:]


def _own_then_slots(mine_ref, lands_ref, rows=slice(None)):
    if len(mine_ref.shape) == 3:
        x, y, c = _mesh_pos()
        total = mine_ref[4 * x + 2 * y + c, rows, :].astype(F32)
    else:
        total = mine_ref[rows, :].astype(F32)
    for j in range(lands_ref.shape[0]):
        total = total + lands_ref[j, rows, :].astype(F32)
    return total


SMALL_ROWS = 2 * PACK_SLICE + 2 * 8


def _small_block(mine, lands, dgpre, name):
    def body(*refs):
        hm, ld, dg = refs[:DEPTH], refs[DEPTH:2 * DEPTH], refs[2 * DEPTH:3 * DEPTH]
        blk, land, sem = refs[3 * DEPTH:]
        for l in range(DEPTH):
            blk[PACK_SLICE * l:PACK_SLICE * (l + 1), :] = _own_then_slots(hm[l], ld[l])
            blk[2 * PACK_SLICE + 8 * l:2 * PACK_SLICE + 8 * (l + 1), :] = dg[l][...]
        cp = pltpu.make_async_copy(blk, _device_rows(land, SMALL_ROWS, *_mesh_pos()), sem)
        cp.start()
        cp.wait()

    vmem = pl.BlockSpec(memory_space=pltpu.VMEM)
    return pl.pallas_call(
        body, name=name,
        in_specs=[vmem] * (3 * DEPTH), out_specs=[vmem, pl.BlockSpec(memory_space=pl.ANY)],
        out_shape=[jax.ShapeDtypeStruct((SMALL_ROWS, 128), F32), jax.ShapeDtypeStruct((N_DEV * SMALL_ROWS, 128), F32)],
        scratch_shapes=[pltpu.SemaphoreType.DMA],
        compiler_params=_compiler_params(),
    )(*mine, *lands, *dgpre)


def _adamw_math(w, g, m, v):
    m = ADAM_B1 * m + (1.0 - ADAM_B1) * g
    v = ADAM_B2 * v + (1.0 - ADAM_B2) * (g * g)
    m_hat = m / (1.0 - ADAM_B1 ** ADAM_STEP)
    v_hat = v / (1.0 - ADAM_B2 ** ADAM_STEP)
    delta = -ADAM_LR * (m_hat / (jnp.sqrt(v_hat) + ADAM_EPS) + ADAM_WD * w)
    return delta, m, v


def _adamw_layer(layer, mine, lands, w, m, v, earlier, token, name, rows):
    _, mm, nn = w.shape

    def body(hm_ref, ld_ref, w_ref, m_ref, v_ref, _, *refs):
        g_ref, d_ref, nm_ref, nv_ref = refs[-4:]
        g = _own_then_slots(hm_ref, ld_ref)
        g_ref[...] = g
        d, nm, nv = _adamw_math(w_ref[...], g, m_ref[...], v_ref[...])
        d_ref[...] = d
        nm_ref[...] = nm
        nv_ref[...] = nv

    spec = pl.BlockSpec((None, rows, nn), lambda i: (layer, i, 0))
    carried = [] if earlier is None else list(earlier)
    return pl.pallas_call(
        body, name=name, grid=(mm // rows,),
        in_specs=([pl.BlockSpec((rows, nn), lambda i: (i, 0)) if mine.ndim == 2
                   else pl.BlockSpec((None, rows, nn), lambda i: (_device_index(), i, 0)),
                   pl.BlockSpec((lands.shape[0], rows, nn), lambda i: (0, i, 0)),
                   spec, spec, spec] + [pl.BlockSpec(memory_space=pl.ANY)] * (1 + len(carried))),
        out_specs=[spec] * 4,
        out_shape=[jax.ShapeDtypeStruct(w.shape, F32)] * 4,
        input_output_aliases={6 + t: t for t in range(len(carried))},
        compiler_params=_compiler_params(("arbitrary",)),
    )(mine, lands, w, m, v, token, *carried)


def _adamw_small(gathered, params):
    def body(all_ref, *refs):
        ins, outs, packs = refs[:15], refs[15:15 + 21], refs[15 + 21]
        loss_ref = outs[0]
        for dev in range(N_DEV):
            for l in range(DEPTH):
                packs[l, PACK_SLICE * dev:PACK_SLICE * (dev + 1), :] = (
                    all_ref[SMALL_ROWS * dev + PACK_SLICE * l:SMALL_ROWS * dev + PACK_SLICE * (l + 1), :])
        loss_ref[...] = packs[DEPTH - 1, ROW_LOSS:ROW_LOSS + 1, 0:1]

        def update(p, sel, g):
            w_ref, m_ref, v_ref = ins[p], ins[5 + p], ins[10 + p]
            d, nm, nv = _adamw_math(w_ref[sel], g, m_ref[sel], v_ref[sel])
            for t, val in enumerate((g, d, nm, nv)):
                outs[1 + 5 * t + p][sel] = val

        for l in range(DEPTH):
            gp = packs.at[l]
            row0 = 2 * PACK_SLICE + 8 * l
            dgpre = all_ref[row0:row0 + 8, :]
            for dev in range(1, N_DEV):
                dgpre = dgpre + all_ref[SMALL_ROWS * dev + row0:SMALL_ROWS * dev + row0 + 8, :]
            for grp in range(4):
                update(0, (l, grp), gp[ROW_PW + BLOCK * grp:ROW_PW + BLOCK * (grp + 1), :])
                update(1, (slice(l, l + 1), slice(128 * grp, 128 * (grp + 1))), gp[ROW_SC + grp:ROW_SC + grp + 1, :])
            update(2, (slice(l, l + 1), slice(None)), gp[ROW_SINK:ROW_SINK + 1, 0:N_HEADS])
            for r in range(D_MODEL // 128):
                sel = (slice(l, l + 1), slice(128 * r, 128 * (r + 1)))
                update(3, sel, dgpre[r:r + 1, :])
                update(4, sel, gp[ROW_NPOST + r:ROW_NPOST + r + 1, :])

    shapes = [jax.ShapeDtypeStruct(p.shape, F32) for p in params[:5]]
    return pl.pallas_call(
        body, name="adamw_small",
        out_shape=[jax.ShapeDtypeStruct((1, 1), F32)] + shapes * 4,
        scratch_shapes=[pltpu.VMEM((DEPTH, PACK_ROWS, 128), F32)],
        compiler_params=_compiler_params(),
    )(gathered, *params)


def kernel(x, w_in, pool_w, pool_scale, attn_sinks, w_out, norm_pre, norm_post, loss_target, m_w_in, m_pool_w, m_pool_scale, m_attn_sinks, m_w_out, m_norm_pre, m_norm_post, v_w_in, v_pool_w, v_pool_scale, v_attn_sinks, v_w_out, v_norm_pre, v_norm_post):
    x0 = x.reshape(SEQ, D_MODEL)
    target = loss_target.reshape(SEQ, D_MODEL)
    bias = jnp.asarray(_attn_bias())
    w_in_t, m_in_t, v_in_t = (jnp.swapaxes(t, 1, 2) for t in (w_in, m_w_in, v_w_in))

    (win0, wout0), later, lands = _allgather([(w_in_t, 0), (w_out, 0)], BF16, "gather_w0",
                                              later=[(w_in_t, 1), (w_out, 1)])
    sems, later, lands, token = _gather_start(later, lands, [SIBLING_AND_SAME_CORE, ALL_PEERS], "gather_w1_start")
    win_full, wout_full = [win0, None], [wout0, None]

    saved = []
    xl = x0
    for layer in range(DEPTH):
        u, pg, q, k, v, ag, z, a = _fwd_front(layer, xl, norm_pre, win_full[layer], token, attn_sinks,
                                             pool_w, pool_scale, bias)
        if layer == 0:
            land = _gather_wait(sems[0], later[0], lands[0], SIBLING_AND_SAME_CORE, z, "gather_w_in1_wait")
            fsems, land, token = _forward_start(land, IN_SHARD, "forward_w_in1_start")
        else:
            wout_full[layer] = _gather_wait(sems[1], later[1], lands[1], ALL_PEERS, z, "gather_w_out1_wait")
        x_next, y = _fwd_out(layer, z, xl, norm_post, wout_full[layer], token)
        if layer == 0:
            win_full[1] = _forward_wait(fsems, land, IN_SHARD, x_next, "forward_w_in1_wait")
        saved.append((xl, u, pg, q, k, v, ag, z, a, y))
        xl = x_next

    params_small = [pool_w, pool_scale, attn_sinks, norm_pre, norm_post,
                    m_pool_w, m_pool_scale, m_attn_sinks, m_norm_pre, m_norm_post,
                    v_pool_w, v_pool_scale, v_attn_sinks, v_norm_pre, v_norm_post]

    def start(blocks, direct, tag):
        srcs, mine = (blocks, None) if direct else _pair_reduce(blocks, f"pair_reduce{tag}")
        send_sem, recv_sem, srcs, lands, started = _exchange_start(srcs, direct, f"exchange_start{tag}")
        return (send_sem, recv_sem, srcs, lands, mine, direct), started

    def finish(handle, after, tag):
        send_sem, recv_sem, srcs, lands, mine, direct = handle
        srcs, lands = _exchange_wait(send_sem, recv_sem, srcs, lands, direct, after, f"exchange_wait{tag}")
        return (srcs if mine is None else mine), lands

    def back(layer, top, first, second):
        xin, u, pg, q, k, v, ag, z, a, y = saved[layer]
        return _bwd_back(layer, top, first, second, y, z, norm_post, wout_full[layer], attn_sinks, u, pg, q, k, v,
                         ag, a, pool_w, pool_scale, bias)

    dgpre = [None] * DEPTH
    dx, dproj, gw_out, pack = back(1, True, xl, target)
    dx, dgpre[1], gw_in_t = _bwd_in(1, "both", token, dproj, saved[1][0], norm_pre, dx, win_full[1])
    top, token = start([gw_in_t.reshape(N_DEV, IN_SHARD, D_MODEL), gw_out.reshape(N_DEV, OUT_SHARD, D_MODEL),
                        pack.reshape(N_DEV, PACK_SLICE, 128)], True, "1")

    dproj, gw_out, pack = back(0, False, dx, token)
    early, token = start([gw_out.reshape(N_DEV, OUT_SHARD, D_MODEL), pack.reshape(N_DEV, PACK_SLICE, 128)], True, "0a")
    (gw_in_t,) = _bwd_in(0, "dw", token, dproj, saved[0][0], norm_pre)
    late, token = start([gw_in_t.reshape(N_DEV, IN_SHARD, D_MODEL)], False, "0b")
    dx, dgpre[0] = _bwd_in(0, "dx", token, dproj, saved[0][0], norm_pre, dx, win_full[0])

    own1, lands1 = finish(top, dx, "1")
    big_in = _adamw_layer(1, own1[0], lands1[0], w_in_t, m_in_t, v_in_t, None, token, "adamw_in1", ADAM_ROWS_IN)
    big_out = _adamw_layer(1, own1[1], lands1[1], w_out, m_w_out, v_w_out, None, token, "adamw_out1", ADAM_ROWS_OUT)
    own0a, lands0a = finish(early, big_out[0], "0a")
    block, land = _small_block([own0a[1], own1[2]], [lands0a[1], lands1[2]], dgpre, "small_block")
    sems, block, land, token = _gather_start([block], [land], [ALL_PEERS], "gather_small_start")
    big_out = _adamw_layer(0, own0a[0], lands0a[0], w_out, m_w_out, v_w_out, big_out, token, "adamw_out0", ADAM_ROWS_OUT)
    own0b, lands0b = finish(late, big_out[0], "0b")
    big_in = _adamw_layer(0, own0b[0], lands0b[0], w_in_t, m_in_t, v_in_t, big_in, token, "adamw_in0", ADAM_ROWS_IN)
    gathered = _gather_wait(sems[0], block[0], land[0], ALL_PEERS, big_in[0], "gather_small_wait")
    small_out = _adamw_small(gathered, params_small)
    loss = small_out[0].reshape(())

    outs = [loss, dx.reshape(1, SEQ, D_MODEL)]
    for t in range(4):
        pw_, sc_, sk_, npre_, npost_ = small_out[1 + 5 * t:6 + 5 * t]
        outs += [jnp.swapaxes(big_in[t], 1, 2), pw_, sc_, sk_, big_out[t], npre_, npost_]
    return tuple(outs)
```

```python
import numpy as np
import jax
import jax.numpy as jnp
from jax import lax
from jax.experimental import pallas as pl
from jax.experimental.pallas import tpu as pltpu

F32 = jnp.float32
BF16 = jnp.bfloat16

N_DEV = 8
SEQ = 2048
D_MODEL = 1024
D_POOL = 512
D_ATTN = 512
D_KV = 128
D_IN = 2304
N_HEADS = 8
GQA = 4
HEAD_DIM = 64
BLOCK = 128
N_BLOCKS = SEQ // BLOCK
POOL_WINDOWS = (2, 4, 8, 16)
DEPTH = 2
EPS = 1e-6
NEG_INF = -1e30
SCALE = HEAD_DIM ** -0.5
IN_SHARD = D_IN // N_DEV
OUT_SHARD = D_MODEL // N_DEV

COL_U, COL_PG, COL_Q, COL_K, COL_V, COL_AG = 0, 512, 1024, 1536, 1664, 1792

ADAM_LR = 0.001
ADAM_B1 = 0.9
ADAM_B2 = 0.999
ADAM_EPS = 1e-08
ADAM_WD = 0.01
ADAM_STEP = 10

TOKEN_TILE = 512
ADAM_ROWS_IN, ADAM_ROWS_OUT = 144, 128
VMEM_LIMIT = 56 * 1024 * 1024
MESH = pl.DeviceIdType.MESH

ROW_PW, ROW_SC, ROW_SINK, ROW_NPRE, ROW_NPOST, ROW_LOSS = 0, 512, 520, 528, 536, 544
PACK_ROWS = 576
PACK_SLICE = PACK_ROWS // N_DEV


def _nn(a, b):
    return jnp.dot(a, b, preferred_element_type=F32)


def _nt(a, b):
    return lax.dot_general(a, b, (((1,), (1,)), ((), ())), preferred_element_type=F32)


def _tn(a, b):
    return lax.dot_general(a, b, (((0,), (0,)), ((), ())), preferred_element_type=F32)


def _silu_parts(g):
    s = jax.nn.sigmoid(g)
    return g * s, s * (1.0 + g * (1.0 - s))


def _resident(shape):
    return pl.BlockSpec(shape, lambda *_: (0,) * len(shape), pipeline_mode=pl.Buffered(1))


def _compiler_params(sem=None):
    if sem is None:
        return pltpu.CompilerParams(vmem_limit_bytes=VMEM_LIMIT)
    return pltpu.CompilerParams(dimension_semantics=sem, vmem_limit_bytes=VMEM_LIMIT)


def _attn_bias():
    t = np.arange(BLOCK)[None, :]
    j = np.arange(BLOCK)[:, None]
    current = j <= t
    dist = np.where(current, t - j, t + BLOCK - j).astype(np.float32)
    out = np.zeros((2, 2, BLOCK, GQA * BLOCK), np.float32)
    for variant in range(2):
        valid = current | (variant == 1)
        for kv in range(2):
            for g in range(GQA):
                slope = np.float32(2.0 ** (-(kv * GQA + g + 1)))
                out[variant, kv, :, g * BLOCK:(g + 1) * BLOCK] = np.where(valid, -slope * dist, np.float32(NEG_INF))
    return out


def _replicate_head(kx, kv):
    rolled = pltpu.roll(kx, 64, 1)
    lane = lax.broadcasted_iota(jnp.int32, kx.shape, 1)
    dup = jnp.where(lane < 64, kx, rolled) if kv == 0 else jnp.where(lane < 64, rolled, kx)
    return jnp.concatenate([dup, dup], axis=1).astype(BF16)


def _stack_heads(qv):
    lane = lax.broadcasted_iota(jnp.int32, qv.shape, 1)
    zero = jnp.zeros_like(qv)
    return jnp.concatenate([jnp.where((lane >= 64 * g) & (lane < 64 * g + 64), qv, zero) for g in range(GQA)], axis=0)


def _unstack_heads(xs):
    lane = lax.broadcasted_iota(jnp.int32, (BLOCK, 256), 1)
    return jnp.where(lane < 64, xs[0:128], jnp.where(lane < 128, xs[128:256], jnp.where(lane < 192, xs[256:384], xs[384:512])))


def _fold_heads(r):
    h = r[:, 0:128] + r[:, 128:256]
    return h + pltpu.roll(h, 64, 1)


def _sink_row(sink_ref, layer, kv):
    lane = lax.broadcasted_iota(jnp.int32, (1, GQA * BLOCK), 1)
    s4 = [sink_ref[layer, kv * GQA + g] for g in range(GQA)]
    return jnp.where(lane < 128, s4[0], jnp.where(lane < 256, s4[1], jnp.where(lane < 384, s4[2], s4[3])))


def _band_is_current():
    j = lax.broadcasted_iota(jnp.int32, (BLOCK, GQA * BLOCK), 0)
    t = lax.broadcasted_iota(jnp.int32, (BLOCK, GQA * BLOCK), 1) & (BLOCK - 1)
    return j <= t


def _pack_band(full, current):
    return jnp.where(current, full[BLOCK:], full[:BLOCK])


def _unpack_band(packed, current):
    zero = jnp.zeros_like(packed)
    return jnp.concatenate([jnp.where(current, zero, packed), jnp.where(current, packed, zero)], axis=0)


def _probs_keys_major(k_rep, q_st, bias, sink, current):
    st = _pack_band(_nt(k_rep, q_st), current) * SCALE + bias
    m = jnp.maximum(jnp.max(st, axis=0, keepdims=True), sink)
    p = jnp.exp(st - m)
    esink = jnp.exp(sink - m)
    rl = 1.0 / (jnp.sum(p, axis=0, keepdims=True) + esink)
    return p * rl, esink * rl


WINDOW_HALO = 16


def _window_sum(ext, w, forward):
    s = ext
    sh = 1
    while sh < w:
        s = s + pltpu.roll(s, (ext.shape[0] - sh) if forward else sh, 0)
        sh *= 2
    return s


def _inv_count(n, w):
    t = n * BLOCK + lax.broadcasted_iota(jnp.int32, (BLOCK, 1), 0) + 1
    return 1.0 / jnp.minimum(t.astype(F32), float(w))


def _kv_ext(ref, n):
    r0 = pl.multiple_of(jnp.maximum(n - 1, 0) * BLOCK, BLOCK)
    r1 = pl.multiple_of(n * BLOCK, BLOCK)
    return jnp.concatenate([ref[pl.ds(r0, BLOCK), :], ref[pl.ds(r1, BLOCK), :]], axis=0)


def _rows_of(vec_ref, pack_ref, row0):
    for r in range(D_MODEL // 128):
        pack_ref[row0 + r:row0 + r + 1, :] = vec_ref[:, 128 * r:128 * (r + 1)]


FRONT_TILE = 2 * BLOCK


def _fwd_front(layer, x, norm_pre, w_in_t, token, sinks, pool_w, pool_scale, bias):
    tm = FRONT_TILE

    def body(sink_ref, x_ref, g_ref, w_ref, _, pw_ref, sc_ref, bias_ref,
             u_ref, pg_ref, q_ref, k_ref, v_ref, ag_ref, z_ref, a_ref, uprev, kprev, vprev):
        i = pl.program_id(0)

        @pl.when(i == 0)
        def _():
            uprev[...] = jnp.zeros_like(uprev)
            kprev[...] = jnp.zeros_like(kprev)
            vprev[...] = jnp.zeros_like(vprev)

        xv = x_ref[...]
        r = lax.rsqrt(jnp.mean(xv * xv, axis=-1, keepdims=True) + EPS)
        h = (xv * r * g_ref[layer:layer + 1, :]).astype(BF16)
        u_ref[...] = _nt(h, w_ref[COL_U:COL_PG, :])
        pg_ref[...] = _nt(h, w_ref[COL_PG:COL_Q, :])
        for sb in range(tm // BLOCK):
            n = (tm // BLOCK) * i + sb
            rows = slice(BLOCK * sb, BLOCK * (sb + 1))
            before = slice(BLOCK * (sb - 1), BLOCK * sb)
            uv = u_ref[rows, :]
            halo = (uprev[BLOCK - WINDOW_HALO:, :] if sb == 0
                    else u_ref[BLOCK * sb - WINDOW_HALO:BLOCK * sb, :])
            ext = jnp.concatenate([halo, uv], axis=0)
            for g, w in enumerate(POOL_WINDOWS):
                cs = slice(BLOCK * g, BLOCK * (g + 1))
                win = _window_sum(ext[:, cs], w, forward=False)[WINDOW_HALO:]
                pooled = win * _inv_count(n, w) - uv[:, cs]
                mixed = _nn(pooled.astype(BF16), pw_ref[g].astype(BF16))
                gate, _ = _silu_parts(pg_ref[rows, cs])
                z_ref[rows, cs] = (mixed * sc_ref[layer:layer + 1, cs] * gate).astype(BF16)

        q_ref[...] = _nt(h, w_ref[COL_Q:COL_K, :]).astype(BF16)
        k_ref[...] = _nt(h, w_ref[COL_K:COL_V, :])
        v_ref[...] = _nt(h, w_ref[COL_V:COL_AG, :])
        ag_ref[...] = _nt(h, w_ref[COL_AG:D_IN, :])

        current = _band_is_current()
        for sb in range(tm // BLOCK):
            n = (tm // BLOCK) * i + sb
            rows = slice(BLOCK * sb, BLOCK * (sb + 1))
            before = slice(BLOCK * (sb - 1), BLOCK * sb)
            kx = jnp.concatenate([kprev[...] if sb == 0 else k_ref[before, :], k_ref[rows, :]], axis=0)
            vx = jnp.concatenate([vprev[...] if sb == 0 else v_ref[before, :], v_ref[rows, :]], axis=0)
            variant = jnp.minimum(n, 1) if sb == 0 else 1
            for kv in range(2):
                cs = slice(256 * kv, 256 * (kv + 1))
                p, _ = _probs_keys_major(_replicate_head(kx, kv), _stack_heads(q_ref[rows, cs]),
                                         bias_ref[variant, kv], _sink_row(sink_ref, layer, kv), current)
                o = _unstack_heads(_tn(_unpack_band(p.astype(BF16), current), _replicate_head(vx, kv)))
                a_ref[rows, cs] = o
                gate, _ = _silu_parts(ag_ref[rows, cs])
                z_ref[rows, D_POOL + 256 * kv:D_POOL + 256 * (kv + 1)] = (o * gate).astype(BF16)

        tail = slice(tm - BLOCK, tm)
        uprev[...] = u_ref[tail, :]
        kprev[...] = k_ref[tail, :]
        vprev[...] = v_ref[tail, :]

    row = lambda c: pl.BlockSpec((tm, c), lambda i: (i, 0))
    const = lambda shape: pl.BlockSpec(shape, lambda i: (0,) * len(shape))
    return pl.pallas_call(
        body, name=f"fwd_front{layer}", grid=(SEQ // tm,),
        in_specs=[pl.BlockSpec(memory_space=pltpu.SMEM), row(D_MODEL), const((DEPTH, D_MODEL)),
                  _resident((D_IN, D_MODEL)), const((8, 128)),
                  pl.BlockSpec((None, 4, BLOCK, BLOCK), lambda i: (layer, 0, 0, 0)), const((DEPTH, D_POOL)),
                  _resident((2, 2, BLOCK, GQA * BLOCK))],
        out_specs=[row(D_POOL), row(D_POOL), row(D_ATTN), row(D_KV), row(D_KV), row(D_ATTN), row(D_MODEL),
                   row(D_ATTN)],
        out_shape=[jax.ShapeDtypeStruct((SEQ, D_POOL), F32), jax.ShapeDtypeStruct((SEQ, D_POOL), F32),
                   jax.ShapeDtypeStruct((SEQ, D_ATTN), BF16), jax.ShapeDtypeStruct((SEQ, D_KV), F32),
                   jax.ShapeDtypeStruct((SEQ, D_KV), F32), jax.ShapeDtypeStruct((SEQ, D_ATTN), F32),
                   jax.ShapeDtypeStruct((SEQ, D_MODEL), BF16), jax.ShapeDtypeStruct((SEQ, D_ATTN), F32)],
        scratch_shapes=[pltpu.VMEM((BLOCK, D_POOL), F32), pltpu.VMEM((BLOCK, D_KV), F32),
                        pltpu.VMEM((BLOCK, D_KV), F32)],
        compiler_params=_compiler_params(("arbitrary",)),
    )(sinks, x, norm_pre, w_in_t, token, pool_w, pool_scale, bias)


def _fwd_out(layer, z, x, norm_post, w_out, token):
    tm = TOKEN_TILE

    def body(z_ref, x_ref, g_ref, w_ref, _, xn_ref, y_ref):
        y = _nn(z_ref[...], w_ref[...])
        y_ref[...] = y
        r = lax.rsqrt(jnp.mean(y * y, axis=-1, keepdims=True) + EPS)
        xn_ref[...] = x_ref[...] + y * r * g_ref[layer:layer + 1, :]

    row = lambda c: pl.BlockSpec((tm, c), lambda i: (i, 0))
    return pl.pallas_call(
        body, name=f"fwd_out{layer}", grid=(SEQ // tm,),
        in_specs=[row(D_MODEL), row(D_MODEL), pl.BlockSpec((DEPTH, D_MODEL), lambda i: (0, 0)),
                  _resident((D_MODEL, D_MODEL)), pl.BlockSpec((8, 128), lambda i: (0, 0))],
        out_specs=[row(D_MODEL), row(D_MODEL)],
        out_shape=[jax.ShapeDtypeStruct((SEQ, D_MODEL), F32), jax.ShapeDtypeStruct((SEQ, D_MODEL), F32)],
        compiler_params=_compiler_params(("arbitrary",)),
    )(z, x, norm_post, w_out, token)


BACK_TILE = 2 * BLOCK


def _bwd_back(layer, top, dxo_or_xf, target_or_token, y, z, norm_post, w_out, sinks, u, pg, q, k, v, ag, a,
              pool_w, pool_scale, bias):
    tm = BACK_TILE
    steps = SEQ // tm
    last = steps - 1
    per = tm // BLOCK

    def body(*refs):
        refs = list(refs)
        sink_ref, first, second = refs[:3]
        (y_ref, z_ref, g_ref, w_ref, u_ref, up_ref, pg_ref, q_ref, k_ref, v_ref, ag_ref, a_ref, pw_ref, sc_ref,
         bias_ref) = refs[3:18]
        del refs[:18]
        dxo_ref = refs.pop(0) if top else None
        dp_ref, dw_ref, pack_ref, acc, dg, lacc, dzs, ck, cv, ce = refs
        i = pl.program_id(0)
        blk = last - i

        @pl.when(i == 0)
        def _():
            acc[...] = jnp.zeros_like(acc)
            dg[...] = jnp.zeros_like(dg)
            lacc[...] = jnp.zeros_like(lacc)
            pack_ref[...] = jnp.zeros_like(pack_ref)
            ck[...] = jnp.zeros_like(ck)
            cv[...] = jnp.zeros_like(cv)
            ce[...] = jnp.zeros_like(ce)

        if top:
            d = first[...] - second[...]
            dxo_v = d * (1.0 / D_MODEL)
            dxo_ref[...] = dxo_v
            part = jnp.sum(d * d, axis=-1, keepdims=True) * (1.0 / D_MODEL)
            lacc[...] += 0.5 * jnp.sum(part, axis=0, keepdims=True)
        else:
            dxo_v = first[...]
        yv = y_ref[...]
        r = lax.rsqrt(jnp.mean(yv * yv, axis=-1, keepdims=True) + EPS)
        yn = yv * r
        dg[...] += jnp.sum(dxo_v * yn, axis=0, keepdims=True)
        dyn = dxo_v * g_ref[layer:layer + 1, :]
        dy = (r * (dyn - yn * jnp.mean(dyn * yn, axis=-1, keepdims=True))).astype(BF16)
        dzs[...] = _nt(dy, w_ref[...])
        acc[...] += _tn(z_ref[...], dy)

        lane = lax.broadcasted_iota(jnp.int32, (1, 128), 1)
        lane2 = lax.broadcasted_iota(jnp.int32, (256, 128), 1)
        current = _band_is_current()
        for sb in reversed(range(per)):
            n = per * blk + sb
            rows = slice(BLOCK * sb, BLOCK * (sb + 1))

            uv = u_ref[rows, :]
            if sb == 0:
                halo = up_ref[BLOCK - WINDOW_HALO:, :] * (n > 0).astype(F32)
            else:
                halo = u_ref[BLOCK * sb - WINDOW_HALO:BLOCK * sb, :]
            ext = jnp.concatenate([halo, uv], axis=0)
            for g, w in enumerate(POOL_WINDOWS):
                cs = slice(BLOCK * g, BLOCK * (g + 1))
                inv = _inv_count(n, w)
                win = _window_sum(ext[:, cs], w, forward=False)[WINDOW_HALO:]
                pooled = win * inv - uv[:, cs]
                pw_g = pw_ref[g].astype(BF16)
                mixed = _nn(pooled.astype(BF16), pw_g)
                gate, dgate = _silu_parts(pg_ref[rows, cs])
                dzp = dzs[rows, cs]
                sc = sc_ref[layer:layer + 1, cs]
                dpm = dzp * gate
                dp_ref[rows, COL_PG + BLOCK * g:COL_PG + BLOCK * (g + 1)] = (dzp * (mixed * sc) * dgate).astype(BF16)
                pack_ref[ROW_SC + g:ROW_SC + g + 1, :] += jnp.sum(dpm * mixed, axis=0, keepdims=True)
                dmixed = (dpm * sc).astype(BF16)
                pack_ref[ROW_PW + BLOCK * g:ROW_PW + BLOCK * (g + 1), :] += _tn(pooled.astype(BF16), dmixed)
                dpooled = _nt(dmixed, pw_g)
                e = dpooled * inv
                lead = _window_sum(jnp.concatenate([e, ce[:WINDOW_HALO, cs]], axis=0), w, forward=True)[:BLOCK]
                dp_ref[rows, COL_U + BLOCK * g:COL_U + BLOCK * (g + 1)] = (lead - dpooled).astype(BF16)
                ce[:, cs] = e

            kx = _kv_ext(k_ref, n)
            vx = _kv_ext(v_ref, n)
            variant = jnp.minimum(n, 1) if sb == 0 else 1
            dsink_row = jnp.zeros((1, 128), F32)
            tks, tvs = [], []
            for kv in range(2):
                cs = slice(256 * kv, 256 * (kv + 1))
                k_rep = _replicate_head(kx, kv)
                v_rep = _replicate_head(vx, kv)
                q_st = _stack_heads(q_ref[rows, cs])
                gate, dgate = _silu_parts(ag_ref[rows, cs])
                dza = dzs[rows, D_POOL + 256 * kv:D_POOL + 256 * (kv + 1)]
                dp_ref[rows, COL_AG + 256 * kv:COL_AG + 256 * (kv + 1)] = (dza * a_ref[rows, cs] * dgate).astype(BF16)
                da_st = _stack_heads((dza * gate).astype(BF16))
                p, psink = _probs_keys_major(k_rep, q_st, bias_ref[variant, kv], _sink_row(sink_ref, layer, kv),
                                             current)
                dpt = _pack_band(_nt(v_rep, da_st), current)
                delta = jnp.sum(p * dpt, axis=0, keepdims=True)
                dst = _unpack_band((p * (dpt - delta) * SCALE).astype(BF16), current)
                sink_terms = psink * delta
                for g in range(GQA):
                    dsink = -jnp.sum(sink_terms[:, BLOCK * g:BLOCK * (g + 1)], axis=1, keepdims=True)
                    dsink_row = dsink_row + jnp.where(lane == kv * GQA + g, dsink, 0.0)
                dp_ref[rows, COL_Q + 256 * kv:COL_Q + 256 * (kv + 1)] = _unstack_heads(_tn(dst, k_rep)).astype(BF16)
                tks.append(_fold_heads(_nn(dst, q_st)))
                tvs.append(_fold_heads(_nn(_unpack_band(p.astype(BF16), current), da_st)))
            pack_ref[ROW_SINK:ROW_SINK + 1, :] += dsink_row
            dkx = jnp.where(lane2 < 64, tks[0], tks[1])
            dvx = jnp.where(lane2 < 64, tvs[0], tvs[1])
            dp_ref[rows, COL_K:COL_V] = (ck[...] + dkx[BLOCK:]).astype(BF16)
            dp_ref[rows, COL_V:COL_AG] = (cv[...] + dvx[BLOCK:]).astype(BF16)
            ck[...] = dkx[:BLOCK]
            cv[...] = dvx[:BLOCK]

        @pl.when(i == steps - 1)
        def _():
            dw_ref[...] = acc[...].astype(BF16)
            _rows_of(dg, pack_ref, ROW_NPOST)
            pack_ref[ROW_LOSS:ROW_LOSS + 1, :] = jnp.where(lane == 0, lacc[...], 0.0)

    row = lambda c: pl.BlockSpec((tm, c), lambda i: (last - i, 0))
    const = lambda shape: pl.BlockSpec(shape, lambda i: (0,) * len(shape))
    act = jax.ShapeDtypeStruct((SEQ, D_MODEL), F32)
    return pl.pallas_call(
        body, name=f"bwd_back{layer}", grid=(steps,),
        in_specs=[pl.BlockSpec(memory_space=pltpu.SMEM), row(D_MODEL), row(D_MODEL) if top else const((8, 128)),
                  row(D_MODEL), row(D_MODEL), const((DEPTH, D_MODEL)), _resident((D_MODEL, D_MODEL)),
                  row(D_POOL), pl.BlockSpec((BLOCK, D_POOL), lambda i: (jnp.maximum(per * (last - i) - 1, 0), 0)),
                  row(D_POOL), row(D_ATTN), _resident((SEQ, D_KV)), _resident((SEQ, D_KV)), row(D_ATTN), row(D_ATTN),
                  pl.BlockSpec((None, 4, BLOCK, BLOCK), lambda i: (layer, 0, 0, 0)), const((DEPTH, D_POOL)),
                  _resident((2, 2, BLOCK, GQA * BLOCK))],
        out_specs=([row(D_MODEL)] * (1 if top else 0)
                   + [row(D_IN), const((D_MODEL, D_MODEL)), const((PACK_ROWS, 128))]),
        out_shape=([act] * (1 if top else 0)
                   + [jax.ShapeDtypeStruct((SEQ, D_IN), BF16), jax.ShapeDtypeStruct((D_MODEL, D_MODEL), BF16),
                      jax.ShapeDtypeStruct((PACK_ROWS, 128), F32)]),
        scratch_shapes=[pltpu.VMEM((D_MODEL, D_MODEL), F32), pltpu.VMEM((1, D_MODEL), F32), pltpu.VMEM((1, 1), F32),
                        pltpu.VMEM((tm, D_MODEL), F32), pltpu.VMEM((BLOCK, D_KV), F32), pltpu.VMEM((BLOCK, D_KV), F32),
                        pltpu.VMEM((BLOCK, D_POOL), F32)],
        compiler_params=_compiler_params(("arbitrary",)),
    )(sinks, dxo_or_xf, target_or_token, y, z, norm_post, w_out, u, u, pg, q, k, v, ag, a, pool_w, pool_scale, bias)


def _bwd_in(layer, part, token, dproj, x, norm_pre, dxo=None, w_in_t=None):
    want_dw, want_dx = part in ("both", "dw"), part in ("both", "dx")
    tm = TOKEN_TILE
    steps = SEQ // tm
    cw = 256

    def body(*refs):
        refs = list(refs)
        dp_ref, x_ref, g_ref = refs[1:4]
        del refs[:4]
        if want_dx:
            dxo_ref, w_ref, dx_ref, dgo_ref = refs[:4]
            del refs[:4]
            dg = refs.pop()
        if want_dw:
            dw_ref, acc = refs
        i = pl.program_id(0)

        @pl.when(i == 0)
        def _():
            if want_dw:
                acc[...] = jnp.zeros_like(acc)
            if want_dx:
                dg[...] = jnp.zeros_like(dg)

        xv = x_ref[...]
        gv = g_ref[layer:layer + 1, :]
        r = lax.rsqrt(jnp.mean(xv * xv, axis=-1, keepdims=True) + EPS)
        xn = xv * r
        if want_dw:
            hb = (xn * gv).astype(BF16)
            for c in range(0, D_IN, cw):
                acc[c:c + cw, :] += _tn(dp_ref[:, c:c + cw], hb)
        if want_dx:
            dh = _nn(dp_ref[...], w_ref[...])
            dg[...] += jnp.sum(dh * xn, axis=0, keepdims=True)
            dhn = dh * gv
            dx_ref[...] = dxo_ref[...] + r * (dhn - xn * jnp.mean(dhn * xn, axis=-1, keepdims=True))

        @pl.when(i == steps - 1)
        def _():
            if want_dw:
                dw_ref[...] = acc[...].astype(BF16)
            if want_dx:
                _rows_of(dg, dgo_ref, 0)

    row = lambda c: pl.BlockSpec((tm, c), lambda i: (i, 0))
    const = lambda shape: pl.BlockSpec(shape, lambda i: (0,) * len(shape))
    in_specs = [const((8, 128)), row(D_IN), row(D_MODEL), const((DEPTH, D_MODEL))]
    operands = [token, dproj, x, norm_pre]
    out_specs, out_shape, scratch = [], [], []
    if want_dx:
        in_specs += [row(D_MODEL), _resident((D_IN, D_MODEL))]
        operands += [dxo, w_in_t]
        out_specs += [row(D_MODEL), const((8, 128))]
        out_shape += [jax.ShapeDtypeStruct((SEQ, D_MODEL), F32), jax.ShapeDtypeStruct((8, 128), F32)]
    if want_dw:
        out_specs.append(const((D_IN, D_MODEL)))
        out_shape.append(jax.ShapeDtypeStruct((D_IN, D_MODEL), BF16))
        scratch.append(pltpu.VMEM((D_IN, D_MODEL), F32))
    if want_dx:
        scratch.append(pltpu.VMEM((1, D_MODEL), F32))
    return pl.pallas_call(
        body, name=f"bwd_in_{part}{layer}", grid=(steps,),
        in_specs=in_specs, out_specs=out_specs, out_shape=out_shape, scratch_shapes=scratch,
        compiler_params=_compiler_params(("arbitrary",)),
    )(*operands)


def _mesh_pos():
    return lax.axis_index("x"), lax.axis_index("y"), lax.axis_index("c")


def _device_index():
    x, y, c = _mesh_pos()
    return 4 * x + 2 * y + c


def _device_rows(ref, m, px, py, pc):
    return ref.at[pl.ds(pl.multiple_of((4 * px + 2 * py + pc) * m, 16 if m % 16 == 0 else 8), m), :]


def _allgather(srcs, out_dtype, name, later=()):
    na, nb = len(srcs), len(later)
    every = list(srcs) + list(later)
    shapes = [(a.shape[-2], a.shape[-1]) for a, _ in every]

    def body(*refs):
        xs, refs = refs[:na + nb], refs[na + nb:]
        outs, cast, land, refs = refs[:na], refs[na:na + nb], refs[na + nb:na + 2 * nb], refs[na + 2 * nb:]
        stage, (send_sems, recv_sems, local_sems) = refs[:na], refs[na:]
        x, y, c = _mesh_pos()
        me, sibling = (x, y, c), (x, y, 1 - c)
        near = [(1 - x, y), (x, 1 - y)]
        far = (1 - x, 1 - y)
        relay_from, relay_to = (x ^ (1 - c), y ^ c), (x ^ c, y ^ (1 - c))
        k_from, k_to = 1 + c, 2 - c

        def slot(a, px, py, pc):
            return _device_rows(outs[a], shapes[a][0], px, py, pc)

        def copy(a, k, block, to, src=None):
            return pltpu.make_async_remote_copy(
                src_ref=slot(a, *block) if src is None else src, dst_ref=slot(a, *block),
                send_sem=send_sems.at[a, k], recv_sem=recv_sems.at[a, k], device_id=to, device_id_type=MESH)

        def cast_block(i):
            layer = every[i][1]
            return (xs[i][...] if layer is None else xs[i][layer]).astype(out_dtype)

        for a in range(na):
            stage[a][...] = cast_block(a)
        mine = [pltpu.make_async_copy(stage[a], slot(a, *me), local_sems.at[a]) for a in range(na)]
        for cp in mine:
            cp.start()
        sent = []
        for a in range(na):
            sent.append(copy(a, 0, me, sibling, src=stage[a]))
            sent += [copy(a, 1 + j, me, (*chip, c), src=stage[a]) for j, chip in enumerate(near)]
        for cp in sent:
            cp.start()
        for b in range(nb):
            cast[b][...] = cast_block(na + b)
            cp = pltpu.make_async_copy(cast[b], _device_rows(land[b], shapes[na + b][0], *me), local_sems.at[na + b])
            cp.start()
            mine.append(cp)
        for a in range(na):
            copy(a, k_from, (*relay_from, c), me).wait_recv()
            sent += [copy(a, 3, (*relay_from, c), (*relay_to, c)), copy(a, 3 + k_from, (*relay_from, c), sibling)]
            sent[-2].start()
            sent[-1].start()
        for a in range(na):
            copy(a, k_to, (*relay_to, c), me).wait_recv()
            sent.append(copy(a, 3 + k_to, (*relay_to, c), sibling))
            sent[-1].start()
        for a in range(na):
            copy(a, 3, (*far, c), me).wait_recv()
            sent.append(copy(a, 6, (*far, c), sibling))
            sent[-1].start()
        for a in range(na):
            copy(a, 0, sibling, me).wait_recv()
            for j, chip in enumerate(near + [far]):
                copy(a, 4 + j, (*chip, 1 - c), me).wait_recv()
        for cp in sent:
            cp.wait_send()
        for cp in mine:
            cp.wait()

    vmem = pl.BlockSpec(memory_space=pltpu.VMEM)
    hbm = pl.BlockSpec(memory_space=pl.ANY)
    gathered = [jax.ShapeDtypeStruct((N_DEV * m, n), out_dtype) for m, n in shapes]
    out = pl.pallas_call(
        body, name=name,
        in_specs=[vmem] * (na + nb),
        out_specs=[hbm] * na + [vmem] * nb + [hbm] * nb,
        out_shape=gathered[:na] + [jax.ShapeDtypeStruct(s, out_dtype) for s in shapes[na:]] + gathered[na:],
        scratch_shapes=([pltpu.VMEM(s, out_dtype) for s in shapes[:na]]
                        + [pltpu.SemaphoreType.DMA((na, 7)), pltpu.SemaphoreType.DMA((na, 7)),
                           pltpu.SemaphoreType.DMA((na + nb,))]),
        compiler_params=_compiler_params(),
    )(*[a for a, _ in every])
    return out[:na], out[na:na + nb], out[na + nb:]


ALL_PEERS = tuple(range(1, N_DEV))
SIBLING_AND_SAME_CORE = (1, 2, 4, 6)


def _related(k, x, y, c):
    return x ^ ((k >> 2) & 1), y ^ ((k >> 1) & 1), c ^ (k & 1)


def _gather_start(blocks, lands, relations, collective_id, name):
    na = len(blocks)

    def body(*refs):
        src, land, sems, token = refs[:na], refs[na:2 * na], refs[2 * na:4 * na], refs[-1]
        x, y, c = _mesh_pos()
        _handshake([_related(k, x, y, c) for k in sorted(set().union(*relations))])
        for a in range(na):
            for k in relations[a]:
                pltpu.make_async_remote_copy(
                    src_ref=src[a], dst_ref=_device_rows(land[a], blocks[a].shape[0], x, y, c),
                    send_sem=sems[2 * a].at[k - 1], recv_sem=sems[2 * a + 1].at[k - 1],
                    device_id=_related(k, x, y, c), device_id_type=MESH).start()
        token[...] = jnp.zeros_like(token)

    bufs = [pltpu.HBM(t.shape, t.dtype) for t in list(blocks) + list(lands)]
    out = pl.pallas_call(
        body, name=name,
        out_shape=(*([pltpu.SemaphoreType.DMA((N_DEV - 1,))] * (2 * na)), *bufs, jax.ShapeDtypeStruct((8, 128), F32)),
        in_specs=[_HBM] * (2 * na),
        out_specs=(*([_SEM] * (2 * na)), *([_HBM] * (2 * na)), pl.BlockSpec(memory_space=pltpu.VMEM)),
        input_output_aliases={i: 2 * na + i for i in range(2 * na)},
        compiler_params=pltpu.CompilerParams(has_side_effects=_EFFECT, collective_id=collective_id),
    )(*[pltpu.with_memory_space_constraint(t, pltpu.HBM) for t in list(blocks) + list(lands)])
    sems = [(out[2 * a], out[2 * a + 1]) for a in range(na)]
    return sems, out[2 * na:3 * na], out[3 * na:4 * na], out[-1]


def _gather_wait(sems, block, land, relations, after, name):
    def body(src, land_ref, send_sem, recv_sem, after_ref, src_out, land_out):
        x, y, c = _mesh_pos()
        for k in relations:
            peer = _related(k, x, y, c)
            cp = pltpu.make_async_remote_copy(
                src_ref=src, dst_ref=_device_rows(land_ref, block.shape[0], *peer),
                send_sem=send_sem.at[k - 1], recv_sem=recv_sem.at[k - 1], device_id=peer, device_id_type=MESH)
            cp.wait_send()
            cp.wait_recv()

    out = pl.pallas_call(
        body, name=name,
        out_shape=(pltpu.HBM(block.shape, block.dtype), pltpu.HBM(land.shape, land.dtype)),
        in_specs=[_HBM, _HBM, _SEM, _SEM, pl.BlockSpec(memory_space=pl.ANY)],
        out_specs=[_HBM, _HBM],
        input_output_aliases={0: 0, 1: 1},
        compiler_params=pltpu.CompilerParams(has_side_effects=_EFFECT),
    )(block, land, sems[0], sems[1], after)
    return out[1]


(COLLECTIVE_GATHER_W1, COLLECTIVE_FORWARD_W_IN1, COLLECTIVE_EXCHANGE_1, COLLECTIVE_EXCHANGE_0A, COLLECTIVE_EXCHANGE_0B,
 COLLECTIVE_GATHER_SMALL) = range(1, 7)


def _handshake(peers):
    barrier = pltpu.get_barrier_semaphore()
    for peer in peers:
        pl.semaphore_signal(barrier, inc=1, device_id=peer, device_id_type=MESH)
    pl.semaphore_wait(barrier, len(peers))


def _forward_plan(land_ref, m):
    x, y, c = _mesh_pos()
    return [_device_rows(land_ref, m, qx, qy, c) for qx, qy in ((1 - x, y), (x, 1 - y), (1 - x, 1 - y))], (x, y, 1 - c)


def _forward_start(land, m, name):
    def body(land_ref, send_sem, recv_sem, land_out, token):
        _handshake([(lax.axis_index("x"), lax.axis_index("y"), 1 - lax.axis_index("c"))])
        rows, sibling = _forward_plan(land_ref, m)
        for j, r in enumerate(rows):
            pltpu.make_async_remote_copy(src_ref=r, dst_ref=r, send_sem=send_sem.at[j], recv_sem=recv_sem.at[j],
                                         device_id=sibling, device_id_type=MESH).start()
        token[...] = jnp.zeros_like(token)

    out = pl.pallas_call(
        body, name=name,
        out_shape=(pltpu.SemaphoreType.DMA((3,)), pltpu.SemaphoreType.DMA((3,)), pltpu.HBM(land.shape, land.dtype),
                   jax.ShapeDtypeStruct((8, 128), F32)),
        in_specs=[_HBM],
        out_specs=(_SEM, _SEM, _HBM, pl.BlockSpec(memory_space=pltpu.VMEM)),
        input_output_aliases={0: 2},
        compiler_params=pltpu.CompilerParams(has_side_effects=_EFFECT, collective_id=COLLECTIVE_FORWARD_W_IN1),
    )(pltpu.with_memory_space_constraint(land, pltpu.HBM))
    return (out[0], out[1]), out[2], out[3]


def _forward_wait(sems, land, m, after, name):
    def body(land_ref, send_sem, recv_sem, after_ref, land_out):
        x, y, c = _mesh_pos()
        mine, sibling = _forward_plan(land_ref, m)
        theirs = [_device_rows(land_ref, m, qx, qy, 1 - c) for qx, qy in ((1 - x, y), (x, 1 - y), (1 - x, 1 - y))]
        for j in range(3):
            cp = pltpu.make_async_remote_copy(src_ref=mine[j], dst_ref=theirs[j], send_sem=send_sem.at[j],
                                              recv_sem=recv_sem.at[j], device_id=sibling, device_id_type=MESH)
            cp.wait_send()
            cp.wait_recv()

    return pl.pallas_call(
        body, name=name,
        out_shape=pltpu.HBM(land.shape, land.dtype),
        in_specs=[_HBM, _SEM, _SEM, pl.BlockSpec(memory_space=pl.ANY)],
        out_specs=_HBM,
        input_output_aliases={0: 0},
        compiler_params=pltpu.CompilerParams(has_side_effects=_EFFECT),
    )(land, sems[0], sems[1], after)


def _row_step(m):
    return next(s for s in (32, 24, 16, 8) if m % s == 0)


def _pair_reduce(arrs, name):
    na = len(arrs)

    def body(*refs):
        gs, hs, hm = refs[:na], refs[na:2 * na], refs[2 * na:3 * na]
        own, ra = refs[3 * na:4 * na], refs[4 * na:5 * na]
        d2d_send, d2d_recv, local_sems = refs[5 * na:]
        x, y, c = _mesh_pos()
        sibling = (x, y, 1 - c)
        loads, sends = [], []
        for a in range(na):
            for q in range(4):
                cp = pltpu.make_async_copy(gs[a].at[2 * q + c], own[a].at[q], local_sems.at[a, q])
                cp.start()
                loads.append(cp)
                cp = pltpu.make_async_remote_copy(
                    src_ref=gs[a].at[2 * q + (1 - c)], dst_ref=ra[a].at[q], send_sem=d2d_send.at[a, q],
                    recv_sem=d2d_recv.at[a, q], device_id=sibling, device_id_type=MESH)
                cp.start()
                sends.append(cp)
        for cp in loads:
            cp.wait()
        for cp in sends:
            cp.wait_recv()
        others = [2 * (1 - x) + y, 2 * x + (1 - y), 2 * (1 - x) + (1 - y)]
        for a in range(na):
            m = arrs[a].shape[1]
            step = _row_step(m)

            def add(i, carry, a=a, step=step):
                rs = pl.ds(pl.multiple_of(i * step, step), step)
                for j, q in enumerate(others):
                    hs[a][j, rs, :] = (own[a][q, rs, :].astype(F32) + ra[a][q, rs, :].astype(F32)).astype(hs[a].dtype)
                q = 2 * x + y
                hm[a][rs, :] = own[a][q, rs, :].astype(F32) + ra[a][q, rs, :].astype(F32)
                return carry

            lax.fori_loop(0, m // step, add, 0)
        for cp in sends:
            cp.wait_send()

    vmem = pl.BlockSpec(memory_space=pltpu.VMEM)
    scratch = [pltpu.VMEM((4,) + t.shape[1:], t.dtype) for t in arrs] * 2
    scratch += [pltpu.SemaphoreType.DMA((na, 4)), pltpu.SemaphoreType.DMA((na, 4)), pltpu.SemaphoreType.DMA((na, 4))]
    out = pl.pallas_call(
        body, name=name,
        in_specs=[pl.BlockSpec(memory_space=pl.ANY)] * na, out_specs=[vmem] * (2 * na),
        out_shape=([jax.ShapeDtypeStruct((3,) + t.shape[1:], t.dtype) for t in arrs]
                   + [jax.ShapeDtypeStruct(t.shape[1:], F32) for t in arrs]),
        scratch_shapes=scratch,
        compiler_params=_compiler_params(),
    )(*arrs)
    return out[:na], out[na:]


_HBM = pl.BlockSpec(memory_space=pltpu.HBM)
_SEM = pl.BlockSpec(memory_space=pltpu.SEMAPHORE)
_EFFECT = pltpu.SideEffectType.DATAFLOW_SIDE_EFFECTING


def _exchange_plan(direct):
    x, y, c = _mesh_pos()
    if not direct:
        return [(j, j, (qx, qy, c)) for j, (qx, qy) in enumerate([(1 - x, y), (x, 1 - y), (1 - x, 1 - y)])]
    plan = []
    for k in range(1, N_DEV):
        px, py, pc = x ^ ((k >> 2) & 1), y ^ ((k >> 1) & 1), c ^ (k & 1)
        plan.append((4 * px + 2 * py + pc, k - 1, (px, py, pc)))
    return plan


def _exchange_start(srcs, direct, collective_id, name):
    na = len(srcs)
    slots = N_DEV - 1 if direct else 3

    def body(*refs):
        src, land = refs[:na], refs[na:2 * na]
        send_sem, recv_sem = refs[2 * na], refs[2 * na + 1]
        token = refs[-1]
        plan = _exchange_plan(direct)
        _handshake([peer for _, _, peer in plan])
        for block, slot, peer in plan:
            for a in range(na):
                pltpu.make_async_remote_copy(
                    src_ref=src[a].at[block], dst_ref=land[a].at[slot], send_sem=send_sem.at[slots * a + slot],
                    recv_sem=recv_sem.at[slots * a + slot], device_id=peer, device_id_type=MESH).start()
        token[...] = jnp.zeros_like(token)

    zones = [jax.ShapeDtypeStruct((slots,) + t.shape[1:], t.dtype) for t in srcs]
    bufs = [pltpu.HBM(t.shape, t.dtype) for t in list(srcs) + zones]
    out = pl.pallas_call(
        body, name=name,
        out_shape=(pltpu.SemaphoreType.DMA((slots * na,)), pltpu.SemaphoreType.DMA((slots * na,)), *bufs,
                   jax.ShapeDtypeStruct((8, 128), F32)),
        in_specs=[_HBM] * (2 * na),
        out_specs=(_SEM, _SEM, *([_HBM] * (2 * na)), pl.BlockSpec(memory_space=pltpu.VMEM)),
        input_output_aliases={i: 2 + i for i in range(2 * na)},
        compiler_params=pltpu.CompilerParams(has_side_effects=_EFFECT, collective_id=collective_id),
    )(*[pltpu.with_memory_space_constraint(t, pltpu.HBM) for t in srcs],
      *[pltpu.with_memory_space_constraint(lax.empty(t.shape, t.dtype), pltpu.HBM) for t in zones])
    return out[0], out[1], out[2:2 + na], out[2 + na:2 + 2 * na], out[-1]


def _exchange_wait(send_sem, recv_sem, srcs, lands, direct, after, name):
    na = len(srcs)
    slots = N_DEV - 1 if direct else 3

    def body(*refs):
        src, land = refs[:na], refs[na:2 * na]
        send_sem_ref, recv_sem_ref = refs[2 * na], refs[2 * na + 1]
        for block, slot, peer in _exchange_plan(direct):
            for a in range(na):
                cp = pltpu.make_async_remote_copy(
                    src_ref=src[a].at[block], dst_ref=land[a].at[slot], send_sem=send_sem_ref.at[slots * a + slot],
                    recv_sem=recv_sem_ref.at[slots * a + slot], device_id=peer, device_id_type=MESH)
                cp.wait_send()
                cp.wait_recv()

    bufs = [pltpu.HBM(t.shape, t.dtype) for t in list(srcs) + list(lands)]
    out = pl.pallas_call(
        body, name=name,
        out_shape=tuple(bufs),
        in_specs=[_HBM] * (2 * na) + [_SEM, _SEM, pl.BlockSpec(memory_space=pl.ANY)],
        out_specs=[_HBM] * (2 * na),
        input_output_aliases={i: i for i in range(2 * na)},
        compiler_params=pltpu.CompilerParams(has_side_effects=_EFFECT),
    )(*srcs, *lands, send_sem, recv_sem, after)
    return out[:na], out[na:]


def _own_then_slots(mine_ref, lands_ref, rows=slice(None)):
    if len(mine_ref.shape) == 3:
        x, y, c = _mesh_pos()
        total = mine_ref[4 * x + 2 * y + c, rows, :].astype(F32)
    else:
        total = mine_ref[rows, :].astype(F32)
    for j in range(lands_ref.shape[0]):
        total = total + lands_ref[j, rows, :].astype(F32)
    return total


SMALL_ROWS = 2 * PACK_SLICE + 2 * 8


def _small_block(mine, lands, dgpre, name):
    def body(*refs):
        hm, ld, dg = refs[:DEPTH], refs[DEPTH:2 * DEPTH], refs[2 * DEPTH:3 * DEPTH]
        blk, land, sem = refs[3 * DEPTH:]
        for l in range(DEPTH):
            blk[PACK_SLICE * l:PACK_SLICE * (l + 1), :] = _own_then_slots(hm[l], ld[l])
            blk[2 * PACK_SLICE + 8 * l:2 * PACK_SLICE + 8 * (l + 1), :] = dg[l][...]
        cp = pltpu.make_async_copy(blk, _device_rows(land, SMALL_ROWS, *_mesh_pos()), sem)
        cp.start()
        cp.wait()

    vmem = pl.BlockSpec(memory_space=pltpu.VMEM)
    return pl.pallas_call(
        body, name=name,
        in_specs=[vmem] * (3 * DEPTH), out_specs=[vmem, pl.BlockSpec(memory_space=pl.ANY)],
        out_shape=[jax.ShapeDtypeStruct((SMALL_ROWS, 128), F32), jax.ShapeDtypeStruct((N_DEV * SMALL_ROWS, 128), F32)],
        scratch_shapes=[pltpu.SemaphoreType.DMA],
        compiler_params=_compiler_params(),
    )(*mine, *lands, *dgpre)


def _adamw_math(w, g, m, v):
    m = ADAM_B1 * m + (1.0 - ADAM_B1) * g
    v = ADAM_B2 * v + (1.0 - ADAM_B2) * (g * g)
    m_hat = m / (1.0 - ADAM_B1 ** ADAM_STEP)
    v_hat = v / (1.0 - ADAM_B2 ** ADAM_STEP)
    delta = -ADAM_LR * (m_hat / (jnp.sqrt(v_hat) + ADAM_EPS) + ADAM_WD * w)
    return delta, m, v


def _adamw_layer(layer, mine, lands, w, m, v, earlier, token, name, rows):
    _, mm, nn = w.shape

    def body(hm_ref, ld_ref, w_ref, m_ref, v_ref, _, *refs):
        g_ref, d_ref, nm_ref, nv_ref = refs[-4:]
        g = _own_then_slots(hm_ref, ld_ref)
        g_ref[...] = g
        d, nm, nv = _adamw_math(w_ref[...], g, m_ref[...], v_ref[...])
        d_ref[...] = d
        nm_ref[...] = nm
        nv_ref[...] = nv

    spec = pl.BlockSpec((None, rows, nn), lambda i: (layer, i, 0))
    carried = [] if earlier is None else list(earlier)
    return pl.pallas_call(
        body, name=name, grid=(mm // rows,),
        in_specs=([pl.BlockSpec((rows, nn), lambda i: (i, 0)) if mine.ndim == 2
                   else pl.BlockSpec((None, rows, nn), lambda i: (_device_index(), i, 0)),
                   pl.BlockSpec((lands.shape[0], rows, nn), lambda i: (0, i, 0)),
                   spec, spec, spec] + [pl.BlockSpec(memory_space=pl.ANY)] * (1 + len(carried))),
        out_specs=[spec] * 4,
        out_shape=[jax.ShapeDtypeStruct(w.shape, F32)] * 4,
        input_output_aliases={6 + t: t for t in range(len(carried))},
        compiler_params=_compiler_params(("arbitrary",)),
    )(mine, lands, w, m, v, token, *carried)


def _adamw_small(gathered, params):
    def body(all_ref, *refs):
        ins, outs, packs = refs[:15], refs[15:15 + 21], refs[15 + 21]
        loss_ref = outs[0]
        for dev in range(N_DEV):
            for l in range(DEPTH):
                packs[l, PACK_SLICE * dev:PACK_SLICE * (dev + 1), :] = (
                    all_ref[SMALL_ROWS * dev + PACK_SLICE * l:SMALL_ROWS * dev + PACK_SLICE * (l + 1), :])
        loss_ref[...] = packs[DEPTH - 1, ROW_LOSS:ROW_LOSS + 1, 0:1]

        def update(p, sel, g):
            w_ref, m_ref, v_ref = ins[p], ins[5 + p], ins[10 + p]
            d, nm, nv = _adamw_math(w_ref[sel], g, m_ref[sel], v_ref[sel])
            for t, val in enumerate((g, d, nm, nv)):
                outs[1 + 5 * t + p][sel] = val

        for l in range(DEPTH):
            gp = packs.at[l]
            row0 = 2 * PACK_SLICE + 8 * l
            dgpre = all_ref[row0:row0 + 8, :]
            for dev in range(1, N_DEV):
                dgpre = dgpre + all_ref[SMALL_ROWS * dev + row0:SMALL_ROWS * dev + row0 + 8, :]
            for grp in range(4):
                update(0, (l, grp), gp[ROW_PW + BLOCK * grp:ROW_PW + BLOCK * (grp + 1), :])
                update(1, (slice(l, l + 1), slice(128 * grp, 128 * (grp + 1))), gp[ROW_SC + grp:ROW_SC + grp + 1, :])
            update(2, (slice(l, l + 1), slice(None)), gp[ROW_SINK:ROW_SINK + 1, 0:N_HEADS])
            for r in range(D_MODEL // 128):
                sel = (slice(l, l + 1), slice(128 * r, 128 * (r + 1)))
                update(3, sel, dgpre[r:r + 1, :])
                update(4, sel, gp[ROW_NPOST + r:ROW_NPOST + r + 1, :])

    shapes = [jax.ShapeDtypeStruct(p.shape, F32) for p in params[:5]]
    return pl.pallas_call(
        body, name="adamw_small",
        out_shape=[jax.ShapeDtypeStruct((1, 1), F32)] + shapes * 4,
        scratch_shapes=[pltpu.VMEM((DEPTH, PACK_ROWS, 128), F32)],
        compiler_params=_compiler_params(),
    )(gathered, *params)


def kernel(x, w_in, pool_w, pool_scale, attn_sinks, w_out, norm_pre, norm_post, loss_target, m_w_in, m_pool_w, m_pool_scale, m_attn_sinks, m_w_out, m_norm_pre, m_norm_post, v_w_in, v_pool_w, v_pool_scale, v_attn_sinks, v_w_out, v_norm_pre, v_norm_post):
    x0 = x.reshape(SEQ, D_MODEL)
    target = loss_target.reshape(SEQ, D_MODEL)
    bias = jnp.asarray(_attn_bias())
    w_in_t, m_in_t, v_in_t = (jnp.swapaxes(t, 1, 2) for t in (w_in, m_w_in, v_w_in))

    (win0, wout0), later, lands = _allgather([(w_in_t, 0), (w_out, 0)], BF16, "gather_w0",
                                              later=[(w_in_t, 1), (w_out, 1)])
    sems, later, lands, token = _gather_start(later, lands, [SIBLING_AND_SAME_CORE, ALL_PEERS], COLLECTIVE_GATHER_W1,
                                              "gather_w1_start")
    win_full, wout_full = [win0, None], [wout0, None]

    saved = []
    xl = x0
    for layer in range(DEPTH):
        u, pg, q, k, v, ag, z, a = _fwd_front(layer, xl, norm_pre, win_full[layer], token, attn_sinks,
                                             pool_w, pool_scale, bias)
        if layer == 0:
            land = _gather_wait(sems[0], later[0], lands[0], SIBLING_AND_SAME_CORE, z, "gather_w_in1_wait")
            fsems, land, token = _forward_start(land, IN_SHARD, "forward_w_in1_start")
        else:
            wout_full[layer] = _gather_wait(sems[1], later[1], lands[1], ALL_PEERS, z, "gather_w_out1_wait")
        x_next, y = _fwd_out(layer, z, xl, norm_post, wout_full[layer], token)
        if layer == 0:
            win_full[1] = _forward_wait(fsems, land, IN_SHARD, x_next, "forward_w_in1_wait")
        saved.append((xl, u, pg, q, k, v, ag, z, a, y))
        xl = x_next

    params_small = [pool_w, pool_scale, attn_sinks, norm_pre, norm_post,
                    m_pool_w, m_pool_scale, m_attn_sinks, m_norm_pre, m_norm_post,
                    v_pool_w, v_pool_scale, v_attn_sinks, v_norm_pre, v_norm_post]

    def start(blocks, direct, collective_id, tag):
        srcs, mine = (blocks, None) if direct else _pair_reduce(blocks, f"pair_reduce{tag}")
        send_sem, recv_sem, srcs, lands, started = _exchange_start(srcs, direct, collective_id, f"exchange_start{tag}")
        return (send_sem, recv_sem, srcs, lands, mine, direct), started

    def finish(handle, after, tag):
        send_sem, recv_sem, srcs, lands, mine, direct = handle
        srcs, lands = _exchange_wait(send_sem, recv_sem, srcs, lands, direct, after, f"exchange_wait{tag}")
        return (srcs if mine is None else mine), lands

    def back(layer, top, first, second):
        xin, u, pg, q, k, v, ag, z, a, y = saved[layer]
        return _bwd_back(layer, top, first, second, y, z, norm_post, wout_full[layer], attn_sinks, u, pg, q, k, v,
                         ag, a, pool_w, pool_scale, bias)

    dgpre = [None] * DEPTH
    dx, dproj, gw_out, pack = back(1, True, xl, target)
    dx, dgpre[1], gw_in_t = _bwd_in(1, "both", token, dproj, saved[1][0], norm_pre, dx, win_full[1])
    top, token = start([gw_in_t.reshape(N_DEV, IN_SHARD, D_MODEL), gw_out.reshape(N_DEV, OUT_SHARD, D_MODEL),
                        pack.reshape(N_DEV, PACK_SLICE, 128)], True, COLLECTIVE_EXCHANGE_1, "1")

    dproj, gw_out, pack = back(0, False, dx, token)
    early, token = start([gw_out.reshape(N_DEV, OUT_SHARD, D_MODEL), pack.reshape(N_DEV, PACK_SLICE, 128)], True,
                         COLLECTIVE_EXCHANGE_0A, "0a")
    (gw_in_t,) = _bwd_in(0, "dw", token, dproj, saved[0][0], norm_pre)
    late, token = start([gw_in_t.reshape(N_DEV, IN_SHARD, D_MODEL)], False, COLLECTIVE_EXCHANGE_0B, "0b")
    dx, dgpre[0] = _bwd_in(0, "dx", token, dproj, saved[0][0], norm_pre, dx, win_full[0])

    own1, lands1 = finish(top, dx, "1")
    big_in = _adamw_layer(1, own1[0], lands1[0], w_in_t, m_in_t, v_in_t, None, token, "adamw_in1", ADAM_ROWS_IN)
    big_out = _adamw_layer(1, own1[1], lands1[1], w_out, m_w_out, v_w_out, None, token, "adamw_out1", ADAM_ROWS_OUT)
    own0a, lands0a = finish(early, big_out[0], "0a")
    block, land = _small_block([own0a[1], own1[2]], [lands0a[1], lands1[2]], dgpre, "small_block")
    sems, block, land, token = _gather_start([block], [land], [ALL_PEERS], COLLECTIVE_GATHER_SMALL, "gather_small_start")
    big_out = _adamw_layer(0, own0a[0], lands0a[0], w_out, m_w_out, v_w_out, big_out, token, "adamw_out0", ADAM_ROWS_OUT)
    own0b, lands0b = finish(late, big_out[0], "0b")
    big_in = _adamw_layer(0, own0b[0], lands0b[0], w_in_t, m_in_t, v_in_t, big_in, token, "adamw_in0", ADAM_ROWS_IN)
    gathered = _gather_wait(sems[0], block[0], land[0], ALL_PEERS, big_in[0], "gather_small_wait")
    small_out = _adamw_small(gathered, params_small)
    loss = small_out[0].reshape(())

    outs = [loss, dx.reshape(1, SEQ, D_MODEL)]
    for t in range(4):
        pw_, sc_, sk_, npre_, npost_ = small_out[1 + 5 * t:6 + 5 * t]
        outs += [jnp.swapaxes(big_in[t], 1, 2), pw_, sc_, sk_, big_out[t], npre_, npost_]
    return tuple(outs)
```

```python
import numpy as np
import jax
import jax.numpy as jnp
from jax import lax
from jax.experimental import pallas as pl
from jax.experimental.pallas import tpu as pltpu

F32 = jnp.float32
BF16 = jnp.bfloat16

N_DEV = 8
SEQ = 2048
D_MODEL = 1024
D_POOL = 512
D_ATTN = 512
D_KV = 128
D_IN = 2304
N_HEADS = 8
GQA = 4
HEAD_DIM = 64
BLOCK = 128
N_BLOCKS = SEQ // BLOCK
POOL_WINDOWS = (2, 4, 8, 16)
DEPTH = 2
EPS = 1e-6
NEG_INF = -1e30
SCALE = HEAD_DIM ** -0.5
IN_SHARD = D_IN // N_DEV
OUT_SHARD = D_MODEL // N_DEV

COL_U, COL_PG, COL_Q, COL_K, COL_V, COL_AG = 0, 512, 1024, 1536, 1664, 1792

ADAM_LR = 0.001
ADAM_B1 = 0.9
ADAM_B2 = 0.999
ADAM_EPS = 1e-08
ADAM_WD = 0.01
ADAM_STEP = 10

TOKEN_TILE = 512
ADAM_ROWS_IN, ADAM_ROWS_OUT = 144, 128
VMEM_LIMIT = 56 * 1024 * 1024
MESH = pl.DeviceIdType.MESH

ROW_PW, ROW_SC, ROW_SINK, ROW_NPRE, ROW_NPOST, ROW_LOSS = 0, 512, 520, 528, 536, 544
PACK_ROWS = 576
PACK_SLICE = PACK_ROWS // N_DEV


def _nn(a, b):
    return jnp.dot(a, b, preferred_element_type=F32)


def _nt(a, b):
    return lax.dot_general(a, b, (((1,), (1,)), ((), ())), preferred_element_type=F32)


def _tn(a, b):
    return lax.dot_general(a, b, (((0,), (0,)), ((), ())), preferred_element_type=F32)


def _silu_parts(g):
    s = jax.nn.sigmoid(g)
    return g * s, s * (1.0 + g * (1.0 - s))


def _resident(shape):
    return pl.BlockSpec(shape, lambda *_: (0,) * len(shape), pipeline_mode=pl.Buffered(1))


def _compiler_params(sem=None):
    if sem is None:
        return pltpu.CompilerParams(vmem_limit_bytes=VMEM_LIMIT)
    return pltpu.CompilerParams(dimension_semantics=sem, vmem_limit_bytes=VMEM_LIMIT)


def _attn_bias():
    t = np.arange(BLOCK)[None, :]
    j = np.arange(BLOCK)[:, None]
    current = j <= t
    dist = np.where(current, t - j, t + BLOCK - j).astype(np.float32)
    out = np.zeros((2, 2, BLOCK, GQA * BLOCK), np.float32)
    for variant in range(2):
        valid = current | (variant == 1)
        for kv in range(2):
            for g in range(GQA):
                slope = np.float32(2.0 ** (-(kv * GQA + g + 1)))
                out[variant, kv, :, g * BLOCK:(g + 1) * BLOCK] = np.where(valid, -slope * dist, np.float32(NEG_INF))
    return out


def _replicate_head(kx, kv):
    rolled = pltpu.roll(kx, 64, 1)
    lane = lax.broadcasted_iota(jnp.int32, kx.shape, 1)
    dup = jnp.where(lane < 64, kx, rolled) if kv == 0 else jnp.where(lane < 64, rolled, kx)
    return jnp.concatenate([dup, dup], axis=1).astype(BF16)


def _stack_heads(qv):
    lane = lax.broadcasted_iota(jnp.int32, qv.shape, 1)
    zero = jnp.zeros_like(qv)
    return jnp.concatenate([jnp.where((lane >= 64 * g) & (lane < 64 * g + 64), qv, zero) for g in range(GQA)], axis=0)


def _unstack_heads(xs):
    lane = lax.broadcasted_iota(jnp.int32, (BLOCK, 256), 1)
    return jnp.where(lane < 64, xs[0:128], jnp.where(lane < 128, xs[128:256], jnp.where(lane < 192, xs[256:384], xs[384:512])))


def _fold_heads(r):
    h = r[:, 0:128] + r[:, 128:256]
    return h + pltpu.roll(h, 64, 1)


def _sink_row(sink_ref, layer, kv):
    lane = lax.broadcasted_iota(jnp.int32, (1, GQA * BLOCK), 1)
    s4 = [sink_ref[layer, kv * GQA + g] for g in range(GQA)]
    return jnp.where(lane < 128, s4[0], jnp.where(lane < 256, s4[1], jnp.where(lane < 384, s4[2], s4[3])))


def _band_is_current():
    j = lax.broadcasted_iota(jnp.int32, (BLOCK, GQA * BLOCK), 0)
    t = lax.broadcasted_iota(jnp.int32, (BLOCK, GQA * BLOCK), 1) & (BLOCK - 1)
    return j <= t


def _pack_band(full, current):
    return jnp.where(current, full[BLOCK:], full[:BLOCK])


def _unpack_band(packed, current):
    zero = jnp.zeros_like(packed)
    return jnp.concatenate([jnp.where(current, zero, packed), jnp.where(current, packed, zero)], axis=0)


def _probs_keys_major(k_rep, q_st, bias, sink, current):
    st = _pack_band(_nt(k_rep, q_st), current) * SCALE + bias
    m = jnp.maximum(jnp.max(st, axis=0, keepdims=True), sink)
    p = jnp.exp(st - m)
    esink = jnp.exp(sink - m)
    rl = 1.0 / (jnp.sum(p, axis=0, keepdims=True) + esink)
    return p * rl, esink * rl


WINDOW_HALO = 16


def _window_sum(ext, w, forward):
    s = ext
    sh = 1
    while sh < w:
        s = s + pltpu.roll(s, (ext.shape[0] - sh) if forward else sh, 0)
        sh *= 2
    return s


def _inv_count(n, w):
    t = n * BLOCK + lax.broadcasted_iota(jnp.int32, (BLOCK, 1), 0) + 1
    return 1.0 / jnp.minimum(t.astype(F32), float(w))


def _kv_ext(ref, n):
    r0 = pl.multiple_of(jnp.maximum(n - 1, 0) * BLOCK, BLOCK)
    r1 = pl.multiple_of(n * BLOCK, BLOCK)
    return jnp.concatenate([ref[pl.ds(r0, BLOCK), :], ref[pl.ds(r1, BLOCK), :]], axis=0)


def _rows_of(vec_ref, pack_ref, row0):
    for r in range(D_MODEL // 128):
        pack_ref[row0 + r:row0 + r + 1, :] = vec_ref[:, 128 * r:128 * (r + 1)]


FRONT_TILE = 2 * BLOCK


def _fwd_front(layer, x, norm_pre, w_in_t, token, sinks, pool_w, pool_scale, bias):
    tm = FRONT_TILE

    def body(sink_ref, x_ref, g_ref, w_ref, _, pw_ref, sc_ref, bias_ref,
             u_ref, pg_ref, q_ref, k_ref, v_ref, ag_ref, z_ref, a_ref, uprev, kprev, vprev):
        i = pl.program_id(0)

        @pl.when(i == 0)
        def _():
            uprev[...] = jnp.zeros_like(uprev)
            kprev[...] = jnp.zeros_like(kprev)
            vprev[...] = jnp.zeros_like(vprev)

        xv = x_ref[...]
        r = lax.rsqrt(jnp.mean(xv * xv, axis=-1, keepdims=True) + EPS)
        h = (xv * r * g_ref[layer:layer + 1, :]).astype(BF16)
        u_ref[...] = _nt(h, w_ref[COL_U:COL_PG, :])
        pg_ref[...] = _nt(h, w_ref[COL_PG:COL_Q, :])
        for sb in range(tm // BLOCK):
            n = (tm // BLOCK) * i + sb
            rows = slice(BLOCK * sb, BLOCK * (sb + 1))
            before = slice(BLOCK * (sb - 1), BLOCK * sb)
            uv = u_ref[rows, :]
            halo = (uprev[BLOCK - WINDOW_HALO:, :] if sb == 0
                    else u_ref[BLOCK * sb - WINDOW_HALO:BLOCK * sb, :])
            ext = jnp.concatenate([halo, uv], axis=0)
            for g, w in enumerate(POOL_WINDOWS):
                cs = slice(BLOCK * g, BLOCK * (g + 1))
                win = _window_sum(ext[:, cs], w, forward=False)[WINDOW_HALO:]
                pooled = win * _inv_count(n, w) - uv[:, cs]
                mixed = _nn(pooled.astype(BF16), pw_ref[g].astype(BF16))
                gate, _ = _silu_parts(pg_ref[rows, cs])
                z_ref[rows, cs] = (mixed * sc_ref[layer:layer + 1, cs] * gate).astype(BF16)

        q_ref[...] = _nt(h, w_ref[COL_Q:COL_K, :]).astype(BF16)
        k_ref[...] = _nt(h, w_ref[COL_K:COL_V, :])
        v_ref[...] = _nt(h, w_ref[COL_V:COL_AG, :])
        ag_ref[...] = _nt(h, w_ref[COL_AG:D_IN, :])

        current = _band_is_current()
        for sb in range(tm // BLOCK):
            n = (tm // BLOCK) * i + sb
            rows = slice(BLOCK * sb, BLOCK * (sb + 1))
            before = slice(BLOCK * (sb - 1), BLOCK * sb)
            kx = jnp.concatenate([kprev[...] if sb == 0 else k_ref[before, :], k_ref[rows, :]], axis=0)
            vx = jnp.concatenate([vprev[...] if sb == 0 else v_ref[before, :], v_ref[rows, :]], axis=0)
            variant = jnp.minimum(n, 1) if sb == 0 else 1
            for kv in range(2):
                cs = slice(256 * kv, 256 * (kv + 1))
                p, _ = _probs_keys_major(_replicate_head(kx, kv), _stack_heads(q_ref[rows, cs]),
                                         bias_ref[variant, kv], _sink_row(sink_ref, layer, kv), current)
                o = _unstack_heads(_tn(_unpack_band(p.astype(BF16), current), _replicate_head(vx, kv)))
                a_ref[rows, cs] = o
                gate, _ = _silu_parts(ag_ref[rows, cs])
                z_ref[rows, D_POOL + 256 * kv:D_POOL + 256 * (kv + 1)] = (o * gate).astype(BF16)

        tail = slice(tm - BLOCK, tm)
        uprev[...] = u_ref[tail, :]
        kprev[...] = k_ref[tail, :]
        vprev[...] = v_ref[tail, :]

    row = lambda c: pl.BlockSpec((tm, c), lambda i: (i, 0))
    const = lambda shape: pl.BlockSpec(shape, lambda i: (0,) * len(shape))
    return pl.pallas_call(
        body, name=f"fwd_front{layer}", grid=(SEQ // tm,),
        in_specs=[pl.BlockSpec(memory_space=pltpu.SMEM), row(D_MODEL), const((DEPTH, D_MODEL)),
                  _resident((D_IN, D_MODEL)), const((8, 128)),
                  pl.BlockSpec((None, 4, BLOCK, BLOCK), lambda i: (layer, 0, 0, 0)), const((DEPTH, D_POOL)),
                  _resident((2, 2, BLOCK, GQA * BLOCK))],
        out_specs=[row(D_POOL), row(D_POOL), row(D_ATTN), row(D_KV), row(D_KV), row(D_ATTN), row(D_MODEL),
                   row(D_ATTN)],
        out_shape=[jax.ShapeDtypeStruct((SEQ, D_POOL), F32), jax.ShapeDtypeStruct((SEQ, D_POOL), F32),
                   jax.ShapeDtypeStruct((SEQ, D_ATTN), BF16), jax.ShapeDtypeStruct((SEQ, D_KV), F32),
                   jax.ShapeDtypeStruct((SEQ, D_KV), F32), jax.ShapeDtypeStruct((SEQ, D_ATTN), F32),
                   jax.ShapeDtypeStruct((SEQ, D_MODEL), BF16), jax.ShapeDtypeStruct((SEQ, D_ATTN), F32)],
        scratch_shapes=[pltpu.VMEM((BLOCK, D_POOL), F32), pltpu.VMEM((BLOCK, D_KV), F32),
                        pltpu.VMEM((BLOCK, D_KV), F32)],
        compiler_params=_compiler_params(("arbitrary",)),
    )(sinks, x, norm_pre, w_in_t, token, pool_w, pool_scale, bias)


def _fwd_out(layer, z, x, norm_post, w_out, token):
    tm = TOKEN_TILE

    def body(z_ref, x_ref, g_ref, w_ref, _, xn_ref, y_ref):
        y = _nn(z_ref[...], w_ref[...])
        y_ref[...] = y
        r = lax.rsqrt(jnp.mean(y * y, axis=-1, keepdims=True) + EPS)
        xn_ref[...] = x_ref[...] + y * r * g_ref[layer:layer + 1, :]

    row = lambda c: pl.BlockSpec((tm, c), lambda i: (i, 0))
    return pl.pallas_call(
        body, name=f"fwd_out{layer}", grid=(SEQ // tm,),
        in_specs=[row(D_MODEL), row(D_MODEL), pl.BlockSpec((DEPTH, D_MODEL), lambda i: (0, 0)),
                  _resident((D_MODEL, D_MODEL)), pl.BlockSpec((8, 128), lambda i: (0, 0))],
        out_specs=[row(D_MODEL), row(D_MODEL)],
        out_shape=[jax.ShapeDtypeStruct((SEQ, D_MODEL), F32), jax.ShapeDtypeStruct((SEQ, D_MODEL), F32)],
        compiler_params=_compiler_params(("arbitrary",)),
    )(z, x, norm_post, w_out, token)


BACK_TILE = 2 * BLOCK


def _bwd_back(layer, top, dxo_or_xf, target_or_token, y, z, norm_post, w_out, sinks, u, pg, q, k, v, ag, a,
              pool_w, pool_scale, bias):
    tm = BACK_TILE
    steps = SEQ // tm
    last = steps - 1
    per = tm // BLOCK

    def body(*refs):
        refs = list(refs)
        sink_ref, first, second = refs[:3]
        (y_ref, z_ref, g_ref, w_ref, u_ref, up_ref, pg_ref, q_ref, k_ref, v_ref, ag_ref, a_ref, pw_ref, sc_ref,
         bias_ref) = refs[3:18]
        del refs[:18]
        dxo_ref = refs.pop(0) if top else None
        dp_ref, dw_ref, pack_ref, acc, dg, lacc, dzs, ck, cv, ce = refs
        i = pl.program_id(0)
        blk = last - i

        @pl.when(i == 0)
        def _():
            acc[...] = jnp.zeros_like(acc)
            dg[...] = jnp.zeros_like(dg)
            lacc[...] = jnp.zeros_like(lacc)
            pack_ref[...] = jnp.zeros_like(pack_ref)
            ck[...] = jnp.zeros_like(ck)
            cv[...] = jnp.zeros_like(cv)
            ce[...] = jnp.zeros_like(ce)

        if top:
            d = first[...] - second[...]
            dxo_v = d * (1.0 / D_MODEL)
            dxo_ref[...] = dxo_v
            part = jnp.sum(d * d, axis=-1, keepdims=True) * (1.0 / D_MODEL)
            lacc[...] += 0.5 * jnp.sum(part, axis=0, keepdims=True)
        else:
            dxo_v = first[...]
        yv = y_ref[...]
        r = lax.rsqrt(jnp.mean(yv * yv, axis=-1, keepdims=True) + EPS)
        yn = yv * r
        dg[...] += jnp.sum(dxo_v * yn, axis=0, keepdims=True)
        dyn = dxo_v * g_ref[layer:layer + 1, :]
        dy = (r * (dyn - yn * jnp.mean(dyn * yn, axis=-1, keepdims=True))).astype(BF16)
        dzs[...] = _nt(dy, w_ref[...])
        acc[...] += _tn(z_ref[...], dy)

        lane = lax.broadcasted_iota(jnp.int32, (1, 128), 1)
        lane2 = lax.broadcasted_iota(jnp.int32, (256, 128), 1)
        current = _band_is_current()
        for sb in reversed(range(per)):
            n = per * blk + sb
            rows = slice(BLOCK * sb, BLOCK * (sb + 1))

            uv = u_ref[rows, :]
            if sb == 0:
                halo = up_ref[BLOCK - WINDOW_HALO:, :] * (n > 0).astype(F32)
            else:
                halo = u_ref[BLOCK * sb - WINDOW_HALO:BLOCK * sb, :]
            ext = jnp.concatenate([halo, uv], axis=0)
            for g, w in enumerate(POOL_WINDOWS):
                cs = slice(BLOCK * g, BLOCK * (g + 1))
                inv = _inv_count(n, w)
                win = _window_sum(ext[:, cs], w, forward=False)[WINDOW_HALO:]
                pooled = win * inv - uv[:, cs]
                pw_g = pw_ref[g].astype(BF16)
                mixed = _nn(pooled.astype(BF16), pw_g)
                gate, dgate = _silu_parts(pg_ref[rows, cs])
                dzp = dzs[rows, cs]
                sc = sc_ref[layer:layer + 1, cs]
                dpm = dzp * gate
                dp_ref[rows, COL_PG + BLOCK * g:COL_PG + BLOCK * (g + 1)] = (dzp * (mixed * sc) * dgate).astype(BF16)
                pack_ref[ROW_SC + g:ROW_SC + g + 1, :] += jnp.sum(dpm * mixed, axis=0, keepdims=True)
                dmixed = (dpm * sc).astype(BF16)
                pack_ref[ROW_PW + BLOCK * g:ROW_PW + BLOCK * (g + 1), :] += _tn(pooled.astype(BF16), dmixed)
                dpooled = _nt(dmixed, pw_g)
                e = dpooled * inv
                lead = _window_sum(jnp.concatenate([e, ce[:WINDOW_HALO, cs]], axis=0), w, forward=True)[:BLOCK]
                dp_ref[rows, COL_U + BLOCK * g:COL_U + BLOCK * (g + 1)] = (lead - dpooled).astype(BF16)
                ce[:, cs] = e

            kx = _kv_ext(k_ref, n)
            vx = _kv_ext(v_ref, n)
            variant = jnp.minimum(n, 1) if sb == 0 else 1
            dsink_row = jnp.zeros((1, 128), F32)
            tks, tvs = [], []
            for kv in range(2):
                cs = slice(256 * kv, 256 * (kv + 1))
                k_rep = _replicate_head(kx, kv)
                v_rep = _replicate_head(vx, kv)
                q_st = _stack_heads(q_ref[rows, cs])
                gate, dgate = _silu_parts(ag_ref[rows, cs])
                dza = dzs[rows, D_POOL + 256 * kv:D_POOL + 256 * (kv + 1)]
                dp_ref[rows, COL_AG + 256 * kv:COL_AG + 256 * (kv + 1)] = (dza * a_ref[rows, cs] * dgate).astype(BF16)
                da_st = _stack_heads((dza * gate).astype(BF16))
                p, psink = _probs_keys_major(k_rep, q_st, bias_ref[variant, kv], _sink_row(sink_ref, layer, kv),
                                             current)
                dpt = _pack_band(_nt(v_rep, da_st), current)
                delta = jnp.sum(p * dpt, axis=0, keepdims=True)
                dst = _unpack_band((p * (dpt - delta) * SCALE).astype(BF16), current)
                sink_terms = psink * delta
                for g in range(GQA):
                    dsink = -jnp.sum(sink_terms[:, BLOCK * g:BLOCK * (g + 1)], axis=1, keepdims=True)
                    dsink_row = dsink_row + jnp.where(lane == kv * GQA + g, dsink, 0.0)
                dp_ref[rows, COL_Q + 256 * kv:COL_Q + 256 * (kv + 1)] = _unstack_heads(_tn(dst, k_rep)).astype(BF16)
                tks.append(_fold_heads(_nn(dst, q_st)))
                tvs.append(_fold_heads(_nn(_unpack_band(p.astype(BF16), current), da_st)))
            pack_ref[ROW_SINK:ROW_SINK + 1, :] += dsink_row
            dkx = jnp.where(lane2 < 64, tks[0], tks[1])
            dvx = jnp.where(lane2 < 64, tvs[0], tvs[1])
            dp_ref[rows, COL_K:COL_V] = (ck[...] + dkx[BLOCK:]).astype(BF16)
            dp_ref[rows, COL_V:COL_AG] = (cv[...] + dvx[BLOCK:]).astype(BF16)
            ck[...] = dkx[:BLOCK]
            cv[...] = dvx[:BLOCK]

        @pl.when(i == steps - 1)
        def _():
            dw_ref[...] = acc[...].astype(BF16)
            _rows_of(dg, pack_ref, ROW_NPOST)
            pack_ref[ROW_LOSS:ROW_LOSS + 1, :] = jnp.where(lane == 0, lacc[...], 0.0)

    row = lambda c: pl.BlockSpec((tm, c), lambda i: (last - i, 0))
    const = lambda shape: pl.BlockSpec(shape, lambda i: (0,) * len(shape))
    act = jax.ShapeDtypeStruct((SEQ, D_MODEL), F32)
    return pl.pallas_call(
        body, name=f"bwd_back{layer}", grid=(steps,),
        in_specs=[pl.BlockSpec(memory_space=pltpu.SMEM), row(D_MODEL), row(D_MODEL) if top else const((8, 128)),
                  row(D_MODEL), row(D_MODEL), const((DEPTH, D_MODEL)), _resident((D_MODEL, D_MODEL)),
                  row(D_POOL), pl.BlockSpec((BLOCK, D_POOL), lambda i: (jnp.maximum(per * (last - i) - 1, 0), 0)),
                  row(D_POOL), row(D_ATTN), _resident((SEQ, D_KV)), _resident((SEQ, D_KV)), row(D_ATTN), row(D_ATTN),
                  pl.BlockSpec((None, 4, BLOCK, BLOCK), lambda i: (layer, 0, 0, 0)), const((DEPTH, D_POOL)),
                  _resident((2, 2, BLOCK, GQA * BLOCK))],
        out_specs=([row(D_MODEL)] * (1 if top else 0)
                   + [row(D_IN), const((D_MODEL, D_MODEL)), const((PACK_ROWS, 128))]),
        out_shape=([act] * (1 if top else 0)
                   + [jax.ShapeDtypeStruct((SEQ, D_IN), BF16), jax.ShapeDtypeStruct((D_MODEL, D_MODEL), BF16),
                      jax.ShapeDtypeStruct((PACK_ROWS, 128), F32)]),
        scratch_shapes=[pltpu.VMEM((D_MODEL, D_MODEL), F32), pltpu.VMEM((1, D_MODEL), F32), pltpu.VMEM((1, 1), F32),
                        pltpu.VMEM((tm, D_MODEL), F32), pltpu.VMEM((BLOCK, D_KV), F32), pltpu.VMEM((BLOCK, D_KV), F32),
                        pltpu.VMEM((BLOCK, D_POOL), F32)],
        compiler_params=_compiler_params(("arbitrary",)),
    )(sinks, dxo_or_xf, target_or_token, y, z, norm_post, w_out, u, u, pg, q, k, v, ag, a, pool_w, pool_scale, bias)


def _bwd_in(layer, part, token, dproj, x, norm_pre, dxo=None, w_in_t=None):
    want_dw, want_dx = part in ("both", "dw"), part in ("both", "dx")
    tm = TOKEN_TILE
    steps = SEQ // tm
    cw = 256

    def body(*refs):
        refs = list(refs)
        dp_ref, x_ref, g_ref = refs[1:4]
        del refs[:4]
        if want_dx:
            dxo_ref, w_ref, dx_ref, dgo_ref = refs[:4]
            del refs[:4]
            dg = refs.pop()
        if want_dw:
            dw_ref, acc = refs
        i = pl.program_id(0)

        @pl.when(i == 0)
        def _():
            if want_dw:
                acc[...] = jnp.zeros_like(acc)
            if want_dx:
                dg[...] = jnp.zeros_like(dg)

        xv = x_ref[...]
        gv = g_ref[layer:layer + 1, :]
        r = lax.rsqrt(jnp.mean(xv * xv, axis=-1, keepdims=True) + EPS)
        xn = xv * r
        if want_dw:
            hb = (xn * gv).astype(BF16)
            for c in range(0, D_IN, cw):
                acc[c:c + cw, :] += _tn(dp_ref[:, c:c + cw], hb)
        if want_dx:
            dh = _nn(dp_ref[...], w_ref[...])
            dg[...] += jnp.sum(dh * xn, axis=0, keepdims=True)
            dhn = dh * gv
            dx_ref[...] = dxo_ref[...] + r * (dhn - xn * jnp.mean(dhn * xn, axis=-1, keepdims=True))

        @pl.when(i == steps - 1)
        def _():
            if want_dw:
                dw_ref[...] = acc[...].astype(BF16)
            if want_dx:
                _rows_of(dg, dgo_ref, 0)

    row = lambda c: pl.BlockSpec((tm, c), lambda i: (i, 0))
    const = lambda shape: pl.BlockSpec(shape, lambda i: (0,) * len(shape))
    in_specs = [const((8, 128)), row(D_IN), row(D_MODEL), const((DEPTH, D_MODEL))]
    operands = [token, dproj, x, norm_pre]
    out_specs, out_shape, scratch = [], [], []
    if want_dx:
        in_specs += [row(D_MODEL), _resident((D_IN, D_MODEL))]
        operands += [dxo, w_in_t]
        out_specs += [row(D_MODEL), const((8, 128))]
        out_shape += [jax.ShapeDtypeStruct((SEQ, D_MODEL), F32), jax.ShapeDtypeStruct((8, 128), F32)]
    if want_dw:
        out_specs.append(const((D_IN, D_MODEL)))
        out_shape.append(jax.ShapeDtypeStruct((D_IN, D_MODEL), BF16))
        scratch.append(pltpu.VMEM((D_IN, D_MODEL), F32))
    if want_dx:
        scratch.append(pltpu.VMEM((1, D_MODEL), F32))
    return pl.pallas_call(
        body, name=f"bwd_in_{part}{layer}", grid=(steps,),
        in_specs=in_specs, out_specs=out_specs, out_shape=out_shape, scratch_shapes=scratch,
        compiler_params=_compiler_params(("arbitrary",)),
    )(*operands)


def _mesh_pos():
    return lax.axis_index("x"), lax.axis_index("y"), lax.axis_index("c")


def _device_rows(ref, m, px, py, pc):
    return ref.at[pl.ds(pl.multiple_of((4 * px + 2 * py + pc) * m, 16 if m % 16 == 0 else 8), m), :]


def _allgather(srcs, out_dtype, name, later=()):
    na, nb = len(srcs), len(later)
    every = list(srcs) + list(later)
    shapes = [(a.shape[-2], a.shape[-1]) for a, _ in every]

    def body(*refs):
        xs, refs = refs[:na + nb], refs[na + nb:]
        outs, cast, land, refs = refs[:na], refs[na:na + nb], refs[na + nb:na + 2 * nb], refs[na + 2 * nb:]
        stage, (send_sems, recv_sems, local_sems) = refs[:na], refs[na:]
        x, y, c = _mesh_pos()
        me, sibling = (x, y, c), (x, y, 1 - c)
        near = [(1 - x, y), (x, 1 - y)]
        far = (1 - x, 1 - y)
        relay_from, relay_to = (x ^ (1 - c), y ^ c), (x ^ c, y ^ (1 - c))
        _handshake([sibling] + [(*chip, c) for chip in near])
        k_from, k_to = 1 + c, 2 - c

        def slot(a, px, py, pc):
            return _device_rows(outs[a], shapes[a][0], px, py, pc)

        def copy(a, k, block, to, src=None):
            return pltpu.make_async_remote_copy(
                src_ref=slot(a, *block) if src is None else src, dst_ref=slot(a, *block),
                send_sem=send_sems.at[a, k], recv_sem=recv_sems.at[a, k], device_id=to, device_id_type=MESH)

        def cast_block(i):
            layer = every[i][1]
            return (xs[i][...] if layer is None else xs[i][layer]).astype(out_dtype)

        for a in range(na):
            stage[a][...] = cast_block(a)
        mine = [pltpu.make_async_copy(stage[a], slot(a, *me), local_sems.at[a]) for a in range(na)]
        for cp in mine:
            cp.start()
        sent = []
        for a in range(na):
            sent.append(copy(a, 0, me, sibling, src=stage[a]))
            sent += [copy(a, 1 + j, me, (*chip, c), src=stage[a]) for j, chip in enumerate(near)]
        for cp in sent:
            cp.start()
        for b in range(nb):
            cast[b][...] = cast_block(na + b)
            cp = pltpu.make_async_copy(cast[b], _device_rows(land[b], shapes[na + b][0], *me), local_sems.at[na + b])
            cp.start()
            mine.append(cp)
        for a in range(na):
            copy(a, k_from, (*relay_from, c), me).wait_recv()
            sent += [copy(a, 3, (*relay_from, c), (*relay_to, c)), copy(a, 3 + k_from, (*relay_from, c), sibling)]
            sent[-2].start()
            sent[-1].start()
        for a in range(na):
            copy(a, k_to, (*relay_to, c), me).wait_recv()
            sent.append(copy(a, 3 + k_to, (*relay_to, c), sibling))
            sent[-1].start()
        for a in range(na):
            copy(a, 3, (*far, c), me).wait_recv()
            sent.append(copy(a, 6, (*far, c), sibling))
            sent[-1].start()
        for a in range(na):
            copy(a, 0, sibling, me).wait_recv()
            for j, chip in enumerate(near + [far]):
                copy(a, 4 + j, (*chip, 1 - c), me).wait_recv()
        for cp in sent:
            cp.wait_send()
        for cp in mine:
            cp.wait()

    vmem = pl.BlockSpec(memory_space=pltpu.VMEM)
    hbm = pl.BlockSpec(memory_space=pl.ANY)
    gathered = [jax.ShapeDtypeStruct((N_DEV * m, n), out_dtype) for m, n in shapes]
    out = pl.pallas_call(
        body, name=name,
        in_specs=[vmem] * (na + nb),
        out_specs=[hbm] * na + [vmem] * nb + [hbm] * nb,
        out_shape=gathered[:na] + [jax.ShapeDtypeStruct(s, out_dtype) for s in shapes[na:]] + gathered[na:],
        scratch_shapes=([pltpu.VMEM(s, out_dtype) for s in shapes[:na]]
                        + [pltpu.SemaphoreType.DMA((na, 7)), pltpu.SemaphoreType.DMA((na, 7)),
                           pltpu.SemaphoreType.DMA((na + nb,))]),
        compiler_params=pltpu.CompilerParams(vmem_limit_bytes=VMEM_LIMIT, collective_id=COLLECTIVE_GATHER_W0),
    )(*[a for a, _ in every])
    return out[:na], out[na:na + nb], out[na + nb:]


ALL_PEERS = tuple(range(1, N_DEV))
SIBLING_AND_SAME_CORE = (1, 2, 4, 6)


def _related(k, x, y, c):
    return x ^ ((k >> 2) & 1), y ^ ((k >> 1) & 1), c ^ (k & 1)


def _gather_start(blocks, lands, relations, collective_id, name):
    na = len(blocks)

    def body(*refs):
        src, land, sems, token = refs[:na], refs[na:2 * na], refs[2 * na:4 * na], refs[-1]
        x, y, c = _mesh_pos()
        _handshake([_related(k, x, y, c) for k in sorted(set().union(*relations))])
        for a in range(na):
            for k in relations[a]:
                pltpu.make_async_remote_copy(
                    src_ref=src[a], dst_ref=_device_rows(land[a], blocks[a].shape[0], x, y, c),
                    send_sem=sems[2 * a].at[k - 1], recv_sem=sems[2 * a + 1].at[k - 1],
                    device_id=_related(k, x, y, c), device_id_type=MESH).start()
        token[...] = jnp.zeros_like(token)

    bufs = [pltpu.HBM(t.shape, t.dtype) for t in list(blocks) + list(lands)]
    out = pl.pallas_call(
        body, name=name,
        out_shape=(*([pltpu.SemaphoreType.DMA((N_DEV - 1,))] * (2 * na)), *bufs, jax.ShapeDtypeStruct((8, 128), F32)),
        in_specs=[_HBM] * (2 * na),
        out_specs=(*([_SEM] * (2 * na)), *([_HBM] * (2 * na)), pl.BlockSpec(memory_space=pltpu.VMEM)),
        input_output_aliases={i: 2 * na + i for i in range(2 * na)},
        compiler_params=pltpu.CompilerParams(has_side_effects=_EFFECT, collective_id=collective_id),
    )(*[pltpu.with_memory_space_constraint(t, pltpu.HBM) for t in list(blocks) + list(lands)])
    sems = [(out[2 * a], out[2 * a + 1]) for a in range(na)]
    return sems, out[2 * na:3 * na], out[3 * na:4 * na], out[-1]


def _gather_wait(sems, block, land, relations, after, name):
    def body(src, land_ref, send_sem, recv_sem, after_ref, src_out, land_out):
        x, y, c = _mesh_pos()
        for k in relations:
            peer = _related(k, x, y, c)
            cp = pltpu.make_async_remote_copy(
                src_ref=src, dst_ref=_device_rows(land_ref, block.shape[0], *peer),
                send_sem=send_sem.at[k - 1], recv_sem=recv_sem.at[k - 1], device_id=peer, device_id_type=MESH)
            cp.wait_send()
            cp.wait_recv()

    out = pl.pallas_call(
        body, name=name,
        out_shape=(pltpu.HBM(block.shape, block.dtype), pltpu.HBM(land.shape, land.dtype)),
        in_specs=[_HBM, _HBM, _SEM, _SEM, pl.BlockSpec(memory_space=pl.ANY)],
        out_specs=[_HBM, _HBM],
        input_output_aliases={0: 0, 1: 1},
        compiler_params=pltpu.CompilerParams(has_side_effects=_EFFECT),
    )(block, land, sems[0], sems[1], after)
    return out[1]


(COLLECTIVE_GATHER_W0, COLLECTIVE_GATHER_W1, COLLECTIVE_FORWARD_W_IN1, COLLECTIVE_EXCHANGE_1, COLLECTIVE_EXCHANGE_0A,
 COLLECTIVE_PAIR_REDUCE, COLLECTIVE_EXCHANGE_0B, COLLECTIVE_GATHER_SMALL) = range(1, 9)


def _handshake(peers):
    barrier = pltpu.get_barrier_semaphore()
    for peer in peers:
        pl.semaphore_signal(barrier, inc=1, device_id=peer, device_id_type=MESH)
    pl.semaphore_wait(barrier, len(peers))


def _forward_plan(land_ref, m):
    x, y, c = _mesh_pos()
    return [_device_rows(land_ref, m, qx, qy, c) for qx, qy in ((1 - x, y), (x, 1 - y), (1 - x, 1 - y))], (x, y, 1 - c)


def _forward_start(land, m, name):
    def body(land_ref, send_sem, recv_sem, land_out, token):
        _handshake([(lax.axis_index("x"), lax.axis_index("y"), 1 - lax.axis_index("c"))])
        rows, sibling = _forward_plan(land_ref, m)
        for j, r in enumerate(rows):
            pltpu.make_async_remote_copy(src_ref=r, dst_ref=r, send_sem=send_sem.at[j], recv_sem=recv_sem.at[j],
                                         device_id=sibling, device_id_type=MESH).start()
        token[...] = jnp.zeros_like(token)

    out = pl.pallas_call(
        body, name=name,
        out_shape=(pltpu.SemaphoreType.DMA((3,)), pltpu.SemaphoreType.DMA((3,)), pltpu.HBM(land.shape, land.dtype),
                   jax.ShapeDtypeStruct((8, 128), F32)),
        in_specs=[_HBM],
        out_specs=(_SEM, _SEM, _HBM, pl.BlockSpec(memory_space=pltpu.VMEM)),
        input_output_aliases={0: 2},
        compiler_params=pltpu.CompilerParams(has_side_effects=_EFFECT, collective_id=COLLECTIVE_FORWARD_W_IN1),
    )(pltpu.with_memory_space_constraint(land, pltpu.HBM))
    return (out[0], out[1]), out[2], out[3]


def _forward_wait(sems, land, m, after, name):
    def body(land_ref, send_sem, recv_sem, after_ref, land_out):
        x, y, c = _mesh_pos()
        mine, sibling = _forward_plan(land_ref, m)
        theirs = [_device_rows(land_ref, m, qx, qy, 1 - c) for qx, qy in ((1 - x, y), (x, 1 - y), (1 - x, 1 - y))]
        for j in range(3):
            cp = pltpu.make_async_remote_copy(src_ref=mine[j], dst_ref=theirs[j], send_sem=send_sem.at[j],
                                              recv_sem=recv_sem.at[j], device_id=sibling, device_id_type=MESH)
            cp.wait_send()
            cp.wait_recv()

    return pl.pallas_call(
        body, name=name,
        out_shape=pltpu.HBM(land.shape, land.dtype),
        in_specs=[_HBM, _SEM, _SEM, pl.BlockSpec(memory_space=pl.ANY)],
        out_specs=_HBM,
        input_output_aliases={0: 0},
        compiler_params=pltpu.CompilerParams(has_side_effects=_EFFECT),
    )(land, sems[0], sems[1], after)


def _row_step(m):
    return next(s for s in (32, 24, 16, 8) if m % s == 0)


def _pair_reduce(arrs, name):
    na = len(arrs)

    def body(*refs):
        gs, hs, hm = refs[:na], refs[na:2 * na], refs[2 * na:3 * na]
        own, ra = refs[3 * na:4 * na], refs[4 * na:5 * na]
        d2d_send, d2d_recv, local_sems = refs[5 * na:]
        x, y, c = _mesh_pos()
        sibling = (x, y, 1 - c)
        _handshake([sibling])
        loads, sends = [], []
        for a in range(na):
            for q in range(4):
                cp = pltpu.make_async_copy(gs[a].at[2 * q + c], own[a].at[q], local_sems.at[a, q])
                cp.start()
                loads.append(cp)
                cp = pltpu.make_async_remote_copy(
                    src_ref=gs[a].at[2 * q + (1 - c)], dst_ref=ra[a].at[q], send_sem=d2d_send.at[a, q],
                    recv_sem=d2d_recv.at[a, q], device_id=sibling, device_id_type=MESH)
                cp.start()
                sends.append(cp)
        for cp in loads:
            cp.wait()
        for cp in sends:
            cp.wait_recv()
        others = [2 * (1 - x) + y, 2 * x + (1 - y), 2 * (1 - x) + (1 - y)]
        for a in range(na):
            m = arrs[a].shape[1]
            step = _row_step(m)

            def add(i, carry, a=a, step=step):
                rs = pl.ds(pl.multiple_of(i * step, step), step)
                for j, q in enumerate(others):
                    hs[a][j, rs, :] = (own[a][q, rs, :].astype(F32) + ra[a][q, rs, :].astype(F32)).astype(hs[a].dtype)
                q = 2 * x + y
                hm[a][rs, :] = own[a][q, rs, :].astype(F32) + ra[a][q, rs, :].astype(F32)
                return carry

            lax.fori_loop(0, m // step, add, 0)
        for cp in sends:
            cp.wait_send()

    vmem = pl.BlockSpec(memory_space=pltpu.VMEM)
    scratch = [pltpu.VMEM((4,) + t.shape[1:], t.dtype) for t in arrs] * 2
    scratch += [pltpu.SemaphoreType.DMA((na, 4)), pltpu.SemaphoreType.DMA((na, 4)), pltpu.SemaphoreType.DMA((na, 4))]
    out = pl.pallas_call(
        body, name=name,
        in_specs=[pl.BlockSpec(memory_space=pl.ANY)] * na, out_specs=[vmem] * (2 * na),
        out_shape=([jax.ShapeDtypeStruct((3,) + t.shape[1:], t.dtype) for t in arrs]
                   + [jax.ShapeDtypeStruct(t.shape[1:], F32) for t in arrs]),
        scratch_shapes=scratch,
        compiler_params=pltpu.CompilerParams(vmem_limit_bytes=VMEM_LIMIT, collective_id=COLLECTIVE_PAIR_REDUCE),
    )(*arrs)
    return out[:na], out[na:]


_HBM = pl.BlockSpec(memory_space=pltpu.HBM)
_SEM = pl.BlockSpec(memory_space=pltpu.SEMAPHORE)
_EFFECT = pltpu.SideEffectType.DATAFLOW_SIDE_EFFECTING


def _exchange_plan(direct):
    x, y, c = _mesh_pos()
    if not direct:
        return [(j, j, (qx, qy, c)) for j, (qx, qy) in enumerate([(1 - x, y), (x, 1 - y), (1 - x, 1 - y)])]
    plan = []
    for k in range(1, N_DEV):
        px, py, pc = x ^ ((k >> 2) & 1), y ^ ((k >> 1) & 1), c ^ (k & 1)
        plan.append((4 * px + 2 * py + pc, k - 1, (px, py, pc)))
    return plan


def _exchange_start(srcs, direct, collective_id, name):
    na = len(srcs)
    slots = N_DEV - 1 if direct else 3

    def body(*refs):
        src, land = refs[:na], refs[na:2 * na]
        send_sem, recv_sem = refs[2 * na], refs[2 * na + 1]
        token = refs[-1]
        plan = _exchange_plan(direct)
        _handshake([peer for _, _, peer in plan])
        for block, slot, peer in plan:
            for a in range(na):
                pltpu.make_async_remote_copy(
                    src_ref=src[a].at[block], dst_ref=land[a].at[slot], send_sem=send_sem.at[slots * a + slot],
                    recv_sem=recv_sem.at[slots * a + slot], device_id=peer, device_id_type=MESH).start()
        token[...] = jnp.zeros_like(token)

    zones = [jax.ShapeDtypeStruct((slots,) + t.shape[1:], t.dtype) for t in srcs]
    bufs = [pltpu.HBM(t.shape, t.dtype) for t in list(srcs) + zones]
    out = pl.pallas_call(
        body, name=name,
        out_shape=(pltpu.SemaphoreType.DMA((slots * na,)), pltpu.SemaphoreType.DMA((slots * na,)), *bufs,
                   jax.ShapeDtypeStruct((8, 128), F32)),
        in_specs=[_HBM] * (2 * na),
        out_specs=(_SEM, _SEM, *([_HBM] * (2 * na)), pl.BlockSpec(memory_space=pltpu.VMEM)),
        input_output_aliases={i: 2 + i for i in range(2 * na)},
        compiler_params=pltpu.CompilerParams(has_side_effects=_EFFECT, collective_id=collective_id),
    )(*[pltpu.with_memory_space_constraint(t, pltpu.HBM) for t in srcs],
      *[pltpu.with_memory_space_constraint(lax.empty(t.shape, t.dtype), pltpu.HBM) for t in zones])
    return out[0], out[1], out[2:2 + na], out[2 + na:2 + 2 * na], out[-1]


def _exchange_wait(send_sem, recv_sem, srcs, lands, direct, after, name):
    na = len(srcs)
    slots = N_DEV - 1 if direct else 3

    def body(*refs):
        src, land = refs[:na], refs[na:2 * na]
        send_sem_ref, recv_sem_ref = refs[2 * na], refs[2 * na + 1]
        for block, slot, peer in _exchange_plan(direct):
            for a in range(na):
                cp = pltpu.make_async_remote_copy(
                    src_ref=src[a].at[block], dst_ref=land[a].at[slot], send_sem=send_sem_ref.at[slots * a + slot],
                    recv_sem=recv_sem_ref.at[slots * a + slot], device_id=peer, device_id_type=MESH)
                cp.wait_send()
                cp.wait_recv()

    bufs = [pltpu.HBM(t.shape, t.dtype) for t in list(srcs) + list(lands)]
    out = pl.pallas_call(
        body, name=name,
        out_shape=tuple(bufs),
        in_specs=[_HBM] * (2 * na) + [_SEM, _SEM, pl.BlockSpec(memory_space=pl.ANY)],
        out_specs=[_HBM] * (2 * na),
        input_output_aliases={i: i for i in range(2 * na)},
        compiler_params=pltpu.CompilerParams(has_side_effects=_EFFECT),
    )(*srcs, *lands, send_sem, recv_sem, after)
    return out[:na], out[na:]


def _own_then_slots(mine_ref, lands_ref, rows=slice(None)):
    if len(mine_ref.shape) == 3:
        x, y, c = _mesh_pos()
        total = mine_ref[4 * x + 2 * y + c, rows, :].astype(F32)
    else:
        total = mine_ref[rows, :].astype(F32)
    for j in range(lands_ref.shape[0]):
        total = total + lands_ref[j, rows, :].astype(F32)
    return total


SMALL_ROWS = 2 * PACK_SLICE + 2 * 8


def _small_block(mine, lands, dgpre, name):
    def body(*refs):
        hm, ld, dg = refs[:DEPTH], refs[DEPTH:2 * DEPTH], refs[2 * DEPTH:3 * DEPTH]
        blk, land, sem = refs[3 * DEPTH:]
        for l in range(DEPTH):
            blk[PACK_SLICE * l:PACK_SLICE * (l + 1), :] = _own_then_slots(hm[l], ld[l])
            blk[2 * PACK_SLICE + 8 * l:2 * PACK_SLICE + 8 * (l + 1), :] = dg[l][...]
        cp = pltpu.make_async_copy(blk, _device_rows(land, SMALL_ROWS, *_mesh_pos()), sem)
        cp.start()
        cp.wait()

    vmem = pl.BlockSpec(memory_space=pltpu.VMEM)
    return pl.pallas_call(
        body, name=name,
        in_specs=[vmem] * (3 * DEPTH), out_specs=[vmem, pl.BlockSpec(memory_space=pl.ANY)],
        out_shape=[jax.ShapeDtypeStruct((SMALL_ROWS, 128), F32), jax.ShapeDtypeStruct((N_DEV * SMALL_ROWS, 128), F32)],
        scratch_shapes=[pltpu.SemaphoreType.DMA],
        compiler_params=_compiler_params(),
    )(*mine, *lands, *dgpre)


def _adamw_math(w, g, m, v):
    m = ADAM_B1 * m + (1.0 - ADAM_B1) * g
    v = ADAM_B2 * v + (1.0 - ADAM_B2) * (g * g)
    m_hat = m / (1.0 - ADAM_B1 ** ADAM_STEP)
    v_hat = v / (1.0 - ADAM_B2 ** ADAM_STEP)
    delta = -ADAM_LR * (m_hat / (jnp.sqrt(v_hat) + ADAM_EPS) + ADAM_WD * w)
    return delta, m, v


def _adamw_layer(layer, mine, lands, w, m, v, earlier, token, name, rows):
    _, mm, nn = w.shape

    def body(hm_ref, ld_ref, w_ref, m_ref, v_ref, _, *refs):
        g_ref, d_ref, nm_ref, nv_ref = refs[-4:]
        g = _own_then_slots(hm_ref, ld_ref)
        g_ref[...] = g
        d, nm, nv = _adamw_math(w_ref[...], g, m_ref[...], v_ref[...])
        d_ref[...] = d
        nm_ref[...] = nm
        nv_ref[...] = nv

    spec = pl.BlockSpec((None, rows, nn), lambda i: (layer, i, 0))
    carried = [] if earlier is None else list(earlier)
    return pl.pallas_call(
        body, name=name, grid=(mm // rows,),
        in_specs=([pl.BlockSpec((rows, nn), lambda i: (i, 0)) if mine.ndim == 2
                   else pl.BlockSpec((N_DEV, rows, nn), lambda i: (0, i, 0)),
                   pl.BlockSpec((lands.shape[0], rows, nn), lambda i: (0, i, 0)),
                   spec, spec, spec] + [pl.BlockSpec(memory_space=pl.ANY)] * (1 + len(carried))),
        out_specs=[spec] * 4,
        out_shape=[jax.ShapeDtypeStruct(w.shape, F32)] * 4,
        input_output_aliases={6 + t: t for t in range(len(carried))},
        compiler_params=_compiler_params(("arbitrary",)),
    )(mine, lands, w, m, v, token, *carried)


def _adamw_small(gathered, params):
    def body(all_ref, *refs):
        ins, outs, packs = refs[:15], refs[15:15 + 21], refs[15 + 21]
        loss_ref = outs[0]
        for dev in range(N_DEV):
            for l in range(DEPTH):
                packs[l, PACK_SLICE * dev:PACK_SLICE * (dev + 1), :] = (
                    all_ref[SMALL_ROWS * dev + PACK_SLICE * l:SMALL_ROWS * dev + PACK_SLICE * (l + 1), :])
        loss_ref[...] = packs[DEPTH - 1, ROW_LOSS:ROW_LOSS + 1, 0:1]

        def update(p, sel, g):
            w_ref, m_ref, v_ref = ins[p], ins[5 + p], ins[10 + p]
            d, nm, nv = _adamw_math(w_ref[sel], g, m_ref[sel], v_ref[sel])
            for t, val in enumerate((g, d, nm, nv)):
                outs[1 + 5 * t + p][sel] = val

        for l in range(DEPTH):
            gp = packs.at[l]
            row0 = 2 * PACK_SLICE + 8 * l
            dgpre = all_ref[row0:row0 + 8, :]
            for dev in range(1, N_DEV):
                dgpre = dgpre + all_ref[SMALL_ROWS * dev + row0:SMALL_ROWS * dev + row0 + 8, :]
            for grp in range(4):
                update(0, (l, grp), gp[ROW_PW + BLOCK * grp:ROW_PW + BLOCK * (grp + 1), :])
                update(1, (slice(l, l + 1), slice(128 * grp, 128 * (grp + 1))), gp[ROW_SC + grp:ROW_SC + grp + 1, :])
            update(2, (slice(l, l + 1), slice(None)), gp[ROW_SINK:ROW_SINK + 1, 0:N_HEADS])
            for r in range(D_MODEL // 128):
                sel = (slice(l, l + 1), slice(128 * r, 128 * (r + 1)))
                update(3, sel, dgpre[r:r + 1, :])
                update(4, sel, gp[ROW_NPOST + r:ROW_NPOST + r + 1, :])

    shapes = [jax.ShapeDtypeStruct(p.shape, F32) for p in params[:5]]
    return pl.pallas_call(
        body, name="adamw_small",
        out_shape=[jax.ShapeDtypeStruct((1, 1), F32)] + shapes * 4,
        scratch_shapes=[pltpu.VMEM((DEPTH, PACK_ROWS, 128), F32)],
        compiler_params=_compiler_params(),
    )(gathered, *params)


def kernel(x, w_in, pool_w, pool_scale, attn_sinks, w_out, norm_pre, norm_post, loss_target, m_w_in, m_pool_w, m_pool_scale, m_attn_sinks, m_w_out, m_norm_pre, m_norm_post, v_w_in, v_pool_w, v_pool_scale, v_attn_sinks, v_w_out, v_norm_pre, v_norm_post):
    x0 = x.reshape(SEQ, D_MODEL)
    target = loss_target.reshape(SEQ, D_MODEL)
    bias = jnp.asarray(_attn_bias())
    w_in_t, m_in_t, v_in_t = (jnp.swapaxes(t, 1, 2) for t in (w_in, m_w_in, v_w_in))

    (win0, wout0), later, lands = _allgather([(w_in_t, 0), (w_out, 0)], BF16, "gather_w0",
                                              later=[(w_in_t, 1), (w_out, 1)])
    sems, later, lands, token = _gather_start(later, lands, [SIBLING_AND_SAME_CORE, ALL_PEERS], COLLECTIVE_GATHER_W1,
                                              "gather_w1_start")
    win_full, wout_full = [win0, None], [wout0, None]

    saved = []
    xl = x0
    for layer in range(DEPTH):
        u, pg, q, k, v, ag, z, a = _fwd_front(layer, xl, norm_pre, win_full[layer], token, attn_sinks,
                                             pool_w, pool_scale, bias)
        if layer == 0:
            land = _gather_wait(sems[0], later[0], lands[0], SIBLING_AND_SAME_CORE, z, "gather_w_in1_wait")
            fsems, land, token = _forward_start(land, IN_SHARD, "forward_w_in1_start")
        else:
            wout_full[layer] = _gather_wait(sems[1], later[1], lands[1], ALL_PEERS, z, "gather_w_out1_wait")
        x_next, y = _fwd_out(layer, z, xl, norm_post, wout_full[layer], token)
        if layer == 0:
            win_full[1] = _forward_wait(fsems, land, IN_SHARD, x_next, "forward_w_in1_wait")
        saved.append((xl, u, pg, q, k, v, ag, z, a, y))
        xl = x_next

    params_small = [pool_w, pool_scale, attn_sinks, norm_pre, norm_post,
                    m_pool_w, m_pool_scale, m_attn_sinks, m_norm_pre, m_norm_post,
                    v_pool_w, v_pool_scale, v_attn_sinks, v_norm_pre, v_norm_post]

    def start(blocks, direct, collective_id, tag):
        srcs, mine = (blocks, None) if direct else _pair_reduce(blocks, f"pair_reduce{tag}")
        send_sem, recv_sem, srcs, lands, started = _exchange_start(srcs, direct, collective_id, f"exchange_start{tag}")
        return (send_sem, recv_sem, srcs, lands, mine, direct), started

    def finish(handle, after, tag):
        send_sem, recv_sem, srcs, lands, mine, direct = handle
        srcs, lands = _exchange_wait(send_sem, recv_sem, srcs, lands, direct, after, f"exchange_wait{tag}")
        return (srcs if mine is None else mine), lands

    def back(layer, top, first, second):
        xin, u, pg, q, k, v, ag, z, a, y = saved[layer]
        return _bwd_back(layer, top, first, second, y, z, norm_post, wout_full[layer], attn_sinks, u, pg, q, k, v,
                         ag, a, pool_w, pool_scale, bias)

    dgpre = [None] * DEPTH
    dx, dproj, gw_out, pack = back(1, True, xl, target)
    dx, dgpre[1], gw_in_t = _bwd_in(1, "both", token, dproj, saved[1][0], norm_pre, dx, win_full[1])
    top, token = start([gw_in_t.reshape(N_DEV, IN_SHARD, D_MODEL), gw_out.reshape(N_DEV, OUT_SHARD, D_MODEL),
                        pack.reshape(N_DEV, PACK_SLICE, 128)], True, COLLECTIVE_EXCHANGE_1, "1")

    dproj, gw_out, pack = back(0, False, dx, token)
    early, token = start([gw_out.reshape(N_DEV, OUT_SHARD, D_MODEL), pack.reshape(N_DEV, PACK_SLICE, 128)], True,
                         COLLECTIVE_EXCHANGE_0A, "0a")
    (gw_in_t,) = _bwd_in(0, "dw", token, dproj, saved[0][0], norm_pre)
    late, token = start([gw_in_t.reshape(N_DEV, IN_SHARD, D_MODEL)], False, COLLECTIVE_EXCHANGE_0B, "0b")
    dx, dgpre[0] = _bwd_in(0, "dx", token, dproj, saved[0][0], norm_pre, dx, win_full[0])

    own1, lands1 = finish(top, dx, "1")
    big_in = _adamw_layer(1, own1[0], lands1[0], w_in_t, m_in_t, v_in_t, None, token, "adamw_in1", ADAM_ROWS_IN)
    big_out = _adamw_layer(1, own1[1], lands1[1], w_out, m_w_out, v_w_out, None, token, "adamw_out1", ADAM_ROWS_OUT)
    own0a, lands0a = finish(early, big_out[0], "0a")
    block, land = _small_block([own0a[1], own1[2]], [lands0a[1], lands1[2]], dgpre, "small_block")
    sems, block, land, token = _gather_start([block], [land], [ALL_PEERS], COLLECTIVE_GATHER_SMALL, "gather_small_start")
    big_out = _adamw_layer(0, own0a[0], lands0a[0], w_out, m_w_out, v_w_out, big_out, token, "adamw_out0", ADAM_ROWS_OUT)
    own0b, lands0b = finish(late, big_out[0], "0b")
    big_in = _adamw_layer(0, own0b[0], lands0b[0], w_in_t, m_in_t, v_in_t, big_in, token, "adamw_in0", ADAM_ROWS_IN)
    gathered = _gather_wait(sems[0], block[0], land[0], ALL_PEERS, big_in[0], "gather_small_wait")
    small_out = _adamw_small(gathered, params_small)
    loss = small_out[0].reshape(())

    outs = [loss, dx.reshape(1, SEQ, D_MODEL)]
    for t in range(4):
        pw_, sc_, sk_, npre_, npost_ = small_out[1 + 5 * t:6 + 5 * t]
        outs += [jnp.swapaxes(big_in[t], 1, 2), pw_, sc_, sk_, big_out[t], npre_, npost_]
    return tuple(outs)
```

```python
import numpy as np
import jax
import jax.numpy as jnp
from jax import lax
from jax.experimental import pallas as pl
from jax.experimental.pallas import tpu as pltpu

F32 = jnp.float32
BF16 = jnp.bfloat16

N_DEV = 8
SEQ = 2048
D_MODEL = 1024
D_POOL = 512
D_ATTN = 512
D_KV = 128
D_IN = 2304
N_HEADS = 8
GQA = 4
HEAD_DIM = 64
BLOCK = 128
N_BLOCKS = SEQ // BLOCK
POOL_WINDOWS = (2, 4, 8, 16)
DEPTH = 2
EPS = 1e-6
NEG_INF = -1e30
SCALE = HEAD_DIM ** -0.5
IN_SHARD = D_IN // N_DEV
OUT_SHARD = D_MODEL // N_DEV

COL_U, COL_PG, COL_Q, COL_K, COL_V, COL_AG = 0, 512, 1024, 1536, 1664, 1792

ADAM_LR = 0.001
ADAM_B1 = 0.9
ADAM_B2 = 0.999
ADAM_EPS = 1e-08
ADAM_WD = 0.01
ADAM_STEP = 10

TOKEN_TILE = 512
FWD_OUT_TILE = 1024
ADAM_ROWS_IN, ADAM_ROWS_OUT = 144, 128
VMEM_LIMIT = 56 * 1024 * 1024
MESH = pl.DeviceIdType.MESH

ROW_PW, ROW_SC, ROW_SINK, ROW_NPRE, ROW_NPOST, ROW_LOSS = 0, 512, 520, 528, 536, 544
PACK_ROWS = 576
PACK_SLICE = PACK_ROWS // N_DEV


def _nn(a, b):
    return jnp.dot(a, b, preferred_element_type=F32)


def _nt(a, b):
    return lax.dot_general(a, b, (((1,), (1,)), ((), ())), preferred_element_type=F32)


def _tn(a, b):
    return lax.dot_general(a, b, (((0,), (0,)), ((), ())), preferred_element_type=F32)


def _silu_parts(g):
    s = jax.nn.sigmoid(g)
    return g * s, s * (1.0 + g * (1.0 - s))


def _resident(shape):
    return pl.BlockSpec(shape, lambda *_: (0,) * len(shape), pipeline_mode=pl.Buffered(1))


def _compiler_params(sem=None):
    if sem is None:
        return pltpu.CompilerParams(vmem_limit_bytes=VMEM_LIMIT)
    return pltpu.CompilerParams(dimension_semantics=sem, vmem_limit_bytes=VMEM_LIMIT)


def _attn_bias():
    t = np.arange(BLOCK)[None, :]
    j = np.arange(BLOCK)[:, None]
    current = j <= t
    dist = np.where(current, t - j, t + BLOCK - j).astype(np.float32)
    out = np.zeros((2, 2, BLOCK, GQA * BLOCK), np.float32)
    for variant in range(2):
        valid = current | (variant == 1)
        for kv in range(2):
            for g in range(GQA):
                slope = np.float32(2.0 ** (-(kv * GQA + g + 1)))
                out[variant, kv, :, g * BLOCK:(g + 1) * BLOCK] = np.where(valid, -slope * dist, np.float32(NEG_INF))
    return out


def _replicate_head(kx, kv):
    rolled = pltpu.roll(kx, 64, 1)
    lane = lax.broadcasted_iota(jnp.int32, kx.shape, 1)
    dup = jnp.where(lane < 64, kx, rolled) if kv == 0 else jnp.where(lane < 64, rolled, kx)
    return jnp.concatenate([dup, dup], axis=1).astype(BF16)


def _stack_heads(qv):
    lane = lax.broadcasted_iota(jnp.int32, qv.shape, 1)
    zero = jnp.zeros_like(qv)
    return jnp.concatenate([jnp.where((lane >= 64 * g) & (lane < 64 * g + 64), qv, zero) for g in range(GQA)], axis=0)


def _unstack_heads(xs):
    lane = lax.broadcasted_iota(jnp.int32, (BLOCK, 256), 1)
    return jnp.where(lane < 64, xs[0:128], jnp.where(lane < 128, xs[128:256], jnp.where(lane < 192, xs[256:384], xs[384:512])))


def _fold_heads(r):
    h = r[:, 0:128] + r[:, 128:256]
    return h + pltpu.roll(h, 64, 1)


def _sink_row(sink_ref, layer, kv):
    lane = lax.broadcasted_iota(jnp.int32, (1, GQA * BLOCK), 1)
    s4 = [sink_ref[layer, kv * GQA + g] for g in range(GQA)]
    return jnp.where(lane < 128, s4[0], jnp.where(lane < 256, s4[1], jnp.where(lane < 384, s4[2], s4[3])))


def _band_is_current():
    j = lax.broadcasted_iota(jnp.int32, (BLOCK, GQA * BLOCK), 0)
    t = lax.broadcasted_iota(jnp.int32, (BLOCK, GQA * BLOCK), 1) & (BLOCK - 1)
    return j <= t


def _pack_band(full, current):
    return jnp.where(current, full[BLOCK:], full[:BLOCK])


def _unpack_band(packed, current):
    zero = jnp.zeros_like(packed)
    return jnp.concatenate([jnp.where(current, zero, packed), jnp.where(current, packed, zero)], axis=0)


def _probs_keys_major(k_rep, q_st, bias, sink, current):
    st = _pack_band(_nt(k_rep, q_st), current) * SCALE + bias
    m = jnp.maximum(jnp.max(st, axis=0, keepdims=True), sink)
    p = jnp.exp(st - m)
    esink = jnp.exp(sink - m)
    rl = 1.0 / (jnp.sum(p, axis=0, keepdims=True) + esink)
    return p * rl, esink * rl


WINDOW_HALO = 16


def _window_sum(ext, w, forward):
    s = ext
    sh = 1
    while sh < w:
        s = s + pltpu.roll(s, (ext.shape[0] - sh) if forward else sh, 0)
        sh *= 2
    return s


def _inv_count(n, w):
    t = n * BLOCK + lax.broadcasted_iota(jnp.int32, (BLOCK, 1), 0) + 1
    return 1.0 / jnp.minimum(t.astype(F32), float(w))


def _kv_ext(ref, n):
    r0 = pl.multiple_of(jnp.maximum(n - 1, 0) * BLOCK, BLOCK)
    r1 = pl.multiple_of(n * BLOCK, BLOCK)
    return jnp.concatenate([ref[pl.ds(r0, BLOCK), :], ref[pl.ds(r1, BLOCK), :]], axis=0)


def _rows_of(vec_ref, pack_ref, row0):
    for r in range(D_MODEL // 128):
        pack_ref[row0 + r:row0 + r + 1, :] = vec_ref[:, 128 * r:128 * (r + 1)]


FRONT_TILE = 4 * BLOCK


def _fwd_front(layer, x, norm_pre, w_in_t, token, sinks, pool_w, pool_scale, bias):
    tm = FRONT_TILE

    def body(sink_ref, x_ref, g_ref, w_ref, _, pw_ref, sc_ref, bias_ref,
             u_ref, pg_ref, q_ref, k_ref, v_ref, ag_ref, z_ref, a_ref, uprev, kprev, vprev):
        i = pl.program_id(0)

        @pl.when(i == 0)
        def _():
            uprev[...] = jnp.zeros_like(uprev)
            kprev[...] = jnp.zeros_like(kprev)
            vprev[...] = jnp.zeros_like(vprev)

        xv = x_ref[...]
        r = lax.rsqrt(jnp.mean(xv * xv, axis=-1, keepdims=True) + EPS)
        h = (xv * r * g_ref[layer:layer + 1, :]).astype(BF16)
        u_ref[...] = _nt(h, w_ref[COL_U:COL_PG, :])
        pg_ref[...] = _nt(h, w_ref[COL_PG:COL_Q, :])
        for sb in range(tm // BLOCK):
            n = (tm // BLOCK) * i + sb
            rows = slice(BLOCK * sb, BLOCK * (sb + 1))
            before = slice(BLOCK * (sb - 1), BLOCK * sb)
            uv = u_ref[rows, :]
            halo = (uprev[BLOCK - WINDOW_HALO:, :] if sb == 0
                    else u_ref[BLOCK * sb - WINDOW_HALO:BLOCK * sb, :])
            ext = jnp.concatenate([halo, uv], axis=0)
            for g, w in enumerate(POOL_WINDOWS):
                cs = slice(BLOCK * g, BLOCK * (g + 1))
                win = _window_sum(ext[:, cs], w, forward=False)[WINDOW_HALO:]
                pooled = win * _inv_count(n, w) - uv[:, cs]
                mixed = _nn(pooled.astype(BF16), pw_ref[g].astype(BF16))
                gate, _ = _silu_parts(pg_ref[rows, cs])
                z_ref[rows, cs] = (mixed * sc_ref[layer:layer + 1, cs] * gate).astype(BF16)

        q_ref[...] = _nt(h, w_ref[COL_Q:COL_K, :]).astype(BF16)
        k_ref[...] = _nt(h, w_ref[COL_K:COL_V, :])
        v_ref[...] = _nt(h, w_ref[COL_V:COL_AG, :])
        ag_ref[...] = _nt(h, w_ref[COL_AG:D_IN, :])

        current = _band_is_current()
        for sb in range(tm // BLOCK):
            n = (tm // BLOCK) * i + sb
            rows = slice(BLOCK * sb, BLOCK * (sb + 1))
            before = slice(BLOCK * (sb - 1), BLOCK * sb)
            kx = jnp.concatenate([kprev[...] if sb == 0 else k_ref[before, :], k_ref[rows, :]], axis=0)
            vx = jnp.concatenate([vprev[...] if sb == 0 else v_ref[before, :], v_ref[rows, :]], axis=0)
            variant = jnp.minimum(n, 1) if sb == 0 else 1
            for kv in range(2):
                cs = slice(256 * kv, 256 * (kv + 1))
                p, _ = _probs_keys_major(_replicate_head(kx, kv), _stack_heads(q_ref[rows, cs]),
                                         bias_ref[variant, kv], _sink_row(sink_ref, layer, kv), current)
                o = _unstack_heads(_tn(_unpack_band(p.astype(BF16), current), _replicate_head(vx, kv)))
                a_ref[rows, cs] = o
                gate, _ = _silu_parts(ag_ref[rows, cs])
                z_ref[rows, D_POOL + 256 * kv:D_POOL + 256 * (kv + 1)] = (o * gate).astype(BF16)

        tail = slice(tm - BLOCK, tm)
        uprev[...] = u_ref[tail, :]
        kprev[...] = k_ref[tail, :]
        vprev[...] = v_ref[tail, :]

    row = lambda c: pl.BlockSpec((tm, c), lambda i: (i, 0))
    const = lambda shape: pl.BlockSpec(shape, lambda i: (0,) * len(shape))
    return pl.pallas_call(
        body, name=f"fwd_front{layer}", grid=(SEQ // tm,),
        in_specs=[pl.BlockSpec(memory_space=pltpu.SMEM), row(D_MODEL), const((DEPTH, D_MODEL)),
                  _resident((D_IN, D_MODEL)), const((8, 128)),
                  pl.BlockSpec((None, 4, BLOCK, BLOCK), lambda i: (layer, 0, 0, 0)), const((DEPTH, D_POOL)),
                  _resident((2, 2, BLOCK, GQA * BLOCK))],
        out_specs=[row(D_POOL), row(D_POOL), row(D_ATTN), row(D_KV), row(D_KV), row(D_ATTN), row(D_MODEL),
                   row(D_ATTN)],
        out_shape=[jax.ShapeDtypeStruct((SEQ, D_POOL), F32), jax.ShapeDtypeStruct((SEQ, D_POOL), F32),
                   jax.ShapeDtypeStruct((SEQ, D_ATTN), BF16), jax.ShapeDtypeStruct((SEQ, D_KV), F32),
                   jax.ShapeDtypeStruct((SEQ, D_KV), F32), jax.ShapeDtypeStruct((SEQ, D_ATTN), F32),
                   jax.ShapeDtypeStruct((SEQ, D_MODEL), BF16), jax.ShapeDtypeStruct((SEQ, D_ATTN), F32)],
        scratch_shapes=[pltpu.VMEM((BLOCK, D_POOL), F32), pltpu.VMEM((BLOCK, D_KV), F32),
                        pltpu.VMEM((BLOCK, D_KV), F32)],
        compiler_params=_compiler_params(("arbitrary",)),
    )(sinks, x, norm_pre, w_in_t, token, pool_w, pool_scale, bias)


def _fwd_out(layer, z, x, norm_post, w_out, token):
    tm = FWD_OUT_TILE

    def body(z_ref, x_ref, g_ref, w_ref, _, xn_ref, y_ref):
        y = _nn(z_ref[...], w_ref[...])
        y_ref[...] = y
        r = lax.rsqrt(jnp.mean(y * y, axis=-1, keepdims=True) + EPS)
        xn_ref[...] = x_ref[...] + y * r * g_ref[layer:layer + 1, :]

    row = lambda c: pl.BlockSpec((tm, c), lambda i: (i, 0))
    return pl.pallas_call(
        body, name=f"fwd_out{layer}", grid=(SEQ // tm,),
        in_specs=[row(D_MODEL), row(D_MODEL), pl.BlockSpec((DEPTH, D_MODEL), lambda i: (0, 0)),
                  _resident((D_MODEL, D_MODEL)), pl.BlockSpec((8, 128), lambda i: (0, 0))],
        out_specs=[row(D_MODEL), row(D_MODEL)],
        out_shape=[jax.ShapeDtypeStruct((SEQ, D_MODEL), F32), jax.ShapeDtypeStruct((SEQ, D_MODEL), F32)],
        compiler_params=_compiler_params(("arbitrary",)),
    )(z, x, norm_post, w_out, token)


BACK_TILE = 2 * BLOCK


def _bwd_back(layer, top, dxo_or_xf, target_or_token, y, z, norm_post, w_out, sinks, u, pg, q, k, v, ag, a,
              pool_w, pool_scale, bias):
    tm = BACK_TILE
    steps = SEQ // tm
    last = steps - 1
    per = tm // BLOCK

    def body(*refs):
        refs = list(refs)
        sink_ref, first, second = refs[:3]
        (y_ref, z_ref, g_ref, w_ref, u_ref, up_ref, pg_ref, q_ref, k_ref, v_ref, ag_ref, a_ref, pw_ref, sc_ref,
         bias_ref) = refs[3:18]
        del refs[:18]
        dxo_ref = refs.pop(0) if top else None
        dp_ref, dw_ref, pack_ref, acc, dg, lacc, dzs, ck, cv, ce = refs
        i = pl.program_id(0)
        blk = last - i

        @pl.when(i == 0)
        def _():
            acc[...] = jnp.zeros_like(acc)
            dg[...] = jnp.zeros_like(dg)
            lacc[...] = jnp.zeros_like(lacc)
            pack_ref[...] = jnp.zeros_like(pack_ref)
            ck[...] = jnp.zeros_like(ck)
            cv[...] = jnp.zeros_like(cv)
            ce[...] = jnp.zeros_like(ce)

        if top:
            d = first[...] - second[...]
            dxo_v = d * (1.0 / D_MODEL)
            dxo_ref[...] = dxo_v
            part = jnp.sum(d * d, axis=-1, keepdims=True) * (1.0 / D_MODEL)
            lacc[...] += 0.5 * jnp.sum(part, axis=0, keepdims=True)
        else:
            dxo_v = first[...]
        yv = y_ref[...]
        r = lax.rsqrt(jnp.mean(yv * yv, axis=-1, keepdims=True) + EPS)
        yn = yv * r
        dg[...] += jnp.sum(dxo_v * yn, axis=0, keepdims=True)
        dyn = dxo_v * g_ref[layer:layer + 1, :]
        dy = (r * (dyn - yn * jnp.mean(dyn * yn, axis=-1, keepdims=True))).astype(BF16)
        dzs[...] = _nt(dy, w_ref[...])
        acc[...] += _tn(z_ref[...], dy)

        lane = lax.broadcasted_iota(jnp.int32, (1, 128), 1)
        lane2 = lax.broadcasted_iota(jnp.int32, (256, 128), 1)
        current = _band_is_current()
        for sb in reversed(range(per)):
            n = per * blk + sb
            rows = slice(BLOCK * sb, BLOCK * (sb + 1))

            uv = u_ref[rows, :]
            if sb == 0:
                halo = up_ref[BLOCK - WINDOW_HALO:, :] * (n > 0).astype(F32)
            else:
                halo = u_ref[BLOCK * sb - WINDOW_HALO:BLOCK * sb, :]
            ext = jnp.concatenate([halo, uv], axis=0)
            for g, w in enumerate(POOL_WINDOWS):
                cs = slice(BLOCK * g, BLOCK * (g + 1))
                inv = _inv_count(n, w)
                win = _window_sum(ext[:, cs], w, forward=False)[WINDOW_HALO:]
                pooled = win * inv - uv[:, cs]
                pw_g = pw_ref[g].astype(BF16)
                mixed = _nn(pooled.astype(BF16), pw_g)
                gate, dgate = _silu_parts(pg_ref[rows, cs])
                dzp = dzs[rows, cs]
                sc = sc_ref[layer:layer + 1, cs]
                dpm = dzp * gate
                dp_ref[rows, COL_PG + BLOCK * g:COL_PG + BLOCK * (g + 1)] = (dzp * (mixed * sc) * dgate).astype(BF16)
                pack_ref[ROW_SC + g:ROW_SC + g + 1, :] += jnp.sum(dpm * mixed, axis=0, keepdims=True)
                dmixed = (dpm * sc).astype(BF16)
                pack_ref[ROW_PW + BLOCK * g:ROW_PW + BLOCK * (g + 1), :] += _tn(pooled.astype(BF16), dmixed)
                dpooled = _nt(dmixed, pw_g)
                e = dpooled * inv
                lead = _window_sum(jnp.concatenate([e, ce[:WINDOW_HALO, cs]], axis=0), w, forward=True)[:BLOCK]
                dp_ref[rows, COL_U + BLOCK * g:COL_U + BLOCK * (g + 1)] = (lead - dpooled).astype(BF16)
                ce[:, cs] = e

            kx = _kv_ext(k_ref, n)
            vx = _kv_ext(v_ref, n)
            variant = jnp.minimum(n, 1) if sb == 0 else 1
            dsink_row = jnp.zeros((1, 128), F32)
            tks, tvs = [], []
            for kv in range(2):
                cs = slice(256 * kv, 256 * (kv + 1))
                k_rep = _replicate_head(kx, kv)
                v_rep = _replicate_head(vx, kv)
                q_st = _stack_heads(q_ref[rows, cs])
                gate, dgate = _silu_parts(ag_ref[rows, cs])
                dza = dzs[rows, D_POOL + 256 * kv:D_POOL + 256 * (kv + 1)]
                dp_ref[rows, COL_AG + 256 * kv:COL_AG + 256 * (kv + 1)] = (dza * a_ref[rows, cs] * dgate).astype(BF16)
                da_st = _stack_heads((dza * gate).astype(BF16))
                p, psink = _probs_keys_major(k_rep, q_st, bias_ref[variant, kv], _sink_row(sink_ref, layer, kv),
                                             current)
                dpt = _pack_band(_nt(v_rep, da_st), current)
                delta = jnp.sum(p * dpt, axis=0, keepdims=True)
                dst = _unpack_band((p * (dpt - delta) * SCALE).astype(BF16), current)
                sink_terms = psink * delta
                for g in range(GQA):
                    dsink = -jnp.sum(sink_terms[:, BLOCK * g:BLOCK * (g + 1)], axis=1, keepdims=True)
                    dsink_row = dsink_row + jnp.where(lane == kv * GQA + g, dsink, 0.0)
                dp_ref[rows, COL_Q + 256 * kv:COL_Q + 256 * (kv + 1)] = _unstack_heads(_tn(dst, k_rep)).astype(BF16)
                tks.append(_fold_heads(_nn(dst, q_st)))
                tvs.append(_fold_heads(_nn(_unpack_band(p.astype(BF16), current), da_st)))
            pack_ref[ROW_SINK:ROW_SINK + 1, :] += dsink_row
            dkx = jnp.where(lane2 < 64, tks[0], tks[1])
            dvx = jnp.where(lane2 < 64, tvs[0], tvs[1])
            dp_ref[rows, COL_K:COL_V] = (ck[...] + dkx[BLOCK:]).astype(BF16)
            dp_ref[rows, COL_V:COL_AG] = (cv[...] + dvx[BLOCK:]).astype(BF16)
            ck[...] = dkx[:BLOCK]
            cv[...] = dvx[:BLOCK]

        @pl.when(i == steps - 1)
        def _():
            dw_ref[...] = acc[...].astype(BF16)
            _rows_of(dg, pack_ref, ROW_NPOST)
            pack_ref[ROW_LOSS:ROW_LOSS + 1, :] = jnp.where(lane == 0, lacc[...], 0.0)

    row = lambda c: pl.BlockSpec((tm, c), lambda i: (last - i, 0))
    const = lambda shape: pl.BlockSpec(shape, lambda i: (0,) * len(shape))
    act = jax.ShapeDtypeStruct((SEQ, D_MODEL), F32)
    return pl.pallas_call(
        body, name=f"bwd_back{layer}", grid=(steps,),
        in_specs=[pl.BlockSpec(memory_space=pltpu.SMEM), row(D_MODEL), row(D_MODEL) if top else const((8, 128)),
                  row(D_MODEL), row(D_MODEL), const((DEPTH, D_MODEL)), _resident((D_MODEL, D_MODEL)),
                  row(D_POOL), pl.BlockSpec((BLOCK, D_POOL), lambda i: (jnp.maximum(per * (last - i) - 1, 0), 0)),
                  row(D_POOL), row(D_ATTN), _resident((SEQ, D_KV)), _resident((SEQ, D_KV)), row(D_ATTN), row(D_ATTN),
                  pl.BlockSpec((None, 4, BLOCK, BLOCK), lambda i: (layer, 0, 0, 0)), const((DEPTH, D_POOL)),
                  _resident((2, 2, BLOCK, GQA * BLOCK))],
        out_specs=([row(D_MODEL)] * (1 if top else 0)
                   + [row(D_IN), const((D_MODEL, D_MODEL)), const((PACK_ROWS, 128))]),
        out_shape=([act] * (1 if top else 0)
                   + [jax.ShapeDtypeStruct((SEQ, D_IN), BF16), jax.ShapeDtypeStruct((D_MODEL, D_MODEL), BF16),
                      jax.ShapeDtypeStruct((PACK_ROWS, 128), F32)]),
        scratch_shapes=[pltpu.VMEM((D_MODEL, D_MODEL), F32), pltpu.VMEM((1, D_MODEL), F32), pltpu.VMEM((1, 1), F32),
                        pltpu.VMEM((tm, D_MODEL), F32), pltpu.VMEM((BLOCK, D_KV), F32), pltpu.VMEM((BLOCK, D_KV), F32),
                        pltpu.VMEM((BLOCK, D_POOL), F32)],
        compiler_params=_compiler_params(("arbitrary",)),
    )(sinks, dxo_or_xf, target_or_token, y, z, norm_post, w_out, u, u, pg, q, k, v, ag, a, pool_w, pool_scale, bias)


def _bwd_in(layer, part, token, dproj, x, norm_pre, dxo=None, w_in_t=None):
    want_dw, want_dx = part in ("both", "dw"), part in ("both", "dx")
    tm = TOKEN_TILE
    steps = SEQ // tm
    cw = 256

    def body(*refs):
        refs = list(refs)
        dp_ref, x_ref, g_ref = refs[1:4]
        del refs[:4]
        if want_dx:
            dxo_ref, w_ref, dx_ref, dgo_ref = refs[:4]
            del refs[:4]
            dg = refs.pop()
        if want_dw:
            dw_ref, acc = refs
        i = pl.program_id(0)

        @pl.when(i == 0)
        def _():
            if want_dw:
                acc[...] = jnp.zeros_like(acc)
            if want_dx:
                dg[...] = jnp.zeros_like(dg)

        xv = x_ref[...]
        gv = g_ref[layer:layer + 1, :]
        r = lax.rsqrt(jnp.mean(xv * xv, axis=-1, keepdims=True) + EPS)
        xn = xv * r
        if want_dw:
            hb = (xn * gv).astype(BF16)
            for c in range(0, D_IN, cw):
                acc[c:c + cw, :] += _tn(dp_ref[:, c:c + cw], hb)
        if want_dx:
            dh = _nn(dp_ref[...], w_ref[...])
            dg[...] += jnp.sum(dh * xn, axis=0, keepdims=True)
            dhn = dh * gv
            dx_ref[...] = dxo_ref[...] + r * (dhn - xn * jnp.mean(dhn * xn, axis=-1, keepdims=True))

        @pl.when(i == steps - 1)
        def _():
            if want_dw:
                dw_ref[...] = acc[...].astype(BF16)
            if want_dx:
                _rows_of(dg, dgo_ref, 0)

    row = lambda c: pl.BlockSpec((tm, c), lambda i: (i, 0))
    const = lambda shape: pl.BlockSpec(shape, lambda i: (0,) * len(shape))
    in_specs = [const((8, 128)), row(D_IN), row(D_MODEL), const((DEPTH, D_MODEL))]
    operands = [token, dproj, x, norm_pre]
    out_specs, out_shape, scratch = [], [], []
    if want_dx:
        in_specs += [row(D_MODEL), _resident((D_IN, D_MODEL))]
        operands += [dxo, w_in_t]
        out_specs += [row(D_MODEL), const((8, 128))]
        out_shape += [jax.ShapeDtypeStruct((SEQ, D_MODEL), F32), jax.ShapeDtypeStruct((8, 128), F32)]
    if want_dw:
        out_specs.append(const((D_IN, D_MODEL)))
        out_shape.append(jax.ShapeDtypeStruct((D_IN, D_MODEL), BF16))
        scratch.append(pltpu.VMEM((D_IN, D_MODEL), F32))
    if want_dx:
        scratch.append(pltpu.VMEM((1, D_MODEL), F32))
    return pl.pallas_call(
        body, name=f"bwd_in_{part}{layer}", grid=(steps,),
        in_specs=in_specs, out_specs=out_specs, out_shape=out_shape, scratch_shapes=scratch,
        compiler_params=_compiler_params(("arbitrary",)),
    )(*operands)


def _mesh_pos():
    return lax.axis_index("x"), lax.axis_index("y"), lax.axis_index("c")


def _device_rows(ref, m, px, py, pc):
    return ref.at[pl.ds(pl.multiple_of((4 * px + 2 * py + pc) * m, 16 if m % 16 == 0 else 8), m), :]


def _allgather(srcs, out_dtype, name, later=()):
    na, nb = len(srcs), len(later)
    every = list(srcs) + list(later)
    shapes = [(a.shape[-2], a.shape[-1]) for a, _ in every]

    def body(*refs):
        xs, refs = refs[:na + nb], refs[na + nb:]
        outs, cast, land, refs = refs[:na], refs[na:na + nb], refs[na + nb:na + 2 * nb], refs[na + 2 * nb:]
        stage, (send_sems, recv_sems, local_sems) = refs[:na], refs[na:]
        x, y, c = _mesh_pos()
        me, sibling = (x, y, c), (x, y, 1 - c)
        near = [(1 - x, y), (x, 1 - y)]
        far = (1 - x, 1 - y)
        relay_from, relay_to = (x ^ (1 - c), y ^ c), (x ^ c, y ^ (1 - c))
        _handshake([sibling] + [(*chip, c) for chip in near])
        k_from, k_to = 1 + c, 2 - c

        def slot(a, px, py, pc):
            return _device_rows(outs[a], shapes[a][0], px, py, pc)

        def copy(a, k, block, to, src=None):
            return pltpu.make_async_remote_copy(
                src_ref=slot(a, *block) if src is None else src, dst_ref=slot(a, *block),
                send_sem=send_sems.at[a, k], recv_sem=recv_sems.at[a, k], device_id=to, device_id_type=MESH)

        def cast_block(i):
            layer = every[i][1]
            return (xs[i][...] if layer is None else xs[i][layer]).astype(out_dtype)

        for a in range(na):
            stage[a][...] = cast_block(a)
        mine = [pltpu.make_async_copy(stage[a], slot(a, *me), local_sems.at[a]) for a in range(na)]
        for cp in mine:
            cp.start()
        sent = []
        for a in range(na):
            sent.append(copy(a, 0, me, sibling, src=stage[a]))
            sent += [copy(a, 1 + j, me, (*chip, c), src=stage[a]) for j, chip in enumerate(near)]
        for cp in sent:
            cp.start()
        for b in range(nb):
            cast[b][...] = cast_block(na + b)
            cp = pltpu.make_async_copy(cast[b], _device_rows(land[b], shapes[na + b][0], *me), local_sems.at[na + b])
            cp.start()
            mine.append(cp)
        for a in range(na):
            copy(a, k_from, (*relay_from, c), me).wait_recv()
            sent += [copy(a, 3, (*relay_from, c), (*relay_to, c)), copy(a, 3 + k_from, (*relay_from, c), sibling)]
            sent[-2].start()
            sent[-1].start()
        for a in range(na):
            copy(a, k_to, (*relay_to, c), me).wait_recv()
            sent.append(copy(a, 3 + k_to, (*relay_to, c), sibling))
            sent[-1].start()
        for a in range(na):
            copy(a, 3, (*far, c), me).wait_recv()
            sent.append(copy(a, 6, (*far, c), sibling))
            sent[-1].start()
        for a in range(na):
            copy(a, 0, sibling, me).wait_recv()
            for j, chip in enumerate(near + [far]):
                copy(a, 4 + j, (*chip, 1 - c), me).wait_recv()
        for cp in sent:
            cp.wait_send()
        for cp in mine:
            cp.wait()

    vmem = pl.BlockSpec(memory_space=pltpu.VMEM)
    hbm = pl.BlockSpec(memory_space=pl.ANY)
    gathered = [jax.ShapeDtypeStruct((N_DEV * m, n), out_dtype) for m, n in shapes]
    out = pl.pallas_call(
        body, name=name,
        in_specs=[vmem] * (na + nb),
        out_specs=[hbm] * na + [vmem] * nb + [hbm] * nb,
        out_shape=gathered[:na] + [jax.ShapeDtypeStruct(s, out_dtype) for s in shapes[na:]] + gathered[na:],
        scratch_shapes=([pltpu.VMEM(s, out_dtype) for s in shapes[:na]]
                        + [pltpu.SemaphoreType.DMA((na, 7)), pltpu.SemaphoreType.DMA((na, 7)),
                           pltpu.SemaphoreType.DMA((na + nb,))]),
        compiler_params=pltpu.CompilerParams(vmem_limit_bytes=VMEM_LIMIT, collective_id=COLLECTIVE_GATHER_W0),
    )(*[a for a, _ in every])
    return out[:na], out[na:na + nb], out[na + nb:]


ALL_PEERS = tuple(range(1, N_DEV))
SIBLING_AND_SAME_CORE = (1, 2, 4, 6)


def _related(k, x, y, c):
    return x ^ ((k >> 2) & 1), y ^ ((k >> 1) & 1), c ^ (k & 1)


def _gather_start(blocks, lands, relations, collective_id, name):
    na = len(blocks)

    def body(*refs):
        src, land, sems, token = refs[:na], refs[na:2 * na], refs[2 * na:4 * na], refs[-1]
        x, y, c = _mesh_pos()
        _handshake([_related(k, x, y, c) for k in sorted(set().union(*relations))])
        for a in range(na):
            for k in relations[a]:
                pltpu.make_async_remote_copy(
                    src_ref=src[a], dst_ref=_device_rows(land[a], blocks[a].shape[0], x, y, c),
                    send_sem=sems[2 * a].at[k - 1], recv_sem=sems[2 * a + 1].at[k - 1],
                    device_id=_related(k, x, y, c), device_id_type=MESH).start()
        token[...] = jnp.zeros_like(token)

    bufs = [pltpu.HBM(t.shape, t.dtype) for t in list(blocks) + list(lands)]
    out = pl.pallas_call(
        body, name=name,
        out_shape=(*([pltpu.SemaphoreType.DMA((N_DEV - 1,))] * (2 * na)), *bufs, jax.ShapeDtypeStruct((8, 128), F32)),
        in_specs=[_HBM] * (2 * na),
        out_specs=(*([_SEM] * (2 * na)), *([_HBM] * (2 * na)), pl.BlockSpec(memory_space=pltpu.VMEM)),
        input_output_aliases={i: 2 * na + i for i in range(2 * na)},
        compiler_params=pltpu.CompilerParams(has_side_effects=_EFFECT, collective_id=collective_id),
    )(*[pltpu.with_memory_space_constraint(t, pltpu.HBM) for t in list(blocks) + list(lands)])
    sems = [(out[2 * a], out[2 * a + 1]) for a in range(na)]
    return sems, out[2 * na:3 * na], out[3 * na:4 * na], out[-1]


def _gather_wait(sems, block, land, relations, after, name):
    def body(src, land_ref, send_sem, recv_sem, after_ref, src_out, land_out):
        x, y, c = _mesh_pos()
        for k in relations:
            peer = _related(k, x, y, c)
            cp = pltpu.make_async_remote_copy(
                src_ref=src, dst_ref=_device_rows(land_ref, block.shape[0], *peer),
                send_sem=send_sem.at[k - 1], recv_sem=recv_sem.at[k - 1], device_id=peer, device_id_type=MESH)
            cp.wait_send()
            cp.wait_recv()

    out = pl.pallas_call(
        body, name=name,
        out_shape=(pltpu.HBM(block.shape, block.dtype), pltpu.HBM(land.shape, land.dtype)),
        in_specs=[_HBM, _HBM, _SEM, _SEM, pl.BlockSpec(memory_space=pl.ANY)],
        out_specs=[_HBM, _HBM],
        input_output_aliases={0: 0, 1: 1},
        compiler_params=pltpu.CompilerParams(has_side_effects=_EFFECT),
    )(block, land, sems[0], sems[1], after)
    return out[1]


(COLLECTIVE_GATHER_W0, COLLECTIVE_GATHER_W1, COLLECTIVE_FORWARD_W_IN1, COLLECTIVE_EXCHANGE_1, COLLECTIVE_EXCHANGE_0A,
 COLLECTIVE_PAIR_REDUCE, COLLECTIVE_EXCHANGE_0B, COLLECTIVE_GATHER_SMALL) = range(1, 9)


def _handshake(peers):
    barrier = pltpu.get_barrier_semaphore()
    for peer in peers:
        pl.semaphore_signal(barrier, inc=1, device_id=peer, device_id_type=MESH)
    pl.semaphore_wait(barrier, len(peers))


def _forward_plan(land_ref, m):
    x, y, c = _mesh_pos()
    return [_device_rows(land_ref, m, qx, qy, c) for qx, qy in ((1 - x, y), (x, 1 - y), (1 - x, 1 - y))], (x, y, 1 - c)


def _forward_start(land, m, name):
    def body(land_ref, send_sem, recv_sem, land_out, token):
        _handshake([(lax.axis_index("x"), lax.axis_index("y"), 1 - lax.axis_index("c"))])
        rows, sibling = _forward_plan(land_ref, m)
        for j, r in enumerate(rows):
            pltpu.make_async_remote_copy(src_ref=r, dst_ref=r, send_sem=send_sem.at[j], recv_sem=recv_sem.at[j],
                                         device_id=sibling, device_id_type=MESH).start()
        token[...] = jnp.zeros_like(token)

    out = pl.pallas_call(
        body, name=name,
        out_shape=(pltpu.SemaphoreType.DMA((3,)), pltpu.SemaphoreType.DMA((3,)), pltpu.HBM(land.shape, land.dtype),
                   jax.ShapeDtypeStruct((8, 128), F32)),
        in_specs=[_HBM],
        out_specs=(_SEM, _SEM, _HBM, pl.BlockSpec(memory_space=pltpu.VMEM)),
        input_output_aliases={0: 2},
        compiler_params=pltpu.CompilerParams(has_side_effects=_EFFECT, collective_id=COLLECTIVE_FORWARD_W_IN1),
    )(pltpu.with_memory_space_constraint(land, pltpu.HBM))
    return (out[0], out[1]), out[2], out[3]


def _forward_wait(sems, land, m, after, name):
    def body(land_ref, send_sem, recv_sem, after_ref, land_out):
        x, y, c = _mesh_pos()
        mine, sibling = _forward_plan(land_ref, m)
        theirs = [_device_rows(land_ref, m, qx, qy, 1 - c) for qx, qy in ((1 - x, y), (x, 1 - y), (1 - x, 1 - y))]
        for j in range(3):
            cp = pltpu.make_async_remote_copy(src_ref=mine[j], dst_ref=theirs[j], send_sem=send_sem.at[j],
                                              recv_sem=recv_sem.at[j], device_id=sibling, device_id_type=MESH)
            cp.wait_send()
            cp.wait_recv()

    return pl.pallas_call(
        body, name=name,
        out_shape=pltpu.HBM(land.shape, land.dtype),
        in_specs=[_HBM, _SEM, _SEM, pl.BlockSpec(memory_space=pl.ANY)],
        out_specs=_HBM,
        input_output_aliases={0: 0},
        compiler_params=pltpu.CompilerParams(has_side_effects=_EFFECT),
    )(land, sems[0], sems[1], after)


def _row_step(m):
    return next(s for s in (32, 24, 16, 8) if m % s == 0)


def _pair_reduce(arrs, name):
    na = len(arrs)

    def body(*refs):
        gs, hs, hm = refs[:na], refs[na:2 * na], refs[2 * na:3 * na]
        own, ra = refs[3 * na:4 * na], refs[4 * na:5 * na]
        d2d_send, d2d_recv, local_sems = refs[5 * na:]
        x, y, c = _mesh_pos()
        sibling = (x, y, 1 - c)
        _handshake([sibling])
        loads, sends = [], []
        for a in range(na):
            for q in range(4):
                cp = pltpu.make_async_copy(gs[a].at[2 * q + c], own[a].at[q], local_sems.at[a, q])
                cp.start()
                loads.append(cp)
                cp = pltpu.make_async_remote_copy(
                    src_ref=gs[a].at[2 * q + (1 - c)], dst_ref=ra[a].at[q], send_sem=d2d_send.at[a, q],
                    recv_sem=d2d_recv.at[a, q], device_id=sibling, device_id_type=MESH)
                cp.start()
                sends.append(cp)
        for cp in loads:
            cp.wait()
        for cp in sends:
            cp.wait_recv()
        others = [2 * (1 - x) + y, 2 * x + (1 - y), 2 * (1 - x) + (1 - y)]
        for a in range(na):
            m = arrs[a].shape[1]
            step = _row_step(m)

            def add(i, carry, a=a, step=step):
                rs = pl.ds(pl.multiple_of(i * step, step), step)
                for j, q in enumerate(others):
                    hs[a][j, rs, :] = (own[a][q, rs, :].astype(F32) + ra[a][q, rs, :].astype(F32)).astype(hs[a].dtype)
                q = 2 * x + y
                hm[a][rs, :] = own[a][q, rs, :].astype(F32) + ra[a][q, rs, :].astype(F32)
                return carry

            lax.fori_loop(0, m // step, add, 0)
        for cp in sends:
            cp.wait_send()

    vmem = pl.BlockSpec(memory_space=pltpu.VMEM)
    scratch = [pltpu.VMEM((4,) + t.shape[1:], t.dtype) for t in arrs] * 2
    scratch += [pltpu.SemaphoreType.DMA((na, 4)), pltpu.SemaphoreType.DMA((na, 4)), pltpu.SemaphoreType.DMA((na, 4))]
    out = pl.pallas_call(
        body, name=name,
        in_specs=[pl.BlockSpec(memory_space=pl.ANY)] * na, out_specs=[vmem] * (2 * na),
        out_shape=([jax.ShapeDtypeStruct((3,) + t.shape[1:], t.dtype) for t in arrs]
                   + [jax.ShapeDtypeStruct(t.shape[1:], F32) for t in arrs]),
        scratch_shapes=scratch,
        compiler_params=pltpu.CompilerParams(vmem_limit_bytes=VMEM_LIMIT, collective_id=COLLECTIVE_PAIR_REDUCE),
    )(*arrs)
    return out[:na], out[na:]


_HBM = pl.BlockSpec(memory_space=pltpu.HBM)
_SEM = pl.BlockSpec(memory_space=pltpu.SEMAPHORE)
_EFFECT = pltpu.SideEffectType.DATAFLOW_SIDE_EFFECTING


def _exchange_plan(direct):
    x, y, c = _mesh_pos()
    if not direct:
        return [(j, j, (qx, qy, c)) for j, (qx, qy) in enumerate([(1 - x, y), (x, 1 - y), (1 - x, 1 - y)])]
    plan = []
    for k in range(1, N_DEV):
        px, py, pc = x ^ ((k >> 2) & 1), y ^ ((k >> 1) & 1), c ^ (k & 1)
        plan.append((4 * px + 2 * py + pc, k - 1, (px, py, pc)))
    return plan


def _exchange_start(srcs, direct, collective_id, name):
    na = len(srcs)
    slots = N_DEV - 1 if direct else 3

    def body(*refs):
        src, land = refs[:na], refs[na:2 * na]
        send_sem, recv_sem = refs[2 * na], refs[2 * na + 1]
        token = refs[-1]
        plan = _exchange_plan(direct)
        _handshake([peer for _, _, peer in plan])
        for block, slot, peer in plan:
            for a in range(na):
                pltpu.make_async_remote_copy(
                    src_ref=src[a].at[block], dst_ref=land[a].at[slot], send_sem=send_sem.at[slots * a + slot],
                    recv_sem=recv_sem.at[slots * a + slot], device_id=peer, device_id_type=MESH).start()
        token[...] = jnp.zeros_like(token)

    zones = [jax.ShapeDtypeStruct((slots,) + t.shape[1:], t.dtype) for t in srcs]
    bufs = [pltpu.HBM(t.shape, t.dtype) for t in list(srcs) + zones]
    out = pl.pallas_call(
        body, name=name,
        out_shape=(pltpu.SemaphoreType.DMA((slots * na,)), pltpu.SemaphoreType.DMA((slots * na,)), *bufs,
                   jax.ShapeDtypeStruct((8, 128), F32)),
        in_specs=[_HBM] * (2 * na),
        out_specs=(_SEM, _SEM, *([_HBM] * (2 * na)), pl.BlockSpec(memory_space=pltpu.VMEM)),
        input_output_aliases={i: 2 + i for i in range(2 * na)},
        compiler_params=pltpu.CompilerParams(has_side_effects=_EFFECT, collective_id=collective_id),
    )(*[pltpu.with_memory_space_constraint(t, pltpu.HBM) for t in srcs],
      *[pltpu.with_memory_space_constraint(lax.empty(t.shape, t.dtype), pltpu.HBM) for t in zones])
    return out[0], out[1], out[2:2 + na], out[2 + na:2 + 2 * na], out[-1]


def _exchange_wait(send_sem, recv_sem, srcs, lands, direct, after, name):
    na = len(srcs)
    slots = N_DEV - 1 if direct else 3

    def body(*refs):
        src, land = refs[:na], refs[na:2 * na]
        send_sem_ref, recv_sem_ref = refs[2 * na], refs[2 * na + 1]
        for block, slot, peer in _exchange_plan(direct):
            for a in range(na):
                cp = pltpu.make_async_remote_copy(
                    src_ref=src[a].at[block], dst_ref=land[a].at[slot], send_sem=send_sem_ref.at[slots * a + slot],
                    recv_sem=recv_sem_ref.at[slots * a + slot], device_id=peer, device_id_type=MESH)
                cp.wait_send()
                cp.wait_recv()

    bufs = [pltpu.HBM(t.shape, t.dtype) for t in list(srcs) + list(lands)]
    out = pl.pallas_call(
        body, name=name,
        out_shape=tuple(bufs),
        in_specs=[_HBM] * (2 * na) + [_SEM, _SEM, pl.BlockSpec(memory_space=pl.ANY)],
        out_specs=[_HBM] * (2 * na),
        input_output_aliases={i: i for i in range(2 * na)},
        compiler_params=pltpu.CompilerParams(has_side_effects=_EFFECT),
    )(*srcs, *lands, send_sem, recv_sem, after)
    return out[:na], out[na:]


def _own_then_slots(mine_ref, lands_ref, rows=slice(None)):
    if len(mine_ref.shape) == 3:
        x, y, c = _mesh_pos()
        total = mine_ref[4 * x + 2 * y + c, rows, :].astype(F32)
    else:
        total = mine_ref[rows, :].astype(F32)
    for j in range(lands_ref.shape[0]):
        total = total + lands_ref[j, rows, :].astype(F32)
    return total


SMALL_ROWS = 2 * PACK_SLICE + 2 * 8


def _small_block(mine, lands, dgpre, name):
    def body(*refs):
        hm, ld, dg = refs[:DEPTH], refs[DEPTH:2 * DEPTH], refs[2 * DEPTH:3 * DEPTH]
        blk, land, sem = refs[3 * DEPTH:]
        for l in range(DEPTH):
            blk[PACK_SLICE * l:PACK_SLICE * (l + 1), :] = _own_then_slots(hm[l], ld[l])
            blk[2 * PACK_SLICE + 8 * l:2 * PACK_SLICE + 8 * (l + 1), :] = dg[l][...]
        cp = pltpu.make_async_copy(blk, _device_rows(land, SMALL_ROWS, *_mesh_pos()), sem)
        cp.start()
        cp.wait()

    vmem = pl.BlockSpec(memory_space=pltpu.VMEM)
    return pl.pallas_call(
        body, name=name,
        in_specs=[vmem] * (3 * DEPTH), out_specs=[vmem, pl.BlockSpec(memory_space=pl.ANY)],
        out_shape=[jax.ShapeDtypeStruct((SMALL_ROWS, 128), F32), jax.ShapeDtypeStruct((N_DEV * SMALL_ROWS, 128), F32)],
        scratch_shapes=[pltpu.SemaphoreType.DMA],
        compiler_params=_compiler_params(),
    )(*mine, *lands, *dgpre)


def _adamw_math(w, g, m, v):
    m = ADAM_B1 * m + (1.0 - ADAM_B1) * g
    v = ADAM_B2 * v + (1.0 - ADAM_B2) * (g * g)
    m_hat = m / (1.0 - ADAM_B1 ** ADAM_STEP)
    v_hat = v / (1.0 - ADAM_B2 ** ADAM_STEP)
    delta = -ADAM_LR * (m_hat / (jnp.sqrt(v_hat) + ADAM_EPS) + ADAM_WD * w)
    return delta, m, v


def _adamw_layer(layer, mine, lands, w, m, v, earlier, token, name, rows):
    _, mm, nn = w.shape

    def body(hm_ref, ld_ref, w_ref, m_ref, v_ref, _, *refs):
        g_ref, d_ref, nm_ref, nv_ref = refs[-4:]
        g = _own_then_slots(hm_ref, ld_ref)
        g_ref[...] = g
        d, nm, nv = _adamw_math(w_ref[...], g, m_ref[...], v_ref[...])
        d_ref[...] = d
        nm_ref[...] = nm
        nv_ref[...] = nv

    spec = pl.BlockSpec((None, rows, nn), lambda i: (layer, i, 0))
    carried = [] if earlier is None else list(earlier)
    return pl.pallas_call(
        body, name=name, grid=(mm // rows,),
        in_specs=([pl.BlockSpec((rows, nn), lambda i: (i, 0)) if mine.ndim == 2
                   else pl.BlockSpec((N_DEV, rows, nn), lambda i: (0, i, 0)),
                   pl.BlockSpec((lands.shape[0], rows, nn), lambda i: (0, i, 0)),
                   spec, spec, spec] + [pl.BlockSpec(memory_space=pl.ANY)] * (1 + len(carried))),
        out_specs=[spec] * 4,
        out_shape=[jax.ShapeDtypeStruct(w.shape, F32)] * 4,
        input_output_aliases={6 + t: t for t in range(len(carried))},
        compiler_params=_compiler_params(("arbitrary",)),
    )(mine, lands, w, m, v, token, *carried)


def _adamw_small(gathered, params):
    def body(all_ref, *refs):
        ins, outs, packs = refs[:15], refs[15:15 + 21], refs[15 + 21]
        loss_ref = outs[0]
        for dev in range(N_DEV):
            for l in range(DEPTH):
                packs[l, PACK_SLICE * dev:PACK_SLICE * (dev + 1), :] = (
                    all_ref[SMALL_ROWS * dev + PACK_SLICE * l:SMALL_ROWS * dev + PACK_SLICE * (l + 1), :])
        loss_ref[...] = packs[DEPTH - 1, ROW_LOSS:ROW_LOSS + 1, 0:1]

        def update(p, sel, g):
            w_ref, m_ref, v_ref = ins[p], ins[5 + p], ins[10 + p]
            d, nm, nv = _adamw_math(w_ref[sel], g, m_ref[sel], v_ref[sel])
            for t, val in enumerate((g, d, nm, nv)):
                outs[1 + 5 * t + p][sel] = val

        for l in range(DEPTH):
            gp = packs.at[l]
            row0 = 2 * PACK_SLICE + 8 * l
            dgpre = all_ref[row0:row0 + 8, :]
            for dev in range(1, N_DEV):
                dgpre = dgpre + all_ref[SMALL_ROWS * dev + row0:SMALL_ROWS * dev + row0 + 8, :]
            for grp in range(4):
                update(0, (l, grp), gp[ROW_PW + BLOCK * grp:ROW_PW + BLOCK * (grp + 1), :])
                update(1, (slice(l, l + 1), slice(128 * grp, 128 * (grp + 1))), gp[ROW_SC + grp:ROW_SC + grp + 1, :])
            update(2, (slice(l, l + 1), slice(None)), gp[ROW_SINK:ROW_SINK + 1, 0:N_HEADS])
            for r in range(D_MODEL // 128):
                sel = (slice(l, l + 1), slice(128 * r, 128 * (r + 1)))
                update(3, sel, dgpre[r:r + 1, :])
                update(4, sel, gp[ROW_NPOST + r:ROW_NPOST + r + 1, :])

    shapes = [jax.ShapeDtypeStruct(p.shape, F32) for p in params[:5]]
    return pl.pallas_call(
        body, name="adamw_small",
        out_shape=[jax.ShapeDtypeStruct((1, 1), F32)] + shapes * 4,
        scratch_shapes=[pltpu.VMEM((DEPTH, PACK_ROWS, 128), F32)],
        compiler_params=_compiler_params(),
    )(gathered, *params)


def kernel(x, w_in, pool_w, pool_scale, attn_sinks, w_out, norm_pre, norm_post, loss_target, m_w_in, m_pool_w, m_pool_scale, m_attn_sinks, m_w_out, m_norm_pre, m_norm_post, v_w_in, v_pool_w, v_pool_scale, v_attn_sinks, v_w_out, v_norm_pre, v_norm_post):
    x0 = x.reshape(SEQ, D_MODEL)
    target = loss_target.reshape(SEQ, D_MODEL)
    bias = jnp.asarray(_attn_bias())
    w_in_t, m_in_t, v_in_t = (jnp.swapaxes(t, 1, 2) for t in (w_in, m_w_in, v_w_in))

    (win0, wout0), later, lands = _allgather([(w_in_t, 0), (w_out, 0)], BF16, "gather_w0",
                                              later=[(w_in_t, 1), (w_out, 1)])
    sems, later, lands, token = _gather_start(later, lands, [SIBLING_AND_SAME_CORE, ALL_PEERS], COLLECTIVE_GATHER_W1,
                                              "gather_w1_start")
    win_full, wout_full = [win0, None], [wout0, None]

    saved = []
    xl = x0
    for layer in range(DEPTH):
        u, pg, q, k, v, ag, z, a = _fwd_front(layer, xl, norm_pre, win_full[layer], token, attn_sinks,
                                             pool_w, pool_scale, bias)
        if layer == 0:
            land = _gather_wait(sems[0], later[0], lands[0], SIBLING_AND_SAME_CORE, z, "gather_w_in1_wait")
            fsems, land, token = _forward_start(land, IN_SHARD, "forward_w_in1_start")
        else:
            wout_full[layer] = _gather_wait(sems[1], later[1], lands[1], ALL_PEERS, z, "gather_w_out1_wait")
        x_next, y = _fwd_out(layer, z, xl, norm_post, wout_full[layer], token)
        if layer == 0:
            win_full[1] = _forward_wait(fsems, land, IN_SHARD, x_next, "forward_w_in1_wait")
        saved.append((xl, u, pg, q, k, v, ag, z, a, y))
        xl = x_next

    params_small = [pool_w, pool_scale, attn_sinks, norm_pre, norm_post,
                    m_pool_w, m_pool_scale, m_attn_sinks, m_norm_pre, m_norm_post,
                    v_pool_w, v_pool_scale, v_attn_sinks, v_norm_pre, v_norm_post]

    def start(blocks, direct, collective_id, tag):
        srcs, mine = (blocks, None) if direct else _pair_reduce(blocks, f"pair_reduce{tag}")
        send_sem, recv_sem, srcs, lands, started = _exchange_start(srcs, direct, collective_id, f"exchange_start{tag}")
        return (send_sem, recv_sem, srcs, lands, mine, direct), started

    def finish(handle, after, tag):
        send_sem, recv_sem, srcs, lands, mine, direct = handle
        srcs, lands = _exchange_wait(send_sem, recv_sem, srcs, lands, direct, after, f"exchange_wait{tag}")
        return (srcs if mine is None else mine), lands

    def back(layer, top, first, second):
        xin, u, pg, q, k, v, ag, z, a, y = saved[layer]
        return _bwd_back(layer, top, first, second, y, z, norm_post, wout_full[layer], attn_sinks, u, pg, q, k, v,
                         ag, a, pool_w, pool_scale, bias)

    dgpre = [None] * DEPTH
    dx, dproj, gw_out, pack = back(1, True, xl, target)
    dx, dgpre[1], gw_in_t = _bwd_in(1, "both", token, dproj, saved[1][0], norm_pre, dx, win_full[1])
    top, token = start([gw_in_t.reshape(N_DEV, IN_SHARD, D_MODEL), gw_out.reshape(N_DEV, OUT_SHARD, D_MODEL),
                        pack.reshape(N_DEV, PACK_SLICE, 128)], True, COLLECTIVE_EXCHANGE_1, "1")

    dproj, gw_out, pack = back(0, False, dx, token)
    early, token = start([gw_out.reshape(N_DEV, OUT_SHARD, D_MODEL), pack.reshape(N_DEV, PACK_SLICE, 128)], True,
                         COLLECTIVE_EXCHANGE_0A, "0a")
    (gw_in_t,) = _bwd_in(0, "dw", token, dproj, saved[0][0], norm_pre)
    late, token = start([gw_in_t.reshape(N_DEV, IN_SHARD, D_MODEL)], False, COLLECTIVE_EXCHANGE_0B, "0b")
    dx, dgpre[0] = _bwd_in(0, "dx", token, dproj, saved[0][0], norm_pre, dx, win_full[0])

    own1, lands1 = finish(top, dx, "1")
    big_in = _adamw_layer(1, own1[0], lands1[0], w_in_t, m_in_t, v_in_t, None, token, "adamw_in1", ADAM_ROWS_IN)
    big_out = _adamw_layer(1, own1[1], lands1[1], w_out, m_w_out, v_w_out, None, token, "adamw_out1", ADAM_ROWS_OUT)
    own0a, lands0a = finish(early, big_out[0], "0a")
    block, land = _small_block([own0a[1], own1[2]], [lands0a[1], lands1[2]], dgpre, "small_block")
    sems, block, land, token = _gather_start([block], [land], [ALL_PEERS], COLLECTIVE_GATHER_SMALL, "gather_small_start")
    big_out = _adamw_layer(0, own0a[0], lands0a[0], w_out, m_w_out, v_w_out, big_out, token, "adamw_out0", ADAM_ROWS_OUT)
    own0b, lands0b = finish(late, big_out[0], "0b")
    big_in = _adamw_layer(0, own0b[0], lands0b[0], w_in_t, m_in_t, v_in_t, big_in, token, "adamw_in0", ADAM_ROWS_IN)
    gathered = _gather_wait(sems[0], block[0], land[0], ALL_PEERS, big_in[0], "gather_small_wait")
    small_out = _adamw_small(gathered, params_small)
    loss = small_out[0].reshape(())

    outs = [loss, dx.reshape(1, SEQ, D_MODEL)]
    for t in range(4):
        pw_, sc_, sk_, npre_, npost_ = small_out[1 + 5 * t:6 + 5 * t]
        outs += [jnp.swapaxes(big_in[t], 1, 2), pw_, sc_, sk_, big_out[t], npre_, npost_]
    return tuple(outs)
```

```python
import numpy as np
import jax
import jax.numpy as jnp
from jax import lax
from jax.experimental import pallas as pl
from jax.experimental.pallas import tpu as pltpu

F32 = jnp.float32
BF16 = jnp.bfloat16

N_DEV = 8
SEQ = 2048
D_MODEL = 1024
D_POOL = 512
D_ATTN = 512
D_KV = 128
D_IN = 2304
N_HEADS = 8
GQA = 4
HEAD_DIM = 64
BLOCK = 128
POOL_WINDOWS = (2, 4, 8, 16)
DEPTH = 2
EPS = 1e-6
NEG_INF = -1e30
SCALE = HEAD_DIM ** -0.5
IN_SHARD = D_IN // N_DEV
OUT_SHARD = D_MODEL // N_DEV

COL_U, COL_PG, COL_Q, COL_K, COL_V, COL_AG = 0, 512, 1024, 1536, 1664, 1792

ADAM_LR = 0.001
ADAM_B1 = 0.9
ADAM_B2 = 0.999
ADAM_EPS = 1e-08
ADAM_WD = 0.01
ADAM_STEP = 10

TOKEN_TILE = 512
FWD_OUT_TILE = 1024
ADAM_ROWS_IN, ADAM_ROWS_OUT = 144, 128
VMEM_LIMIT = 56 * 1024 * 1024
MESH = pl.DeviceIdType.MESH

ROW_PW, ROW_SC, ROW_SINK, ROW_NPOST, ROW_LOSS = 0, 512, 520, 536, 544
PACK_ROWS = 576
PACK_SLICE = PACK_ROWS // N_DEV


def _nn(a, b):
    return jnp.dot(a, b, preferred_element_type=F32)


def _nt(a, b):
    return lax.dot_general(a, b, (((1,), (1,)), ((), ())), preferred_element_type=F32)


def _tn(a, b):
    return lax.dot_general(a, b, (((0,), (0,)), ((), ())), preferred_element_type=F32)


def _silu_parts(g):
    s = jax.nn.sigmoid(g)
    return g * s, s * (1.0 + g * (1.0 - s))


def _resident(shape):
    return pl.BlockSpec(shape, lambda *_: (0,) * len(shape), pipeline_mode=pl.Buffered(1))


def _compiler_params(sem=None):
    if sem is None:
        return pltpu.CompilerParams(vmem_limit_bytes=VMEM_LIMIT)
    return pltpu.CompilerParams(dimension_semantics=sem, vmem_limit_bytes=VMEM_LIMIT)


def _attn_bias():
    t = np.arange(BLOCK)[None, :]
    j = np.arange(BLOCK)[:, None]
    current = j <= t
    dist = np.where(current, t - j, t + BLOCK - j).astype(np.float32)
    out = np.zeros((2, 2, BLOCK, GQA * BLOCK), np.float32)
    for variant in range(2):
        valid = current | (variant == 1)
        for kv in range(2):
            for g in range(GQA):
                slope = np.float32(2.0 ** (-(kv * GQA + g + 1)))
                out[variant, kv, :, g * BLOCK:(g + 1) * BLOCK] = np.where(valid, -slope * dist, np.float32(NEG_INF))
    return out


def _replicate_head(kx, kv):
    rolled = pltpu.roll(kx, 64, 1)
    lane = lax.broadcasted_iota(jnp.int32, kx.shape, 1)
    dup = jnp.where(lane < 64, kx, rolled) if kv == 0 else jnp.where(lane < 64, rolled, kx)
    return jnp.concatenate([dup, dup], axis=1).astype(BF16)


def _stack_heads(qv):
    lane = lax.broadcasted_iota(jnp.int32, qv.shape, 1)
    zero = jnp.zeros_like(qv)
    return jnp.concatenate([jnp.where((lane >= 64 * g) & (lane < 64 * g + 64), qv, zero) for g in range(GQA)], axis=0)


def _unstack_heads(xs):
    lane = lax.broadcasted_iota(jnp.int32, (BLOCK, 256), 1)
    return jnp.where(lane < 64, xs[0:128], jnp.where(lane < 128, xs[128:256], jnp.where(lane < 192, xs[256:384], xs[384:512])))


def _fold_heads(r):
    h = r[:, 0:128] + r[:, 128:256]
    return h + pltpu.roll(h, 64, 1)


def _sink_row(sink_ref, layer, kv):
    lane = lax.broadcasted_iota(jnp.int32, (1, GQA * BLOCK), 1)
    s4 = [sink_ref[layer, kv * GQA + g] for g in range(GQA)]
    return jnp.where(lane < 128, s4[0], jnp.where(lane < 256, s4[1], jnp.where(lane < 384, s4[2], s4[3])))


def _band_is_current():
    j = lax.broadcasted_iota(jnp.int32, (BLOCK, GQA * BLOCK), 0)
    t = lax.broadcasted_iota(jnp.int32, (BLOCK, GQA * BLOCK), 1) & (BLOCK - 1)
    return j <= t


def _pack_band(full, current):
    return jnp.where(current, full[BLOCK:], full[:BLOCK])


def _unpack_band(packed, current):
    zero = jnp.zeros_like(packed)
    return jnp.concatenate([jnp.where(current, zero, packed), jnp.where(current, packed, zero)], axis=0)


def _probs_keys_major(k_rep, q_st, bias, sink, current):
    st = _pack_band(_nt(k_rep, q_st), current) * SCALE + bias
    m = jnp.maximum(jnp.max(st, axis=0, keepdims=True), sink)
    p = jnp.exp(st - m)
    esink = jnp.exp(sink - m)
    rl = 1.0 / (jnp.sum(p, axis=0, keepdims=True) + esink)
    return p * rl, esink * rl


WINDOW_HALO = 16


def _window_sum(ext, w, forward):
    s = ext
    sh = 1
    while sh < w:
        s = s + pltpu.roll(s, (ext.shape[0] - sh) if forward else sh, 0)
        sh *= 2
    return s


def _inv_count(n, w):
    t = n * BLOCK + lax.broadcasted_iota(jnp.int32, (BLOCK, 1), 0) + 1
    return 1.0 / jnp.minimum(t.astype(F32), float(w))


def _kv_ext(ref, n):
    r0 = pl.multiple_of(jnp.maximum(n - 1, 0) * BLOCK, BLOCK)
    r1 = pl.multiple_of(n * BLOCK, BLOCK)
    return jnp.concatenate([ref[pl.ds(r0, BLOCK), :], ref[pl.ds(r1, BLOCK), :]], axis=0)


def _rows_of(vec_ref, pack_ref, row0):
    for r in range(D_MODEL // 128):
        pack_ref[row0 + r:row0 + r + 1, :] = vec_ref[:, 128 * r:128 * (r + 1)]


FRONT_TILE = 4 * BLOCK


def _fwd_front(layer, x, norm_pre, w_in_t, token, sinks, pool_w, pool_scale, bias):
    tm = FRONT_TILE

    def body(sink_ref, x_ref, g_ref, w_ref, _, pw_ref, sc_ref, bias_ref,
             u_ref, pg_ref, q_ref, k_ref, v_ref, ag_ref, z_ref, a_ref, uprev, kprev, vprev):
        i = pl.program_id(0)

        @pl.when(i == 0)
        def _():
            uprev[...] = jnp.zeros_like(uprev)
            kprev[...] = jnp.zeros_like(kprev)
            vprev[...] = jnp.zeros_like(vprev)

        xv = x_ref[...]
        r = lax.rsqrt(jnp.mean(xv * xv, axis=-1, keepdims=True) + EPS)
        h = (xv * r * g_ref[layer:layer + 1, :]).astype(BF16)
        u_ref[...] = _nt(h, w_ref[COL_U:COL_PG, :])
        pg_ref[...] = _nt(h, w_ref[COL_PG:COL_Q, :])
        for sb in range(tm // BLOCK):
            n = (tm // BLOCK) * i + sb
            rows = slice(BLOCK * sb, BLOCK * (sb + 1))
            before = slice(BLOCK * (sb - 1), BLOCK * sb)
            uv = u_ref[rows, :]
            halo = (uprev[BLOCK - WINDOW_HALO:, :] if sb == 0
                    else u_ref[BLOCK * sb - WINDOW_HALO:BLOCK * sb, :])
            ext = jnp.concatenate([halo, uv], axis=0)
            for g, w in enumerate(POOL_WINDOWS):
                cs = slice(BLOCK * g, BLOCK * (g + 1))
                win = _window_sum(ext[:, cs], w, forward=False)[WINDOW_HALO:]
                pooled = win * _inv_count(n, w) - uv[:, cs]
                mixed = _nn(pooled.astype(BF16), pw_ref[g].astype(BF16))
                gate, _ = _silu_parts(pg_ref[rows, cs])
                z_ref[rows, cs] = (mixed * sc_ref[layer:layer + 1, cs] * gate).astype(BF16)

        q_ref[...] = _nt(h, w_ref[COL_Q:COL_K, :]).astype(BF16)
        k_ref[...] = _nt(h, w_ref[COL_K:COL_V, :])
        v_ref[...] = _nt(h, w_ref[COL_V:COL_AG, :])
        ag_ref[...] = _nt(h, w_ref[COL_AG:D_IN, :])

        current = _band_is_current()
        for sb in range(tm // BLOCK):
            n = (tm // BLOCK) * i + sb
            rows = slice(BLOCK * sb, BLOCK * (sb + 1))
            before = slice(BLOCK * (sb - 1), BLOCK * sb)
            kx = jnp.concatenate([kprev[...] if sb == 0 else k_ref[before, :], k_ref[rows, :]], axis=0)
            vx = jnp.concatenate([vprev[...] if sb == 0 else v_ref[before, :], v_ref[rows, :]], axis=0)
            variant = jnp.minimum(n, 1) if sb == 0 else 1
            for kv in range(2):
                cs = slice(256 * kv, 256 * (kv + 1))
                p, _ = _probs_keys_major(_replicate_head(kx, kv), _stack_heads(q_ref[rows, cs]),
                                         bias_ref[variant, kv], _sink_row(sink_ref, layer, kv), current)
                o = _unstack_heads(_tn(_unpack_band(p.astype(BF16), current), _replicate_head(vx, kv)))
                a_ref[rows, cs] = o
                gate, _ = _silu_parts(ag_ref[rows, cs])
                z_ref[rows, D_POOL + 256 * kv:D_POOL + 256 * (kv + 1)] = (o * gate).astype(BF16)

        tail = slice(tm - BLOCK, tm)
        uprev[...] = u_ref[tail, :]
        kprev[...] = k_ref[tail, :]
        vprev[...] = v_ref[tail, :]

    row = lambda c: pl.BlockSpec((tm, c), lambda i: (i, 0))
    const = lambda shape: pl.BlockSpec(shape, lambda i: (0,) * len(shape))
    return pl.pallas_call(
        body, name=f"fwd_front{layer}", grid=(SEQ // tm,),
        in_specs=[pl.BlockSpec(memory_space=pltpu.SMEM), row(D_MODEL), const((DEPTH, D_MODEL)),
                  _resident((D_IN, D_MODEL)), const((8, 128)),
                  pl.BlockSpec((None, 4, BLOCK, BLOCK), lambda i: (layer, 0, 0, 0)), const((DEPTH, D_POOL)),
                  _resident((2, 2, BLOCK, GQA * BLOCK))],
        out_specs=[row(D_POOL), row(D_POOL), row(D_ATTN), row(D_KV), row(D_KV), row(D_ATTN), row(D_MODEL),
                   row(D_ATTN)],
        out_shape=[jax.ShapeDtypeStruct((SEQ, D_POOL), F32), jax.ShapeDtypeStruct((SEQ, D_POOL), F32),
                   jax.ShapeDtypeStruct((SEQ, D_ATTN), BF16), jax.ShapeDtypeStruct((SEQ, D_KV), F32),
                   jax.ShapeDtypeStruct((SEQ, D_KV), F32), jax.ShapeDtypeStruct((SEQ, D_ATTN), F32),
                   jax.ShapeDtypeStruct((SEQ, D_MODEL), BF16), jax.ShapeDtypeStruct((SEQ, D_ATTN), F32)],
        scratch_shapes=[pltpu.VMEM((BLOCK, D_POOL), F32), pltpu.VMEM((BLOCK, D_KV), F32),
                        pltpu.VMEM((BLOCK, D_KV), F32)],
        compiler_params=_compiler_params(("arbitrary",)),
    )(sinks, x, norm_pre, w_in_t, token, pool_w, pool_scale, bias)


def _fwd_out(layer, z, x, norm_post, w_out, token):
    tm = FWD_OUT_TILE

    def body(z_ref, x_ref, g_ref, w_ref, _, xn_ref, y_ref):
        y = _nn(z_ref[...], w_ref[...])
        y_ref[...] = y
        r = lax.rsqrt(jnp.mean(y * y, axis=-1, keepdims=True) + EPS)
        xn_ref[...] = x_ref[...] + y * r * g_ref[layer:layer + 1, :]

    row = lambda c: pl.BlockSpec((tm, c), lambda i: (i, 0))
    return pl.pallas_call(
        body, name=f"fwd_out{layer}", grid=(SEQ // tm,),
        in_specs=[row(D_MODEL), row(D_MODEL), pl.BlockSpec((DEPTH, D_MODEL), lambda i: (0, 0)),
                  _resident((D_MODEL, D_MODEL)), pl.BlockSpec((8, 128), lambda i: (0, 0))],
        out_specs=[row(D_MODEL), row(D_MODEL)],
        out_shape=[jax.ShapeDtypeStruct((SEQ, D_MODEL), F32), jax.ShapeDtypeStruct((SEQ, D_MODEL), F32)],
        compiler_params=_compiler_params(("arbitrary",)),
    )(z, x, norm_post, w_out, token)


BACK_TILE = 2 * BLOCK


def _bwd_back(layer, top, dxo_or_xf, target_or_token, y, z, norm_post, w_out, sinks, u, pg, q, k, v, ag, a,
              pool_w, pool_scale, bias):
    tm = BACK_TILE
    steps = SEQ // tm
    last = steps - 1
    per = tm // BLOCK

    def body(*refs):
        refs = list(refs)
        sink_ref, first, second = refs[:3]
        (y_ref, z_ref, g_ref, w_ref, u_ref, up_ref, pg_ref, q_ref, k_ref, v_ref, ag_ref, a_ref, pw_ref, sc_ref,
         bias_ref) = refs[3:18]
        del refs[:18]
        dxo_ref = refs.pop(0) if top else None
        dp_ref, dw_ref, pack_ref, acc, dg, lacc, dzs, ck, cv, ce = refs
        i = pl.program_id(0)
        blk = last - i

        @pl.when(i == 0)
        def _():
            acc[...] = jnp.zeros_like(acc)
            dg[...] = jnp.zeros_like(dg)
            lacc[...] = jnp.zeros_like(lacc)
            pack_ref[...] = jnp.zeros_like(pack_ref)
            ck[...] = jnp.zeros_like(ck)
            cv[...] = jnp.zeros_like(cv)
            ce[...] = jnp.zeros_like(ce)

        if top:
            d = first[...] - second[...]
            dxo_v = d * (1.0 / D_MODEL)
            dxo_ref[...] = dxo_v
            part = jnp.sum(d * d, axis=-1, keepdims=True) * (1.0 / D_MODEL)
            lacc[...] += 0.5 * jnp.sum(part, axis=0, keepdims=True)
        else:
            dxo_v = first[...]
        yv = y_ref[...]
        r = lax.rsqrt(jnp.mean(yv * yv, axis=-1, keepdims=True) + EPS)
        yn = yv * r
        dg[...] += jnp.sum(dxo_v * yn, axis=0, keepdims=True)
        dyn = dxo_v * g_ref[layer:layer + 1, :]
        dy = (r * (dyn - yn * jnp.mean(dyn * yn, axis=-1, keepdims=True))).astype(BF16)
        dzs[...] = _nt(dy, w_ref[...])
        acc[...] += _tn(z_ref[...], dy)

        lane = lax.broadcasted_iota(jnp.int32, (1, 128), 1)
        lane2 = lax.broadcasted_iota(jnp.int32, (256, 128), 1)
        current = _band_is_current()
        for sb in reversed(range(per)):
            n = per * blk + sb
            rows = slice(BLOCK * sb, BLOCK * (sb + 1))

            uv = u_ref[rows, :]
            if sb == 0:
                halo = up_ref[BLOCK - WINDOW_HALO:, :] * (n > 0).astype(F32)
            else:
                halo = u_ref[BLOCK * sb - WINDOW_HALO:BLOCK * sb, :]
            ext = jnp.concatenate([halo, uv], axis=0)
            for g, w in enumerate(POOL_WINDOWS):
                cs = slice(BLOCK * g, BLOCK * (g + 1))
                inv = _inv_count(n, w)
                win = _window_sum(ext[:, cs], w, forward=False)[WINDOW_HALO:]
                pooled = win * inv - uv[:, cs]
                pw_g = pw_ref[g].astype(BF16)
                mixed = _nn(pooled.astype(BF16), pw_g)
                gate, dgate = _silu_parts(pg_ref[rows, cs])
                dzp = dzs[rows, cs]
                sc = sc_ref[layer:layer + 1, cs]
                dpm = dzp * gate
                dp_ref[rows, COL_PG + BLOCK * g:COL_PG + BLOCK * (g + 1)] = (dzp * (mixed * sc) * dgate).astype(BF16)
                pack_ref[ROW_SC + g:ROW_SC + g + 1, :] += jnp.sum(dpm * mixed, axis=0, keepdims=True)
                dmixed = (dpm * sc).astype(BF16)
                pack_ref[ROW_PW + BLOCK * g:ROW_PW + BLOCK * (g + 1), :] += _tn(pooled.astype(BF16), dmixed)
                dpooled = _nt(dmixed, pw_g)
                e = dpooled * inv
                lead = _window_sum(jnp.concatenate([e, ce[:WINDOW_HALO, cs]], axis=0), w, forward=True)[:BLOCK]
                dp_ref[rows, COL_U + BLOCK * g:COL_U + BLOCK * (g + 1)] = (lead - dpooled).astype(BF16)
                ce[:, cs] = e

            kx = _kv_ext(k_ref, n)
            vx = _kv_ext(v_ref, n)
            variant = jnp.minimum(n, 1) if sb == 0 else 1
            dsink_row = jnp.zeros((1, 128), F32)
            tks, tvs = [], []
            for kv in range(2):
                cs = slice(256 * kv, 256 * (kv + 1))
                k_rep = _replicate_head(kx, kv)
                v_rep = _replicate_head(vx, kv)
                q_st = _stack_heads(q_ref[rows, cs])
                gate, dgate = _silu_parts(ag_ref[rows, cs])
                dza = dzs[rows, D_POOL + 256 * kv:D_POOL + 256 * (kv + 1)]
                dp_ref[rows, COL_AG + 256 * kv:COL_AG + 256 * (kv + 1)] = (dza * a_ref[rows, cs] * dgate).astype(BF16)
                da_st = _stack_heads((dza * gate).astype(BF16))
                p, psink = _probs_keys_major(k_rep, q_st, bias_ref[variant, kv], _sink_row(sink_ref, layer, kv),
                                             current)
                dpt = _pack_band(_nt(v_rep, da_st), current)
                delta = jnp.sum(p * dpt, axis=0, keepdims=True)
                dst = _unpack_band((p * (dpt - delta) * SCALE).astype(BF16), current)
                sink_terms = psink * delta
                for g in range(GQA):
                    dsink = -jnp.sum(sink_terms[:, BLOCK * g:BLOCK * (g + 1)], axis=1, keepdims=True)
                    dsink_row = dsink_row + jnp.where(lane == kv * GQA + g, dsink, 0.0)
                dp_ref[rows, COL_Q + 256 * kv:COL_Q + 256 * (kv + 1)] = _unstack_heads(_tn(dst, k_rep)).astype(BF16)
                tks.append(_fold_heads(_nn(dst, q_st)))
                tvs.append(_fold_heads(_nn(_unpack_band(p.astype(BF16), current), da_st)))
            pack_ref[ROW_SINK:ROW_SINK + 1, :] += dsink_row
            dkx = jnp.where(lane2 < 64, tks[0], tks[1])
            dvx = jnp.where(lane2 < 64, tvs[0], tvs[1])
            dp_ref[rows, COL_K:COL_V] = (ck[...] + dkx[BLOCK:]).astype(BF16)
            dp_ref[rows, COL_V:COL_AG] = (cv[...] + dvx[BLOCK:]).astype(BF16)
            ck[...] = dkx[:BLOCK]
            cv[...] = dvx[:BLOCK]

        @pl.when(i == steps - 1)
        def _():
            dw_ref[...] = acc[...].astype(BF16)
            _rows_of(dg, pack_ref, ROW_NPOST)
            pack_ref[ROW_LOSS:ROW_LOSS + 1, :] = jnp.where(lane == 0, lacc[...], 0.0)

    row = lambda c: pl.BlockSpec((tm, c), lambda i: (last - i, 0))
    const = lambda shape: pl.BlockSpec(shape, lambda i: (0,) * len(shape))
    act = jax.ShapeDtypeStruct((SEQ, D_MODEL), F32)
    return pl.pallas_call(
        body, name=f"bwd_back{layer}", grid=(steps,),
        in_specs=[pl.BlockSpec(memory_space=pltpu.SMEM), row(D_MODEL), row(D_MODEL) if top else const((8, 128)),
                  row(D_MODEL), row(D_MODEL), const((DEPTH, D_MODEL)), _resident((D_MODEL, D_MODEL)),
                  row(D_POOL), pl.BlockSpec((BLOCK, D_POOL), lambda i: (jnp.maximum(per * (last - i) - 1, 0), 0)),
                  row(D_POOL), row(D_ATTN), _resident((SEQ, D_KV)), _resident((SEQ, D_KV)), row(D_ATTN), row(D_ATTN),
                  pl.BlockSpec((None, 4, BLOCK, BLOCK), lambda i: (layer, 0, 0, 0)), const((DEPTH, D_POOL)),
                  _resident((2, 2, BLOCK, GQA * BLOCK))],
        out_specs=([row(D_MODEL)] * (1 if top else 0)
                   + [row(D_IN), const((D_MODEL, D_MODEL)), const((PACK_ROWS, 128))]),
        out_shape=([act] * (1 if top else 0)
                   + [jax.ShapeDtypeStruct((SEQ, D_IN), BF16), jax.ShapeDtypeStruct((D_MODEL, D_MODEL), BF16),
                      jax.ShapeDtypeStruct((PACK_ROWS, 128), F32)]),
        scratch_shapes=[pltpu.VMEM((D_MODEL, D_MODEL), F32), pltpu.VMEM((1, D_MODEL), F32), pltpu.VMEM((1, 1), F32),
                        pltpu.VMEM((tm, D_MODEL), F32), pltpu.VMEM((BLOCK, D_KV), F32), pltpu.VMEM((BLOCK, D_KV), F32),
                        pltpu.VMEM((BLOCK, D_POOL), F32)],
        compiler_params=_compiler_params(("arbitrary",)),
    )(sinks, dxo_or_xf, target_or_token, y, z, norm_post, w_out, u, u, pg, q, k, v, ag, a, pool_w, pool_scale, bias)


def _bwd_in(layer, part, token, dproj, x, norm_pre, dxo=None, w_in_t=None):
    want_dw, want_dx = part in ("both", "dw"), part in ("both", "dx")
    tm = TOKEN_TILE
    steps = SEQ // tm
    cw = 256

    def body(*refs):
        refs = list(refs)
        dp_ref, x_ref, g_ref = refs[1:4]
        del refs[:4]
        if want_dx:
            dxo_ref, w_ref, dx_ref, dgo_ref = refs[:4]
            del refs[:4]
            dg = refs.pop()
        if want_dw:
            dw_ref, acc = refs
        i = pl.program_id(0)

        @pl.when(i == 0)
        def _():
            if want_dw:
                acc[...] = jnp.zeros_like(acc)
            if want_dx:
                dg[...] = jnp.zeros_like(dg)

        xv = x_ref[...]
        gv = g_ref[layer:layer + 1, :]
        r = lax.rsqrt(jnp.mean(xv * xv, axis=-1, keepdims=True) + EPS)
        xn = xv * r
        if want_dw:
            hb = (xn * gv).astype(BF16)
            for c in range(0, D_IN, cw):
                acc[c:c + cw, :] += _tn(dp_ref[:, c:c + cw], hb)
        if want_dx:
            dh = _nn(dp_ref[...], w_ref[...])
            dg[...] += jnp.sum(dh * xn, axis=0, keepdims=True)
            dhn = dh * gv
            dx_ref[...] = dxo_ref[...] + r * (dhn - xn * jnp.mean(dhn * xn, axis=-1, keepdims=True))

        @pl.when(i == steps - 1)
        def _():
            if want_dw:
                dw_ref[...] = acc[...].astype(BF16)
            if want_dx:
                _rows_of(dg, dgo_ref, 0)

    row = lambda c: pl.BlockSpec((tm, c), lambda i: (i, 0))
    const = lambda shape: pl.BlockSpec(shape, lambda i: (0,) * len(shape))
    in_specs = [const((8, 128)), row(D_IN), row(D_MODEL), const((DEPTH, D_MODEL))]
    operands = [token, dproj, x, norm_pre]
    out_specs, out_shape, scratch = [], [], []
    if want_dx:
        in_specs += [row(D_MODEL), _resident((D_IN, D_MODEL))]
        operands += [dxo, w_in_t]
        out_specs += [row(D_MODEL), const((8, 128))]
        out_shape += [jax.ShapeDtypeStruct((SEQ, D_MODEL), F32), jax.ShapeDtypeStruct((8, 128), F32)]
    if want_dw:
        out_specs.append(const((D_IN, D_MODEL)))
        out_shape.append(jax.ShapeDtypeStruct((D_IN, D_MODEL), BF16))
        scratch.append(pltpu.VMEM((D_IN, D_MODEL), F32))
    if want_dx:
        scratch.append(pltpu.VMEM((1, D_MODEL), F32))
    return pl.pallas_call(
        body, name=f"bwd_in_{part}{layer}", grid=(steps,),
        in_specs=in_specs, out_specs=out_specs, out_shape=out_shape, scratch_shapes=scratch,
        compiler_params=_compiler_params(("arbitrary",)),
    )(*operands)


def _mesh_pos():
    return lax.axis_index("x"), lax.axis_index("y"), lax.axis_index("c")


def _device_rows(ref, m, px, py, pc):
    return ref.at[pl.ds(pl.multiple_of((4 * px + 2 * py + pc) * m, 16 if m % 16 == 0 else 8), m), :]


def _allgather(srcs, out_dtype, name, later=()):
    na, nb = len(srcs), len(later)
    every = list(srcs) + list(later)
    shapes = [(a.shape[-2], a.shape[-1]) for a, _ in every]

    def body(*refs):
        xs, refs = refs[:na + nb], refs[na + nb:]
        outs, cast, land, refs = refs[:na], refs[na:na + nb], refs[na + nb:na + 2 * nb], refs[na + 2 * nb:]
        stage, raw, (send_sems, recv_sems, local_sems, load_sems) = refs[:na], refs[na:2 * na + nb], refs[2 * na + nb:]
        loads = [pltpu.make_async_copy(xs[i].at[every[i][1]], raw[i], load_sems.at[i]) for i in range(na + nb)]
        for cp in loads:
            cp.start()
        x, y, c = _mesh_pos()
        me, sibling = (x, y, c), (x, y, 1 - c)
        near = [(1 - x, y), (x, 1 - y)]
        far = (1 - x, 1 - y)
        relay_from, relay_to = (x ^ (1 - c), y ^ c), (x ^ c, y ^ (1 - c))
        _handshake([sibling] + [(*chip, c) for chip in near])
        k_from, k_to = 1 + c, 2 - c

        def slot(a, px, py, pc):
            return _device_rows(outs[a], shapes[a][0], px, py, pc)

        def copy(a, k, block, to, src=None):
            return pltpu.make_async_remote_copy(
                src_ref=slot(a, *block) if src is None else src, dst_ref=slot(a, *block),
                send_sem=send_sems.at[a, k], recv_sem=recv_sems.at[a, k], device_id=to, device_id_type=MESH)

        def cast_block(i):
            loads[i].wait()
            return raw[i][...].astype(out_dtype)

        for a in range(na):
            stage[a][...] = cast_block(a)
        mine = [pltpu.make_async_copy(stage[a], slot(a, *me), local_sems.at[a]) for a in range(na)]
        for cp in mine:
            cp.start()
        sent = []
        for a in range(na):
            sent.append(copy(a, 0, me, sibling, src=stage[a]))
            sent += [copy(a, 1 + j, me, (*chip, c), src=stage[a]) for j, chip in enumerate(near)]
        for cp in sent:
            cp.start()
        for b in range(nb):
            cast[b][...] = cast_block(na + b)
            cp = pltpu.make_async_copy(cast[b], _device_rows(land[b], shapes[na + b][0], *me), local_sems.at[na + b])
            cp.start()
            mine.append(cp)
        for a in range(na):
            copy(a, k_from, (*relay_from, c), me).wait_recv()
            sent += [copy(a, 3, (*relay_from, c), (*relay_to, c)), copy(a, 3 + k_from, (*relay_from, c), sibling)]
            sent[-2].start()
            sent[-1].start()
        for a in range(na):
            copy(a, k_to, (*relay_to, c), me).wait_recv()
            sent.append(copy(a, 3 + k_to, (*relay_to, c), sibling))
            sent[-1].start()
        for a in range(na):
            copy(a, 3, (*far, c), me).wait_recv()
            sent.append(copy(a, 6, (*far, c), sibling))
            sent[-1].start()
        for a in range(na):
            copy(a, 0, sibling, me).wait_recv()
            for j, chip in enumerate(near + [far]):
                copy(a, 4 + j, (*chip, 1 - c), me).wait_recv()
        for cp in sent:
            cp.wait_send()
        for cp in mine:
            cp.wait()

    vmem = pl.BlockSpec(memory_space=pltpu.VMEM)
    hbm = pl.BlockSpec(memory_space=pl.ANY)
    gathered = [jax.ShapeDtypeStruct((N_DEV * m, n), out_dtype) for m, n in shapes]
    out = pl.pallas_call(
        body, name=name,
        in_specs=[hbm] * (na + nb),
        out_specs=[hbm] * na + [vmem] * nb + [hbm] * nb,
        out_shape=gathered[:na] + [jax.ShapeDtypeStruct(s, out_dtype) for s in shapes[na:]] + gathered[na:],
        scratch_shapes=([pltpu.VMEM(s, out_dtype) for s in shapes[:na]]
                        + [pltpu.VMEM(s, a.dtype) for s, (a, _) in zip(shapes, every)]
                        + [pltpu.SemaphoreType.DMA((na, 7)), pltpu.SemaphoreType.DMA((na, 7)),
                           pltpu.SemaphoreType.DMA((na + nb,)), pltpu.SemaphoreType.DMA((na + nb,))]),
        compiler_params=pltpu.CompilerParams(vmem_limit_bytes=VMEM_LIMIT, collective_id=COLLECTIVE_GATHER_W0),
    )(*[a for a, _ in every])
    return out[:na], out[na:na + nb], out[na + nb:]


ALL_PEERS = tuple(range(1, N_DEV))
SIBLING_AND_SAME_CORE = (1, 2, 4, 6)


def _related(k, x, y, c):
    return x ^ ((k >> 2) & 1), y ^ ((k >> 1) & 1), c ^ (k & 1)


def _gather_start(blocks, lands, relations, collective_id, name):
    na = len(blocks)

    def body(*refs):
        src, land, sems, token = refs[:na], refs[na:2 * na], refs[2 * na:4 * na], refs[-1]
        x, y, c = _mesh_pos()
        _handshake([_related(k, x, y, c) for k in sorted(set().union(*relations))])
        for a in range(na):
            for k in relations[a]:
                pltpu.make_async_remote_copy(
                    src_ref=src[a], dst_ref=_device_rows(land[a], blocks[a].shape[0], x, y, c),
                    send_sem=sems[2 * a].at[k - 1], recv_sem=sems[2 * a + 1].at[k - 1],
                    device_id=_related(k, x, y, c), device_id_type=MESH).start()
        token[...] = jnp.zeros_like(token)

    bufs = [pltpu.HBM(t.shape, t.dtype) for t in list(blocks) + list(lands)]
    out = pl.pallas_call(
        body, name=name,
        out_shape=(*([pltpu.SemaphoreType.DMA((N_DEV - 1,))] * (2 * na)), *bufs, jax.ShapeDtypeStruct((8, 128), F32)),
        in_specs=[_HBM] * (2 * na),
        out_specs=(*([_SEM] * (2 * na)), *([_HBM] * (2 * na)), pl.BlockSpec(memory_space=pltpu.VMEM)),
        input_output_aliases={i: 2 * na + i for i in range(2 * na)},
        compiler_params=pltpu.CompilerParams(has_side_effects=_EFFECT, collective_id=collective_id),
    )(*[pltpu.with_memory_space_constraint(t, pltpu.HBM) for t in list(blocks) + list(lands)])
    sems = [(out[2 * a], out[2 * a + 1]) for a in range(na)]
    return sems, out[2 * na:3 * na], out[3 * na:4 * na], out[-1]


def _gather_wait(sems, block, land, relations, after, name):
    def body(src, land_ref, send_sem, recv_sem, after_ref, src_out, land_out):
        x, y, c = _mesh_pos()
        for k in relations:
            peer = _related(k, x, y, c)
            cp = pltpu.make_async_remote_copy(
                src_ref=src, dst_ref=_device_rows(land_ref, block.shape[0], *peer),
                send_sem=send_sem.at[k - 1], recv_sem=recv_sem.at[k - 1], device_id=peer, device_id_type=MESH)
            cp.wait_send()
            cp.wait_recv()

    out = pl.pallas_call(
        body, name=name,
        out_shape=(pltpu.HBM(block.shape, block.dtype), pltpu.HBM(land.shape, land.dtype)),
        in_specs=[_HBM, _HBM, _SEM, _SEM, pl.BlockSpec(memory_space=pl.ANY)],
        out_specs=[_HBM, _HBM],
        input_output_aliases={0: 0, 1: 1},
        compiler_params=pltpu.CompilerParams(has_side_effects=_EFFECT),
    )(block, land, sems[0], sems[1], after)
    return out[1]


(COLLECTIVE_GATHER_W0, COLLECTIVE_GATHER_W1, COLLECTIVE_FORWARD_W_IN1, COLLECTIVE_EXCHANGE_1, COLLECTIVE_EXCHANGE_0A,
 COLLECTIVE_PAIR_REDUCE, COLLECTIVE_EXCHANGE_0B, COLLECTIVE_GATHER_SMALL) = range(1, 9)


def _handshake(peers):
    barrier = pltpu.get_barrier_semaphore()
    for peer in peers:
        pl.semaphore_signal(barrier, inc=1, device_id=peer, device_id_type=MESH)
    pl.semaphore_wait(barrier, len(peers))


def _forward_plan(land_ref, m):
    x, y, c = _mesh_pos()
    return [_device_rows(land_ref, m, qx, qy, c) for qx, qy in ((1 - x, y), (x, 1 - y), (1 - x, 1 - y))], (x, y, 1 - c)


def _forward_start(land, m, name):
    def body(land_ref, send_sem, recv_sem, land_out, token):
        _handshake([(lax.axis_index("x"), lax.axis_index("y"), 1 - lax.axis_index("c"))])
        rows, sibling = _forward_plan(land_ref, m)
        for j, r in enumerate(rows):
            pltpu.make_async_remote_copy(src_ref=r, dst_ref=r, send_sem=send_sem.at[j], recv_sem=recv_sem.at[j],
                                         device_id=sibling, device_id_type=MESH).start()
        token[...] = jnp.zeros_like(token)

    out = pl.pallas_call(
        body, name=name,
        out_shape=(pltpu.SemaphoreType.DMA((3,)), pltpu.SemaphoreType.DMA((3,)), pltpu.HBM(land.shape, land.dtype),
                   jax.ShapeDtypeStruct((8, 128), F32)),
        in_specs=[_HBM],
        out_specs=(_SEM, _SEM, _HBM, pl.BlockSpec(memory_space=pltpu.VMEM)),
        input_output_aliases={0: 2},
        compiler_params=pltpu.CompilerParams(has_side_effects=_EFFECT, collective_id=COLLECTIVE_FORWARD_W_IN1),
    )(pltpu.with_memory_space_constraint(land, pltpu.HBM))
    return (out[0], out[1]), out[2], out[3]


def _forward_wait(sems, land, m, after, name):
    def body(land_ref, send_sem, recv_sem, after_ref, land_out):
        x, y, c = _mesh_pos()
        mine, sibling = _forward_plan(land_ref, m)
        theirs = [_device_rows(land_ref, m, qx, qy, 1 - c) for qx, qy in ((1 - x, y), (x, 1 - y), (1 - x, 1 - y))]
        for j in range(3):
            cp = pltpu.make_async_remote_copy(src_ref=mine[j], dst_ref=theirs[j], send_sem=send_sem.at[j],
                                              recv_sem=recv_sem.at[j], device_id=sibling, device_id_type=MESH)
            cp.wait_send()
            cp.wait_recv()

    return pl.pallas_call(
        body, name=name,
        out_shape=pltpu.HBM(land.shape, land.dtype),
        in_specs=[_HBM, _SEM, _SEM, pl.BlockSpec(memory_space=pl.ANY)],
        out_specs=_HBM,
        input_output_aliases={0: 0},
        compiler_params=pltpu.CompilerParams(has_side_effects=_EFFECT),
    )(land, sems[0], sems[1], after)


def _row_step(m):
    return next(s for s in (32, 24, 16, 8) if m % s == 0)


def _pair_reduce(arrs, name):
    na = len(arrs)

    def body(*refs):
        gs, hs, hm = refs[:na], refs[na:2 * na], refs[2 * na:3 * na]
        own, ra = refs[3 * na:4 * na], refs[4 * na:5 * na]
        d2d_send, d2d_recv, local_sems = refs[5 * na:]
        x, y, c = _mesh_pos()
        sibling = (x, y, 1 - c)
        _handshake([sibling])
        loads, sends = [], []
        for a in range(na):
            for q in range(4):
                cp = pltpu.make_async_copy(gs[a].at[2 * q + c], own[a].at[q], local_sems.at[a, q])
                cp.start()
                loads.append(cp)
                cp = pltpu.make_async_remote_copy(
                    src_ref=gs[a].at[2 * q + (1 - c)], dst_ref=ra[a].at[q], send_sem=d2d_send.at[a, q],
                    recv_sem=d2d_recv.at[a, q], device_id=sibling, device_id_type=MESH)
                cp.start()
                sends.append(cp)
        for cp in loads:
            cp.wait()
        for cp in sends:
            cp.wait_recv()
        others = [2 * (1 - x) + y, 2 * x + (1 - y), 2 * (1 - x) + (1 - y)]
        for a in range(na):
            m = arrs[a].shape[1]
            step = _row_step(m)

            def add(i, carry, a=a, step=step):
                rs = pl.ds(pl.multiple_of(i * step, step), step)
                for j, q in enumerate(others):
                    hs[a][j, rs, :] = (own[a][q, rs, :].astype(F32) + ra[a][q, rs, :].astype(F32)).astype(hs[a].dtype)
                q = 2 * x + y
                hm[a][rs, :] = own[a][q, rs, :].astype(F32) + ra[a][q, rs, :].astype(F32)
                return carry

            lax.fori_loop(0, m // step, add, 0)
        for cp in sends:
            cp.wait_send()

    vmem = pl.BlockSpec(memory_space=pltpu.VMEM)
    scratch = [pltpu.VMEM((4,) + t.shape[1:], t.dtype) for t in arrs] * 2
    scratch += [pltpu.SemaphoreType.DMA((na, 4)), pltpu.SemaphoreType.DMA((na, 4)), pltpu.SemaphoreType.DMA((na, 4))]
    out = pl.pallas_call(
        body, name=name,
        in_specs=[pl.BlockSpec(memory_space=pl.ANY)] * na, out_specs=[vmem] * (2 * na),
        out_shape=([jax.ShapeDtypeStruct((3,) + t.shape[1:], t.dtype) for t in arrs]
                   + [jax.ShapeDtypeStruct(t.shape[1:], F32) for t in arrs]),
        scratch_shapes=scratch,
        compiler_params=pltpu.CompilerParams(vmem_limit_bytes=VMEM_LIMIT, collective_id=COLLECTIVE_PAIR_REDUCE),
    )(*arrs)
    return out[:na], out[na:]


_HBM = pl.BlockSpec(memory_space=pltpu.HBM)
_SEM = pl.BlockSpec(memory_space=pltpu.SEMAPHORE)
_EFFECT = pltpu.SideEffectType.DATAFLOW_SIDE_EFFECTING


def _exchange_plan(direct):
    x, y, c = _mesh_pos()
    if not direct:
        return [(j, j, (qx, qy, c)) for j, (qx, qy) in enumerate([(1 - x, y), (x, 1 - y), (1 - x, 1 - y)])]
    plan = []
    for k in range(1, N_DEV):
        px, py, pc = x ^ ((k >> 2) & 1), y ^ ((k >> 1) & 1), c ^ (k & 1)
        plan.append((4 * px + 2 * py + pc, k - 1, (px, py, pc)))
    return plan


def _exchange_start(srcs, direct, collective_id, name):
    na = len(srcs)
    slots = N_DEV - 1 if direct else 3

    def body(*refs):
        src, land = refs[:na], refs[na:2 * na]
        send_sem, recv_sem = refs[2 * na], refs[2 * na + 1]
        token = refs[-1]
        plan = _exchange_plan(direct)
        _handshake([peer for _, _, peer in plan])
        for block, slot, peer in plan:
            for a in range(na):
                pltpu.make_async_remote_copy(
                    src_ref=src[a].at[block], dst_ref=land[a].at[slot], send_sem=send_sem.at[slots * a + slot],
                    recv_sem=recv_sem.at[slots * a + slot], device_id=peer, device_id_type=MESH).start()
        token[...] = jnp.zeros_like(token)

    zones = [jax.ShapeDtypeStruct((slots,) + t.shape[1:], t.dtype) for t in srcs]
    bufs = [pltpu.HBM(t.shape, t.dtype) for t in list(srcs) + zones]
    out = pl.pallas_call(
        body, name=name,
        out_shape=(pltpu.SemaphoreType.DMA((slots * na,)), pltpu.SemaphoreType.DMA((slots * na,)), *bufs,
                   jax.ShapeDtypeStruct((8, 128), F32)),
        in_specs=[_HBM] * (2 * na),
        out_specs=(_SEM, _SEM, *([_HBM] * (2 * na)), pl.BlockSpec(memory_space=pltpu.VMEM)),
        input_output_aliases={i: 2 + i for i in range(2 * na)},
        compiler_params=pltpu.CompilerParams(has_side_effects=_EFFECT, collective_id=collective_id),
    )(*[pltpu.with_memory_space_constraint(t, pltpu.HBM) for t in srcs],
      *[pltpu.with_memory_space_constraint(lax.empty(t.shape, t.dtype), pltpu.HBM) for t in zones])
    return out[0], out[1], out[2:2 + na], out[2 + na:2 + 2 * na], out[-1]


def _exchange_wait(send_sem, recv_sem, srcs, lands, direct, after, name):
    na = len(srcs)
    slots = N_DEV - 1 if direct else 3

    def body(*refs):
        src, land = refs[:na], refs[na:2 * na]
        send_sem_ref, recv_sem_ref = refs[2 * na], refs[2 * na + 1]
        for block, slot, peer in _exchange_plan(direct):
            for a in range(na):
                cp = pltpu.make_async_remote_copy(
                    src_ref=src[a].at[block], dst_ref=land[a].at[slot], send_sem=send_sem_ref.at[slots * a + slot],
                    recv_sem=recv_sem_ref.at[slots * a + slot], device_id=peer, device_id_type=MESH)
                cp.wait_send()
                cp.wait_recv()

    bufs = [pltpu.HBM(t.shape, t.dtype) for t in list(srcs) + list(lands)]
    out = pl.pallas_call(
        body, name=name,
        out_shape=tuple(bufs),
        in_specs=[_HBM] * (2 * na) + [_SEM, _SEM, pl.BlockSpec(memory_space=pl.ANY)],
        out_specs=[_HBM] * (2 * na),
        input_output_aliases={i: i for i in range(2 * na)},
        compiler_params=pltpu.CompilerParams(has_side_effects=_EFFECT),
    )(*srcs, *lands, send_sem, recv_sem, after)
    return out[:na], out[na:]


def _own_then_slots(mine_ref, lands_ref, rows=slice(None)):
    if len(mine_ref.shape) == 3:
        x, y, c = _mesh_pos()
        total = mine_ref[4 * x + 2 * y + c, rows, :].astype(F32)
    else:
        total = mine_ref[rows, :].astype(F32)
    for j in range(lands_ref.shape[0]):
        total = total + lands_ref[j, rows, :].astype(F32)
    return total


SMALL_ROWS = 2 * PACK_SLICE + 2 * 8


def _small_block(mine, lands, dgpre, name):
    def body(*refs):
        hm, ld, dg = refs[:DEPTH], refs[DEPTH:2 * DEPTH], refs[2 * DEPTH:3 * DEPTH]
        blk, land, sem = refs[3 * DEPTH:]
        for l in range(DEPTH):
            blk[PACK_SLICE * l:PACK_SLICE * (l + 1), :] = _own_then_slots(hm[l], ld[l])
            blk[2 * PACK_SLICE + 8 * l:2 * PACK_SLICE + 8 * (l + 1), :] = dg[l][...]
        cp = pltpu.make_async_copy(blk, _device_rows(land, SMALL_ROWS, *_mesh_pos()), sem)
        cp.start()
        cp.wait()

    vmem = pl.BlockSpec(memory_space=pltpu.VMEM)
    return pl.pallas_call(
        body, name=name,
        in_specs=[vmem] * (3 * DEPTH), out_specs=[vmem, pl.BlockSpec(memory_space=pl.ANY)],
        out_shape=[jax.ShapeDtypeStruct((SMALL_ROWS, 128), F32), jax.ShapeDtypeStruct((N_DEV * SMALL_ROWS, 128), F32)],
        scratch_shapes=[pltpu.SemaphoreType.DMA],
        compiler_params=_compiler_params(),
    )(*mine, *lands, *dgpre)


def _adamw_math(w, g, m, v):
    m = ADAM_B1 * m + (1.0 - ADAM_B1) * g
    v = ADAM_B2 * v + (1.0 - ADAM_B2) * (g * g)
    m_hat = m / (1.0 - ADAM_B1 ** ADAM_STEP)
    v_hat = v / (1.0 - ADAM_B2 ** ADAM_STEP)
    delta = -ADAM_LR * (m_hat / (jnp.sqrt(v_hat) + ADAM_EPS) + ADAM_WD * w)
    return delta, m, v


def _adamw_layer(layer, mine, lands, w, m, v, earlier, token, name, rows):
    _, mm, nn = w.shape

    def body(hm_ref, ld_ref, w_ref, m_ref, v_ref, _, *refs):
        g_ref, d_ref, nm_ref, nv_ref = refs[-4:]
        g = _own_then_slots(hm_ref, ld_ref)
        g_ref[...] = g
        d, nm, nv = _adamw_math(w_ref[...], g, m_ref[...], v_ref[...])
        d_ref[...] = d
        nm_ref[...] = nm
        nv_ref[...] = nv

    spec = pl.BlockSpec((None, rows, nn), lambda i: (layer, i, 0))
    carried = [] if earlier is None else list(earlier)
    return pl.pallas_call(
        body, name=name, grid=(mm // rows,),
        in_specs=([pl.BlockSpec((rows, nn), lambda i: (i, 0)) if mine.ndim == 2
                   else pl.BlockSpec((N_DEV, rows, nn), lambda i: (0, i, 0)),
                   pl.BlockSpec((lands.shape[0], rows, nn), lambda i: (0, i, 0)),
                   spec, spec, spec] + [pl.BlockSpec(memory_space=pl.ANY)] * (1 + len(carried))),
        out_specs=[spec] * 4,
        out_shape=[jax.ShapeDtypeStruct(w.shape, F32)] * 4,
        input_output_aliases={6 + t: t for t in range(len(carried))},
        compiler_params=_compiler_params(("arbitrary",)),
    )(mine, lands, w, m, v, token, *carried)


def _adamw_small(gathered, params):
    def body(all_ref, *refs):
        ins, outs, packs = refs[:15], refs[15:15 + 21], refs[15 + 21]
        loss_ref = outs[0]
        for dev in range(N_DEV):
            for l in range(DEPTH):
                packs[l, PACK_SLICE * dev:PACK_SLICE * (dev + 1), :] = (
                    all_ref[SMALL_ROWS * dev + PACK_SLICE * l:SMALL_ROWS * dev + PACK_SLICE * (l + 1), :])
        loss_ref[...] = packs[DEPTH - 1, ROW_LOSS:ROW_LOSS + 1, 0:1]

        def update(p, sel, g):
            w_ref, m_ref, v_ref = ins[p], ins[5 + p], ins[10 + p]
            d, nm, nv = _adamw_math(w_ref[sel], g, m_ref[sel], v_ref[sel])
            for t, val in enumerate((g, d, nm, nv)):
                outs[1 + 5 * t + p][sel] = val

        for l in range(DEPTH):
            gp = packs.at[l]
            row0 = 2 * PACK_SLICE + 8 * l
            dgpre = all_ref[row0:row0 + 8, :]
            for dev in range(1, N_DEV):
                dgpre = dgpre + all_ref[SMALL_ROWS * dev + row0:SMALL_ROWS * dev + row0 + 8, :]
            for grp in range(4):
                update(0, (l, grp), gp[ROW_PW + BLOCK * grp:ROW_PW + BLOCK * (grp + 1), :])
                update(1, (slice(l, l + 1), slice(128 * grp, 128 * (grp + 1))), gp[ROW_SC + grp:ROW_SC + grp + 1, :])
            update(2, (slice(l, l + 1), slice(None)), gp[ROW_SINK:ROW_SINK + 1, 0:N_HEADS])
            for r in range(D_MODEL // 128):
                sel = (slice(l, l + 1), slice(128 * r, 128 * (r + 1)))
                update(3, sel, dgpre[r:r + 1, :])
                update(4, sel, gp[ROW_NPOST + r:ROW_NPOST + r + 1, :])

    shapes = [jax.ShapeDtypeStruct(p.shape, F32) for p in params[:5]]
    return pl.pallas_call(
        body, name="adamw_small",
        out_shape=[jax.ShapeDtypeStruct((1, 1), F32)] + shapes * 4,
        scratch_shapes=[pltpu.VMEM((DEPTH, PACK_ROWS, 128), F32)],
        compiler_params=_compiler_params(),
    )(gathered, *params)


def kernel(x, w_in, pool_w, pool_scale, attn_sinks, w_out, norm_pre, norm_post, loss_target, m_w_in, m_pool_w, m_pool_scale, m_attn_sinks, m_w_out, m_norm_pre, m_norm_post, v_w_in, v_pool_w, v_pool_scale, v_attn_sinks, v_w_out, v_norm_pre, v_norm_post):
    x0 = x.reshape(SEQ, D_MODEL)
    target = loss_target.reshape(SEQ, D_MODEL)
    bias = jnp.asarray(_attn_bias())
    w_in_t, m_in_t, v_in_t = (jnp.swapaxes(t, 1, 2) for t in (w_in, m_w_in, v_w_in))

    (win0, wout0), later, lands = _allgather([(w_in_t, 0), (w_out, 0)], BF16, "gather_w0",
                                              later=[(w_in_t, 1), (w_out, 1)])
    sems, later, lands, token = _gather_start(later, lands, [SIBLING_AND_SAME_CORE, ALL_PEERS], COLLECTIVE_GATHER_W1,
                                              "gather_w1_start")
    win_full, wout_full = [win0, None], [wout0, None]

    saved = []
    xl = x0
    for layer in range(DEPTH):
        u, pg, q, k, v, ag, z, a = _fwd_front(layer, xl, norm_pre, win_full[layer], token, attn_sinks,
                                             pool_w, pool_scale, bias)
        if layer == 0:
            land = _gather_wait(sems[0], later[0], lands[0], SIBLING_AND_SAME_CORE, z, "gather_w_in1_wait")
            fsems, land, token = _forward_start(land, IN_SHARD, "forward_w_in1_start")
        else:
            wout_full[layer] = _gather_wait(sems[1], later[1], lands[1], ALL_PEERS, z, "gather_w_out1_wait")
        x_next, y = _fwd_out(layer, z, xl, norm_post, wout_full[layer], token)
        if layer == 0:
            win_full[1] = _forward_wait(fsems, land, IN_SHARD, x_next, "forward_w_in1_wait")
        saved.append((xl, u, pg, q, k, v, ag, z, a, y))
        xl = x_next

    params_small = [pool_w, pool_scale, attn_sinks, norm_pre, norm_post,
                    m_pool_w, m_pool_scale, m_attn_sinks, m_norm_pre, m_norm_post,
                    v_pool_w, v_pool_scale, v_attn_sinks, v_norm_pre, v_norm_post]

    def start(blocks, direct, collective_id, tag):
        srcs, mine = (blocks, None) if direct else _pair_reduce(blocks, f"pair_reduce{tag}")
        send_sem, recv_sem, srcs, lands, started = _exchange_start(srcs, direct, collective_id, f"exchange_start{tag}")
        return (send_sem, recv_sem, srcs, lands, mine, direct), started

    def finish(handle, after, tag):
        send_sem, recv_sem, srcs, lands, mine, direct = handle
        srcs, lands = _exchange_wait(send_sem, recv_sem, srcs, lands, direct, after, f"exchange_wait{tag}")
        return (srcs if mine is None else mine), lands

    def back(layer, top, first, second):
        xin, u, pg, q, k, v, ag, z, a, y = saved[layer]
        return _bwd_back(layer, top, first, second, y, z, norm_post, wout_full[layer], attn_sinks, u, pg, q, k, v,
                         ag, a, pool_w, pool_scale, bias)

    dgpre = [None] * DEPTH
    dx, dproj, gw_out, pack = back(1, True, xl, target)
    dx, dgpre[1], gw_in_t = _bwd_in(1, "both", token, dproj, saved[1][0], norm_pre, dx, win_full[1])
    top, token = start([gw_in_t.reshape(N_DEV, IN_SHARD, D_MODEL), gw_out.reshape(N_DEV, OUT_SHARD, D_MODEL),
                        pack.reshape(N_DEV, PACK_SLICE, 128)], True, COLLECTIVE_EXCHANGE_1, "1")

    dproj, gw_out, pack = back(0, False, dx, token)
    early, token = start([gw_out.reshape(N_DEV, OUT_SHARD, D_MODEL), pack.reshape(N_DEV, PACK_SLICE, 128)], True,
                         COLLECTIVE_EXCHANGE_0A, "0a")
    (gw_in_t,) = _bwd_in(0, "dw", token, dproj, saved[0][0], norm_pre)
    late, token = start([gw_in_t.reshape(N_DEV, IN_SHARD, D_MODEL)], False, COLLECTIVE_EXCHANGE_0B, "0b")
    dx, dgpre[0] = _bwd_in(0, "dx", token, dproj, saved[0][0], norm_pre, dx, win_full[0])

    own1, lands1 = finish(top, dx, "1")
    big_in = _adamw_layer(1, own1[0], lands1[0], w_in_t, m_in_t, v_in_t, None, token, "adamw_in1", ADAM_ROWS_IN)
    big_out = _adamw_layer(1, own1[1], lands1[1], w_out, m_w_out, v_w_out, None, token, "adamw_out1", ADAM_ROWS_OUT)
    own0a, lands0a = finish(early, big_out[0], "0a")
    block, land = _small_block([own0a[1], own1[2]], [lands0a[1], lands1[2]], dgpre, "small_block")
    sems, block, land, token = _gather_start([block], [land], [ALL_PEERS], COLLECTIVE_GATHER_SMALL, "gather_small_start")
    big_out = _adamw_layer(0, own0a[0], lands0a[0], w_out, m_w_out, v_w_out, big_out, token, "adamw_out0", ADAM_ROWS_OUT)
    own0b, lands0b = finish(late, big_out[0], "0b")
    big_in = _adamw_layer(0, own0b[0], lands0b[0], w_in_t, m_in_t, v_in_t, big_in, token, "adamw_in0", ADAM_ROWS_IN)
    gathered = _gather_wait(sems[0], block[0], land[0], ALL_PEERS, big_in[0], "gather_small_wait")
    small_out = _adamw_small(gathered, params_small)
    loss = small_out[0].reshape(())

    outs = [loss, dx.reshape(1, SEQ, D_MODEL)]
    for t in range(4):
        pw_, sc_, sk_, npre_, npost_ = small_out[1 + 5 * t:6 + 5 * t]
        outs += [jnp.swapaxes(big_in[t], 1, 2), pw_, sc_, sk_, big_out[t], npre_, npost_]
    return tuple(outs)
```

```python
import numpy as np
import jax
import jax.numpy as jnp
from jax import lax
from jax.experimental import pallas as pl
from jax.experimental.pallas import tpu as pltpu

F32 = jnp.float32
BF16 = jnp.bfloat16

N_DEV = 8
SEQ = 2048
D_MODEL = 1024
D_POOL = 512
D_ATTN = 512
D_KV = 128
D_IN = 2304
N_HEADS = 8
GQA = 4
HEAD_DIM = 64
BLOCK = 128
POOL_WINDOWS = (2, 4, 8, 16)
DEPTH = 2
EPS = 1e-6
NEG_INF = -1e30
SCALE = HEAD_DIM ** -0.5
IN_SHARD = D_IN // N_DEV
OUT_SHARD = D_MODEL // N_DEV

COL_U, COL_PG, COL_Q, COL_K, COL_V, COL_AG = 0, 512, 1024, 1536, 1664, 1792

ADAM_LR = 0.001
ADAM_B1 = 0.9
ADAM_B2 = 0.999
ADAM_EPS = 1e-08
ADAM_WD = 0.01
ADAM_STEP = 10

TOKEN_TILE = 512
FWD_OUT_TILE = 1024
ADAM_ROWS_IN, ADAM_ROWS_OUT = 144, 128
VMEM_LIMIT = 56 * 1024 * 1024
MESH = pl.DeviceIdType.MESH

ROW_PW, ROW_SC, ROW_SINK, ROW_NPOST, ROW_LOSS = 0, 512, 520, 536, 544
PACK_ROWS = 576
PACK_SLICE = PACK_ROWS // N_DEV


def _nn(a, b):
    return jnp.dot(a, b, preferred_element_type=F32)


def _nt(a, b):
    return lax.dot_general(a, b, (((1,), (1,)), ((), ())), preferred_element_type=F32)


def _tn(a, b):
    return lax.dot_general(a, b, (((0,), (0,)), ((), ())), preferred_element_type=F32)


def _silu_parts(g):
    s = jax.nn.sigmoid(g)
    return g * s, s * (1.0 + g * (1.0 - s))


def _resident(shape):
    return pl.BlockSpec(shape, lambda *_: (0,) * len(shape), pipeline_mode=pl.Buffered(1))


def _compiler_params(sem=None):
    if sem is None:
        return pltpu.CompilerParams(vmem_limit_bytes=VMEM_LIMIT)
    return pltpu.CompilerParams(dimension_semantics=sem, vmem_limit_bytes=VMEM_LIMIT)


def _attn_bias():
    t = np.arange(BLOCK)[None, :]
    j = np.arange(BLOCK)[:, None]
    current = j <= t
    dist = np.where(current, t - j, t + BLOCK - j).astype(np.float32)
    out = np.zeros((2, 2, BLOCK, GQA * BLOCK), np.float32)
    for variant in range(2):
        valid = current | (variant == 1)
        for kv in range(2):
            for g in range(GQA):
                slope = np.float32(2.0 ** (-(kv * GQA + g + 1)))
                out[variant, kv, :, g * BLOCK:(g + 1) * BLOCK] = np.where(valid, -slope * dist, np.float32(NEG_INF))
    return out


def _replicate_head(kx, kv):
    rolled = pltpu.roll(kx, 64, 1)
    lane = lax.broadcasted_iota(jnp.int32, kx.shape, 1)
    dup = jnp.where(lane < 64, kx, rolled) if kv == 0 else jnp.where(lane < 64, rolled, kx)
    return jnp.concatenate([dup, dup], axis=1).astype(BF16)


def _stack_heads(qv):
    lane = lax.broadcasted_iota(jnp.int32, qv.shape, 1)
    zero = jnp.zeros_like(qv)
    return jnp.concatenate([jnp.where((lane >= 64 * g) & (lane < 64 * g + 64), qv, zero) for g in range(GQA)], axis=0)


def _unstack_heads(xs):
    lane = lax.broadcasted_iota(jnp.int32, (BLOCK, 256), 1)
    return jnp.where(lane < 64, xs[0:128], jnp.where(lane < 128, xs[128:256], jnp.where(lane < 192, xs[256:384], xs[384:512])))


def _fold_heads(r):
    h = r[:, 0:128] + r[:, 128:256]
    return h + pltpu.roll(h, 64, 1)


def _sink_row(sink_ref, layer, kv):
    lane = lax.broadcasted_iota(jnp.int32, (1, GQA * BLOCK), 1)
    s4 = [sink_ref[layer, kv * GQA + g] for g in range(GQA)]
    return jnp.where(lane < 128, s4[0], jnp.where(lane < 256, s4[1], jnp.where(lane < 384, s4[2], s4[3])))


def _band_is_current():
    j = lax.broadcasted_iota(jnp.int32, (BLOCK, GQA * BLOCK), 0)
    t = lax.broadcasted_iota(jnp.int32, (BLOCK, GQA * BLOCK), 1) & (BLOCK - 1)
    return j <= t


def _pack_band(full, current):
    return jnp.where(current, full[BLOCK:], full[:BLOCK])


def _unpack_band(packed, current):
    zero = jnp.zeros_like(packed)
    return jnp.concatenate([jnp.where(current, zero, packed), jnp.where(current, packed, zero)], axis=0)


def _probs_keys_major(k_rep, q_st, bias, sink, current):
    st = _pack_band(_nt(k_rep, q_st), current) * SCALE + bias
    m = jnp.maximum(jnp.max(st, axis=0, keepdims=True), sink)
    p = jnp.exp(st - m)
    esink = jnp.exp(sink - m)
    rl = 1.0 / (jnp.sum(p, axis=0, keepdims=True) + esink)
    return p * rl, esink * rl


WINDOW_HALO = 16


def _window_sum(ext, w, forward):
    s = ext
    sh = 1
    while sh < w:
        s = s + pltpu.roll(s, (ext.shape[0] - sh) if forward else sh, 0)
        sh *= 2
    return s


def _inv_count(n, w):
    t = n * BLOCK + lax.broadcasted_iota(jnp.int32, (BLOCK, 1), 0) + 1
    return 1.0 / jnp.minimum(t.astype(F32), float(w))


def _kv_ext(ref, n):
    r0 = pl.multiple_of(jnp.maximum(n - 1, 0) * BLOCK, BLOCK)
    r1 = pl.multiple_of(n * BLOCK, BLOCK)
    return jnp.concatenate([ref[pl.ds(r0, BLOCK), :], ref[pl.ds(r1, BLOCK), :]], axis=0)


def _rows_of(vec_ref, pack_ref, row0):
    for r in range(D_MODEL // 128):
        pack_ref[row0 + r:row0 + r + 1, :] = vec_ref[:, 128 * r:128 * (r + 1)]


FRONT_TILE = 4 * BLOCK


def _fwd_front(layer, x, norm_pre, w_in_t, token, sinks, pool_w, pool_scale, bias):
    tm = FRONT_TILE

    def body(sink_ref, x_ref, g_ref, w_ref, _, pw_ref, sc_ref, bias_ref,
             u_ref, pg_ref, q_ref, k_ref, v_ref, ag_ref, z_ref, a_ref, uprev, kprev, vprev):
        i = pl.program_id(0)

        @pl.when(i == 0)
        def _():
            uprev[...] = jnp.zeros_like(uprev)
            kprev[...] = jnp.zeros_like(kprev)
            vprev[...] = jnp.zeros_like(vprev)

        xv = x_ref[...]
        r = lax.rsqrt(jnp.mean(xv * xv, axis=-1, keepdims=True) + EPS)
        h = (xv * r * g_ref[layer:layer + 1, :]).astype(BF16)
        u_ref[...] = _nt(h, w_ref[COL_U:COL_PG, :])
        pg_ref[...] = _nt(h, w_ref[COL_PG:COL_Q, :])
        for sb in range(tm // BLOCK):
            n = (tm // BLOCK) * i + sb
            rows = slice(BLOCK * sb, BLOCK * (sb + 1))
            before = slice(BLOCK * (sb - 1), BLOCK * sb)
            uv = u_ref[rows, :]
            halo = (uprev[BLOCK - WINDOW_HALO:, :] if sb == 0
                    else u_ref[BLOCK * sb - WINDOW_HALO:BLOCK * sb, :])
            ext = jnp.concatenate([halo, uv], axis=0)
            for g, w in enumerate(POOL_WINDOWS):
                cs = slice(BLOCK * g, BLOCK * (g + 1))
                win = _window_sum(ext[:, cs], w, forward=False)[WINDOW_HALO:]
                pooled = win * _inv_count(n, w) - uv[:, cs]
                mixed = _nn(pooled.astype(BF16), pw_ref[g].astype(BF16))
                gate, _ = _silu_parts(pg_ref[rows, cs])
                z_ref[rows, cs] = (mixed * sc_ref[layer:layer + 1, cs] * gate).astype(BF16)

        q_ref[...] = _nt(h, w_ref[COL_Q:COL_K, :]).astype(BF16)
        k_ref[...] = _nt(h, w_ref[COL_K:COL_V, :])
        v_ref[...] = _nt(h, w_ref[COL_V:COL_AG, :])
        ag_ref[...] = _nt(h, w_ref[COL_AG:D_IN, :])

        current = _band_is_current()
        for sb in range(tm // BLOCK):
            n = (tm // BLOCK) * i + sb
            rows = slice(BLOCK * sb, BLOCK * (sb + 1))
            before = slice(BLOCK * (sb - 1), BLOCK * sb)
            kx = jnp.concatenate([kprev[...] if sb == 0 else k_ref[before, :], k_ref[rows, :]], axis=0)
            vx = jnp.concatenate([vprev[...] if sb == 0 else v_ref[before, :], v_ref[rows, :]], axis=0)
            variant = jnp.minimum(n, 1) if sb == 0 else 1
            for kv in range(2):
                cs = slice(256 * kv, 256 * (kv + 1))
                p, _ = _probs_keys_major(_replicate_head(kx, kv), _stack_heads(q_ref[rows, cs]),
                                         bias_ref[variant, kv], _sink_row(sink_ref, layer, kv), current)
                o = _unstack_heads(_tn(_unpack_band(p.astype(BF16), current), _replicate_head(vx, kv)))
                a_ref[rows, cs] = o
                gate, _ = _silu_parts(ag_ref[rows, cs])
                z_ref[rows, D_POOL + 256 * kv:D_POOL + 256 * (kv + 1)] = (o * gate).astype(BF16)

        tail = slice(tm - BLOCK, tm)
        uprev[...] = u_ref[tail, :]
        kprev[...] = k_ref[tail, :]
        vprev[...] = v_ref[tail, :]

    row = lambda c: pl.BlockSpec((tm, c), lambda i: (i, 0))
    const = lambda shape: pl.BlockSpec(shape, lambda i: (0,) * len(shape))
    return pl.pallas_call(
        body, name=f"fwd_front{layer}", grid=(SEQ // tm,),
        in_specs=[pl.BlockSpec(memory_space=pltpu.SMEM), row(D_MODEL), const((DEPTH, D_MODEL)),
                  _resident((D_IN, D_MODEL)), const((8, 128)),
                  pl.BlockSpec((None, 4, BLOCK, BLOCK), lambda i: (layer, 0, 0, 0)), const((DEPTH, D_POOL)),
                  _resident((2, 2, BLOCK, GQA * BLOCK))],
        out_specs=[row(D_POOL), row(D_POOL), row(D_ATTN), row(D_KV), row(D_KV), row(D_ATTN), row(D_MODEL),
                   row(D_ATTN)],
        out_shape=[jax.ShapeDtypeStruct((SEQ, D_POOL), F32), jax.ShapeDtypeStruct((SEQ, D_POOL), F32),
                   jax.ShapeDtypeStruct((SEQ, D_ATTN), BF16), jax.ShapeDtypeStruct((SEQ, D_KV), F32),
                   jax.ShapeDtypeStruct((SEQ, D_KV), F32), jax.ShapeDtypeStruct((SEQ, D_ATTN), F32),
                   jax.ShapeDtypeStruct((SEQ, D_MODEL), BF16), jax.ShapeDtypeStruct((SEQ, D_ATTN), F32)],
        scratch_shapes=[pltpu.VMEM((BLOCK, D_POOL), F32), pltpu.VMEM((BLOCK, D_KV), F32),
                        pltpu.VMEM((BLOCK, D_KV), F32)],
        compiler_params=_compiler_params(("arbitrary",)),
    )(sinks, x, norm_pre, w_in_t, token, pool_w, pool_scale, bias)


def _fwd_out(layer, z, x, norm_post, w_out, token):
    tm = FWD_OUT_TILE

    def body(z_ref, x_ref, g_ref, w_ref, _, xn_ref, y_ref):
        y = _nn(z_ref[...], w_ref[...])
        y_ref[...] = y
        r = lax.rsqrt(jnp.mean(y * y, axis=-1, keepdims=True) + EPS)
        xn_ref[...] = x_ref[...] + y * r * g_ref[layer:layer + 1, :]

    row = lambda c: pl.BlockSpec((tm, c), lambda i: (i, 0))
    return pl.pallas_call(
        body, name=f"fwd_out{layer}", grid=(SEQ // tm,),
        in_specs=[row(D_MODEL), row(D_MODEL), pl.BlockSpec((DEPTH, D_MODEL), lambda i: (0, 0)),
                  _resident((D_MODEL, D_MODEL)), pl.BlockSpec((8, 128), lambda i: (0, 0))],
        out_specs=[row(D_MODEL), row(D_MODEL)],
        out_shape=[jax.ShapeDtypeStruct((SEQ, D_MODEL), F32), jax.ShapeDtypeStruct((SEQ, D_MODEL), F32)],
        compiler_params=_compiler_params(("arbitrary",)),
    )(z, x, norm_post, w_out, token)


BACK_TILE = 2 * BLOCK


def _bwd_back(layer, top, dxo_or_xf, target_or_token, y, z, norm_post, w_out, sinks, u, pg, q, k, v, ag, a,
              pool_w, pool_scale, bias):
    tm = BACK_TILE
    steps = SEQ // tm
    last = steps - 1
    per = tm // BLOCK

    def body(*refs):
        refs = list(refs)
        sink_ref, first, second = refs[:3]
        (y_ref, z_ref, g_ref, w_ref, u_ref, up_ref, pg_ref, q_ref, k_ref, v_ref, ag_ref, a_ref, pw_ref, sc_ref,
         bias_ref) = refs[3:18]
        del refs[:18]
        dxo_ref = refs.pop(0) if top else None
        dp_ref, dw_ref, pack_ref, acc, dg, lacc, dzs, ck, cv, ce = refs
        i = pl.program_id(0)
        blk = last - i

        @pl.when(i == 0)
        def _():
            acc[...] = jnp.zeros_like(acc)
            dg[...] = jnp.zeros_like(dg)
            lacc[...] = jnp.zeros_like(lacc)
            pack_ref[...] = jnp.zeros_like(pack_ref)
            ck[...] = jnp.zeros_like(ck)
            cv[...] = jnp.zeros_like(cv)
            ce[...] = jnp.zeros_like(ce)

        if top:
            d = first[...] - second[...]
            dxo_v = d * (1.0 / D_MODEL)
            dxo_ref[...] = dxo_v
            part = jnp.sum(d * d, axis=-1, keepdims=True) * (1.0 / D_MODEL)
            lacc[...] += 0.5 * jnp.sum(part, axis=0, keepdims=True)
        else:
            dxo_v = first[...]
        yv = y_ref[...]
        r = lax.rsqrt(jnp.mean(yv * yv, axis=-1, keepdims=True) + EPS)
        yn = yv * r
        dg[...] += jnp.sum(dxo_v * yn, axis=0, keepdims=True)
        dyn = dxo_v * g_ref[layer:layer + 1, :]
        dy = (r * (dyn - yn * jnp.mean(dyn * yn, axis=-1, keepdims=True))).astype(BF16)
        dzs[...] = _nt(dy, w_ref[...])
        acc[...] += _tn(z_ref[...], dy)

        lane = lax.broadcasted_iota(jnp.int32, (1, 128), 1)
        lane2 = lax.broadcasted_iota(jnp.int32, (256, 128), 1)
        current = _band_is_current()
        for sb in reversed(range(per)):
            n = per * blk + sb
            rows = slice(BLOCK * sb, BLOCK * (sb + 1))

            uv = u_ref[rows, :]
            if sb == 0:
                halo = up_ref[BLOCK - WINDOW_HALO:, :] * (n > 0).astype(F32)
            else:
                halo = u_ref[BLOCK * sb - WINDOW_HALO:BLOCK * sb, :]
            ext = jnp.concatenate([halo, uv], axis=0)
            for g, w in enumerate(POOL_WINDOWS):
                cs = slice(BLOCK * g, BLOCK * (g + 1))
                inv = _inv_count(n, w)
                win = _window_sum(ext[:, cs], w, forward=False)[WINDOW_HALO:]
                pooled = win * inv - uv[:, cs]
                pw_g = pw_ref[g].astype(BF16)
                mixed = _nn(pooled.astype(BF16), pw_g)
                gate, dgate = _silu_parts(pg_ref[rows, cs])
                dzp = dzs[rows, cs]
                sc = sc_ref[layer:layer + 1, cs]
                dpm = dzp * gate
                dp_ref[rows, COL_PG + BLOCK * g:COL_PG + BLOCK * (g + 1)] = (dzp * (mixed * sc) * dgate).astype(BF16)
                pack_ref[ROW_SC + g:ROW_SC + g + 1, :] += jnp.sum(dpm * mixed, axis=0, keepdims=True)
                dmixed = (dpm * sc).astype(BF16)
                pack_ref[ROW_PW + BLOCK * g:ROW_PW + BLOCK * (g + 1), :] += _tn(pooled.astype(BF16), dmixed)
                dpooled = _nt(dmixed, pw_g)
                e = dpooled * inv
                lead = _window_sum(jnp.concatenate([e, ce[:WINDOW_HALO, cs]], axis=0), w, forward=True)[:BLOCK]
                dp_ref[rows, COL_U + BLOCK * g:COL_U + BLOCK * (g + 1)] = (lead - dpooled).astype(BF16)
                ce[:, cs] = e

            kx = _kv_ext(k_ref, n)
            vx = _kv_ext(v_ref, n)
            variant = jnp.minimum(n, 1) if sb == 0 else 1
            dsink_row = jnp.zeros((1, 128), F32)
            tks, tvs = [], []
            for kv in range(2):
                cs = slice(256 * kv, 256 * (kv + 1))
                k_rep = _replicate_head(kx, kv)
                v_rep = _replicate_head(vx, kv)
                q_st = _stack_heads(q_ref[rows, cs])
                gate, dgate = _silu_parts(ag_ref[rows, cs])
                dza = dzs[rows, D_POOL + 256 * kv:D_POOL + 256 * (kv + 1)]
                dp_ref[rows, COL_AG + 256 * kv:COL_AG + 256 * (kv + 1)] = (dza * a_ref[rows, cs] * dgate).astype(BF16)
                da_st = _stack_heads((dza * gate).astype(BF16))
                p, psink = _probs_keys_major(k_rep, q_st, bias_ref[variant, kv], _sink_row(sink_ref, layer, kv),
                                             current)
                dpt = _pack_band(_nt(v_rep, da_st), current)
                delta = jnp.sum(p * dpt, axis=0, keepdims=True)
                dst = _unpack_band((p * (dpt - delta) * SCALE).astype(BF16), current)
                sink_terms = psink * delta
                for g in range(GQA):
                    dsink = -jnp.sum(sink_terms[:, BLOCK * g:BLOCK * (g + 1)], axis=1, keepdims=True)
                    dsink_row = dsink_row + jnp.where(lane == kv * GQA + g, dsink, 0.0)
                dp_ref[rows, COL_Q + 256 * kv:COL_Q + 256 * (kv + 1)] = _unstack_heads(_tn(dst, k_rep)).astype(BF16)
                tks.append(_fold_heads(_nn(dst, q_st)))
                tvs.append(_fold_heads(_nn(_unpack_band(p.astype(BF16), current), da_st)))
            pack_ref[ROW_SINK:ROW_SINK + 1, :] += dsink_row
            dkx = jnp.where(lane2 < 64, tks[0], tks[1])
            dvx = jnp.where(lane2 < 64, tvs[0], tvs[1])
            dp_ref[rows, COL_K:COL_V] = (ck[...] + dkx[BLOCK:]).astype(BF16)
            dp_ref[rows, COL_V:COL_AG] = (cv[...] + dvx[BLOCK:]).astype(BF16)
            ck[...] = dkx[:BLOCK]
            cv[...] = dvx[:BLOCK]

        @pl.when(i == steps - 1)
        def _():
            dw_ref[...] = acc[...].astype(BF16)
            _rows_of(dg, pack_ref, ROW_NPOST)
            pack_ref[ROW_LOSS:ROW_LOSS + 1, :] = jnp.where(lane == 0, lacc[...], 0.0)

    row = lambda c: pl.BlockSpec((tm, c), lambda i: (last - i, 0))
    const = lambda shape: pl.BlockSpec(shape, lambda i: (0,) * len(shape))
    act = jax.ShapeDtypeStruct((SEQ, D_MODEL), F32)
    return pl.pallas_call(
        body, name=f"bwd_back{layer}", grid=(steps,),
        in_specs=[pl.BlockSpec(memory_space=pltpu.SMEM), row(D_MODEL), row(D_MODEL) if top else const((8, 128)),
                  row(D_MODEL), row(D_MODEL), const((DEPTH, D_MODEL)), _resident((D_MODEL, D_MODEL)),
                  row(D_POOL), pl.BlockSpec((BLOCK, D_POOL), lambda i: (jnp.maximum(per * (last - i) - 1, 0), 0)),
                  row(D_POOL), row(D_ATTN), _resident((SEQ, D_KV)), _resident((SEQ, D_KV)), row(D_ATTN), row(D_ATTN),
                  pl.BlockSpec((None, 4, BLOCK, BLOCK), lambda i: (layer, 0, 0, 0)), const((DEPTH, D_POOL)),
                  _resident((2, 2, BLOCK, GQA * BLOCK))],
        out_specs=([row(D_MODEL)] * (1 if top else 0)
                   + [row(D_IN), const((D_MODEL, D_MODEL)), const((PACK_ROWS, 128))]),
        out_shape=([act] * (1 if top else 0)
                   + [jax.ShapeDtypeStruct((SEQ, D_IN), BF16), jax.ShapeDtypeStruct((D_MODEL, D_MODEL), BF16),
                      jax.ShapeDtypeStruct((PACK_ROWS, 128), F32)]),
        scratch_shapes=[pltpu.VMEM((D_MODEL, D_MODEL), F32), pltpu.VMEM((1, D_MODEL), F32), pltpu.VMEM((1, 1), F32),
                        pltpu.VMEM((tm, D_MODEL), F32), pltpu.VMEM((BLOCK, D_KV), F32), pltpu.VMEM((BLOCK, D_KV), F32),
                        pltpu.VMEM((BLOCK, D_POOL), F32)],
        compiler_params=_compiler_params(("arbitrary",)),
    )(sinks, dxo_or_xf, target_or_token, y, z, norm_post, w_out, u, u, pg, q, k, v, ag, a, pool_w, pool_scale, bias)


def _bwd_in(layer, part, token, dproj, x, norm_pre, dxo=None, w_in_t=None):
    pair = part in ("dw_pair", "both_pair")
    want_dw, want_dx = part != "dx", part in ("both", "dx", "both_pair")
    tm = TOKEN_TILE
    steps = SEQ // tm
    cw = 256

    def body(*refs):
        refs = list(refs)
        dp_ref, x_ref, g_ref = refs[1:4]
        del refs[:4]
        if want_dx:
            dxo_ref, w_ref, dx_ref, dgo_ref = refs[:4]
            del refs[:4]
            dg = refs.pop()
        if pair:
            hs_ref, hm_ref, acc, mine_buf, theirs_buf, send_sem, recv_sem = refs
        elif want_dw:
            dw_ref, acc = refs
        i = pl.program_id(0)

        @pl.when(i == 0)
        def _():
            if pair:
                _handshake([(lax.axis_index("x"), lax.axis_index("y"), 1 - lax.axis_index("c"))])
            if want_dw:
                acc[...] = jnp.zeros_like(acc)
            if want_dx:
                dg[...] = jnp.zeros_like(dg)

        xv = x_ref[...]
        gv = g_ref[layer:layer + 1, :]
        r = lax.rsqrt(jnp.mean(xv * xv, axis=-1, keepdims=True) + EPS)
        xn = xv * r
        if want_dw:
            hb = (xn * gv).astype(BF16)
            for c in range(0, D_IN, cw):
                acc[c:c + cw, :] += _tn(dp_ref[:, c:c + cw], hb)
        if want_dx:
            dh = _nn(dp_ref[...], w_ref[...])
            dg[...] += jnp.sum(dh * xn, axis=0, keepdims=True)
            dhn = dh * gv
            dx_ref[...] = dxo_ref[...] + r * (dhn - xn * jnp.mean(dhn * xn, axis=-1, keepdims=True))

        @pl.when(i == steps - 1)
        def _():
            if pair:
                x, y, c = _mesh_pos()

                def rows_for(q, core):
                    return pl.ds(pl.multiple_of((2 * q + core) * IN_SHARD, 8), IN_SHARD)

                swaps = []
                for q in range(4):
                    mine_buf[q] = acc[rows_for(q, 1 - c), :].astype(BF16)
                    swaps.append(pltpu.make_async_remote_copy(
                        src_ref=mine_buf.at[q], dst_ref=theirs_buf.at[q], send_sem=send_sem.at[q],
                        recv_sem=recv_sem.at[q], device_id=(x, y, 1 - c), device_id_type=MESH))
                    swaps[-1].start()
                for cp in swaps:
                    cp.wait()
                for j, q in enumerate([2 * (1 - x) + y, 2 * x + (1 - y), 2 * (1 - x) + (1 - y)]):
                    hs_ref[j] = (acc[rows_for(q, c), :] + theirs_buf[q].astype(F32)).astype(BF16)
                hm_ref[...] = acc[rows_for(2 * x + y, c), :] + theirs_buf[2 * x + y].astype(F32)
            elif want_dw:
                dw_ref[...] = acc[...].astype(BF16)
            if want_dx:
                _rows_of(dg, dgo_ref, 0)

    row = lambda c: pl.BlockSpec((tm, c), lambda i: (i, 0))
    const = lambda shape: pl.BlockSpec(shape, lambda i: (0,) * len(shape))
    in_specs = [const((8, 128)), row(D_IN), row(D_MODEL), const((DEPTH, D_MODEL))]
    operands = [token, dproj, x, norm_pre]
    out_specs, out_shape, scratch = [], [], []
    if want_dx:
        in_specs += [row(D_MODEL), _resident((D_IN, D_MODEL))]
        operands += [dxo, w_in_t]
        out_specs += [row(D_MODEL), const((8, 128))]
        out_shape += [jax.ShapeDtypeStruct((SEQ, D_MODEL), F32), jax.ShapeDtypeStruct((8, 128), F32)]
    if pair:
        out_specs += [const((3, IN_SHARD, D_MODEL)), const((IN_SHARD, D_MODEL))]
        out_shape += [jax.ShapeDtypeStruct((3, IN_SHARD, D_MODEL), BF16), jax.ShapeDtypeStruct((IN_SHARD, D_MODEL), F32)]
        scratch += [pltpu.VMEM((D_IN, D_MODEL), F32), pltpu.VMEM((4, IN_SHARD, D_MODEL), BF16),
                    pltpu.VMEM((4, IN_SHARD, D_MODEL), BF16), pltpu.SemaphoreType.DMA((4,)), pltpu.SemaphoreType.DMA((4,))]
    elif want_dw:
        out_specs.append(const((D_IN, D_MODEL)))
        out_shape.append(jax.ShapeDtypeStruct((D_IN, D_MODEL), BF16))
        scratch.append(pltpu.VMEM((D_IN, D_MODEL), F32))
    if want_dx:
        scratch.append(pltpu.VMEM((1, D_MODEL), F32))
    params = pltpu.CompilerParams(dimension_semantics=("arbitrary",), vmem_limit_bytes=VMEM_LIMIT,
                                  collective_id=COLLECTIVE_PAIR_SUM[layer] if pair else None)
    return pl.pallas_call(
        body, name=f"bwd_in_{part}{layer}", grid=(steps,),
        in_specs=in_specs, out_specs=out_specs, out_shape=out_shape, scratch_shapes=scratch,
        compiler_params=params,
    )(*operands)


def _mesh_pos():
    return lax.axis_index("x"), lax.axis_index("y"), lax.axis_index("c")


def _device_rows(ref, m, px, py, pc):
    return ref.at[pl.ds(pl.multiple_of((4 * px + 2 * py + pc) * m, 16 if m % 16 == 0 else 8), m), :]


def _allgather(srcs, out_dtype, name, later=()):
    na, nb = len(srcs), len(later)
    every = list(srcs) + list(later)
    shapes = [(a.shape[-2], a.shape[-1]) for a, _ in every]

    def body(*refs):
        xs, refs = refs[:na + nb], refs[na + nb:]
        outs, cast, land, refs = refs[:na], refs[na:na + nb], refs[na + nb:na + 2 * nb], refs[na + 2 * nb:]
        stage, raw, (send_sems, recv_sems, local_sems, load_sems) = refs[:na], refs[na:2 * na + nb], refs[2 * na + nb:]
        loads = [pltpu.make_async_copy(xs[i].at[every[i][1]], raw[i], load_sems.at[i]) for i in range(na + nb)]
        for cp in loads:
            cp.start()
        x, y, c = _mesh_pos()
        me, sibling = (x, y, c), (x, y, 1 - c)
        near = [(1 - x, y), (x, 1 - y)]
        far = (1 - x, 1 - y)
        relay_from, relay_to = (x ^ (1 - c), y ^ c), (x ^ c, y ^ (1 - c))
        _handshake([sibling] + [(*chip, c) for chip in near])
        k_from, k_to = 1 + c, 2 - c

        def slot(a, px, py, pc):
            return _device_rows(outs[a], shapes[a][0], px, py, pc)

        def copy(a, k, block, to, src=None):
            return pltpu.make_async_remote_copy(
                src_ref=slot(a, *block) if src is None else src, dst_ref=slot(a, *block),
                send_sem=send_sems.at[a, k], recv_sem=recv_sems.at[a, k], device_id=to, device_id_type=MESH)

        def cast_block(i):
            loads[i].wait()
            return raw[i][...].astype(out_dtype)

        for a in range(na):
            stage[a][...] = cast_block(a)
        mine = [pltpu.make_async_copy(stage[a], slot(a, *me), local_sems.at[a]) for a in range(na)]
        for cp in mine:
            cp.start()
        sent = []
        for a in range(na):
            sent.append(copy(a, 0, me, sibling, src=stage[a]))
            sent += [copy(a, 1 + j, me, (*chip, c), src=stage[a]) for j, chip in enumerate(near)]
        for cp in sent:
            cp.start()
        for b in range(nb):
            cast[b][...] = cast_block(na + b)
            cp = pltpu.make_async_copy(cast[b], _device_rows(land[b], shapes[na + b][0], *me), local_sems.at[na + b])
            cp.start()
            mine.append(cp)
        for a in range(na):
            copy(a, k_from, (*relay_from, c), me).wait_recv()
            sent += [copy(a, 3, (*relay_from, c), (*relay_to, c)), copy(a, 3 + k_from, (*relay_from, c), sibling)]
            sent[-2].start()
            sent[-1].start()
        for a in range(na):
            copy(a, k_to, (*relay_to, c), me).wait_recv()
            sent.append(copy(a, 3 + k_to, (*relay_to, c), sibling))
            sent[-1].start()
        for a in range(na):
            copy(a, 3, (*far, c), me).wait_recv()
            sent.append(copy(a, 6, (*far, c), sibling))
            sent[-1].start()
        for a in range(na):
            copy(a, 0, sibling, me).wait_recv()
            for j, chip in enumerate(near + [far]):
                copy(a, 4 + j, (*chip, 1 - c), me).wait_recv()
        for cp in sent:
            cp.wait_send()
        for cp in mine:
            cp.wait()

    vmem = pl.BlockSpec(memory_space=pltpu.VMEM)
    hbm = pl.BlockSpec(memory_space=pl.ANY)
    gathered = [jax.ShapeDtypeStruct((N_DEV * m, n), out_dtype) for m, n in shapes]
    out = pl.pallas_call(
        body, name=name,
        in_specs=[hbm] * (na + nb),
        out_specs=[hbm] * na + [vmem] * nb + [hbm] * nb,
        out_shape=gathered[:na] + [jax.ShapeDtypeStruct(s, out_dtype) for s in shapes[na:]] + gathered[na:],
        scratch_shapes=([pltpu.VMEM(s, out_dtype) for s in shapes[:na]]
                        + [pltpu.VMEM(s, a.dtype) for s, (a, _) in zip(shapes, every)]
                        + [pltpu.SemaphoreType.DMA((na, 7)), pltpu.SemaphoreType.DMA((na, 7)),
                           pltpu.SemaphoreType.DMA((na + nb,)), pltpu.SemaphoreType.DMA((na + nb,))]),
        compiler_params=pltpu.CompilerParams(vmem_limit_bytes=VMEM_LIMIT, collective_id=COLLECTIVE_GATHER_W0),
    )(*[a for a, _ in every])
    return out[:na], out[na:na + nb], out[na + nb:]


ALL_PEERS = tuple(range(1, N_DEV))
SIBLING_AND_SAME_CORE = (1, 2, 4, 6)


def _related(k, x, y, c):
    return x ^ ((k >> 2) & 1), y ^ ((k >> 1) & 1), c ^ (k & 1)


def _gather_start(blocks, lands, relations, collective_id, name):
    na = len(blocks)

    def body(*refs):
        src, land, sems, token = refs[:na], refs[na:2 * na], refs[2 * na:4 * na], refs[-1]
        x, y, c = _mesh_pos()
        _handshake([_related(k, x, y, c) for k in sorted(set().union(*relations))])
        for a in range(na):
            for k in relations[a]:
                pltpu.make_async_remote_copy(
                    src_ref=src[a], dst_ref=_device_rows(land[a], blocks[a].shape[0], x, y, c),
                    send_sem=sems[2 * a].at[k - 1], recv_sem=sems[2 * a + 1].at[k - 1],
                    device_id=_related(k, x, y, c), device_id_type=MESH).start()
        token[...] = jnp.zeros_like(token)

    bufs = [pltpu.HBM(t.shape, t.dtype) for t in list(blocks) + list(lands)]
    out = pl.pallas_call(
        body, name=name,
        out_shape=(*([pltpu.SemaphoreType.DMA((N_DEV - 1,))] * (2 * na)), *bufs, jax.ShapeDtypeStruct((8, 128), F32)),
        in_specs=[_HBM] * (2 * na),
        out_specs=(*([_SEM] * (2 * na)), *([_HBM] * (2 * na)), pl.BlockSpec(memory_space=pltpu.VMEM)),
        input_output_aliases={i: 2 * na + i for i in range(2 * na)},
        compiler_params=pltpu.CompilerParams(has_side_effects=_EFFECT, collective_id=collective_id),
    )(*[pltpu.with_memory_space_constraint(t, pltpu.HBM) for t in list(blocks) + list(lands)])
    sems = [(out[2 * a], out[2 * a + 1]) for a in range(na)]
    return sems, out[2 * na:3 * na], out[3 * na:4 * na], out[-1]


def _gather_wait(sems, block, land, relations, after, name):
    def body(src, land_ref, send_sem, recv_sem, after_ref, src_out, land_out):
        x, y, c = _mesh_pos()
        for k in relations:
            peer = _related(k, x, y, c)
            cp = pltpu.make_async_remote_copy(
                src_ref=src, dst_ref=_device_rows(land_ref, block.shape[0], *peer),
                send_sem=send_sem.at[k - 1], recv_sem=recv_sem.at[k - 1], device_id=peer, device_id_type=MESH)
            cp.wait_send()
            cp.wait_recv()

    out = pl.pallas_call(
        body, name=name,
        out_shape=(pltpu.HBM(block.shape, block.dtype), pltpu.HBM(land.shape, land.dtype)),
        in_specs=[_HBM, _HBM, _SEM, _SEM, pl.BlockSpec(memory_space=pl.ANY)],
        out_specs=[_HBM, _HBM],
        input_output_aliases={0: 0, 1: 1},
        compiler_params=pltpu.CompilerParams(has_side_effects=_EFFECT),
    )(block, land, sems[0], sems[1], after)
    return out[1]


(COLLECTIVE_GATHER_W0, COLLECTIVE_GATHER_W1, COLLECTIVE_FORWARD_W_IN1, COLLECTIVE_EXCHANGE_1, COLLECTIVE_EXCHANGE_0A,
 COLLECTIVE_EXCHANGE_0B, COLLECTIVE_GATHER_SMALL) = range(1, 8)
COLLECTIVE_PAIR_SUM = (8, 9)


def _handshake(peers):
    barrier = pltpu.get_barrier_semaphore()
    for peer in peers:
        pl.semaphore_signal(barrier, inc=1, device_id=peer, device_id_type=MESH)
    pl.semaphore_wait(barrier, len(peers))


def _forward_plan(land_ref, m):
    x, y, c = _mesh_pos()
    return [_device_rows(land_ref, m, qx, qy, c) for qx, qy in ((1 - x, y), (x, 1 - y), (1 - x, 1 - y))], (x, y, 1 - c)


def _forward_start(land, m, name):
    def body(land_ref, send_sem, recv_sem, land_out, token):
        _handshake([(lax.axis_index("x"), lax.axis_index("y"), 1 - lax.axis_index("c"))])
        rows, sibling = _forward_plan(land_ref, m)
        for j, r in enumerate(rows):
            pltpu.make_async_remote_copy(src_ref=r, dst_ref=r, send_sem=send_sem.at[j], recv_sem=recv_sem.at[j],
                                         device_id=sibling, device_id_type=MESH).start()
        token[...] = jnp.zeros_like(token)

    out = pl.pallas_call(
        body, name=name,
        out_shape=(pltpu.SemaphoreType.DMA((3,)), pltpu.SemaphoreType.DMA((3,)), pltpu.HBM(land.shape, land.dtype),
                   jax.ShapeDtypeStruct((8, 128), F32)),
        in_specs=[_HBM],
        out_specs=(_SEM, _SEM, _HBM, pl.BlockSpec(memory_space=pltpu.VMEM)),
        input_output_aliases={0: 2},
        compiler_params=pltpu.CompilerParams(has_side_effects=_EFFECT, collective_id=COLLECTIVE_FORWARD_W_IN1),
    )(pltpu.with_memory_space_constraint(land, pltpu.HBM))
    return (out[0], out[1]), out[2], out[3]


def _forward_wait(sems, land, m, after, name):
    def body(land_ref, send_sem, recv_sem, after_ref, land_out):
        x, y, c = _mesh_pos()
        mine, sibling = _forward_plan(land_ref, m)
        theirs = [_device_rows(land_ref, m, qx, qy, 1 - c) for qx, qy in ((1 - x, y), (x, 1 - y), (1 - x, 1 - y))]
        for j in range(3):
            cp = pltpu.make_async_remote_copy(src_ref=mine[j], dst_ref=theirs[j], send_sem=send_sem.at[j],
                                              recv_sem=recv_sem.at[j], device_id=sibling, device_id_type=MESH)
            cp.wait_send()
            cp.wait_recv()

    return pl.pallas_call(
        body, name=name,
        out_shape=pltpu.HBM(land.shape, land.dtype),
        in_specs=[_HBM, _SEM, _SEM, pl.BlockSpec(memory_space=pl.ANY)],
        out_specs=_HBM,
        input_output_aliases={0: 0},
        compiler_params=pltpu.CompilerParams(has_side_effects=_EFFECT),
    )(land, sems[0], sems[1], after)


_HBM = pl.BlockSpec(memory_space=pltpu.HBM)
_SEM = pl.BlockSpec(memory_space=pltpu.SEMAPHORE)
_EFFECT = pltpu.SideEffectType.DATAFLOW_SIDE_EFFECTING


def _exchange_plan(direct):
    x, y, c = _mesh_pos()
    if not direct:
        return [(j, j, (qx, qy, c)) for j, (qx, qy) in enumerate([(1 - x, y), (x, 1 - y), (1 - x, 1 - y)])]
    plan = []
    for k in range(1, N_DEV):
        px, py, pc = x ^ ((k >> 2) & 1), y ^ ((k >> 1) & 1), c ^ (k & 1)
        plan.append((4 * px + 2 * py + pc, k - 1, (px, py, pc)))
    return plan


def _exchange_copies(directs):
    copies, base = [], 0
    for a, direct in enumerate(directs):
        plan = _exchange_plan(direct)
        copies += [(a, block, slot, peer, base + slot) for block, slot, peer in plan]
        base += len(plan)
    return copies, base


def _exchange_start(srcs, directs, collective_id, name):
    na = len(srcs)
    slots = [N_DEV - 1 if direct else 3 for direct in directs]

    def body(*refs):
        src, land = refs[:na], refs[na:2 * na]
        send_sem, recv_sem = refs[2 * na], refs[2 * na + 1]
        token = refs[-1]
        _handshake([peer for _, _, peer in _exchange_plan(any(directs))])
        for a, block, slot, peer, sem in _exchange_copies(directs)[0]:
            pltpu.make_async_remote_copy(
                src_ref=src[a].at[block], dst_ref=land[a].at[slot], send_sem=send_sem.at[sem],
                recv_sem=recv_sem.at[sem], device_id=peer, device_id_type=MESH).start()
        token[...] = jnp.zeros_like(token)

    zones = [jax.ShapeDtypeStruct((n,) + t.shape[1:], t.dtype) for n, t in zip(slots, srcs)]
    bufs = [pltpu.HBM(t.shape, t.dtype) for t in list(srcs) + zones]
    out = pl.pallas_call(
        body, name=name,
        out_shape=(pltpu.SemaphoreType.DMA((sum(slots),)), pltpu.SemaphoreType.DMA((sum(slots),)), *bufs,
                   jax.ShapeDtypeStruct((8, 128), F32)),
        in_specs=[_HBM] * (2 * na),
        out_specs=(_SEM, _SEM, *([_HBM] * (2 * na)), pl.BlockSpec(memory_space=pltpu.VMEM)),
        input_output_aliases={i: 2 + i for i in range(2 * na)},
        compiler_params=pltpu.CompilerParams(has_side_effects=_EFFECT, collective_id=collective_id),
    )(*[pltpu.with_memory_space_constraint(t, pltpu.HBM) for t in srcs],
      *[pltpu.with_memory_space_constraint(lax.empty(t.shape, t.dtype), pltpu.HBM) for t in zones])
    return out[0], out[1], out[2:2 + na], out[2 + na:2 + 2 * na], out[-1]


def _exchange_wait(send_sem, recv_sem, srcs, lands, directs, after, name):
    na = len(srcs)

    def body(*refs):
        src, land = refs[:na], refs[na:2 * na]
        send_sem_ref, recv_sem_ref = refs[2 * na], refs[2 * na + 1]
        for a, block, slot, peer, sem in _exchange_copies(directs)[0]:
            cp = pltpu.make_async_remote_copy(
                src_ref=src[a].at[block], dst_ref=land[a].at[slot], send_sem=send_sem_ref.at[sem],
                recv_sem=recv_sem_ref.at[sem], device_id=peer, device_id_type=MESH)
            cp.wait_send()
            cp.wait_recv()

    bufs = [pltpu.HBM(t.shape, t.dtype) for t in list(srcs) + list(lands)]
    out = pl.pallas_call(
        body, name=name,
        out_shape=tuple(bufs),
        in_specs=[_HBM] * (2 * na) + [_SEM, _SEM, pl.BlockSpec(memory_space=pl.ANY)],
        out_specs=[_HBM] * (2 * na),
        input_output_aliases={i: i for i in range(2 * na)},
        compiler_params=pltpu.CompilerParams(has_side_effects=_EFFECT),
    )(*srcs, *lands, send_sem, recv_sem, after)
    return out[:na], out[na:]


def _own_then_slots(mine_ref, lands_ref, rows=slice(None)):
    if len(mine_ref.shape) == 3:
        x, y, c = _mesh_pos()
        total = mine_ref[4 * x + 2 * y + c, rows, :].astype(F32)
    else:
        total = mine_ref[rows, :].astype(F32)
    for j in range(lands_ref.shape[0]):
        total = total + lands_ref[j, rows, :].astype(F32)
    return total


SMALL_ROWS = 2 * PACK_SLICE + 2 * 8


def _small_block(mine, lands, dgpre, name):
    def body(*refs):
        hm, ld, dg = refs[:DEPTH], refs[DEPTH:2 * DEPTH], refs[2 * DEPTH:3 * DEPTH]
        blk, land, sem = refs[3 * DEPTH:]
        for l in range(DEPTH):
            blk[PACK_SLICE * l:PACK_SLICE * (l + 1), :] = _own_then_slots(hm[l], ld[l])
            blk[2 * PACK_SLICE + 8 * l:2 * PACK_SLICE + 8 * (l + 1), :] = dg[l][...]
        cp = pltpu.make_async_copy(blk, _device_rows(land, SMALL_ROWS, *_mesh_pos()), sem)
        cp.start()
        cp.wait()

    vmem = pl.BlockSpec(memory_space=pltpu.VMEM)
    return pl.pallas_call(
        body, name=name,
        in_specs=[vmem] * (3 * DEPTH), out_specs=[vmem, pl.BlockSpec(memory_space=pl.ANY)],
        out_shape=[jax.ShapeDtypeStruct((SMALL_ROWS, 128), F32), jax.ShapeDtypeStruct((N_DEV * SMALL_ROWS, 128), F32)],
        scratch_shapes=[pltpu.SemaphoreType.DMA],
        compiler_params=_compiler_params(),
    )(*mine, *lands, *dgpre)


def _adamw_math(w, g, m, v):
    m = ADAM_B1 * m + (1.0 - ADAM_B1) * g
    v = ADAM_B2 * v + (1.0 - ADAM_B2) * (g * g)
    m_hat = m / (1.0 - ADAM_B1 ** ADAM_STEP)
    v_hat = v / (1.0 - ADAM_B2 ** ADAM_STEP)
    delta = -ADAM_LR * (m_hat / (jnp.sqrt(v_hat) + ADAM_EPS) + ADAM_WD * w)
    return delta, m, v


def _adamw_layer(layer, mine, lands, w, m, v, earlier, token, name, rows):
    _, mm, nn = w.shape

    def body(hm_ref, ld_ref, w_ref, m_ref, v_ref, _, *refs):
        g_ref, d_ref, nm_ref, nv_ref = refs[-4:]
        g = _own_then_slots(hm_ref, ld_ref)
        g_ref[...] = g
        d, nm, nv = _adamw_math(w_ref[...], g, m_ref[...], v_ref[...])
        d_ref[...] = d
        nm_ref[...] = nm
        nv_ref[...] = nv

    spec = pl.BlockSpec((None, rows, nn), lambda i: (layer, i, 0))
    carried = [] if earlier is None else list(earlier)
    return pl.pallas_call(
        body, name=name, grid=(mm // rows,),
        in_specs=([pl.BlockSpec((rows, nn), lambda i: (i, 0)) if mine.ndim == 2
                   else pl.BlockSpec((N_DEV, rows, nn), lambda i: (0, i, 0)),
                   pl.BlockSpec((lands.shape[0], rows, nn), lambda i: (0, i, 0)),
                   spec, spec, spec] + [pl.BlockSpec(memory_space=pl.ANY)] * (1 + len(carried))),
        out_specs=[spec] * 4,
        out_shape=[jax.ShapeDtypeStruct(w.shape, F32)] * 4,
        input_output_aliases={6 + t: t for t in range(len(carried))},
        compiler_params=_compiler_params(("arbitrary",)),
    )(mine, lands, w, m, v, token, *carried)


def _adamw_small(gathered, params):
    def body(all_ref, *refs):
        ins, outs, packs = refs[:15], refs[15:15 + 21], refs[15 + 21]
        loss_ref = outs[0]
        for dev in range(N_DEV):
            for l in range(DEPTH):
                packs[l, PACK_SLICE * dev:PACK_SLICE * (dev + 1), :] = (
                    all_ref[SMALL_ROWS * dev + PACK_SLICE * l:SMALL_ROWS * dev + PACK_SLICE * (l + 1), :])
        loss_ref[...] = packs[DEPTH - 1, ROW_LOSS:ROW_LOSS + 1, 0:1]

        def update(p, sel, g):
            w_ref, m_ref, v_ref = ins[p], ins[5 + p], ins[10 + p]
            d, nm, nv = _adamw_math(w_ref[sel], g, m_ref[sel], v_ref[sel])
            for t, val in enumerate((g, d, nm, nv)):
                outs[1 + 5 * t + p][sel] = val

        for l in range(DEPTH):
            gp = packs.at[l]
            row0 = 2 * PACK_SLICE + 8 * l
            dgpre = all_ref[row0:row0 + 8, :]
            for dev in range(1, N_DEV):
                dgpre = dgpre + all_ref[SMALL_ROWS * dev + row0:SMALL_ROWS * dev + row0 + 8, :]
            for grp in range(4):
                update(0, (l, grp), gp[ROW_PW + BLOCK * grp:ROW_PW + BLOCK * (grp + 1), :])
                update(1, (slice(l, l + 1), slice(128 * grp, 128 * (grp + 1))), gp[ROW_SC + grp:ROW_SC + grp + 1, :])
            update(2, (slice(l, l + 1), slice(None)), gp[ROW_SINK:ROW_SINK + 1, 0:N_HEADS])
            for r in range(D_MODEL // 128):
                sel = (slice(l, l + 1), slice(128 * r, 128 * (r + 1)))
                update(3, sel, dgpre[r:r + 1, :])
                update(4, sel, gp[ROW_NPOST + r:ROW_NPOST + r + 1, :])

    shapes = [jax.ShapeDtypeStruct(p.shape, F32) for p in params[:5]]
    return pl.pallas_call(
        body, name="adamw_small",
        out_shape=[jax.ShapeDtypeStruct((1, 1), F32)] + shapes * 4,
        scratch_shapes=[pltpu.VMEM((DEPTH, PACK_ROWS, 128), F32)],
        compiler_params=_compiler_params(),
    )(gathered, *params)


def kernel(x, w_in, pool_w, pool_scale, attn_sinks, w_out, norm_pre, norm_post, loss_target, m_w_in, m_pool_w, m_pool_scale, m_attn_sinks, m_w_out, m_norm_pre, m_norm_post, v_w_in, v_pool_w, v_pool_scale, v_attn_sinks, v_w_out, v_norm_pre, v_norm_post):
    x0 = x.reshape(SEQ, D_MODEL)
    target = loss_target.reshape(SEQ, D_MODEL)
    bias = jnp.asarray(_attn_bias())
    w_in_t, m_in_t, v_in_t = (jnp.swapaxes(t, 1, 2) for t in (w_in, m_w_in, v_w_in))

    (win0, wout0), later, lands = _allgather([(w_in_t, 0), (w_out, 0)], BF16, "gather_w0",
                                              later=[(w_in_t, 1), (w_out, 1)])
    sems, later, lands, token = _gather_start(later, lands, [SIBLING_AND_SAME_CORE, ALL_PEERS], COLLECTIVE_GATHER_W1,
                                              "gather_w1_start")
    win_full, wout_full = [win0, None], [wout0, None]

    saved = []
    xl = x0
    for layer in range(DEPTH):
        u, pg, q, k, v, ag, z, a = _fwd_front(layer, xl, norm_pre, win_full[layer], token, attn_sinks,
                                             pool_w, pool_scale, bias)
        if layer == 0:
            land = _gather_wait(sems[0], later[0], lands[0], SIBLING_AND_SAME_CORE, z, "gather_w_in1_wait")
            fsems, land, token = _forward_start(land, IN_SHARD, "forward_w_in1_start")
        else:
            wout_full[layer] = _gather_wait(sems[1], later[1], lands[1], ALL_PEERS, z, "gather_w_out1_wait")
        x_next, y = _fwd_out(layer, z, xl, norm_post, wout_full[layer], token)
        if layer == 0:
            win_full[1] = _forward_wait(fsems, land, IN_SHARD, x_next, "forward_w_in1_wait")
        saved.append((xl, u, pg, q, k, v, ag, z, a, y))
        xl = x_next

    params_small = [pool_w, pool_scale, attn_sinks, norm_pre, norm_post,
                    m_pool_w, m_pool_scale, m_attn_sinks, m_norm_pre, m_norm_post,
                    v_pool_w, v_pool_scale, v_attn_sinks, v_norm_pre, v_norm_post]

    def start(srcs, directs, paired, collective_id, tag):
        send_sem, recv_sem, srcs, lands, started = _exchange_start(srcs, directs, collective_id, f"exchange_start{tag}")
        return (send_sem, recv_sem, srcs, lands, paired, directs), started

    def finish(handle, after, tag):
        send_sem, recv_sem, srcs, lands, paired, directs = handle
        srcs, lands = _exchange_wait(send_sem, recv_sem, srcs, lands, directs, after, f"exchange_wait{tag}")
        return [s if p is None else p for s, p in zip(srcs, paired)], lands

    def back(layer, top, first, second):
        xin, u, pg, q, k, v, ag, z, a, y = saved[layer]
        return _bwd_back(layer, top, first, second, y, z, norm_post, wout_full[layer], attn_sinks, u, pg, q, k, v,
                         ag, a, pool_w, pool_scale, bias)

    dgpre = [None] * DEPTH
    dx, dproj, gw_out, pack = back(1, True, xl, target)
    dx, dgpre[1], chip_sums, own_sum = _bwd_in(1, "both_pair", token, dproj, saved[1][0], norm_pre, dx, win_full[1])
    top, token = start([chip_sums, gw_out.reshape(N_DEV, OUT_SHARD, D_MODEL), pack.reshape(N_DEV, PACK_SLICE, 128)],
                       [False, True, True], [own_sum, None, None], COLLECTIVE_EXCHANGE_1, "1")

    dproj, gw_out, pack = back(0, False, dx, token)
    early, token = start([gw_out.reshape(N_DEV, OUT_SHARD, D_MODEL), pack.reshape(N_DEV, PACK_SLICE, 128)],
                         [True, True], [None, None], COLLECTIVE_EXCHANGE_0A, "0a")
    chip_sums, own_sum = _bwd_in(0, "dw_pair", token, dproj, saved[0][0], norm_pre)
    late, token = start([chip_sums], [False], [own_sum], COLLECTIVE_EXCHANGE_0B, "0b")
    dx, dgpre[0] = _bwd_in(0, "dx", token, dproj, saved[0][0], norm_pre, dx, win_full[0])

    own1, lands1 = finish(top, dx, "1")
    big_in = _adamw_layer(1, own1[0], lands1[0], w_in_t, m_in_t, v_in_t, None, token, "adamw_in1", ADAM_ROWS_IN)
    big_out = _adamw_layer(1, own1[1], lands1[1], w_out, m_w_out, v_w_out, None, token, "adamw_out1", ADAM_ROWS_OUT)
    own0a, lands0a = finish(early, big_out[0], "0a")
    block, land = _small_block([own0a[1], own1[2]], [lands0a[1], lands1[2]], dgpre, "small_block")
    sems, block, land, token = _gather_start([block], [land], [ALL_PEERS], COLLECTIVE_GATHER_SMALL, "gather_small_start")
    big_out = _adamw_layer(0, own0a[0], lands0a[0], w_out, m_w_out, v_w_out, big_out, token, "adamw_out0", ADAM_ROWS_OUT)
    own0b, lands0b = finish(late, big_out[0], "0b")
    big_in = _adamw_layer(0, own0b[0], lands0b[0], w_in_t, m_in_t, v_in_t, big_in, token, "adamw_in0", ADAM_ROWS_IN)
    gathered = _gather_wait(sems[0], block[0], land[0], ALL_PEERS, big_in[0], "gather_small_wait")
    small_out = _adamw_small(gathered, params_small)
    loss = small_out[0].reshape(())

    outs = [loss, dx.reshape(1, SEQ, D_MODEL)]
    for t in range(4):
        pw_, sc_, sk_, npre_, npost_ = small_out[1 + 5 * t:6 + 5 * t]
        outs += [jnp.swapaxes(big_in[t], 1, 2), pw_, sc_, sk_, big_out[t], npre_, npost_]
    return tuple(outs)
```

```python
import numpy as np
import jax
import jax.numpy as jnp
from jax import lax
from jax.experimental import pallas as pl
from jax.experimental.pallas import tpu as pltpu

F32 = jnp.float32
BF16 = jnp.bfloat16

N_DEV = 8
SEQ = 2048
D_MODEL = 1024
D_POOL = 512
D_ATTN = 512
D_KV = 128
D_IN = 2304
N_HEADS = 8
GQA = 4
HEAD_DIM = 64
BLOCK = 128
POOL_WINDOWS = (2, 4, 8, 16)
DEPTH = 2
EPS = 1e-6
NEG_INF = -1e30
SCALE = HEAD_DIM ** -0.5
IN_SHARD = D_IN // N_DEV
OUT_SHARD = D_MODEL // N_DEV

COL_U, COL_PG, COL_Q, COL_K, COL_V, COL_AG = 0, 512, 1024, 1536, 1664, 1792

ADAM_LR = 0.001
ADAM_B1 = 0.9
ADAM_B2 = 0.999
ADAM_EPS = 1e-08
ADAM_WD = 0.01
ADAM_STEP = 10

TOKEN_TILE = 512
FWD_OUT_TILE = 1024
ADAM_ROWS_IN, ADAM_ROWS_OUT = 144, 128
VMEM_LIMIT = 56 * 1024 * 1024
MESH = pl.DeviceIdType.MESH

ROW_PW, ROW_SC, ROW_SINK, ROW_NPOST, ROW_LOSS = 0, 512, 520, 536, 544
PACK_ROWS = 576
PACK_SLICE = PACK_ROWS // N_DEV


def _nn(a, b):
    return jnp.dot(a, b, preferred_element_type=F32)


def _nt(a, b):
    return lax.dot_general(a, b, (((1,), (1,)), ((), ())), preferred_element_type=F32)


def _tn(a, b):
    return lax.dot_general(a, b, (((0,), (0,)), ((), ())), preferred_element_type=F32)


def _silu_parts(g):
    s = jax.nn.sigmoid(g)
    return g * s, s * (1.0 + g * (1.0 - s))


def _resident(shape):
    return pl.BlockSpec(shape, lambda *_: (0,) * len(shape), pipeline_mode=pl.Buffered(1))


def _compiler_params(sem=None):
    if sem is None:
        return pltpu.CompilerParams(vmem_limit_bytes=VMEM_LIMIT)
    return pltpu.CompilerParams(dimension_semantics=sem, vmem_limit_bytes=VMEM_LIMIT)


def _attn_bias():
    t = np.arange(BLOCK)[None, :]
    j = np.arange(BLOCK)[:, None]
    current = j <= t
    dist = np.where(current, t - j, t + BLOCK - j).astype(np.float32)
    out = np.zeros((2, 2, BLOCK, GQA * BLOCK), np.float32)
    for variant in range(2):
        valid = current | (variant == 1)
        for kv in range(2):
            for g in range(GQA):
                slope = np.float32(2.0 ** (-(kv * GQA + g + 1)))
                out[variant, kv, :, g * BLOCK:(g + 1) * BLOCK] = np.where(valid, -slope * dist, np.float32(NEG_INF))
    return out


def _replicate_head(kx, kv):
    rolled = pltpu.roll(kx, 64, 1)
    lane = lax.broadcasted_iota(jnp.int32, kx.shape, 1)
    dup = jnp.where(lane < 64, kx, rolled) if kv == 0 else jnp.where(lane < 64, rolled, kx)
    return jnp.concatenate([dup, dup], axis=1).astype(BF16)


def _stack_heads(qv):
    lane = lax.broadcasted_iota(jnp.int32, qv.shape, 1)
    zero = jnp.zeros_like(qv)
    return jnp.concatenate([jnp.where((lane >= 64 * g) & (lane < 64 * g + 64), qv, zero) for g in range(GQA)], axis=0)


def _unstack_heads(xs):
    lane = lax.broadcasted_iota(jnp.int32, (BLOCK, 256), 1)
    return jnp.where(lane < 64, xs[0:128], jnp.where(lane < 128, xs[128:256], jnp.where(lane < 192, xs[256:384], xs[384:512])))


def _fold_heads(r):
    h = r[:, 0:128] + r[:, 128:256]
    return h + pltpu.roll(h, 64, 1)


def _sink_row(sink_ref, layer, kv):
    lane = lax.broadcasted_iota(jnp.int32, (1, GQA * BLOCK), 1)
    s4 = [sink_ref[layer, kv * GQA + g] for g in range(GQA)]
    return jnp.where(lane < 128, s4[0], jnp.where(lane < 256, s4[1], jnp.where(lane < 384, s4[2], s4[3])))


def _band_is_current():
    j = lax.broadcasted_iota(jnp.int32, (BLOCK, GQA * BLOCK), 0)
    t = lax.broadcasted_iota(jnp.int32, (BLOCK, GQA * BLOCK), 1) & (BLOCK - 1)
    return j <= t


def _pack_band(full, current):
    return jnp.where(current, full[BLOCK:], full[:BLOCK])


def _unpack_band(packed, current):
    zero = jnp.zeros_like(packed)
    return jnp.concatenate([jnp.where(current, zero, packed), jnp.where(current, packed, zero)], axis=0)


def _probs_keys_major(k_rep, q_st, bias, sink, current):
    st = _pack_band(_nt(k_rep, q_st), current) * SCALE + bias
    m = jnp.maximum(jnp.max(st, axis=0, keepdims=True), sink)
    p = jnp.exp(st - m)
    esink = jnp.exp(sink - m)
    rl = 1.0 / (jnp.sum(p, axis=0, keepdims=True) + esink)
    return p * rl, esink * rl


WINDOW_HALO = 16


def _window_sum(ext, w, forward):
    s = ext
    sh = 1
    while sh < w:
        s = s + pltpu.roll(s, (ext.shape[0] - sh) if forward else sh, 0)
        sh *= 2
    return s


def _inv_count(n, w):
    t = n * BLOCK + lax.broadcasted_iota(jnp.int32, (BLOCK, 1), 0) + 1
    return 1.0 / jnp.minimum(t.astype(F32), float(w))


def _kv_ext(ref, n):
    r0 = pl.multiple_of(jnp.maximum(n - 1, 0) * BLOCK, BLOCK)
    r1 = pl.multiple_of(n * BLOCK, BLOCK)
    return jnp.concatenate([ref[pl.ds(r0, BLOCK), :], ref[pl.ds(r1, BLOCK), :]], axis=0)


def _rows_of(vec_ref, pack_ref, row0):
    for r in range(D_MODEL // 128):
        pack_ref[row0 + r:row0 + r + 1, :] = vec_ref[:, 128 * r:128 * (r + 1)]


FRONT_TILE = 4 * BLOCK


def _fwd_front(layer, x, norm_pre, w_in_t, token, sinks, pool_w, pool_scale, bias):
    tm = FRONT_TILE

    def body(sink_ref, x_ref, g_ref, w_ref, _, pw_ref, sc_ref, bias_ref,
             u_ref, pg_ref, q_ref, k_ref, v_ref, ag_ref, z_ref, a_ref, uprev, kprev, vprev):
        i = pl.program_id(0)

        @pl.when(i == 0)
        def _():
            uprev[...] = jnp.zeros_like(uprev)
            kprev[...] = jnp.zeros_like(kprev)
            vprev[...] = jnp.zeros_like(vprev)

        xv = x_ref[...]
        r = lax.rsqrt(jnp.mean(xv * xv, axis=-1, keepdims=True) + EPS)
        h = (xv * r * g_ref[layer:layer + 1, :]).astype(BF16)
        u_ref[...] = _nt(h, w_ref[COL_U:COL_PG, :])
        pg_ref[...] = _nt(h, w_ref[COL_PG:COL_Q, :])
        for sb in range(tm // BLOCK):
            n = (tm // BLOCK) * i + sb
            rows = slice(BLOCK * sb, BLOCK * (sb + 1))
            before = slice(BLOCK * (sb - 1), BLOCK * sb)
            uv = u_ref[rows, :]
            halo = (uprev[BLOCK - WINDOW_HALO:, :] if sb == 0
                    else u_ref[BLOCK * sb - WINDOW_HALO:BLOCK * sb, :])
            ext = jnp.concatenate([halo, uv], axis=0)
            for g, w in enumerate(POOL_WINDOWS):
                cs = slice(BLOCK * g, BLOCK * (g + 1))
                win = _window_sum(ext[:, cs], w, forward=False)[WINDOW_HALO:]
                pooled = win * _inv_count(n, w) - uv[:, cs]
                mixed = _nn(pooled.astype(BF16), pw_ref[g].astype(BF16))
                gate, _ = _silu_parts(pg_ref[rows, cs])
                z_ref[rows, cs] = (mixed * sc_ref[layer:layer + 1, cs] * gate).astype(BF16)

        q_ref[...] = _nt(h, w_ref[COL_Q:COL_K, :]).astype(BF16)
        k_ref[...] = _nt(h, w_ref[COL_K:COL_V, :])
        v_ref[...] = _nt(h, w_ref[COL_V:COL_AG, :])
        ag_ref[...] = _nt(h, w_ref[COL_AG:D_IN, :])

        current = _band_is_current()
        for sb in range(tm // BLOCK):
            n = (tm // BLOCK) * i + sb
            rows = slice(BLOCK * sb, BLOCK * (sb + 1))
            before = slice(BLOCK * (sb - 1), BLOCK * sb)
            kx = jnp.concatenate([kprev[...] if sb == 0 else k_ref[before, :], k_ref[rows, :]], axis=0)
            vx = jnp.concatenate([vprev[...] if sb == 0 else v_ref[before, :], v_ref[rows, :]], axis=0)
            variant = jnp.minimum(n, 1) if sb == 0 else 1
            for kv in range(2):
                cs = slice(256 * kv, 256 * (kv + 1))
                p, _ = _probs_keys_major(_replicate_head(kx, kv), _stack_heads(q_ref[rows, cs]),
                                         bias_ref[variant, kv], _sink_row(sink_ref, layer, kv), current)
                o = _unstack_heads(_tn(_unpack_band(p.astype(BF16), current), _replicate_head(vx, kv)))
                a_ref[rows, cs] = o
                gate, _ = _silu_parts(ag_ref[rows, cs])
                z_ref[rows, D_POOL + 256 * kv:D_POOL + 256 * (kv + 1)] = (o * gate).astype(BF16)

        tail = slice(tm - BLOCK, tm)
        uprev[...] = u_ref[tail, :]
        kprev[...] = k_ref[tail, :]
        vprev[...] = v_ref[tail, :]

    row = lambda c: pl.BlockSpec((tm, c), lambda i: (i, 0))
    const = lambda shape: pl.BlockSpec(shape, lambda i: (0,) * len(shape))
    return pl.pallas_call(
        body, name=f"fwd_front{layer}", grid=(SEQ // tm,),
        in_specs=[pl.BlockSpec(memory_space=pltpu.SMEM), row(D_MODEL), const((DEPTH, D_MODEL)),
                  _resident((D_IN, D_MODEL)), const((8, 128)),
                  pl.BlockSpec((None, 4, BLOCK, BLOCK), lambda i: (layer, 0, 0, 0)), const((DEPTH, D_POOL)),
                  _resident((2, 2, BLOCK, GQA * BLOCK))],
        out_specs=[row(D_POOL), row(D_POOL), row(D_ATTN), row(D_KV), row(D_KV), row(D_ATTN), row(D_MODEL),
                   row(D_ATTN)],
        out_shape=[jax.ShapeDtypeStruct((SEQ, D_POOL), F32), jax.ShapeDtypeStruct((SEQ, D_POOL), F32),
                   jax.ShapeDtypeStruct((SEQ, D_ATTN), BF16), jax.ShapeDtypeStruct((SEQ, D_KV), F32),
                   jax.ShapeDtypeStruct((SEQ, D_KV), F32), jax.ShapeDtypeStruct((SEQ, D_ATTN), F32),
                   jax.ShapeDtypeStruct((SEQ, D_MODEL), BF16), jax.ShapeDtypeStruct((SEQ, D_ATTN), F32)],
        scratch_shapes=[pltpu.VMEM((BLOCK, D_POOL), F32), pltpu.VMEM((BLOCK, D_KV), F32),
                        pltpu.VMEM((BLOCK, D_KV), F32)],
        compiler_params=_compiler_params(("arbitrary",)),
    )(sinks, x, norm_pre, w_in_t, token, pool_w, pool_scale, bias)


def _fwd_out(layer, z, x, norm_post, w_out, token):
    tm = FWD_OUT_TILE

    def body(z_ref, x_ref, g_ref, w_ref, _, xn_ref, y_ref):
        y = _nn(z_ref[...], w_ref[...])
        y_ref[...] = y
        r = lax.rsqrt(jnp.mean(y * y, axis=-1, keepdims=True) + EPS)
        xn_ref[...] = x_ref[...] + y * r * g_ref[layer:layer + 1, :]

    row = lambda c: pl.BlockSpec((tm, c), lambda i: (i, 0))
    return pl.pallas_call(
        body, name=f"fwd_out{layer}", grid=(SEQ // tm,),
        in_specs=[row(D_MODEL), row(D_MODEL), pl.BlockSpec((DEPTH, D_MODEL), lambda i: (0, 0)),
                  _resident((D_MODEL, D_MODEL)), pl.BlockSpec((8, 128), lambda i: (0, 0))],
        out_specs=[row(D_MODEL), row(D_MODEL)],
        out_shape=[jax.ShapeDtypeStruct((SEQ, D_MODEL), F32), jax.ShapeDtypeStruct((SEQ, D_MODEL), F32)],
        compiler_params=_compiler_params(("arbitrary",)),
    )(z, x, norm_post, w_out, token)


BACK_TILE = 2 * BLOCK


def _bwd_back(layer, top, dxo_or_xf, target_or_token, y, z, norm_post, w_out, sinks, u, pg, q, k, v, ag, a,
              pool_w, pool_scale, bias):
    tm = BACK_TILE
    steps = SEQ // tm
    last = steps - 1
    per = tm // BLOCK

    def body(*refs):
        refs = list(refs)
        sink_ref, first, second = refs[:3]
        (y_ref, z_ref, g_ref, w_ref, u_ref, up_ref, pg_ref, q_ref, k_ref, v_ref, ag_ref, a_ref, pw_ref, sc_ref,
         bias_ref) = refs[3:18]
        del refs[:18]
        dxo_ref = refs.pop(0) if top else None
        dp_ref, dw_ref, pack_ref, acc, dg, lacc, dzs, ck, cv, ce = refs
        i = pl.program_id(0)
        blk = last - i

        @pl.when(i == 0)
        def _():
            acc[...] = jnp.zeros_like(acc)
            dg[...] = jnp.zeros_like(dg)
            lacc[...] = jnp.zeros_like(lacc)
            pack_ref[...] = jnp.zeros_like(pack_ref)
            ck[...] = jnp.zeros_like(ck)
            cv[...] = jnp.zeros_like(cv)
            ce[...] = jnp.zeros_like(ce)

        if top:
            d = first[...] - second[...]
            dxo_v = d * (1.0 / D_MODEL)
            dxo_ref[...] = dxo_v
            part = jnp.sum(d * d, axis=-1, keepdims=True) * (1.0 / D_MODEL)
            lacc[...] += 0.5 * jnp.sum(part, axis=0, keepdims=True)
        else:
            dxo_v = first[...]
        yv = y_ref[...]
        r = lax.rsqrt(jnp.mean(yv * yv, axis=-1, keepdims=True) + EPS)
        yn = yv * r
        dg[...] += jnp.sum(dxo_v * yn, axis=0, keepdims=True)
        dyn = dxo_v * g_ref[layer:layer + 1, :]
        dy = (r * (dyn - yn * jnp.mean(dyn * yn, axis=-1, keepdims=True))).astype(BF16)
        dzs[...] = _nt(dy, w_ref[...])
        acc[...] += _tn(z_ref[...], dy)

        lane = lax.broadcasted_iota(jnp.int32, (1, 128), 1)
        lane2 = lax.broadcasted_iota(jnp.int32, (256, 128), 1)
        current = _band_is_current()
        for sb in reversed(range(per)):
            n = per * blk + sb
            rows = slice(BLOCK * sb, BLOCK * (sb + 1))

            uv = u_ref[rows, :]
            if sb == 0:
                halo = up_ref[BLOCK - WINDOW_HALO:, :] * (n > 0).astype(F32)
            else:
                halo = u_ref[BLOCK * sb - WINDOW_HALO:BLOCK * sb, :]
            ext = jnp.concatenate([halo, uv], axis=0)
            for g, w in enumerate(POOL_WINDOWS):
                cs = slice(BLOCK * g, BLOCK * (g + 1))
                inv = _inv_count(n, w)
                win = _window_sum(ext[:, cs], w, forward=False)[WINDOW_HALO:]
                pooled = win * inv - uv[:, cs]
                pw_g = pw_ref[g].astype(BF16)
                mixed = _nn(pooled.astype(BF16), pw_g)
                gate, dgate = _silu_parts(pg_ref[rows, cs])
                dzp = dzs[rows, cs]
                sc = sc_ref[layer:layer + 1, cs]
                dpm = dzp * gate
                dp_ref[rows, COL_PG + BLOCK * g:COL_PG + BLOCK * (g + 1)] = (dzp * (mixed * sc) * dgate).astype(BF16)
                pack_ref[ROW_SC + g:ROW_SC + g + 1, :] += jnp.sum(dpm * mixed, axis=0, keepdims=True)
                dmixed = (dpm * sc).astype(BF16)
                pack_ref[ROW_PW + BLOCK * g:ROW_PW + BLOCK * (g + 1), :] += _tn(pooled.astype(BF16), dmixed)
                dpooled = _nt(dmixed, pw_g)
                e = dpooled * inv
                lead = _window_sum(jnp.concatenate([e, ce[:WINDOW_HALO, cs]], axis=0), w, forward=True)[:BLOCK]
                dp_ref[rows, COL_U + BLOCK * g:COL_U + BLOCK * (g + 1)] = (lead - dpooled).astype(BF16)
                ce[:, cs] = e

            kx = _kv_ext(k_ref, n)
            vx = _kv_ext(v_ref, n)
            variant = jnp.minimum(n, 1) if sb == 0 else 1
            dsink_row = jnp.zeros((1, 128), F32)
            tks, tvs = [], []
            for kv in range(2):
                cs = slice(256 * kv, 256 * (kv + 1))
                k_rep = _replicate_head(kx, kv)
                v_rep = _replicate_head(vx, kv)
                q_st = _stack_heads(q_ref[rows, cs])
                gate, dgate = _silu_parts(ag_ref[rows, cs])
                dza = dzs[rows, D_POOL + 256 * kv:D_POOL + 256 * (kv + 1)]
                dp_ref[rows, COL_AG + 256 * kv:COL_AG + 256 * (kv + 1)] = (dza * a_ref[rows, cs] * dgate).astype(BF16)
                da_st = _stack_heads((dza * gate).astype(BF16))
                p, psink = _probs_keys_major(k_rep, q_st, bias_ref[variant, kv], _sink_row(sink_ref, layer, kv),
                                             current)
                dpt = _pack_band(_nt(v_rep, da_st), current)
                delta = jnp.sum(p * dpt, axis=0, keepdims=True)
                dst = _unpack_band((p * (dpt - delta) * SCALE).astype(BF16), current)
                sink_terms = psink * delta
                for g in range(GQA):
                    dsink = -jnp.sum(sink_terms[:, BLOCK * g:BLOCK * (g + 1)], axis=1, keepdims=True)
                    dsink_row = dsink_row + jnp.where(lane == kv * GQA + g, dsink, 0.0)
                dp_ref[rows, COL_Q + 256 * kv:COL_Q + 256 * (kv + 1)] = _unstack_heads(_tn(dst, k_rep)).astype(BF16)
                tks.append(_fold_heads(_nn(dst, q_st)))
                tvs.append(_fold_heads(_nn(_unpack_band(p.astype(BF16), current), da_st)))
            pack_ref[ROW_SINK:ROW_SINK + 1, :] += dsink_row
            dkx = jnp.where(lane2 < 64, tks[0], tks[1])
            dvx = jnp.where(lane2 < 64, tvs[0], tvs[1])
            dp_ref[rows, COL_K:COL_V] = (ck[...] + dkx[BLOCK:]).astype(BF16)
            dp_ref[rows, COL_V:COL_AG] = (cv[...] + dvx[BLOCK:]).astype(BF16)
            ck[...] = dkx[:BLOCK]
            cv[...] = dvx[:BLOCK]

        @pl.when(i == steps - 1)
        def _():
            dw_ref[...] = acc[...].astype(BF16)
            _rows_of(dg, pack_ref, ROW_NPOST)
            pack_ref[ROW_LOSS:ROW_LOSS + 1, :] = jnp.where(lane == 0, lacc[...], 0.0)

    row = lambda c: pl.BlockSpec((tm, c), lambda i: (last - i, 0))
    const = lambda shape: pl.BlockSpec(shape, lambda i: (0,) * len(shape))
    act = jax.ShapeDtypeStruct((SEQ, D_MODEL), F32)
    return pl.pallas_call(
        body, name=f"bwd_back{layer}", grid=(steps,),
        in_specs=[pl.BlockSpec(memory_space=pltpu.SMEM), row(D_MODEL), row(D_MODEL) if top else const((8, 128)),
                  row(D_MODEL), row(D_MODEL), const((DEPTH, D_MODEL)), _resident((D_MODEL, D_MODEL)),
                  row(D_POOL), pl.BlockSpec((BLOCK, D_POOL), lambda i: (jnp.maximum(per * (last - i) - 1, 0), 0)),
                  row(D_POOL), row(D_ATTN), _resident((SEQ, D_KV)), _resident((SEQ, D_KV)), row(D_ATTN), row(D_ATTN),
                  pl.BlockSpec((None, 4, BLOCK, BLOCK), lambda i: (layer, 0, 0, 0)), const((DEPTH, D_POOL)),
                  _resident((2, 2, BLOCK, GQA * BLOCK))],
        out_specs=([row(D_MODEL)] * (1 if top else 0)
                   + [row(D_IN), const((D_MODEL, D_MODEL)), const((PACK_ROWS, 128))]),
        out_shape=([act] * (1 if top else 0)
                   + [jax.ShapeDtypeStruct((SEQ, D_IN), BF16), jax.ShapeDtypeStruct((D_MODEL, D_MODEL), BF16),
                      jax.ShapeDtypeStruct((PACK_ROWS, 128), F32)]),
        scratch_shapes=[pltpu.VMEM((D_MODEL, D_MODEL), F32), pltpu.VMEM((1, D_MODEL), F32), pltpu.VMEM((1, 1), F32),
                        pltpu.VMEM((tm, D_MODEL), F32), pltpu.VMEM((BLOCK, D_KV), F32), pltpu.VMEM((BLOCK, D_KV), F32),
                        pltpu.VMEM((BLOCK, D_POOL), F32)],
        compiler_params=_compiler_params(("arbitrary",)),
    )(sinks, dxo_or_xf, target_or_token, y, z, norm_post, w_out, u, u, pg, q, k, v, ag, a, pool_w, pool_scale, bias)


def _bwd_in(layer, part, token, dproj, x, norm_pre, dxo=None, w_in_t=None):
    pair = part in ("dw_pair", "both_pair")
    want_dw, want_dx = part != "dx", part in ("both", "dx", "both_pair")
    tm = TOKEN_TILE
    steps = SEQ // tm
    cw = 256

    def body(*refs):
        refs = list(refs)
        dp_ref, x_ref, g_ref = refs[1:4]
        del refs[:4]
        if want_dx:
            dxo_ref, w_ref, dx_ref, dgo_ref = refs[:4]
            del refs[:4]
            dg = refs.pop()
        if pair:
            hs_ref, hm_ref, acc, mine_buf, theirs_buf, send_sem, recv_sem = refs
        elif want_dw:
            dw_ref, acc = refs
        i = pl.program_id(0)

        @pl.when(i == 0)
        def _():
            if pair:
                _handshake([(lax.axis_index("x"), lax.axis_index("y"), 1 - lax.axis_index("c"))])
            if want_dw:
                acc[...] = jnp.zeros_like(acc)
            if want_dx:
                dg[...] = jnp.zeros_like(dg)

        xv = x_ref[...]
        gv = g_ref[layer:layer + 1, :]
        r = lax.rsqrt(jnp.mean(xv * xv, axis=-1, keepdims=True) + EPS)
        xn = xv * r
        if want_dw:
            hb = (xn * gv).astype(BF16)
            for c in range(0, D_IN, cw):
                acc[c:c + cw, :] += _tn(dp_ref[:, c:c + cw], hb)
        def rows_for(q, core):
            return pl.ds(pl.multiple_of((2 * q + core) * IN_SHARD, 8), IN_SHARD)

        def swap(q):
            x, y, c = _mesh_pos()
            return pltpu.make_async_remote_copy(
                src_ref=mine_buf.at[q], dst_ref=theirs_buf.at[q], send_sem=send_sem.at[q], recv_sem=recv_sem.at[q],
                device_id=(x, y, 1 - c), device_id_type=MESH)

        if pair:
            @pl.when(i == steps - 1)
            def _():
                for q in range(4):
                    mine_buf[q] = acc[rows_for(q, 1 - lax.axis_index("c")), :].astype(BF16)
                    swap(q).start()

        if want_dx:
            dh = _nn(dp_ref[...], w_ref[...])
            dg[...] += jnp.sum(dh * xn, axis=0, keepdims=True)
            dhn = dh * gv
            dx_ref[...] = dxo_ref[...] + r * (dhn - xn * jnp.mean(dhn * xn, axis=-1, keepdims=True))

        @pl.when(i == steps - 1)
        def _():
            if pair:
                x, y, c = _mesh_pos()
                for q in range(4):
                    swap(q).wait()
                for j, q in enumerate([2 * (1 - x) + y, 2 * x + (1 - y), 2 * (1 - x) + (1 - y)]):
                    hs_ref[j] = (acc[rows_for(q, c), :] + theirs_buf[q].astype(F32)).astype(BF16)
                hm_ref[...] = acc[rows_for(2 * x + y, c), :] + theirs_buf[2 * x + y].astype(F32)
            elif want_dw:
                dw_ref[...] = acc[...].astype(BF16)
            if want_dx:
                _rows_of(dg, dgo_ref, 0)

    row = lambda c: pl.BlockSpec((tm, c), lambda i: (i, 0))
    const = lambda shape: pl.BlockSpec(shape, lambda i: (0,) * len(shape))
    in_specs = [const((8, 128)), row(D_IN), row(D_MODEL), const((DEPTH, D_MODEL))]
    operands = [token, dproj, x, norm_pre]
    out_specs, out_shape, scratch = [], [], []
    if want_dx:
        in_specs += [row(D_MODEL), _resident((D_IN, D_MODEL))]
        operands += [dxo, w_in_t]
        out_specs += [row(D_MODEL), const((8, 128))]
        out_shape += [jax.ShapeDtypeStruct((SEQ, D_MODEL), F32), jax.ShapeDtypeStruct((8, 128), F32)]
    if pair:
        out_specs += [const((3, IN_SHARD, D_MODEL)), const((IN_SHARD, D_MODEL))]
        out_shape += [jax.ShapeDtypeStruct((3, IN_SHARD, D_MODEL), BF16), jax.ShapeDtypeStruct((IN_SHARD, D_MODEL), F32)]
        scratch += [pltpu.VMEM((D_IN, D_MODEL), F32), pltpu.VMEM((4, IN_SHARD, D_MODEL), BF16),
                    pltpu.VMEM((4, IN_SHARD, D_MODEL), BF16), pltpu.SemaphoreType.DMA((4,)), pltpu.SemaphoreType.DMA((4,))]
    elif want_dw:
        out_specs.append(const((D_IN, D_MODEL)))
        out_shape.append(jax.ShapeDtypeStruct((D_IN, D_MODEL), BF16))
        scratch.append(pltpu.VMEM((D_IN, D_MODEL), F32))
    if want_dx:
        scratch.append(pltpu.VMEM((1, D_MODEL), F32))
    params = pltpu.CompilerParams(dimension_semantics=("arbitrary",), vmem_limit_bytes=VMEM_LIMIT,
                                  collective_id=COLLECTIVE_PAIR_SUM[layer] if pair else None)
    return pl.pallas_call(
        body, name=f"bwd_in_{part}{layer}", grid=(steps,),
        in_specs=in_specs, out_specs=out_specs, out_shape=out_shape, scratch_shapes=scratch,
        compiler_params=params,
    )(*operands)


def _mesh_pos():
    return lax.axis_index("x"), lax.axis_index("y"), lax.axis_index("c")


def _device_rows(ref, m, px, py, pc):
    return ref.at[pl.ds(pl.multiple_of((4 * px + 2 * py + pc) * m, 16 if m % 16 == 0 else 8), m), :]


def _allgather(srcs, out_dtype, name, later=()):
    na, nb = len(srcs), len(later)
    every = list(srcs) + list(later)
    shapes = [(a.shape[-2], a.shape[-1]) for a, _ in every]

    def body(*refs):
        xs, refs = refs[:na + nb], refs[na + nb:]
        outs, cast, land, refs = refs[:na], refs[na:na + nb], refs[na + nb:na + 2 * nb], refs[na + 2 * nb:]
        stage, raw, (send_sems, recv_sems, local_sems, load_sems) = refs[:na], refs[na:2 * na + nb], refs[2 * na + nb:]
        loads = [pltpu.make_async_copy(xs[i].at[every[i][1]], raw[i], load_sems.at[i]) for i in range(na + nb)]
        for cp in loads:
            cp.start()
        x, y, c = _mesh_pos()
        me, sibling = (x, y, c), (x, y, 1 - c)
        near = [(1 - x, y), (x, 1 - y)]
        far = (1 - x, 1 - y)
        relay_from, relay_to = (x ^ (1 - c), y ^ c), (x ^ c, y ^ (1 - c))
        _handshake([sibling] + [(*chip, c) for chip in near])
        k_from, k_to = 1 + c, 2 - c

        def slot(a, px, py, pc):
            return _device_rows(outs[a], shapes[a][0], px, py, pc)

        def copy(a, k, block, to, src=None):
            return pltpu.make_async_remote_copy(
                src_ref=slot(a, *block) if src is None else src, dst_ref=slot(a, *block),
                send_sem=send_sems.at[a, k], recv_sem=recv_sems.at[a, k], device_id=to, device_id_type=MESH)

        def cast_block(i):
            loads[i].wait()
            return raw[i][...].astype(out_dtype)

        for a in range(na):
            stage[a][...] = cast_block(a)
        mine = [pltpu.make_async_copy(stage[a], slot(a, *me), local_sems.at[a]) for a in range(na)]
        for cp in mine:
            cp.start()
        sent = []
        for a in range(na):
            sent.append(copy(a, 0, me, sibling, src=stage[a]))
            sent += [copy(a, 1 + j, me, (*chip, c), src=stage[a]) for j, chip in enumerate(near)]
        for cp in sent:
            cp.start()
        for b in range(nb):
            cast[b][...] = cast_block(na + b)
            cp = pltpu.make_async_copy(cast[b], _device_rows(land[b], shapes[na + b][0], *me), local_sems.at[na + b])
            cp.start()
            mine.append(cp)
        for a in range(na):
            copy(a, k_from, (*relay_from, c), me).wait_recv()
            sent += [copy(a, 3, (*relay_from, c), (*relay_to, c)), copy(a, 3 + k_from, (*relay_from, c), sibling)]
            sent[-2].start()
            sent[-1].start()
        for a in range(na):
            copy(a, k_to, (*relay_to, c), me).wait_recv()
            sent.append(copy(a, 3 + k_to, (*relay_to, c), sibling))
            sent[-1].start()
        for a in range(na):
            copy(a, 3, (*far, c), me).wait_recv()
            sent.append(copy(a, 6, (*far, c), sibling))
            sent[-1].start()
        for a in range(na):
            copy(a, 0, sibling, me).wait_recv()
            for j, chip in enumerate(near + [far]):
                copy(a, 4 + j, (*chip, 1 - c), me).wait_recv()
        for cp in sent:
            cp.wait_send()
        for cp in mine:
            cp.wait()

    vmem = pl.BlockSpec(memory_space=pltpu.VMEM)
    hbm = pl.BlockSpec(memory_space=pl.ANY)
    gathered = [jax.ShapeDtypeStruct((N_DEV * m, n), out_dtype) for m, n in shapes]
    out = pl.pallas_call(
        body, name=name,
        in_specs=[hbm] * (na + nb),
        out_specs=[hbm] * na + [vmem] * nb + [hbm] * nb,
        out_shape=gathered[:na] + [jax.ShapeDtypeStruct(s, out_dtype) for s in shapes[na:]] + gathered[na:],
        scratch_shapes=([pltpu.VMEM(s, out_dtype) for s in shapes[:na]]
                        + [pltpu.VMEM(s, a.dtype) for s, (a, _) in zip(shapes, every)]
                        + [pltpu.SemaphoreType.DMA((na, 7)), pltpu.SemaphoreType.DMA((na, 7)),
                           pltpu.SemaphoreType.DMA((na + nb,)), pltpu.SemaphoreType.DMA((na + nb,))]),
        compiler_params=pltpu.CompilerParams(vmem_limit_bytes=VMEM_LIMIT, collective_id=COLLECTIVE_GATHER_W0),
    )(*[a for a, _ in every])
    return out[:na], out[na:na + nb], out[na + nb:]


ALL_PEERS = tuple(range(1, N_DEV))
SIBLING_AND_SAME_CORE = (1, 2, 4, 6)


def _related(k, x, y, c):
    return x ^ ((k >> 2) & 1), y ^ ((k >> 1) & 1), c ^ (k & 1)


def _gather_start(blocks, lands, relations, collective_id, name):
    na = len(blocks)

    def body(*refs):
        src, land, sems, token = refs[:na], refs[na:2 * na], refs[2 * na:4 * na], refs[-1]
        x, y, c = _mesh_pos()
        _handshake([_related(k, x, y, c) for k in sorted(set().union(*relations))])
        for a in range(na):
            for k in relations[a]:
                pltpu.make_async_remote_copy(
                    src_ref=src[a], dst_ref=_device_rows(land[a], blocks[a].shape[0], x, y, c),
                    send_sem=sems[2 * a].at[k - 1], recv_sem=sems[2 * a + 1].at[k - 1],
                    device_id=_related(k, x, y, c), device_id_type=MESH).start()
        token[...] = jnp.zeros_like(token)

    bufs = [pltpu.HBM(t.shape, t.dtype) for t in list(blocks) + list(lands)]
    out = pl.pallas_call(
        body, name=name,
        out_shape=(*([pltpu.SemaphoreType.DMA((N_DEV - 1,))] * (2 * na)), *bufs, jax.ShapeDtypeStruct((8, 128), F32)),
        in_specs=[_HBM] * (2 * na),
        out_specs=(*([_SEM] * (2 * na)), *([_HBM] * (2 * na)), pl.BlockSpec(memory_space=pltpu.VMEM)),
        input_output_aliases={i: 2 * na + i for i in range(2 * na)},
        compiler_params=pltpu.CompilerParams(has_side_effects=_EFFECT, collective_id=collective_id),
    )(*[pltpu.with_memory_space_constraint(t, pltpu.HBM) for t in list(blocks) + list(lands)])
    sems = [(out[2 * a], out[2 * a + 1]) for a in range(na)]
    return sems, out[2 * na:3 * na], out[3 * na:4 * na], out[-1]


def _gather_wait(sems, block, land, relations, after, name):
    def body(src, land_ref, send_sem, recv_sem, after_ref, src_out, land_out):
        x, y, c = _mesh_pos()
        for k in relations:
            peer = _related(k, x, y, c)
            cp = pltpu.make_async_remote_copy(
                src_ref=src, dst_ref=_device_rows(land_ref, block.shape[0], *peer),
                send_sem=send_sem.at[k - 1], recv_sem=recv_sem.at[k - 1], device_id=peer, device_id_type=MESH)
            cp.wait_send()
            cp.wait_recv()

    out = pl.pallas_call(
        body, name=name,
        out_shape=(pltpu.HBM(block.shape, block.dtype), pltpu.HBM(land.shape, land.dtype)),
        in_specs=[_HBM, _HBM, _SEM, _SEM, pl.BlockSpec(memory_space=pl.ANY)],
        out_specs=[_HBM, _HBM],
        input_output_aliases={0: 0, 1: 1},
        compiler_params=pltpu.CompilerParams(has_side_effects=_EFFECT),
    )(block, land, sems[0], sems[1], after)
    return out[1]


(COLLECTIVE_GATHER_W0, COLLECTIVE_GATHER_W1, COLLECTIVE_FORWARD_W_IN1, COLLECTIVE_EXCHANGE_1, COLLECTIVE_EXCHANGE_0A,
 COLLECTIVE_EXCHANGE_0B, COLLECTIVE_GATHER_SMALL) = range(1, 8)
COLLECTIVE_PAIR_SUM = (8, 9)


def _handshake(peers):
    barrier = pltpu.get_barrier_semaphore()
    for peer in peers:
        pl.semaphore_signal(barrier, inc=1, device_id=peer, device_id_type=MESH)
    pl.semaphore_wait(barrier, len(peers))


def _forward_plan(land_ref, m):
    x, y, c = _mesh_pos()
    return [_device_rows(land_ref, m, qx, qy, c) for qx, qy in ((1 - x, y), (x, 1 - y), (1 - x, 1 - y))], (x, y, 1 - c)


def _forward_start(land, m, name):
    def body(land_ref, send_sem, recv_sem, land_out, token):
        _handshake([(lax.axis_index("x"), lax.axis_index("y"), 1 - lax.axis_index("c"))])
        rows, sibling = _forward_plan(land_ref, m)
        for j, r in enumerate(rows):
            pltpu.make_async_remote_copy(src_ref=r, dst_ref=r, send_sem=send_sem.at[j], recv_sem=recv_sem.at[j],
                                         device_id=sibling, device_id_type=MESH).start()
        token[...] = jnp.zeros_like(token)

    out = pl.pallas_call(
        body, name=name,
        out_shape=(pltpu.SemaphoreType.DMA((3,)), pltpu.SemaphoreType.DMA((3,)), pltpu.HBM(land.shape, land.dtype),
                   jax.ShapeDtypeStruct((8, 128), F32)),
        in_specs=[_HBM],
        out_specs=(_SEM, _SEM, _HBM, pl.BlockSpec(memory_space=pltpu.VMEM)),
        input_output_aliases={0: 2},
        compiler_params=pltpu.CompilerParams(has_side_effects=_EFFECT, collective_id=COLLECTIVE_FORWARD_W_IN1),
    )(pltpu.with_memory_space_constraint(land, pltpu.HBM))
    return (out[0], out[1]), out[2], out[3]


def _forward_wait(sems, land, m, after, name):
    def body(land_ref, send_sem, recv_sem, after_ref, land_out):
        x, y, c = _mesh_pos()
        mine, sibling = _forward_plan(land_ref, m)
        theirs = [_device_rows(land_ref, m, qx, qy, 1 - c) for qx, qy in ((1 - x, y), (x, 1 - y), (1 - x, 1 - y))]
        for j in range(3):
            cp = pltpu.make_async_remote_copy(src_ref=mine[j], dst_ref=theirs[j], send_sem=send_sem.at[j],
                                              recv_sem=recv_sem.at[j], device_id=sibling, device_id_type=MESH)
            cp.wait_send()
            cp.wait_recv()

    return pl.pallas_call(
        body, name=name,
        out_shape=pltpu.HBM(land.shape, land.dtype),
        in_specs=[_HBM, _SEM, _SEM, pl.BlockSpec(memory_space=pl.ANY)],
        out_specs=_HBM,
        input_output_aliases={0: 0},
        compiler_params=pltpu.CompilerParams(has_side_effects=_EFFECT),
    )(land, sems[0], sems[1], after)


_HBM = pl.BlockSpec(memory_space=pltpu.HBM)
_SEM = pl.BlockSpec(memory_space=pltpu.SEMAPHORE)
_EFFECT = pltpu.SideEffectType.DATAFLOW_SIDE_EFFECTING


def _exchange_plan(direct):
    x, y, c = _mesh_pos()
    if not direct:
        return [(j, j, (qx, qy, c)) for j, (qx, qy) in enumerate([(1 - x, y), (x, 1 - y), (1 - x, 1 - y)])]
    plan = []
    for k in range(1, N_DEV):
        px, py, pc = x ^ ((k >> 2) & 1), y ^ ((k >> 1) & 1), c ^ (k & 1)
        plan.append((4 * px + 2 * py + pc, k - 1, (px, py, pc)))
    return plan


def _exchange_copies(directs):
    copies, base = [], 0
    for a, direct in enumerate(directs):
        plan = _exchange_plan(direct)
        copies += [(a, block, slot, peer, base + slot) for block, slot, peer in plan]
        base += len(plan)
    return copies, base


def _exchange_start(srcs, directs, collective_id, name):
    na = len(srcs)
    slots = [N_DEV - 1 if direct else 3 for direct in directs]

    def body(*refs):
        src, land = refs[:na], refs[na:2 * na]
        send_sem, recv_sem = refs[2 * na], refs[2 * na + 1]
        token = refs[-1]
        _handshake([peer for _, _, peer in _exchange_plan(any(directs))])
        for a, block, slot, peer, sem in _exchange_copies(directs)[0]:
            pltpu.make_async_remote_copy(
                src_ref=src[a].at[block], dst_ref=land[a].at[slot], send_sem=send_sem.at[sem],
                recv_sem=recv_sem.at[sem], device_id=peer, device_id_type=MESH).start()
        token[...] = jnp.zeros_like(token)

    zones = [jax.ShapeDtypeStruct((n,) + t.shape[1:], t.dtype) for n, t in zip(slots, srcs)]
    bufs = [pltpu.HBM(t.shape, t.dtype) for t in list(srcs) + zones]
    out = pl.pallas_call(
        body, name=name,
        out_shape=(pltpu.SemaphoreType.DMA((sum(slots),)), pltpu.SemaphoreType.DMA((sum(slots),)), *bufs,
                   jax.ShapeDtypeStruct((8, 128), F32)),
        in_specs=[_HBM] * (2 * na),
        out_specs=(_SEM, _SEM, *([_HBM] * (2 * na)), pl.BlockSpec(memory_space=pltpu.VMEM)),
        input_output_aliases={i: 2 + i for i in range(2 * na)},
        compiler_params=pltpu.CompilerParams(has_side_effects=_EFFECT, collective_id=collective_id),
    )(*[pltpu.with_memory_space_constraint(t, pltpu.HBM) for t in srcs],
      *[pltpu.with_memory_space_constraint(lax.empty(t.shape, t.dtype), pltpu.HBM) for t in zones])
    return out[0], out[1], out[2:2 + na], out[2 + na:2 + 2 * na], out[-1]


def _exchange_wait(send_sem, recv_sem, srcs, lands, directs, after, name):
    na = len(srcs)

    def body(*refs):
        src, land = refs[:na], refs[na:2 * na]
        send_sem_ref, recv_sem_ref = refs[2 * na], refs[2 * na + 1]
        for a, block, slot, peer, sem in _exchange_copies(directs)[0]:
            cp = pltpu.make_async_remote_copy(
                src_ref=src[a].at[block], dst_ref=land[a].at[slot], send_sem=send_sem_ref.at[sem],
                recv_sem=recv_sem_ref.at[sem], device_id=peer, device_id_type=MESH)
            cp.wait_send()
            cp.wait_recv()

    bufs = [pltpu.HBM(t.shape, t.dtype) for t in list(srcs) + list(lands)]
    out = pl.pallas_call(
        body, name=name,
        out_shape=tuple(bufs),
        in_specs=[_HBM] * (2 * na) + [_SEM, _SEM, pl.BlockSpec(memory_space=pl.ANY)],
        out_specs=[_HBM] * (2 * na),
        input_output_aliases={i: i for i in range(2 * na)},
        compiler_params=pltpu.CompilerParams(has_side_effects=_EFFECT),
    )(*srcs, *lands, send_sem, recv_sem, after)
    return out[:na], out[na:]


def _own_then_slots(mine_ref, lands_ref, rows=slice(None)):
    if len(mine_ref.shape) == 3:
        x, y, c = _mesh_pos()
        total = mine_ref[4 * x + 2 * y + c, rows, :].astype(F32)
    else:
        total = mine_ref[rows, :].astype(F32)
    for j in range(lands_ref.shape[0]):
        total = total + lands_ref[j, rows, :].astype(F32)
    return total


SMALL_ROWS = 2 * PACK_SLICE + 2 * 8


def _small_block(mine, lands, dgpre, name):
    def body(*refs):
        hm, ld, dg = refs[:DEPTH], refs[DEPTH:2 * DEPTH], refs[2 * DEPTH:3 * DEPTH]
        blk, land, sem = refs[3 * DEPTH:]
        for l in range(DEPTH):
            blk[PACK_SLICE * l:PACK_SLICE * (l + 1), :] = _own_then_slots(hm[l], ld[l])
            blk[2 * PACK_SLICE + 8 * l:2 * PACK_SLICE + 8 * (l + 1), :] = dg[l][...]
        cp = pltpu.make_async_copy(blk, _device_rows(land, SMALL_ROWS, *_mesh_pos()), sem)
        cp.start()
        cp.wait()

    vmem = pl.BlockSpec(memory_space=pltpu.VMEM)
    return pl.pallas_call(
        body, name=name,
        in_specs=[vmem] * (3 * DEPTH), out_specs=[vmem, pl.BlockSpec(memory_space=pl.ANY)],
        out_shape=[jax.ShapeDtypeStruct((SMALL_ROWS, 128), F32), jax.ShapeDtypeStruct((N_DEV * SMALL_ROWS, 128), F32)],
        scratch_shapes=[pltpu.SemaphoreType.DMA],
        compiler_params=_compiler_params(),
    )(*mine, *lands, *dgpre)


def _adamw_math(w, g, m, v):
    m = ADAM_B1 * m + (1.0 - ADAM_B1) * g
    v = ADAM_B2 * v + (1.0 - ADAM_B2) * (g * g)
    m_hat = m / (1.0 - ADAM_B1 ** ADAM_STEP)
    v_hat = v / (1.0 - ADAM_B2 ** ADAM_STEP)
    delta = -ADAM_LR * (m_hat / (jnp.sqrt(v_hat) + ADAM_EPS) + ADAM_WD * w)
    return delta, m, v


def _adamw_layer(layer, mine, lands, w, m, v, earlier, token, name, rows):
    _, mm, nn = w.shape

    def body(hm_ref, ld_ref, w_ref, m_ref, v_ref, _, *refs):
        g_ref, d_ref, nm_ref, nv_ref = refs[-4:]
        g = _own_then_slots(hm_ref, ld_ref)
        g_ref[...] = g
        d, nm, nv = _adamw_math(w_ref[...], g, m_ref[...], v_ref[...])
        d_ref[...] = d
        nm_ref[...] = nm
        nv_ref[...] = nv

    spec = pl.BlockSpec((None, rows, nn), lambda i: (layer, i, 0))
    carried = [] if earlier is None else list(earlier)
    return pl.pallas_call(
        body, name=name, grid=(mm // rows,),
        in_specs=([pl.BlockSpec((rows, nn), lambda i: (i, 0)) if mine.ndim == 2
                   else pl.BlockSpec((N_DEV, rows, nn), lambda i: (0, i, 0)),
                   pl.BlockSpec((lands.shape[0], rows, nn), lambda i: (0, i, 0)),
                   spec, spec, spec] + [pl.BlockSpec(memory_space=pl.ANY)] * (1 + len(carried))),
        out_specs=[spec] * 4,
        out_shape=[jax.ShapeDtypeStruct(w.shape, F32)] * 4,
        input_output_aliases={6 + t: t for t in range(len(carried))},
        compiler_params=_compiler_params(("arbitrary",)),
    )(mine, lands, w, m, v, token, *carried)


def _adamw_small(gathered, params):
    def body(all_ref, *refs):
        ins, outs, packs = refs[:15], refs[15:15 + 21], refs[15 + 21]
        loss_ref = outs[0]
        for dev in range(N_DEV):
            for l in range(DEPTH):
                packs[l, PACK_SLICE * dev:PACK_SLICE * (dev + 1), :] = (
                    all_ref[SMALL_ROWS * dev + PACK_SLICE * l:SMALL_ROWS * dev + PACK_SLICE * (l + 1), :])
        loss_ref[...] = packs[DEPTH - 1, ROW_LOSS:ROW_LOSS + 1, 0:1]

        def update(p, sel, g):
            w_ref, m_ref, v_ref = ins[p], ins[5 + p], ins[10 + p]
            d, nm, nv = _adamw_math(w_ref[sel], g, m_ref[sel], v_ref[sel])
            for t, val in enumerate((g, d, nm, nv)):
                outs[1 + 5 * t + p][sel] = val

        for l in range(DEPTH):
            gp = packs.at[l]
            row0 = 2 * PACK_SLICE + 8 * l
            dgpre = all_ref[row0:row0 + 8, :]
            for dev in range(1, N_DEV):
                dgpre = dgpre + all_ref[SMALL_ROWS * dev + row0:SMALL_ROWS * dev + row0 + 8, :]
            for grp in range(4):
                update(0, (l, grp), gp[ROW_PW + BLOCK * grp:ROW_PW + BLOCK * (grp + 1), :])
                update(1, (slice(l, l + 1), slice(128 * grp, 128 * (grp + 1))), gp[ROW_SC + grp:ROW_SC + grp + 1, :])
            update(2, (slice(l, l + 1), slice(None)), gp[ROW_SINK:ROW_SINK + 1, 0:N_HEADS])
            for r in range(D_MODEL // 128):
                sel = (slice(l, l + 1), slice(128 * r, 128 * (r + 1)))
                update(3, sel, dgpre[r:r + 1, :])
                update(4, sel, gp[ROW_NPOST + r:ROW_NPOST + r + 1, :])

    shapes = [jax.ShapeDtypeStruct(p.shape, F32) for p in params[:5]]
    return pl.pallas_call(
        body, name="adamw_small",
        out_shape=[jax.ShapeDtypeStruct((1, 1), F32)] + shapes * 4,
        scratch_shapes=[pltpu.VMEM((DEPTH, PACK_ROWS, 128), F32)],
        compiler_params=_compiler_params(),
    )(gathered, *params)


def kernel(x, w_in, pool_w, pool_scale, attn_sinks, w_out, norm_pre, norm_post, loss_target, m_w_in, m_pool_w, m_pool_scale, m_attn_sinks, m_w_out, m_norm_pre, m_norm_post, v_w_in, v_pool_w, v_pool_scale, v_attn_sinks, v_w_out, v_norm_pre, v_norm_post):
    x0 = x.reshape(SEQ, D_MODEL)
    target = loss_target.reshape(SEQ, D_MODEL)
    bias = jnp.asarray(_attn_bias())
    w_in_t, m_in_t, v_in_t = (jnp.swapaxes(t, 1, 2) for t in (w_in, m_w_in, v_w_in))

    (win0, wout0), later, lands = _allgather([(w_in_t, 0), (w_out, 0)], BF16, "gather_w0",
                                              later=[(w_in_t, 1), (w_out, 1)])
    sems, later, lands, token = _gather_start(later, lands, [SIBLING_AND_SAME_CORE, ALL_PEERS], COLLECTIVE_GATHER_W1,
                                              "gather_w1_start")
    win_full, wout_full = [win0, None], [wout0, None]

    saved = []
    xl = x0
    for layer in range(DEPTH):
        u, pg, q, k, v, ag, z, a = _fwd_front(layer, xl, norm_pre, win_full[layer], token, attn_sinks,
                                             pool_w, pool_scale, bias)
        if layer == 0:
            land = _gather_wait(sems[0], later[0], lands[0], SIBLING_AND_SAME_CORE, z, "gather_w_in1_wait")
            fsems, land, token = _forward_start(land, IN_SHARD, "forward_w_in1_start")
        else:
            wout_full[layer] = _gather_wait(sems[1], later[1], lands[1], ALL_PEERS, z, "gather_w_out1_wait")
        x_next, y = _fwd_out(layer, z, xl, norm_post, wout_full[layer], token)
        if layer == 0:
            win_full[1] = _forward_wait(fsems, land, IN_SHARD, x_next, "forward_w_in1_wait")
        saved.append((xl, u, pg, q, k, v, ag, z, a, y))
        xl = x_next

    params_small = [pool_w, pool_scale, attn_sinks, norm_pre, norm_post,
                    m_pool_w, m_pool_scale, m_attn_sinks, m_norm_pre, m_norm_post,
                    v_pool_w, v_pool_scale, v_attn_sinks, v_norm_pre, v_norm_post]

    def start(srcs, directs, paired, collective_id, tag):
        send_sem, recv_sem, srcs, lands, started = _exchange_start(srcs, directs, collective_id, f"exchange_start{tag}")
        return (send_sem, recv_sem, srcs, lands, paired, directs), started

    def finish(handle, after, tag):
        send_sem, recv_sem, srcs, lands, paired, directs = handle
        srcs, lands = _exchange_wait(send_sem, recv_sem, srcs, lands, directs, after, f"exchange_wait{tag}")
        return [s if p is None else p for s, p in zip(srcs, paired)], lands

    def back(layer, top, first, second):
        xin, u, pg, q, k, v, ag, z, a, y = saved[layer]
        return _bwd_back(layer, top, first, second, y, z, norm_post, wout_full[layer], attn_sinks, u, pg, q, k, v,
                         ag, a, pool_w, pool_scale, bias)

    dgpre = [None] * DEPTH
    dx, dproj, gw_out, pack = back(1, True, xl, target)
    dx, dgpre[1], chip_sums, own_sum = _bwd_in(1, "both_pair", token, dproj, saved[1][0], norm_pre, dx, win_full[1])
    top, token = start([chip_sums, gw_out.reshape(N_DEV, OUT_SHARD, D_MODEL), pack.reshape(N_DEV, PACK_SLICE, 128)],
                       [False, True, True], [own_sum, None, None], COLLECTIVE_EXCHANGE_1, "1")

    dproj, gw_out, pack = back(0, False, dx, token)
    early, token = start([gw_out.reshape(N_DEV, OUT_SHARD, D_MODEL), pack.reshape(N_DEV, PACK_SLICE, 128)],
                         [True, True], [None, None], COLLECTIVE_EXCHANGE_0A, "0a")
    chip_sums, own_sum = _bwd_in(0, "dw_pair", token, dproj, saved[0][0], norm_pre)
    late, token = start([chip_sums], [False], [own_sum], COLLECTIVE_EXCHANGE_0B, "0b")
    dx, dgpre[0] = _bwd_in(0, "dx", token, dproj, saved[0][0], norm_pre, dx, win_full[0])

    own1, lands1 = finish(top, dx, "1")
    big_in = _adamw_layer(1, own1[0], lands1[0], w_in_t, m_in_t, v_in_t, None, token, "adamw_in1", ADAM_ROWS_IN)
    big_out = _adamw_layer(1, own1[1], lands1[1], w_out, m_w_out, v_w_out, None, token, "adamw_out1", ADAM_ROWS_OUT)
    own0a, lands0a = finish(early, big_out[0], "0a")
    block, land = _small_block([own0a[1], own1[2]], [lands0a[1], lands1[2]], dgpre, "small_block")
    sems, block, land, token = _gather_start([block], [land], [ALL_PEERS], COLLECTIVE_GATHER_SMALL, "gather_small_start")
    big_out = _adamw_layer(0, own0a[0], lands0a[0], w_out, m_w_out, v_w_out, big_out, token, "adamw_out0", ADAM_ROWS_OUT)
    own0b, lands0b = finish(late, big_out[0], "0b")
    big_in = _adamw_layer(0, own0b[0], lands0b[0], w_in_t, m_in_t, v_in_t, big_in, token, "adamw_in0", ADAM_ROWS_IN)
    gathered = _gather_wait(sems[0], block[0], land[0], ALL_PEERS, big_in[0], "gather_small_wait")
    small_out = _adamw_small(gathered, params_small)
    loss = small_out[0].reshape(())

    outs = [loss, dx.reshape(1, SEQ, D_MODEL)]
    for t in range(4):
        pw_, sc_, sk_, npre_, npost_ = small_out[1 + 5 * t:6 + 5 * t]
        outs += [jnp.swapaxes(big_in[t], 1, 2), pw_, sc_, sk_, big_out[t], npre_, npost_]
    return tuple(outs)
```

```python
import numpy as np
import jax
import jax.numpy as jnp
from jax import lax
from jax.experimental import pallas as pl
from jax.experimental.pallas import tpu as pltpu

F32 = jnp.float32
BF16 = jnp.bfloat16

N_DEV = 8
SEQ = 2048
D_MODEL = 1024
D_POOL = 512
D_ATTN = 512
D_KV = 128
D_IN = 2304
N_HEADS = 8
GQA = 4
HEAD_DIM = 64
BLOCK = 128
POOL_WINDOWS = (2, 4, 8, 16)
DEPTH = 2
EPS = 1e-6
NEG_INF = -1e30
SCALE = HEAD_DIM ** -0.5
IN_SHARD = D_IN // N_DEV
OUT_SHARD = D_MODEL // N_DEV

COL_U, COL_PG, COL_Q, COL_K, COL_V, COL_AG = 0, 512, 1024, 1536, 1664, 1792

ADAM_LR = 0.001
ADAM_B1 = 0.9
ADAM_B2 = 0.999
ADAM_EPS = 1e-08
ADAM_WD = 0.01
ADAM_STEP = 10

TOKEN_TILE = 512
FWD_OUT_TILE = 1024
ADAM_ROWS_IN, ADAM_ROWS_OUT = 144, 64
VMEM_LIMIT = 56 * 1024 * 1024
MESH = pl.DeviceIdType.MESH

ROW_PW, ROW_SC, ROW_SINK, ROW_NPOST, ROW_LOSS = 0, 512, 520, 536, 544
PACK_ROWS = 576
PACK_SLICE = PACK_ROWS // N_DEV


def _nn(a, b):
    return jnp.dot(a, b, preferred_element_type=F32)


def _nt(a, b):
    return lax.dot_general(a, b, (((1,), (1,)), ((), ())), preferred_element_type=F32)


def _tn(a, b):
    return lax.dot_general(a, b, (((0,), (0,)), ((), ())), preferred_element_type=F32)


def _silu_parts(g):
    s = jax.nn.sigmoid(g)
    return g * s, s * (1.0 + g * (1.0 - s))


def _resident(shape):
    return pl.BlockSpec(shape, lambda *_: (0,) * len(shape), pipeline_mode=pl.Buffered(1))


def _compiler_params(sem=None):
    if sem is None:
        return pltpu.CompilerParams(vmem_limit_bytes=VMEM_LIMIT)
    return pltpu.CompilerParams(dimension_semantics=sem, vmem_limit_bytes=VMEM_LIMIT)


def _attn_bias():
    t = np.arange(BLOCK)[None, :]
    j = np.arange(BLOCK)[:, None]
    current = j <= t
    dist = np.where(current, t - j, t + BLOCK - j).astype(np.float32)
    out = np.zeros((2, 2, BLOCK, GQA * BLOCK), np.float32)
    for variant in range(2):
        valid = current | (variant == 1)
        for kv in range(2):
            for g in range(GQA):
                slope = np.float32(2.0 ** (-(kv * GQA + g + 1)))
                out[variant, kv, :, g * BLOCK:(g + 1) * BLOCK] = np.where(valid, -slope * dist, np.float32(NEG_INF))
    return out


def _replicate_head(kx, kv):
    rolled = pltpu.roll(kx, 64, 1)
    lane = lax.broadcasted_iota(jnp.int32, kx.shape, 1)
    dup = jnp.where(lane < 64, kx, rolled) if kv == 0 else jnp.where(lane < 64, rolled, kx)
    return jnp.concatenate([dup, dup], axis=1).astype(BF16)


def _stack_heads(qv):
    lane = lax.broadcasted_iota(jnp.int32, qv.shape, 1)
    zero = jnp.zeros_like(qv)
    return jnp.concatenate([jnp.where((lane >= 64 * g) & (lane < 64 * g + 64), qv, zero) for g in range(GQA)], axis=0)


def _unstack_heads(xs):
    lane = lax.broadcasted_iota(jnp.int32, (BLOCK, 256), 1)
    return jnp.where(lane < 64, xs[0:128], jnp.where(lane < 128, xs[128:256], jnp.where(lane < 192, xs[256:384], xs[384:512])))


def _fold_heads(r):
    h = r[:, 0:128] + r[:, 128:256]
    return h + pltpu.roll(h, 64, 1)


def _sink_row(sink_ref, layer, kv):
    lane = lax.broadcasted_iota(jnp.int32, (1, GQA * BLOCK), 1)
    s4 = [sink_ref[layer, kv * GQA + g] for g in range(GQA)]
    return jnp.where(lane < 128, s4[0], jnp.where(lane < 256, s4[1], jnp.where(lane < 384, s4[2], s4[3])))


def _band_is_current():
    j = lax.broadcasted_iota(jnp.int32, (BLOCK, GQA * BLOCK), 0)
    t = lax.broadcasted_iota(jnp.int32, (BLOCK, GQA * BLOCK), 1) & (BLOCK - 1)
    return j <= t


def _pack_band(full, current):
    return jnp.where(current, full[BLOCK:], full[:BLOCK])


def _unpack_band(packed, current):
    zero = jnp.zeros_like(packed)
    return jnp.concatenate([jnp.where(current, zero, packed), jnp.where(current, packed, zero)], axis=0)


def _probs_keys_major(k_rep, q_st, bias, sink, current):
    st = _pack_band(_nt(k_rep, q_st), current) * SCALE + bias
    m = jnp.maximum(jnp.max(st, axis=0, keepdims=True), sink)
    p = jnp.exp(st - m)
    esink = jnp.exp(sink - m)
    rl = 1.0 / (jnp.sum(p, axis=0, keepdims=True) + esink)
    return p * rl, esink * rl


WINDOW_HALO = 16


def _window_sum(ext, w, forward):
    s = ext
    sh = 1
    while sh < w:
        s = s + pltpu.roll(s, (ext.shape[0] - sh) if forward else sh, 0)
        sh *= 2
    return s


def _inv_count(n, w):
    t = n * BLOCK + lax.broadcasted_iota(jnp.int32, (BLOCK, 1), 0) + 1
    return 1.0 / jnp.minimum(t.astype(F32), float(w))


def _kv_ext(ref, n):
    r0 = pl.multiple_of(jnp.maximum(n - 1, 0) * BLOCK, BLOCK)
    r1 = pl.multiple_of(n * BLOCK, BLOCK)
    return jnp.concatenate([ref[pl.ds(r0, BLOCK), :], ref[pl.ds(r1, BLOCK), :]], axis=0)


def _rows_of(vec_ref, pack_ref, row0):
    for r in range(D_MODEL // 128):
        pack_ref[row0 + r:row0 + r + 1, :] = vec_ref[:, 128 * r:128 * (r + 1)]


FRONT_TILE = 4 * BLOCK


def _fwd_front(layer, x, norm_pre, w_in_t, token, sinks, pool_w, pool_scale, bias):
    tm = FRONT_TILE

    def body(sink_ref, x_ref, g_ref, w_ref, _, pw_ref, sc_ref, bias_ref,
             u_ref, pg_ref, q_ref, k_ref, v_ref, ag_ref, z_ref, a_ref, uprev, kprev, vprev):
        i = pl.program_id(0)

        @pl.when(i == 0)
        def _():
            uprev[...] = jnp.zeros_like(uprev)
            kprev[...] = jnp.zeros_like(kprev)
            vprev[...] = jnp.zeros_like(vprev)

        xv = x_ref[...]
        r = lax.rsqrt(jnp.mean(xv * xv, axis=-1, keepdims=True) + EPS)
        h = (xv * r * g_ref[layer:layer + 1, :]).astype(BF16)
        u_ref[...] = _nt(h, w_ref[COL_U:COL_PG, :])
        pg_ref[...] = _nt(h, w_ref[COL_PG:COL_Q, :])
        for sb in range(tm // BLOCK):
            n = (tm // BLOCK) * i + sb
            rows = slice(BLOCK * sb, BLOCK * (sb + 1))
            before = slice(BLOCK * (sb - 1), BLOCK * sb)
            uv = u_ref[rows, :]
            halo = (uprev[BLOCK - WINDOW_HALO:, :] if sb == 0
                    else u_ref[BLOCK * sb - WINDOW_HALO:BLOCK * sb, :])
            ext = jnp.concatenate([halo, uv], axis=0)
            for g, w in enumerate(POOL_WINDOWS):
                cs = slice(BLOCK * g, BLOCK * (g + 1))
                win = _window_sum(ext[:, cs], w, forward=False)[WINDOW_HALO:]
                pooled = win * _inv_count(n, w) - uv[:, cs]
                mixed = _nn(pooled.astype(BF16), pw_ref[g].astype(BF16))
                gate, _ = _silu_parts(pg_ref[rows, cs])
                z_ref[rows, cs] = (mixed * sc_ref[layer:layer + 1, cs] * gate).astype(BF16)

        q_ref[...] = _nt(h, w_ref[COL_Q:COL_K, :]).astype(BF16)
        k_ref[...] = _nt(h, w_ref[COL_K:COL_V, :])
        v_ref[...] = _nt(h, w_ref[COL_V:COL_AG, :])
        ag_ref[...] = _nt(h, w_ref[COL_AG:D_IN, :])

        current = _band_is_current()
        for sb in range(tm // BLOCK):
            n = (tm // BLOCK) * i + sb
            rows = slice(BLOCK * sb, BLOCK * (sb + 1))
            before = slice(BLOCK * (sb - 1), BLOCK * sb)
            kx = jnp.concatenate([kprev[...] if sb == 0 else k_ref[before, :], k_ref[rows, :]], axis=0)
            vx = jnp.concatenate([vprev[...] if sb == 0 else v_ref[before, :], v_ref[rows, :]], axis=0)
            variant = jnp.minimum(n, 1) if sb == 0 else 1
            for kv in range(2):
                cs = slice(256 * kv, 256 * (kv + 1))
                p, _ = _probs_keys_major(_replicate_head(kx, kv), _stack_heads(q_ref[rows, cs]),
                                         bias_ref[variant, kv], _sink_row(sink_ref, layer, kv), current)
                o = _unstack_heads(_tn(_unpack_band(p.astype(BF16), current), _replicate_head(vx, kv)))
                a_ref[rows, cs] = o
                gate, _ = _silu_parts(ag_ref[rows, cs])
                z_ref[rows, D_POOL + 256 * kv:D_POOL + 256 * (kv + 1)] = (o * gate).astype(BF16)

        tail = slice(tm - BLOCK, tm)
        uprev[...] = u_ref[tail, :]
        kprev[...] = k_ref[tail, :]
        vprev[...] = v_ref[tail, :]

    row = lambda c: pl.BlockSpec((tm, c), lambda i: (i, 0))
    const = lambda shape: pl.BlockSpec(shape, lambda i: (0,) * len(shape))
    return pl.pallas_call(
        body, name=f"fwd_front{layer}", grid=(SEQ // tm,),
        in_specs=[pl.BlockSpec(memory_space=pltpu.SMEM), row(D_MODEL), const((DEPTH, D_MODEL)),
                  _resident((D_IN, D_MODEL)), const((8, 128)),
                  pl.BlockSpec((None, 4, BLOCK, BLOCK), lambda i: (layer, 0, 0, 0)), const((DEPTH, D_POOL)),
                  _resident((2, 2, BLOCK, GQA * BLOCK))],
        out_specs=[row(D_POOL), row(D_POOL), row(D_ATTN), row(D_KV), row(D_KV), row(D_ATTN), row(D_MODEL),
                   row(D_ATTN)],
        out_shape=[jax.ShapeDtypeStruct((SEQ, D_POOL), F32), jax.ShapeDtypeStruct((SEQ, D_POOL), F32),
                   jax.ShapeDtypeStruct((SEQ, D_ATTN), BF16), jax.ShapeDtypeStruct((SEQ, D_KV), F32),
                   jax.ShapeDtypeStruct((SEQ, D_KV), F32), jax.ShapeDtypeStruct((SEQ, D_ATTN), F32),
                   jax.ShapeDtypeStruct((SEQ, D_MODEL), BF16), jax.ShapeDtypeStruct((SEQ, D_ATTN), F32)],
        scratch_shapes=[pltpu.VMEM((BLOCK, D_POOL), F32), pltpu.VMEM((BLOCK, D_KV), F32),
                        pltpu.VMEM((BLOCK, D_KV), F32)],
        compiler_params=_compiler_params(("arbitrary",)),
    )(sinks, x, norm_pre, w_in_t, token, pool_w, pool_scale, bias)


def _fwd_out(layer, z, x, norm_post, w_out, token):
    tm = FWD_OUT_TILE

    def body(z_ref, x_ref, g_ref, w_ref, _, xn_ref, y_ref):
        y = _nn(z_ref[...], w_ref[...])
        y_ref[...] = y
        r = lax.rsqrt(jnp.mean(y * y, axis=-1, keepdims=True) + EPS)
        xn_ref[...] = x_ref[...] + y * r * g_ref[layer:layer + 1, :]

    row = lambda c: pl.BlockSpec((tm, c), lambda i: (i, 0))
    return pl.pallas_call(
        body, name=f"fwd_out{layer}", grid=(SEQ // tm,),
        in_specs=[row(D_MODEL), row(D_MODEL), pl.BlockSpec((DEPTH, D_MODEL), lambda i: (0, 0)),
                  _resident((D_MODEL, D_MODEL)), pl.BlockSpec((8, 128), lambda i: (0, 0))],
        out_specs=[row(D_MODEL), row(D_MODEL)],
        out_shape=[jax.ShapeDtypeStruct((SEQ, D_MODEL), F32), jax.ShapeDtypeStruct((SEQ, D_MODEL), F32)],
        compiler_params=_compiler_params(("arbitrary",)),
    )(z, x, norm_post, w_out, token)


BACK_TILE = 2 * BLOCK


def _bwd_back(layer, top, dxo_or_xf, target_or_token, y, z, norm_post, w_out, sinks, u, pg, q, k, v, ag, a,
              pool_w, pool_scale, bias):
    tm = BACK_TILE
    steps = SEQ // tm
    last = steps - 1
    per = tm // BLOCK

    def body(*refs):
        refs = list(refs)
        sink_ref, first, second = refs[:3]
        (y_ref, z_ref, g_ref, w_ref, u_ref, up_ref, pg_ref, q_ref, k_ref, v_ref, ag_ref, a_ref, pw_ref, sc_ref,
         bias_ref) = refs[3:18]
        del refs[:18]
        dxo_ref = refs.pop(0) if top else None
        dp_ref, dw_ref, pack_ref, acc, dg, lacc, dzs, ck, cv, ce = refs
        i = pl.program_id(0)
        blk = last - i

        @pl.when(i == 0)
        def _():
            acc[...] = jnp.zeros_like(acc)
            dg[...] = jnp.zeros_like(dg)
            lacc[...] = jnp.zeros_like(lacc)
            pack_ref[...] = jnp.zeros_like(pack_ref)
            ck[...] = jnp.zeros_like(ck)
            cv[...] = jnp.zeros_like(cv)
            ce[...] = jnp.zeros_like(ce)

        if top:
            d = first[...] - second[...]
            dxo_v = d * (1.0 / D_MODEL)
            dxo_ref[...] = dxo_v
            part = jnp.sum(d * d, axis=-1, keepdims=True) * (1.0 / D_MODEL)
            lacc[...] += 0.5 * jnp.sum(part, axis=0, keepdims=True)
        else:
            dxo_v = first[...]
        yv = y_ref[...]
        r = lax.rsqrt(jnp.mean(yv * yv, axis=-1, keepdims=True) + EPS)
        yn = yv * r
        dg[...] += jnp.sum(dxo_v * yn, axis=0, keepdims=True)
        dyn = dxo_v * g_ref[layer:layer + 1, :]
        dy = (r * (dyn - yn * jnp.mean(dyn * yn, axis=-1, keepdims=True))).astype(BF16)
        dzs[...] = _nt(dy, w_ref[...])
        acc[...] += _tn(z_ref[...], dy)

        lane = lax.broadcasted_iota(jnp.int32, (1, 128), 1)
        lane2 = lax.broadcasted_iota(jnp.int32, (256, 128), 1)
        current = _band_is_current()
        for sb in reversed(range(per)):
            n = per * blk + sb
            rows = slice(BLOCK * sb, BLOCK * (sb + 1))

            uv = u_ref[rows, :]
            if sb == 0:
                halo = up_ref[BLOCK - WINDOW_HALO:, :] * (n > 0).astype(F32)
            else:
                halo = u_ref[BLOCK * sb - WINDOW_HALO:BLOCK * sb, :]
            ext = jnp.concatenate([halo, uv], axis=0)
            for g, w in enumerate(POOL_WINDOWS):
                cs = slice(BLOCK * g, BLOCK * (g + 1))
                inv = _inv_count(n, w)
                win = _window_sum(ext[:, cs], w, forward=False)[WINDOW_HALO:]
                pooled = win * inv - uv[:, cs]
                pw_g = pw_ref[g].astype(BF16)
                mixed = _nn(pooled.astype(BF16), pw_g)
                gate, dgate = _silu_parts(pg_ref[rows, cs])
                dzp = dzs[rows, cs]
                sc = sc_ref[layer:layer + 1, cs]
                dpm = dzp * gate
                dp_ref[rows, COL_PG + BLOCK * g:COL_PG + BLOCK * (g + 1)] = (dzp * (mixed * sc) * dgate).astype(BF16)
                pack_ref[ROW_SC + g:ROW_SC + g + 1, :] += jnp.sum(dpm * mixed, axis=0, keepdims=True)
                dmixed = (dpm * sc).astype(BF16)
                pack_ref[ROW_PW + BLOCK * g:ROW_PW + BLOCK * (g + 1), :] += _tn(pooled.astype(BF16), dmixed)
                dpooled = _nt(dmixed, pw_g)
                e = dpooled * inv
                lead = _window_sum(jnp.concatenate([e, ce[:WINDOW_HALO, cs]], axis=0), w, forward=True)[:BLOCK]
                dp_ref[rows, COL_U + BLOCK * g:COL_U + BLOCK * (g + 1)] = (lead - dpooled).astype(BF16)
                ce[:, cs] = e

            kx = _kv_ext(k_ref, n)
            vx = _kv_ext(v_ref, n)
            variant = jnp.minimum(n, 1) if sb == 0 else 1
            dsink_row = jnp.zeros((1, 128), F32)
            tks, tvs = [], []
            for kv in range(2):
                cs = slice(256 * kv, 256 * (kv + 1))
                k_rep = _replicate_head(kx, kv)
                v_rep = _replicate_head(vx, kv)
                q_st = _stack_heads(q_ref[rows, cs])
                gate, dgate = _silu_parts(ag_ref[rows, cs])
                dza = dzs[rows, D_POOL + 256 * kv:D_POOL + 256 * (kv + 1)]
                dp_ref[rows, COL_AG + 256 * kv:COL_AG + 256 * (kv + 1)] = (dza * a_ref[rows, cs] * dgate).astype(BF16)
                da_st = _stack_heads((dza * gate).astype(BF16))
                p, psink = _probs_keys_major(k_rep, q_st, bias_ref[variant, kv], _sink_row(sink_ref, layer, kv),
                                             current)
                dpt = _pack_band(_nt(v_rep, da_st), current)
                delta = jnp.sum(p * dpt, axis=0, keepdims=True)
                dst = _unpack_band((p * (dpt - delta) * SCALE).astype(BF16), current)
                sink_terms = psink * delta
                for g in range(GQA):
                    dsink = -jnp.sum(sink_terms[:, BLOCK * g:BLOCK * (g + 1)], axis=1, keepdims=True)
                    dsink_row = dsink_row + jnp.where(lane == kv * GQA + g, dsink, 0.0)
                dp_ref[rows, COL_Q + 256 * kv:COL_Q + 256 * (kv + 1)] = _unstack_heads(_tn(dst, k_rep)).astype(BF16)
                tks.append(_fold_heads(_nn(dst, q_st)))
                tvs.append(_fold_heads(_nn(_unpack_band(p.astype(BF16), current), da_st)))
            pack_ref[ROW_SINK:ROW_SINK + 1, :] += dsink_row
            dkx = jnp.where(lane2 < 64, tks[0], tks[1])
            dvx = jnp.where(lane2 < 64, tvs[0], tvs[1])
            dp_ref[rows, COL_K:COL_V] = (ck[...] + dkx[BLOCK:]).astype(BF16)
            dp_ref[rows, COL_V:COL_AG] = (cv[...] + dvx[BLOCK:]).astype(BF16)
            ck[...] = dkx[:BLOCK]
            cv[...] = dvx[:BLOCK]

        @pl.when(i == steps - 1)
        def _():
            dw_ref[...] = acc[...].astype(BF16)
            _rows_of(dg, pack_ref, ROW_NPOST)
            pack_ref[ROW_LOSS:ROW_LOSS + 1, :] = jnp.where(lane == 0, lacc[...], 0.0)

    row = lambda c: pl.BlockSpec((tm, c), lambda i: (last - i, 0))
    const = lambda shape: pl.BlockSpec(shape, lambda i: (0,) * len(shape))
    act = jax.ShapeDtypeStruct((SEQ, D_MODEL), F32)
    return pl.pallas_call(
        body, name=f"bwd_back{layer}", grid=(steps,),
        in_specs=[pl.BlockSpec(memory_space=pltpu.SMEM), row(D_MODEL), row(D_MODEL) if top else const((8, 128)),
                  row(D_MODEL), row(D_MODEL), const((DEPTH, D_MODEL)), _resident((D_MODEL, D_MODEL)),
                  row(D_POOL), pl.BlockSpec((BLOCK, D_POOL), lambda i: (jnp.maximum(per * (last - i) - 1, 0), 0)),
                  row(D_POOL), row(D_ATTN), _resident((SEQ, D_KV)), _resident((SEQ, D_KV)), row(D_ATTN), row(D_ATTN),
                  pl.BlockSpec((None, 4, BLOCK, BLOCK), lambda i: (layer, 0, 0, 0)), const((DEPTH, D_POOL)),
                  _resident((2, 2, BLOCK, GQA * BLOCK))],
        out_specs=([row(D_MODEL)] * (1 if top else 0)
                   + [row(D_IN), const((D_MODEL, D_MODEL)), const((PACK_ROWS, 128))]),
        out_shape=([act] * (1 if top else 0)
                   + [jax.ShapeDtypeStruct((SEQ, D_IN), BF16), jax.ShapeDtypeStruct((D_MODEL, D_MODEL), BF16),
                      jax.ShapeDtypeStruct((PACK_ROWS, 128), F32)]),
        scratch_shapes=[pltpu.VMEM((D_MODEL, D_MODEL), F32), pltpu.VMEM((1, D_MODEL), F32), pltpu.VMEM((1, 1), F32),
                        pltpu.VMEM((tm, D_MODEL), F32), pltpu.VMEM((BLOCK, D_KV), F32), pltpu.VMEM((BLOCK, D_KV), F32),
                        pltpu.VMEM((BLOCK, D_POOL), F32)],
        compiler_params=_compiler_params(("arbitrary",)),
    )(sinks, dxo_or_xf, target_or_token, y, z, norm_post, w_out, u, u, pg, q, k, v, ag, a, pool_w, pool_scale, bias)


def _bwd_in(layer, part, token, dproj, x, norm_pre, dxo=None, w_in_t=None):
    pair = part in ("dw_pair", "both_pair")
    want_dw, want_dx = part != "dx", part in ("both", "dx", "both_pair")
    tm = TOKEN_TILE
    steps = SEQ // tm
    cw = 256

    def body(*refs):
        refs = list(refs)
        dp_ref, x_ref, g_ref = refs[1:4]
        del refs[:4]
        if want_dx:
            dxo_ref, w_ref, dx_ref, dgo_ref = refs[:4]
            del refs[:4]
            dg = refs.pop()
        if pair:
            hs_ref, hm_ref, acc, mine_buf, theirs_buf, send_sem, recv_sem = refs
        elif want_dw:
            dw_ref, acc = refs
        i = pl.program_id(0)

        @pl.when(i == 0)
        def _():
            if pair:
                _handshake([(lax.axis_index("x"), lax.axis_index("y"), 1 - lax.axis_index("c"))])
            if want_dw:
                acc[...] = jnp.zeros_like(acc)
            if want_dx:
                dg[...] = jnp.zeros_like(dg)

        xv = x_ref[...]
        gv = g_ref[layer:layer + 1, :]
        r = lax.rsqrt(jnp.mean(xv * xv, axis=-1, keepdims=True) + EPS)
        xn = xv * r
        if want_dw:
            hb = (xn * gv).astype(BF16)
            for c in range(0, D_IN, cw):
                acc[c:c + cw, :] += _tn(dp_ref[:, c:c + cw], hb)
        def rows_for(q, core):
            return pl.ds(pl.multiple_of((2 * q + core) * IN_SHARD, 8), IN_SHARD)

        def swap(q):
            x, y, c = _mesh_pos()
            return pltpu.make_async_remote_copy(
                src_ref=mine_buf.at[q], dst_ref=theirs_buf.at[q], send_sem=send_sem.at[q], recv_sem=recv_sem.at[q],
                device_id=(x, y, 1 - c), device_id_type=MESH)

        if pair:
            @pl.when(i == steps - 1)
            def _():
                for q in range(4):
                    mine_buf[q] = acc[rows_for(q, 1 - lax.axis_index("c")), :].astype(BF16)
                    swap(q).start()

        if want_dx:
            dh = _nn(dp_ref[...], w_ref[...])
            dg[...] += jnp.sum(dh * xn, axis=0, keepdims=True)
            dhn = dh * gv
            dx_ref[...] = dxo_ref[...] + r * (dhn - xn * jnp.mean(dhn * xn, axis=-1, keepdims=True))

        @pl.when(i == steps - 1)
        def _():
            if pair:
                x, y, c = _mesh_pos()
                for q in range(4):
                    swap(q).wait()
                for j, q in enumerate([2 * (1 - x) + y, 2 * x + (1 - y), 2 * (1 - x) + (1 - y)]):
                    hs_ref[j] = (acc[rows_for(q, c), :] + theirs_buf[q].astype(F32)).astype(BF16)
                hm_ref[...] = acc[rows_for(2 * x + y, c), :] + theirs_buf[2 * x + y].astype(F32)
            elif want_dw:
                dw_ref[...] = acc[...].astype(BF16)
            if want_dx:
                _rows_of(dg, dgo_ref, 0)

    row = lambda c: pl.BlockSpec((tm, c), lambda i: (i, 0))
    const = lambda shape: pl.BlockSpec(shape, lambda i: (0,) * len(shape))
    in_specs = [const((8, 128)), row(D_IN), row(D_MODEL), const((DEPTH, D_MODEL))]
    operands = [token, dproj, x, norm_pre]
    out_specs, out_shape, scratch = [], [], []
    if want_dx:
        in_specs += [row(D_MODEL), _resident((D_IN, D_MODEL))]
        operands += [dxo, w_in_t]
        out_specs += [row(D_MODEL), const((8, 128))]
        out_shape += [jax.ShapeDtypeStruct((SEQ, D_MODEL), F32), jax.ShapeDtypeStruct((8, 128), F32)]
    if pair:
        out_specs += [const((3, IN_SHARD, D_MODEL)), const((IN_SHARD, D_MODEL))]
        out_shape += [jax.ShapeDtypeStruct((3, IN_SHARD, D_MODEL), BF16), jax.ShapeDtypeStruct((IN_SHARD, D_MODEL), F32)]
        scratch += [pltpu.VMEM((D_IN, D_MODEL), F32), pltpu.VMEM((4, IN_SHARD, D_MODEL), BF16),
                    pltpu.VMEM((4, IN_SHARD, D_MODEL), BF16), pltpu.SemaphoreType.DMA((4,)), pltpu.SemaphoreType.DMA((4,))]
    elif want_dw:
        out_specs.append(const((D_IN, D_MODEL)))
        out_shape.append(jax.ShapeDtypeStruct((D_IN, D_MODEL), BF16))
        scratch.append(pltpu.VMEM((D_IN, D_MODEL), F32))
    if want_dx:
        scratch.append(pltpu.VMEM((1, D_MODEL), F32))
    params = pltpu.CompilerParams(dimension_semantics=("arbitrary",), vmem_limit_bytes=VMEM_LIMIT,
                                  collective_id=COLLECTIVE_PAIR_SUM[layer] if pair else None)
    return pl.pallas_call(
        body, name=f"bwd_in_{part}{layer}", grid=(steps,),
        in_specs=in_specs, out_specs=out_specs, out_shape=out_shape, scratch_shapes=scratch,
        compiler_params=params,
    )(*operands)


def _mesh_pos():
    return lax.axis_index("x"), lax.axis_index("y"), lax.axis_index("c")


def _device_rows(ref, m, px, py, pc):
    return ref.at[pl.ds(pl.multiple_of((4 * px + 2 * py + pc) * m, 16 if m % 16 == 0 else 8), m), :]


def _allgather(srcs, out_dtype, name, later=()):
    na, nb = len(srcs), len(later)
    every = list(srcs) + list(later)
    shapes = [(a.shape[-2], a.shape[-1]) for a, _ in every]

    def body(*refs):
        xs, refs = refs[:na + nb], refs[na + nb:]
        outs, cast, land, refs = refs[:na], refs[na:na + nb], refs[na + nb:na + 2 * nb], refs[na + 2 * nb:]
        stage, raw, (send_sems, recv_sems, local_sems, load_sems) = refs[:na], refs[na:2 * na + nb], refs[2 * na + nb:]
        loads = [pltpu.make_async_copy(xs[i].at[every[i][1]], raw[i], load_sems.at[i]) for i in range(na + nb)]
        for cp in loads:
            cp.start()
        x, y, c = _mesh_pos()
        me, sibling = (x, y, c), (x, y, 1 - c)
        near = [(1 - x, y), (x, 1 - y)]
        far = (1 - x, 1 - y)
        relay_from, relay_to = (x ^ (1 - c), y ^ c), (x ^ c, y ^ (1 - c))
        _handshake([sibling] + [(*chip, c) for chip in near])
        k_from, k_to = 1 + c, 2 - c

        def slot(a, px, py, pc):
            return _device_rows(outs[a], shapes[a][0], px, py, pc)

        def copy(a, k, block, to, src=None):
            return pltpu.make_async_remote_copy(
                src_ref=slot(a, *block) if src is None else src, dst_ref=slot(a, *block),
                send_sem=send_sems.at[a, k], recv_sem=recv_sems.at[a, k], device_id=to, device_id_type=MESH)

        def cast_block(i):
            loads[i].wait()
            return raw[i][...].astype(out_dtype)

        for a in range(na):
            stage[a][...] = cast_block(a)
        mine = [pltpu.make_async_copy(stage[a], slot(a, *me), local_sems.at[a]) for a in range(na)]
        for cp in mine:
            cp.start()
        sent = []
        for a in range(na):
            sent.append(copy(a, 0, me, sibling, src=stage[a]))
            sent += [copy(a, 1 + j, me, (*chip, c), src=stage[a]) for j, chip in enumerate(near)]
        for cp in sent:
            cp.start()
        for b in range(nb):
            cast[b][...] = cast_block(na + b)
            cp = pltpu.make_async_copy(cast[b], _device_rows(land[b], shapes[na + b][0], *me), local_sems.at[na + b])
            cp.start()
            mine.append(cp)
        for a in range(na):
            copy(a, k_from, (*relay_from, c), me).wait_recv()
            sent += [copy(a, 3, (*relay_from, c), (*relay_to, c)), copy(a, 3 + k_from, (*relay_from, c), sibling)]
            sent[-2].start()
            sent[-1].start()
        for a in range(na):
            copy(a, k_to, (*relay_to, c), me).wait_recv()
            sent.append(copy(a, 3 + k_to, (*relay_to, c), sibling))
            sent[-1].start()
        for a in range(na):
            copy(a, 3, (*far, c), me).wait_recv()
            sent.append(copy(a, 6, (*far, c), sibling))
            sent[-1].start()
        for a in range(na):
            copy(a, 0, sibling, me).wait_recv()
            for j, chip in enumerate(near + [far]):
                copy(a, 4 + j, (*chip, 1 - c), me).wait_recv()
        for cp in sent:
            cp.wait_send()
        for cp in mine:
            cp.wait()

    vmem = pl.BlockSpec(memory_space=pltpu.VMEM)
    hbm = pl.BlockSpec(memory_space=pl.ANY)
    gathered = [jax.ShapeDtypeStruct((N_DEV * m, n), out_dtype) for m, n in shapes]
    out = pl.pallas_call(
        body, name=name,
        in_specs=[hbm] * (na + nb),
        out_specs=[hbm] * na + [vmem] * nb + [hbm] * nb,
        out_shape=gathered[:na] + [jax.ShapeDtypeStruct(s, out_dtype) for s in shapes[na:]] + gathered[na:],
        scratch_shapes=([pltpu.VMEM(s, out_dtype) for s in shapes[:na]]
                        + [pltpu.VMEM(s, a.dtype) for s, (a, _) in zip(shapes, every)]
                        + [pltpu.SemaphoreType.DMA((na, 7)), pltpu.SemaphoreType.DMA((na, 7)),
                           pltpu.SemaphoreType.DMA((na + nb,)), pltpu.SemaphoreType.DMA((na + nb,))]),
        compiler_params=pltpu.CompilerParams(vmem_limit_bytes=VMEM_LIMIT, collective_id=COLLECTIVE_GATHER_W0),
    )(*[a for a, _ in every])
    return out[:na], out[na:na + nb], out[na + nb:]


ALL_PEERS = tuple(range(1, N_DEV))
SIBLING_AND_SAME_CORE = (1, 2, 4, 6)


def _related(k, x, y, c):
    return x ^ ((k >> 2) & 1), y ^ ((k >> 1) & 1), c ^ (k & 1)


def _gather_start(blocks, lands, relations, collective_id, name):
    na = len(blocks)

    def body(*refs):
        src, land, sems, token = refs[:na], refs[na:2 * na], refs[2 * na:4 * na], refs[-1]
        x, y, c = _mesh_pos()
        _handshake([_related(k, x, y, c) for k in sorted(set().union(*relations))])
        for a in range(na):
            for k in relations[a]:
                pltpu.make_async_remote_copy(
                    src_ref=src[a], dst_ref=_device_rows(land[a], blocks[a].shape[0], x, y, c),
                    send_sem=sems[2 * a].at[k - 1], recv_sem=sems[2 * a + 1].at[k - 1],
                    device_id=_related(k, x, y, c), device_id_type=MESH).start()
        token[...] = jnp.zeros_like(token)

    bufs = [pltpu.HBM(t.shape, t.dtype) for t in list(blocks) + list(lands)]
    out = pl.pallas_call(
        body, name=name,
        out_shape=(*([pltpu.SemaphoreType.DMA((N_DEV - 1,))] * (2 * na)), *bufs, jax.ShapeDtypeStruct((8, 128), F32)),
        in_specs=[_HBM] * (2 * na),
        out_specs=(*([_SEM] * (2 * na)), *([_HBM] * (2 * na)), pl.BlockSpec(memory_space=pltpu.VMEM)),
        input_output_aliases={i: 2 * na + i for i in range(2 * na)},
        compiler_params=pltpu.CompilerParams(has_side_effects=_EFFECT, collective_id=collective_id),
    )(*[pltpu.with_memory_space_constraint(t, pltpu.HBM) for t in list(blocks) + list(lands)])
    sems = [(out[2 * a], out[2 * a + 1]) for a in range(na)]
    return sems, out[2 * na:3 * na], out[3 * na:4 * na], out[-1]


def _gather_wait(sems, block, land, relations, after, name):
    def body(src, land_ref, send_sem, recv_sem, after_ref, src_out, land_out):
        x, y, c = _mesh_pos()
        for k in relations:
            peer = _related(k, x, y, c)
            cp = pltpu.make_async_remote_copy(
                src_ref=src, dst_ref=_device_rows(land_ref, block.shape[0], *peer),
                send_sem=send_sem.at[k - 1], recv_sem=recv_sem.at[k - 1], device_id=peer, device_id_type=MESH)
            cp.wait_send()
            cp.wait_recv()

    out = pl.pallas_call(
        body, name=name,
        out_shape=(pltpu.HBM(block.shape, block.dtype), pltpu.HBM(land.shape, land.dtype)),
        in_specs=[_HBM, _HBM, _SEM, _SEM, pl.BlockSpec(memory_space=pl.ANY)],
        out_specs=[_HBM, _HBM],
        input_output_aliases={0: 0, 1: 1},
        compiler_params=pltpu.CompilerParams(has_side_effects=_EFFECT),
    )(block, land, sems[0], sems[1], after)
    return out[1]


(COLLECTIVE_GATHER_W0, COLLECTIVE_GATHER_W1, COLLECTIVE_FORWARD_W_IN1, COLLECTIVE_EXCHANGE_1, COLLECTIVE_EXCHANGE_0A,
 COLLECTIVE_EXCHANGE_0B, COLLECTIVE_GATHER_SMALL) = range(1, 8)
COLLECTIVE_PAIR_SUM = (8, 9)


def _handshake(peers):
    barrier = pltpu.get_barrier_semaphore()
    for peer in peers:
        pl.semaphore_signal(barrier, inc=1, device_id=peer, device_id_type=MESH)
    pl.semaphore_wait(barrier, len(peers))


def _forward_plan(land_ref, m):
    x, y, c = _mesh_pos()
    return [_device_rows(land_ref, m, qx, qy, c) for qx, qy in ((1 - x, y), (x, 1 - y), (1 - x, 1 - y))], (x, y, 1 - c)


def _forward_start(land, m, name):
    def body(land_ref, send_sem, recv_sem, land_out, token):
        _handshake([(lax.axis_index("x"), lax.axis_index("y"), 1 - lax.axis_index("c"))])
        rows, sibling = _forward_plan(land_ref, m)
        for j, r in enumerate(rows):
            pltpu.make_async_remote_copy(src_ref=r, dst_ref=r, send_sem=send_sem.at[j], recv_sem=recv_sem.at[j],
                                         device_id=sibling, device_id_type=MESH).start()
        token[...] = jnp.zeros_like(token)

    out = pl.pallas_call(
        body, name=name,
        out_shape=(pltpu.SemaphoreType.DMA((3,)), pltpu.SemaphoreType.DMA((3,)), pltpu.HBM(land.shape, land.dtype),
                   jax.ShapeDtypeStruct((8, 128), F32)),
        in_specs=[_HBM],
        out_specs=(_SEM, _SEM, _HBM, pl.BlockSpec(memory_space=pltpu.VMEM)),
        input_output_aliases={0: 2},
        compiler_params=pltpu.CompilerParams(has_side_effects=_EFFECT, collective_id=COLLECTIVE_FORWARD_W_IN1),
    )(pltpu.with_memory_space_constraint(land, pltpu.HBM))
    return (out[0], out[1]), out[2], out[3]


def _forward_wait(sems, land, m, after, name):
    def body(land_ref, send_sem, recv_sem, after_ref, land_out):
        x, y, c = _mesh_pos()
        mine, sibling = _forward_plan(land_ref, m)
        theirs = [_device_rows(land_ref, m, qx, qy, 1 - c) for qx, qy in ((1 - x, y), (x, 1 - y), (1 - x, 1 - y))]
        for j in range(3):
            cp = pltpu.make_async_remote_copy(src_ref=mine[j], dst_ref=theirs[j], send_sem=send_sem.at[j],
                                              recv_sem=recv_sem.at[j], device_id=sibling, device_id_type=MESH)
            cp.wait_send()
            cp.wait_recv()

    return pl.pallas_call(
        body, name=name,
        out_shape=pltpu.HBM(land.shape, land.dtype),
        in_specs=[_HBM, _SEM, _SEM, pl.BlockSpec(memory_space=pl.ANY)],
        out_specs=_HBM,
        input_output_aliases={0: 0},
        compiler_params=pltpu.CompilerParams(has_side_effects=_EFFECT),
    )(land, sems[0], sems[1], after)


_HBM = pl.BlockSpec(memory_space=pltpu.HBM)
_SEM = pl.BlockSpec(memory_space=pltpu.SEMAPHORE)
_EFFECT = pltpu.SideEffectType.DATAFLOW_SIDE_EFFECTING


def _exchange_plan(direct):
    x, y, c = _mesh_pos()
    if not direct:
        return [(j, j, (qx, qy, c)) for j, (qx, qy) in enumerate([(1 - x, y), (x, 1 - y), (1 - x, 1 - y)])]
    plan = []
    for k in range(1, N_DEV):
        px, py, pc = x ^ ((k >> 2) & 1), y ^ ((k >> 1) & 1), c ^ (k & 1)
        plan.append((4 * px + 2 * py + pc, k - 1, (px, py, pc)))
    return plan


def _exchange_copies(directs):
    copies, base = [], 0
    for a, direct in enumerate(directs):
        plan = _exchange_plan(direct)
        copies += [(a, block, slot, peer, base + slot) for block, slot, peer in plan]
        base += len(plan)
    return copies, base


def _exchange_start(srcs, directs, collective_id, name):
    na = len(srcs)
    slots = [N_DEV - 1 if direct else 3 for direct in directs]

    def body(*refs):
        src, land = refs[:na], refs[na:2 * na]
        send_sem, recv_sem = refs[2 * na], refs[2 * na + 1]
        token = refs[-1]
        _handshake([peer for _, _, peer in _exchange_plan(any(directs))])
        for a, block, slot, peer, sem in _exchange_copies(directs)[0]:
            pltpu.make_async_remote_copy(
                src_ref=src[a].at[block], dst_ref=land[a].at[slot], send_sem=send_sem.at[sem],
                recv_sem=recv_sem.at[sem], device_id=peer, device_id_type=MESH).start()
        token[...] = jnp.zeros_like(token)

    zones = [jax.ShapeDtypeStruct((n,) + t.shape[1:], t.dtype) for n, t in zip(slots, srcs)]
    bufs = [pltpu.HBM(t.shape, t.dtype) for t in list(srcs) + zones]
    out = pl.pallas_call(
        body, name=name,
        out_shape=(pltpu.SemaphoreType.DMA((sum(slots),)), pltpu.SemaphoreType.DMA((sum(slots),)), *bufs,
                   jax.ShapeDtypeStruct((8, 128), F32)),
        in_specs=[_HBM] * (2 * na),
        out_specs=(_SEM, _SEM, *([_HBM] * (2 * na)), pl.BlockSpec(memory_space=pltpu.VMEM)),
        input_output_aliases={i: 2 + i for i in range(2 * na)},
        compiler_params=pltpu.CompilerParams(has_side_effects=_EFFECT, collective_id=collective_id),
    )(*[pltpu.with_memory_space_constraint(t, pltpu.HBM) for t in srcs],
      *[pltpu.with_memory_space_constraint(lax.empty(t.shape, t.dtype), pltpu.HBM) for t in zones])
    return out[0], out[1], out[2:2 + na], out[2 + na:2 + 2 * na], out[-1]


def _exchange_wait(send_sem, recv_sem, srcs, lands, directs, after, name):
    na = len(srcs)

    def body(*refs):
        src, land = refs[:na], refs[na:2 * na]
        send_sem_ref, recv_sem_ref = refs[2 * na], refs[2 * na + 1]
        for a, block, slot, peer, sem in _exchange_copies(directs)[0]:
            cp = pltpu.make_async_remote_copy(
                src_ref=src[a].at[block], dst_ref=land[a].at[slot], send_sem=send_sem_ref.at[sem],
                recv_sem=recv_sem_ref.at[sem], device_id=peer, device_id_type=MESH)
            cp.wait_send()
            cp.wait_recv()

    bufs = [pltpu.HBM(t.shape, t.dtype) for t in list(srcs) + list(lands)]
    out = pl.pallas_call(
        body, name=name,
        out_shape=tuple(bufs),
        in_specs=[_HBM] * (2 * na) + [_SEM, _SEM, pl.BlockSpec(memory_space=pl.ANY)],
        out_specs=[_HBM] * (2 * na),
        input_output_aliases={i: i for i in range(2 * na)},
        compiler_params=pltpu.CompilerParams(has_side_effects=_EFFECT),
    )(*srcs, *lands, send_sem, recv_sem, after)
    return out[:na], out[na:]


def _own_then_slots(mine_ref, lands_ref, rows=slice(None)):
    if len(mine_ref.shape) == 3:
        x, y, c = _mesh_pos()
        total = mine_ref[4 * x + 2 * y + c, rows, :].astype(F32)
    else:
        total = mine_ref[rows, :].astype(F32)
    for j in range(lands_ref.shape[0]):
        total = total + lands_ref[j, rows, :].astype(F32)
    return total


SMALL_ROWS = 2 * PACK_SLICE + 2 * 8


def _small_block(mine, lands, dgpre, name):
    def body(*refs):
        hm, ld, dg = refs[:DEPTH], refs[DEPTH:2 * DEPTH], refs[2 * DEPTH:3 * DEPTH]
        blk, land, sem = refs[3 * DEPTH:]
        for l in range(DEPTH):
            blk[PACK_SLICE * l:PACK_SLICE * (l + 1), :] = _own_then_slots(hm[l], ld[l])
            blk[2 * PACK_SLICE + 8 * l:2 * PACK_SLICE + 8 * (l + 1), :] = dg[l][...]
        cp = pltpu.make_async_copy(blk, _device_rows(land, SMALL_ROWS, *_mesh_pos()), sem)
        cp.start()
        cp.wait()

    vmem = pl.BlockSpec(memory_space=pltpu.VMEM)
    return pl.pallas_call(
        body, name=name,
        in_specs=[vmem] * (3 * DEPTH), out_specs=[vmem, pl.BlockSpec(memory_space=pl.ANY)],
        out_shape=[jax.ShapeDtypeStruct((SMALL_ROWS, 128), F32), jax.ShapeDtypeStruct((N_DEV * SMALL_ROWS, 128), F32)],
        scratch_shapes=[pltpu.SemaphoreType.DMA],
        compiler_params=_compiler_params(),
    )(*mine, *lands, *dgpre)


def _adamw_math(w, g, m, v):
    m = ADAM_B1 * m + (1.0 - ADAM_B1) * g
    v = ADAM_B2 * v + (1.0 - ADAM_B2) * (g * g)
    m_hat = m / (1.0 - ADAM_B1 ** ADAM_STEP)
    v_hat = v / (1.0 - ADAM_B2 ** ADAM_STEP)
    delta = -ADAM_LR * (m_hat / (jnp.sqrt(v_hat) + ADAM_EPS) + ADAM_WD * w)
    return delta, m, v


def _adamw_layer(layer, mine, lands, w, m, v, earlier, token, name, rows):
    _, mm, nn = w.shape

    def body(hm_ref, ld_ref, w_ref, m_ref, v_ref, _, *refs):
        g_ref, d_ref, nm_ref, nv_ref = refs[-4:]
        g = _own_then_slots(hm_ref, ld_ref)
        g_ref[...] = g
        d, nm, nv = _adamw_math(w_ref[...], g, m_ref[...], v_ref[...])
        d_ref[...] = d
        nm_ref[...] = nm
        nv_ref[...] = nv

    spec = pl.BlockSpec((None, rows, nn), lambda i: (layer, i, 0))
    carried = [] if earlier is None else list(earlier)
    return pl.pallas_call(
        body, name=name, grid=(mm // rows,),
        in_specs=([pl.BlockSpec((rows, nn), lambda i: (i, 0)) if mine.ndim == 2
                   else pl.BlockSpec((N_DEV, rows, nn), lambda i: (0, i, 0)),
                   pl.BlockSpec((lands.shape[0], rows, nn), lambda i: (0, i, 0)),
                   spec, spec, spec] + [pl.BlockSpec(memory_space=pl.ANY)] * (1 + len(carried))),
        out_specs=[spec] * 4,
        out_shape=[jax.ShapeDtypeStruct(w.shape, F32)] * 4,
        input_output_aliases={6 + t: t for t in range(len(carried))},
        compiler_params=_compiler_params(("arbitrary",)),
    )(mine, lands, w, m, v, token, *carried)


def _adamw_small(gathered, params):
    def body(all_ref, *refs):
        ins, outs, packs = refs[:15], refs[15:15 + 21], refs[15 + 21]
        loss_ref = outs[0]
        for dev in range(N_DEV):
            for l in range(DEPTH):
                packs[l, PACK_SLICE * dev:PACK_SLICE * (dev + 1), :] = (
                    all_ref[SMALL_ROWS * dev + PACK_SLICE * l:SMALL_ROWS * dev + PACK_SLICE * (l + 1), :])
        loss_ref[...] = packs[DEPTH - 1, ROW_LOSS:ROW_LOSS + 1, 0:1]

        def update(p, sel, g):
            w_ref, m_ref, v_ref = ins[p], ins[5 + p], ins[10 + p]
            d, nm, nv = _adamw_math(w_ref[sel], g, m_ref[sel], v_ref[sel])
            for t, val in enumerate((g, d, nm, nv)):
                outs[1 + 5 * t + p][sel] = val

        for l in range(DEPTH):
            gp = packs.at[l]
            row0 = 2 * PACK_SLICE + 8 * l
            dgpre = all_ref[row0:row0 + 8, :]
            for dev in range(1, N_DEV):
                dgpre = dgpre + all_ref[SMALL_ROWS * dev + row0:SMALL_ROWS * dev + row0 + 8, :]
            for grp in range(4):
                update(0, (l, grp), gp[ROW_PW + BLOCK * grp:ROW_PW + BLOCK * (grp + 1), :])
                update(1, (slice(l, l + 1), slice(128 * grp, 128 * (grp + 1))), gp[ROW_SC + grp:ROW_SC + grp + 1, :])
            update(2, (slice(l, l + 1), slice(None)), gp[ROW_SINK:ROW_SINK + 1, 0:N_HEADS])
            for r in range(D_MODEL // 128):
                sel = (slice(l, l + 1), slice(128 * r, 128 * (r + 1)))
                update(3, sel, dgpre[r:r + 1, :])
                update(4, sel, gp[ROW_NPOST + r:ROW_NPOST + r + 1, :])

    shapes = [jax.ShapeDtypeStruct(p.shape, F32) for p in params[:5]]
    return pl.pallas_call(
        body, name="adamw_small",
        out_shape=[jax.ShapeDtypeStruct((1, 1), F32)] + shapes * 4,
        scratch_shapes=[pltpu.VMEM((DEPTH, PACK_ROWS, 128), F32)],
        compiler_params=_compiler_params(),
    )(gathered, *params)


def kernel(x, w_in, pool_w, pool_scale, attn_sinks, w_out, norm_pre, norm_post, loss_target, m_w_in, m_pool_w, m_pool_scale, m_attn_sinks, m_w_out, m_norm_pre, m_norm_post, v_w_in, v_pool_w, v_pool_scale, v_attn_sinks, v_w_out, v_norm_pre, v_norm_post):
    x0 = x.reshape(SEQ, D_MODEL)
    target = loss_target.reshape(SEQ, D_MODEL)
    bias = jnp.asarray(_attn_bias())
    w_in_t, m_in_t, v_in_t = (jnp.swapaxes(t, 1, 2) for t in (w_in, m_w_in, v_w_in))

    (win0, wout0), later, lands = _allgather([(w_in_t, 0), (w_out, 0)], BF16, "gather_w0",
                                              later=[(w_in_t, 1), (w_out, 1)])
    sems, later, lands, token = _gather_start(later, lands, [SIBLING_AND_SAME_CORE, ALL_PEERS], COLLECTIVE_GATHER_W1,
                                              "gather_w1_start")
    win_full, wout_full = [win0, None], [wout0, None]

    saved = []
    xl = x0
    for layer in range(DEPTH):
        u, pg, q, k, v, ag, z, a = _fwd_front(layer, xl, norm_pre, win_full[layer], token, attn_sinks,
                                             pool_w, pool_scale, bias)
        if layer == 0:
            land = _gather_wait(sems[0], later[0], lands[0], SIBLING_AND_SAME_CORE, z, "gather_w_in1_wait")
            fsems, land, token = _forward_start(land, IN_SHARD, "forward_w_in1_start")
        else:
            wout_full[layer] = _gather_wait(sems[1], later[1], lands[1], ALL_PEERS, z, "gather_w_out1_wait")
        x_next, y = _fwd_out(layer, z, xl, norm_post, wout_full[layer], token)
        if layer == 0:
            win_full[1] = _forward_wait(fsems, land, IN_SHARD, x_next, "forward_w_in1_wait")
        saved.append((xl, u, pg, q, k, v, ag, z, a, y))
        xl = x_next

    params_small = [pool_w, pool_scale, attn_sinks, norm_pre, norm_post,
                    m_pool_w, m_pool_scale, m_attn_sinks, m_norm_pre, m_norm_post,
                    v_pool_w, v_pool_scale, v_attn_sinks, v_norm_pre, v_norm_post]

    def start(srcs, directs, paired, collective_id, tag):
        send_sem, recv_sem, srcs, lands, started = _exchange_start(srcs, directs, collective_id, f"exchange_start{tag}")
        return (send_sem, recv_sem, srcs, lands, paired, directs), started

    def finish(handle, after, tag):
        send_sem, recv_sem, srcs, lands, paired, directs = handle
        srcs, lands = _exchange_wait(send_sem, recv_sem, srcs, lands, directs, after, f"exchange_wait{tag}")
        return [s if p is None else p for s, p in zip(srcs, paired)], lands

    def back(layer, top, first, second):
        xin, u, pg, q, k, v, ag, z, a, y = saved[layer]
        return _bwd_back(layer, top, first, second, y, z, norm_post, wout_full[layer], attn_sinks, u, pg, q, k, v,
                         ag, a, pool_w, pool_scale, bias)

    dgpre = [None] * DEPTH
    dx, dproj, gw_out, pack = back(1, True, xl, target)
    dx, dgpre[1], chip_sums, own_sum = _bwd_in(1, "both_pair", token, dproj, saved[1][0], norm_pre, dx, win_full[1])
    top, token = start([chip_sums, gw_out.reshape(N_DEV, OUT_SHARD, D_MODEL), pack.reshape(N_DEV, PACK_SLICE, 128)],
                       [False, True, True], [own_sum, None, None], COLLECTIVE_EXCHANGE_1, "1")

    dproj, gw_out, pack = back(0, False, dx, token)
    early, token = start([gw_out.reshape(N_DEV, OUT_SHARD, D_MODEL), pack.reshape(N_DEV, PACK_SLICE, 128)],
                         [True, True], [None, None], COLLECTIVE_EXCHANGE_0A, "0a")
    chip_sums, own_sum = _bwd_in(0, "dw_pair", token, dproj, saved[0][0], norm_pre)
    late, token = start([chip_sums], [False], [own_sum], COLLECTIVE_EXCHANGE_0B, "0b")
    dx, dgpre[0] = _bwd_in(0, "dx", token, dproj, saved[0][0], norm_pre, dx, win_full[0])

    own1, lands1 = finish(top, dx, "1")
    big_in = _adamw_layer(1, own1[0], lands1[0], w_in_t, m_in_t, v_in_t, None, token, "adamw_in1", ADAM_ROWS_IN)
    big_out = _adamw_layer(1, own1[1], lands1[1], w_out, m_w_out, v_w_out, None, token, "adamw_out1", ADAM_ROWS_OUT)
    own0a, lands0a = finish(early, big_out[0], "0a")
    block, land = _small_block([own0a[1], own1[2]], [lands0a[1], lands1[2]], dgpre, "small_block")
    sems, block, land, token = _gather_start([block], [land], [ALL_PEERS], COLLECTIVE_GATHER_SMALL, "gather_small_start")
    big_out = _adamw_layer(0, own0a[0], lands0a[0], w_out, m_w_out, v_w_out, big_out, token, "adamw_out0", ADAM_ROWS_OUT)
    own0b, lands0b = finish(late, big_out[0], "0b")
    big_in = _adamw_layer(0, own0b[0], lands0b[0], w_in_t, m_in_t, v_in_t, big_in, token, "adamw_in0", ADAM_ROWS_IN)
    gathered = _gather_wait(sems[0], block[0], land[0], ALL_PEERS, big_in[0], "gather_small_wait")
    small_out = _adamw_small(gathered, params_small)
    loss = small_out[0].reshape(())

    outs = [loss, dx.reshape(1, SEQ, D_MODEL)]
    for t in range(4):
        pw_, sc_, sk_, npre_, npost_ = small_out[1 + 5 * t:6 + 5 * t]
        outs += [jnp.swapaxes(big_in[t], 1, 2), pw_, sc_, sk_, big_out[t], npre_, npost_]
    return tuple(outs)
```

```python
import numpy as np
import jax
import jax.numpy as jnp
from jax import lax
from jax.experimental import pallas as pl
from jax.experimental.pallas import tpu as pltpu

F32 = jnp.float32
BF16 = jnp.bfloat16

N_DEV = 8
SEQ = 2048
D_MODEL = 1024
D_POOL = 512
D_ATTN = 512
D_KV = 128
D_IN = 2304
N_HEADS = 8
GQA = 4
HEAD_DIM = 64
BLOCK = 128
POOL_WINDOWS = (2, 4, 8, 16)
DEPTH = 2
EPS = 1e-6
NEG_INF = -1e30
SCALE = HEAD_DIM ** -0.5
IN_SHARD = D_IN // N_DEV
OUT_SHARD = D_MODEL // N_DEV

COL_U, COL_PG, COL_Q, COL_K, COL_V, COL_AG = 0, 512, 1024, 1536, 1664, 1792

ADAM_LR = 0.001
ADAM_B1 = 0.9
ADAM_B2 = 0.999
ADAM_EPS = 1e-08
ADAM_WD = 0.01
ADAM_STEP = 10

TOKEN_TILE = 512
FWD_OUT_TILE = 1024
ADAM_ROWS_IN, ADAM_ROWS_OUT = 144, 64
VMEM_LIMIT = 56 * 1024 * 1024
MESH = pl.DeviceIdType.MESH

ROW_PW, ROW_SC, ROW_SINK, ROW_NPOST, ROW_LOSS = 0, 512, 520, 536, 544
PACK_ROWS = 576
PACK_SLICE = PACK_ROWS // N_DEV


def _nn(a, b):
    return jnp.dot(a, b, preferred_element_type=F32)


def _nt(a, b):
    return lax.dot_general(a, b, (((1,), (1,)), ((), ())), preferred_element_type=F32)


def _tn(a, b):
    return lax.dot_general(a, b, (((0,), (0,)), ((), ())), preferred_element_type=F32)


def _silu_parts(g):
    s = jax.nn.sigmoid(g)
    return g * s, s * (1.0 + g * (1.0 - s))


def _resident(shape):
    return pl.BlockSpec(shape, lambda *_: (0,) * len(shape), pipeline_mode=pl.Buffered(1))


def _compiler_params(sem=None):
    if sem is None:
        return pltpu.CompilerParams(vmem_limit_bytes=VMEM_LIMIT)
    return pltpu.CompilerParams(dimension_semantics=sem, vmem_limit_bytes=VMEM_LIMIT)


def _attn_bias():
    t = np.arange(BLOCK)[None, :]
    j = np.arange(BLOCK)[:, None]
    current = j <= t
    dist = np.where(current, t - j, t + BLOCK - j).astype(np.float32)
    out = np.zeros((2, 2, BLOCK, GQA * BLOCK), np.float32)
    for variant in range(2):
        valid = current | (variant == 1)
        for kv in range(2):
            for g in range(GQA):
                slope = np.float32(2.0 ** (-(kv * GQA + g + 1)))
                out[variant, kv, :, g * BLOCK:(g + 1) * BLOCK] = np.where(valid, -slope * dist, np.float32(NEG_INF))
    return out


def _replicate_head(kx, kv):
    rolled = pltpu.roll(kx, 64, 1)
    lane = lax.broadcasted_iota(jnp.int32, kx.shape, 1)
    dup = jnp.where(lane < 64, kx, rolled) if kv == 0 else jnp.where(lane < 64, rolled, kx)
    return jnp.concatenate([dup, dup], axis=1).astype(BF16)


def _stack_heads(qv):
    lane = lax.broadcasted_iota(jnp.int32, qv.shape, 1)
    zero = jnp.zeros_like(qv)
    return jnp.concatenate([jnp.where((lane >= 64 * g) & (lane < 64 * g + 64), qv, zero) for g in range(GQA)], axis=0)


def _unstack_heads(xs):
    lane = lax.broadcasted_iota(jnp.int32, (BLOCK, 256), 1)
    return jnp.where(lane < 64, xs[0:128], jnp.where(lane < 128, xs[128:256], jnp.where(lane < 192, xs[256:384], xs[384:512])))


def _fold_heads(r):
    h = r[:, 0:128] + r[:, 128:256]
    return h + pltpu.roll(h, 64, 1)


def _sink_row(sink_ref, layer, kv):
    lane = lax.broadcasted_iota(jnp.int32, (1, GQA * BLOCK), 1)
    s4 = [sink_ref[layer, kv * GQA + g] for g in range(GQA)]
    return jnp.where(lane < 128, s4[0], jnp.where(lane < 256, s4[1], jnp.where(lane < 384, s4[2], s4[3])))


def _band_is_current():
    j = lax.broadcasted_iota(jnp.int32, (BLOCK, GQA * BLOCK), 0)
    t = lax.broadcasted_iota(jnp.int32, (BLOCK, GQA * BLOCK), 1) & (BLOCK - 1)
    return j <= t


def _pack_band(full, current):
    return jnp.where(current, full[BLOCK:], full[:BLOCK])


def _unpack_band(packed, current):
    zero = jnp.zeros_like(packed)
    return jnp.concatenate([jnp.where(current, zero, packed), jnp.where(current, packed, zero)], axis=0)


def _probs_keys_major(k_rep, q_st, bias, sink, current):
    st = _pack_band(_nt(k_rep, q_st), current) * SCALE + bias
    m = jnp.maximum(jnp.max(st, axis=0, keepdims=True), sink)
    p = jnp.exp(st - m)
    esink = jnp.exp(sink - m)
    rl = 1.0 / (jnp.sum(p, axis=0, keepdims=True) + esink)
    return p * rl, esink * rl


WINDOW_HALO = 16


def _window_sum(ext, w, forward):
    s = ext
    sh = 1
    while sh < w:
        s = s + pltpu.roll(s, (ext.shape[0] - sh) if forward else sh, 0)
        sh *= 2
    return s


def _inv_count(n, w):
    t = n * BLOCK + lax.broadcasted_iota(jnp.int32, (BLOCK, 1), 0) + 1
    return 1.0 / jnp.minimum(t.astype(F32), float(w))


def _kv_ext(ref, n):
    r0 = pl.multiple_of(jnp.maximum(n - 1, 0) * BLOCK, BLOCK)
    r1 = pl.multiple_of(n * BLOCK, BLOCK)
    return jnp.concatenate([ref[pl.ds(r0, BLOCK), :], ref[pl.ds(r1, BLOCK), :]], axis=0)


def _rows_of(vec_ref, pack_ref, row0):
    for r in range(D_MODEL // 128):
        pack_ref[row0 + r:row0 + r + 1, :] = vec_ref[:, 128 * r:128 * (r + 1)]


FRONT_TILE = 4 * BLOCK


def _fwd_front(layer, x, norm_pre, w_in_t, token, sinks, pool_w, pool_scale, bias):
    tm = FRONT_TILE

    def body(sink_ref, x_ref, g_ref, w_ref, _, pw_ref, sc_ref, bias_ref,
             u_ref, pg_ref, q_ref, k_ref, v_ref, ag_ref, z_ref, a_ref, uprev, kprev, vprev):
        i = pl.program_id(0)

        @pl.when(i == 0)
        def _():
            uprev[...] = jnp.zeros_like(uprev)
            kprev[...] = jnp.zeros_like(kprev)
            vprev[...] = jnp.zeros_like(vprev)

        xv = x_ref[...]
        r = lax.rsqrt(jnp.mean(xv * xv, axis=-1, keepdims=True) + EPS)
        h = (xv * r * g_ref[layer:layer + 1, :]).astype(BF16)
        u_ref[...] = _nt(h, w_ref[COL_U:COL_PG, :])
        pg_ref[...] = _nt(h, w_ref[COL_PG:COL_Q, :])
        for sb in range(tm // BLOCK):
            n = (tm // BLOCK) * i + sb
            rows = slice(BLOCK * sb, BLOCK * (sb + 1))
            before = slice(BLOCK * (sb - 1), BLOCK * sb)
            uv = u_ref[rows, :]
            halo = (uprev[BLOCK - WINDOW_HALO:, :] if sb == 0
                    else u_ref[BLOCK * sb - WINDOW_HALO:BLOCK * sb, :])
            ext = jnp.concatenate([halo, uv], axis=0)
            for g, w in enumerate(POOL_WINDOWS):
                cs = slice(BLOCK * g, BLOCK * (g + 1))
                win = _window_sum(ext[:, cs], w, forward=False)[WINDOW_HALO:]
                pooled = win * _inv_count(n, w) - uv[:, cs]
                mixed = _nn(pooled.astype(BF16), pw_ref[g].astype(BF16))
                gate, _ = _silu_parts(pg_ref[rows, cs])
                z_ref[rows, cs] = (mixed * sc_ref[layer:layer + 1, cs] * gate).astype(BF16)

        q_ref[...] = _nt(h, w_ref[COL_Q:COL_K, :]).astype(BF16)
        k_ref[...] = _nt(h, w_ref[COL_K:COL_V, :])
        v_ref[...] = _nt(h, w_ref[COL_V:COL_AG, :])
        ag_ref[...] = _nt(h, w_ref[COL_AG:D_IN, :])

        current = _band_is_current()
        for sb in range(tm // BLOCK):
            n = (tm // BLOCK) * i + sb
            rows = slice(BLOCK * sb, BLOCK * (sb + 1))
            before = slice(BLOCK * (sb - 1), BLOCK * sb)
            kx = jnp.concatenate([kprev[...] if sb == 0 else k_ref[before, :], k_ref[rows, :]], axis=0)
            vx = jnp.concatenate([vprev[...] if sb == 0 else v_ref[before, :], v_ref[rows, :]], axis=0)
            variant = jnp.minimum(n, 1) if sb == 0 else 1
            for kv in range(2):
                cs = slice(256 * kv, 256 * (kv + 1))
                p, _ = _probs_keys_major(_replicate_head(kx, kv), _stack_heads(q_ref[rows, cs]),
                                         bias_ref[variant, kv], _sink_row(sink_ref, layer, kv), current)
                o = _unstack_heads(_tn(_unpack_band(p.astype(BF16), current), _replicate_head(vx, kv)))
                a_ref[rows, cs] = o
                gate, _ = _silu_parts(ag_ref[rows, cs])
                z_ref[rows, D_POOL + 256 * kv:D_POOL + 256 * (kv + 1)] = (o * gate).astype(BF16)

        tail = slice(tm - BLOCK, tm)
        uprev[...] = u_ref[tail, :]
        kprev[...] = k_ref[tail, :]
        vprev[...] = v_ref[tail, :]

    row = lambda c: pl.BlockSpec((tm, c), lambda i: (i, 0))
    const = lambda shape: pl.BlockSpec(shape, lambda i: (0,) * len(shape))
    return pl.pallas_call(
        body, name=f"fwd_front{layer}", grid=(SEQ // tm,),
        in_specs=[pl.BlockSpec(memory_space=pltpu.SMEM), row(D_MODEL), const((DEPTH, D_MODEL)),
                  _resident((D_IN, D_MODEL)), const((8, 128)),
                  pl.BlockSpec((None, 4, BLOCK, BLOCK), lambda i: (layer, 0, 0, 0)), const((DEPTH, D_POOL)),
                  _resident((2, 2, BLOCK, GQA * BLOCK))],
        out_specs=[row(D_POOL), row(D_POOL), row(D_ATTN), row(D_KV), row(D_KV), row(D_ATTN), row(D_MODEL),
                   row(D_ATTN)],
        out_shape=[jax.ShapeDtypeStruct((SEQ, D_POOL), F32), jax.ShapeDtypeStruct((SEQ, D_POOL), F32),
                   jax.ShapeDtypeStruct((SEQ, D_ATTN), BF16), jax.ShapeDtypeStruct((SEQ, D_KV), F32),
                   jax.ShapeDtypeStruct((SEQ, D_KV), F32), jax.ShapeDtypeStruct((SEQ, D_ATTN), F32),
                   jax.ShapeDtypeStruct((SEQ, D_MODEL), BF16), jax.ShapeDtypeStruct((SEQ, D_ATTN), F32)],
        scratch_shapes=[pltpu.VMEM((BLOCK, D_POOL), F32), pltpu.VMEM((BLOCK, D_KV), F32),
                        pltpu.VMEM((BLOCK, D_KV), F32)],
        compiler_params=_compiler_params(("arbitrary",)),
    )(sinks, x, norm_pre, w_in_t, token, pool_w, pool_scale, bias)


def _fwd_out(layer, z, x, norm_post, w_out, token):
    tm = FWD_OUT_TILE

    def body(z_ref, x_ref, g_ref, w_ref, _, xn_ref, y_ref):
        y = _nn(z_ref[...], w_ref[...])
        y_ref[...] = y
        r = lax.rsqrt(jnp.mean(y * y, axis=-1, keepdims=True) + EPS)
        xn_ref[...] = x_ref[...] + y * r * g_ref[layer:layer + 1, :]

    row = lambda c: pl.BlockSpec((tm, c), lambda i: (i, 0))
    return pl.pallas_call(
        body, name=f"fwd_out{layer}", grid=(SEQ // tm,),
        in_specs=[row(D_MODEL), row(D_MODEL), pl.BlockSpec((DEPTH, D_MODEL), lambda i: (0, 0)),
                  _resident((D_MODEL, D_MODEL)), pl.BlockSpec((8, 128), lambda i: (0, 0))],
        out_specs=[row(D_MODEL), row(D_MODEL)],
        out_shape=[jax.ShapeDtypeStruct((SEQ, D_MODEL), F32), jax.ShapeDtypeStruct((SEQ, D_MODEL), F32)],
        compiler_params=_compiler_params(("arbitrary",)),
    )(z, x, norm_post, w_out, token)


BACK_TILE = 2 * BLOCK


def _bwd_back(layer, top, dxo_or_xf, target_or_token, y, z, norm_post, w_out, sinks, u, pg, q, k, v, ag, a,
              pool_w, pool_scale, bias):
    tm = BACK_TILE
    steps = SEQ // tm
    last = steps - 1
    per = tm // BLOCK

    def body(*refs):
        refs = list(refs)
        sink_ref, first, second = refs[:3]
        (y_ref, z_ref, g_ref, w_ref, u_ref, up_ref, pg_ref, q_ref, k_ref, v_ref, ag_ref, a_ref, pw_ref, sc_ref,
         bias_ref) = refs[3:18]
        del refs[:18]
        dxo_ref = refs.pop(0) if top else None
        dp_ref, dw_ref, pack_ref, acc, dg, lacc, dzs, ck, cv, ce = refs
        i = pl.program_id(0)
        blk = last - i

        @pl.when(i == 0)
        def _():
            acc[...] = jnp.zeros_like(acc)
            dg[...] = jnp.zeros_like(dg)
            lacc[...] = jnp.zeros_like(lacc)
            pack_ref[...] = jnp.zeros_like(pack_ref)
            ck[...] = jnp.zeros_like(ck)
            cv[...] = jnp.zeros_like(cv)
            ce[...] = jnp.zeros_like(ce)

        if top:
            d = first[...] - second[...]
            dxo_v = d * (1.0 / D_MODEL)
            dxo_ref[...] = dxo_v
            lacc[...] += jnp.sum((d * d).reshape(tm // 8, 8, D_MODEL), axis=0)
        else:
            dxo_v = first[...]
        yv = y_ref[...]
        r = lax.rsqrt(jnp.mean(yv * yv, axis=-1, keepdims=True) + EPS)
        yn = yv * r
        dg[...] += jnp.sum(dxo_v * yn, axis=0, keepdims=True)
        dyn = dxo_v * g_ref[layer:layer + 1, :]
        dy = (r * (dyn - yn * jnp.mean(dyn * yn, axis=-1, keepdims=True))).astype(BF16)
        dzs[...] = _nt(dy, w_ref[...])
        acc[...] += _tn(z_ref[...], dy)

        lane = lax.broadcasted_iota(jnp.int32, (1, 128), 1)
        lane2 = lax.broadcasted_iota(jnp.int32, (256, 128), 1)
        current = _band_is_current()
        for sb in reversed(range(per)):
            n = per * blk + sb
            rows = slice(BLOCK * sb, BLOCK * (sb + 1))

            uv = u_ref[rows, :]
            if sb == 0:
                halo = up_ref[BLOCK - WINDOW_HALO:, :] * (n > 0).astype(F32)
            else:
                halo = u_ref[BLOCK * sb - WINDOW_HALO:BLOCK * sb, :]
            ext = jnp.concatenate([halo, uv], axis=0)
            for g, w in enumerate(POOL_WINDOWS):
                cs = slice(BLOCK * g, BLOCK * (g + 1))
                inv = _inv_count(n, w)
                win = _window_sum(ext[:, cs], w, forward=False)[WINDOW_HALO:]
                pooled = win * inv - uv[:, cs]
                pw_g = pw_ref[g].astype(BF16)
                mixed = _nn(pooled.astype(BF16), pw_g)
                gate, dgate = _silu_parts(pg_ref[rows, cs])
                dzp = dzs[rows, cs]
                sc = sc_ref[layer:layer + 1, cs]
                dpm = dzp * gate
                dp_ref[rows, COL_PG + BLOCK * g:COL_PG + BLOCK * (g + 1)] = (dzp * (mixed * sc) * dgate).astype(BF16)
                pack_ref[ROW_SC + g:ROW_SC + g + 1, :] += jnp.sum(dpm * mixed, axis=0, keepdims=True)
                dmixed = (dpm * sc).astype(BF16)
                pack_ref[ROW_PW + BLOCK * g:ROW_PW + BLOCK * (g + 1), :] += _tn(pooled.astype(BF16), dmixed)
                dpooled = _nt(dmixed, pw_g)
                e = dpooled * inv
                lead = _window_sum(jnp.concatenate([e, ce[:WINDOW_HALO, cs]], axis=0), w, forward=True)[:BLOCK]
                dp_ref[rows, COL_U + BLOCK * g:COL_U + BLOCK * (g + 1)] = (lead - dpooled).astype(BF16)
                ce[:, cs] = e

            kx = _kv_ext(k_ref, n)
            vx = _kv_ext(v_ref, n)
            variant = jnp.minimum(n, 1) if sb == 0 else 1
            dsink_row = jnp.zeros((1, 128), F32)
            tks, tvs = [], []
            for kv in range(2):
                cs = slice(256 * kv, 256 * (kv + 1))
                k_rep = _replicate_head(kx, kv)
                v_rep = _replicate_head(vx, kv)
                q_st = _stack_heads(q_ref[rows, cs])
                gate, dgate = _silu_parts(ag_ref[rows, cs])
                dza = dzs[rows, D_POOL + 256 * kv:D_POOL + 256 * (kv + 1)]
                dp_ref[rows, COL_AG + 256 * kv:COL_AG + 256 * (kv + 1)] = (dza * a_ref[rows, cs] * dgate).astype(BF16)
                da_st = _stack_heads((dza * gate).astype(BF16))
                p, psink = _probs_keys_major(k_rep, q_st, bias_ref[variant, kv], _sink_row(sink_ref, layer, kv),
                                             current)
                dpt = _pack_band(_nt(v_rep, da_st), current)
                delta = jnp.sum(p * dpt, axis=0, keepdims=True)
                dst = _unpack_band((p * (dpt - delta) * SCALE).astype(BF16), current)
                sink_terms = psink * delta
                for g in range(GQA):
                    dsink = -jnp.sum(sink_terms[:, BLOCK * g:BLOCK * (g + 1)], axis=1, keepdims=True)
                    dsink_row = dsink_row + jnp.where(lane == kv * GQA + g, dsink, 0.0)
                dp_ref[rows, COL_Q + 256 * kv:COL_Q + 256 * (kv + 1)] = _unstack_heads(_tn(dst, k_rep)).astype(BF16)
                tks.append(_fold_heads(_nn(dst, q_st)))
                tvs.append(_fold_heads(_nn(_unpack_band(p.astype(BF16), current), da_st)))
            pack_ref[ROW_SINK:ROW_SINK + 1, :] += dsink_row
            dkx = jnp.where(lane2 < 64, tks[0], tks[1])
            dvx = jnp.where(lane2 < 64, tvs[0], tvs[1])
            dp_ref[rows, COL_K:COL_V] = (ck[...] + dkx[BLOCK:]).astype(BF16)
            dp_ref[rows, COL_V:COL_AG] = (cv[...] + dvx[BLOCK:]).astype(BF16)
            ck[...] = dkx[:BLOCK]
            cv[...] = dvx[:BLOCK]

        @pl.when(i == steps - 1)
        def _():
            dw_ref[...] = acc[...].astype(BF16)
            _rows_of(dg, pack_ref, ROW_NPOST)
            loss = jnp.sum(jnp.sum(lacc[...], axis=-1, keepdims=True), axis=0, keepdims=True) * (0.5 / D_MODEL)
            pack_ref[ROW_LOSS:ROW_LOSS + 1, :] = jnp.where(lane == 0, loss, 0.0)

    row = lambda c: pl.BlockSpec((tm, c), lambda i: (last - i, 0))
    const = lambda shape: pl.BlockSpec(shape, lambda i: (0,) * len(shape))
    act = jax.ShapeDtypeStruct((SEQ, D_MODEL), F32)
    return pl.pallas_call(
        body, name=f"bwd_back{layer}", grid=(steps,),
        in_specs=[pl.BlockSpec(memory_space=pltpu.SMEM), row(D_MODEL), row(D_MODEL) if top else const((8, 128)),
                  row(D_MODEL), row(D_MODEL), const((DEPTH, D_MODEL)), _resident((D_MODEL, D_MODEL)),
                  row(D_POOL), pl.BlockSpec((BLOCK, D_POOL), lambda i: (jnp.maximum(per * (last - i) - 1, 0), 0)),
                  row(D_POOL), row(D_ATTN), _resident((SEQ, D_KV)), _resident((SEQ, D_KV)), row(D_ATTN), row(D_ATTN),
                  pl.BlockSpec((None, 4, BLOCK, BLOCK), lambda i: (layer, 0, 0, 0)), const((DEPTH, D_POOL)),
                  _resident((2, 2, BLOCK, GQA * BLOCK))],
        out_specs=([row(D_MODEL)] * (1 if top else 0)
                   + [row(D_IN), const((D_MODEL, D_MODEL)), const((PACK_ROWS, 128))]),
        out_shape=([act] * (1 if top else 0)
                   + [jax.ShapeDtypeStruct((SEQ, D_IN), BF16), jax.ShapeDtypeStruct((D_MODEL, D_MODEL), BF16),
                      jax.ShapeDtypeStruct((PACK_ROWS, 128), F32)]),
        scratch_shapes=[pltpu.VMEM((D_MODEL, D_MODEL), F32), pltpu.VMEM((1, D_MODEL), F32), pltpu.VMEM((8, D_MODEL), F32),
                        pltpu.VMEM((tm, D_MODEL), F32), pltpu.VMEM((BLOCK, D_KV), F32), pltpu.VMEM((BLOCK, D_KV), F32),
                        pltpu.VMEM((BLOCK, D_POOL), F32)],
        compiler_params=_compiler_params(("arbitrary",)),
    )(sinks, dxo_or_xf, target_or_token, y, z, norm_post, w_out, u, u, pg, q, k, v, ag, a, pool_w, pool_scale, bias)


def _bwd_in(layer, part, token, dproj, x, norm_pre, dxo=None, w_in_t=None):
    pair = part in ("dw_pair", "both_pair")
    want_dw, want_dx = part != "dx", part in ("both", "dx", "both_pair")
    tm = TOKEN_TILE
    steps = SEQ // tm
    cw = 256

    def body(*refs):
        refs = list(refs)
        dp_ref, x_ref, g_ref = refs[1:4]
        del refs[:4]
        if want_dx:
            dxo_ref, w_ref, dx_ref, dgo_ref = refs[:4]
            del refs[:4]
            dg = refs.pop()
        if pair:
            hs_ref, hm_ref, acc, mine_buf, theirs_buf, send_sem, recv_sem = refs
        elif want_dw:
            dw_ref, acc = refs
        i = pl.program_id(0)

        @pl.when(i == 0)
        def _():
            if pair:
                _handshake([(lax.axis_index("x"), lax.axis_index("y"), 1 - lax.axis_index("c"))])
            if want_dw:
                acc[...] = jnp.zeros_like(acc)
            if want_dx:
                dg[...] = jnp.zeros_like(dg)

        xv = x_ref[...]
        gv = g_ref[layer:layer + 1, :]
        r = lax.rsqrt(jnp.mean(xv * xv, axis=-1, keepdims=True) + EPS)
        xn = xv * r
        if want_dw:
            hb = (xn * gv).astype(BF16)
            for c in range(0, D_IN, cw):
                acc[c:c + cw, :] += _tn(dp_ref[:, c:c + cw], hb)
        def rows_for(q, core):
            return pl.ds(pl.multiple_of((2 * q + core) * IN_SHARD, 8), IN_SHARD)

        def swap(q):
            x, y, c = _mesh_pos()
            return pltpu.make_async_remote_copy(
                src_ref=mine_buf.at[q], dst_ref=theirs_buf.at[q], send_sem=send_sem.at[q], recv_sem=recv_sem.at[q],
                device_id=(x, y, 1 - c), device_id_type=MESH)

        if pair:
            @pl.when(i == steps - 1)
            def _():
                for q in range(4):
                    mine_buf[q] = acc[rows_for(q, 1 - lax.axis_index("c")), :].astype(BF16)
                    swap(q).start()

        if want_dx:
            dh = _nn(dp_ref[...], w_ref[...])
            dg[...] += jnp.sum(dh * xn, axis=0, keepdims=True)
            dhn = dh * gv
            dx_ref[...] = dxo_ref[...] + r * (dhn - xn * jnp.mean(dhn * xn, axis=-1, keepdims=True))

        @pl.when(i == steps - 1)
        def _():
            if pair:
                x, y, c = _mesh_pos()
                for q in range(4):
                    swap(q).wait()
                for j, q in enumerate([2 * (1 - x) + y, 2 * x + (1 - y), 2 * (1 - x) + (1 - y)]):
                    hs_ref[j] = (acc[rows_for(q, c), :] + theirs_buf[q].astype(F32)).astype(BF16)
                hm_ref[...] = acc[rows_for(2 * x + y, c), :] + theirs_buf[2 * x + y].astype(F32)
            elif want_dw:
                dw_ref[...] = acc[...].astype(BF16)
            if want_dx:
                _rows_of(dg, dgo_ref, 0)

    row = lambda c: pl.BlockSpec((tm, c), lambda i: (i, 0))
    const = lambda shape: pl.BlockSpec(shape, lambda i: (0,) * len(shape))
    in_specs = [const((8, 128)), row(D_IN), row(D_MODEL), const((DEPTH, D_MODEL))]
    operands = [token, dproj, x, norm_pre]
    out_specs, out_shape, scratch = [], [], []
    if want_dx:
        in_specs += [row(D_MODEL), _resident((D_IN, D_MODEL))]
        operands += [dxo, w_in_t]
        out_specs += [row(D_MODEL), const((8, 128))]
        out_shape += [jax.ShapeDtypeStruct((SEQ, D_MODEL), F32), jax.ShapeDtypeStruct((8, 128), F32)]
    if pair:
        out_specs += [const((3, IN_SHARD, D_MODEL)), const((IN_SHARD, D_MODEL))]
        out_shape += [jax.ShapeDtypeStruct((3, IN_SHARD, D_MODEL), BF16), jax.ShapeDtypeStruct((IN_SHARD, D_MODEL), F32)]
        scratch += [pltpu.VMEM((D_IN, D_MODEL), F32), pltpu.VMEM((4, IN_SHARD, D_MODEL), BF16),
                    pltpu.VMEM((4, IN_SHARD, D_MODEL), BF16), pltpu.SemaphoreType.DMA((4,)), pltpu.SemaphoreType.DMA((4,))]
    elif want_dw:
        out_specs.append(const((D_IN, D_MODEL)))
        out_shape.append(jax.ShapeDtypeStruct((D_IN, D_MODEL), BF16))
        scratch.append(pltpu.VMEM((D_IN, D_MODEL), F32))
    if want_dx:
        scratch.append(pltpu.VMEM((1, D_MODEL), F32))
    params = pltpu.CompilerParams(dimension_semantics=("arbitrary",), vmem_limit_bytes=VMEM_LIMIT,
                                  collective_id=COLLECTIVE_PAIR_SUM[layer] if pair else None)
    return pl.pallas_call(
        body, name=f"bwd_in_{part}{layer}", grid=(steps,),
        in_specs=in_specs, out_specs=out_specs, out_shape=out_shape, scratch_shapes=scratch,
        compiler_params=params,
    )(*operands)


def _mesh_pos():
    return lax.axis_index("x"), lax.axis_index("y"), lax.axis_index("c")


def _device_rows(ref, m, px, py, pc):
    return ref.at[pl.ds(pl.multiple_of((4 * px + 2 * py + pc) * m, 16 if m % 16 == 0 else 8), m), :]


def _allgather(srcs, out_dtype, name, later=()):
    na, nb = len(srcs), len(later)
    every = list(srcs) + list(later)
    shapes = [(a.shape[-2], a.shape[-1]) for a, _ in every]

    def body(*refs):
        xs, refs = refs[:na + nb], refs[na + nb:]
        outs, cast, land, refs = refs[:na], refs[na:na + nb], refs[na + nb:na + 2 * nb], refs[na + 2 * nb:]
        stage, raw, (send_sems, recv_sems, local_sems, load_sems) = refs[:na], refs[na:2 * na + nb], refs[2 * na + nb:]
        loads = [pltpu.make_async_copy(xs[i].at[every[i][1]], raw[i], load_sems.at[i]) for i in range(na + nb)]
        for cp in loads:
            cp.start()
        x, y, c = _mesh_pos()
        me, sibling = (x, y, c), (x, y, 1 - c)
        near = [(1 - x, y), (x, 1 - y)]
        far = (1 - x, 1 - y)
        relay_from, relay_to = (x ^ (1 - c), y ^ c), (x ^ c, y ^ (1 - c))
        _handshake([sibling] + [(*chip, c) for chip in near])
        k_from, k_to = 1 + c, 2 - c

        def slot(a, px, py, pc):
            return _device_rows(outs[a], shapes[a][0], px, py, pc)

        def copy(a, k, block, to, src=None):
            return pltpu.make_async_remote_copy(
                src_ref=slot(a, *block) if src is None else src, dst_ref=slot(a, *block),
                send_sem=send_sems.at[a, k], recv_sem=recv_sems.at[a, k], device_id=to, device_id_type=MESH)

        def cast_block(i):
            loads[i].wait()
            return raw[i][...].astype(out_dtype)

        for a in range(na):
            stage[a][...] = cast_block(a)
        mine = [pltpu.make_async_copy(stage[a], slot(a, *me), local_sems.at[a]) for a in range(na)]
        for cp in mine:
            cp.start()
        sent = []
        for a in range(na):
            sent.append(copy(a, 0, me, sibling, src=stage[a]))
            sent += [copy(a, 1 + j, me, (*chip, c), src=stage[a]) for j, chip in enumerate(near)]
        for cp in sent:
            cp.start()
        for b in range(nb):
            cast[b][...] = cast_block(na + b)
            cp = pltpu.make_async_copy(cast[b], _device_rows(land[b], shapes[na + b][0], *me), local_sems.at[na + b])
            cp.start()
            mine.append(cp)
        for a in range(na):
            copy(a, k_from, (*relay_from, c), me).wait_recv()
            sent += [copy(a, 3, (*relay_from, c), (*relay_to, c)), copy(a, 3 + k_from, (*relay_from, c), sibling)]
            sent[-2].start()
            sent[-1].start()
        for a in range(na):
            copy(a, k_to, (*relay_to, c), me).wait_recv()
            sent.append(copy(a, 3 + k_to, (*relay_to, c), sibling))
            sent[-1].start()
        for a in range(na):
            copy(a, 3, (*far, c), me).wait_recv()
            sent.append(copy(a, 6, (*far, c), sibling))
            sent[-1].start()
        for a in range(na):
            copy(a, 0, sibling, me).wait_recv()
            for j, chip in enumerate(near + [far]):
                copy(a, 4 + j, (*chip, 1 - c), me).wait_recv()
        for cp in sent:
            cp.wait_send()
        for cp in mine:
            cp.wait()

    vmem = pl.BlockSpec(memory_space=pltpu.VMEM)
    hbm = pl.BlockSpec(memory_space=pl.ANY)
    gathered = [jax.ShapeDtypeStruct((N_DEV * m, n), out_dtype) for m, n in shapes]
    out = pl.pallas_call(
        body, name=name,
        in_specs=[hbm] * (na + nb),
        out_specs=[hbm] * na + [vmem] * nb + [hbm] * nb,
        out_shape=gathered[:na] + [jax.ShapeDtypeStruct(s, out_dtype) for s in shapes[na:]] + gathered[na:],
        scratch_shapes=([pltpu.VMEM(s, out_dtype) for s in shapes[:na]]
                        + [pltpu.VMEM(s, a.dtype) for s, (a, _) in zip(shapes, every)]
                        + [pltpu.SemaphoreType.DMA((na, 7)), pltpu.SemaphoreType.DMA((na, 7)),
                           pltpu.SemaphoreType.DMA((na + nb,)), pltpu.SemaphoreType.DMA((na + nb,))]),
        compiler_params=pltpu.CompilerParams(vmem_limit_bytes=VMEM_LIMIT, collective_id=COLLECTIVE_GATHER_W0),
    )(*[a for a, _ in every])
    return out[:na], out[na:na + nb], out[na + nb:]


ALL_PEERS = tuple(range(1, N_DEV))
SIBLING_AND_SAME_CORE = (1, 2, 4, 6)


def _related(k, x, y, c):
    return x ^ ((k >> 2) & 1), y ^ ((k >> 1) & 1), c ^ (k & 1)


def _gather_start(blocks, lands, relations, collective_id, name):
    na = len(blocks)

    def body(*refs):
        src, land, sems, token = refs[:na], refs[na:2 * na], refs[2 * na:4 * na], refs[-1]
        x, y, c = _mesh_pos()
        _handshake([_related(k, x, y, c) for k in sorted(set().union(*relations))])
        for a in range(na):
            for k in relations[a]:
                pltpu.make_async_remote_copy(
                    src_ref=src[a], dst_ref=_device_rows(land[a], blocks[a].shape[0], x, y, c),
                    send_sem=sems[2 * a].at[k - 1], recv_sem=sems[2 * a + 1].at[k - 1],
                    device_id=_related(k, x, y, c), device_id_type=MESH).start()
        token[...] = jnp.zeros_like(token)

    bufs = [pltpu.HBM(t.shape, t.dtype) for t in list(blocks) + list(lands)]
    out = pl.pallas_call(
        body, name=name,
        out_shape=(*([pltpu.SemaphoreType.DMA((N_DEV - 1,))] * (2 * na)), *bufs, jax.ShapeDtypeStruct((8, 128), F32)),
        in_specs=[_HBM] * (2 * na),
        out_specs=(*([_SEM] * (2 * na)), *([_HBM] * (2 * na)), pl.BlockSpec(memory_space=pltpu.VMEM)),
        input_output_aliases={i: 2 * na + i for i in range(2 * na)},
        compiler_params=pltpu.CompilerParams(has_side_effects=_EFFECT, collective_id=collective_id),
    )(*[pltpu.with_memory_space_constraint(t, pltpu.HBM) for t in list(blocks) + list(lands)])
    sems = [(out[2 * a], out[2 * a + 1]) for a in range(na)]
    return sems, out[2 * na:3 * na], out[3 * na:4 * na], out[-1]


def _gather_wait(sems, block, land, relations, after, name):
    def body(src, land_ref, send_sem, recv_sem, after_ref, src_out, land_out):
        x, y, c = _mesh_pos()
        for k in relations:
            peer = _related(k, x, y, c)
            cp = pltpu.make_async_remote_copy(
                src_ref=src, dst_ref=_device_rows(land_ref, block.shape[0], *peer),
                send_sem=send_sem.at[k - 1], recv_sem=recv_sem.at[k - 1], device_id=peer, device_id_type=MESH)
            cp.wait_send()
            cp.wait_recv()

    out = pl.pallas_call(
        body, name=name,
        out_shape=(pltpu.HBM(block.shape, block.dtype), pltpu.HBM(land.shape, land.dtype)),
        in_specs=[_HBM, _HBM, _SEM, _SEM, pl.BlockSpec(memory_space=pl.ANY)],
        out_specs=[_HBM, _HBM],
        input_output_aliases={0: 0, 1: 1},
        compiler_params=pltpu.CompilerParams(has_side_effects=_EFFECT),
    )(block, land, sems[0], sems[1], after)
    return out[1]


(COLLECTIVE_GATHER_W0, COLLECTIVE_GATHER_W1, COLLECTIVE_FORWARD_W_IN1, COLLECTIVE_EXCHANGE_1, COLLECTIVE_EXCHANGE_0A,
 COLLECTIVE_EXCHANGE_0B, COLLECTIVE_GATHER_SMALL) = range(1, 8)
COLLECTIVE_PAIR_SUM = (8, 9)


def _handshake(peers):
    barrier = pltpu.get_barrier_semaphore()
    for peer in peers:
        pl.semaphore_signal(barrier, inc=1, device_id=peer, device_id_type=MESH)
    pl.semaphore_wait(barrier, len(peers))


def _forward_plan(land_ref, m):
    x, y, c = _mesh_pos()
    return [_device_rows(land_ref, m, qx, qy, c) for qx, qy in ((1 - x, y), (x, 1 - y), (1 - x, 1 - y))], (x, y, 1 - c)


def _forward_start(land, m, name):
    def body(land_ref, send_sem, recv_sem, land_out, token):
        _handshake([(lax.axis_index("x"), lax.axis_index("y"), 1 - lax.axis_index("c"))])
        rows, sibling = _forward_plan(land_ref, m)
        for j, r in enumerate(rows):
            pltpu.make_async_remote_copy(src_ref=r, dst_ref=r, send_sem=send_sem.at[j], recv_sem=recv_sem.at[j],
                                         device_id=sibling, device_id_type=MESH).start()
        token[...] = jnp.zeros_like(token)

    out = pl.pallas_call(
        body, name=name,
        out_shape=(pltpu.SemaphoreType.DMA((3,)), pltpu.SemaphoreType.DMA((3,)), pltpu.HBM(land.shape, land.dtype),
                   jax.ShapeDtypeStruct((8, 128), F32)),
        in_specs=[_HBM],
        out_specs=(_SEM, _SEM, _HBM, pl.BlockSpec(memory_space=pltpu.VMEM)),
        input_output_aliases={0: 2},
        compiler_params=pltpu.CompilerParams(has_side_effects=_EFFECT, collective_id=COLLECTIVE_FORWARD_W_IN1),
    )(pltpu.with_memory_space_constraint(land, pltpu.HBM))
    return (out[0], out[1]), out[2], out[3]


def _forward_wait(sems, land, m, after, name):
    def body(land_ref, send_sem, recv_sem, after_ref, land_out):
        x, y, c = _mesh_pos()
        mine, sibling = _forward_plan(land_ref, m)
        theirs = [_device_rows(land_ref, m, qx, qy, 1 - c) for qx, qy in ((1 - x, y), (x, 1 - y), (1 - x, 1 - y))]
        for j in range(3):
            cp = pltpu.make_async_remote_copy(src_ref=mine[j], dst_ref=theirs[j], send_sem=send_sem.at[j],
                                              recv_sem=recv_sem.at[j], device_id=sibling, device_id_type=MESH)
            cp.wait_send()
            cp.wait_recv()

    return pl.pallas_call(
        body, name=name,
        out_shape=pltpu.HBM(land.shape, land.dtype),
        in_specs=[_HBM, _SEM, _SEM, pl.BlockSpec(memory_space=pl.ANY)],
        out_specs=_HBM,
        input_output_aliases={0: 0},
        compiler_params=pltpu.CompilerParams(has_side_effects=_EFFECT),
    )(land, sems[0], sems[1], after)


_HBM = pl.BlockSpec(memory_space=pltpu.HBM)
_SEM = pl.BlockSpec(memory_space=pltpu.SEMAPHORE)
_EFFECT = pltpu.SideEffectType.DATAFLOW_SIDE_EFFECTING


def _exchange_plan(direct):
    x, y, c = _mesh_pos()
    if not direct:
        return [(j, j, (qx, qy, c)) for j, (qx, qy) in enumerate([(1 - x, y), (x, 1 - y), (1 - x, 1 - y)])]
    plan = []
    for k in range(1, N_DEV):
        px, py, pc = x ^ ((k >> 2) & 1), y ^ ((k >> 1) & 1), c ^ (k & 1)
        plan.append((4 * px + 2 * py + pc, k - 1, (px, py, pc)))
    return plan


def _exchange_copies(directs):
    copies, base = [], 0
    for a, direct in enumerate(directs):
        plan = _exchange_plan(direct)
        copies += [(a, block, slot, peer, base + slot) for block, slot, peer in plan]
        base += len(plan)
    return copies, base


def _exchange_start(srcs, directs, collective_id, name):
    na = len(srcs)
    slots = [N_DEV - 1 if direct else 3 for direct in directs]

    def body(*refs):
        src, land = refs[:na], refs[na:2 * na]
        send_sem, recv_sem = refs[2 * na], refs[2 * na + 1]
        token = refs[-1]
        _handshake([peer for _, _, peer in _exchange_plan(any(directs))])
        for a, block, slot, peer, sem in _exchange_copies(directs)[0]:
            pltpu.make_async_remote_copy(
                src_ref=src[a].at[block], dst_ref=land[a].at[slot], send_sem=send_sem.at[sem],
                recv_sem=recv_sem.at[sem], device_id=peer, device_id_type=MESH).start()
        token[...] = jnp.zeros_like(token)

    zones = [jax.ShapeDtypeStruct((n,) + t.shape[1:], t.dtype) for n, t in zip(slots, srcs)]
    bufs = [pltpu.HBM(t.shape, t.dtype) for t in list(srcs) + zones]
    out = pl.pallas_call(
        body, name=name,
        out_shape=(pltpu.SemaphoreType.DMA((sum(slots),)), pltpu.SemaphoreType.DMA((sum(slots),)), *bufs,
                   jax.ShapeDtypeStruct((8, 128), F32)),
        in_specs=[_HBM] * (2 * na),
        out_specs=(_SEM, _SEM, *([_HBM] * (2 * na)), pl.BlockSpec(memory_space=pltpu.VMEM)),
        input_output_aliases={i: 2 + i for i in range(2 * na)},
        compiler_params=pltpu.CompilerParams(has_side_effects=_EFFECT, collective_id=collective_id),
    )(*[pltpu.with_memory_space_constraint(t, pltpu.HBM) for t in srcs],
      *[pltpu.with_memory_space_constraint(lax.empty(t.shape, t.dtype), pltpu.HBM) for t in zones])
    return out[0], out[1], out[2:2 + na], out[2 + na:2 + 2 * na], out[-1]


def _exchange_wait(send_sem, recv_sem, srcs, lands, directs, after, name):
    na = len(srcs)

    def body(*refs):
        src, land = refs[:na], refs[na:2 * na]
        send_sem_ref, recv_sem_ref = refs[2 * na], refs[2 * na + 1]
        for a, block, slot, peer, sem in _exchange_copies(directs)[0]:
            cp = pltpu.make_async_remote_copy(
                src_ref=src[a].at[block], dst_ref=land[a].at[slot], send_sem=send_sem_ref.at[sem],
                recv_sem=recv_sem_ref.at[sem], device_id=peer, device_id_type=MESH)
            cp.wait_send()
            cp.wait_recv()

    bufs = [pltpu.HBM(t.shape, t.dtype) for t in list(srcs) + list(lands)]
    out = pl.pallas_call(
        body, name=name,
        out_shape=tuple(bufs),
        in_specs=[_HBM] * (2 * na) + [_SEM, _SEM, pl.BlockSpec(memory_space=pl.ANY)],
        out_specs=[_HBM] * (2 * na),
        input_output_aliases={i: i for i in range(2 * na)},
        compiler_params=pltpu.CompilerParams(has_side_effects=_EFFECT),
    )(*srcs, *lands, send_sem, recv_sem, after)
    return out[:na], out[na:]


def _own_then_slots(mine_ref, lands_ref, rows=slice(None)):
    if len(mine_ref.shape) == 3:
        x, y, c = _mesh_pos()
        total = mine_ref[4 * x + 2 * y + c, rows, :].astype(F32)
    else:
        total = mine_ref[rows, :].astype(F32)
    for j in range(lands_ref.shape[0]):
        total = total + lands_ref[j, rows, :].astype(F32)
    return total


SMALL_ROWS = 2 * PACK_SLICE + 2 * 8


def _small_block(mine, lands, dgpre, name):
    def body(*refs):
        hm, ld, dg = refs[:DEPTH], refs[DEPTH:2 * DEPTH], refs[2 * DEPTH:3 * DEPTH]
        blk, land, sem = refs[3 * DEPTH:]
        for l in range(DEPTH):
            blk[PACK_SLICE * l:PACK_SLICE * (l + 1), :] = _own_then_slots(hm[l], ld[l])
            blk[2 * PACK_SLICE + 8 * l:2 * PACK_SLICE + 8 * (l + 1), :] = dg[l][...]
        cp = pltpu.make_async_copy(blk, _device_rows(land, SMALL_ROWS, *_mesh_pos()), sem)
        cp.start()
        cp.wait()

    vmem = pl.BlockSpec(memory_space=pltpu.VMEM)
    return pl.pallas_call(
        body, name=name,
        in_specs=[vmem] * (3 * DEPTH), out_specs=[vmem, pl.BlockSpec(memory_space=pl.ANY)],
        out_shape=[jax.ShapeDtypeStruct((SMALL_ROWS, 128), F32), jax.ShapeDtypeStruct((N_DEV * SMALL_ROWS, 128), F32)],
        scratch_shapes=[pltpu.SemaphoreType.DMA],
        compiler_params=_compiler_params(),
    )(*mine, *lands, *dgpre)


def _adamw_math(w, g, m, v):
    m = ADAM_B1 * m + (1.0 - ADAM_B1) * g
    v = ADAM_B2 * v + (1.0 - ADAM_B2) * (g * g)
    m_hat = m / (1.0 - ADAM_B1 ** ADAM_STEP)
    v_hat = v / (1.0 - ADAM_B2 ** ADAM_STEP)
    delta = -ADAM_LR * (m_hat / (jnp.sqrt(v_hat) + ADAM_EPS) + ADAM_WD * w)
    return delta, m, v


def _adamw_layer(layer, mine, lands, w, m, v, earlier, token, name, rows):
    _, mm, nn = w.shape

    def body(hm_ref, ld_ref, w_ref, m_ref, v_ref, _, *refs):
        g_ref, d_ref, nm_ref, nv_ref = refs[-4:]
        g = _own_then_slots(hm_ref, ld_ref)
        g_ref[...] = g
        d, nm, nv = _adamw_math(w_ref[...], g, m_ref[...], v_ref[...])
        d_ref[...] = d
        nm_ref[...] = nm
        nv_ref[...] = nv

    spec = pl.BlockSpec((None, rows, nn), lambda i: (layer, i, 0))
    carried = [] if earlier is None else list(earlier)
    return pl.pallas_call(
        body, name=name, grid=(mm // rows,),
        in_specs=([pl.BlockSpec((rows, nn), lambda i: (i, 0)) if mine.ndim == 2
                   else pl.BlockSpec((N_DEV, rows, nn), lambda i: (0, i, 0)),
                   pl.BlockSpec((lands.shape[0], rows, nn), lambda i: (0, i, 0)),
                   spec, spec, spec] + [pl.BlockSpec(memory_space=pl.ANY)] * (1 + len(carried))),
        out_specs=[spec] * 4,
        out_shape=[jax.ShapeDtypeStruct(w.shape, F32)] * 4,
        input_output_aliases={6 + t: t for t in range(len(carried))},
        compiler_params=_compiler_params(("arbitrary",)),
    )(mine, lands, w, m, v, token, *carried)


def _adamw_small(gathered, params):
    def body(all_ref, *refs):
        ins, outs, packs = refs[:15], refs[15:15 + 21], refs[15 + 21]
        loss_ref = outs[0]
        for dev in range(N_DEV):
            for l in range(DEPTH):
                packs[l, PACK_SLICE * dev:PACK_SLICE * (dev + 1), :] = (
                    all_ref[SMALL_ROWS * dev + PACK_SLICE * l:SMALL_ROWS * dev + PACK_SLICE * (l + 1), :])
        loss_ref[...] = packs[DEPTH - 1, ROW_LOSS:ROW_LOSS + 1, 0:1]

        def update(p, sel, g):
            w_ref, m_ref, v_ref = ins[p], ins[5 + p], ins[10 + p]
            d, nm, nv = _adamw_math(w_ref[sel], g, m_ref[sel], v_ref[sel])
            for t, val in enumerate((g, d, nm, nv)):
                outs[1 + 5 * t + p][sel] = val

        for l in range(DEPTH):
            gp = packs.at[l]
            row0 = 2 * PACK_SLICE + 8 * l
            dgpre = all_ref[row0:row0 + 8, :]
            for dev in range(1, N_DEV):
                dgpre = dgpre + all_ref[SMALL_ROWS * dev + row0:SMALL_ROWS * dev + row0 + 8, :]
            for grp in range(4):
                update(0, (l, grp), gp[ROW_PW + BLOCK * grp:ROW_PW + BLOCK * (grp + 1), :])
                update(1, (slice(l, l + 1), slice(128 * grp, 128 * (grp + 1))), gp[ROW_SC + grp:ROW_SC + grp + 1, :])
            update(2, (slice(l, l + 1), slice(None)), gp[ROW_SINK:ROW_SINK + 1, 0:N_HEADS])
            for r in range(D_MODEL // 128):
                sel = (slice(l, l + 1), slice(128 * r, 128 * (r + 1)))
                update(3, sel, dgpre[r:r + 1, :])
                update(4, sel, gp[ROW_NPOST + r:ROW_NPOST + r + 1, :])

    shapes = [jax.ShapeDtypeStruct(p.shape, F32) for p in params[:5]]
    return pl.pallas_call(
        body, name="adamw_small",
        out_shape=[jax.ShapeDtypeStruct((1, 1), F32)] + shapes * 4,
        scratch_shapes=[pltpu.VMEM((DEPTH, PACK_ROWS, 128), F32)],
        compiler_params=_compiler_params(),
    )(gathered, *params)


def kernel(x, w_in, pool_w, pool_scale, attn_sinks, w_out, norm_pre, norm_post, loss_target, m_w_in, m_pool_w, m_pool_scale, m_attn_sinks, m_w_out, m_norm_pre, m_norm_post, v_w_in, v_pool_w, v_pool_scale, v_attn_sinks, v_w_out, v_norm_pre, v_norm_post):
    x0 = x.reshape(SEQ, D_MODEL)
    target = loss_target.reshape(SEQ, D_MODEL)
    bias = jnp.asarray(_attn_bias())
    w_in_t, m_in_t, v_in_t = (jnp.swapaxes(t, 1, 2) for t in (w_in, m_w_in, v_w_in))

    (win0, wout0), later, lands = _allgather([(w_in_t, 0), (w_out, 0)], BF16, "gather_w0",
                                              later=[(w_in_t, 1), (w_out, 1)])
    sems, later, lands, token = _gather_start(later, lands, [SIBLING_AND_SAME_CORE, ALL_PEERS], COLLECTIVE_GATHER_W1,
                                              "gather_w1_start")
    win_full, wout_full = [win0, None], [wout0, None]

    saved = []
    xl = x0
    for layer in range(DEPTH):
        u, pg, q, k, v, ag, z, a = _fwd_front(layer, xl, norm_pre, win_full[layer], token, attn_sinks,
                                             pool_w, pool_scale, bias)
        if layer == 0:
            land = _gather_wait(sems[0], later[0], lands[0], SIBLING_AND_SAME_CORE, z, "gather_w_in1_wait")
            fsems, land, token = _forward_start(land, IN_SHARD, "forward_w_in1_start")
        else:
            wout_full[layer] = _gather_wait(sems[1], later[1], lands[1], ALL_PEERS, z, "gather_w_out1_wait")
        x_next, y = _fwd_out(layer, z, xl, norm_post, wout_full[layer], token)
        if layer == 0:
            win_full[1] = _forward_wait(fsems, land, IN_SHARD, x_next, "forward_w_in1_wait")
        saved.append((xl, u, pg, q, k, v, ag, z, a, y))
        xl = x_next

    params_small = [pool_w, pool_scale, attn_sinks, norm_pre, norm_post,
                    m_pool_w, m_pool_scale, m_attn_sinks, m_norm_pre, m_norm_post,
                    v_pool_w, v_pool_scale, v_attn_sinks, v_norm_pre, v_norm_post]

    def start(srcs, directs, paired, collective_id, tag):
        send_sem, recv_sem, srcs, lands, started = _exchange_start(srcs, directs, collective_id, f"exchange_start{tag}")
        return (send_sem, recv_sem, srcs, lands, paired, directs), started

    def finish(handle, after, tag):
        send_sem, recv_sem, srcs, lands, paired, directs = handle
        srcs, lands = _exchange_wait(send_sem, recv_sem, srcs, lands, directs, after, f"exchange_wait{tag}")
        return [s if p is None else p for s, p in zip(srcs, paired)], lands

    def back(layer, top, first, second):
        xin, u, pg, q, k, v, ag, z, a, y = saved[layer]
        return _bwd_back(layer, top, first, second, y, z, norm_post, wout_full[layer], attn_sinks, u, pg, q, k, v,
                         ag, a, pool_w, pool_scale, bias)

    dgpre = [None] * DEPTH
    dx, dproj, gw_out, pack = back(1, True, xl, target)
    dx, dgpre[1], chip_sums, own_sum = _bwd_in(1, "both_pair", token, dproj, saved[1][0], norm_pre, dx, win_full[1])
    top, token = start([chip_sums, gw_out.reshape(N_DEV, OUT_SHARD, D_MODEL), pack.reshape(N_DEV, PACK_SLICE, 128)],
                       [False, True, True], [own_sum, None, None], COLLECTIVE_EXCHANGE_1, "1")

    dproj, gw_out, pack = back(0, False, dx, token)
    early, token = start([gw_out.reshape(N_DEV, OUT_SHARD, D_MODEL), pack.reshape(N_DEV, PACK_SLICE, 128)],
                         [True, True], [None, None], COLLECTIVE_EXCHANGE_0A, "0a")
    chip_sums, own_sum = _bwd_in(0, "dw_pair", token, dproj, saved[0][0], norm_pre)
    late, token = start([chip_sums], [False], [own_sum], COLLECTIVE_EXCHANGE_0B, "0b")
    dx, dgpre[0] = _bwd_in(0, "dx", token, dproj, saved[0][0], norm_pre, dx, win_full[0])

    own1, lands1 = finish(top, dx, "1")
    big_in = _adamw_layer(1, own1[0], lands1[0], w_in_t, m_in_t, v_in_t, None, token, "adamw_in1", ADAM_ROWS_IN)
    big_out = _adamw_layer(1, own1[1], lands1[1], w_out, m_w_out, v_w_out, None, token, "adamw_out1", ADAM_ROWS_OUT)
    own0a, lands0a = finish(early, big_out[0], "0a")
    block, land = _small_block([own0a[1], own1[2]], [lands0a[1], lands1[2]], dgpre, "small_block")
    sems, block, land, token = _gather_start([block], [land], [ALL_PEERS], COLLECTIVE_GATHER_SMALL, "gather_small_start")
    big_out = _adamw_layer(0, own0a[0], lands0a[0], w_out, m_w_out, v_w_out, big_out, token, "adamw_out0", ADAM_ROWS_OUT)
    own0b, lands0b = finish(late, big_out[0], "0b")
    big_in = _adamw_layer(0, own0b[0], lands0b[0], w_in_t, m_in_t, v_in_t, big_in, token, "adamw_in0", ADAM_ROWS_IN)
    gathered = _gather_wait(sems[0], block[0], land[0], ALL_PEERS, big_in[0], "gather_small_wait")
    small_out = _adamw_small(gathered, params_small)
    loss = small_out[0].reshape(())

    outs = [loss, dx.reshape(1, SEQ, D_MODEL)]
    for t in range(4):
        pw_, sc_, sk_, npre_, npost_ = small_out[1 + 5 * t:6 + 5 * t]
        outs += [jnp.swapaxes(big_in[t], 1, 2), pw_, sc_, sk_, big_out[t], npre_, npost_]
    return tuple(outs)
```

```python
import numpy as np
import jax
import jax.numpy as jnp
from jax import lax
from jax.experimental import pallas as pl
from jax.experimental.pallas import tpu as pltpu

F32 = jnp.float32
BF16 = jnp.bfloat16

N_DEV = 8
SEQ = 2048
D_MODEL = 1024
D_POOL = 512
D_ATTN = 512
D_KV = 128
D_IN = 2304
N_HEADS = 8
GQA = 4
HEAD_DIM = 64
BLOCK = 128
POOL_WINDOWS = (2, 4, 8, 16)
DEPTH = 2
EPS = 1e-6
NEG_INF = -1e30
SCALE = HEAD_DIM ** -0.5
IN_SHARD = D_IN // N_DEV
OUT_SHARD = D_MODEL // N_DEV

COL_U, COL_PG, COL_Q, COL_K, COL_V, COL_AG = 0, 512, 1024, 1536, 1664, 1792

ADAM_LR = 0.001
ADAM_B1 = 0.9
ADAM_B2 = 0.999
ADAM_EPS = 1e-08
ADAM_WD = 0.01
ADAM_STEP = 10

TOKEN_TILE = 512
FWD_OUT_TILE = 1024
ADAM_ROWS_IN, ADAM_ROWS_OUT = 144, 64
VMEM_LIMIT = 56 * 1024 * 1024
MESH = pl.DeviceIdType.MESH

ROW_PW, ROW_SC, ROW_SINK, ROW_NPOST, ROW_LOSS = 0, 512, 520, 536, 544
PACK_ROWS = 576
PACK_SLICE = PACK_ROWS // N_DEV


def _nn(a, b):
    return jnp.dot(a, b, preferred_element_type=F32)


def _nt(a, b):
    return lax.dot_general(a, b, (((1,), (1,)), ((), ())), preferred_element_type=F32)


def _tn(a, b):
    return lax.dot_general(a, b, (((0,), (0,)), ((), ())), preferred_element_type=F32)


def _silu_parts(g):
    s = jax.nn.sigmoid(g)
    return g * s, s * (1.0 + g * (1.0 - s))


def _resident(shape):
    return pl.BlockSpec(shape, lambda *_: (0,) * len(shape), pipeline_mode=pl.Buffered(1))


def _compiler_params(sem=None):
    if sem is None:
        return pltpu.CompilerParams(vmem_limit_bytes=VMEM_LIMIT)
    return pltpu.CompilerParams(dimension_semantics=sem, vmem_limit_bytes=VMEM_LIMIT)


def _attn_bias():
    t = np.arange(BLOCK)[None, :]
    j = np.arange(BLOCK)[:, None]
    current = j <= t
    dist = np.where(current, t - j, t + BLOCK - j).astype(np.float32)
    out = np.zeros((2, 2, BLOCK, GQA * BLOCK), np.float32)
    for variant in range(2):
        valid = current | (variant == 1)
        for kv in range(2):
            for g in range(GQA):
                slope = np.float32(2.0 ** (-(kv * GQA + g + 1)))
                out[variant, kv, :, g * BLOCK:(g + 1) * BLOCK] = np.where(valid, -slope * dist, np.float32(NEG_INF))
    return out


def _replicate_head(kx, kv):
    rolled = pltpu.roll(kx, 64, 1)
    lane = lax.broadcasted_iota(jnp.int32, kx.shape, 1)
    dup = jnp.where(lane < 64, kx, rolled) if kv == 0 else jnp.where(lane < 64, rolled, kx)
    return jnp.concatenate([dup, dup], axis=1).astype(BF16)


def _stack_heads(qv):
    lane = lax.broadcasted_iota(jnp.int32, qv.shape, 1)
    zero = jnp.zeros_like(qv)
    return jnp.concatenate([jnp.where((lane >= 64 * g) & (lane < 64 * g + 64), qv, zero) for g in range(GQA)], axis=0)


def _unstack_heads(xs):
    lane = lax.broadcasted_iota(jnp.int32, (BLOCK, 256), 1)
    return jnp.where(lane < 64, xs[0:128], jnp.where(lane < 128, xs[128:256], jnp.where(lane < 192, xs[256:384], xs[384:512])))


def _fold_heads(r):
    h = r[:, 0:128] + r[:, 128:256]
    return h + pltpu.roll(h, 64, 1)


def _sink_row(sink_ref, layer, kv):
    lane = lax.broadcasted_iota(jnp.int32, (1, GQA * BLOCK), 1)
    s4 = [sink_ref[layer, kv * GQA + g] for g in range(GQA)]
    return jnp.where(lane < 128, s4[0], jnp.where(lane < 256, s4[1], jnp.where(lane < 384, s4[2], s4[3])))


def _band_is_current():
    j = lax.broadcasted_iota(jnp.int32, (BLOCK, GQA * BLOCK), 0)
    t = lax.broadcasted_iota(jnp.int32, (BLOCK, GQA * BLOCK), 1) & (BLOCK - 1)
    return j <= t


def _pack_band(full, current):
    return jnp.where(current, full[BLOCK:], full[:BLOCK])


def _unpack_band(packed, current):
    zero = jnp.zeros_like(packed)
    return jnp.concatenate([jnp.where(current, zero, packed), jnp.where(current, packed, zero)], axis=0)


def _probs_keys_major(k_rep, q_st, bias, sink, current):
    st = _pack_band(_nt(k_rep, q_st), current) * SCALE + bias
    m = jnp.maximum(jnp.max(st, axis=0, keepdims=True), sink)
    p = jnp.exp(st - m)
    esink = jnp.exp(sink - m)
    rl = 1.0 / (jnp.sum(p, axis=0, keepdims=True) + esink)
    return p * rl, esink * rl


WINDOW_HALO = 16


def _window_sum(ext, w, forward):
    s = ext
    sh = 1
    while sh < w:
        s = s + pltpu.roll(s, (ext.shape[0] - sh) if forward else sh, 0)
        sh *= 2
    return s


def _inv_count(n, w):
    t = n * BLOCK + lax.broadcasted_iota(jnp.int32, (BLOCK, 1), 0) + 1
    return 1.0 / jnp.minimum(t.astype(F32), float(w))


def _kv_ext(ref, n):
    r0 = pl.multiple_of(jnp.maximum(n - 1, 0) * BLOCK, BLOCK)
    r1 = pl.multiple_of(n * BLOCK, BLOCK)
    return jnp.concatenate([ref[pl.ds(r0, BLOCK), :], ref[pl.ds(r1, BLOCK), :]], axis=0)


def _rows_of(vec_ref, pack_ref, row0):
    for r in range(D_MODEL // 128):
        pack_ref[row0 + r:row0 + r + 1, :] = vec_ref[:, 128 * r:128 * (r + 1)]


FRONT_TILE = 4 * BLOCK


def _fwd_front(layer, x, norm_pre, w_in_t, token, sinks, pool_w, pool_scale, bias):
    tm = FRONT_TILE

    def body(sink_ref, x_ref, g_ref, w_ref, _, pw_ref, sc_ref, bias_ref,
             u_ref, pg_ref, q_ref, k_ref, v_ref, ag_ref, z_ref, a_ref, uprev, kprev, vprev):
        i = pl.program_id(0)

        @pl.when(i == 0)
        def _():
            uprev[...] = jnp.zeros_like(uprev)
            kprev[...] = jnp.zeros_like(kprev)
            vprev[...] = jnp.zeros_like(vprev)

        xv = x_ref[...]
        r = lax.rsqrt(jnp.mean(xv * xv, axis=-1, keepdims=True) + EPS)
        h = (xv * r * g_ref[layer:layer + 1, :]).astype(BF16)
        u_ref[...] = _nt(h, w_ref[COL_U:COL_PG, :])
        pg_ref[...] = _nt(h, w_ref[COL_PG:COL_Q, :])
        for sb in range(tm // BLOCK):
            n = (tm // BLOCK) * i + sb
            rows = slice(BLOCK * sb, BLOCK * (sb + 1))
            before = slice(BLOCK * (sb - 1), BLOCK * sb)
            uv = u_ref[rows, :]
            halo = (uprev[BLOCK - WINDOW_HALO:, :] if sb == 0
                    else u_ref[BLOCK * sb - WINDOW_HALO:BLOCK * sb, :])
            ext = jnp.concatenate([halo, uv], axis=0)
            for g, w in enumerate(POOL_WINDOWS):
                cs = slice(BLOCK * g, BLOCK * (g + 1))
                win = _window_sum(ext[:, cs], w, forward=False)[WINDOW_HALO:]
                pooled = win * _inv_count(n, w) - uv[:, cs]
                mixed = _nn(pooled.astype(BF16), pw_ref[g].astype(BF16))
                gate, _ = _silu_parts(pg_ref[rows, cs])
                z_ref[rows, cs] = (mixed * sc_ref[layer:layer + 1, cs] * gate).astype(BF16)

        q_ref[...] = _nt(h, w_ref[COL_Q:COL_K, :]).astype(BF16)
        k_ref[...] = _nt(h, w_ref[COL_K:COL_V, :])
        v_ref[...] = _nt(h, w_ref[COL_V:COL_AG, :])
        ag_ref[...] = _nt(h, w_ref[COL_AG:D_IN, :])

        current = _band_is_current()
        for sb in range(tm // BLOCK):
            n = (tm // BLOCK) * i + sb
            rows = slice(BLOCK * sb, BLOCK * (sb + 1))
            before = slice(BLOCK * (sb - 1), BLOCK * sb)
            kx = jnp.concatenate([kprev[...] if sb == 0 else k_ref[before, :], k_ref[rows, :]], axis=0)
            vx = jnp.concatenate([vprev[...] if sb == 0 else v_ref[before, :], v_ref[rows, :]], axis=0)
            variant = jnp.minimum(n, 1) if sb == 0 else 1
            for kv in range(2):
                cs = slice(256 * kv, 256 * (kv + 1))
                p, _ = _probs_keys_major(_replicate_head(kx, kv), _stack_heads(q_ref[rows, cs]),
                                         bias_ref[variant, kv], _sink_row(sink_ref, layer, kv), current)
                o = _unstack_heads(_tn(_unpack_band(p.astype(BF16), current), _replicate_head(vx, kv)))
                a_ref[rows, cs] = o
                gate, _ = _silu_parts(ag_ref[rows, cs])
                z_ref[rows, D_POOL + 256 * kv:D_POOL + 256 * (kv + 1)] = (o * gate).astype(BF16)

        tail = slice(tm - BLOCK, tm)
        uprev[...] = u_ref[tail, :]
        kprev[...] = k_ref[tail, :]
        vprev[...] = v_ref[tail, :]

    row = lambda c: pl.BlockSpec((tm, c), lambda i: (i, 0))
    const = lambda shape: pl.BlockSpec(shape, lambda i: (0,) * len(shape))
    return pl.pallas_call(
        body, name=f"fwd_front{layer}", grid=(SEQ // tm,),
        in_specs=[pl.BlockSpec(memory_space=pltpu.SMEM), row(D_MODEL), const((DEPTH, D_MODEL)),
                  _resident((D_IN, D_MODEL)), const((8, 128)),
                  pl.BlockSpec((None, 4, BLOCK, BLOCK), lambda i: (layer, 0, 0, 0)), const((DEPTH, D_POOL)),
                  _resident((2, 2, BLOCK, GQA * BLOCK))],
        out_specs=[row(D_POOL), row(D_POOL), row(D_ATTN), row(D_KV), row(D_KV), row(D_ATTN), row(D_MODEL),
                   row(D_ATTN)],
        out_shape=[jax.ShapeDtypeStruct((SEQ, D_POOL), F32), jax.ShapeDtypeStruct((SEQ, D_POOL), F32),
                   jax.ShapeDtypeStruct((SEQ, D_ATTN), BF16), jax.ShapeDtypeStruct((SEQ, D_KV), F32),
                   jax.ShapeDtypeStruct((SEQ, D_KV), F32), jax.ShapeDtypeStruct((SEQ, D_ATTN), F32),
                   jax.ShapeDtypeStruct((SEQ, D_MODEL), BF16), jax.ShapeDtypeStruct((SEQ, D_ATTN), F32)],
        scratch_shapes=[pltpu.VMEM((BLOCK, D_POOL), F32), pltpu.VMEM((BLOCK, D_KV), F32),
                        pltpu.VMEM((BLOCK, D_KV), F32)],
        compiler_params=_compiler_params(("arbitrary",)),
    )(sinks, x, norm_pre, w_in_t, token, pool_w, pool_scale, bias)


def _fwd_out(layer, z, x, norm_post, w_out, token):
    tm = FWD_OUT_TILE

    def body(z_ref, x_ref, g_ref, w_ref, _, xn_ref, y_ref):
        y = _nn(z_ref[...], w_ref[...])
        y_ref[...] = y
        r = lax.rsqrt(jnp.mean(y * y, axis=-1, keepdims=True) + EPS)
        xn_ref[...] = x_ref[...] + y * r * g_ref[layer:layer + 1, :]

    row = lambda c: pl.BlockSpec((tm, c), lambda i: (i, 0))
    return pl.pallas_call(
        body, name=f"fwd_out{layer}", grid=(SEQ // tm,),
        in_specs=[row(D_MODEL), row(D_MODEL), pl.BlockSpec((DEPTH, D_MODEL), lambda i: (0, 0)),
                  _resident((D_MODEL, D_MODEL)), pl.BlockSpec((8, 128), lambda i: (0, 0))],
        out_specs=[row(D_MODEL), row(D_MODEL)],
        out_shape=[jax.ShapeDtypeStruct((SEQ, D_MODEL), F32), jax.ShapeDtypeStruct((SEQ, D_MODEL), F32)],
        compiler_params=_compiler_params(("arbitrary",)),
    )(z, x, norm_post, w_out, token)


BACK_TILE = 2 * BLOCK


def _bwd_back(layer, top, dxo_or_xf, target_or_token, y, z, norm_post, w_out, sinks, u, pg, q, k, v, ag, a,
              pool_w, pool_scale, bias):
    tm = BACK_TILE
    steps = SEQ // tm
    last = steps - 1
    per = tm // BLOCK

    def body(*refs):
        refs = list(refs)
        sink_ref, first, second = refs[:3]
        (y_ref, z_ref, g_ref, w_ref, u_ref, up_ref, pg_ref, q_ref, k_ref, v_ref, ag_ref, a_ref, pw_ref, sc_ref,
         bias_ref) = refs[3:18]
        del refs[:18]
        dxo_ref = refs.pop(0) if top else None
        dp_ref, dw_ref, pack_ref, acc, dg, lacc, dzs, ck, cv, ce = refs
        i = pl.program_id(0)
        blk = last - i

        @pl.when(i == 0)
        def _():
            acc[...] = jnp.zeros_like(acc)
            dg[...] = jnp.zeros_like(dg)
            lacc[...] = jnp.zeros_like(lacc)
            pack_ref[...] = jnp.zeros_like(pack_ref)
            ck[...] = jnp.zeros_like(ck)
            cv[...] = jnp.zeros_like(cv)
            ce[...] = jnp.zeros_like(ce)

        if top:
            d = first[...] - second[...]
            dxo_v = d * (1.0 / D_MODEL)
            dxo_ref[...] = dxo_v
            part = jnp.sum(d * d, axis=-1, keepdims=True) * (1.0 / D_MODEL)
            lacc[...] += 0.5 * jnp.sum(part, axis=0, keepdims=True)
        else:
            dxo_v = first[...]
        yv = y_ref[...]
        r = lax.rsqrt(jnp.mean(yv * yv, axis=-1, keepdims=True) + EPS)
        yn = yv * r
        dg[...] += jnp.sum(dxo_v * yn, axis=0, keepdims=True)
        dyn = dxo_v * g_ref[layer:layer + 1, :]
        dy = (r * (dyn - yn * jnp.mean(dyn * yn, axis=-1, keepdims=True))).astype(BF16)
        dzs[...] = _nt(dy, w_ref[...])
        acc[...] += _tn(z_ref[...], dy)

        lane = lax.broadcasted_iota(jnp.int32, (1, 128), 1)
        lane2 = lax.broadcasted_iota(jnp.int32, (256, 128), 1)
        current = _band_is_current()
        for sb in reversed(range(per)):
            n = per * blk + sb
            rows = slice(BLOCK * sb, BLOCK * (sb + 1))

            uv = u_ref[rows, :]
            if sb == 0:
                halo = up_ref[BLOCK - WINDOW_HALO:, :] * (n > 0).astype(F32)
            else:
                halo = u_ref[BLOCK * sb - WINDOW_HALO:BLOCK * sb, :]
            ext = jnp.concatenate([halo, uv], axis=0)
            for g, w in enumerate(POOL_WINDOWS):
                cs = slice(BLOCK * g, BLOCK * (g + 1))
                inv = _inv_count(n, w)
                win = _window_sum(ext[:, cs], w, forward=False)[WINDOW_HALO:]
                pooled = win * inv - uv[:, cs]
                pw_g = pw_ref[g].astype(BF16)
                mixed = _nn(pooled.astype(BF16), pw_g)
                gate, dgate = _silu_parts(pg_ref[rows, cs])
                dzp = dzs[rows, cs]
                sc = sc_ref[layer:layer + 1, cs]
                dpm = dzp * gate
                dp_ref[rows, COL_PG + BLOCK * g:COL_PG + BLOCK * (g + 1)] = (dzp * (mixed * sc) * dgate).astype(BF16)
                pack_ref[ROW_SC + g:ROW_SC + g + 1, :] += jnp.sum(dpm * mixed, axis=0, keepdims=True)
                dmixed = (dpm * sc).astype(BF16)
                pack_ref[ROW_PW + BLOCK * g:ROW_PW + BLOCK * (g + 1), :] += _tn(pooled.astype(BF16), dmixed)
                dpooled = _nt(dmixed, pw_g)
                e = dpooled * inv
                lead = _window_sum(jnp.concatenate([e, ce[:WINDOW_HALO, cs]], axis=0), w, forward=True)[:BLOCK]
                dp_ref[rows, COL_U + BLOCK * g:COL_U + BLOCK * (g + 1)] = (lead - dpooled).astype(BF16)
                ce[:, cs] = e

            kx = _kv_ext(k_ref, n)
            vx = _kv_ext(v_ref, n)
            variant = jnp.minimum(n, 1) if sb == 0 else 1
            dsink_row = jnp.zeros((1, 128), F32)
            tks, tvs = [], []
            for kv in range(2):
                cs = slice(256 * kv, 256 * (kv + 1))
                k_rep = _replicate_head(kx, kv)
                v_rep = _replicate_head(vx, kv)
                q_st = _stack_heads(q_ref[rows, cs])
                gate, dgate = _silu_parts(ag_ref[rows, cs])
                dza = dzs[rows, D_POOL + 256 * kv:D_POOL + 256 * (kv + 1)]
                dp_ref[rows, COL_AG + 256 * kv:COL_AG + 256 * (kv + 1)] = (dza * a_ref[rows, cs] * dgate).astype(BF16)
                da_st = _stack_heads((dza * gate).astype(BF16))
                p, psink = _probs_keys_major(k_rep, q_st, bias_ref[variant, kv], _sink_row(sink_ref, layer, kv),
                                             current)
                dpt = _pack_band(_nt(v_rep, da_st), current)
                delta = jnp.sum(p * dpt, axis=0, keepdims=True)
                dst = _unpack_band((p * (dpt - delta) * SCALE).astype(BF16), current)
                sink_terms = psink * delta
                for g in range(GQA):
                    dsink = -jnp.sum(sink_terms[:, BLOCK * g:BLOCK * (g + 1)], axis=1, keepdims=True)
                    dsink_row = dsink_row + jnp.where(lane == kv * GQA + g, dsink, 0.0)
                dp_ref[rows, COL_Q + 256 * kv:COL_Q + 256 * (kv + 1)] = _unstack_heads(_tn(dst, k_rep)).astype(BF16)
                tks.append(_fold_heads(_nn(dst, q_st)))
                tvs.append(_fold_heads(_nn(_unpack_band(p.astype(BF16), current), da_st)))
            pack_ref[ROW_SINK:ROW_SINK + 1, :] += dsink_row
            dkx = jnp.where(lane2 < 64, tks[0], tks[1])
            dvx = jnp.where(lane2 < 64, tvs[0], tvs[1])
            dp_ref[rows, COL_K:COL_V] = (ck[...] + dkx[BLOCK:]).astype(BF16)
            dp_ref[rows, COL_V:COL_AG] = (cv[...] + dvx[BLOCK:]).astype(BF16)
            ck[...] = dkx[:BLOCK]
            cv[...] = dvx[:BLOCK]

        @pl.when(i == steps - 1)
        def _():
            dw_ref[...] = acc[...].astype(BF16)
            _rows_of(dg, pack_ref, ROW_NPOST)
            pack_ref[ROW_LOSS:ROW_LOSS + 1, :] = jnp.where(lane == 0, lacc[...], 0.0)

    row = lambda c: pl.BlockSpec((tm, c), lambda i: (last - i, 0))
    const = lambda shape: pl.BlockSpec(shape, lambda i: (0,) * len(shape))
    act = jax.ShapeDtypeStruct((SEQ, D_MODEL), F32)
    return pl.pallas_call(
        body, name=f"bwd_back{layer}", grid=(steps,),
        in_specs=[pl.BlockSpec(memory_space=pltpu.SMEM), row(D_MODEL), row(D_MODEL) if top else const((8, 128)),
                  row(D_MODEL), row(D_MODEL), const((DEPTH, D_MODEL)), _resident((D_MODEL, D_MODEL)),
                  row(D_POOL), pl.BlockSpec((BLOCK, D_POOL), lambda i: (jnp.maximum(per * (last - i) - 1, 0), 0)),
                  row(D_POOL), row(D_ATTN), _resident((SEQ, D_KV)), _resident((SEQ, D_KV)), row(D_ATTN), row(D_ATTN),
                  pl.BlockSpec((None, 4, BLOCK, BLOCK), lambda i: (layer, 0, 0, 0)), const((DEPTH, D_POOL)),
                  _resident((2, 2, BLOCK, GQA * BLOCK))],
        out_specs=([row(D_MODEL)] * (1 if top else 0)
                   + [row(D_IN), const((D_MODEL, D_MODEL)), const((PACK_ROWS, 128))]),
        out_shape=([act] * (1 if top else 0)
                   + [jax.ShapeDtypeStruct((SEQ, D_IN), BF16), jax.ShapeDtypeStruct((D_MODEL, D_MODEL), BF16),
                      jax.ShapeDtypeStruct((PACK_ROWS, 128), F32)]),
        scratch_shapes=[pltpu.VMEM((D_MODEL, D_MODEL), F32), pltpu.VMEM((1, D_MODEL), F32), pltpu.VMEM((1, 1), F32),
                        pltpu.VMEM((tm, D_MODEL), F32), pltpu.VMEM((BLOCK, D_KV), F32), pltpu.VMEM((BLOCK, D_KV), F32),
                        pltpu.VMEM((BLOCK, D_POOL), F32)],
        compiler_params=_compiler_params(("arbitrary",)),
    )(sinks, dxo_or_xf, target_or_token, y, z, norm_post, w_out, u, u, pg, q, k, v, ag, a, pool_w, pool_scale, bias)


def _bwd_in(layer, part, token, dproj, x, norm_pre, dxo=None, w_in_t=None):
    pair = part in ("dw_pair", "both_pair")
    want_dw, want_dx = part != "dx", part in ("both", "dx", "both_pair")
    tm = TOKEN_TILE
    steps = SEQ // tm
    cw = 256

    def body(*refs):
        refs = list(refs)
        dp_ref, x_ref, g_ref = refs[1:4]
        del refs[:4]
        if want_dx:
            dxo_ref, w_ref, dx_ref, dgo_ref = refs[:4]
            del refs[:4]
            dg = refs.pop()
        if pair:
            hs_ref, hm_ref, acc, mine_buf, theirs_buf, send_sem, recv_sem = refs
        elif want_dw:
            dw_ref, acc = refs
        i = pl.program_id(0)

        @pl.when(i == 0)
        def _():
            if pair:
                _handshake([(lax.axis_index("x"), lax.axis_index("y"), 1 - lax.axis_index("c"))])
            if want_dw:
                acc[...] = jnp.zeros_like(acc)
            if want_dx:
                dg[...] = jnp.zeros_like(dg)

        xv = x_ref[...]
        gv = g_ref[layer:layer + 1, :]
        r = lax.rsqrt(jnp.mean(xv * xv, axis=-1, keepdims=True) + EPS)
        xn = xv * r
        if want_dw:
            hb = (xn * gv).astype(BF16)
            for c in range(0, D_IN, cw):
                acc[c:c + cw, :] += _tn(dp_ref[:, c:c + cw], hb)
        def rows_for(q, core):
            return pl.ds(pl.multiple_of((2 * q + core) * IN_SHARD, 8), IN_SHARD)

        def swap(q):
            x, y, c = _mesh_pos()
            return pltpu.make_async_remote_copy(
                src_ref=mine_buf.at[q], dst_ref=theirs_buf.at[q], send_sem=send_sem.at[q], recv_sem=recv_sem.at[q],
                device_id=(x, y, 1 - c), device_id_type=MESH)

        if pair:
            @pl.when(i == steps - 1)
            def _():
                for q in range(4):
                    mine_buf[q] = acc[rows_for(q, 1 - lax.axis_index("c")), :].astype(BF16)
                    swap(q).start()

        if want_dx:
            dh = _nn(dp_ref[...], w_ref[...])
            dg[...] += jnp.sum(dh * xn, axis=0, keepdims=True)
            dhn = dh * gv
            dx_ref[...] = dxo_ref[...] + r * (dhn - xn * jnp.mean(dhn * xn, axis=-1, keepdims=True))

        @pl.when(i == steps - 1)
        def _():
            if pair:
                x, y, c = _mesh_pos()
                for q in range(4):
                    swap(q).wait()
                for j, q in enumerate([2 * (1 - x) + y, 2 * x + (1 - y), 2 * (1 - x) + (1 - y)]):
                    hs_ref[j] = (acc[rows_for(q, c), :] + theirs_buf[q].astype(F32)).astype(BF16)
                hm_ref[...] = acc[rows_for(2 * x + y, c), :] + theirs_buf[2 * x + y].astype(F32)
            elif want_dw:
                dw_ref[...] = acc[...].astype(BF16)
            if want_dx:
                _rows_of(dg, dgo_ref, 0)

    row = lambda c: pl.BlockSpec((tm, c), lambda i: (i, 0))
    const = lambda shape: pl.BlockSpec(shape, lambda i: (0,) * len(shape))
    in_specs = [const((8, 128)), row(D_IN), row(D_MODEL), const((DEPTH, D_MODEL))]
    operands = [token, dproj, x, norm_pre]
    out_specs, out_shape, scratch = [], [], []
    if want_dx:
        in_specs += [row(D_MODEL), _resident((D_IN, D_MODEL))]
        operands += [dxo, w_in_t]
        out_specs += [row(D_MODEL), const((8, 128))]
        out_shape += [jax.ShapeDtypeStruct((SEQ, D_MODEL), F32), jax.ShapeDtypeStruct((8, 128), F32)]
    if pair:
        out_specs += [const((3, IN_SHARD, D_MODEL)), const((IN_SHARD, D_MODEL))]
        out_shape += [jax.ShapeDtypeStruct((3, IN_SHARD, D_MODEL), BF16), jax.ShapeDtypeStruct((IN_SHARD, D_MODEL), F32)]
        scratch += [pltpu.VMEM((D_IN, D_MODEL), F32), pltpu.VMEM((4, IN_SHARD, D_MODEL), BF16),
                    pltpu.VMEM((4, IN_SHARD, D_MODEL), BF16), pltpu.SemaphoreType.DMA((4,)), pltpu.SemaphoreType.DMA((4,))]
    elif want_dw:
        out_specs.append(const((D_IN, D_MODEL)))
        out_shape.append(jax.ShapeDtypeStruct((D_IN, D_MODEL), BF16))
        scratch.append(pltpu.VMEM((D_IN, D_MODEL), F32))
    if want_dx:
        scratch.append(pltpu.VMEM((1, D_MODEL), F32))
    params = pltpu.CompilerParams(dimension_semantics=("arbitrary",), vmem_limit_bytes=VMEM_LIMIT,
                                  collective_id=COLLECTIVE_PAIR_SUM[layer] if pair else None)
    return pl.pallas_call(
        body, name=f"bwd_in_{part}{layer}", grid=(steps,),
        in_specs=in_specs, out_specs=out_specs, out_shape=out_shape, scratch_shapes=scratch,
        compiler_params=params,
    )(*operands)


def _mesh_pos():
    return lax.axis_index("x"), lax.axis_index("y"), lax.axis_index("c")


def _device_rows(ref, m, px, py, pc):
    return ref.at[pl.ds(pl.multiple_of((4 * px + 2 * py + pc) * m, 16 if m % 16 == 0 else 8), m), :]


def _allgather(srcs, out_dtype, name, later=()):
    na, nb = len(srcs), len(later)
    every = list(srcs) + list(later)
    shapes = [(a.shape[-2], a.shape[-1]) for a, _ in every]

    def body(*refs):
        xs, refs = refs[:na + nb], refs[na + nb:]
        outs, cast, land, refs = refs[:na], refs[na:na + nb], refs[na + nb:na + 2 * nb], refs[na + 2 * nb:]
        stage, raw, (send_sems, recv_sems, local_sems, load_sems) = refs[:na], refs[na:2 * na + nb], refs[2 * na + nb:]
        loads = [pltpu.make_async_copy(xs[i].at[every[i][1]], raw[i], load_sems.at[i]) for i in range(na + nb)]
        for cp in loads:
            cp.start()
        x, y, c = _mesh_pos()
        me, sibling = (x, y, c), (x, y, 1 - c)
        near = [(1 - x, y), (x, 1 - y)]
        far = (1 - x, 1 - y)
        relay_from, relay_to = (x ^ (1 - c), y ^ c), (x ^ c, y ^ (1 - c))
        _handshake([sibling] + [(*chip, c) for chip in near])
        k_from, k_to = 1 + c, 2 - c

        def slot(a, px, py, pc):
            return _device_rows(outs[a], shapes[a][0], px, py, pc)

        def copy(a, k, block, to, src=None):
            return pltpu.make_async_remote_copy(
                src_ref=slot(a, *block) if src is None else src, dst_ref=slot(a, *block),
                send_sem=send_sems.at[a, k], recv_sem=recv_sems.at[a, k], device_id=to, device_id_type=MESH)

        def cast_block(i):
            loads[i].wait()
            return raw[i][...].astype(out_dtype)

        for a in range(na):
            stage[a][...] = cast_block(a)
        mine = [pltpu.make_async_copy(stage[a], slot(a, *me), local_sems.at[a]) for a in range(na)]
        for cp in mine:
            cp.start()
        sent = []
        for a in range(na):
            sent.append(copy(a, 0, me, sibling, src=stage[a]))
            sent += [copy(a, 1 + j, me, (*chip, c), src=stage[a]) for j, chip in enumerate(near)]
        for cp in sent:
            cp.start()
        for b in range(nb):
            cast[b][...] = cast_block(na + b)
            cp = pltpu.make_async_copy(cast[b], _device_rows(land[b], shapes[na + b][0], *me), local_sems.at[na + b])
            cp.start()
            mine.append(cp)
        for a in range(na):
            copy(a, k_from, (*relay_from, c), me).wait_recv()
            sent += [copy(a, 3, (*relay_from, c), (*relay_to, c)), copy(a, 3 + k_from, (*relay_from, c), sibling)]
            sent[-2].start()
            sent[-1].start()
        for a in range(na):
            copy(a, k_to, (*relay_to, c), me).wait_recv()
            sent.append(copy(a, 3 + k_to, (*relay_to, c), sibling))
            sent[-1].start()
        for a in range(na):
            copy(a, 3, (*far, c), me).wait_recv()
            sent.append(copy(a, 6, (*far, c), sibling))
            sent[-1].start()
        for a in range(na):
            copy(a, 0, sibling, me).wait_recv()
            for j, chip in enumerate(near + [far]):
                copy(a, 4 + j, (*chip, 1 - c), me).wait_recv()
        for cp in sent:
            cp.wait_send()
        for cp in mine:
            cp.wait()

    vmem = pl.BlockSpec(memory_space=pltpu.VMEM)
    hbm = pl.BlockSpec(memory_space=pl.ANY)
    gathered = [jax.ShapeDtypeStruct((N_DEV * m, n), out_dtype) for m, n in shapes]
    out = pl.pallas_call(
        body, name=name,
        in_specs=[hbm] * (na + nb),
        out_specs=[hbm] * na + [vmem] * nb + [hbm] * nb,
        out_shape=gathered[:na] + [jax.ShapeDtypeStruct(s, out_dtype) for s in shapes[na:]] + gathered[na:],
        scratch_shapes=([pltpu.VMEM(s, out_dtype) for s in shapes[:na]]
                        + [pltpu.VMEM(s, a.dtype) for s, (a, _) in zip(shapes, every)]
                        + [pltpu.SemaphoreType.DMA((na, 7)), pltpu.SemaphoreType.DMA((na, 7)),
                           pltpu.SemaphoreType.DMA((na + nb,)), pltpu.SemaphoreType.DMA((na + nb,))]),
        compiler_params=pltpu.CompilerParams(vmem_limit_bytes=VMEM_LIMIT, collective_id=COLLECTIVE_GATHER_W0),
    )(*[a for a, _ in every])
    return out[:na], out[na:na + nb], out[na + nb:]


ALL_PEERS = tuple(range(1, N_DEV))
SIBLING_AND_SAME_CORE = (1, 2, 4, 6)


def _related(k, x, y, c):
    return x ^ ((k >> 2) & 1), y ^ ((k >> 1) & 1), c ^ (k & 1)


def _gather_start(blocks, lands, relations, collective_id, name):
    na = len(blocks)

    def body(*refs):
        src, land, sems, token = refs[:na], refs[na:2 * na], refs[2 * na:4 * na], refs[-1]
        x, y, c = _mesh_pos()
        _handshake([_related(k, x, y, c) for k in sorted(set().union(*relations))])
        for a in range(na):
            for k in relations[a]:
                pltpu.make_async_remote_copy(
                    src_ref=src[a], dst_ref=_device_rows(land[a], blocks[a].shape[0], x, y, c),
                    send_sem=sems[2 * a].at[k - 1], recv_sem=sems[2 * a + 1].at[k - 1],
                    device_id=_related(k, x, y, c), device_id_type=MESH).start()
        token[...] = jnp.zeros_like(token)

    bufs = [pltpu.HBM(t.shape, t.dtype) for t in list(blocks) + list(lands)]
    out = pl.pallas_call(
        body, name=name,
        out_shape=(*([pltpu.SemaphoreType.DMA((N_DEV - 1,))] * (2 * na)), *bufs, jax.ShapeDtypeStruct((8, 128), F32)),
        in_specs=[_HBM] * (2 * na),
        out_specs=(*([_SEM] * (2 * na)), *([_HBM] * (2 * na)), pl.BlockSpec(memory_space=pltpu.VMEM)),
        input_output_aliases={i: 2 * na + i for i in range(2 * na)},
        compiler_params=pltpu.CompilerParams(has_side_effects=_EFFECT, collective_id=collective_id),
    )(*[pltpu.with_memory_space_constraint(t, pltpu.HBM) for t in list(blocks) + list(lands)])
    sems = [(out[2 * a], out[2 * a + 1]) for a in range(na)]
    return sems, out[2 * na:3 * na], out[3 * na:4 * na], out[-1]


def _gather_wait(sems, block, land, relations, after, name):
    def body(src, land_ref, send_sem, recv_sem, after_ref, src_out, land_out):
        x, y, c = _mesh_pos()
        for k in relations:
            peer = _related(k, x, y, c)
            cp = pltpu.make_async_remote_copy(
                src_ref=src, dst_ref=_device_rows(land_ref, block.shape[0], *peer),
                send_sem=send_sem.at[k - 1], recv_sem=recv_sem.at[k - 1], device_id=peer, device_id_type=MESH)
            cp.wait_send()
            cp.wait_recv()

    out = pl.pallas_call(
        body, name=name,
        out_shape=(pltpu.HBM(block.shape, block.dtype), pltpu.HBM(land.shape, land.dtype)),
        in_specs=[_HBM, _HBM, _SEM, _SEM, pl.BlockSpec(memory_space=pl.ANY)],
        out_specs=[_HBM, _HBM],
        input_output_aliases={0: 0, 1: 1},
        compiler_params=pltpu.CompilerParams(has_side_effects=_EFFECT),
    )(block, land, sems[0], sems[1], after)
    return out[1]


(COLLECTIVE_GATHER_W0, COLLECTIVE_GATHER_W1, COLLECTIVE_FORWARD_W_IN1, COLLECTIVE_EXCHANGE_1, COLLECTIVE_EXCHANGE_0A,
 COLLECTIVE_EXCHANGE_0B, COLLECTIVE_GATHER_SMALL) = range(1, 8)
COLLECTIVE_PAIR_SUM = (8, 9)


def _handshake(peers):
    barrier = pltpu.get_barrier_semaphore()
    for peer in peers:
        pl.semaphore_signal(barrier, inc=1, device_id=peer, device_id_type=MESH)
    pl.semaphore_wait(barrier, len(peers))


def _forward_plan(land_ref, m):
    x, y, c = _mesh_pos()
    return [_device_rows(land_ref, m, qx, qy, c) for qx, qy in ((1 - x, y), (x, 1 - y), (1 - x, 1 - y))], (x, y, 1 - c)


def _forward_start(land, m, name):
    def body(land_ref, send_sem, recv_sem, land_out, token):
        _handshake([(lax.axis_index("x"), lax.axis_index("y"), 1 - lax.axis_index("c"))])
        rows, sibling = _forward_plan(land_ref, m)
        for j, r in enumerate(rows):
            pltpu.make_async_remote_copy(src_ref=r, dst_ref=r, send_sem=send_sem.at[j], recv_sem=recv_sem.at[j],
                                         device_id=sibling, device_id_type=MESH).start()
        token[...] = jnp.zeros_like(token)

    out = pl.pallas_call(
        body, name=name,
        out_shape=(pltpu.SemaphoreType.DMA((3,)), pltpu.SemaphoreType.DMA((3,)), pltpu.HBM(land.shape, land.dtype),
                   jax.ShapeDtypeStruct((8, 128), F32)),
        in_specs=[_HBM],
        out_specs=(_SEM, _SEM, _HBM, pl.BlockSpec(memory_space=pltpu.VMEM)),
        input_output_aliases={0: 2},
        compiler_params=pltpu.CompilerParams(has_side_effects=_EFFECT, collective_id=COLLECTIVE_FORWARD_W_IN1),
    )(pltpu.with_memory_space_constraint(land, pltpu.HBM))
    return (out[0], out[1]), out[2], out[3]


def _forward_wait(sems, land, m, after, name):
    def body(land_ref, send_sem, recv_sem, after_ref, land_out):
        x, y, c = _mesh_pos()
        mine, sibling = _forward_plan(land_ref, m)
        theirs = [_device_rows(land_ref, m, qx, qy, 1 - c) for qx, qy in ((1 - x, y), (x, 1 - y), (1 - x, 1 - y))]
        for j in range(3):
            cp = pltpu.make_async_remote_copy(src_ref=mine[j], dst_ref=theirs[j], send_sem=send_sem.at[j],
                                              recv_sem=recv_sem.at[j], device_id=sibling, device_id_type=MESH)
            cp.wait_send()
            cp.wait_recv()

    return pl.pallas_call(
        body, name=name,
        out_shape=pltpu.HBM(land.shape, land.dtype),
        in_specs=[_HBM, _SEM, _SEM, pl.BlockSpec(memory_space=pl.ANY)],
        out_specs=_HBM,
        input_output_aliases={0: 0},
        compiler_params=pltpu.CompilerParams(has_side_effects=_EFFECT),
    )(land, sems[0], sems[1], after)


_HBM = pl.BlockSpec(memory_space=pltpu.HBM)
_SEM = pl.BlockSpec(memory_space=pltpu.SEMAPHORE)
_EFFECT = pltpu.SideEffectType.DATAFLOW_SIDE_EFFECTING


def _exchange_plan(direct):
    x, y, c = _mesh_pos()
    if not direct:
        return [(j, j, (qx, qy, c)) for j, (qx, qy) in enumerate([(1 - x, y), (x, 1 - y), (1 - x, 1 - y)])]
    plan = []
    for k in range(1, N_DEV):
        px, py, pc = x ^ ((k >> 2) & 1), y ^ ((k >> 1) & 1), c ^ (k & 1)
        plan.append((4 * px + 2 * py + pc, k - 1, (px, py, pc)))
    return plan


def _exchange_copies(directs):
    copies, base = [], 0
    for a, direct in enumerate(directs):
        plan = _exchange_plan(direct)
        copies += [(a, block, slot, peer, base + slot) for block, slot, peer in plan]
        base += len(plan)
    return copies, base


def _exchange_start(srcs, directs, collective_id, name):
    na = len(srcs)
    slots = [N_DEV - 1 if direct else 3 for direct in directs]

    def body(*refs):
        src, land = refs[:na], refs[na:2 * na]
        send_sem, recv_sem = refs[2 * na], refs[2 * na + 1]
        token = refs[-1]
        _handshake([peer for _, _, peer in _exchange_plan(any(directs))])
        for a, block, slot, peer, sem in _exchange_copies(directs)[0]:
            pltpu.make_async_remote_copy(
                src_ref=src[a].at[block], dst_ref=land[a].at[slot], send_sem=send_sem.at[sem],
                recv_sem=recv_sem.at[sem], device_id=peer, device_id_type=MESH).start()
        token[...] = jnp.zeros_like(token)

    zones = [jax.ShapeDtypeStruct((n,) + t.shape[1:], t.dtype) for n, t in zip(slots, srcs)]
    bufs = [pltpu.HBM(t.shape, t.dtype) for t in list(srcs) + zones]
    out = pl.pallas_call(
        body, name=name,
        out_shape=(pltpu.SemaphoreType.DMA((sum(slots),)), pltpu.SemaphoreType.DMA((sum(slots),)), *bufs,
                   jax.ShapeDtypeStruct((8, 128), F32)),
        in_specs=[_HBM] * (2 * na),
        out_specs=(_SEM, _SEM, *([_HBM] * (2 * na)), pl.BlockSpec(memory_space=pltpu.VMEM)),
        input_output_aliases={i: 2 + i for i in range(2 * na)},
        compiler_params=pltpu.CompilerParams(has_side_effects=_EFFECT, collective_id=collective_id),
    )(*[pltpu.with_memory_space_constraint(t, pltpu.HBM) for t in srcs],
      *[pltpu.with_memory_space_constraint(lax.empty(t.shape, t.dtype), pltpu.HBM) for t in zones])
    return out[0], out[1], out[2:2 + na], out[2 + na:2 + 2 * na], out[-1]


def _exchange_wait(send_sem, recv_sem, srcs, lands, directs, after, name):
    na = len(srcs)

    def body(*refs):
        src, land = refs[:na], refs[na:2 * na]
        send_sem_ref, recv_sem_ref = refs[2 * na], refs[2 * na + 1]
        for a, block, slot, peer, sem in _exchange_copies(directs)[0]:
            cp = pltpu.make_async_remote_copy(
                src_ref=src[a].at[block], dst_ref=land[a].at[slot], send_sem=send_sem_ref.at[sem],
                recv_sem=recv_sem_ref.at[sem], device_id=peer, device_id_type=MESH)
            cp.wait_send()
            cp.wait_recv()

    bufs = [pltpu.HBM(t.shape, t.dtype) for t in list(srcs) + list(lands)]
    out = pl.pallas_call(
        body, name=name,
        out_shape=tuple(bufs),
        in_specs=[_HBM] * (2 * na) + [_SEM, _SEM, pl.BlockSpec(memory_space=pl.ANY)],
        out_specs=[_HBM] * (2 * na),
        input_output_aliases={i: i for i in range(2 * na)},
        compiler_params=pltpu.CompilerParams(has_side_effects=_EFFECT),
    )(*srcs, *lands, send_sem, recv_sem, after)
    return out[:na], out[na:]


def _own_then_slots(mine_ref, lands_ref, rows=slice(None)):
    if len(mine_ref.shape) == 3:
        x, y, c = _mesh_pos()
        total = mine_ref[4 * x + 2 * y + c, rows, :].astype(F32)
    else:
        total = mine_ref[rows, :].astype(F32)
    for j in range(lands_ref.shape[0]):
        total = total + lands_ref[j, rows, :].astype(F32)
    return total


SMALL_ROWS = 2 * PACK_SLICE + 2 * 8


def _small_gather_start(mine, lands, dgpre, name):
    def body(*refs):
        hm, ld, dg = refs[:DEPTH], refs[DEPTH:2 * DEPTH], refs[2 * DEPTH:3 * DEPTH]
        send_sem, recv_sem, blk, land, token, own, slots, rows, built, local_sems = refs[3 * DEPTH:]
        x, y, c = _mesh_pos()
        loads = []
        for l in range(DEPTH):
            loads += [pltpu.make_async_copy(hm[l].at[4 * x + 2 * y + c], own.at[l], local_sems.at[3 * l]),
                      pltpu.make_async_copy(ld[l], slots.at[l], local_sems.at[3 * l + 1]),
                      pltpu.make_async_copy(dg[l], rows.at[l], local_sems.at[3 * l + 2])]
        for cp in loads:
            cp.start()
        _handshake([_related(k, x, y, c) for k in ALL_PEERS])
        for cp in loads:
            cp.wait()
        for l in range(DEPTH):
            total = own[l]
            for j in range(N_DEV - 1):
                total = total + slots[l, j]
            built[PACK_SLICE * l:PACK_SLICE * (l + 1), :] = total
            built[2 * PACK_SLICE + 8 * l:2 * PACK_SLICE + 8 * (l + 1), :] = rows[l]
        stores = [pltpu.make_async_copy(built, blk, local_sems.at[3 * DEPTH]),
                  pltpu.make_async_copy(built, _device_rows(land, SMALL_ROWS, x, y, c), local_sems.at[3 * DEPTH + 1])]
        for cp in stores:
            cp.start()
        for cp in stores:
            cp.wait()
        for k in ALL_PEERS:
            pltpu.make_async_remote_copy(
                src_ref=blk, dst_ref=_device_rows(land, SMALL_ROWS, x, y, c), send_sem=send_sem.at[k - 1],
                recv_sem=recv_sem.at[k - 1], device_id=_related(k, x, y, c), device_id_type=MESH).start()
        token[...] = jnp.zeros_like(token)

    hbm = pl.BlockSpec(memory_space=pl.ANY)
    out = pl.pallas_call(
        body, name=name,
        in_specs=[hbm] * (3 * DEPTH),
        out_specs=(_SEM, _SEM, _HBM, _HBM, pl.BlockSpec(memory_space=pltpu.VMEM)),
        out_shape=(pltpu.SemaphoreType.DMA((N_DEV - 1,)), pltpu.SemaphoreType.DMA((N_DEV - 1,)),
                   pltpu.HBM((SMALL_ROWS, 128), F32), pltpu.HBM((N_DEV * SMALL_ROWS, 128), F32),
                   jax.ShapeDtypeStruct((8, 128), F32)),
        scratch_shapes=[pltpu.VMEM((DEPTH, PACK_SLICE, 128), F32), pltpu.VMEM((DEPTH, N_DEV - 1, PACK_SLICE, 128), F32),
                        pltpu.VMEM((DEPTH, 8, 128), F32), pltpu.VMEM((SMALL_ROWS, 128), F32),
                        pltpu.SemaphoreType.DMA((3 * DEPTH + 2,))],
        compiler_params=pltpu.CompilerParams(has_side_effects=_EFFECT, collective_id=COLLECTIVE_GATHER_SMALL),
    )(*mine, *lands, *dgpre)
    return (out[0], out[1]), out[2], out[3], out[4]


def _adamw_math(w, g, m, v):
    m = ADAM_B1 * m + (1.0 - ADAM_B1) * g
    v = ADAM_B2 * v + (1.0 - ADAM_B2) * (g * g)
    m_hat = m / (1.0 - ADAM_B1 ** ADAM_STEP)
    v_hat = v / (1.0 - ADAM_B2 ** ADAM_STEP)
    delta = -ADAM_LR * (m_hat / (jnp.sqrt(v_hat) + ADAM_EPS) + ADAM_WD * w)
    return delta, m, v


def _adamw_layer(layer, mine, lands, w, m, v, earlier, token, name, rows):
    _, mm, nn = w.shape

    def body(hm_ref, ld_ref, w_ref, m_ref, v_ref, _, *refs):
        g_ref, d_ref, nm_ref, nv_ref = refs[-4:]
        g = _own_then_slots(hm_ref, ld_ref)
        g_ref[...] = g
        d, nm, nv = _adamw_math(w_ref[...], g, m_ref[...], v_ref[...])
        d_ref[...] = d
        nm_ref[...] = nm
        nv_ref[...] = nv

    spec = pl.BlockSpec((None, rows, nn), lambda i: (layer, i, 0))
    carried = [] if earlier is None else list(earlier)
    return pl.pallas_call(
        body, name=name, grid=(mm // rows,),
        in_specs=([pl.BlockSpec((rows, nn), lambda i: (i, 0)) if mine.ndim == 2
                   else pl.BlockSpec((N_DEV, rows, nn), lambda i: (0, i, 0)),
                   pl.BlockSpec((lands.shape[0], rows, nn), lambda i: (0, i, 0)),
                   spec, spec, spec] + [pl.BlockSpec(memory_space=pl.ANY)] * (1 + len(carried))),
        out_specs=[spec] * 4,
        out_shape=[jax.ShapeDtypeStruct(w.shape, F32)] * 4,
        input_output_aliases={6 + t: t for t in range(len(carried))},
        compiler_params=_compiler_params(("arbitrary",)),
    )(mine, lands, w, m, v, token, *carried)


def _adamw_small(gathered, params):
    def body(all_ref, *refs):
        ins, outs, packs = refs[:15], refs[15:15 + 21], refs[15 + 21]
        loss_ref = outs[0]
        for dev in range(N_DEV):
            for l in range(DEPTH):
                packs[l, PACK_SLICE * dev:PACK_SLICE * (dev + 1), :] = (
                    all_ref[SMALL_ROWS * dev + PACK_SLICE * l:SMALL_ROWS * dev + PACK_SLICE * (l + 1), :])
        loss_ref[...] = packs[DEPTH - 1, ROW_LOSS:ROW_LOSS + 1, 0:1]

        def update(p, sel, g):
            w_ref, m_ref, v_ref = ins[p], ins[5 + p], ins[10 + p]
            d, nm, nv = _adamw_math(w_ref[sel], g, m_ref[sel], v_ref[sel])
            for t, val in enumerate((g, d, nm, nv)):
                outs[1 + 5 * t + p][sel] = val

        for l in range(DEPTH):
            gp = packs.at[l]
            row0 = 2 * PACK_SLICE + 8 * l
            dgpre = all_ref[row0:row0 + 8, :]
            for dev in range(1, N_DEV):
                dgpre = dgpre + all_ref[SMALL_ROWS * dev + row0:SMALL_ROWS * dev + row0 + 8, :]
            for grp in range(4):
                update(0, (l, grp), gp[ROW_PW + BLOCK * grp:ROW_PW + BLOCK * (grp + 1), :])
                update(1, (slice(l, l + 1), slice(128 * grp, 128 * (grp + 1))), gp[ROW_SC + grp:ROW_SC + grp + 1, :])
            update(2, (slice(l, l + 1), slice(None)), gp[ROW_SINK:ROW_SINK + 1, 0:N_HEADS])
            for r in range(D_MODEL // 128):
                sel = (slice(l, l + 1), slice(128 * r, 128 * (r + 1)))
                update(3, sel, dgpre[r:r + 1, :])
                update(4, sel, gp[ROW_NPOST + r:ROW_NPOST + r + 1, :])

    shapes = [jax.ShapeDtypeStruct(p.shape, F32) for p in params[:5]]
    return pl.pallas_call(
        body, name="adamw_small",
        out_shape=[jax.ShapeDtypeStruct((1, 1), F32)] + shapes * 4,
        scratch_shapes=[pltpu.VMEM((DEPTH, PACK_ROWS, 128), F32)],
        compiler_params=_compiler_params(),
    )(gathered, *params)


def kernel(x, w_in, pool_w, pool_scale, attn_sinks, w_out, norm_pre, norm_post, loss_target, m_w_in, m_pool_w, m_pool_scale, m_attn_sinks, m_w_out, m_norm_pre, m_norm_post, v_w_in, v_pool_w, v_pool_scale, v_attn_sinks, v_w_out, v_norm_pre, v_norm_post):
    x0 = x.reshape(SEQ, D_MODEL)
    target = loss_target.reshape(SEQ, D_MODEL)
    bias = jnp.asarray(_attn_bias())
    w_in_t, m_in_t, v_in_t = (jnp.swapaxes(t, 1, 2) for t in (w_in, m_w_in, v_w_in))

    (win0, wout0), later, lands = _allgather([(w_in_t, 0), (w_out, 0)], BF16, "gather_w0",
                                              later=[(w_in_t, 1), (w_out, 1)])
    sems, later, lands, token = _gather_start(later, lands, [SIBLING_AND_SAME_CORE, ALL_PEERS], COLLECTIVE_GATHER_W1,
                                              "gather_w1_start")
    win_full, wout_full = [win0, None], [wout0, None]

    saved = []
    xl = x0
    for layer in range(DEPTH):
        u, pg, q, k, v, ag, z, a = _fwd_front(layer, xl, norm_pre, win_full[layer], token, attn_sinks,
                                             pool_w, pool_scale, bias)
        if layer == 0:
            land = _gather_wait(sems[0], later[0], lands[0], SIBLING_AND_SAME_CORE, z, "gather_w_in1_wait")
            fsems, land, token = _forward_start(land, IN_SHARD, "forward_w_in1_start")
        else:
            wout_full[layer] = _gather_wait(sems[1], later[1], lands[1], ALL_PEERS, z, "gather_w_out1_wait")
        x_next, y = _fwd_out(layer, z, xl, norm_post, wout_full[layer], token)
        if layer == 0:
            win_full[1] = _forward_wait(fsems, land, IN_SHARD, x_next, "forward_w_in1_wait")
        saved.append((xl, u, pg, q, k, v, ag, z, a, y))
        xl = x_next

    params_small = [pool_w, pool_scale, attn_sinks, norm_pre, norm_post,
                    m_pool_w, m_pool_scale, m_attn_sinks, m_norm_pre, m_norm_post,
                    v_pool_w, v_pool_scale, v_attn_sinks, v_norm_pre, v_norm_post]

    def start(srcs, directs, paired, collective_id, tag):
        send_sem, recv_sem, srcs, lands, started = _exchange_start(srcs, directs, collective_id, f"exchange_start{tag}")
        return (send_sem, recv_sem, srcs, lands, paired, directs), started

    def finish(handle, after, tag):
        send_sem, recv_sem, srcs, lands, paired, directs = handle
        srcs, lands = _exchange_wait(send_sem, recv_sem, srcs, lands, directs, after, f"exchange_wait{tag}")
        return [s if p is None else p for s, p in zip(srcs, paired)], lands

    def back(layer, top, first, second):
        xin, u, pg, q, k, v, ag, z, a, y = saved[layer]
        return _bwd_back(layer, top, first, second, y, z, norm_post, wout_full[layer], attn_sinks, u, pg, q, k, v,
                         ag, a, pool_w, pool_scale, bias)

    dgpre = [None] * DEPTH
    dx, dproj, gw_out, pack = back(1, True, xl, target)
    dx, dgpre[1], chip_sums, own_sum = _bwd_in(1, "both_pair", token, dproj, saved[1][0], norm_pre, dx, win_full[1])
    top, token = start([chip_sums, gw_out.reshape(N_DEV, OUT_SHARD, D_MODEL), pack.reshape(N_DEV, PACK_SLICE, 128)],
                       [False, True, True], [own_sum, None, None], COLLECTIVE_EXCHANGE_1, "1")

    dproj, gw_out, pack = back(0, False, dx, token)
    early, token = start([gw_out.reshape(N_DEV, OUT_SHARD, D_MODEL), pack.reshape(N_DEV, PACK_SLICE, 128)],
                         [True, True], [None, None], COLLECTIVE_EXCHANGE_0A, "0a")
    chip_sums, own_sum = _bwd_in(0, "dw_pair", token, dproj, saved[0][0], norm_pre)
    late, token = start([chip_sums], [False], [own_sum], COLLECTIVE_EXCHANGE_0B, "0b")
    dx, dgpre[0] = _bwd_in(0, "dx", token, dproj, saved[0][0], norm_pre, dx, win_full[0])

    own1, lands1 = finish(top, dx, "1")
    big_in = _adamw_layer(1, own1[0], lands1[0], w_in_t, m_in_t, v_in_t, None, token, "adamw_in1", ADAM_ROWS_IN)
    big_out = _adamw_layer(1, own1[1], lands1[1], w_out, m_w_out, v_w_out, None, token, "adamw_out1", ADAM_ROWS_OUT)
    own0a, lands0a = finish(early, big_out[0], "0a")
    sems, block, land, token = _small_gather_start([own0a[1], own1[2]], [lands0a[1], lands1[2]], dgpre,
                                                   "gather_small_start")
    big_out = _adamw_layer(0, own0a[0], lands0a[0], w_out, m_w_out, v_w_out, big_out, token, "adamw_out0", ADAM_ROWS_OUT)
    own0b, lands0b = finish(late, big_out[0], "0b")
    big_in = _adamw_layer(0, own0b[0], lands0b[0], w_in_t, m_in_t, v_in_t, big_in, token, "adamw_in0", ADAM_ROWS_IN)
    gathered = _gather_wait(sems, block, land, ALL_PEERS, big_in[0], "gather_small_wait")
    small_out = _adamw_small(gathered, params_small)
    loss = small_out[0].reshape(())

    outs = [loss, dx.reshape(1, SEQ, D_MODEL)]
    for t in range(4):
        pw_, sc_, sk_, npre_, npost_ = small_out[1 + 5 * t:6 + 5 * t]
        outs += [jnp.swapaxes(big_in[t], 1, 2), pw_, sc_, sk_, big_out[t], npre_, npost_]
    return tuple(outs)
```

```python
import numpy as np
import jax
import jax.numpy as jnp
from jax import lax
from jax.experimental import pallas as pl
from jax.experimental.pallas import tpu as pltpu

F32 = jnp.float32
BF16 = jnp.bfloat16

N_DEV = 8
SEQ = 2048
D_MODEL = 1024
D_POOL = 512
D_ATTN = 512
D_KV = 128
D_IN = 2304
N_HEADS = 8
GQA = 4
HEAD_DIM = 64
BLOCK = 128
POOL_WINDOWS = (2, 4, 8, 16)
DEPTH = 2
EPS = 1e-6
NEG_INF = -1e30
SCALE = HEAD_DIM ** -0.5
IN_SHARD = D_IN // N_DEV
OUT_SHARD = D_MODEL // N_DEV

COL_U, COL_PG, COL_Q, COL_K, COL_V, COL_AG = 0, 512, 1024, 1536, 1664, 1792

ADAM_LR = 0.001
ADAM_B1 = 0.9
ADAM_B2 = 0.999
ADAM_EPS = 1e-08
ADAM_WD = 0.01
ADAM_STEP = 10

TOKEN_TILE = 512
FWD_OUT_TILE = 1024
ADAM_ROWS_IN, ADAM_ROWS_OUT = 144, 64
VMEM_LIMIT = 56 * 1024 * 1024
MESH = pl.DeviceIdType.MESH

ROW_PW, ROW_SC, ROW_SINK, ROW_NPOST, ROW_LOSS = 0, 512, 520, 536, 544
PACK_ROWS = 576
PACK_SLICE = PACK_ROWS // N_DEV


def _nn(a, b):
    return jnp.dot(a, b, preferred_element_type=F32)


def _nt(a, b):
    return lax.dot_general(a, b, (((1,), (1,)), ((), ())), preferred_element_type=F32)


def _tn(a, b):
    return lax.dot_general(a, b, (((0,), (0,)), ((), ())), preferred_element_type=F32)


def _silu_parts(g):
    s = jax.nn.sigmoid(g)
    return g * s, s * (1.0 + g * (1.0 - s))


def _resident(shape):
    return pl.BlockSpec(shape, lambda *_: (0,) * len(shape), pipeline_mode=pl.Buffered(1))


def _compiler_params(sem=None):
    if sem is None:
        return pltpu.CompilerParams(vmem_limit_bytes=VMEM_LIMIT)
    return pltpu.CompilerParams(dimension_semantics=sem, vmem_limit_bytes=VMEM_LIMIT)


def _attn_bias():
    t = np.arange(BLOCK)[None, :]
    j = np.arange(BLOCK)[:, None]
    current = j <= t
    dist = np.where(current, t - j, t + BLOCK - j).astype(np.float32)
    out = np.zeros((2, 2, BLOCK, GQA * BLOCK), np.float32)
    for variant in range(2):
        valid = current | (variant == 1)
        for kv in range(2):
            for g in range(GQA):
                slope = np.float32(2.0 ** (-(kv * GQA + g + 1)))
                out[variant, kv, :, g * BLOCK:(g + 1) * BLOCK] = np.where(valid, -slope * dist, np.float32(NEG_INF))
    return out


def _replicate_head(kx, kv):
    rolled = pltpu.roll(kx, 64, 1)
    lane = lax.broadcasted_iota(jnp.int32, kx.shape, 1)
    dup = jnp.where(lane < 64, kx, rolled) if kv == 0 else jnp.where(lane < 64, rolled, kx)
    return jnp.concatenate([dup, dup], axis=1).astype(BF16)


def _stack_heads(qv):
    lane = lax.broadcasted_iota(jnp.int32, qv.shape, 1)
    zero = jnp.zeros_like(qv)
    return jnp.concatenate([jnp.where((lane >= 64 * g) & (lane < 64 * g + 64), qv, zero) for g in range(GQA)], axis=0)


def _unstack_heads(xs):
    lane = lax.broadcasted_iota(jnp.int32, (BLOCK, 256), 1)
    return jnp.where(lane < 64, xs[0:128], jnp.where(lane < 128, xs[128:256], jnp.where(lane < 192, xs[256:384], xs[384:512])))


def _fold_heads(r):
    h = r[:, 0:128] + r[:, 128:256]
    return h + pltpu.roll(h, 64, 1)


def _sink_row(sink_ref, layer, kv):
    lane = lax.broadcasted_iota(jnp.int32, (1, GQA * BLOCK), 1)
    s4 = [sink_ref[layer, kv * GQA + g] for g in range(GQA)]
    return jnp.where(lane < 128, s4[0], jnp.where(lane < 256, s4[1], jnp.where(lane < 384, s4[2], s4[3])))


def _band_is_current():
    j = lax.broadcasted_iota(jnp.int32, (BLOCK, GQA * BLOCK), 0)
    t = lax.broadcasted_iota(jnp.int32, (BLOCK, GQA * BLOCK), 1) & (BLOCK - 1)
    return j <= t


def _pack_band(full, current):
    return jnp.where(current, full[BLOCK:], full[:BLOCK])


def _unpack_band(packed, current):
    zero = jnp.zeros_like(packed)
    return jnp.concatenate([jnp.where(current, zero, packed), jnp.where(current, packed, zero)], axis=0)


def _probs_keys_major(k_rep, q_st, bias, sink, current):
    st = _pack_band(_nt(k_rep, q_st), current) * SCALE + bias
    m = jnp.maximum(jnp.max(st, axis=0, keepdims=True), sink)
    p = jnp.exp(st - m)
    esink = jnp.exp(sink - m)
    rl = 1.0 / (jnp.sum(p, axis=0, keepdims=True) + esink)
    return p * rl, esink * rl


WINDOW_HALO = 16


def _window_sum(ext, w, forward):
    s = ext
    sh = 1
    while sh < w:
        s = s + pltpu.roll(s, (ext.shape[0] - sh) if forward else sh, 0)
        sh *= 2
    return s


def _inv_count(n, w):
    t = n * BLOCK + lax.broadcasted_iota(jnp.int32, (BLOCK, 1), 0) + 1
    return 1.0 / jnp.minimum(t.astype(F32), float(w))


def _kv_ext(ref, n):
    r0 = pl.multiple_of(jnp.maximum(n - 1, 0) * BLOCK, BLOCK)
    r1 = pl.multiple_of(n * BLOCK, BLOCK)
    return jnp.concatenate([ref[pl.ds(r0, BLOCK), :], ref[pl.ds(r1, BLOCK), :]], axis=0)


def _rows_of(vec_ref, pack_ref, row0):
    for r in range(D_MODEL // 128):
        pack_ref[row0 + r:row0 + r + 1, :] = vec_ref[:, 128 * r:128 * (r + 1)]


FRONT_TILE = 4 * BLOCK


def _fwd_front(layer, x, norm_pre, w_in_t, token, sinks, pool_w, pool_scale, bias):
    tm = FRONT_TILE

    def body(sink_ref, x_ref, g_ref, w_ref, _, pw_ref, sc_ref, bias_ref,
             u_ref, pg_ref, q_ref, k_ref, v_ref, ag_ref, z_ref, a_ref, uprev, kprev, vprev):
        i = pl.program_id(0)

        @pl.when(i == 0)
        def _():
            uprev[...] = jnp.zeros_like(uprev)
            kprev[...] = jnp.zeros_like(kprev)
            vprev[...] = jnp.zeros_like(vprev)

        xv = x_ref[...]
        r = lax.rsqrt(jnp.mean(xv * xv, axis=-1, keepdims=True) + EPS)
        h = (xv * r * g_ref[layer:layer + 1, :]).astype(BF16)
        u_ref[...] = _nt(h, w_ref[COL_U:COL_PG, :])
        pg_ref[...] = _nt(h, w_ref[COL_PG:COL_Q, :])
        for sb in range(tm // BLOCK):
            n = (tm // BLOCK) * i + sb
            rows = slice(BLOCK * sb, BLOCK * (sb + 1))
            before = slice(BLOCK * (sb - 1), BLOCK * sb)
            uv = u_ref[rows, :]
            halo = (uprev[BLOCK - WINDOW_HALO:, :] if sb == 0
                    else u_ref[BLOCK * sb - WINDOW_HALO:BLOCK * sb, :])
            ext = jnp.concatenate([halo, uv], axis=0)
            for g, w in enumerate(POOL_WINDOWS):
                cs = slice(BLOCK * g, BLOCK * (g + 1))
                win = _window_sum(ext[:, cs], w, forward=False)[WINDOW_HALO:]
                pooled = win * _inv_count(n, w) - uv[:, cs]
                mixed = _nn(pooled.astype(BF16), pw_ref[g].astype(BF16))
                gate, _ = _silu_parts(pg_ref[rows, cs])
                z_ref[rows, cs] = (mixed * sc_ref[layer:layer + 1, cs] * gate).astype(BF16)

        q_ref[...] = _nt(h, w_ref[COL_Q:COL_K, :]).astype(BF16)
        k_ref[...] = _nt(h, w_ref[COL_K:COL_V, :])
        v_ref[...] = _nt(h, w_ref[COL_V:COL_AG, :])
        ag_ref[...] = _nt(h, w_ref[COL_AG:D_IN, :])

        current = _band_is_current()
        for sb in range(tm // BLOCK):
            n = (tm // BLOCK) * i + sb
            rows = slice(BLOCK * sb, BLOCK * (sb + 1))
            before = slice(BLOCK * (sb - 1), BLOCK * sb)
            kx = jnp.concatenate([kprev[...] if sb == 0 else k_ref[before, :], k_ref[rows, :]], axis=0)
            vx = jnp.concatenate([vprev[...] if sb == 0 else v_ref[before, :], v_ref[rows, :]], axis=0)
            variant = jnp.minimum(n, 1) if sb == 0 else 1
            for kv in range(2):
                cs = slice(256 * kv, 256 * (kv + 1))
                p, _ = _probs_keys_major(_replicate_head(kx, kv), _stack_heads(q_ref[rows, cs]),
                                         bias_ref[variant, kv], _sink_row(sink_ref, layer, kv), current)
                o = _unstack_heads(_tn(_unpack_band(p.astype(BF16), current), _replicate_head(vx, kv)))
                a_ref[rows, cs] = o
                gate, _ = _silu_parts(ag_ref[rows, cs])
                z_ref[rows, D_POOL + 256 * kv:D_POOL + 256 * (kv + 1)] = (o * gate).astype(BF16)

        tail = slice(tm - BLOCK, tm)
        uprev[...] = u_ref[tail, :]
        kprev[...] = k_ref[tail, :]
        vprev[...] = v_ref[tail, :]

    row = lambda c: pl.BlockSpec((tm, c), lambda i: (i, 0))
    const = lambda shape: pl.BlockSpec(shape, lambda i: (0,) * len(shape))
    return pl.pallas_call(
        body, name=f"fwd_front{layer}", grid=(SEQ // tm,),
        in_specs=[pl.BlockSpec(memory_space=pltpu.SMEM), row(D_MODEL), const((DEPTH, D_MODEL)),
                  _resident((D_IN, D_MODEL)), const((8, 128)),
                  pl.BlockSpec((None, 4, BLOCK, BLOCK), lambda i: (layer, 0, 0, 0)), const((DEPTH, D_POOL)),
                  _resident((2, 2, BLOCK, GQA * BLOCK))],
        out_specs=[row(D_POOL), row(D_POOL), row(D_ATTN), row(D_KV), row(D_KV), row(D_ATTN), row(D_MODEL),
                   row(D_ATTN)],
        out_shape=[jax.ShapeDtypeStruct((SEQ, D_POOL), F32), jax.ShapeDtypeStruct((SEQ, D_POOL), F32),
                   jax.ShapeDtypeStruct((SEQ, D_ATTN), BF16), jax.ShapeDtypeStruct((SEQ, D_KV), F32),
                   jax.ShapeDtypeStruct((SEQ, D_KV), F32), jax.ShapeDtypeStruct((SEQ, D_ATTN), F32),
                   jax.ShapeDtypeStruct((SEQ, D_MODEL), BF16), jax.ShapeDtypeStruct((SEQ, D_ATTN), F32)],
        scratch_shapes=[pltpu.VMEM((BLOCK, D_POOL), F32), pltpu.VMEM((BLOCK, D_KV), F32),
                        pltpu.VMEM((BLOCK, D_KV), F32)],
        compiler_params=_compiler_params(("arbitrary",)),
    )(sinks, x, norm_pre, w_in_t, token, pool_w, pool_scale, bias)


def _fwd_out(layer, z, x, norm_post, w_out, token):
    tm = FWD_OUT_TILE

    def body(z_ref, x_ref, g_ref, w_ref, _, xn_ref, y_ref):
        y = _nn(z_ref[...], w_ref[...])
        y_ref[...] = y
        r = lax.rsqrt(jnp.mean(y * y, axis=-1, keepdims=True) + EPS)
        xn_ref[...] = x_ref[...] + y * r * g_ref[layer:layer + 1, :]

    row = lambda c: pl.BlockSpec((tm, c), lambda i: (i, 0))
    return pl.pallas_call(
        body, name=f"fwd_out{layer}", grid=(SEQ // tm,),
        in_specs=[row(D_MODEL), row(D_MODEL), pl.BlockSpec((DEPTH, D_MODEL), lambda i: (0, 0)),
                  _resident((D_MODEL, D_MODEL)), pl.BlockSpec((8, 128), lambda i: (0, 0))],
        out_specs=[row(D_MODEL), row(D_MODEL)],
        out_shape=[jax.ShapeDtypeStruct((SEQ, D_MODEL), F32), jax.ShapeDtypeStruct((SEQ, D_MODEL), F32)],
        compiler_params=_compiler_params(("arbitrary",)),
    )(z, x, norm_post, w_out, token)


BACK_TILE = 2 * BLOCK


def _bwd_back(layer, top, dxo_or_xf, target_or_token, y, z, norm_post, w_out, sinks, u, pg, q, k, v, ag, a,
              pool_w, pool_scale, bias):
    tm = BACK_TILE
    steps = SEQ // tm
    last = steps - 1
    per = tm // BLOCK

    def body(*refs):
        refs = list(refs)
        sink_ref, first, second = refs[:3]
        (y_ref, z_ref, g_ref, w_ref, u_ref, up_ref, pg_ref, q_ref, k_ref, v_ref, ag_ref, a_ref, pw_ref, sc_ref,
         bias_ref) = refs[3:18]
        del refs[:18]
        dxo_ref = refs.pop(0) if top else None
        dp_ref, dw_ref, pack_ref, acc, dg, lacc, dzs, ck, cv, ce = refs
        i = pl.program_id(0)
        blk = last - i

        @pl.when(i == 0)
        def _():
            acc[...] = jnp.zeros_like(acc)
            dg[...] = jnp.zeros_like(dg)
            lacc[...] = jnp.zeros_like(lacc)
            pack_ref[...] = jnp.zeros_like(pack_ref)
            ck[...] = jnp.zeros_like(ck)
            cv[...] = jnp.zeros_like(cv)
            ce[...] = jnp.zeros_like(ce)

        if top:
            d = first[...] - second[...]
            dxo_v = d * (1.0 / D_MODEL)
            dxo_ref[...] = dxo_v
            part = jnp.sum(d * d, axis=-1, keepdims=True) * (1.0 / D_MODEL)
            lacc[...] += 0.5 * jnp.sum(part, axis=0, keepdims=True)
        else:
            dxo_v = first[...]
        yv = y_ref[...]
        r = lax.rsqrt(jnp.mean(yv * yv, axis=-1, keepdims=True) + EPS)
        yn = yv * r
        dg[...] += jnp.sum(dxo_v * yn, axis=0, keepdims=True)
        dyn = dxo_v * g_ref[layer:layer + 1, :]
        dy = (r * (dyn - yn * jnp.mean(dyn * yn, axis=-1, keepdims=True))).astype(BF16)
        dzs[...] = _nt(dy, w_ref[...])
        acc[...] += _tn(z_ref[...], dy)

        lane = lax.broadcasted_iota(jnp.int32, (1, 128), 1)
        lane2 = lax.broadcasted_iota(jnp.int32, (256, 128), 1)
        current = _band_is_current()
        for sb in reversed(range(per)):
            n = per * blk + sb
            rows = slice(BLOCK * sb, BLOCK * (sb + 1))

            uv = u_ref[rows, :]
            if sb == 0:
                halo = up_ref[BLOCK - WINDOW_HALO:, :] * (n > 0).astype(F32)
            else:
                halo = u_ref[BLOCK * sb - WINDOW_HALO:BLOCK * sb, :]
            ext = jnp.concatenate([halo, uv], axis=0)
            for g, w in enumerate(POOL_WINDOWS):
                cs = slice(BLOCK * g, BLOCK * (g + 1))
                inv = _inv_count(n, w)
                win = _window_sum(ext[:, cs], w, forward=False)[WINDOW_HALO:]
                pooled = win * inv - uv[:, cs]
                pw_g = pw_ref[g].astype(BF16)
                mixed = _nn(pooled.astype(BF16), pw_g)
                gate, dgate = _silu_parts(pg_ref[rows, cs])
                dzp = dzs[rows, cs]
                sc = sc_ref[layer:layer + 1, cs]
                dpm = dzp * gate
                dp_ref[rows, COL_PG + BLOCK * g:COL_PG + BLOCK * (g + 1)] = (dzp * (mixed * sc) * dgate).astype(BF16)
                pack_ref[ROW_SC + g:ROW_SC + g + 1, :] += jnp.sum(dpm * mixed, axis=0, keepdims=True)
                dmixed = (dpm * sc).astype(BF16)
                pack_ref[ROW_PW + BLOCK * g:ROW_PW + BLOCK * (g + 1), :] += _tn(pooled.astype(BF16), dmixed)
                dpooled = _nt(dmixed, pw_g)
                e = dpooled * inv
                lead = _window_sum(jnp.concatenate([e, ce[:WINDOW_HALO, cs]], axis=0), w, forward=True)[:BLOCK]
                dp_ref[rows, COL_U + BLOCK * g:COL_U + BLOCK * (g + 1)] = (lead - dpooled).astype(BF16)
                ce[:, cs] = e

            kx = _kv_ext(k_ref, n)
            vx = _kv_ext(v_ref, n)
            variant = jnp.minimum(n, 1) if sb == 0 else 1
            dsink_row = jnp.zeros((1, 128), F32)
            tks, tvs = [], []
            for kv in range(2):
                cs = slice(256 * kv, 256 * (kv + 1))
                k_rep = _replicate_head(kx, kv)
                v_rep = _replicate_head(vx, kv)
                q_st = _stack_heads(q_ref[rows, cs])
                gate, dgate = _silu_parts(ag_ref[rows, cs])
                dza = dzs[rows, D_POOL + 256 * kv:D_POOL + 256 * (kv + 1)]
                dp_ref[rows, COL_AG + 256 * kv:COL_AG + 256 * (kv + 1)] = (dza * a_ref[rows, cs] * dgate).astype(BF16)
                da_st = _stack_heads((dza * gate).astype(BF16))
                p, psink = _probs_keys_major(k_rep, q_st, bias_ref[variant, kv], _sink_row(sink_ref, layer, kv),
                                             current)
                dpt = _pack_band(_nt(v_rep, da_st), current)
                delta = jnp.sum(p * dpt, axis=0, keepdims=True)
                dst = _unpack_band((p * (dpt - delta) * SCALE).astype(BF16), current)
                sink_terms = psink * delta
                for g in range(GQA):
                    dsink = -jnp.sum(sink_terms[:, BLOCK * g:BLOCK * (g + 1)], axis=1, keepdims=True)
                    dsink_row = dsink_row + jnp.where(lane == kv * GQA + g, dsink, 0.0)
                dp_ref[rows, COL_Q + 256 * kv:COL_Q + 256 * (kv + 1)] = _unstack_heads(_tn(dst, k_rep)).astype(BF16)
                tks.append(_fold_heads(_nn(dst, q_st)))
                tvs.append(_fold_heads(_nn(_unpack_band(p.astype(BF16), current), da_st)))
            pack_ref[ROW_SINK:ROW_SINK + 1, :] += dsink_row
            dkx = jnp.where(lane2 < 64, tks[0], tks[1])
            dvx = jnp.where(lane2 < 64, tvs[0], tvs[1])
            dp_ref[rows, COL_K:COL_V] = (ck[...] + dkx[BLOCK:]).astype(BF16)
            dp_ref[rows, COL_V:COL_AG] = (cv[...] + dvx[BLOCK:]).astype(BF16)
            ck[...] = dkx[:BLOCK]
            cv[...] = dvx[:BLOCK]

        @pl.when(i == steps - 1)
        def _():
            dw_ref[...] = acc[...].astype(BF16)
            _rows_of(dg, pack_ref, ROW_NPOST)
            pack_ref[ROW_LOSS:ROW_LOSS + 1, :] = jnp.where(lane == 0, lacc[...], 0.0)

    row = lambda c: pl.BlockSpec((tm, c), lambda i: (last - i, 0))
    const = lambda shape: pl.BlockSpec(shape, lambda i: (0,) * len(shape))
    act = jax.ShapeDtypeStruct((SEQ, D_MODEL), F32)
    return pl.pallas_call(
        body, name=f"bwd_back{layer}", grid=(steps,),
        in_specs=[pl.BlockSpec(memory_space=pltpu.SMEM), row(D_MODEL), row(D_MODEL) if top else const((8, 128)),
                  row(D_MODEL), row(D_MODEL), const((DEPTH, D_MODEL)), _resident((D_MODEL, D_MODEL)),
                  row(D_POOL), pl.BlockSpec((BLOCK, D_POOL), lambda i: (jnp.maximum(per * (last - i) - 1, 0), 0)),
                  row(D_POOL), row(D_ATTN), _resident((SEQ, D_KV)), _resident((SEQ, D_KV)), row(D_ATTN), row(D_ATTN),
                  pl.BlockSpec((None, 4, BLOCK, BLOCK), lambda i: (layer, 0, 0, 0)), const((DEPTH, D_POOL)),
                  _resident((2, 2, BLOCK, GQA * BLOCK))],
        out_specs=([row(D_MODEL)] * (1 if top else 0)
                   + [row(D_IN), const((D_MODEL, D_MODEL)), const((PACK_ROWS, 128))]),
        out_shape=([act] * (1 if top else 0)
                   + [jax.ShapeDtypeStruct((SEQ, D_IN), BF16), jax.ShapeDtypeStruct((D_MODEL, D_MODEL), BF16),
                      jax.ShapeDtypeStruct((PACK_ROWS, 128), F32)]),
        scratch_shapes=[pltpu.VMEM((D_MODEL, D_MODEL), F32), pltpu.VMEM((1, D_MODEL), F32), pltpu.VMEM((1, 1), F32),
                        pltpu.VMEM((tm, D_MODEL), F32), pltpu.VMEM((BLOCK, D_KV), F32), pltpu.VMEM((BLOCK, D_KV), F32),
                        pltpu.VMEM((BLOCK, D_POOL), F32)],
        compiler_params=_compiler_params(("arbitrary",)),
    )(sinks, dxo_or_xf, target_or_token, y, z, norm_post, w_out, u, u, pg, q, k, v, ag, a, pool_w, pool_scale, bias)


def _bwd_in(layer, part, token, dproj, x, norm_pre, dxo=None, w_in_t=None):
    pair = part in ("dw_pair", "both_pair")
    want_dw, want_dx = part != "dx", part in ("both", "dx", "both_pair")
    tm = TOKEN_TILE
    steps = SEQ // tm
    cw = 256

    def body(*refs):
        refs = list(refs)
        dp_ref, x_ref, g_ref = refs[1:4]
        del refs[:4]
        if want_dx:
            dxo_ref, w_ref, dx_ref, dgo_ref = refs[:4]
            del refs[:4]
            dg = refs.pop()
        if pair:
            hs_ref, hm_ref, acc, mine_buf, theirs_buf, send_sem, recv_sem = refs
        elif want_dw:
            dw_ref, acc = refs
        i = pl.program_id(0)

        @pl.when(i == 0)
        def _():
            if pair:
                _handshake([(lax.axis_index("x"), lax.axis_index("y"), 1 - lax.axis_index("c"))])
            if want_dw:
                acc[...] = jnp.zeros_like(acc)
            if want_dx:
                dg[...] = jnp.zeros_like(dg)

        xv = x_ref[...]
        gv = g_ref[layer:layer + 1, :]
        r = lax.rsqrt(jnp.mean(xv * xv, axis=-1, keepdims=True) + EPS)
        xn = xv * r
        if want_dw:
            hb = (xn * gv).astype(BF16)
            for c in range(0, D_IN, cw):
                acc[c:c + cw, :] += _tn(dp_ref[:, c:c + cw], hb)
        def rows_for(q, core):
            return pl.ds(pl.multiple_of((2 * q + core) * IN_SHARD, 8), IN_SHARD)

        def swap(q):
            x, y, c = _mesh_pos()
            return pltpu.make_async_remote_copy(
                src_ref=mine_buf.at[q], dst_ref=theirs_buf.at[q], send_sem=send_sem.at[q], recv_sem=recv_sem.at[q],
                device_id=(x, y, 1 - c), device_id_type=MESH)

        if pair:
            @pl.when(i == steps - 1)
            def _():
                for q in range(4):
                    mine_buf[q] = acc[rows_for(q, 1 - lax.axis_index("c")), :].astype(BF16)
                    swap(q).start()

        if want_dx:
            dh = _nn(dp_ref[...], w_ref[...])
            dg[...] += jnp.sum(dh * xn, axis=0, keepdims=True)
            dhn = dh * gv
            dx_ref[...] = dxo_ref[...] + r * (dhn - xn * jnp.mean(dhn * xn, axis=-1, keepdims=True))

        @pl.when(i == steps - 1)
        def _():
            if pair:
                x, y, c = _mesh_pos()
                for q in range(4):
                    swap(q).wait()
                for j, q in enumerate([2 * (1 - x) + y, 2 * x + (1 - y), 2 * (1 - x) + (1 - y)]):
                    hs_ref[j] = (acc[rows_for(q, c), :] + theirs_buf[q].astype(F32)).astype(BF16)
                hm_ref[...] = acc[rows_for(2 * x + y, c), :] + theirs_buf[2 * x + y].astype(F32)
            elif want_dw:
                dw_ref[...] = acc[...].astype(BF16)
            if want_dx:
                _rows_of(dg, dgo_ref, 0)

    row = lambda c: pl.BlockSpec((tm, c), lambda i: (i, 0))
    const = lambda shape: pl.BlockSpec(shape, lambda i: (0,) * len(shape))
    in_specs = [const((8, 128)), row(D_IN), row(D_MODEL), const((DEPTH, D_MODEL))]
    operands = [token, dproj, x, norm_pre]
    out_specs, out_shape, scratch = [], [], []
    if want_dx:
        in_specs += [row(D_MODEL), _resident((D_IN, D_MODEL))]
        operands += [dxo, w_in_t]
        out_specs += [row(D_MODEL), const((8, 128))]
        out_shape += [jax.ShapeDtypeStruct((SEQ, D_MODEL), F32), jax.ShapeDtypeStruct((8, 128), F32)]
    if pair:
        out_specs += [const((3, IN_SHARD, D_MODEL)), const((IN_SHARD, D_MODEL))]
        out_shape += [jax.ShapeDtypeStruct((3, IN_SHARD, D_MODEL), BF16), jax.ShapeDtypeStruct((IN_SHARD, D_MODEL), F32)]
        scratch += [pltpu.VMEM((D_IN, D_MODEL), F32), pltpu.VMEM((4, IN_SHARD, D_MODEL), BF16),
                    pltpu.VMEM((4, IN_SHARD, D_MODEL), BF16), pltpu.SemaphoreType.DMA((4,)), pltpu.SemaphoreType.DMA((4,))]
    elif want_dw:
        out_specs.append(const((D_IN, D_MODEL)))
        out_shape.append(jax.ShapeDtypeStruct((D_IN, D_MODEL), BF16))
        scratch.append(pltpu.VMEM((D_IN, D_MODEL), F32))
    if want_dx:
        scratch.append(pltpu.VMEM((1, D_MODEL), F32))
    params = pltpu.CompilerParams(dimension_semantics=("arbitrary",), vmem_limit_bytes=VMEM_LIMIT,
                                  collective_id=COLLECTIVE_PAIR_SUM[layer] if pair else None)
    return pl.pallas_call(
        body, name=f"bwd_in_{part}{layer}", grid=(steps,),
        in_specs=in_specs, out_specs=out_specs, out_shape=out_shape, scratch_shapes=scratch,
        compiler_params=params,
    )(*operands)


def _mesh_pos():
    return lax.axis_index("x"), lax.axis_index("y"), lax.axis_index("c")


def _device_rows(ref, m, px, py, pc):
    return ref.at[pl.ds(pl.multiple_of((4 * px + 2 * py + pc) * m, 16 if m % 16 == 0 else 8), m), :]


def _allgather(srcs, out_dtype, name, later=()):
    na, nb = len(srcs), len(later)
    every = list(srcs) + list(later)
    shapes = [(a.shape[-2], a.shape[-1]) for a, _ in every]

    def body(*refs):
        xs, refs = refs[:na + nb], refs[na + nb:]
        outs, cast, land, refs = refs[:na], refs[na:na + nb], refs[na + nb:na + 2 * nb], refs[na + 2 * nb:]
        stage, raw, (send_sems, recv_sems, local_sems, load_sems) = refs[:na], refs[na:2 * na + nb], refs[2 * na + nb:]
        loads = [pltpu.make_async_copy(xs[i].at[every[i][1]], raw[i], load_sems.at[i]) for i in range(na + nb)]
        for cp in loads:
            cp.start()
        x, y, c = _mesh_pos()
        me, sibling = (x, y, c), (x, y, 1 - c)
        near = [(1 - x, y), (x, 1 - y)]
        far = (1 - x, 1 - y)
        relay_from, relay_to = (x ^ (1 - c), y ^ c), (x ^ c, y ^ (1 - c))
        _handshake([sibling] + [(*chip, c) for chip in near])
        k_from, k_to = 1 + c, 2 - c

        def slot(a, px, py, pc):
            return _device_rows(outs[a], shapes[a][0], px, py, pc)

        def copy(a, k, block, to, src=None):
            return pltpu.make_async_remote_copy(
                src_ref=slot(a, *block) if src is None else src, dst_ref=slot(a, *block),
                send_sem=send_sems.at[a, k], recv_sem=recv_sems.at[a, k], device_id=to, device_id_type=MESH)

        def cast_block(i):
            loads[i].wait()
            return raw[i][...].astype(out_dtype)

        for a in range(na):
            stage[a][...] = cast_block(a)
        mine = [pltpu.make_async_copy(stage[a], slot(a, *me), local_sems.at[a]) for a in range(na)]
        for cp in mine:
            cp.start()
        sent = []
        for a in range(na):
            sent.append(copy(a, 0, me, sibling, src=stage[a]))
            sent += [copy(a, 1 + j, me, (*chip, c), src=stage[a]) for j, chip in enumerate(near)]
        for cp in sent:
            cp.start()
        for b in range(nb):
            cast[b][...] = cast_block(na + b)
            cp = pltpu.make_async_copy(cast[b], _device_rows(land[b], shapes[na + b][0], *me), local_sems.at[na + b])
            cp.start()
            mine.append(cp)
        for a in range(na):
            copy(a, k_from, (*relay_from, c), me).wait_recv()
            sent += [copy(a, 3, (*relay_from, c), (*relay_to, c)), copy(a, 3 + k_from, (*relay_from, c), sibling)]
            sent[-2].start()
            sent[-1].start()
        for a in range(na):
            copy(a, k_to, (*relay_to, c), me).wait_recv()
            sent.append(copy(a, 3 + k_to, (*relay_to, c), sibling))
            sent[-1].start()
        for a in range(na):
            copy(a, 3, (*far, c), me).wait_recv()
            sent.append(copy(a, 6, (*far, c), sibling))
            sent[-1].start()
        for a in range(na):
            copy(a, 0, sibling, me).wait_recv()
            for j, chip in enumerate(near + [far]):
                copy(a, 4 + j, (*chip, 1 - c), me).wait_recv()
        for cp in sent:
            cp.wait_send()
        for cp in mine:
            cp.wait()

    vmem = pl.BlockSpec(memory_space=pltpu.VMEM)
    hbm = pl.BlockSpec(memory_space=pl.ANY)
    gathered = [jax.ShapeDtypeStruct((N_DEV * m, n), out_dtype) for m, n in shapes]
    out = pl.pallas_call(
        body, name=name,
        in_specs=[hbm] * (na + nb),
        out_specs=[hbm] * na + [vmem] * nb + [hbm] * nb,
        out_shape=gathered[:na] + [jax.ShapeDtypeStruct(s, out_dtype) for s in shapes[na:]] + gathered[na:],
        scratch_shapes=([pltpu.VMEM(s, out_dtype) for s in shapes[:na]]
                        + [pltpu.VMEM(s, a.dtype) for s, (a, _) in zip(shapes, every)]
                        + [pltpu.SemaphoreType.DMA((na, 7)), pltpu.SemaphoreType.DMA((na, 7)),
                           pltpu.SemaphoreType.DMA((na + nb,)), pltpu.SemaphoreType.DMA((na + nb,))]),
        compiler_params=pltpu.CompilerParams(vmem_limit_bytes=VMEM_LIMIT, collective_id=COLLECTIVE_GATHER_W0),
    )(*[a for a, _ in every])
    return out[:na], out[na:na + nb], out[na + nb:]


ALL_PEERS = tuple(range(1, N_DEV))
SIBLING_AND_SAME_CORE = (1, 2, 4, 6)


def _related(k, x, y, c):
    return x ^ ((k >> 2) & 1), y ^ ((k >> 1) & 1), c ^ (k & 1)


def _gather_start(blocks, lands, relations, collective_id, name):
    na = len(blocks)

    def body(*refs):
        src, land, sems, token = refs[:na], refs[na:2 * na], refs[2 * na:4 * na], refs[-1]
        x, y, c = _mesh_pos()
        _handshake([_related(k, x, y, c) for k in sorted(set().union(*relations))])
        for a in range(na):
            for k in relations[a]:
                pltpu.make_async_remote_copy(
                    src_ref=src[a], dst_ref=_device_rows(land[a], blocks[a].shape[0], x, y, c),
                    send_sem=sems[2 * a].at[k - 1], recv_sem=sems[2 * a + 1].at[k - 1],
                    device_id=_related(k, x, y, c), device_id_type=MESH).start()
        token[...] = jnp.zeros_like(token)

    bufs = [pltpu.HBM(t.shape, t.dtype) for t in list(blocks) + list(lands)]
    out = pl.pallas_call(
        body, name=name,
        out_shape=(*([pltpu.SemaphoreType.DMA((N_DEV - 1,))] * (2 * na)), *bufs, jax.ShapeDtypeStruct((8, 128), F32)),
        in_specs=[_HBM] * (2 * na),
        out_specs=(*([_SEM] * (2 * na)), *([_HBM] * (2 * na)), pl.BlockSpec(memory_space=pltpu.VMEM)),
        input_output_aliases={i: 2 * na + i for i in range(2 * na)},
        compiler_params=pltpu.CompilerParams(has_side_effects=_EFFECT, collective_id=collective_id),
    )(*[pltpu.with_memory_space_constraint(t, pltpu.HBM) for t in list(blocks) + list(lands)])
    sems = [(out[2 * a], out[2 * a + 1]) for a in range(na)]
    return sems, out[2 * na:3 * na], out[3 * na:4 * na], out[-1]


def _gather_wait(sems, block, land, relations, after, name):
    def body(src, land_ref, send_sem, recv_sem, after_ref, src_out, land_out):
        x, y, c = _mesh_pos()
        for k in relations:
            peer = _related(k, x, y, c)
            cp = pltpu.make_async_remote_copy(
                src_ref=src, dst_ref=_device_rows(land_ref, block.shape[0], *peer),
                send_sem=send_sem.at[k - 1], recv_sem=recv_sem.at[k - 1], device_id=peer, device_id_type=MESH)
            cp.wait_send()
            cp.wait_recv()

    out = pl.pallas_call(
        body, name=name,
        out_shape=(pltpu.HBM(block.shape, block.dtype), pltpu.HBM(land.shape, land.dtype)),
        in_specs=[_HBM, _HBM, _SEM, _SEM, pl.BlockSpec(memory_space=pl.ANY)],
        out_specs=[_HBM, _HBM],
        input_output_aliases={0: 0, 1: 1},
        compiler_params=pltpu.CompilerParams(has_side_effects=_EFFECT),
    )(block, land, sems[0], sems[1], after)
    return out[1]


(COLLECTIVE_GATHER_W0, COLLECTIVE_GATHER_W1, COLLECTIVE_FORWARD_W_IN1, COLLECTIVE_EXCHANGE_1, COLLECTIVE_EXCHANGE_0A,
 COLLECTIVE_EXCHANGE_0B, COLLECTIVE_GATHER_SMALL) = range(1, 8)
COLLECTIVE_PAIR_SUM = (8, 9)


def _handshake(peers):
    barrier = pltpu.get_barrier_semaphore()
    for peer in peers:
        pl.semaphore_signal(barrier, inc=1, device_id=peer, device_id_type=MESH)
    pl.semaphore_wait(barrier, len(peers))


def _forward_plan(land_ref, m):
    x, y, c = _mesh_pos()
    return [_device_rows(land_ref, m, qx, qy, c) for qx, qy in ((1 - x, y), (x, 1 - y), (1 - x, 1 - y))], (x, y, 1 - c)


def _forward_start(land, m, name):
    def body(land_ref, send_sem, recv_sem, land_out, token):
        _handshake([(lax.axis_index("x"), lax.axis_index("y"), 1 - lax.axis_index("c"))])
        rows, sibling = _forward_plan(land_ref, m)
        for j, r in enumerate(rows):
            pltpu.make_async_remote_copy(src_ref=r, dst_ref=r, send_sem=send_sem.at[j], recv_sem=recv_sem.at[j],
                                         device_id=sibling, device_id_type=MESH).start()
        token[...] = jnp.zeros_like(token)

    out = pl.pallas_call(
        body, name=name,
        out_shape=(pltpu.SemaphoreType.DMA((3,)), pltpu.SemaphoreType.DMA((3,)), pltpu.HBM(land.shape, land.dtype),
                   jax.ShapeDtypeStruct((8, 128), F32)),
        in_specs=[_HBM],
        out_specs=(_SEM, _SEM, _HBM, pl.BlockSpec(memory_space=pltpu.VMEM)),
        input_output_aliases={0: 2},
        compiler_params=pltpu.CompilerParams(has_side_effects=_EFFECT, collective_id=COLLECTIVE_FORWARD_W_IN1),
    )(pltpu.with_memory_space_constraint(land, pltpu.HBM))
    return (out[0], out[1]), out[2], out[3]


def _forward_wait(sems, land, m, after, name):
    def body(land_ref, send_sem, recv_sem, after_ref, land_out):
        x, y, c = _mesh_pos()
        mine, sibling = _forward_plan(land_ref, m)
        theirs = [_device_rows(land_ref, m, qx, qy, 1 - c) for qx, qy in ((1 - x, y), (x, 1 - y), (1 - x, 1 - y))]
        for j in range(3):
            cp = pltpu.make_async_remote_copy(src_ref=mine[j], dst_ref=theirs[j], send_sem=send_sem.at[j],
                                              recv_sem=recv_sem.at[j], device_id=sibling, device_id_type=MESH)
            cp.wait_send()
            cp.wait_recv()

    return pl.pallas_call(
        body, name=name,
        out_shape=pltpu.HBM(land.shape, land.dtype),
        in_specs=[_HBM, _SEM, _SEM, pl.BlockSpec(memory_space=pl.ANY)],
        out_specs=_HBM,
        input_output_aliases={0: 0},
        compiler_params=pltpu.CompilerParams(has_side_effects=_EFFECT),
    )(land, sems[0], sems[1], after)


_HBM = pl.BlockSpec(memory_space=pltpu.HBM)
_SEM = pl.BlockSpec(memory_space=pltpu.SEMAPHORE)
_EFFECT = pltpu.SideEffectType.DATAFLOW_SIDE_EFFECTING


def _exchange_plan(direct):
    x, y, c = _mesh_pos()
    if not direct:
        return [(j, j, (qx, qy, c)) for j, (qx, qy) in enumerate([(1 - x, y), (x, 1 - y), (1 - x, 1 - y)])]
    plan = []
    for k in range(1, N_DEV):
        px, py, pc = x ^ ((k >> 2) & 1), y ^ ((k >> 1) & 1), c ^ (k & 1)
        plan.append((4 * px + 2 * py + pc, k - 1, (px, py, pc)))
    return plan


def _exchange_copies(directs):
    copies, base = [], 0
    for a, direct in enumerate(directs):
        plan = _exchange_plan(direct)
        copies += [(a, block, slot, peer, base + slot) for block, slot, peer in plan]
        base += len(plan)
    return copies, base


ADAM_PARTS = 2


def _adam_job_io(jobs):
    operands, aliases, results, scratch = [], [], [], []
    for _, mine, lands, w, m, v, earlier in jobs:
        _, mm, nn = w.shape
        if earlier is not None:
            aliases += [(len(operands) + 5 + t, len(results) + t) for t in range(4)]
        operands += [mine, lands, w, m, v] + ([] if earlier is None else list(earlier))
        results += [jax.ShapeDtypeStruct(w.shape, F32)] * 4
        scratch += [pltpu.VMEM((mm, nn), mine.dtype), pltpu.VMEM((lands.shape[0], mm, nn), lands.dtype),
                    pltpu.VMEM((3, mm, nn), F32), pltpu.VMEM((4, mm, nn), F32),
                    pltpu.SemaphoreType.DMA((ADAM_PARTS, 5)), pltpu.SemaphoreType.DMA((ADAM_PARTS, 4))]
    return operands, aliases, results, scratch


def _adam_jobs_load(jobs, operands, scratch):
    loads = []
    for layer, mine, _, w, _, _, earlier in jobs:
        hm, ld, w_ref, m_ref, v_ref = operands[:5]
        operands = operands[5 + (0 if earlier is None else 4):]
        own, slots, wmv, _, sems, _ = scratch[:6]
        scratch = scratch[6:]
        if mine.ndim == 3:
            x, y, c = _mesh_pos()
            hm = hm.at[4 * x + 2 * y + c]
        step = w.shape[1] // ADAM_PARTS
        parts = []
        for p in range(ADAM_PARTS):
            rows = slice(step * p, step * (p + 1))
            cps = [pltpu.make_async_copy(hm.at[rows], own.at[rows], sems.at[p, 0]),
                   pltpu.make_async_copy(ld.at[:, rows], slots.at[:, rows], sems.at[p, 1])]
            cps += [pltpu.make_async_copy(t.at[layer, rows], wmv.at[j, rows], sems.at[p, 2 + j])
                    for j, t in enumerate((w_ref, m_ref, v_ref))]
            for cp in cps:
                cp.start()
            parts.append(cps)
        loads.append(parts)
    return loads


def _adam_jobs_finish(jobs, loads, results, scratch):
    stores = []
    for (layer, _, _, w, _, _, _), parts in zip(jobs, loads):
        outs, results = results[:4], results[4:]
        own, slots, wmv, res, _, sems = scratch[:6]
        scratch = scratch[6:]
        step = w.shape[1] // ADAM_PARTS
        for p, cps in enumerate(parts):
            for cp in cps:
                cp.wait()
            rows = slice(step * p, step * (p + 1))
            g = own[rows, :].astype(F32)
            for j in range(slots.shape[0]):
                g = g + slots[j, rows, :].astype(F32)
            d, nm, nv = _adamw_math(wmv[0, rows, :], g, wmv[1, rows, :], wmv[2, rows, :])
            for t, val in enumerate((g, d, nm, nv)):
                res[t, rows, :] = val
                stores.append(pltpu.make_async_copy(res.at[t, rows], outs[t].at[layer, rows], sems.at[p, t]))
                stores[-1].start()
    for cp in stores:
        cp.wait()


def _exchange_start(srcs, directs, collective_id, name, jobs=()):
    na = len(srcs)
    slots = [N_DEV - 1 if direct else 3 for direct in directs]
    job_operands, job_aliases, job_results, job_scratch = _adam_job_io(jobs)
    first_result = 2 + 2 * na + 1

    def body(*refs):
        src, land = refs[:na], refs[na:2 * na]
        operands, refs = refs[2 * na:2 * na + len(job_operands)], refs[2 * na + len(job_operands):]
        send_sem, recv_sem, token = refs[0], refs[1], refs[first_result - 1]
        results, scratch = refs[first_result:first_result + len(job_results)], refs[first_result + len(job_results):]
        loads = _adam_jobs_load(jobs, operands, scratch)
        _handshake([peer for _, _, peer in _exchange_plan(any(directs))])
        for a, block, slot, peer, sem in _exchange_copies(directs)[0]:
            pltpu.make_async_remote_copy(
                src_ref=src[a].at[block], dst_ref=land[a].at[slot], send_sem=send_sem.at[sem],
                recv_sem=recv_sem.at[sem], device_id=peer, device_id_type=MESH).start()
        _adam_jobs_finish(jobs, loads, results, scratch)
        token[...] = jnp.zeros_like(token)

    zones = [jax.ShapeDtypeStruct((n,) + t.shape[1:], t.dtype) for n, t in zip(slots, srcs)]
    bufs = [pltpu.HBM(t.shape, t.dtype) for t in list(srcs) + zones]
    hbm = pl.BlockSpec(memory_space=pl.ANY)
    aliases = {i: 2 + i for i in range(2 * na)}
    aliases.update({2 * na + i: first_result + o for i, o in job_aliases})
    out = pl.pallas_call(
        body, name=name,
        out_shape=(pltpu.SemaphoreType.DMA((sum(slots),)), pltpu.SemaphoreType.DMA((sum(slots),)), *bufs,
                   jax.ShapeDtypeStruct((8, 128), F32), *job_results),
        in_specs=[_HBM] * (2 * na) + [hbm] * len(job_operands),
        out_specs=(_SEM, _SEM, *([_HBM] * (2 * na)), pl.BlockSpec(memory_space=pltpu.VMEM),
                   *([hbm] * len(job_results))),
        scratch_shapes=job_scratch,
        input_output_aliases=aliases,
        compiler_params=pltpu.CompilerParams(has_side_effects=_EFFECT, collective_id=collective_id,
                                             vmem_limit_bytes=VMEM_LIMIT),
    )(*[pltpu.with_memory_space_constraint(t, pltpu.HBM) for t in srcs],
      *[pltpu.with_memory_space_constraint(lax.empty(t.shape, t.dtype), pltpu.HBM) for t in zones], *job_operands)
    results = out[first_result:]
    return out[0], out[1], out[2:2 + na], out[2 + na:2 + 2 * na], out[first_result - 1], results


def _exchange_wait(send_sem, recv_sem, srcs, lands, directs, after, name):
    na = len(srcs)

    def body(*refs):
        src, land = refs[:na], refs[na:2 * na]
        send_sem_ref, recv_sem_ref = refs[2 * na], refs[2 * na + 1]
        for a, block, slot, peer, sem in _exchange_copies(directs)[0]:
            cp = pltpu.make_async_remote_copy(
                src_ref=src[a].at[block], dst_ref=land[a].at[slot], send_sem=send_sem_ref.at[sem],
                recv_sem=recv_sem_ref.at[sem], device_id=peer, device_id_type=MESH)
            cp.wait_send()
            cp.wait_recv()

    bufs = [pltpu.HBM(t.shape, t.dtype) for t in list(srcs) + list(lands)]
    out = pl.pallas_call(
        body, name=name,
        out_shape=tuple(bufs),
        in_specs=[_HBM] * (2 * na) + [_SEM, _SEM, pl.BlockSpec(memory_space=pl.ANY)],
        out_specs=[_HBM] * (2 * na),
        input_output_aliases={i: i for i in range(2 * na)},
        compiler_params=pltpu.CompilerParams(has_side_effects=_EFFECT),
    )(*srcs, *lands, send_sem, recv_sem, after)
    return out[:na], out[na:]


def _own_then_slots(mine_ref, lands_ref, rows=slice(None)):
    if len(mine_ref.shape) == 3:
        x, y, c = _mesh_pos()
        total = mine_ref[4 * x + 2 * y + c, rows, :].astype(F32)
    else:
        total = mine_ref[rows, :].astype(F32)
    for j in range(lands_ref.shape[0]):
        total = total + lands_ref[j, rows, :].astype(F32)
    return total


SMALL_ROWS = 2 * PACK_SLICE + 2 * 8


def _small_gather_start(mine, lands, dgpre, name, jobs=()):
    job_operands, job_aliases, job_results, job_scratch = _adam_job_io(jobs)

    def body(*refs):
        hm, ld, dg = refs[:DEPTH], refs[DEPTH:2 * DEPTH], refs[2 * DEPTH:3 * DEPTH]
        operands, refs = refs[3 * DEPTH:3 * DEPTH + len(job_operands)], refs[3 * DEPTH + len(job_operands):]
        (send_sem, recv_sem, blk, land, token), refs = refs[:5], refs[5:]
        results, refs = refs[:len(job_results)], refs[len(job_results):]
        (own, slots, rows, built, local_sems), scratch = refs[:5], refs[5:]
        x, y, c = _mesh_pos()
        job_loads = _adam_jobs_load(jobs, operands, scratch)
        loads = []
        for l in range(DEPTH):
            loads += [pltpu.make_async_copy(hm[l].at[4 * x + 2 * y + c], own.at[l], local_sems.at[3 * l]),
                      pltpu.make_async_copy(ld[l], slots.at[l], local_sems.at[3 * l + 1]),
                      pltpu.make_async_copy(dg[l], rows.at[l], local_sems.at[3 * l + 2])]
        for cp in loads:
            cp.start()
        _handshake([_related(k, x, y, c) for k in ALL_PEERS])
        for cp in loads:
            cp.wait()
        for l in range(DEPTH):
            total = own[l]
            for j in range(N_DEV - 1):
                total = total + slots[l, j]
            built[PACK_SLICE * l:PACK_SLICE * (l + 1), :] = total
            built[2 * PACK_SLICE + 8 * l:2 * PACK_SLICE + 8 * (l + 1), :] = rows[l]
        stores = [pltpu.make_async_copy(built, blk, local_sems.at[3 * DEPTH]),
                  pltpu.make_async_copy(built, _device_rows(land, SMALL_ROWS, x, y, c), local_sems.at[3 * DEPTH + 1])]
        for cp in stores:
            cp.start()
        for cp in stores:
            cp.wait()
        for k in ALL_PEERS:
            pltpu.make_async_remote_copy(
                src_ref=blk, dst_ref=_device_rows(land, SMALL_ROWS, x, y, c), send_sem=send_sem.at[k - 1],
                recv_sem=recv_sem.at[k - 1], device_id=_related(k, x, y, c), device_id_type=MESH).start()
        _adam_jobs_finish(jobs, job_loads, results, scratch)
        token[...] = jnp.zeros_like(token)

    hbm = pl.BlockSpec(memory_space=pl.ANY)
    out = pl.pallas_call(
        body, name=name,
        in_specs=[hbm] * (3 * DEPTH + len(job_operands)),
        out_specs=(_SEM, _SEM, _HBM, _HBM, pl.BlockSpec(memory_space=pltpu.VMEM), *([hbm] * len(job_results))),
        out_shape=(pltpu.SemaphoreType.DMA((N_DEV - 1,)), pltpu.SemaphoreType.DMA((N_DEV - 1,)),
                   pltpu.HBM((SMALL_ROWS, 128), F32), pltpu.HBM((N_DEV * SMALL_ROWS, 128), F32),
                   jax.ShapeDtypeStruct((8, 128), F32), *job_results),
        scratch_shapes=[pltpu.VMEM((DEPTH, PACK_SLICE, 128), F32), pltpu.VMEM((DEPTH, N_DEV - 1, PACK_SLICE, 128), F32),
                        pltpu.VMEM((DEPTH, 8, 128), F32), pltpu.VMEM((SMALL_ROWS, 128), F32),
                        pltpu.SemaphoreType.DMA((3 * DEPTH + 2,))] + job_scratch,
        input_output_aliases={3 * DEPTH + i: 5 + o for i, o in job_aliases},
        compiler_params=pltpu.CompilerParams(has_side_effects=_EFFECT, collective_id=COLLECTIVE_GATHER_SMALL,
                                             vmem_limit_bytes=VMEM_LIMIT),
    )(*mine, *lands, *dgpre, *job_operands)
    return (out[0], out[1]), out[2], out[3], out[4], out[5:]


def _adamw_math(w, g, m, v):
    m = ADAM_B1 * m + (1.0 - ADAM_B1) * g
    v = ADAM_B2 * v + (1.0 - ADAM_B2) * (g * g)
    m_hat = m / (1.0 - ADAM_B1 ** ADAM_STEP)
    v_hat = v / (1.0 - ADAM_B2 ** ADAM_STEP)
    delta = -ADAM_LR * (m_hat / (jnp.sqrt(v_hat) + ADAM_EPS) + ADAM_WD * w)
    return delta, m, v


def _adamw_layer(layer, mine, lands, w, m, v, earlier, token, name, rows):
    _, mm, nn = w.shape

    def body(hm_ref, ld_ref, w_ref, m_ref, v_ref, _, *refs):
        g_ref, d_ref, nm_ref, nv_ref = refs[-4:]
        g = _own_then_slots(hm_ref, ld_ref)
        g_ref[...] = g
        d, nm, nv = _adamw_math(w_ref[...], g, m_ref[...], v_ref[...])
        d_ref[...] = d
        nm_ref[...] = nm
        nv_ref[...] = nv

    spec = pl.BlockSpec((None, rows, nn), lambda i: (layer, i, 0))
    carried = [] if earlier is None else list(earlier)
    return pl.pallas_call(
        body, name=name, grid=(mm // rows,),
        in_specs=([pl.BlockSpec((rows, nn), lambda i: (i, 0)) if mine.ndim == 2
                   else pl.BlockSpec((N_DEV, rows, nn), lambda i: (0, i, 0)),
                   pl.BlockSpec((lands.shape[0], rows, nn), lambda i: (0, i, 0)),
                   spec, spec, spec] + [pl.BlockSpec(memory_space=pl.ANY)] * (1 + len(carried))),
        out_specs=[spec] * 4,
        out_shape=[jax.ShapeDtypeStruct(w.shape, F32)] * 4,
        input_output_aliases={6 + t: t for t in range(len(carried))},
        compiler_params=_compiler_params(("arbitrary",)),
    )(mine, lands, w, m, v, token, *carried)


def _adamw_small(gathered, params):
    def body(all_ref, *refs):
        ins, outs, packs = refs[:15], refs[15:15 + 21], refs[15 + 21]
        loss_ref = outs[0]
        for dev in range(N_DEV):
            for l in range(DEPTH):
                packs[l, PACK_SLICE * dev:PACK_SLICE * (dev + 1), :] = (
                    all_ref[SMALL_ROWS * dev + PACK_SLICE * l:SMALL_ROWS * dev + PACK_SLICE * (l + 1), :])
        loss_ref[...] = packs[DEPTH - 1, ROW_LOSS:ROW_LOSS + 1, 0:1]

        def update(p, sel, g):
            w_ref, m_ref, v_ref = ins[p], ins[5 + p], ins[10 + p]
            d, nm, nv = _adamw_math(w_ref[sel], g, m_ref[sel], v_ref[sel])
            for t, val in enumerate((g, d, nm, nv)):
                outs[1 + 5 * t + p][sel] = val

        for l in range(DEPTH):
            gp = packs.at[l]
            row0 = 2 * PACK_SLICE + 8 * l
            dgpre = all_ref[row0:row0 + 8, :]
            for dev in range(1, N_DEV):
                dgpre = dgpre + all_ref[SMALL_ROWS * dev + row0:SMALL_ROWS * dev + row0 + 8, :]
            for grp in range(4):
                update(0, (l, grp), gp[ROW_PW + BLOCK * grp:ROW_PW + BLOCK * (grp + 1), :])
                update(1, (slice(l, l + 1), slice(128 * grp, 128 * (grp + 1))), gp[ROW_SC + grp:ROW_SC + grp + 1, :])
            update(2, (slice(l, l + 1), slice(None)), gp[ROW_SINK:ROW_SINK + 1, 0:N_HEADS])
            for r in range(D_MODEL // 128):
                sel = (slice(l, l + 1), slice(128 * r, 128 * (r + 1)))
                update(3, sel, dgpre[r:r + 1, :])
                update(4, sel, gp[ROW_NPOST + r:ROW_NPOST + r + 1, :])

    shapes = [jax.ShapeDtypeStruct(p.shape, F32) for p in params[:5]]
    return pl.pallas_call(
        body, name="adamw_small",
        out_shape=[jax.ShapeDtypeStruct((1, 1), F32)] + shapes * 4,
        scratch_shapes=[pltpu.VMEM((DEPTH, PACK_ROWS, 128), F32)],
        compiler_params=_compiler_params(),
    )(gathered, *params)


def kernel(x, w_in, pool_w, pool_scale, attn_sinks, w_out, norm_pre, norm_post, loss_target, m_w_in, m_pool_w, m_pool_scale, m_attn_sinks, m_w_out, m_norm_pre, m_norm_post, v_w_in, v_pool_w, v_pool_scale, v_attn_sinks, v_w_out, v_norm_pre, v_norm_post):
    x0 = x.reshape(SEQ, D_MODEL)
    target = loss_target.reshape(SEQ, D_MODEL)
    bias = jnp.asarray(_attn_bias())
    w_in_t, m_in_t, v_in_t = (jnp.swapaxes(t, 1, 2) for t in (w_in, m_w_in, v_w_in))

    (win0, wout0), later, lands = _allgather([(w_in_t, 0), (w_out, 0)], BF16, "gather_w0",
                                              later=[(w_in_t, 1), (w_out, 1)])
    sems, later, lands, token = _gather_start(later, lands, [SIBLING_AND_SAME_CORE, ALL_PEERS], COLLECTIVE_GATHER_W1,
                                              "gather_w1_start")
    win_full, wout_full = [win0, None], [wout0, None]

    saved = []
    xl = x0
    for layer in range(DEPTH):
        u, pg, q, k, v, ag, z, a = _fwd_front(layer, xl, norm_pre, win_full[layer], token, attn_sinks,
                                             pool_w, pool_scale, bias)
        if layer == 0:
            land = _gather_wait(sems[0], later[0], lands[0], SIBLING_AND_SAME_CORE, z, "gather_w_in1_wait")
            fsems, land, token = _forward_start(land, IN_SHARD, "forward_w_in1_start")
        else:
            wout_full[layer] = _gather_wait(sems[1], later[1], lands[1], ALL_PEERS, z, "gather_w_out1_wait")
        x_next, y = _fwd_out(layer, z, xl, norm_post, wout_full[layer], token)
        if layer == 0:
            win_full[1] = _forward_wait(fsems, land, IN_SHARD, x_next, "forward_w_in1_wait")
        saved.append((xl, u, pg, q, k, v, ag, z, a, y))
        xl = x_next

    params_small = [pool_w, pool_scale, attn_sinks, norm_pre, norm_post,
                    m_pool_w, m_pool_scale, m_attn_sinks, m_norm_pre, m_norm_post,
                    v_pool_w, v_pool_scale, v_attn_sinks, v_norm_pre, v_norm_post]

    def start(srcs, directs, paired, collective_id, tag, jobs=()):
        send_sem, recv_sem, srcs, lands, started, results = _exchange_start(srcs, directs, collective_id,
                                                                            f"exchange_start{tag}", jobs)
        return (send_sem, recv_sem, srcs, lands, paired, directs), started, results

    def finish(handle, after, tag):
        send_sem, recv_sem, srcs, lands, paired, directs = handle
        srcs, lands = _exchange_wait(send_sem, recv_sem, srcs, lands, directs, after, f"exchange_wait{tag}")
        return [s if p is None else p for s, p in zip(srcs, paired)], lands

    def back(layer, top, first, second):
        xin, u, pg, q, k, v, ag, z, a, y = saved[layer]
        return _bwd_back(layer, top, first, second, y, z, norm_post, wout_full[layer], attn_sinks, u, pg, q, k, v,
                         ag, a, pool_w, pool_scale, bias)

    dgpre = [None] * DEPTH
    dx, dproj, gw_out, pack = back(1, True, xl, target)
    dx, dgpre[1], chip_sums, own_sum = _bwd_in(1, "both_pair", token, dproj, saved[1][0], norm_pre, dx, win_full[1])
    top, token, _ = start([chip_sums, gw_out.reshape(N_DEV, OUT_SHARD, D_MODEL), pack.reshape(N_DEV, PACK_SLICE, 128)],
                          [False, True, True], [own_sum, None, None], COLLECTIVE_EXCHANGE_1, "1")

    dproj, gw_out, pack = back(0, False, dx, token)
    early, token, _ = start([gw_out.reshape(N_DEV, OUT_SHARD, D_MODEL), pack.reshape(N_DEV, PACK_SLICE, 128)],
                            [True, True], [None, None], COLLECTIVE_EXCHANGE_0A, "0a")
    chip_sums, own_sum = _bwd_in(0, "dw_pair", token, dproj, saved[0][0], norm_pre)
    own1, lands1 = finish(top, chip_sums, "1")
    late, token, updated = start([chip_sums], [False], [own_sum], COLLECTIVE_EXCHANGE_0B, "0b",
                                 jobs=[(1, own1[0], lands1[0], w_in_t, m_in_t, v_in_t, None),
                                       (1, own1[1], lands1[1], w_out, m_w_out, v_w_out, None)])
    big_in, big_out = updated[:4], updated[4:]
    dx, dgpre[0] = _bwd_in(0, "dx", token, dproj, saved[0][0], norm_pre, dx, win_full[0])

    own0a, lands0a = finish(early, dx, "0a")
    sems, block, land, token, big_out = _small_gather_start(
        [own0a[1], own1[2]], [lands0a[1], lands1[2]], dgpre, "gather_small_start",
        jobs=[(0, own0a[0], lands0a[0], w_out, m_w_out, v_w_out, big_out)])
    own0b, lands0b = finish(late, big_out[0], "0b")
    big_in = _adamw_layer(0, own0b[0], lands0b[0], w_in_t, m_in_t, v_in_t, big_in, token, "adamw_in0", ADAM_ROWS_IN)
    gathered = _gather_wait(sems, block, land, ALL_PEERS, big_in[0], "gather_small_wait")
    small_out = _adamw_small(gathered, params_small)
    loss = small_out[0].reshape(())

    outs = [loss, dx.reshape(1, SEQ, D_MODEL)]
    for t in range(4):
        pw_, sc_, sk_, npre_, npost_ = small_out[1 + 5 * t:6 + 5 * t]
        outs += [jnp.swapaxes(big_in[t], 1, 2), pw_, sc_, sk_, big_out[t], npre_, npost_]
    return tuple(outs)
```

```python
import numpy as np
import jax
import jax.numpy as jnp
from jax import lax
from jax.experimental import pallas as pl
from jax.experimental.pallas import tpu as pltpu

F32 = jnp.float32
BF16 = jnp.bfloat16

N_DEV = 8
SEQ = 2048
D_MODEL = 1024
D_POOL = 512
D_ATTN = 512
D_KV = 128
D_IN = 2304
N_HEADS = 8
GQA = 4
HEAD_DIM = 64
BLOCK = 128
POOL_WINDOWS = (2, 4, 8, 16)
DEPTH = 2
EPS = 1e-6
NEG_INF = -1e30
SCALE = HEAD_DIM ** -0.5
IN_SHARD = D_IN // N_DEV
OUT_SHARD = D_MODEL // N_DEV

COL_U, COL_PG, COL_Q, COL_K, COL_V, COL_AG = 0, 512, 1024, 1536, 1664, 1792

ADAM_LR = 0.001
ADAM_B1 = 0.9
ADAM_B2 = 0.999
ADAM_EPS = 1e-08
ADAM_WD = 0.01
ADAM_STEP = 10

TOKEN_TILE = 512
FWD_OUT_TILE = 1024
ADAM_ROWS_IN, ADAM_ROWS_OUT = 144, 64
VMEM_LIMIT = 56 * 1024 * 1024
MESH = pl.DeviceIdType.MESH

ROW_PW, ROW_SC, ROW_SINK, ROW_NPOST, ROW_LOSS = 0, 512, 520, 536, 544
PACK_ROWS = 576
PACK_SLICE = PACK_ROWS // N_DEV


def _nn(a, b):
    return jnp.dot(a, b, preferred_element_type=F32)


def _nt(a, b):
    return lax.dot_general(a, b, (((1,), (1,)), ((), ())), preferred_element_type=F32)


def _tn(a, b):
    return lax.dot_general(a, b, (((0,), (0,)), ((), ())), preferred_element_type=F32)


def _silu_parts(g):
    s = jax.nn.sigmoid(g)
    return g * s, s * (1.0 + g * (1.0 - s))


def _resident(shape):
    return pl.BlockSpec(shape, lambda *_: (0,) * len(shape), pipeline_mode=pl.Buffered(1))


def _compiler_params(sem=None):
    if sem is None:
        return pltpu.CompilerParams(vmem_limit_bytes=VMEM_LIMIT)
    return pltpu.CompilerParams(dimension_semantics=sem, vmem_limit_bytes=VMEM_LIMIT)


def _attn_bias():
    t = np.arange(BLOCK)[None, :]
    j = np.arange(BLOCK)[:, None]
    current = j <= t
    dist = np.where(current, t - j, t + BLOCK - j).astype(np.float32)
    out = np.zeros((2, 2, BLOCK, GQA * BLOCK), np.float32)
    for variant in range(2):
        valid = current | (variant == 1)
        for kv in range(2):
            for g in range(GQA):
                slope = np.float32(2.0 ** (-(kv * GQA + g + 1)))
                out[variant, kv, :, g * BLOCK:(g + 1) * BLOCK] = np.where(valid, -slope * dist, np.float32(NEG_INF))
    return out


def _replicate_head(kx, kv):
    rolled = pltpu.roll(kx, 64, 1)
    lane = lax.broadcasted_iota(jnp.int32, kx.shape, 1)
    dup = jnp.where(lane < 64, kx, rolled) if kv == 0 else jnp.where(lane < 64, rolled, kx)
    return jnp.concatenate([dup, dup], axis=1).astype(BF16)


def _stack_heads(qv):
    lane = lax.broadcasted_iota(jnp.int32, qv.shape, 1)
    zero = jnp.zeros_like(qv)
    return jnp.concatenate([jnp.where((lane >= 64 * g) & (lane < 64 * g + 64), qv, zero) for g in range(GQA)], axis=0)


def _unstack_heads(xs):
    lane = lax.broadcasted_iota(jnp.int32, (BLOCK, 256), 1)
    return jnp.where(lane < 64, xs[0:128], jnp.where(lane < 128, xs[128:256], jnp.where(lane < 192, xs[256:384], xs[384:512])))


def _fold_heads(r):
    h = r[:, 0:128] + r[:, 128:256]
    return h + pltpu.roll(h, 64, 1)


def _sink_row(sink_ref, layer, kv):
    lane = lax.broadcasted_iota(jnp.int32, (1, GQA * BLOCK), 1)
    s4 = [sink_ref[layer, kv * GQA + g] for g in range(GQA)]
    return jnp.where(lane < 128, s4[0], jnp.where(lane < 256, s4[1], jnp.where(lane < 384, s4[2], s4[3])))


def _band_is_current():
    j = lax.broadcasted_iota(jnp.int32, (BLOCK, GQA * BLOCK), 0)
    t = lax.broadcasted_iota(jnp.int32, (BLOCK, GQA * BLOCK), 1) & (BLOCK - 1)
    return j <= t


def _pack_band(full, current):
    return jnp.where(current, full[BLOCK:], full[:BLOCK])


def _unpack_band(packed, current):
    zero = jnp.zeros_like(packed)
    return jnp.concatenate([jnp.where(current, zero, packed), jnp.where(current, packed, zero)], axis=0)


def _probs_keys_major(k_rep, q_st, bias, sink, current):
    st = _pack_band(_nt(k_rep, q_st), current) * SCALE + bias
    m = jnp.maximum(jnp.max(st, axis=0, keepdims=True), sink)
    p = jnp.exp(st - m)
    esink = jnp.exp(sink - m)
    rl = 1.0 / (jnp.sum(p, axis=0, keepdims=True) + esink)
    return p * rl, esink * rl


WINDOW_HALO = 16


def _window_sum(ext, w, forward):
    s = ext
    sh = 1
    while sh < w:
        s = s + pltpu.roll(s, (ext.shape[0] - sh) if forward else sh, 0)
        sh *= 2
    return s


def _inv_count(n, w):
    t = n * BLOCK + lax.broadcasted_iota(jnp.int32, (BLOCK, 1), 0) + 1
    return 1.0 / jnp.minimum(t.astype(F32), float(w))


def _kv_ext(ref, n):
    r0 = pl.multiple_of(jnp.maximum(n - 1, 0) * BLOCK, BLOCK)
    r1 = pl.multiple_of(n * BLOCK, BLOCK)
    return jnp.concatenate([ref[pl.ds(r0, BLOCK), :], ref[pl.ds(r1, BLOCK), :]], axis=0)


def _rows_of(vec_ref, pack_ref, row0):
    for r in range(D_MODEL // 128):
        pack_ref[row0 + r:row0 + r + 1, :] = vec_ref[:, 128 * r:128 * (r + 1)]


FRONT_TILE = 4 * BLOCK


def _fwd_front(layer, x, norm_pre, w_in_t, token, sinks, pool_w, pool_scale, bias, out=None):
    tm = FRONT_TILE

    def body(sink_ref, x_ref, g_ref, w_ref, _, pw_ref, sc_ref, bias_ref, *refs):
        if out is not None:
            gpost_ref, wo_ref, *refs = refs
            xn_ref, y_ref = refs[8:10]
        u_ref, pg_ref, q_ref, k_ref, v_ref, ag_ref, z_ref, a_ref = refs[:8]
        uprev, kprev, vprev = refs[-3:]
        i = pl.program_id(0)

        @pl.when(i == 0)
        def _():
            uprev[...] = jnp.zeros_like(uprev)
            kprev[...] = jnp.zeros_like(kprev)
            vprev[...] = jnp.zeros_like(vprev)

        xv = x_ref[...]
        r = lax.rsqrt(jnp.mean(xv * xv, axis=-1, keepdims=True) + EPS)
        h = (xv * r * g_ref[layer:layer + 1, :]).astype(BF16)
        u_ref[...] = _nt(h, w_ref[COL_U:COL_PG, :])
        pg_ref[...] = _nt(h, w_ref[COL_PG:COL_Q, :])
        for sb in range(tm // BLOCK):
            n = (tm // BLOCK) * i + sb
            rows = slice(BLOCK * sb, BLOCK * (sb + 1))
            before = slice(BLOCK * (sb - 1), BLOCK * sb)
            uv = u_ref[rows, :]
            halo = (uprev[BLOCK - WINDOW_HALO:, :] if sb == 0
                    else u_ref[BLOCK * sb - WINDOW_HALO:BLOCK * sb, :])
            ext = jnp.concatenate([halo, uv], axis=0)
            for g, w in enumerate(POOL_WINDOWS):
                cs = slice(BLOCK * g, BLOCK * (g + 1))
                win = _window_sum(ext[:, cs], w, forward=False)[WINDOW_HALO:]
                pooled = win * _inv_count(n, w) - uv[:, cs]
                mixed = _nn(pooled.astype(BF16), pw_ref[g].astype(BF16))
                gate, _ = _silu_parts(pg_ref[rows, cs])
                z_ref[rows, cs] = (mixed * sc_ref[layer:layer + 1, cs] * gate).astype(BF16)

        q_ref[...] = _nt(h, w_ref[COL_Q:COL_K, :]).astype(BF16)
        k_ref[...] = _nt(h, w_ref[COL_K:COL_V, :])
        v_ref[...] = _nt(h, w_ref[COL_V:COL_AG, :])
        ag_ref[...] = _nt(h, w_ref[COL_AG:D_IN, :])

        current = _band_is_current()
        for sb in range(tm // BLOCK):
            n = (tm // BLOCK) * i + sb
            rows = slice(BLOCK * sb, BLOCK * (sb + 1))
            before = slice(BLOCK * (sb - 1), BLOCK * sb)
            kx = jnp.concatenate([kprev[...] if sb == 0 else k_ref[before, :], k_ref[rows, :]], axis=0)
            vx = jnp.concatenate([vprev[...] if sb == 0 else v_ref[before, :], v_ref[rows, :]], axis=0)
            variant = jnp.minimum(n, 1) if sb == 0 else 1
            for kv in range(2):
                cs = slice(256 * kv, 256 * (kv + 1))
                p, _ = _probs_keys_major(_replicate_head(kx, kv), _stack_heads(q_ref[rows, cs]),
                                         bias_ref[variant, kv], _sink_row(sink_ref, layer, kv), current)
                o = _unstack_heads(_tn(_unpack_band(p.astype(BF16), current), _replicate_head(vx, kv)))
                a_ref[rows, cs] = o
                gate, _ = _silu_parts(ag_ref[rows, cs])
                z_ref[rows, D_POOL + 256 * kv:D_POOL + 256 * (kv + 1)] = (o * gate).astype(BF16)

        tail = slice(tm - BLOCK, tm)
        uprev[...] = u_ref[tail, :]
        kprev[...] = k_ref[tail, :]
        vprev[...] = v_ref[tail, :]

        if out is not None:
            y = _nn(z_ref[...], wo_ref[...])
            y_ref[...] = y
            r = lax.rsqrt(jnp.mean(y * y, axis=-1, keepdims=True) + EPS)
            xn_ref[...] = x_ref[...] + y * r * gpost_ref[layer:layer + 1, :]

    row = lambda c: pl.BlockSpec((tm, c), lambda i: (i, 0))
    const = lambda shape: pl.BlockSpec(shape, lambda i: (0,) * len(shape))
    act = jax.ShapeDtypeStruct((SEQ, D_MODEL), F32)
    fused = out is not None
    return pl.pallas_call(
        body, name=f"fwd_front{layer}", grid=(SEQ // tm,),
        in_specs=[pl.BlockSpec(memory_space=pltpu.SMEM), row(D_MODEL), const((DEPTH, D_MODEL)),
                  _resident((D_IN, D_MODEL)), const((8, 128)),
                  pl.BlockSpec((None, 4, BLOCK, BLOCK), lambda i: (layer, 0, 0, 0)), const((DEPTH, D_POOL)),
                  _resident((2, 2, BLOCK, GQA * BLOCK))]
                 + ([const((DEPTH, D_MODEL)), _resident((D_MODEL, D_MODEL))] if fused else []),
        out_specs=[row(D_POOL), row(D_POOL), row(D_ATTN), row(D_KV), row(D_KV), row(D_ATTN), row(D_MODEL),
                   row(D_ATTN)] + ([row(D_MODEL)] * 2 if fused else []),
        out_shape=[jax.ShapeDtypeStruct((SEQ, D_POOL), F32), jax.ShapeDtypeStruct((SEQ, D_POOL), F32),
                   jax.ShapeDtypeStruct((SEQ, D_ATTN), BF16), jax.ShapeDtypeStruct((SEQ, D_KV), F32),
                   jax.ShapeDtypeStruct((SEQ, D_KV), F32), jax.ShapeDtypeStruct((SEQ, D_ATTN), F32),
                   jax.ShapeDtypeStruct((SEQ, D_MODEL), BF16), jax.ShapeDtypeStruct((SEQ, D_ATTN), F32)]
                  + ([act] * 2 if fused else []),
        scratch_shapes=[pltpu.VMEM((BLOCK, D_POOL), F32), pltpu.VMEM((BLOCK, D_KV), F32),
                        pltpu.VMEM((BLOCK, D_KV), F32)],
        compiler_params=_compiler_params(("arbitrary",)),
    )(sinks, x, norm_pre, w_in_t, token, pool_w, pool_scale, bias, *(out if fused else ()))


def _fwd_out(layer, z, x, norm_post, w_out, token):
    tm = FWD_OUT_TILE

    def body(z_ref, x_ref, g_ref, w_ref, _, xn_ref, y_ref):
        y = _nn(z_ref[...], w_ref[...])
        y_ref[...] = y
        r = lax.rsqrt(jnp.mean(y * y, axis=-1, keepdims=True) + EPS)
        xn_ref[...] = x_ref[...] + y * r * g_ref[layer:layer + 1, :]

    row = lambda c: pl.BlockSpec((tm, c), lambda i: (i, 0))
    return pl.pallas_call(
        body, name=f"fwd_out{layer}", grid=(SEQ // tm,),
        in_specs=[row(D_MODEL), row(D_MODEL), pl.BlockSpec((DEPTH, D_MODEL), lambda i: (0, 0)),
                  _resident((D_MODEL, D_MODEL)), pl.BlockSpec((8, 128), lambda i: (0, 0))],
        out_specs=[row(D_MODEL), row(D_MODEL)],
        out_shape=[jax.ShapeDtypeStruct((SEQ, D_MODEL), F32), jax.ShapeDtypeStruct((SEQ, D_MODEL), F32)],
        compiler_params=_compiler_params(("arbitrary",)),
    )(z, x, norm_post, w_out, token)


BACK_TILE = 2 * BLOCK


def _bwd_back(layer, top, dxo_or_xf, target_or_token, y, z, norm_post, w_out, sinks, u, pg, q, k, v, ag, a,
              pool_w, pool_scale, bias):
    tm = BACK_TILE
    steps = SEQ // tm
    last = steps - 1
    per = tm // BLOCK

    def body(*refs):
        refs = list(refs)
        sink_ref, first, second = refs[:3]
        (y_ref, z_ref, g_ref, w_ref, u_ref, up_ref, pg_ref, q_ref, k_ref, v_ref, ag_ref, a_ref, pw_ref, sc_ref,
         bias_ref) = refs[3:18]
        del refs[:18]
        dxo_ref = refs.pop(0) if top else None
        dp_ref, dw_ref, pack_ref, acc, dg, lacc, dzs, ck, cv, ce = refs
        i = pl.program_id(0)
        blk = last - i

        @pl.when(i == 0)
        def _():
            acc[...] = jnp.zeros_like(acc)
            dg[...] = jnp.zeros_like(dg)
            lacc[...] = jnp.zeros_like(lacc)
            pack_ref[...] = jnp.zeros_like(pack_ref)
            ck[...] = jnp.zeros_like(ck)
            cv[...] = jnp.zeros_like(cv)
            ce[...] = jnp.zeros_like(ce)

        if top:
            d = first[...] - second[...]
            dxo_v = d * (1.0 / D_MODEL)
            dxo_ref[...] = dxo_v
            part = jnp.sum(d * d, axis=-1, keepdims=True) * (1.0 / D_MODEL)
            lacc[...] += 0.5 * jnp.sum(part, axis=0, keepdims=True)
        else:
            dxo_v = first[...]
        yv = y_ref[...]
        r = lax.rsqrt(jnp.mean(yv * yv, axis=-1, keepdims=True) + EPS)
        yn = yv * r
        dg[...] += jnp.sum(dxo_v * yn, axis=0, keepdims=True)
        dyn = dxo_v * g_ref[layer:layer + 1, :]
        dy = (r * (dyn - yn * jnp.mean(dyn * yn, axis=-1, keepdims=True))).astype(BF16)
        dzs[...] = _nt(dy, w_ref[...])
        acc[...] += _tn(z_ref[...], dy)

        lane = lax.broadcasted_iota(jnp.int32, (1, 128), 1)
        lane2 = lax.broadcasted_iota(jnp.int32, (256, 128), 1)
        current = _band_is_current()
        for sb in reversed(range(per)):
            n = per * blk + sb
            rows = slice(BLOCK * sb, BLOCK * (sb + 1))

            uv = u_ref[rows, :]
            if sb == 0:
                halo = up_ref[BLOCK - WINDOW_HALO:, :] * (n > 0).astype(F32)
            else:
                halo = u_ref[BLOCK * sb - WINDOW_HALO:BLOCK * sb, :]
            ext = jnp.concatenate([halo, uv], axis=0)
            for g, w in enumerate(POOL_WINDOWS):
                cs = slice(BLOCK * g, BLOCK * (g + 1))
                inv = _inv_count(n, w)
                win = _window_sum(ext[:, cs], w, forward=False)[WINDOW_HALO:]
                pooled = win * inv - uv[:, cs]
                pw_g = pw_ref[g].astype(BF16)
                mixed = _nn(pooled.astype(BF16), pw_g)
                gate, dgate = _silu_parts(pg_ref[rows, cs])
                dzp = dzs[rows, cs]
                sc = sc_ref[layer:layer + 1, cs]
                dpm = dzp * gate
                dp_ref[rows, COL_PG + BLOCK * g:COL_PG + BLOCK * (g + 1)] = (dzp * (mixed * sc) * dgate).astype(BF16)
                pack_ref[ROW_SC + g:ROW_SC + g + 1, :] += jnp.sum(dpm * mixed, axis=0, keepdims=True)
                dmixed = (dpm * sc).astype(BF16)
                pack_ref[ROW_PW + BLOCK * g:ROW_PW + BLOCK * (g + 1), :] += _tn(pooled.astype(BF16), dmixed)
                dpooled = _nt(dmixed, pw_g)
                e = dpooled * inv
                lead = _window_sum(jnp.concatenate([e, ce[:WINDOW_HALO, cs]], axis=0), w, forward=True)[:BLOCK]
                dp_ref[rows, COL_U + BLOCK * g:COL_U + BLOCK * (g + 1)] = (lead - dpooled).astype(BF16)
                ce[:, cs] = e

            kx = _kv_ext(k_ref, n)
            vx = _kv_ext(v_ref, n)
            variant = jnp.minimum(n, 1) if sb == 0 else 1
            dsink_row = jnp.zeros((1, 128), F32)
            tks, tvs = [], []
            for kv in range(2):
                cs = slice(256 * kv, 256 * (kv + 1))
                k_rep = _replicate_head(kx, kv)
                v_rep = _replicate_head(vx, kv)
                q_st = _stack_heads(q_ref[rows, cs])
                gate, dgate = _silu_parts(ag_ref[rows, cs])
                dza = dzs[rows, D_POOL + 256 * kv:D_POOL + 256 * (kv + 1)]
                dp_ref[rows, COL_AG + 256 * kv:COL_AG + 256 * (kv + 1)] = (dza * a_ref[rows, cs] * dgate).astype(BF16)
                da_st = _stack_heads((dza * gate).astype(BF16))
                p, psink = _probs_keys_major(k_rep, q_st, bias_ref[variant, kv], _sink_row(sink_ref, layer, kv),
                                             current)
                dpt = _pack_band(_nt(v_rep, da_st), current)
                delta = jnp.sum(p * dpt, axis=0, keepdims=True)
                dst = _unpack_band((p * (dpt - delta) * SCALE).astype(BF16), current)
                sink_terms = psink * delta
                for g in range(GQA):
                    dsink = -jnp.sum(sink_terms[:, BLOCK * g:BLOCK * (g + 1)], axis=1, keepdims=True)
                    dsink_row = dsink_row + jnp.where(lane == kv * GQA + g, dsink, 0.0)
                dp_ref[rows, COL_Q + 256 * kv:COL_Q + 256 * (kv + 1)] = _unstack_heads(_tn(dst, k_rep)).astype(BF16)
                tks.append(_fold_heads(_nn(dst, q_st)))
                tvs.append(_fold_heads(_nn(_unpack_band(p.astype(BF16), current), da_st)))
            pack_ref[ROW_SINK:ROW_SINK + 1, :] += dsink_row
            dkx = jnp.where(lane2 < 64, tks[0], tks[1])
            dvx = jnp.where(lane2 < 64, tvs[0], tvs[1])
            dp_ref[rows, COL_K:COL_V] = (ck[...] + dkx[BLOCK:]).astype(BF16)
            dp_ref[rows, COL_V:COL_AG] = (cv[...] + dvx[BLOCK:]).astype(BF16)
            ck[...] = dkx[:BLOCK]
            cv[...] = dvx[:BLOCK]

        @pl.when(i == steps - 1)
        def _():
            dw_ref[...] = acc[...].astype(BF16)
            _rows_of(dg, pack_ref, ROW_NPOST)
            pack_ref[ROW_LOSS:ROW_LOSS + 1, :] = jnp.where(lane == 0, lacc[...], 0.0)

    row = lambda c: pl.BlockSpec((tm, c), lambda i: (last - i, 0))
    const = lambda shape: pl.BlockSpec(shape, lambda i: (0,) * len(shape))
    act = jax.ShapeDtypeStruct((SEQ, D_MODEL), F32)
    return pl.pallas_call(
        body, name=f"bwd_back{layer}", grid=(steps,),
        in_specs=[pl.BlockSpec(memory_space=pltpu.SMEM), row(D_MODEL), row(D_MODEL) if top else const((8, 128)),
                  row(D_MODEL), row(D_MODEL), const((DEPTH, D_MODEL)), _resident((D_MODEL, D_MODEL)),
                  row(D_POOL), pl.BlockSpec((BLOCK, D_POOL), lambda i: (jnp.maximum(per * (last - i) - 1, 0), 0)),
                  row(D_POOL), row(D_ATTN), _resident((SEQ, D_KV)), _resident((SEQ, D_KV)), row(D_ATTN), row(D_ATTN),
                  pl.BlockSpec((None, 4, BLOCK, BLOCK), lambda i: (layer, 0, 0, 0)), const((DEPTH, D_POOL)),
                  _resident((2, 2, BLOCK, GQA * BLOCK))],
        out_specs=([row(D_MODEL)] * (1 if top else 0)
                   + [row(D_IN), const((D_MODEL, D_MODEL)), const((PACK_ROWS, 128))]),
        out_shape=([act] * (1 if top else 0)
                   + [jax.ShapeDtypeStruct((SEQ, D_IN), BF16), jax.ShapeDtypeStruct((D_MODEL, D_MODEL), BF16),
                      jax.ShapeDtypeStruct((PACK_ROWS, 128), F32)]),
        scratch_shapes=[pltpu.VMEM((D_MODEL, D_MODEL), F32), pltpu.VMEM((1, D_MODEL), F32), pltpu.VMEM((1, 1), F32),
                        pltpu.VMEM((tm, D_MODEL), F32), pltpu.VMEM((BLOCK, D_KV), F32), pltpu.VMEM((BLOCK, D_KV), F32),
                        pltpu.VMEM((BLOCK, D_POOL), F32)],
        compiler_params=_compiler_params(("arbitrary",)),
    )(sinks, dxo_or_xf, target_or_token, y, z, norm_post, w_out, u, u, pg, q, k, v, ag, a, pool_w, pool_scale, bias)


def _bwd_in(layer, part, token, dproj, x, norm_pre, dxo=None, w_in_t=None):
    pair = part in ("dw_pair", "both_pair")
    want_dw, want_dx = part != "dx", part in ("both", "dx", "both_pair")
    tm = TOKEN_TILE
    steps = SEQ // tm
    cw = 256

    def body(*refs):
        refs = list(refs)
        dp_ref, x_ref, g_ref = refs[1:4]
        del refs[:4]
        if want_dx:
            dxo_ref, w_ref, dx_ref, dgo_ref = refs[:4]
            del refs[:4]
            dg = refs.pop()
        if pair:
            hs_ref, hm_ref, acc, mine_buf, theirs_buf, send_sem, recv_sem = refs
        elif want_dw:
            dw_ref, acc = refs
        i = pl.program_id(0)

        @pl.when(i == 0)
        def _():
            if pair:
                _handshake([(lax.axis_index("x"), lax.axis_index("y"), 1 - lax.axis_index("c"))])
            if want_dw:
                acc[...] = jnp.zeros_like(acc)
            if want_dx:
                dg[...] = jnp.zeros_like(dg)

        xv = x_ref[...]
        gv = g_ref[layer:layer + 1, :]
        r = lax.rsqrt(jnp.mean(xv * xv, axis=-1, keepdims=True) + EPS)
        xn = xv * r
        if want_dw:
            hb = (xn * gv).astype(BF16)
            for c in range(0, D_IN, cw):
                acc[c:c + cw, :] += _tn(dp_ref[:, c:c + cw], hb)
        def rows_for(q, core):
            return pl.ds(pl.multiple_of((2 * q + core) * IN_SHARD, 8), IN_SHARD)

        def swap(q):
            x, y, c = _mesh_pos()
            return pltpu.make_async_remote_copy(
                src_ref=mine_buf.at[q], dst_ref=theirs_buf.at[q], send_sem=send_sem.at[q], recv_sem=recv_sem.at[q],
                device_id=(x, y, 1 - c), device_id_type=MESH)

        if pair:
            @pl.when(i == steps - 1)
            def _():
                for q in range(4):
                    mine_buf[q] = acc[rows_for(q, 1 - lax.axis_index("c")), :].astype(BF16)
                    swap(q).start()

        if want_dx:
            dh = _nn(dp_ref[...], w_ref[...])
            dg[...] += jnp.sum(dh * xn, axis=0, keepdims=True)
            dhn = dh * gv
            dx_ref[...] = dxo_ref[...] + r * (dhn - xn * jnp.mean(dhn * xn, axis=-1, keepdims=True))

        @pl.when(i == steps - 1)
        def _():
            if pair:
                x, y, c = _mesh_pos()
                for q in range(4):
                    swap(q).wait()
                for j, q in enumerate([2 * (1 - x) + y, 2 * x + (1 - y), 2 * (1 - x) + (1 - y)]):
                    hs_ref[j] = (acc[rows_for(q, c), :] + theirs_buf[q].astype(F32)).astype(BF16)
                hm_ref[...] = acc[rows_for(2 * x + y, c), :] + theirs_buf[2 * x + y].astype(F32)
            elif want_dw:
                dw_ref[...] = acc[...].astype(BF16)
            if want_dx:
                _rows_of(dg, dgo_ref, 0)

    row = lambda c: pl.BlockSpec((tm, c), lambda i: (i, 0))
    const = lambda shape: pl.BlockSpec(shape, lambda i: (0,) * len(shape))
    in_specs = [const((8, 128)), row(D_IN), row(D_MODEL), const((DEPTH, D_MODEL))]
    operands = [token, dproj, x, norm_pre]
    out_specs, out_shape, scratch = [], [], []
    if want_dx:
        in_specs += [row(D_MODEL), _resident((D_IN, D_MODEL))]
        operands += [dxo, w_in_t]
        out_specs += [row(D_MODEL), const((8, 128))]
        out_shape += [jax.ShapeDtypeStruct((SEQ, D_MODEL), F32), jax.ShapeDtypeStruct((8, 128), F32)]
    if pair:
        out_specs += [const((3, IN_SHARD, D_MODEL)), const((IN_SHARD, D_MODEL))]
        out_shape += [jax.ShapeDtypeStruct((3, IN_SHARD, D_MODEL), BF16), jax.ShapeDtypeStruct((IN_SHARD, D_MODEL), F32)]
        scratch += [pltpu.VMEM((D_IN, D_MODEL), F32), pltpu.VMEM((4, IN_SHARD, D_MODEL), BF16),
                    pltpu.VMEM((4, IN_SHARD, D_MODEL), BF16), pltpu.SemaphoreType.DMA((4,)), pltpu.SemaphoreType.DMA((4,))]
    elif want_dw:
        out_specs.append(const((D_IN, D_MODEL)))
        out_shape.append(jax.ShapeDtypeStruct((D_IN, D_MODEL), BF16))
        scratch.append(pltpu.VMEM((D_IN, D_MODEL), F32))
    if want_dx:
        scratch.append(pltpu.VMEM((1, D_MODEL), F32))
    params = pltpu.CompilerParams(dimension_semantics=("arbitrary",), vmem_limit_bytes=VMEM_LIMIT,
                                  collective_id=COLLECTIVE_PAIR_SUM[layer] if pair else None)
    return pl.pallas_call(
        body, name=f"bwd_in_{part}{layer}", grid=(steps,),
        in_specs=in_specs, out_specs=out_specs, out_shape=out_shape, scratch_shapes=scratch,
        compiler_params=params,
    )(*operands)


def _mesh_pos():
    return lax.axis_index("x"), lax.axis_index("y"), lax.axis_index("c")


def _device_rows(ref, m, px, py, pc):
    return ref.at[pl.ds(pl.multiple_of((4 * px + 2 * py + pc) * m, 16 if m % 16 == 0 else 8), m), :]


def _allgather(srcs, out_dtype, name, later=()):
    na, nb = len(srcs), len(later)
    every = list(srcs) + list(later)
    shapes = [(a.shape[-2], a.shape[-1]) for a, _ in every]

    def body(*refs):
        xs, refs = refs[:na + nb], refs[na + nb:]
        outs, cast, land, refs = refs[:na], refs[na:na + nb], refs[na + nb:na + 2 * nb], refs[na + 2 * nb:]
        stage, raw, (send_sems, recv_sems, local_sems, load_sems) = refs[:na], refs[na:2 * na + nb], refs[2 * na + nb:]
        loads = [pltpu.make_async_copy(xs[i].at[every[i][1]], raw[i], load_sems.at[i]) for i in range(na + nb)]
        for cp in loads:
            cp.start()
        x, y, c = _mesh_pos()
        me, sibling = (x, y, c), (x, y, 1 - c)
        near = [(1 - x, y), (x, 1 - y)]
        far = (1 - x, 1 - y)
        relay_from, relay_to = (x ^ (1 - c), y ^ c), (x ^ c, y ^ (1 - c))
        _handshake([sibling] + [(*chip, c) for chip in near])
        k_from, k_to = 1 + c, 2 - c

        def slot(a, px, py, pc):
            return _device_rows(outs[a], shapes[a][0], px, py, pc)

        def copy(a, k, block, to, src=None):
            return pltpu.make_async_remote_copy(
                src_ref=slot(a, *block) if src is None else src, dst_ref=slot(a, *block),
                send_sem=send_sems.at[a, k], recv_sem=recv_sems.at[a, k], device_id=to, device_id_type=MESH)

        def cast_block(i):
            loads[i].wait()
            return raw[i][...].astype(out_dtype)

        for a in range(na):
            stage[a][...] = cast_block(a)
        mine = [pltpu.make_async_copy(stage[a], slot(a, *me), local_sems.at[a]) for a in range(na)]
        for cp in mine:
            cp.start()
        sent = []
        for a in range(na):
            sent.append(copy(a, 0, me, sibling, src=stage[a]))
            sent += [copy(a, 1 + j, me, (*chip, c), src=stage[a]) for j, chip in enumerate(near)]
        for cp in sent:
            cp.start()
        for b in range(nb):
            cast[b][...] = cast_block(na + b)
            cp = pltpu.make_async_copy(cast[b], _device_rows(land[b], shapes[na + b][0], *me), local_sems.at[na + b])
            cp.start()
            mine.append(cp)
        for a in range(na):
            copy(a, k_from, (*relay_from, c), me).wait_recv()
            sent += [copy(a, 3, (*relay_from, c), (*relay_to, c)), copy(a, 3 + k_from, (*relay_from, c), sibling)]
            sent[-2].start()
            sent[-1].start()
        for a in range(na):
            copy(a, k_to, (*relay_to, c), me).wait_recv()
            sent.append(copy(a, 3 + k_to, (*relay_to, c), sibling))
            sent[-1].start()
        for a in range(na):
            copy(a, 3, (*far, c), me).wait_recv()
            sent.append(copy(a, 6, (*far, c), sibling))
            sent[-1].start()
        for a in range(na):
            copy(a, 0, sibling, me).wait_recv()
            for j, chip in enumerate(near + [far]):
                copy(a, 4 + j, (*chip, 1 - c), me).wait_recv()
        for cp in sent:
            cp.wait_send()
        for cp in mine:
            cp.wait()

    vmem = pl.BlockSpec(memory_space=pltpu.VMEM)
    hbm = pl.BlockSpec(memory_space=pl.ANY)
    gathered = [jax.ShapeDtypeStruct((N_DEV * m, n), out_dtype) for m, n in shapes]
    out = pl.pallas_call(
        body, name=name,
        in_specs=[hbm] * (na + nb),
        out_specs=[hbm] * na + [vmem] * nb + [hbm] * nb,
        out_shape=gathered[:na] + [jax.ShapeDtypeStruct(s, out_dtype) for s in shapes[na:]] + gathered[na:],
        scratch_shapes=([pltpu.VMEM(s, out_dtype) for s in shapes[:na]]
                        + [pltpu.VMEM(s, a.dtype) for s, (a, _) in zip(shapes, every)]
                        + [pltpu.SemaphoreType.DMA((na, 7)), pltpu.SemaphoreType.DMA((na, 7)),
                           pltpu.SemaphoreType.DMA((na + nb,)), pltpu.SemaphoreType.DMA((na + nb,))]),
        compiler_params=pltpu.CompilerParams(vmem_limit_bytes=VMEM_LIMIT, collective_id=COLLECTIVE_GATHER_W0),
    )(*[a for a, _ in every])
    return out[:na], out[na:na + nb], out[na + nb:]


ALL_PEERS = tuple(range(1, N_DEV))
SIBLING_AND_SAME_CORE = (1, 2, 4, 6)


def _related(k, x, y, c):
    return x ^ ((k >> 2) & 1), y ^ ((k >> 1) & 1), c ^ (k & 1)


def _gather_start(blocks, lands, relations, collective_id, name):
    na = len(blocks)

    def body(*refs):
        src, land, sems, token = refs[:na], refs[na:2 * na], refs[2 * na:4 * na], refs[-1]
        x, y, c = _mesh_pos()
        _handshake([_related(k, x, y, c) for k in sorted(set().union(*relations))])
        for a in range(na):
            for k in relations[a]:
                pltpu.make_async_remote_copy(
                    src_ref=src[a], dst_ref=_device_rows(land[a], blocks[a].shape[0], x, y, c),
                    send_sem=sems[2 * a].at[k - 1], recv_sem=sems[2 * a + 1].at[k - 1],
                    device_id=_related(k, x, y, c), device_id_type=MESH).start()
        token[...] = jnp.zeros_like(token)

    bufs = [pltpu.HBM(t.shape, t.dtype) for t in list(blocks) + list(lands)]
    out = pl.pallas_call(
        body, name=name,
        out_shape=(*([pltpu.SemaphoreType.DMA((N_DEV - 1,))] * (2 * na)), *bufs, jax.ShapeDtypeStruct((8, 128), F32)),
        in_specs=[_HBM] * (2 * na),
        out_specs=(*([_SEM] * (2 * na)), *([_HBM] * (2 * na)), pl.BlockSpec(memory_space=pltpu.VMEM)),
        input_output_aliases={i: 2 * na + i for i in range(2 * na)},
        compiler_params=pltpu.CompilerParams(has_side_effects=_EFFECT, collective_id=collective_id),
    )(*[pltpu.with_memory_space_constraint(t, pltpu.HBM) for t in list(blocks) + list(lands)])
    sems = [(out[2 * a], out[2 * a + 1]) for a in range(na)]
    return sems, out[2 * na:3 * na], out[3 * na:4 * na], out[-1]


def _gather_wait(sems, block, land, relations, after, name):
    def body(src, land_ref, send_sem, recv_sem, after_ref, src_out, land_out):
        x, y, c = _mesh_pos()
        for k in relations:
            peer = _related(k, x, y, c)
            cp = pltpu.make_async_remote_copy(
                src_ref=src, dst_ref=_device_rows(land_ref, block.shape[0], *peer),
                send_sem=send_sem.at[k - 1], recv_sem=recv_sem.at[k - 1], device_id=peer, device_id_type=MESH)
            cp.wait_send()
            cp.wait_recv()

    out = pl.pallas_call(
        body, name=name,
        out_shape=(pltpu.HBM(block.shape, block.dtype), pltpu.HBM(land.shape, land.dtype)),
        in_specs=[_HBM, _HBM, _SEM, _SEM, pl.BlockSpec(memory_space=pl.ANY)],
        out_specs=[_HBM, _HBM],
        input_output_aliases={0: 0, 1: 1},
        compiler_params=pltpu.CompilerParams(has_side_effects=_EFFECT),
    )(block, land, sems[0], sems[1], after)
    return out[1]


(COLLECTIVE_GATHER_W0, COLLECTIVE_GATHER_W1, COLLECTIVE_FORWARD_W_IN1, COLLECTIVE_EXCHANGE_1, COLLECTIVE_EXCHANGE_0A,
 COLLECTIVE_EXCHANGE_0B, COLLECTIVE_GATHER_SMALL) = range(1, 8)
COLLECTIVE_PAIR_SUM = (8, 9)


def _handshake(peers):
    barrier = pltpu.get_barrier_semaphore()
    for peer in peers:
        pl.semaphore_signal(barrier, inc=1, device_id=peer, device_id_type=MESH)
    pl.semaphore_wait(barrier, len(peers))


def _forward_plan(land_ref, m):
    x, y, c = _mesh_pos()
    return [_device_rows(land_ref, m, qx, qy, c) for qx, qy in ((1 - x, y), (x, 1 - y), (1 - x, 1 - y))], (x, y, 1 - c)


def _forward_start(land, m, name):
    def body(land_ref, send_sem, recv_sem, land_out, token):
        _handshake([(lax.axis_index("x"), lax.axis_index("y"), 1 - lax.axis_index("c"))])
        rows, sibling = _forward_plan(land_ref, m)
        for j, r in enumerate(rows):
            pltpu.make_async_remote_copy(src_ref=r, dst_ref=r, send_sem=send_sem.at[j], recv_sem=recv_sem.at[j],
                                         device_id=sibling, device_id_type=MESH).start()
        token[...] = jnp.zeros_like(token)

    out = pl.pallas_call(
        body, name=name,
        out_shape=(pltpu.SemaphoreType.DMA((3,)), pltpu.SemaphoreType.DMA((3,)), pltpu.HBM(land.shape, land.dtype),
                   jax.ShapeDtypeStruct((8, 128), F32)),
        in_specs=[_HBM],
        out_specs=(_SEM, _SEM, _HBM, pl.BlockSpec(memory_space=pltpu.VMEM)),
        input_output_aliases={0: 2},
        compiler_params=pltpu.CompilerParams(has_side_effects=_EFFECT, collective_id=COLLECTIVE_FORWARD_W_IN1),
    )(pltpu.with_memory_space_constraint(land, pltpu.HBM))
    return (out[0], out[1]), out[2], out[3]


def _forward_wait(sems, land, m, after, name):
    def body(land_ref, send_sem, recv_sem, after_ref, land_out):
        x, y, c = _mesh_pos()
        mine, sibling = _forward_plan(land_ref, m)
        theirs = [_device_rows(land_ref, m, qx, qy, 1 - c) for qx, qy in ((1 - x, y), (x, 1 - y), (1 - x, 1 - y))]
        for j in range(3):
            cp = pltpu.make_async_remote_copy(src_ref=mine[j], dst_ref=theirs[j], send_sem=send_sem.at[j],
                                              recv_sem=recv_sem.at[j], device_id=sibling, device_id_type=MESH)
            cp.wait_send()
            cp.wait_recv()

    return pl.pallas_call(
        body, name=name,
        out_shape=pltpu.HBM(land.shape, land.dtype),
        in_specs=[_HBM, _SEM, _SEM, pl.BlockSpec(memory_space=pl.ANY)],
        out_specs=_HBM,
        input_output_aliases={0: 0},
        compiler_params=pltpu.CompilerParams(has_side_effects=_EFFECT),
    )(land, sems[0], sems[1], after)


_HBM = pl.BlockSpec(memory_space=pltpu.HBM)
_SEM = pl.BlockSpec(memory_space=pltpu.SEMAPHORE)
_EFFECT = pltpu.SideEffectType.DATAFLOW_SIDE_EFFECTING


def _exchange_plan(direct):
    x, y, c = _mesh_pos()
    if not direct:
        return [(j, j, (qx, qy, c)) for j, (qx, qy) in enumerate([(1 - x, y), (x, 1 - y), (1 - x, 1 - y)])]
    plan = []
    for k in range(1, N_DEV):
        px, py, pc = x ^ ((k >> 2) & 1), y ^ ((k >> 1) & 1), c ^ (k & 1)
        plan.append((4 * px + 2 * py + pc, k - 1, (px, py, pc)))
    return plan


def _exchange_copies(directs):
    copies, base = [], 0
    for a, direct in enumerate(directs):
        plan = _exchange_plan(direct)
        copies += [(a, block, slot, peer, base + slot) for block, slot, peer in plan]
        base += len(plan)
    return copies, base


def _exchange_start(srcs, directs, collective_id, name):
    na = len(srcs)
    slots = [N_DEV - 1 if direct else 3 for direct in directs]

    def body(*refs):
        src, land = refs[:na], refs[na:2 * na]
        send_sem, recv_sem = refs[2 * na], refs[2 * na + 1]
        token = refs[-1]
        _handshake([peer for _, _, peer in _exchange_plan(any(directs))])
        for a, block, slot, peer, sem in _exchange_copies(directs)[0]:
            pltpu.make_async_remote_copy(
                src_ref=src[a].at[block], dst_ref=land[a].at[slot], send_sem=send_sem.at[sem],
                recv_sem=recv_sem.at[sem], device_id=peer, device_id_type=MESH).start()
        token[...] = jnp.zeros_like(token)

    zones = [jax.ShapeDtypeStruct((n,) + t.shape[1:], t.dtype) for n, t in zip(slots, srcs)]
    bufs = [pltpu.HBM(t.shape, t.dtype) for t in list(srcs) + zones]
    out = pl.pallas_call(
        body, name=name,
        out_shape=(pltpu.SemaphoreType.DMA((sum(slots),)), pltpu.SemaphoreType.DMA((sum(slots),)), *bufs,
                   jax.ShapeDtypeStruct((8, 128), F32)),
        in_specs=[_HBM] * (2 * na),
        out_specs=(_SEM, _SEM, *([_HBM] * (2 * na)), pl.BlockSpec(memory_space=pltpu.VMEM)),
        input_output_aliases={i: 2 + i for i in range(2 * na)},
        compiler_params=pltpu.CompilerParams(has_side_effects=_EFFECT, collective_id=collective_id),
    )(*[pltpu.with_memory_space_constraint(t, pltpu.HBM) for t in srcs],
      *[pltpu.with_memory_space_constraint(lax.empty(t.shape, t.dtype), pltpu.HBM) for t in zones])
    return out[0], out[1], out[2:2 + na], out[2 + na:2 + 2 * na], out[-1]


def _exchange_wait(send_sem, recv_sem, srcs, lands, directs, after, name):
    na = len(srcs)

    def body(*refs):
        src, land = refs[:na], refs[na:2 * na]
        send_sem_ref, recv_sem_ref = refs[2 * na], refs[2 * na + 1]
        for a, block, slot, peer, sem in _exchange_copies(directs)[0]:
            cp = pltpu.make_async_remote_copy(
                src_ref=src[a].at[block], dst_ref=land[a].at[slot], send_sem=send_sem_ref.at[sem],
                recv_sem=recv_sem_ref.at[sem], device_id=peer, device_id_type=MESH)
            cp.wait_send()
            cp.wait_recv()

    bufs = [pltpu.HBM(t.shape, t.dtype) for t in list(srcs) + list(lands)]
    out = pl.pallas_call(
        body, name=name,
        out_shape=tuple(bufs),
        in_specs=[_HBM] * (2 * na) + [_SEM, _SEM, pl.BlockSpec(memory_space=pl.ANY)],
        out_specs=[_HBM] * (2 * na),
        input_output_aliases={i: i for i in range(2 * na)},
        compiler_params=pltpu.CompilerParams(has_side_effects=_EFFECT),
    )(*srcs, *lands, send_sem, recv_sem, after)
    return out[:na], out[na:]


def _own_then_slots(mine_ref, lands_ref, rows=slice(None)):
    if len(mine_ref.shape) == 3:
        x, y, c = _mesh_pos()
        total = mine_ref[4 * x + 2 * y + c, rows, :].astype(F32)
    else:
        total = mine_ref[rows, :].astype(F32)
    for j in range(lands_ref.shape[0]):
        total = total + lands_ref[j, rows, :].astype(F32)
    return total


SMALL_ROWS = 2 * PACK_SLICE + 2 * 8


def _small_gather_start(mine, lands, dgpre, name):
    def body(*refs):
        hm, ld, dg = refs[:DEPTH], refs[DEPTH:2 * DEPTH], refs[2 * DEPTH:3 * DEPTH]
        send_sem, recv_sem, blk, land, token, own, slots, rows, built, local_sems = refs[3 * DEPTH:]
        x, y, c = _mesh_pos()
        loads = []
        for l in range(DEPTH):
            loads += [pltpu.make_async_copy(hm[l].at[4 * x + 2 * y + c], own.at[l], local_sems.at[3 * l]),
                      pltpu.make_async_copy(ld[l], slots.at[l], local_sems.at[3 * l + 1]),
                      pltpu.make_async_copy(dg[l], rows.at[l], local_sems.at[3 * l + 2])]
        for cp in loads:
            cp.start()
        _handshake([_related(k, x, y, c) for k in ALL_PEERS])
        for cp in loads:
            cp.wait()
        for l in range(DEPTH):
            total = own[l]
            for j in range(N_DEV - 1):
                total = total + slots[l, j]
            built[PACK_SLICE * l:PACK_SLICE * (l + 1), :] = total
            built[2 * PACK_SLICE + 8 * l:2 * PACK_SLICE + 8 * (l + 1), :] = rows[l]
        stores = [pltpu.make_async_copy(built, blk, local_sems.at[3 * DEPTH]),
                  pltpu.make_async_copy(built, _device_rows(land, SMALL_ROWS, x, y, c), local_sems.at[3 * DEPTH + 1])]
        for cp in stores:
            cp.start()
        for cp in stores:
            cp.wait()
        for k in ALL_PEERS:
            pltpu.make_async_remote_copy(
                src_ref=blk, dst_ref=_device_rows(land, SMALL_ROWS, x, y, c), send_sem=send_sem.at[k - 1],
                recv_sem=recv_sem.at[k - 1], device_id=_related(k, x, y, c), device_id_type=MESH).start()
        token[...] = jnp.zeros_like(token)

    hbm = pl.BlockSpec(memory_space=pl.ANY)
    out = pl.pallas_call(
        body, name=name,
        in_specs=[hbm] * (3 * DEPTH),
        out_specs=(_SEM, _SEM, _HBM, _HBM, pl.BlockSpec(memory_space=pltpu.VMEM)),
        out_shape=(pltpu.SemaphoreType.DMA((N_DEV - 1,)), pltpu.SemaphoreType.DMA((N_DEV - 1,)),
                   pltpu.HBM((SMALL_ROWS, 128), F32), pltpu.HBM((N_DEV * SMALL_ROWS, 128), F32),
                   jax.ShapeDtypeStruct((8, 128), F32)),
        scratch_shapes=[pltpu.VMEM((DEPTH, PACK_SLICE, 128), F32), pltpu.VMEM((DEPTH, N_DEV - 1, PACK_SLICE, 128), F32),
                        pltpu.VMEM((DEPTH, 8, 128), F32), pltpu.VMEM((SMALL_ROWS, 128), F32),
                        pltpu.SemaphoreType.DMA((3 * DEPTH + 2,))],
        compiler_params=pltpu.CompilerParams(has_side_effects=_EFFECT, collective_id=COLLECTIVE_GATHER_SMALL),
    )(*mine, *lands, *dgpre)
    return (out[0], out[1]), out[2], out[3], out[4]


def _adamw_math(w, g, m, v):
    m = ADAM_B1 * m + (1.0 - ADAM_B1) * g
    v = ADAM_B2 * v + (1.0 - ADAM_B2) * (g * g)
    m_hat = m / (1.0 - ADAM_B1 ** ADAM_STEP)
    v_hat = v / (1.0 - ADAM_B2 ** ADAM_STEP)
    delta = -ADAM_LR * (m_hat / (jnp.sqrt(v_hat) + ADAM_EPS) + ADAM_WD * w)
    return delta, m, v


def _adamw_layer(layer, mine, lands, w, m, v, earlier, token, name, rows):
    _, mm, nn = w.shape

    def body(hm_ref, ld_ref, w_ref, m_ref, v_ref, _, *refs):
        g_ref, d_ref, nm_ref, nv_ref = refs[-4:]
        g = _own_then_slots(hm_ref, ld_ref)
        g_ref[...] = g
        d, nm, nv = _adamw_math(w_ref[...], g, m_ref[...], v_ref[...])
        d_ref[...] = d
        nm_ref[...] = nm
        nv_ref[...] = nv

    spec = pl.BlockSpec((None, rows, nn), lambda i: (layer, i, 0))
    carried = [] if earlier is None else list(earlier)
    return pl.pallas_call(
        body, name=name, grid=(mm // rows,),
        in_specs=([pl.BlockSpec((rows, nn), lambda i: (i, 0)) if mine.ndim == 2
                   else pl.BlockSpec((N_DEV, rows, nn), lambda i: (0, i, 0)),
                   pl.BlockSpec((lands.shape[0], rows, nn), lambda i: (0, i, 0)),
                   spec, spec, spec] + [pl.BlockSpec(memory_space=pl.ANY)] * (1 + len(carried))),
        out_specs=[spec] * 4,
        out_shape=[jax.ShapeDtypeStruct(w.shape, F32)] * 4,
        input_output_aliases={6 + t: t for t in range(len(carried))},
        compiler_params=_compiler_params(("arbitrary",)),
    )(mine, lands, w, m, v, token, *carried)


def _adamw_small(gathered, params):
    def body(all_ref, *refs):
        ins, outs, packs = refs[:15], refs[15:15 + 21], refs[15 + 21]
        loss_ref = outs[0]
        for dev in range(N_DEV):
            for l in range(DEPTH):
                packs[l, PACK_SLICE * dev:PACK_SLICE * (dev + 1), :] = (
                    all_ref[SMALL_ROWS * dev + PACK_SLICE * l:SMALL_ROWS * dev + PACK_SLICE * (l + 1), :])
        loss_ref[...] = packs[DEPTH - 1, ROW_LOSS:ROW_LOSS + 1, 0:1]

        def update(p, sel, g):
            w_ref, m_ref, v_ref = ins[p], ins[5 + p], ins[10 + p]
            d, nm, nv = _adamw_math(w_ref[sel], g, m_ref[sel], v_ref[sel])
            for t, val in enumerate((g, d, nm, nv)):
                outs[1 + 5 * t + p][sel] = val

        for l in range(DEPTH):
            gp = packs.at[l]
            row0 = 2 * PACK_SLICE + 8 * l
            dgpre = all_ref[row0:row0 + 8, :]
            for dev in range(1, N_DEV):
                dgpre = dgpre + all_ref[SMALL_ROWS * dev + row0:SMALL_ROWS * dev + row0 + 8, :]
            for grp in range(4):
                update(0, (l, grp), gp[ROW_PW + BLOCK * grp:ROW_PW + BLOCK * (grp + 1), :])
                update(1, (slice(l, l + 1), slice(128 * grp, 128 * (grp + 1))), gp[ROW_SC + grp:ROW_SC + grp + 1, :])
            update(2, (slice(l, l + 1), slice(None)), gp[ROW_SINK:ROW_SINK + 1, 0:N_HEADS])
            for r in range(D_MODEL // 128):
                sel = (slice(l, l + 1), slice(128 * r, 128 * (r + 1)))
                update(3, sel, dgpre[r:r + 1, :])
                update(4, sel, gp[ROW_NPOST + r:ROW_NPOST + r + 1, :])

    shapes = [jax.ShapeDtypeStruct(p.shape, F32) for p in params[:5]]
    return pl.pallas_call(
        body, name="adamw_small",
        out_shape=[jax.ShapeDtypeStruct((1, 1), F32)] + shapes * 4,
        scratch_shapes=[pltpu.VMEM((DEPTH, PACK_ROWS, 128), F32)],
        compiler_params=_compiler_params(),
    )(gathered, *params)


def kernel(x, w_in, pool_w, pool_scale, attn_sinks, w_out, norm_pre, norm_post, loss_target, m_w_in, m_pool_w, m_pool_scale, m_attn_sinks, m_w_out, m_norm_pre, m_norm_post, v_w_in, v_pool_w, v_pool_scale, v_attn_sinks, v_w_out, v_norm_pre, v_norm_post):
    x0 = x.reshape(SEQ, D_MODEL)
    target = loss_target.reshape(SEQ, D_MODEL)
    bias = jnp.asarray(_attn_bias())
    w_in_t, m_in_t, v_in_t = (jnp.swapaxes(t, 1, 2) for t in (w_in, m_w_in, v_w_in))

    (win0, wout0), later, lands = _allgather([(w_in_t, 0), (w_out, 0)], BF16, "gather_w0",
                                              later=[(w_in_t, 1), (w_out, 1)])
    sems, later, lands, token = _gather_start(later, lands, [SIBLING_AND_SAME_CORE, ALL_PEERS], COLLECTIVE_GATHER_W1,
                                              "gather_w1_start")
    win_full, wout_full = [win0, None], [wout0, None]

    saved = []
    xl = x0
    for layer in range(DEPTH):
        front = (layer, xl, norm_pre, win_full[layer], token, attn_sinks, pool_w, pool_scale, bias)
        if layer == 0:
            u, pg, q, k, v, ag, z, a = _fwd_front(*front)
            land = _gather_wait(sems[0], later[0], lands[0], SIBLING_AND_SAME_CORE, z, "gather_w_in1_wait")
            fsems, land, token = _forward_start(land, IN_SHARD, "forward_w_in1_start")
            x_next, y = _fwd_out(layer, z, xl, norm_post, wout_full[layer], token)
            win_full[1] = _forward_wait(fsems, land, IN_SHARD, x_next, "forward_w_in1_wait")
            wout_full[1] = _gather_wait(sems[1], later[1], lands[1], ALL_PEERS, win_full[1], "gather_w_out1_wait")
        else:
            u, pg, q, k, v, ag, z, a, x_next, y = _fwd_front(*front, out=(norm_post, wout_full[layer]))
        saved.append((xl, u, pg, q, k, v, ag, z, a, y))
        xl = x_next

    params_small = [pool_w, pool_scale, attn_sinks, norm_pre, norm_post,
                    m_pool_w, m_pool_scale, m_attn_sinks, m_norm_pre, m_norm_post,
                    v_pool_w, v_pool_scale, v_attn_sinks, v_norm_pre, v_norm_post]

    def start(srcs, directs, paired, collective_id, tag):
        send_sem, recv_sem, srcs, lands, started = _exchange_start(srcs, directs, collective_id, f"exchange_start{tag}")
        return (send_sem, recv_sem, srcs, lands, paired, directs), started

    def finish(handle, after, tag):
        send_sem, recv_sem, srcs, lands, paired, directs = handle
        srcs, lands = _exchange_wait(send_sem, recv_sem, srcs, lands, directs, after, f"exchange_wait{tag}")
        return [s if p is None else p for s, p in zip(srcs, paired)], lands

    def back(layer, top, first, second):
        xin, u, pg, q, k, v, ag, z, a, y = saved[layer]
        return _bwd_back(layer, top, first, second, y, z, norm_post, wout_full[layer], attn_sinks, u, pg, q, k, v,
                         ag, a, pool_w, pool_scale, bias)

    dgpre = [None] * DEPTH
    dx, dproj, gw_out, pack = back(1, True, xl, target)
    dx, dgpre[1], chip_sums, own_sum = _bwd_in(1, "both_pair", token, dproj, saved[1][0], norm_pre, dx, win_full[1])
    top, token = start([chip_sums, gw_out.reshape(N_DEV, OUT_SHARD, D_MODEL), pack.reshape(N_DEV, PACK_SLICE, 128)],
                       [False, True, True], [own_sum, None, None], COLLECTIVE_EXCHANGE_1, "1")

    dproj, gw_out, pack = back(0, False, dx, token)
    early, token = start([gw_out.reshape(N_DEV, OUT_SHARD, D_MODEL), pack.reshape(N_DEV, PACK_SLICE, 128)],
                         [True, True], [None, None], COLLECTIVE_EXCHANGE_0A, "0a")
    chip_sums, own_sum = _bwd_in(0, "dw_pair", token, dproj, saved[0][0], norm_pre)
    late, token = start([chip_sums], [False], [own_sum], COLLECTIVE_EXCHANGE_0B, "0b")
    dx, dgpre[0] = _bwd_in(0, "dx", token, dproj, saved[0][0], norm_pre, dx, win_full[0])

    own1, lands1 = finish(top, dx, "1")
    big_in = _adamw_layer(1, own1[0], lands1[0], w_in_t, m_in_t, v_in_t, None, token, "adamw_in1", ADAM_ROWS_IN)
    big_out = _adamw_layer(1, own1[1], lands1[1], w_out, m_w_out, v_w_out, None, token, "adamw_out1", ADAM_ROWS_OUT)
    own0a, lands0a = finish(early, big_out[0], "0a")
    sems, block, land, token = _small_gather_start([own0a[1], own1[2]], [lands0a[1], lands1[2]], dgpre,
                                                   "gather_small_start")
    big_out = _adamw_layer(0, own0a[0], lands0a[0], w_out, m_w_out, v_w_out, big_out, token, "adamw_out0", ADAM_ROWS_OUT)
    own0b, lands0b = finish(late, big_out[0], "0b")
    big_in = _adamw_layer(0, own0b[0], lands0b[0], w_in_t, m_in_t, v_in_t, big_in, token, "adamw_in0", ADAM_ROWS_IN)
    gathered = _gather_wait(sems, block, land, ALL_PEERS, big_in[0], "gather_small_wait")
    small_out = _adamw_small(gathered, params_small)
    loss = small_out[0].reshape(())

    outs = [loss, dx.reshape(1, SEQ, D_MODEL)]
    for t in range(4):
        pw_, sc_, sk_, npre_, npost_ = small_out[1 + 5 * t:6 + 5 * t]
        outs += [jnp.swapaxes(big_in[t], 1, 2), pw_, sc_, sk_, big_out[t], npre_, npost_]
    return tuple(outs)
```

```python
import numpy as np
import jax
import jax.numpy as jnp
from jax import lax
from jax.experimental import pallas as pl
from jax.experimental.pallas import tpu as pltpu

F32 = jnp.float32
BF16 = jnp.bfloat16

N_DEV = 8
SEQ = 2048
D_MODEL = 1024
D_POOL = 512
D_ATTN = 512
D_KV = 128
D_IN = 2304
N_HEADS = 8
GQA = 4
HEAD_DIM = 64
BLOCK = 128
POOL_WINDOWS = (2, 4, 8, 16)
DEPTH = 2
EPS = 1e-6
NEG_INF = -1e30
SCALE = HEAD_DIM ** -0.5
IN_SHARD = D_IN // N_DEV
OUT_SHARD = D_MODEL // N_DEV

COL_U, COL_PG, COL_Q, COL_K, COL_V, COL_AG = 0, 512, 1024, 1536, 1664, 1792

ADAM_LR = 0.001
ADAM_B1 = 0.9
ADAM_B2 = 0.999
ADAM_EPS = 1e-08
ADAM_WD = 0.01
ADAM_STEP = 10

TOKEN_TILE = 512
FWD_OUT_TILE = 1024
ADAM_ROWS_IN, ADAM_ROWS_OUT = 144, 64
VMEM_LIMIT = 56 * 1024 * 1024
MESH = pl.DeviceIdType.MESH

ROW_PW, ROW_SC, ROW_SINK, ROW_NPOST, ROW_LOSS = 0, 512, 520, 536, 544
PACK_ROWS = 576
PACK_SLICE = PACK_ROWS // N_DEV


def _nn(a, b):
    return jnp.dot(a, b, preferred_element_type=F32)


def _nt(a, b):
    return lax.dot_general(a, b, (((1,), (1,)), ((), ())), preferred_element_type=F32)


def _tn(a, b):
    return lax.dot_general(a, b, (((0,), (0,)), ((), ())), preferred_element_type=F32)


def _silu_parts(g):
    s = jax.nn.sigmoid(g)
    return g * s, s * (1.0 + g * (1.0 - s))


def _resident(shape):
    return pl.BlockSpec(shape, lambda *_: (0,) * len(shape), pipeline_mode=pl.Buffered(1))


def _compiler_params(sem=None):
    if sem is None:
        return pltpu.CompilerParams(vmem_limit_bytes=VMEM_LIMIT)
    return pltpu.CompilerParams(dimension_semantics=sem, vmem_limit_bytes=VMEM_LIMIT)


def _attn_bias():
    t = np.arange(BLOCK)[None, :]
    j = np.arange(BLOCK)[:, None]
    current = j <= t
    dist = np.where(current, t - j, t + BLOCK - j).astype(np.float32)
    out = np.zeros((2, 2, BLOCK, GQA * BLOCK), np.float32)
    for variant in range(2):
        valid = current | (variant == 1)
        for kv in range(2):
            for g in range(GQA):
                slope = np.float32(2.0 ** (-(kv * GQA + g + 1)))
                out[variant, kv, :, g * BLOCK:(g + 1) * BLOCK] = np.where(valid, -slope * dist, np.float32(NEG_INF))
    return out


def _replicate_head(kx, kv):
    rolled = pltpu.roll(kx, 64, 1)
    lane = lax.broadcasted_iota(jnp.int32, kx.shape, 1)
    dup = jnp.where(lane < 64, kx, rolled) if kv == 0 else jnp.where(lane < 64, rolled, kx)
    return jnp.concatenate([dup, dup], axis=1).astype(BF16)


def _stack_heads(qv):
    lane = lax.broadcasted_iota(jnp.int32, qv.shape, 1)
    zero = jnp.zeros_like(qv)
    return jnp.concatenate([jnp.where((lane >= 64 * g) & (lane < 64 * g + 64), qv, zero) for g in range(GQA)], axis=0)


def _unstack_heads(xs):
    lane = lax.broadcasted_iota(jnp.int32, (BLOCK, 256), 1)
    return jnp.where(lane < 64, xs[0:128], jnp.where(lane < 128, xs[128:256], jnp.where(lane < 192, xs[256:384], xs[384:512])))


def _fold_heads(r):
    h = r[:, 0:128] + r[:, 128:256]
    return h + pltpu.roll(h, 64, 1)


def _sink_row(sink_ref, layer, kv):
    lane = lax.broadcasted_iota(jnp.int32, (1, GQA * BLOCK), 1)
    s4 = [sink_ref[layer, kv * GQA + g] for g in range(GQA)]
    return jnp.where(lane < 128, s4[0], jnp.where(lane < 256, s4[1], jnp.where(lane < 384, s4[2], s4[3])))


def _band_is_current():
    j = lax.broadcasted_iota(jnp.int32, (BLOCK, GQA * BLOCK), 0)
    t = lax.broadcasted_iota(jnp.int32, (BLOCK, GQA * BLOCK), 1) & (BLOCK - 1)
    return j <= t


def _pack_band(full, current):
    return jnp.where(current, full[BLOCK:], full[:BLOCK])


def _unpack_band(packed, current):
    zero = jnp.zeros_like(packed)
    return jnp.concatenate([jnp.where(current, zero, packed), jnp.where(current, packed, zero)], axis=0)


def _probs_keys_major(k_rep, q_st, bias, sink, current):
    st = _pack_band(_nt(k_rep, q_st), current) * SCALE + bias
    m = jnp.maximum(jnp.max(st, axis=0, keepdims=True), sink)
    p = jnp.exp(st - m)
    esink = jnp.exp(sink - m)
    rl = 1.0 / (jnp.sum(p, axis=0, keepdims=True) + esink)
    return p * rl, esink * rl


WINDOW_HALO = 16


def _window_sum(ext, w, forward):
    s = ext
    sh = 1
    while sh < w:
        s = s + pltpu.roll(s, (ext.shape[0] - sh) if forward else sh, 0)
        sh *= 2
    return s


def _inv_count(n, w):
    t = n * BLOCK + lax.broadcasted_iota(jnp.int32, (BLOCK, 1), 0) + 1
    return 1.0 / jnp.minimum(t.astype(F32), float(w))


def _kv_ext(ref, n):
    r0 = pl.multiple_of(jnp.maximum(n - 1, 0) * BLOCK, BLOCK)
    r1 = pl.multiple_of(n * BLOCK, BLOCK)
    return jnp.concatenate([ref[pl.ds(r0, BLOCK), :], ref[pl.ds(r1, BLOCK), :]], axis=0)


def _rows_of(vec_ref, pack_ref, row0):
    for r in range(D_MODEL // 128):
        pack_ref[row0 + r:row0 + r + 1, :] = vec_ref[:, 128 * r:128 * (r + 1)]


FRONT_TILE = 4 * BLOCK


def _fwd_front(layer, x, norm_pre, w_in_t, token, sinks, pool_w, pool_scale, bias, out=None):
    tm = FRONT_TILE

    def body(sink_ref, x_ref, g_ref, w_ref, _, pw_ref, sc_ref, bias_ref, *refs):
        if out is not None:
            gpost_ref, wo_ref, *refs = refs
            xn_ref, y_ref = refs[8:10]
        u_ref, pg_ref, q_ref, k_ref, v_ref, ag_ref, z_ref, a_ref = refs[:8]
        uprev, kprev, vprev = refs[-3:]
        i = pl.program_id(0)

        @pl.when(i == 0)
        def _():
            uprev[...] = jnp.zeros_like(uprev)
            kprev[...] = jnp.zeros_like(kprev)
            vprev[...] = jnp.zeros_like(vprev)

        xv = x_ref[...]
        r = lax.rsqrt(jnp.mean(xv * xv, axis=-1, keepdims=True) + EPS)
        h = (xv * r * g_ref[layer:layer + 1, :]).astype(BF16)
        u_ref[...] = _nt(h, w_ref[COL_U:COL_PG, :])
        pg_ref[...] = _nt(h, w_ref[COL_PG:COL_Q, :])
        for sb in range(tm // BLOCK):
            n = (tm // BLOCK) * i + sb
            rows = slice(BLOCK * sb, BLOCK * (sb + 1))
            before = slice(BLOCK * (sb - 1), BLOCK * sb)
            uv = u_ref[rows, :]
            halo = (uprev[BLOCK - WINDOW_HALO:, :] if sb == 0
                    else u_ref[BLOCK * sb - WINDOW_HALO:BLOCK * sb, :])
            ext = jnp.concatenate([halo, uv], axis=0)
            for g, w in enumerate(POOL_WINDOWS):
                cs = slice(BLOCK * g, BLOCK * (g + 1))
                win = _window_sum(ext[:, cs], w, forward=False)[WINDOW_HALO:]
                pooled = win * _inv_count(n, w) - uv[:, cs]
                mixed = _nn(pooled.astype(BF16), pw_ref[g].astype(BF16))
                gate, _ = _silu_parts(pg_ref[rows, cs])
                z_ref[rows, cs] = (mixed * sc_ref[layer:layer + 1, cs] * gate).astype(BF16)

        q_ref[...] = _nt(h, w_ref[COL_Q:COL_K, :]).astype(BF16)
        k_ref[...] = _nt(h, w_ref[COL_K:COL_V, :])
        v_ref[...] = _nt(h, w_ref[COL_V:COL_AG, :])
        ag_ref[...] = _nt(h, w_ref[COL_AG:D_IN, :])

        current = _band_is_current()
        for sb in range(tm // BLOCK):
            n = (tm // BLOCK) * i + sb
            rows = slice(BLOCK * sb, BLOCK * (sb + 1))
            before = slice(BLOCK * (sb - 1), BLOCK * sb)
            kx = jnp.concatenate([kprev[...] if sb == 0 else k_ref[before, :], k_ref[rows, :]], axis=0)
            vx = jnp.concatenate([vprev[...] if sb == 0 else v_ref[before, :], v_ref[rows, :]], axis=0)
            variant = jnp.minimum(n, 1) if sb == 0 else 1
            for kv in range(2):
                cs = slice(256 * kv, 256 * (kv + 1))
                p, _ = _probs_keys_major(_replicate_head(kx, kv), _stack_heads(q_ref[rows, cs]),
                                         bias_ref[variant, kv], _sink_row(sink_ref, layer, kv), current)
                o = _unstack_heads(_tn(_unpack_band(p.astype(BF16), current), _replicate_head(vx, kv)))
                a_ref[rows, cs] = o
                gate, _ = _silu_parts(ag_ref[rows, cs])
                z_ref[rows, D_POOL + 256 * kv:D_POOL + 256 * (kv + 1)] = (o * gate).astype(BF16)

        tail = slice(tm - BLOCK, tm)
        uprev[...] = u_ref[tail, :]
        kprev[...] = k_ref[tail, :]
        vprev[...] = v_ref[tail, :]

        if out is not None:
            y = _nn(z_ref[...], wo_ref[...])
            y_ref[...] = y
            r = lax.rsqrt(jnp.mean(y * y, axis=-1, keepdims=True) + EPS)
            xn_ref[...] = x_ref[...] + y * r * gpost_ref[layer:layer + 1, :]

    row = lambda c: pl.BlockSpec((tm, c), lambda i: (i, 0))
    const = lambda shape: pl.BlockSpec(shape, lambda i: (0,) * len(shape))
    act = jax.ShapeDtypeStruct((SEQ, D_MODEL), F32)
    fused = out is not None
    return pl.pallas_call(
        body, name=f"fwd_front{layer}", grid=(SEQ // tm,),
        in_specs=[pl.BlockSpec(memory_space=pltpu.SMEM), row(D_MODEL), const((DEPTH, D_MODEL)),
                  _resident((D_IN, D_MODEL)), const((8, 128)),
                  pl.BlockSpec((None, 4, BLOCK, BLOCK), lambda i: (layer, 0, 0, 0)), const((DEPTH, D_POOL)),
                  _resident((2, 2, BLOCK, GQA * BLOCK))]
                 + ([const((DEPTH, D_MODEL)), _resident((D_MODEL, D_MODEL))] if fused else []),
        out_specs=[row(D_POOL), row(D_POOL), row(D_ATTN), row(D_KV), row(D_KV), row(D_ATTN), row(D_MODEL),
                   row(D_ATTN)] + ([row(D_MODEL)] * 2 if fused else []),
        out_shape=[jax.ShapeDtypeStruct((SEQ, D_POOL), F32), jax.ShapeDtypeStruct((SEQ, D_POOL), F32),
                   jax.ShapeDtypeStruct((SEQ, D_ATTN), BF16), jax.ShapeDtypeStruct((SEQ, D_KV), F32),
                   jax.ShapeDtypeStruct((SEQ, D_KV), F32), jax.ShapeDtypeStruct((SEQ, D_ATTN), F32),
                   jax.ShapeDtypeStruct((SEQ, D_MODEL), BF16), jax.ShapeDtypeStruct((SEQ, D_ATTN), F32)]
                  + ([act] * 2 if fused else []),
        scratch_shapes=[pltpu.VMEM((BLOCK, D_POOL), F32), pltpu.VMEM((BLOCK, D_KV), F32),
                        pltpu.VMEM((BLOCK, D_KV), F32)],
        compiler_params=_compiler_params(("arbitrary",)),
    )(sinks, x, norm_pre, w_in_t, token, pool_w, pool_scale, bias, *(out if fused else ()))


def _fwd_out(layer, z, x, norm_post, w_out, token):
    tm = FWD_OUT_TILE

    def body(z_ref, x_ref, g_ref, w_ref, _, xn_ref, y_ref):
        y = _nn(z_ref[...], w_ref[...])
        y_ref[...] = y
        r = lax.rsqrt(jnp.mean(y * y, axis=-1, keepdims=True) + EPS)
        xn_ref[...] = x_ref[...] + y * r * g_ref[layer:layer + 1, :]

    row = lambda c: pl.BlockSpec((tm, c), lambda i: (i, 0))
    return pl.pallas_call(
        body, name=f"fwd_out{layer}", grid=(SEQ // tm,),
        in_specs=[row(D_MODEL), row(D_MODEL), pl.BlockSpec((DEPTH, D_MODEL), lambda i: (0, 0)),
                  _resident((D_MODEL, D_MODEL)), pl.BlockSpec((8, 128), lambda i: (0, 0))],
        out_specs=[row(D_MODEL), row(D_MODEL)],
        out_shape=[jax.ShapeDtypeStruct((SEQ, D_MODEL), F32), jax.ShapeDtypeStruct((SEQ, D_MODEL), F32)],
        compiler_params=_compiler_params(("arbitrary",)),
    )(z, x, norm_post, w_out, token)


BACK_TILE = 2 * BLOCK


def _bwd_back(layer, top, dxo_or_xf, target_or_token, y, z, norm_post, w_out, sinks, u, pg, q, k, v, ag, a,
              pool_w, pool_scale, bias):
    tm = BACK_TILE
    steps = SEQ // tm
    last = steps - 1
    per = tm // BLOCK

    def body(*refs):
        refs = list(refs)
        sink_ref, first, second = refs[:3]
        (y_ref, z_ref, g_ref, w_ref, u_ref, up_ref, pg_ref, q_ref, k_ref, v_ref, ag_ref, a_ref, pw_ref, sc_ref,
         bias_ref) = refs[3:18]
        del refs[:18]
        dxo_ref = refs.pop(0) if top else None
        dp_ref, dw_ref, pack_ref, acc, dg, lacc, dzs, ck, cv, ce = refs
        i = pl.program_id(0)
        blk = last - i

        @pl.when(i == 0)
        def _():
            acc[...] = jnp.zeros_like(acc)
            dg[...] = jnp.zeros_like(dg)
            lacc[...] = jnp.zeros_like(lacc)
            pack_ref[...] = jnp.zeros_like(pack_ref)
            ck[...] = jnp.zeros_like(ck)
            cv[...] = jnp.zeros_like(cv)
            ce[...] = jnp.zeros_like(ce)

        if top:
            d = first[...] - second[...]
            dxo_v = d * (1.0 / D_MODEL)
            dxo_ref[...] = dxo_v
            part = jnp.sum(d * d, axis=-1, keepdims=True) * (1.0 / D_MODEL)
            lacc[...] += 0.5 * jnp.sum(part, axis=0, keepdims=True)
        else:
            dxo_v = first[...]
        yv = y_ref[...]
        r = lax.rsqrt(jnp.mean(yv * yv, axis=-1, keepdims=True) + EPS)
        yn = yv * r
        dg[...] += jnp.sum(dxo_v * yn, axis=0, keepdims=True)
        dyn = dxo_v * g_ref[layer:layer + 1, :]
        dy = (r * (dyn - yn * jnp.mean(dyn * yn, axis=-1, keepdims=True))).astype(BF16)
        dzs[...] = _nt(dy, w_ref[...])
        acc[...] += _tn(z_ref[...], dy)

        lane = lax.broadcasted_iota(jnp.int32, (1, 128), 1)
        lane2 = lax.broadcasted_iota(jnp.int32, (256, 128), 1)
        current = _band_is_current()
        for sb in reversed(range(per)):
            n = per * blk + sb
            rows = slice(BLOCK * sb, BLOCK * (sb + 1))

            uv = u_ref[rows, :]
            if sb == 0:
                halo = up_ref[BLOCK - WINDOW_HALO:, :] * (n > 0).astype(F32)
            else:
                halo = u_ref[BLOCK * sb - WINDOW_HALO:BLOCK * sb, :]
            ext = jnp.concatenate([halo, uv], axis=0)
            for g, w in enumerate(POOL_WINDOWS):
                cs = slice(BLOCK * g, BLOCK * (g + 1))
                inv = _inv_count(n, w)
                win = _window_sum(ext[:, cs], w, forward=False)[WINDOW_HALO:]
                pooled = win * inv - uv[:, cs]
                pw_g = pw_ref[g].astype(BF16)
                mixed = _nn(pooled.astype(BF16), pw_g)
                gate, dgate = _silu_parts(pg_ref[rows, cs])
                dzp = dzs[rows, cs]
                sc = sc_ref[layer:layer + 1, cs]
                dpm = dzp * gate
                dp_ref[rows, COL_PG + BLOCK * g:COL_PG + BLOCK * (g + 1)] = (dzp * (mixed * sc) * dgate).astype(BF16)
                pack_ref[ROW_SC + g:ROW_SC + g + 1, :] += jnp.sum(dpm * mixed, axis=0, keepdims=True)
                dmixed = (dpm * sc).astype(BF16)
                pack_ref[ROW_PW + BLOCK * g:ROW_PW + BLOCK * (g + 1), :] += _tn(pooled.astype(BF16), dmixed)
                dpooled = _nt(dmixed, pw_g)
                e = dpooled * inv
                lead = _window_sum(jnp.concatenate([e, ce[:WINDOW_HALO, cs]], axis=0), w, forward=True)[:BLOCK]
                dp_ref[rows, COL_U + BLOCK * g:COL_U + BLOCK * (g + 1)] = (lead - dpooled).astype(BF16)
                ce[:, cs] = e

            kx = _kv_ext(k_ref, n)
            vx = _kv_ext(v_ref, n)
            variant = jnp.minimum(n, 1) if sb == 0 else 1
            dsink_row = jnp.zeros((1, 128), F32)
            tks, tvs = [], []
            for kv in range(2):
                cs = slice(256 * kv, 256 * (kv + 1))
                k_rep = _replicate_head(kx, kv)
                v_rep = _replicate_head(vx, kv)
                q_st = _stack_heads(q_ref[rows, cs])
                gate, dgate = _silu_parts(ag_ref[rows, cs])
                dza = dzs[rows, D_POOL + 256 * kv:D_POOL + 256 * (kv + 1)]
                dp_ref[rows, COL_AG + 256 * kv:COL_AG + 256 * (kv + 1)] = (dza * a_ref[rows, cs] * dgate).astype(BF16)
                da_st = _stack_heads((dza * gate).astype(BF16))
                p, psink = _probs_keys_major(k_rep, q_st, bias_ref[variant, kv], _sink_row(sink_ref, layer, kv),
                                             current)
                dpt = _pack_band(_nt(v_rep, da_st), current)
                delta = jnp.sum(p * dpt, axis=0, keepdims=True)
                dst = _unpack_band((p * (dpt - delta) * SCALE).astype(BF16), current)
                sink_terms = psink * delta
                for g in range(GQA):
                    dsink = -jnp.sum(sink_terms[:, BLOCK * g:BLOCK * (g + 1)], axis=1, keepdims=True)
                    dsink_row = dsink_row + jnp.where(lane == kv * GQA + g, dsink, 0.0)
                dp_ref[rows, COL_Q + 256 * kv:COL_Q + 256 * (kv + 1)] = _unstack_heads(_tn(dst, k_rep)).astype(BF16)
                tks.append(_fold_heads(_nn(dst, q_st)))
                tvs.append(_fold_heads(_nn(_unpack_band(p.astype(BF16), current), da_st)))
            pack_ref[ROW_SINK:ROW_SINK + 1, :] += dsink_row
            dkx = jnp.where(lane2 < 64, tks[0], tks[1])
            dvx = jnp.where(lane2 < 64, tvs[0], tvs[1])
            dp_ref[rows, COL_K:COL_V] = (ck[...] + dkx[BLOCK:]).astype(BF16)
            dp_ref[rows, COL_V:COL_AG] = (cv[...] + dvx[BLOCK:]).astype(BF16)
            ck[...] = dkx[:BLOCK]
            cv[...] = dvx[:BLOCK]

        @pl.when(i == steps - 1)
        def _():
            dw_ref[...] = acc[...].astype(BF16)
            _rows_of(dg, pack_ref, ROW_NPOST)
            pack_ref[ROW_LOSS:ROW_LOSS + 1, :] = jnp.where(lane == 0, lacc[...], 0.0)

    row = lambda c: pl.BlockSpec((tm, c), lambda i: (last - i, 0))
    const = lambda shape: pl.BlockSpec(shape, lambda i: (0,) * len(shape))
    act = jax.ShapeDtypeStruct((SEQ, D_MODEL), F32)
    return pl.pallas_call(
        body, name=f"bwd_back{layer}", grid=(steps,),
        in_specs=[pl.BlockSpec(memory_space=pltpu.SMEM), row(D_MODEL), row(D_MODEL) if top else const((8, 128)),
                  row(D_MODEL), row(D_MODEL), const((DEPTH, D_MODEL)), _resident((D_MODEL, D_MODEL)),
                  row(D_POOL), pl.BlockSpec((BLOCK, D_POOL), lambda i: (jnp.maximum(per * (last - i) - 1, 0), 0)),
                  row(D_POOL), row(D_ATTN), _resident((SEQ, D_KV)), _resident((SEQ, D_KV)), row(D_ATTN), row(D_ATTN),
                  pl.BlockSpec((None, 4, BLOCK, BLOCK), lambda i: (layer, 0, 0, 0)), const((DEPTH, D_POOL)),
                  _resident((2, 2, BLOCK, GQA * BLOCK))],
        out_specs=([row(D_MODEL)] * (1 if top else 0)
                   + [row(D_IN), const((D_MODEL, D_MODEL)), const((PACK_ROWS, 128))]),
        out_shape=([act] * (1 if top else 0)
                   + [jax.ShapeDtypeStruct((SEQ, D_IN), BF16), jax.ShapeDtypeStruct((D_MODEL, D_MODEL), BF16),
                      jax.ShapeDtypeStruct((PACK_ROWS, 128), F32)]),
        scratch_shapes=[pltpu.VMEM((D_MODEL, D_MODEL), F32), pltpu.VMEM((1, D_MODEL), F32), pltpu.VMEM((1, 1), F32),
                        pltpu.VMEM((tm, D_MODEL), F32), pltpu.VMEM((BLOCK, D_KV), F32), pltpu.VMEM((BLOCK, D_KV), F32),
                        pltpu.VMEM((BLOCK, D_POOL), F32)],
        compiler_params=_compiler_params(("arbitrary",)),
    )(sinks, dxo_or_xf, target_or_token, y, z, norm_post, w_out, u, u, pg, q, k, v, ag, a, pool_w, pool_scale, bias)


def _bwd_in(layer, part, token, dproj, x, norm_pre, dxo=None, w_in_t=None):
    pair = part in ("dw_pair", "both_pair")
    want_dw, want_dx = part != "dx", part in ("both", "dx", "both_pair")
    tm = TOKEN_TILE
    steps = SEQ // tm
    cw = 256

    def body(*refs):
        refs = list(refs)
        dp_ref, x_ref, g_ref = refs[1:4]
        del refs[:4]
        if want_dx:
            dxo_ref, w_ref, dx_ref, dgo_ref = refs[:4]
            del refs[:4]
            dg = refs.pop()
        if pair:
            hs_ref, hm_ref, acc, mine_buf, theirs_buf, send_sem, recv_sem = refs
        elif want_dw:
            dw_ref, acc = refs
        i = pl.program_id(0)

        @pl.when(i == 0)
        def _():
            if pair:
                _handshake([(lax.axis_index("x"), lax.axis_index("y"), 1 - lax.axis_index("c"))])
            if want_dw:
                acc[...] = jnp.zeros_like(acc)
            if want_dx:
                dg[...] = jnp.zeros_like(dg)

        xv = x_ref[...]
        gv = g_ref[layer:layer + 1, :]
        r = lax.rsqrt(jnp.mean(xv * xv, axis=-1, keepdims=True) + EPS)
        xn = xv * r
        if want_dw:
            hb = (xn * gv).astype(BF16)
            for c in range(0, D_IN, cw):
                acc[c:c + cw, :] += _tn(dp_ref[:, c:c + cw], hb)
        def rows_for(q, core):
            return pl.ds(pl.multiple_of((2 * q + core) * IN_SHARD, 8), IN_SHARD)

        def swap(q):
            x, y, c = _mesh_pos()
            return pltpu.make_async_remote_copy(
                src_ref=mine_buf.at[q], dst_ref=theirs_buf.at[q], send_sem=send_sem.at[q], recv_sem=recv_sem.at[q],
                device_id=(x, y, 1 - c), device_id_type=MESH)

        if pair:
            @pl.when(i == steps - 1)
            def _():
                for q in range(4):
                    mine_buf[q] = acc[rows_for(q, 1 - lax.axis_index("c")), :].astype(BF16)
                    swap(q).start()

        if want_dx:
            dh = _nn(dp_ref[...], w_ref[...])
            dg[...] += jnp.sum(dh * xn, axis=0, keepdims=True)
            dhn = dh * gv
            dx_ref[...] = dxo_ref[...] + r * (dhn - xn * jnp.mean(dhn * xn, axis=-1, keepdims=True))

        @pl.when(i == steps - 1)
        def _():
            if pair:
                x, y, c = _mesh_pos()
                for q in range(4):
                    swap(q).wait()
                for j, q in enumerate([2 * (1 - x) + y, 2 * x + (1 - y), 2 * (1 - x) + (1 - y)]):
                    hs_ref[j] = (acc[rows_for(q, c), :] + theirs_buf[q].astype(F32)).astype(BF16)
                hm_ref[...] = acc[rows_for(2 * x + y, c), :] + theirs_buf[2 * x + y].astype(F32)
            elif want_dw:
                dw_ref[...] = acc[...].astype(BF16)
            if want_dx:
                _rows_of(dg, dgo_ref, 0)

    row = lambda c: pl.BlockSpec((tm, c), lambda i: (i, 0))
    const = lambda shape: pl.BlockSpec(shape, lambda i: (0,) * len(shape))
    in_specs = [const((8, 128)), row(D_IN), row(D_MODEL), const((DEPTH, D_MODEL))]
    operands = [token, dproj, x, norm_pre]
    out_specs, out_shape, scratch = [], [], []
    if want_dx:
        in_specs += [row(D_MODEL), _resident((D_IN, D_MODEL))]
        operands += [dxo, w_in_t]
        out_specs += [row(D_MODEL), const((8, 128))]
        out_shape += [jax.ShapeDtypeStruct((SEQ, D_MODEL), F32), jax.ShapeDtypeStruct((8, 128), F32)]
    if pair:
        out_specs += [const((3, IN_SHARD, D_MODEL)), const((IN_SHARD, D_MODEL))]
        out_shape += [jax.ShapeDtypeStruct((3, IN_SHARD, D_MODEL), BF16), jax.ShapeDtypeStruct((IN_SHARD, D_MODEL), F32)]
        scratch += [pltpu.VMEM((D_IN, D_MODEL), F32), pltpu.VMEM((4, IN_SHARD, D_MODEL), BF16),
                    pltpu.VMEM((4, IN_SHARD, D_MODEL), BF16), pltpu.SemaphoreType.DMA((4,)), pltpu.SemaphoreType.DMA((4,))]
    elif want_dw:
        out_specs.append(const((D_IN, D_MODEL)))
        out_shape.append(jax.ShapeDtypeStruct((D_IN, D_MODEL), BF16))
        scratch.append(pltpu.VMEM((D_IN, D_MODEL), F32))
    if want_dx:
        scratch.append(pltpu.VMEM((1, D_MODEL), F32))
    params = pltpu.CompilerParams(dimension_semantics=("arbitrary",), vmem_limit_bytes=VMEM_LIMIT,
                                  collective_id=COLLECTIVE_PAIR_SUM[layer] if pair else None)
    return pl.pallas_call(
        body, name=f"bwd_in_{part}{layer}", grid=(steps,),
        in_specs=in_specs, out_specs=out_specs, out_shape=out_shape, scratch_shapes=scratch,
        compiler_params=params,
    )(*operands)


def _mesh_pos():
    return lax.axis_index("x"), lax.axis_index("y"), lax.axis_index("c")


def _device_rows(ref, m, px, py, pc):
    return ref.at[pl.ds(pl.multiple_of((4 * px + 2 * py + pc) * m, 16 if m % 16 == 0 else 8), m), :]


def _allgather(srcs, out_dtype, name, later=()):
    na, nb = len(srcs), len(later)
    every = list(srcs) + list(later)
    shapes = [(a.shape[-2], a.shape[-1]) for a, _ in every]

    def body(*refs):
        xs, refs = refs[:na + nb], refs[na + nb:]
        outs, cast, land, refs = refs[:na], refs[na:na + nb], refs[na + nb:na + 2 * nb], refs[na + 2 * nb:]
        stage, raw, (send_sems, recv_sems, local_sems, load_sems) = refs[:na], refs[na:2 * na + nb], refs[2 * na + nb:]
        loads = [pltpu.make_async_copy(xs[i].at[every[i][1]], raw[i], load_sems.at[i]) for i in range(na + nb)]
        for cp in loads:
            cp.start()
        x, y, c = _mesh_pos()
        me, sibling = (x, y, c), (x, y, 1 - c)
        near = [(1 - x, y), (x, 1 - y)]
        far = (1 - x, 1 - y)
        relay_from, relay_to = (x ^ (1 - c), y ^ c), (x ^ c, y ^ (1 - c))
        _handshake([sibling] + [(*chip, c) for chip in near])
        k_from, k_to = 1 + c, 2 - c

        def slot(a, px, py, pc):
            return _device_rows(outs[a], shapes[a][0], px, py, pc)

        def copy(a, k, block, to, src=None):
            return pltpu.make_async_remote_copy(
                src_ref=slot(a, *block) if src is None else src, dst_ref=slot(a, *block),
                send_sem=send_sems.at[a, k], recv_sem=recv_sems.at[a, k], device_id=to, device_id_type=MESH)

        def cast_block(i):
            loads[i].wait()
            return raw[i][...].astype(out_dtype)

        for a in range(na):
            stage[a][...] = cast_block(a)
        mine = [pltpu.make_async_copy(stage[a], slot(a, *me), local_sems.at[a]) for a in range(na)]
        for cp in mine:
            cp.start()
        sent = []
        for a in range(na):
            sent.append(copy(a, 0, me, sibling, src=stage[a]))
            sent += [copy(a, 1 + j, me, (*chip, c), src=stage[a]) for j, chip in enumerate(near)]
        for cp in sent:
            cp.start()
        for b in range(nb):
            cast[b][...] = cast_block(na + b)
            cp = pltpu.make_async_copy(cast[b], _device_rows(land[b], shapes[na + b][0], *me), local_sems.at[na + b])
            cp.start()
            mine.append(cp)
        for a in range(na):
            copy(a, k_from, (*relay_from, c), me).wait_recv()
            sent += [copy(a, 3, (*relay_from, c), (*relay_to, c)), copy(a, 3 + k_from, (*relay_from, c), sibling)]
            sent[-2].start()
            sent[-1].start()
        for a in range(na):
            copy(a, k_to, (*relay_to, c), me).wait_recv()
            sent.append(copy(a, 3 + k_to, (*relay_to, c), sibling))
            sent[-1].start()
        for a in range(na):
            copy(a, 3, (*far, c), me).wait_recv()
            sent.append(copy(a, 6, (*far, c), sibling))
            sent[-1].start()
        for a in range(na):
            copy(a, 0, sibling, me).wait_recv()
            for j, chip in enumerate(near + [far]):
                copy(a, 4 + j, (*chip, 1 - c), me).wait_recv()
        for cp in sent:
            cp.wait_send()
        for cp in mine:
            cp.wait()

    vmem = pl.BlockSpec(memory_space=pltpu.VMEM)
    hbm = pl.BlockSpec(memory_space=pl.ANY)
    gathered = [jax.ShapeDtypeStruct((N_DEV * m, n), out_dtype) for m, n in shapes]
    out = pl.pallas_call(
        body, name=name,
        in_specs=[hbm] * (na + nb),
        out_specs=[hbm] * na + [vmem] * nb + [hbm] * nb,
        out_shape=gathered[:na] + [jax.ShapeDtypeStruct(s, out_dtype) for s in shapes[na:]] + gathered[na:],
        scratch_shapes=([pltpu.VMEM(s, out_dtype) for s in shapes[:na]]
                        + [pltpu.VMEM(s, a.dtype) for s, (a, _) in zip(shapes, every)]
                        + [pltpu.SemaphoreType.DMA((na, 7)), pltpu.SemaphoreType.DMA((na, 7)),
                           pltpu.SemaphoreType.DMA((na + nb,)), pltpu.SemaphoreType.DMA((na + nb,))]),
        compiler_params=pltpu.CompilerParams(vmem_limit_bytes=VMEM_LIMIT, collective_id=COLLECTIVE_GATHER_W0),
    )(*[a for a, _ in every])
    return out[:na], out[na:na + nb], out[na + nb:]


ALL_PEERS = tuple(range(1, N_DEV))
SIBLING_AND_SAME_CORE = (1, 2, 4, 6)


def _related(k, x, y, c):
    return x ^ ((k >> 2) & 1), y ^ ((k >> 1) & 1), c ^ (k & 1)


def _gather_start(blocks, lands, relations, collective_id, name):
    na = len(blocks)

    def body(*refs):
        src, land, sems, token = refs[:na], refs[na:2 * na], refs[2 * na:4 * na], refs[-1]
        x, y, c = _mesh_pos()
        _handshake([_related(k, x, y, c) for k in sorted(set().union(*relations))])
        for a in range(na):
            for k in relations[a]:
                pltpu.make_async_remote_copy(
                    src_ref=src[a], dst_ref=_device_rows(land[a], blocks[a].shape[0], x, y, c),
                    send_sem=sems[2 * a].at[k - 1], recv_sem=sems[2 * a + 1].at[k - 1],
                    device_id=_related(k, x, y, c), device_id_type=MESH).start()
        token[...] = jnp.zeros_like(token)

    bufs = [pltpu.HBM(t.shape, t.dtype) for t in list(blocks) + list(lands)]
    out = pl.pallas_call(
        body, name=name,
        out_shape=(*([pltpu.SemaphoreType.DMA((N_DEV - 1,))] * (2 * na)), *bufs, jax.ShapeDtypeStruct((8, 128), F32)),
        in_specs=[_HBM] * (2 * na),
        out_specs=(*([_SEM] * (2 * na)), *([_HBM] * (2 * na)), pl.BlockSpec(memory_space=pltpu.VMEM)),
        input_output_aliases={i: 2 * na + i for i in range(2 * na)},
        compiler_params=pltpu.CompilerParams(has_side_effects=_EFFECT, collective_id=collective_id),
    )(*[pltpu.with_memory_space_constraint(t, pltpu.HBM) for t in list(blocks) + list(lands)])
    sems = [(out[2 * a], out[2 * a + 1]) for a in range(na)]
    return sems, out[2 * na:3 * na], out[3 * na:4 * na], out[-1]


def _gather_wait(sems, block, land, relations, after, name):
    def body(src, land_ref, send_sem, recv_sem, after_ref, src_out, land_out):
        x, y, c = _mesh_pos()
        for k in relations:
            peer = _related(k, x, y, c)
            cp = pltpu.make_async_remote_copy(
                src_ref=src, dst_ref=_device_rows(land_ref, block.shape[0], *peer),
                send_sem=send_sem.at[k - 1], recv_sem=recv_sem.at[k - 1], device_id=peer, device_id_type=MESH)
            cp.wait_send()
            cp.wait_recv()

    out = pl.pallas_call(
        body, name=name,
        out_shape=(pltpu.HBM(block.shape, block.dtype), pltpu.HBM(land.shape, land.dtype)),
        in_specs=[_HBM, _HBM, _SEM, _SEM, pl.BlockSpec(memory_space=pl.ANY)],
        out_specs=[_HBM, _HBM],
        input_output_aliases={0: 0, 1: 1},
        compiler_params=pltpu.CompilerParams(has_side_effects=_EFFECT),
    )(block, land, sems[0], sems[1], after)
    return out[1]


(COLLECTIVE_GATHER_W0, COLLECTIVE_GATHER_W1, COLLECTIVE_FORWARD_W_IN1, COLLECTIVE_EXCHANGE_1, COLLECTIVE_EXCHANGE_0A,
 COLLECTIVE_EXCHANGE_0B, COLLECTIVE_GATHER_SMALL) = range(1, 8)
COLLECTIVE_PAIR_SUM = (8, 9)
COLLECTIVE_FORWARD_W_OUT1 = 10


def _handshake(peers):
    barrier = pltpu.get_barrier_semaphore()
    for peer in peers:
        pl.semaphore_signal(barrier, inc=1, device_id=peer, device_id_type=MESH)
    pl.semaphore_wait(barrier, len(peers))


def _forward_plan(land_ref, m):
    x, y, c = _mesh_pos()
    return [_device_rows(land_ref, m, qx, qy, c) for qx, qy in ((1 - x, y), (x, 1 - y), (1 - x, 1 - y))], (x, y, 1 - c)


def _forward_start(land, m, collective_id, name):
    def body(land_ref, send_sem, recv_sem, land_out, token):
        _handshake([(lax.axis_index("x"), lax.axis_index("y"), 1 - lax.axis_index("c"))])
        rows, sibling = _forward_plan(land_ref, m)
        for j, r in enumerate(rows):
            pltpu.make_async_remote_copy(src_ref=r, dst_ref=r, send_sem=send_sem.at[j], recv_sem=recv_sem.at[j],
                                         device_id=sibling, device_id_type=MESH).start()
        token[...] = jnp.zeros_like(token)

    out = pl.pallas_call(
        body, name=name,
        out_shape=(pltpu.SemaphoreType.DMA((3,)), pltpu.SemaphoreType.DMA((3,)), pltpu.HBM(land.shape, land.dtype),
                   jax.ShapeDtypeStruct((8, 128), F32)),
        in_specs=[_HBM],
        out_specs=(_SEM, _SEM, _HBM, pl.BlockSpec(memory_space=pltpu.VMEM)),
        input_output_aliases={0: 2},
        compiler_params=pltpu.CompilerParams(has_side_effects=_EFFECT, collective_id=collective_id),
    )(pltpu.with_memory_space_constraint(land, pltpu.HBM))
    return (out[0], out[1]), out[2], out[3]


def _forward_wait(sems, land, m, after, name):
    def body(land_ref, send_sem, recv_sem, after_ref, land_out):
        x, y, c = _mesh_pos()
        mine, sibling = _forward_plan(land_ref, m)
        theirs = [_device_rows(land_ref, m, qx, qy, 1 - c) for qx, qy in ((1 - x, y), (x, 1 - y), (1 - x, 1 - y))]
        for j in range(3):
            cp = pltpu.make_async_remote_copy(src_ref=mine[j], dst_ref=theirs[j], send_sem=send_sem.at[j],
                                              recv_sem=recv_sem.at[j], device_id=sibling, device_id_type=MESH)
            cp.wait_send()
            cp.wait_recv()

    return pl.pallas_call(
        body, name=name,
        out_shape=pltpu.HBM(land.shape, land.dtype),
        in_specs=[_HBM, _SEM, _SEM, pl.BlockSpec(memory_space=pl.ANY)],
        out_specs=_HBM,
        input_output_aliases={0: 0},
        compiler_params=pltpu.CompilerParams(has_side_effects=_EFFECT),
    )(land, sems[0], sems[1], after)


_HBM = pl.BlockSpec(memory_space=pltpu.HBM)
_SEM = pl.BlockSpec(memory_space=pltpu.SEMAPHORE)
_EFFECT = pltpu.SideEffectType.DATAFLOW_SIDE_EFFECTING


def _exchange_plan(direct):
    x, y, c = _mesh_pos()
    if not direct:
        return [(j, j, (qx, qy, c)) for j, (qx, qy) in enumerate([(1 - x, y), (x, 1 - y), (1 - x, 1 - y)])]
    plan = []
    for k in range(1, N_DEV):
        px, py, pc = x ^ ((k >> 2) & 1), y ^ ((k >> 1) & 1), c ^ (k & 1)
        plan.append((4 * px + 2 * py + pc, k - 1, (px, py, pc)))
    return plan


def _exchange_copies(directs):
    copies, base = [], 0
    for a, direct in enumerate(directs):
        plan = _exchange_plan(direct)
        copies += [(a, block, slot, peer, base + slot) for block, slot, peer in plan]
        base += len(plan)
    return copies, base


def _exchange_start(srcs, directs, collective_id, name):
    na = len(srcs)
    slots = [N_DEV - 1 if direct else 3 for direct in directs]

    def body(*refs):
        src, land = refs[:na], refs[na:2 * na]
        send_sem, recv_sem = refs[2 * na], refs[2 * na + 1]
        token = refs[-1]
        _handshake([peer for _, _, peer in _exchange_plan(any(directs))])
        for a, block, slot, peer, sem in _exchange_copies(directs)[0]:
            pltpu.make_async_remote_copy(
                src_ref=src[a].at[block], dst_ref=land[a].at[slot], send_sem=send_sem.at[sem],
                recv_sem=recv_sem.at[sem], device_id=peer, device_id_type=MESH).start()
        token[...] = jnp.zeros_like(token)

    zones = [jax.ShapeDtypeStruct((n,) + t.shape[1:], t.dtype) for n, t in zip(slots, srcs)]
    bufs = [pltpu.HBM(t.shape, t.dtype) for t in list(srcs) + zones]
    out = pl.pallas_call(
        body, name=name,
        out_shape=(pltpu.SemaphoreType.DMA((sum(slots),)), pltpu.SemaphoreType.DMA((sum(slots),)), *bufs,
                   jax.ShapeDtypeStruct((8, 128), F32)),
        in_specs=[_HBM] * (2 * na),
        out_specs=(_SEM, _SEM, *([_HBM] * (2 * na)), pl.BlockSpec(memory_space=pltpu.VMEM)),
        input_output_aliases={i: 2 + i for i in range(2 * na)},
        compiler_params=pltpu.CompilerParams(has_side_effects=_EFFECT, collective_id=collective_id),
    )(*[pltpu.with_memory_space_constraint(t, pltpu.HBM) for t in srcs],
      *[pltpu.with_memory_space_constraint(lax.empty(t.shape, t.dtype), pltpu.HBM) for t in zones])
    return out[0], out[1], out[2:2 + na], out[2 + na:2 + 2 * na], out[-1]


def _exchange_wait(send_sem, recv_sem, srcs, lands, directs, after, name):
    na = len(srcs)

    def body(*refs):
        src, land = refs[:na], refs[na:2 * na]
        send_sem_ref, recv_sem_ref = refs[2 * na], refs[2 * na + 1]
        for a, block, slot, peer, sem in _exchange_copies(directs)[0]:
            cp = pltpu.make_async_remote_copy(
                src_ref=src[a].at[block], dst_ref=land[a].at[slot], send_sem=send_sem_ref.at[sem],
                recv_sem=recv_sem_ref.at[sem], device_id=peer, device_id_type=MESH)
            cp.wait_send()
            cp.wait_recv()

    bufs = [pltpu.HBM(t.shape, t.dtype) for t in list(srcs) + list(lands)]
    out = pl.pallas_call(
        body, name=name,
        out_shape=tuple(bufs),
        in_specs=[_HBM] * (2 * na) + [_SEM, _SEM, pl.BlockSpec(memory_space=pl.ANY)],
        out_specs=[_HBM] * (2 * na),
        input_output_aliases={i: i for i in range(2 * na)},
        compiler_params=pltpu.CompilerParams(has_side_effects=_EFFECT),
    )(*srcs, *lands, send_sem, recv_sem, after)
    return out[:na], out[na:]


def _own_then_slots(mine_ref, lands_ref, rows=slice(None)):
    if len(mine_ref.shape) == 3:
        x, y, c = _mesh_pos()
        total = mine_ref[4 * x + 2 * y + c, rows, :].astype(F32)
    else:
        total = mine_ref[rows, :].astype(F32)
    for j in range(lands_ref.shape[0]):
        total = total + lands_ref[j, rows, :].astype(F32)
    return total


SMALL_ROWS = 2 * PACK_SLICE + 2 * 8


def _small_gather_start(mine, lands, dgpre, name):
    def body(*refs):
        hm, ld, dg = refs[:DEPTH], refs[DEPTH:2 * DEPTH], refs[2 * DEPTH:3 * DEPTH]
        send_sem, recv_sem, blk, land, token, own, slots, rows, built, local_sems = refs[3 * DEPTH:]
        x, y, c = _mesh_pos()
        loads = []
        for l in range(DEPTH):
            loads += [pltpu.make_async_copy(hm[l].at[4 * x + 2 * y + c], own.at[l], local_sems.at[3 * l]),
                      pltpu.make_async_copy(ld[l], slots.at[l], local_sems.at[3 * l + 1]),
                      pltpu.make_async_copy(dg[l], rows.at[l], local_sems.at[3 * l + 2])]
        for cp in loads:
            cp.start()
        _handshake([_related(k, x, y, c) for k in ALL_PEERS])
        for cp in loads:
            cp.wait()
        for l in range(DEPTH):
            total = own[l]
            for j in range(N_DEV - 1):
                total = total + slots[l, j]
            built[PACK_SLICE * l:PACK_SLICE * (l + 1), :] = total
            built[2 * PACK_SLICE + 8 * l:2 * PACK_SLICE + 8 * (l + 1), :] = rows[l]
        stores = [pltpu.make_async_copy(built, blk, local_sems.at[3 * DEPTH]),
                  pltpu.make_async_copy(built, _device_rows(land, SMALL_ROWS, x, y, c), local_sems.at[3 * DEPTH + 1])]
        for cp in stores:
            cp.start()
        for cp in stores:
            cp.wait()
        for k in ALL_PEERS:
            pltpu.make_async_remote_copy(
                src_ref=blk, dst_ref=_device_rows(land, SMALL_ROWS, x, y, c), send_sem=send_sem.at[k - 1],
                recv_sem=recv_sem.at[k - 1], device_id=_related(k, x, y, c), device_id_type=MESH).start()
        token[...] = jnp.zeros_like(token)

    hbm = pl.BlockSpec(memory_space=pl.ANY)
    out = pl.pallas_call(
        body, name=name,
        in_specs=[hbm] * (3 * DEPTH),
        out_specs=(_SEM, _SEM, _HBM, _HBM, pl.BlockSpec(memory_space=pltpu.VMEM)),
        out_shape=(pltpu.SemaphoreType.DMA((N_DEV - 1,)), pltpu.SemaphoreType.DMA((N_DEV - 1,)),
                   pltpu.HBM((SMALL_ROWS, 128), F32), pltpu.HBM((N_DEV * SMALL_ROWS, 128), F32),
                   jax.ShapeDtypeStruct((8, 128), F32)),
        scratch_shapes=[pltpu.VMEM((DEPTH, PACK_SLICE, 128), F32), pltpu.VMEM((DEPTH, N_DEV - 1, PACK_SLICE, 128), F32),
                        pltpu.VMEM((DEPTH, 8, 128), F32), pltpu.VMEM((SMALL_ROWS, 128), F32),
                        pltpu.SemaphoreType.DMA((3 * DEPTH + 2,))],
        compiler_params=pltpu.CompilerParams(has_side_effects=_EFFECT, collective_id=COLLECTIVE_GATHER_SMALL),
    )(*mine, *lands, *dgpre)
    return (out[0], out[1]), out[2], out[3], out[4]


def _adamw_math(w, g, m, v):
    m = ADAM_B1 * m + (1.0 - ADAM_B1) * g
    v = ADAM_B2 * v + (1.0 - ADAM_B2) * (g * g)
    m_hat = m / (1.0 - ADAM_B1 ** ADAM_STEP)
    v_hat = v / (1.0 - ADAM_B2 ** ADAM_STEP)
    delta = -ADAM_LR * (m_hat / (jnp.sqrt(v_hat) + ADAM_EPS) + ADAM_WD * w)
    return delta, m, v


def _adamw_layer(layer, mine, lands, w, m, v, earlier, token, name, rows):
    _, mm, nn = w.shape

    def body(hm_ref, ld_ref, w_ref, m_ref, v_ref, _, *refs):
        g_ref, d_ref, nm_ref, nv_ref = refs[-4:]
        g = _own_then_slots(hm_ref, ld_ref)
        g_ref[...] = g
        d, nm, nv = _adamw_math(w_ref[...], g, m_ref[...], v_ref[...])
        d_ref[...] = d
        nm_ref[...] = nm
        nv_ref[...] = nv

    spec = pl.BlockSpec((None, rows, nn), lambda i: (layer, i, 0))
    carried = [] if earlier is None else list(earlier)
    return pl.pallas_call(
        body, name=name, grid=(mm // rows,),
        in_specs=([pl.BlockSpec((rows, nn), lambda i: (i, 0)) if mine.ndim == 2
                   else pl.BlockSpec((N_DEV, rows, nn), lambda i: (0, i, 0)),
                   pl.BlockSpec((lands.shape[0], rows, nn), lambda i: (0, i, 0)),
                   spec, spec, spec] + [pl.BlockSpec(memory_space=pl.ANY)] * (1 + len(carried))),
        out_specs=[spec] * 4,
        out_shape=[jax.ShapeDtypeStruct(w.shape, F32)] * 4,
        input_output_aliases={6 + t: t for t in range(len(carried))},
        compiler_params=_compiler_params(("arbitrary",)),
    )(mine, lands, w, m, v, token, *carried)


def _adamw_small(gathered, params):
    def body(all_ref, *refs):
        ins, outs, packs = refs[:15], refs[15:15 + 21], refs[15 + 21]
        loss_ref = outs[0]
        for dev in range(N_DEV):
            for l in range(DEPTH):
                packs[l, PACK_SLICE * dev:PACK_SLICE * (dev + 1), :] = (
                    all_ref[SMALL_ROWS * dev + PACK_SLICE * l:SMALL_ROWS * dev + PACK_SLICE * (l + 1), :])
        loss_ref[...] = packs[DEPTH - 1, ROW_LOSS:ROW_LOSS + 1, 0:1]

        def update(p, sel, g):
            w_ref, m_ref, v_ref = ins[p], ins[5 + p], ins[10 + p]
            d, nm, nv = _adamw_math(w_ref[sel], g, m_ref[sel], v_ref[sel])
            for t, val in enumerate((g, d, nm, nv)):
                outs[1 + 5 * t + p][sel] = val

        for l in range(DEPTH):
            gp = packs.at[l]
            row0 = 2 * PACK_SLICE + 8 * l
            dgpre = all_ref[row0:row0 + 8, :]
            for dev in range(1, N_DEV):
                dgpre = dgpre + all_ref[SMALL_ROWS * dev + row0:SMALL_ROWS * dev + row0 + 8, :]
            for grp in range(4):
                update(0, (l, grp), gp[ROW_PW + BLOCK * grp:ROW_PW + BLOCK * (grp + 1), :])
                update(1, (slice(l, l + 1), slice(128 * grp, 128 * (grp + 1))), gp[ROW_SC + grp:ROW_SC + grp + 1, :])
            update(2, (slice(l, l + 1), slice(None)), gp[ROW_SINK:ROW_SINK + 1, 0:N_HEADS])
            for r in range(D_MODEL // 128):
                sel = (slice(l, l + 1), slice(128 * r, 128 * (r + 1)))
                update(3, sel, dgpre[r:r + 1, :])
                update(4, sel, gp[ROW_NPOST + r:ROW_NPOST + r + 1, :])

    shapes = [jax.ShapeDtypeStruct(p.shape, F32) for p in params[:5]]
    return pl.pallas_call(
        body, name="adamw_small",
        out_shape=[jax.ShapeDtypeStruct((1, 1), F32)] + shapes * 4,
        scratch_shapes=[pltpu.VMEM((DEPTH, PACK_ROWS, 128), F32)],
        compiler_params=_compiler_params(),
    )(gathered, *params)


def kernel(x, w_in, pool_w, pool_scale, attn_sinks, w_out, norm_pre, norm_post, loss_target, m_w_in, m_pool_w, m_pool_scale, m_attn_sinks, m_w_out, m_norm_pre, m_norm_post, v_w_in, v_pool_w, v_pool_scale, v_attn_sinks, v_w_out, v_norm_pre, v_norm_post):
    x0 = x.reshape(SEQ, D_MODEL)
    target = loss_target.reshape(SEQ, D_MODEL)
    bias = jnp.asarray(_attn_bias())
    w_in_t, m_in_t, v_in_t = (jnp.swapaxes(t, 1, 2) for t in (w_in, m_w_in, v_w_in))

    (win0, wout0), later, lands = _allgather([(w_in_t, 0), (w_out, 0)], BF16, "gather_w0",
                                              later=[(w_in_t, 1), (w_out, 1)])
    sems, later, lands, token = _gather_start(later, lands, [SIBLING_AND_SAME_CORE] * 2, COLLECTIVE_GATHER_W1,
                                              "gather_w1_start")
    win_full, wout_full = [win0, None], [wout0, None]

    saved = []
    xl = x0
    for layer in range(DEPTH):
        front = (layer, xl, norm_pre, win_full[layer], token, attn_sinks, pool_w, pool_scale, bias)
        if layer == 0:
            u, pg, q, k, v, ag, z, a = _fwd_front(*front)
            land = _gather_wait(sems[0], later[0], lands[0], SIBLING_AND_SAME_CORE, z, "gather_w_in1_wait")
            fsems, land, token = _forward_start(land, IN_SHARD, COLLECTIVE_FORWARD_W_IN1, "forward_w_in1_start")
            x_next, y = _fwd_out(layer, z, xl, norm_post, wout_full[layer], token)
            win_full[1] = _forward_wait(fsems, land, IN_SHARD, x_next, "forward_w_in1_wait")
            land = _gather_wait(sems[1], later[1], lands[1], SIBLING_AND_SAME_CORE, win_full[1], "gather_w_out1_wait")
            fsems, land, token = _forward_start(land, OUT_SHARD, COLLECTIVE_FORWARD_W_OUT1, "forward_w_out1_start")
            wout_full[1] = _forward_wait(fsems, land, OUT_SHARD, token, "forward_w_out1_wait")
        else:
            u, pg, q, k, v, ag, z, a, x_next, y = _fwd_front(*front, out=(norm_post, wout_full[layer]))
        saved.append((xl, u, pg, q, k, v, ag, z, a, y))
        xl = x_next

    params_small = [pool_w, pool_scale, attn_sinks, norm_pre, norm_post,
                    m_pool_w, m_pool_scale, m_attn_sinks, m_norm_pre, m_norm_post,
                    v_pool_w, v_pool_scale, v_attn_sinks, v_norm_pre, v_norm_post]

    def start(srcs, directs, paired, collective_id, tag):
        send_sem, recv_sem, srcs, lands, started = _exchange_start(srcs, directs, collective_id, f"exchange_start{tag}")
        return (send_sem, recv_sem, srcs, lands, paired, directs), started

    def finish(handle, after, tag):
        send_sem, recv_sem, srcs, lands, paired, directs = handle
        srcs, lands = _exchange_wait(send_sem, recv_sem, srcs, lands, directs, after, f"exchange_wait{tag}")
        return [s if p is None else p for s, p in zip(srcs, paired)], lands

    def back(layer, top, first, second):
        xin, u, pg, q, k, v, ag, z, a, y = saved[layer]
        return _bwd_back(layer, top, first, second, y, z, norm_post, wout_full[layer], attn_sinks, u, pg, q, k, v,
                         ag, a, pool_w, pool_scale, bias)

    dgpre = [None] * DEPTH
    dx, dproj, gw_out, pack = back(1, True, xl, target)
    dx, dgpre[1], chip_sums, own_sum = _bwd_in(1, "both_pair", token, dproj, saved[1][0], norm_pre, dx, win_full[1])
    top, token = start([chip_sums, gw_out.reshape(N_DEV, OUT_SHARD, D_MODEL), pack.reshape(N_DEV, PACK_SLICE, 128)],
                       [False, True, True], [own_sum, None, None], COLLECTIVE_EXCHANGE_1, "1")

    dproj, gw_out, pack = back(0, False, dx, token)
    early, token = start([gw_out.reshape(N_DEV, OUT_SHARD, D_MODEL), pack.reshape(N_DEV, PACK_SLICE, 128)],
                         [True, True], [None, None], COLLECTIVE_EXCHANGE_0A, "0a")
    chip_sums, own_sum = _bwd_in(0, "dw_pair", token, dproj, saved[0][0], norm_pre)
    late, token = start([chip_sums], [False], [own_sum], COLLECTIVE_EXCHANGE_0B, "0b")
    dx, dgpre[0] = _bwd_in(0, "dx", token, dproj, saved[0][0], norm_pre, dx, win_full[0])

    own1, lands1 = finish(top, dx, "1")
    big_in = _adamw_layer(1, own1[0], lands1[0], w_in_t, m_in_t, v_in_t, None, token, "adamw_in1", ADAM_ROWS_IN)
    big_out = _adamw_layer(1, own1[1], lands1[1], w_out, m_w_out, v_w_out, None, token, "adamw_out1", ADAM_ROWS_OUT)
    own0a, lands0a = finish(early, big_out[0], "0a")
    sems, block, land, token = _small_gather_start([own0a[1], own1[2]], [lands0a[1], lands1[2]], dgpre,
                                                   "gather_small_start")
    big_out = _adamw_layer(0, own0a[0], lands0a[0], w_out, m_w_out, v_w_out, big_out, token, "adamw_out0", ADAM_ROWS_OUT)
    own0b, lands0b = finish(late, big_out[0], "0b")
    big_in = _adamw_layer(0, own0b[0], lands0b[0], w_in_t, m_in_t, v_in_t, big_in, token, "adamw_in0", ADAM_ROWS_IN)
    gathered = _gather_wait(sems, block, land, ALL_PEERS, big_in[0], "gather_small_wait")
    small_out = _adamw_small(gathered, params_small)
    loss = small_out[0].reshape(())

    outs = [loss, dx.reshape(1, SEQ, D_MODEL)]
    for t in range(4):
        pw_, sc_, sk_, npre_, npost_ = small_out[1 + 5 * t:6 + 5 * t]
        outs += [jnp.swapaxes(big_in[t], 1, 2), pw_, sc_, sk_, big_out[t], npre_, npost_]
    return tuple(outs)
```

```python
import numpy as np
import jax
import jax.numpy as jnp
from jax import lax
from jax.experimental import pallas as pl
from jax.experimental.pallas import tpu as pltpu

F32 = jnp.float32
BF16 = jnp.bfloat16

N_DEV = 8
SEQ = 2048
D_MODEL = 1024
D_POOL = 512
D_ATTN = 512
D_KV = 128
D_IN = 2304
N_HEADS = 8
GQA = 4
HEAD_DIM = 64
BLOCK = 128
POOL_WINDOWS = (2, 4, 8, 16)
DEPTH = 2
EPS = 1e-6
NEG_INF = -1e30
SCALE = HEAD_DIM ** -0.5
IN_SHARD = D_IN // N_DEV
OUT_SHARD = D_MODEL // N_DEV

COL_U, COL_PG, COL_Q, COL_K, COL_V, COL_AG = 0, 512, 1024, 1536, 1664, 1792

ADAM_LR = 0.001
ADAM_B1 = 0.9
ADAM_B2 = 0.999
ADAM_EPS = 1e-08
ADAM_WD = 0.01
ADAM_STEP = 10

TOKEN_TILE = 512
FWD_OUT_TILE = 1024
ADAM_ROWS_IN, ADAM_ROWS_OUT = 144, 64
VMEM_LIMIT = 56 * 1024 * 1024
MESH = pl.DeviceIdType.MESH

ROW_PW, ROW_SC, ROW_SINK, ROW_NPOST, ROW_LOSS = 0, 512, 520, 536, 544
PACK_ROWS = 576
PACK_SLICE = PACK_ROWS // N_DEV


def _nn(a, b):
    return jnp.dot(a, b, preferred_element_type=F32)


def _nt(a, b):
    return lax.dot_general(a, b, (((1,), (1,)), ((), ())), preferred_element_type=F32)


def _tn(a, b):
    return lax.dot_general(a, b, (((0,), (0,)), ((), ())), preferred_element_type=F32)


def _silu_parts(g):
    s = jax.nn.sigmoid(g)
    return g * s, s * (1.0 + g * (1.0 - s))


def _resident(shape):
    return pl.BlockSpec(shape, lambda *_: (0,) * len(shape), pipeline_mode=pl.Buffered(1))


def _compiler_params(sem=None):
    if sem is None:
        return pltpu.CompilerParams(vmem_limit_bytes=VMEM_LIMIT)
    return pltpu.CompilerParams(dimension_semantics=sem, vmem_limit_bytes=VMEM_LIMIT)


def _attn_bias():
    t = np.arange(BLOCK)[None, :]
    j = np.arange(BLOCK)[:, None]
    current = j <= t
    dist = np.where(current, t - j, t + BLOCK - j).astype(np.float32)
    out = np.zeros((2, 2, BLOCK, GQA * BLOCK), np.float32)
    for variant in range(2):
        valid = current | (variant == 1)
        for kv in range(2):
            for g in range(GQA):
                slope = np.float32(2.0 ** (-(kv * GQA + g + 1)))
                out[variant, kv, :, g * BLOCK:(g + 1) * BLOCK] = np.where(valid, -slope * dist, np.float32(NEG_INF))
    return out


def _replicate_head(kx, kv):
    rolled = pltpu.roll(kx, 64, 1)
    lane = lax.broadcasted_iota(jnp.int32, kx.shape, 1)
    dup = jnp.where(lane < 64, kx, rolled) if kv == 0 else jnp.where(lane < 64, rolled, kx)
    return jnp.concatenate([dup, dup], axis=1).astype(BF16)


def _stack_heads(qv):
    lane = lax.broadcasted_iota(jnp.int32, qv.shape, 1)
    zero = jnp.zeros_like(qv)
    return jnp.concatenate([jnp.where((lane >= 64 * g) & (lane < 64 * g + 64), qv, zero) for g in range(GQA)], axis=0)


def _unstack_heads(xs):
    lane = lax.broadcasted_iota(jnp.int32, (BLOCK, 256), 1)
    return jnp.where(lane < 64, xs[0:128], jnp.where(lane < 128, xs[128:256], jnp.where(lane < 192, xs[256:384], xs[384:512])))


def _fold_heads(r):
    h = r[:, 0:128] + r[:, 128:256]
    return h + pltpu.roll(h, 64, 1)


def _sink_row(sink_ref, layer, kv):
    lane = lax.broadcasted_iota(jnp.int32, (1, GQA * BLOCK), 1)
    s4 = [sink_ref[layer, kv * GQA + g] for g in range(GQA)]
    return jnp.where(lane < 128, s4[0], jnp.where(lane < 256, s4[1], jnp.where(lane < 384, s4[2], s4[3])))


def _band_is_current():
    j = lax.broadcasted_iota(jnp.int32, (BLOCK, GQA * BLOCK), 0)
    t = lax.broadcasted_iota(jnp.int32, (BLOCK, GQA * BLOCK), 1) & (BLOCK - 1)
    return j <= t


def _pack_band(full, current):
    return jnp.where(current, full[BLOCK:], full[:BLOCK])


def _unpack_band(packed, current):
    zero = jnp.zeros_like(packed)
    return jnp.concatenate([jnp.where(current, zero, packed), jnp.where(current, packed, zero)], axis=0)


def _probs_keys_major(k_rep, q_st, bias, sink, current):
    st = _pack_band(_nt(k_rep, q_st), current) * SCALE + bias
    m = jnp.maximum(jnp.max(st, axis=0, keepdims=True), sink)
    p = jnp.exp(st - m)
    esink = jnp.exp(sink - m)
    rl = 1.0 / (jnp.sum(p, axis=0, keepdims=True) + esink)
    return p * rl, esink * rl


WINDOW_HALO = 16


def _window_sum(ext, w, forward):
    s = ext
    sh = 1
    while sh < w:
        s = s + pltpu.roll(s, (ext.shape[0] - sh) if forward else sh, 0)
        sh *= 2
    return s


def _inv_count(n, w):
    t = n * BLOCK + lax.broadcasted_iota(jnp.int32, (BLOCK, 1), 0) + 1
    return 1.0 / jnp.minimum(t.astype(F32), float(w))


def _kv_ext(ref, n):
    r0 = pl.multiple_of(jnp.maximum(n - 1, 0) * BLOCK, BLOCK)
    r1 = pl.multiple_of(n * BLOCK, BLOCK)
    return jnp.concatenate([ref[pl.ds(r0, BLOCK), :], ref[pl.ds(r1, BLOCK), :]], axis=0)


def _rows_of(vec_ref, pack_ref, row0):
    for r in range(D_MODEL // 128):
        pack_ref[row0 + r:row0 + r + 1, :] = vec_ref[:, 128 * r:128 * (r + 1)]


FRONT_TILE = 4 * BLOCK


def _fwd_front(layer, x, norm_pre, w_in_t, token, sinks, pool_w, pool_scale, bias, out=None):
    tm = FRONT_TILE

    def body(sink_ref, x_ref, g_ref, w_ref, _, pw_ref, sc_ref, bias_ref, *refs):
        if out is not None:
            gpost_ref, wo_ref, *refs = refs
            xn_ref, y_ref = refs[8:10]
        u_ref, pg_ref, q_ref, k_ref, v_ref, ag_ref, z_ref, a_ref = refs[:8]
        uprev, kprev, vprev = refs[-3:]
        i = pl.program_id(0)

        @pl.when(i == 0)
        def _():
            uprev[...] = jnp.zeros_like(uprev)
            kprev[...] = jnp.zeros_like(kprev)
            vprev[...] = jnp.zeros_like(vprev)

        xv = x_ref[...]
        r = lax.rsqrt(jnp.mean(xv * xv, axis=-1, keepdims=True) + EPS)
        h = (xv * r * g_ref[layer:layer + 1, :]).astype(BF16)
        u_ref[...] = _nt(h, w_ref[COL_U:COL_PG, :])
        pg_ref[...] = _nt(h, w_ref[COL_PG:COL_Q, :])
        for sb in range(tm // BLOCK):
            n = (tm // BLOCK) * i + sb
            rows = slice(BLOCK * sb, BLOCK * (sb + 1))
            before = slice(BLOCK * (sb - 1), BLOCK * sb)
            uv = u_ref[rows, :]
            halo = (uprev[BLOCK - WINDOW_HALO:, :] if sb == 0
                    else u_ref[BLOCK * sb - WINDOW_HALO:BLOCK * sb, :])
            ext = jnp.concatenate([halo, uv], axis=0)
            for g, w in enumerate(POOL_WINDOWS):
                cs = slice(BLOCK * g, BLOCK * (g + 1))
                win = _window_sum(ext[:, cs], w, forward=False)[WINDOW_HALO:]
                pooled = win * _inv_count(n, w) - uv[:, cs]
                mixed = _nn(pooled.astype(BF16), pw_ref[g].astype(BF16))
                gate, _ = _silu_parts(pg_ref[rows, cs])
                z_ref[rows, cs] = (mixed * sc_ref[layer:layer + 1, cs] * gate).astype(BF16)

        q_ref[...] = _nt(h, w_ref[COL_Q:COL_K, :]).astype(BF16)
        k_ref[...] = _nt(h, w_ref[COL_K:COL_V, :])
        v_ref[...] = _nt(h, w_ref[COL_V:COL_AG, :])
        ag_ref[...] = _nt(h, w_ref[COL_AG:D_IN, :])

        current = _band_is_current()
        for sb in range(tm // BLOCK):
            n = (tm // BLOCK) * i + sb
            rows = slice(BLOCK * sb, BLOCK * (sb + 1))
            before = slice(BLOCK * (sb - 1), BLOCK * sb)
            kx = jnp.concatenate([kprev[...] if sb == 0 else k_ref[before, :], k_ref[rows, :]], axis=0)
            vx = jnp.concatenate([vprev[...] if sb == 0 else v_ref[before, :], v_ref[rows, :]], axis=0)
            variant = jnp.minimum(n, 1) if sb == 0 else 1
            for kv in range(2):
                cs = slice(256 * kv, 256 * (kv + 1))
                p, _ = _probs_keys_major(_replicate_head(kx, kv), _stack_heads(q_ref[rows, cs]),
                                         bias_ref[variant, kv], _sink_row(sink_ref, layer, kv), current)
                o = _unstack_heads(_tn(_unpack_band(p.astype(BF16), current), _replicate_head(vx, kv)))
                a_ref[rows, cs] = o
                gate, _ = _silu_parts(ag_ref[rows, cs])
                z_ref[rows, D_POOL + 256 * kv:D_POOL + 256 * (kv + 1)] = (o * gate).astype(BF16)

        tail = slice(tm - BLOCK, tm)
        uprev[...] = u_ref[tail, :]
        kprev[...] = k_ref[tail, :]
        vprev[...] = v_ref[tail, :]

        if out is not None:
            y = _nn(z_ref[...], wo_ref[...])
            y_ref[...] = y
            r = lax.rsqrt(jnp.mean(y * y, axis=-1, keepdims=True) + EPS)
            xn_ref[...] = x_ref[...] + y * r * gpost_ref[layer:layer + 1, :]

    row = lambda c: pl.BlockSpec((tm, c), lambda i: (i, 0))
    const = lambda shape: pl.BlockSpec(shape, lambda i: (0,) * len(shape))
    act = jax.ShapeDtypeStruct((SEQ, D_MODEL), F32)
    fused = out is not None
    return pl.pallas_call(
        body, name=f"fwd_front{layer}", grid=(SEQ // tm,),
        in_specs=[pl.BlockSpec(memory_space=pltpu.SMEM), row(D_MODEL), const((DEPTH, D_MODEL)),
                  _resident((D_IN, D_MODEL)), const((8, 128)),
                  pl.BlockSpec((None, 4, BLOCK, BLOCK), lambda i: (layer, 0, 0, 0)), const((DEPTH, D_POOL)),
                  _resident((2, 2, BLOCK, GQA * BLOCK))]
                 + ([const((DEPTH, D_MODEL)), _resident((D_MODEL, D_MODEL))] if fused else []),
        out_specs=[row(D_POOL), row(D_POOL), row(D_ATTN), row(D_KV), row(D_KV), row(D_ATTN), row(D_MODEL),
                   row(D_ATTN)] + ([row(D_MODEL)] * 2 if fused else []),
        out_shape=[jax.ShapeDtypeStruct((SEQ, D_POOL), F32), jax.ShapeDtypeStruct((SEQ, D_POOL), F32),
                   jax.ShapeDtypeStruct((SEQ, D_ATTN), BF16), jax.ShapeDtypeStruct((SEQ, D_KV), F32),
                   jax.ShapeDtypeStruct((SEQ, D_KV), F32), jax.ShapeDtypeStruct((SEQ, D_ATTN), F32),
                   jax.ShapeDtypeStruct((SEQ, D_MODEL), BF16), jax.ShapeDtypeStruct((SEQ, D_ATTN), F32)]
                  + ([act] * 2 if fused else []),
        scratch_shapes=[pltpu.VMEM((BLOCK, D_POOL), F32), pltpu.VMEM((BLOCK, D_KV), F32),
                        pltpu.VMEM((BLOCK, D_KV), F32)],
        compiler_params=_compiler_params(("arbitrary",)),
    )(sinks, x, norm_pre, w_in_t, token, pool_w, pool_scale, bias, *(out if fused else ()))


def _fwd_out(layer, z, x, norm_post, w_out, token):
    tm = FWD_OUT_TILE

    def body(z_ref, x_ref, g_ref, w_ref, _, xn_ref, y_ref):
        y = _nn(z_ref[...], w_ref[...])
        y_ref[...] = y
        r = lax.rsqrt(jnp.mean(y * y, axis=-1, keepdims=True) + EPS)
        xn_ref[...] = x_ref[...] + y * r * g_ref[layer:layer + 1, :]

    row = lambda c: pl.BlockSpec((tm, c), lambda i: (i, 0))
    return pl.pallas_call(
        body, name=f"fwd_out{layer}", grid=(SEQ // tm,),
        in_specs=[row(D_MODEL), row(D_MODEL), pl.BlockSpec((DEPTH, D_MODEL), lambda i: (0, 0)),
                  _resident((D_MODEL, D_MODEL)), pl.BlockSpec((8, 128), lambda i: (0, 0))],
        out_specs=[row(D_MODEL), row(D_MODEL)],
        out_shape=[jax.ShapeDtypeStruct((SEQ, D_MODEL), F32), jax.ShapeDtypeStruct((SEQ, D_MODEL), F32)],
        compiler_params=_compiler_params(("arbitrary",)),
    )(z, x, norm_post, w_out, token)


BACK_TILE = 2 * BLOCK


def _bwd_back(layer, top, dxo_or_xf, target_or_token, y, z, norm_post, w_out, sinks, u, pg, q, k, v, ag, a,
              pool_w, pool_scale, bias):
    tm = BACK_TILE
    steps = SEQ // tm
    last = steps - 1
    per = tm // BLOCK

    def body(*refs):
        refs = list(refs)
        sink_ref, first, second = refs[:3]
        (y_ref, z_ref, g_ref, w_ref, u_ref, up_ref, pg_ref, q_ref, k_ref, v_ref, ag_ref, a_ref, pw_ref, sc_ref,
         bias_ref) = refs[3:18]
        del refs[:18]
        dxo_ref = refs.pop(0) if top else None
        dp_ref, dw_ref, pack_ref, acc, dg, lacc, dzs, ck, cv, ce = refs
        i = pl.program_id(0)
        blk = last - i

        @pl.when(i == 0)
        def _():
            acc[...] = jnp.zeros_like(acc)
            dg[...] = jnp.zeros_like(dg)
            lacc[...] = jnp.zeros_like(lacc)
            pack_ref[...] = jnp.zeros_like(pack_ref)
            ck[...] = jnp.zeros_like(ck)
            cv[...] = jnp.zeros_like(cv)
            ce[...] = jnp.zeros_like(ce)

        if top:
            d = first[...] - second[...]
            dxo_v = d * (1.0 / D_MODEL)
            dxo_ref[...] = dxo_v
            part = jnp.sum(d * d, axis=-1, keepdims=True) * (1.0 / D_MODEL)
            lacc[...] += 0.5 * jnp.sum(part, axis=0, keepdims=True)
        else:
            dxo_v = first[...]
        yv = y_ref[...]
        r = lax.rsqrt(jnp.mean(yv * yv, axis=-1, keepdims=True) + EPS)
        yn = yv * r
        dg[...] += jnp.sum(dxo_v * yn, axis=0, keepdims=True)
        dyn = dxo_v * g_ref[layer:layer + 1, :]
        dy = (r * (dyn - yn * jnp.mean(dyn * yn, axis=-1, keepdims=True))).astype(BF16)
        dzs[...] = _nt(dy, w_ref[...])
        acc[...] += _tn(z_ref[...], dy)

        lane = lax.broadcasted_iota(jnp.int32, (1, 128), 1)
        lane2 = lax.broadcasted_iota(jnp.int32, (256, 128), 1)
        current = _band_is_current()
        for sb in reversed(range(per)):
            n = per * blk + sb
            rows = slice(BLOCK * sb, BLOCK * (sb + 1))

            uv = u_ref[rows, :]
            if sb == 0:
                halo = up_ref[BLOCK - WINDOW_HALO:, :] * (n > 0).astype(F32)
            else:
                halo = u_ref[BLOCK * sb - WINDOW_HALO:BLOCK * sb, :]
            ext = jnp.concatenate([halo, uv], axis=0)
            for g, w in enumerate(POOL_WINDOWS):
                cs = slice(BLOCK * g, BLOCK * (g + 1))
                inv = _inv_count(n, w)
                win = _window_sum(ext[:, cs], w, forward=False)[WINDOW_HALO:]
                pooled = win * inv - uv[:, cs]
                pw_g = pw_ref[g].astype(BF16)
                mixed = _nn(pooled.astype(BF16), pw_g)
                gate, dgate = _silu_parts(pg_ref[rows, cs])
                dzp = dzs[rows, cs]
                sc = sc_ref[layer:layer + 1, cs]
                dpm = dzp * gate
                dp_ref[rows, COL_PG + BLOCK * g:COL_PG + BLOCK * (g + 1)] = (dzp * (mixed * sc) * dgate).astype(BF16)
                pack_ref[ROW_SC + g:ROW_SC + g + 1, :] += jnp.sum(dpm * mixed, axis=0, keepdims=True)
                dmixed = (dpm * sc).astype(BF16)
                pack_ref[ROW_PW + BLOCK * g:ROW_PW + BLOCK * (g + 1), :] += _tn(pooled.astype(BF16), dmixed)
                dpooled = _nt(dmixed, pw_g)
                e = dpooled * inv
                lead = _window_sum(jnp.concatenate([e, ce[:WINDOW_HALO, cs]], axis=0), w, forward=True)[:BLOCK]
                dp_ref[rows, COL_U + BLOCK * g:COL_U + BLOCK * (g + 1)] = (lead - dpooled).astype(BF16)
                ce[:, cs] = e

            kx = _kv_ext(k_ref, n)
            vx = _kv_ext(v_ref, n)
            variant = jnp.minimum(n, 1) if sb == 0 else 1
            dsink_row = jnp.zeros((1, 128), F32)
            tks, tvs = [], []
            for kv in range(2):
                cs = slice(256 * kv, 256 * (kv + 1))
                k_rep = _replicate_head(kx, kv)
                v_rep = _replicate_head(vx, kv)
                q_st = _stack_heads(q_ref[rows, cs])
                gate, dgate = _silu_parts(ag_ref[rows, cs])
                dza = dzs[rows, D_POOL + 256 * kv:D_POOL + 256 * (kv + 1)]
                dp_ref[rows, COL_AG + 256 * kv:COL_AG + 256 * (kv + 1)] = (dza * a_ref[rows, cs] * dgate).astype(BF16)
                da_st = _stack_heads((dza * gate).astype(BF16))
                p, psink = _probs_keys_major(k_rep, q_st, bias_ref[variant, kv], _sink_row(sink_ref, layer, kv),
                                             current)
                dpt = _pack_band(_nt(v_rep, da_st), current)
                delta = jnp.sum(p * dpt, axis=0, keepdims=True)
                dst = _unpack_band((p * (dpt - delta) * SCALE).astype(BF16), current)
                sink_terms = psink * delta
                for g in range(GQA):
                    dsink = -jnp.sum(sink_terms[:, BLOCK * g:BLOCK * (g + 1)], axis=1, keepdims=True)
                    dsink_row = dsink_row + jnp.where(lane == kv * GQA + g, dsink, 0.0)
                dp_ref[rows, COL_Q + 256 * kv:COL_Q + 256 * (kv + 1)] = _unstack_heads(_tn(dst, k_rep)).astype(BF16)
                tks.append(_fold_heads(_nn(dst, q_st)))
                tvs.append(_fold_heads(_nn(_unpack_band(p.astype(BF16), current), da_st)))
            pack_ref[ROW_SINK:ROW_SINK + 1, :] += dsink_row
            dkx = jnp.where(lane2 < 64, tks[0], tks[1])
            dvx = jnp.where(lane2 < 64, tvs[0], tvs[1])
            dp_ref[rows, COL_K:COL_V] = (ck[...] + dkx[BLOCK:]).astype(BF16)
            dp_ref[rows, COL_V:COL_AG] = (cv[...] + dvx[BLOCK:]).astype(BF16)
            ck[...] = dkx[:BLOCK]
            cv[...] = dvx[:BLOCK]

        @pl.when(i == steps - 1)
        def _():
            dw_ref[...] = acc[...].astype(BF16)
            _rows_of(dg, pack_ref, ROW_NPOST)
            pack_ref[ROW_LOSS:ROW_LOSS + 1, :] = jnp.where(lane == 0, lacc[...], 0.0)

    row = lambda c: pl.BlockSpec((tm, c), lambda i: (last - i, 0))
    const = lambda shape: pl.BlockSpec(shape, lambda i: (0,) * len(shape))
    act = jax.ShapeDtypeStruct((SEQ, D_MODEL), F32)
    return pl.pallas_call(
        body, name=f"bwd_back{layer}", grid=(steps,),
        in_specs=[pl.BlockSpec(memory_space=pltpu.SMEM), row(D_MODEL), row(D_MODEL) if top else const((8, 128)),
                  row(D_MODEL), row(D_MODEL), const((DEPTH, D_MODEL)), _resident((D_MODEL, D_MODEL)),
                  row(D_POOL), pl.BlockSpec((BLOCK, D_POOL), lambda i: (jnp.maximum(per * (last - i) - 1, 0), 0)),
                  row(D_POOL), row(D_ATTN), _resident((SEQ, D_KV)), _resident((SEQ, D_KV)), row(D_ATTN), row(D_ATTN),
                  pl.BlockSpec((None, 4, BLOCK, BLOCK), lambda i: (layer, 0, 0, 0)), const((DEPTH, D_POOL)),
                  _resident((2, 2, BLOCK, GQA * BLOCK))],
        out_specs=([row(D_MODEL)] * (1 if top else 0)
                   + [row(D_IN), const((D_MODEL, D_MODEL)), const((PACK_ROWS, 128))]),
        out_shape=([act] * (1 if top else 0)
                   + [jax.ShapeDtypeStruct((SEQ, D_IN), BF16), jax.ShapeDtypeStruct((D_MODEL, D_MODEL), BF16),
                      jax.ShapeDtypeStruct((PACK_ROWS, 128), F32)]),
        scratch_shapes=[pltpu.VMEM((D_MODEL, D_MODEL), F32), pltpu.VMEM((1, D_MODEL), F32), pltpu.VMEM((1, 1), F32),
                        pltpu.VMEM((tm, D_MODEL), F32), pltpu.VMEM((BLOCK, D_KV), F32), pltpu.VMEM((BLOCK, D_KV), F32),
                        pltpu.VMEM((BLOCK, D_POOL), F32)],
        compiler_params=_compiler_params(("arbitrary",)),
    )(sinks, dxo_or_xf, target_or_token, y, z, norm_post, w_out, u, u, pg, q, k, v, ag, a, pool_w, pool_scale, bias)


def _bwd_in(layer, part, token, dproj, x, norm_pre, dxo=None, w_in_t=None):
    pair = part in ("dw_pair", "both_pair")
    want_dw, want_dx = part != "dx", part in ("both", "dx", "both_pair")
    tm = TOKEN_TILE
    steps = SEQ // tm
    cw = 256

    def body(*refs):
        refs = list(refs)
        dp_ref, x_ref, g_ref = refs[1:4]
        del refs[:4]
        if want_dx:
            dxo_ref, w_ref, dx_ref, dgo_ref = refs[:4]
            del refs[:4]
            dg = refs.pop()
        if pair:
            hs_ref, hm_ref, acc, mine_buf, theirs_buf, send_sem, recv_sem = refs
        elif want_dw:
            dw_ref, acc = refs
        i = pl.program_id(0)

        @pl.when(i == 0)
        def _():
            if pair:
                _handshake([(lax.axis_index("x"), lax.axis_index("y"), 1 - lax.axis_index("c"))])
            if want_dw:
                acc[...] = jnp.zeros_like(acc)
            if want_dx:
                dg[...] = jnp.zeros_like(dg)

        xv = x_ref[...]
        gv = g_ref[layer:layer + 1, :]
        r = lax.rsqrt(jnp.mean(xv * xv, axis=-1, keepdims=True) + EPS)
        xn = xv * r
        if want_dw:
            hb = (xn * gv).astype(BF16)
            for c in range(0, D_IN, cw):
                acc[c:c + cw, :] += _tn(dp_ref[:, c:c + cw], hb)
        def rows_for(q, core):
            return pl.ds(pl.multiple_of((2 * q + core) * IN_SHARD, 8), IN_SHARD)

        def swap(q):
            x, y, c = _mesh_pos()
            return pltpu.make_async_remote_copy(
                src_ref=mine_buf.at[q], dst_ref=theirs_buf.at[q], send_sem=send_sem.at[q], recv_sem=recv_sem.at[q],
                device_id=(x, y, 1 - c), device_id_type=MESH)

        if pair:
            @pl.when(i == steps - 1)
            def _():
                for q in range(4):
                    mine_buf[q] = acc[rows_for(q, 1 - lax.axis_index("c")), :].astype(BF16)
                    swap(q).start()

        if want_dx:
            dh = _nn(dp_ref[...], w_ref[...])
            dg[...] += jnp.sum(dh * xn, axis=0, keepdims=True)
            dhn = dh * gv
            dx_ref[...] = dxo_ref[...] + r * (dhn - xn * jnp.mean(dhn * xn, axis=-1, keepdims=True))

        @pl.when(i == steps - 1)
        def _():
            if pair:
                x, y, c = _mesh_pos()
                for q in range(4):
                    swap(q).wait()
                for j, q in enumerate([2 * (1 - x) + y, 2 * x + (1 - y), 2 * (1 - x) + (1 - y)]):
                    hs_ref[j] = (acc[rows_for(q, c), :] + theirs_buf[q].astype(F32)).astype(BF16)
                hm_ref[...] = acc[rows_for(2 * x + y, c), :] + theirs_buf[2 * x + y].astype(F32)
            elif want_dw:
                dw_ref[...] = acc[...].astype(BF16)
            if want_dx:
                _rows_of(dg, dgo_ref, 0)

    row = lambda c: pl.BlockSpec((tm, c), lambda i: (i, 0))
    const = lambda shape: pl.BlockSpec(shape, lambda i: (0,) * len(shape))
    in_specs = [const((8, 128)), row(D_IN), row(D_MODEL), const((DEPTH, D_MODEL))]
    operands = [token, dproj, x, norm_pre]
    out_specs, out_shape, scratch = [], [], []
    if want_dx:
        in_specs += [row(D_MODEL), _resident((D_IN, D_MODEL))]
        operands += [dxo, w_in_t]
        out_specs += [row(D_MODEL), const((8, 128))]
        out_shape += [jax.ShapeDtypeStruct((SEQ, D_MODEL), F32), jax.ShapeDtypeStruct((8, 128), F32)]
    if pair:
        out_specs += [const((3, IN_SHARD, D_MODEL)), const((IN_SHARD, D_MODEL))]
        out_shape += [jax.ShapeDtypeStruct((3, IN_SHARD, D_MODEL), BF16), jax.ShapeDtypeStruct((IN_SHARD, D_MODEL), F32)]
        scratch += [pltpu.VMEM((D_IN, D_MODEL), F32), pltpu.VMEM((4, IN_SHARD, D_MODEL), BF16),
                    pltpu.VMEM((4, IN_SHARD, D_MODEL), BF16), pltpu.SemaphoreType.DMA((4,)), pltpu.SemaphoreType.DMA((4,))]
    elif want_dw:
        out_specs.append(const((D_IN, D_MODEL)))
        out_shape.append(jax.ShapeDtypeStruct((D_IN, D_MODEL), BF16))
        scratch.append(pltpu.VMEM((D_IN, D_MODEL), F32))
    if want_dx:
        scratch.append(pltpu.VMEM((1, D_MODEL), F32))
    params = pltpu.CompilerParams(dimension_semantics=("arbitrary",), vmem_limit_bytes=VMEM_LIMIT,
                                  collective_id=COLLECTIVE_PAIR_SUM[layer] if pair else None)
    return pl.pallas_call(
        body, name=f"bwd_in_{part}{layer}", grid=(steps,),
        in_specs=in_specs, out_specs=out_specs, out_shape=out_shape, scratch_shapes=scratch,
        compiler_params=params,
    )(*operands)


def _mesh_pos():
    return lax.axis_index("x"), lax.axis_index("y"), lax.axis_index("c")


def _device_rows(ref, m, px, py, pc):
    return ref.at[pl.ds(pl.multiple_of((4 * px + 2 * py + pc) * m, 16 if m % 16 == 0 else 8), m), :]


def _allgather(srcs, out_dtype, name, later=()):
    na, nb = len(srcs), len(later)
    every = list(srcs) + list(later)
    shapes = [(a.shape[-2], a.shape[-1]) for a, _ in every]

    def body(*refs):
        xs, refs = refs[:na + nb], refs[na + nb:]
        outs, cast, land, refs = refs[:na], refs[na:na + nb], refs[na + nb:na + 2 * nb], refs[na + 2 * nb:]
        stage, raw, (send_sems, recv_sems, local_sems, load_sems) = refs[:na], refs[na:2 * na + nb], refs[2 * na + nb:]
        loads = [pltpu.make_async_copy(xs[i].at[every[i][1]], raw[i], load_sems.at[i]) for i in range(na + nb)]
        for cp in loads:
            cp.start()
        x, y, c = _mesh_pos()
        me, sibling = (x, y, c), (x, y, 1 - c)
        near = [(1 - x, y), (x, 1 - y)]
        far = (1 - x, 1 - y)
        relay_from, relay_to = (x ^ (1 - c), y ^ c), (x ^ c, y ^ (1 - c))
        _handshake([sibling] + [(*chip, c) for chip in near])
        k_from, k_to = 1 + c, 2 - c

        def slot(a, px, py, pc):
            return _device_rows(outs[a], shapes[a][0], px, py, pc)

        def copy(a, k, block, to, src=None):
            return pltpu.make_async_remote_copy(
                src_ref=slot(a, *block) if src is None else src, dst_ref=slot(a, *block),
                send_sem=send_sems.at[a, k], recv_sem=recv_sems.at[a, k], device_id=to, device_id_type=MESH)

        def cast_block(i):
            loads[i].wait()
            return raw[i][...].astype(out_dtype)

        for a in range(na):
            stage[a][...] = cast_block(a)
        mine = [pltpu.make_async_copy(stage[a], slot(a, *me), local_sems.at[a]) for a in range(na)]
        for cp in mine:
            cp.start()
        sent = []
        for a in range(na):
            sent.append(copy(a, 0, me, sibling, src=stage[a]))
            sent += [copy(a, 1 + j, me, (*chip, c), src=stage[a]) for j, chip in enumerate(near)]
        for cp in sent:
            cp.start()
        for b in range(nb):
            cast[b][...] = cast_block(na + b)
            cp = pltpu.make_async_copy(cast[b], _device_rows(land[b], shapes[na + b][0], *me), local_sems.at[na + b])
            cp.start()
            mine.append(cp)
        for a in range(na):
            copy(a, k_from, (*relay_from, c), me).wait_recv()
            sent += [copy(a, 3, (*relay_from, c), (*relay_to, c)), copy(a, 3 + k_from, (*relay_from, c), sibling)]
            sent[-2].start()
            sent[-1].start()
        for a in range(na):
            copy(a, k_to, (*relay_to, c), me).wait_recv()
            sent.append(copy(a, 3 + k_to, (*relay_to, c), sibling))
            sent[-1].start()
        for a in range(na):
            copy(a, 3, (*far, c), me).wait_recv()
            sent.append(copy(a, 6, (*far, c), sibling))
            sent[-1].start()
        for a in range(na):
            copy(a, 0, sibling, me).wait_recv()
            for j, chip in enumerate(near + [far]):
                copy(a, 4 + j, (*chip, 1 - c), me).wait_recv()
        for cp in sent:
            cp.wait_send()
        for cp in mine:
            cp.wait()

    vmem = pl.BlockSpec(memory_space=pltpu.VMEM)
    hbm = pl.BlockSpec(memory_space=pl.ANY)
    gathered = [jax.ShapeDtypeStruct((N_DEV * m, n), out_dtype) for m, n in shapes]
    out = pl.pallas_call(
        body, name=name,
        in_specs=[hbm] * (na + nb),
        out_specs=[hbm] * na + [vmem] * nb + [hbm] * nb,
        out_shape=gathered[:na] + [jax.ShapeDtypeStruct(s, out_dtype) for s in shapes[na:]] + gathered[na:],
        scratch_shapes=([pltpu.VMEM(s, out_dtype) for s in shapes[:na]]
                        + [pltpu.VMEM(s, a.dtype) for s, (a, _) in zip(shapes, every)]
                        + [pltpu.SemaphoreType.DMA((na, 7)), pltpu.SemaphoreType.DMA((na, 7)),
                           pltpu.SemaphoreType.DMA((na + nb,)), pltpu.SemaphoreType.DMA((na + nb,))]),
        compiler_params=pltpu.CompilerParams(vmem_limit_bytes=VMEM_LIMIT, collective_id=COLLECTIVE_GATHER_W0),
    )(*[a for a, _ in every])
    return out[:na], out[na:na + nb], out[na + nb:]


ALL_PEERS = tuple(range(1, N_DEV))
SIBLING_AND_SAME_CORE = (1, 2, 4, 6)


def _related(k, x, y, c):
    return x ^ ((k >> 2) & 1), y ^ ((k >> 1) & 1), c ^ (k & 1)


def _gather_start(blocks, lands, relations, collective_id, name):
    na = len(blocks)

    def body(*refs):
        src, land, sems, token = refs[:na], refs[na:2 * na], refs[2 * na:4 * na], refs[-1]
        x, y, c = _mesh_pos()
        _handshake([_related(k, x, y, c) for k in sorted(set().union(*relations))])
        for a in range(na):
            for k in relations[a]:
                pltpu.make_async_remote_copy(
                    src_ref=src[a], dst_ref=_device_rows(land[a], blocks[a].shape[0], x, y, c),
                    send_sem=sems[2 * a].at[k - 1], recv_sem=sems[2 * a + 1].at[k - 1],
                    device_id=_related(k, x, y, c), device_id_type=MESH).start()
        token[...] = jnp.zeros_like(token)

    bufs = [pltpu.HBM(t.shape, t.dtype) for t in list(blocks) + list(lands)]
    out = pl.pallas_call(
        body, name=name,
        out_shape=(*([pltpu.SemaphoreType.DMA((N_DEV - 1,))] * (2 * na)), *bufs, jax.ShapeDtypeStruct((8, 128), F32)),
        in_specs=[_HBM] * (2 * na),
        out_specs=(*([_SEM] * (2 * na)), *([_HBM] * (2 * na)), pl.BlockSpec(memory_space=pltpu.VMEM)),
        input_output_aliases={i: 2 * na + i for i in range(2 * na)},
        compiler_params=pltpu.CompilerParams(has_side_effects=_EFFECT, collective_id=collective_id),
    )(*[pltpu.with_memory_space_constraint(t, pltpu.HBM) for t in list(blocks) + list(lands)])
    sems = [(out[2 * a], out[2 * a + 1]) for a in range(na)]
    return sems, out[2 * na:3 * na], out[3 * na:4 * na], out[-1]


def _gather_wait(sems, block, land, relations, after, name):
    def body(src, land_ref, send_sem, recv_sem, after_ref, src_out, land_out):
        x, y, c = _mesh_pos()
        for k in relations:
            peer = _related(k, x, y, c)
            cp = pltpu.make_async_remote_copy(
                src_ref=src, dst_ref=_device_rows(land_ref, block.shape[0], *peer),
                send_sem=send_sem.at[k - 1], recv_sem=recv_sem.at[k - 1], device_id=peer, device_id_type=MESH)
            cp.wait_send()
            cp.wait_recv()

    out = pl.pallas_call(
        body, name=name,
        out_shape=(pltpu.HBM(block.shape, block.dtype), pltpu.HBM(land.shape, land.dtype)),
        in_specs=[_HBM, _HBM, _SEM, _SEM, pl.BlockSpec(memory_space=pl.ANY)],
        out_specs=[_HBM, _HBM],
        input_output_aliases={0: 0, 1: 1},
        compiler_params=pltpu.CompilerParams(has_side_effects=_EFFECT),
    )(block, land, sems[0], sems[1], after)
    return out[1]


(COLLECTIVE_GATHER_W0, COLLECTIVE_GATHER_W1, COLLECTIVE_FORWARD_W_IN1, COLLECTIVE_EXCHANGE_1, COLLECTIVE_EXCHANGE_0A,
 COLLECTIVE_EXCHANGE_0B, COLLECTIVE_GATHER_SMALL) = range(1, 8)
COLLECTIVE_PAIR_SUM = (8, 9)
COLLECTIVE_FORWARD_W_OUT1 = 10


def _handshake(peers):
    barrier = pltpu.get_barrier_semaphore()
    for peer in peers:
        pl.semaphore_signal(barrier, inc=1, device_id=peer, device_id_type=MESH)
    pl.semaphore_wait(barrier, len(peers))


def _forward_plan(land_ref, m):
    x, y, c = _mesh_pos()
    return [_device_rows(land_ref, m, qx, qy, c) for qx, qy in ((1 - x, y), (x, 1 - y), (1 - x, 1 - y))], (x, y, 1 - c)


def _forward_start(land, m, collective_id, name):
    def body(land_ref, send_sem, recv_sem, land_out, token):
        _handshake([(lax.axis_index("x"), lax.axis_index("y"), 1 - lax.axis_index("c"))])
        rows, sibling = _forward_plan(land_ref, m)
        for j, r in enumerate(rows):
            pltpu.make_async_remote_copy(src_ref=r, dst_ref=r, send_sem=send_sem.at[j], recv_sem=recv_sem.at[j],
                                         device_id=sibling, device_id_type=MESH).start()
        token[...] = jnp.zeros_like(token)

    out = pl.pallas_call(
        body, name=name,
        out_shape=(pltpu.SemaphoreType.DMA((3,)), pltpu.SemaphoreType.DMA((3,)), pltpu.HBM(land.shape, land.dtype),
                   jax.ShapeDtypeStruct((8, 128), F32)),
        in_specs=[_HBM],
        out_specs=(_SEM, _SEM, _HBM, pl.BlockSpec(memory_space=pltpu.VMEM)),
        input_output_aliases={0: 2},
        compiler_params=pltpu.CompilerParams(has_side_effects=_EFFECT, collective_id=collective_id),
    )(pltpu.with_memory_space_constraint(land, pltpu.HBM))
    return (out[0], out[1]), out[2], out[3]


def _forward_wait(sems, land, m, after, name):
    def body(land_ref, send_sem, recv_sem, after_ref, land_out):
        x, y, c = _mesh_pos()
        mine, sibling = _forward_plan(land_ref, m)
        theirs = [_device_rows(land_ref, m, qx, qy, 1 - c) for qx, qy in ((1 - x, y), (x, 1 - y), (1 - x, 1 - y))]
        for j in range(3):
            cp = pltpu.make_async_remote_copy(src_ref=mine[j], dst_ref=theirs[j], send_sem=send_sem.at[j],
                                              recv_sem=recv_sem.at[j], device_id=sibling, device_id_type=MESH)
            cp.wait_send()
            cp.wait_recv()

    return pl.pallas_call(
        body, name=name,
        out_shape=pltpu.HBM(land.shape, land.dtype),
        in_specs=[_HBM, _SEM, _SEM, pl.BlockSpec(memory_space=pl.ANY)],
        out_specs=_HBM,
        input_output_aliases={0: 0},
        compiler_params=pltpu.CompilerParams(has_side_effects=_EFFECT),
    )(land, sems[0], sems[1], after)


_HBM = pl.BlockSpec(memory_space=pltpu.HBM)
_SEM = pl.BlockSpec(memory_space=pltpu.SEMAPHORE)
_EFFECT = pltpu.SideEffectType.DATAFLOW_SIDE_EFFECTING


def _exchange_plan(direct):
    x, y, c = _mesh_pos()
    if not direct:
        return [(j, j, (qx, qy, c)) for j, (qx, qy) in enumerate([(1 - x, y), (x, 1 - y), (1 - x, 1 - y)])]
    plan = []
    for k in range(1, N_DEV):
        px, py, pc = x ^ ((k >> 2) & 1), y ^ ((k >> 1) & 1), c ^ (k & 1)
        plan.append((4 * px + 2 * py + pc, k - 1, (px, py, pc)))
    return plan


def _exchange_copies(directs):
    copies, base = [], 0
    for a, direct in enumerate(directs):
        plan = _exchange_plan(direct)
        copies += [(a, block, slot, peer, base + slot) for block, slot, peer in plan]
        base += len(plan)
    return copies, base


def _exchange_start(srcs, directs, collective_id, name):
    na = len(srcs)
    slots = [N_DEV - 1 if direct else 3 for direct in directs]

    def body(*refs):
        src, land = refs[:na], refs[na:2 * na]
        send_sem, recv_sem = refs[2 * na], refs[2 * na + 1]
        token = refs[-1]
        _handshake([peer for _, _, peer in _exchange_plan(any(directs))])
        for a, block, slot, peer, sem in _exchange_copies(directs)[0]:
            pltpu.make_async_remote_copy(
                src_ref=src[a].at[block], dst_ref=land[a].at[slot], send_sem=send_sem.at[sem],
                recv_sem=recv_sem.at[sem], device_id=peer, device_id_type=MESH).start()
        token[...] = jnp.zeros_like(token)

    zones = [jax.ShapeDtypeStruct((n,) + t.shape[1:], t.dtype) for n, t in zip(slots, srcs)]
    bufs = [pltpu.HBM(t.shape, t.dtype) for t in list(srcs) + zones]
    out = pl.pallas_call(
        body, name=name,
        out_shape=(pltpu.SemaphoreType.DMA((sum(slots),)), pltpu.SemaphoreType.DMA((sum(slots),)), *bufs,
                   jax.ShapeDtypeStruct((8, 128), F32)),
        in_specs=[_HBM] * (2 * na),
        out_specs=(_SEM, _SEM, *([_HBM] * (2 * na)), pl.BlockSpec(memory_space=pltpu.VMEM)),
        input_output_aliases={i: 2 + i for i in range(2 * na)},
        compiler_params=pltpu.CompilerParams(has_side_effects=_EFFECT, collective_id=collective_id),
    )(*[pltpu.with_memory_space_constraint(t, pltpu.HBM) for t in srcs],
      *[pltpu.with_memory_space_constraint(lax.empty(t.shape, t.dtype), pltpu.HBM) for t in zones])
    return out[0], out[1], out[2:2 + na], out[2 + na:2 + 2 * na], out[-1]


def _exchange_wait(send_sem, recv_sem, srcs, lands, directs, after, name):
    na = len(srcs)

    def body(*refs):
        src, land = refs[:na], refs[na:2 * na]
        send_sem_ref, recv_sem_ref = refs[2 * na], refs[2 * na + 1]
        for a, block, slot, peer, sem in _exchange_copies(directs)[0]:
            cp = pltpu.make_async_remote_copy(
                src_ref=src[a].at[block], dst_ref=land[a].at[slot], send_sem=send_sem_ref.at[sem],
                recv_sem=recv_sem_ref.at[sem], device_id=peer, device_id_type=MESH)
            cp.wait_send()
            cp.wait_recv()

    bufs = [pltpu.HBM(t.shape, t.dtype) for t in list(srcs) + list(lands)]
    out = pl.pallas_call(
        body, name=name,
        out_shape=tuple(bufs),
        in_specs=[_HBM] * (2 * na) + [_SEM, _SEM, pl.BlockSpec(memory_space=pl.ANY)],
        out_specs=[_HBM] * (2 * na),
        input_output_aliases={i: i for i in range(2 * na)},
        compiler_params=pltpu.CompilerParams(has_side_effects=_EFFECT),
    )(*srcs, *lands, send_sem, recv_sem, after)
    return out[:na], out[na:]


def _own_then_slots(mine_ref, lands_ref, rows=slice(None)):
    if len(mine_ref.shape) == 3:
        x, y, c = _mesh_pos()
        total = mine_ref[4 * x + 2 * y + c, rows, :].astype(F32)
    else:
        total = mine_ref[rows, :].astype(F32)
    for j in range(lands_ref.shape[0]):
        total = total + lands_ref[j, rows, :].astype(F32)
    return total


SMALL_ROWS = 2 * PACK_SLICE + 2 * 8


def _small_gather_start(mine, lands, dgpre, name):
    def body(*refs):
        hm, ld, dg = refs[:DEPTH], refs[DEPTH:2 * DEPTH], refs[2 * DEPTH:3 * DEPTH]
        send_sem, recv_sem, blk, land, token, own, slots, rows, built, local_sems = refs[3 * DEPTH:]
        x, y, c = _mesh_pos()
        loads = []
        for l in range(DEPTH):
            loads += [pltpu.make_async_copy(hm[l].at[4 * x + 2 * y + c], own.at[l], local_sems.at[3 * l]),
                      pltpu.make_async_copy(ld[l], slots.at[l], local_sems.at[3 * l + 1]),
                      pltpu.make_async_copy(dg[l], rows.at[l], local_sems.at[3 * l + 2])]
        for cp in loads:
            cp.start()
        _handshake([_related(k, x, y, c) for k in ALL_PEERS])
        for cp in loads:
            cp.wait()
        for l in range(DEPTH):
            total = own[l]
            for j in range(N_DEV - 1):
                total = total + slots[l, j]
            built[PACK_SLICE * l:PACK_SLICE * (l + 1), :] = total
            built[2 * PACK_SLICE + 8 * l:2 * PACK_SLICE + 8 * (l + 1), :] = rows[l]
        stores = [pltpu.make_async_copy(built, blk, local_sems.at[3 * DEPTH]),
                  pltpu.make_async_copy(built, _device_rows(land, SMALL_ROWS, x, y, c), local_sems.at[3 * DEPTH + 1])]
        for cp in stores:
            cp.start()
        for cp in stores:
            cp.wait()
        for k in ALL_PEERS:
            pltpu.make_async_remote_copy(
                src_ref=blk, dst_ref=_device_rows(land, SMALL_ROWS, x, y, c), send_sem=send_sem.at[k - 1],
                recv_sem=recv_sem.at[k - 1], device_id=_related(k, x, y, c), device_id_type=MESH).start()
        token[...] = jnp.zeros_like(token)

    hbm = pl.BlockSpec(memory_space=pl.ANY)
    out = pl.pallas_call(
        body, name=name,
        in_specs=[hbm] * (3 * DEPTH),
        out_specs=(_SEM, _SEM, _HBM, _HBM, pl.BlockSpec(memory_space=pltpu.VMEM)),
        out_shape=(pltpu.SemaphoreType.DMA((N_DEV - 1,)), pltpu.SemaphoreType.DMA((N_DEV - 1,)),
                   pltpu.HBM((SMALL_ROWS, 128), F32), pltpu.HBM((N_DEV * SMALL_ROWS, 128), F32),
                   jax.ShapeDtypeStruct((8, 128), F32)),
        scratch_shapes=[pltpu.VMEM((DEPTH, PACK_SLICE, 128), F32), pltpu.VMEM((DEPTH, N_DEV - 1, PACK_SLICE, 128), F32),
                        pltpu.VMEM((DEPTH, 8, 128), F32), pltpu.VMEM((SMALL_ROWS, 128), F32),
                        pltpu.SemaphoreType.DMA((3 * DEPTH + 2,))],
        compiler_params=pltpu.CompilerParams(has_side_effects=_EFFECT, collective_id=COLLECTIVE_GATHER_SMALL),
    )(*mine, *lands, *dgpre)
    return (out[0], out[1]), out[2], out[3], out[4]


def _adamw_math(w, g, m, v):
    m = ADAM_B1 * m + (1.0 - ADAM_B1) * g
    v = ADAM_B2 * v + (1.0 - ADAM_B2) * (g * g)
    m_hat = m / (1.0 - ADAM_B1 ** ADAM_STEP)
    v_hat = v / (1.0 - ADAM_B2 ** ADAM_STEP)
    delta = -ADAM_LR * (m_hat / (jnp.sqrt(v_hat) + ADAM_EPS) + ADAM_WD * w)
    return delta, m, v


def _adamw_layer(layer, parts, token, name):
    steps = {w.shape[1] // rows for _, _, w, _, _, _, rows in parts}
    assert len(steps) == 1, steps
    n = len(parts)

    def body(_, *refs):
        for p in range(n):
            hm_ref, ld_ref, w_ref, m_ref, v_ref = refs[5 * p:5 * (p + 1)]
            g_ref, d_ref, nm_ref, nv_ref = refs[len(refs) - 4 * (n - p):len(refs) - 4 * (n - p - 1)]
            g = _own_then_slots(hm_ref, ld_ref)
            g_ref[...] = g
            d, nm, nv = _adamw_math(w_ref[...], g, m_ref[...], v_ref[...])
            d_ref[...] = d
            nm_ref[...] = nm
            nv_ref[...] = nv

    in_specs, out_specs, out_shape, operands, carried, aliases = [pl.BlockSpec(memory_space=pl.ANY)], [], [], [], [], {}
    for p, (mine, lands, w, m, v, earlier, rows) in enumerate(parts):
        nn = w.shape[2]
        spec = pl.BlockSpec((None, rows, nn), lambda i: (layer, i, 0))
        in_specs += [pl.BlockSpec((rows, nn), lambda i: (i, 0)) if mine.ndim == 2
                     else pl.BlockSpec((N_DEV, rows, nn), lambda i: (0, i, 0)),
                     pl.BlockSpec((lands.shape[0], rows, nn), lambda i: (0, i, 0)), spec, spec, spec]
        out_specs += [spec] * 4
        out_shape += [jax.ShapeDtypeStruct(w.shape, F32)] * 4
        operands += [mine, lands, w, m, v]
        if earlier is not None:
            aliases.update({1 + 5 * n + len(carried) + t: 4 * p + t for t in range(4)})
            carried += list(earlier)
    out = pl.pallas_call(
        body, name=name, grid=(steps.pop(),),
        in_specs=in_specs + [pl.BlockSpec(memory_space=pl.ANY)] * len(carried),
        out_specs=out_specs, out_shape=out_shape, input_output_aliases=aliases,
        compiler_params=_compiler_params(("arbitrary",)),
    )(token, *operands, *carried)
    return [out[4 * p:4 * (p + 1)] for p in range(n)]


def _adamw_small(gathered, params):
    def body(all_ref, *refs):
        ins, outs, packs = refs[:15], refs[15:15 + 21], refs[15 + 21]
        loss_ref = outs[0]
        for dev in range(N_DEV):
            for l in range(DEPTH):
                packs[l, PACK_SLICE * dev:PACK_SLICE * (dev + 1), :] = (
                    all_ref[SMALL_ROWS * dev + PACK_SLICE * l:SMALL_ROWS * dev + PACK_SLICE * (l + 1), :])
        loss_ref[...] = packs[DEPTH - 1, ROW_LOSS:ROW_LOSS + 1, 0:1]

        def update(p, sel, g):
            w_ref, m_ref, v_ref = ins[p], ins[5 + p], ins[10 + p]
            d, nm, nv = _adamw_math(w_ref[sel], g, m_ref[sel], v_ref[sel])
            for t, val in enumerate((g, d, nm, nv)):
                outs[1 + 5 * t + p][sel] = val

        for l in range(DEPTH):
            gp = packs.at[l]
            row0 = 2 * PACK_SLICE + 8 * l
            dgpre = all_ref[row0:row0 + 8, :]
            for dev in range(1, N_DEV):
                dgpre = dgpre + all_ref[SMALL_ROWS * dev + row0:SMALL_ROWS * dev + row0 + 8, :]
            for grp in range(4):
                update(0, (l, grp), gp[ROW_PW + BLOCK * grp:ROW_PW + BLOCK * (grp + 1), :])
                update(1, (slice(l, l + 1), slice(128 * grp, 128 * (grp + 1))), gp[ROW_SC + grp:ROW_SC + grp + 1, :])
            update(2, (slice(l, l + 1), slice(None)), gp[ROW_SINK:ROW_SINK + 1, 0:N_HEADS])
            for r in range(D_MODEL // 128):
                sel = (slice(l, l + 1), slice(128 * r, 128 * (r + 1)))
                update(3, sel, dgpre[r:r + 1, :])
                update(4, sel, gp[ROW_NPOST + r:ROW_NPOST + r + 1, :])

    shapes = [jax.ShapeDtypeStruct(p.shape, F32) for p in params[:5]]
    return pl.pallas_call(
        body, name="adamw_small",
        out_shape=[jax.ShapeDtypeStruct((1, 1), F32)] + shapes * 4,
        scratch_shapes=[pltpu.VMEM((DEPTH, PACK_ROWS, 128), F32)],
        compiler_params=_compiler_params(),
    )(gathered, *params)


def kernel(x, w_in, pool_w, pool_scale, attn_sinks, w_out, norm_pre, norm_post, loss_target, m_w_in, m_pool_w, m_pool_scale, m_attn_sinks, m_w_out, m_norm_pre, m_norm_post, v_w_in, v_pool_w, v_pool_scale, v_attn_sinks, v_w_out, v_norm_pre, v_norm_post):
    x0 = x.reshape(SEQ, D_MODEL)
    target = loss_target.reshape(SEQ, D_MODEL)
    bias = jnp.asarray(_attn_bias())
    w_in_t, m_in_t, v_in_t = (jnp.swapaxes(t, 1, 2) for t in (w_in, m_w_in, v_w_in))

    (win0, wout0), later, lands = _allgather([(w_in_t, 0), (w_out, 0)], BF16, "gather_w0",
                                              later=[(w_in_t, 1), (w_out, 1)])
    sems, later, lands, token = _gather_start(later, lands, [SIBLING_AND_SAME_CORE] * 2, COLLECTIVE_GATHER_W1,
                                              "gather_w1_start")
    win_full, wout_full = [win0, None], [wout0, None]

    saved = []
    xl = x0
    for layer in range(DEPTH):
        front = (layer, xl, norm_pre, win_full[layer], token, attn_sinks, pool_w, pool_scale, bias)
        if layer == 0:
            u, pg, q, k, v, ag, z, a = _fwd_front(*front)
            land = _gather_wait(sems[0], later[0], lands[0], SIBLING_AND_SAME_CORE, z, "gather_w_in1_wait")
            fsems, land, token = _forward_start(land, IN_SHARD, COLLECTIVE_FORWARD_W_IN1, "forward_w_in1_start")
            x_next, y = _fwd_out(layer, z, xl, norm_post, wout_full[layer], token)
            win_full[1] = _forward_wait(fsems, land, IN_SHARD, x_next, "forward_w_in1_wait")
            land = _gather_wait(sems[1], later[1], lands[1], SIBLING_AND_SAME_CORE, win_full[1], "gather_w_out1_wait")
            fsems, land, token = _forward_start(land, OUT_SHARD, COLLECTIVE_FORWARD_W_OUT1, "forward_w_out1_start")
            wout_full[1] = _forward_wait(fsems, land, OUT_SHARD, token, "forward_w_out1_wait")
        else:
            u, pg, q, k, v, ag, z, a, x_next, y = _fwd_front(*front, out=(norm_post, wout_full[layer]))
        saved.append((xl, u, pg, q, k, v, ag, z, a, y))
        xl = x_next

    params_small = [pool_w, pool_scale, attn_sinks, norm_pre, norm_post,
                    m_pool_w, m_pool_scale, m_attn_sinks, m_norm_pre, m_norm_post,
                    v_pool_w, v_pool_scale, v_attn_sinks, v_norm_pre, v_norm_post]

    def start(srcs, directs, paired, collective_id, tag):
        send_sem, recv_sem, srcs, lands, started = _exchange_start(srcs, directs, collective_id, f"exchange_start{tag}")
        return (send_sem, recv_sem, srcs, lands, paired, directs), started

    def finish(handle, after, tag):
        send_sem, recv_sem, srcs, lands, paired, directs = handle
        srcs, lands = _exchange_wait(send_sem, recv_sem, srcs, lands, directs, after, f"exchange_wait{tag}")
        return [s if p is None else p for s, p in zip(srcs, paired)], lands

    def back(layer, top, first, second):
        xin, u, pg, q, k, v, ag, z, a, y = saved[layer]
        return _bwd_back(layer, top, first, second, y, z, norm_post, wout_full[layer], attn_sinks, u, pg, q, k, v,
                         ag, a, pool_w, pool_scale, bias)

    dgpre = [None] * DEPTH
    dx, dproj, gw_out, pack = back(1, True, xl, target)
    dx, dgpre[1], chip_sums, own_sum = _bwd_in(1, "both_pair", token, dproj, saved[1][0], norm_pre, dx, win_full[1])
    top, token = start([chip_sums, gw_out.reshape(N_DEV, OUT_SHARD, D_MODEL), pack.reshape(N_DEV, PACK_SLICE, 128)],
                       [False, True, True], [own_sum, None, None], COLLECTIVE_EXCHANGE_1, "1")

    dproj, gw_out, pack = back(0, False, dx, token)
    early, token = start([gw_out.reshape(N_DEV, OUT_SHARD, D_MODEL), pack.reshape(N_DEV, PACK_SLICE, 128)],
                         [True, True], [None, None], COLLECTIVE_EXCHANGE_0A, "0a")
    chip_sums, own_sum = _bwd_in(0, "dw_pair", token, dproj, saved[0][0], norm_pre)
    late, token = start([chip_sums], [False], [own_sum], COLLECTIVE_EXCHANGE_0B, "0b")
    dx, dgpre[0] = _bwd_in(0, "dx", token, dproj, saved[0][0], norm_pre, dx, win_full[0])

    own1, lands1 = finish(top, dx, "1")
    big_in, big_out = _adamw_layer(1, [(own1[0], lands1[0], w_in_t, m_in_t, v_in_t, None, ADAM_ROWS_IN),
                                       (own1[1], lands1[1], w_out, m_w_out, v_w_out, None, ADAM_ROWS_OUT)],
                                   token, "adamw1")
    own0a, lands0a = finish(early, big_out[0], "0a")
    sems, block, land, token = _small_gather_start([own0a[1], own1[2]], [lands0a[1], lands1[2]], dgpre,
                                                   "gather_small_start")
    own0b, lands0b = finish(late, token, "0b")
    big_in, big_out = _adamw_layer(0, [(own0b[0], lands0b[0], w_in_t, m_in_t, v_in_t, big_in, ADAM_ROWS_IN),
                                       (own0a[0], lands0a[0], w_out, m_w_out, v_w_out, big_out, ADAM_ROWS_OUT)],
                                   token, "adamw0")
    gathered = _gather_wait(sems, block, land, ALL_PEERS, big_in[0], "gather_small_wait")
    small_out = _adamw_small(gathered, params_small)
    loss = small_out[0].reshape(())

    outs = [loss, dx.reshape(1, SEQ, D_MODEL)]
    for t in range(4):
        pw_, sc_, sk_, npre_, npost_ = small_out[1 + 5 * t:6 + 5 * t]
        outs += [jnp.swapaxes(big_in[t], 1, 2), pw_, sc_, sk_, big_out[t], npre_, npost_]
    return tuple(outs)
```

```python
import numpy as np
import jax
import jax.numpy as jnp
from jax import lax
from jax.experimental import pallas as pl
from jax.experimental.pallas import tpu as pltpu

F32 = jnp.float32
BF16 = jnp.bfloat16

N_DEV = 8
SEQ = 2048
D_MODEL = 1024
D_POOL = 512
D_ATTN = 512
D_KV = 128
D_IN = 2304
N_HEADS = 8
GQA = 4
HEAD_DIM = 64
BLOCK = 128
POOL_WINDOWS = (2, 4, 8, 16)
DEPTH = 2
EPS = 1e-6
NEG_INF = -1e30
SCALE = HEAD_DIM ** -0.5
IN_SHARD = D_IN // N_DEV
OUT_SHARD = D_MODEL // N_DEV

COL_U, COL_PG, COL_Q, COL_K, COL_V, COL_AG = 0, 512, 1024, 1536, 1664, 1792

ADAM_LR = 0.001
ADAM_B1 = 0.9
ADAM_B2 = 0.999
ADAM_EPS = 1e-08
ADAM_WD = 0.01
ADAM_STEP = 10

TOKEN_TILE = 512
FWD_OUT_TILE = 1024
ADAM_ROWS_IN, ADAM_ROWS_OUT = 144, 64
VMEM_LIMIT = 56 * 1024 * 1024
MESH = pl.DeviceIdType.MESH

ROW_PW, ROW_SC, ROW_SINK, ROW_NPOST, ROW_LOSS = 0, 512, 520, 536, 544
PACK_ROWS = 576
PACK_SLICE = PACK_ROWS // N_DEV


def _nn(a, b):
    return jnp.dot(a, b, preferred_element_type=F32)


def _nt(a, b):
    return lax.dot_general(a, b, (((1,), (1,)), ((), ())), preferred_element_type=F32)


def _tn(a, b):
    return lax.dot_general(a, b, (((0,), (0,)), ((), ())), preferred_element_type=F32)


def _silu_parts(g):
    s = jax.nn.sigmoid(g)
    return g * s, s * (1.0 + g * (1.0 - s))


def _resident(shape):
    return pl.BlockSpec(shape, lambda *_: (0,) * len(shape), pipeline_mode=pl.Buffered(1))


def _compiler_params(sem=None):
    if sem is None:
        return pltpu.CompilerParams(vmem_limit_bytes=VMEM_LIMIT)
    return pltpu.CompilerParams(dimension_semantics=sem, vmem_limit_bytes=VMEM_LIMIT)


def _attn_bias():
    t = np.arange(BLOCK)[None, :]
    j = np.arange(BLOCK)[:, None]
    current = j <= t
    dist = np.where(current, t - j, t + BLOCK - j).astype(np.float32)
    out = np.zeros((2, 2, BLOCK, GQA * BLOCK), np.float32)
    for variant in range(2):
        valid = current | (variant == 1)
        for kv in range(2):
            for g in range(GQA):
                slope = np.float32(2.0 ** (-(kv * GQA + g + 1)))
                out[variant, kv, :, g * BLOCK:(g + 1) * BLOCK] = np.where(valid, -slope * dist, np.float32(NEG_INF))
    return out


def _replicate_head(kx, kv):
    rolled = pltpu.roll(kx, 64, 1)
    lane = lax.broadcasted_iota(jnp.int32, kx.shape, 1)
    dup = jnp.where(lane < 64, kx, rolled) if kv == 0 else jnp.where(lane < 64, rolled, kx)
    return jnp.concatenate([dup, dup], axis=1).astype(BF16)


def _stack_heads(qv):
    lane = lax.broadcasted_iota(jnp.int32, qv.shape, 1)
    zero = jnp.zeros_like(qv)
    return jnp.concatenate([jnp.where((lane >= 64 * g) & (lane < 64 * g + 64), qv, zero) for g in range(GQA)], axis=0)


def _unstack_heads(xs):
    lane = lax.broadcasted_iota(jnp.int32, (BLOCK, 256), 1)
    return jnp.where(lane < 64, xs[0:128], jnp.where(lane < 128, xs[128:256], jnp.where(lane < 192, xs[256:384], xs[384:512])))


def _fold_heads(r):
    h = r[:, 0:128] + r[:, 128:256]
    return h + pltpu.roll(h, 64, 1)


def _sink_row(sink_ref, layer, kv):
    lane = lax.broadcasted_iota(jnp.int32, (1, GQA * BLOCK), 1)
    s4 = [sink_ref[layer, kv * GQA + g] for g in range(GQA)]
    return jnp.where(lane < 128, s4[0], jnp.where(lane < 256, s4[1], jnp.where(lane < 384, s4[2], s4[3])))


def _band_is_current():
    j = lax.broadcasted_iota(jnp.int32, (BLOCK, GQA * BLOCK), 0)
    t = lax.broadcasted_iota(jnp.int32, (BLOCK, GQA * BLOCK), 1) & (BLOCK - 1)
    return j <= t


def _pack_band(full, current):
    return jnp.where(current, full[BLOCK:], full[:BLOCK])


def _unpack_band(packed, current):
    zero = jnp.zeros_like(packed)
    return jnp.concatenate([jnp.where(current, zero, packed), jnp.where(current, packed, zero)], axis=0)


def _probs_keys_major(k_rep, q_st, bias, sink, current):
    st = _pack_band(_nt(k_rep, q_st), current) * SCALE + bias
    m = jnp.maximum(jnp.max(st, axis=0, keepdims=True), sink)
    p = jnp.exp(st - m)
    esink = jnp.exp(sink - m)
    rl = 1.0 / (jnp.sum(p, axis=0, keepdims=True) + esink)
    return p * rl, esink * rl


WINDOW_HALO = 16


def _window_sum(ext, w, forward):
    s = ext
    sh = 1
    while sh < w:
        s = s + pltpu.roll(s, (ext.shape[0] - sh) if forward else sh, 0)
        sh *= 2
    return s


def _inv_count(n, w):
    t = n * BLOCK + lax.broadcasted_iota(jnp.int32, (BLOCK, 1), 0) + 1
    return 1.0 / jnp.minimum(t.astype(F32), float(w))


def _kv_ext(ref, n):
    r0 = pl.multiple_of(jnp.maximum(n - 1, 0) * BLOCK, BLOCK)
    r1 = pl.multiple_of(n * BLOCK, BLOCK)
    return jnp.concatenate([ref[pl.ds(r0, BLOCK), :], ref[pl.ds(r1, BLOCK), :]], axis=0)


def _rows_of(vec_ref, pack_ref, row0):
    for r in range(D_MODEL // 128):
        pack_ref[row0 + r:row0 + r + 1, :] = vec_ref[:, 128 * r:128 * (r + 1)]


FRONT_TILE = 4 * BLOCK


def _fwd_front(layer, x, norm_pre, w_in_t, token, sinks, pool_w, pool_scale, bias, out=None):
    tm = FRONT_TILE

    def body(sink_ref, x_ref, g_ref, w_ref, _, pw_ref, sc_ref, bias_ref, *refs):
        if out is not None:
            gpost_ref, wo_ref, *refs = refs
            xn_ref, y_ref = refs[8:10]
        u_ref, pg_ref, q_ref, k_ref, v_ref, ag_ref, z_ref, a_ref = refs[:8]
        uprev, kprev, vprev = refs[-3:]
        i = pl.program_id(0)

        @pl.when(i == 0)
        def _():
            uprev[...] = jnp.zeros_like(uprev)
            kprev[...] = jnp.zeros_like(kprev)
            vprev[...] = jnp.zeros_like(vprev)

        xv = x_ref[...]
        r = lax.rsqrt(jnp.mean(xv * xv, axis=-1, keepdims=True) + EPS)
        h = (xv * r * g_ref[layer:layer + 1, :]).astype(BF16)
        u_ref[...] = _nt(h, w_ref[COL_U:COL_PG, :])
        pg_ref[...] = _nt(h, w_ref[COL_PG:COL_Q, :])
        for sb in range(tm // BLOCK):
            n = (tm // BLOCK) * i + sb
            rows = slice(BLOCK * sb, BLOCK * (sb + 1))
            before = slice(BLOCK * (sb - 1), BLOCK * sb)
            uv = u_ref[rows, :]
            halo = (uprev[BLOCK - WINDOW_HALO:, :] if sb == 0
                    else u_ref[BLOCK * sb - WINDOW_HALO:BLOCK * sb, :])
            ext = jnp.concatenate([halo, uv], axis=0)
            for g, w in enumerate(POOL_WINDOWS):
                cs = slice(BLOCK * g, BLOCK * (g + 1))
                win = _window_sum(ext[:, cs], w, forward=False)[WINDOW_HALO:]
                pooled = win * _inv_count(n, w) - uv[:, cs]
                mixed = _nn(pooled.astype(BF16), pw_ref[g].astype(BF16))
                gate, _ = _silu_parts(pg_ref[rows, cs])
                z_ref[rows, cs] = (mixed * sc_ref[layer:layer + 1, cs] * gate).astype(BF16)

        q_ref[...] = _nt(h, w_ref[COL_Q:COL_K, :]).astype(BF16)
        k_ref[...] = _nt(h, w_ref[COL_K:COL_V, :])
        v_ref[...] = _nt(h, w_ref[COL_V:COL_AG, :])
        ag_ref[...] = _nt(h, w_ref[COL_AG:D_IN, :])

        current = _band_is_current()
        for sb in range(tm // BLOCK):
            n = (tm // BLOCK) * i + sb
            rows = slice(BLOCK * sb, BLOCK * (sb + 1))
            before = slice(BLOCK * (sb - 1), BLOCK * sb)
            kx = jnp.concatenate([kprev[...] if sb == 0 else k_ref[before, :], k_ref[rows, :]], axis=0)
            vx = jnp.concatenate([vprev[...] if sb == 0 else v_ref[before, :], v_ref[rows, :]], axis=0)
            variant = jnp.minimum(n, 1) if sb == 0 else 1
            for kv in range(2):
                cs = slice(256 * kv, 256 * (kv + 1))
                p, _ = _probs_keys_major(_replicate_head(kx, kv), _stack_heads(q_ref[rows, cs]),
                                         bias_ref[variant, kv], _sink_row(sink_ref, layer, kv), current)
                o = _unstack_heads(_tn(_unpack_band(p.astype(BF16), current), _replicate_head(vx, kv)))
                a_ref[rows, cs] = o
                gate, _ = _silu_parts(ag_ref[rows, cs])
                z_ref[rows, D_POOL + 256 * kv:D_POOL + 256 * (kv + 1)] = (o * gate).astype(BF16)

        tail = slice(tm - BLOCK, tm)
        uprev[...] = u_ref[tail, :]
        kprev[...] = k_ref[tail, :]
        vprev[...] = v_ref[tail, :]

        if out is not None:
            y = _nn(z_ref[...], wo_ref[...])
            y_ref[...] = y
            r = lax.rsqrt(jnp.mean(y * y, axis=-1, keepdims=True) + EPS)
            xn_ref[...] = x_ref[...] + y * r * gpost_ref[layer:layer + 1, :]

    row = lambda c: pl.BlockSpec((tm, c), lambda i: (i, 0))
    const = lambda shape: pl.BlockSpec(shape, lambda i: (0,) * len(shape))
    act = jax.ShapeDtypeStruct((SEQ, D_MODEL), F32)
    fused = out is not None
    return pl.pallas_call(
        body, name=f"fwd_front{layer}", grid=(SEQ // tm,),
        in_specs=[pl.BlockSpec(memory_space=pltpu.SMEM), row(D_MODEL), const((DEPTH, D_MODEL)),
                  _resident((D_IN, D_MODEL)), const((8, 128)),
                  pl.BlockSpec((None, 4, BLOCK, BLOCK), lambda i: (layer, 0, 0, 0)), const((DEPTH, D_POOL)),
                  _resident((2, 2, BLOCK, GQA * BLOCK))]
                 + ([const((DEPTH, D_MODEL)), _resident((D_MODEL, D_MODEL))] if fused else []),
        out_specs=[row(D_POOL), row(D_POOL), row(D_ATTN), row(D_KV), row(D_KV), row(D_ATTN), row(D_MODEL),
                   row(D_ATTN)] + ([row(D_MODEL)] * 2 if fused else []),
        out_shape=[jax.ShapeDtypeStruct((SEQ, D_POOL), F32), jax.ShapeDtypeStruct((SEQ, D_POOL), F32),
                   jax.ShapeDtypeStruct((SEQ, D_ATTN), BF16), jax.ShapeDtypeStruct((SEQ, D_KV), F32),
                   jax.ShapeDtypeStruct((SEQ, D_KV), F32), jax.ShapeDtypeStruct((SEQ, D_ATTN), F32),
                   jax.ShapeDtypeStruct((SEQ, D_MODEL), BF16), jax.ShapeDtypeStruct((SEQ, D_ATTN), F32)]
                  + ([act] * 2 if fused else []),
        scratch_shapes=[pltpu.VMEM((BLOCK, D_POOL), F32), pltpu.VMEM((BLOCK, D_KV), F32),
                        pltpu.VMEM((BLOCK, D_KV), F32)],
        compiler_params=_compiler_params(("arbitrary",)),
    )(sinks, x, norm_pre, w_in_t, token, pool_w, pool_scale, bias, *(out if fused else ()))


def _fwd_out(layer, z, x, norm_post, w_out, token):
    tm = FWD_OUT_TILE

    def body(z_ref, x_ref, g_ref, w_ref, _, xn_ref, y_ref):
        y = _nn(z_ref[...], w_ref[...])
        y_ref[...] = y
        r = lax.rsqrt(jnp.mean(y * y, axis=-1, keepdims=True) + EPS)
        xn_ref[...] = x_ref[...] + y * r * g_ref[layer:layer + 1, :]

    row = lambda c: pl.BlockSpec((tm, c), lambda i: (i, 0))
    return pl.pallas_call(
        body, name=f"fwd_out{layer}", grid=(SEQ // tm,),
        in_specs=[row(D_MODEL), row(D_MODEL), pl.BlockSpec((DEPTH, D_MODEL), lambda i: (0, 0)),
                  _resident((D_MODEL, D_MODEL)), pl.BlockSpec((8, 128), lambda i: (0, 0))],
        out_specs=[row(D_MODEL), row(D_MODEL)],
        out_shape=[jax.ShapeDtypeStruct((SEQ, D_MODEL), F32), jax.ShapeDtypeStruct((SEQ, D_MODEL), F32)],
        compiler_params=_compiler_params(("arbitrary",)),
    )(z, x, norm_post, w_out, token)


BACK_TILE = 2 * BLOCK


def _bwd_back(layer, top, dxo_or_xf, target_or_token, y, z, norm_post, w_out, sinks, u, pg, q, k, v, ag, a,
              pool_w, pool_scale, bias):
    tm = BACK_TILE
    steps = SEQ // tm
    last = steps - 1
    per = tm // BLOCK

    def body(*refs):
        refs = list(refs)
        sink_ref, first, second = refs[:3]
        (y_ref, z_ref, g_ref, w_ref, u_ref, up_ref, pg_ref, q_ref, k_ref, v_ref, ag_ref, a_ref, pw_ref, sc_ref,
         bias_ref) = refs[3:18]
        del refs[:18]
        dxo_ref = refs.pop(0) if top else None
        dp_ref, dw_ref, pack_ref, acc, dg, lacc, dzs, ck, cv, ce = refs
        i = pl.program_id(0)
        blk = last - i

        @pl.when(i == 0)
        def _():
            acc[...] = jnp.zeros_like(acc)
            dg[...] = jnp.zeros_like(dg)
            lacc[...] = jnp.zeros_like(lacc)
            pack_ref[...] = jnp.zeros_like(pack_ref)
            ck[...] = jnp.zeros_like(ck)
            cv[...] = jnp.zeros_like(cv)
            ce[...] = jnp.zeros_like(ce)

        if top:
            d = first[...] - second[...]
            dxo_v = d * (1.0 / D_MODEL)
            dxo_ref[...] = dxo_v
            part = jnp.sum(d * d, axis=-1, keepdims=True) * (1.0 / D_MODEL)
            lacc[...] += 0.5 * jnp.sum(part, axis=0, keepdims=True)
        else:
            dxo_v = first[...]
        yv = y_ref[...]
        r = lax.rsqrt(jnp.mean(yv * yv, axis=-1, keepdims=True) + EPS)
        yn = yv * r
        dg[...] += jnp.sum(dxo_v * yn, axis=0, keepdims=True)
        dyn = dxo_v * g_ref[layer:layer + 1, :]
        dy = (r * (dyn - yn * jnp.mean(dyn * yn, axis=-1, keepdims=True))).astype(BF16)
        dzs[...] = _nt(dy, w_ref[...])
        acc[...] += _tn(z_ref[...], dy)

        lane = lax.broadcasted_iota(jnp.int32, (1, 128), 1)
        lane2 = lax.broadcasted_iota(jnp.int32, (256, 128), 1)
        current = _band_is_current()
        for sb in reversed(range(per)):
            n = per * blk + sb
            rows = slice(BLOCK * sb, BLOCK * (sb + 1))

            uv = u_ref[rows, :]
            if sb == 0:
                halo = up_ref[BLOCK - WINDOW_HALO:, :] * (n > 0).astype(F32)
            else:
                halo = u_ref[BLOCK * sb - WINDOW_HALO:BLOCK * sb, :]
            ext = jnp.concatenate([halo, uv], axis=0)
            for g, w in enumerate(POOL_WINDOWS):
                cs = slice(BLOCK * g, BLOCK * (g + 1))
                inv = _inv_count(n, w)
                win = _window_sum(ext[:, cs], w, forward=False)[WINDOW_HALO:]
                pooled = win * inv - uv[:, cs]
                pw_g = pw_ref[g].astype(BF16)
                mixed = _nn(pooled.astype(BF16), pw_g)
                gate, dgate = _silu_parts(pg_ref[rows, cs])
                dzp = dzs[rows, cs]
                sc = sc_ref[layer:layer + 1, cs]
                dpm = dzp * gate
                dp_ref[rows, COL_PG + BLOCK * g:COL_PG + BLOCK * (g + 1)] = (dzp * (mixed * sc) * dgate).astype(BF16)
                pack_ref[ROW_SC + g:ROW_SC + g + 1, :] += jnp.sum(dpm * mixed, axis=0, keepdims=True)
                dmixed = (dpm * sc).astype(BF16)
                pack_ref[ROW_PW + BLOCK * g:ROW_PW + BLOCK * (g + 1), :] += _tn(pooled.astype(BF16), dmixed)
                dpooled = _nt(dmixed, pw_g)
                e = dpooled * inv
                lead = _window_sum(jnp.concatenate([e, ce[:WINDOW_HALO, cs]], axis=0), w, forward=True)[:BLOCK]
                dp_ref[rows, COL_U + BLOCK * g:COL_U + BLOCK * (g + 1)] = (lead - dpooled).astype(BF16)
                ce[:, cs] = e

            kx = _kv_ext(k_ref, n)
            vx = _kv_ext(v_ref, n)
            variant = jnp.minimum(n, 1) if sb == 0 else 1
            dsink_row = jnp.zeros((1, 128), F32)
            tks, tvs = [], []
            for kv in range(2):
                cs = slice(256 * kv, 256 * (kv + 1))
                k_rep = _replicate_head(kx, kv)
                v_rep = _replicate_head(vx, kv)
                q_st = _stack_heads(q_ref[rows, cs])
                gate, dgate = _silu_parts(ag_ref[rows, cs])
                dza = dzs[rows, D_POOL + 256 * kv:D_POOL + 256 * (kv + 1)]
                dp_ref[rows, COL_AG + 256 * kv:COL_AG + 256 * (kv + 1)] = (dza * a_ref[rows, cs] * dgate).astype(BF16)
                da_st = _stack_heads((dza * gate).astype(BF16))
                p, psink = _probs_keys_major(k_rep, q_st, bias_ref[variant, kv], _sink_row(sink_ref, layer, kv),
                                             current)
                dpt = _pack_band(_nt(v_rep, da_st), current)
                delta = jnp.sum(p * dpt, axis=0, keepdims=True)
                dst = _unpack_band((p * (dpt - delta) * SCALE).astype(BF16), current)
                sink_terms = psink * delta
                for g in range(GQA):
                    dsink = -jnp.sum(sink_terms[:, BLOCK * g:BLOCK * (g + 1)], axis=1, keepdims=True)
                    dsink_row = dsink_row + jnp.where(lane == kv * GQA + g, dsink, 0.0)
                dp_ref[rows, COL_Q + 256 * kv:COL_Q + 256 * (kv + 1)] = _unstack_heads(_tn(dst, k_rep)).astype(BF16)
                tks.append(_fold_heads(_nn(dst, q_st)))
                tvs.append(_fold_heads(_nn(_unpack_band(p.astype(BF16), current), da_st)))
            pack_ref[ROW_SINK:ROW_SINK + 1, :] += dsink_row
            dkx = jnp.where(lane2 < 64, tks[0], tks[1])
            dvx = jnp.where(lane2 < 64, tvs[0], tvs[1])
            dp_ref[rows, COL_K:COL_V] = (ck[...] + dkx[BLOCK:]).astype(BF16)
            dp_ref[rows, COL_V:COL_AG] = (cv[...] + dvx[BLOCK:]).astype(BF16)
            ck[...] = dkx[:BLOCK]
            cv[...] = dvx[:BLOCK]

        @pl.when(i == steps - 1)
        def _():
            dw_ref[...] = acc[...].astype(BF16)
            _rows_of(dg, pack_ref, ROW_NPOST)
            pack_ref[ROW_LOSS:ROW_LOSS + 1, :] = jnp.where(lane == 0, lacc[...], 0.0)

    row = lambda c: pl.BlockSpec((tm, c), lambda i: (last - i, 0))
    const = lambda shape: pl.BlockSpec(shape, lambda i: (0,) * len(shape))
    act = jax.ShapeDtypeStruct((SEQ, D_MODEL), F32)
    return pl.pallas_call(
        body, name=f"bwd_back{layer}", grid=(steps,),
        in_specs=[pl.BlockSpec(memory_space=pltpu.SMEM), row(D_MODEL), row(D_MODEL) if top else const((8, 128)),
                  row(D_MODEL), row(D_MODEL), const((DEPTH, D_MODEL)), _resident((D_MODEL, D_MODEL)),
                  row(D_POOL), pl.BlockSpec((BLOCK, D_POOL), lambda i: (jnp.maximum(per * (last - i) - 1, 0), 0)),
                  row(D_POOL), row(D_ATTN), _resident((SEQ, D_KV)), _resident((SEQ, D_KV)), row(D_ATTN), row(D_ATTN),
                  pl.BlockSpec((None, 4, BLOCK, BLOCK), lambda i: (layer, 0, 0, 0)), const((DEPTH, D_POOL)),
                  _resident((2, 2, BLOCK, GQA * BLOCK))],
        out_specs=([row(D_MODEL)] * (1 if top else 0)
                   + [row(D_IN), const((D_MODEL, D_MODEL)), const((PACK_ROWS, 128))]),
        out_shape=([act] * (1 if top else 0)
                   + [jax.ShapeDtypeStruct((SEQ, D_IN), BF16), jax.ShapeDtypeStruct((D_MODEL, D_MODEL), BF16),
                      jax.ShapeDtypeStruct((PACK_ROWS, 128), F32)]),
        scratch_shapes=[pltpu.VMEM((D_MODEL, D_MODEL), F32), pltpu.VMEM((1, D_MODEL), F32), pltpu.VMEM((1, 1), F32),
                        pltpu.VMEM((tm, D_MODEL), F32), pltpu.VMEM((BLOCK, D_KV), F32), pltpu.VMEM((BLOCK, D_KV), F32),
                        pltpu.VMEM((BLOCK, D_POOL), F32)],
        compiler_params=_compiler_params(("arbitrary",)),
    )(sinks, dxo_or_xf, target_or_token, y, z, norm_post, w_out, u, u, pg, q, k, v, ag, a, pool_w, pool_scale, bias)


def _bwd_in(layer, part, token, dproj, x, norm_pre, dxo=None, w_in_t=None):
    pair = part in ("dw_pair", "both_pair")
    want_dw, want_dx = part != "dx", part in ("both", "dx", "both_pair")
    tm = TOKEN_TILE
    steps = SEQ // tm
    cw = 256

    def body(*refs):
        refs = list(refs)
        dp_ref, x_ref, g_ref = refs[1:4]
        del refs[:4]
        if want_dx:
            dxo_ref, w_ref, dx_ref, dgo_ref = refs[:4]
            del refs[:4]
            dg = refs.pop()
        if pair:
            hs_ref, hm_ref, acc, mine_buf, theirs_buf, send_sem, recv_sem = refs
        elif want_dw:
            dw_ref, acc = refs
        i = pl.program_id(0)

        @pl.when(i == 0)
        def _():
            if pair:
                _handshake([(lax.axis_index("x"), lax.axis_index("y"), 1 - lax.axis_index("c"))])
            if want_dw:
                acc[...] = jnp.zeros_like(acc)
            if want_dx:
                dg[...] = jnp.zeros_like(dg)

        xv = x_ref[...]
        gv = g_ref[layer:layer + 1, :]
        r = lax.rsqrt(jnp.mean(xv * xv, axis=-1, keepdims=True) + EPS)
        xn = xv * r
        if want_dw:
            hb = (xn * gv).astype(BF16)
            for c in range(0, D_IN, cw):
                acc[c:c + cw, :] += _tn(dp_ref[:, c:c + cw], hb)
        def rows_for(q, core):
            return pl.ds(pl.multiple_of((2 * q + core) * IN_SHARD, 8), IN_SHARD)

        def swap(q):
            x, y, c = _mesh_pos()
            return pltpu.make_async_remote_copy(
                src_ref=mine_buf.at[q], dst_ref=theirs_buf.at[q], send_sem=send_sem.at[q], recv_sem=recv_sem.at[q],
                device_id=(x, y, 1 - c), device_id_type=MESH)

        if pair:
            @pl.when(i == steps - 1)
            def _():
                for q in range(4):
                    mine_buf[q] = acc[rows_for(q, 1 - lax.axis_index("c")), :].astype(BF16)
                    swap(q).start()

        if want_dx:
            dh = _nn(dp_ref[...], w_ref[...])
            dg[...] += jnp.sum(dh * xn, axis=0, keepdims=True)
            dhn = dh * gv
            dx_ref[...] = dxo_ref[...] + r * (dhn - xn * jnp.mean(dhn * xn, axis=-1, keepdims=True))

        @pl.when(i == steps - 1)
        def _():
            if pair:
                x, y, c = _mesh_pos()
                for q in range(4):
                    swap(q).wait()
                for j, q in enumerate([2 * (1 - x) + y, 2 * x + (1 - y), 2 * (1 - x) + (1 - y)]):
                    hs_ref[j] = (acc[rows_for(q, c), :] + theirs_buf[q].astype(F32)).astype(BF16)
                hm_ref[...] = acc[rows_for(2 * x + y, c), :] + theirs_buf[2 * x + y].astype(F32)
            elif want_dw:
                dw_ref[...] = acc[...].astype(BF16)
            if want_dx:
                _rows_of(dg, dgo_ref, 0)

    row = lambda c: pl.BlockSpec((tm, c), lambda i: (i, 0))
    const = lambda shape: pl.BlockSpec(shape, lambda i: (0,) * len(shape))
    in_specs = [const((8, 128)), row(D_IN), row(D_MODEL), const((DEPTH, D_MODEL))]
    operands = [token, dproj, x, norm_pre]
    out_specs, out_shape, scratch = [], [], []
    if want_dx:
        in_specs += [row(D_MODEL), _resident((D_IN, D_MODEL))]
        operands += [dxo, w_in_t]
        out_specs += [row(D_MODEL), const((8, 128))]
        out_shape += [jax.ShapeDtypeStruct((SEQ, D_MODEL), F32), jax.ShapeDtypeStruct((8, 128), F32)]
    if pair:
        out_specs += [const((3, IN_SHARD, D_MODEL)), const((IN_SHARD, D_MODEL))]
        out_shape += [jax.ShapeDtypeStruct((3, IN_SHARD, D_MODEL), BF16), jax.ShapeDtypeStruct((IN_SHARD, D_MODEL), F32)]
        scratch += [pltpu.VMEM((D_IN, D_MODEL), F32), pltpu.VMEM((4, IN_SHARD, D_MODEL), BF16),
                    pltpu.VMEM((4, IN_SHARD, D_MODEL), BF16), pltpu.SemaphoreType.DMA((4,)), pltpu.SemaphoreType.DMA((4,))]
    elif want_dw:
        out_specs.append(const((D_IN, D_MODEL)))
        out_shape.append(jax.ShapeDtypeStruct((D_IN, D_MODEL), BF16))
        scratch.append(pltpu.VMEM((D_IN, D_MODEL), F32))
    if want_dx:
        scratch.append(pltpu.VMEM((1, D_MODEL), F32))
    params = pltpu.CompilerParams(dimension_semantics=("arbitrary",), vmem_limit_bytes=VMEM_LIMIT,
                                  collective_id=COLLECTIVE_PAIR_SUM[layer] if pair else None)
    return pl.pallas_call(
        body, name=f"bwd_in_{part}{layer}", grid=(steps,),
        in_specs=in_specs, out_specs=out_specs, out_shape=out_shape, scratch_shapes=scratch,
        compiler_params=params,
    )(*operands)


def _mesh_pos():
    return lax.axis_index("x"), lax.axis_index("y"), lax.axis_index("c")


def _device_rows(ref, m, px, py, pc):
    return ref.at[pl.ds(pl.multiple_of((4 * px + 2 * py + pc) * m, 16 if m % 16 == 0 else 8), m), :]


def _allgather(srcs, out_dtype, name, later=()):
    na, nb = len(srcs), len(later)
    every = list(srcs) + list(later)
    shapes = [(a.shape[-2], a.shape[-1]) for a, _ in every]

    def body(*refs):
        xs, refs = refs[:na + nb], refs[na + nb:]
        outs, cast, land, refs = refs[:na], refs[na:na + nb], refs[na + nb:na + 2 * nb], refs[na + 2 * nb:]
        stage, raw, (send_sems, recv_sems, local_sems, load_sems) = refs[:na], refs[na:2 * na + nb], refs[2 * na + nb:]
        loads = [pltpu.make_async_copy(xs[i].at[every[i][1]], raw[i], load_sems.at[i]) for i in range(na + nb)]
        for cp in loads:
            cp.start()
        x, y, c = _mesh_pos()
        me, sibling = (x, y, c), (x, y, 1 - c)
        near = [(1 - x, y), (x, 1 - y)]
        far = (1 - x, 1 - y)
        relay_from, relay_to = (x ^ (1 - c), y ^ c), (x ^ c, y ^ (1 - c))
        _handshake([sibling] + [(*chip, c) for chip in near])
        k_from, k_to = 1 + c, 2 - c

        def slot(a, px, py, pc):
            return _device_rows(outs[a], shapes[a][0], px, py, pc)

        def copy(a, k, block, to, src=None):
            return pltpu.make_async_remote_copy(
                src_ref=slot(a, *block) if src is None else src, dst_ref=slot(a, *block),
                send_sem=send_sems.at[a, k], recv_sem=recv_sems.at[a, k], device_id=to, device_id_type=MESH)

        def cast_block(i):
            loads[i].wait()
            return raw[i][...].astype(out_dtype)

        for a in range(na):
            stage[a][...] = cast_block(a)
        mine = [pltpu.make_async_copy(stage[a], slot(a, *me), local_sems.at[a]) for a in range(na)]
        for cp in mine:
            cp.start()
        sent = []
        for a in range(na):
            sent.append(copy(a, 0, me, sibling, src=stage[a]))
            sent += [copy(a, 1 + j, me, (*chip, c), src=stage[a]) for j, chip in enumerate(near)]
        for cp in sent:
            cp.start()
        for b in range(nb):
            cast[b][...] = cast_block(na + b)
            cp = pltpu.make_async_copy(cast[b], _device_rows(land[b], shapes[na + b][0], *me), local_sems.at[na + b])
            cp.start()
            mine.append(cp)
        for a in range(na):
            copy(a, k_from, (*relay_from, c), me).wait_recv()
            sent += [copy(a, 3, (*relay_from, c), (*relay_to, c)), copy(a, 3 + k_from, (*relay_from, c), sibling)]
            sent[-2].start()
            sent[-1].start()
        for a in range(na):
            copy(a, k_to, (*relay_to, c), me).wait_recv()
            sent.append(copy(a, 3 + k_to, (*relay_to, c), sibling))
            sent[-1].start()
        for a in range(na):
            copy(a, 3, (*far, c), me).wait_recv()
            sent.append(copy(a, 6, (*far, c), sibling))
            sent[-1].start()
        for a in range(na):
            copy(a, 0, sibling, me).wait_recv()
            for j, chip in enumerate(near + [far]):
                copy(a, 4 + j, (*chip, 1 - c), me).wait_recv()
        for cp in sent:
            cp.wait_send()
        for cp in mine:
            cp.wait()

    vmem = pl.BlockSpec(memory_space=pltpu.VMEM)
    hbm = pl.BlockSpec(memory_space=pl.ANY)
    gathered = [jax.ShapeDtypeStruct((N_DEV * m, n), out_dtype) for m, n in shapes]
    out = pl.pallas_call(
        body, name=name,
        in_specs=[hbm] * (na + nb),
        out_specs=[hbm] * na + [vmem] * nb + [hbm] * nb,
        out_shape=gathered[:na] + [jax.ShapeDtypeStruct(s, out_dtype) for s in shapes[na:]] + gathered[na:],
        scratch_shapes=([pltpu.VMEM(s, out_dtype) for s in shapes[:na]]
                        + [pltpu.VMEM(s, a.dtype) for s, (a, _) in zip(shapes, every)]
                        + [pltpu.SemaphoreType.DMA((na, 7)), pltpu.SemaphoreType.DMA((na, 7)),
                           pltpu.SemaphoreType.DMA((na + nb,)), pltpu.SemaphoreType.DMA((na + nb,))]),
        compiler_params=pltpu.CompilerParams(vmem_limit_bytes=VMEM_LIMIT, collective_id=COLLECTIVE_GATHER_W0),
    )(*[a for a, _ in every])
    return out[:na], out[na:na + nb], out[na + nb:]


ALL_PEERS = tuple(range(1, N_DEV))
SIBLING_AND_SAME_CORE = (1, 2, 4, 6)


def _related(k, x, y, c):
    return x ^ ((k >> 2) & 1), y ^ ((k >> 1) & 1), c ^ (k & 1)


def _gather_start(blocks, lands, relations, collective_id, name):
    na = len(blocks)

    def body(*refs):
        src, land, sems, token = refs[:na], refs[na:2 * na], refs[2 * na:4 * na], refs[-1]
        x, y, c = _mesh_pos()
        _handshake([_related(k, x, y, c) for k in sorted(set().union(*relations))])
        for a in range(na):
            for k in relations[a]:
                pltpu.make_async_remote_copy(
                    src_ref=src[a], dst_ref=_device_rows(land[a], blocks[a].shape[0], x, y, c),
                    send_sem=sems[2 * a].at[k - 1], recv_sem=sems[2 * a + 1].at[k - 1],
                    device_id=_related(k, x, y, c), device_id_type=MESH).start()
        token[...] = jnp.zeros_like(token)

    bufs = [pltpu.HBM(t.shape, t.dtype) for t in list(blocks) + list(lands)]
    out = pl.pallas_call(
        body, name=name,
        out_shape=(*([pltpu.SemaphoreType.DMA((N_DEV - 1,))] * (2 * na)), *bufs, jax.ShapeDtypeStruct((8, 128), F32)),
        in_specs=[_HBM] * (2 * na),
        out_specs=(*([_SEM] * (2 * na)), *([_HBM] * (2 * na)), pl.BlockSpec(memory_space=pltpu.VMEM)),
        input_output_aliases={i: 2 * na + i for i in range(2 * na)},
        compiler_params=pltpu.CompilerParams(has_side_effects=_EFFECT, collective_id=collective_id),
    )(*[pltpu.with_memory_space_constraint(t, pltpu.HBM) for t in list(blocks) + list(lands)])
    sems = [(out[2 * a], out[2 * a + 1]) for a in range(na)]
    return sems, out[2 * na:3 * na], out[3 * na:4 * na], out[-1]


def _gather_wait(sems, block, land, relations, after, name):
    def body(src, land_ref, send_sem, recv_sem, after_ref, src_out, land_out):
        x, y, c = _mesh_pos()
        for k in relations:
            peer = _related(k, x, y, c)
            cp = pltpu.make_async_remote_copy(
                src_ref=src, dst_ref=_device_rows(land_ref, block.shape[0], *peer),
                send_sem=send_sem.at[k - 1], recv_sem=recv_sem.at[k - 1], device_id=peer, device_id_type=MESH)
            cp.wait_send()
            cp.wait_recv()

    out = pl.pallas_call(
        body, name=name,
        out_shape=(pltpu.HBM(block.shape, block.dtype), pltpu.HBM(land.shape, land.dtype)),
        in_specs=[_HBM, _HBM, _SEM, _SEM, pl.BlockSpec(memory_space=pl.ANY)],
        out_specs=[_HBM, _HBM],
        input_output_aliases={0: 0, 1: 1},
        compiler_params=pltpu.CompilerParams(has_side_effects=_EFFECT),
    )(block, land, sems[0], sems[1], after)
    return out[1]


(COLLECTIVE_GATHER_W0, COLLECTIVE_GATHER_W1, COLLECTIVE_FORWARD_W_IN1, COLLECTIVE_EXCHANGE_1, COLLECTIVE_EXCHANGE_0A,
 COLLECTIVE_EXCHANGE_0B, COLLECTIVE_GATHER_SMALL) = range(1, 8)
COLLECTIVE_PAIR_SUM = (8, 9)
COLLECTIVE_FORWARD_W_OUT1 = 10


def _handshake(peers):
    barrier = pltpu.get_barrier_semaphore()
    for peer in peers:
        pl.semaphore_signal(barrier, inc=1, device_id=peer, device_id_type=MESH)
    pl.semaphore_wait(barrier, len(peers))


def _forward_plan(land_ref, m):
    x, y, c = _mesh_pos()
    return [_device_rows(land_ref, m, qx, qy, c) for qx, qy in ((1 - x, y), (x, 1 - y), (1 - x, 1 - y))], (x, y, 1 - c)


def _forward_start(land, m, collective_id, name):
    def body(land_ref, send_sem, recv_sem, land_out, token):
        _handshake([(lax.axis_index("x"), lax.axis_index("y"), 1 - lax.axis_index("c"))])
        rows, sibling = _forward_plan(land_ref, m)
        for j, r in enumerate(rows):
            pltpu.make_async_remote_copy(src_ref=r, dst_ref=r, send_sem=send_sem.at[j], recv_sem=recv_sem.at[j],
                                         device_id=sibling, device_id_type=MESH).start()
        token[...] = jnp.zeros_like(token)

    out = pl.pallas_call(
        body, name=name,
        out_shape=(pltpu.SemaphoreType.DMA((3,)), pltpu.SemaphoreType.DMA((3,)), pltpu.HBM(land.shape, land.dtype),
                   jax.ShapeDtypeStruct((8, 128), F32)),
        in_specs=[_HBM],
        out_specs=(_SEM, _SEM, _HBM, pl.BlockSpec(memory_space=pltpu.VMEM)),
        input_output_aliases={0: 2},
        compiler_params=pltpu.CompilerParams(has_side_effects=_EFFECT, collective_id=collective_id),
    )(pltpu.with_memory_space_constraint(land, pltpu.HBM))
    return (out[0], out[1]), out[2], out[3]


def _forward_wait(sems, land, m, after, name):
    def body(land_ref, send_sem, recv_sem, after_ref, land_out):
        x, y, c = _mesh_pos()
        mine, sibling = _forward_plan(land_ref, m)
        theirs = [_device_rows(land_ref, m, qx, qy, 1 - c) for qx, qy in ((1 - x, y), (x, 1 - y), (1 - x, 1 - y))]
        for j in range(3):
            cp = pltpu.make_async_remote_copy(src_ref=mine[j], dst_ref=theirs[j], send_sem=send_sem.at[j],
                                              recv_sem=recv_sem.at[j], device_id=sibling, device_id_type=MESH)
            cp.wait_send()
            cp.wait_recv()

    return pl.pallas_call(
        body, name=name,
        out_shape=pltpu.HBM(land.shape, land.dtype),
        in_specs=[_HBM, _SEM, _SEM, pl.BlockSpec(memory_space=pl.ANY)],
        out_specs=_HBM,
        input_output_aliases={0: 0},
        compiler_params=pltpu.CompilerParams(has_side_effects=_EFFECT),
    )(land, sems[0], sems[1], after)


_HBM = pl.BlockSpec(memory_space=pltpu.HBM)
_SEM = pl.BlockSpec(memory_space=pltpu.SEMAPHORE)
_EFFECT = pltpu.SideEffectType.DATAFLOW_SIDE_EFFECTING


def _exchange_plan(direct):
    x, y, c = _mesh_pos()
    if not direct:
        return [(j, j, (qx, qy, c)) for j, (qx, qy) in enumerate([(1 - x, y), (x, 1 - y), (1 - x, 1 - y)])]
    plan = []
    for k in range(1, N_DEV):
        px, py, pc = x ^ ((k >> 2) & 1), y ^ ((k >> 1) & 1), c ^ (k & 1)
        plan.append((4 * px + 2 * py + pc, k - 1, (px, py, pc)))
    return plan


def _exchange_copies(directs):
    copies, base = [], 0
    for a, direct in enumerate(directs):
        plan = _exchange_plan(direct)
        copies += [(a, block, slot, peer, base + slot) for block, slot, peer in plan]
        base += len(plan)
    return copies, base


def _exchange_start(srcs, directs, collective_id, name):
    na = len(srcs)
    slots = [N_DEV - 1 if direct else 3 for direct in directs]

    def body(*refs):
        src, land = refs[:na], refs[na:2 * na]
        send_sem, recv_sem = refs[2 * na], refs[2 * na + 1]
        token = refs[-1]
        _handshake([peer for _, _, peer in _exchange_plan(any(directs))])
        for a, block, slot, peer, sem in _exchange_copies(directs)[0]:
            pltpu.make_async_remote_copy(
                src_ref=src[a].at[block], dst_ref=land[a].at[slot], send_sem=send_sem.at[sem],
                recv_sem=recv_sem.at[sem], device_id=peer, device_id_type=MESH).start()
        token[...] = jnp.zeros_like(token)

    zones = [jax.ShapeDtypeStruct((n,) + t.shape[1:], t.dtype) for n, t in zip(slots, srcs)]
    bufs = [pltpu.HBM(t.shape, t.dtype) for t in list(srcs) + zones]
    out = pl.pallas_call(
        body, name=name,
        out_shape=(pltpu.SemaphoreType.DMA((sum(slots),)), pltpu.SemaphoreType.DMA((sum(slots),)), *bufs,
                   jax.ShapeDtypeStruct((8, 128), F32)),
        in_specs=[_HBM] * (2 * na),
        out_specs=(_SEM, _SEM, *([_HBM] * (2 * na)), pl.BlockSpec(memory_space=pltpu.VMEM)),
        input_output_aliases={i: 2 + i for i in range(2 * na)},
        compiler_params=pltpu.CompilerParams(has_side_effects=_EFFECT, collective_id=collective_id),
    )(*[pltpu.with_memory_space_constraint(t, pltpu.HBM) for t in srcs],
      *[pltpu.with_memory_space_constraint(lax.empty(t.shape, t.dtype), pltpu.HBM) for t in zones])
    return out[0], out[1], out[2:2 + na], out[2 + na:2 + 2 * na], out[-1]


def _exchange_wait(send_sem, recv_sem, srcs, lands, directs, after, name):
    na = len(srcs)

    def body(*refs):
        src, land = refs[:na], refs[na:2 * na]
        send_sem_ref, recv_sem_ref = refs[2 * na], refs[2 * na + 1]
        for a, block, slot, peer, sem in _exchange_copies(directs)[0]:
            cp = pltpu.make_async_remote_copy(
                src_ref=src[a].at[block], dst_ref=land[a].at[slot], send_sem=send_sem_ref.at[sem],
                recv_sem=recv_sem_ref.at[sem], device_id=peer, device_id_type=MESH)
            cp.wait_send()
            cp.wait_recv()

    bufs = [pltpu.HBM(t.shape, t.dtype) for t in list(srcs) + list(lands)]
    out = pl.pallas_call(
        body, name=name,
        out_shape=tuple(bufs),
        in_specs=[_HBM] * (2 * na) + [_SEM, _SEM, pl.BlockSpec(memory_space=pl.ANY)],
        out_specs=[_HBM] * (2 * na),
        input_output_aliases={i: i for i in range(2 * na)},
        compiler_params=pltpu.CompilerParams(has_side_effects=_EFFECT),
    )(*srcs, *lands, send_sem, recv_sem, after)
    return out[:na], out[na:]


def _own_then_slots(mine_ref, lands_ref, rows=slice(None)):
    if len(mine_ref.shape) == 3:
        x, y, c = _mesh_pos()
        total = mine_ref[4 * x + 2 * y + c, rows, :].astype(F32)
    else:
        total = mine_ref[rows, :].astype(F32)
    for j in range(lands_ref.shape[0]):
        total = total + lands_ref[j, rows, :].astype(F32)
    return total


SMALL_ROWS = 2 * PACK_SLICE + 2 * 8


def _small_gather_start(mine, lands, dgpre, name):
    def body(*refs):
        hm, ld, dg = refs[:DEPTH], refs[DEPTH:2 * DEPTH], refs[2 * DEPTH:3 * DEPTH]
        send_sem, recv_sem, blk, land, token, own, slots, rows, built, local_sems = refs[3 * DEPTH:]
        x, y, c = _mesh_pos()
        loads = []
        for l in range(DEPTH):
            loads += [pltpu.make_async_copy(hm[l].at[4 * x + 2 * y + c], own.at[l], local_sems.at[3 * l]),
                      pltpu.make_async_copy(ld[l], slots.at[l], local_sems.at[3 * l + 1]),
                      pltpu.make_async_copy(dg[l], rows.at[l], local_sems.at[3 * l + 2])]
        for cp in loads:
            cp.start()
        _handshake([_related(k, x, y, c) for k in ALL_PEERS])
        for cp in loads:
            cp.wait()
        for l in range(DEPTH):
            total = own[l]
            for j in range(N_DEV - 1):
                total = total + slots[l, j]
            built[PACK_SLICE * l:PACK_SLICE * (l + 1), :] = total
            built[2 * PACK_SLICE + 8 * l:2 * PACK_SLICE + 8 * (l + 1), :] = rows[l]
        stores = [pltpu.make_async_copy(built, blk, local_sems.at[3 * DEPTH]),
                  pltpu.make_async_copy(built, _device_rows(land, SMALL_ROWS, x, y, c), local_sems.at[3 * DEPTH + 1])]
        for cp in stores:
            cp.start()
        for cp in stores:
            cp.wait()
        for k in ALL_PEERS:
            pltpu.make_async_remote_copy(
                src_ref=blk, dst_ref=_device_rows(land, SMALL_ROWS, x, y, c), send_sem=send_sem.at[k - 1],
                recv_sem=recv_sem.at[k - 1], device_id=_related(k, x, y, c), device_id_type=MESH).start()
        token[...] = jnp.zeros_like(token)

    hbm = pl.BlockSpec(memory_space=pl.ANY)
    out = pl.pallas_call(
        body, name=name,
        in_specs=[hbm] * (3 * DEPTH),
        out_specs=(_SEM, _SEM, _HBM, _HBM, pl.BlockSpec(memory_space=pltpu.VMEM)),
        out_shape=(pltpu.SemaphoreType.DMA((N_DEV - 1,)), pltpu.SemaphoreType.DMA((N_DEV - 1,)),
                   pltpu.HBM((SMALL_ROWS, 128), F32), pltpu.HBM((N_DEV * SMALL_ROWS, 128), F32),
                   jax.ShapeDtypeStruct((8, 128), F32)),
        scratch_shapes=[pltpu.VMEM((DEPTH, PACK_SLICE, 128), F32), pltpu.VMEM((DEPTH, N_DEV - 1, PACK_SLICE, 128), F32),
                        pltpu.VMEM((DEPTH, 8, 128), F32), pltpu.VMEM((SMALL_ROWS, 128), F32),
                        pltpu.SemaphoreType.DMA((3 * DEPTH + 2,))],
        compiler_params=pltpu.CompilerParams(has_side_effects=_EFFECT, collective_id=COLLECTIVE_GATHER_SMALL),
    )(*mine, *lands, *dgpre)
    return (out[0], out[1]), out[2], out[3], out[4]


def _adamw_math(w, g, m, v):
    m = ADAM_B1 * m + (1.0 - ADAM_B1) * g
    v = ADAM_B2 * v + (1.0 - ADAM_B2) * (g * g)
    m_hat = m / (1.0 - ADAM_B1 ** ADAM_STEP)
    v_hat = v / (1.0 - ADAM_B2 ** ADAM_STEP)
    delta = -ADAM_LR * (m_hat / (jnp.sqrt(v_hat) + ADAM_EPS) + ADAM_WD * w)
    return delta, m, v


def _adamw_layer(layer, parts, token, name):
    steps = {w.shape[1] // rows for _, _, w, _, _, _, rows in parts}
    assert len(steps) == 1, steps
    n = len(parts)

    def body(_, *refs):
        for p in range(n):
            hm_ref, ld_ref, w_ref, m_ref, v_ref = refs[5 * p:5 * (p + 1)]
            g_ref, d_ref, nm_ref, nv_ref = refs[len(refs) - 4 * (n - p):len(refs) - 4 * (n - p - 1)]
            g = _own_then_slots(hm_ref, ld_ref)
            g_ref[...] = g
            d, nm, nv = _adamw_math(w_ref[...], g, m_ref[...], v_ref[...])
            d_ref[...] = d
            nm_ref[...] = nm
            nv_ref[...] = nv

    in_specs, out_specs, out_shape, operands, carried, aliases = [pl.BlockSpec(memory_space=pl.ANY)], [], [], [], [], {}
    for p, (mine, lands, w, m, v, earlier, rows) in enumerate(parts):
        nn = w.shape[2]
        spec = pl.BlockSpec((None, rows, nn), lambda i: (layer, i, 0))
        in_specs += [pl.BlockSpec((rows, nn), lambda i: (i, 0)) if mine.ndim == 2
                     else pl.BlockSpec((N_DEV, rows, nn), lambda i: (0, i, 0)),
                     pl.BlockSpec((lands.shape[0], rows, nn), lambda i: (0, i, 0)), spec, spec, spec]
        out_specs += [spec] * 4
        out_shape += [jax.ShapeDtypeStruct(w.shape, F32)] * 4
        operands += [mine, lands, w, m, v]
        if earlier is not None:
            aliases.update({1 + 5 * n + len(carried) + t: 4 * p + t for t in range(4)})
            carried += list(earlier)
    out = pl.pallas_call(
        body, name=name, grid=(steps.pop(),),
        in_specs=in_specs + [pl.BlockSpec(memory_space=pl.ANY)] * len(carried),
        out_specs=out_specs, out_shape=out_shape, input_output_aliases=aliases,
        compiler_params=_compiler_params(("arbitrary",)),
    )(token, *operands, *carried)
    return [out[4 * p:4 * (p + 1)] for p in range(n)]


def _adamw_small(gathered, params):
    def body(all_ref, *refs):
        ins, outs, packs = refs[:15], refs[15:15 + 21], refs[15 + 21]
        loss_ref = outs[0]
        for dev in range(N_DEV):
            for l in range(DEPTH):
                packs[l, PACK_SLICE * dev:PACK_SLICE * (dev + 1), :] = (
                    all_ref[SMALL_ROWS * dev + PACK_SLICE * l:SMALL_ROWS * dev + PACK_SLICE * (l + 1), :])
        loss_ref[...] = packs[DEPTH - 1, ROW_LOSS:ROW_LOSS + 1, 0:1]

        def update(p, sel, g):
            w_ref, m_ref, v_ref = ins[p], ins[5 + p], ins[10 + p]
            d, nm, nv = _adamw_math(w_ref[sel], g, m_ref[sel], v_ref[sel])
            for t, val in enumerate((g, d, nm, nv)):
                outs[1 + 5 * t + p][sel] = val

        for l in range(DEPTH):
            gp = packs.at[l]
            row0 = 2 * PACK_SLICE + 8 * l
            dgpre = all_ref[row0:row0 + 8, :]
            for dev in range(1, N_DEV):
                dgpre = dgpre + all_ref[SMALL_ROWS * dev + row0:SMALL_ROWS * dev + row0 + 8, :]
            for grp in range(4):
                update(0, (l, grp), gp[ROW_PW + BLOCK * grp:ROW_PW + BLOCK * (grp + 1), :])
                update(1, (slice(l, l + 1), slice(128 * grp, 128 * (grp + 1))), gp[ROW_SC + grp:ROW_SC + grp + 1, :])
            update(2, (slice(l, l + 1), slice(None)), gp[ROW_SINK:ROW_SINK + 1, 0:N_HEADS])
            for r in range(D_MODEL // 128):
                sel = (slice(l, l + 1), slice(128 * r, 128 * (r + 1)))
                update(3, sel, dgpre[r:r + 1, :])
                update(4, sel, gp[ROW_NPOST + r:ROW_NPOST + r + 1, :])

    shapes = [jax.ShapeDtypeStruct(p.shape, F32) for p in params[:5]]
    return pl.pallas_call(
        body, name="adamw_small",
        out_shape=[jax.ShapeDtypeStruct((1, 1), F32)] + shapes * 4,
        scratch_shapes=[pltpu.VMEM((DEPTH, PACK_ROWS, 128), F32)],
        compiler_params=_compiler_params(),
    )(gathered, *params)


def kernel(x, w_in, pool_w, pool_scale, attn_sinks, w_out, norm_pre, norm_post, loss_target, m_w_in, m_pool_w, m_pool_scale, m_attn_sinks, m_w_out, m_norm_pre, m_norm_post, v_w_in, v_pool_w, v_pool_scale, v_attn_sinks, v_w_out, v_norm_pre, v_norm_post):
    x0 = x.reshape(SEQ, D_MODEL)
    target = loss_target.reshape(SEQ, D_MODEL)
    bias = jnp.asarray(_attn_bias())
    w_in_t, m_in_t, v_in_t = (jnp.swapaxes(t, 1, 2) for t in (w_in, m_w_in, v_w_in))

    (win0, wout0), later, lands = _allgather([(w_in_t, 0), (w_out, 0)], BF16, "gather_w0",
                                              later=[(w_in_t, 1), (w_out, 1)])
    sems, later, lands, token = _gather_start(later, lands, [SIBLING_AND_SAME_CORE] * 2, COLLECTIVE_GATHER_W1,
                                              "gather_w1_start")
    win_full, wout_full = [win0, None], [wout0, None]

    saved = []
    xl = x0
    for layer in range(DEPTH):
        front = (layer, xl, norm_pre, win_full[layer], token, attn_sinks, pool_w, pool_scale, bias)
        if layer == 0:
            u, pg, q, k, v, ag, z, a = _fwd_front(*front)
            land = _gather_wait(sems[0], later[0], lands[0], SIBLING_AND_SAME_CORE, z, "gather_w_in1_wait")
            fsems, land, token = _forward_start(land, IN_SHARD, COLLECTIVE_FORWARD_W_IN1, "forward_w_in1_start")
            x_next, y = _fwd_out(layer, z, xl, norm_post, wout_full[layer], token)
            win_full[1] = _forward_wait(fsems, land, IN_SHARD, x_next, "forward_w_in1_wait")
            land = _gather_wait(sems[1], later[1], lands[1], SIBLING_AND_SAME_CORE, win_full[1], "gather_w_out1_wait")
            fsems, land, token = _forward_start(land, OUT_SHARD, COLLECTIVE_FORWARD_W_OUT1, "forward_w_out1_start")
            wout_full[1] = _forward_wait(fsems, land, OUT_SHARD, token, "forward_w_out1_wait")
        else:
            u, pg, q, k, v, ag, z, a, x_next, y = _fwd_front(*front, out=(norm_post, wout_full[layer]))
        saved.append((xl, u, pg, q, k, v, ag, z, a, y))
        xl = x_next

    params_small = [pool_w, pool_scale, attn_sinks, norm_pre, norm_post,
                    m_pool_w, m_pool_scale, m_attn_sinks, m_norm_pre, m_norm_post,
                    v_pool_w, v_pool_scale, v_attn_sinks, v_norm_pre, v_norm_post]

    def start(srcs, directs, paired, collective_id, tag):
        send_sem, recv_sem, srcs, lands, started = _exchange_start(srcs, directs, collective_id, f"exchange_start{tag}")
        return (send_sem, recv_sem, srcs, lands, paired, directs), started

    def finish(handle, after, tag):
        send_sem, recv_sem, srcs, lands, paired, directs = handle
        srcs, lands = _exchange_wait(send_sem, recv_sem, srcs, lands, directs, after, f"exchange_wait{tag}")
        return [s if p is None else p for s, p in zip(srcs, paired)], lands

    def back(layer, top, first, second):
        xin, u, pg, q, k, v, ag, z, a, y = saved[layer]
        return _bwd_back(layer, top, first, second, y, z, norm_post, wout_full[layer], attn_sinks, u, pg, q, k, v,
                         ag, a, pool_w, pool_scale, bias)

    dgpre = [None] * DEPTH
    dx, dproj, gw_out, pack = back(1, True, xl, target)
    dx, dgpre[1], chip_sums, own_sum = _bwd_in(1, "both_pair", token, dproj, saved[1][0], norm_pre, dx, win_full[1])
    top, token = start([chip_sums, gw_out.reshape(N_DEV, OUT_SHARD, D_MODEL), pack.reshape(N_DEV, PACK_SLICE, 128)],
                       [False, True, True], [own_sum, None, None], COLLECTIVE_EXCHANGE_1, "1")

    dproj, gw_out, pack = back(0, False, dx, token)
    early, token = start([gw_out.reshape(N_DEV, OUT_SHARD, D_MODEL), pack.reshape(N_DEV, PACK_SLICE, 128)],
                         [True, True], [None, None], COLLECTIVE_EXCHANGE_0A, "0a")
    chip_sums, own_sum = _bwd_in(0, "dw_pair", token, dproj, saved[0][0], norm_pre)
    late, token = start([chip_sums], [False], [own_sum], COLLECTIVE_EXCHANGE_0B, "0b")
    dx, dgpre[0] = _bwd_in(0, "dx", token, dproj, saved[0][0], norm_pre, dx, win_full[0])

    own1, lands1 = finish(top, dx, "1")
    own0a, lands0a = finish(early, dx, "0a")
    sems, block, land, token = _small_gather_start([own0a[1], own1[2]], [lands0a[1], lands1[2]], dgpre,
                                                   "gather_small_start")
    big_in, big_out = _adamw_layer(1, [(own1[0], lands1[0], w_in_t, m_in_t, v_in_t, None, ADAM_ROWS_IN),
                                       (own1[1], lands1[1], w_out, m_w_out, v_w_out, None, ADAM_ROWS_OUT)],
                                   token, "adamw1")
    own0b, lands0b = finish(late, big_out[0], "0b")
    big_in, big_out = _adamw_layer(0, [(own0b[0], lands0b[0], w_in_t, m_in_t, v_in_t, big_in, ADAM_ROWS_IN),
                                       (own0a[0], lands0a[0], w_out, m_w_out, v_w_out, big_out, ADAM_ROWS_OUT)],
                                   token, "adamw0")
    gathered = _gather_wait(sems, block, land, ALL_PEERS, big_in[0], "gather_small_wait")
    small_out = _adamw_small(gathered, params_small)
    loss = small_out[0].reshape(())

    outs = [loss, dx.reshape(1, SEQ, D_MODEL)]
    for t in range(4):
        pw_, sc_, sk_, npre_, npost_ = small_out[1 + 5 * t:6 + 5 * t]
        outs += [jnp.swapaxes(big_in[t], 1, 2), pw_, sc_, sk_, big_out[t], npre_, npost_]
    return tuple(outs)
```

```python
import numpy as np
import jax
import jax.numpy as jnp
from jax import lax
from jax.experimental import pallas as pl
from jax.experimental.pallas import tpu as pltpu

F32 = jnp.float32
BF16 = jnp.bfloat16

N_DEV = 8
SEQ = 2048
D_MODEL = 1024
D_POOL = 512
D_ATTN = 512
D_KV = 128
D_IN = 2304
N_HEADS = 8
GQA = 4
HEAD_DIM = 64
BLOCK = 128
POOL_WINDOWS = (2, 4, 8, 16)
DEPTH = 2
EPS = 1e-6
NEG_INF = -1e30
SCALE = HEAD_DIM ** -0.5
IN_SHARD = D_IN // N_DEV
OUT_SHARD = D_MODEL // N_DEV

COL_U, COL_PG, COL_Q, COL_K, COL_V, COL_AG = 0, 512, 1024, 1536, 1664, 1792

ADAM_LR = 0.001
ADAM_B1 = 0.9
ADAM_B2 = 0.999
ADAM_EPS = 1e-08
ADAM_WD = 0.01
ADAM_STEP = 10

TOKEN_TILE = 512
FWD_OUT_TILE = 1024
ADAM_ROWS_IN, ADAM_ROWS_OUT = 144, 64
VMEM_LIMIT = 56 * 1024 * 1024
MESH = pl.DeviceIdType.MESH

ROW_PW, ROW_SC, ROW_SINK, ROW_NPOST, ROW_LOSS = 0, 512, 520, 536, 544
PACK_ROWS = 576
PACK_SLICE = PACK_ROWS // N_DEV


def _nn(a, b):
    return jnp.dot(a, b, preferred_element_type=F32)


def _nt(a, b):
    return lax.dot_general(a, b, (((1,), (1,)), ((), ())), preferred_element_type=F32)


def _tn(a, b):
    return lax.dot_general(a, b, (((0,), (0,)), ((), ())), preferred_element_type=F32)


def _silu_parts(g):
    s = jax.nn.sigmoid(g)
    return g * s, s * (1.0 + g * (1.0 - s))


def _resident(shape):
    return pl.BlockSpec(shape, lambda *_: (0,) * len(shape), pipeline_mode=pl.Buffered(1))


def _compiler_params(sem=None):
    if sem is None:
        return pltpu.CompilerParams(vmem_limit_bytes=VMEM_LIMIT)
    return pltpu.CompilerParams(dimension_semantics=sem, vmem_limit_bytes=VMEM_LIMIT)


def _attn_bias():
    t = np.arange(BLOCK)[None, :]
    j = np.arange(BLOCK)[:, None]
    current = j <= t
    dist = np.where(current, t - j, t + BLOCK - j).astype(np.float32)
    out = np.zeros((2, 2, BLOCK, GQA * BLOCK), np.float32)
    for variant in range(2):
        valid = current | (variant == 1)
        for kv in range(2):
            for g in range(GQA):
                slope = np.float32(2.0 ** (-(kv * GQA + g + 1)))
                out[variant, kv, :, g * BLOCK:(g + 1) * BLOCK] = np.where(valid, -slope * dist, np.float32(NEG_INF))
    return out


def _replicate_head(kx, kv):
    rolled = pltpu.roll(kx, 64, 1)
    lane = lax.broadcasted_iota(jnp.int32, kx.shape, 1)
    dup = jnp.where(lane < 64, kx, rolled) if kv == 0 else jnp.where(lane < 64, rolled, kx)
    return jnp.concatenate([dup, dup], axis=1).astype(BF16)


def _stack_heads(qv):
    lane = lax.broadcasted_iota(jnp.int32, qv.shape, 1)
    zero = jnp.zeros_like(qv)
    return jnp.concatenate([jnp.where((lane >= 64 * g) & (lane < 64 * g + 64), qv, zero) for g in range(GQA)], axis=0)


def _unstack_heads(xs):
    lane = lax.broadcasted_iota(jnp.int32, (BLOCK, 256), 1)
    return jnp.where(lane < 64, xs[0:128], jnp.where(lane < 128, xs[128:256], jnp.where(lane < 192, xs[256:384], xs[384:512])))


def _fold_heads(r):
    h = r[:, 0:128] + r[:, 128:256]
    return h + pltpu.roll(h, 64, 1)


def _sink_row(sink_ref, layer, kv):
    lane = lax.broadcasted_iota(jnp.int32, (1, GQA * BLOCK), 1)
    s4 = [sink_ref[layer, kv * GQA + g] for g in range(GQA)]
    return jnp.where(lane < 128, s4[0], jnp.where(lane < 256, s4[1], jnp.where(lane < 384, s4[2], s4[3])))


def _band_is_current():
    j = lax.broadcasted_iota(jnp.int32, (BLOCK, GQA * BLOCK), 0)
    t = lax.broadcasted_iota(jnp.int32, (BLOCK, GQA * BLOCK), 1) & (BLOCK - 1)
    return j <= t


def _pack_band(full, current):
    return jnp.where(current, full[BLOCK:], full[:BLOCK])


def _unpack_band(packed, current):
    zero = jnp.zeros_like(packed)
    return jnp.concatenate([jnp.where(current, zero, packed), jnp.where(current, packed, zero)], axis=0)


def _probs_keys_major(k_rep, q_st, bias, sink, current):
    st = _pack_band(_nt(k_rep, q_st), current) * SCALE + bias
    m = jnp.maximum(jnp.max(st, axis=0, keepdims=True), sink)
    p = jnp.exp(st - m)
    esink = jnp.exp(sink - m)
    rl = 1.0 / (jnp.sum(p, axis=0, keepdims=True) + esink)
    return p * rl, esink * rl


WINDOW_HALO = 16


def _window_sum(ext, w, forward):
    s = ext
    sh = 1
    while sh < w:
        s = s + pltpu.roll(s, (ext.shape[0] - sh) if forward else sh, 0)
        sh *= 2
    return s


def _inv_count(n, w):
    t = n * BLOCK + lax.broadcasted_iota(jnp.int32, (BLOCK, 1), 0) + 1
    return 1.0 / jnp.minimum(t.astype(F32), float(w))


def _kv_ext(ref, n):
    r0 = pl.multiple_of(jnp.maximum(n - 1, 0) * BLOCK, BLOCK)
    r1 = pl.multiple_of(n * BLOCK, BLOCK)
    return jnp.concatenate([ref[pl.ds(r0, BLOCK), :], ref[pl.ds(r1, BLOCK), :]], axis=0)


def _rows_of(vec_ref, pack_ref, row0):
    for r in range(D_MODEL // 128):
        pack_ref[row0 + r:row0 + r + 1, :] = vec_ref[:, 128 * r:128 * (r + 1)]


FRONT_TILE = 4 * BLOCK


def _fwd_front(layer, x, norm_pre, w_in_t, token, sinks, pool_w, pool_scale, bias, out=None, handed=None):
    tm = FRONT_TILE

    def body(sink_ref, x_ref, g_ref, w_ref, _, pw_ref, sc_ref, bias_ref, *refs):
        if out is not None:
            gpost_ref, wo_ref, *refs = refs
        if handed is not None:
            send_sem, recv_sem, *refs = refs
            land_ref, (wo_ref, load_sem) = wo_ref, refs[-2:]
            refs = refs[:-2]
            load = pltpu.make_async_copy(land_ref, wo_ref, load_sem)
        if out is not None:
            xn_ref, y_ref = refs[8:10]
        u_ref, pg_ref, q_ref, k_ref, v_ref, ag_ref, z_ref, a_ref = refs[:8]
        uprev, kprev, vprev = refs[-3:]
        i = pl.program_id(0)

        @pl.when(i == 0)
        def _():
            uprev[...] = jnp.zeros_like(uprev)
            kprev[...] = jnp.zeros_like(kprev)
            vprev[...] = jnp.zeros_like(vprev)

        xv = x_ref[...]
        r = lax.rsqrt(jnp.mean(xv * xv, axis=-1, keepdims=True) + EPS)
        h = (xv * r * g_ref[layer:layer + 1, :]).astype(BF16)
        u_ref[...] = _nt(h, w_ref[COL_U:COL_PG, :])
        pg_ref[...] = _nt(h, w_ref[COL_PG:COL_Q, :])
        if handed is not None:
            @pl.when(i == 0)
            def _():
                _forward_wait_copies(land_ref, OUT_SHARD, send_sem, recv_sem)
                load.start()
        for sb in range(tm // BLOCK):
            n = (tm // BLOCK) * i + sb
            rows = slice(BLOCK * sb, BLOCK * (sb + 1))
            before = slice(BLOCK * (sb - 1), BLOCK * sb)
            uv = u_ref[rows, :]
            halo = (uprev[BLOCK - WINDOW_HALO:, :] if sb == 0
                    else u_ref[BLOCK * sb - WINDOW_HALO:BLOCK * sb, :])
            ext = jnp.concatenate([halo, uv], axis=0)
            for g, w in enumerate(POOL_WINDOWS):
                cs = slice(BLOCK * g, BLOCK * (g + 1))
                win = _window_sum(ext[:, cs], w, forward=False)[WINDOW_HALO:]
                pooled = win * _inv_count(n, w) - uv[:, cs]
                mixed = _nn(pooled.astype(BF16), pw_ref[g].astype(BF16))
                gate, _ = _silu_parts(pg_ref[rows, cs])
                z_ref[rows, cs] = (mixed * sc_ref[layer:layer + 1, cs] * gate).astype(BF16)

        q_ref[...] = _nt(h, w_ref[COL_Q:COL_K, :]).astype(BF16)
        k_ref[...] = _nt(h, w_ref[COL_K:COL_V, :])
        v_ref[...] = _nt(h, w_ref[COL_V:COL_AG, :])
        ag_ref[...] = _nt(h, w_ref[COL_AG:D_IN, :])

        current = _band_is_current()
        for sb in range(tm // BLOCK):
            n = (tm // BLOCK) * i + sb
            rows = slice(BLOCK * sb, BLOCK * (sb + 1))
            before = slice(BLOCK * (sb - 1), BLOCK * sb)
            kx = jnp.concatenate([kprev[...] if sb == 0 else k_ref[before, :], k_ref[rows, :]], axis=0)
            vx = jnp.concatenate([vprev[...] if sb == 0 else v_ref[before, :], v_ref[rows, :]], axis=0)
            variant = jnp.minimum(n, 1) if sb == 0 else 1
            for kv in range(2):
                cs = slice(256 * kv, 256 * (kv + 1))
                p, _ = _probs_keys_major(_replicate_head(kx, kv), _stack_heads(q_ref[rows, cs]),
                                         bias_ref[variant, kv], _sink_row(sink_ref, layer, kv), current)
                o = _unstack_heads(_tn(_unpack_band(p.astype(BF16), current), _replicate_head(vx, kv)))
                a_ref[rows, cs] = o
                gate, _ = _silu_parts(ag_ref[rows, cs])
                z_ref[rows, D_POOL + 256 * kv:D_POOL + 256 * (kv + 1)] = (o * gate).astype(BF16)

        tail = slice(tm - BLOCK, tm)
        uprev[...] = u_ref[tail, :]
        kprev[...] = k_ref[tail, :]
        vprev[...] = v_ref[tail, :]

        if handed is not None:
            @pl.when(i == 0)
            def _():
                load.wait()
        if out is not None:
            y = _nn(z_ref[...], wo_ref[...])
            y_ref[...] = y
            r = lax.rsqrt(jnp.mean(y * y, axis=-1, keepdims=True) + EPS)
            xn_ref[...] = x_ref[...] + y * r * gpost_ref[layer:layer + 1, :]

    row = lambda c: pl.BlockSpec((tm, c), lambda i: (i, 0))
    const = lambda shape: pl.BlockSpec(shape, lambda i: (0,) * len(shape))
    act = jax.ShapeDtypeStruct((SEQ, D_MODEL), F32)
    fused = out is not None
    w_out_spec, extra_in, extra_out, extra_shape, extra_scratch, aliases, params = (
        _resident((D_MODEL, D_MODEL)), [], [], [], [], {}, _compiler_params(("arbitrary",)))
    if handed is not None:
        hbm = pl.BlockSpec(memory_space=pl.ANY)
        w_out_spec, extra_in, extra_out = hbm, [_SEM, _SEM], [hbm]
        extra_shape = [jax.ShapeDtypeStruct(out[1].shape, out[1].dtype)]
        extra_scratch = [pltpu.VMEM((D_MODEL, D_MODEL), BF16), pltpu.SemaphoreType.DMA]
        aliases = {9: 10}
        params = pltpu.CompilerParams(dimension_semantics=("arbitrary",), vmem_limit_bytes=VMEM_LIMIT,
                                      has_side_effects=_EFFECT)
    return pl.pallas_call(
        body, name=f"fwd_front{layer}", grid=(SEQ // tm,),
        in_specs=[pl.BlockSpec(memory_space=pltpu.SMEM), row(D_MODEL), const((DEPTH, D_MODEL)),
                  _resident((D_IN, D_MODEL)), const((8, 128)),
                  pl.BlockSpec((None, 4, BLOCK, BLOCK), lambda i: (layer, 0, 0, 0)), const((DEPTH, D_POOL)),
                  _resident((2, 2, BLOCK, GQA * BLOCK))]
                 + ([const((DEPTH, D_MODEL)), w_out_spec] if fused else []) + extra_in,
        out_specs=[row(D_POOL), row(D_POOL), row(D_ATTN), row(D_KV), row(D_KV), row(D_ATTN), row(D_MODEL),
                   row(D_ATTN)] + ([row(D_MODEL)] * 2 if fused else []) + extra_out,
        out_shape=[jax.ShapeDtypeStruct((SEQ, D_POOL), F32), jax.ShapeDtypeStruct((SEQ, D_POOL), F32),
                   jax.ShapeDtypeStruct((SEQ, D_ATTN), BF16), jax.ShapeDtypeStruct((SEQ, D_KV), F32),
                   jax.ShapeDtypeStruct((SEQ, D_KV), F32), jax.ShapeDtypeStruct((SEQ, D_ATTN), F32),
                   jax.ShapeDtypeStruct((SEQ, D_MODEL), BF16), jax.ShapeDtypeStruct((SEQ, D_ATTN), F32)]
                  + ([act] * 2 if fused else []) + extra_shape,
        scratch_shapes=[pltpu.VMEM((BLOCK, D_POOL), F32), pltpu.VMEM((BLOCK, D_KV), F32),
                        pltpu.VMEM((BLOCK, D_KV), F32)] + extra_scratch,
        input_output_aliases=aliases, compiler_params=params,
    )(sinks, x, norm_pre, w_in_t, token, pool_w, pool_scale, bias, *(out if fused else ()),
      *(handed if handed is not None else ()))


def _fwd_out(layer, z, x, norm_post, w_out, token):
    tm = FWD_OUT_TILE

    def body(z_ref, x_ref, g_ref, w_ref, _, xn_ref, y_ref):
        y = _nn(z_ref[...], w_ref[...])
        y_ref[...] = y
        r = lax.rsqrt(jnp.mean(y * y, axis=-1, keepdims=True) + EPS)
        xn_ref[...] = x_ref[...] + y * r * g_ref[layer:layer + 1, :]

    row = lambda c: pl.BlockSpec((tm, c), lambda i: (i, 0))
    return pl.pallas_call(
        body, name=f"fwd_out{layer}", grid=(SEQ // tm,),
        in_specs=[row(D_MODEL), row(D_MODEL), pl.BlockSpec((DEPTH, D_MODEL), lambda i: (0, 0)),
                  _resident((D_MODEL, D_MODEL)), pl.BlockSpec((8, 128), lambda i: (0, 0))],
        out_specs=[row(D_MODEL), row(D_MODEL)],
        out_shape=[jax.ShapeDtypeStruct((SEQ, D_MODEL), F32), jax.ShapeDtypeStruct((SEQ, D_MODEL), F32)],
        compiler_params=_compiler_params(("arbitrary",)),
    )(z, x, norm_post, w_out, token)


BACK_TILE = 2 * BLOCK


def _bwd_back(layer, top, dxo_or_xf, target_or_token, y, z, norm_post, w_out, sinks, u, pg, q, k, v, ag, a,
              pool_w, pool_scale, bias):
    tm = BACK_TILE
    steps = SEQ // tm
    last = steps - 1
    per = tm // BLOCK

    def body(*refs):
        refs = list(refs)
        sink_ref, first, second = refs[:3]
        (y_ref, z_ref, g_ref, w_ref, u_ref, up_ref, pg_ref, q_ref, k_ref, v_ref, ag_ref, a_ref, pw_ref, sc_ref,
         bias_ref) = refs[3:18]
        del refs[:18]
        dxo_ref = refs.pop(0) if top else None
        dp_ref, dw_ref, pack_ref, acc, dg, lacc, dzs, ck, cv, ce = refs
        i = pl.program_id(0)
        blk = last - i

        @pl.when(i == 0)
        def _():
            acc[...] = jnp.zeros_like(acc)
            dg[...] = jnp.zeros_like(dg)
            lacc[...] = jnp.zeros_like(lacc)
            pack_ref[...] = jnp.zeros_like(pack_ref)
            ck[...] = jnp.zeros_like(ck)
            cv[...] = jnp.zeros_like(cv)
            ce[...] = jnp.zeros_like(ce)

        if top:
            d = first[...] - second[...]
            dxo_v = d * (1.0 / D_MODEL)
            dxo_ref[...] = dxo_v
            part = jnp.sum(d * d, axis=-1, keepdims=True) * (1.0 / D_MODEL)
            lacc[...] += 0.5 * jnp.sum(part, axis=0, keepdims=True)
        else:
            dxo_v = first[...]
        yv = y_ref[...]
        r = lax.rsqrt(jnp.mean(yv * yv, axis=-1, keepdims=True) + EPS)
        yn = yv * r
        dg[...] += jnp.sum(dxo_v * yn, axis=0, keepdims=True)
        dyn = dxo_v * g_ref[layer:layer + 1, :]
        dy = (r * (dyn - yn * jnp.mean(dyn * yn, axis=-1, keepdims=True))).astype(BF16)
        dzs[...] = _nt(dy, w_ref[...])
        acc[...] += _tn(z_ref[...], dy)

        lane = lax.broadcasted_iota(jnp.int32, (1, 128), 1)
        lane2 = lax.broadcasted_iota(jnp.int32, (256, 128), 1)
        current = _band_is_current()
        for sb in reversed(range(per)):
            n = per * blk + sb
            rows = slice(BLOCK * sb, BLOCK * (sb + 1))

            uv = u_ref[rows, :]
            if sb == 0:
                halo = up_ref[BLOCK - WINDOW_HALO:, :] * (n > 0).astype(F32)
            else:
                halo = u_ref[BLOCK * sb - WINDOW_HALO:BLOCK * sb, :]
            ext = jnp.concatenate([halo, uv], axis=0)
            for g, w in enumerate(POOL_WINDOWS):
                cs = slice(BLOCK * g, BLOCK * (g + 1))
                inv = _inv_count(n, w)
                win = _window_sum(ext[:, cs], w, forward=False)[WINDOW_HALO:]
                pooled = win * inv - uv[:, cs]
                pw_g = pw_ref[g].astype(BF16)
                mixed = _nn(pooled.astype(BF16), pw_g)
                gate, dgate = _silu_parts(pg_ref[rows, cs])
                dzp = dzs[rows, cs]
                sc = sc_ref[layer:layer + 1, cs]
                dpm = dzp * gate
                dp_ref[rows, COL_PG + BLOCK * g:COL_PG + BLOCK * (g + 1)] = (dzp * (mixed * sc) * dgate).astype(BF16)
                pack_ref[ROW_SC + g:ROW_SC + g + 1, :] += jnp.sum(dpm * mixed, axis=0, keepdims=True)
                dmixed = (dpm * sc).astype(BF16)
                pack_ref[ROW_PW + BLOCK * g:ROW_PW + BLOCK * (g + 1), :] += _tn(pooled.astype(BF16), dmixed)
                dpooled = _nt(dmixed, pw_g)
                e = dpooled * inv
                lead = _window_sum(jnp.concatenate([e, ce[:WINDOW_HALO, cs]], axis=0), w, forward=True)[:BLOCK]
                dp_ref[rows, COL_U + BLOCK * g:COL_U + BLOCK * (g + 1)] = (lead - dpooled).astype(BF16)
                ce[:, cs] = e

            kx = _kv_ext(k_ref, n)
            vx = _kv_ext(v_ref, n)
            variant = jnp.minimum(n, 1) if sb == 0 else 1
            dsink_row = jnp.zeros((1, 128), F32)
            tks, tvs = [], []
            for kv in range(2):
                cs = slice(256 * kv, 256 * (kv + 1))
                k_rep = _replicate_head(kx, kv)
                v_rep = _replicate_head(vx, kv)
                q_st = _stack_heads(q_ref[rows, cs])
                gate, dgate = _silu_parts(ag_ref[rows, cs])
                dza = dzs[rows, D_POOL + 256 * kv:D_POOL + 256 * (kv + 1)]
                dp_ref[rows, COL_AG + 256 * kv:COL_AG + 256 * (kv + 1)] = (dza * a_ref[rows, cs] * dgate).astype(BF16)
                da_st = _stack_heads((dza * gate).astype(BF16))
                p, psink = _probs_keys_major(k_rep, q_st, bias_ref[variant, kv], _sink_row(sink_ref, layer, kv),
                                             current)
                dpt = _pack_band(_nt(v_rep, da_st), current)
                delta = jnp.sum(p * dpt, axis=0, keepdims=True)
                dst = _unpack_band((p * (dpt - delta) * SCALE).astype(BF16), current)
                sink_terms = psink * delta
                for g in range(GQA):
                    dsink = -jnp.sum(sink_terms[:, BLOCK * g:BLOCK * (g + 1)], axis=1, keepdims=True)
                    dsink_row = dsink_row + jnp.where(lane == kv * GQA + g, dsink, 0.0)
                dp_ref[rows, COL_Q + 256 * kv:COL_Q + 256 * (kv + 1)] = _unstack_heads(_tn(dst, k_rep)).astype(BF16)
                tks.append(_fold_heads(_nn(dst, q_st)))
                tvs.append(_fold_heads(_nn(_unpack_band(p.astype(BF16), current), da_st)))
            pack_ref[ROW_SINK:ROW_SINK + 1, :] += dsink_row
            dkx = jnp.where(lane2 < 64, tks[0], tks[1])
            dvx = jnp.where(lane2 < 64, tvs[0], tvs[1])
            dp_ref[rows, COL_K:COL_V] = (ck[...] + dkx[BLOCK:]).astype(BF16)
            dp_ref[rows, COL_V:COL_AG] = (cv[...] + dvx[BLOCK:]).astype(BF16)
            ck[...] = dkx[:BLOCK]
            cv[...] = dvx[:BLOCK]

        @pl.when(i == steps - 1)
        def _():
            dw_ref[...] = acc[...].astype(BF16)
            _rows_of(dg, pack_ref, ROW_NPOST)
            pack_ref[ROW_LOSS:ROW_LOSS + 1, :] = jnp.where(lane == 0, lacc[...], 0.0)

    row = lambda c: pl.BlockSpec((tm, c), lambda i: (last - i, 0))
    const = lambda shape: pl.BlockSpec(shape, lambda i: (0,) * len(shape))
    act = jax.ShapeDtypeStruct((SEQ, D_MODEL), F32)
    return pl.pallas_call(
        body, name=f"bwd_back{layer}", grid=(steps,),
        in_specs=[pl.BlockSpec(memory_space=pltpu.SMEM), row(D_MODEL), row(D_MODEL) if top else const((8, 128)),
                  row(D_MODEL), row(D_MODEL), const((DEPTH, D_MODEL)), _resident((D_MODEL, D_MODEL)),
                  row(D_POOL), pl.BlockSpec((BLOCK, D_POOL), lambda i: (jnp.maximum(per * (last - i) - 1, 0), 0)),
                  row(D_POOL), row(D_ATTN), _resident((SEQ, D_KV)), _resident((SEQ, D_KV)), row(D_ATTN), row(D_ATTN),
                  pl.BlockSpec((None, 4, BLOCK, BLOCK), lambda i: (layer, 0, 0, 0)), const((DEPTH, D_POOL)),
                  _resident((2, 2, BLOCK, GQA * BLOCK))],
        out_specs=([row(D_MODEL)] * (1 if top else 0)
                   + [row(D_IN), const((D_MODEL, D_MODEL)), const((PACK_ROWS, 128))]),
        out_shape=([act] * (1 if top else 0)
                   + [jax.ShapeDtypeStruct((SEQ, D_IN), BF16), jax.ShapeDtypeStruct((D_MODEL, D_MODEL), BF16),
                      jax.ShapeDtypeStruct((PACK_ROWS, 128), F32)]),
        scratch_shapes=[pltpu.VMEM((D_MODEL, D_MODEL), F32), pltpu.VMEM((1, D_MODEL), F32), pltpu.VMEM((1, 1), F32),
                        pltpu.VMEM((tm, D_MODEL), F32), pltpu.VMEM((BLOCK, D_KV), F32), pltpu.VMEM((BLOCK, D_KV), F32),
                        pltpu.VMEM((BLOCK, D_POOL), F32)],
        compiler_params=_compiler_params(("arbitrary",)),
    )(sinks, dxo_or_xf, target_or_token, y, z, norm_post, w_out, u, u, pg, q, k, v, ag, a, pool_w, pool_scale, bias)


def _bwd_in(layer, part, token, dproj, x, norm_pre, dxo=None, w_in_t=None):
    pair = part in ("dw_pair", "both_pair")
    want_dw, want_dx = part != "dx", part in ("both", "dx", "both_pair")
    tm = TOKEN_TILE
    steps = SEQ // tm
    cw = 256

    def body(*refs):
        refs = list(refs)
        dp_ref, x_ref, g_ref = refs[1:4]
        del refs[:4]
        if want_dx:
            dxo_ref, w_ref, dx_ref, dgo_ref = refs[:4]
            del refs[:4]
            dg = refs.pop()
        if pair:
            hs_ref, hm_ref, acc, mine_buf, theirs_buf, send_sem, recv_sem = refs
        elif want_dw:
            dw_ref, acc = refs
        i = pl.program_id(0)

        @pl.when(i == 0)
        def _():
            if pair:
                _handshake([(lax.axis_index("x"), lax.axis_index("y"), 1 - lax.axis_index("c"))])
            if want_dw:
                acc[...] = jnp.zeros_like(acc)
            if want_dx:
                dg[...] = jnp.zeros_like(dg)

        xv = x_ref[...]
        gv = g_ref[layer:layer + 1, :]
        r = lax.rsqrt(jnp.mean(xv * xv, axis=-1, keepdims=True) + EPS)
        xn = xv * r
        if want_dw:
            hb = (xn * gv).astype(BF16)
            for c in range(0, D_IN, cw):
                acc[c:c + cw, :] += _tn(dp_ref[:, c:c + cw], hb)
        def rows_for(q, core):
            return pl.ds(pl.multiple_of((2 * q + core) * IN_SHARD, 8), IN_SHARD)

        def swap(q):
            x, y, c = _mesh_pos()
            return pltpu.make_async_remote_copy(
                src_ref=mine_buf.at[q], dst_ref=theirs_buf.at[q], send_sem=send_sem.at[q], recv_sem=recv_sem.at[q],
                device_id=(x, y, 1 - c), device_id_type=MESH)

        if pair:
            @pl.when(i == steps - 1)
            def _():
                for q in range(4):
                    mine_buf[q] = acc[rows_for(q, 1 - lax.axis_index("c")), :].astype(BF16)
                    swap(q).start()

        if want_dx:
            dh = _nn(dp_ref[...], w_ref[...])
            dg[...] += jnp.sum(dh * xn, axis=0, keepdims=True)
            dhn = dh * gv
            dx_ref[...] = dxo_ref[...] + r * (dhn - xn * jnp.mean(dhn * xn, axis=-1, keepdims=True))

        @pl.when(i == steps - 1)
        def _():
            if pair:
                x, y, c = _mesh_pos()
                for q in range(4):
                    swap(q).wait()
                for j, q in enumerate([2 * (1 - x) + y, 2 * x + (1 - y), 2 * (1 - x) + (1 - y)]):
                    hs_ref[j] = (acc[rows_for(q, c), :] + theirs_buf[q].astype(F32)).astype(BF16)
                hm_ref[...] = acc[rows_for(2 * x + y, c), :] + theirs_buf[2 * x + y].astype(F32)
            elif want_dw:
                dw_ref[...] = acc[...].astype(BF16)
            if want_dx:
                _rows_of(dg, dgo_ref, 0)

    row = lambda c: pl.BlockSpec((tm, c), lambda i: (i, 0))
    const = lambda shape: pl.BlockSpec(shape, lambda i: (0,) * len(shape))
    in_specs = [const((8, 128)), row(D_IN), row(D_MODEL), const((DEPTH, D_MODEL))]
    operands = [token, dproj, x, norm_pre]
    out_specs, out_shape, scratch = [], [], []
    if want_dx:
        in_specs += [row(D_MODEL), _resident((D_IN, D_MODEL))]
        operands += [dxo, w_in_t]
        out_specs += [row(D_MODEL), const((8, 128))]
        out_shape += [jax.ShapeDtypeStruct((SEQ, D_MODEL), F32), jax.ShapeDtypeStruct((8, 128), F32)]
    if pair:
        out_specs += [const((3, IN_SHARD, D_MODEL)), const((IN_SHARD, D_MODEL))]
        out_shape += [jax.ShapeDtypeStruct((3, IN_SHARD, D_MODEL), BF16), jax.ShapeDtypeStruct((IN_SHARD, D_MODEL), F32)]
        scratch += [pltpu.VMEM((D_IN, D_MODEL), F32), pltpu.VMEM((4, IN_SHARD, D_MODEL), BF16),
                    pltpu.VMEM((4, IN_SHARD, D_MODEL), BF16), pltpu.SemaphoreType.DMA((4,)), pltpu.SemaphoreType.DMA((4,))]
    elif want_dw:
        out_specs.append(const((D_IN, D_MODEL)))
        out_shape.append(jax.ShapeDtypeStruct((D_IN, D_MODEL), BF16))
        scratch.append(pltpu.VMEM((D_IN, D_MODEL), F32))
    if want_dx:
        scratch.append(pltpu.VMEM((1, D_MODEL), F32))
    params = pltpu.CompilerParams(dimension_semantics=("arbitrary",), vmem_limit_bytes=VMEM_LIMIT,
                                  collective_id=COLLECTIVE_PAIR_SUM[layer] if pair else None)
    return pl.pallas_call(
        body, name=f"bwd_in_{part}{layer}", grid=(steps,),
        in_specs=in_specs, out_specs=out_specs, out_shape=out_shape, scratch_shapes=scratch,
        compiler_params=params,
    )(*operands)


def _mesh_pos():
    return lax.axis_index("x"), lax.axis_index("y"), lax.axis_index("c")


def _device_rows(ref, m, px, py, pc):
    return ref.at[pl.ds(pl.multiple_of((4 * px + 2 * py + pc) * m, 16 if m % 16 == 0 else 8), m), :]


def _allgather(srcs, out_dtype, name, later=()):
    na, nb = len(srcs), len(later)
    every = list(srcs) + list(later)
    shapes = [(a.shape[-2], a.shape[-1]) for a, _ in every]

    def body(*refs):
        xs, refs = refs[:na + nb], refs[na + nb:]
        outs, cast, land, refs = refs[:na], refs[na:na + nb], refs[na + nb:na + 2 * nb], refs[na + 2 * nb:]
        stage, raw, (send_sems, recv_sems, local_sems, load_sems) = refs[:na], refs[na:2 * na + nb], refs[2 * na + nb:]
        loads = [pltpu.make_async_copy(xs[i].at[every[i][1]], raw[i], load_sems.at[i]) for i in range(na + nb)]
        for cp in loads:
            cp.start()
        x, y, c = _mesh_pos()
        me, sibling = (x, y, c), (x, y, 1 - c)
        near = [(1 - x, y), (x, 1 - y)]
        far = (1 - x, 1 - y)
        relay_from, relay_to = (x ^ (1 - c), y ^ c), (x ^ c, y ^ (1 - c))
        _handshake([sibling] + [(*chip, c) for chip in near])
        k_from, k_to = 1 + c, 2 - c

        def slot(a, px, py, pc):
            return _device_rows(outs[a], shapes[a][0], px, py, pc)

        def copy(a, k, block, to, src=None):
            return pltpu.make_async_remote_copy(
                src_ref=slot(a, *block) if src is None else src, dst_ref=slot(a, *block),
                send_sem=send_sems.at[a, k], recv_sem=recv_sems.at[a, k], device_id=to, device_id_type=MESH)

        def cast_block(i):
            loads[i].wait()
            return raw[i][...].astype(out_dtype)

        for a in range(na):
            stage[a][...] = cast_block(a)
        mine = [pltpu.make_async_copy(stage[a], slot(a, *me), local_sems.at[a]) for a in range(na)]
        for cp in mine:
            cp.start()
        sent = []
        for a in range(na):
            sent.append(copy(a, 0, me, sibling, src=stage[a]))
            sent += [copy(a, 1 + j, me, (*chip, c), src=stage[a]) for j, chip in enumerate(near)]
        for cp in sent:
            cp.start()
        for b in range(nb):
            cast[b][...] = cast_block(na + b)
            cp = pltpu.make_async_copy(cast[b], _device_rows(land[b], shapes[na + b][0], *me), local_sems.at[na + b])
            cp.start()
            mine.append(cp)
        for a in range(na):
            copy(a, k_from, (*relay_from, c), me).wait_recv()
            sent += [copy(a, 3, (*relay_from, c), (*relay_to, c)), copy(a, 3 + k_from, (*relay_from, c), sibling)]
            sent[-2].start()
            sent[-1].start()
        for a in range(na):
            copy(a, k_to, (*relay_to, c), me).wait_recv()
            sent.append(copy(a, 3 + k_to, (*relay_to, c), sibling))
            sent[-1].start()
        for a in range(na):
            copy(a, 3, (*far, c), me).wait_recv()
            sent.append(copy(a, 6, (*far, c), sibling))
            sent[-1].start()
        for a in range(na):
            copy(a, 0, sibling, me).wait_recv()
            for j, chip in enumerate(near + [far]):
                copy(a, 4 + j, (*chip, 1 - c), me).wait_recv()
        for cp in sent:
            cp.wait_send()
        for cp in mine:
            cp.wait()

    vmem = pl.BlockSpec(memory_space=pltpu.VMEM)
    hbm = pl.BlockSpec(memory_space=pl.ANY)
    gathered = [jax.ShapeDtypeStruct((N_DEV * m, n), out_dtype) for m, n in shapes]
    out = pl.pallas_call(
        body, name=name,
        in_specs=[hbm] * (na + nb),
        out_specs=[hbm] * na + [vmem] * nb + [hbm] * nb,
        out_shape=gathered[:na] + [jax.ShapeDtypeStruct(s, out_dtype) for s in shapes[na:]] + gathered[na:],
        scratch_shapes=([pltpu.VMEM(s, out_dtype) for s in shapes[:na]]
                        + [pltpu.VMEM(s, a.dtype) for s, (a, _) in zip(shapes, every)]
                        + [pltpu.SemaphoreType.DMA((na, 7)), pltpu.SemaphoreType.DMA((na, 7)),
                           pltpu.SemaphoreType.DMA((na + nb,)), pltpu.SemaphoreType.DMA((na + nb,))]),
        compiler_params=pltpu.CompilerParams(vmem_limit_bytes=VMEM_LIMIT, collective_id=COLLECTIVE_GATHER_W0),
    )(*[a for a, _ in every])
    return out[:na], out[na:na + nb], out[na + nb:]


ALL_PEERS = tuple(range(1, N_DEV))
SIBLING_AND_SAME_CORE = (1, 2, 4, 6)


def _related(k, x, y, c):
    return x ^ ((k >> 2) & 1), y ^ ((k >> 1) & 1), c ^ (k & 1)


def _gather_start(blocks, lands, relations, collective_id, name):
    na = len(blocks)

    def body(*refs):
        src, land, sems, token = refs[:na], refs[na:2 * na], refs[2 * na:4 * na], refs[-1]
        x, y, c = _mesh_pos()
        _handshake([_related(k, x, y, c) for k in sorted(set().union(*relations))])
        for a in range(na):
            for k in relations[a]:
                pltpu.make_async_remote_copy(
                    src_ref=src[a], dst_ref=_device_rows(land[a], blocks[a].shape[0], x, y, c),
                    send_sem=sems[2 * a].at[k - 1], recv_sem=sems[2 * a + 1].at[k - 1],
                    device_id=_related(k, x, y, c), device_id_type=MESH).start()
        token[...] = jnp.zeros_like(token)

    bufs = [pltpu.HBM(t.shape, t.dtype) for t in list(blocks) + list(lands)]
    out = pl.pallas_call(
        body, name=name,
        out_shape=(*([pltpu.SemaphoreType.DMA((N_DEV - 1,))] * (2 * na)), *bufs, jax.ShapeDtypeStruct((8, 128), F32)),
        in_specs=[_HBM] * (2 * na),
        out_specs=(*([_SEM] * (2 * na)), *([_HBM] * (2 * na)), pl.BlockSpec(memory_space=pltpu.VMEM)),
        input_output_aliases={i: 2 * na + i for i in range(2 * na)},
        compiler_params=pltpu.CompilerParams(has_side_effects=_EFFECT, collective_id=collective_id),
    )(*[pltpu.with_memory_space_constraint(t, pltpu.HBM) for t in list(blocks) + list(lands)])
    sems = [(out[2 * a], out[2 * a + 1]) for a in range(na)]
    return sems, out[2 * na:3 * na], out[3 * na:4 * na], out[-1]


def _gather_wait(sems, block, land, relations, after, name):
    def body(src, land_ref, send_sem, recv_sem, after_ref, src_out, land_out):
        x, y, c = _mesh_pos()
        for k in relations:
            peer = _related(k, x, y, c)
            cp = pltpu.make_async_remote_copy(
                src_ref=src, dst_ref=_device_rows(land_ref, block.shape[0], *peer),
                send_sem=send_sem.at[k - 1], recv_sem=recv_sem.at[k - 1], device_id=peer, device_id_type=MESH)
            cp.wait_send()
            cp.wait_recv()

    out = pl.pallas_call(
        body, name=name,
        out_shape=(pltpu.HBM(block.shape, block.dtype), pltpu.HBM(land.shape, land.dtype)),
        in_specs=[_HBM, _HBM, _SEM, _SEM, pl.BlockSpec(memory_space=pl.ANY)],
        out_specs=[_HBM, _HBM],
        input_output_aliases={0: 0, 1: 1},
        compiler_params=pltpu.CompilerParams(has_side_effects=_EFFECT),
    )(block, land, sems[0], sems[1], after)
    return out[1]


(COLLECTIVE_GATHER_W0, COLLECTIVE_GATHER_W1, COLLECTIVE_FORWARD_W_IN1, COLLECTIVE_EXCHANGE_1, COLLECTIVE_EXCHANGE_0A,
 COLLECTIVE_EXCHANGE_0B, COLLECTIVE_GATHER_SMALL) = range(1, 8)
COLLECTIVE_PAIR_SUM = (8, 9)
COLLECTIVE_FORWARD_W_OUT1 = 10


def _handshake(peers):
    barrier = pltpu.get_barrier_semaphore()
    for peer in peers:
        pl.semaphore_signal(barrier, inc=1, device_id=peer, device_id_type=MESH)
    pl.semaphore_wait(barrier, len(peers))


def _forward_plan(land_ref, m):
    x, y, c = _mesh_pos()
    return [_device_rows(land_ref, m, qx, qy, c) for qx, qy in ((1 - x, y), (x, 1 - y), (1 - x, 1 - y))], (x, y, 1 - c)


def _forward_start(land, m, collective_id, name):
    def body(land_ref, send_sem, recv_sem, land_out, token):
        _handshake([(lax.axis_index("x"), lax.axis_index("y"), 1 - lax.axis_index("c"))])
        rows, sibling = _forward_plan(land_ref, m)
        for j, r in enumerate(rows):
            pltpu.make_async_remote_copy(src_ref=r, dst_ref=r, send_sem=send_sem.at[j], recv_sem=recv_sem.at[j],
                                         device_id=sibling, device_id_type=MESH).start()
        token[...] = jnp.zeros_like(token)

    out = pl.pallas_call(
        body, name=name,
        out_shape=(pltpu.SemaphoreType.DMA((3,)), pltpu.SemaphoreType.DMA((3,)), pltpu.HBM(land.shape, land.dtype),
                   jax.ShapeDtypeStruct((8, 128), F32)),
        in_specs=[_HBM],
        out_specs=(_SEM, _SEM, _HBM, pl.BlockSpec(memory_space=pltpu.VMEM)),
        input_output_aliases={0: 2},
        compiler_params=pltpu.CompilerParams(has_side_effects=_EFFECT, collective_id=collective_id),
    )(pltpu.with_memory_space_constraint(land, pltpu.HBM))
    return (out[0], out[1]), out[2], out[3]


def _forward_wait_copies(land_ref, m, send_sem, recv_sem):
    x, y, c = _mesh_pos()
    mine, sibling = _forward_plan(land_ref, m)
    theirs = [_device_rows(land_ref, m, qx, qy, 1 - c) for qx, qy in ((1 - x, y), (x, 1 - y), (1 - x, 1 - y))]
    for j in range(3):
        cp = pltpu.make_async_remote_copy(src_ref=mine[j], dst_ref=theirs[j], send_sem=send_sem.at[j],
                                          recv_sem=recv_sem.at[j], device_id=sibling, device_id_type=MESH)
        cp.wait_send()
        cp.wait_recv()


def _forward_wait(sems, land, m, after, name):
    def body(land_ref, send_sem, recv_sem, after_ref, land_out):
        _forward_wait_copies(land_ref, m, send_sem, recv_sem)

    return pl.pallas_call(
        body, name=name,
        out_shape=pltpu.HBM(land.shape, land.dtype),
        in_specs=[_HBM, _SEM, _SEM, pl.BlockSpec(memory_space=pl.ANY)],
        out_specs=_HBM,
        input_output_aliases={0: 0},
        compiler_params=pltpu.CompilerParams(has_side_effects=_EFFECT),
    )(land, sems[0], sems[1], after)


_HBM = pl.BlockSpec(memory_space=pltpu.HBM)
_SEM = pl.BlockSpec(memory_space=pltpu.SEMAPHORE)
_EFFECT = pltpu.SideEffectType.DATAFLOW_SIDE_EFFECTING


def _exchange_plan(direct):
    x, y, c = _mesh_pos()
    if not direct:
        return [(j, j, (qx, qy, c)) for j, (qx, qy) in enumerate([(1 - x, y), (x, 1 - y), (1 - x, 1 - y)])]
    plan = []
    for k in range(1, N_DEV):
        px, py, pc = x ^ ((k >> 2) & 1), y ^ ((k >> 1) & 1), c ^ (k & 1)
        plan.append((4 * px + 2 * py + pc, k - 1, (px, py, pc)))
    return plan


def _exchange_copies(directs):
    copies, base = [], 0
    for a, direct in enumerate(directs):
        plan = _exchange_plan(direct)
        copies += [(a, block, slot, peer, base + slot) for block, slot, peer in plan]
        base += len(plan)
    return copies, base


def _exchange_start(srcs, directs, collective_id, name):
    na = len(srcs)
    slots = [N_DEV - 1 if direct else 3 for direct in directs]

    def body(*refs):
        src, land = refs[:na], refs[na:2 * na]
        send_sem, recv_sem = refs[2 * na], refs[2 * na + 1]
        token = refs[-1]
        _handshake([peer for _, _, peer in _exchange_plan(any(directs))])
        for a, block, slot, peer, sem in _exchange_copies(directs)[0]:
            pltpu.make_async_remote_copy(
                src_ref=src[a].at[block], dst_ref=land[a].at[slot], send_sem=send_sem.at[sem],
                recv_sem=recv_sem.at[sem], device_id=peer, device_id_type=MESH).start()
        token[...] = jnp.zeros_like(token)

    zones = [jax.ShapeDtypeStruct((n,) + t.shape[1:], t.dtype) for n, t in zip(slots, srcs)]
    bufs = [pltpu.HBM(t.shape, t.dtype) for t in list(srcs) + zones]
    out = pl.pallas_call(
        body, name=name,
        out_shape=(pltpu.SemaphoreType.DMA((sum(slots),)), pltpu.SemaphoreType.DMA((sum(slots),)), *bufs,
                   jax.ShapeDtypeStruct((8, 128), F32)),
        in_specs=[_HBM] * (2 * na),
        out_specs=(_SEM, _SEM, *([_HBM] * (2 * na)), pl.BlockSpec(memory_space=pltpu.VMEM)),
        input_output_aliases={i: 2 + i for i in range(2 * na)},
        compiler_params=pltpu.CompilerParams(has_side_effects=_EFFECT, collective_id=collective_id),
    )(*[pltpu.with_memory_space_constraint(t, pltpu.HBM) for t in srcs],
      *[pltpu.with_memory_space_constraint(lax.empty(t.shape, t.dtype), pltpu.HBM) for t in zones])
    return out[0], out[1], out[2:2 + na], out[2 + na:2 + 2 * na], out[-1]


def _exchange_wait(send_sem, recv_sem, srcs, lands, directs, after, name):
    na = len(srcs)

    def body(*refs):
        src, land = refs[:na], refs[na:2 * na]
        send_sem_ref, recv_sem_ref = refs[2 * na], refs[2 * na + 1]
        for a, block, slot, peer, sem in _exchange_copies(directs)[0]:
            cp = pltpu.make_async_remote_copy(
                src_ref=src[a].at[block], dst_ref=land[a].at[slot], send_sem=send_sem_ref.at[sem],
                recv_sem=recv_sem_ref.at[sem], device_id=peer, device_id_type=MESH)
            cp.wait_send()
            cp.wait_recv()

    bufs = [pltpu.HBM(t.shape, t.dtype) for t in list(srcs) + list(lands)]
    out = pl.pallas_call(
        body, name=name,
        out_shape=tuple(bufs),
        in_specs=[_HBM] * (2 * na) + [_SEM, _SEM, pl.BlockSpec(memory_space=pl.ANY)],
        out_specs=[_HBM] * (2 * na),
        input_output_aliases={i: i for i in range(2 * na)},
        compiler_params=pltpu.CompilerParams(has_side_effects=_EFFECT),
    )(*srcs, *lands, send_sem, recv_sem, after)
    return out[:na], out[na:]


def _own_then_slots(mine_ref, lands_ref, rows=slice(None)):
    if len(mine_ref.shape) == 3:
        x, y, c = _mesh_pos()
        total = mine_ref[4 * x + 2 * y + c, rows, :].astype(F32)
    else:
        total = mine_ref[rows, :].astype(F32)
    for j in range(lands_ref.shape[0]):
        total = total + lands_ref[j, rows, :].astype(F32)
    return total


SMALL_ROWS = 2 * PACK_SLICE + 2 * 8


def _small_gather_start(mine, lands, dgpre, name):
    def body(*refs):
        hm, ld, dg = refs[:DEPTH], refs[DEPTH:2 * DEPTH], refs[2 * DEPTH:3 * DEPTH]
        send_sem, recv_sem, blk, land, token, own, slots, rows, built, local_sems = refs[3 * DEPTH:]
        x, y, c = _mesh_pos()
        loads = []
        for l in range(DEPTH):
            loads += [pltpu.make_async_copy(hm[l].at[4 * x + 2 * y + c], own.at[l], local_sems.at[3 * l]),
                      pltpu.make_async_copy(ld[l], slots.at[l], local_sems.at[3 * l + 1]),
                      pltpu.make_async_copy(dg[l], rows.at[l], local_sems.at[3 * l + 2])]
        for cp in loads:
            cp.start()
        _handshake([_related(k, x, y, c) for k in ALL_PEERS])
        for cp in loads:
            cp.wait()
        for l in range(DEPTH):
            total = own[l]
            for j in range(N_DEV - 1):
                total = total + slots[l, j]
            built[PACK_SLICE * l:PACK_SLICE * (l + 1), :] = total
            built[2 * PACK_SLICE + 8 * l:2 * PACK_SLICE + 8 * (l + 1), :] = rows[l]
        stores = [pltpu.make_async_copy(built, blk, local_sems.at[3 * DEPTH]),
                  pltpu.make_async_copy(built, _device_rows(land, SMALL_ROWS, x, y, c), local_sems.at[3 * DEPTH + 1])]
        for cp in stores:
            cp.start()
        for cp in stores:
            cp.wait()
        for k in ALL_PEERS:
            pltpu.make_async_remote_copy(
                src_ref=blk, dst_ref=_device_rows(land, SMALL_ROWS, x, y, c), send_sem=send_sem.at[k - 1],
                recv_sem=recv_sem.at[k - 1], device_id=_related(k, x, y, c), device_id_type=MESH).start()
        token[...] = jnp.zeros_like(token)

    hbm = pl.BlockSpec(memory_space=pl.ANY)
    out = pl.pallas_call(
        body, name=name,
        in_specs=[hbm] * (3 * DEPTH),
        out_specs=(_SEM, _SEM, _HBM, _HBM, pl.BlockSpec(memory_space=pltpu.VMEM)),
        out_shape=(pltpu.SemaphoreType.DMA((N_DEV - 1,)), pltpu.SemaphoreType.DMA((N_DEV - 1,)),
                   pltpu.HBM((SMALL_ROWS, 128), F32), pltpu.HBM((N_DEV * SMALL_ROWS, 128), F32),
                   jax.ShapeDtypeStruct((8, 128), F32)),
        scratch_shapes=[pltpu.VMEM((DEPTH, PACK_SLICE, 128), F32), pltpu.VMEM((DEPTH, N_DEV - 1, PACK_SLICE, 128), F32),
                        pltpu.VMEM((DEPTH, 8, 128), F32), pltpu.VMEM((SMALL_ROWS, 128), F32),
                        pltpu.SemaphoreType.DMA((3 * DEPTH + 2,))],
        compiler_params=pltpu.CompilerParams(has_side_effects=_EFFECT, collective_id=COLLECTIVE_GATHER_SMALL),
    )(*mine, *lands, *dgpre)
    return (out[0], out[1]), out[2], out[3], out[4]


def _adamw_math(w, g, m, v):
    m = ADAM_B1 * m + (1.0 - ADAM_B1) * g
    v = ADAM_B2 * v + (1.0 - ADAM_B2) * (g * g)
    m_hat = m / (1.0 - ADAM_B1 ** ADAM_STEP)
    v_hat = v / (1.0 - ADAM_B2 ** ADAM_STEP)
    delta = -ADAM_LR * (m_hat / (jnp.sqrt(v_hat) + ADAM_EPS) + ADAM_WD * w)
    return delta, m, v


def _adamw_layer(layer, parts, token, name):
    steps = {w.shape[1] // rows for _, _, w, _, _, _, rows in parts}
    assert len(steps) == 1, steps
    n = len(parts)

    def body(_, *refs):
        for p in range(n):
            hm_ref, ld_ref, w_ref, m_ref, v_ref = refs[5 * p:5 * (p + 1)]
            g_ref, d_ref, nm_ref, nv_ref = refs[len(refs) - 4 * (n - p):len(refs) - 4 * (n - p - 1)]
            g = _own_then_slots(hm_ref, ld_ref)
            g_ref[...] = g
            d, nm, nv = _adamw_math(w_ref[...], g, m_ref[...], v_ref[...])
            d_ref[...] = d
            nm_ref[...] = nm
            nv_ref[...] = nv

    in_specs, out_specs, out_shape, operands, carried, aliases = [pl.BlockSpec(memory_space=pl.ANY)], [], [], [], [], {}
    for p, (mine, lands, w, m, v, earlier, rows) in enumerate(parts):
        nn = w.shape[2]
        spec = pl.BlockSpec((None, rows, nn), lambda i: (layer, i, 0))
        in_specs += [pl.BlockSpec((rows, nn), lambda i: (i, 0)) if mine.ndim == 2
                     else pl.BlockSpec((N_DEV, rows, nn), lambda i: (0, i, 0)),
                     pl.BlockSpec((lands.shape[0], rows, nn), lambda i: (0, i, 0)), spec, spec, spec]
        out_specs += [spec] * 4
        out_shape += [jax.ShapeDtypeStruct(w.shape, F32)] * 4
        operands += [mine, lands, w, m, v]
        if earlier is not None:
            aliases.update({1 + 5 * n + len(carried) + t: 4 * p + t for t in range(4)})
            carried += list(earlier)
    out = pl.pallas_call(
        body, name=name, grid=(steps.pop(),),
        in_specs=in_specs + [pl.BlockSpec(memory_space=pl.ANY)] * len(carried),
        out_specs=out_specs, out_shape=out_shape, input_output_aliases=aliases,
        compiler_params=_compiler_params(("arbitrary",)),
    )(token, *operands, *carried)
    return [out[4 * p:4 * (p + 1)] for p in range(n)]


def _adamw_small(gathered, params):
    def body(all_ref, *refs):
        ins, outs, packs = refs[:15], refs[15:15 + 21], refs[15 + 21]
        loss_ref = outs[0]
        for dev in range(N_DEV):
            for l in range(DEPTH):
                packs[l, PACK_SLICE * dev:PACK_SLICE * (dev + 1), :] = (
                    all_ref[SMALL_ROWS * dev + PACK_SLICE * l:SMALL_ROWS * dev + PACK_SLICE * (l + 1), :])
        loss_ref[...] = packs[DEPTH - 1, ROW_LOSS:ROW_LOSS + 1, 0:1]

        def update(p, sel, g):
            w_ref, m_ref, v_ref = ins[p], ins[5 + p], ins[10 + p]
            d, nm, nv = _adamw_math(w_ref[sel], g, m_ref[sel], v_ref[sel])
            for t, val in enumerate((g, d, nm, nv)):
                outs[1 + 5 * t + p][sel] = val

        for l in range(DEPTH):
            gp = packs.at[l]
            row0 = 2 * PACK_SLICE + 8 * l
            dgpre = all_ref[row0:row0 + 8, :]
            for dev in range(1, N_DEV):
                dgpre = dgpre + all_ref[SMALL_ROWS * dev + row0:SMALL_ROWS * dev + row0 + 8, :]
            for grp in range(4):
                update(0, (l, grp), gp[ROW_PW + BLOCK * grp:ROW_PW + BLOCK * (grp + 1), :])
                update(1, (slice(l, l + 1), slice(128 * grp, 128 * (grp + 1))), gp[ROW_SC + grp:ROW_SC + grp + 1, :])
            update(2, (slice(l, l + 1), slice(None)), gp[ROW_SINK:ROW_SINK + 1, 0:N_HEADS])
            for r in range(D_MODEL // 128):
                sel = (slice(l, l + 1), slice(128 * r, 128 * (r + 1)))
                update(3, sel, dgpre[r:r + 1, :])
                update(4, sel, gp[ROW_NPOST + r:ROW_NPOST + r + 1, :])

    shapes = [jax.ShapeDtypeStruct(p.shape, F32) for p in params[:5]]
    return pl.pallas_call(
        body, name="adamw_small",
        out_shape=[jax.ShapeDtypeStruct((1, 1), F32)] + shapes * 4,
        scratch_shapes=[pltpu.VMEM((DEPTH, PACK_ROWS, 128), F32)],
        compiler_params=_compiler_params(),
    )(gathered, *params)


def kernel(x, w_in, pool_w, pool_scale, attn_sinks, w_out, norm_pre, norm_post, loss_target, m_w_in, m_pool_w, m_pool_scale, m_attn_sinks, m_w_out, m_norm_pre, m_norm_post, v_w_in, v_pool_w, v_pool_scale, v_attn_sinks, v_w_out, v_norm_pre, v_norm_post):
    x0 = x.reshape(SEQ, D_MODEL)
    target = loss_target.reshape(SEQ, D_MODEL)
    bias = jnp.asarray(_attn_bias())
    w_in_t, m_in_t, v_in_t = (jnp.swapaxes(t, 1, 2) for t in (w_in, m_w_in, v_w_in))

    (win0, wout0), later, lands = _allgather([(w_in_t, 0), (w_out, 0)], BF16, "gather_w0",
                                              later=[(w_in_t, 1), (w_out, 1)])
    sems, later, lands, token = _gather_start(later, lands, [SIBLING_AND_SAME_CORE] * 2, COLLECTIVE_GATHER_W1,
                                              "gather_w1_start")
    win_full, wout_full = [win0, None], [wout0, None]

    saved = []
    xl = x0
    for layer in range(DEPTH):
        front = (layer, xl, norm_pre, win_full[layer], token, attn_sinks, pool_w, pool_scale, bias)
        if layer == 0:
            u, pg, q, k, v, ag, z, a = _fwd_front(*front)
            land = _gather_wait(sems[0], later[0], lands[0], SIBLING_AND_SAME_CORE, z, "gather_w_in1_wait")
            fsems, land, token = _forward_start(land, IN_SHARD, COLLECTIVE_FORWARD_W_IN1, "forward_w_in1_start")
            x_next, y = _fwd_out(layer, z, xl, norm_post, wout_full[layer], token)
            win_full[1] = _forward_wait(fsems, land, IN_SHARD, x_next, "forward_w_in1_wait")
            land = _gather_wait(sems[1], later[1], lands[1], SIBLING_AND_SAME_CORE, win_full[1], "gather_w_out1_wait")
            handed, wout_full[1], token = _forward_start(land, OUT_SHARD, COLLECTIVE_FORWARD_W_OUT1,
                                                         "forward_w_out1_start")
        else:
            u, pg, q, k, v, ag, z, a, x_next, y, wout_full[1] = _fwd_front(*front, out=(norm_post, wout_full[layer]),
                                                                           handed=handed)
        saved.append((xl, u, pg, q, k, v, ag, z, a, y))
        xl = x_next

    params_small = [pool_w, pool_scale, attn_sinks, norm_pre, norm_post,
                    m_pool_w, m_pool_scale, m_attn_sinks, m_norm_pre, m_norm_post,
                    v_pool_w, v_pool_scale, v_attn_sinks, v_norm_pre, v_norm_post]

    def start(srcs, directs, paired, collective_id, tag):
        send_sem, recv_sem, srcs, lands, started = _exchange_start(srcs, directs, collective_id, f"exchange_start{tag}")
        return (send_sem, recv_sem, srcs, lands, paired, directs), started

    def finish(handle, after, tag):
        send_sem, recv_sem, srcs, lands, paired, directs = handle
        srcs, lands = _exchange_wait(send_sem, recv_sem, srcs, lands, directs, after, f"exchange_wait{tag}")
        return [s if p is None else p for s, p in zip(srcs, paired)], lands

    def back(layer, top, first, second):
        xin, u, pg, q, k, v, ag, z, a, y = saved[layer]
        return _bwd_back(layer, top, first, second, y, z, norm_post, wout_full[layer], attn_sinks, u, pg, q, k, v,
                         ag, a, pool_w, pool_scale, bias)

    dgpre = [None] * DEPTH
    dx, dproj, gw_out, pack = back(1, True, xl, target)
    dx, dgpre[1], chip_sums, own_sum = _bwd_in(1, "both_pair", token, dproj, saved[1][0], norm_pre, dx, win_full[1])
    top, token = start([chip_sums, gw_out.reshape(N_DEV, OUT_SHARD, D_MODEL), pack.reshape(N_DEV, PACK_SLICE, 128)],
                       [False, True, True], [own_sum, None, None], COLLECTIVE_EXCHANGE_1, "1")

    dproj, gw_out, pack = back(0, False, dx, token)
    early, token = start([gw_out.reshape(N_DEV, OUT_SHARD, D_MODEL), pack.reshape(N_DEV, PACK_SLICE, 128)],
                         [True, True], [None, None], COLLECTIVE_EXCHANGE_0A, "0a")
    chip_sums, own_sum = _bwd_in(0, "dw_pair", token, dproj, saved[0][0], norm_pre)
    late, token = start([chip_sums], [False], [own_sum], COLLECTIVE_EXCHANGE_0B, "0b")
    dx, dgpre[0] = _bwd_in(0, "dx", token, dproj, saved[0][0], norm_pre, dx, win_full[0])

    own1, lands1 = finish(top, dx, "1")
    own0a, lands0a = finish(early, dx, "0a")
    sems, block, land, token = _small_gather_start([own0a[1], own1[2]], [lands0a[1], lands1[2]], dgpre,
                                                   "gather_small_start")
    big_in, big_out = _adamw_layer(1, [(own1[0], lands1[0], w_in_t, m_in_t, v_in_t, None, ADAM_ROWS_IN),
                                       (own1[1], lands1[1], w_out, m_w_out, v_w_out, None, ADAM_ROWS_OUT)],
                                   token, "adamw1")
    own0b, lands0b = finish(late, big_out[0], "0b")
    big_in, big_out = _adamw_layer(0, [(own0b[0], lands0b[0], w_in_t, m_in_t, v_in_t, big_in, ADAM_ROWS_IN),
                                       (own0a[0], lands0a[0], w_out, m_w_out, v_w_out, big_out, ADAM_ROWS_OUT)],
                                   token, "adamw0")
    gathered = _gather_wait(sems, block, land, ALL_PEERS, big_in[0], "gather_small_wait")
    small_out = _adamw_small(gathered, params_small)
    loss = small_out[0].reshape(())

    outs = [loss, dx.reshape(1, SEQ, D_MODEL)]
    for t in range(4):
        pw_, sc_, sk_, npre_, npost_ = small_out[1 + 5 * t:6 + 5 * t]
        outs += [jnp.swapaxes(big_in[t], 1, 2), pw_, sc_, sk_, big_out[t], npre_, npost_]
    return tuple(outs)
```

```python
import functools
import numpy as np
import jax
import jax.numpy as jnp
from jax import lax
from jax.experimental import pallas as pl
from jax.experimental.pallas import tpu as pltpu

F32 = jnp.float32
BF16 = jnp.bfloat16

N_DEV = 8
SEQ = 2048
D_MODEL = 1024
D_POOL = 512
D_ATTN = 512
D_KV = 128
D_IN = 2304
N_HEADS = 8
GQA = 4
HEAD_DIM = 64
BLOCK = 128
POOL_WINDOWS = (2, 4, 8, 16)
DEPTH = 2
EPS = 1e-6
NEG_INF = -1e30
SCALE = HEAD_DIM ** -0.5
IN_SHARD = D_IN // N_DEV
OUT_SHARD = D_MODEL // N_DEV

COL_U, COL_PG, COL_Q, COL_K, COL_V, COL_AG = 0, 512, 1024, 1536, 1664, 1792

ADAM_LR = 0.001
ADAM_B1 = 0.9
ADAM_B2 = 0.999
ADAM_EPS = 1e-08
ADAM_WD = 0.01
ADAM_STEP = 10

TOKEN_TILE = 512
FWD_OUT_TILE = 1024
ADAM_ROWS_IN, ADAM_ROWS_OUT = 144, 64
VMEM_LIMIT = 56 * 1024 * 1024
MESH = pl.DeviceIdType.MESH

ROW_PW, ROW_SC, ROW_SINK, ROW_NPOST, ROW_LOSS = 0, 512, 520, 536, 544
PACK_ROWS = 576
PACK_SLICE = PACK_ROWS // N_DEV


def _nn(a, b):
    return jnp.dot(a, b, preferred_element_type=F32)


def _nt(a, b):
    return lax.dot_general(a, b, (((1,), (1,)), ((), ())), preferred_element_type=F32)


def _tn(a, b):
    return lax.dot_general(a, b, (((0,), (0,)), ((), ())), preferred_element_type=F32)


def _silu_parts(g):
    s = jax.nn.sigmoid(g)
    return g * s, s * (1.0 + g * (1.0 - s))


def _resident(shape):
    return pl.BlockSpec(shape, lambda *_: (0,) * len(shape), pipeline_mode=pl.Buffered(1))


def _compiler_params(sem=None):
    if sem is None:
        return pltpu.CompilerParams(vmem_limit_bytes=VMEM_LIMIT)
    return pltpu.CompilerParams(dimension_semantics=sem, vmem_limit_bytes=VMEM_LIMIT)


def _attn_bias():
    t = np.arange(BLOCK)[None, :]
    j = np.arange(BLOCK)[:, None]
    current = j <= t
    dist = np.where(current, t - j, t + BLOCK - j).astype(np.float32)
    out = np.zeros((2, 2, BLOCK, GQA * BLOCK), np.float32)
    for variant in range(2):
        valid = current | (variant == 1)
        for kv in range(2):
            for g in range(GQA):
                slope = np.float32(2.0 ** (-(kv * GQA + g + 1)))
                out[variant, kv, :, g * BLOCK:(g + 1) * BLOCK] = np.where(valid, -slope * dist, np.float32(NEG_INF))
    return out


def _replicate_head(kx, kv):
    rolled = pltpu.roll(kx, 64, 1)
    lane = lax.broadcasted_iota(jnp.int32, kx.shape, 1)
    dup = jnp.where(lane < 64, kx, rolled) if kv == 0 else jnp.where(lane < 64, rolled, kx)
    return jnp.concatenate([dup, dup], axis=1).astype(BF16)


def _stack_heads(qv):
    lane = lax.broadcasted_iota(jnp.int32, qv.shape, 1)
    zero = jnp.zeros_like(qv)
    return jnp.concatenate([jnp.where((lane >= 64 * g) & (lane < 64 * g + 64), qv, zero) for g in range(GQA)], axis=0)


def _unstack_heads(xs):
    lane = lax.broadcasted_iota(jnp.int32, (BLOCK, 256), 1)
    return jnp.where(lane < 64, xs[0:128], jnp.where(lane < 128, xs[128:256], jnp.where(lane < 192, xs[256:384], xs[384:512])))


def _fold_heads(r):
    h = r[:, 0:128] + r[:, 128:256]
    return h + pltpu.roll(h, 64, 1)


def _sink_row(sink_ref, layer, kv):
    lane = lax.broadcasted_iota(jnp.int32, (1, GQA * BLOCK), 1)
    s4 = [sink_ref[layer, kv * GQA + g] for g in range(GQA)]
    return jnp.where(lane < 128, s4[0], jnp.where(lane < 256, s4[1], jnp.where(lane < 384, s4[2], s4[3])))


def _band_is_current():
    j = lax.broadcasted_iota(jnp.int32, (BLOCK, GQA * BLOCK), 0)
    t = lax.broadcasted_iota(jnp.int32, (BLOCK, GQA * BLOCK), 1) & (BLOCK - 1)
    return j <= t


def _pack_band(full, current):
    return jnp.where(current, full[BLOCK:], full[:BLOCK])


def _unpack_band(packed, current):
    zero = jnp.zeros_like(packed)
    return jnp.concatenate([jnp.where(current, zero, packed), jnp.where(current, packed, zero)], axis=0)


def _probs_keys_major(k_rep, q_st, bias, sink, current):
    st = _pack_band(_nt(k_rep, q_st), current) * SCALE + bias
    m = jnp.maximum(jnp.max(st, axis=0, keepdims=True), sink)
    p = jnp.exp(st - m)
    esink = jnp.exp(sink - m)
    rl = 1.0 / (jnp.sum(p, axis=0, keepdims=True) + esink)
    return p * rl, esink * rl


WINDOW_HALO = 16


def _window_sum(ext, w, forward):
    s = ext
    sh = 1
    while sh < w:
        s = s + pltpu.roll(s, (ext.shape[0] - sh) if forward else sh, 0)
        sh *= 2
    return s


def _inv_count(n, w):
    t = n * BLOCK + lax.broadcasted_iota(jnp.int32, (BLOCK, 1), 0) + 1
    return 1.0 / jnp.minimum(t.astype(F32), float(w))


def _kv_ext(ref, n):
    r0 = pl.multiple_of(jnp.maximum(n - 1, 0) * BLOCK, BLOCK)
    r1 = pl.multiple_of(n * BLOCK, BLOCK)
    return jnp.concatenate([ref[pl.ds(r0, BLOCK), :], ref[pl.ds(r1, BLOCK), :]], axis=0)


def _rows_of(vec_ref, pack_ref, row0):
    for r in range(D_MODEL // 128):
        pack_ref[row0 + r:row0 + r + 1, :] = vec_ref[:, 128 * r:128 * (r + 1)]


FRONT_TILE = 4 * BLOCK


def _fwd_front(layer, x, norm_pre, w_in_t, token, sinks, pool_w, pool_scale, bias, out=None, gather=None):
    tm = FRONT_TILE

    def body(sink_ref, x_ref, g_ref, w_ref, _, pw_ref, sc_ref, bias_ref, *refs):
        if out is not None:
            gpost_ref, wo_ref, *refs = refs
        if gather is not None:
            refs = refs[2:]
            zones, (send_sems, recv_sems), refs = refs[10:12], refs[-2:], refs[:-2]
        if out is not None:
            xn_ref, y_ref = refs[8:10]
        u_ref, pg_ref, q_ref, k_ref, v_ref, ag_ref, z_ref, a_ref = refs[:8]
        uprev, kprev, vprev = refs[-3:]
        i = pl.program_id(0)

        @pl.when(i == 0)
        def _():
            uprev[...] = jnp.zeros_like(uprev)
            kprev[...] = jnp.zeros_like(kprev)
            vprev[...] = jnp.zeros_like(vprev)

        if gather is not None:
            px, py, pc = _mesh_pos()
            me, sibling = (px, py, pc), (px, py, 1 - pc)
            near = [(1 - px, py), (px, 1 - py)]
            far = (1 - px, 1 - py)
            relay_from, relay_to = (px ^ (1 - pc), py ^ pc), (px ^ pc, py ^ (1 - pc))
            k_from, k_to = 1 + pc, 2 - pc

            def copy(a, sem, block, to):
                rows_of_block = _device_rows(zones[a], gather[a].shape[0] // N_DEV, *block)
                return pltpu.make_async_remote_copy(
                    src_ref=rows_of_block, dst_ref=rows_of_block, send_sem=send_sems.at[a, sem],
                    recv_sem=recv_sems.at[a, sem], device_id=to, device_id_type=MESH)

            def pass_on(a):
                copy(a, k_from, (*relay_from, pc), me).wait_recv()
                copy(a, 3, (*relay_from, pc), (*relay_to, pc)).start()
                copy(a, 3 + k_from, (*relay_from, pc), sibling).start()
                copy(a, k_to, (*relay_to, pc), me).wait_recv()
                copy(a, 3 + k_to, (*relay_to, pc), sibling).start()

            @pl.when(i == 0)
            def _():
                _handshake([sibling] + [(*chip, pc) for chip in near])
                for a in range(2):
                    copy(a, 0, me, sibling).start()
                    for j, chip in enumerate(near):
                        copy(a, 1 + j, me, (*chip, pc)).start()

            for a in range(2):
                pl.when(i == 2 + a)(functools.partial(pass_on, a))

        xv = x_ref[...]
        r = lax.rsqrt(jnp.mean(xv * xv, axis=-1, keepdims=True) + EPS)
        h = (xv * r * g_ref[layer:layer + 1, :]).astype(BF16)
        u_ref[...] = _nt(h, w_ref[COL_U:COL_PG, :])
        pg_ref[...] = _nt(h, w_ref[COL_PG:COL_Q, :])
        for sb in range(tm // BLOCK):
            n = (tm // BLOCK) * i + sb
            rows = slice(BLOCK * sb, BLOCK * (sb + 1))
            before = slice(BLOCK * (sb - 1), BLOCK * sb)
            uv = u_ref[rows, :]
            halo = (uprev[BLOCK - WINDOW_HALO:, :] if sb == 0
                    else u_ref[BLOCK * sb - WINDOW_HALO:BLOCK * sb, :])
            ext = jnp.concatenate([halo, uv], axis=0)
            for g, w in enumerate(POOL_WINDOWS):
                cs = slice(BLOCK * g, BLOCK * (g + 1))
                win = _window_sum(ext[:, cs], w, forward=False)[WINDOW_HALO:]
                pooled = win * _inv_count(n, w) - uv[:, cs]
                mixed = _nn(pooled.astype(BF16), pw_ref[g].astype(BF16))
                gate, _ = _silu_parts(pg_ref[rows, cs])
                z_ref[rows, cs] = (mixed * sc_ref[layer:layer + 1, cs] * gate).astype(BF16)

        q_ref[...] = _nt(h, w_ref[COL_Q:COL_K, :]).astype(BF16)
        k_ref[...] = _nt(h, w_ref[COL_K:COL_V, :])
        v_ref[...] = _nt(h, w_ref[COL_V:COL_AG, :])
        ag_ref[...] = _nt(h, w_ref[COL_AG:D_IN, :])

        current = _band_is_current()
        for sb in range(tm // BLOCK):
            n = (tm // BLOCK) * i + sb
            rows = slice(BLOCK * sb, BLOCK * (sb + 1))
            before = slice(BLOCK * (sb - 1), BLOCK * sb)
            kx = jnp.concatenate([kprev[...] if sb == 0 else k_ref[before, :], k_ref[rows, :]], axis=0)
            vx = jnp.concatenate([vprev[...] if sb == 0 else v_ref[before, :], v_ref[rows, :]], axis=0)
            variant = jnp.minimum(n, 1) if sb == 0 else 1
            for kv in range(2):
                cs = slice(256 * kv, 256 * (kv + 1))
                p, _ = _probs_keys_major(_replicate_head(kx, kv), _stack_heads(q_ref[rows, cs]),
                                         bias_ref[variant, kv], _sink_row(sink_ref, layer, kv), current)
                o = _unstack_heads(_tn(_unpack_band(p.astype(BF16), current), _replicate_head(vx, kv)))
                a_ref[rows, cs] = o
                gate, _ = _silu_parts(ag_ref[rows, cs])
                z_ref[rows, D_POOL + 256 * kv:D_POOL + 256 * (kv + 1)] = (o * gate).astype(BF16)

        tail = slice(tm - BLOCK, tm)
        uprev[...] = u_ref[tail, :]
        kprev[...] = k_ref[tail, :]
        vprev[...] = v_ref[tail, :]

        if out is not None:
            y = _nn(z_ref[...], wo_ref[...])
            y_ref[...] = y
            r = lax.rsqrt(jnp.mean(y * y, axis=-1, keepdims=True) + EPS)
            xn_ref[...] = x_ref[...] + y * r * gpost_ref[layer:layer + 1, :]

        if gather is not None:
            @pl.when(i == SEQ // tm - 1)
            def _():
                for a in range(2):
                    copy(a, 3, (*far, pc), me).wait_recv()
                    copy(a, 6, (*far, pc), sibling).start()
                for a in range(2):
                    copy(a, 0, sibling, me).wait_recv()
                    for j, chip in enumerate(near + [far]):
                        copy(a, 4 + j, (*chip, 1 - pc), me).wait_recv()
                for a in range(2):
                    copy(a, 0, me, sibling).wait_send()
                    for j, chip in enumerate(near):
                        copy(a, 1 + j, me, (*chip, pc)).wait_send()
                    copy(a, 3, (*relay_from, pc), (*relay_to, pc)).wait_send()
                    copy(a, 3 + k_from, (*relay_from, pc), sibling).wait_send()
                    copy(a, 3 + k_to, (*relay_to, pc), sibling).wait_send()
                    copy(a, 6, (*far, pc), sibling).wait_send()

    row = lambda c: pl.BlockSpec((tm, c), lambda i: (i, 0))
    const = lambda shape: pl.BlockSpec(shape, lambda i: (0,) * len(shape))
    act = jax.ShapeDtypeStruct((SEQ, D_MODEL), F32)
    fused = out is not None
    hbm = pl.BlockSpec(memory_space=pl.ANY)
    zones, params = [], _compiler_params(("arbitrary",))
    if gather is not None:
        assert fused and SEQ // tm == 4, "the gather's copies are spread over four grid steps"
        zones = list(gather)
        params = pltpu.CompilerParams(dimension_semantics=("arbitrary",), vmem_limit_bytes=VMEM_LIMIT,
                                      collective_id=COLLECTIVE_GATHER_W1)
    return pl.pallas_call(
        body, name=f"fwd_front{layer}", grid=(SEQ // tm,),
        in_specs=[pl.BlockSpec(memory_space=pltpu.SMEM), row(D_MODEL), const((DEPTH, D_MODEL)),
                  _resident((D_IN, D_MODEL)), const((8, 128)),
                  pl.BlockSpec((None, 4, BLOCK, BLOCK), lambda i: (layer, 0, 0, 0)), const((DEPTH, D_POOL)),
                  _resident((2, 2, BLOCK, GQA * BLOCK))]
                 + ([const((DEPTH, D_MODEL)), _resident((D_MODEL, D_MODEL))] if fused else []) + [hbm] * len(zones),
        out_specs=[row(D_POOL), row(D_POOL), row(D_ATTN), row(D_KV), row(D_KV), row(D_ATTN), row(D_MODEL),
                   row(D_ATTN)] + ([row(D_MODEL)] * 2 if fused else []) + [hbm] * len(zones),
        out_shape=[jax.ShapeDtypeStruct((SEQ, D_POOL), F32), jax.ShapeDtypeStruct((SEQ, D_POOL), F32),
                   jax.ShapeDtypeStruct((SEQ, D_ATTN), BF16), jax.ShapeDtypeStruct((SEQ, D_KV), F32),
                   jax.ShapeDtypeStruct((SEQ, D_KV), F32), jax.ShapeDtypeStruct((SEQ, D_ATTN), F32),
                   jax.ShapeDtypeStruct((SEQ, D_MODEL), BF16), jax.ShapeDtypeStruct((SEQ, D_ATTN), F32)]
                  + ([act] * 2 if fused else []) + [jax.ShapeDtypeStruct(t.shape, t.dtype) for t in zones],
        scratch_shapes=[pltpu.VMEM((BLOCK, D_POOL), F32), pltpu.VMEM((BLOCK, D_KV), F32),
                        pltpu.VMEM((BLOCK, D_KV), F32)]
                       + ([pltpu.SemaphoreType.DMA((2, 7)), pltpu.SemaphoreType.DMA((2, 7))] if zones else []),
        input_output_aliases={10 + a: 10 + a for a in range(len(zones))},
        compiler_params=params,
    )(sinks, x, norm_pre, w_in_t, token, pool_w, pool_scale, bias, *(out if fused else ()), *zones)


def _fwd_out(layer, z, x, norm_post, w_out, token):
    tm = FWD_OUT_TILE

    def body(z_ref, x_ref, g_ref, w_ref, _, xn_ref, y_ref):
        y = _nn(z_ref[...], w_ref[...])
        y_ref[...] = y
        r = lax.rsqrt(jnp.mean(y * y, axis=-1, keepdims=True) + EPS)
        xn_ref[...] = x_ref[...] + y * r * g_ref[layer:layer + 1, :]

    row = lambda c: pl.BlockSpec((tm, c), lambda i: (i, 0))
    return pl.pallas_call(
        body, name=f"fwd_out{layer}", grid=(SEQ // tm,),
        in_specs=[row(D_MODEL), row(D_MODEL), pl.BlockSpec((DEPTH, D_MODEL), lambda i: (0, 0)),
                  _resident((D_MODEL, D_MODEL)), pl.BlockSpec((8, 128), lambda i: (0, 0))],
        out_specs=[row(D_MODEL), row(D_MODEL)],
        out_shape=[jax.ShapeDtypeStruct((SEQ, D_MODEL), F32), jax.ShapeDtypeStruct((SEQ, D_MODEL), F32)],
        compiler_params=_compiler_params(("arbitrary",)),
    )(z, x, norm_post, w_out, token)


BACK_TILE = 2 * BLOCK


def _bwd_back(layer, top, dxo_or_xf, target_or_token, y, z, norm_post, w_out, sinks, u, pg, q, k, v, ag, a,
              pool_w, pool_scale, bias):
    tm = BACK_TILE
    steps = SEQ // tm
    last = steps - 1
    per = tm // BLOCK

    def body(*refs):
        refs = list(refs)
        sink_ref, first, second = refs[:3]
        (y_ref, z_ref, g_ref, w_ref, u_ref, up_ref, pg_ref, q_ref, k_ref, v_ref, ag_ref, a_ref, pw_ref, sc_ref,
         bias_ref) = refs[3:18]
        del refs[:18]
        dxo_ref = refs.pop(0) if top else None
        dp_ref, dw_ref, pack_ref, acc, dg, lacc, dzs, ck, cv, ce = refs
        i = pl.program_id(0)
        blk = last - i

        @pl.when(i == 0)
        def _():
            acc[...] = jnp.zeros_like(acc)
            dg[...] = jnp.zeros_like(dg)
            lacc[...] = jnp.zeros_like(lacc)
            pack_ref[...] = jnp.zeros_like(pack_ref)
            ck[...] = jnp.zeros_like(ck)
            cv[...] = jnp.zeros_like(cv)
            ce[...] = jnp.zeros_like(ce)

        if top:
            d = first[...] - second[...]
            dxo_v = d * (1.0 / D_MODEL)
            dxo_ref[...] = dxo_v
            part = jnp.sum(d * d, axis=-1, keepdims=True) * (1.0 / D_MODEL)
            lacc[...] += 0.5 * jnp.sum(part, axis=0, keepdims=True)
        else:
            dxo_v = first[...]
        yv = y_ref[...]
        r = lax.rsqrt(jnp.mean(yv * yv, axis=-1, keepdims=True) + EPS)
        yn = yv * r
        dg[...] += jnp.sum(dxo_v * yn, axis=0, keepdims=True)
        dyn = dxo_v * g_ref[layer:layer + 1, :]
        dy = (r * (dyn - yn * jnp.mean(dyn * yn, axis=-1, keepdims=True))).astype(BF16)
        dzs[...] = _nt(dy, w_ref[...])
        acc[...] += _tn(z_ref[...], dy)

        lane = lax.broadcasted_iota(jnp.int32, (1, 128), 1)
        lane2 = lax.broadcasted_iota(jnp.int32, (256, 128), 1)
        current = _band_is_current()
        for sb in reversed(range(per)):
            n = per * blk + sb
            rows = slice(BLOCK * sb, BLOCK * (sb + 1))

            uv = u_ref[rows, :]
            if sb == 0:
                halo = up_ref[BLOCK - WINDOW_HALO:, :] * (n > 0).astype(F32)
            else:
                halo = u_ref[BLOCK * sb - WINDOW_HALO:BLOCK * sb, :]
            ext = jnp.concatenate([halo, uv], axis=0)
            for g, w in enumerate(POOL_WINDOWS):
                cs = slice(BLOCK * g, BLOCK * (g + 1))
                inv = _inv_count(n, w)
                win = _window_sum(ext[:, cs], w, forward=False)[WINDOW_HALO:]
                pooled = win * inv - uv[:, cs]
                pw_g = pw_ref[g].astype(BF16)
                mixed = _nn(pooled.astype(BF16), pw_g)
                gate, dgate = _silu_parts(pg_ref[rows, cs])
                dzp = dzs[rows, cs]
                sc = sc_ref[layer:layer + 1, cs]
                dpm = dzp * gate
                dp_ref[rows, COL_PG + BLOCK * g:COL_PG + BLOCK * (g + 1)] = (dzp * (mixed * sc) * dgate).astype(BF16)
                pack_ref[ROW_SC + g:ROW_SC + g + 1, :] += jnp.sum(dpm * mixed, axis=0, keepdims=True)
                dmixed = (dpm * sc).astype(BF16)
                pack_ref[ROW_PW + BLOCK * g:ROW_PW + BLOCK * (g + 1), :] += _tn(pooled.astype(BF16), dmixed)
                dpooled = _nt(dmixed, pw_g)
                e = dpooled * inv
                lead = _window_sum(jnp.concatenate([e, ce[:WINDOW_HALO, cs]], axis=0), w, forward=True)[:BLOCK]
                dp_ref[rows, COL_U + BLOCK * g:COL_U + BLOCK * (g + 1)] = (lead - dpooled).astype(BF16)
                ce[:, cs] = e

            kx = _kv_ext(k_ref, n)
            vx = _kv_ext(v_ref, n)
            variant = jnp.minimum(n, 1) if sb == 0 else 1
            dsink_row = jnp.zeros((1, 128), F32)
            tks, tvs = [], []
            for kv in range(2):
                cs = slice(256 * kv, 256 * (kv + 1))
                k_rep = _replicate_head(kx, kv)
                v_rep = _replicate_head(vx, kv)
                q_st = _stack_heads(q_ref[rows, cs])
                gate, dgate = _silu_parts(ag_ref[rows, cs])
                dza = dzs[rows, D_POOL + 256 * kv:D_POOL + 256 * (kv + 1)]
                dp_ref[rows, COL_AG + 256 * kv:COL_AG + 256 * (kv + 1)] = (dza * a_ref[rows, cs] * dgate).astype(BF16)
                da_st = _stack_heads((dza * gate).astype(BF16))
                p, psink = _probs_keys_major(k_rep, q_st, bias_ref[variant, kv], _sink_row(sink_ref, layer, kv),
                                             current)
                dpt = _pack_band(_nt(v_rep, da_st), current)
                delta = jnp.sum(p * dpt, axis=0, keepdims=True)
                dst = _unpack_band((p * (dpt - delta) * SCALE).astype(BF16), current)
                sink_terms = psink * delta
                for g in range(GQA):
                    dsink = -jnp.sum(sink_terms[:, BLOCK * g:BLOCK * (g + 1)], axis=1, keepdims=True)
                    dsink_row = dsink_row + jnp.where(lane == kv * GQA + g, dsink, 0.0)
                dp_ref[rows, COL_Q + 256 * kv:COL_Q + 256 * (kv + 1)] = _unstack_heads(_tn(dst, k_rep)).astype(BF16)
                tks.append(_fold_heads(_nn(dst, q_st)))
                tvs.append(_fold_heads(_nn(_unpack_band(p.astype(BF16), current), da_st)))
            pack_ref[ROW_SINK:ROW_SINK + 1, :] += dsink_row
            dkx = jnp.where(lane2 < 64, tks[0], tks[1])
            dvx = jnp.where(lane2 < 64, tvs[0], tvs[1])
            dp_ref[rows, COL_K:COL_V] = (ck[...] + dkx[BLOCK:]).astype(BF16)
            dp_ref[rows, COL_V:COL_AG] = (cv[...] + dvx[BLOCK:]).astype(BF16)
            ck[...] = dkx[:BLOCK]
            cv[...] = dvx[:BLOCK]

        @pl.when(i == steps - 1)
        def _():
            dw_ref[...] = acc[...].astype(BF16)
            _rows_of(dg, pack_ref, ROW_NPOST)
            pack_ref[ROW_LOSS:ROW_LOSS + 1, :] = jnp.where(lane == 0, lacc[...], 0.0)

    row = lambda c: pl.BlockSpec((tm, c), lambda i: (last - i, 0))
    const = lambda shape: pl.BlockSpec(shape, lambda i: (0,) * len(shape))
    act = jax.ShapeDtypeStruct((SEQ, D_MODEL), F32)
    return pl.pallas_call(
        body, name=f"bwd_back{layer}", grid=(steps,),
        in_specs=[pl.BlockSpec(memory_space=pltpu.SMEM), row(D_MODEL), row(D_MODEL) if top else const((8, 128)),
                  row(D_MODEL), row(D_MODEL), const((DEPTH, D_MODEL)), _resident((D_MODEL, D_MODEL)),
                  row(D_POOL), pl.BlockSpec((BLOCK, D_POOL), lambda i: (jnp.maximum(per * (last - i) - 1, 0), 0)),
                  row(D_POOL), row(D_ATTN), _resident((SEQ, D_KV)), _resident((SEQ, D_KV)), row(D_ATTN), row(D_ATTN),
                  pl.BlockSpec((None, 4, BLOCK, BLOCK), lambda i: (layer, 0, 0, 0)), const((DEPTH, D_POOL)),
                  _resident((2, 2, BLOCK, GQA * BLOCK))],
        out_specs=([row(D_MODEL)] * (1 if top else 0)
                   + [row(D_IN), const((D_MODEL, D_MODEL)), const((PACK_ROWS, 128))]),
        out_shape=([act] * (1 if top else 0)
                   + [jax.ShapeDtypeStruct((SEQ, D_IN), BF16), jax.ShapeDtypeStruct((D_MODEL, D_MODEL), BF16),
                      jax.ShapeDtypeStruct((PACK_ROWS, 128), F32)]),
        scratch_shapes=[pltpu.VMEM((D_MODEL, D_MODEL), F32), pltpu.VMEM((1, D_MODEL), F32), pltpu.VMEM((1, 1), F32),
                        pltpu.VMEM((tm, D_MODEL), F32), pltpu.VMEM((BLOCK, D_KV), F32), pltpu.VMEM((BLOCK, D_KV), F32),
                        pltpu.VMEM((BLOCK, D_POOL), F32)],
        compiler_params=_compiler_params(("arbitrary",)),
    )(sinks, dxo_or_xf, target_or_token, y, z, norm_post, w_out, u, u, pg, q, k, v, ag, a, pool_w, pool_scale, bias)


def _bwd_in(layer, part, token, dproj, x, norm_pre, dxo=None, w_in_t=None):
    pair = part in ("dw_pair", "both_pair")
    want_dw, want_dx = part != "dx", part in ("both", "dx", "both_pair")
    tm = TOKEN_TILE
    steps = SEQ // tm
    cw = 256

    def body(*refs):
        refs = list(refs)
        dp_ref, x_ref, g_ref = refs[1:4]
        del refs[:4]
        if want_dx:
            dxo_ref, w_ref, dx_ref, dgo_ref = refs[:4]
            del refs[:4]
            dg = refs.pop()
        if pair:
            hs_ref, hm_ref, acc, mine_buf, theirs_buf, send_sem, recv_sem = refs
        elif want_dw:
            dw_ref, acc = refs
        i = pl.program_id(0)

        @pl.when(i == 0)
        def _():
            if pair:
                _handshake([(lax.axis_index("x"), lax.axis_index("y"), 1 - lax.axis_index("c"))])
            if want_dw:
                acc[...] = jnp.zeros_like(acc)
            if want_dx:
                dg[...] = jnp.zeros_like(dg)

        xv = x_ref[...]
        gv = g_ref[layer:layer + 1, :]
        r = lax.rsqrt(jnp.mean(xv * xv, axis=-1, keepdims=True) + EPS)
        xn = xv * r
        if want_dw:
            hb = (xn * gv).astype(BF16)
            for c in range(0, D_IN, cw):
                acc[c:c + cw, :] += _tn(dp_ref[:, c:c + cw], hb)
        def rows_for(q, core):
            return pl.ds(pl.multiple_of((2 * q + core) * IN_SHARD, 8), IN_SHARD)

        def swap(q):
            x, y, c = _mesh_pos()
            return pltpu.make_async_remote_copy(
                src_ref=mine_buf.at[q], dst_ref=theirs_buf.at[q], send_sem=send_sem.at[q], recv_sem=recv_sem.at[q],
                device_id=(x, y, 1 - c), device_id_type=MESH)

        if pair:
            @pl.when(i == steps - 1)
            def _():
                for q in range(4):
                    mine_buf[q] = acc[rows_for(q, 1 - lax.axis_index("c")), :].astype(BF16)
                    swap(q).start()

        if want_dx:
            dh = _nn(dp_ref[...], w_ref[...])
            dg[...] += jnp.sum(dh * xn, axis=0, keepdims=True)
            dhn = dh * gv
            dx_ref[...] = dxo_ref[...] + r * (dhn - xn * jnp.mean(dhn * xn, axis=-1, keepdims=True))

        @pl.when(i == steps - 1)
        def _():
            if pair:
                x, y, c = _mesh_pos()
                for q in range(4):
                    swap(q).wait()
                for j, q in enumerate([2 * (1 - x) + y, 2 * x + (1 - y), 2 * (1 - x) + (1 - y)]):
                    hs_ref[j] = (acc[rows_for(q, c), :] + theirs_buf[q].astype(F32)).astype(BF16)
                hm_ref[...] = acc[rows_for(2 * x + y, c), :] + theirs_buf[2 * x + y].astype(F32)
            elif want_dw:
                dw_ref[...] = acc[...].astype(BF16)
            if want_dx:
                _rows_of(dg, dgo_ref, 0)

    row = lambda c: pl.BlockSpec((tm, c), lambda i: (i, 0))
    const = lambda shape: pl.BlockSpec(shape, lambda i: (0,) * len(shape))
    in_specs = [const((8, 128)), row(D_IN), row(D_MODEL), const((DEPTH, D_MODEL))]
    operands = [token, dproj, x, norm_pre]
    out_specs, out_shape, scratch = [], [], []
    if want_dx:
        in_specs += [row(D_MODEL), _resident((D_IN, D_MODEL))]
        operands += [dxo, w_in_t]
        out_specs += [row(D_MODEL), const((8, 128))]
        out_shape += [jax.ShapeDtypeStruct((SEQ, D_MODEL), F32), jax.ShapeDtypeStruct((8, 128), F32)]
    if pair:
        out_specs += [const((3, IN_SHARD, D_MODEL)), const((IN_SHARD, D_MODEL))]
        out_shape += [jax.ShapeDtypeStruct((3, IN_SHARD, D_MODEL), BF16), jax.ShapeDtypeStruct((IN_SHARD, D_MODEL), F32)]
        scratch += [pltpu.VMEM((D_IN, D_MODEL), F32), pltpu.VMEM((4, IN_SHARD, D_MODEL), BF16),
                    pltpu.VMEM((4, IN_SHARD, D_MODEL), BF16), pltpu.SemaphoreType.DMA((4,)), pltpu.SemaphoreType.DMA((4,))]
    elif want_dw:
        out_specs.append(const((D_IN, D_MODEL)))
        out_shape.append(jax.ShapeDtypeStruct((D_IN, D_MODEL), BF16))
        scratch.append(pltpu.VMEM((D_IN, D_MODEL), F32))
    if want_dx:
        scratch.append(pltpu.VMEM((1, D_MODEL), F32))
    params = pltpu.CompilerParams(dimension_semantics=("arbitrary",), vmem_limit_bytes=VMEM_LIMIT,
                                  collective_id=COLLECTIVE_PAIR_SUM[layer] if pair else None)
    return pl.pallas_call(
        body, name=f"bwd_in_{part}{layer}", grid=(steps,),
        in_specs=in_specs, out_specs=out_specs, out_shape=out_shape, scratch_shapes=scratch,
        compiler_params=params,
    )(*operands)


def _mesh_pos():
    return lax.axis_index("x"), lax.axis_index("y"), lax.axis_index("c")


def _device_rows(ref, m, px, py, pc):
    return ref.at[pl.ds(pl.multiple_of((4 * px + 2 * py + pc) * m, 16 if m % 16 == 0 else 8), m), :]


def _allgather(srcs, out_dtype, name, later=()):
    na, nb = len(srcs), len(later)
    every = list(srcs) + list(later)
    shapes = [(a.shape[-2], a.shape[-1]) for a, _ in every]

    def body(*refs):
        xs, refs = refs[:na + nb], refs[na + nb:]
        outs, cast, land, refs = refs[:na], refs[na:na + nb], refs[na + nb:na + 2 * nb], refs[na + 2 * nb:]
        stage, raw, (send_sems, recv_sems, local_sems, load_sems) = refs[:na], refs[na:2 * na + nb], refs[2 * na + nb:]
        loads = [pltpu.make_async_copy(xs[i].at[every[i][1]], raw[i], load_sems.at[i]) for i in range(na + nb)]
        for cp in loads:
            cp.start()
        x, y, c = _mesh_pos()
        me, sibling = (x, y, c), (x, y, 1 - c)
        near = [(1 - x, y), (x, 1 - y)]
        far = (1 - x, 1 - y)
        relay_from, relay_to = (x ^ (1 - c), y ^ c), (x ^ c, y ^ (1 - c))
        _handshake([sibling] + [(*chip, c) for chip in near])
        k_from, k_to = 1 + c, 2 - c

        def slot(a, px, py, pc):
            return _device_rows(outs[a], shapes[a][0], px, py, pc)

        def copy(a, k, block, to, src=None):
            return pltpu.make_async_remote_copy(
                src_ref=slot(a, *block) if src is None else src, dst_ref=slot(a, *block),
                send_sem=send_sems.at[a, k], recv_sem=recv_sems.at[a, k], device_id=to, device_id_type=MESH)

        def cast_block(i):
            loads[i].wait()
            return raw[i][...].astype(out_dtype)

        for a in range(na):
            stage[a][...] = cast_block(a)
        mine = [pltpu.make_async_copy(stage[a], slot(a, *me), local_sems.at[a]) for a in range(na)]
        for cp in mine:
            cp.start()
        sent = []
        for a in range(na):
            sent.append(copy(a, 0, me, sibling, src=stage[a]))
            sent += [copy(a, 1 + j, me, (*chip, c), src=stage[a]) for j, chip in enumerate(near)]
        for cp in sent:
            cp.start()
        for b in range(nb):
            cast[b][...] = cast_block(na + b)
            cp = pltpu.make_async_copy(cast[b], _device_rows(land[b], shapes[na + b][0], *me), local_sems.at[na + b])
            cp.start()
            mine.append(cp)
        for a in range(na):
            copy(a, k_from, (*relay_from, c), me).wait_recv()
            sent += [copy(a, 3, (*relay_from, c), (*relay_to, c)), copy(a, 3 + k_from, (*relay_from, c), sibling)]
            sent[-2].start()
            sent[-1].start()
        for a in range(na):
            copy(a, k_to, (*relay_to, c), me).wait_recv()
            sent.append(copy(a, 3 + k_to, (*relay_to, c), sibling))
            sent[-1].start()
        for a in range(na):
            copy(a, 3, (*far, c), me).wait_recv()
            sent.append(copy(a, 6, (*far, c), sibling))
            sent[-1].start()
        for a in range(na):
            copy(a, 0, sibling, me).wait_recv()
            for j, chip in enumerate(near + [far]):
                copy(a, 4 + j, (*chip, 1 - c), me).wait_recv()
        for cp in sent:
            cp.wait_send()
        for cp in mine:
            cp.wait()

    vmem = pl.BlockSpec(memory_space=pltpu.VMEM)
    hbm = pl.BlockSpec(memory_space=pl.ANY)
    gathered = [jax.ShapeDtypeStruct((N_DEV * m, n), out_dtype) for m, n in shapes]
    out = pl.pallas_call(
        body, name=name,
        in_specs=[hbm] * (na + nb),
        out_specs=[hbm] * na + [vmem] * nb + [hbm] * nb,
        out_shape=gathered[:na] + [jax.ShapeDtypeStruct(s, out_dtype) for s in shapes[na:]] + gathered[na:],
        scratch_shapes=([pltpu.VMEM(s, out_dtype) for s in shapes[:na]]
                        + [pltpu.VMEM(s, a.dtype) for s, (a, _) in zip(shapes, every)]
                        + [pltpu.SemaphoreType.DMA((na, 7)), pltpu.SemaphoreType.DMA((na, 7)),
                           pltpu.SemaphoreType.DMA((na + nb,)), pltpu.SemaphoreType.DMA((na + nb,))]),
        compiler_params=pltpu.CompilerParams(vmem_limit_bytes=VMEM_LIMIT, collective_id=COLLECTIVE_GATHER_W0),
    )(*[a for a, _ in every])
    return out[:na], out[na:na + nb], out[na + nb:]


ALL_PEERS = tuple(range(1, N_DEV))
SIBLING_AND_SAME_CORE = (1, 2, 4, 6)


def _related(k, x, y, c):
    return x ^ ((k >> 2) & 1), y ^ ((k >> 1) & 1), c ^ (k & 1)


def _gather_start(blocks, lands, relations, collective_id, name):
    na = len(blocks)

    def body(*refs):
        src, land, sems, token = refs[:na], refs[na:2 * na], refs[2 * na:4 * na], refs[-1]
        x, y, c = _mesh_pos()
        _handshake([_related(k, x, y, c) for k in sorted(set().union(*relations))])
        for a in range(na):
            for k in relations[a]:
                pltpu.make_async_remote_copy(
                    src_ref=src[a], dst_ref=_device_rows(land[a], blocks[a].shape[0], x, y, c),
                    send_sem=sems[2 * a].at[k - 1], recv_sem=sems[2 * a + 1].at[k - 1],
                    device_id=_related(k, x, y, c), device_id_type=MESH).start()
        token[...] = jnp.zeros_like(token)

    bufs = [pltpu.HBM(t.shape, t.dtype) for t in list(blocks) + list(lands)]
    out = pl.pallas_call(
        body, name=name,
        out_shape=(*([pltpu.SemaphoreType.DMA((N_DEV - 1,))] * (2 * na)), *bufs, jax.ShapeDtypeStruct((8, 128), F32)),
        in_specs=[_HBM] * (2 * na),
        out_specs=(*([_SEM] * (2 * na)), *([_HBM] * (2 * na)), pl.BlockSpec(memory_space=pltpu.VMEM)),
        input_output_aliases={i: 2 * na + i for i in range(2 * na)},
        compiler_params=pltpu.CompilerParams(has_side_effects=_EFFECT, collective_id=collective_id),
    )(*[pltpu.with_memory_space_constraint(t, pltpu.HBM) for t in list(blocks) + list(lands)])
    sems = [(out[2 * a], out[2 * a + 1]) for a in range(na)]
    return sems, out[2 * na:3 * na], out[3 * na:4 * na], out[-1]


def _gather_wait(sems, block, land, relations, after, name):
    def body(src, land_ref, send_sem, recv_sem, after_ref, src_out, land_out):
        x, y, c = _mesh_pos()
        for k in relations:
            peer = _related(k, x, y, c)
            cp = pltpu.make_async_remote_copy(
                src_ref=src, dst_ref=_device_rows(land_ref, block.shape[0], *peer),
                send_sem=send_sem.at[k - 1], recv_sem=recv_sem.at[k - 1], device_id=peer, device_id_type=MESH)
            cp.wait_send()
            cp.wait_recv()

    out = pl.pallas_call(
        body, name=name,
        out_shape=(pltpu.HBM(block.shape, block.dtype), pltpu.HBM(land.shape, land.dtype)),
        in_specs=[_HBM, _HBM, _SEM, _SEM, pl.BlockSpec(memory_space=pl.ANY)],
        out_specs=[_HBM, _HBM],
        input_output_aliases={0: 0, 1: 1},
        compiler_params=pltpu.CompilerParams(has_side_effects=_EFFECT),
    )(block, land, sems[0], sems[1], after)
    return out[1]


(COLLECTIVE_GATHER_W0, COLLECTIVE_GATHER_W1, COLLECTIVE_FORWARD_W_IN1, COLLECTIVE_EXCHANGE_1, COLLECTIVE_EXCHANGE_0A,
 COLLECTIVE_EXCHANGE_0B, COLLECTIVE_GATHER_SMALL) = range(1, 8)
COLLECTIVE_PAIR_SUM = (8, 9)
COLLECTIVE_FORWARD_W_OUT1 = 10


def _handshake(peers):
    barrier = pltpu.get_barrier_semaphore()
    for peer in peers:
        pl.semaphore_signal(barrier, inc=1, device_id=peer, device_id_type=MESH)
    pl.semaphore_wait(barrier, len(peers))


def _forward_plan(land_ref, m):
    x, y, c = _mesh_pos()
    return [_device_rows(land_ref, m, qx, qy, c) for qx, qy in ((1 - x, y), (x, 1 - y), (1 - x, 1 - y))], (x, y, 1 - c)


def _forward_start(land, m, collective_id, name):
    def body(land_ref, send_sem, recv_sem, land_out, token):
        _handshake([(lax.axis_index("x"), lax.axis_index("y"), 1 - lax.axis_index("c"))])
        rows, sibling = _forward_plan(land_ref, m)
        for j, r in enumerate(rows):
            pltpu.make_async_remote_copy(src_ref=r, dst_ref=r, send_sem=send_sem.at[j], recv_sem=recv_sem.at[j],
                                         device_id=sibling, device_id_type=MESH).start()
        token[...] = jnp.zeros_like(token)

    out = pl.pallas_call(
        body, name=name,
        out_shape=(pltpu.SemaphoreType.DMA((3,)), pltpu.SemaphoreType.DMA((3,)), pltpu.HBM(land.shape, land.dtype),
                   jax.ShapeDtypeStruct((8, 128), F32)),
        in_specs=[_HBM],
        out_specs=(_SEM, _SEM, _HBM, pl.BlockSpec(memory_space=pltpu.VMEM)),
        input_output_aliases={0: 2},
        compiler_params=pltpu.CompilerParams(has_side_effects=_EFFECT, collective_id=collective_id),
    )(pltpu.with_memory_space_constraint(land, pltpu.HBM))
    return (out[0], out[1]), out[2], out[3]


def _forward_wait(sems, land, m, after, name):
    def body(land_ref, send_sem, recv_sem, after_ref, land_out):
        x, y, c = _mesh_pos()
        mine, sibling = _forward_plan(land_ref, m)
        theirs = [_device_rows(land_ref, m, qx, qy, 1 - c) for qx, qy in ((1 - x, y), (x, 1 - y), (1 - x, 1 - y))]
        for j in range(3):
            cp = pltpu.make_async_remote_copy(src_ref=mine[j], dst_ref=theirs[j], send_sem=send_sem.at[j],
                                              recv_sem=recv_sem.at[j], device_id=sibling, device_id_type=MESH)
            cp.wait_send()
            cp.wait_recv()

    return pl.pallas_call(
        body, name=name,
        out_shape=pltpu.HBM(land.shape, land.dtype),
        in_specs=[_HBM, _SEM, _SEM, pl.BlockSpec(memory_space=pl.ANY)],
        out_specs=_HBM,
        input_output_aliases={0: 0},
        compiler_params=pltpu.CompilerParams(has_side_effects=_EFFECT),
    )(land, sems[0], sems[1], after)


_HBM = pl.BlockSpec(memory_space=pltpu.HBM)
_SEM = pl.BlockSpec(memory_space=pltpu.SEMAPHORE)
_EFFECT = pltpu.SideEffectType.DATAFLOW_SIDE_EFFECTING


def _exchange_plan(direct):
    x, y, c = _mesh_pos()
    if not direct:
        return [(j, j, (qx, qy, c)) for j, (qx, qy) in enumerate([(1 - x, y), (x, 1 - y), (1 - x, 1 - y)])]
    plan = []
    for k in range(1, N_DEV):
        px, py, pc = x ^ ((k >> 2) & 1), y ^ ((k >> 1) & 1), c ^ (k & 1)
        plan.append((4 * px + 2 * py + pc, k - 1, (px, py, pc)))
    return plan


def _exchange_copies(directs):
    copies, base = [], 0
    for a, direct in enumerate(directs):
        plan = _exchange_plan(direct)
        copies += [(a, block, slot, peer, base + slot) for block, slot, peer in plan]
        base += len(plan)
    return copies, base


def _exchange_start(srcs, directs, collective_id, name):
    na = len(srcs)
    slots = [N_DEV - 1 if direct else 3 for direct in directs]

    def body(*refs):
        src, land = refs[:na], refs[na:2 * na]
        send_sem, recv_sem = refs[2 * na], refs[2 * na + 1]
        token = refs[-1]
        _handshake([peer for _, _, peer in _exchange_plan(any(directs))])
        for a, block, slot, peer, sem in _exchange_copies(directs)[0]:
            pltpu.make_async_remote_copy(
                src_ref=src[a].at[block], dst_ref=land[a].at[slot], send_sem=send_sem.at[sem],
                recv_sem=recv_sem.at[sem], device_id=peer, device_id_type=MESH).start()
        token[...] = jnp.zeros_like(token)

    zones = [jax.ShapeDtypeStruct((n,) + t.shape[1:], t.dtype) for n, t in zip(slots, srcs)]
    bufs = [pltpu.HBM(t.shape, t.dtype) for t in list(srcs) + zones]
    out = pl.pallas_call(
        body, name=name,
        out_shape=(pltpu.SemaphoreType.DMA((sum(slots),)), pltpu.SemaphoreType.DMA((sum(slots),)), *bufs,
                   jax.ShapeDtypeStruct((8, 128), F32)),
        in_specs=[_HBM] * (2 * na),
        out_specs=(_SEM, _SEM, *([_HBM] * (2 * na)), pl.BlockSpec(memory_space=pltpu.VMEM)),
        input_output_aliases={i: 2 + i for i in range(2 * na)},
        compiler_params=pltpu.CompilerParams(has_side_effects=_EFFECT, collective_id=collective_id),
    )(*[pltpu.with_memory_space_constraint(t, pltpu.HBM) for t in srcs],
      *[pltpu.with_memory_space_constraint(lax.empty(t.shape, t.dtype), pltpu.HBM) for t in zones])
    return out[0], out[1], out[2:2 + na], out[2 + na:2 + 2 * na], out[-1]


def _exchange_wait(send_sem, recv_sem, srcs, lands, directs, after, name):
    na = len(srcs)

    def body(*refs):
        src, land = refs[:na], refs[na:2 * na]
        send_sem_ref, recv_sem_ref = refs[2 * na], refs[2 * na + 1]
        for a, block, slot, peer, sem in _exchange_copies(directs)[0]:
            cp = pltpu.make_async_remote_copy(
                src_ref=src[a].at[block], dst_ref=land[a].at[slot], send_sem=send_sem_ref.at[sem],
                recv_sem=recv_sem_ref.at[sem], device_id=peer, device_id_type=MESH)
            cp.wait_send()
            cp.wait_recv()

    bufs = [pltpu.HBM(t.shape, t.dtype) for t in list(srcs) + list(lands)]
    out = pl.pallas_call(
        body, name=name,
        out_shape=tuple(bufs),
        in_specs=[_HBM] * (2 * na) + [_SEM, _SEM, pl.BlockSpec(memory_space=pl.ANY)],
        out_specs=[_HBM] * (2 * na),
        input_output_aliases={i: i for i in range(2 * na)},
        compiler_params=pltpu.CompilerParams(has_side_effects=_EFFECT),
    )(*srcs, *lands, send_sem, recv_sem, after)
    return out[:na], out[na:]


def _own_then_slots(mine_ref, lands_ref, rows=slice(None)):
    if len(mine_ref.shape) == 3:
        x, y, c = _mesh_pos()
        total = mine_ref[4 * x + 2 * y + c, rows, :].astype(F32)
    else:
        total = mine_ref[rows, :].astype(F32)
    for j in range(lands_ref.shape[0]):
        total = total + lands_ref[j, rows, :].astype(F32)
    return total


SMALL_ROWS = 2 * PACK_SLICE + 2 * 8


def _small_gather_start(mine, lands, dgpre, name):
    def body(*refs):
        hm, ld, dg = refs[:DEPTH], refs[DEPTH:2 * DEPTH], refs[2 * DEPTH:3 * DEPTH]
        send_sem, recv_sem, blk, land, token, own, slots, rows, built, local_sems = refs[3 * DEPTH:]
        x, y, c = _mesh_pos()
        loads = []
        for l in range(DEPTH):
            loads += [pltpu.make_async_copy(hm[l].at[4 * x + 2 * y + c], own.at[l], local_sems.at[3 * l]),
                      pltpu.make_async_copy(ld[l], slots.at[l], local_sems.at[3 * l + 1]),
                      pltpu.make_async_copy(dg[l], rows.at[l], local_sems.at[3 * l + 2])]
        for cp in loads:
            cp.start()
        _handshake([_related(k, x, y, c) for k in ALL_PEERS])
        for cp in loads:
            cp.wait()
        for l in range(DEPTH):
            total = own[l]
            for j in range(N_DEV - 1):
                total = total + slots[l, j]
            built[PACK_SLICE * l:PACK_SLICE * (l + 1), :] = total
            built[2 * PACK_SLICE + 8 * l:2 * PACK_SLICE + 8 * (l + 1), :] = rows[l]
        stores = [pltpu.make_async_copy(built, blk, local_sems.at[3 * DEPTH]),
                  pltpu.make_async_copy(built, _device_rows(land, SMALL_ROWS, x, y, c), local_sems.at[3 * DEPTH + 1])]
        for cp in stores:
            cp.start()
        for cp in stores:
            cp.wait()
        for k in ALL_PEERS:
            pltpu.make_async_remote_copy(
                src_ref=blk, dst_ref=_device_rows(land, SMALL_ROWS, x, y, c), send_sem=send_sem.at[k - 1],
                recv_sem=recv_sem.at[k - 1], device_id=_related(k, x, y, c), device_id_type=MESH).start()
        token[...] = jnp.zeros_like(token)

    hbm = pl.BlockSpec(memory_space=pl.ANY)
    out = pl.pallas_call(
        body, name=name,
        in_specs=[hbm] * (3 * DEPTH),
        out_specs=(_SEM, _SEM, _HBM, _HBM, pl.BlockSpec(memory_space=pltpu.VMEM)),
        out_shape=(pltpu.SemaphoreType.DMA((N_DEV - 1,)), pltpu.SemaphoreType.DMA((N_DEV - 1,)),
                   pltpu.HBM((SMALL_ROWS, 128), F32), pltpu.HBM((N_DEV * SMALL_ROWS, 128), F32),
                   jax.ShapeDtypeStruct((8, 128), F32)),
        scratch_shapes=[pltpu.VMEM((DEPTH, PACK_SLICE, 128), F32), pltpu.VMEM((DEPTH, N_DEV - 1, PACK_SLICE, 128), F32),
                        pltpu.VMEM((DEPTH, 8, 128), F32), pltpu.VMEM((SMALL_ROWS, 128), F32),
                        pltpu.SemaphoreType.DMA((3 * DEPTH + 2,))],
        compiler_params=pltpu.CompilerParams(has_side_effects=_EFFECT, collective_id=COLLECTIVE_GATHER_SMALL),
    )(*mine, *lands, *dgpre)
    return (out[0], out[1]), out[2], out[3], out[4]


def _adamw_math(w, g, m, v):
    m = ADAM_B1 * m + (1.0 - ADAM_B1) * g
    v = ADAM_B2 * v + (1.0 - ADAM_B2) * (g * g)
    m_hat = m / (1.0 - ADAM_B1 ** ADAM_STEP)
    v_hat = v / (1.0 - ADAM_B2 ** ADAM_STEP)
    delta = -ADAM_LR * (m_hat / (jnp.sqrt(v_hat) + ADAM_EPS) + ADAM_WD * w)
    return delta, m, v


def _adamw_layer(layer, parts, token, name):
    steps = {w.shape[1] // rows for _, _, w, _, _, _, rows in parts}
    assert len(steps) == 1, steps
    n = len(parts)

    def body(_, *refs):
        for p in range(n):
            hm_ref, ld_ref, w_ref, m_ref, v_ref = refs[5 * p:5 * (p + 1)]
            g_ref, d_ref, nm_ref, nv_ref = refs[len(refs) - 4 * (n - p):len(refs) - 4 * (n - p - 1)]
            g = _own_then_slots(hm_ref, ld_ref)
            g_ref[...] = g
            d, nm, nv = _adamw_math(w_ref[...], g, m_ref[...], v_ref[...])
            d_ref[...] = d
            nm_ref[...] = nm
            nv_ref[...] = nv

    in_specs, out_specs, out_shape, operands, carried, aliases = [pl.BlockSpec(memory_space=pl.ANY)], [], [], [], [], {}
    for p, (mine, lands, w, m, v, earlier, rows) in enumerate(parts):
        nn = w.shape[2]
        spec = pl.BlockSpec((None, rows, nn), lambda i: (layer, i, 0))
        in_specs += [pl.BlockSpec((rows, nn), lambda i: (i, 0)) if mine.ndim == 2
                     else pl.BlockSpec((N_DEV, rows, nn), lambda i: (0, i, 0)),
                     pl.BlockSpec((lands.shape[0], rows, nn), lambda i: (0, i, 0)), spec, spec, spec]
        out_specs += [spec] * 4
        out_shape += [jax.ShapeDtypeStruct(w.shape, F32)] * 4
        operands += [mine, lands, w, m, v]
        if earlier is not None:
            aliases.update({1 + 5 * n + len(carried) + t: 4 * p + t for t in range(4)})
            carried += list(earlier)
    out = pl.pallas_call(
        body, name=name, grid=(steps.pop(),),
        in_specs=in_specs + [pl.BlockSpec(memory_space=pl.ANY)] * len(carried),
        out_specs=out_specs, out_shape=out_shape, input_output_aliases=aliases,
        compiler_params=_compiler_params(("arbitrary",)),
    )(token, *operands, *carried)
    return [out[4 * p:4 * (p + 1)] for p in range(n)]


def _adamw_small(gathered, params):
    def body(all_ref, *refs):
        ins, outs, packs = refs[:15], refs[15:15 + 21], refs[15 + 21]
        loss_ref = outs[0]
        for dev in range(N_DEV):
            for l in range(DEPTH):
                packs[l, PACK_SLICE * dev:PACK_SLICE * (dev + 1), :] = (
                    all_ref[SMALL_ROWS * dev + PACK_SLICE * l:SMALL_ROWS * dev + PACK_SLICE * (l + 1), :])
        loss_ref[...] = packs[DEPTH - 1, ROW_LOSS:ROW_LOSS + 1, 0:1]

        def update(p, sel, g):
            w_ref, m_ref, v_ref = ins[p], ins[5 + p], ins[10 + p]
            d, nm, nv = _adamw_math(w_ref[sel], g, m_ref[sel], v_ref[sel])
            for t, val in enumerate((g, d, nm, nv)):
                outs[1 + 5 * t + p][sel] = val

        for l in range(DEPTH):
            gp = packs.at[l]
            row0 = 2 * PACK_SLICE + 8 * l
            dgpre = all_ref[row0:row0 + 8, :]
            for dev in range(1, N_DEV):
                dgpre = dgpre + all_ref[SMALL_ROWS * dev + row0:SMALL_ROWS * dev + row0 + 8, :]
            for grp in range(4):
                update(0, (l, grp), gp[ROW_PW + BLOCK * grp:ROW_PW + BLOCK * (grp + 1), :])
                update(1, (slice(l, l + 1), slice(128 * grp, 128 * (grp + 1))), gp[ROW_SC + grp:ROW_SC + grp + 1, :])
            update(2, (slice(l, l + 1), slice(None)), gp[ROW_SINK:ROW_SINK + 1, 0:N_HEADS])
            for r in range(D_MODEL // 128):
                sel = (slice(l, l + 1), slice(128 * r, 128 * (r + 1)))
                update(3, sel, dgpre[r:r + 1, :])
                update(4, sel, gp[ROW_NPOST + r:ROW_NPOST + r + 1, :])

    shapes = [jax.ShapeDtypeStruct(p.shape, F32) for p in params[:5]]
    return pl.pallas_call(
        body, name="adamw_small",
        out_shape=[jax.ShapeDtypeStruct((1, 1), F32)] + shapes * 4,
        scratch_shapes=[pltpu.VMEM((DEPTH, PACK_ROWS, 128), F32)],
        compiler_params=_compiler_params(),
    )(gathered, *params)


def kernel(x, w_in, pool_w, pool_scale, attn_sinks, w_out, norm_pre, norm_post, loss_target, m_w_in, m_pool_w, m_pool_scale, m_attn_sinks, m_w_out, m_norm_pre, m_norm_post, v_w_in, v_pool_w, v_pool_scale, v_attn_sinks, v_w_out, v_norm_pre, v_norm_post):
    x0 = x.reshape(SEQ, D_MODEL)
    target = loss_target.reshape(SEQ, D_MODEL)
    bias = jnp.asarray(_attn_bias())
    w_in_t, m_in_t, v_in_t = (jnp.swapaxes(t, 1, 2) for t in (w_in, m_w_in, v_w_in))

    (win0, wout0), later, lands = _allgather([(w_in_t, 0), (w_out, 0)], BF16, "gather_w0",
                                              later=[(w_in_t, 1), (w_out, 1)])
    win_full, wout_full = [win0, None], [wout0, None]
    token = jnp.zeros((8, 128), F32)

    saved = []
    xl = x0
    for layer in range(DEPTH):
        front = (layer, xl, norm_pre, win_full[layer], token, attn_sinks, pool_w, pool_scale, bias)
        if layer == 0:
            u, pg, q, k, v, ag, z, a, x_next, y, win_full[1], wout_full[1] = _fwd_front(
                *front, out=(norm_post, wout_full[layer]), gather=lands)
        else:
            u, pg, q, k, v, ag, z, a, x_next, y = _fwd_front(*front, out=(norm_post, wout_full[layer]))
        saved.append((xl, u, pg, q, k, v, ag, z, a, y))
        xl = x_next

    params_small = [pool_w, pool_scale, attn_sinks, norm_pre, norm_post,
                    m_pool_w, m_pool_scale, m_attn_sinks, m_norm_pre, m_norm_post,
                    v_pool_w, v_pool_scale, v_attn_sinks, v_norm_pre, v_norm_post]

    def start(srcs, directs, paired, collective_id, tag):
        send_sem, recv_sem, srcs, lands, started = _exchange_start(srcs, directs, collective_id, f"exchange_start{tag}")
        return (send_sem, recv_sem, srcs, lands, paired, directs), started

    def finish(handle, after, tag):
        send_sem, recv_sem, srcs, lands, paired, directs = handle
        srcs, lands = _exchange_wait(send_sem, recv_sem, srcs, lands, directs, after, f"exchange_wait{tag}")
        return [s if p is None else p for s, p in zip(srcs, paired)], lands

    def back(layer, top, first, second):
        xin, u, pg, q, k, v, ag, z, a, y = saved[layer]
        return _bwd_back(layer, top, first, second, y, z, norm_post, wout_full[layer], attn_sinks, u, pg, q, k, v,
                         ag, a, pool_w, pool_scale, bias)

    dgpre = [None] * DEPTH
    dx, dproj, gw_out, pack = back(1, True, xl, target)
    dx, dgpre[1], chip_sums, own_sum = _bwd_in(1, "both_pair", token, dproj, saved[1][0], norm_pre, dx, win_full[1])
    top, token = start([chip_sums, gw_out.reshape(N_DEV, OUT_SHARD, D_MODEL), pack.reshape(N_DEV, PACK_SLICE, 128)],
                       [False, True, True], [own_sum, None, None], COLLECTIVE_EXCHANGE_1, "1")

    dproj, gw_out, pack = back(0, False, dx, token)
    early, token = start([gw_out.reshape(N_DEV, OUT_SHARD, D_MODEL), pack.reshape(N_DEV, PACK_SLICE, 128)],
                         [True, True], [None, None], COLLECTIVE_EXCHANGE_0A, "0a")
    chip_sums, own_sum = _bwd_in(0, "dw_pair", token, dproj, saved[0][0], norm_pre)
    late, token = start([chip_sums], [False], [own_sum], COLLECTIVE_EXCHANGE_0B, "0b")
    dx, dgpre[0] = _bwd_in(0, "dx", token, dproj, saved[0][0], norm_pre, dx, win_full[0])

    own1, lands1 = finish(top, dx, "1")
    own0a, lands0a = finish(early, dx, "0a")
    sems, block, land, token = _small_gather_start([own0a[1], own1[2]], [lands0a[1], lands1[2]], dgpre,
                                                   "gather_small_start")
    big_in, big_out = _adamw_layer(1, [(own1[0], lands1[0], w_in_t, m_in_t, v_in_t, None, ADAM_ROWS_IN),
                                       (own1[1], lands1[1], w_out, m_w_out, v_w_out, None, ADAM_ROWS_OUT)],
                                   token, "adamw1")
    own0b, lands0b = finish(late, big_out[0], "0b")
    big_in, big_out = _adamw_layer(0, [(own0b[0], lands0b[0], w_in_t, m_in_t, v_in_t, big_in, ADAM_ROWS_IN),
                                       (own0a[0], lands0a[0], w_out, m_w_out, v_w_out, big_out, ADAM_ROWS_OUT)],
                                   token, "adamw0")
    gathered = _gather_wait(sems, block, land, ALL_PEERS, big_in[0], "gather_small_wait")
    small_out = _adamw_small(gathered, params_small)
    loss = small_out[0].reshape(())

    outs = [loss, dx.reshape(1, SEQ, D_MODEL)]
    for t in range(4):
        pw_, sc_, sk_, npre_, npost_ = small_out[1 + 5 * t:6 + 5 * t]
        outs += [jnp.swapaxes(big_in[t], 1, 2), pw_, sc_, sk_, big_out[t], npre_, npost_]
    return tuple(outs)
```

```python
import functools
import numpy as np
import jax
import jax.numpy as jnp
from jax import lax
from jax.experimental import pallas as pl
from jax.experimental.pallas import tpu as pltpu

F32 = jnp.float32
BF16 = jnp.bfloat16

N_DEV = 8
SEQ = 2048
D_MODEL = 1024
D_POOL = 512
D_ATTN = 512
D_KV = 128
D_IN = 2304
N_HEADS = 8
GQA = 4
HEAD_DIM = 64
BLOCK = 128
POOL_WINDOWS = (2, 4, 8, 16)
DEPTH = 2
EPS = 1e-6
NEG_INF = -1e30
SCALE = HEAD_DIM ** -0.5
IN_SHARD = D_IN // N_DEV
OUT_SHARD = D_MODEL // N_DEV

COL_U, COL_PG, COL_Q, COL_K, COL_V, COL_AG = 0, 512, 1024, 1536, 1664, 1792

ADAM_LR = 0.001
ADAM_B1 = 0.9
ADAM_B2 = 0.999
ADAM_EPS = 1e-08
ADAM_WD = 0.01
ADAM_STEP = 10

TOKEN_TILE = 512
ADAM_ROWS_IN, ADAM_ROWS_OUT = 144, 64
VMEM_LIMIT = 56 * 1024 * 1024
MESH = pl.DeviceIdType.MESH

ROW_PW, ROW_SC, ROW_SINK, ROW_NPOST, ROW_LOSS = 0, 512, 520, 536, 544
PACK_ROWS = 576
PACK_SLICE = PACK_ROWS // N_DEV


def _nn(a, b):
    return jnp.dot(a, b, preferred_element_type=F32)


def _nt(a, b):
    return lax.dot_general(a, b, (((1,), (1,)), ((), ())), preferred_element_type=F32)


def _tn(a, b):
    return lax.dot_general(a, b, (((0,), (0,)), ((), ())), preferred_element_type=F32)


def _silu_parts(g):
    s = jax.nn.sigmoid(g)
    return g * s, s * (1.0 + g * (1.0 - s))


def _resident(shape):
    return pl.BlockSpec(shape, lambda *_: (0,) * len(shape), pipeline_mode=pl.Buffered(1))


def _compiler_params(sem=None):
    if sem is None:
        return pltpu.CompilerParams(vmem_limit_bytes=VMEM_LIMIT)
    return pltpu.CompilerParams(dimension_semantics=sem, vmem_limit_bytes=VMEM_LIMIT)


def _attn_bias():
    t = np.arange(BLOCK)[None, :]
    j = np.arange(BLOCK)[:, None]
    current = j <= t
    dist = np.where(current, t - j, t + BLOCK - j).astype(np.float32)
    out = np.zeros((2, 2, BLOCK, GQA * BLOCK), np.float32)
    for variant in range(2):
        valid = current | (variant == 1)
        for kv in range(2):
            for g in range(GQA):
                slope = np.float32(2.0 ** (-(kv * GQA + g + 1)))
                out[variant, kv, :, g * BLOCK:(g + 1) * BLOCK] = np.where(valid, -slope * dist, np.float32(NEG_INF))
    return out


def _replicate_head(kx, kv):
    rolled = pltpu.roll(kx, 64, 1)
    lane = lax.broadcasted_iota(jnp.int32, kx.shape, 1)
    dup = jnp.where(lane < 64, kx, rolled) if kv == 0 else jnp.where(lane < 64, rolled, kx)
    return jnp.concatenate([dup, dup], axis=1).astype(BF16)


def _stack_heads(qv):
    lane = lax.broadcasted_iota(jnp.int32, qv.shape, 1)
    zero = jnp.zeros_like(qv)
    return jnp.concatenate([jnp.where((lane >= 64 * g) & (lane < 64 * g + 64), qv, zero) for g in range(GQA)], axis=0)


def _unstack_heads(xs):
    lane = lax.broadcasted_iota(jnp.int32, (BLOCK, 256), 1)
    return jnp.where(lane < 64, xs[0:128], jnp.where(lane < 128, xs[128:256], jnp.where(lane < 192, xs[256:384], xs[384:512])))


def _fold_heads(r):
    h = r[:, 0:128] + r[:, 128:256]
    return h + pltpu.roll(h, 64, 1)


def _sink_row(sink_ref, layer, kv):
    lane = lax.broadcasted_iota(jnp.int32, (1, GQA * BLOCK), 1)
    s4 = [sink_ref[layer, kv * GQA + g] for g in range(GQA)]
    return jnp.where(lane < 128, s4[0], jnp.where(lane < 256, s4[1], jnp.where(lane < 384, s4[2], s4[3])))


def _band_is_current():
    j = lax.broadcasted_iota(jnp.int32, (BLOCK, GQA * BLOCK), 0)
    t = lax.broadcasted_iota(jnp.int32, (BLOCK, GQA * BLOCK), 1) & (BLOCK - 1)
    return j <= t


def _pack_band(full, current):
    return jnp.where(current, full[BLOCK:], full[:BLOCK])


def _unpack_band(packed, current):
    zero = jnp.zeros_like(packed)
    return jnp.concatenate([jnp.where(current, zero, packed), jnp.where(current, packed, zero)], axis=0)


def _probs_keys_major(k_rep, q_st, bias, sink, current):
    st = _pack_band(_nt(k_rep, q_st), current) * SCALE + bias
    m = jnp.maximum(jnp.max(st, axis=0, keepdims=True), sink)
    p = jnp.exp(st - m)
    esink = jnp.exp(sink - m)
    rl = 1.0 / (jnp.sum(p, axis=0, keepdims=True) + esink)
    return p * rl, esink * rl


WINDOW_HALO = 16


def _window_sum(ext, w, forward):
    s = ext
    sh = 1
    while sh < w:
        s = s + pltpu.roll(s, (ext.shape[0] - sh) if forward else sh, 0)
        sh *= 2
    return s


def _inv_count(n, w):
    t = n * BLOCK + lax.broadcasted_iota(jnp.int32, (BLOCK, 1), 0) + 1
    return 1.0 / jnp.minimum(t.astype(F32), float(w))


def _kv_ext(ref, n):
    r0 = pl.multiple_of(jnp.maximum(n - 1, 0) * BLOCK, BLOCK)
    r1 = pl.multiple_of(n * BLOCK, BLOCK)
    return jnp.concatenate([ref[pl.ds(r0, BLOCK), :], ref[pl.ds(r1, BLOCK), :]], axis=0)


def _rows_of(vec_ref, pack_ref, row0):
    for r in range(D_MODEL // 128):
        pack_ref[row0 + r:row0 + r + 1, :] = vec_ref[:, 128 * r:128 * (r + 1)]


FRONT_TILE = 4 * BLOCK


def _fwd_front(layer, x, norm_pre, w_in_t, token, sinks, pool_w, pool_scale, bias, out=None, gather=None):
    tm = FRONT_TILE

    def body(sink_ref, x_ref, g_ref, w_ref, _, pw_ref, sc_ref, bias_ref, *refs):
        if out is not None:
            gpost_ref, wo_ref, *refs = refs
        if gather is not None:
            refs = refs[2:]
            zones, (send_sems, recv_sems), refs = refs[10:12], refs[-2:], refs[:-2]
        if out is not None:
            xn_ref, y_ref = refs[8:10]
        u_ref, pg_ref, q_ref, k_ref, v_ref, ag_ref, z_ref, a_ref = refs[:8]
        uprev, kprev, vprev = refs[-3:]
        i = pl.program_id(0)

        @pl.when(i == 0)
        def _():
            uprev[...] = jnp.zeros_like(uprev)
            kprev[...] = jnp.zeros_like(kprev)
            vprev[...] = jnp.zeros_like(vprev)

        if gather is not None:
            px, py, pc = _mesh_pos()
            me, sibling = (px, py, pc), (px, py, 1 - pc)
            near = [(1 - px, py), (px, 1 - py)]
            far = (1 - px, 1 - py)
            relay_from, relay_to = (px ^ (1 - pc), py ^ pc), (px ^ pc, py ^ (1 - pc))
            k_from, k_to = 1 + pc, 2 - pc

            def copy(a, sem, block, to):
                rows_of_block = _device_rows(zones[a], gather[a].shape[0] // N_DEV, *block)
                return pltpu.make_async_remote_copy(
                    src_ref=rows_of_block, dst_ref=rows_of_block, send_sem=send_sems.at[a, sem],
                    recv_sem=recv_sems.at[a, sem], device_id=to, device_id_type=MESH)

            def pass_on(a):
                copy(a, k_from, (*relay_from, pc), me).wait_recv()
                copy(a, 3, (*relay_from, pc), (*relay_to, pc)).start()
                copy(a, 3 + k_from, (*relay_from, pc), sibling).start()
                copy(a, k_to, (*relay_to, pc), me).wait_recv()
                copy(a, 3 + k_to, (*relay_to, pc), sibling).start()

            @pl.when(i == 0)
            def _():
                _handshake([sibling] + [(*chip, pc) for chip in near])
                for a in range(2):
                    copy(a, 0, me, sibling).start()
                    for j, chip in enumerate(near):
                        copy(a, 1 + j, me, (*chip, pc)).start()

            for a in range(2):
                pl.when(i == 1 + a)(functools.partial(pass_on, a))

        xv = x_ref[...]
        r = lax.rsqrt(jnp.mean(xv * xv, axis=-1, keepdims=True) + EPS)
        h = (xv * r * g_ref[layer:layer + 1, :]).astype(BF16)
        u_ref[...] = _nt(h, w_ref[COL_U:COL_PG, :])
        pg_ref[...] = _nt(h, w_ref[COL_PG:COL_Q, :])
        for sb in range(tm // BLOCK):
            n = (tm // BLOCK) * i + sb
            rows = slice(BLOCK * sb, BLOCK * (sb + 1))
            before = slice(BLOCK * (sb - 1), BLOCK * sb)
            uv = u_ref[rows, :]
            halo = (uprev[BLOCK - WINDOW_HALO:, :] if sb == 0
                    else u_ref[BLOCK * sb - WINDOW_HALO:BLOCK * sb, :])
            ext = jnp.concatenate([halo, uv], axis=0)
            for g, w in enumerate(POOL_WINDOWS):
                cs = slice(BLOCK * g, BLOCK * (g + 1))
                win = _window_sum(ext[:, cs], w, forward=False)[WINDOW_HALO:]
                pooled = win * _inv_count(n, w) - uv[:, cs]
                mixed = _nn(pooled.astype(BF16), pw_ref[g].astype(BF16))
                gate, _ = _silu_parts(pg_ref[rows, cs])
                z_ref[rows, cs] = (mixed * sc_ref[layer:layer + 1, cs] * gate).astype(BF16)

        q_ref[...] = _nt(h, w_ref[COL_Q:COL_K, :]).astype(BF16)
        k_ref[...] = _nt(h, w_ref[COL_K:COL_V, :])
        v_ref[...] = _nt(h, w_ref[COL_V:COL_AG, :])
        ag_ref[...] = _nt(h, w_ref[COL_AG:D_IN, :])

        current = _band_is_current()
        for sb in range(tm // BLOCK):
            n = (tm // BLOCK) * i + sb
            rows = slice(BLOCK * sb, BLOCK * (sb + 1))
            before = slice(BLOCK * (sb - 1), BLOCK * sb)
            kx = jnp.concatenate([kprev[...] if sb == 0 else k_ref[before, :], k_ref[rows, :]], axis=0)
            vx = jnp.concatenate([vprev[...] if sb == 0 else v_ref[before, :], v_ref[rows, :]], axis=0)
            variant = jnp.minimum(n, 1) if sb == 0 else 1
            for kv in range(2):
                cs = slice(256 * kv, 256 * (kv + 1))
                p, _ = _probs_keys_major(_replicate_head(kx, kv), _stack_heads(q_ref[rows, cs]),
                                         bias_ref[variant, kv], _sink_row(sink_ref, layer, kv), current)
                o = _unstack_heads(_tn(_unpack_band(p.astype(BF16), current), _replicate_head(vx, kv)))
                a_ref[rows, cs] = o
                gate, _ = _silu_parts(ag_ref[rows, cs])
                z_ref[rows, D_POOL + 256 * kv:D_POOL + 256 * (kv + 1)] = (o * gate).astype(BF16)

        tail = slice(tm - BLOCK, tm)
        uprev[...] = u_ref[tail, :]
        kprev[...] = k_ref[tail, :]
        vprev[...] = v_ref[tail, :]

        if out is not None:
            y = _nn(z_ref[...], wo_ref[...])
            y_ref[...] = y
            r = lax.rsqrt(jnp.mean(y * y, axis=-1, keepdims=True) + EPS)
            xn_ref[...] = x_ref[...] + y * r * gpost_ref[layer:layer + 1, :]

        if gather is not None:
            @pl.when(i == SEQ // tm - 1)
            def _():
                for a in range(2):
                    copy(a, 3, (*far, pc), me).wait_recv()
                    copy(a, 6, (*far, pc), sibling).start()
                for a in range(2):
                    copy(a, 0, sibling, me).wait_recv()
                    for j, chip in enumerate(near + [far]):
                        copy(a, 4 + j, (*chip, 1 - pc), me).wait_recv()
                for a in range(2):
                    copy(a, 0, me, sibling).wait_send()
                    for j, chip in enumerate(near):
                        copy(a, 1 + j, me, (*chip, pc)).wait_send()
                    copy(a, 3, (*relay_from, pc), (*relay_to, pc)).wait_send()
                    copy(a, 3 + k_from, (*relay_from, pc), sibling).wait_send()
                    copy(a, 3 + k_to, (*relay_to, pc), sibling).wait_send()
                    copy(a, 6, (*far, pc), sibling).wait_send()

    row = lambda c: pl.BlockSpec((tm, c), lambda i: (i, 0))
    const = lambda shape: pl.BlockSpec(shape, lambda i: (0,) * len(shape))
    act = jax.ShapeDtypeStruct((SEQ, D_MODEL), F32)
    fused = out is not None
    hbm = pl.BlockSpec(memory_space=pl.ANY)
    zones, params = [], _compiler_params(("arbitrary",))
    if gather is not None:
        assert fused and SEQ // tm == 4, "the gather's copies are spread over four grid steps"
        zones = list(gather)
        params = pltpu.CompilerParams(dimension_semantics=("arbitrary",), vmem_limit_bytes=VMEM_LIMIT,
                                      collective_id=COLLECTIVE_GATHER_W1)
    return pl.pallas_call(
        body, name=f"fwd_front{layer}", grid=(SEQ // tm,),
        in_specs=[pl.BlockSpec(memory_space=pltpu.SMEM), row(D_MODEL), const((DEPTH, D_MODEL)),
                  _resident((D_IN, D_MODEL)), const((8, 128)),
                  pl.BlockSpec((None, 4, BLOCK, BLOCK), lambda i: (layer, 0, 0, 0)), const((DEPTH, D_POOL)),
                  _resident((2, 2, BLOCK, GQA * BLOCK))]
                 + ([const((DEPTH, D_MODEL)), _resident((D_MODEL, D_MODEL))] if fused else []) + [hbm] * len(zones),
        out_specs=[row(D_POOL), row(D_POOL), row(D_ATTN), row(D_KV), row(D_KV), row(D_ATTN), row(D_MODEL),
                   row(D_ATTN)] + ([row(D_MODEL)] * 2 if fused else []) + [hbm] * len(zones),
        out_shape=[jax.ShapeDtypeStruct((SEQ, D_POOL), F32), jax.ShapeDtypeStruct((SEQ, D_POOL), F32),
                   jax.ShapeDtypeStruct((SEQ, D_ATTN), BF16), jax.ShapeDtypeStruct((SEQ, D_KV), F32),
                   jax.ShapeDtypeStruct((SEQ, D_KV), F32), jax.ShapeDtypeStruct((SEQ, D_ATTN), F32),
                   jax.ShapeDtypeStruct((SEQ, D_MODEL), BF16), jax.ShapeDtypeStruct((SEQ, D_ATTN), F32)]
                  + ([act] * 2 if fused else []) + [jax.ShapeDtypeStruct(t.shape, t.dtype) for t in zones],
        scratch_shapes=[pltpu.VMEM((BLOCK, D_POOL), F32), pltpu.VMEM((BLOCK, D_KV), F32),
                        pltpu.VMEM((BLOCK, D_KV), F32)]
                       + ([pltpu.SemaphoreType.DMA((2, 7)), pltpu.SemaphoreType.DMA((2, 7))] if zones else []),
        input_output_aliases={10 + a: 10 + a for a in range(len(zones))},
        compiler_params=params,
    )(sinks, x, norm_pre, w_in_t, token, pool_w, pool_scale, bias, *(out if fused else ()), *zones)


BACK_TILE = 2 * BLOCK


def _bwd_back(layer, top, dxo_or_xf, target_or_token, y, z, norm_post, w_out, sinks, u, pg, q, k, v, ag, a,
              pool_w, pool_scale, bias):
    tm = BACK_TILE
    steps = SEQ // tm
    last = steps - 1
    per = tm // BLOCK

    def body(*refs):
        refs = list(refs)
        sink_ref, first, second = refs[:3]
        (y_ref, z_ref, g_ref, w_ref, u_ref, up_ref, pg_ref, q_ref, k_ref, v_ref, ag_ref, a_ref, pw_ref, sc_ref,
         bias_ref) = refs[3:18]
        del refs[:18]
        dxo_ref = refs.pop(0) if top else None
        dp_ref, dw_ref, pack_ref, acc, dg, lacc, dzs, ck, cv, ce = refs
        i = pl.program_id(0)
        blk = last - i

        @pl.when(i == 0)
        def _():
            acc[...] = jnp.zeros_like(acc)
            dg[...] = jnp.zeros_like(dg)
            lacc[...] = jnp.zeros_like(lacc)
            pack_ref[...] = jnp.zeros_like(pack_ref)
            ck[...] = jnp.zeros_like(ck)
            cv[...] = jnp.zeros_like(cv)
            ce[...] = jnp.zeros_like(ce)

        if top:
            d = first[...] - second[...]
            dxo_v = d * (1.0 / D_MODEL)
            dxo_ref[...] = dxo_v
            part = jnp.sum(d * d, axis=-1, keepdims=True) * (1.0 / D_MODEL)
            lacc[...] += 0.5 * jnp.sum(part, axis=0, keepdims=True)
        else:
            dxo_v = first[...]
        yv = y_ref[...]
        r = lax.rsqrt(jnp.mean(yv * yv, axis=-1, keepdims=True) + EPS)
        yn = yv * r
        dg[...] += jnp.sum(dxo_v * yn, axis=0, keepdims=True)
        dyn = dxo_v * g_ref[layer:layer + 1, :]
        dy = (r * (dyn - yn * jnp.mean(dyn * yn, axis=-1, keepdims=True))).astype(BF16)
        dzs[...] = _nt(dy, w_ref[...])
        acc[...] += _tn(z_ref[...], dy)

        lane = lax.broadcasted_iota(jnp.int32, (1, 128), 1)
        lane2 = lax.broadcasted_iota(jnp.int32, (256, 128), 1)
        current = _band_is_current()
        for sb in reversed(range(per)):
            n = per * blk + sb
            rows = slice(BLOCK * sb, BLOCK * (sb + 1))

            uv = u_ref[rows, :]
            if sb == 0:
                halo = up_ref[BLOCK - WINDOW_HALO:, :] * (n > 0).astype(F32)
            else:
                halo = u_ref[BLOCK * sb - WINDOW_HALO:BLOCK * sb, :]
            ext = jnp.concatenate([halo, uv], axis=0)
            for g, w in enumerate(POOL_WINDOWS):
                cs = slice(BLOCK * g, BLOCK * (g + 1))
                inv = _inv_count(n, w)
                win = _window_sum(ext[:, cs], w, forward=False)[WINDOW_HALO:]
                pooled = win * inv - uv[:, cs]
                pw_g = pw_ref[g].astype(BF16)
                mixed = _nn(pooled.astype(BF16), pw_g)
                gate, dgate = _silu_parts(pg_ref[rows, cs])
                dzp = dzs[rows, cs]
                sc = sc_ref[layer:layer + 1, cs]
                dpm = dzp * gate
                dp_ref[rows, COL_PG + BLOCK * g:COL_PG + BLOCK * (g + 1)] = (dzp * (mixed * sc) * dgate).astype(BF16)
                pack_ref[ROW_SC + g:ROW_SC + g + 1, :] += jnp.sum(dpm * mixed, axis=0, keepdims=True)
                dmixed = (dpm * sc).astype(BF16)
                pack_ref[ROW_PW + BLOCK * g:ROW_PW + BLOCK * (g + 1), :] += _tn(pooled.astype(BF16), dmixed)
                dpooled = _nt(dmixed, pw_g)
                e = dpooled * inv
                lead = _window_sum(jnp.concatenate([e, ce[:WINDOW_HALO, cs]], axis=0), w, forward=True)[:BLOCK]
                dp_ref[rows, COL_U + BLOCK * g:COL_U + BLOCK * (g + 1)] = (lead - dpooled).astype(BF16)
                ce[:, cs] = e

            kx = _kv_ext(k_ref, n)
            vx = _kv_ext(v_ref, n)
            variant = jnp.minimum(n, 1) if sb == 0 else 1
            dsink_row = jnp.zeros((1, 128), F32)
            tks, tvs = [], []
            for kv in range(2):
                cs = slice(256 * kv, 256 * (kv + 1))
                k_rep = _replicate_head(kx, kv)
                v_rep = _replicate_head(vx, kv)
                q_st = _stack_heads(q_ref[rows, cs])
                gate, dgate = _silu_parts(ag_ref[rows, cs])
                dza = dzs[rows, D_POOL + 256 * kv:D_POOL + 256 * (kv + 1)]
                dp_ref[rows, COL_AG + 256 * kv:COL_AG + 256 * (kv + 1)] = (dza * a_ref[rows, cs] * dgate).astype(BF16)
                da_st = _stack_heads((dza * gate).astype(BF16))
                p, psink = _probs_keys_major(k_rep, q_st, bias_ref[variant, kv], _sink_row(sink_ref, layer, kv),
                                             current)
                dpt = _pack_band(_nt(v_rep, da_st), current)
                delta = jnp.sum(p * dpt, axis=0, keepdims=True)
                dst = _unpack_band((p * (dpt - delta) * SCALE).astype(BF16), current)
                sink_terms = psink * delta
                for g in range(GQA):
                    dsink = -jnp.sum(sink_terms[:, BLOCK * g:BLOCK * (g + 1)], axis=1, keepdims=True)
                    dsink_row = dsink_row + jnp.where(lane == kv * GQA + g, dsink, 0.0)
                dp_ref[rows, COL_Q + 256 * kv:COL_Q + 256 * (kv + 1)] = _unstack_heads(_tn(dst, k_rep)).astype(BF16)
                tks.append(_fold_heads(_nn(dst, q_st)))
                tvs.append(_fold_heads(_nn(_unpack_band(p.astype(BF16), current), da_st)))
            pack_ref[ROW_SINK:ROW_SINK + 1, :] += dsink_row
            dkx = jnp.where(lane2 < 64, tks[0], tks[1])
            dvx = jnp.where(lane2 < 64, tvs[0], tvs[1])
            dp_ref[rows, COL_K:COL_V] = (ck[...] + dkx[BLOCK:]).astype(BF16)
            dp_ref[rows, COL_V:COL_AG] = (cv[...] + dvx[BLOCK:]).astype(BF16)
            ck[...] = dkx[:BLOCK]
            cv[...] = dvx[:BLOCK]

        @pl.when(i == steps - 1)
        def _():
            dw_ref[...] = acc[...].astype(BF16)
            _rows_of(dg, pack_ref, ROW_NPOST)
            pack_ref[ROW_LOSS:ROW_LOSS + 1, :] = jnp.where(lane == 0, lacc[...], 0.0)

    row = lambda c: pl.BlockSpec((tm, c), lambda i: (last - i, 0))
    const = lambda shape: pl.BlockSpec(shape, lambda i: (0,) * len(shape))
    act = jax.ShapeDtypeStruct((SEQ, D_MODEL), F32)
    return pl.pallas_call(
        body, name=f"bwd_back{layer}", grid=(steps,),
        in_specs=[pl.BlockSpec(memory_space=pltpu.SMEM), row(D_MODEL), row(D_MODEL) if top else const((8, 128)),
                  row(D_MODEL), row(D_MODEL), const((DEPTH, D_MODEL)), _resident((D_MODEL, D_MODEL)),
                  row(D_POOL), pl.BlockSpec((BLOCK, D_POOL), lambda i: (jnp.maximum(per * (last - i) - 1, 0), 0)),
                  row(D_POOL), row(D_ATTN), _resident((SEQ, D_KV)), _resident((SEQ, D_KV)), row(D_ATTN), row(D_ATTN),
                  pl.BlockSpec((None, 4, BLOCK, BLOCK), lambda i: (layer, 0, 0, 0)), const((DEPTH, D_POOL)),
                  _resident((2, 2, BLOCK, GQA * BLOCK))],
        out_specs=([row(D_MODEL)] * (1 if top else 0)
                   + [row(D_IN), const((D_MODEL, D_MODEL)), const((PACK_ROWS, 128))]),
        out_shape=([act] * (1 if top else 0)
                   + [jax.ShapeDtypeStruct((SEQ, D_IN), BF16), jax.ShapeDtypeStruct((D_MODEL, D_MODEL), BF16),
                      jax.ShapeDtypeStruct((PACK_ROWS, 128), F32)]),
        scratch_shapes=[pltpu.VMEM((D_MODEL, D_MODEL), F32), pltpu.VMEM((1, D_MODEL), F32), pltpu.VMEM((1, 1), F32),
                        pltpu.VMEM((tm, D_MODEL), F32), pltpu.VMEM((BLOCK, D_KV), F32), pltpu.VMEM((BLOCK, D_KV), F32),
                        pltpu.VMEM((BLOCK, D_POOL), F32)],
        compiler_params=_compiler_params(("arbitrary",)),
    )(sinks, dxo_or_xf, target_or_token, y, z, norm_post, w_out, u, u, pg, q, k, v, ag, a, pool_w, pool_scale, bias)


def _bwd_in(layer, part, token, dproj, x, norm_pre, dxo=None, w_in_t=None):
    pair = part in ("dw_pair", "both_pair")
    want_dw, want_dx = part != "dx", part in ("both", "dx", "both_pair")
    tm = TOKEN_TILE
    steps = SEQ // tm
    cw = 256

    def body(*refs):
        refs = list(refs)
        dp_ref, x_ref, g_ref = refs[1:4]
        del refs[:4]
        if want_dx:
            dxo_ref, w_ref, dx_ref, dgo_ref = refs[:4]
            del refs[:4]
            dg = refs.pop()
        if pair:
            hs_ref, hm_ref, acc, mine_buf, theirs_buf, send_sem, recv_sem = refs
        elif want_dw:
            dw_ref, acc = refs
        i = pl.program_id(0)

        @pl.when(i == 0)
        def _():
            if pair:
                _handshake([(lax.axis_index("x"), lax.axis_index("y"), 1 - lax.axis_index("c"))])
            if want_dw:
                acc[...] = jnp.zeros_like(acc)
            if want_dx:
                dg[...] = jnp.zeros_like(dg)

        xv = x_ref[...]
        gv = g_ref[layer:layer + 1, :]
        r = lax.rsqrt(jnp.mean(xv * xv, axis=-1, keepdims=True) + EPS)
        xn = xv * r
        if want_dw:
            hb = (xn * gv).astype(BF16)
            for c in range(0, D_IN, cw):
                acc[c:c + cw, :] += _tn(dp_ref[:, c:c + cw], hb)
        def rows_for(q, core):
            return pl.ds(pl.multiple_of((2 * q + core) * IN_SHARD, 8), IN_SHARD)

        def swap(q):
            x, y, c = _mesh_pos()
            return pltpu.make_async_remote_copy(
                src_ref=mine_buf.at[q], dst_ref=theirs_buf.at[q], send_sem=send_sem.at[q], recv_sem=recv_sem.at[q],
                device_id=(x, y, 1 - c), device_id_type=MESH)

        if pair:
            @pl.when(i == steps - 1)
            def _():
                for q in range(4):
                    mine_buf[q] = acc[rows_for(q, 1 - lax.axis_index("c")), :].astype(BF16)
                    swap(q).start()

        if want_dx:
            dh = _nn(dp_ref[...], w_ref[...])
            dg[...] += jnp.sum(dh * xn, axis=0, keepdims=True)
            dhn = dh * gv
            dx_ref[...] = dxo_ref[...] + r * (dhn - xn * jnp.mean(dhn * xn, axis=-1, keepdims=True))

        @pl.when(i == steps - 1)
        def _():
            if pair:
                x, y, c = _mesh_pos()
                for q in range(4):
                    swap(q).wait()
                for j, q in enumerate([2 * (1 - x) + y, 2 * x + (1 - y), 2 * (1 - x) + (1 - y)]):
                    hs_ref[j] = (acc[rows_for(q, c), :] + theirs_buf[q].astype(F32)).astype(BF16)
                hm_ref[...] = acc[rows_for(2 * x + y, c), :] + theirs_buf[2 * x + y].astype(F32)
            elif want_dw:
                dw_ref[...] = acc[...].astype(BF16)
            if want_dx:
                _rows_of(dg, dgo_ref, 0)

    row = lambda c: pl.BlockSpec((tm, c), lambda i: (i, 0))
    const = lambda shape: pl.BlockSpec(shape, lambda i: (0,) * len(shape))
    in_specs = [const((8, 128)), row(D_IN), row(D_MODEL), const((DEPTH, D_MODEL))]
    operands = [token, dproj, x, norm_pre]
    out_specs, out_shape, scratch = [], [], []
    if want_dx:
        in_specs += [row(D_MODEL), _resident((D_IN, D_MODEL))]
        operands += [dxo, w_in_t]
        out_specs += [row(D_MODEL), const((8, 128))]
        out_shape += [jax.ShapeDtypeStruct((SEQ, D_MODEL), F32), jax.ShapeDtypeStruct((8, 128), F32)]
    if pair:
        out_specs += [const((3, IN_SHARD, D_MODEL)), const((IN_SHARD, D_MODEL))]
        out_shape += [jax.ShapeDtypeStruct((3, IN_SHARD, D_MODEL), BF16), jax.ShapeDtypeStruct((IN_SHARD, D_MODEL), F32)]
        scratch += [pltpu.VMEM((D_IN, D_MODEL), F32), pltpu.VMEM((4, IN_SHARD, D_MODEL), BF16),
                    pltpu.VMEM((4, IN_SHARD, D_MODEL), BF16), pltpu.SemaphoreType.DMA((4,)), pltpu.SemaphoreType.DMA((4,))]
    elif want_dw:
        out_specs.append(const((D_IN, D_MODEL)))
        out_shape.append(jax.ShapeDtypeStruct((D_IN, D_MODEL), BF16))
        scratch.append(pltpu.VMEM((D_IN, D_MODEL), F32))
    if want_dx:
        scratch.append(pltpu.VMEM((1, D_MODEL), F32))
    params = pltpu.CompilerParams(dimension_semantics=("arbitrary",), vmem_limit_bytes=VMEM_LIMIT,
                                  collective_id=COLLECTIVE_PAIR_SUM[layer] if pair else None)
    return pl.pallas_call(
        body, name=f"bwd_in_{part}{layer}", grid=(steps,),
        in_specs=in_specs, out_specs=out_specs, out_shape=out_shape, scratch_shapes=scratch,
        compiler_params=params,
    )(*operands)


def _mesh_pos():
    return lax.axis_index("x"), lax.axis_index("y"), lax.axis_index("c")


def _device_rows(ref, m, px, py, pc):
    return ref.at[pl.ds(pl.multiple_of((4 * px + 2 * py + pc) * m, 16 if m % 16 == 0 else 8), m), :]


def _allgather(srcs, out_dtype, name, later=()):
    na, nb = len(srcs), len(later)
    every = list(srcs) + list(later)
    shapes = [(a.shape[-2], a.shape[-1]) for a, _ in every]

    def body(*refs):
        xs, refs = refs[:na + nb], refs[na + nb:]
        outs, cast, land, refs = refs[:na], refs[na:na + nb], refs[na + nb:na + 2 * nb], refs[na + 2 * nb:]
        stage, raw, (send_sems, recv_sems, local_sems, load_sems) = refs[:na], refs[na:2 * na + nb], refs[2 * na + nb:]
        loads = [pltpu.make_async_copy(xs[i].at[every[i][1]], raw[i], load_sems.at[i]) for i in range(na + nb)]
        for cp in loads:
            cp.start()
        x, y, c = _mesh_pos()
        me, sibling = (x, y, c), (x, y, 1 - c)
        near = [(1 - x, y), (x, 1 - y)]
        far = (1 - x, 1 - y)
        relay_from, relay_to = (x ^ (1 - c), y ^ c), (x ^ c, y ^ (1 - c))
        _handshake([sibling] + [(*chip, c) for chip in near])
        k_from, k_to = 1 + c, 2 - c

        def slot(a, px, py, pc):
            return _device_rows(outs[a], shapes[a][0], px, py, pc)

        def copy(a, k, block, to, src=None):
            return pltpu.make_async_remote_copy(
                src_ref=slot(a, *block) if src is None else src, dst_ref=slot(a, *block),
                send_sem=send_sems.at[a, k], recv_sem=recv_sems.at[a, k], device_id=to, device_id_type=MESH)

        def cast_block(i):
            loads[i].wait()
            return raw[i][...].astype(out_dtype)

        for a in range(na):
            stage[a][...] = cast_block(a)
        mine = [pltpu.make_async_copy(stage[a], slot(a, *me), local_sems.at[a]) for a in range(na)]
        for cp in mine:
            cp.start()
        sent = []
        for a in range(na):
            sent.append(copy(a, 0, me, sibling, src=stage[a]))
            sent += [copy(a, 1 + j, me, (*chip, c), src=stage[a]) for j, chip in enumerate(near)]
        for cp in sent:
            cp.start()
        for b in range(nb):
            cast[b][...] = cast_block(na + b)
            cp = pltpu.make_async_copy(cast[b], _device_rows(land[b], shapes[na + b][0], *me), local_sems.at[na + b])
            cp.start()
            mine.append(cp)
        for a in range(na):
            copy(a, k_from, (*relay_from, c), me).wait_recv()
            sent += [copy(a, 3, (*relay_from, c), (*relay_to, c)), copy(a, 3 + k_from, (*relay_from, c), sibling)]
            sent[-2].start()
            sent[-1].start()
        for a in range(na):
            copy(a, k_to, (*relay_to, c), me).wait_recv()
            sent.append(copy(a, 3 + k_to, (*relay_to, c), sibling))
            sent[-1].start()
        for a in range(na):
            copy(a, 3, (*far, c), me).wait_recv()
            sent.append(copy(a, 6, (*far, c), sibling))
            sent[-1].start()
        for a in range(na):
            copy(a, 0, sibling, me).wait_recv()
            for j, chip in enumerate(near + [far]):
                copy(a, 4 + j, (*chip, 1 - c), me).wait_recv()
        for cp in sent:
            cp.wait_send()
        for cp in mine:
            cp.wait()

    vmem = pl.BlockSpec(memory_space=pltpu.VMEM)
    hbm = pl.BlockSpec(memory_space=pl.ANY)
    gathered = [jax.ShapeDtypeStruct((N_DEV * m, n), out_dtype) for m, n in shapes]
    out = pl.pallas_call(
        body, name=name,
        in_specs=[hbm] * (na + nb),
        out_specs=[hbm] * na + [vmem] * nb + [hbm] * nb,
        out_shape=gathered[:na] + [jax.ShapeDtypeStruct(s, out_dtype) for s in shapes[na:]] + gathered[na:],
        scratch_shapes=([pltpu.VMEM(s, out_dtype) for s in shapes[:na]]
                        + [pltpu.VMEM(s, a.dtype) for s, (a, _) in zip(shapes, every)]
                        + [pltpu.SemaphoreType.DMA((na, 7)), pltpu.SemaphoreType.DMA((na, 7)),
                           pltpu.SemaphoreType.DMA((na + nb,)), pltpu.SemaphoreType.DMA((na + nb,))]),
        compiler_params=pltpu.CompilerParams(vmem_limit_bytes=VMEM_LIMIT, collective_id=COLLECTIVE_GATHER_W0),
    )(*[a for a, _ in every])
    return out[:na], out[na:na + nb], out[na + nb:]


ALL_PEERS = tuple(range(1, N_DEV))


def _related(k, x, y, c):
    return x ^ ((k >> 2) & 1), y ^ ((k >> 1) & 1), c ^ (k & 1)


def _gather_wait(sems, block, land, relations, after, name):
    def body(src, land_ref, send_sem, recv_sem, after_ref, src_out, land_out):
        x, y, c = _mesh_pos()
        for k in relations:
            peer = _related(k, x, y, c)
            cp = pltpu.make_async_remote_copy(
                src_ref=src, dst_ref=_device_rows(land_ref, block.shape[0], *peer),
                send_sem=send_sem.at[k - 1], recv_sem=recv_sem.at[k - 1], device_id=peer, device_id_type=MESH)
            cp.wait_send()
            cp.wait_recv()

    out = pl.pallas_call(
        body, name=name,
        out_shape=(pltpu.HBM(block.shape, block.dtype), pltpu.HBM(land.shape, land.dtype)),
        in_specs=[_HBM, _HBM, _SEM, _SEM, pl.BlockSpec(memory_space=pl.ANY)],
        out_specs=[_HBM, _HBM],
        input_output_aliases={0: 0, 1: 1},
        compiler_params=pltpu.CompilerParams(has_side_effects=_EFFECT),
    )(block, land, sems[0], sems[1], after)
    return out[1]


(COLLECTIVE_GATHER_W0, COLLECTIVE_GATHER_W1, COLLECTIVE_EXCHANGE_1, COLLECTIVE_EXCHANGE_0A, COLLECTIVE_EXCHANGE_0B,
 COLLECTIVE_GATHER_SMALL) = range(1, 7)
COLLECTIVE_PAIR_SUM = (7, 8)


def _handshake(peers):
    barrier = pltpu.get_barrier_semaphore()
    for peer in peers:
        pl.semaphore_signal(barrier, inc=1, device_id=peer, device_id_type=MESH)
    pl.semaphore_wait(barrier, len(peers))


_HBM = pl.BlockSpec(memory_space=pltpu.HBM)
_SEM = pl.BlockSpec(memory_space=pltpu.SEMAPHORE)
_EFFECT = pltpu.SideEffectType.DATAFLOW_SIDE_EFFECTING


def _exchange_plan(direct):
    x, y, c = _mesh_pos()
    if not direct:
        return [(j, j, (qx, qy, c)) for j, (qx, qy) in enumerate([(1 - x, y), (x, 1 - y), (1 - x, 1 - y)])]
    plan = []
    for k in range(1, N_DEV):
        px, py, pc = x ^ ((k >> 2) & 1), y ^ ((k >> 1) & 1), c ^ (k & 1)
        plan.append((4 * px + 2 * py + pc, k - 1, (px, py, pc)))
    return plan


def _exchange_copies(directs):
    copies, base = [], 0
    for a, direct in enumerate(directs):
        plan = _exchange_plan(direct)
        copies += [(a, block, slot, peer, base + slot) for block, slot, peer in plan]
        base += len(plan)
    return copies, base


def _exchange_start(srcs, directs, collective_id, name):
    na = len(srcs)
    slots = [N_DEV - 1 if direct else 3 for direct in directs]

    def body(*refs):
        src, land = refs[:na], refs[na:2 * na]
        send_sem, recv_sem = refs[2 * na], refs[2 * na + 1]
        token = refs[-1]
        _handshake([peer for _, _, peer in _exchange_plan(any(directs))])
        for a, block, slot, peer, sem in _exchange_copies(directs)[0]:
            pltpu.make_async_remote_copy(
                src_ref=src[a].at[block], dst_ref=land[a].at[slot], send_sem=send_sem.at[sem],
                recv_sem=recv_sem.at[sem], device_id=peer, device_id_type=MESH).start()
        token[...] = jnp.zeros_like(token)

    zones = [jax.ShapeDtypeStruct((n,) + t.shape[1:], t.dtype) for n, t in zip(slots, srcs)]
    bufs = [pltpu.HBM(t.shape, t.dtype) for t in list(srcs) + zones]
    out = pl.pallas_call(
        body, name=name,
        out_shape=(pltpu.SemaphoreType.DMA((sum(slots),)), pltpu.SemaphoreType.DMA((sum(slots),)), *bufs,
                   jax.ShapeDtypeStruct((8, 128), F32)),
        in_specs=[_HBM] * (2 * na),
        out_specs=(_SEM, _SEM, *([_HBM] * (2 * na)), pl.BlockSpec(memory_space=pltpu.VMEM)),
        input_output_aliases={i: 2 + i for i in range(2 * na)},
        compiler_params=pltpu.CompilerParams(has_side_effects=_EFFECT, collective_id=collective_id),
    )(*[pltpu.with_memory_space_constraint(t, pltpu.HBM) for t in srcs],
      *[pltpu.with_memory_space_constraint(lax.empty(t.shape, t.dtype), pltpu.HBM) for t in zones])
    return out[0], out[1], out[2:2 + na], out[2 + na:2 + 2 * na], out[-1]


def _exchange_wait(send_sem, recv_sem, srcs, lands, directs, after, name):
    na = len(srcs)

    def body(*refs):
        src, land = refs[:na], refs[na:2 * na]
        send_sem_ref, recv_sem_ref = refs[2 * na], refs[2 * na + 1]
        for a, block, slot, peer, sem in _exchange_copies(directs)[0]:
            cp = pltpu.make_async_remote_copy(
                src_ref=src[a].at[block], dst_ref=land[a].at[slot], send_sem=send_sem_ref.at[sem],
                recv_sem=recv_sem_ref.at[sem], device_id=peer, device_id_type=MESH)
            cp.wait_send()
            cp.wait_recv()

    bufs = [pltpu.HBM(t.shape, t.dtype) for t in list(srcs) + list(lands)]
    out = pl.pallas_call(
        body, name=name,
        out_shape=tuple(bufs),
        in_specs=[_HBM] * (2 * na) + [_SEM, _SEM, pl.BlockSpec(memory_space=pl.ANY)],
        out_specs=[_HBM] * (2 * na),
        input_output_aliases={i: i for i in range(2 * na)},
        compiler_params=pltpu.CompilerParams(has_side_effects=_EFFECT),
    )(*srcs, *lands, send_sem, recv_sem, after)
    return out[:na], out[na:]


def _own_then_slots(mine_ref, lands_ref, rows=slice(None)):
    if len(mine_ref.shape) == 3:
        x, y, c = _mesh_pos()
        total = mine_ref[4 * x + 2 * y + c, rows, :].astype(F32)
    else:
        total = mine_ref[rows, :].astype(F32)
    for j in range(lands_ref.shape[0]):
        total = total + lands_ref[j, rows, :].astype(F32)
    return total


SMALL_ROWS = 2 * PACK_SLICE + 2 * 8


def _small_gather_start(mine, lands, dgpre, name):
    def body(*refs):
        hm, ld, dg = refs[:DEPTH], refs[DEPTH:2 * DEPTH], refs[2 * DEPTH:3 * DEPTH]
        send_sem, recv_sem, blk, land, token, own, slots, rows, built, local_sems = refs[3 * DEPTH:]
        x, y, c = _mesh_pos()
        loads = []
        for l in range(DEPTH):
            loads += [pltpu.make_async_copy(hm[l].at[4 * x + 2 * y + c], own.at[l], local_sems.at[3 * l]),
                      pltpu.make_async_copy(ld[l], slots.at[l], local_sems.at[3 * l + 1]),
                      pltpu.make_async_copy(dg[l], rows.at[l], local_sems.at[3 * l + 2])]
        for cp in loads:
            cp.start()
        _handshake([_related(k, x, y, c) for k in ALL_PEERS])
        for cp in loads:
            cp.wait()
        for l in range(DEPTH):
            total = own[l]
            for j in range(N_DEV - 1):
                total = total + slots[l, j]
            built[PACK_SLICE * l:PACK_SLICE * (l + 1), :] = total
            built[2 * PACK_SLICE + 8 * l:2 * PACK_SLICE + 8 * (l + 1), :] = rows[l]
        stores = [pltpu.make_async_copy(built, blk, local_sems.at[3 * DEPTH]),
                  pltpu.make_async_copy(built, _device_rows(land, SMALL_ROWS, x, y, c), local_sems.at[3 * DEPTH + 1])]
        for cp in stores:
            cp.start()
        for cp in stores:
            cp.wait()
        for k in ALL_PEERS:
            pltpu.make_async_remote_copy(
                src_ref=blk, dst_ref=_device_rows(land, SMALL_ROWS, x, y, c), send_sem=send_sem.at[k - 1],
                recv_sem=recv_sem.at[k - 1], device_id=_related(k, x, y, c), device_id_type=MESH).start()
        token[...] = jnp.zeros_like(token)

    hbm = pl.BlockSpec(memory_space=pl.ANY)
    out = pl.pallas_call(
        body, name=name,
        in_specs=[hbm] * (3 * DEPTH),
        out_specs=(_SEM, _SEM, _HBM, _HBM, pl.BlockSpec(memory_space=pltpu.VMEM)),
        out_shape=(pltpu.SemaphoreType.DMA((N_DEV - 1,)), pltpu.SemaphoreType.DMA((N_DEV - 1,)),
                   pltpu.HBM((SMALL_ROWS, 128), F32), pltpu.HBM((N_DEV * SMALL_ROWS, 128), F32),
                   jax.ShapeDtypeStruct((8, 128), F32)),
        scratch_shapes=[pltpu.VMEM((DEPTH, PACK_SLICE, 128), F32), pltpu.VMEM((DEPTH, N_DEV - 1, PACK_SLICE, 128), F32),
                        pltpu.VMEM((DEPTH, 8, 128), F32), pltpu.VMEM((SMALL_ROWS, 128), F32),
                        pltpu.SemaphoreType.DMA((3 * DEPTH + 2,))],
        compiler_params=pltpu.CompilerParams(has_side_effects=_EFFECT, collective_id=COLLECTIVE_GATHER_SMALL),
    )(*mine, *lands, *dgpre)
    return (out[0], out[1]), out[2], out[3], out[4]


def _adamw_math(w, g, m, v):
    m = ADAM_B1 * m + (1.0 - ADAM_B1) * g
    v = ADAM_B2 * v + (1.0 - ADAM_B2) * (g * g)
    m_hat = m / (1.0 - ADAM_B1 ** ADAM_STEP)
    v_hat = v / (1.0 - ADAM_B2 ** ADAM_STEP)
    delta = -ADAM_LR * (m_hat / (jnp.sqrt(v_hat) + ADAM_EPS) + ADAM_WD * w)
    return delta, m, v


def _adamw_layer(layer, parts, token, name):
    steps = {w.shape[1] // rows for _, _, w, _, _, _, rows in parts}
    assert len(steps) == 1, steps
    n = len(parts)

    def body(_, *refs):
        for p in range(n):
            hm_ref, ld_ref, w_ref, m_ref, v_ref = refs[5 * p:5 * (p + 1)]
            g_ref, d_ref, nm_ref, nv_ref = refs[len(refs) - 4 * (n - p):len(refs) - 4 * (n - p - 1)]
            g = _own_then_slots(hm_ref, ld_ref)
            g_ref[...] = g
            d, nm, nv = _adamw_math(w_ref[...], g, m_ref[...], v_ref[...])
            d_ref[...] = d
            nm_ref[...] = nm
            nv_ref[...] = nv

    in_specs, out_specs, out_shape, operands, carried, aliases = [pl.BlockSpec(memory_space=pl.ANY)], [], [], [], [], {}
    for p, (mine, lands, w, m, v, earlier, rows) in enumerate(parts):
        nn = w.shape[2]
        spec = pl.BlockSpec((None, rows, nn), lambda i: (layer, i, 0))
        in_specs += [pl.BlockSpec((rows, nn), lambda i: (i, 0)) if mine.ndim == 2
                     else pl.BlockSpec((N_DEV, rows, nn), lambda i: (0, i, 0)),
                     pl.BlockSpec((lands.shape[0], rows, nn), lambda i: (0, i, 0)), spec, spec, spec]
        out_specs += [spec] * 4
        out_shape += [jax.ShapeDtypeStruct(w.shape, F32)] * 4
        operands += [mine, lands, w, m, v]
        if earlier is not None:
            aliases.update({1 + 5 * n + len(carried) + t: 4 * p + t for t in range(4)})
            carried += list(earlier)
    out = pl.pallas_call(
        body, name=name, grid=(steps.pop(),),
        in_specs=in_specs + [pl.BlockSpec(memory_space=pl.ANY)] * len(carried),
        out_specs=out_specs, out_shape=out_shape, input_output_aliases=aliases,
        compiler_params=_compiler_params(("arbitrary",)),
    )(token, *operands, *carried)
    return [out[4 * p:4 * (p + 1)] for p in range(n)]


def _adamw_small(gathered, params):
    def body(all_ref, *refs):
        ins, outs, packs = refs[:15], refs[15:15 + 21], refs[15 + 21]
        loss_ref = outs[0]
        for dev in range(N_DEV):
            for l in range(DEPTH):
                packs[l, PACK_SLICE * dev:PACK_SLICE * (dev + 1), :] = (
                    all_ref[SMALL_ROWS * dev + PACK_SLICE * l:SMALL_ROWS * dev + PACK_SLICE * (l + 1), :])
        loss_ref[...] = packs[DEPTH - 1, ROW_LOSS:ROW_LOSS + 1, 0:1]

        def update(p, sel, g):
            w_ref, m_ref, v_ref = ins[p], ins[5 + p], ins[10 + p]
            d, nm, nv = _adamw_math(w_ref[sel], g, m_ref[sel], v_ref[sel])
            for t, val in enumerate((g, d, nm, nv)):
                outs[1 + 5 * t + p][sel] = val

        for l in range(DEPTH):
            gp = packs.at[l]
            row0 = 2 * PACK_SLICE + 8 * l
            dgpre = all_ref[row0:row0 + 8, :]
            for dev in range(1, N_DEV):
                dgpre = dgpre + all_ref[SMALL_ROWS * dev + row0:SMALL_ROWS * dev + row0 + 8, :]
            for grp in range(4):
                update(0, (l, grp), gp[ROW_PW + BLOCK * grp:ROW_PW + BLOCK * (grp + 1), :])
                update(1, (slice(l, l + 1), slice(128 * grp, 128 * (grp + 1))), gp[ROW_SC + grp:ROW_SC + grp + 1, :])
            update(2, (slice(l, l + 1), slice(None)), gp[ROW_SINK:ROW_SINK + 1, 0:N_HEADS])
            for r in range(D_MODEL // 128):
                sel = (slice(l, l + 1), slice(128 * r, 128 * (r + 1)))
                update(3, sel, dgpre[r:r + 1, :])
                update(4, sel, gp[ROW_NPOST + r:ROW_NPOST + r + 1, :])

    shapes = [jax.ShapeDtypeStruct(p.shape, F32) for p in params[:5]]
    return pl.pallas_call(
        body, name="adamw_small",
        out_shape=[jax.ShapeDtypeStruct((1, 1), F32)] + shapes * 4,
        scratch_shapes=[pltpu.VMEM((DEPTH, PACK_ROWS, 128), F32)],
        compiler_params=_compiler_params(),
    )(gathered, *params)


def kernel(x, w_in, pool_w, pool_scale, attn_sinks, w_out, norm_pre, norm_post, loss_target, m_w_in, m_pool_w, m_pool_scale, m_attn_sinks, m_w_out, m_norm_pre, m_norm_post, v_w_in, v_pool_w, v_pool_scale, v_attn_sinks, v_w_out, v_norm_pre, v_norm_post):
    x0 = x.reshape(SEQ, D_MODEL)
    target = loss_target.reshape(SEQ, D_MODEL)
    bias = jnp.asarray(_attn_bias())
    w_in_t, m_in_t, v_in_t = (jnp.swapaxes(t, 1, 2) for t in (w_in, m_w_in, v_w_in))

    (win0, wout0), later, lands = _allgather([(w_in_t, 0), (w_out, 0)], BF16, "gather_w0",
                                              later=[(w_in_t, 1), (w_out, 1)])
    win_full, wout_full = [win0, None], [wout0, None]
    token = x0

    saved = []
    xl = x0
    for layer in range(DEPTH):
        front = (layer, xl, norm_pre, win_full[layer], token, attn_sinks, pool_w, pool_scale, bias)
        if layer == 0:
            u, pg, q, k, v, ag, z, a, x_next, y, win_full[1], wout_full[1] = _fwd_front(
                *front, out=(norm_post, wout_full[layer]), gather=lands)
        else:
            u, pg, q, k, v, ag, z, a, x_next, y = _fwd_front(*front, out=(norm_post, wout_full[layer]))
        saved.append((xl, u, pg, q, k, v, ag, z, a, y))
        xl = x_next

    params_small = [pool_w, pool_scale, attn_sinks, norm_pre, norm_post,
                    m_pool_w, m_pool_scale, m_attn_sinks, m_norm_pre, m_norm_post,
                    v_pool_w, v_pool_scale, v_attn_sinks, v_norm_pre, v_norm_post]

    def start(srcs, directs, paired, collective_id, tag):
        send_sem, recv_sem, srcs, lands, started = _exchange_start(srcs, directs, collective_id, f"exchange_start{tag}")
        return (send_sem, recv_sem, srcs, lands, paired, directs), started

    def finish(handle, after, tag):
        send_sem, recv_sem, srcs, lands, paired, directs = handle
        srcs, lands = _exchange_wait(send_sem, recv_sem, srcs, lands, directs, after, f"exchange_wait{tag}")
        return [s if p is None else p for s, p in zip(srcs, paired)], lands

    def back(layer, top, first, second):
        xin, u, pg, q, k, v, ag, z, a, y = saved[layer]
        return _bwd_back(layer, top, first, second, y, z, norm_post, wout_full[layer], attn_sinks, u, pg, q, k, v,
                         ag, a, pool_w, pool_scale, bias)

    dgpre = [None] * DEPTH
    dx, dproj, gw_out, pack = back(1, True, xl, target)
    dx, dgpre[1], chip_sums, own_sum = _bwd_in(1, "both_pair", token, dproj, saved[1][0], norm_pre, dx, win_full[1])
    top, token = start([chip_sums, gw_out.reshape(N_DEV, OUT_SHARD, D_MODEL), pack.reshape(N_DEV, PACK_SLICE, 128)],
                       [False, True, True], [own_sum, None, None], COLLECTIVE_EXCHANGE_1, "1")

    dproj, gw_out, pack = back(0, False, dx, token)
    early, token = start([gw_out.reshape(N_DEV, OUT_SHARD, D_MODEL), pack.reshape(N_DEV, PACK_SLICE, 128)],
                         [True, True], [None, None], COLLECTIVE_EXCHANGE_0A, "0a")
    chip_sums, own_sum = _bwd_in(0, "dw_pair", token, dproj, saved[0][0], norm_pre)
    late, token = start([chip_sums], [False], [own_sum], COLLECTIVE_EXCHANGE_0B, "0b")
    dx, dgpre[0] = _bwd_in(0, "dx", token, dproj, saved[0][0], norm_pre, dx, win_full[0])

    own1, lands1 = finish(top, dx, "1")
    own0a, lands0a = finish(early, dx, "0a")
    sems, block, land, token = _small_gather_start([own0a[1], own1[2]], [lands0a[1], lands1[2]], dgpre,
                                                   "gather_small_start")
    big_in, big_out = _adamw_layer(1, [(own1[0], lands1[0], w_in_t, m_in_t, v_in_t, None, ADAM_ROWS_IN),
                                       (own1[1], lands1[1], w_out, m_w_out, v_w_out, None, ADAM_ROWS_OUT)],
                                   token, "adamw1")
    own0b, lands0b = finish(late, big_out[0], "0b")
    big_in, big_out = _adamw_layer(0, [(own0b[0], lands0b[0], w_in_t, m_in_t, v_in_t, big_in, ADAM_ROWS_IN),
                                       (own0a[0], lands0a[0], w_out, m_w_out, v_w_out, big_out, ADAM_ROWS_OUT)],
                                   token, "adamw0")
    gathered = _gather_wait(sems, block, land, ALL_PEERS, big_in[0], "gather_small_wait")
    small_out = _adamw_small(gathered, params_small)
    loss = small_out[0].reshape(())

    outs = [loss, dx.reshape(1, SEQ, D_MODEL)]
    for t in range(4):
        pw_, sc_, sk_, npre_, npost_ = small_out[1 + 5 * t:6 + 5 * t]
        outs += [jnp.swapaxes(big_in[t], 1, 2), pw_, sc_, sk_, big_out[t], npre_, npost_]
    return tuple(outs)
```

```python
import functools
import numpy as np
import jax
import jax.numpy as jnp
from jax import lax
from jax.experimental import pallas as pl
from jax.experimental.pallas import tpu as pltpu

F32 = jnp.float32
BF16 = jnp.bfloat16

N_DEV = 8
SEQ = 2048
D_MODEL = 1024
D_POOL = 512
D_ATTN = 512
D_KV = 128
D_IN = 2304
N_HEADS = 8
GQA = 4
HEAD_DIM = 64
BLOCK = 128
POOL_WINDOWS = (2, 4, 8, 16)
DEPTH = 2
EPS = 1e-6
NEG_INF = -1e30
SCALE = HEAD_DIM ** -0.5
IN_SHARD = D_IN // N_DEV
OUT_SHARD = D_MODEL // N_DEV

COL_U, COL_PG, COL_Q, COL_K, COL_V, COL_AG = 0, 512, 1024, 1536, 1664, 1792

ADAM_LR = 0.001
ADAM_B1 = 0.9
ADAM_B2 = 0.999
ADAM_EPS = 1e-08
ADAM_WD = 0.01
ADAM_STEP = 10

TOKEN_TILE = 512
ADAM_ROWS_IN, ADAM_ROWS_OUT = 144, 64
VMEM_LIMIT = 56 * 1024 * 1024
MESH = pl.DeviceIdType.MESH

ROW_PW, ROW_SC, ROW_SINK, ROW_NPOST, ROW_LOSS = 0, 512, 520, 536, 544
PACK_ROWS = 576
PACK_SLICE = PACK_ROWS // N_DEV


def _nn(a, b):
    return jnp.dot(a, b, preferred_element_type=F32)


def _nt(a, b):
    return lax.dot_general(a, b, (((1,), (1,)), ((), ())), preferred_element_type=F32)


def _tn(a, b):
    return lax.dot_general(a, b, (((0,), (0,)), ((), ())), preferred_element_type=F32)


def _silu_parts(g):
    s = jax.nn.sigmoid(g)
    return g * s, s * (1.0 + g * (1.0 - s))


def _resident(shape):
    return pl.BlockSpec(shape, lambda *_: (0,) * len(shape), pipeline_mode=pl.Buffered(1))


def _compiler_params(sem=None):
    if sem is None:
        return pltpu.CompilerParams(vmem_limit_bytes=VMEM_LIMIT)
    return pltpu.CompilerParams(dimension_semantics=sem, vmem_limit_bytes=VMEM_LIMIT)


def _attn_bias():
    t = np.arange(BLOCK)[None, :]
    j = np.arange(BLOCK)[:, None]
    current = j <= t
    dist = np.where(current, t - j, t + BLOCK - j).astype(np.float32)
    out = np.zeros((2, 2, BLOCK, GQA * BLOCK), np.float32)
    for variant in range(2):
        valid = current | (variant == 1)
        for kv in range(2):
            for g in range(GQA):
                slope = np.float32(2.0 ** (-(kv * GQA + g + 1)))
                out[variant, kv, :, g * BLOCK:(g + 1) * BLOCK] = np.where(valid, -slope * dist, np.float32(NEG_INF))
    return out


def _replicate_head(kx, kv):
    rolled = pltpu.roll(kx, 64, 1)
    lane = lax.broadcasted_iota(jnp.int32, kx.shape, 1)
    dup = jnp.where(lane < 64, kx, rolled) if kv == 0 else jnp.where(lane < 64, rolled, kx)
    return jnp.concatenate([dup, dup], axis=1).astype(BF16)


def _stack_heads(qv):
    lane = lax.broadcasted_iota(jnp.int32, qv.shape, 1)
    zero = jnp.zeros_like(qv)
    return jnp.concatenate([jnp.where((lane >= 64 * g) & (lane < 64 * g + 64), qv, zero) for g in range(GQA)], axis=0)


def _unstack_heads(xs):
    lane = lax.broadcasted_iota(jnp.int32, (BLOCK, 256), 1)
    return jnp.where(lane < 64, xs[0:128], jnp.where(lane < 128, xs[128:256], jnp.where(lane < 192, xs[256:384], xs[384:512])))


def _fold_heads(r):
    h = r[:, 0:128] + r[:, 128:256]
    return h + pltpu.roll(h, 64, 1)


def _sink_row(sink_ref, layer, kv):
    lane = lax.broadcasted_iota(jnp.int32, (1, GQA * BLOCK), 1)
    s4 = [sink_ref[layer, kv * GQA + g] for g in range(GQA)]
    return jnp.where(lane < 128, s4[0], jnp.where(lane < 256, s4[1], jnp.where(lane < 384, s4[2], s4[3])))


def _band_is_current():
    j = lax.broadcasted_iota(jnp.int32, (BLOCK, GQA * BLOCK), 0)
    t = lax.broadcasted_iota(jnp.int32, (BLOCK, GQA * BLOCK), 1) & (BLOCK - 1)
    return j <= t


def _pack_band(full, current):
    return jnp.where(current, full[BLOCK:], full[:BLOCK])


def _unpack_band(packed, current):
    zero = jnp.zeros_like(packed)
    return jnp.concatenate([jnp.where(current, zero, packed), jnp.where(current, packed, zero)], axis=0)


def _probs_keys_major(k_rep, q_st, bias, sink, current):
    st = _pack_band(_nt(k_rep, q_st), current) * SCALE + bias
    m = jnp.maximum(jnp.max(st, axis=0, keepdims=True), sink)
    p = jnp.exp(st - m)
    esink = jnp.exp(sink - m)
    rl = 1.0 / (jnp.sum(p, axis=0, keepdims=True) + esink)
    return p * rl, esink * rl


WINDOW_HALO = 16


def _window_sum(ext, w, forward):
    s = ext
    sh = 1
    while sh < w:
        s = s + pltpu.roll(s, (ext.shape[0] - sh) if forward else sh, 0)
        sh *= 2
    return s


def _inv_count(n, w):
    t = n * BLOCK + lax.broadcasted_iota(jnp.int32, (BLOCK, 1), 0) + 1
    return 1.0 / jnp.minimum(t.astype(F32), float(w))


def _kv_ext(ref, n):
    r0 = pl.multiple_of(jnp.maximum(n - 1, 0) * BLOCK, BLOCK)
    r1 = pl.multiple_of(n * BLOCK, BLOCK)
    return jnp.concatenate([ref[pl.ds(r0, BLOCK), :], ref[pl.ds(r1, BLOCK), :]], axis=0)


def _rows_of(vec_ref, pack_ref, row0):
    for r in range(D_MODEL // 128):
        pack_ref[row0 + r:row0 + r + 1, :] = vec_ref[:, 128 * r:128 * (r + 1)]


FRONT_TILE = 4 * BLOCK


def _fwd_front(layer, x, norm_pre, w_in_t, token, sinks, pool_w, pool_scale, bias, out=None, gather=None):
    tm = FRONT_TILE

    def body(sink_ref, x_ref, g_ref, w_ref, _, pw_ref, sc_ref, bias_ref, *refs):
        if out is not None:
            gpost_ref, wo_ref, *refs = refs
        if gather is not None:
            refs = refs[2:]
            zones, (send_sems, recv_sems), refs = refs[10:12], refs[-2:], refs[:-2]
        if out is not None:
            xn_ref, y_ref = refs[8:10]
        u_ref, pg_ref, q_ref, k_ref, v_ref, ag_ref, z_ref, a_ref = refs[:8]
        uprev, kprev, vprev = refs[-3:]
        i = pl.program_id(0)

        @pl.when(i == 0)
        def _():
            uprev[...] = jnp.zeros_like(uprev)
            kprev[...] = jnp.zeros_like(kprev)
            vprev[...] = jnp.zeros_like(vprev)

        if gather is not None:
            px, py, pc = _mesh_pos()
            me, sibling = (px, py, pc), (px, py, 1 - pc)
            near = [(1 - px, py), (px, 1 - py)]
            far = (1 - px, 1 - py)
            relay_from, relay_to = (px ^ (1 - pc), py ^ pc), (px ^ pc, py ^ (1 - pc))
            k_from, k_to = 1 + pc, 2 - pc

            def copy(a, sem, block, to):
                rows_of_block = _device_rows(zones[a], gather[a].shape[0] // N_DEV, *block)
                return pltpu.make_async_remote_copy(
                    src_ref=rows_of_block, dst_ref=rows_of_block, send_sem=send_sems.at[a, sem],
                    recv_sem=recv_sems.at[a, sem], device_id=to, device_id_type=MESH)

            def pass_on(a):
                copy(a, k_from, (*relay_from, pc), me).wait_recv()
                copy(a, 3, (*relay_from, pc), (*relay_to, pc)).start()
                copy(a, 3 + k_from, (*relay_from, pc), sibling).start()
                copy(a, k_to, (*relay_to, pc), me).wait_recv()
                copy(a, 3 + k_to, (*relay_to, pc), sibling).start()

            @pl.when(i == 0)
            def _():
                _handshake([sibling] + [(*chip, pc) for chip in near])
                for a in range(2):
                    copy(a, 0, me, sibling).start()
                    for j, chip in enumerate(near):
                        copy(a, 1 + j, me, (*chip, pc)).start()

            for a in range(2):
                pl.when(i == 2 + a)(functools.partial(pass_on, a))

        xv = x_ref[...]
        r = lax.rsqrt(jnp.mean(xv * xv, axis=-1, keepdims=True) + EPS)
        h = (xv * r * g_ref[layer:layer + 1, :]).astype(BF16)
        u_ref[...] = _nt(h, w_ref[COL_U:COL_PG, :])
        pg_ref[...] = _nt(h, w_ref[COL_PG:COL_Q, :])
        for sb in range(tm // BLOCK):
            n = (tm // BLOCK) * i + sb
            rows = slice(BLOCK * sb, BLOCK * (sb + 1))
            before = slice(BLOCK * (sb - 1), BLOCK * sb)
            uv = u_ref[rows, :]
            halo = (uprev[BLOCK - WINDOW_HALO:, :] if sb == 0
                    else u_ref[BLOCK * sb - WINDOW_HALO:BLOCK * sb, :])
            ext = jnp.concatenate([halo, uv], axis=0)
            for g, w in enumerate(POOL_WINDOWS):
                cs = slice(BLOCK * g, BLOCK * (g + 1))
                win = _window_sum(ext[:, cs], w, forward=False)[WINDOW_HALO:]
                pooled = win * _inv_count(n, w) - uv[:, cs]
                mixed = _nn(pooled.astype(BF16), pw_ref[g].astype(BF16))
                gate, _ = _silu_parts(pg_ref[rows, cs])
                z_ref[rows, cs] = (mixed * sc_ref[layer:layer + 1, cs] * gate).astype(BF16)

        q_ref[...] = _nt(h, w_ref[COL_Q:COL_K, :]).astype(BF16)
        k_ref[...] = _nt(h, w_ref[COL_K:COL_V, :])
        v_ref[...] = _nt(h, w_ref[COL_V:COL_AG, :])
        ag_ref[...] = _nt(h, w_ref[COL_AG:D_IN, :])

        current = _band_is_current()
        for sb in range(tm // BLOCK):
            n = (tm // BLOCK) * i + sb
            rows = slice(BLOCK * sb, BLOCK * (sb + 1))
            before = slice(BLOCK * (sb - 1), BLOCK * sb)
            kx = jnp.concatenate([kprev[...] if sb == 0 else k_ref[before, :], k_ref[rows, :]], axis=0)
            vx = jnp.concatenate([vprev[...] if sb == 0 else v_ref[before, :], v_ref[rows, :]], axis=0)
            variant = jnp.minimum(n, 1) if sb == 0 else 1
            for kv in range(2):
                cs = slice(256 * kv, 256 * (kv + 1))
                p, _ = _probs_keys_major(_replicate_head(kx, kv), _stack_heads(q_ref[rows, cs]),
                                         bias_ref[variant, kv], _sink_row(sink_ref, layer, kv), current)
                o = _unstack_heads(_tn(_unpack_band(p.astype(BF16), current), _replicate_head(vx, kv)))
                a_ref[rows, cs] = o
                gate, _ = _silu_parts(ag_ref[rows, cs])
                z_ref[rows, D_POOL + 256 * kv:D_POOL + 256 * (kv + 1)] = (o * gate).astype(BF16)

        tail = slice(tm - BLOCK, tm)
        uprev[...] = u_ref[tail, :]
        kprev[...] = k_ref[tail, :]
        vprev[...] = v_ref[tail, :]

        if out is not None:
            y = _nn(z_ref[...], wo_ref[...])
            y_ref[...] = y
            r = lax.rsqrt(jnp.mean(y * y, axis=-1, keepdims=True) + EPS)
            xn_ref[...] = x_ref[...] + y * r * gpost_ref[layer:layer + 1, :]

        if gather is not None:
            @pl.when(i == SEQ // tm - 1)
            def _():
                for a in range(2):
                    copy(a, 3, (*far, pc), me).wait_recv()
                    copy(a, 6, (*far, pc), sibling).start()
                for a in range(2):
                    copy(a, 0, sibling, me).wait_recv()
                    for j, chip in enumerate(near + [far]):
                        copy(a, 4 + j, (*chip, 1 - pc), me).wait_recv()
                for a in range(2):
                    copy(a, 0, me, sibling).wait_send()
                    for j, chip in enumerate(near):
                        copy(a, 1 + j, me, (*chip, pc)).wait_send()
                    copy(a, 3, (*relay_from, pc), (*relay_to, pc)).wait_send()
                    copy(a, 3 + k_from, (*relay_from, pc), sibling).wait_send()
                    copy(a, 3 + k_to, (*relay_to, pc), sibling).wait_send()
                    copy(a, 6, (*far, pc), sibling).wait_send()

    row = lambda c: pl.BlockSpec((tm, c), lambda i: (i, 0))
    const = lambda shape: pl.BlockSpec(shape, lambda i: (0,) * len(shape))
    act = jax.ShapeDtypeStruct((SEQ, D_MODEL), F32)
    fused = out is not None
    hbm = pl.BlockSpec(memory_space=pl.ANY)
    zones, params = [], _compiler_params(("arbitrary",))
    if gather is not None:
        assert fused and SEQ // tm == 4, "the gather's copies are spread over four grid steps"
        zones = list(gather)
        params = pltpu.CompilerParams(dimension_semantics=("arbitrary",), vmem_limit_bytes=VMEM_LIMIT,
                                      collective_id=COLLECTIVE_GATHER_W1)
    return pl.pallas_call(
        body, name=f"fwd_front{layer}", grid=(SEQ // tm,),
        in_specs=[pl.BlockSpec(memory_space=pltpu.SMEM), row(D_MODEL), const((DEPTH, D_MODEL)),
                  _resident((D_IN, D_MODEL)), const((8, 128)),
                  pl.BlockSpec((None, 4, BLOCK, BLOCK), lambda i: (layer, 0, 0, 0)), const((DEPTH, D_POOL)),
                  _resident((2, 2, BLOCK, GQA * BLOCK))]
                 + ([const((DEPTH, D_MODEL)), _resident((D_MODEL, D_MODEL))] if fused else []) + [hbm] * len(zones),
        out_specs=[row(D_POOL), row(D_POOL), row(D_ATTN), row(D_KV), row(D_KV), row(D_ATTN), row(D_MODEL),
                   row(D_ATTN)] + ([row(D_MODEL)] * 2 if fused else []) + [hbm] * len(zones),
        out_shape=[jax.ShapeDtypeStruct((SEQ, D_POOL), F32), jax.ShapeDtypeStruct((SEQ, D_POOL), F32),
                   jax.ShapeDtypeStruct((SEQ, D_ATTN), BF16), jax.ShapeDtypeStruct((SEQ, D_KV), F32),
                   jax.ShapeDtypeStruct((SEQ, D_KV), F32), jax.ShapeDtypeStruct((SEQ, D_ATTN), F32),
                   jax.ShapeDtypeStruct((SEQ, D_MODEL), BF16), jax.ShapeDtypeStruct((SEQ, D_ATTN), F32)]
                  + ([act] * 2 if fused else []) + [jax.ShapeDtypeStruct(t.shape, t.dtype) for t in zones],
        scratch_shapes=[pltpu.VMEM((BLOCK, D_POOL), F32), pltpu.VMEM((BLOCK, D_KV), F32),
                        pltpu.VMEM((BLOCK, D_KV), F32)]
                       + ([pltpu.SemaphoreType.DMA((2, 7)), pltpu.SemaphoreType.DMA((2, 7))] if zones else []),
        input_output_aliases={10 + a: 10 + a for a in range(len(zones))},
        compiler_params=params,
    )(sinks, x, norm_pre, w_in_t, token, pool_w, pool_scale, bias, *(out if fused else ()), *zones)


BACK_TILE = 2 * BLOCK


def _bwd_back(layer, top, dxo_or_xf, target_or_token, y, z, norm_post, w_out, sinks, u, pg, q, k, v, ag, a,
              pool_w, pool_scale, bias):
    tm = BACK_TILE
    steps = SEQ // tm
    last = steps - 1
    per = tm // BLOCK

    def body(*refs):
        refs = list(refs)
        sink_ref, first, second = refs[:3]
        (y_ref, z_ref, g_ref, w_ref, u_ref, up_ref, pg_ref, q_ref, k_ref, v_ref, ag_ref, a_ref, pw_ref, sc_ref,
         bias_ref) = refs[3:18]
        del refs[:18]
        dxo_ref = refs.pop(0) if top else None
        dp_ref, dw_ref, pack_ref, acc, dg, lacc, dzs, ck, cv, ce = refs
        i = pl.program_id(0)
        blk = last - i

        @pl.when(i == 0)
        def _():
            acc[...] = jnp.zeros_like(acc)
            dg[...] = jnp.zeros_like(dg)
            lacc[...] = jnp.zeros_like(lacc)
            pack_ref[...] = jnp.zeros_like(pack_ref)
            ck[...] = jnp.zeros_like(ck)
            cv[...] = jnp.zeros_like(cv)
            ce[...] = jnp.zeros_like(ce)

        if top:
            d = first[...] - second[...]
            dxo_v = d * (1.0 / D_MODEL)
            dxo_ref[...] = dxo_v
            part = jnp.sum(d * d, axis=-1, keepdims=True) * (1.0 / D_MODEL)
            lacc[...] += 0.5 * jnp.sum(part, axis=0, keepdims=True)
        else:
            dxo_v = first[...]
        yv = y_ref[...]
        r = lax.rsqrt(jnp.mean(yv * yv, axis=-1, keepdims=True) + EPS)
        yn = yv * r
        dg[...] += jnp.sum(dxo_v * yn, axis=0, keepdims=True)
        dyn = dxo_v * g_ref[layer:layer + 1, :]
        dy = (r * (dyn - yn * jnp.mean(dyn * yn, axis=-1, keepdims=True))).astype(BF16)
        dzs[...] = _nt(dy, w_ref[...])
        acc[...] += _tn(z_ref[...], dy)

        lane = lax.broadcasted_iota(jnp.int32, (1, 128), 1)
        lane2 = lax.broadcasted_iota(jnp.int32, (256, 128), 1)
        current = _band_is_current()
        for sb in reversed(range(per)):
            n = per * blk + sb
            rows = slice(BLOCK * sb, BLOCK * (sb + 1))

            uv = u_ref[rows, :]
            if sb == 0:
                halo = up_ref[BLOCK - WINDOW_HALO:, :] * (n > 0).astype(F32)
            else:
                halo = u_ref[BLOCK * sb - WINDOW_HALO:BLOCK * sb, :]
            ext = jnp.concatenate([halo, uv], axis=0)
            for g, w in enumerate(POOL_WINDOWS):
                cs = slice(BLOCK * g, BLOCK * (g + 1))
                inv = _inv_count(n, w)
                win = _window_sum(ext[:, cs], w, forward=False)[WINDOW_HALO:]
                pooled = win * inv - uv[:, cs]
                pw_g = pw_ref[g].astype(BF16)
                mixed = _nn(pooled.astype(BF16), pw_g)
                gate, dgate = _silu_parts(pg_ref[rows, cs])
                dzp = dzs[rows, cs]
                sc = sc_ref[layer:layer + 1, cs]
                dpm = dzp * gate
                dp_ref[rows, COL_PG + BLOCK * g:COL_PG + BLOCK * (g + 1)] = (dzp * (mixed * sc) * dgate).astype(BF16)
                pack_ref[ROW_SC + g:ROW_SC + g + 1, :] += jnp.sum(dpm * mixed, axis=0, keepdims=True)
                dmixed = (dpm * sc).astype(BF16)
                pack_ref[ROW_PW + BLOCK * g:ROW_PW + BLOCK * (g + 1), :] += _tn(pooled.astype(BF16), dmixed)
                dpooled = _nt(dmixed, pw_g)
                e = dpooled * inv
                lead = _window_sum(jnp.concatenate([e, ce[:WINDOW_HALO, cs]], axis=0), w, forward=True)[:BLOCK]
                dp_ref[rows, COL_U + BLOCK * g:COL_U + BLOCK * (g + 1)] = (lead - dpooled).astype(BF16)
                ce[:, cs] = e

            kx = _kv_ext(k_ref, n)
            vx = _kv_ext(v_ref, n)
            variant = jnp.minimum(n, 1) if sb == 0 else 1
            dsink_row = jnp.zeros((1, 128), F32)
            tks, tvs = [], []
            for kv in range(2):
                cs = slice(256 * kv, 256 * (kv + 1))
                k_rep = _replicate_head(kx, kv)
                v_rep = _replicate_head(vx, kv)
                q_st = _stack_heads(q_ref[rows, cs])
                gate, dgate = _silu_parts(ag_ref[rows, cs])
                dza = dzs[rows, D_POOL + 256 * kv:D_POOL + 256 * (kv + 1)]
                dp_ref[rows, COL_AG + 256 * kv:COL_AG + 256 * (kv + 1)] = (dza * a_ref[rows, cs] * dgate).astype(BF16)
                da_st = _stack_heads((dza * gate).astype(BF16))
                p, psink = _probs_keys_major(k_rep, q_st, bias_ref[variant, kv], _sink_row(sink_ref, layer, kv),
                                             current)
                dpt = _pack_band(_nt(v_rep, da_st), current)
                delta = jnp.sum(p * dpt, axis=0, keepdims=True)
                dst = _unpack_band((p * (dpt - delta) * SCALE).astype(BF16), current)
                sink_terms = psink * delta
                for g in range(GQA):
                    dsink = -jnp.sum(sink_terms[:, BLOCK * g:BLOCK * (g + 1)], axis=1, keepdims=True)
                    dsink_row = dsink_row + jnp.where(lane == kv * GQA + g, dsink, 0.0)
                dp_ref[rows, COL_Q + 256 * kv:COL_Q + 256 * (kv + 1)] = _unstack_heads(_tn(dst, k_rep)).astype(BF16)
                tks.append(_fold_heads(_nn(dst, q_st)))
                tvs.append(_fold_heads(_nn(_unpack_band(p.astype(BF16), current), da_st)))
            pack_ref[ROW_SINK:ROW_SINK + 1, :] += dsink_row
            dkx = jnp.where(lane2 < 64, tks[0], tks[1])
            dvx = jnp.where(lane2 < 64, tvs[0], tvs[1])
            dp_ref[rows, COL_K:COL_V] = (ck[...] + dkx[BLOCK:]).astype(BF16)
            dp_ref[rows, COL_V:COL_AG] = (cv[...] + dvx[BLOCK:]).astype(BF16)
            ck[...] = dkx[:BLOCK]
            cv[...] = dvx[:BLOCK]

        @pl.when(i == steps - 1)
        def _():
            dw_ref[...] = acc[...].astype(BF16)
            _rows_of(dg, pack_ref, ROW_NPOST)
            pack_ref[ROW_LOSS:ROW_LOSS + 1, :] = jnp.where(lane == 0, lacc[...], 0.0)

    row = lambda c: pl.BlockSpec((tm, c), lambda i: (last - i, 0))
    const = lambda shape: pl.BlockSpec(shape, lambda i: (0,) * len(shape))
    act = jax.ShapeDtypeStruct((SEQ, D_MODEL), F32)
    return pl.pallas_call(
        body, name=f"bwd_back{layer}", grid=(steps,),
        in_specs=[pl.BlockSpec(memory_space=pltpu.SMEM), row(D_MODEL), row(D_MODEL) if top else const((8, 128)),
                  row(D_MODEL), row(D_MODEL), const((DEPTH, D_MODEL)), _resident((D_MODEL, D_MODEL)),
                  row(D_POOL), pl.BlockSpec((BLOCK, D_POOL), lambda i: (jnp.maximum(per * (last - i) - 1, 0), 0)),
                  row(D_POOL), row(D_ATTN), _resident((SEQ, D_KV)), _resident((SEQ, D_KV)), row(D_ATTN), row(D_ATTN),
                  pl.BlockSpec((None, 4, BLOCK, BLOCK), lambda i: (layer, 0, 0, 0)), const((DEPTH, D_POOL)),
                  _resident((2, 2, BLOCK, GQA * BLOCK))],
        out_specs=([row(D_MODEL)] * (1 if top else 0)
                   + [row(D_IN), const((D_MODEL, D_MODEL)), const((PACK_ROWS, 128))]),
        out_shape=([act] * (1 if top else 0)
                   + [jax.ShapeDtypeStruct((SEQ, D_IN), BF16), jax.ShapeDtypeStruct((D_MODEL, D_MODEL), BF16),
                      jax.ShapeDtypeStruct((PACK_ROWS, 128), F32)]),
        scratch_shapes=[pltpu.VMEM((D_MODEL, D_MODEL), F32), pltpu.VMEM((1, D_MODEL), F32), pltpu.VMEM((1, 1), F32),
                        pltpu.VMEM((tm, D_MODEL), F32), pltpu.VMEM((BLOCK, D_KV), F32), pltpu.VMEM((BLOCK, D_KV), F32),
                        pltpu.VMEM((BLOCK, D_POOL), F32)],
        compiler_params=_compiler_params(("arbitrary",)),
    )(sinks, dxo_or_xf, target_or_token, y, z, norm_post, w_out, u, u, pg, q, k, v, ag, a, pool_w, pool_scale, bias)


def _bwd_in(layer, part, token, dproj, x, norm_pre, dxo=None, w_in_t=None):
    pair = part in ("dw_pair", "both_pair")
    want_dw, want_dx = part != "dx", part in ("both", "dx", "both_pair")
    tm = TOKEN_TILE
    steps = SEQ // tm
    cw = 256

    def body(*refs):
        refs = list(refs)
        dp_ref, x_ref, g_ref = refs[1:4]
        del refs[:4]
        if want_dx:
            dxo_ref, w_ref, dx_ref, dgo_ref = refs[:4]
            del refs[:4]
            dg = refs.pop()
        if pair:
            hs_ref, hm_ref, acc, mine_buf, theirs_buf, send_sem, recv_sem = refs
        elif want_dw:
            dw_ref, acc = refs
        i = pl.program_id(0)

        @pl.when(i == 0)
        def _():
            if pair:
                _handshake([(lax.axis_index("x"), lax.axis_index("y"), 1 - lax.axis_index("c"))])
            if want_dw:
                acc[...] = jnp.zeros_like(acc)
            if want_dx:
                dg[...] = jnp.zeros_like(dg)

        xv = x_ref[...]
        gv = g_ref[layer:layer + 1, :]
        r = lax.rsqrt(jnp.mean(xv * xv, axis=-1, keepdims=True) + EPS)
        xn = xv * r
        if want_dw:
            hb = (xn * gv).astype(BF16)
            for c in range(0, D_IN, cw):
                acc[c:c + cw, :] += _tn(dp_ref[:, c:c + cw], hb)
        def rows_for(q, core):
            return pl.ds(pl.multiple_of((2 * q + core) * IN_SHARD, 8), IN_SHARD)

        def swap(q):
            x, y, c = _mesh_pos()
            return pltpu.make_async_remote_copy(
                src_ref=mine_buf.at[q], dst_ref=theirs_buf.at[q], send_sem=send_sem.at[q], recv_sem=recv_sem.at[q],
                device_id=(x, y, 1 - c), device_id_type=MESH)

        if pair:
            @pl.when(i == steps - 1)
            def _():
                for q in range(4):
                    mine_buf[q] = acc[rows_for(q, 1 - lax.axis_index("c")), :].astype(BF16)
                    swap(q).start()

        if want_dx:
            dh = _nn(dp_ref[...], w_ref[...])
            dg[...] += jnp.sum(dh * xn, axis=0, keepdims=True)
            dhn = dh * gv
            dx_ref[...] = dxo_ref[...] + r * (dhn - xn * jnp.mean(dhn * xn, axis=-1, keepdims=True))

        @pl.when(i == steps - 1)
        def _():
            if pair:
                x, y, c = _mesh_pos()
                for q in range(4):
                    swap(q).wait()
                for j, q in enumerate([2 * (1 - x) + y, 2 * x + (1 - y), 2 * (1 - x) + (1 - y)]):
                    hs_ref[j] = (acc[rows_for(q, c), :] + theirs_buf[q].astype(F32)).astype(BF16)
                hm_ref[...] = acc[rows_for(2 * x + y, c), :] + theirs_buf[2 * x + y].astype(F32)
            elif want_dw:
                dw_ref[...] = acc[...].astype(BF16)
            if want_dx:
                _rows_of(dg, dgo_ref, 0)

    row = lambda c: pl.BlockSpec((tm, c), lambda i: (i, 0))
    const = lambda shape: pl.BlockSpec(shape, lambda i: (0,) * len(shape))
    in_specs = [const((8, 128)), row(D_IN), row(D_MODEL), const((DEPTH, D_MODEL))]
    operands = [token, dproj, x, norm_pre]
    out_specs, out_shape, scratch = [], [], []
    if want_dx:
        in_specs += [row(D_MODEL), _resident((D_IN, D_MODEL))]
        operands += [dxo, w_in_t]
        out_specs += [row(D_MODEL), const((8, 128))]
        out_shape += [jax.ShapeDtypeStruct((SEQ, D_MODEL), F32), jax.ShapeDtypeStruct((8, 128), F32)]
    if pair:
        out_specs += [const((3, IN_SHARD, D_MODEL)), const((IN_SHARD, D_MODEL))]
        out_shape += [jax.ShapeDtypeStruct((3, IN_SHARD, D_MODEL), BF16), jax.ShapeDtypeStruct((IN_SHARD, D_MODEL), F32)]
        scratch += [pltpu.VMEM((D_IN, D_MODEL), F32), pltpu.VMEM((4, IN_SHARD, D_MODEL), BF16),
                    pltpu.VMEM((4, IN_SHARD, D_MODEL), BF16), pltpu.SemaphoreType.DMA((4,)), pltpu.SemaphoreType.DMA((4,))]
    elif want_dw:
        out_specs.append(const((D_IN, D_MODEL)))
        out_shape.append(jax.ShapeDtypeStruct((D_IN, D_MODEL), BF16))
        scratch.append(pltpu.VMEM((D_IN, D_MODEL), F32))
    if want_dx:
        scratch.append(pltpu.VMEM((1, D_MODEL), F32))
    params = pltpu.CompilerParams(dimension_semantics=("arbitrary",), vmem_limit_bytes=VMEM_LIMIT,
                                  collective_id=COLLECTIVE_PAIR_SUM[layer] if pair else None)
    return pl.pallas_call(
        body, name=f"bwd_in_{part}{layer}", grid=(steps,),
        in_specs=in_specs, out_specs=out_specs, out_shape=out_shape, scratch_shapes=scratch,
        compiler_params=params,
    )(*operands)


def _mesh_pos():
    return lax.axis_index("x"), lax.axis_index("y"), lax.axis_index("c")


def _device_rows(ref, m, px, py, pc):
    return ref.at[pl.ds(pl.multiple_of((4 * px + 2 * py + pc) * m, 16 if m % 16 == 0 else 8), m), :]


def _allgather(srcs, out_dtype, name, later=()):
    na, nb = len(srcs), len(later)
    every = list(srcs) + list(later)
    shapes = [(a.shape[-2], a.shape[-1]) for a, _ in every]

    def body(*refs):
        xs, refs = refs[:na + nb], refs[na + nb:]
        outs, cast, land, refs = refs[:na], refs[na:na + nb], refs[na + nb:na + 2 * nb], refs[na + 2 * nb:]
        stage, raw, (send_sems, recv_sems, local_sems, load_sems) = refs[:na], refs[na:2 * na + nb], refs[2 * na + nb:]
        loads = [pltpu.make_async_copy(xs[i].at[every[i][1]], raw[i], load_sems.at[i]) for i in range(na + nb)]
        for cp in loads:
            cp.start()
        x, y, c = _mesh_pos()
        me, sibling = (x, y, c), (x, y, 1 - c)
        near = [(1 - x, y), (x, 1 - y)]
        far = (1 - x, 1 - y)
        relay_from, relay_to = (x ^ (1 - c), y ^ c), (x ^ c, y ^ (1 - c))
        _handshake([sibling] + [(*chip, c) for chip in near])
        k_from, k_to = 1 + c, 2 - c

        def slot(a, px, py, pc):
            return _device_rows(outs[a], shapes[a][0], px, py, pc)

        def copy(a, k, block, to, src=None):
            return pltpu.make_async_remote_copy(
                src_ref=slot(a, *block) if src is None else src, dst_ref=slot(a, *block),
                send_sem=send_sems.at[a, k], recv_sem=recv_sems.at[a, k], device_id=to, device_id_type=MESH)

        def cast_block(i):
            loads[i].wait()
            return raw[i][...].astype(out_dtype)

        for a in range(na):
            stage[a][...] = cast_block(a)
        mine = [pltpu.make_async_copy(stage[a], slot(a, *me), local_sems.at[a]) for a in range(na)]
        for cp in mine:
            cp.start()
        sent = []
        for a in range(na):
            sent.append(copy(a, 0, me, sibling, src=stage[a]))
            sent += [copy(a, 1 + j, me, (*chip, c), src=stage[a]) for j, chip in enumerate(near)]
        for cp in sent:
            cp.start()
        for b in range(nb):
            cast[b][...] = cast_block(na + b)
            cp = pltpu.make_async_copy(cast[b], _device_rows(land[b], shapes[na + b][0], *me), local_sems.at[na + b])
            cp.start()
            mine.append(cp)
        for a in range(na):
            copy(a, k_from, (*relay_from, c), me).wait_recv()
            sent += [copy(a, 3, (*relay_from, c), (*relay_to, c)), copy(a, 3 + k_from, (*relay_from, c), sibling)]
            sent[-2].start()
            sent[-1].start()
        for a in range(na):
            copy(a, k_to, (*relay_to, c), me).wait_recv()
            sent.append(copy(a, 3 + k_to, (*relay_to, c), sibling))
            sent[-1].start()
        for a in range(na):
            copy(a, 3, (*far, c), me).wait_recv()
            sent.append(copy(a, 6, (*far, c), sibling))
            sent[-1].start()
        for a in range(na):
            copy(a, 0, sibling, me).wait_recv()
            for j, chip in enumerate(near + [far]):
                copy(a, 4 + j, (*chip, 1 - c), me).wait_recv()
        for cp in sent:
            cp.wait_send()
        for cp in mine:
            cp.wait()

    vmem = pl.BlockSpec(memory_space=pltpu.VMEM)
    hbm = pl.BlockSpec(memory_space=pl.ANY)
    gathered = [jax.ShapeDtypeStruct((N_DEV * m, n), out_dtype) for m, n in shapes]
    out = pl.pallas_call(
        body, name=name,
        in_specs=[hbm] * (na + nb),
        out_specs=[hbm] * na + [vmem] * nb + [hbm] * nb,
        out_shape=gathered[:na] + [jax.ShapeDtypeStruct(s, out_dtype) for s in shapes[na:]] + gathered[na:],
        scratch_shapes=([pltpu.VMEM(s, out_dtype) for s in shapes[:na]]
                        + [pltpu.VMEM(s, a.dtype) for s, (a, _) in zip(shapes, every)]
                        + [pltpu.SemaphoreType.DMA((na, 7)), pltpu.SemaphoreType.DMA((na, 7)),
                           pltpu.SemaphoreType.DMA((na + nb,)), pltpu.SemaphoreType.DMA((na + nb,))]),
        compiler_params=pltpu.CompilerParams(vmem_limit_bytes=VMEM_LIMIT, collective_id=COLLECTIVE_GATHER_W0),
    )(*[a for a, _ in every])
    return out[:na], out[na:na + nb], out[na + nb:]


ALL_PEERS = tuple(range(1, N_DEV))


def _related(k, x, y, c):
    return x ^ ((k >> 2) & 1), y ^ ((k >> 1) & 1), c ^ (k & 1)


def _gather_wait(sems, block, land, relations, after, name):
    def body(src, land_ref, send_sem, recv_sem, after_ref, src_out, land_out):
        x, y, c = _mesh_pos()
        for k in relations:
            peer = _related(k, x, y, c)
            cp = pltpu.make_async_remote_copy(
                src_ref=src, dst_ref=_device_rows(land_ref, block.shape[0], *peer),
                send_sem=send_sem.at[k - 1], recv_sem=recv_sem.at[k - 1], device_id=peer, device_id_type=MESH)
            cp.wait_send()
            cp.wait_recv()

    out = pl.pallas_call(
        body, name=name,
        out_shape=(pltpu.HBM(block.shape, block.dtype), pltpu.HBM(land.shape, land.dtype)),
        in_specs=[_HBM, _HBM, _SEM, _SEM, pl.BlockSpec(memory_space=pl.ANY)],
        out_specs=[_HBM, _HBM],
        input_output_aliases={0: 0, 1: 1},
        compiler_params=pltpu.CompilerParams(has_side_effects=_EFFECT),
    )(block, land, sems[0], sems[1], after)
    return out[1]


(COLLECTIVE_GATHER_W0, COLLECTIVE_GATHER_W1, COLLECTIVE_EXCHANGE_1, COLLECTIVE_EXCHANGE_0A, COLLECTIVE_EXCHANGE_0B,
 COLLECTIVE_GATHER_SMALL) = range(1, 7)
COLLECTIVE_PAIR_SUM = (7, 8)


def _handshake(peers):
    barrier = pltpu.get_barrier_semaphore()
    for peer in peers:
        pl.semaphore_signal(barrier, inc=1, device_id=peer, device_id_type=MESH)
    pl.semaphore_wait(barrier, len(peers))


_HBM = pl.BlockSpec(memory_space=pltpu.HBM)
_SEM = pl.BlockSpec(memory_space=pltpu.SEMAPHORE)
_EFFECT = pltpu.SideEffectType.DATAFLOW_SIDE_EFFECTING


def _exchange_plan(direct):
    x, y, c = _mesh_pos()
    if not direct:
        return [(j, j, (qx, qy, c)) for j, (qx, qy) in enumerate([(1 - x, y), (x, 1 - y), (1 - x, 1 - y)])]
    plan = []
    for k in range(1, N_DEV):
        px, py, pc = x ^ ((k >> 2) & 1), y ^ ((k >> 1) & 1), c ^ (k & 1)
        plan.append((4 * px + 2 * py + pc, k - 1, (px, py, pc)))
    return plan


def _exchange_copies(directs):
    copies, base = [], 0
    for a, direct in enumerate(directs):
        plan = _exchange_plan(direct)
        copies += [(a, block, slot, peer, base + slot) for block, slot, peer in plan]
        base += len(plan)
    return copies, base


def _exchange_start(srcs, directs, collective_id, name):
    na = len(srcs)
    slots = [N_DEV - 1 if direct else 3 for direct in directs]

    def body(*refs):
        src, land = refs[:na], refs[na:2 * na]
        send_sem, recv_sem = refs[2 * na], refs[2 * na + 1]
        token = refs[-1]
        _handshake([peer for _, _, peer in _exchange_plan(any(directs))])
        for a, block, slot, peer, sem in _exchange_copies(directs)[0]:
            pltpu.make_async_remote_copy(
                src_ref=src[a].at[block], dst_ref=land[a].at[slot], send_sem=send_sem.at[sem],
                recv_sem=recv_sem.at[sem], device_id=peer, device_id_type=MESH).start()
        token[...] = jnp.zeros_like(token)

    zones = [jax.ShapeDtypeStruct((n,) + t.shape[1:], t.dtype) for n, t in zip(slots, srcs)]
    bufs = [pltpu.HBM(t.shape, t.dtype) for t in list(srcs) + zones]
    out = pl.pallas_call(
        body, name=name,
        out_shape=(pltpu.SemaphoreType.DMA((sum(slots),)), pltpu.SemaphoreType.DMA((sum(slots),)), *bufs,
                   jax.ShapeDtypeStruct((8, 128), F32)),
        in_specs=[_HBM] * (2 * na),
        out_specs=(_SEM, _SEM, *([_HBM] * (2 * na)), pl.BlockSpec(memory_space=pltpu.VMEM)),
        input_output_aliases={i: 2 + i for i in range(2 * na)},
        compiler_params=pltpu.CompilerParams(has_side_effects=_EFFECT, collective_id=collective_id),
    )(*[pltpu.with_memory_space_constraint(t, pltpu.HBM) for t in srcs],
      *[pltpu.with_memory_space_constraint(lax.empty(t.shape, t.dtype), pltpu.HBM) for t in zones])
    return out[0], out[1], out[2:2 + na], out[2 + na:2 + 2 * na], out[-1]


def _exchange_wait(send_sem, recv_sem, srcs, lands, directs, after, name):
    na = len(srcs)

    def body(*refs):
        src, land = refs[:na], refs[na:2 * na]
        send_sem_ref, recv_sem_ref = refs[2 * na], refs[2 * na + 1]
        for a, block, slot, peer, sem in _exchange_copies(directs)[0]:
            cp = pltpu.make_async_remote_copy(
                src_ref=src[a].at[block], dst_ref=land[a].at[slot], send_sem=send_sem_ref.at[sem],
                recv_sem=recv_sem_ref.at[sem], device_id=peer, device_id_type=MESH)
            cp.wait_send()
            cp.wait_recv()

    bufs = [pltpu.HBM(t.shape, t.dtype) for t in list(srcs) + list(lands)]
    out = pl.pallas_call(
        body, name=name,
        out_shape=tuple(bufs),
        in_specs=[_HBM] * (2 * na) + [_SEM, _SEM, pl.BlockSpec(memory_space=pl.ANY)],
        out_specs=[_HBM] * (2 * na),
        input_output_aliases={i: i for i in range(2 * na)},
        compiler_params=pltpu.CompilerParams(has_side_effects=_EFFECT),
    )(*srcs, *lands, send_sem, recv_sem, after)
    return out[:na], out[na:]


def _own_then_slots(mine_ref, lands_ref, rows=slice(None)):
    if len(mine_ref.shape) == 3:
        x, y, c = _mesh_pos()
        total = mine_ref[4 * x + 2 * y + c, rows, :].astype(F32)
    else:
        total = mine_ref[rows, :].astype(F32)
    for j in range(lands_ref.shape[0]):
        total = total + lands_ref[j, rows, :].astype(F32)
    return total


SMALL_ROWS = 2 * PACK_SLICE + 2 * 8


def _small_gather_start(mine, lands, dgpre, name):
    def body(*refs):
        hm, ld, dg = refs[:DEPTH], refs[DEPTH:2 * DEPTH], refs[2 * DEPTH:3 * DEPTH]
        send_sem, recv_sem, blk, land, token, own, slots, rows, built, local_sems = refs[3 * DEPTH:]
        x, y, c = _mesh_pos()
        loads = []
        for l in range(DEPTH):
            loads += [pltpu.make_async_copy(hm[l].at[4 * x + 2 * y + c], own.at[l], local_sems.at[3 * l]),
                      pltpu.make_async_copy(ld[l], slots.at[l], local_sems.at[3 * l + 1]),
                      pltpu.make_async_copy(dg[l], rows.at[l], local_sems.at[3 * l + 2])]
        for cp in loads:
            cp.start()
        _handshake([_related(k, x, y, c) for k in ALL_PEERS])
        for cp in loads:
            cp.wait()
        for l in range(DEPTH):
            total = own[l]
            for j in range(N_DEV - 1):
                total = total + slots[l, j]
            built[PACK_SLICE * l:PACK_SLICE * (l + 1), :] = total
            built[2 * PACK_SLICE + 8 * l:2 * PACK_SLICE + 8 * (l + 1), :] = rows[l]
        stores = [pltpu.make_async_copy(built, blk, local_sems.at[3 * DEPTH]),
                  pltpu.make_async_copy(built, _device_rows(land, SMALL_ROWS, x, y, c), local_sems.at[3 * DEPTH + 1])]
        for cp in stores:
            cp.start()
        for cp in stores:
            cp.wait()
        for k in ALL_PEERS:
            pltpu.make_async_remote_copy(
                src_ref=blk, dst_ref=_device_rows(land, SMALL_ROWS, x, y, c), send_sem=send_sem.at[k - 1],
                recv_sem=recv_sem.at[k - 1], device_id=_related(k, x, y, c), device_id_type=MESH).start()
        token[...] = jnp.zeros_like(token)

    hbm = pl.BlockSpec(memory_space=pl.ANY)
    out = pl.pallas_call(
        body, name=name,
        in_specs=[hbm] * (3 * DEPTH),
        out_specs=(_SEM, _SEM, _HBM, _HBM, pl.BlockSpec(memory_space=pltpu.VMEM)),
        out_shape=(pltpu.SemaphoreType.DMA((N_DEV - 1,)), pltpu.SemaphoreType.DMA((N_DEV - 1,)),
                   pltpu.HBM((SMALL_ROWS, 128), F32), pltpu.HBM((N_DEV * SMALL_ROWS, 128), F32),
                   jax.ShapeDtypeStruct((8, 128), F32)),
        scratch_shapes=[pltpu.VMEM((DEPTH, PACK_SLICE, 128), F32), pltpu.VMEM((DEPTH, N_DEV - 1, PACK_SLICE, 128), F32),
                        pltpu.VMEM((DEPTH, 8, 128), F32), pltpu.VMEM((SMALL_ROWS, 128), F32),
                        pltpu.SemaphoreType.DMA((3 * DEPTH + 2,))],
        compiler_params=pltpu.CompilerParams(has_side_effects=_EFFECT, collective_id=COLLECTIVE_GATHER_SMALL),
    )(*mine, *lands, *dgpre)
    return (out[0], out[1]), out[2], out[3], out[4]


def _adamw_math(w, g, m, v):
    m = ADAM_B1 * m + (1.0 - ADAM_B1) * g
    v = ADAM_B2 * v + (1.0 - ADAM_B2) * (g * g)
    m_hat = m / (1.0 - ADAM_B1 ** ADAM_STEP)
    v_hat = v / (1.0 - ADAM_B2 ** ADAM_STEP)
    delta = -ADAM_LR * (m_hat / (jnp.sqrt(v_hat) + ADAM_EPS) + ADAM_WD * w)
    return delta, m, v


def _adamw_layer(layer, parts, token, name):
    steps = {w.shape[1] // rows for _, _, w, _, _, _, rows in parts}
    assert len(steps) == 1, steps
    n = len(parts)

    def body(_, *refs):
        for p in range(n):
            hm_ref, ld_ref, w_ref, m_ref, v_ref = refs[5 * p:5 * (p + 1)]
            g_ref, d_ref, nm_ref, nv_ref = refs[len(refs) - 4 * (n - p):len(refs) - 4 * (n - p - 1)]
            g = _own_then_slots(hm_ref, ld_ref)
            g_ref[...] = g
            d, nm, nv = _adamw_math(w_ref[...], g, m_ref[...], v_ref[...])
            d_ref[...] = d
            nm_ref[...] = nm
            nv_ref[...] = nv

    in_specs, out_specs, out_shape, operands, carried, aliases = [pl.BlockSpec(memory_space=pl.ANY)], [], [], [], [], {}
    for p, (mine, lands, w, m, v, earlier, rows) in enumerate(parts):
        nn = w.shape[2]
        spec = pl.BlockSpec((None, rows, nn), lambda i: (layer, i, 0))
        in_specs += [pl.BlockSpec((rows, nn), lambda i: (i, 0)) if mine.ndim == 2
                     else pl.BlockSpec((N_DEV, rows, nn), lambda i: (0, i, 0)),
                     pl.BlockSpec((lands.shape[0], rows, nn), lambda i: (0, i, 0)), spec, spec, spec]
        out_specs += [spec] * 4
        out_shape += [jax.ShapeDtypeStruct(w.shape, F32)] * 4
        operands += [mine, lands, w, m, v]
        if earlier is not None:
            aliases.update({1 + 5 * n + len(carried) + t: 4 * p + t for t in range(4)})
            carried += list(earlier)
    out = pl.pallas_call(
        body, name=name, grid=(steps.pop(),),
        in_specs=in_specs + [pl.BlockSpec(memory_space=pl.ANY)] * len(carried),
        out_specs=out_specs, out_shape=out_shape, input_output_aliases=aliases,
        compiler_params=_compiler_params(("arbitrary",)),
    )(token, *operands, *carried)
    return [out[4 * p:4 * (p + 1)] for p in range(n)]


def _adamw_small(gathered, params):
    def body(all_ref, *refs):
        ins, outs, packs = refs[:15], refs[15:15 + 21], refs[15 + 21]
        loss_ref = outs[0]
        for dev in range(N_DEV):
            for l in range(DEPTH):
                packs[l, PACK_SLICE * dev:PACK_SLICE * (dev + 1), :] = (
                    all_ref[SMALL_ROWS * dev + PACK_SLICE * l:SMALL_ROWS * dev + PACK_SLICE * (l + 1), :])
        loss_ref[...] = packs[DEPTH - 1, ROW_LOSS:ROW_LOSS + 1, 0:1]

        def update(p, sel, g):
            w_ref, m_ref, v_ref = ins[p], ins[5 + p], ins[10 + p]
            d, nm, nv = _adamw_math(w_ref[sel], g, m_ref[sel], v_ref[sel])
            for t, val in enumerate((g, d, nm, nv)):
                outs[1 + 5 * t + p][sel] = val

        for l in range(DEPTH):
            gp = packs.at[l]
            row0 = 2 * PACK_SLICE + 8 * l
            dgpre = all_ref[row0:row0 + 8, :]
            for dev in range(1, N_DEV):
                dgpre = dgpre + all_ref[SMALL_ROWS * dev + row0:SMALL_ROWS * dev + row0 + 8, :]
            for grp in range(4):
                update(0, (l, grp), gp[ROW_PW + BLOCK * grp:ROW_PW + BLOCK * (grp + 1), :])
                update(1, (slice(l, l + 1), slice(128 * grp, 128 * (grp + 1))), gp[ROW_SC + grp:ROW_SC + grp + 1, :])
            update(2, (slice(l, l + 1), slice(None)), gp[ROW_SINK:ROW_SINK + 1, 0:N_HEADS])
            for r in range(D_MODEL // 128):
                sel = (slice(l, l + 1), slice(128 * r, 128 * (r + 1)))
                update(3, sel, dgpre[r:r + 1, :])
                update(4, sel, gp[ROW_NPOST + r:ROW_NPOST + r + 1, :])

    shapes = [jax.ShapeDtypeStruct(p.shape, F32) for p in params[:5]]
    return pl.pallas_call(
        body, name="adamw_small",
        out_shape=[jax.ShapeDtypeStruct((1, 1), F32)] + shapes * 4,
        scratch_shapes=[pltpu.VMEM((DEPTH, PACK_ROWS, 128), F32)],
        compiler_params=_compiler_params(),
    )(gathered, *params)


def kernel(x, w_in, pool_w, pool_scale, attn_sinks, w_out, norm_pre, norm_post, loss_target, m_w_in, m_pool_w, m_pool_scale, m_attn_sinks, m_w_out, m_norm_pre, m_norm_post, v_w_in, v_pool_w, v_pool_scale, v_attn_sinks, v_w_out, v_norm_pre, v_norm_post):
    x0 = x.reshape(SEQ, D_MODEL)
    target = loss_target.reshape(SEQ, D_MODEL)
    bias = jnp.asarray(_attn_bias())
    w_in_t, m_in_t, v_in_t = (jnp.swapaxes(t, 1, 2) for t in (w_in, m_w_in, v_w_in))

    (win0, wout0), later, lands = _allgather([(w_in_t, 0), (w_out, 0)], BF16, "gather_w0",
                                              later=[(w_in_t, 1), (w_out, 1)])
    win_full, wout_full = [win0, None], [wout0, None]
    token = x0

    saved = []
    xl = x0
    for layer in range(DEPTH):
        front = (layer, xl, norm_pre, win_full[layer], token, attn_sinks, pool_w, pool_scale, bias)
        if layer == 0:
            u, pg, q, k, v, ag, z, a, x_next, y, win_full[1], wout_full[1] = _fwd_front(
                *front, out=(norm_post, wout_full[layer]), gather=lands)
        else:
            u, pg, q, k, v, ag, z, a, x_next, y = _fwd_front(*front, out=(norm_post, wout_full[layer]))
        saved.append((xl, u, pg, q, k, v, ag, z, a, y))
        xl = x_next

    params_small = [pool_w, pool_scale, attn_sinks, norm_pre, norm_post,
                    m_pool_w, m_pool_scale, m_attn_sinks, m_norm_pre, m_norm_post,
                    v_pool_w, v_pool_scale, v_attn_sinks, v_norm_pre, v_norm_post]

    def start(srcs, directs, paired, collective_id, tag):
        send_sem, recv_sem, srcs, lands, started = _exchange_start(srcs, directs, collective_id, f"exchange_start{tag}")
        return (send_sem, recv_sem, srcs, lands, paired, directs), started

    def finish(handle, after, tag):
        send_sem, recv_sem, srcs, lands, paired, directs = handle
        srcs, lands = _exchange_wait(send_sem, recv_sem, srcs, lands, directs, after, f"exchange_wait{tag}")
        return [s if p is None else p for s, p in zip(srcs, paired)], lands

    def back(layer, top, first, second):
        xin, u, pg, q, k, v, ag, z, a, y = saved[layer]
        return _bwd_back(layer, top, first, second, y, z, norm_post, wout_full[layer], attn_sinks, u, pg, q, k, v,
                         ag, a, pool_w, pool_scale, bias)

    dgpre = [None] * DEPTH
    dx, dproj, gw_out, pack = back(1, True, xl, target)
    dx, dgpre[1], chip_sums, own_sum = _bwd_in(1, "both_pair", token, dproj, saved[1][0], norm_pre, dx, win_full[1])
    top, token = start([chip_sums, gw_out.reshape(N_DEV, OUT_SHARD, D_MODEL), pack.reshape(N_DEV, PACK_SLICE, 128)],
                       [False, True, True], [own_sum, None, None], COLLECTIVE_EXCHANGE_1, "1")

    dproj, gw_out, pack = back(0, False, dx, token)
    early, token = start([gw_out.reshape(N_DEV, OUT_SHARD, D_MODEL), pack.reshape(N_DEV, PACK_SLICE, 128)],
                         [True, True], [None, None], COLLECTIVE_EXCHANGE_0A, "0a")
    chip_sums, own_sum = _bwd_in(0, "dw_pair", token, dproj, saved[0][0], norm_pre)
    late, token = start([chip_sums], [False], [own_sum], COLLECTIVE_EXCHANGE_0B, "0b")
    dx, dgpre[0] = _bwd_in(0, "dx", token, dproj, saved[0][0], norm_pre, dx, win_full[0])

    own1, lands1 = finish(top, dx, "1")
    own0a, lands0a = finish(early, dx, "0a")
    sems, block, land, token = _small_gather_start([own0a[1], own1[2]], [lands0a[1], lands1[2]], dgpre,
                                                   "gather_small_start")
    big_in, big_out = _adamw_layer(1, [(own1[0], lands1[0], w_in_t, m_in_t, v_in_t, None, ADAM_ROWS_IN),
                                       (own1[1], lands1[1], w_out, m_w_out, v_w_out, None, ADAM_ROWS_OUT)],
                                   token, "adamw1")
    own0b, lands0b = finish(late, big_out[0], "0b")
    big_in, big_out = _adamw_layer(0, [(own0b[0], lands0b[0], w_in_t, m_in_t, v_in_t, big_in, ADAM_ROWS_IN),
                                       (own0a[0], lands0a[0], w_out, m_w_out, v_w_out, big_out, ADAM_ROWS_OUT)],
                                   token, "adamw0")
    gathered = _gather_wait(sems, block, land, ALL_PEERS, big_in[0], "gather_small_wait")
    small_out = _adamw_small(gathered, params_small)
    loss = small_out[0].reshape(())

    outs = [loss, dx.reshape(1, SEQ, D_MODEL)]
    for t in range(4):
        pw_, sc_, sk_, npre_, npost_ = small_out[1 + 5 * t:6 + 5 * t]
        outs += [jnp.swapaxes(big_in[t], 1, 2), pw_, sc_, sk_, big_out[t], npre_, npost_]
    return tuple(outs)
```

```python
import functools
import numpy as np
import jax
import jax.numpy as jnp
from jax import lax
from jax.experimental import pallas as pl
from jax.experimental.pallas import tpu as pltpu

F32 = jnp.float32
BF16 = jnp.bfloat16

N_DEV = 8
SEQ = 2048
D_MODEL = 1024
D_POOL = 512
D_ATTN = 512
D_KV = 128
D_IN = 2304
N_HEADS = 8
GQA = 4
HEAD_DIM = 64
BLOCK = 128
POOL_WINDOWS = (2, 4, 8, 16)
DEPTH = 2
EPS = 1e-6
NEG_INF = -1e30
SCALE = HEAD_DIM ** -0.5
IN_SHARD = D_IN // N_DEV
OUT_SHARD = D_MODEL // N_DEV

COL_U, COL_PG, COL_Q, COL_K, COL_V, COL_AG = 0, 512, 1024, 1536, 1664, 1792

ADAM_LR = 0.001
ADAM_B1 = 0.9
ADAM_B2 = 0.999
ADAM_EPS = 1e-08
ADAM_WD = 0.01
ADAM_STEP = 10

TOKEN_TILE = 512
ADAM_ROWS_IN, ADAM_ROWS_OUT = 144, 64
VMEM_LIMIT = 56 * 1024 * 1024
MESH = pl.DeviceIdType.MESH

ROW_PW, ROW_SC, ROW_SINK, ROW_NPOST, ROW_LOSS = 0, 512, 520, 536, 544
PACK_ROWS = 576
PACK_SLICE = PACK_ROWS // N_DEV


def _nn(a, b):
    return jnp.dot(a, b, preferred_element_type=F32)


def _nt(a, b):
    return lax.dot_general(a, b, (((1,), (1,)), ((), ())), preferred_element_type=F32)


def _tn(a, b):
    return lax.dot_general(a, b, (((0,), (0,)), ((), ())), preferred_element_type=F32)


def _silu_parts(g):
    s = jax.nn.sigmoid(g)
    return g * s, s * (1.0 + g * (1.0 - s))


def _resident(shape):
    return pl.BlockSpec(shape, lambda *_: (0,) * len(shape), pipeline_mode=pl.Buffered(1))


def _compiler_params(sem=None):
    if sem is None:
        return pltpu.CompilerParams(vmem_limit_bytes=VMEM_LIMIT)
    return pltpu.CompilerParams(dimension_semantics=sem, vmem_limit_bytes=VMEM_LIMIT)


def _attn_bias():
    t = np.arange(BLOCK)[None, :]
    j = np.arange(BLOCK)[:, None]
    current = j <= t
    dist = np.where(current, t - j, t + BLOCK - j).astype(np.float32)
    out = np.zeros((2, 2, BLOCK, GQA * BLOCK), np.float32)
    for variant in range(2):
        valid = current | (variant == 1)
        for kv in range(2):
            for g in range(GQA):
                slope = np.float32(2.0 ** (-(kv * GQA + g + 1)))
                out[variant, kv, :, g * BLOCK:(g + 1) * BLOCK] = np.where(valid, -slope * dist, np.float32(NEG_INF))
    return out


def _replicate_head(kx, kv):
    rolled = pltpu.roll(kx, 64, 1)
    lane = lax.broadcasted_iota(jnp.int32, kx.shape, 1)
    dup = jnp.where(lane < 64, kx, rolled) if kv == 0 else jnp.where(lane < 64, rolled, kx)
    return jnp.concatenate([dup, dup], axis=1).astype(BF16)


def _stack_heads(qv):
    lane = lax.broadcasted_iota(jnp.int32, qv.shape, 1)
    zero = jnp.zeros_like(qv)
    return jnp.concatenate([jnp.where((lane >= 64 * g) & (lane < 64 * g + 64), qv, zero) for g in range(GQA)], axis=0)


def _unstack_heads(xs):
    lane = lax.broadcasted_iota(jnp.int32, (BLOCK, 256), 1)
    return jnp.where(lane < 64, xs[0:128], jnp.where(lane < 128, xs[128:256], jnp.where(lane < 192, xs[256:384], xs[384:512])))


def _fold_heads(r):
    h = r[:, 0:128] + r[:, 128:256]
    return h + pltpu.roll(h, 64, 1)


def _sink_row(sink_ref, layer, kv):
    lane = lax.broadcasted_iota(jnp.int32, (1, GQA * BLOCK), 1)
    s4 = [sink_ref[layer, kv * GQA + g] for g in range(GQA)]
    return jnp.where(lane < 128, s4[0], jnp.where(lane < 256, s4[1], jnp.where(lane < 384, s4[2], s4[3])))


def _band_is_current():
    j = lax.broadcasted_iota(jnp.int32, (BLOCK, GQA * BLOCK), 0)
    t = lax.broadcasted_iota(jnp.int32, (BLOCK, GQA * BLOCK), 1) & (BLOCK - 1)
    return j <= t


def _pack_band(full, current):
    return jnp.where(current, full[BLOCK:], full[:BLOCK])


def _unpack_band(packed, current):
    zero = jnp.zeros_like(packed)
    return jnp.concatenate([jnp.where(current, zero, packed), jnp.where(current, packed, zero)], axis=0)


def _probs_keys_major(k_rep, q_st, bias, sink, current):
    st = _pack_band(_nt(k_rep, q_st), current) * SCALE + bias
    m = jnp.maximum(jnp.max(st, axis=0, keepdims=True), sink)
    p = jnp.exp(st - m)
    esink = jnp.exp(sink - m)
    rl = 1.0 / (jnp.sum(p, axis=0, keepdims=True) + esink)
    return p * rl, esink * rl


WINDOW_HALO = 16


def _window_sum(ext, w, forward):
    s = ext
    sh = 1
    while sh < w:
        s = s + pltpu.roll(s, (ext.shape[0] - sh) if forward else sh, 0)
        sh *= 2
    return s


def _inv_count(n, w):
    t = n * BLOCK + lax.broadcasted_iota(jnp.int32, (BLOCK, 1), 0) + 1
    return 1.0 / jnp.minimum(t.astype(F32), float(w))


def _kv_ext(ref, n):
    r0 = pl.multiple_of(jnp.maximum(n - 1, 0) * BLOCK, BLOCK)
    r1 = pl.multiple_of(n * BLOCK, BLOCK)
    return jnp.concatenate([ref[pl.ds(r0, BLOCK), :], ref[pl.ds(r1, BLOCK), :]], axis=0)


def _rows_of(vec_ref, pack_ref, row0):
    for r in range(D_MODEL // 128):
        pack_ref[row0 + r:row0 + r + 1, :] = vec_ref[:, 128 * r:128 * (r + 1)]


FRONT_TILE = 4 * BLOCK


def _fwd_front(layer, x, norm_pre, w_in_t, token, sinks, pool_w, pool_scale, bias, out=None, gather=None):
    tm = FRONT_TILE

    def body(sink_ref, x_ref, g_ref, w_ref, _, pw_ref, sc_ref, bias_ref, *refs):
        if out is not None:
            gpost_ref, wo_ref, *refs = refs
        if gather is not None:
            refs = refs[2:]
            zones, (send_sems, recv_sems), refs = refs[10:12], refs[-2:], refs[:-2]
        if out is not None:
            xn_ref, y_ref = refs[8:10]
        u_ref, pg_ref, q_ref, k_ref, v_ref, ag_ref, z_ref, a_ref = refs[:8]
        uprev, kprev, vprev = refs[-3:]
        i = pl.program_id(0)

        @pl.when(i == 0)
        def _():
            uprev[...] = jnp.zeros_like(uprev)
            kprev[...] = jnp.zeros_like(kprev)
            vprev[...] = jnp.zeros_like(vprev)

        if gather is not None:
            px, py, pc = _mesh_pos()
            me, sibling = (px, py, pc), (px, py, 1 - pc)
            near = [(1 - px, py), (px, 1 - py)]
            far = (1 - px, 1 - py)
            relay_from, relay_to = (px ^ (1 - pc), py ^ pc), (px ^ pc, py ^ (1 - pc))
            k_from, k_to = 1 + pc, 2 - pc

            def copy(a, sem, block, to):
                rows_of_block = _device_rows(zones[a], gather[a].shape[0] // N_DEV, *block)
                return pltpu.make_async_remote_copy(
                    src_ref=rows_of_block, dst_ref=rows_of_block, send_sem=send_sems.at[a, sem],
                    recv_sem=recv_sems.at[a, sem], device_id=to, device_id_type=MESH)

            def pass_on(a):
                copy(a, k_from, (*relay_from, pc), me).wait_recv()
                copy(a, 3, (*relay_from, pc), (*relay_to, pc)).start()
                copy(a, 3 + k_from, (*relay_from, pc), sibling).start()
                copy(a, k_to, (*relay_to, pc), me).wait_recv()
                copy(a, 3 + k_to, (*relay_to, pc), sibling).start()

            @pl.when(i == 0)
            def _():
                _handshake([sibling] + [(*chip, pc) for chip in near])
                for a in range(2):
                    copy(a, 0, me, sibling).start()
                    for j, chip in enumerate(near):
                        copy(a, 1 + j, me, (*chip, pc)).start()

            for a in range(2):
                pl.when(i == 2 + a)(functools.partial(pass_on, a))

        xv = x_ref[...]
        r = lax.rsqrt(jnp.mean(xv * xv, axis=-1, keepdims=True) + EPS)
        h = (xv * r * g_ref[layer:layer + 1, :]).astype(BF16)
        u_ref[...] = _nt(h, w_ref[COL_U:COL_PG, :])
        pg_ref[...] = _nt(h, w_ref[COL_PG:COL_Q, :])
        for sb in range(tm // BLOCK):
            n = (tm // BLOCK) * i + sb
            rows = slice(BLOCK * sb, BLOCK * (sb + 1))
            before = slice(BLOCK * (sb - 1), BLOCK * sb)
            uv = u_ref[rows, :]
            halo = (uprev[BLOCK - WINDOW_HALO:, :] if sb == 0
                    else u_ref[BLOCK * sb - WINDOW_HALO:BLOCK * sb, :])
            ext = jnp.concatenate([halo, uv], axis=0)
            for g, w in enumerate(POOL_WINDOWS):
                cs = slice(BLOCK * g, BLOCK * (g + 1))
                win = _window_sum(ext[:, cs], w, forward=False)[WINDOW_HALO:]
                pooled = win * _inv_count(n, w) - uv[:, cs]
                mixed = _nn(pooled.astype(BF16), pw_ref[g].astype(BF16))
                gate, _ = _silu_parts(pg_ref[rows, cs])
                z_ref[rows, cs] = (mixed * sc_ref[layer:layer + 1, cs] * gate).astype(BF16)

        q_ref[...] = _nt(h, w_ref[COL_Q:COL_K, :]).astype(BF16)
        k_ref[...] = _nt(h, w_ref[COL_K:COL_V, :])
        v_ref[...] = _nt(h, w_ref[COL_V:COL_AG, :])
        ag_ref[...] = _nt(h, w_ref[COL_AG:D_IN, :])

        current = _band_is_current()
        for sb in range(tm // BLOCK):
            n = (tm // BLOCK) * i + sb
            rows = slice(BLOCK * sb, BLOCK * (sb + 1))
            before = slice(BLOCK * (sb - 1), BLOCK * sb)
            kx = jnp.concatenate([kprev[...] if sb == 0 else k_ref[before, :], k_ref[rows, :]], axis=0)
            vx = jnp.concatenate([vprev[...] if sb == 0 else v_ref[before, :], v_ref[rows, :]], axis=0)
            variant = jnp.minimum(n, 1) if sb == 0 else 1
            for kv in range(2):
                cs = slice(256 * kv, 256 * (kv + 1))
                p, _ = _probs_keys_major(_replicate_head(kx, kv), _stack_heads(q_ref[rows, cs]),
                                         bias_ref[variant, kv], _sink_row(sink_ref, layer, kv), current)
                o = _unstack_heads(_tn(_unpack_band(p.astype(BF16), current), _replicate_head(vx, kv)))
                a_ref[rows, cs] = o
                gate, _ = _silu_parts(ag_ref[rows, cs])
                z_ref[rows, D_POOL + 256 * kv:D_POOL + 256 * (kv + 1)] = (o * gate).astype(BF16)

        tail = slice(tm - BLOCK, tm)
        uprev[...] = u_ref[tail, :]
        kprev[...] = k_ref[tail, :]
        vprev[...] = v_ref[tail, :]

        if out is not None:
            y = _nn(z_ref[...], wo_ref[...])
            y_ref[...] = y
            r = lax.rsqrt(jnp.mean(y * y, axis=-1, keepdims=True) + EPS)
            xn_ref[...] = x_ref[...] + y * r * gpost_ref[layer:layer + 1, :]

        if gather is not None:
            @pl.when(i == SEQ // tm - 1)
            def _():
                for a in range(2):
                    copy(a, 3, (*far, pc), me).wait_recv()
                    copy(a, 6, (*far, pc), sibling).start()
                for a in range(2):
                    copy(a, 0, sibling, me).wait_recv()
                    for j, chip in enumerate(near + [far]):
                        copy(a, 4 + j, (*chip, 1 - pc), me).wait_recv()
                for a in range(2):
                    copy(a, 0, me, sibling).wait_send()
                    for j, chip in enumerate(near):
                        copy(a, 1 + j, me, (*chip, pc)).wait_send()
                    copy(a, 3, (*relay_from, pc), (*relay_to, pc)).wait_send()
                    copy(a, 3 + k_from, (*relay_from, pc), sibling).wait_send()
                    copy(a, 3 + k_to, (*relay_to, pc), sibling).wait_send()
                    copy(a, 6, (*far, pc), sibling).wait_send()

    row = lambda c: pl.BlockSpec((tm, c), lambda i: (i, 0))
    const = lambda shape: pl.BlockSpec(shape, lambda i: (0,) * len(shape))
    act = jax.ShapeDtypeStruct((SEQ, D_MODEL), F32)
    fused = out is not None
    hbm = pl.BlockSpec(memory_space=pl.ANY)
    zones, params = [], _compiler_params(("arbitrary",))
    if gather is not None:
        assert fused and SEQ // tm == 4, "the gather's copies are spread over four grid steps"
        zones = list(gather)
        params = pltpu.CompilerParams(dimension_semantics=("arbitrary",), vmem_limit_bytes=VMEM_LIMIT,
                                      collective_id=COLLECTIVE_GATHER_W1)
    return pl.pallas_call(
        body, name=f"fwd_front{layer}", grid=(SEQ // tm,),
        in_specs=[pl.BlockSpec(memory_space=pltpu.SMEM), row(D_MODEL), const((DEPTH, D_MODEL)),
                  _resident((D_IN, D_MODEL)), const((8, 128)),
                  pl.BlockSpec((None, 4, BLOCK, BLOCK), lambda i: (layer, 0, 0, 0)), const((DEPTH, D_POOL)),
                  _resident((2, 2, BLOCK, GQA * BLOCK))]
                 + ([const((DEPTH, D_MODEL)), _resident((D_MODEL, D_MODEL))] if fused else []) + [hbm] * len(zones),
        out_specs=[row(D_POOL), row(D_POOL), row(D_ATTN), row(D_KV), row(D_KV), row(D_ATTN), row(D_MODEL),
                   row(D_ATTN)] + ([row(D_MODEL)] * 2 if fused else []) + [hbm] * len(zones),
        out_shape=[jax.ShapeDtypeStruct((SEQ, D_POOL), F32), jax.ShapeDtypeStruct((SEQ, D_POOL), F32),
                   jax.ShapeDtypeStruct((SEQ, D_ATTN), BF16), jax.ShapeDtypeStruct((SEQ, D_KV), F32),
                   jax.ShapeDtypeStruct((SEQ, D_KV), F32), jax.ShapeDtypeStruct((SEQ, D_ATTN), F32),
                   jax.ShapeDtypeStruct((SEQ, D_MODEL), BF16), jax.ShapeDtypeStruct((SEQ, D_ATTN), F32)]
                  + ([act] * 2 if fused else []) + [jax.ShapeDtypeStruct(t.shape, t.dtype) for t in zones],
        scratch_shapes=[pltpu.VMEM((BLOCK, D_POOL), F32), pltpu.VMEM((BLOCK, D_KV), F32),
                        pltpu.VMEM((BLOCK, D_KV), F32)]
                       + ([pltpu.SemaphoreType.DMA((2, 7)), pltpu.SemaphoreType.DMA((2, 7))] if zones else []),
        input_output_aliases={10 + a: 10 + a for a in range(len(zones))},
        compiler_params=params,
    )(sinks, x, norm_pre, w_in_t, token, pool_w, pool_scale, bias, *(out if fused else ()), *zones)


BACK_TILE = 2 * BLOCK


def _bwd_back(layer, top, dxo_or_xf, target_or_token, y, z, norm_post, w_out, sinks, u, pg, q, k, v, ag, a,
              pool_w, pool_scale, bias):
    tm = BACK_TILE
    steps = SEQ // tm
    last = steps - 1
    per = tm // BLOCK

    def body(*refs):
        refs = list(refs)
        sink_ref, first, second = refs[:3]
        (y_ref, z_ref, g_ref, w_ref, u_ref, up_ref, pg_ref, q_ref, k_ref, v_ref, ag_ref, a_ref, pw_ref, sc_ref,
         bias_ref) = refs[3:18]
        del refs[:18]
        dxo_ref = refs.pop(0) if top else None
        dp_ref, dw_ref, pack_ref, acc, dg, lacc, dzs, ck, cv, ce = refs
        i = pl.program_id(0)
        blk = last - i

        @pl.when(i == 0)
        def _():
            acc[...] = jnp.zeros_like(acc)
            dg[...] = jnp.zeros_like(dg)
            lacc[...] = jnp.zeros_like(lacc)
            pack_ref[...] = jnp.zeros_like(pack_ref)
            ck[...] = jnp.zeros_like(ck)
            cv[...] = jnp.zeros_like(cv)
            ce[...] = jnp.zeros_like(ce)

        if top:
            d = first[...] - second[...]
            dxo_v = d * (1.0 / D_MODEL)
            dxo_ref[...] = dxo_v
            part = jnp.sum(d * d, axis=-1, keepdims=True) * (1.0 / D_MODEL)
            lacc[...] += 0.5 * jnp.sum(part, axis=0, keepdims=True)
        else:
            dxo_v = first[...]
        yv = y_ref[...]
        r = lax.rsqrt(jnp.mean(yv * yv, axis=-1, keepdims=True) + EPS)
        yn = yv * r
        dg[...] += jnp.sum(dxo_v * yn, axis=0, keepdims=True)
        dyn = dxo_v * g_ref[layer:layer + 1, :]
        dy = (r * (dyn - yn * jnp.mean(dyn * yn, axis=-1, keepdims=True))).astype(BF16)
        dzs[...] = _nt(dy, w_ref[...])
        acc[...] += _tn(z_ref[...], dy)

        lane = lax.broadcasted_iota(jnp.int32, (1, 128), 1)
        lane2 = lax.broadcasted_iota(jnp.int32, (256, 128), 1)
        current = _band_is_current()
        for sb in reversed(range(per)):
            n = per * blk + sb
            rows = slice(BLOCK * sb, BLOCK * (sb + 1))

            uv = u_ref[rows, :]
            if sb == 0:
                halo = up_ref[BLOCK - WINDOW_HALO:, :] * (n > 0).astype(F32)
            else:
                halo = u_ref[BLOCK * sb - WINDOW_HALO:BLOCK * sb, :]
            ext = jnp.concatenate([halo, uv], axis=0)
            for g, w in enumerate(POOL_WINDOWS):
                cs = slice(BLOCK * g, BLOCK * (g + 1))
                inv = _inv_count(n, w)
                win = _window_sum(ext[:, cs], w, forward=False)[WINDOW_HALO:]
                pooled = win * inv - uv[:, cs]
                pw_g = pw_ref[g].astype(BF16)
                mixed = _nn(pooled.astype(BF16), pw_g)
                gate, dgate = _silu_parts(pg_ref[rows, cs])
                dzp = dzs[rows, cs]
                sc = sc_ref[layer:layer + 1, cs]
                dpm = dzp * gate
                dp_ref[rows, COL_PG + BLOCK * g:COL_PG + BLOCK * (g + 1)] = (dzp * (mixed * sc) * dgate).astype(BF16)
                pack_ref[ROW_SC + g:ROW_SC + g + 1, :] += jnp.sum(dpm * mixed, axis=0, keepdims=True)
                dmixed = (dpm * sc).astype(BF16)
                pack_ref[ROW_PW + BLOCK * g:ROW_PW + BLOCK * (g + 1), :] += _tn(pooled.astype(BF16), dmixed)
                dpooled = _nt(dmixed, pw_g)
                e = dpooled * inv
                lead = _window_sum(jnp.concatenate([e, ce[:WINDOW_HALO, cs]], axis=0), w, forward=True)[:BLOCK]
                dp_ref[rows, COL_U + BLOCK * g:COL_U + BLOCK * (g + 1)] = (lead - dpooled).astype(BF16)
                ce[:, cs] = e

            kx = _kv_ext(k_ref, n)
            vx = _kv_ext(v_ref, n)
            variant = jnp.minimum(n, 1) if sb == 0 else 1
            dsink_row = jnp.zeros((1, 128), F32)
            tks, tvs = [], []
            for kv in range(2):
                cs = slice(256 * kv, 256 * (kv + 1))
                k_rep = _replicate_head(kx, kv)
                v_rep = _replicate_head(vx, kv)
                q_st = _stack_heads(q_ref[rows, cs])
                gate, dgate = _silu_parts(ag_ref[rows, cs])
                dza = dzs[rows, D_POOL + 256 * kv:D_POOL + 256 * (kv + 1)]
                dp_ref[rows, COL_AG + 256 * kv:COL_AG + 256 * (kv + 1)] = (dza * a_ref[rows, cs] * dgate).astype(BF16)
                da_st = _stack_heads((dza * gate).astype(BF16))
                p, psink = _probs_keys_major(k_rep, q_st, bias_ref[variant, kv], _sink_row(sink_ref, layer, kv),
                                             current)
                dpt = _pack_band(_nt(v_rep, da_st), current)
                delta = jnp.sum(p * dpt, axis=0, keepdims=True)
                dst = _unpack_band((p * (dpt - delta) * SCALE).astype(BF16), current)
                sink_terms = psink * delta
                for g in range(GQA):
                    dsink = -jnp.sum(sink_terms[:, BLOCK * g:BLOCK * (g + 1)], axis=1, keepdims=True)
                    dsink_row = dsink_row + jnp.where(lane == kv * GQA + g, dsink, 0.0)
                dp_ref[rows, COL_Q + 256 * kv:COL_Q + 256 * (kv + 1)] = _unstack_heads(_tn(dst, k_rep)).astype(BF16)
                tks.append(_fold_heads(_nn(dst, q_st)))
                tvs.append(_fold_heads(_nn(_unpack_band(p.astype(BF16), current), da_st)))
            pack_ref[ROW_SINK:ROW_SINK + 1, :] += dsink_row
            dkx = jnp.where(lane2 < 64, tks[0], tks[1])
            dvx = jnp.where(lane2 < 64, tvs[0], tvs[1])
            dp_ref[rows, COL_K:COL_V] = (ck[...] + dkx[BLOCK:]).astype(BF16)
            dp_ref[rows, COL_V:COL_AG] = (cv[...] + dvx[BLOCK:]).astype(BF16)
            ck[...] = dkx[:BLOCK]
            cv[...] = dvx[:BLOCK]

        @pl.when(i == steps - 1)
        def _():
            dw_ref[...] = acc[...].astype(BF16)
            _rows_of(dg, pack_ref, ROW_NPOST)
            pack_ref[ROW_LOSS:ROW_LOSS + 1, :] = jnp.where(lane == 0, lacc[...], 0.0)

    row = lambda c: pl.BlockSpec((tm, c), lambda i: (last - i, 0))
    const = lambda shape: pl.BlockSpec(shape, lambda i: (0,) * len(shape))
    act = jax.ShapeDtypeStruct((SEQ, D_MODEL), F32)
    return pl.pallas_call(
        body, name=f"bwd_back{layer}", grid=(steps,),
        in_specs=[pl.BlockSpec(memory_space=pltpu.SMEM), row(D_MODEL), row(D_MODEL) if top else const((8, 128)),
                  row(D_MODEL), row(D_MODEL), const((DEPTH, D_MODEL)), _resident((D_MODEL, D_MODEL)),
                  row(D_POOL), pl.BlockSpec((BLOCK, D_POOL), lambda i: (jnp.maximum(per * (last - i) - 1, 0), 0)),
                  row(D_POOL), row(D_ATTN), _resident((SEQ, D_KV)), _resident((SEQ, D_KV)), row(D_ATTN), row(D_ATTN),
                  pl.BlockSpec((None, 4, BLOCK, BLOCK), lambda i: (layer, 0, 0, 0)), const((DEPTH, D_POOL)),
                  _resident((2, 2, BLOCK, GQA * BLOCK))],
        out_specs=([row(D_MODEL)] * (1 if top else 0)
                   + [row(D_IN), const((D_MODEL, D_MODEL)), const((PACK_ROWS, 128))]),
        out_shape=([act] * (1 if top else 0)
                   + [jax.ShapeDtypeStruct((SEQ, D_IN), BF16), jax.ShapeDtypeStruct((D_MODEL, D_MODEL), BF16),
                      jax.ShapeDtypeStruct((PACK_ROWS, 128), F32)]),
        scratch_shapes=[pltpu.VMEM((D_MODEL, D_MODEL), F32), pltpu.VMEM((1, D_MODEL), F32), pltpu.VMEM((1, 1), F32),
                        pltpu.VMEM((tm, D_MODEL), F32), pltpu.VMEM((BLOCK, D_KV), F32), pltpu.VMEM((BLOCK, D_KV), F32),
                        pltpu.VMEM((BLOCK, D_POOL), F32)],
        compiler_params=_compiler_params(("arbitrary",)),
    )(sinks, dxo_or_xf, target_or_token, y, z, norm_post, w_out, u, u, pg, q, k, v, ag, a, pool_w, pool_scale, bias)


def _bwd_in(layer, part, token, dproj, x, norm_pre, dxo=None, w_in_t=None):
    pair = part in ("dw_pair", "both_pair")
    want_dw, want_dx = part != "dx", part in ("both", "dx", "both_pair")
    tm = TOKEN_TILE
    steps = SEQ // tm
    cw = 256

    def body(*refs):
        refs = list(refs)
        dp_ref, x_ref, g_ref = refs[1:4]
        del refs[:4]
        if want_dx:
            dxo_ref, w_ref, dx_ref, dgo_ref = refs[:4]
            del refs[:4]
            dg = refs.pop()
        if pair:
            hs_ref, hm_ref, acc, mine_buf, theirs_buf, send_sem, recv_sem = refs
        elif want_dw:
            dw_ref, acc = refs
        i = pl.program_id(0)

        @pl.when(i == 0)
        def _():
            if pair:
                _handshake([(lax.axis_index("x"), lax.axis_index("y"), 1 - lax.axis_index("c"))])
            if want_dw:
                acc[...] = jnp.zeros_like(acc)
            if want_dx:
                dg[...] = jnp.zeros_like(dg)

        xv = x_ref[...]
        gv = g_ref[layer:layer + 1, :]
        r = lax.rsqrt(jnp.mean(xv * xv, axis=-1, keepdims=True) + EPS)
        xn = xv * r
        if want_dw:
            hb = (xn * gv).astype(BF16)
            for c in range(0, D_IN, cw):
                acc[c:c + cw, :] += _tn(dp_ref[:, c:c + cw], hb)
        def rows_for(q, core):
            return pl.ds(pl.multiple_of((2 * q + core) * IN_SHARD, 8), IN_SHARD)

        def swap(q):
            x, y, c = _mesh_pos()
            return pltpu.make_async_remote_copy(
                src_ref=mine_buf.at[q], dst_ref=theirs_buf.at[q], send_sem=send_sem.at[q], recv_sem=recv_sem.at[q],
                device_id=(x, y, 1 - c), device_id_type=MESH)

        if pair:
            @pl.when(i == steps - 1)
            def _():
                for q in range(4):
                    mine_buf[q] = acc[rows_for(q, 1 - lax.axis_index("c")), :].astype(BF16)
                    swap(q).start()

        if want_dx:
            dh = _nn(dp_ref[...], w_ref[...])
            dg[...] += jnp.sum(dh * xn, axis=0, keepdims=True)
            dhn = dh * gv
            dx_ref[...] = dxo_ref[...] + r * (dhn - xn * jnp.mean(dhn * xn, axis=-1, keepdims=True))

        @pl.when(i == steps - 1)
        def _():
            if pair:
                x, y, c = _mesh_pos()
                for q in range(4):
                    swap(q).wait()
                for j, q in enumerate([2 * (1 - x) + y, 2 * x + (1 - y), 2 * (1 - x) + (1 - y)]):
                    hs_ref[j] = (acc[rows_for(q, c), :] + theirs_buf[q].astype(F32)).astype(BF16)
                hm_ref[...] = acc[rows_for(2 * x + y, c), :] + theirs_buf[2 * x + y].astype(F32)
            elif want_dw:
                dw_ref[...] = acc[...].astype(BF16)
            if want_dx:
                _rows_of(dg, dgo_ref, 0)

    row = lambda c: pl.BlockSpec((tm, c), lambda i: (i, 0))
    const = lambda shape: pl.BlockSpec(shape, lambda i: (0,) * len(shape))
    in_specs = [const((8, 128)), row(D_IN), row(D_MODEL), const((DEPTH, D_MODEL))]
    operands = [token, dproj, x, norm_pre]
    out_specs, out_shape, scratch = [], [], []
    if want_dx:
        in_specs += [row(D_MODEL), _resident((D_IN, D_MODEL))]
        operands += [dxo, w_in_t]
        out_specs += [row(D_MODEL), const((8, 128))]
        out_shape += [jax.ShapeDtypeStruct((SEQ, D_MODEL), F32), jax.ShapeDtypeStruct((8, 128), F32)]
    if pair:
        out_specs += [const((3, IN_SHARD, D_MODEL)), const((IN_SHARD, D_MODEL))]
        out_shape += [jax.ShapeDtypeStruct((3, IN_SHARD, D_MODEL), BF16), jax.ShapeDtypeStruct((IN_SHARD, D_MODEL), F32)]
        scratch += [pltpu.VMEM((D_IN, D_MODEL), F32), pltpu.VMEM((4, IN_SHARD, D_MODEL), BF16),
                    pltpu.VMEM((4, IN_SHARD, D_MODEL), BF16), pltpu.SemaphoreType.DMA((4,)), pltpu.SemaphoreType.DMA((4,))]
    elif want_dw:
        out_specs.append(const((D_IN, D_MODEL)))
        out_shape.append(jax.ShapeDtypeStruct((D_IN, D_MODEL), BF16))
        scratch.append(pltpu.VMEM((D_IN, D_MODEL), F32))
    if want_dx:
        scratch.append(pltpu.VMEM((1, D_MODEL), F32))
    params = pltpu.CompilerParams(dimension_semantics=("arbitrary",), vmem_limit_bytes=VMEM_LIMIT,
                                  collective_id=COLLECTIVE_PAIR_SUM[layer] if pair else None)
    return pl.pallas_call(
        body, name=f"bwd_in_{part}{layer}", grid=(steps,),
        in_specs=in_specs, out_specs=out_specs, out_shape=out_shape, scratch_shapes=scratch,
        compiler_params=params,
    )(*operands)


def _mesh_pos():
    return lax.axis_index("x"), lax.axis_index("y"), lax.axis_index("c")


def _device_rows(ref, m, px, py, pc):
    return ref.at[pl.ds(pl.multiple_of((4 * px + 2 * py + pc) * m, 16 if m % 16 == 0 else 8), m), :]


def _allgather(srcs, out_dtype, name, later=()):
    na, nb = len(srcs), len(later)
    every = list(srcs) + list(later)
    shapes = [(a.shape[-2], a.shape[-1]) for a, _ in every]

    def body(*refs):
        xs, refs = refs[:na + nb], refs[na + nb:]
        outs, land, refs = refs[:na], refs[na:na + nb], refs[na + nb:]
        stage, cast, raw = refs[:na], refs[na:na + nb], refs[na + nb:2 * (na + nb)]
        send_sems, recv_sems, local_sems, load_sems = refs[2 * (na + nb):]
        loads = [pltpu.make_async_copy(xs[i].at[every[i][1]], raw[i], load_sems.at[i]) for i in range(na + nb)]
        for cp in loads:
            cp.start()
        x, y, c = _mesh_pos()
        me, sibling = (x, y, c), (x, y, 1 - c)
        near = [(1 - x, y), (x, 1 - y)]
        far = (1 - x, 1 - y)
        relay_from, relay_to = (x ^ (1 - c), y ^ c), (x ^ c, y ^ (1 - c))
        _handshake([sibling] + [(*chip, c) for chip in near])
        k_from, k_to = 1 + c, 2 - c

        def slot(a, px, py, pc):
            return _device_rows(outs[a], shapes[a][0], px, py, pc)

        def copy(a, k, block, to, src=None):
            return pltpu.make_async_remote_copy(
                src_ref=slot(a, *block) if src is None else src, dst_ref=slot(a, *block),
                send_sem=send_sems.at[a, k], recv_sem=recv_sems.at[a, k], device_id=to, device_id_type=MESH)

        def cast_block(i):
            loads[i].wait()
            return raw[i][...].astype(out_dtype)

        for a in range(na):
            stage[a][...] = cast_block(a)
        mine = [pltpu.make_async_copy(stage[a], slot(a, *me), local_sems.at[a]) for a in range(na)]
        for cp in mine:
            cp.start()
        sent = []
        for a in range(na):
            sent.append(copy(a, 0, me, sibling, src=stage[a]))
            sent += [copy(a, 1 + j, me, (*chip, c), src=stage[a]) for j, chip in enumerate(near)]
        for cp in sent:
            cp.start()
        for b in range(nb):
            cast[b][...] = cast_block(na + b)
            cp = pltpu.make_async_copy(cast[b], _device_rows(land[b], shapes[na + b][0], *me), local_sems.at[na + b])
            cp.start()
            mine.append(cp)
        for a in range(na):
            copy(a, k_from, (*relay_from, c), me).wait_recv()
            sent += [copy(a, 3, (*relay_from, c), (*relay_to, c)), copy(a, 3 + k_from, (*relay_from, c), sibling)]
            sent[-2].start()
            sent[-1].start()
        for a in range(na):
            copy(a, k_to, (*relay_to, c), me).wait_recv()
            sent.append(copy(a, 3 + k_to, (*relay_to, c), sibling))
            sent[-1].start()
        for a in range(na):
            copy(a, 3, (*far, c), me).wait_recv()
            sent.append(copy(a, 6, (*far, c), sibling))
            sent[-1].start()
        for a in range(na):
            copy(a, 0, sibling, me).wait_recv()
            for j, chip in enumerate(near + [far]):
                copy(a, 4 + j, (*chip, 1 - c), me).wait_recv()
        for cp in sent:
            cp.wait_send()
        for cp in mine:
            cp.wait()

    hbm = pl.BlockSpec(memory_space=pl.ANY)
    gathered = [jax.ShapeDtypeStruct((N_DEV * m, n), out_dtype) for m, n in shapes]
    out = pl.pallas_call(
        body, name=name,
        in_specs=[hbm] * (na + nb),
        out_specs=[hbm] * (na + nb),
        out_shape=gathered,
        scratch_shapes=([pltpu.VMEM(s, out_dtype) for s in shapes]
                        + [pltpu.VMEM(s, a.dtype) for s, (a, _) in zip(shapes, every)]
                        + [pltpu.SemaphoreType.DMA((na, 7)), pltpu.SemaphoreType.DMA((na, 7)),
                           pltpu.SemaphoreType.DMA((na + nb,)), pltpu.SemaphoreType.DMA((na + nb,))]),
        compiler_params=pltpu.CompilerParams(vmem_limit_bytes=VMEM_LIMIT, collective_id=COLLECTIVE_GATHER_W0),
    )(*[a for a, _ in every])
    return out[:na], out[na:]


ALL_PEERS = tuple(range(1, N_DEV))


def _related(k, x, y, c):
    return x ^ ((k >> 2) & 1), y ^ ((k >> 1) & 1), c ^ (k & 1)


def _gather_wait(sems, block, land, relations, after, name):
    def body(src, land_ref, send_sem, recv_sem, after_ref, src_out, land_out):
        x, y, c = _mesh_pos()
        for k in relations:
            peer = _related(k, x, y, c)
            cp = pltpu.make_async_remote_copy(
                src_ref=src, dst_ref=_device_rows(land_ref, block.shape[0], *peer),
                send_sem=send_sem.at[k - 1], recv_sem=recv_sem.at[k - 1], device_id=peer, device_id_type=MESH)
            cp.wait_send()
            cp.wait_recv()

    out = pl.pallas_call(
        body, name=name,
        out_shape=(pltpu.HBM(block.shape, block.dtype), pltpu.HBM(land.shape, land.dtype)),
        in_specs=[_HBM, _HBM, _SEM, _SEM, pl.BlockSpec(memory_space=pl.ANY)],
        out_specs=[_HBM, _HBM],
        input_output_aliases={0: 0, 1: 1},
        compiler_params=pltpu.CompilerParams(has_side_effects=_EFFECT),
    )(block, land, sems[0], sems[1], after)
    return out[1]


(COLLECTIVE_GATHER_W0, COLLECTIVE_GATHER_W1, COLLECTIVE_EXCHANGE_1, COLLECTIVE_EXCHANGE_0A, COLLECTIVE_EXCHANGE_0B,
 COLLECTIVE_GATHER_SMALL) = range(1, 7)
COLLECTIVE_PAIR_SUM = (7, 8)


def _handshake(peers):
    barrier = pltpu.get_barrier_semaphore()
    for peer in peers:
        pl.semaphore_signal(barrier, inc=1, device_id=peer, device_id_type=MESH)
    pl.semaphore_wait(barrier, len(peers))


_HBM = pl.BlockSpec(memory_space=pltpu.HBM)
_SEM = pl.BlockSpec(memory_space=pltpu.SEMAPHORE)
_EFFECT = pltpu.SideEffectType.DATAFLOW_SIDE_EFFECTING


def _exchange_plan(direct):
    x, y, c = _mesh_pos()
    if not direct:
        return [(j, j, (qx, qy, c)) for j, (qx, qy) in enumerate([(1 - x, y), (x, 1 - y), (1 - x, 1 - y)])]
    plan = []
    for k in range(1, N_DEV):
        px, py, pc = x ^ ((k >> 2) & 1), y ^ ((k >> 1) & 1), c ^ (k & 1)
        plan.append((4 * px + 2 * py + pc, k - 1, (px, py, pc)))
    return plan


def _exchange_copies(directs):
    copies, base = [], 0
    for a, direct in enumerate(directs):
        plan = _exchange_plan(direct)
        copies += [(a, block, slot, peer, base + slot) for block, slot, peer in plan]
        base += len(plan)
    return copies, base


def _exchange_start(srcs, directs, collective_id, name):
    na = len(srcs)
    slots = [N_DEV - 1 if direct else 3 for direct in directs]

    def body(*refs):
        src, land = refs[:na], refs[na:2 * na]
        send_sem, recv_sem = refs[2 * na], refs[2 * na + 1]
        token = refs[-1]
        _handshake([peer for _, _, peer in _exchange_plan(any(directs))])
        for a, block, slot, peer, sem in _exchange_copies(directs)[0]:
            pltpu.make_async_remote_copy(
                src_ref=src[a].at[block], dst_ref=land[a].at[slot], send_sem=send_sem.at[sem],
                recv_sem=recv_sem.at[sem], device_id=peer, device_id_type=MESH).start()
        token[...] = jnp.zeros_like(token)

    zones = [jax.ShapeDtypeStruct((n,) + t.shape[1:], t.dtype) for n, t in zip(slots, srcs)]
    bufs = [pltpu.HBM(t.shape, t.dtype) for t in list(srcs) + zones]
    out = pl.pallas_call(
        body, name=name,
        out_shape=(pltpu.SemaphoreType.DMA((sum(slots),)), pltpu.SemaphoreType.DMA((sum(slots),)), *bufs,
                   jax.ShapeDtypeStruct((8, 128), F32)),
        in_specs=[_HBM] * (2 * na),
        out_specs=(_SEM, _SEM, *([_HBM] * (2 * na)), pl.BlockSpec(memory_space=pltpu.VMEM)),
        input_output_aliases={i: 2 + i for i in range(2 * na)},
        compiler_params=pltpu.CompilerParams(has_side_effects=_EFFECT, collective_id=collective_id),
    )(*[pltpu.with_memory_space_constraint(t, pltpu.HBM) for t in srcs],
      *[pltpu.with_memory_space_constraint(lax.empty(t.shape, t.dtype), pltpu.HBM) for t in zones])
    return out[0], out[1], out[2:2 + na], out[2 + na:2 + 2 * na], out[-1]


def _exchange_wait(send_sem, recv_sem, srcs, lands, directs, after, name):
    na = len(srcs)

    def body(*refs):
        src, land = refs[:na], refs[na:2 * na]
        send_sem_ref, recv_sem_ref = refs[2 * na], refs[2 * na + 1]
        for a, block, slot, peer, sem in _exchange_copies(directs)[0]:
            cp = pltpu.make_async_remote_copy(
                src_ref=src[a].at[block], dst_ref=land[a].at[slot], send_sem=send_sem_ref.at[sem],
                recv_sem=recv_sem_ref.at[sem], device_id=peer, device_id_type=MESH)
            cp.wait_send()
            cp.wait_recv()

    bufs = [pltpu.HBM(t.shape, t.dtype) for t in list(srcs) + list(lands)]
    out = pl.pallas_call(
        body, name=name,
        out_shape=tuple(bufs),
        in_specs=[_HBM] * (2 * na) + [_SEM, _SEM, pl.BlockSpec(memory_space=pl.ANY)],
        out_specs=[_HBM] * (2 * na),
        input_output_aliases={i: i for i in range(2 * na)},
        compiler_params=pltpu.CompilerParams(has_side_effects=_EFFECT),
    )(*srcs, *lands, send_sem, recv_sem, after)
    return out[:na], out[na:]


def _own_then_slots(mine_ref, lands_ref, rows=slice(None)):
    if len(mine_ref.shape) == 3:
        x, y, c = _mesh_pos()
        total = mine_ref[4 * x + 2 * y + c, rows, :].astype(F32)
    else:
        total = mine_ref[rows, :].astype(F32)
    for j in range(lands_ref.shape[0]):
        total = total + lands_ref[j, rows, :].astype(F32)
    return total


SMALL_ROWS = 2 * PACK_SLICE + 2 * 8


def _small_gather_start(mine, lands, dgpre, name):
    def body(*refs):
        hm, ld, dg = refs[:DEPTH], refs[DEPTH:2 * DEPTH], refs[2 * DEPTH:3 * DEPTH]
        send_sem, recv_sem, blk, land, token, own, slots, rows, built, local_sems = refs[3 * DEPTH:]
        x, y, c = _mesh_pos()
        loads = []
        for l in range(DEPTH):
            loads += [pltpu.make_async_copy(hm[l].at[4 * x + 2 * y + c], own.at[l], local_sems.at[3 * l]),
                      pltpu.make_async_copy(ld[l], slots.at[l], local_sems.at[3 * l + 1]),
                      pltpu.make_async_copy(dg[l], rows.at[l], local_sems.at[3 * l + 2])]
        for cp in loads:
            cp.start()
        _handshake([_related(k, x, y, c) for k in ALL_PEERS])
        for cp in loads:
            cp.wait()
        for l in range(DEPTH):
            total = own[l]
            for j in range(N_DEV - 1):
                total = total + slots[l, j]
            built[PACK_SLICE * l:PACK_SLICE * (l + 1), :] = total
            built[2 * PACK_SLICE + 8 * l:2 * PACK_SLICE + 8 * (l + 1), :] = rows[l]
        stores = [pltpu.make_async_copy(built, blk, local_sems.at[3 * DEPTH]),
                  pltpu.make_async_copy(built, _device_rows(land, SMALL_ROWS, x, y, c), local_sems.at[3 * DEPTH + 1])]
        for cp in stores:
            cp.start()
        for cp in stores:
            cp.wait()
        for k in ALL_PEERS:
            pltpu.make_async_remote_copy(
                src_ref=blk, dst_ref=_device_rows(land, SMALL_ROWS, x, y, c), send_sem=send_sem.at[k - 1],
                recv_sem=recv_sem.at[k - 1], device_id=_related(k, x, y, c), device_id_type=MESH).start()
        token[...] = jnp.zeros_like(token)

    hbm = pl.BlockSpec(memory_space=pl.ANY)
    out = pl.pallas_call(
        body, name=name,
        in_specs=[hbm] * (3 * DEPTH),
        out_specs=(_SEM, _SEM, _HBM, _HBM, pl.BlockSpec(memory_space=pltpu.VMEM)),
        out_shape=(pltpu.SemaphoreType.DMA((N_DEV - 1,)), pltpu.SemaphoreType.DMA((N_DEV - 1,)),
                   pltpu.HBM((SMALL_ROWS, 128), F32), pltpu.HBM((N_DEV * SMALL_ROWS, 128), F32),
                   jax.ShapeDtypeStruct((8, 128), F32)),
        scratch_shapes=[pltpu.VMEM((DEPTH, PACK_SLICE, 128), F32), pltpu.VMEM((DEPTH, N_DEV - 1, PACK_SLICE, 128), F32),
                        pltpu.VMEM((DEPTH, 8, 128), F32), pltpu.VMEM((SMALL_ROWS, 128), F32),
                        pltpu.SemaphoreType.DMA((3 * DEPTH + 2,))],
        compiler_params=pltpu.CompilerParams(has_side_effects=_EFFECT, collective_id=COLLECTIVE_GATHER_SMALL),
    )(*mine, *lands, *dgpre)
    return (out[0], out[1]), out[2], out[3], out[4]


def _adamw_math(w, g, m, v):
    m = ADAM_B1 * m + (1.0 - ADAM_B1) * g
    v = ADAM_B2 * v + (1.0 - ADAM_B2) * (g * g)
    m_hat = m / (1.0 - ADAM_B1 ** ADAM_STEP)
    v_hat = v / (1.0 - ADAM_B2 ** ADAM_STEP)
    delta = -ADAM_LR * (m_hat / (jnp.sqrt(v_hat) + ADAM_EPS) + ADAM_WD * w)
    return delta, m, v


def _adamw_layer(layer, parts, token, name):
    steps = {w.shape[1] // rows for _, _, w, _, _, _, rows in parts}
    assert len(steps) == 1, steps
    n = len(parts)

    def body(_, *refs):
        for p in range(n):
            hm_ref, ld_ref, w_ref, m_ref, v_ref = refs[5 * p:5 * (p + 1)]
            g_ref, d_ref, nm_ref, nv_ref = refs[len(refs) - 4 * (n - p):len(refs) - 4 * (n - p - 1)]
            g = _own_then_slots(hm_ref, ld_ref)
            g_ref[...] = g
            d, nm, nv = _adamw_math(w_ref[...], g, m_ref[...], v_ref[...])
            d_ref[...] = d
            nm_ref[...] = nm
            nv_ref[...] = nv

    in_specs, out_specs, out_shape, operands, carried, aliases = [pl.BlockSpec(memory_space=pl.ANY)], [], [], [], [], {}
    for p, (mine, lands, w, m, v, earlier, rows) in enumerate(parts):
        nn = w.shape[2]
        spec = pl.BlockSpec((None, rows, nn), lambda i: (layer, i, 0))
        in_specs += [pl.BlockSpec((rows, nn), lambda i: (i, 0)) if mine.ndim == 2
                     else pl.BlockSpec((N_DEV, rows, nn), lambda i: (0, i, 0)),
                     pl.BlockSpec((lands.shape[0], rows, nn), lambda i: (0, i, 0)), spec, spec, spec]
        out_specs += [spec] * 4
        out_shape += [jax.ShapeDtypeStruct(w.shape, F32)] * 4
        operands += [mine, lands, w, m, v]
        if earlier is not None:
            aliases.update({1 + 5 * n + len(carried) + t: 4 * p + t for t in range(4)})
            carried += list(earlier)
    out = pl.pallas_call(
        body, name=name, grid=(steps.pop(),),
        in_specs=in_specs + [pl.BlockSpec(memory_space=pl.ANY)] * len(carried),
        out_specs=out_specs, out_shape=out_shape, input_output_aliases=aliases,
        compiler_params=_compiler_params(("arbitrary",)),
    )(token, *operands, *carried)
    return [out[4 * p:4 * (p + 1)] for p in range(n)]


def _adamw_small(gathered, params):
    def body(all_ref, *refs):
        ins, outs, packs = refs[:15], refs[15:15 + 21], refs[15 + 21]
        loss_ref = outs[0]
        for dev in range(N_DEV):
            for l in range(DEPTH):
                packs[l, PACK_SLICE * dev:PACK_SLICE * (dev + 1), :] = (
                    all_ref[SMALL_ROWS * dev + PACK_SLICE * l:SMALL_ROWS * dev + PACK_SLICE * (l + 1), :])
        loss_ref[...] = packs[DEPTH - 1, ROW_LOSS:ROW_LOSS + 1, 0:1]

        def update(p, sel, g):
            w_ref, m_ref, v_ref = ins[p], ins[5 + p], ins[10 + p]
            d, nm, nv = _adamw_math(w_ref[sel], g, m_ref[sel], v_ref[sel])
            for t, val in enumerate((g, d, nm, nv)):
                outs[1 + 5 * t + p][sel] = val

        for l in range(DEPTH):
            gp = packs.at[l]
            row0 = 2 * PACK_SLICE + 8 * l
            dgpre = all_ref[row0:row0 + 8, :]
            for dev in range(1, N_DEV):
                dgpre = dgpre + all_ref[SMALL_ROWS * dev + row0:SMALL_ROWS * dev + row0 + 8, :]
            for grp in range(4):
                update(0, (l, grp), gp[ROW_PW + BLOCK * grp:ROW_PW + BLOCK * (grp + 1), :])
                update(1, (slice(l, l + 1), slice(128 * grp, 128 * (grp + 1))), gp[ROW_SC + grp:ROW_SC + grp + 1, :])
            update(2, (slice(l, l + 1), slice(None)), gp[ROW_SINK:ROW_SINK + 1, 0:N_HEADS])
            for r in range(D_MODEL // 128):
                sel = (slice(l, l + 1), slice(128 * r, 128 * (r + 1)))
                update(3, sel, dgpre[r:r + 1, :])
                update(4, sel, gp[ROW_NPOST + r:ROW_NPOST + r + 1, :])

    shapes = [jax.ShapeDtypeStruct(p.shape, F32) for p in params[:5]]
    return pl.pallas_call(
        body, name="adamw_small",
        out_shape=[jax.ShapeDtypeStruct((1, 1), F32)] + shapes * 4,
        scratch_shapes=[pltpu.VMEM((DEPTH, PACK_ROWS, 128), F32)],
        compiler_params=_compiler_params(),
    )(gathered, *params)


def kernel(x, w_in, pool_w, pool_scale, attn_sinks, w_out, norm_pre, norm_post, loss_target, m_w_in, m_pool_w, m_pool_scale, m_attn_sinks, m_w_out, m_norm_pre, m_norm_post, v_w_in, v_pool_w, v_pool_scale, v_attn_sinks, v_w_out, v_norm_pre, v_norm_post):
    x0 = x.reshape(SEQ, D_MODEL)
    target = loss_target.reshape(SEQ, D_MODEL)
    bias = jnp.asarray(_attn_bias())
    w_in_t, m_in_t, v_in_t = (jnp.swapaxes(t, 1, 2) for t in (w_in, m_w_in, v_w_in))

    (win0, wout0), lands = _allgather([(w_in_t, 0), (w_out, 0)], BF16, "gather_w0",
                                              later=[(w_in_t, 1), (w_out, 1)])
    win_full, wout_full = [win0, None], [wout0, None]
    token = x0

    saved = []
    xl = x0
    for layer in range(DEPTH):
        front = (layer, xl, norm_pre, win_full[layer], token, attn_sinks, pool_w, pool_scale, bias)
        if layer == 0:
            u, pg, q, k, v, ag, z, a, x_next, y, win_full[1], wout_full[1] = _fwd_front(
                *front, out=(norm_post, wout_full[layer]), gather=lands)
        else:
            u, pg, q, k, v, ag, z, a, x_next, y = _fwd_front(*front, out=(norm_post, wout_full[layer]))
        saved.append((xl, u, pg, q, k, v, ag, z, a, y))
        xl = x_next

    params_small = [pool_w, pool_scale, attn_sinks, norm_pre, norm_post,
                    m_pool_w, m_pool_scale, m_attn_sinks, m_norm_pre, m_norm_post,
                    v_pool_w, v_pool_scale, v_attn_sinks, v_norm_pre, v_norm_post]

    def start(srcs, directs, paired, collective_id, tag):
        send_sem, recv_sem, srcs, lands, started = _exchange_start(srcs, directs, collective_id, f"exchange_start{tag}")
        return (send_sem, recv_sem, srcs, lands, paired, directs), started

    def finish(handle, after, tag):
        send_sem, recv_sem, srcs, lands, paired, directs = handle
        srcs, lands = _exchange_wait(send_sem, recv_sem, srcs, lands, directs, after, f"exchange_wait{tag}")
        return [s if p is None else p for s, p in zip(srcs, paired)], lands

    def back(layer, top, first, second):
        xin, u, pg, q, k, v, ag, z, a, y = saved[layer]
        return _bwd_back(layer, top, first, second, y, z, norm_post, wout_full[layer], attn_sinks, u, pg, q, k, v,
                         ag, a, pool_w, pool_scale, bias)

    dgpre = [None] * DEPTH
    dx, dproj, gw_out, pack = back(1, True, xl, target)
    dx, dgpre[1], chip_sums, own_sum = _bwd_in(1, "both_pair", token, dproj, saved[1][0], norm_pre, dx, win_full[1])
    top, token = start([chip_sums, gw_out.reshape(N_DEV, OUT_SHARD, D_MODEL), pack.reshape(N_DEV, PACK_SLICE, 128)],
                       [False, True, True], [own_sum, None, None], COLLECTIVE_EXCHANGE_1, "1")

    dproj, gw_out, pack = back(0, False, dx, token)
    early, token = start([gw_out.reshape(N_DEV, OUT_SHARD, D_MODEL), pack.reshape(N_DEV, PACK_SLICE, 128)],
                         [True, True], [None, None], COLLECTIVE_EXCHANGE_0A, "0a")
    chip_sums, own_sum = _bwd_in(0, "dw_pair", token, dproj, saved[0][0], norm_pre)
    late, token = start([chip_sums], [False], [own_sum], COLLECTIVE_EXCHANGE_0B, "0b")
    dx, dgpre[0] = _bwd_in(0, "dx", token, dproj, saved[0][0], norm_pre, dx, win_full[0])

    own1, lands1 = finish(top, dx, "1")
    own0a, lands0a = finish(early, dx, "0a")
    sems, block, land, token = _small_gather_start([own0a[1], own1[2]], [lands0a[1], lands1[2]], dgpre,
                                                   "gather_small_start")
    big_in, big_out = _adamw_layer(1, [(own1[0], lands1[0], w_in_t, m_in_t, v_in_t, None, ADAM_ROWS_IN),
                                       (own1[1], lands1[1], w_out, m_w_out, v_w_out, None, ADAM_ROWS_OUT)],
                                   token, "adamw1")
    own0b, lands0b = finish(late, big_out[0], "0b")
    big_in, big_out = _adamw_layer(0, [(own0b[0], lands0b[0], w_in_t, m_in_t, v_in_t, big_in, ADAM_ROWS_IN),
                                       (own0a[0], lands0a[0], w_out, m_w_out, v_w_out, big_out, ADAM_ROWS_OUT)],
                                   token, "adamw0")
    gathered = _gather_wait(sems, block, land, ALL_PEERS, big_in[0], "gather_small_wait")
    small_out = _adamw_small(gathered, params_small)
    loss = small_out[0].reshape(())

    outs = [loss, dx.reshape(1, SEQ, D_MODEL)]
    for t in range(4):
        pw_, sc_, sk_, npre_, npost_ = small_out[1 + 5 * t:6 + 5 * t]
        outs += [jnp.swapaxes(big_in[t], 1, 2), pw_, sc_, sk_, big_out[t], npre_, npost_]
    return tuple(outs)
```

```python
import functools
import numpy as np
import jax
import jax.numpy as jnp
from jax import lax
from jax.experimental import pallas as pl
from jax.experimental.pallas import tpu as pltpu

F32 = jnp.float32
BF16 = jnp.bfloat16

N_DEV = 8
SEQ = 2048
D_MODEL = 1024
D_POOL = 512
D_ATTN = 512
D_KV = 128
D_IN = 2304
N_HEADS = 8
GQA = 4
HEAD_DIM = 64
BLOCK = 128
POOL_WINDOWS = (2, 4, 8, 16)
DEPTH = 2
EPS = 1e-6
NEG_INF = -1e30
SCALE = HEAD_DIM ** -0.5
IN_SHARD = D_IN // N_DEV
OUT_SHARD = D_MODEL // N_DEV

COL_U, COL_PG, COL_Q, COL_K, COL_V, COL_AG = 0, 512, 1024, 1536, 1664, 1792

ADAM_LR = 0.001
ADAM_B1 = 0.9
ADAM_B2 = 0.999
ADAM_EPS = 1e-08
ADAM_WD = 0.01
ADAM_STEP = 10

TOKEN_TILE = 512
ADAM_ROWS_IN, ADAM_ROWS_OUT = 144, 64
VMEM_LIMIT = 56 * 1024 * 1024
MESH = pl.DeviceIdType.MESH

ROW_PW, ROW_SC, ROW_SINK, ROW_NPOST, ROW_LOSS = 0, 512, 520, 536, 544
PACK_ROWS = 576
PACK_SLICE = PACK_ROWS // N_DEV


def _nn(a, b):
    return jnp.dot(a, b, preferred_element_type=F32)


def _nt(a, b):
    return lax.dot_general(a, b, (((1,), (1,)), ((), ())), preferred_element_type=F32)


def _tn(a, b):
    return lax.dot_general(a, b, (((0,), (0,)), ((), ())), preferred_element_type=F32)


def _silu_parts(g):
    s = jax.nn.sigmoid(g)
    return g * s, s * (1.0 + g * (1.0 - s))


def _resident(shape):
    return pl.BlockSpec(shape, lambda *_: (0,) * len(shape), pipeline_mode=pl.Buffered(1))


def _compiler_params(sem=None):
    if sem is None:
        return pltpu.CompilerParams(vmem_limit_bytes=VMEM_LIMIT)
    return pltpu.CompilerParams(dimension_semantics=sem, vmem_limit_bytes=VMEM_LIMIT)


def _attn_bias():
    t = np.arange(BLOCK)[None, :]
    j = np.arange(BLOCK)[:, None]
    current = j <= t
    dist = np.where(current, t - j, t + BLOCK - j).astype(np.float32)
    out = np.zeros((2, 2, BLOCK, GQA * BLOCK), np.float32)
    for variant in range(2):
        valid = current | (variant == 1)
        for kv in range(2):
            for g in range(GQA):
                slope = np.float32(2.0 ** (-(kv * GQA + g + 1)))
                out[variant, kv, :, g * BLOCK:(g + 1) * BLOCK] = np.where(valid, -slope * dist, np.float32(NEG_INF))
    return out


def _replicate_head(kx, kv):
    rolled = pltpu.roll(kx, 64, 1)
    lane = lax.broadcasted_iota(jnp.int32, kx.shape, 1)
    dup = jnp.where(lane < 64, kx, rolled) if kv == 0 else jnp.where(lane < 64, rolled, kx)
    return jnp.concatenate([dup, dup], axis=1).astype(BF16)


def _stack_heads(qv):
    lane = lax.broadcasted_iota(jnp.int32, qv.shape, 1)
    zero = jnp.zeros_like(qv)
    return jnp.concatenate([jnp.where((lane >= 64 * g) & (lane < 64 * g + 64), qv, zero) for g in range(GQA)], axis=0)


def _unstack_heads(xs):
    lane = lax.broadcasted_iota(jnp.int32, (BLOCK, 256), 1)
    return jnp.where(lane < 64, xs[0:128], jnp.where(lane < 128, xs[128:256], jnp.where(lane < 192, xs[256:384], xs[384:512])))


def _fold_heads(r):
    h = r[:, 0:128] + r[:, 128:256]
    return h + pltpu.roll(h, 64, 1)


def _sink_row(sink_ref, layer, kv):
    lane = lax.broadcasted_iota(jnp.int32, (1, GQA * BLOCK), 1)
    s4 = [sink_ref[layer, kv * GQA + g] for g in range(GQA)]
    return jnp.where(lane < 128, s4[0], jnp.where(lane < 256, s4[1], jnp.where(lane < 384, s4[2], s4[3])))


def _band_is_current():
    j = lax.broadcasted_iota(jnp.int32, (BLOCK, GQA * BLOCK), 0)
    t = lax.broadcasted_iota(jnp.int32, (BLOCK, GQA * BLOCK), 1) & (BLOCK - 1)
    return j <= t


def _pack_band(full, current):
    return jnp.where(current, full[BLOCK:], full[:BLOCK])


def _unpack_band(packed, current):
    zero = jnp.zeros_like(packed)
    return jnp.concatenate([jnp.where(current, zero, packed), jnp.where(current, packed, zero)], axis=0)


def _probs_keys_major(k_rep, q_st, bias, sink, current):
    st = _pack_band(_nt(k_rep, q_st), current) * SCALE + bias
    m = jnp.maximum(jnp.max(st, axis=0, keepdims=True), sink)
    p = jnp.exp(st - m)
    esink = jnp.exp(sink - m)
    rl = 1.0 / (jnp.sum(p, axis=0, keepdims=True) + esink)
    return p * rl, esink * rl


WINDOW_HALO = 16


def _window_sum(ext, w, forward):
    s = ext
    sh = 1
    while sh < w:
        s = s + pltpu.roll(s, (ext.shape[0] - sh) if forward else sh, 0)
        sh *= 2
    return s


def _inv_count(n, w):
    t = n * BLOCK + lax.broadcasted_iota(jnp.int32, (BLOCK, 1), 0) + 1
    return 1.0 / jnp.minimum(t.astype(F32), float(w))


def _kv_ext(ref, n):
    r0 = pl.multiple_of(jnp.maximum(n - 1, 0) * BLOCK, BLOCK)
    r1 = pl.multiple_of(n * BLOCK, BLOCK)
    return jnp.concatenate([ref[pl.ds(r0, BLOCK), :], ref[pl.ds(r1, BLOCK), :]], axis=0)


def _rows_of(vec_ref, pack_ref, row0):
    for r in range(D_MODEL // 128):
        pack_ref[row0 + r:row0 + r + 1, :] = vec_ref[:, 128 * r:128 * (r + 1)]


FRONT_TILE = 4 * BLOCK


def _fwd_front(layer, x, norm_pre, w_in_t, token, sinks, pool_w, pool_scale, bias, out=None, gather=None):
    tm = FRONT_TILE

    def body(sink_ref, x_ref, g_ref, w_ref, _, pw_ref, sc_ref, bias_ref, *refs):
        if out is not None:
            gpost_ref, wo_ref, *refs = refs
        if gather is not None:
            refs = refs[2:]
            zones, (send_sems, recv_sems), refs = refs[10:12], refs[-2:], refs[:-2]
        if out is not None:
            xn_ref, y_ref = refs[8:10]
        u_ref, pg_ref, q_ref, k_ref, v_ref, ag_ref, z_ref, a_ref = refs[:8]
        uprev, kprev, vprev = refs[-3:]
        i = pl.program_id(0)

        @pl.when(i == 0)
        def _():
            uprev[...] = jnp.zeros_like(uprev)
            kprev[...] = jnp.zeros_like(kprev)
            vprev[...] = jnp.zeros_like(vprev)

        if gather is not None:
            px, py, pc = _mesh_pos()
            me, sibling = (px, py, pc), (px, py, 1 - pc)
            near = [(1 - px, py), (px, 1 - py)]
            far = (1 - px, 1 - py)
            relay_from, relay_to = (px ^ (1 - pc), py ^ pc), (px ^ pc, py ^ (1 - pc))
            k_from, k_to = 1 + pc, 2 - pc

            def copy(a, sem, block, to):
                rows_of_block = _device_rows(zones[a], gather[a].shape[0] // N_DEV, *block)
                return pltpu.make_async_remote_copy(
                    src_ref=rows_of_block, dst_ref=rows_of_block, send_sem=send_sems.at[a, sem],
                    recv_sem=recv_sems.at[a, sem], device_id=to, device_id_type=MESH)

            def pass_on(a):
                copy(a, k_from, (*relay_from, pc), me).wait_recv()
                copy(a, 3, (*relay_from, pc), (*relay_to, pc)).start()
                copy(a, 3 + k_from, (*relay_from, pc), sibling).start()
                copy(a, k_to, (*relay_to, pc), me).wait_recv()
                copy(a, 3 + k_to, (*relay_to, pc), sibling).start()

            @pl.when(i == 0)
            def _():
                _handshake([sibling] + [(*chip, pc) for chip in near])
                for a in range(2):
                    copy(a, 0, me, sibling).start()
                    for j, chip in enumerate(near):
                        copy(a, 1 + j, me, (*chip, pc)).start()

            for a in range(2):
                pl.when(i == 2 + a)(functools.partial(pass_on, a))

        xv = x_ref[...]
        r = lax.rsqrt(jnp.mean(xv * xv, axis=-1, keepdims=True) + EPS)
        h = (xv * r * g_ref[layer:layer + 1, :]).astype(BF16)
        u_ref[...] = _nt(h, w_ref[COL_U:COL_PG, :])
        pg_ref[...] = _nt(h, w_ref[COL_PG:COL_Q, :])
        for sb in range(tm // BLOCK):
            n = (tm // BLOCK) * i + sb
            rows = slice(BLOCK * sb, BLOCK * (sb + 1))
            before = slice(BLOCK * (sb - 1), BLOCK * sb)
            uv = u_ref[rows, :]
            halo = (uprev[BLOCK - WINDOW_HALO:, :] if sb == 0
                    else u_ref[BLOCK * sb - WINDOW_HALO:BLOCK * sb, :])
            ext = jnp.concatenate([halo, uv], axis=0)
            for g, w in enumerate(POOL_WINDOWS):
                cs = slice(BLOCK * g, BLOCK * (g + 1))
                win = _window_sum(ext[:, cs], w, forward=False)[WINDOW_HALO:]
                pooled = win * _inv_count(n, w) - uv[:, cs]
                mixed = _nn(pooled.astype(BF16), pw_ref[g].astype(BF16))
                gate, _ = _silu_parts(pg_ref[rows, cs])
                z_ref[rows, cs] = (mixed * sc_ref[layer:layer + 1, cs] * gate).astype(BF16)

        q_ref[...] = _nt(h, w_ref[COL_Q:COL_K, :]).astype(BF16)
        k_ref[...] = _nt(h, w_ref[COL_K:COL_V, :])
        v_ref[...] = _nt(h, w_ref[COL_V:COL_AG, :])
        ag_ref[...] = _nt(h, w_ref[COL_AG:D_IN, :])

        current = _band_is_current()
        for sb in range(tm // BLOCK):
            n = (tm // BLOCK) * i + sb
            rows = slice(BLOCK * sb, BLOCK * (sb + 1))
            before = slice(BLOCK * (sb - 1), BLOCK * sb)
            kx = jnp.concatenate([kprev[...] if sb == 0 else k_ref[before, :], k_ref[rows, :]], axis=0)
            vx = jnp.concatenate([vprev[...] if sb == 0 else v_ref[before, :], v_ref[rows, :]], axis=0)
            variant = jnp.minimum(n, 1) if sb == 0 else 1
            for kv in range(2):
                cs = slice(256 * kv, 256 * (kv + 1))
                p, _ = _probs_keys_major(_replicate_head(kx, kv), _stack_heads(q_ref[rows, cs]),
                                         bias_ref[variant, kv], _sink_row(sink_ref, layer, kv), current)
                o = _unstack_heads(_tn(_unpack_band(p.astype(BF16), current), _replicate_head(vx, kv)))
                a_ref[rows, cs] = o
                gate, _ = _silu_parts(ag_ref[rows, cs])
                z_ref[rows, D_POOL + 256 * kv:D_POOL + 256 * (kv + 1)] = (o * gate).astype(BF16)

        tail = slice(tm - BLOCK, tm)
        uprev[...] = u_ref[tail, :]
        kprev[...] = k_ref[tail, :]
        vprev[...] = v_ref[tail, :]

        if out is not None:
            y = _nn(z_ref[...], wo_ref[...])
            y_ref[...] = y
            r = lax.rsqrt(jnp.mean(y * y, axis=-1, keepdims=True) + EPS)
            xn_ref[...] = x_ref[...] + y * r * gpost_ref[layer:layer + 1, :]

        if gather is not None:
            @pl.when(i == SEQ // tm - 1)
            def _():
                for a in range(2):
                    copy(a, 3, (*far, pc), me).wait_recv()
                    copy(a, 6, (*far, pc), sibling).start()
                for a in range(2):
                    copy(a, 0, sibling, me).wait_recv()
                    for j, chip in enumerate(near + [far]):
                        copy(a, 4 + j, (*chip, 1 - pc), me).wait_recv()
                for a in range(2):
                    copy(a, 0, me, sibling).wait_send()
                    for j, chip in enumerate(near):
                        copy(a, 1 + j, me, (*chip, pc)).wait_send()
                    copy(a, 3, (*relay_from, pc), (*relay_to, pc)).wait_send()
                    copy(a, 3 + k_from, (*relay_from, pc), sibling).wait_send()
                    copy(a, 3 + k_to, (*relay_to, pc), sibling).wait_send()
                    copy(a, 6, (*far, pc), sibling).wait_send()

    row = lambda c: pl.BlockSpec((tm, c), lambda i: (i, 0))
    const = lambda shape: pl.BlockSpec(shape, lambda i: (0,) * len(shape))
    act = jax.ShapeDtypeStruct((SEQ, D_MODEL), F32)
    fused = out is not None
    hbm = pl.BlockSpec(memory_space=pl.ANY)
    zones, params = [], _compiler_params(("arbitrary",))
    if gather is not None:
        assert fused and SEQ // tm == 4, "the gather's copies are spread over four grid steps"
        zones = list(gather)
        params = pltpu.CompilerParams(dimension_semantics=("arbitrary",), vmem_limit_bytes=VMEM_LIMIT,
                                      collective_id=COLLECTIVE_GATHER_W1)
    return pl.pallas_call(
        body, name=f"fwd_front{layer}", grid=(SEQ // tm,),
        in_specs=[pl.BlockSpec(memory_space=pltpu.SMEM), row(D_MODEL), const((DEPTH, D_MODEL)),
                  _resident((D_IN, D_MODEL)), const((8, 128)),
                  pl.BlockSpec((None, 4, BLOCK, BLOCK), lambda i: (layer, 0, 0, 0)), const((DEPTH, D_POOL)),
                  _resident((2, 2, BLOCK, GQA * BLOCK))]
                 + ([const((DEPTH, D_MODEL)), _resident((D_MODEL, D_MODEL))] if fused else []) + [hbm] * len(zones),
        out_specs=[row(D_POOL), row(D_POOL), row(D_ATTN), row(D_KV), row(D_KV), row(D_ATTN), row(D_MODEL),
                   row(D_ATTN)] + ([row(D_MODEL)] * 2 if fused else []) + [hbm] * len(zones),
        out_shape=[jax.ShapeDtypeStruct((SEQ, D_POOL), F32), jax.ShapeDtypeStruct((SEQ, D_POOL), F32),
                   jax.ShapeDtypeStruct((SEQ, D_ATTN), BF16), jax.ShapeDtypeStruct((SEQ, D_KV), F32),
                   jax.ShapeDtypeStruct((SEQ, D_KV), F32), jax.ShapeDtypeStruct((SEQ, D_ATTN), F32),
                   jax.ShapeDtypeStruct((SEQ, D_MODEL), BF16), jax.ShapeDtypeStruct((SEQ, D_ATTN), F32)]
                  + ([act] * 2 if fused else []) + [jax.ShapeDtypeStruct(t.shape, t.dtype) for t in zones],
        scratch_shapes=[pltpu.VMEM((BLOCK, D_POOL), F32), pltpu.VMEM((BLOCK, D_KV), F32),
                        pltpu.VMEM((BLOCK, D_KV), F32)]
                       + ([pltpu.SemaphoreType.DMA((2, 7)), pltpu.SemaphoreType.DMA((2, 7))] if zones else []),
        input_output_aliases={10 + a: 10 + a for a in range(len(zones))},
        compiler_params=params,
    )(sinks, x, norm_pre, w_in_t, token, pool_w, pool_scale, bias, *(out if fused else ()), *zones)


BACK_TILE = 2 * BLOCK


def _bwd_back(layer, top, dxo_or_xf, target_or_token, y, z, norm_post, w_out, sinks, u, pg, q, k, v, ag, a,
              pool_w, pool_scale, bias):
    tm = BACK_TILE
    steps = SEQ // tm
    last = steps - 1
    per = tm // BLOCK

    def body(*refs):
        refs = list(refs)
        sink_ref, first, second = refs[:3]
        (y_ref, z_ref, g_ref, w_ref, u_ref, up_ref, pg_ref, q_ref, k_ref, v_ref, ag_ref, a_ref, pw_ref, sc_ref,
         bias_ref) = refs[3:18]
        del refs[:18]
        dxo_ref = refs.pop(0) if top else None
        dp_ref, dw_ref, pack_ref, acc, dg, lacc, dzs, ck, cv, ce = refs
        i = pl.program_id(0)
        blk = last - i

        @pl.when(i == 0)
        def _():
            acc[...] = jnp.zeros_like(acc)
            dg[...] = jnp.zeros_like(dg)
            lacc[...] = jnp.zeros_like(lacc)
            pack_ref[...] = jnp.zeros_like(pack_ref)
            ck[...] = jnp.zeros_like(ck)
            cv[...] = jnp.zeros_like(cv)
            ce[...] = jnp.zeros_like(ce)

        if top:
            d = first[...] - second[...]
            dxo_v = d * (1.0 / D_MODEL)
            dxo_ref[...] = dxo_v
            part = jnp.sum(d * d, axis=-1, keepdims=True) * (1.0 / D_MODEL)
            lacc[...] += 0.5 * jnp.sum(part, axis=0, keepdims=True)
        else:
            dxo_v = first[...]
        yv = y_ref[...]
        r = lax.rsqrt(jnp.mean(yv * yv, axis=-1, keepdims=True) + EPS)
        yn = yv * r
        dg[...] += jnp.sum(dxo_v * yn, axis=0, keepdims=True)
        dyn = dxo_v * g_ref[layer:layer + 1, :]
        dy = (r * (dyn - yn * jnp.mean(dyn * yn, axis=-1, keepdims=True))).astype(BF16)
        dzs[...] = _nt(dy, w_ref[...])
        acc[...] += _tn(z_ref[...], dy)

        lane = lax.broadcasted_iota(jnp.int32, (1, 128), 1)
        lane2 = lax.broadcasted_iota(jnp.int32, (256, 128), 1)
        current = _band_is_current()
        for sb in reversed(range(per)):
            n = per * blk + sb
            rows = slice(BLOCK * sb, BLOCK * (sb + 1))

            uv = u_ref[rows, :]
            if sb == 0:
                halo = up_ref[BLOCK - WINDOW_HALO:, :] * (n > 0).astype(F32)
            else:
                halo = u_ref[BLOCK * sb - WINDOW_HALO:BLOCK * sb, :]
            ext = jnp.concatenate([halo, uv], axis=0)
            for g, w in enumerate(POOL_WINDOWS):
                cs = slice(BLOCK * g, BLOCK * (g + 1))
                inv = _inv_count(n, w)
                win = _window_sum(ext[:, cs], w, forward=False)[WINDOW_HALO:]
                pooled = win * inv - uv[:, cs]
                pw_g = pw_ref[g].astype(BF16)
                mixed = _nn(pooled.astype(BF16), pw_g)
                gate, dgate = _silu_parts(pg_ref[rows, cs])
                dzp = dzs[rows, cs]
                sc = sc_ref[layer:layer + 1, cs]
                dpm = dzp * gate
                dp_ref[rows, COL_PG + BLOCK * g:COL_PG + BLOCK * (g + 1)] = (dzp * (mixed * sc) * dgate).astype(BF16)
                pack_ref[ROW_SC + g:ROW_SC + g + 1, :] += jnp.sum(dpm * mixed, axis=0, keepdims=True)
                dmixed = (dpm * sc).astype(BF16)
                pack_ref[ROW_PW + BLOCK * g:ROW_PW + BLOCK * (g + 1), :] += _tn(pooled.astype(BF16), dmixed)
                dpooled = _nt(dmixed, pw_g)
                e = dpooled * inv
                lead = _window_sum(jnp.concatenate([e, ce[:WINDOW_HALO, cs]], axis=0), w, forward=True)[:BLOCK]
                dp_ref[rows, COL_U + BLOCK * g:COL_U + BLOCK * (g + 1)] = (lead - dpooled).astype(BF16)
                ce[:, cs] = e

            kx = _kv_ext(k_ref, n)
            vx = _kv_ext(v_ref, n)
            variant = jnp.minimum(n, 1) if sb == 0 else 1
            dsink_row = jnp.zeros((1, 128), F32)
            tks, tvs = [], []
            for kv in range(2):
                cs = slice(256 * kv, 256 * (kv + 1))
                k_rep = _replicate_head(kx, kv)
                v_rep = _replicate_head(vx, kv)
                q_st = _stack_heads(q_ref[rows, cs])
                gate, dgate = _silu_parts(ag_ref[rows, cs])
                dza = dzs[rows, D_POOL + 256 * kv:D_POOL + 256 * (kv + 1)]
                dp_ref[rows, COL_AG + 256 * kv:COL_AG + 256 * (kv + 1)] = (dza * a_ref[rows, cs] * dgate).astype(BF16)
                da_st = _stack_heads((dza * gate).astype(BF16))
                p, psink = _probs_keys_major(k_rep, q_st, bias_ref[variant, kv], _sink_row(sink_ref, layer, kv),
                                             current)
                dpt = _pack_band(_nt(v_rep, da_st), current)
                delta = jnp.sum(p * dpt, axis=0, keepdims=True)
                dst = _unpack_band((p * (dpt - delta) * SCALE).astype(BF16), current)
                sink_terms = psink * delta
                for g in range(GQA):
                    dsink = -jnp.sum(sink_terms[:, BLOCK * g:BLOCK * (g + 1)], axis=1, keepdims=True)
                    dsink_row = dsink_row + jnp.where(lane == kv * GQA + g, dsink, 0.0)
                dp_ref[rows, COL_Q + 256 * kv:COL_Q + 256 * (kv + 1)] = _unstack_heads(_tn(dst, k_rep)).astype(BF16)
                tks.append(_fold_heads(_nn(dst, q_st)))
                tvs.append(_fold_heads(_nn(_unpack_band(p.astype(BF16), current), da_st)))
            pack_ref[ROW_SINK:ROW_SINK + 1, :] += dsink_row
            dkx = jnp.where(lane2 < 64, tks[0], tks[1])
            dvx = jnp.where(lane2 < 64, tvs[0], tvs[1])
            dp_ref[rows, COL_K:COL_V] = (ck[...] + dkx[BLOCK:]).astype(BF16)
            dp_ref[rows, COL_V:COL_AG] = (cv[...] + dvx[BLOCK:]).astype(BF16)
            ck[...] = dkx[:BLOCK]
            cv[...] = dvx[:BLOCK]

        @pl.when(i == steps - 1)
        def _():
            dw_ref[...] = acc[...].astype(BF16)
            _rows_of(dg, pack_ref, ROW_NPOST)
            pack_ref[ROW_LOSS:ROW_LOSS + 1, :] = jnp.where(lane == 0, lacc[...], 0.0)

    row = lambda c: pl.BlockSpec((tm, c), lambda i: (last - i, 0))
    const = lambda shape: pl.BlockSpec(shape, lambda i: (0,) * len(shape))
    act = jax.ShapeDtypeStruct((SEQ, D_MODEL), F32)
    return pl.pallas_call(
        body, name=f"bwd_back{layer}", grid=(steps,),
        in_specs=[pl.BlockSpec(memory_space=pltpu.SMEM), row(D_MODEL), row(D_MODEL) if top else const((8, 128)),
                  row(D_MODEL), row(D_MODEL), const((DEPTH, D_MODEL)), _resident((D_MODEL, D_MODEL)),
                  row(D_POOL), pl.BlockSpec((BLOCK, D_POOL), lambda i: (jnp.maximum(per * (last - i) - 1, 0), 0)),
                  row(D_POOL), row(D_ATTN), _resident((SEQ, D_KV)), _resident((SEQ, D_KV)), row(D_ATTN), row(D_ATTN),
                  pl.BlockSpec((None, 4, BLOCK, BLOCK), lambda i: (layer, 0, 0, 0)), const((DEPTH, D_POOL)),
                  _resident((2, 2, BLOCK, GQA * BLOCK))],
        out_specs=([row(D_MODEL)] * (1 if top else 0)
                   + [row(D_IN), const((D_MODEL, D_MODEL)), const((PACK_ROWS, 128))]),
        out_shape=([act] * (1 if top else 0)
                   + [jax.ShapeDtypeStruct((SEQ, D_IN), BF16), jax.ShapeDtypeStruct((D_MODEL, D_MODEL), BF16),
                      jax.ShapeDtypeStruct((PACK_ROWS, 128), F32)]),
        scratch_shapes=[pltpu.VMEM((D_MODEL, D_MODEL), F32), pltpu.VMEM((1, D_MODEL), F32), pltpu.VMEM((1, 1), F32),
                        pltpu.VMEM((tm, D_MODEL), F32), pltpu.VMEM((BLOCK, D_KV), F32), pltpu.VMEM((BLOCK, D_KV), F32),
                        pltpu.VMEM((BLOCK, D_POOL), F32)],
        compiler_params=_compiler_params(("arbitrary",)),
    )(sinks, dxo_or_xf, target_or_token, y, z, norm_post, w_out, u, u, pg, q, k, v, ag, a, pool_w, pool_scale, bias)


def _bwd_in(layer, part, token, dproj, x, norm_pre, dxo=None, w_in_t=None, adam=None):
    pair = part in ("dw_pair", "both_pair")
    want_dw, want_dx = part != "dx", part in ("both", "dx", "both_pair")
    tm = TOKEN_TILE
    steps = SEQ // tm
    adam_layer, adam_parts = adam if adam is not None else (None, [])
    n_adam = len(adam_parts)
    adam_steps = adam_parts[0][2].shape[1] // adam_parts[0][6] if n_adam else 0
    assert part == "dx" or not n_adam
    assert all(w.shape[1] // rows == adam_steps <= steps for _, _, w, _, _, _, rows in adam_parts)
    cw = 256

    def body(*refs):
        refs = list(refs)
        dp_ref, x_ref, g_ref = refs[1:4]
        del refs[:4]
        if want_dx:
            dxo_ref, w_ref = refs[:2]
            adam_in, adam_out = refs[2:2 + 5 * n_adam], refs[4 + 5 * n_adam:4 + 9 * n_adam]
            dx_ref, dgo_ref = refs[2 + 5 * n_adam:4 + 5 * n_adam]
            del refs[:4 + 9 * n_adam]
            dg = refs.pop()
        if pair:
            hs_ref, hm_ref, acc, mine_buf, theirs_buf, send_sem, recv_sem = refs
        elif want_dw:
            dw_ref, acc = refs
        i = pl.program_id(0)

        @pl.when(i == 0)
        def _():
            if pair:
                _handshake([(lax.axis_index("x"), lax.axis_index("y"), 1 - lax.axis_index("c"))])
            if want_dw:
                acc[...] = jnp.zeros_like(acc)
            if want_dx:
                dg[...] = jnp.zeros_like(dg)

        if n_adam:
            @pl.when(i < adam_steps)
            def _():
                for p in range(n_adam):
                    hm_ref, ld_ref, wa_ref, ma_ref, va_ref = adam_in[5 * p:5 * (p + 1)]
                    g = _own_then_slots(hm_ref, ld_ref)
                    for ref, val in zip(adam_out[4 * p:4 * (p + 1)], (g, *_adamw_math(wa_ref[...], g, ma_ref[...], va_ref[...]))):
                        ref[...] = val

        xv = x_ref[...]
        gv = g_ref[layer:layer + 1, :]
        r = lax.rsqrt(jnp.mean(xv * xv, axis=-1, keepdims=True) + EPS)
        xn = xv * r
        if want_dw:
            hb = (xn * gv).astype(BF16)
            for c in range(0, D_IN, cw):
                acc[c:c + cw, :] += _tn(dp_ref[:, c:c + cw], hb)
        def rows_for(q, core):
            return pl.ds(pl.multiple_of((2 * q + core) * IN_SHARD, 8), IN_SHARD)

        def swap(q):
            x, y, c = _mesh_pos()
            return pltpu.make_async_remote_copy(
                src_ref=mine_buf.at[q], dst_ref=theirs_buf.at[q], send_sem=send_sem.at[q], recv_sem=recv_sem.at[q],
                device_id=(x, y, 1 - c), device_id_type=MESH)

        if pair:
            @pl.when(i == steps - 1)
            def _():
                for q in range(4):
                    mine_buf[q] = acc[rows_for(q, 1 - lax.axis_index("c")), :].astype(BF16)
                    swap(q).start()

        if want_dx:
            dh = _nn(dp_ref[...], w_ref[...])
            dg[...] += jnp.sum(dh * xn, axis=0, keepdims=True)
            dhn = dh * gv
            dx_ref[...] = dxo_ref[...] + r * (dhn - xn * jnp.mean(dhn * xn, axis=-1, keepdims=True))

        @pl.when(i == steps - 1)
        def _():
            if pair:
                x, y, c = _mesh_pos()
                for q in range(4):
                    swap(q).wait()
                for j, q in enumerate([2 * (1 - x) + y, 2 * x + (1 - y), 2 * (1 - x) + (1 - y)]):
                    hs_ref[j] = (acc[rows_for(q, c), :] + theirs_buf[q].astype(F32)).astype(BF16)
                hm_ref[...] = acc[rows_for(2 * x + y, c), :] + theirs_buf[2 * x + y].astype(F32)
            elif want_dw:
                dw_ref[...] = acc[...].astype(BF16)
            if want_dx:
                _rows_of(dg, dgo_ref, 0)

    row = lambda c: pl.BlockSpec((tm, c), lambda i: (i, 0))
    const = lambda shape: pl.BlockSpec(shape, lambda i: (0,) * len(shape))
    in_specs = [const((8, 128)), row(D_IN), row(D_MODEL), const((DEPTH, D_MODEL))]
    operands = [token, dproj, x, norm_pre]
    out_specs, out_shape, scratch = [], [], []
    if want_dx:
        in_specs += [row(D_MODEL), _resident((D_IN, D_MODEL))]
        operands += [dxo, w_in_t]
        out_specs += [row(D_MODEL), const((8, 128))]
        out_shape += [jax.ShapeDtypeStruct((SEQ, D_MODEL), F32), jax.ShapeDtypeStruct((8, 128), F32)]
    for mine, lands, w, m, v, _, rows in adam_parts:
        slab = lambda i: jnp.minimum(i, adam_steps - 1)
        spec = pl.BlockSpec((None, rows, w.shape[2]), lambda i: (adam_layer, slab(i), 0))
        in_specs += [pl.BlockSpec((rows, w.shape[2]), lambda i: (slab(i), 0)) if mine.ndim == 2
                     else pl.BlockSpec((N_DEV, rows, w.shape[2]), lambda i: (0, slab(i), 0)),
                     pl.BlockSpec((lands.shape[0], rows, w.shape[2]), lambda i: (0, slab(i), 0)), spec, spec, spec]
        operands += [mine, lands, w, m, v]
        out_specs += [spec] * 4
        out_shape += [jax.ShapeDtypeStruct(w.shape, F32)] * 4
    if pair:
        out_specs += [const((3, IN_SHARD, D_MODEL)), const((IN_SHARD, D_MODEL))]
        out_shape += [jax.ShapeDtypeStruct((3, IN_SHARD, D_MODEL), BF16), jax.ShapeDtypeStruct((IN_SHARD, D_MODEL), F32)]
        scratch += [pltpu.VMEM((D_IN, D_MODEL), F32), pltpu.VMEM((4, IN_SHARD, D_MODEL), BF16),
                    pltpu.VMEM((4, IN_SHARD, D_MODEL), BF16), pltpu.SemaphoreType.DMA((4,)), pltpu.SemaphoreType.DMA((4,))]
    elif want_dw:
        out_specs.append(const((D_IN, D_MODEL)))
        out_shape.append(jax.ShapeDtypeStruct((D_IN, D_MODEL), BF16))
        scratch.append(pltpu.VMEM((D_IN, D_MODEL), F32))
    if want_dx:
        scratch.append(pltpu.VMEM((1, D_MODEL), F32))
    params = pltpu.CompilerParams(dimension_semantics=("arbitrary",), vmem_limit_bytes=VMEM_LIMIT,
                                  collective_id=COLLECTIVE_PAIR_SUM[layer] if pair else None)
    return pl.pallas_call(
        body, name=f"bwd_in_{part}{layer}", grid=(steps,),
        in_specs=in_specs, out_specs=out_specs, out_shape=out_shape, scratch_shapes=scratch,
        compiler_params=params,
    )(*operands)


def _mesh_pos():
    return lax.axis_index("x"), lax.axis_index("y"), lax.axis_index("c")


def _device_rows(ref, m, px, py, pc):
    return ref.at[pl.ds(pl.multiple_of((4 * px + 2 * py + pc) * m, 16 if m % 16 == 0 else 8), m), :]


def _allgather(srcs, out_dtype, name, later=()):
    na, nb = len(srcs), len(later)
    every = list(srcs) + list(later)
    shapes = [(a.shape[-2], a.shape[-1]) for a, _ in every]

    def body(*refs):
        xs, refs = refs[:na + nb], refs[na + nb:]
        outs, land, refs = refs[:na], refs[na:na + nb], refs[na + nb:]
        stage, cast, raw = refs[:na], refs[na:na + nb], refs[na + nb:2 * (na + nb)]
        send_sems, recv_sems, local_sems, load_sems = refs[2 * (na + nb):]
        loads = [pltpu.make_async_copy(xs[i].at[every[i][1]], raw[i], load_sems.at[i]) for i in range(na + nb)]
        for cp in loads:
            cp.start()
        x, y, c = _mesh_pos()
        me, sibling = (x, y, c), (x, y, 1 - c)
        near = [(1 - x, y), (x, 1 - y)]
        far = (1 - x, 1 - y)
        relay_from, relay_to = (x ^ (1 - c), y ^ c), (x ^ c, y ^ (1 - c))
        _handshake([sibling] + [(*chip, c) for chip in near])
        k_from, k_to = 1 + c, 2 - c

        def slot(a, px, py, pc):
            return _device_rows(outs[a], shapes[a][0], px, py, pc)

        def copy(a, k, block, to, src=None):
            return pltpu.make_async_remote_copy(
                src_ref=slot(a, *block) if src is None else src, dst_ref=slot(a, *block),
                send_sem=send_sems.at[a, k], recv_sem=recv_sems.at[a, k], device_id=to, device_id_type=MESH)

        def cast_block(i):
            loads[i].wait()
            return raw[i][...].astype(out_dtype)

        for a in range(na):
            stage[a][...] = cast_block(a)
        mine = [pltpu.make_async_copy(stage[a], slot(a, *me), local_sems.at[a]) for a in range(na)]
        for cp in mine:
            cp.start()
        sent = []
        for a in range(na):
            sent.append(copy(a, 0, me, sibling, src=stage[a]))
            sent += [copy(a, 1 + j, me, (*chip, c), src=stage[a]) for j, chip in enumerate(near)]
        for cp in sent:
            cp.start()
        for b in range(nb):
            cast[b][...] = cast_block(na + b)
            cp = pltpu.make_async_copy(cast[b], _device_rows(land[b], shapes[na + b][0], *me), local_sems.at[na + b])
            cp.start()
            mine.append(cp)
        for a in range(na):
            copy(a, k_from, (*relay_from, c), me).wait_recv()
            sent += [copy(a, 3, (*relay_from, c), (*relay_to, c)), copy(a, 3 + k_from, (*relay_from, c), sibling)]
            sent[-2].start()
            sent[-1].start()
        for a in range(na):
            copy(a, k_to, (*relay_to, c), me).wait_recv()
            sent.append(copy(a, 3 + k_to, (*relay_to, c), sibling))
            sent[-1].start()
        for a in range(na):
            copy(a, 3, (*far, c), me).wait_recv()
            sent.append(copy(a, 6, (*far, c), sibling))
            sent[-1].start()
        for a in range(na):
            copy(a, 0, sibling, me).wait_recv()
            for j, chip in enumerate(near + [far]):
                copy(a, 4 + j, (*chip, 1 - c), me).wait_recv()
        for cp in sent:
            cp.wait_send()
        for cp in mine:
            cp.wait()

    hbm = pl.BlockSpec(memory_space=pl.ANY)
    gathered = [jax.ShapeDtypeStruct((N_DEV * m, n), out_dtype) for m, n in shapes]
    out = pl.pallas_call(
        body, name=name,
        in_specs=[hbm] * (na + nb),
        out_specs=[hbm] * (na + nb),
        out_shape=gathered,
        scratch_shapes=([pltpu.VMEM(s, out_dtype) for s in shapes]
                        + [pltpu.VMEM(s, a.dtype) for s, (a, _) in zip(shapes, every)]
                        + [pltpu.SemaphoreType.DMA((na, 7)), pltpu.SemaphoreType.DMA((na, 7)),
                           pltpu.SemaphoreType.DMA((na + nb,)), pltpu.SemaphoreType.DMA((na + nb,))]),
        compiler_params=pltpu.CompilerParams(vmem_limit_bytes=VMEM_LIMIT, collective_id=COLLECTIVE_GATHER_W0),
    )(*[a for a, _ in every])
    return out[:na], out[na:]


ALL_PEERS = tuple(range(1, N_DEV))


def _related(k, x, y, c):
    return x ^ ((k >> 2) & 1), y ^ ((k >> 1) & 1), c ^ (k & 1)


def _gather_wait(sems, block, land, relations, after, name):
    def body(src, land_ref, send_sem, recv_sem, after_ref, src_out, land_out):
        x, y, c = _mesh_pos()
        for k in relations:
            peer = _related(k, x, y, c)
            cp = pltpu.make_async_remote_copy(
                src_ref=src, dst_ref=_device_rows(land_ref, block.shape[0], *peer),
                send_sem=send_sem.at[k - 1], recv_sem=recv_sem.at[k - 1], device_id=peer, device_id_type=MESH)
            cp.wait_send()
            cp.wait_recv()

    out = pl.pallas_call(
        body, name=name,
        out_shape=(pltpu.HBM(block.shape, block.dtype), pltpu.HBM(land.shape, land.dtype)),
        in_specs=[_HBM, _HBM, _SEM, _SEM, pl.BlockSpec(memory_space=pl.ANY)],
        out_specs=[_HBM, _HBM],
        input_output_aliases={0: 0, 1: 1},
        compiler_params=pltpu.CompilerParams(has_side_effects=_EFFECT),
    )(block, land, sems[0], sems[1], after)
    return out[1]


(COLLECTIVE_GATHER_W0, COLLECTIVE_GATHER_W1, COLLECTIVE_EXCHANGE_1, COLLECTIVE_EXCHANGE_0A, COLLECTIVE_EXCHANGE_0B,
 COLLECTIVE_GATHER_SMALL) = range(1, 7)
COLLECTIVE_PAIR_SUM = (7, 8)


def _handshake(peers):
    barrier = pltpu.get_barrier_semaphore()
    for peer in peers:
        pl.semaphore_signal(barrier, inc=1, device_id=peer, device_id_type=MESH)
    pl.semaphore_wait(barrier, len(peers))


_HBM = pl.BlockSpec(memory_space=pltpu.HBM)
_SEM = pl.BlockSpec(memory_space=pltpu.SEMAPHORE)
_EFFECT = pltpu.SideEffectType.DATAFLOW_SIDE_EFFECTING


def _exchange_plan(direct):
    x, y, c = _mesh_pos()
    if not direct:
        return [(j, j, (qx, qy, c)) for j, (qx, qy) in enumerate([(1 - x, y), (x, 1 - y), (1 - x, 1 - y)])]
    plan = []
    for k in range(1, N_DEV):
        px, py, pc = x ^ ((k >> 2) & 1), y ^ ((k >> 1) & 1), c ^ (k & 1)
        plan.append((4 * px + 2 * py + pc, k - 1, (px, py, pc)))
    return plan


def _exchange_copies(directs):
    copies, base = [], 0
    for a, direct in enumerate(directs):
        plan = _exchange_plan(direct)
        copies += [(a, block, slot, peer, base + slot) for block, slot, peer in plan]
        base += len(plan)
    return copies, base


def _exchange_start(srcs, directs, collective_id, name):
    na = len(srcs)
    slots = [N_DEV - 1 if direct else 3 for direct in directs]

    def body(*refs):
        src, land = refs[:na], refs[na:2 * na]
        send_sem, recv_sem = refs[2 * na], refs[2 * na + 1]
        token = refs[-1]
        _handshake([peer for _, _, peer in _exchange_plan(any(directs))])
        for a, block, slot, peer, sem in _exchange_copies(directs)[0]:
            pltpu.make_async_remote_copy(
                src_ref=src[a].at[block], dst_ref=land[a].at[slot], send_sem=send_sem.at[sem],
                recv_sem=recv_sem.at[sem], device_id=peer, device_id_type=MESH).start()
        token[...] = jnp.zeros_like(token)

    zones = [jax.ShapeDtypeStruct((n,) + t.shape[1:], t.dtype) for n, t in zip(slots, srcs)]
    bufs = [pltpu.HBM(t.shape, t.dtype) for t in list(srcs) + zones]
    out = pl.pallas_call(
        body, name=name,
        out_shape=(pltpu.SemaphoreType.DMA((sum(slots),)), pltpu.SemaphoreType.DMA((sum(slots),)), *bufs,
                   jax.ShapeDtypeStruct((8, 128), F32)),
        in_specs=[_HBM] * (2 * na),
        out_specs=(_SEM, _SEM, *([_HBM] * (2 * na)), pl.BlockSpec(memory_space=pltpu.VMEM)),
        input_output_aliases={i: 2 + i for i in range(2 * na)},
        compiler_params=pltpu.CompilerParams(has_side_effects=_EFFECT, collective_id=collective_id),
    )(*[pltpu.with_memory_space_constraint(t, pltpu.HBM) for t in srcs],
      *[pltpu.with_memory_space_constraint(lax.empty(t.shape, t.dtype), pltpu.HBM) for t in zones])
    return out[0], out[1], out[2:2 + na], out[2 + na:2 + 2 * na], out[-1]


def _exchange_wait(send_sem, recv_sem, srcs, lands, directs, after, name):
    na = len(srcs)

    def body(*refs):
        src, land = refs[:na], refs[na:2 * na]
        send_sem_ref, recv_sem_ref = refs[2 * na], refs[2 * na + 1]
        for a, block, slot, peer, sem in _exchange_copies(directs)[0]:
            cp = pltpu.make_async_remote_copy(
                src_ref=src[a].at[block], dst_ref=land[a].at[slot], send_sem=send_sem_ref.at[sem],
                recv_sem=recv_sem_ref.at[sem], device_id=peer, device_id_type=MESH)
            cp.wait_send()
            cp.wait_recv()

    bufs = [pltpu.HBM(t.shape, t.dtype) for t in list(srcs) + list(lands)]
    out = pl.pallas_call(
        body, name=name,
        out_shape=tuple(bufs),
        in_specs=[_HBM] * (2 * na) + [_SEM, _SEM, pl.BlockSpec(memory_space=pl.ANY)],
        out_specs=[_HBM] * (2 * na),
        input_output_aliases={i: i for i in range(2 * na)},
        compiler_params=pltpu.CompilerParams(has_side_effects=_EFFECT),
    )(*srcs, *lands, send_sem, recv_sem, after)
    return out[:na], out[na:]


def _own_then_slots(mine_ref, lands_ref, rows=slice(None)):
    if len(mine_ref.shape) == 3:
        x, y, c = _mesh_pos()
        total = mine_ref[4 * x + 2 * y + c, rows, :].astype(F32)
    else:
        total = mine_ref[rows, :].astype(F32)
    for j in range(lands_ref.shape[0]):
        total = total + lands_ref[j, rows, :].astype(F32)
    return total


SMALL_ROWS = 2 * PACK_SLICE + 2 * 8


def _small_gather_start(mine, lands, dgpre, name):
    def body(*refs):
        hm, ld, dg = refs[:DEPTH], refs[DEPTH:2 * DEPTH], refs[2 * DEPTH:3 * DEPTH]
        send_sem, recv_sem, blk, land, token, own, slots, rows, built, local_sems = refs[3 * DEPTH:]
        x, y, c = _mesh_pos()
        loads = []
        for l in range(DEPTH):
            loads += [pltpu.make_async_copy(hm[l].at[4 * x + 2 * y + c], own.at[l], local_sems.at[3 * l]),
                      pltpu.make_async_copy(ld[l], slots.at[l], local_sems.at[3 * l + 1]),
                      pltpu.make_async_copy(dg[l], rows.at[l], local_sems.at[3 * l + 2])]
        for cp in loads:
            cp.start()
        _handshake([_related(k, x, y, c) for k in ALL_PEERS])
        for cp in loads:
            cp.wait()
        for l in range(DEPTH):
            total = own[l]
            for j in range(N_DEV - 1):
                total = total + slots[l, j]
            built[PACK_SLICE * l:PACK_SLICE * (l + 1), :] = total
            built[2 * PACK_SLICE + 8 * l:2 * PACK_SLICE + 8 * (l + 1), :] = rows[l]
        stores = [pltpu.make_async_copy(built, blk, local_sems.at[3 * DEPTH]),
                  pltpu.make_async_copy(built, _device_rows(land, SMALL_ROWS, x, y, c), local_sems.at[3 * DEPTH + 1])]
        for cp in stores:
            cp.start()
        for cp in stores:
            cp.wait()
        for k in ALL_PEERS:
            pltpu.make_async_remote_copy(
                src_ref=blk, dst_ref=_device_rows(land, SMALL_ROWS, x, y, c), send_sem=send_sem.at[k - 1],
                recv_sem=recv_sem.at[k - 1], device_id=_related(k, x, y, c), device_id_type=MESH).start()
        token[...] = jnp.zeros_like(token)

    hbm = pl.BlockSpec(memory_space=pl.ANY)
    out = pl.pallas_call(
        body, name=name,
        in_specs=[hbm] * (3 * DEPTH),
        out_specs=(_SEM, _SEM, _HBM, _HBM, pl.BlockSpec(memory_space=pltpu.VMEM)),
        out_shape=(pltpu.SemaphoreType.DMA((N_DEV - 1,)), pltpu.SemaphoreType.DMA((N_DEV - 1,)),
                   pltpu.HBM((SMALL_ROWS, 128), F32), pltpu.HBM((N_DEV * SMALL_ROWS, 128), F32),
                   jax.ShapeDtypeStruct((8, 128), F32)),
        scratch_shapes=[pltpu.VMEM((DEPTH, PACK_SLICE, 128), F32), pltpu.VMEM((DEPTH, N_DEV - 1, PACK_SLICE, 128), F32),
                        pltpu.VMEM((DEPTH, 8, 128), F32), pltpu.VMEM((SMALL_ROWS, 128), F32),
                        pltpu.SemaphoreType.DMA((3 * DEPTH + 2,))],
        compiler_params=pltpu.CompilerParams(has_side_effects=_EFFECT, collective_id=COLLECTIVE_GATHER_SMALL),
    )(*mine, *lands, *dgpre)
    return (out[0], out[1]), out[2], out[3], out[4]


def _adamw_math(w, g, m, v):
    m = ADAM_B1 * m + (1.0 - ADAM_B1) * g
    v = ADAM_B2 * v + (1.0 - ADAM_B2) * (g * g)
    m_hat = m / (1.0 - ADAM_B1 ** ADAM_STEP)
    v_hat = v / (1.0 - ADAM_B2 ** ADAM_STEP)
    delta = -ADAM_LR * (m_hat / (jnp.sqrt(v_hat) + ADAM_EPS) + ADAM_WD * w)
    return delta, m, v


def _adamw_layer(layer, parts, token, name):
    steps = {w.shape[1] // rows for _, _, w, _, _, _, rows in parts}
    assert len(steps) == 1, steps
    n = len(parts)

    def body(_, *refs):
        for p in range(n):
            hm_ref, ld_ref, w_ref, m_ref, v_ref = refs[5 * p:5 * (p + 1)]
            g_ref, d_ref, nm_ref, nv_ref = refs[len(refs) - 4 * (n - p):len(refs) - 4 * (n - p - 1)]
            g = _own_then_slots(hm_ref, ld_ref)
            g_ref[...] = g
            d, nm, nv = _adamw_math(w_ref[...], g, m_ref[...], v_ref[...])
            d_ref[...] = d
            nm_ref[...] = nm
            nv_ref[...] = nv

    in_specs, out_specs, out_shape, operands, carried, aliases = [pl.BlockSpec(memory_space=pl.ANY)], [], [], [], [], {}
    for p, (mine, lands, w, m, v, earlier, rows) in enumerate(parts):
        nn = w.shape[2]
        spec = pl.BlockSpec((None, rows, nn), lambda i: (layer, i, 0))
        in_specs += [pl.BlockSpec((rows, nn), lambda i: (i, 0)) if mine.ndim == 2
                     else pl.BlockSpec((N_DEV, rows, nn), lambda i: (0, i, 0)),
                     pl.BlockSpec((lands.shape[0], rows, nn), lambda i: (0, i, 0)), spec, spec, spec]
        out_specs += [spec] * 4
        out_shape += [jax.ShapeDtypeStruct(w.shape, F32)] * 4
        operands += [mine, lands, w, m, v]
        if earlier is not None:
            aliases.update({1 + 5 * n + len(carried) + t: 4 * p + t for t in range(4)})
            carried += list(earlier)
    out = pl.pallas_call(
        body, name=name, grid=(steps.pop(),),
        in_specs=in_specs + [pl.BlockSpec(memory_space=pl.ANY)] * len(carried),
        out_specs=out_specs, out_shape=out_shape, input_output_aliases=aliases,
        compiler_params=_compiler_params(("arbitrary",)),
    )(token, *operands, *carried)
    return [out[4 * p:4 * (p + 1)] for p in range(n)]


def _adamw_small(gathered, params):
    def body(all_ref, *refs):
        ins, outs, packs = refs[:15], refs[15:15 + 21], refs[15 + 21]
        loss_ref = outs[0]
        for dev in range(N_DEV):
            for l in range(DEPTH):
                packs[l, PACK_SLICE * dev:PACK_SLICE * (dev + 1), :] = (
                    all_ref[SMALL_ROWS * dev + PACK_SLICE * l:SMALL_ROWS * dev + PACK_SLICE * (l + 1), :])
        loss_ref[...] = packs[DEPTH - 1, ROW_LOSS:ROW_LOSS + 1, 0:1]

        def update(p, sel, g):
            w_ref, m_ref, v_ref = ins[p], ins[5 + p], ins[10 + p]
            d, nm, nv = _adamw_math(w_ref[sel], g, m_ref[sel], v_ref[sel])
            for t, val in enumerate((g, d, nm, nv)):
                outs[1 + 5 * t + p][sel] = val

        for l in range(DEPTH):
            gp = packs.at[l]
            row0 = 2 * PACK_SLICE + 8 * l
            dgpre = all_ref[row0:row0 + 8, :]
            for dev in range(1, N_DEV):
                dgpre = dgpre + all_ref[SMALL_ROWS * dev + row0:SMALL_ROWS * dev + row0 + 8, :]
            for grp in range(4):
                update(0, (l, grp), gp[ROW_PW + BLOCK * grp:ROW_PW + BLOCK * (grp + 1), :])
                update(1, (slice(l, l + 1), slice(128 * grp, 128 * (grp + 1))), gp[ROW_SC + grp:ROW_SC + grp + 1, :])
            update(2, (slice(l, l + 1), slice(None)), gp[ROW_SINK:ROW_SINK + 1, 0:N_HEADS])
            for r in range(D_MODEL // 128):
                sel = (slice(l, l + 1), slice(128 * r, 128 * (r + 1)))
                update(3, sel, dgpre[r:r + 1, :])
                update(4, sel, gp[ROW_NPOST + r:ROW_NPOST + r + 1, :])

    shapes = [jax.ShapeDtypeStruct(p.shape, F32) for p in params[:5]]
    return pl.pallas_call(
        body, name="adamw_small",
        out_shape=[jax.ShapeDtypeStruct((1, 1), F32)] + shapes * 4,
        scratch_shapes=[pltpu.VMEM((DEPTH, PACK_ROWS, 128), F32)],
        compiler_params=_compiler_params(),
    )(gathered, *params)


def kernel(x, w_in, pool_w, pool_scale, attn_sinks, w_out, norm_pre, norm_post, loss_target, m_w_in, m_pool_w, m_pool_scale, m_attn_sinks, m_w_out, m_norm_pre, m_norm_post, v_w_in, v_pool_w, v_pool_scale, v_attn_sinks, v_w_out, v_norm_pre, v_norm_post):
    x0 = x.reshape(SEQ, D_MODEL)
    target = loss_target.reshape(SEQ, D_MODEL)
    bias = jnp.asarray(_attn_bias())
    w_in_t, m_in_t, v_in_t = (jnp.swapaxes(t, 1, 2) for t in (w_in, m_w_in, v_w_in))

    (win0, wout0), lands = _allgather([(w_in_t, 0), (w_out, 0)], BF16, "gather_w0",
                                              later=[(w_in_t, 1), (w_out, 1)])
    win_full, wout_full = [win0, None], [wout0, None]
    token = x0

    saved = []
    xl = x0
    for layer in range(DEPTH):
        front = (layer, xl, norm_pre, win_full[layer], token, attn_sinks, pool_w, pool_scale, bias)
        if layer == 0:
            u, pg, q, k, v, ag, z, a, x_next, y, win_full[1], wout_full[1] = _fwd_front(
                *front, out=(norm_post, wout_full[layer]), gather=lands)
        else:
            u, pg, q, k, v, ag, z, a, x_next, y = _fwd_front(*front, out=(norm_post, wout_full[layer]))
        saved.append((xl, u, pg, q, k, v, ag, z, a, y))
        xl = x_next

    params_small = [pool_w, pool_scale, attn_sinks, norm_pre, norm_post,
                    m_pool_w, m_pool_scale, m_attn_sinks, m_norm_pre, m_norm_post,
                    v_pool_w, v_pool_scale, v_attn_sinks, v_norm_pre, v_norm_post]

    def start(srcs, directs, paired, collective_id, tag):
        send_sem, recv_sem, srcs, lands, started = _exchange_start(srcs, directs, collective_id, f"exchange_start{tag}")
        return (send_sem, recv_sem, srcs, lands, paired, directs), started

    def finish(handle, after, tag):
        send_sem, recv_sem, srcs, lands, paired, directs = handle
        srcs, lands = _exchange_wait(send_sem, recv_sem, srcs, lands, directs, after, f"exchange_wait{tag}")
        return [s if p is None else p for s, p in zip(srcs, paired)], lands

    def back(layer, top, first, second):
        xin, u, pg, q, k, v, ag, z, a, y = saved[layer]
        return _bwd_back(layer, top, first, second, y, z, norm_post, wout_full[layer], attn_sinks, u, pg, q, k, v,
                         ag, a, pool_w, pool_scale, bias)

    dgpre = [None] * DEPTH
    dx, dproj, gw_out, pack = back(1, True, xl, target)
    dx, dgpre[1], chip_sums, own_sum = _bwd_in(1, "both_pair", token, dproj, saved[1][0], norm_pre, dx, win_full[1])
    top, token = start([chip_sums, gw_out.reshape(N_DEV, OUT_SHARD, D_MODEL), pack.reshape(N_DEV, PACK_SLICE, 128)],
                       [False, True, True], [own_sum, None, None], COLLECTIVE_EXCHANGE_1, "1")

    dproj, gw_out, pack = back(0, False, dx, token)
    early, token = start([gw_out.reshape(N_DEV, OUT_SHARD, D_MODEL), pack.reshape(N_DEV, PACK_SLICE, 128)],
                         [True, True], [None, None], COLLECTIVE_EXCHANGE_0A, "0a")
    chip_sums, own_sum = _bwd_in(0, "dw_pair", token, dproj, saved[0][0], norm_pre)
    own1, lands1 = finish(top, chip_sums, "1")
    late, token = start([chip_sums], [False], [own_sum], COLLECTIVE_EXCHANGE_0B, "0b")
    dx, dgpre[0], *updated = _bwd_in(0, "dx", token, dproj, saved[0][0], norm_pre, dx, win_full[0],
                                     adam=(1, [(own1[0], lands1[0], w_in_t, m_in_t, v_in_t, None, ADAM_ROWS_IN),
                                               (own1[1], lands1[1], w_out, m_w_out, v_w_out, None, ADAM_ROWS_OUT)]))
    big_in, big_out = updated[:4], updated[4:]

    own0a, lands0a = finish(early, dx, "0a")
    sems, block, land, token = _small_gather_start([own0a[1], own1[2]], [lands0a[1], lands1[2]], dgpre,
                                                   "gather_small_start")
    own0b, lands0b = finish(late, token, "0b")
    big_in, big_out = _adamw_layer(0, [(own0b[0], lands0b[0], w_in_t, m_in_t, v_in_t, big_in, ADAM_ROWS_IN),
                                       (own0a[0], lands0a[0], w_out, m_w_out, v_w_out, big_out, ADAM_ROWS_OUT)],
                                   token, "adamw0")
    gathered = _gather_wait(sems, block, land, ALL_PEERS, big_in[0], "gather_small_wait")
    small_out = _adamw_small(gathered, params_small)
    loss = small_out[0].reshape(())

    outs = [loss, dx.reshape(1, SEQ, D_MODEL)]
    for t in range(4):
        pw_, sc_, sk_, npre_, npost_ = small_out[1 + 5 * t:6 + 5 * t]
        outs += [jnp.swapaxes(big_in[t], 1, 2), pw_, sc_, sk_, big_out[t], npre_, npost_]
    return tuple(outs)
```

```python
import functools
import numpy as np
import jax
import jax.numpy as jnp
from jax import lax
from jax.experimental import pallas as pl
from jax.experimental.pallas import tpu as pltpu

F32 = jnp.float32
BF16 = jnp.bfloat16

N_DEV = 8
SEQ = 2048
D_MODEL = 1024
D_POOL = 512
D_ATTN = 512
D_KV = 128
D_IN = 2304
N_HEADS = 8
GQA = 4
HEAD_DIM = 64
BLOCK = 128
POOL_WINDOWS = (2, 4, 8, 16)
DEPTH = 2
EPS = 1e-6
NEG_INF = -1e30
SCALE = HEAD_DIM ** -0.5
IN_SHARD = D_IN // N_DEV
OUT_SHARD = D_MODEL // N_DEV

COL_U, COL_PG, COL_Q, COL_K, COL_V, COL_AG = 0, 512, 1024, 1536, 1664, 1792

ADAM_LR = 0.001
ADAM_B1 = 0.9
ADAM_B2 = 0.999
ADAM_EPS = 1e-08
ADAM_WD = 0.01
ADAM_STEP = 10

TOKEN_TILE = 512
ADAM_ROWS_IN, ADAM_ROWS_OUT = 144, 64
ADAM_CHUNK_IN, ADAM_CHUNK_OUT = 48, 32
VMEM_LIMIT = 56 * 1024 * 1024
MESH = pl.DeviceIdType.MESH

ROW_PW, ROW_SC, ROW_SINK, ROW_NPOST, ROW_LOSS = 0, 512, 520, 536, 544
PACK_ROWS = 576
PACK_SLICE = PACK_ROWS // N_DEV


def _nn(a, b):
    return jnp.dot(a, b, preferred_element_type=F32)


def _nt(a, b):
    return lax.dot_general(a, b, (((1,), (1,)), ((), ())), preferred_element_type=F32)


def _tn(a, b):
    return lax.dot_general(a, b, (((0,), (0,)), ((), ())), preferred_element_type=F32)


def _silu_parts(g):
    s = jax.nn.sigmoid(g)
    return g * s, s * (1.0 + g * (1.0 - s))


def _resident(shape):
    return pl.BlockSpec(shape, lambda *_: (0,) * len(shape), pipeline_mode=pl.Buffered(1))


def _compiler_params(sem=None):
    if sem is None:
        return pltpu.CompilerParams(vmem_limit_bytes=VMEM_LIMIT)
    return pltpu.CompilerParams(dimension_semantics=sem, vmem_limit_bytes=VMEM_LIMIT)


def _attn_bias():
    t = np.arange(BLOCK)[None, :]
    j = np.arange(BLOCK)[:, None]
    current = j <= t
    dist = np.where(current, t - j, t + BLOCK - j).astype(np.float32)
    out = np.zeros((2, 2, BLOCK, GQA * BLOCK), np.float32)
    for variant in range(2):
        valid = current | (variant == 1)
        for kv in range(2):
            for g in range(GQA):
                slope = np.float32(2.0 ** (-(kv * GQA + g + 1)))
                out[variant, kv, :, g * BLOCK:(g + 1) * BLOCK] = np.where(valid, -slope * dist, np.float32(NEG_INF))
    return out


def _replicate_head(kx, kv):
    rolled = pltpu.roll(kx, 64, 1)
    lane = lax.broadcasted_iota(jnp.int32, kx.shape, 1)
    dup = jnp.where(lane < 64, kx, rolled) if kv == 0 else jnp.where(lane < 64, rolled, kx)
    return jnp.concatenate([dup, dup], axis=1).astype(BF16)


def _stack_heads(qv):
    lane = lax.broadcasted_iota(jnp.int32, qv.shape, 1)
    zero = jnp.zeros_like(qv)
    return jnp.concatenate([jnp.where((lane >= 64 * g) & (lane < 64 * g + 64), qv, zero) for g in range(GQA)], axis=0)


def _unstack_heads(xs):
    lane = lax.broadcasted_iota(jnp.int32, (BLOCK, 256), 1)
    return jnp.where(lane < 64, xs[0:128], jnp.where(lane < 128, xs[128:256], jnp.where(lane < 192, xs[256:384], xs[384:512])))


def _fold_heads(r):
    h = r[:, 0:128] + r[:, 128:256]
    return h + pltpu.roll(h, 64, 1)


def _sink_row(sink_ref, layer, kv):
    lane = lax.broadcasted_iota(jnp.int32, (1, GQA * BLOCK), 1)
    s4 = [sink_ref[layer, kv * GQA + g] for g in range(GQA)]
    return jnp.where(lane < 128, s4[0], jnp.where(lane < 256, s4[1], jnp.where(lane < 384, s4[2], s4[3])))


def _band_is_current():
    j = lax.broadcasted_iota(jnp.int32, (BLOCK, GQA * BLOCK), 0)
    t = lax.broadcasted_iota(jnp.int32, (BLOCK, GQA * BLOCK), 1) & (BLOCK - 1)
    return j <= t


def _pack_band(full, current):
    return jnp.where(current, full[BLOCK:], full[:BLOCK])


def _unpack_band(packed, current):
    zero = jnp.zeros_like(packed)
    return jnp.concatenate([jnp.where(current, zero, packed), jnp.where(current, packed, zero)], axis=0)


def _probs_keys_major(k_rep, q_st, bias, sink, current):
    st = _pack_band(_nt(k_rep, q_st), current) * SCALE + bias
    m = jnp.maximum(jnp.max(st, axis=0, keepdims=True), sink)
    p = jnp.exp(st - m)
    esink = jnp.exp(sink - m)
    rl = 1.0 / (jnp.sum(p, axis=0, keepdims=True) + esink)
    return p * rl, esink * rl


WINDOW_HALO = 16


def _window_sum(ext, w, forward):
    s = ext
    sh = 1
    while sh < w:
        s = s + pltpu.roll(s, (ext.shape[0] - sh) if forward else sh, 0)
        sh *= 2
    return s


def _inv_count(n, w):
    t = n * BLOCK + lax.broadcasted_iota(jnp.int32, (BLOCK, 1), 0) + 1
    return 1.0 / jnp.minimum(t.astype(F32), float(w))


def _kv_ext(ref, n):
    r0 = pl.multiple_of(jnp.maximum(n - 1, 0) * BLOCK, BLOCK)
    r1 = pl.multiple_of(n * BLOCK, BLOCK)
    return jnp.concatenate([ref[pl.ds(r0, BLOCK), :], ref[pl.ds(r1, BLOCK), :]], axis=0)


def _rows_of(vec_ref, pack_ref, row0):
    for r in range(D_MODEL // 128):
        pack_ref[row0 + r:row0 + r + 1, :] = vec_ref[:, 128 * r:128 * (r + 1)]


FRONT_TILE = 4 * BLOCK


def _fwd_front(layer, x, norm_pre, w_in_t, token, sinks, pool_w, pool_scale, bias, out=None, gather=None):
    tm = FRONT_TILE

    def body(sink_ref, x_ref, g_ref, w_ref, _, pw_ref, sc_ref, bias_ref, *refs):
        if out is not None:
            gpost_ref, wo_ref, *refs = refs
        if gather is not None:
            refs = refs[2:]
            zones, (send_sems, recv_sems), refs = refs[10:12], refs[-2:], refs[:-2]
        if out is not None:
            xn_ref, y_ref = refs[8:10]
        u_ref, pg_ref, q_ref, k_ref, v_ref, ag_ref, z_ref, a_ref = refs[:8]
        uprev, kprev, vprev = refs[-3:]
        i = pl.program_id(0)

        @pl.when(i == 0)
        def _():
            uprev[...] = jnp.zeros_like(uprev)
            kprev[...] = jnp.zeros_like(kprev)
            vprev[...] = jnp.zeros_like(vprev)

        if gather is not None:
            px, py, pc = _mesh_pos()
            me, sibling = (px, py, pc), (px, py, 1 - pc)
            near = [(1 - px, py), (px, 1 - py)]
            far = (1 - px, 1 - py)
            relay_from, relay_to = (px ^ (1 - pc), py ^ pc), (px ^ pc, py ^ (1 - pc))
            k_from, k_to = 1 + pc, 2 - pc

            def copy(a, sem, block, to):
                rows_of_block = _device_rows(zones[a], gather[a].shape[0] // N_DEV, *block)
                return pltpu.make_async_remote_copy(
                    src_ref=rows_of_block, dst_ref=rows_of_block, send_sem=send_sems.at[a, sem],
                    recv_sem=recv_sems.at[a, sem], device_id=to, device_id_type=MESH)

            def pass_on(a):
                copy(a, k_from, (*relay_from, pc), me).wait_recv()
                copy(a, 3, (*relay_from, pc), (*relay_to, pc)).start()
                copy(a, 3 + k_from, (*relay_from, pc), sibling).start()
                copy(a, k_to, (*relay_to, pc), me).wait_recv()
                copy(a, 3 + k_to, (*relay_to, pc), sibling).start()

            @pl.when(i == 0)
            def _():
                _handshake([sibling] + [(*chip, pc) for chip in near])
                for a in range(2):
                    copy(a, 0, me, sibling).start()
                    for j, chip in enumerate(near):
                        copy(a, 1 + j, me, (*chip, pc)).start()

            for a in range(2):
                pl.when(i == 2 + a)(functools.partial(pass_on, a))

        xv = x_ref[...]
        r = lax.rsqrt(jnp.mean(xv * xv, axis=-1, keepdims=True) + EPS)
        h = (xv * r * g_ref[layer:layer + 1, :]).astype(BF16)
        u_ref[...] = _nt(h, w_ref[COL_U:COL_PG, :])
        pg_ref[...] = _nt(h, w_ref[COL_PG:COL_Q, :])
        for sb in range(tm // BLOCK):
            n = (tm // BLOCK) * i + sb
            rows = slice(BLOCK * sb, BLOCK * (sb + 1))
            before = slice(BLOCK * (sb - 1), BLOCK * sb)
            uv = u_ref[rows, :]
            halo = (uprev[BLOCK - WINDOW_HALO:, :] if sb == 0
                    else u_ref[BLOCK * sb - WINDOW_HALO:BLOCK * sb, :])
            ext = jnp.concatenate([halo, uv], axis=0)
            for g, w in enumerate(POOL_WINDOWS):
                cs = slice(BLOCK * g, BLOCK * (g + 1))
                win = _window_sum(ext[:, cs], w, forward=False)[WINDOW_HALO:]
                pooled = win * _inv_count(n, w) - uv[:, cs]
                mixed = _nn(pooled.astype(BF16), pw_ref[g].astype(BF16))
                gate, _ = _silu_parts(pg_ref[rows, cs])
                z_ref[rows, cs] = (mixed * sc_ref[layer:layer + 1, cs] * gate).astype(BF16)

        q_ref[...] = _nt(h, w_ref[COL_Q:COL_K, :]).astype(BF16)
        k_ref[...] = _nt(h, w_ref[COL_K:COL_V, :])
        v_ref[...] = _nt(h, w_ref[COL_V:COL_AG, :])
        ag_ref[...] = _nt(h, w_ref[COL_AG:D_IN, :])

        current = _band_is_current()
        for sb in range(tm // BLOCK):
            n = (tm // BLOCK) * i + sb
            rows = slice(BLOCK * sb, BLOCK * (sb + 1))
            before = slice(BLOCK * (sb - 1), BLOCK * sb)
            kx = jnp.concatenate([kprev[...] if sb == 0 else k_ref[before, :], k_ref[rows, :]], axis=0)
            vx = jnp.concatenate([vprev[...] if sb == 0 else v_ref[before, :], v_ref[rows, :]], axis=0)
            variant = jnp.minimum(n, 1) if sb == 0 else 1
            for kv in range(2):
                cs = slice(256 * kv, 256 * (kv + 1))
                p, _ = _probs_keys_major(_replicate_head(kx, kv), _stack_heads(q_ref[rows, cs]),
                                         bias_ref[variant, kv], _sink_row(sink_ref, layer, kv), current)
                o = _unstack_heads(_tn(_unpack_band(p.astype(BF16), current), _replicate_head(vx, kv)))
                a_ref[rows, cs] = o
                gate, _ = _silu_parts(ag_ref[rows, cs])
                z_ref[rows, D_POOL + 256 * kv:D_POOL + 256 * (kv + 1)] = (o * gate).astype(BF16)

        tail = slice(tm - BLOCK, tm)
        uprev[...] = u_ref[tail, :]
        kprev[...] = k_ref[tail, :]
        vprev[...] = v_ref[tail, :]

        if out is not None:
            y = _nn(z_ref[...], wo_ref[...])
            y_ref[...] = y
            r = lax.rsqrt(jnp.mean(y * y, axis=-1, keepdims=True) + EPS)
            xn_ref[...] = x_ref[...] + y * r * gpost_ref[layer:layer + 1, :]

        if gather is not None:
            @pl.when(i == SEQ // tm - 1)
            def _():
                for a in range(2):
                    copy(a, 3, (*far, pc), me).wait_recv()
                    copy(a, 6, (*far, pc), sibling).start()
                for a in range(2):
                    copy(a, 0, sibling, me).wait_recv()
                    for j, chip in enumerate(near + [far]):
                        copy(a, 4 + j, (*chip, 1 - pc), me).wait_recv()
                for a in range(2):
                    copy(a, 0, me, sibling).wait_send()
                    for j, chip in enumerate(near):
                        copy(a, 1 + j, me, (*chip, pc)).wait_send()
                    copy(a, 3, (*relay_from, pc), (*relay_to, pc)).wait_send()
                    copy(a, 3 + k_from, (*relay_from, pc), sibling).wait_send()
                    copy(a, 3 + k_to, (*relay_to, pc), sibling).wait_send()
                    copy(a, 6, (*far, pc), sibling).wait_send()

    row = lambda c: pl.BlockSpec((tm, c), lambda i: (i, 0))
    const = lambda shape: pl.BlockSpec(shape, lambda i: (0,) * len(shape))
    act = jax.ShapeDtypeStruct((SEQ, D_MODEL), F32)
    fused = out is not None
    hbm = pl.BlockSpec(memory_space=pl.ANY)
    zones, params = [], _compiler_params(("arbitrary",))
    if gather is not None:
        assert fused and SEQ // tm == 4, "the gather's copies are spread over four grid steps"
        zones = list(gather)
        params = pltpu.CompilerParams(dimension_semantics=("arbitrary",), vmem_limit_bytes=VMEM_LIMIT,
                                      collective_id=COLLECTIVE_GATHER_W1)
    return pl.pallas_call(
        body, name=f"fwd_front{layer}", grid=(SEQ // tm,),
        in_specs=[pl.BlockSpec(memory_space=pltpu.SMEM), row(D_MODEL), const((DEPTH, D_MODEL)),
                  _resident((D_IN, D_MODEL)), const((8, 128)),
                  pl.BlockSpec((None, 4, BLOCK, BLOCK), lambda i: (layer, 0, 0, 0)), const((DEPTH, D_POOL)),
                  _resident((2, 2, BLOCK, GQA * BLOCK))]
                 + ([const((DEPTH, D_MODEL)), _resident((D_MODEL, D_MODEL))] if fused else []) + [hbm] * len(zones),
        out_specs=[row(D_POOL), row(D_POOL), row(D_ATTN), row(D_KV), row(D_KV), row(D_ATTN), row(D_MODEL),
                   row(D_ATTN)] + ([row(D_MODEL)] * 2 if fused else []) + [hbm] * len(zones),
        out_shape=[jax.ShapeDtypeStruct((SEQ, D_POOL), F32), jax.ShapeDtypeStruct((SEQ, D_POOL), F32),
                   jax.ShapeDtypeStruct((SEQ, D_ATTN), BF16), jax.ShapeDtypeStruct((SEQ, D_KV), F32),
                   jax.ShapeDtypeStruct((SEQ, D_KV), F32), jax.ShapeDtypeStruct((SEQ, D_ATTN), F32),
                   jax.ShapeDtypeStruct((SEQ, D_MODEL), BF16), jax.ShapeDtypeStruct((SEQ, D_ATTN), F32)]
                  + ([act] * 2 if fused else []) + [jax.ShapeDtypeStruct(t.shape, t.dtype) for t in zones],
        scratch_shapes=[pltpu.VMEM((BLOCK, D_POOL), F32), pltpu.VMEM((BLOCK, D_KV), F32),
                        pltpu.VMEM((BLOCK, D_KV), F32)]
                       + ([pltpu.SemaphoreType.DMA((2, 7)), pltpu.SemaphoreType.DMA((2, 7))] if zones else []),
        input_output_aliases={10 + a: 10 + a for a in range(len(zones))},
        compiler_params=params,
    )(sinks, x, norm_pre, w_in_t, token, pool_w, pool_scale, bias, *(out if fused else ()), *zones)


BACK_TILE = 2 * BLOCK


def _bwd_back(layer, top, dxo_or_xf, target_or_token, y, z, norm_post, w_out, sinks, u, pg, q, k, v, ag, a,
              pool_w, pool_scale, bias):
    tm = BACK_TILE
    steps = SEQ // tm
    last = steps - 1
    per = tm // BLOCK

    def body(*refs):
        refs = list(refs)
        sink_ref, first, second = refs[:3]
        (y_ref, z_ref, g_ref, w_ref, u_ref, up_ref, pg_ref, q_ref, k_ref, v_ref, ag_ref, a_ref, pw_ref, sc_ref,
         bias_ref) = refs[3:18]
        del refs[:18]
        dxo_ref = refs.pop(0) if top else None
        dp_ref, dw_ref, pack_ref, acc, dg, lacc, dzs, ck, cv, ce = refs
        i = pl.program_id(0)
        blk = last - i

        @pl.when(i == 0)
        def _():
            acc[...] = jnp.zeros_like(acc)
            dg[...] = jnp.zeros_like(dg)
            lacc[...] = jnp.zeros_like(lacc)
            pack_ref[...] = jnp.zeros_like(pack_ref)
            ck[...] = jnp.zeros_like(ck)
            cv[...] = jnp.zeros_like(cv)
            ce[...] = jnp.zeros_like(ce)

        if top:
            d = first[...] - second[...]
            dxo_v = d * (1.0 / D_MODEL)
            dxo_ref[...] = dxo_v
            part = jnp.sum(d * d, axis=-1, keepdims=True) * (1.0 / D_MODEL)
            lacc[...] += 0.5 * jnp.sum(part, axis=0, keepdims=True)
        else:
            dxo_v = first[...]
        yv = y_ref[...]
        r = lax.rsqrt(jnp.mean(yv * yv, axis=-1, keepdims=True) + EPS)
        yn = yv * r
        dg[...] += jnp.sum(dxo_v * yn, axis=0, keepdims=True)
        dyn = dxo_v * g_ref[layer:layer + 1, :]
        dy = (r * (dyn - yn * jnp.mean(dyn * yn, axis=-1, keepdims=True))).astype(BF16)
        dzs[...] = _nt(dy, w_ref[...])
        acc[...] += _tn(z_ref[...], dy)

        lane = lax.broadcasted_iota(jnp.int32, (1, 128), 1)
        lane2 = lax.broadcasted_iota(jnp.int32, (256, 128), 1)
        current = _band_is_current()
        for sb in reversed(range(per)):
            n = per * blk + sb
            rows = slice(BLOCK * sb, BLOCK * (sb + 1))

            uv = u_ref[rows, :]
            if sb == 0:
                halo = up_ref[BLOCK - WINDOW_HALO:, :] * (n > 0).astype(F32)
            else:
                halo = u_ref[BLOCK * sb - WINDOW_HALO:BLOCK * sb, :]
            ext = jnp.concatenate([halo, uv], axis=0)
            for g, w in enumerate(POOL_WINDOWS):
                cs = slice(BLOCK * g, BLOCK * (g + 1))
                inv = _inv_count(n, w)
                win = _window_sum(ext[:, cs], w, forward=False)[WINDOW_HALO:]
                pooled = win * inv - uv[:, cs]
                pw_g = pw_ref[g].astype(BF16)
                mixed = _nn(pooled.astype(BF16), pw_g)
                gate, dgate = _silu_parts(pg_ref[rows, cs])
                dzp = dzs[rows, cs]
                sc = sc_ref[layer:layer + 1, cs]
                dpm = dzp * gate
                dp_ref[rows, COL_PG + BLOCK * g:COL_PG + BLOCK * (g + 1)] = (dzp * (mixed * sc) * dgate).astype(BF16)
                pack_ref[ROW_SC + g:ROW_SC + g + 1, :] += jnp.sum(dpm * mixed, axis=0, keepdims=True)
                dmixed = (dpm * sc).astype(BF16)
                pack_ref[ROW_PW + BLOCK * g:ROW_PW + BLOCK * (g + 1), :] += _tn(pooled.astype(BF16), dmixed)
                dpooled = _nt(dmixed, pw_g)
                e = dpooled * inv
                lead = _window_sum(jnp.concatenate([e, ce[:WINDOW_HALO, cs]], axis=0), w, forward=True)[:BLOCK]
                dp_ref[rows, COL_U + BLOCK * g:COL_U + BLOCK * (g + 1)] = (lead - dpooled).astype(BF16)
                ce[:, cs] = e

            kx = _kv_ext(k_ref, n)
            vx = _kv_ext(v_ref, n)
            variant = jnp.minimum(n, 1) if sb == 0 else 1
            dsink_row = jnp.zeros((1, 128), F32)
            tks, tvs = [], []
            for kv in range(2):
                cs = slice(256 * kv, 256 * (kv + 1))
                k_rep = _replicate_head(kx, kv)
                v_rep = _replicate_head(vx, kv)
                q_st = _stack_heads(q_ref[rows, cs])
                gate, dgate = _silu_parts(ag_ref[rows, cs])
                dza = dzs[rows, D_POOL + 256 * kv:D_POOL + 256 * (kv + 1)]
                dp_ref[rows, COL_AG + 256 * kv:COL_AG + 256 * (kv + 1)] = (dza * a_ref[rows, cs] * dgate).astype(BF16)
                da_st = _stack_heads((dza * gate).astype(BF16))
                p, psink = _probs_keys_major(k_rep, q_st, bias_ref[variant, kv], _sink_row(sink_ref, layer, kv),
                                             current)
                dpt = _pack_band(_nt(v_rep, da_st), current)
                delta = jnp.sum(p * dpt, axis=0, keepdims=True)
                dst = _unpack_band((p * (dpt - delta) * SCALE).astype(BF16), current)
                sink_terms = psink * delta
                for g in range(GQA):
                    dsink = -jnp.sum(sink_terms[:, BLOCK * g:BLOCK * (g + 1)], axis=1, keepdims=True)
                    dsink_row = dsink_row + jnp.where(lane == kv * GQA + g, dsink, 0.0)
                dp_ref[rows, COL_Q + 256 * kv:COL_Q + 256 * (kv + 1)] = _unstack_heads(_tn(dst, k_rep)).astype(BF16)
                tks.append(_fold_heads(_nn(dst, q_st)))
                tvs.append(_fold_heads(_nn(_unpack_band(p.astype(BF16), current), da_st)))
            pack_ref[ROW_SINK:ROW_SINK + 1, :] += dsink_row
            dkx = jnp.where(lane2 < 64, tks[0], tks[1])
            dvx = jnp.where(lane2 < 64, tvs[0], tvs[1])
            dp_ref[rows, COL_K:COL_V] = (ck[...] + dkx[BLOCK:]).astype(BF16)
            dp_ref[rows, COL_V:COL_AG] = (cv[...] + dvx[BLOCK:]).astype(BF16)
            ck[...] = dkx[:BLOCK]
            cv[...] = dvx[:BLOCK]

        @pl.when(i == steps - 1)
        def _():
            dw_ref[...] = acc[...].astype(BF16)
            _rows_of(dg, pack_ref, ROW_NPOST)
            pack_ref[ROW_LOSS:ROW_LOSS + 1, :] = jnp.where(lane == 0, lacc[...], 0.0)

    row = lambda c: pl.BlockSpec((tm, c), lambda i: (last - i, 0))
    const = lambda shape: pl.BlockSpec(shape, lambda i: (0,) * len(shape))
    act = jax.ShapeDtypeStruct((SEQ, D_MODEL), F32)
    return pl.pallas_call(
        body, name=f"bwd_back{layer}", grid=(steps,),
        in_specs=[pl.BlockSpec(memory_space=pltpu.SMEM), row(D_MODEL), row(D_MODEL) if top else const((8, 128)),
                  row(D_MODEL), row(D_MODEL), const((DEPTH, D_MODEL)), _resident((D_MODEL, D_MODEL)),
                  row(D_POOL), pl.BlockSpec((BLOCK, D_POOL), lambda i: (jnp.maximum(per * (last - i) - 1, 0), 0)),
                  row(D_POOL), row(D_ATTN), _resident((SEQ, D_KV)), _resident((SEQ, D_KV)), row(D_ATTN), row(D_ATTN),
                  pl.BlockSpec((None, 4, BLOCK, BLOCK), lambda i: (layer, 0, 0, 0)), const((DEPTH, D_POOL)),
                  _resident((2, 2, BLOCK, GQA * BLOCK))],
        out_specs=([row(D_MODEL)] * (1 if top else 0)
                   + [row(D_IN), const((D_MODEL, D_MODEL)), const((PACK_ROWS, 128))]),
        out_shape=([act] * (1 if top else 0)
                   + [jax.ShapeDtypeStruct((SEQ, D_IN), BF16), jax.ShapeDtypeStruct((D_MODEL, D_MODEL), BF16),
                      jax.ShapeDtypeStruct((PACK_ROWS, 128), F32)]),
        scratch_shapes=[pltpu.VMEM((D_MODEL, D_MODEL), F32), pltpu.VMEM((1, D_MODEL), F32), pltpu.VMEM((1, 1), F32),
                        pltpu.VMEM((tm, D_MODEL), F32), pltpu.VMEM((BLOCK, D_KV), F32), pltpu.VMEM((BLOCK, D_KV), F32),
                        pltpu.VMEM((BLOCK, D_POOL), F32)],
        compiler_params=_compiler_params(("arbitrary",)),
    )(sinks, dxo_or_xf, target_or_token, y, z, norm_post, w_out, u, u, pg, q, k, v, ag, a, pool_w, pool_scale, bias)


def _bwd_in(layer, part, token, dproj, x, norm_pre, dxo=None, w_in_t=None, adam=None):
    pair = part in ("dw_pair", "both_pair")
    want_dw, want_dx = part != "dx", part in ("both", "dx", "both_pair")
    tm = TOKEN_TILE
    steps = SEQ // tm
    adam_layer, adam_parts = adam if adam is not None else (None, [])
    n_adam = len(adam_parts)
    adam_steps = adam_parts[0][2].shape[1] // adam_parts[0][6] if n_adam else 0
    assert part == "dx" or not n_adam
    assert all(w.shape[1] // rows == adam_steps <= steps for _, _, w, _, _, _, rows in adam_parts)
    cw = 256

    def body(*refs):
        refs = list(refs)
        dp_ref, x_ref, g_ref = refs[1:4]
        del refs[:4]
        if want_dx:
            dxo_ref, w_ref = refs[:2]
            adam_in, adam_out = refs[2:2 + 5 * n_adam], refs[4 + 5 * n_adam:4 + 9 * n_adam]
            dx_ref, dgo_ref = refs[2 + 5 * n_adam:4 + 5 * n_adam]
            del refs[:4 + 9 * n_adam]
            dg = refs.pop()
        if pair:
            hs_ref, hm_ref, acc, mine_buf, theirs_buf, send_sem, recv_sem = refs
        elif want_dw:
            dw_ref, acc = refs
        i = pl.program_id(0)

        @pl.when(i == 0)
        def _():
            if pair:
                _handshake([(lax.axis_index("x"), lax.axis_index("y"), 1 - lax.axis_index("c"))])
            if want_dw:
                acc[...] = jnp.zeros_like(acc)
            if want_dx:
                dg[...] = jnp.zeros_like(dg)

        if n_adam:
            @pl.when(i < adam_steps)
            def _():
                for p in range(n_adam):
                    hm_ref, ld_ref, wa_ref, ma_ref, va_ref = adam_in[5 * p:5 * (p + 1)]
                    g = _own_then_slots(hm_ref, ld_ref)
                    for ref, val in zip(adam_out[4 * p:4 * (p + 1)], (g, *_adamw_math(wa_ref[...], g, ma_ref[...], va_ref[...]))):
                        ref[...] = val

        xv = x_ref[...]
        gv = g_ref[layer:layer + 1, :]
        r = lax.rsqrt(jnp.mean(xv * xv, axis=-1, keepdims=True) + EPS)
        xn = xv * r
        if want_dw:
            hb = (xn * gv).astype(BF16)
            for c in range(0, D_IN, cw):
                acc[c:c + cw, :] += _tn(dp_ref[:, c:c + cw], hb)
        def rows_for(q, core):
            return pl.ds(pl.multiple_of((2 * q + core) * IN_SHARD, 8), IN_SHARD)

        def swap(q):
            x, y, c = _mesh_pos()
            return pltpu.make_async_remote_copy(
                src_ref=mine_buf.at[q], dst_ref=theirs_buf.at[q], send_sem=send_sem.at[q], recv_sem=recv_sem.at[q],
                device_id=(x, y, 1 - c), device_id_type=MESH)

        if pair:
            @pl.when(i == steps - 1)
            def _():
                for q in range(4):
                    mine_buf[q] = acc[rows_for(q, 1 - lax.axis_index("c")), :].astype(BF16)
                    swap(q).start()

        if want_dx:
            dh = _nn(dp_ref[...], w_ref[...])
            dg[...] += jnp.sum(dh * xn, axis=0, keepdims=True)
            dhn = dh * gv
            dx_ref[...] = dxo_ref[...] + r * (dhn - xn * jnp.mean(dhn * xn, axis=-1, keepdims=True))

        @pl.when(i == steps - 1)
        def _():
            if pair:
                x, y, c = _mesh_pos()
                for q in range(4):
                    swap(q).wait()
                for j, q in enumerate([2 * (1 - x) + y, 2 * x + (1 - y), 2 * (1 - x) + (1 - y)]):
                    hs_ref[j] = (acc[rows_for(q, c), :] + theirs_buf[q].astype(F32)).astype(BF16)
                hm_ref[...] = acc[rows_for(2 * x + y, c), :] + theirs_buf[2 * x + y].astype(F32)
            elif want_dw:
                dw_ref[...] = acc[...].astype(BF16)
            if want_dx:
                _rows_of(dg, dgo_ref, 0)

    row = lambda c: pl.BlockSpec((tm, c), lambda i: (i, 0))
    const = lambda shape: pl.BlockSpec(shape, lambda i: (0,) * len(shape))
    in_specs = [const((8, 128)), row(D_IN), row(D_MODEL), const((DEPTH, D_MODEL))]
    operands = [token, dproj, x, norm_pre]
    out_specs, out_shape, scratch = [], [], []
    if want_dx:
        in_specs += [row(D_MODEL), _resident((D_IN, D_MODEL))]
        operands += [dxo, w_in_t]
        out_specs += [row(D_MODEL), const((8, 128))]
        out_shape += [jax.ShapeDtypeStruct((SEQ, D_MODEL), F32), jax.ShapeDtypeStruct((8, 128), F32)]
    for mine, lands, w, m, v, _, rows in adam_parts:
        slab = lambda i: jnp.minimum(i, adam_steps - 1)
        spec = pl.BlockSpec((None, rows, w.shape[2]), lambda i: (adam_layer, slab(i), 0))
        in_specs += [pl.BlockSpec((rows, w.shape[2]), lambda i: (slab(i), 0)) if mine.ndim == 2
                     else pl.BlockSpec((N_DEV, rows, w.shape[2]), lambda i: (0, slab(i), 0)),
                     pl.BlockSpec((lands.shape[0], rows, w.shape[2]), lambda i: (0, slab(i), 0)), spec, spec, spec]
        operands += [mine, lands, w, m, v]
        out_specs += [spec] * 4
        out_shape += [jax.ShapeDtypeStruct(w.shape, F32)] * 4
    if pair:
        out_specs += [const((3, IN_SHARD, D_MODEL)), const((IN_SHARD, D_MODEL))]
        out_shape += [jax.ShapeDtypeStruct((3, IN_SHARD, D_MODEL), BF16), jax.ShapeDtypeStruct((IN_SHARD, D_MODEL), F32)]
        scratch += [pltpu.VMEM((D_IN, D_MODEL), F32), pltpu.VMEM((4, IN_SHARD, D_MODEL), BF16),
                    pltpu.VMEM((4, IN_SHARD, D_MODEL), BF16), pltpu.SemaphoreType.DMA((4,)), pltpu.SemaphoreType.DMA((4,))]
    elif want_dw:
        out_specs.append(const((D_IN, D_MODEL)))
        out_shape.append(jax.ShapeDtypeStruct((D_IN, D_MODEL), BF16))
        scratch.append(pltpu.VMEM((D_IN, D_MODEL), F32))
    if want_dx:
        scratch.append(pltpu.VMEM((1, D_MODEL), F32))
    params = pltpu.CompilerParams(dimension_semantics=("arbitrary",), vmem_limit_bytes=VMEM_LIMIT,
                                  collective_id=COLLECTIVE_PAIR_SUM[layer] if pair else None)
    return pl.pallas_call(
        body, name=f"bwd_in_{part}{layer}", grid=(steps,),
        in_specs=in_specs, out_specs=out_specs, out_shape=out_shape, scratch_shapes=scratch,
        compiler_params=params,
    )(*operands)


def _mesh_pos():
    return lax.axis_index("x"), lax.axis_index("y"), lax.axis_index("c")


def _device_rows(ref, m, px, py, pc):
    return ref.at[pl.ds(pl.multiple_of((4 * px + 2 * py + pc) * m, 16 if m % 16 == 0 else 8), m), :]


def _allgather(srcs, out_dtype, name, later=()):
    na, nb = len(srcs), len(later)
    every = list(srcs) + list(later)
    shapes = [(a.shape[-2], a.shape[-1]) for a, _ in every]

    def body(*refs):
        xs, refs = refs[:na + nb], refs[na + nb:]
        outs, land, refs = refs[:na], refs[na:na + nb], refs[na + nb:]
        stage, cast, raw = refs[:na], refs[na:na + nb], refs[na + nb:2 * (na + nb)]
        send_sems, recv_sems, local_sems, load_sems = refs[2 * (na + nb):]
        loads = [pltpu.make_async_copy(xs[i].at[every[i][1]], raw[i], load_sems.at[i]) for i in range(na + nb)]
        for cp in loads:
            cp.start()
        x, y, c = _mesh_pos()
        me, sibling = (x, y, c), (x, y, 1 - c)
        near = [(1 - x, y), (x, 1 - y)]
        far = (1 - x, 1 - y)
        relay_from, relay_to = (x ^ (1 - c), y ^ c), (x ^ c, y ^ (1 - c))
        _handshake([sibling] + [(*chip, c) for chip in near])
        k_from, k_to = 1 + c, 2 - c

        def slot(a, px, py, pc):
            return _device_rows(outs[a], shapes[a][0], px, py, pc)

        def copy(a, k, block, to, src=None):
            return pltpu.make_async_remote_copy(
                src_ref=slot(a, *block) if src is None else src, dst_ref=slot(a, *block),
                send_sem=send_sems.at[a, k], recv_sem=recv_sems.at[a, k], device_id=to, device_id_type=MESH)

        def cast_block(i):
            loads[i].wait()
            return raw[i][...].astype(out_dtype)

        for a in range(na):
            stage[a][...] = cast_block(a)
        mine = [pltpu.make_async_copy(stage[a], slot(a, *me), local_sems.at[a]) for a in range(na)]
        for cp in mine:
            cp.start()
        sent = []
        for a in range(na):
            sent.append(copy(a, 0, me, sibling, src=stage[a]))
            sent += [copy(a, 1 + j, me, (*chip, c), src=stage[a]) for j, chip in enumerate(near)]
        for cp in sent:
            cp.start()
        for b in range(nb):
            cast[b][...] = cast_block(na + b)
            cp = pltpu.make_async_copy(cast[b], _device_rows(land[b], shapes[na + b][0], *me), local_sems.at[na + b])
            cp.start()
            mine.append(cp)
        for a in range(na):
            copy(a, k_from, (*relay_from, c), me).wait_recv()
            sent += [copy(a, 3, (*relay_from, c), (*relay_to, c)), copy(a, 3 + k_from, (*relay_from, c), sibling)]
            sent[-2].start()
            sent[-1].start()
        for a in range(na):
            copy(a, k_to, (*relay_to, c), me).wait_recv()
            sent.append(copy(a, 3 + k_to, (*relay_to, c), sibling))
            sent[-1].start()
        for a in range(na):
            copy(a, 3, (*far, c), me).wait_recv()
            sent.append(copy(a, 6, (*far, c), sibling))
            sent[-1].start()
        for a in range(na):
            copy(a, 0, sibling, me).wait_recv()
            for j, chip in enumerate(near + [far]):
                copy(a, 4 + j, (*chip, 1 - c), me).wait_recv()
        for cp in sent:
            cp.wait_send()
        for cp in mine:
            cp.wait()

    hbm = pl.BlockSpec(memory_space=pl.ANY)
    gathered = [jax.ShapeDtypeStruct((N_DEV * m, n), out_dtype) for m, n in shapes]
    out = pl.pallas_call(
        body, name=name,
        in_specs=[hbm] * (na + nb),
        out_specs=[hbm] * (na + nb),
        out_shape=gathered,
        scratch_shapes=([pltpu.VMEM(s, out_dtype) for s in shapes]
                        + [pltpu.VMEM(s, a.dtype) for s, (a, _) in zip(shapes, every)]
                        + [pltpu.SemaphoreType.DMA((na, 7)), pltpu.SemaphoreType.DMA((na, 7)),
                           pltpu.SemaphoreType.DMA((na + nb,)), pltpu.SemaphoreType.DMA((na + nb,))]),
        compiler_params=pltpu.CompilerParams(vmem_limit_bytes=VMEM_LIMIT, collective_id=COLLECTIVE_GATHER_W0),
    )(*[a for a, _ in every])
    return out[:na], out[na:]


ALL_PEERS = tuple(range(1, N_DEV))


def _related(k, x, y, c):
    return x ^ ((k >> 2) & 1), y ^ ((k >> 1) & 1), c ^ (k & 1)


def _gather_wait(sems, block, land, relations, after, name):
    def body(src, land_ref, send_sem, recv_sem, after_ref, src_out, land_out):
        x, y, c = _mesh_pos()
        for k in relations:
            peer = _related(k, x, y, c)
            cp = pltpu.make_async_remote_copy(
                src_ref=src, dst_ref=_device_rows(land_ref, block.shape[0], *peer),
                send_sem=send_sem.at[k - 1], recv_sem=recv_sem.at[k - 1], device_id=peer, device_id_type=MESH)
            cp.wait_send()
            cp.wait_recv()

    out = pl.pallas_call(
        body, name=name,
        out_shape=(pltpu.HBM(block.shape, block.dtype), pltpu.HBM(land.shape, land.dtype)),
        in_specs=[_HBM, _HBM, _SEM, _SEM, pl.BlockSpec(memory_space=pl.ANY)],
        out_specs=[_HBM, _HBM],
        input_output_aliases={0: 0, 1: 1},
        compiler_params=pltpu.CompilerParams(has_side_effects=_EFFECT),
    )(block, land, sems[0], sems[1], after)
    return out[1]


(COLLECTIVE_GATHER_W0, COLLECTIVE_GATHER_W1, COLLECTIVE_EXCHANGE_1, COLLECTIVE_EXCHANGE_0A, COLLECTIVE_EXCHANGE_0B,
 COLLECTIVE_GATHER_SMALL) = range(1, 7)
COLLECTIVE_PAIR_SUM = (7, 8)


def _handshake(peers):
    barrier = pltpu.get_barrier_semaphore()
    for peer in peers:
        pl.semaphore_signal(barrier, inc=1, device_id=peer, device_id_type=MESH)
    pl.semaphore_wait(barrier, len(peers))


_HBM = pl.BlockSpec(memory_space=pltpu.HBM)
_SEM = pl.BlockSpec(memory_space=pltpu.SEMAPHORE)
_EFFECT = pltpu.SideEffectType.DATAFLOW_SIDE_EFFECTING


def _exchange_plan(direct):
    x, y, c = _mesh_pos()
    if not direct:
        return [(j, j, (qx, qy, c)) for j, (qx, qy) in enumerate([(1 - x, y), (x, 1 - y), (1 - x, 1 - y)])]
    plan = []
    for k in range(1, N_DEV):
        px, py, pc = x ^ ((k >> 2) & 1), y ^ ((k >> 1) & 1), c ^ (k & 1)
        plan.append((4 * px + 2 * py + pc, k - 1, (px, py, pc)))
    return plan


def _exchange_copies(directs):
    copies, base = [], 0
    for a, direct in enumerate(directs):
        plan = _exchange_plan(direct)
        copies += [(a, block, slot, peer, base + slot) for block, slot, peer in plan]
        base += len(plan)
    return copies, base


def _exchange_start(srcs, directs, collective_id, name):
    na = len(srcs)
    slots = [N_DEV - 1 if direct else 3 for direct in directs]

    def body(*refs):
        src, land = refs[:na], refs[na:2 * na]
        send_sem, recv_sem = refs[2 * na], refs[2 * na + 1]
        token = refs[-1]
        _handshake([peer for _, _, peer in _exchange_plan(any(directs))])
        for a, block, slot, peer, sem in _exchange_copies(directs)[0]:
            pltpu.make_async_remote_copy(
                src_ref=src[a].at[block], dst_ref=land[a].at[slot], send_sem=send_sem.at[sem],
                recv_sem=recv_sem.at[sem], device_id=peer, device_id_type=MESH).start()
        token[...] = jnp.zeros_like(token)

    zones = [jax.ShapeDtypeStruct((n,) + t.shape[1:], t.dtype) for n, t in zip(slots, srcs)]
    bufs = [pltpu.HBM(t.shape, t.dtype) for t in list(srcs) + zones]
    out = pl.pallas_call(
        body, name=name,
        out_shape=(pltpu.SemaphoreType.DMA((sum(slots),)), pltpu.SemaphoreType.DMA((sum(slots),)), *bufs,
                   jax.ShapeDtypeStruct((8, 128), F32)),
        in_specs=[_HBM] * (2 * na),
        out_specs=(_SEM, _SEM, *([_HBM] * (2 * na)), pl.BlockSpec(memory_space=pltpu.VMEM)),
        input_output_aliases={i: 2 + i for i in range(2 * na)},
        compiler_params=pltpu.CompilerParams(has_side_effects=_EFFECT, collective_id=collective_id),
    )(*[pltpu.with_memory_space_constraint(t, pltpu.HBM) for t in srcs],
      *[pltpu.with_memory_space_constraint(lax.empty(t.shape, t.dtype), pltpu.HBM) for t in zones])
    return out[0], out[1], out[2:2 + na], out[2 + na:2 + 2 * na], out[-1]


def _exchange_wait(send_sem, recv_sem, srcs, lands, directs, after, name):
    na = len(srcs)

    def body(*refs):
        src, land = refs[:na], refs[na:2 * na]
        send_sem_ref, recv_sem_ref = refs[2 * na], refs[2 * na + 1]
        for a, block, slot, peer, sem in _exchange_copies(directs)[0]:
            cp = pltpu.make_async_remote_copy(
                src_ref=src[a].at[block], dst_ref=land[a].at[slot], send_sem=send_sem_ref.at[sem],
                recv_sem=recv_sem_ref.at[sem], device_id=peer, device_id_type=MESH)
            cp.wait_send()
            cp.wait_recv()

    bufs = [pltpu.HBM(t.shape, t.dtype) for t in list(srcs) + list(lands)]
    out = pl.pallas_call(
        body, name=name,
        out_shape=tuple(bufs),
        in_specs=[_HBM] * (2 * na) + [_SEM, _SEM, pl.BlockSpec(memory_space=pl.ANY)],
        out_specs=[_HBM] * (2 * na),
        input_output_aliases={i: i for i in range(2 * na)},
        compiler_params=pltpu.CompilerParams(has_side_effects=_EFFECT),
    )(*srcs, *lands, send_sem, recv_sem, after)
    return out[:na], out[na:]


def _own_then_slots(mine_ref, lands_ref, rows=slice(None)):
    if len(mine_ref.shape) == 3:
        x, y, c = _mesh_pos()
        total = mine_ref[4 * x + 2 * y + c, rows, :].astype(F32)
    else:
        total = mine_ref[rows, :].astype(F32)
    for j in range(lands_ref.shape[0]):
        total = total + lands_ref[j, rows, :].astype(F32)
    return total


SMALL_ROWS = 2 * PACK_SLICE + 2 * 8


def _small_gather_start(mine, lands, dgpre, name):
    def body(*refs):
        hm, ld, dg = refs[:DEPTH], refs[DEPTH:2 * DEPTH], refs[2 * DEPTH:3 * DEPTH]
        send_sem, recv_sem, blk, land, token, own, slots, rows, built, local_sems = refs[3 * DEPTH:]
        x, y, c = _mesh_pos()
        loads = []
        for l in range(DEPTH):
            loads += [pltpu.make_async_copy(hm[l].at[4 * x + 2 * y + c], own.at[l], local_sems.at[3 * l]),
                      pltpu.make_async_copy(ld[l], slots.at[l], local_sems.at[3 * l + 1]),
                      pltpu.make_async_copy(dg[l], rows.at[l], local_sems.at[3 * l + 2])]
        for cp in loads:
            cp.start()
        _handshake([_related(k, x, y, c) for k in ALL_PEERS])
        for cp in loads:
            cp.wait()
        for l in range(DEPTH):
            total = own[l]
            for j in range(N_DEV - 1):
                total = total + slots[l, j]
            built[PACK_SLICE * l:PACK_SLICE * (l + 1), :] = total
            built[2 * PACK_SLICE + 8 * l:2 * PACK_SLICE + 8 * (l + 1), :] = rows[l]
        stores = [pltpu.make_async_copy(built, blk, local_sems.at[3 * DEPTH]),
                  pltpu.make_async_copy(built, _device_rows(land, SMALL_ROWS, x, y, c), local_sems.at[3 * DEPTH + 1])]
        for cp in stores:
            cp.start()
        for cp in stores:
            cp.wait()
        for k in ALL_PEERS:
            pltpu.make_async_remote_copy(
                src_ref=blk, dst_ref=_device_rows(land, SMALL_ROWS, x, y, c), send_sem=send_sem.at[k - 1],
                recv_sem=recv_sem.at[k - 1], device_id=_related(k, x, y, c), device_id_type=MESH).start()
        token[...] = jnp.zeros_like(token)

    hbm = pl.BlockSpec(memory_space=pl.ANY)
    out = pl.pallas_call(
        body, name=name,
        in_specs=[hbm] * (3 * DEPTH),
        out_specs=(_SEM, _SEM, _HBM, _HBM, pl.BlockSpec(memory_space=pltpu.VMEM)),
        out_shape=(pltpu.SemaphoreType.DMA((N_DEV - 1,)), pltpu.SemaphoreType.DMA((N_DEV - 1,)),
                   pltpu.HBM((SMALL_ROWS, 128), F32), pltpu.HBM((N_DEV * SMALL_ROWS, 128), F32),
                   jax.ShapeDtypeStruct((8, 128), F32)),
        scratch_shapes=[pltpu.VMEM((DEPTH, PACK_SLICE, 128), F32), pltpu.VMEM((DEPTH, N_DEV - 1, PACK_SLICE, 128), F32),
                        pltpu.VMEM((DEPTH, 8, 128), F32), pltpu.VMEM((SMALL_ROWS, 128), F32),
                        pltpu.SemaphoreType.DMA((3 * DEPTH + 2,))],
        compiler_params=pltpu.CompilerParams(has_side_effects=_EFFECT, collective_id=COLLECTIVE_GATHER_SMALL),
    )(*mine, *lands, *dgpre)
    return (out[0], out[1]), out[2], out[3], out[4]


def _adamw_math(w, g, m, v):
    m = ADAM_B1 * m + (1.0 - ADAM_B1) * g
    v = ADAM_B2 * v + (1.0 - ADAM_B2) * (g * g)
    m_hat = m / (1.0 - ADAM_B1 ** ADAM_STEP)
    v_hat = v / (1.0 - ADAM_B2 ** ADAM_STEP)
    delta = -ADAM_LR * (m_hat / (jnp.sqrt(v_hat) + ADAM_EPS) + ADAM_WD * w)
    return delta, m, v


def _adamw_layer(layer, parts, token, name):
    steps = {w.shape[1] // rows for _, _, w, _, _, _, rows in parts}
    assert len(steps) == 1, steps
    n = len(parts)

    def body(_, *refs):
        for p in range(n):
            hm_ref, ld_ref, w_ref, m_ref, v_ref = refs[5 * p:5 * (p + 1)]
            g_ref, d_ref, nm_ref, nv_ref = refs[len(refs) - 4 * (n - p):len(refs) - 4 * (n - p - 1)]
            g = _own_then_slots(hm_ref, ld_ref)
            g_ref[...] = g
            d, nm, nv = _adamw_math(w_ref[...], g, m_ref[...], v_ref[...])
            d_ref[...] = d
            nm_ref[...] = nm
            nv_ref[...] = nv

    in_specs, out_specs, out_shape, operands, carried, aliases = [pl.BlockSpec(memory_space=pl.ANY)], [], [], [], [], {}
    for p, (mine, lands, w, m, v, earlier, rows) in enumerate(parts):
        nn = w.shape[2]
        spec = pl.BlockSpec((None, rows, nn), lambda i: (layer, i, 0))
        in_specs += [pl.BlockSpec((rows, nn), lambda i: (i, 0)) if mine.ndim == 2
                     else pl.BlockSpec((N_DEV, rows, nn), lambda i: (0, i, 0)),
                     pl.BlockSpec((lands.shape[0], rows, nn), lambda i: (0, i, 0)), spec, spec, spec]
        out_specs += [spec] * 4
        out_shape += [jax.ShapeDtypeStruct(w.shape, F32)] * 4
        operands += [mine, lands, w, m, v]
        if earlier is not None:
            aliases.update({1 + 5 * n + len(carried) + t: 4 * p + t for t in range(4)})
            carried += list(earlier)
    out = pl.pallas_call(
        body, name=name, grid=(steps.pop(),),
        in_specs=in_specs + [pl.BlockSpec(memory_space=pl.ANY)] * len(carried),
        out_specs=out_specs, out_shape=out_shape, input_output_aliases=aliases,
        compiler_params=_compiler_params(("arbitrary",)),
    )(token, *operands, *carried)
    return [out[4 * p:4 * (p + 1)] for p in range(n)]


def _adamw_streamed(layer, parts, token, name):
    n = len(parts)

    def body(_, *refs):
        operands, results, scratch = refs[:9 * n], refs[9 * n:13 * n], refs[13 * n:]
        x, y, c = _mesh_pos()
        loads = []
        for p, (mine, _, w, _, _, _, rows) in enumerate(parts):
            hm, ld, w_ref, m_ref, v_ref = operands[9 * p:9 * p + 5]
            own, slots, wmv, _, load_sems, _ = scratch[6 * p:6 * (p + 1)]
            if mine.ndim == 3:
                hm = hm.at[4 * x + 2 * y + c]
            chunks = []
            for q in range(w.shape[1] // rows):
                rs = slice(rows * q, rows * (q + 1))
                cps = [pltpu.make_async_copy(hm.at[rs], own.at[rs], load_sems.at[q, 0]),
                       pltpu.make_async_copy(ld.at[:, rs], slots.at[:, rs], load_sems.at[q, 1])]
                cps += [pltpu.make_async_copy(t.at[layer, rs], wmv.at[j, rs], load_sems.at[q, 2 + j])
                        for j, t in enumerate((w_ref, m_ref, v_ref))]
                for cp in cps:
                    cp.start()
                chunks.append(cps)
            loads.append(chunks)
        stores = []
        for p, (_, _, _, _, _, _, rows) in enumerate(parts):
            own, slots, wmv, res, _, store_sems = scratch[6 * p:6 * (p + 1)]
            for q, cps in enumerate(loads[p]):
                for cp in cps:
                    cp.wait()
                rs = slice(rows * q, rows * (q + 1))
                g = _own_then_slots(own, slots, rs)
                for t, val in enumerate((g, *_adamw_math(wmv[0, rs, :], g, wmv[1, rs, :], wmv[2, rs, :]))):
                    res[t, rs, :] = val
                    stores.append(pltpu.make_async_copy(res.at[t, rs], results[4 * p + t].at[layer, rs],
                                                        store_sems.at[q, t]))
                    stores[-1].start()
        for cp in stores:
            cp.wait()

    operands, scratch, aliases = [], [], {}
    for p, (mine, lands, w, m, v, earlier, rows) in enumerate(parts):
        _, mm, nn = w.shape
        aliases.update({1 + 9 * p + 5 + t: 4 * p + t for t in range(4)})
        operands += [mine, lands, w, m, v, *earlier]
        scratch += [pltpu.VMEM((mm, nn), mine.dtype), pltpu.VMEM((lands.shape[0], mm, nn), lands.dtype),
                    pltpu.VMEM((3, mm, nn), F32), pltpu.VMEM((4, mm, nn), F32),
                    pltpu.SemaphoreType.DMA((mm // rows, 5)), pltpu.SemaphoreType.DMA((mm // rows, 4))]
    hbm = pl.BlockSpec(memory_space=pl.ANY)
    out = pl.pallas_call(
        body, name=name,
        in_specs=[hbm] * (1 + 9 * n), out_specs=[hbm] * (4 * n),
        out_shape=[jax.ShapeDtypeStruct(w.shape, F32) for _, _, w, _, _, _, _ in parts for _ in range(4)],
        scratch_shapes=scratch, input_output_aliases=aliases, compiler_params=_compiler_params(),
    )(token, *operands)
    return [out[4 * p:4 * (p + 1)] for p in range(n)]


def _adamw_small(gathered, params):
    def body(all_ref, *refs):
        ins, outs, packs = refs[:15], refs[15:15 + 21], refs[15 + 21]
        loss_ref = outs[0]
        for dev in range(N_DEV):
            for l in range(DEPTH):
                packs[l, PACK_SLICE * dev:PACK_SLICE * (dev + 1), :] = (
                    all_ref[SMALL_ROWS * dev + PACK_SLICE * l:SMALL_ROWS * dev + PACK_SLICE * (l + 1), :])
        loss_ref[...] = packs[DEPTH - 1, ROW_LOSS:ROW_LOSS + 1, 0:1]

        def update(p, sel, g):
            w_ref, m_ref, v_ref = ins[p], ins[5 + p], ins[10 + p]
            d, nm, nv = _adamw_math(w_ref[sel], g, m_ref[sel], v_ref[sel])
            for t, val in enumerate((g, d, nm, nv)):
                outs[1 + 5 * t + p][sel] = val

        for l in range(DEPTH):
            gp = packs.at[l]
            row0 = 2 * PACK_SLICE + 8 * l
            dgpre = all_ref[row0:row0 + 8, :]
            for dev in range(1, N_DEV):
                dgpre = dgpre + all_ref[SMALL_ROWS * dev + row0:SMALL_ROWS * dev + row0 + 8, :]
            for grp in range(4):
                update(0, (l, grp), gp[ROW_PW + BLOCK * grp:ROW_PW + BLOCK * (grp + 1), :])
                update(1, (slice(l, l + 1), slice(128 * grp, 128 * (grp + 1))), gp[ROW_SC + grp:ROW_SC + grp + 1, :])
            update(2, (slice(l, l + 1), slice(None)), gp[ROW_SINK:ROW_SINK + 1, 0:N_HEADS])
            for r in range(D_MODEL // 128):
                sel = (slice(l, l + 1), slice(128 * r, 128 * (r + 1)))
                update(3, sel, dgpre[r:r + 1, :])
                update(4, sel, gp[ROW_NPOST + r:ROW_NPOST + r + 1, :])

    shapes = [jax.ShapeDtypeStruct(p.shape, F32) for p in params[:5]]
    return pl.pallas_call(
        body, name="adamw_small",
        out_shape=[jax.ShapeDtypeStruct((1, 1), F32)] + shapes * 4,
        scratch_shapes=[pltpu.VMEM((DEPTH, PACK_ROWS, 128), F32)],
        compiler_params=_compiler_params(),
    )(gathered, *params)


def kernel(x, w_in, pool_w, pool_scale, attn_sinks, w_out, norm_pre, norm_post, loss_target, m_w_in, m_pool_w, m_pool_scale, m_attn_sinks, m_w_out, m_norm_pre, m_norm_post, v_w_in, v_pool_w, v_pool_scale, v_attn_sinks, v_w_out, v_norm_pre, v_norm_post):
    x0 = x.reshape(SEQ, D_MODEL)
    target = loss_target.reshape(SEQ, D_MODEL)
    bias = jnp.asarray(_attn_bias())
    w_in_t, m_in_t, v_in_t = (jnp.swapaxes(t, 1, 2) for t in (w_in, m_w_in, v_w_in))

    (win0, wout0), lands = _allgather([(w_in_t, 0), (w_out, 0)], BF16, "gather_w0",
                                              later=[(w_in_t, 1), (w_out, 1)])
    win_full, wout_full = [win0, None], [wout0, None]
    token = x0

    saved = []
    xl = x0
    for layer in range(DEPTH):
        front = (layer, xl, norm_pre, win_full[layer], token, attn_sinks, pool_w, pool_scale, bias)
        if layer == 0:
            u, pg, q, k, v, ag, z, a, x_next, y, win_full[1], wout_full[1] = _fwd_front(
                *front, out=(norm_post, wout_full[layer]), gather=lands)
        else:
            u, pg, q, k, v, ag, z, a, x_next, y = _fwd_front(*front, out=(norm_post, wout_full[layer]))
        saved.append((xl, u, pg, q, k, v, ag, z, a, y))
        xl = x_next

    params_small = [pool_w, pool_scale, attn_sinks, norm_pre, norm_post,
                    m_pool_w, m_pool_scale, m_attn_sinks, m_norm_pre, m_norm_post,
                    v_pool_w, v_pool_scale, v_attn_sinks, v_norm_pre, v_norm_post]

    def start(srcs, directs, paired, collective_id, tag):
        send_sem, recv_sem, srcs, lands, started = _exchange_start(srcs, directs, collective_id, f"exchange_start{tag}")
        return (send_sem, recv_sem, srcs, lands, paired, directs), started

    def finish(handle, after, tag):
        send_sem, recv_sem, srcs, lands, paired, directs = handle
        srcs, lands = _exchange_wait(send_sem, recv_sem, srcs, lands, directs, after, f"exchange_wait{tag}")
        return [s if p is None else p for s, p in zip(srcs, paired)], lands

    def back(layer, top, first, second):
        xin, u, pg, q, k, v, ag, z, a, y = saved[layer]
        return _bwd_back(layer, top, first, second, y, z, norm_post, wout_full[layer], attn_sinks, u, pg, q, k, v,
                         ag, a, pool_w, pool_scale, bias)

    dgpre = [None] * DEPTH
    dx, dproj, gw_out, pack = back(1, True, xl, target)
    dx, dgpre[1], chip_sums, own_sum = _bwd_in(1, "both_pair", token, dproj, saved[1][0], norm_pre, dx, win_full[1])
    top, token = start([chip_sums, gw_out.reshape(N_DEV, OUT_SHARD, D_MODEL), pack.reshape(N_DEV, PACK_SLICE, 128)],
                       [False, True, True], [own_sum, None, None], COLLECTIVE_EXCHANGE_1, "1")

    dproj, gw_out, pack = back(0, False, dx, token)
    early, token = start([gw_out.reshape(N_DEV, OUT_SHARD, D_MODEL), pack.reshape(N_DEV, PACK_SLICE, 128)],
                         [True, True], [None, None], COLLECTIVE_EXCHANGE_0A, "0a")
    chip_sums, own_sum = _bwd_in(0, "dw_pair", token, dproj, saved[0][0], norm_pre)
    own1, lands1 = finish(top, chip_sums, "1")
    late, token = start([chip_sums], [False], [own_sum], COLLECTIVE_EXCHANGE_0B, "0b")
    dx, dgpre[0], *updated = _bwd_in(0, "dx", token, dproj, saved[0][0], norm_pre, dx, win_full[0],
                                     adam=(1, [(own1[0], lands1[0], w_in_t, m_in_t, v_in_t, None, ADAM_ROWS_IN),
                                               (own1[1], lands1[1], w_out, m_w_out, v_w_out, None, ADAM_ROWS_OUT)]))
    big_in, big_out = updated[:4], updated[4:]

    own0a, lands0a = finish(early, dx, "0a")
    sems, block, land, token = _small_gather_start([own0a[1], own1[2]], [lands0a[1], lands1[2]], dgpre,
                                                   "gather_small_start")
    own0b, lands0b = finish(late, token, "0b")
    big_in, big_out = _adamw_streamed(0, [(own0b[0], lands0b[0], w_in_t, m_in_t, v_in_t, big_in, ADAM_CHUNK_IN),
                                          (own0a[0], lands0a[0], w_out, m_w_out, v_w_out, big_out, ADAM_CHUNK_OUT)],
                                      token, "adamw0")
    gathered = _gather_wait(sems, block, land, ALL_PEERS, big_in[0], "gather_small_wait")
    small_out = _adamw_small(gathered, params_small)
    loss = small_out[0].reshape(())

    outs = [loss, dx.reshape(1, SEQ, D_MODEL)]
    for t in range(4):
        pw_, sc_, sk_, npre_, npost_ = small_out[1 + 5 * t:6 + 5 * t]
        outs += [jnp.swapaxes(big_in[t], 1, 2), pw_, sc_, sk_, big_out[t], npre_, npost_]
    return tuple(outs)
```

```python
import functools
import numpy as np
import jax
import jax.numpy as jnp
from jax import lax
from jax.experimental import pallas as pl
from jax.experimental.pallas import tpu as pltpu

F32 = jnp.float32
BF16 = jnp.bfloat16

N_DEV = 8
SEQ = 2048
D_MODEL = 1024
D_POOL = 512
D_ATTN = 512
D_KV = 128
D_IN = 2304
N_HEADS = 8
GQA = 4
HEAD_DIM = 64
BLOCK = 128
POOL_WINDOWS = (2, 4, 8, 16)
DEPTH = 2
EPS = 1e-6
NEG_INF = -1e30
SCALE = HEAD_DIM ** -0.5
IN_SHARD = D_IN // N_DEV
OUT_SHARD = D_MODEL // N_DEV

COL_U, COL_PG, COL_Q, COL_K, COL_V, COL_AG = 0, 512, 1024, 1536, 1664, 1792

ADAM_LR = 0.001
ADAM_B1 = 0.9
ADAM_B2 = 0.999
ADAM_EPS = 1e-08
ADAM_WD = 0.01
ADAM_STEP = 10

TOKEN_TILE = 512
ADAM_ROWS_IN, ADAM_ROWS_OUT = 144, 64
VMEM_LIMIT = 56 * 1024 * 1024
MESH = pl.DeviceIdType.MESH

ROW_PW, ROW_SC, ROW_SINK, ROW_NPOST, ROW_LOSS = 0, 512, 520, 536, 544
PACK_ROWS = 576
PACK_SLICE = PACK_ROWS // N_DEV


def _nn(a, b):
    return jnp.dot(a, b, preferred_element_type=F32)


def _nt(a, b):
    return lax.dot_general(a, b, (((1,), (1,)), ((), ())), preferred_element_type=F32)


def _tn(a, b):
    return lax.dot_general(a, b, (((0,), (0,)), ((), ())), preferred_element_type=F32)


def _silu_parts(g):
    s = jax.nn.sigmoid(g)
    return g * s, s * (1.0 + g * (1.0 - s))


def _resident(shape):
    return pl.BlockSpec(shape, lambda *_: (0,) * len(shape), pipeline_mode=pl.Buffered(1))


def _compiler_params(sem=None):
    if sem is None:
        return pltpu.CompilerParams(vmem_limit_bytes=VMEM_LIMIT)
    return pltpu.CompilerParams(dimension_semantics=sem, vmem_limit_bytes=VMEM_LIMIT)


def _attn_bias():
    t = np.arange(BLOCK)[None, :]
    j = np.arange(BLOCK)[:, None]
    current = j <= t
    dist = np.where(current, t - j, t + BLOCK - j).astype(np.float32)
    out = np.zeros((2, 2, BLOCK, GQA * BLOCK), np.float32)
    for variant in range(2):
        valid = current | (variant == 1)
        for kv in range(2):
            for g in range(GQA):
                slope = np.float32(2.0 ** (-(kv * GQA + g + 1)))
                out[variant, kv, :, g * BLOCK:(g + 1) * BLOCK] = np.where(valid, -slope * dist, np.float32(NEG_INF))
    return out


def _replicate_head(kx, kv):
    rolled = pltpu.roll(kx, 64, 1)
    lane = lax.broadcasted_iota(jnp.int32, kx.shape, 1)
    dup = jnp.where(lane < 64, kx, rolled) if kv == 0 else jnp.where(lane < 64, rolled, kx)
    return jnp.concatenate([dup, dup], axis=1).astype(BF16)


def _stack_heads(qv):
    lane = lax.broadcasted_iota(jnp.int32, qv.shape, 1)
    zero = jnp.zeros_like(qv)
    return jnp.concatenate([jnp.where((lane >= 64 * g) & (lane < 64 * g + 64), qv, zero) for g in range(GQA)], axis=0)


def _unstack_heads(xs):
    lane = lax.broadcasted_iota(jnp.int32, (BLOCK, 256), 1)
    return jnp.where(lane < 64, xs[0:128], jnp.where(lane < 128, xs[128:256], jnp.where(lane < 192, xs[256:384], xs[384:512])))


def _fold_heads(r):
    h = r[:, 0:128] + r[:, 128:256]
    return h + pltpu.roll(h, 64, 1)


def _sink_row(sink_ref, layer, kv):
    lane = lax.broadcasted_iota(jnp.int32, (1, GQA * BLOCK), 1)
    s4 = [sink_ref[layer, kv * GQA + g] for g in range(GQA)]
    return jnp.where(lane < 128, s4[0], jnp.where(lane < 256, s4[1], jnp.where(lane < 384, s4[2], s4[3])))


def _band_is_current():
    j = lax.broadcasted_iota(jnp.int32, (BLOCK, GQA * BLOCK), 0)
    t = lax.broadcasted_iota(jnp.int32, (BLOCK, GQA * BLOCK), 1) & (BLOCK - 1)
    return j <= t


def _pack_band(full, current):
    return jnp.where(current, full[BLOCK:], full[:BLOCK])


def _unpack_band(packed, current):
    zero = jnp.zeros_like(packed)
    return jnp.concatenate([jnp.where(current, zero, packed), jnp.where(current, packed, zero)], axis=0)


def _probs_keys_major(k_rep, q_st, bias, sink, current):
    st = _pack_band(_nt(k_rep, q_st), current) * SCALE + bias
    m = jnp.maximum(jnp.max(st, axis=0, keepdims=True), sink)
    p = jnp.exp(st - m)
    esink = jnp.exp(sink - m)
    rl = 1.0 / (jnp.sum(p, axis=0, keepdims=True) + esink)
    return p * rl, esink * rl


WINDOW_HALO = 16


def _window_sum(ext, w, forward):
    s = ext
    sh = 1
    while sh < w:
        s = s + pltpu.roll(s, (ext.shape[0] - sh) if forward else sh, 0)
        sh *= 2
    return s


def _inv_count(n, w):
    t = n * BLOCK + lax.broadcasted_iota(jnp.int32, (BLOCK, 1), 0) + 1
    return 1.0 / jnp.minimum(t.astype(F32), float(w))


def _kv_ext(ref, n):
    r0 = pl.multiple_of(jnp.maximum(n - 1, 0) * BLOCK, BLOCK)
    r1 = pl.multiple_of(n * BLOCK, BLOCK)
    return jnp.concatenate([ref[pl.ds(r0, BLOCK), :], ref[pl.ds(r1, BLOCK), :]], axis=0)


def _rows_of(vec_ref, pack_ref, row0):
    for r in range(D_MODEL // 128):
        pack_ref[row0 + r:row0 + r + 1, :] = vec_ref[:, 128 * r:128 * (r + 1)]


FRONT_TILE = 4 * BLOCK


def _fwd_front(layer, x, norm_pre, w_in_t, token, sinks, pool_w, pool_scale, bias, out=None, gather=None):
    tm = FRONT_TILE

    def body(sink_ref, x_ref, g_ref, w_ref, _, pw_ref, sc_ref, bias_ref, *refs):
        if out is not None:
            gpost_ref, wo_ref, *refs = refs
        if gather is not None:
            refs = refs[2:]
            zones, (send_sems, recv_sems), refs = refs[10:12], refs[-2:], refs[:-2]
        if out is not None:
            xn_ref, y_ref = refs[8:10]
        u_ref, pg_ref, q_ref, k_ref, v_ref, ag_ref, z_ref, a_ref = refs[:8]
        uprev, kprev, vprev = refs[-3:]
        i = pl.program_id(0)

        @pl.when(i == 0)
        def _():
            uprev[...] = jnp.zeros_like(uprev)
            kprev[...] = jnp.zeros_like(kprev)
            vprev[...] = jnp.zeros_like(vprev)

        if gather is not None:
            px, py, pc = _mesh_pos()
            me, sibling = (px, py, pc), (px, py, 1 - pc)
            near = [(1 - px, py), (px, 1 - py)]
            far = (1 - px, 1 - py)
            relay_from, relay_to = (px ^ (1 - pc), py ^ pc), (px ^ pc, py ^ (1 - pc))
            k_from, k_to = 1 + pc, 2 - pc

            def copy(a, sem, block, to):
                rows_of_block = _device_rows(zones[a], gather[a].shape[0] // N_DEV, *block)
                return pltpu.make_async_remote_copy(
                    src_ref=rows_of_block, dst_ref=rows_of_block, send_sem=send_sems.at[a, sem],
                    recv_sem=recv_sems.at[a, sem], device_id=to, device_id_type=MESH)

            def pass_on(a):
                copy(a, k_from, (*relay_from, pc), me).wait_recv()
                copy(a, 3, (*relay_from, pc), (*relay_to, pc)).start()
                copy(a, 3 + k_from, (*relay_from, pc), sibling).start()
                copy(a, k_to, (*relay_to, pc), me).wait_recv()
                copy(a, 3 + k_to, (*relay_to, pc), sibling).start()

            @pl.when(i == 0)
            def _():
                _handshake([sibling] + [(*chip, pc) for chip in near])
                for a in range(2):
                    copy(a, 0, me, sibling).start()
                    for j, chip in enumerate(near):
                        copy(a, 1 + j, me, (*chip, pc)).start()

            for a in range(2):
                pl.when(i == 2 + a)(functools.partial(pass_on, a))

        xv = x_ref[...]
        r = lax.rsqrt(jnp.mean(xv * xv, axis=-1, keepdims=True) + EPS)
        h = (xv * r * g_ref[layer:layer + 1, :]).astype(BF16)
        u_ref[...] = _nt(h, w_ref[COL_U:COL_PG, :])
        pg_ref[...] = _nt(h, w_ref[COL_PG:COL_Q, :])
        for sb in range(tm // BLOCK):
            n = (tm // BLOCK) * i + sb
            rows = slice(BLOCK * sb, BLOCK * (sb + 1))
            before = slice(BLOCK * (sb - 1), BLOCK * sb)
            uv = u_ref[rows, :]
            halo = (uprev[BLOCK - WINDOW_HALO:, :] if sb == 0
                    else u_ref[BLOCK * sb - WINDOW_HALO:BLOCK * sb, :])
            ext = jnp.concatenate([halo, uv], axis=0)
            for g, w in enumerate(POOL_WINDOWS):
                cs = slice(BLOCK * g, BLOCK * (g + 1))
                win = _window_sum(ext[:, cs], w, forward=False)[WINDOW_HALO:]
                pooled = win * _inv_count(n, w) - uv[:, cs]
                mixed = _nn(pooled.astype(BF16), pw_ref[g].astype(BF16))
                gate, _ = _silu_parts(pg_ref[rows, cs])
                z_ref[rows, cs] = (mixed * sc_ref[layer:layer + 1, cs] * gate).astype(BF16)

        q_ref[...] = _nt(h, w_ref[COL_Q:COL_K, :]).astype(BF16)
        k_ref[...] = _nt(h, w_ref[COL_K:COL_V, :])
        v_ref[...] = _nt(h, w_ref[COL_V:COL_AG, :])
        ag_ref[...] = _nt(h, w_ref[COL_AG:D_IN, :])

        current = _band_is_current()
        for sb in range(tm // BLOCK):
            n = (tm // BLOCK) * i + sb
            rows = slice(BLOCK * sb, BLOCK * (sb + 1))
            before = slice(BLOCK * (sb - 1), BLOCK * sb)
            kx = jnp.concatenate([kprev[...] if sb == 0 else k_ref[before, :], k_ref[rows, :]], axis=0)
            vx = jnp.concatenate([vprev[...] if sb == 0 else v_ref[before, :], v_ref[rows, :]], axis=0)
            variant = jnp.minimum(n, 1) if sb == 0 else 1
            for kv in range(2):
                cs = slice(256 * kv, 256 * (kv + 1))
                p, _ = _probs_keys_major(_replicate_head(kx, kv), _stack_heads(q_ref[rows, cs]),
                                         bias_ref[variant, kv], _sink_row(sink_ref, layer, kv), current)
                o = _unstack_heads(_tn(_unpack_band(p.astype(BF16), current), _replicate_head(vx, kv)))
                a_ref[rows, cs] = o
                gate, _ = _silu_parts(ag_ref[rows, cs])
                z_ref[rows, D_POOL + 256 * kv:D_POOL + 256 * (kv + 1)] = (o * gate).astype(BF16)

        tail = slice(tm - BLOCK, tm)
        uprev[...] = u_ref[tail, :]
        kprev[...] = k_ref[tail, :]
        vprev[...] = v_ref[tail, :]

        if out is not None:
            y = _nn(z_ref[...], wo_ref[...])
            y_ref[...] = y
            r = lax.rsqrt(jnp.mean(y * y, axis=-1, keepdims=True) + EPS)
            xn_ref[...] = x_ref[...] + y * r * gpost_ref[layer:layer + 1, :]

        if gather is not None:
            @pl.when(i == SEQ // tm - 1)
            def _():
                for a in range(2):
                    copy(a, 3, (*far, pc), me).wait_recv()
                    copy(a, 6, (*far, pc), sibling).start()
                for a in range(2):
                    copy(a, 0, sibling, me).wait_recv()
                    for j, chip in enumerate(near + [far]):
                        copy(a, 4 + j, (*chip, 1 - pc), me).wait_recv()
                for a in range(2):
                    copy(a, 0, me, sibling).wait_send()
                    for j, chip in enumerate(near):
                        copy(a, 1 + j, me, (*chip, pc)).wait_send()
                    copy(a, 3, (*relay_from, pc), (*relay_to, pc)).wait_send()
                    copy(a, 3 + k_from, (*relay_from, pc), sibling).wait_send()
                    copy(a, 3 + k_to, (*relay_to, pc), sibling).wait_send()
                    copy(a, 6, (*far, pc), sibling).wait_send()

    row = lambda c: pl.BlockSpec((tm, c), lambda i: (i, 0))
    const = lambda shape: pl.BlockSpec(shape, lambda i: (0,) * len(shape))
    act = jax.ShapeDtypeStruct((SEQ, D_MODEL), F32)
    fused = out is not None
    hbm = pl.BlockSpec(memory_space=pl.ANY)
    zones, params = [], _compiler_params(("arbitrary",))
    if gather is not None:
        assert fused and SEQ // tm == 4, "the gather's copies are spread over four grid steps"
        zones = list(gather)
        params = pltpu.CompilerParams(dimension_semantics=("arbitrary",), vmem_limit_bytes=VMEM_LIMIT,
                                      collective_id=COLLECTIVE_GATHER_W1)
    return pl.pallas_call(
        body, name=f"fwd_front{layer}", grid=(SEQ // tm,),
        in_specs=[pl.BlockSpec(memory_space=pltpu.SMEM), row(D_MODEL), const((DEPTH, D_MODEL)),
                  _resident((D_IN, D_MODEL)), const((8, 128)),
                  pl.BlockSpec((None, 4, BLOCK, BLOCK), lambda i: (layer, 0, 0, 0)), const((DEPTH, D_POOL)),
                  _resident((2, 2, BLOCK, GQA * BLOCK))]
                 + ([const((DEPTH, D_MODEL)), _resident((D_MODEL, D_MODEL))] if fused else []) + [hbm] * len(zones),
        out_specs=[row(D_POOL), row(D_POOL), row(D_ATTN), row(D_KV), row(D_KV), row(D_ATTN), row(D_MODEL),
                   row(D_ATTN)] + ([row(D_MODEL)] * 2 if fused else []) + [hbm] * len(zones),
        out_shape=[jax.ShapeDtypeStruct((SEQ, D_POOL), F32), jax.ShapeDtypeStruct((SEQ, D_POOL), F32),
                   jax.ShapeDtypeStruct((SEQ, D_ATTN), BF16), jax.ShapeDtypeStruct((SEQ, D_KV), F32),
                   jax.ShapeDtypeStruct((SEQ, D_KV), F32), jax.ShapeDtypeStruct((SEQ, D_ATTN), F32),
                   jax.ShapeDtypeStruct((SEQ, D_MODEL), BF16), jax.ShapeDtypeStruct((SEQ, D_ATTN), F32)]
                  + ([act] * 2 if fused else []) + [jax.ShapeDtypeStruct(t.shape, t.dtype) for t in zones],
        scratch_shapes=[pltpu.VMEM((BLOCK, D_POOL), F32), pltpu.VMEM((BLOCK, D_KV), F32),
                        pltpu.VMEM((BLOCK, D_KV), F32)]
                       + ([pltpu.SemaphoreType.DMA((2, 7)), pltpu.SemaphoreType.DMA((2, 7))] if zones else []),
        input_output_aliases={10 + a: 10 + a for a in range(len(zones))},
        compiler_params=params,
    )(sinks, x, norm_pre, w_in_t, token, pool_w, pool_scale, bias, *(out if fused else ()), *zones)


BACK_TILE = 2 * BLOCK


def _bwd_back(layer, top, dxo_or_xf, target_or_token, y, z, norm_post, w_out, sinks, u, pg, q, k, v, ag, a,
              pool_w, pool_scale, bias):
    tm = BACK_TILE
    steps = SEQ // tm
    last = steps - 1
    per = tm // BLOCK

    def body(*refs):
        refs = list(refs)
        sink_ref, first, second = refs[:3]
        (y_ref, z_ref, g_ref, w_ref, u_ref, up_ref, pg_ref, q_ref, k_ref, v_ref, ag_ref, a_ref, pw_ref, sc_ref,
         bias_ref) = refs[3:18]
        del refs[:18]
        dxo_ref = refs.pop(0) if top else None
        dp_ref, dw_ref, pack_ref, acc, dg, lacc, dzs, ck, cv, ce = refs
        i = pl.program_id(0)
        blk = last - i

        @pl.when(i == 0)
        def _():
            acc[...] = jnp.zeros_like(acc)
            dg[...] = jnp.zeros_like(dg)
            lacc[...] = jnp.zeros_like(lacc)
            pack_ref[...] = jnp.zeros_like(pack_ref)
            ck[...] = jnp.zeros_like(ck)
            cv[...] = jnp.zeros_like(cv)
            ce[...] = jnp.zeros_like(ce)

        if top:
            d = first[...] - second[...]
            dxo_v = d * (1.0 / D_MODEL)
            dxo_ref[...] = dxo_v
            part = jnp.sum(d * d, axis=-1, keepdims=True) * (1.0 / D_MODEL)
            lacc[...] += 0.5 * jnp.sum(part, axis=0, keepdims=True)
        else:
            dxo_v = first[...]
        yv = y_ref[...]
        r = lax.rsqrt(jnp.mean(yv * yv, axis=-1, keepdims=True) + EPS)
        yn = yv * r
        dg[...] += jnp.sum(dxo_v * yn, axis=0, keepdims=True)
        dyn = dxo_v * g_ref[layer:layer + 1, :]
        dy = (r * (dyn - yn * jnp.mean(dyn * yn, axis=-1, keepdims=True))).astype(BF16)
        dzs[...] = _nt(dy, w_ref[...])
        acc[...] += _tn(z_ref[...], dy)

        lane = lax.broadcasted_iota(jnp.int32, (1, 128), 1)
        lane2 = lax.broadcasted_iota(jnp.int32, (256, 128), 1)
        current = _band_is_current()
        for sb in reversed(range(per)):
            n = per * blk + sb
            rows = slice(BLOCK * sb, BLOCK * (sb + 1))

            uv = u_ref[rows, :]
            if sb == 0:
                halo = up_ref[BLOCK - WINDOW_HALO:, :] * (n > 0).astype(F32)
            else:
                halo = u_ref[BLOCK * sb - WINDOW_HALO:BLOCK * sb, :]
            ext = jnp.concatenate([halo, uv], axis=0)
            for g, w in enumerate(POOL_WINDOWS):
                cs = slice(BLOCK * g, BLOCK * (g + 1))
                inv = _inv_count(n, w)
                win = _window_sum(ext[:, cs], w, forward=False)[WINDOW_HALO:]
                pooled = win * inv - uv[:, cs]
                pw_g = pw_ref[g].astype(BF16)
                mixed = _nn(pooled.astype(BF16), pw_g)
                gate, dgate = _silu_parts(pg_ref[rows, cs])
                dzp = dzs[rows, cs]
                sc = sc_ref[layer:layer + 1, cs]
                dpm = dzp * gate
                dp_ref[rows, COL_PG + BLOCK * g:COL_PG + BLOCK * (g + 1)] = (dzp * (mixed * sc) * dgate).astype(BF16)
                pack_ref[ROW_SC + g:ROW_SC + g + 1, :] += jnp.sum(dpm * mixed, axis=0, keepdims=True)
                dmixed = (dpm * sc).astype(BF16)
                pack_ref[ROW_PW + BLOCK * g:ROW_PW + BLOCK * (g + 1), :] += _tn(pooled.astype(BF16), dmixed)
                dpooled = _nt(dmixed, pw_g)
                e = dpooled * inv
                lead = _window_sum(jnp.concatenate([e, ce[:WINDOW_HALO, cs]], axis=0), w, forward=True)[:BLOCK]
                dp_ref[rows, COL_U + BLOCK * g:COL_U + BLOCK * (g + 1)] = (lead - dpooled).astype(BF16)
                ce[:, cs] = e

            kx = _kv_ext(k_ref, n)
            vx = _kv_ext(v_ref, n)
            variant = jnp.minimum(n, 1) if sb == 0 else 1
            dsink_row = jnp.zeros((1, 128), F32)
            tks, tvs = [], []
            for kv in range(2):
                cs = slice(256 * kv, 256 * (kv + 1))
                k_rep = _replicate_head(kx, kv)
                v_rep = _replicate_head(vx, kv)
                q_st = _stack_heads(q_ref[rows, cs])
                gate, dgate = _silu_parts(ag_ref[rows, cs])
                dza = dzs[rows, D_POOL + 256 * kv:D_POOL + 256 * (kv + 1)]
                dp_ref[rows, COL_AG + 256 * kv:COL_AG + 256 * (kv + 1)] = (dza * a_ref[rows, cs] * dgate).astype(BF16)
                da_st = _stack_heads((dza * gate).astype(BF16))
                p, psink = _probs_keys_major(k_rep, q_st, bias_ref[variant, kv], _sink_row(sink_ref, layer, kv),
                                             current)
                dpt = _pack_band(_nt(v_rep, da_st), current)
                delta = jnp.sum(p * dpt, axis=0, keepdims=True)
                dst = _unpack_band((p * (dpt - delta) * SCALE).astype(BF16), current)
                sink_terms = psink * delta
                for g in range(GQA):
                    dsink = -jnp.sum(sink_terms[:, BLOCK * g:BLOCK * (g + 1)], axis=1, keepdims=True)
                    dsink_row = dsink_row + jnp.where(lane == kv * GQA + g, dsink, 0.0)
                dp_ref[rows, COL_Q + 256 * kv:COL_Q + 256 * (kv + 1)] = _unstack_heads(_tn(dst, k_rep)).astype(BF16)
                tks.append(_fold_heads(_nn(dst, q_st)))
                tvs.append(_fold_heads(_nn(_unpack_band(p.astype(BF16), current), da_st)))
            pack_ref[ROW_SINK:ROW_SINK + 1, :] += dsink_row
            dkx = jnp.where(lane2 < 64, tks[0], tks[1])
            dvx = jnp.where(lane2 < 64, tvs[0], tvs[1])
            dp_ref[rows, COL_K:COL_V] = (ck[...] + dkx[BLOCK:]).astype(BF16)
            dp_ref[rows, COL_V:COL_AG] = (cv[...] + dvx[BLOCK:]).astype(BF16)
            ck[...] = dkx[:BLOCK]
            cv[...] = dvx[:BLOCK]

        @pl.when(i == steps - 1)
        def _():
            dw_ref[...] = acc[...].astype(BF16)
            _rows_of(dg, pack_ref, ROW_NPOST)
            pack_ref[ROW_LOSS:ROW_LOSS + 1, :] = jnp.where(lane == 0, lacc[...], 0.0)

    row = lambda c: pl.BlockSpec((tm, c), lambda i: (last - i, 0))
    const = lambda shape: pl.BlockSpec(shape, lambda i: (0,) * len(shape))
    act = jax.ShapeDtypeStruct((SEQ, D_MODEL), F32)
    return pl.pallas_call(
        body, name=f"bwd_back{layer}", grid=(steps,),
        in_specs=[pl.BlockSpec(memory_space=pltpu.SMEM), row(D_MODEL), row(D_MODEL) if top else const((8, 128)),
                  row(D_MODEL), row(D_MODEL), const((DEPTH, D_MODEL)), _resident((D_MODEL, D_MODEL)),
                  row(D_POOL), pl.BlockSpec((BLOCK, D_POOL), lambda i: (jnp.maximum(per * (last - i) - 1, 0), 0)),
                  row(D_POOL), row(D_ATTN), _resident((SEQ, D_KV)), _resident((SEQ, D_KV)), row(D_ATTN), row(D_ATTN),
                  pl.BlockSpec((None, 4, BLOCK, BLOCK), lambda i: (layer, 0, 0, 0)), const((DEPTH, D_POOL)),
                  _resident((2, 2, BLOCK, GQA * BLOCK))],
        out_specs=([row(D_MODEL)] * (1 if top else 0)
                   + [row(D_IN), const((D_MODEL, D_MODEL)), const((PACK_ROWS, 128))]),
        out_shape=([act] * (1 if top else 0)
                   + [jax.ShapeDtypeStruct((SEQ, D_IN), BF16), jax.ShapeDtypeStruct((D_MODEL, D_MODEL), BF16),
                      jax.ShapeDtypeStruct((PACK_ROWS, 128), F32)]),
        scratch_shapes=[pltpu.VMEM((D_MODEL, D_MODEL), F32), pltpu.VMEM((1, D_MODEL), F32), pltpu.VMEM((1, 1), F32),
                        pltpu.VMEM((tm, D_MODEL), F32), pltpu.VMEM((BLOCK, D_KV), F32), pltpu.VMEM((BLOCK, D_KV), F32),
                        pltpu.VMEM((BLOCK, D_POOL), F32)],
        compiler_params=_compiler_params(("arbitrary",)),
    )(sinks, dxo_or_xf, target_or_token, y, z, norm_post, w_out, u, u, pg, q, k, v, ag, a, pool_w, pool_scale, bias)


def _bwd_in(layer, part, token, dproj, x, norm_pre, dxo=None, w_in_t=None, adam=None):
    pair = part in ("dw_pair", "both_pair")
    want_dw, want_dx = part != "dx", part in ("both", "dx", "both_pair")
    tm = TOKEN_TILE
    steps = SEQ // tm
    adam_layer, adam_parts = adam if adam is not None else (None, [])
    n_adam = len(adam_parts)
    adam_steps = adam_parts[0][2].shape[1] // adam_parts[0][6] if n_adam else 0
    assert part == "dx" or not n_adam
    assert all(w.shape[1] // rows == adam_steps <= steps for _, _, w, _, _, _, rows in adam_parts)
    cw = 256

    def body(*refs):
        refs = list(refs)
        dp_ref, x_ref, g_ref = refs[1:4]
        del refs[:4]
        if want_dx:
            dxo_ref, w_ref = refs[:2]
            adam_in, adam_out = refs[2:2 + 5 * n_adam], refs[4 + 5 * n_adam:4 + 9 * n_adam]
            dx_ref, dgo_ref = refs[2 + 5 * n_adam:4 + 5 * n_adam]
            del refs[:4 + 9 * n_adam]
            dg = refs.pop()
        if pair:
            hs_ref, hm_ref, acc, mine_buf, theirs_buf, send_sem, recv_sem = refs
        elif want_dw:
            dw_ref, acc = refs
        i = pl.program_id(0)

        @pl.when(i == 0)
        def _():
            if pair:
                _handshake([(lax.axis_index("x"), lax.axis_index("y"), 1 - lax.axis_index("c"))])
            if want_dw:
                acc[...] = jnp.zeros_like(acc)
            if want_dx:
                dg[...] = jnp.zeros_like(dg)

        if n_adam:
            @pl.when(i < adam_steps)
            def _():
                for p in range(n_adam):
                    hm_ref, ld_ref, wa_ref, ma_ref, va_ref = adam_in[5 * p:5 * (p + 1)]
                    g = _own_then_slots(hm_ref, ld_ref)
                    for ref, val in zip(adam_out[4 * p:4 * (p + 1)], (g, *_adamw_math(wa_ref[...], g, ma_ref[...], va_ref[...]))):
                        ref[...] = val

        xv = x_ref[...]
        gv = g_ref[layer:layer + 1, :]
        r = lax.rsqrt(jnp.mean(xv * xv, axis=-1, keepdims=True) + EPS)
        xn = xv * r
        if want_dw:
            hb = (xn * gv).astype(BF16)
            for c in range(0, D_IN, cw):
                acc[c:c + cw, :] += _tn(dp_ref[:, c:c + cw], hb)
        def rows_for(q, core):
            return pl.ds(pl.multiple_of((2 * q + core) * IN_SHARD, 8), IN_SHARD)

        def swap(q):
            x, y, c = _mesh_pos()
            return pltpu.make_async_remote_copy(
                src_ref=mine_buf.at[q], dst_ref=theirs_buf.at[q], send_sem=send_sem.at[q], recv_sem=recv_sem.at[q],
                device_id=(x, y, 1 - c), device_id_type=MESH)

        if pair:
            @pl.when(i == steps - 1)
            def _():
                for q in range(4):
                    mine_buf[q] = acc[rows_for(q, 1 - lax.axis_index("c")), :].astype(BF16)
                    swap(q).start()

        if want_dx:
            dh = _nn(dp_ref[...], w_ref[...])
            dg[...] += jnp.sum(dh * xn, axis=0, keepdims=True)
            dhn = dh * gv
            dx_ref[...] = dxo_ref[...] + r * (dhn - xn * jnp.mean(dhn * xn, axis=-1, keepdims=True))

        @pl.when(i == steps - 1)
        def _():
            if pair:
                x, y, c = _mesh_pos()
                for q in range(4):
                    swap(q).wait()
                for j, q in enumerate([2 * (1 - x) + y, 2 * x + (1 - y), 2 * (1 - x) + (1 - y)]):
                    hs_ref[j] = (acc[rows_for(q, c), :] + theirs_buf[q].astype(F32)).astype(BF16)
                hm_ref[...] = acc[rows_for(2 * x + y, c), :] + theirs_buf[2 * x + y].astype(F32)
            elif want_dw:
                dw_ref[...] = acc[...].astype(BF16)
            if want_dx:
                _rows_of(dg, dgo_ref, 0)

    row = lambda c: pl.BlockSpec((tm, c), lambda i: (i, 0))
    const = lambda shape: pl.BlockSpec(shape, lambda i: (0,) * len(shape))
    in_specs = [const((8, 128)), row(D_IN), row(D_MODEL), const((DEPTH, D_MODEL))]
    operands = [token, dproj, x, norm_pre]
    out_specs, out_shape, scratch = [], [], []
    if want_dx:
        in_specs += [row(D_MODEL), _resident((D_IN, D_MODEL))]
        operands += [dxo, w_in_t]
        out_specs += [row(D_MODEL), const((8, 128))]
        out_shape += [jax.ShapeDtypeStruct((SEQ, D_MODEL), F32), jax.ShapeDtypeStruct((8, 128), F32)]
    for mine, lands, w, m, v, _, rows in adam_parts:
        slab = lambda i: jnp.minimum(i, adam_steps - 1)
        spec = pl.BlockSpec((None, rows, w.shape[2]), lambda i: (adam_layer, slab(i), 0))
        in_specs += [pl.BlockSpec((rows, w.shape[2]), lambda i: (slab(i), 0)) if mine.ndim == 2
                     else pl.BlockSpec((N_DEV, rows, w.shape[2]), lambda i: (0, slab(i), 0)),
                     pl.BlockSpec((lands.shape[0], rows, w.shape[2]), lambda i: (0, slab(i), 0)), spec, spec, spec]
        operands += [mine, lands, w, m, v]
        out_specs += [spec] * 4
        out_shape += [jax.ShapeDtypeStruct(w.shape, F32)] * 4
    if pair:
        out_specs += [const((3, IN_SHARD, D_MODEL)), const((IN_SHARD, D_MODEL))]
        out_shape += [jax.ShapeDtypeStruct((3, IN_SHARD, D_MODEL), BF16), jax.ShapeDtypeStruct((IN_SHARD, D_MODEL), F32)]
        scratch += [pltpu.VMEM((D_IN, D_MODEL), F32), pltpu.VMEM((4, IN_SHARD, D_MODEL), BF16),
                    pltpu.VMEM((4, IN_SHARD, D_MODEL), BF16), pltpu.SemaphoreType.DMA((4,)), pltpu.SemaphoreType.DMA((4,))]
    elif want_dw:
        out_specs.append(const((D_IN, D_MODEL)))
        out_shape.append(jax.ShapeDtypeStruct((D_IN, D_MODEL), BF16))
        scratch.append(pltpu.VMEM((D_IN, D_MODEL), F32))
    if want_dx:
        scratch.append(pltpu.VMEM((1, D_MODEL), F32))
    params = pltpu.CompilerParams(dimension_semantics=("arbitrary",), vmem_limit_bytes=VMEM_LIMIT,
                                  collective_id=COLLECTIVE_PAIR_SUM[layer] if pair else None)
    return pl.pallas_call(
        body, name=f"bwd_in_{part}{layer}", grid=(steps,),
        in_specs=in_specs, out_specs=out_specs, out_shape=out_shape, scratch_shapes=scratch,
        compiler_params=params,
    )(*operands)


def _mesh_pos():
    return lax.axis_index("x"), lax.axis_index("y"), lax.axis_index("c")


def _device_rows(ref, m, px, py, pc):
    return ref.at[pl.ds(pl.multiple_of((4 * px + 2 * py + pc) * m, 16 if m % 16 == 0 else 8), m), :]


def _allgather(srcs, out_dtype, name, later=()):
    na, nb = len(srcs), len(later)
    every = list(srcs) + list(later)
    shapes = [(a.shape[-2], a.shape[-1]) for a, _ in every]

    def body(*refs):
        xs, refs = refs[:na + nb], refs[na + nb:]
        outs, land, refs = refs[:na], refs[na:na + nb], refs[na + nb:]
        stage, cast, raw = refs[:na], refs[na:na + nb], refs[na + nb:2 * (na + nb)]
        send_sems, recv_sems, local_sems, load_sems = refs[2 * (na + nb):]
        loads = [pltpu.make_async_copy(xs[i].at[every[i][1]], raw[i], load_sems.at[i]) for i in range(na + nb)]
        for t, cp in enumerate(loads):
            cp.start(priority=t % 2)
        x, y, c = _mesh_pos()
        me, sibling = (x, y, c), (x, y, 1 - c)
        near = [(1 - x, y), (x, 1 - y)]
        far = (1 - x, 1 - y)
        relay_from, relay_to = (x ^ (1 - c), y ^ c), (x ^ c, y ^ (1 - c))
        _handshake([sibling] + [(*chip, c) for chip in near])
        k_from, k_to = 1 + c, 2 - c

        def slot(a, px, py, pc):
            return _device_rows(outs[a], shapes[a][0], px, py, pc)

        def copy(a, k, block, to, src=None):
            return pltpu.make_async_remote_copy(
                src_ref=slot(a, *block) if src is None else src, dst_ref=slot(a, *block),
                send_sem=send_sems.at[a, k], recv_sem=recv_sems.at[a, k], device_id=to, device_id_type=MESH)

        def cast_block(i):
            loads[i].wait()
            return raw[i][...].astype(out_dtype)

        for a in range(na):
            stage[a][...] = cast_block(a)
        mine = [pltpu.make_async_copy(stage[a], slot(a, *me), local_sems.at[a]) for a in range(na)]
        for cp in mine:
            cp.start()
        sent = []
        for a in range(na):
            sent.append(copy(a, 0, me, sibling, src=stage[a]))
            sent += [copy(a, 1 + j, me, (*chip, c), src=stage[a]) for j, chip in enumerate(near)]
        for cp in sent:
            cp.start()
        for b in range(nb):
            cast[b][...] = cast_block(na + b)
            cp = pltpu.make_async_copy(cast[b], _device_rows(land[b], shapes[na + b][0], *me), local_sems.at[na + b])
            cp.start()
            mine.append(cp)
        for a in range(na):
            copy(a, k_from, (*relay_from, c), me).wait_recv()
            sent += [copy(a, 3, (*relay_from, c), (*relay_to, c)), copy(a, 3 + k_from, (*relay_from, c), sibling)]
            sent[-2].start()
            sent[-1].start()
        for a in range(na):
            copy(a, k_to, (*relay_to, c), me).wait_recv()
            sent.append(copy(a, 3 + k_to, (*relay_to, c), sibling))
            sent[-1].start()
        for a in range(na):
            copy(a, 3, (*far, c), me).wait_recv()
            sent.append(copy(a, 6, (*far, c), sibling))
            sent[-1].start()
        for a in range(na):
            copy(a, 0, sibling, me).wait_recv()
            for j, chip in enumerate(near + [far]):
                copy(a, 4 + j, (*chip, 1 - c), me).wait_recv()
        for cp in sent:
            cp.wait_send()
        for cp in mine:
            cp.wait()

    hbm = pl.BlockSpec(memory_space=pl.ANY)
    gathered = [jax.ShapeDtypeStruct((N_DEV * m, n), out_dtype) for m, n in shapes]
    out = pl.pallas_call(
        body, name=name,
        in_specs=[hbm] * (na + nb),
        out_specs=[hbm] * (na + nb),
        out_shape=gathered,
        scratch_shapes=([pltpu.VMEM(s, out_dtype) for s in shapes]
                        + [pltpu.VMEM(s, a.dtype) for s, (a, _) in zip(shapes, every)]
                        + [pltpu.SemaphoreType.DMA((na, 7)), pltpu.SemaphoreType.DMA((na, 7)),
                           pltpu.SemaphoreType.DMA((na + nb,)), pltpu.SemaphoreType.DMA((na + nb,))]),
        compiler_params=pltpu.CompilerParams(vmem_limit_bytes=VMEM_LIMIT, collective_id=COLLECTIVE_GATHER_W0),
    )(*[a for a, _ in every])
    return out[:na], out[na:]


ALL_PEERS = tuple(range(1, N_DEV))


def _related(k, x, y, c):
    return x ^ ((k >> 2) & 1), y ^ ((k >> 1) & 1), c ^ (k & 1)


def _gather_wait(sems, block, land, relations, after, name):
    def body(src, land_ref, send_sem, recv_sem, after_ref, src_out, land_out):
        x, y, c = _mesh_pos()
        for k in relations:
            peer = _related(k, x, y, c)
            cp = pltpu.make_async_remote_copy(
                src_ref=src, dst_ref=_device_rows(land_ref, block.shape[0], *peer),
                send_sem=send_sem.at[k - 1], recv_sem=recv_sem.at[k - 1], device_id=peer, device_id_type=MESH)
            cp.wait_send()
            cp.wait_recv()

    out = pl.pallas_call(
        body, name=name,
        out_shape=(pltpu.HBM(block.shape, block.dtype), pltpu.HBM(land.shape, land.dtype)),
        in_specs=[_HBM, _HBM, _SEM, _SEM, pl.BlockSpec(memory_space=pl.ANY)],
        out_specs=[_HBM, _HBM],
        input_output_aliases={0: 0, 1: 1},
        compiler_params=pltpu.CompilerParams(has_side_effects=_EFFECT),
    )(block, land, sems[0], sems[1], after)
    return out[1]


(COLLECTIVE_GATHER_W0, COLLECTIVE_GATHER_W1, COLLECTIVE_EXCHANGE_1, COLLECTIVE_EXCHANGE_0A, COLLECTIVE_EXCHANGE_0B,
 COLLECTIVE_GATHER_SMALL) = range(1, 7)
COLLECTIVE_PAIR_SUM = (7, 8)


def _handshake(peers):
    barrier = pltpu.get_barrier_semaphore()
    for peer in peers:
        pl.semaphore_signal(barrier, inc=1, device_id=peer, device_id_type=MESH)
    pl.semaphore_wait(barrier, len(peers))


_HBM = pl.BlockSpec(memory_space=pltpu.HBM)
_SEM = pl.BlockSpec(memory_space=pltpu.SEMAPHORE)
_EFFECT = pltpu.SideEffectType.DATAFLOW_SIDE_EFFECTING


def _exchange_plan(direct):
    x, y, c = _mesh_pos()
    if not direct:
        return [(j, j, (qx, qy, c)) for j, (qx, qy) in enumerate([(1 - x, y), (x, 1 - y), (1 - x, 1 - y)])]
    plan = []
    for k in range(1, N_DEV):
        px, py, pc = x ^ ((k >> 2) & 1), y ^ ((k >> 1) & 1), c ^ (k & 1)
        plan.append((4 * px + 2 * py + pc, k - 1, (px, py, pc)))
    return plan


def _exchange_copies(directs):
    copies, base = [], 0
    for a, direct in enumerate(directs):
        plan = _exchange_plan(direct)
        copies += [(a, block, slot, peer, base + slot) for block, slot, peer in plan]
        base += len(plan)
    return copies, base


def _exchange_start(srcs, directs, collective_id, name):
    na = len(srcs)
    slots = [N_DEV - 1 if direct else 3 for direct in directs]

    def body(*refs):
        src, land = refs[:na], refs[na:2 * na]
        send_sem, recv_sem = refs[2 * na], refs[2 * na + 1]
        token = refs[-1]
        _handshake([peer for _, _, peer in _exchange_plan(any(directs))])
        for a, block, slot, peer, sem in _exchange_copies(directs)[0]:
            pltpu.make_async_remote_copy(
                src_ref=src[a].at[block], dst_ref=land[a].at[slot], send_sem=send_sem.at[sem],
                recv_sem=recv_sem.at[sem], device_id=peer, device_id_type=MESH).start()
        token[...] = jnp.zeros_like(token)

    zones = [jax.ShapeDtypeStruct((n,) + t.shape[1:], t.dtype) for n, t in zip(slots, srcs)]
    bufs = [pltpu.HBM(t.shape, t.dtype) for t in list(srcs) + zones]
    out = pl.pallas_call(
        body, name=name,
        out_shape=(pltpu.SemaphoreType.DMA((sum(slots),)), pltpu.SemaphoreType.DMA((sum(slots),)), *bufs,
                   jax.ShapeDtypeStruct((8, 128), F32)),
        in_specs=[_HBM] * (2 * na),
        out_specs=(_SEM, _SEM, *([_HBM] * (2 * na)), pl.BlockSpec(memory_space=pltpu.VMEM)),
        input_output_aliases={i: 2 + i for i in range(2 * na)},
        compiler_params=pltpu.CompilerParams(has_side_effects=_EFFECT, collective_id=collective_id),
    )(*[pltpu.with_memory_space_constraint(t, pltpu.HBM) for t in srcs],
      *[pltpu.with_memory_space_constraint(lax.empty(t.shape, t.dtype), pltpu.HBM) for t in zones])
    return out[0], out[1], out[2:2 + na], out[2 + na:2 + 2 * na], out[-1]


def _exchange_wait(send_sem, recv_sem, srcs, lands, directs, after, name):
    na = len(srcs)

    def body(*refs):
        src, land = refs[:na], refs[na:2 * na]
        send_sem_ref, recv_sem_ref = refs[2 * na], refs[2 * na + 1]
        for a, block, slot, peer, sem in _exchange_copies(directs)[0]:
            cp = pltpu.make_async_remote_copy(
                src_ref=src[a].at[block], dst_ref=land[a].at[slot], send_sem=send_sem_ref.at[sem],
                recv_sem=recv_sem_ref.at[sem], device_id=peer, device_id_type=MESH)
            cp.wait_send()
            cp.wait_recv()

    bufs = [pltpu.HBM(t.shape, t.dtype) for t in list(srcs) + list(lands)]
    out = pl.pallas_call(
        body, name=name,
        out_shape=tuple(bufs),
        in_specs=[_HBM] * (2 * na) + [_SEM, _SEM, pl.BlockSpec(memory_space=pl.ANY)],
        out_specs=[_HBM] * (2 * na),
        input_output_aliases={i: i for i in range(2 * na)},
        compiler_params=pltpu.CompilerParams(has_side_effects=_EFFECT),
    )(*srcs, *lands, send_sem, recv_sem, after)
    return out[:na], out[na:]


def _own_then_slots(mine_ref, lands_ref, rows=slice(None)):
    if len(mine_ref.shape) == 3:
        x, y, c = _mesh_pos()
        total = mine_ref[4 * x + 2 * y + c, rows, :].astype(F32)
    else:
        total = mine_ref[rows, :].astype(F32)
    for j in range(lands_ref.shape[0]):
        total = total + lands_ref[j, rows, :].astype(F32)
    return total


SMALL_ROWS = 2 * PACK_SLICE + 2 * 8


def _small_gather_start(mine, lands, dgpre, name):
    def body(*refs):
        hm, ld, dg = refs[:DEPTH], refs[DEPTH:2 * DEPTH], refs[2 * DEPTH:3 * DEPTH]
        send_sem, recv_sem, blk, land, token, own, slots, rows, built, local_sems = refs[3 * DEPTH:]
        x, y, c = _mesh_pos()
        loads = []
        for l in range(DEPTH):
            loads += [pltpu.make_async_copy(hm[l].at[4 * x + 2 * y + c], own.at[l], local_sems.at[3 * l]),
                      pltpu.make_async_copy(ld[l], slots.at[l], local_sems.at[3 * l + 1]),
                      pltpu.make_async_copy(dg[l], rows.at[l], local_sems.at[3 * l + 2])]
        for t, cp in enumerate(loads):
            cp.start(priority=t % 2)
        _handshake([_related(k, x, y, c) for k in ALL_PEERS])
        for cp in loads:
            cp.wait()
        for l in range(DEPTH):
            total = own[l]
            for j in range(N_DEV - 1):
                total = total + slots[l, j]
            built[PACK_SLICE * l:PACK_SLICE * (l + 1), :] = total
            built[2 * PACK_SLICE + 8 * l:2 * PACK_SLICE + 8 * (l + 1), :] = rows[l]
        stores = [pltpu.make_async_copy(built, blk, local_sems.at[3 * DEPTH]),
                  pltpu.make_async_copy(built, _device_rows(land, SMALL_ROWS, x, y, c), local_sems.at[3 * DEPTH + 1])]
        for cp in stores:
            cp.start()
        for cp in stores:
            cp.wait()
        for k in ALL_PEERS:
            pltpu.make_async_remote_copy(
                src_ref=blk, dst_ref=_device_rows(land, SMALL_ROWS, x, y, c), send_sem=send_sem.at[k - 1],
                recv_sem=recv_sem.at[k - 1], device_id=_related(k, x, y, c), device_id_type=MESH).start()
        token[...] = jnp.zeros_like(token)

    hbm = pl.BlockSpec(memory_space=pl.ANY)
    out = pl.pallas_call(
        body, name=name,
        in_specs=[hbm] * (3 * DEPTH),
        out_specs=(_SEM, _SEM, _HBM, _HBM, pl.BlockSpec(memory_space=pltpu.VMEM)),
        out_shape=(pltpu.SemaphoreType.DMA((N_DEV - 1,)), pltpu.SemaphoreType.DMA((N_DEV - 1,)),
                   pltpu.HBM((SMALL_ROWS, 128), F32), pltpu.HBM((N_DEV * SMALL_ROWS, 128), F32),
                   jax.ShapeDtypeStruct((8, 128), F32)),
        scratch_shapes=[pltpu.VMEM((DEPTH, PACK_SLICE, 128), F32), pltpu.VMEM((DEPTH, N_DEV - 1, PACK_SLICE, 128), F32),
                        pltpu.VMEM((DEPTH, 8, 128), F32), pltpu.VMEM((SMALL_ROWS, 128), F32),
                        pltpu.SemaphoreType.DMA((3 * DEPTH + 2,))],
        compiler_params=pltpu.CompilerParams(has_side_effects=_EFFECT, collective_id=COLLECTIVE_GATHER_SMALL),
    )(*mine, *lands, *dgpre)
    return (out[0], out[1]), out[2], out[3], out[4]


def _adamw_math(w, g, m, v):
    m = ADAM_B1 * m + (1.0 - ADAM_B1) * g
    v = ADAM_B2 * v + (1.0 - ADAM_B2) * (g * g)
    m_hat = m / (1.0 - ADAM_B1 ** ADAM_STEP)
    v_hat = v / (1.0 - ADAM_B2 ** ADAM_STEP)
    delta = -ADAM_LR * (m_hat / (jnp.sqrt(v_hat) + ADAM_EPS) + ADAM_WD * w)
    return delta, m, v


def _adamw_layer(layer, parts, token, name):
    steps = {w.shape[1] // rows for _, _, w, _, _, _, rows in parts}
    assert len(steps) == 1, steps
    n = len(parts)

    def body(_, *refs):
        for p in range(n):
            hm_ref, ld_ref, w_ref, m_ref, v_ref = refs[5 * p:5 * (p + 1)]
            g_ref, d_ref, nm_ref, nv_ref = refs[len(refs) - 4 * (n - p):len(refs) - 4 * (n - p - 1)]
            g = _own_then_slots(hm_ref, ld_ref)
            g_ref[...] = g
            d, nm, nv = _adamw_math(w_ref[...], g, m_ref[...], v_ref[...])
            d_ref[...] = d
            nm_ref[...] = nm
            nv_ref[...] = nv

    in_specs, out_specs, out_shape, operands, carried, aliases = [pl.BlockSpec(memory_space=pl.ANY)], [], [], [], [], {}
    for p, (mine, lands, w, m, v, earlier, rows) in enumerate(parts):
        nn = w.shape[2]
        spec = pl.BlockSpec((None, rows, nn), lambda i: (layer, i, 0))
        in_specs += [pl.BlockSpec((rows, nn), lambda i: (i, 0)) if mine.ndim == 2
                     else pl.BlockSpec((N_DEV, rows, nn), lambda i: (0, i, 0)),
                     pl.BlockSpec((lands.shape[0], rows, nn), lambda i: (0, i, 0)), spec, spec, spec]
        out_specs += [spec] * 4
        out_shape += [jax.ShapeDtypeStruct(w.shape, F32)] * 4
        operands += [mine, lands, w, m, v]
        if earlier is not None:
            aliases.update({1 + 5 * n + len(carried) + t: 4 * p + t for t in range(4)})
            carried += list(earlier)
    out = pl.pallas_call(
        body, name=name, grid=(steps.pop(),),
        in_specs=in_specs + [pl.BlockSpec(memory_space=pl.ANY)] * len(carried),
        out_specs=out_specs, out_shape=out_shape, input_output_aliases=aliases,
        compiler_params=_compiler_params(("arbitrary",)),
    )(token, *operands, *carried)
    return [out[4 * p:4 * (p + 1)] for p in range(n)]


def _adamw_small(gathered, params):
    def body(all_ref, *refs):
        ins, outs, packs = refs[:15], refs[15:15 + 21], refs[15 + 21]
        loss_ref = outs[0]
        for dev in range(N_DEV):
            for l in range(DEPTH):
                packs[l, PACK_SLICE * dev:PACK_SLICE * (dev + 1), :] = (
                    all_ref[SMALL_ROWS * dev + PACK_SLICE * l:SMALL_ROWS * dev + PACK_SLICE * (l + 1), :])
        loss_ref[...] = packs[DEPTH - 1, ROW_LOSS:ROW_LOSS + 1, 0:1]

        def update(p, sel, g):
            w_ref, m_ref, v_ref = ins[p], ins[5 + p], ins[10 + p]
            d, nm, nv = _adamw_math(w_ref[sel], g, m_ref[sel], v_ref[sel])
            for t, val in enumerate((g, d, nm, nv)):
                outs[1 + 5 * t + p][sel] = val

        for l in range(DEPTH):
            gp = packs.at[l]
            row0 = 2 * PACK_SLICE + 8 * l
            dgpre = all_ref[row0:row0 + 8, :]
            for dev in range(1, N_DEV):
                dgpre = dgpre + all_ref[SMALL_ROWS * dev + row0:SMALL_ROWS * dev + row0 + 8, :]
            for grp in range(4):
                update(0, (l, grp), gp[ROW_PW + BLOCK * grp:ROW_PW + BLOCK * (grp + 1), :])
                update(1, (slice(l, l + 1), slice(128 * grp, 128 * (grp + 1))), gp[ROW_SC + grp:ROW_SC + grp + 1, :])
            update(2, (slice(l, l + 1), slice(None)), gp[ROW_SINK:ROW_SINK + 1, 0:N_HEADS])
            for r in range(D_MODEL // 128):
                sel = (slice(l, l + 1), slice(128 * r, 128 * (r + 1)))
                update(3, sel, dgpre[r:r + 1, :])
                update(4, sel, gp[ROW_NPOST + r:ROW_NPOST + r + 1, :])

    shapes = [jax.ShapeDtypeStruct(p.shape, F32) for p in params[:5]]
    return pl.pallas_call(
        body, name="adamw_small",
        out_shape=[jax.ShapeDtypeStruct((1, 1), F32)] + shapes * 4,
        scratch_shapes=[pltpu.VMEM((DEPTH, PACK_ROWS, 128), F32)],
        compiler_params=_compiler_params(),
    )(gathered, *params)


def kernel(x, w_in, pool_w, pool_scale, attn_sinks, w_out, norm_pre, norm_post, loss_target, m_w_in, m_pool_w, m_pool_scale, m_attn_sinks, m_w_out, m_norm_pre, m_norm_post, v_w_in, v_pool_w, v_pool_scale, v_attn_sinks, v_w_out, v_norm_pre, v_norm_post):
    x0 = x.reshape(SEQ, D_MODEL)
    target = loss_target.reshape(SEQ, D_MODEL)
    bias = jnp.asarray(_attn_bias())
    w_in_t, m_in_t, v_in_t = (jnp.swapaxes(t, 1, 2) for t in (w_in, m_w_in, v_w_in))

    (win0, wout0), lands = _allgather([(w_in_t, 0), (w_out, 0)], BF16, "gather_w0",
                                              later=[(w_in_t, 1), (w_out, 1)])
    win_full, wout_full = [win0, None], [wout0, None]
    token = x0

    saved = []
    xl = x0
    for layer in range(DEPTH):
        front = (layer, xl, norm_pre, win_full[layer], token, attn_sinks, pool_w, pool_scale, bias)
        if layer == 0:
            u, pg, q, k, v, ag, z, a, x_next, y, win_full[1], wout_full[1] = _fwd_front(
                *front, out=(norm_post, wout_full[layer]), gather=lands)
        else:
            u, pg, q, k, v, ag, z, a, x_next, y = _fwd_front(*front, out=(norm_post, wout_full[layer]))
        saved.append((xl, u, pg, q, k, v, ag, z, a, y))
        xl = x_next

    params_small = [pool_w, pool_scale, attn_sinks, norm_pre, norm_post,
                    m_pool_w, m_pool_scale, m_attn_sinks, m_norm_pre, m_norm_post,
                    v_pool_w, v_pool_scale, v_attn_sinks, v_norm_pre, v_norm_post]

    def start(srcs, directs, paired, collective_id, tag):
        send_sem, recv_sem, srcs, lands, started = _exchange_start(srcs, directs, collective_id, f"exchange_start{tag}")
        return (send_sem, recv_sem, srcs, lands, paired, directs), started

    def finish(handle, after, tag):
        send_sem, recv_sem, srcs, lands, paired, directs = handle
        srcs, lands = _exchange_wait(send_sem, recv_sem, srcs, lands, directs, after, f"exchange_wait{tag}")
        return [s if p is None else p for s, p in zip(srcs, paired)], lands

    def back(layer, top, first, second):
        xin, u, pg, q, k, v, ag, z, a, y = saved[layer]
        return _bwd_back(layer, top, first, second, y, z, norm_post, wout_full[layer], attn_sinks, u, pg, q, k, v,
                         ag, a, pool_w, pool_scale, bias)

    dgpre = [None] * DEPTH
    dx, dproj, gw_out, pack = back(1, True, xl, target)
    dx, dgpre[1], chip_sums, own_sum = _bwd_in(1, "both_pair", token, dproj, saved[1][0], norm_pre, dx, win_full[1])
    top, token = start([chip_sums, gw_out.reshape(N_DEV, OUT_SHARD, D_MODEL), pack.reshape(N_DEV, PACK_SLICE, 128)],
                       [False, True, True], [own_sum, None, None], COLLECTIVE_EXCHANGE_1, "1")

    dproj, gw_out, pack = back(0, False, dx, token)
    early, token = start([gw_out.reshape(N_DEV, OUT_SHARD, D_MODEL), pack.reshape(N_DEV, PACK_SLICE, 128)],
                         [True, True], [None, None], COLLECTIVE_EXCHANGE_0A, "0a")
    chip_sums, own_sum = _bwd_in(0, "dw_pair", token, dproj, saved[0][0], norm_pre)
    own1, lands1 = finish(top, chip_sums, "1")
    late, token = start([chip_sums], [False], [own_sum], COLLECTIVE_EXCHANGE_0B, "0b")
    dx, dgpre[0], *updated = _bwd_in(0, "dx", token, dproj, saved[0][0], norm_pre, dx, win_full[0],
                                     adam=(1, [(own1[0], lands1[0], w_in_t, m_in_t, v_in_t, None, ADAM_ROWS_IN),
                                               (own1[1], lands1[1], w_out, m_w_out, v_w_out, None, ADAM_ROWS_OUT)]))
    big_in, big_out = updated[:4], updated[4:]

    own0a, lands0a = finish(early, dx, "0a")
    sems, block, land, token = _small_gather_start([own0a[1], own1[2]], [lands0a[1], lands1[2]], dgpre,
                                                   "gather_small_start")
    own0b, lands0b = finish(late, token, "0b")
    big_in, big_out = _adamw_layer(0, [(own0b[0], lands0b[0], w_in_t, m_in_t, v_in_t, big_in, ADAM_ROWS_IN),
                                       (own0a[0], lands0a[0], w_out, m_w_out, v_w_out, big_out, ADAM_ROWS_OUT)],
                                   token, "adamw0")
    gathered = _gather_wait(sems, block, land, ALL_PEERS, big_in[0], "gather_small_wait")
    small_out = _adamw_small(gathered, params_small)
    loss = small_out[0].reshape(())

    outs = [loss, dx.reshape(1, SEQ, D_MODEL)]
    for t in range(4):
        pw_, sc_, sk_, npre_, npost_ = small_out[1 + 5 * t:6 + 5 * t]
        outs += [jnp.swapaxes(big_in[t], 1, 2), pw_, sc_, sk_, big_out[t], npre_, npost_]
    return tuple(outs)
```
